```python
import jax, jax.numpy as jnp
from jax import lax
import numpy as np

D_MODEL = 1024
BATCH = 8
SEQ = 4096
DEPTH = 1

DN_HEADS = 8
DN_HEAD_DIM = 128
DN_WIDTH = DN_HEADS * DN_HEAD_DIM
DN_CONV = 4
DN_CHUNK = 64
SB_HEADS = 8
SB_HEAD_DIM = 128
SB_WIDTH = SB_HEADS * SB_HEAD_DIM
SB_BLOCK = 128
D_FF = 2816
FFN_CONV = 3
EPS = 1e-6

DN_QKV_END = 3 * DN_WIDTH
DN_A_END = DN_QKV_END + DN_HEADS
DN_B_END = DN_A_END + DN_HEADS
DN_G_END = DN_B_END + DN_WIDTH
SB_QKV_END = DN_G_END + 3 * SB_WIDTH
IN_WIDTH = SB_QKV_END + 2 * D_MODEL
SPLIT_IDX = (DN_QKV_END, DN_A_END, DN_B_END, DN_G_END, SB_QKV_END)

kernel_name = 'hybrid_gdn_stickbreak_convffn'


def rmsnorm(x, w):
    xf = x.astype(jnp.float32)
    y = xf * lax.rsqrt(jnp.mean(xf * xf, axis=-1, keepdims=True) + EPS)
    return (y * w.astype(jnp.float32)).astype(x.dtype)


def l2norm(x):
    xf = x.astype(jnp.float32)
    return (xf * lax.rsqrt(jnp.sum(xf * xf, axis=-1, keepdims=True) + EPS)).astype(x.dtype)


def causal_dwconv(x, w):
    K = w.shape[0]
    T = x.shape[1]
    xp = jnp.pad(x, ((0, 0), (K - 1, 0), (0, 0)))
    y = w[0] * xp[:, 0:T]
    for i in range(1, K):
        y = y + w[i] * xp[:, i:i + T]
    return y


def to_heads(t, n_heads, head_dim):
    B, T, _ = t.shape
    return t.reshape(B, T, n_heads, head_dim).transpose(0, 2, 1, 3)


def gated_delta_rule(q, k, v, g, beta):
    out_dtype = v.dtype
    q, k, v, g, beta = (t.astype(jnp.float32) for t in (q, k, v, g, beta))
    B, H, T, dk = q.shape
    dv = v.shape[-1]
    C = DN_CHUNK
    N = T // C
    q = q * (dk ** -0.5)
    rs = lambda t: t.reshape(B, H, N, C, *t.shape[3:])
    q, k, v, g, beta = rs(q), rs(k), rs(v), rs(g), rs(beta)
    g = jnp.cumsum(g, axis=-1)
    k_beta = k * beta[..., None]
    v_beta = v * beta[..., None]
    lower = jnp.tril(jnp.ones((C, C), dtype=bool))
    strict = jnp.tril(jnp.ones((C, C), dtype=bool), -1)
    diff = g[..., :, None] - g[..., None, :]
    decay = jnp.where(lower, jnp.exp(jnp.where(lower, diff, 0.0)), 0.0)
    L = jnp.where(strict, jnp.einsum('bhncd,bhnsd->bhncs', k_beta, k) * decay, 0.0)
    rhs = jnp.concatenate([v_beta, k_beta * jnp.exp(g)[..., None]], axis=-1)
    sol = lax.linalg.triangular_solve(L, rhs, left_side=True, lower=True, unit_diagonal=True)
    u = sol[..., :dv]
    w = sol[..., dv:]
    a_qk = jnp.where(lower, jnp.einsum('bhncd,bhnsd->bhncs', q, k) * decay, 0.0)

    def step(S, xs):
        q_i, k_i, u_i, w_i, g_i, a_i = xs
        v_new = u_i - jnp.einsum('bhcd,bhde->bhce', w_i, S)
        o_i = (jnp.einsum('bhcd,bhde->bhce', q_i * jnp.exp(g_i)[..., None], S)
               + jnp.einsum('bhcs,bhse->bhce', a_i, v_new))
        g_last = g_i[..., -1]
        S = (S * jnp.exp(g_last)[..., None, None]
             + jnp.einsum('bhcd,bhce->bhde', k_i * jnp.exp(g_last[..., None] - g_i)[..., None], v_new))
        return S, o_i

    xs = tuple(jnp.moveaxis(t, 2, 0) for t in (q, k, u, w, g, a_qk))
    S0 = jnp.zeros((B, H, dk, dv), jnp.float32)
    _, o = lax.scan(step, S0, xs)
    o = jnp.moveaxis(o, 0, 2).reshape(B, H, T, dv)
    return o.astype(out_dtype)


def stick_breaking_attention(q, k, v):
    B, H, T, d = q.shape
    nb = T // SB_BLOCK
    scale = d ** -0.5
    qb = q.reshape(B, H, nb, SB_BLOCK, d).transpose(2, 0, 1, 3, 4)
    key_pos = jnp.arange(T)

    def block(args):
        q_blk, i = args
        z = jnp.einsum('bhqd,bhkd->bhqk', q_blk, k).astype(jnp.float32) * scale
        q_pos = i * SB_BLOCK + jnp.arange(SB_BLOCK)
        mask = key_pos[None, :] < q_pos[:, None]
        log_keep = jnp.where(mask, jax.nn.log_sigmoid(-z), 0.0)
        between = lax.cumsum(log_keep, axis=3, reverse=True) - log_keep
        log_a = jax.nn.log_sigmoid(z) + between
        a = jnp.where(mask, jnp.exp(log_a), 0.0)
        return jnp.einsum('bhqk,bhkd->bhqd', a.astype(v.dtype), v)

    o = lax.map(block, (qb, jnp.arange(nb)))
    return o.transpose(1, 2, 0, 3, 4).reshape(B, H, T, d)


def _fwd_setup_inputs(seed: int = 0) -> dict:
    key = jax.random.key(seed)
    ks = jax.random.split(key, 20)
    L = DEPTH
    nrm = lambda k, shape, fan_in: jax.random.normal(k, shape, jnp.float32) * (fan_in ** -0.5)
    gain = lambda k, shape: 1.0 + 0.02 * jax.random.normal(k, shape, jnp.float32)
    x = jax.random.normal(ks[0], (BATCH, SEQ, D_MODEL), jnp.float32)
    norm1_w = gain(ks[1], (L, D_MODEL))
    w_in = nrm(ks[2], (L, D_MODEL, IN_WIDTH), D_MODEL)
    dn_conv_w = nrm(ks[3], (L, DN_CONV, 3 * DN_WIDTH), DN_CONV)
    dn_A_log = jnp.log(jax.random.uniform(ks[4], (L, DN_HEADS), jnp.float32, 1.0, 16.0))
    dt = jnp.exp(jax.random.uniform(ks[5], (L, DN_HEADS), jnp.float32, np.log(1e-3), np.log(1e-1)))
    dn_dt_bias = dt + jnp.log(-jnp.expm1(-dt))
    dn_norm_w = gain(ks[6], (L, DN_HEAD_DIM))
    w_proj_dn = nrm(ks[7], (L, DN_WIDTH, D_MODEL), DN_WIDTH)
    w_proj_sb = nrm(ks[8], (L, SB_WIDTH, D_MODEL), SB_WIDTH)
    w_out = nrm(ks[9], (L, D_MODEL, D_MODEL), D_MODEL)
    norm2_w = gain(ks[10], (L, D_MODEL))
    ffn_w_up = nrm(ks[11], (L, D_MODEL, 2 * D_FF), D_MODEL)
    ffn_conv_w = nrm(ks[12], (L, FFN_CONV, 2 * D_FF), FFN_CONV)
    ffn_w_down = nrm(ks[13], (L, D_FF, D_MODEL), D_FF)
    norm_f_w = gain(ks[14], (D_MODEL,))
    return {'x': x, 'norm1_w': norm1_w, 'w_in': w_in, 'dn_conv_w': dn_conv_w,
            'dn_A_log': dn_A_log, 'dn_dt_bias': dn_dt_bias, 'dn_norm_w': dn_norm_w,
            'w_proj_dn': w_proj_dn, 'w_proj_sb': w_proj_sb, 'w_out': w_out,
            'norm2_w': norm2_w, 'ffn_w_up': ffn_w_up, 'ffn_conv_w': ffn_conv_w,
            'ffn_w_down': ffn_w_down, 'norm_f_w': norm_f_w}


def _fwd_reference(x, norm1_w, w_in, dn_conv_w, dn_A_log, dn_dt_bias, dn_norm_w,
              w_proj_dn, w_proj_sb, w_out, norm2_w, ffn_w_up, ffn_conv_w,
              ffn_w_down, norm_f_w):
    B, T, _ = x.shape
    for l in range(DEPTH):
        n1 = rmsnorm(x, norm1_w[l])
        h = n1 @ w_in[l]
        dn_qkv, dn_a, dn_b, dn_gate, sb_qkv, gate_logits = jnp.split(h, SPLIT_IDX, axis=-1)

        dn_qkv = jax.nn.silu(causal_dwconv(dn_qkv, dn_conv_w[l]))
        dq, dk, dv = jnp.split(dn_qkv, 3, axis=-1)
        dq = l2norm(to_heads(dq, DN_HEADS, DN_HEAD_DIM))
        dk = l2norm(to_heads(dk, DN_HEADS, DN_HEAD_DIM))
        dv = to_heads(dv, DN_HEADS, DN_HEAD_DIM)
        beta = jax.nn.sigmoid(dn_b.astype(jnp.float32)).transpose(0, 2, 1)
        g = (-jnp.exp(dn_A_log[l].astype(jnp.float32))
             * jax.nn.softplus(dn_a.astype(jnp.float32) + dn_dt_bias[l].astype(jnp.float32))).transpose(0, 2, 1)
        o_dn = gated_delta_rule(dq, dk, dv, g, beta)
        o_dn = rmsnorm(o_dn, dn_norm_w[l]).transpose(0, 2, 1, 3)
        o_dn = o_dn * jax.nn.silu(dn_gate.reshape(B, T, DN_HEADS, DN_HEAD_DIM))
        o_dn = o_dn.reshape(B, T, DN_WIDTH)

        sq, sk, sv = jnp.split(sb_qkv, 3, axis=-1)
        o_sb = stick_breaking_attention(to_heads(sq, SB_HEADS, SB_HEAD_DIM),
                                        to_heads(sk, SB_HEADS, SB_HEAD_DIM),
                                        to_heads(sv, SB_HEADS, SB_HEAD_DIM))
        o_sb = o_sb.transpose(0, 2, 1, 3).reshape(B, T, SB_WIDTH)

        gate_dn, gate_sb = jnp.split(jax.nn.sigmoid(gate_logits), 2, axis=-1)
        mixed = gate_dn * (o_dn @ w_proj_dn[l]) + gate_sb * (o_sb @ w_proj_sb[l])
        x = x + mixed @ w_out[l]

        n2 = rmsnorm(x, norm2_w[l])
        u = causal_dwconv(n2 @ ffn_w_up[l], ffn_conv_w[l])
        gate, up = jnp.split(u, 2, axis=-1)
        x = x + (jax.nn.silu(gate) * up) @ ffn_w_down[l]
    return rmsnorm(x, norm_f_w)


import jax as _jax
import jax.numpy as _jnp

TWIN_FORMAT = 'train_step'
FWD_PARAMS = ['x', 'norm1_w', 'w_in', 'dn_conv_w', 'dn_A_log', 'dn_dt_bias', 'dn_norm_w', 'w_proj_dn', 'w_proj_sb', 'w_out', 'norm2_w', 'ffn_w_up', 'ffn_conv_w', 'ffn_w_down', 'norm_f_w']
TWIN_WEIGHTS = ['norm1_w', 'w_in', 'dn_conv_w', 'dn_A_log', 'dn_dt_bias', 'dn_norm_w', 'w_proj_dn', 'w_proj_sb', 'w_out', 'norm2_w', 'ffn_w_up', 'ffn_conv_w', 'ffn_w_down', 'norm_f_w']
TWIN_DIFF_INPUT = 'x'
TWIN_INPUTS = ['x', 'norm1_w', 'w_in', 'dn_conv_w', 'dn_A_log', 'dn_dt_bias', 'dn_norm_w', 'w_proj_dn', 'w_proj_sb', 'w_out', 'norm2_w', 'ffn_w_up', 'ffn_conv_w', 'ffn_w_down', 'norm_f_w', 'loss_target', 'm_norm1_w', 'm_w_in', 'm_dn_conv_w', 'm_dn_A_log', 'm_dn_dt_bias', 'm_dn_norm_w', 'm_w_proj_dn', 'm_w_proj_sb', 'm_w_out', 'm_norm2_w', 'm_ffn_w_up', 'm_ffn_conv_w', 'm_ffn_w_down', 'm_norm_f_w', 'v_norm1_w', 'v_w_in', 'v_dn_conv_w', 'v_dn_A_log', 'v_dn_dt_bias', 'v_dn_norm_w', 'v_w_proj_dn', 'v_w_proj_sb', 'v_w_out', 'v_norm2_w', 'v_ffn_w_up', 'v_ffn_conv_w', 'v_ffn_w_down', 'v_norm_f_w']
TWIN_OUTPUTS = ['loss', 'grad_x', 'grad_norm1_w', 'grad_w_in', 'grad_dn_conv_w', 'grad_dn_A_log', 'grad_dn_dt_bias', 'grad_dn_norm_w', 'grad_w_proj_dn', 'grad_w_proj_sb', 'grad_w_out', 'grad_norm2_w', 'grad_ffn_w_up', 'grad_ffn_conv_w', 'grad_ffn_w_down', 'grad_norm_f_w', 'delta_norm1_w', 'delta_w_in', 'delta_dn_conv_w', 'delta_dn_A_log', 'delta_dn_dt_bias', 'delta_dn_norm_w', 'delta_w_proj_dn', 'delta_w_proj_sb', 'delta_w_out', 'delta_norm2_w', 'delta_ffn_w_up', 'delta_ffn_conv_w', 'delta_ffn_w_down', 'delta_norm_f_w', 'new_m_norm1_w', 'new_m_w_in', 'new_m_dn_conv_w', 'new_m_dn_A_log', 'new_m_dn_dt_bias', 'new_m_dn_norm_w', 'new_m_w_proj_dn', 'new_m_w_proj_sb', 'new_m_w_out', 'new_m_norm2_w', 'new_m_ffn_w_up', 'new_m_ffn_conv_w', 'new_m_ffn_w_down', 'new_m_norm_f_w', 'new_v_norm1_w', 'new_v_w_in', 'new_v_dn_conv_w', 'new_v_dn_A_log', 'new_v_dn_dt_bias', 'new_v_dn_norm_w', 'new_v_w_proj_dn', 'new_v_w_proj_sb', 'new_v_w_out', 'new_v_norm2_w', 'new_v_ffn_w_up', 'new_v_ffn_conv_w', 'new_v_ffn_w_down', 'new_v_norm_f_w']
TWIN_LEAF_KINDS = {'loss': 'loss', 'grad_x': 'grad_x', 'grad_norm1_w': 'grad_w', 'grad_w_in': 'grad_w', 'grad_dn_conv_w': 'grad_w', 'grad_dn_A_log': 'grad_w', 'grad_dn_dt_bias': 'grad_w', 'grad_dn_norm_w': 'grad_w', 'grad_w_proj_dn': 'grad_w', 'grad_w_proj_sb': 'grad_w', 'grad_w_out': 'grad_w', 'grad_norm2_w': 'grad_w', 'grad_ffn_w_up': 'grad_w', 'grad_ffn_conv_w': 'grad_w', 'grad_ffn_w_down': 'grad_w', 'grad_norm_f_w': 'grad_w', 'delta_norm1_w': 'delta_w', 'delta_w_in': 'delta_w', 'delta_dn_conv_w': 'delta_w', 'delta_dn_A_log': 'delta_w', 'delta_dn_dt_bias': 'delta_w', 'delta_dn_norm_w': 'delta_w', 'delta_w_proj_dn': 'delta_w', 'delta_w_proj_sb': 'delta_w', 'delta_w_out': 'delta_w', 'delta_norm2_w': 'delta_w', 'delta_ffn_w_up': 'delta_w', 'delta_ffn_conv_w': 'delta_w', 'delta_ffn_w_down': 'delta_w', 'delta_norm_f_w': 'delta_w', 'new_m_norm1_w': 'new_m', 'new_m_w_in': 'new_m', 'new_m_dn_conv_w': 'new_m', 'new_m_dn_A_log': 'new_m', 'new_m_dn_dt_bias': 'new_m', 'new_m_dn_norm_w': 'new_m', 'new_m_w_proj_dn': 'new_m', 'new_m_w_proj_sb': 'new_m', 'new_m_w_out': 'new_m', 'new_m_norm2_w': 'new_m', 'new_m_ffn_w_up': 'new_m', 'new_m_ffn_conv_w': 'new_m', 'new_m_ffn_w_down': 'new_m', 'new_m_norm_f_w': 'new_m', 'new_v_norm1_w': 'new_v', 'new_v_w_in': 'new_v', 'new_v_dn_conv_w': 'new_v', 'new_v_dn_A_log': 'new_v', 'new_v_dn_dt_bias': 'new_v', 'new_v_dn_norm_w': 'new_v', 'new_v_w_proj_dn': 'new_v', 'new_v_w_proj_sb': 'new_v', 'new_v_w_out': 'new_v', 'new_v_norm2_w': 'new_v', 'new_v_ffn_w_up': 'new_v', 'new_v_ffn_conv_w': 'new_v', 'new_v_ffn_w_down': 'new_v', 'new_v_norm_f_w': 'new_v'}


def _forward(args):
    return _fwd_reference(*[args[k] for k in FWD_PARAMS])


def _output_shape():
    out = _jax.eval_shape(lambda: _forward(_fwd_setup_inputs(0)))
    return out.shape, out.dtype

N_MICROBATCH = 1
ADAM_LR = 0.001
ADAM_B1 = 0.9
ADAM_B2 = 0.999
ADAM_EPS = 1e-08
ADAM_WD = 0.01
ADAM_STEP = 10
PER_EXAMPLE_BATCH_AXIS = {'x': 0, 'loss_target': 0}
SHARED_INPUTS = []
_WEIGHT_DTYPES = {'norm1_w': _jnp.float32, 'w_in': _jnp.float32, 'dn_conv_w': _jnp.float32, 'dn_A_log': _jnp.float32, 'dn_dt_bias': _jnp.float32, 'dn_norm_w': _jnp.float32, 'w_proj_dn': _jnp.float32, 'w_proj_sb': _jnp.float32, 'w_out': _jnp.float32, 'norm2_w': _jnp.float32, 'ffn_w_up': _jnp.float32, 'ffn_conv_w': _jnp.float32, 'ffn_w_down': _jnp.float32, 'norm_f_w': _jnp.float32}
MOMENT_SCALE = {'norm1_w': 1.368933e-01, 'w_in': 4.185042e-02, 'dn_conv_w': 4.375574e-02, 'dn_A_log': 2.638581e-01, 'dn_dt_bias': 2.496510e-01, 'dn_norm_w': 1.798221e-01, 'w_proj_dn': 5.638491e-02, 'w_proj_sb': 6.217381e-02, 'w_out': 8.433908e-02, 'norm2_w': 1.262048e-01, 'ffn_w_up': 5.217982e-02, 'ffn_conv_w': 5.257503e-02, 'ffn_w_down': 8.519670e-02, 'norm_f_w': 3.202137e+01}


def _to_microbatches(a, axis):
    t = _jnp.moveaxis(a, axis, 0)
    t = t.reshape((N_MICROBATCH, t.shape[0] // N_MICROBATCH) + t.shape[1:])
    return _jnp.moveaxis(t, 1, axis + 1)


def setup_inputs(seed: int = 0) -> dict:
    inp = _fwd_setup_inputs(seed)
    key = _jax.random.fold_in(_jax.random.key(seed), 7919)
    shape, _ = _output_shape()
    out = dict(inp)
    out["loss_target"] = _jax.random.normal(_jax.random.fold_in(key, 0), shape, _jnp.float32)
    for i, name in enumerate(TWIN_WEIGHTS):
        w = inp[name].astype(_jnp.float32)
        if MOMENT_SCALE is None:
            s = _jnp.sqrt(_jnp.mean(_jnp.square(w)) + 1e-30)
        else:
            s = MOMENT_SCALE[name]
        km, kv = _jax.random.split(_jax.random.fold_in(key, i + 1))
        out[name] = w
        out["m_" + name] = s * _jax.random.normal(km, w.shape, _jnp.float32)
        out["v_" + name] = (s * s) * _jax.random.uniform(kv, w.shape, _jnp.float32, 0.5, 1.5)
    if N_MICROBATCH > 1:
        for name, axis in PER_EXAMPLE_BATCH_AXIS.items():
            out[name] = _to_microbatches(out[name], axis)
    return {'x': out['x'], 'norm1_w': out['norm1_w'], 'w_in': out['w_in'], 'dn_conv_w': out['dn_conv_w'], 'dn_A_log': out['dn_A_log'], 'dn_dt_bias': out['dn_dt_bias'], 'dn_norm_w': out['dn_norm_w'], 'w_proj_dn': out['w_proj_dn'], 'w_proj_sb': out['w_proj_sb'], 'w_out': out['w_out'], 'norm2_w': out['norm2_w'], 'ffn_w_up': out['ffn_w_up'], 'ffn_conv_w': out['ffn_conv_w'], 'ffn_w_down': out['ffn_w_down'], 'norm_f_w': out['norm_f_w'], 'loss_target': out['loss_target'], 'm_norm1_w': out['m_norm1_w'], 'm_w_in': out['m_w_in'], 'm_dn_conv_w': out['m_dn_conv_w'], 'm_dn_A_log': out['m_dn_A_log'], 'm_dn_dt_bias': out['m_dn_dt_bias'], 'm_dn_norm_w': out['m_dn_norm_w'], 'm_w_proj_dn': out['m_w_proj_dn'], 'm_w_proj_sb': out['m_w_proj_sb'], 'm_w_out': out['m_w_out'], 'm_norm2_w': out['m_norm2_w'], 'm_ffn_w_up': out['m_ffn_w_up'], 'm_ffn_conv_w': out['m_ffn_conv_w'], 'm_ffn_w_down': out['m_ffn_w_down'], 'm_norm_f_w': out['m_norm_f_w'], 'v_norm1_w': out['v_norm1_w'], 'v_w_in': out['v_w_in'], 'v_dn_conv_w': out['v_dn_conv_w'], 'v_dn_A_log': out['v_dn_A_log'], 'v_dn_dt_bias': out['v_dn_dt_bias'], 'v_dn_norm_w': out['v_dn_norm_w'], 'v_w_proj_dn': out['v_w_proj_dn'], 'v_w_proj_sb': out['v_w_proj_sb'], 'v_w_out': out['v_w_out'], 'v_norm2_w': out['v_norm2_w'], 'v_ffn_w_up': out['v_ffn_w_up'], 'v_ffn_conv_w': out['v_ffn_conv_w'], 'v_ffn_w_down': out['v_ffn_w_down'], 'v_norm_f_w': out['v_norm_f_w']}


def _loss(weights, diff, rest, loss_target):
    with _jax.named_scope("forward"):
        args = {**rest, TWIN_DIFF_INPUT: diff, **{k: w.astype(_WEIGHT_DTYPES[k]) for k, w in weights.items()}}
        y = _forward(args)
    with _jax.named_scope("loss_head"):
        err = _jnp.square(y.astype(_jnp.float32) - loss_target)
        return 0.5 * _jnp.sum(_jnp.mean(err, axis=-1)) if err.ndim else 0.5 * err


def _adamw(w, g, m, v):
    m = ADAM_B1 * m + (1.0 - ADAM_B1) * g
    v = ADAM_B2 * v + (1.0 - ADAM_B2) * _jnp.square(g)
    m_hat = m / (1.0 - ADAM_B1 ** ADAM_STEP)
    v_hat = v / (1.0 - ADAM_B2 ** ADAM_STEP)
    delta = -ADAM_LR * (m_hat / (_jnp.sqrt(v_hat) + ADAM_EPS) + ADAM_WD * w)
    return delta, m, v


def reference(x, norm1_w, w_in, dn_conv_w, dn_A_log, dn_dt_bias, dn_norm_w, w_proj_dn, w_proj_sb, w_out, norm2_w, ffn_w_up, ffn_conv_w, ffn_w_down, norm_f_w, loss_target, m_norm1_w, m_w_in, m_dn_conv_w, m_dn_A_log, m_dn_dt_bias, m_dn_norm_w, m_w_proj_dn, m_w_proj_sb, m_w_out, m_norm2_w, m_ffn_w_up, m_ffn_conv_w, m_ffn_w_down, m_norm_f_w, v_norm1_w, v_w_in, v_dn_conv_w, v_dn_A_log, v_dn_dt_bias, v_dn_norm_w, v_w_proj_dn, v_w_proj_sb, v_w_out, v_norm2_w, v_ffn_w_up, v_ffn_conv_w, v_ffn_w_down, v_norm_f_w):
    given = dict(x=x, norm1_w=norm1_w, w_in=w_in, dn_conv_w=dn_conv_w, dn_A_log=dn_A_log, dn_dt_bias=dn_dt_bias, dn_norm_w=dn_norm_w, w_proj_dn=w_proj_dn, w_proj_sb=w_proj_sb, w_out=w_out, norm2_w=norm2_w, ffn_w_up=ffn_w_up, ffn_conv_w=ffn_conv_w, ffn_w_down=ffn_w_down, norm_f_w=norm_f_w, loss_target=loss_target, m_norm1_w=m_norm1_w, m_w_in=m_w_in, m_dn_conv_w=m_dn_conv_w, m_dn_A_log=m_dn_A_log, m_dn_dt_bias=m_dn_dt_bias, m_dn_norm_w=m_dn_norm_w, m_w_proj_dn=m_w_proj_dn, m_w_proj_sb=m_w_proj_sb, m_w_out=m_w_out, m_norm2_w=m_norm2_w, m_ffn_w_up=m_ffn_w_up, m_ffn_conv_w=m_ffn_conv_w, m_ffn_w_down=m_ffn_w_down, m_norm_f_w=m_norm_f_w, v_norm1_w=v_norm1_w, v_w_in=v_w_in, v_dn_conv_w=v_dn_conv_w, v_dn_A_log=v_dn_A_log, v_dn_dt_bias=v_dn_dt_bias, v_dn_norm_w=v_dn_norm_w, v_w_proj_dn=v_w_proj_dn, v_w_proj_sb=v_w_proj_sb, v_w_out=v_w_out, v_norm2_w=v_norm2_w, v_ffn_w_up=v_ffn_w_up, v_ffn_conv_w=v_ffn_conv_w, v_ffn_w_down=v_ffn_w_down, v_norm_f_w=v_norm_f_w)
    weights = {n: given[n] for n in TWIN_WEIGHTS}
    shared = {n: given[n] for n in SHARED_INPUTS}
    per_example = {n: given[n] for n in ['x']}
    grad_fn = _jax.value_and_grad(_loss, argnums=(0, 1))

    def one_microbatch(ex, loss_target):
        ex = dict(ex)
        diff = ex.pop(TWIN_DIFF_INPUT)
        return grad_fn(weights, diff, {**shared, **ex}, loss_target)

    if N_MICROBATCH == 1:
        loss, (grad_w, grad_x) = one_microbatch(per_example, given["loss_target"])
    else:
        def body(carry, xs):
            loss_sum, grad_sum = carry
            l_k, (gw_k, gx_k) = one_microbatch(xs[0], xs[1])
            with _jax.named_scope("update"):
                return (loss_sum + l_k, _jax.tree.map(_jnp.add, grad_sum, gw_k)), gx_k

        init = (_jnp.zeros((), _jnp.float32), _jax.tree.map(_jnp.zeros_like, weights))
        (loss, grad_w), grad_x = _jax.lax.scan(body, init, (per_example, given["loss_target"]))
    with _jax.named_scope("update"):
        delta_w, new_m, new_v = {}, {}, {}
        for n in TWIN_WEIGHTS:
            delta_w[n], new_m[n], new_v[n] = _adamw(weights[n], grad_w[n], given["m_" + n], given["v_" + n])
    return (loss, grad_x, *[grad_w[n] for n in TWIN_WEIGHTS], *[delta_w[n] for n in TWIN_WEIGHTS],
            *[new_m[n] for n in TWIN_WEIGHTS], *[new_v[n] for n in TWIN_WEIGHTS])
```

```python
import functools

import jax
import jax.numpy as jnp
from jax import lax
from jax.experimental import pallas as pl
from jax.experimental.pallas import tpu as pltpu

F32 = jnp.float32
BF16 = jnp.bfloat16
HIGHEST = lax.Precision.HIGHEST
MESH = pl.DeviceIdType.MESH

EPS = 1e-6
D_MODEL = 1024
N_HEADS = 8
HEAD_DIM = 128
DN_CONV = 4
DN_CHUNK = 64
D_FF = 2816
FFN_CONV = 3
ADAM_LR, ADAM_B1, ADAM_B2, ADAM_EPS, ADAM_WD, ADAM_STEP = 0.001, 0.9, 0.999, 1e-08, 0.01, 10

N_CHIPS = 4
LANES = 128
HALO = 8
VMEM_LIMIT = 48 * 1024 * 1024
PACK_ROWS = 5248
SMALL_ROWS = 8


def _params(sem=None):
    return pltpu.CompilerParams(dimension_semantics=sem, vmem_limit_bytes=VMEM_LIMIT)


def _pick(n, target):
    best = None
    for b in range(LANES, min(n, target) + 1, LANES):
        if n % b == 0:
            best = b
    return best or n


def _rows(t, target=256):
    return min(t, target)


def _dot(a, b, precision=None):
    return lax.dot_general(a, b, (((1,), (0,)), ((), ())), precision=precision, preferred_element_type=F32)


def _dot_nt(a, b, precision=None):
    return lax.dot_general(a, b, (((1,), (1,)), ((), ())), precision=precision, preferred_element_type=F32)


def _dot_tn(a, b, precision=None):
    return lax.dot_general(a, b, (((0,), (0,)), ((), ())), precision=precision, preferred_element_type=F32)


def _rms(x, w):
    return x * lax.rsqrt(jnp.mean(x * x, axis=-1, keepdims=True) + EPS) * w


def _silu(x):
    return x * jax.nn.sigmoid(x)


def _softplus(x):
    return jnp.maximum(x, 0.0) + jnp.log(1.0 + jnp.exp(-jnp.abs(x)))


def _mm(a, b, *, ta=False, tb=False, add=None, out_dtype=F32, name, bm=512, bn=512, bk=1024):
    m = a.shape[1] if ta else a.shape[0]
    k = a.shape[0] if ta else a.shape[1]
    n = b.shape[0] if tb else b.shape[1]
    bm, bn, bk = _pick(m, bm), _pick(n, bn), _pick(k, bk)
    nk = k // bk
    dims = (((0 if ta else 1,), (1 if tb else 0,)), ((), ()))

    def body(*refs):
        if add is None:
            a_ref, b_ref, o_ref, acc = refs
        else:
            a_ref, b_ref, c_ref, o_ref, acc = refs
        kk = pl.program_id(2)

        @pl.when(kk == 0)
        def _():
            acc[...] = jnp.zeros_like(acc)

        acc[...] += lax.dot_general(a_ref[...].astype(BF16), b_ref[...].astype(BF16), dims,
                                    preferred_element_type=F32)

        @pl.when(kk == nk - 1)
        def _():
            r = acc[...]
            if add is not None:
                r = r + c_ref[...].astype(F32)
            o_ref[...] = r.astype(out_dtype)

    a_spec = (pl.BlockSpec((bk, bm), lambda i, j, kk: (kk, i)) if ta
              else pl.BlockSpec((bm, bk), lambda i, j, kk: (i, kk)))
    b_spec = (pl.BlockSpec((bn, bk), lambda i, j, kk: (j, kk)) if tb
              else pl.BlockSpec((bk, bn), lambda i, j, kk: (kk, j)))
    o_spec = pl.BlockSpec((bm, bn), lambda i, j, kk: (i, j))
    in_specs = [a_spec, b_spec] + ([o_spec] if add is not None else [])
    args = (a, b) + ((add,) if add is not None else ())
    return pl.pallas_call(
        body, name=name, grid=(m // bm, n // bn, nk),
        in_specs=in_specs, out_specs=o_spec,
        out_shape=jax.ShapeDtypeStruct((m, n), out_dtype),
        scratch_shapes=[pltpu.VMEM((bm, bn), F32)],
        compiler_params=_params(("parallel", "parallel", "arbitrary")),
    )(*args)


def _norm1_fwd(x, w, w_ab):
    t = x.shape[0]
    tb = _rows(t)

    def body(x_ref, w_ref, wab_ref, n_ref, hab_ref):
        n = _rms(x_ref[...], w_ref[...]).astype(BF16)
        n_ref[...] = n
        hab_ref[...] = _dot(n, wab_ref[...])

    return pl.pallas_call(
        body, name="norm1_fwd", grid=(t // tb,),
        in_specs=[pl.BlockSpec((tb, D_MODEL), lambda i: (i, 0)),
                  pl.BlockSpec((1, D_MODEL), lambda i: (0, 0)),
                  pl.BlockSpec((D_MODEL, LANES), lambda i: (0, 0))],
        out_specs=[pl.BlockSpec((tb, D_MODEL), lambda i: (i, 0)),
                   pl.BlockSpec((tb, LANES), lambda i: (i, 0))],
        out_shape=[jax.ShapeDtypeStruct((t, D_MODEL), BF16), jax.ShapeDtypeStruct((t, LANES), F32)],
        compiler_params=_params(("arbitrary",)),
    )(x, w, w_ab)


def _norm1_bwd(x, w, dn, dres, dab, w_ab):
    t = x.shape[0]
    tb = _rows(t)

    def body(x_ref, w_ref, dn_ref, dres_ref, dab_ref, wab_ref, dx_ref, dw_ref):
        i = pl.program_id(0)
        g = dn_ref[...] + _dot_nt(dab_ref[...].astype(BF16), wab_ref[...])
        _, vjp = jax.vjp(_rms, x_ref[...], w_ref[...])
        dx, dw = vjp(g)
        dx_ref[...] = dres_ref[...] + dx

        @pl.when(i == 0)
        def _():
            dw_ref[...] = jnp.zeros_like(dw_ref)

        dw_ref[...] += dw

    row = pl.BlockSpec((tb, D_MODEL), lambda i: (i, 0))
    vec = pl.BlockSpec((1, D_MODEL), lambda i: (0, 0))
    return pl.pallas_call(
        body, name="norm1_bwd", grid=(t // tb,),
        in_specs=[row, vec, row, row, pl.BlockSpec((tb, LANES), lambda i: (i, 0)),
                  pl.BlockSpec((D_MODEL, LANES), lambda i: (0, 0))],
        out_specs=[row, vec],
        out_shape=[jax.ShapeDtypeStruct((t, D_MODEL), F32), jax.ShapeDtypeStruct((1, D_MODEL), F32)],
        compiler_params=_params(("arbitrary",)),
    )(x, w, dn, dres, dab, w_ab)


def _conv_fwd(x, w, name):
    t, c = x.shape
    kk = w.shape[0]
    tb, cb = _rows(t), _pick(c, 512)
    per = tb // HALO

    def body(x_ref, halo_ref, w_ref, y_ref, buf):
        i = pl.program_id(0)
        buf[pl.ds(HALO, tb), :] = x_ref[...]
        buf[pl.ds(0, HALO), :] = jnp.where(i == 0, 0.0, halo_ref[...])
        y = w_ref[0:1, :] * buf[pl.ds(HALO - (kk - 1), tb), :]
        for s in range(1, kk):
            y = y + w_ref[s:s + 1, :] * buf[pl.ds(HALO - (kk - 1) + s, tb), :]
        y_ref[...] = y

    return pl.pallas_call(
        body, name=name, grid=(t // tb, c // cb),
        in_specs=[pl.BlockSpec((tb, cb), lambda i, j: (i, j)),
                  pl.BlockSpec((HALO, cb), lambda i, j: (jnp.maximum(i * per - 1, 0), j)),
                  pl.BlockSpec((kk, cb), lambda i, j: (0, j))],
        out_specs=pl.BlockSpec((tb, cb), lambda i, j: (i, j)),
        out_shape=jax.ShapeDtypeStruct((t, c), F32),
        scratch_shapes=[pltpu.VMEM((tb + HALO, cb), F32)],
        compiler_params=_params(("parallel", "parallel")),
    )(x, x, w)


def _conv_bwd(dy, x, w, name, dx_dtype):
    t, c = x.shape
    kk = w.shape[0]
    tb, cb = _rows(t), _pick(c, 512)
    per = tb // HALO
    nblk = t // tb

    def body(dy_ref, after_ref, x_ref, before_ref, w_ref, dx_ref, dw_ref, dbuf, xbuf):
        i = pl.program_id(1)
        dy = dy_ref[...]
        dbuf[pl.ds(0, tb), :] = dy
        dbuf[pl.ds(tb, HALO), :] = jnp.where(i == nblk - 1, 0.0, after_ref[...])
        xbuf[pl.ds(HALO, tb), :] = x_ref[...]
        xbuf[pl.ds(0, HALO), :] = jnp.where(i == 0, 0.0, before_ref[...])
        dx = w_ref[0:1, :] * dbuf[pl.ds(kk - 1, tb), :]
        for s in range(1, kk):
            dx = dx + w_ref[s:s + 1, :] * dbuf[pl.ds(kk - 1 - s, tb), :]
        dx_ref[...] = dx.astype(dx_dtype)

        @pl.when(i == 0)
        def _():
            dw_ref[...] = jnp.zeros_like(dw_ref)

        for s in range(kk):
            part = jnp.sum(dy * xbuf[pl.ds(HALO - (kk - 1) + s, tb), :], axis=0, keepdims=True)
            dw_ref[s:s + 1, :] += part

    blk = pl.BlockSpec((tb, cb), lambda j, i: (i, j))
    return pl.pallas_call(
        body, name=name, grid=(c // cb, nblk),
        in_specs=[blk,
                  pl.BlockSpec((HALO, cb), lambda j, i: (jnp.minimum((i + 1) * per, t // HALO - 1), j)),
                  blk,
                  pl.BlockSpec((HALO, cb), lambda j, i: (jnp.maximum(i * per - 1, 0), j)),
                  pl.BlockSpec((kk, cb), lambda j, i: (0, j))],
        out_specs=[blk, pl.BlockSpec((HALO, cb), lambda j, i: (0, j))],
        out_shape=[jax.ShapeDtypeStruct((t, c), dx_dtype), jax.ShapeDtypeStruct((HALO, c), F32)],
        scratch_shapes=[pltpu.VMEM((tb + HALO, cb), F32), pltpu.VMEM((tb + HALO, cb), F32)],
        compiler_params=_params(("parallel", "arbitrary")),
    )(dy, dy, x, x, w)


def _dn_prep_fn(c, hab, alog, dtb):
    s = _silu(c)
    heads = []
    for h in range(2 * N_HEADS):
        sh = s[:, h * HEAD_DIM:(h + 1) * HEAD_DIM]
        heads.append(sh * lax.rsqrt(jnp.sum(sh * sh, axis=-1, keepdims=True) + EPS))
    qn = jnp.concatenate(heads[:N_HEADS], axis=1)
    kn = jnp.concatenate(heads[N_HEADS:], axis=1)
    v = s[:, 2 * D_MODEL:]
    lane = lax.broadcasted_iota(jnp.int32, hab.shape, 1)
    g = -jnp.exp(alog) * _softplus(hab + dtb)
    beta = jax.nn.sigmoid(hab)
    gb = jnp.where(lane < N_HEADS, g, jnp.where(lane < 2 * N_HEADS, beta, 0.0))
    return qn, kn, v, gb


def _dn_prep_fwd(c, hab, alog, dtb):
    t = c.shape[0]
    tb = _rows(t)

    def body(c_ref, hab_ref, alog_ref, dtb_ref, q_ref, k_ref, v_ref, gb_ref):
        qn, kn, v, gb = _dn_prep_fn(c_ref[...], hab_ref[...], alog_ref[...], dtb_ref[...])
        q_ref[...] = qn
        k_ref[...] = kn
        v_ref[...] = v
        gb_ref[...] = gb

    row = pl.BlockSpec((tb, D_MODEL), lambda i: (i, 0))
    nar = pl.BlockSpec((tb, LANES), lambda i: (i, 0))
    vec = pl.BlockSpec((1, LANES), lambda i: (0, 0))
    return pl.pallas_call(
        body, name="dn_prep_fwd", grid=(t // tb,),
        in_specs=[pl.BlockSpec((tb, 3 * D_MODEL), lambda i: (i, 0)), nar, vec, vec],
        out_specs=[row, row, row, nar],
        out_shape=[jax.ShapeDtypeStruct((t, D_MODEL), F32)] * 3 + [jax.ShapeDtypeStruct((t, LANES), F32)],
        compiler_params=_params(("parallel",)),
    )(c, hab, alog, dtb)


def _dn_prep_bwd(c, hab, alog, dtb, dq, dk, dv, dgb):
    t = c.shape[0]
    tb = _rows(t)

    def body(c_ref, hab_ref, alog_ref, dtb_ref, dq_ref, dk_ref, dv_ref, dgb_ref,
             dc_ref, dhab_ref, dalog_ref, ddtb_ref):
        i = pl.program_id(0)
        _, vjp = jax.vjp(_dn_prep_fn, c_ref[...], hab_ref[...], alog_ref[...], dtb_ref[...])
        dc, dhab, dalog, ddtb = vjp((dq_ref[...], dk_ref[...], dv_ref[...], dgb_ref[...]))
        dc_ref[...] = dc
        dhab_ref[...] = dhab

        @pl.when(i == 0)
        def _():
            dalog_ref[...] = jnp.zeros_like(dalog_ref)
            ddtb_ref[...] = jnp.zeros_like(ddtb_ref)

        dalog_ref[...] += dalog
        ddtb_ref[...] += ddtb

    row = pl.BlockSpec((tb, D_MODEL), lambda i: (i, 0))
    wide = pl.BlockSpec((tb, 3 * D_MODEL), lambda i: (i, 0))
    nar = pl.BlockSpec((tb, LANES), lambda i: (i, 0))
    vec = pl.BlockSpec((1, LANES), lambda i: (0, 0))
    return pl.pallas_call(
        body, name="dn_prep_bwd", grid=(t // tb,),
        in_specs=[wide, nar, vec, vec, row, row, row, nar],
        out_specs=[wide, nar, vec, vec],
        out_shape=[jax.ShapeDtypeStruct((t, 3 * D_MODEL), F32), jax.ShapeDtypeStruct((t, LANES), F32),
                   jax.ShapeDtypeStruct((1, LANES), F32), jax.ShapeDtypeStruct((1, LANES), F32)],
        compiler_params=_params(("arbitrary",)),
    )(c, hab, alog, dtb, dq, dk, dv, dgb)


def _dn_chunk(q, k, v, gcol, grow, bcol, s_in):
    c = q.shape[0]
    ri = lax.broadcasted_iota(jnp.int32, (c, c), 0)
    ci = lax.broadcasted_iota(jnp.int32, (c, c), 1)
    lower = ri >= ci
    strict = ri > ci
    gc_col = jnp.sum(jnp.where(lower, jnp.broadcast_to(grow, (c, c)), 0.0), axis=1, keepdims=True)
    gc_row = jnp.sum(jnp.where(ri <= ci, jnp.broadcast_to(gcol, (c, c)), 0.0), axis=0, keepdims=True)
    q = q * (HEAD_DIM ** -0.5)
    kb = k * bcol
    vb = v * bcol
    diff = gc_col - gc_row
    decay = jnp.where(lower, jnp.exp(jnp.where(lower, diff, 0.0)), 0.0)
    lmat = jnp.where(strict, _dot_nt(kb, k, HIGHEST) * decay, 0.0)
    neg = -lmat
    tinv = jnp.where(ri == ci, 1.0, 0.0) + neg
    p = neg
    steps = max(c.bit_length() - 2, 0)
    for _ in range(steps):
        p = _dot(p, p, HIGHEST)
        tinv = tinv + _dot(tinv, p, HIGHEST)
    eg = jnp.exp(gc_col)
    u = _dot(tinv, vb, HIGHEST)
    w = _dot(tinv, kb * eg, HIGHEST)
    a_qk = jnp.where(lower, _dot_nt(q, k, HIGHEST) * decay, 0.0)
    v_new = u - _dot(w, s_in, HIGHEST)
    o = _dot(q * eg, s_in, HIGHEST) + _dot(a_qk, v_new, HIGHEST)
    g_last = jnp.sum(grow, axis=1, keepdims=True)
    s_out = s_in * jnp.exp(g_last) + _dot_tn(k * jnp.exp(g_last - gc_col), v_new, HIGHEST)
    return o, s_out


def _dn_specs(nchunk, rev):
    def idx(n):
        return nchunk - 1 - n if rev else n

    blk = pl.BlockSpec((DN_CHUNK, HEAD_DIM), lambda h, n: (idx(n), h))
    col = pl.BlockSpec((1, 1, DN_CHUNK, 1), lambda h, n: (h, idx(n), 0, 0))
    row = pl.BlockSpec((1, 1, 1, DN_CHUNK), lambda h, n: (h, idx(n), 0, 0))
    st = pl.BlockSpec((1, 1, HEAD_DIM, HEAD_DIM), lambda h, n: (h, idx(n), 0, 0))
    return blk, col, row, st


def _dn_fwd(q, k, v, gcol, grow, bcol):
    t = q.shape[0]
    nchunk = t // DN_CHUNK
    blk, col, row, st = _dn_specs(nchunk, False)

    def body(q_ref, k_ref, v_ref, gc_ref, gr_ref, bc_ref, o_ref, s_ref, state):
        n = pl.program_id(1)

        @pl.when(n == 0)
        def _():
            state[...] = jnp.zeros_like(state)

        s_in = state[...]
        s_ref[0, 0] = s_in
        o, s_out = _dn_chunk(q_ref[...], k_ref[...], v_ref[...], gc_ref[0, 0], gr_ref[0, 0], bc_ref[0, 0], s_in)
        o_ref[...] = o
        state[...] = s_out

    return pl.pallas_call(
        body, name="dn_fwd", grid=(N_HEADS, nchunk),
        in_specs=[blk, blk, blk, col, row, col],
        out_specs=[blk, st],
        out_shape=[jax.ShapeDtypeStruct((t, D_MODEL), F32),
                   jax.ShapeDtypeStruct((N_HEADS, nchunk, HEAD_DIM, HEAD_DIM), F32)],
        scratch_shapes=[pltpu.VMEM((HEAD_DIM, HEAD_DIM), F32)],
        compiler_params=_params(("parallel", "arbitrary")),
    )(q, k, v, gcol, grow, bcol)


def _dn_bwd(q, k, v, gcol, grow, bcol, states, do):
    t = q.shape[0]
    nchunk = t // DN_CHUNK
    blk, col, row, st = _dn_specs(nchunk, True)

    def body(q_ref, k_ref, v_ref, gc_ref, gr_ref, bc_ref, s_ref, do_ref,
             dq_ref, dk_ref, dv_ref, dgc_ref, dgr_ref, dbc_ref, dstate):
        n = pl.program_id(1)

        @pl.when(n == 0)
        def _():
            dstate[...] = jnp.zeros_like(dstate)

        _, vjp = jax.vjp(_dn_chunk, q_ref[...], k_ref[...], v_ref[...], gc_ref[0, 0], gr_ref[0, 0],
                         bc_ref[0, 0], s_ref[0, 0])
        dq, dk, dv, dgc, dgr, dbc, ds = vjp((do_ref[...], dstate[...]))
        dq_ref[...] = dq
        dk_ref[...] = dk
        dv_ref[...] = dv
        dgc_ref[0, 0] = dgc
        dgr_ref[0, 0] = dgr
        dbc_ref[0, 0] = dbc
        dstate[...] = ds

    big = jax.ShapeDtypeStruct((t, D_MODEL), F32)
    colshape = jax.ShapeDtypeStruct((N_HEADS, nchunk, DN_CHUNK, 1), F32)
    rowshape = jax.ShapeDtypeStruct((N_HEADS, nchunk, 1, DN_CHUNK), F32)
    return pl.pallas_call(
        body, name="dn_bwd", grid=(N_HEADS, nchunk),
        in_specs=[blk, blk, blk, col, row, col, st, blk],
        out_specs=[blk, blk, blk, col, row, col],
        out_shape=[big, big, big, colshape, rowshape, colshape],
        scratch_shapes=[pltpu.VMEM((HEAD_DIM, HEAD_DIM), F32)],
        compiler_params=_params(("parallel", "arbitrary")),
    )(q, k, v, gcol, grow, bcol, states, do)


def _dn_post_fn(o, gate, w):
    outs = []
    for h in range(N_HEADS):
        sl = slice(h * HEAD_DIM, (h + 1) * HEAD_DIM)
        outs.append(_rms(o[:, sl], w) * _silu(gate[:, sl]))
    return jnp.concatenate(outs, axis=1)


def _dn_post_fwd(o, gate, w):
    t = o.shape[0]
    tb = _rows(t)

    def body(o_ref, g_ref, w_ref, y_ref):
        y_ref[...] = _dn_post_fn(o_ref[...], g_ref[...], w_ref[...]).astype(BF16)

    row = pl.BlockSpec((tb, D_MODEL), lambda i: (i, 0))
    return pl.pallas_call(
        body, name="dn_post_fwd", grid=(t // tb,),
        in_specs=[row, row, pl.BlockSpec((1, HEAD_DIM), lambda i: (0, 0))],
        out_specs=row, out_shape=jax.ShapeDtypeStruct((t, D_MODEL), BF16),
        compiler_params=_params(("parallel",)),
    )(o, gate, w)


def _dn_post_bwd(o, gate, w, dy):
    t = o.shape[0]
    tb = _rows(t)

    def body(o_ref, g_ref, w_ref, dy_ref, do_ref, dg_ref, dw_ref):
        i = pl.program_id(0)
        _, vjp = jax.vjp(_dn_post_fn, o_ref[...], g_ref[...], w_ref[...])
        do, dg, dw = vjp(dy_ref[...])
        do_ref[...] = do
        dg_ref[...] = dg.astype(BF16)

        @pl.when(i == 0)
        def _():
            dw_ref[...] = jnp.zeros_like(dw_ref)

        dw_ref[...] += dw

    row = pl.BlockSpec((tb, D_MODEL), lambda i: (i, 0))
    vec = pl.BlockSpec((1, HEAD_DIM), lambda i: (0, 0))
    return pl.pallas_call(
        body, name="dn_post_bwd", grid=(t // tb,),
        in_specs=[row, row, vec, row],
        out_specs=[row, row, vec],
        out_shape=[jax.ShapeDtypeStruct((t, D_MODEL), F32), jax.ShapeDtypeStruct((t, D_MODEL), BF16),
                   jax.ShapeDtypeStruct((1, HEAD_DIM), F32)],
        compiler_params=_params(("arbitrary",)),
    )(o, gate, w, dy)


def _split_bf16(x):
    hi = x.astype(BF16)
    lo = (x - hi.astype(F32)).astype(BF16)
    return hi, lo


def _sb_logits(q, kb, off, tpos, scale):
    z = _dot_nt(q, kb) * scale
    ls = jnp.minimum(z, 0.0) - jnp.log(1.0 + jnp.exp(-jnp.abs(z)))
    spos = off + lax.broadcasted_iota(jnp.int32, z.shape, 1)
    mask = spos < tpos
    lk = jnp.where(mask, ls - z, 0.0)
    return ls, lk, mask


def _sb_fwd(qkv, blk=256):
    t = qkv.shape[0]
    blk = min(blk, t)
    scale = HEAD_DIM ** -0.5

    def body(q_ref, k_ref, v_ref, o_ref, tot_ref):
        i = pl.program_id(1)
        q = q_ref[...]
        rj = lax.broadcasted_iota(jnp.int32, (blk, blk), 0)
        cj = lax.broadcasted_iota(jnp.int32, (blk, blk), 1)
        after = (rj > cj).astype(BF16)
        tpos = i * blk + rj

        def step(it, carry):
            run, acc = carry
            j = i - it
            off = pl.multiple_of(j * blk, blk)
            kb = k_ref[pl.ds(off, blk), :]
            vb = v_ref[pl.ds(off, blk), :]
            ls, lk, mask = _sb_logits(q, kb, off, tpos, scale)
            hi, lo = _split_bf16(lk)
            between = _dot(hi, after) + _dot(lo, after) + run
            a = jnp.where(mask, jnp.exp(ls + between), 0.0)
            acc = acc + _dot(a.astype(BF16), vb)
            run = run + jnp.sum(lk, axis=1, keepdims=True)
            return run, acc

        run, acc = lax.fori_loop(0, i + 1, step, (jnp.zeros((blk, 1), F32), jnp.zeros((blk, HEAD_DIM), F32)))
        o_ref[...] = acc.astype(BF16)
        tot_ref[...] = jnp.broadcast_to(run, (blk, HEAD_DIM))

    qs = pl.BlockSpec((blk, HEAD_DIM), lambda h, i: (i, h))
    ks = pl.BlockSpec((t, HEAD_DIM), lambda h, i: (0, N_HEADS + h))
    vs = pl.BlockSpec((t, HEAD_DIM), lambda h, i: (0, 2 * N_HEADS + h))
    return pl.pallas_call(
        body, name="sb_fwd", grid=(N_HEADS, t // blk),
        in_specs=[qs, ks, vs], out_specs=[qs, qs],
        out_shape=[jax.ShapeDtypeStruct((t, D_MODEL), BF16), jax.ShapeDtypeStruct((t, D_MODEL), F32)],
        compiler_params=_params(("parallel", "arbitrary")),
    )(qkv, qkv, qkv)


def _sb_bwd(qkv, tot, do, blk=256):
    t = qkv.shape[0]
    blk = min(blk, t)
    scale = HEAD_DIM ** -0.5

    def body(q_ref, k_ref, v_ref, tot_ref, do_ref, dq_ref, dk_ref, dv_ref):
        i = pl.program_id(1)

        @pl.when(i == 0)
        def _():
            dk_ref[...] = jnp.zeros_like(dk_ref)
            dv_ref[...] = jnp.zeros_like(dv_ref)

        q = q_ref[...]
        do = do_ref[...]
        total = tot_ref[:, 0:1]
        rj = lax.broadcasted_iota(jnp.int32, (blk, blk), 0)
        cj = lax.broadcasted_iota(jnp.int32, (blk, blk), 1)
        upto = (rj <= cj).astype(BF16)
        before = (rj < cj).astype(BF16)
        tpos = i * blk + rj

        def step(j, carry):
            run_k, run_e, dq = carry
            off = pl.multiple_of(j * blk, blk)
            kb = k_ref[pl.ds(off, blk), :]
            vb = v_ref[pl.ds(off, blk), :]
            ls, lk, mask = _sb_logits(q, kb, off, tpos, scale)
            hi, lo = _split_bf16(lk)
            between = total - (_dot(hi, upto) + _dot(lo, upto) + run_k)
            a = jnp.where(mask, jnp.exp(ls + between), 0.0)
            e = a * _dot_nt(do, vb)
            ehi, elo = _split_bf16(e)
            pre = _dot(ehi, before) + _dot(elo, before) + run_e
            sig = jnp.exp(ls)
            dz = (jnp.where(mask, e * (1.0 - sig) - pre * sig, 0.0) * scale).astype(BF16)
            dq = dq + _dot(dz, kb)
            dk_ref[pl.ds(off, blk), :] += _dot_tn(dz, q)
            dv_ref[pl.ds(off, blk), :] += _dot_tn(a.astype(BF16), do)
            return (run_k + jnp.sum(lk, axis=1, keepdims=True),
                    run_e + jnp.sum(e, axis=1, keepdims=True), dq)

        zero = jnp.zeros((blk, 1), F32)
        _, _, dq = lax.fori_loop(0, i + 1, step, (zero, zero, jnp.zeros((blk, HEAD_DIM), F32)))
        dq_ref[...] = dq

    qs = pl.BlockSpec((blk, HEAD_DIM), lambda h, i: (i, h))
    ks = pl.BlockSpec((t, HEAD_DIM), lambda h, i: (0, N_HEADS + h))
    vs = pl.BlockSpec((t, HEAD_DIM), lambda h, i: (0, 2 * N_HEADS + h))
    full = pl.BlockSpec((t, HEAD_DIM), lambda h, i: (0, h))
    big = jax.ShapeDtypeStruct((t, D_MODEL), F32)
    return pl.pallas_call(
        body, name="sb_bwd", grid=(N_HEADS, t // blk),
        in_specs=[qs, ks, vs, qs, qs], out_specs=[qs, full, full],
        out_shape=[big, big, big],
        compiler_params=_params(("parallel", "arbitrary")),
    )(qkv, qkv, qkv, tot, do)


def _merge_fwd(o_dn, o_sb, gl, x, wp_dn, wp_sb, w_out, w2):
    t = x.shape[0]
    tb = _rows(t)

    def body(odn_ref, osb_ref, gl_ref, x_ref, wpd_ref, wps_ref, wo_ref, w2_ref,
             pdn_ref, psb_ref, mix_ref, x1_ref, n2_ref):
        pdn = _dot(odn_ref[...], wpd_ref[...])
        psb = _dot(osb_ref[...], wps_ref[...])
        gates = jax.nn.sigmoid(gl_ref[...])
        mixed = (gates[:, :D_MODEL] * pdn + gates[:, D_MODEL:] * psb).astype(BF16)
        x1 = x_ref[...] + _dot(mixed, wo_ref[...])
        pdn_ref[...] = pdn
        psb_ref[...] = psb
        mix_ref[...] = mixed
        x1_ref[...] = x1
        n2_ref[...] = _rms(x1, w2_ref[...]).astype(BF16)

    row = pl.BlockSpec((tb, D_MODEL), lambda i: (i, 0))
    sq = pl.BlockSpec((D_MODEL, D_MODEL), lambda i: (0, 0))
    f = jax.ShapeDtypeStruct((t, D_MODEL), F32)
    b = jax.ShapeDtypeStruct((t, D_MODEL), BF16)
    return pl.pallas_call(
        body, name="merge_fwd", grid=(t // tb,),
        in_specs=[row, row, pl.BlockSpec((tb, 2 * D_MODEL), lambda i: (i, 0)), row, sq, sq, sq,
                  pl.BlockSpec((1, D_MODEL), lambda i: (0, 0))],
        out_specs=[row] * 5, out_shape=[f, f, b, f, b],
        compiler_params=_params(("parallel",)),
    )(o_dn, o_sb, gl, x, wp_dn, wp_sb, w_out, w2)


def _merge_bwd(dx2, dn2, x1, w2, gl, pdn, psb, wp_dn, wp_sb, w_out):
    t = x1.shape[0]
    tb = _rows(t)

    def body(dx2_ref, dn2_ref, x1_ref, w2_ref, gl_ref, pdn_ref, psb_ref, wpd_ref, wps_ref, wo_ref,
             dx1_ref, dw2_ref, dgl_ref, dpdn_ref, dpsb_ref, dodn_ref, dosb_ref):
        i = pl.program_id(0)
        _, vjp = jax.vjp(_rms, x1_ref[...], w2_ref[...])
        dxn, dw2 = vjp(dn2_ref[...])
        dx1 = dx2_ref[...] + dxn
        dx1_ref[...] = dx1

        @pl.when(i == 0)
        def _():
            dw2_ref[...] = jnp.zeros_like(dw2_ref)

        dw2_ref[...] += dw2
        dmix = _dot_nt(dx1.astype(BF16), wo_ref[...])
        gates = jax.nn.sigmoid(gl_ref[...])
        g_dn, g_sb = gates[:, :D_MODEL], gates[:, D_MODEL:]
        dpdn = (dmix * g_dn).astype(BF16)
        dpsb = (dmix * g_sb).astype(BF16)
        dgl_ref[:, :D_MODEL] = (dmix * pdn_ref[...] * g_dn * (1.0 - g_dn)).astype(BF16)
        dgl_ref[:, D_MODEL:] = (dmix * psb_ref[...] * g_sb * (1.0 - g_sb)).astype(BF16)
        dpdn_ref[...] = dpdn
        dpsb_ref[...] = dpsb
        dodn_ref[...] = _dot_nt(dpdn, wpd_ref[...])
        dosb_ref[...] = _dot_nt(dpsb, wps_ref[...]).astype(BF16)

    row = pl.BlockSpec((tb, D_MODEL), lambda i: (i, 0))
    wide = pl.BlockSpec((tb, 2 * D_MODEL), lambda i: (i, 0))
    sq = pl.BlockSpec((D_MODEL, D_MODEL), lambda i: (0, 0))
    vec = pl.BlockSpec((1, D_MODEL), lambda i: (0, 0))
    f = jax.ShapeDtypeStruct((t, D_MODEL), F32)
    b = jax.ShapeDtypeStruct((t, D_MODEL), BF16)
    return pl.pallas_call(
        body, name="merge_bwd", grid=(t // tb,),
        in_specs=[row, row, row, vec, wide, row, row, sq, sq, sq],
        out_specs=[row, vec, wide, row, row, row, row],
        out_shape=[f, jax.ShapeDtypeStruct((1, D_MODEL), F32), jax.ShapeDtypeStruct((t, 2 * D_MODEL), BF16),
                   b, b, f, b],
        compiler_params=_params(("arbitrary",)),
    )(dx2, dn2, x1, w2, gl, pdn, psb, wp_dn, wp_sb, w_out)


def _swiglu_fwd(ug, uu):
    t, c = ug.shape
    tb, cb = _rows(t), _pick(c, 512)

    def body(g_ref, u_ref, a_ref):
        a_ref[...] = (_silu(g_ref[...]) * u_ref[...]).astype(BF16)

    blk = pl.BlockSpec((tb, cb), lambda i, j: (i, j))
    return pl.pallas_call(
        body, name="swiglu_fwd", grid=(t // tb, c // cb), in_specs=[blk, blk], out_specs=blk,
        out_shape=jax.ShapeDtypeStruct((t, c), BF16), compiler_params=_params(("parallel", "parallel")),
    )(ug, uu)


def _swiglu_bwd(ug, uu, da):
    t, c = ug.shape
    tb, cb = _rows(t), _pick(c, 512)

    def body(g_ref, u_ref, da_ref, dg_ref, du_ref):
        _, vjp = jax.vjp(lambda g, u: _silu(g) * u, g_ref[...], u_ref[...])
        dg, du = vjp(da_ref[...])
        dg_ref[...] = dg
        du_ref[...] = du

    blk = pl.BlockSpec((tb, cb), lambda i, j: (i, j))
    f = jax.ShapeDtypeStruct((t, c), F32)
    return pl.pallas_call(
        body, name="swiglu_bwd", grid=(t // tb, c // cb), in_specs=[blk, blk, blk], out_specs=[blk, blk],
        out_shape=[f, f], compiler_params=_params(("parallel", "parallel")),
    )(ug, uu, da)


def _down_loss(a, w_down, x1, wf, target):
    t = x1.shape[0]
    tb = _rows(t)

    def body(a_ref, wd_ref, x1_ref, wf_ref, tgt_ref, dx2_ref, dwf_ref, loss_ref):
        i = pl.program_id(0)
        x2 = x1_ref[...] + _dot(a_ref[...], wd_ref[...])
        y, vjp = jax.vjp(_rms, x2, wf_ref[...])
        err = y - tgt_ref[...]
        dx2, dwf = vjp(err * (1.0 / D_MODEL))
        dx2_ref[...] = dx2
        part = jnp.sum(jnp.sum(err * err, axis=1, keepdims=True), axis=0, keepdims=True) * (0.5 / D_MODEL)

        @pl.when(i == 0)
        def _():
            dwf_ref[...] = jnp.zeros_like(dwf_ref)
            loss_ref[...] = jnp.zeros_like(loss_ref)

        dwf_ref[...] += dwf
        loss_ref[...] += jnp.broadcast_to(part, loss_ref.shape)

    row = pl.BlockSpec((tb, D_MODEL), lambda i: (i, 0))
    vec = pl.BlockSpec((1, D_MODEL), lambda i: (0, 0))
    return pl.pallas_call(
        body, name="down_loss", grid=(t // tb,),
        in_specs=[pl.BlockSpec((tb, D_FF), lambda i: (i, 0)), pl.BlockSpec((D_FF, D_MODEL), lambda i: (0, 0)),
                  row, vec, row],
        out_specs=[row, vec, pl.BlockSpec((1, LANES), lambda i: (0, 0))],
        out_shape=[jax.ShapeDtypeStruct((t, D_MODEL), F32), jax.ShapeDtypeStruct((1, D_MODEL), F32),
                   jax.ShapeDtypeStruct((1, LANES), F32)],
        compiler_params=_params(("arbitrary",)),
    )(a, w_down, x1, wf, target)


def _local_step(x, target, wts):
    t = x.shape[0]
    nchunk = t // DN_CHUNK

    n1, hab = _norm1_fwd(x, wts["norm1"], wts["w_ab"])
    dnqkv = _mm(n1, wts["w_dnqkv"], name="h_dnqkv")
    dngate = _mm(n1, wts["w_dngate"], name="h_dngate")
    sbqkv = _mm(n1, wts["w_sbqkv"], out_dtype=BF16, name="h_sbqkv")
    gl = _mm(n1, wts["w_gl"], name="h_gl")

    cdn = _conv_fwd(dnqkv, wts["dn_conv"], "dn_conv_fwd")
    qn, kn, vv, gb = _dn_prep_fwd(cdn, hab, wts["alog"], wts["dtb"])
    per_head = gb[:, :2 * N_HEADS].T.reshape(2 * N_HEADS, nchunk, DN_CHUNK)
    gcol, bcol = per_head[:N_HEADS, :, :, None], per_head[N_HEADS:, :, :, None]
    grow = per_head[:N_HEADS, :, None, :]
    o_raw, states = _dn_fwd(qn, kn, vv, gcol, grow, bcol)
    o_dn = _dn_post_fwd(o_raw, dngate, wts["dn_norm"])

    o_sb, tot = _sb_fwd(sbqkv)

    pdn, psb, mixed, x1, n2 = _merge_fwd(o_dn, o_sb, gl, x, wts["wp_dn"], wts["wp_sb"], wts["w_out"],
                                         wts["norm2"])
    pre_g = _mm(n2, wts["w_up_g"], name="ffn_up_g")
    pre_u = _mm(n2, wts["w_up_u"], name="ffn_up_u")
    ug = _conv_fwd(pre_g, wts["ffn_conv_g"], "ffn_conv_g_fwd")
    uu = _conv_fwd(pre_u, wts["ffn_conv_u"], "ffn_conv_u_fwd")
    act = _swiglu_fwd(ug, uu)
    dx2, d_normf, loss_part = _down_loss(act, wts["w_down"], x1, wts["normf"], target)

    grads = {"normf": d_normf}
    da = _mm(dx2, wts["w_down"], tb=True, name="d_act")
    grads["w_down"] = _mm(act, dx2, ta=True, name="dw_down")
    dug, duu = _swiglu_bwd(ug, uu, da)
    dpre_g, dcw_g = _conv_bwd(dug, pre_g, wts["ffn_conv_g"], "ffn_conv_g_bwd", BF16)
    dpre_u, dcw_u = _conv_bwd(duu, pre_u, wts["ffn_conv_u"], "ffn_conv_u_bwd", BF16)
    grads["ffn_conv"] = jnp.concatenate([dcw_g[:FFN_CONV], dcw_u[:FFN_CONV]], axis=1)
    dn2 = _mm(dpre_g, wts["w_up_g"], tb=True, name="dn2_g")
    dn2 = _mm(dpre_u, wts["w_up_u"], tb=True, add=dn2, name="dn2_u")
    grads["w_up"] = jnp.concatenate([_mm(n2, dpre_g, ta=True, name="dw_up_g"),
                                     _mm(n2, dpre_u, ta=True, name="dw_up_u")], axis=1)

    dx1, grads["norm2"], dgl, dpdn, dpsb, do_dn, do_sb = _merge_bwd(
        dx2, dn2, x1, wts["norm2"], gl, pdn, psb, wts["wp_dn"], wts["wp_sb"], wts["w_out"])
    grads["w_out"] = _mm(mixed, dx1, ta=True, name="dw_out")
    grads["wp_dn"] = _mm(o_dn, dpdn, ta=True, name="dw_proj_dn")
    grads["wp_sb"] = _mm(o_sb, dpsb, ta=True, name="dw_proj_sb")

    dsq, dsk, dsv = _sb_bwd(sbqkv, tot, do_sb)
    dsbqkv = jnp.concatenate([dsq, dsk, dsv], axis=1).astype(BF16)

    do_raw, ddngate, grads["dn_norm"] = _dn_post_bwd(o_raw, dngate, wts["dn_norm"], do_dn)
    dqn, dkn, dvv, dgcol, dgrow, dbcol = _dn_bwd(qn, kn, vv, gcol, grow, bcol, states, do_raw)
    dg = (dgcol[..., 0] + dgrow[:, :, 0, :]).reshape(N_HEADS, t)
    dgb = jnp.concatenate([dg, dbcol[..., 0].reshape(N_HEADS, t)], axis=0).T
    dgb = jnp.pad(dgb, ((0, 0), (0, LANES - 2 * N_HEADS)))
    dcdn, dhab, grads["alog"], grads["dtb"] = _dn_prep_bwd(cdn, hab, wts["alog"], wts["dtb"], dqn, dkn, dvv, dgb)
    ddnqkv, dcw_dn = _conv_bwd(dcdn, dnqkv, wts["dn_conv"], "dn_conv_bwd", BF16)
    grads["dn_conv"] = dcw_dn[:DN_CONV]

    dn1 = _mm(ddnqkv, wts["w_dnqkv"], tb=True, name="dn1_dnqkv")
    dn1 = _mm(ddngate, wts["w_dngate"], tb=True, add=dn1, name="dn1_dngate")
    dn1 = _mm(dsbqkv, wts["w_sbqkv"], tb=True, add=dn1, name="dn1_sbqkv")
    dn1 = _mm(dgl, wts["w_gl"], tb=True, add=dn1, name="dn1_gl")
    grads["w_dnqkv"] = _mm(n1, ddnqkv, ta=True, name="dw_dnqkv")
    grads["w_dngate"] = _mm(n1, ddngate, ta=True, name="dw_dngate")
    grads["w_sbqkv"] = _mm(n1, dsbqkv, ta=True, name="dw_sbqkv")
    grads["w_gl"] = _mm(n1, dgl, ta=True, name="dw_gl")
    grads["w_ab"] = _mm(n1, dhab, ta=True, name="dw_ab")
    grad_x, grads["norm1"] = _norm1_bwd(x, wts["norm1"], dn1, dx1, dhab, wts["w_ab"])
    return loss_part, grad_x, grads


def _place():
    return lax.axis_index("x"), lax.axis_index("y"), lax.axis_index("c")


def _gather_shards(shard):
    rows, cols = shard.shape
    half = rows // 2

    def body(in_ref, out_ref, send_sems, recv_sems, local_sem):
        x, y, c = _place()
        me = 2 * x + y
        sibling = (x, y, 1 - c)
        chips = [(1 - x, y), (x, 1 - y), (1 - x, 1 - y)]

        def slab(chip_index, part):
            return out_ref.at[chip_index, pl.ds(part * half, half), :]

        def copy(k, src, dst, to):
            return pltpu.make_async_remote_copy(src_ref=src, dst_ref=dst, send_sem=send_sems.at[k],
                                                recv_sem=recv_sems.at[k], device_id=to, device_id_type=MESH)

        mine = pltpu.make_async_copy(in_ref, out_ref.at[me], local_sem)
        mine.start()
        my_half = in_ref.at[pl.ds(c * half, half), :]
        first = [copy(j, my_half, slab(me, c), (px, py, c)) for j, (px, py) in enumerate(chips)]
        for cp in first:
            cp.start()
        passed = []
        for j, (px, py) in enumerate(chips):
            landed = slab(2 * px + py, c)
            copy(j, landed, landed, (px, py, c)).wait_recv()
            fwd = copy(3 + j, landed, landed, sibling)
            fwd.start()
            passed.append(fwd)
        for j, (px, py) in enumerate(chips):
            there = slab(2 * px + py, 1 - c)
            copy(3 + j, there, there, sibling).wait_recv()
        for cp in first + passed:
            cp.wait_send()
        mine.wait()

    return pl.pallas_call(
        body, name="gather_weights",
        in_specs=[pl.BlockSpec(memory_space=pltpu.HBM)],
        out_specs=pl.BlockSpec(memory_space=pltpu.HBM),
        out_shape=jax.ShapeDtypeStruct((N_CHIPS, rows, cols), shard.dtype),
        scratch_shapes=[pltpu.SemaphoreType.DMA((6,)), pltpu.SemaphoreType.DMA((6,)), pltpu.SemaphoreType.DMA],
    )(shard)


def _pair_exchange_halves(g):
    nsh, rows, cols = g.shape
    half = rows // 2

    def body(in_ref, out_ref, send_sem, recv_sem):
        x, y, c = _place()
        src = in_ref.at[:, pl.ds((1 - c) * half, half), :]
        cp = pltpu.make_async_remote_copy(src_ref=src, dst_ref=out_ref, send_sem=send_sem, recv_sem=recv_sem,
                                          device_id=(x, y, 1 - c), device_id_type=MESH)
        cp.start()
        cp.wait()

    return pl.pallas_call(
        body, name="grad_pair_exchange",
        in_specs=[pl.BlockSpec(memory_space=pltpu.HBM)],
        out_specs=pl.BlockSpec(memory_space=pltpu.HBM),
        out_shape=jax.ShapeDtypeStruct((nsh, half, cols), g.dtype),
        scratch_shapes=[pltpu.SemaphoreType.DMA, pltpu.SemaphoreType.DMA],
    )(g)


def _pair_add(g, got, c_idx):
    nsh, rows, cols = g.shape
    half = rows // 2
    rb = _pick_rows(half)

    def body(c_ref, g_ref, got_ref, o_ref):
        o_ref[...] = (g_ref[...].astype(F32) + got_ref[...].astype(F32)).astype(BF16)

    nb = half // rb
    grid_spec = pltpu.PrefetchScalarGridSpec(
        num_scalar_prefetch=1, grid=(nsh, nb),
        in_specs=[pl.BlockSpec((1, rb, cols), lambda s, i, c_ref: (s, c_ref[0] * nb + i, 0)),
                  pl.BlockSpec((1, rb, cols), lambda s, i, c_ref: (s, i, 0))],
        out_specs=pl.BlockSpec((1, rb, cols), lambda s, i, c_ref: (s, i, 0)))
    return pl.pallas_call(
        body, name="grad_pair_add", grid_spec=grid_spec,
        out_shape=jax.ShapeDtypeStruct((nsh, half, cols), BF16),
        compiler_params=_params(("parallel", "parallel")),
    )(c_idx, g, got)


def _pick_rows(n, target=512):
    best = 16
    for b in range(16, min(n, target) + 1, 16):
        if n % b == 0:
            best = b
    return best


def _chip_exchange(p):
    nsh, half, cols = p.shape

    def body(in_ref, out_ref, send_sems, recv_sems, local_sem):
        x, y, c = _place()
        me = 2 * x + y
        chips = [(1 - x, y), (x, 1 - y), (1 - x, 1 - y)]
        mine = pltpu.make_async_copy(in_ref.at[me], out_ref.at[me], local_sem)
        mine.start()
        sends = []
        for j, (px, py) in enumerate(chips):
            cp = pltpu.make_async_remote_copy(src_ref=in_ref.at[2 * px + py], dst_ref=out_ref.at[me],
                                              send_sem=send_sems.at[j], recv_sem=recv_sems.at[j],
                                              device_id=(px, py, c), device_id_type=MESH)
            cp.start()
            sends.append(cp)
        for j, (px, py) in enumerate(chips):
            there = out_ref.at[2 * px + py]
            pltpu.make_async_remote_copy(src_ref=there, dst_ref=there, send_sem=send_sems.at[j],
                                         recv_sem=recv_sems.at[j], device_id=(px, py, c),
                                         device_id_type=MESH).wait_recv()
        for cp in sends:
            cp.wait_send()
        mine.wait()

    return pl.pallas_call(
        body, name="grad_chip_exchange",
        in_specs=[pl.BlockSpec(memory_space=pltpu.HBM)],
        out_specs=pl.BlockSpec(memory_space=pltpu.HBM),
        out_shape=jax.ShapeDtypeStruct((nsh, half, cols), p.dtype),
        scratch_shapes=[pltpu.SemaphoreType.DMA((3,)), pltpu.SemaphoreType.DMA((3,)), pltpu.SemaphoreType.DMA],
    )(p)


def _sum_slots(parts):
    nsh, half, cols = parts.shape
    rb = _pick_rows(half)

    def body(p_ref, o_ref):
        acc = p_ref[0].astype(F32)
        for s in range(1, nsh):
            acc = acc + p_ref[s].astype(F32)
        o_ref[...] = acc

    return pl.pallas_call(
        body, name="grad_sum_chips", grid=(half // rb,),
        in_specs=[pl.BlockSpec((nsh, rb, cols), lambda i: (0, i, 0))],
        out_specs=pl.BlockSpec((rb, cols), lambda i: (i, 0)),
        out_shape=jax.ShapeDtypeStruct((half, cols), F32),
        compiler_params=_params(("parallel",)),
    )(parts)


def _pair_share(r):
    half, cols = r.shape

    def body(in_ref, out_ref, send_sem, recv_sem, local_sem):
        x, y, c = _place()
        mine = pltpu.make_async_copy(in_ref, out_ref.at[c], local_sem)
        mine.start()
        cp = pltpu.make_async_remote_copy(src_ref=in_ref, dst_ref=out_ref.at[c], send_sem=send_sem,
                                          recv_sem=recv_sem, device_id=(x, y, 1 - c), device_id_type=MESH)
        cp.start()
        there = out_ref.at[1 - c]
        pltpu.make_async_remote_copy(src_ref=there, dst_ref=there, send_sem=send_sem, recv_sem=recv_sem,
                                     device_id=(x, y, 1 - c), device_id_type=MESH).wait_recv()
        cp.wait_send()
        mine.wait()

    return pl.pallas_call(
        body, name="grad_pair_share",
        in_specs=[pl.BlockSpec(memory_space=pltpu.HBM)],
        out_specs=pl.BlockSpec(memory_space=pltpu.HBM),
        out_shape=jax.ShapeDtypeStruct((2, half, cols), r.dtype),
        scratch_shapes=[pltpu.SemaphoreType.DMA, pltpu.SemaphoreType.DMA, pltpu.SemaphoreType.DMA],
    )(r)


def _small_allreduce(v):
    rows, cols = v.shape
    ndev = 8

    def body(in_ref, out_ref, slots, send_sems, recv_sems):
        x, y, c = _place()
        me = 4 * x + 2 * y + c
        slots[me] = in_ref[...]
        sends = []
        for k in range(1, ndev):
            peer = (x ^ (k >> 2), y ^ ((k >> 1) & 1), c ^ (k & 1))
            cp = pltpu.make_async_remote_copy(src_ref=in_ref, dst_ref=slots.at[me], send_sem=send_sems.at[k - 1],
                                              recv_sem=recv_sems.at[k - 1], device_id=peer, device_id_type=MESH)
            cp.start()
            sends.append(cp)
        for k in range(1, ndev):
            there = slots.at[me ^ k]
            pltpu.make_async_remote_copy(src_ref=there, dst_ref=there, send_sem=send_sems.at[k - 1],
                                         recv_sem=recv_sems.at[k - 1], device_id=(x, y, c),
                                         device_id_type=MESH).wait_recv()
        for cp in sends:
            cp.wait_send()
        acc = slots[0]
        for s in range(1, ndev):
            acc = acc + slots[s]
        out_ref[...] = acc

    return pl.pallas_call(
        body, name="small_allreduce",
        in_specs=[pl.BlockSpec(memory_space=pltpu.VMEM)],
        out_specs=pl.BlockSpec(memory_space=pltpu.VMEM),
        out_shape=jax.ShapeDtypeStruct((rows, cols), F32),
        scratch_shapes=[pltpu.VMEM((ndev, rows, cols), F32), pltpu.SemaphoreType.DMA((ndev - 1,)),
                        pltpu.SemaphoreType.DMA((ndev - 1,))],
    )(v)


def _adamw(w, g, m, v, name):
    r, c = w.shape
    rb = r if r <= 128 else _pick_rows_8(r, 128)
    c1 = 1.0 - ADAM_B1 ** ADAM_STEP
    c2 = 1.0 - ADAM_B2 ** ADAM_STEP

    def body(w_ref, g_ref, m_ref, v_ref, d_ref, nm_ref, nv_ref):
        gg = g_ref[...]
        nm = ADAM_B1 * m_ref[...] + (1.0 - ADAM_B1) * gg
        nv = ADAM_B2 * v_ref[...] + (1.0 - ADAM_B2) * (gg * gg)
        d_ref[...] = -ADAM_LR * ((nm / c1) / (jnp.sqrt(nv / c2) + ADAM_EPS) + ADAM_WD * w_ref[...])
        nm_ref[...] = nm
        nv_ref[...] = nv

    blk = pl.BlockSpec((rb, c), lambda i: (i, 0))
    shp = jax.ShapeDtypeStruct((r, c), F32)
    return pl.pallas_call(
        body, name=name, grid=(r // rb,), in_specs=[blk] * 4, out_specs=[blk] * 3, out_shape=[shp] * 3,
        compiler_params=_params(("parallel",)),
    )(w, g, m, v)


def _pick_rows_8(n, target):
    best = n
    for b in range(8, min(n, target) + 1, 8):
        if n % b == 0:
            best = b
    return best


W_IN_COLS = 2308
W_UP_COLS = 1408
W_DOWN_ROWS = 704
DN_CONV_COLS = 768
FFN_CONV_COLS = 1408
PROJ_ROWS = 256
ROW_TILE = 16
SEG = [("w_in", W_IN_COLS), ("wp_dn", PROJ_ROWS), ("wp_sb", PROJ_ROWS), ("w_out", PROJ_ROWS),
       ("w_up", W_UP_COLS), ("w_down", W_DOWN_ROWS), ("dn_conv", ROW_TILE), ("ffn_conv", ROW_TILE)]


def _seg_offsets():
    offs, at = {}, 0
    for nm, n in SEG:
        offs[nm] = (at, n)
        at += -(-n // ROW_TILE) * ROW_TILE
    assert at <= PACK_ROWS and PACK_ROWS % (2 * ROW_TILE) == 0
    return offs, at


PACK_OFFS, PACK_USED = _seg_offsets()


def _tile_rows(a, axis):
    n = a.shape[axis]
    pad = [(0, 0)] * a.ndim
    pad[axis] = (0, -(-n // ROW_TILE) * ROW_TILE - n)
    return jnp.pad(a, pad)


def _flat_rows(a, nrows):
    flat = a.reshape(-1)
    return jnp.pad(flat, (0, nrows * D_MODEL - flat.shape[0])).reshape(nrows, D_MODEL)


def _pack_weight_shard(w_in, wp_dn, wp_sb, w_out, w_up, w_down, dn_conv, ffn_conv):
    parts = [w_in.astype(BF16).reshape(W_IN_COLS, D_MODEL), wp_dn.astype(BF16), wp_sb.astype(BF16),
             w_out.astype(BF16), w_up.astype(BF16).reshape(W_UP_COLS, D_MODEL), w_down.astype(BF16),
             _flat_rows(lax.bitcast_convert_type(dn_conv, BF16), ROW_TILE),
             _flat_rows(lax.bitcast_convert_type(ffn_conv, BF16), ROW_TILE),
             jnp.zeros((PACK_ROWS - PACK_USED, D_MODEL), BF16)]
    return jnp.concatenate([_tile_rows(p, 0) for p in parts], axis=0)


def _unpack_weights(g):
    def seg(nm):
        at, n = PACK_OFFS[nm]
        return g[:, at:at + n, :]

    def cols(nm, ncols):
        return seg(nm).reshape(N_CHIPS, D_MODEL, ncols).transpose(1, 0, 2).reshape(D_MODEL, N_CHIPS * ncols)

    def f32_rows(nm, k, ncols):
        raw = seg(nm).reshape(N_CHIPS, -1)[:, :2 * k * ncols].reshape(N_CHIPS, k * ncols, 2)
        vals = lax.bitcast_convert_type(raw, F32).reshape(N_CHIPS, k, ncols)
        return vals.transpose(1, 0, 2).reshape(k, N_CHIPS * ncols)

    w_in = cols("w_in", W_IN_COLS)
    w_up = cols("w_up", W_UP_COLS)
    ffn_conv = f32_rows("ffn_conv", FFN_CONV, FFN_CONV_COLS)
    q_end, a_end, g_end, s_end = 3 * D_MODEL, 3 * D_MODEL + 2 * N_HEADS, 4 * D_MODEL + 2 * N_HEADS, 7 * D_MODEL + 2 * N_HEADS
    return {
        "w_dnqkv": w_in[:, :q_end],
        "w_ab": jnp.pad(w_in[:, q_end:a_end], ((0, 0), (0, LANES - 2 * N_HEADS))),
        "w_dngate": w_in[:, a_end:g_end],
        "w_sbqkv": w_in[:, g_end:s_end],
        "w_gl": w_in[:, s_end:],
        "wp_dn": seg("wp_dn").reshape(D_MODEL, D_MODEL),
        "wp_sb": seg("wp_sb").reshape(D_MODEL, D_MODEL),
        "w_out": seg("w_out").reshape(D_MODEL, D_MODEL),
        "w_up_g": w_up[:, :D_FF], "w_up_u": w_up[:, D_FF:],
        "w_down": seg("w_down").reshape(D_FF, D_MODEL),
        "dn_conv": f32_rows("dn_conv", DN_CONV, DN_CONV_COLS),
        "ffn_conv_g": ffn_conv[:, :D_FF], "ffn_conv_u": ffn_conv[:, D_FF:],
    }


def _pack_grads(gr):
    w_in = jnp.concatenate([gr["w_dnqkv"], gr["w_ab"][:, :2 * N_HEADS], gr["w_dngate"], gr["w_sbqkv"], gr["w_gl"]],
                           axis=1)

    def cols(a, ncols):
        return a.reshape(a.shape[0], N_CHIPS, ncols).transpose(1, 0, 2)

    def rows(a, nrows):
        return a.reshape(N_CHIPS, nrows, a.shape[1])

    def flat(a, nrows):
        a = a.reshape(N_CHIPS, -1)
        return jnp.pad(a, ((0, 0), (0, nrows * D_MODEL - a.shape[1]))).reshape(N_CHIPS, nrows, D_MODEL)

    parts = [cols(w_in, W_IN_COLS).reshape(N_CHIPS, W_IN_COLS, D_MODEL),
             rows(gr["wp_dn"], PROJ_ROWS), rows(gr["wp_sb"], PROJ_ROWS), rows(gr["w_out"], PROJ_ROWS),
             cols(gr["w_up"], W_UP_COLS).reshape(N_CHIPS, W_UP_COLS, D_MODEL),
             rows(gr["w_down"], W_DOWN_ROWS),
             flat(cols(gr["dn_conv"], DN_CONV_COLS), ROW_TILE), flat(cols(gr["ffn_conv"], FFN_CONV_COLS), ROW_TILE),
             jnp.zeros((N_CHIPS, PACK_ROWS - PACK_USED, D_MODEL), F32)]
    return jnp.concatenate([_tile_rows(p, 1) for p in parts], axis=1).astype(BF16)


def _unpack_grad_shard(r):
    def seg(nm):
        at, n = PACK_OFFS[nm]
        return r[at:at + n, :]

    return {
        "w_in": seg("w_in").reshape(D_MODEL, W_IN_COLS),
        "wp_dn": seg("wp_dn"), "wp_sb": seg("wp_sb"), "w_out": seg("w_out"),
        "w_up": seg("w_up").reshape(D_MODEL, W_UP_COLS),
        "w_down": seg("w_down"),
        "dn_conv": seg("dn_conv").reshape(-1)[:DN_CONV * DN_CONV_COLS].reshape(DN_CONV, DN_CONV_COLS),
        "ffn_conv": seg("ffn_conv").reshape(-1)[:FFN_CONV * FFN_CONV_COLS].reshape(FFN_CONV, FFN_CONV_COLS),
    }


def _lane_row(v):
    return jnp.pad(v.reshape(1, -1), ((0, 0), (0, LANES - v.size)))


def kernel(x, norm1_w, w_in, dn_conv_w, dn_A_log, dn_dt_bias, dn_norm_w, w_proj_dn, w_proj_sb, w_out, norm2_w, ffn_w_up, ffn_conv_w, ffn_w_down, norm_f_w, loss_target, m_norm1_w, m_w_in, m_dn_conv_w, m_dn_A_log, m_dn_dt_bias, m_dn_norm_w, m_w_proj_dn, m_w_proj_sb, m_w_out, m_norm2_w, m_ffn_w_up, m_ffn_conv_w, m_ffn_w_down, m_norm_f_w, v_norm1_w, v_w_in, v_dn_conv_w, v_dn_A_log, v_dn_dt_bias, v_dn_norm_w, v_w_proj_dn, v_w_proj_sb, v_w_out, v_norm2_w, v_ffn_w_up, v_ffn_conv_w, v_ffn_w_down, v_norm_f_w):
    shard = _pack_weight_shard(w_in[0], w_proj_dn[0], w_proj_sb[0], w_out[0], ffn_w_up[0], ffn_w_down[0],
                               dn_conv_w[0], ffn_conv_w[0])
    wts = _unpack_weights(_gather_shards(shard))
    wts.update(norm1=norm1_w, norm2=norm2_w, normf=norm_f_w.reshape(1, D_MODEL), dn_norm=dn_norm_w,
               alog=_lane_row(dn_A_log), dtb=_lane_row(dn_dt_bias))

    loss_part, grad_x, gr = _local_step(x[0], loss_target[0], wts)

    c_idx = lax.axis_index("c").astype(jnp.int32).reshape(1)
    packed = _pack_grads(gr)
    partial_sum = _pair_add(packed, _pair_exchange_halves(packed), c_idx)
    reduced_half = _sum_slots(_chip_exchange(partial_sum))
    gsh = _unpack_grad_shard(_pair_share(reduced_half).reshape(PACK_ROWS, D_MODEL))

    tail = jnp.concatenate([gr["dn_norm"], gr["alog"][:, :N_HEADS], gr["dtb"][:, :N_HEADS], loss_part[:, :1]], axis=1)
    small = jnp.concatenate([gr["norm1"], gr["norm2"], gr["normf"],
                             jnp.pad(tail, ((0, 0), (0, D_MODEL - tail.shape[1]))),
                             jnp.zeros((SMALL_ROWS - 4, D_MODEL), F32)], axis=0)
    small = _small_allreduce(small)
    at = HEAD_DIM
    g_small = {"norm1_w": small[0:1], "norm2_w": small[1:2], "norm_f_w": small[2],
               "dn_norm_w": small[3:4, :at], "dn_A_log": small[3:4, at:at + N_HEADS],
               "dn_dt_bias": small[3:4, at + N_HEADS:at + 2 * N_HEADS]}
    loss = small[3, at + 2 * N_HEADS]

    big = {"w_in": (w_in, m_w_in, v_w_in, gsh["w_in"]), "dn_conv_w": (dn_conv_w, m_dn_conv_w, v_dn_conv_w, gsh["dn_conv"]),
           "w_proj_dn": (w_proj_dn, m_w_proj_dn, v_w_proj_dn, gsh["wp_dn"]),
           "w_proj_sb": (w_proj_sb, m_w_proj_sb, v_w_proj_sb, gsh["wp_sb"]),
           "w_out": (w_out, m_w_out, v_w_out, gsh["w_out"]),
           "ffn_w_up": (ffn_w_up, m_ffn_w_up, v_ffn_w_up, gsh["w_up"]),
           "ffn_conv_w": (ffn_conv_w, m_ffn_conv_w, v_ffn_conv_w, gsh["ffn_conv"]),
           "ffn_w_down": (ffn_w_down, m_ffn_w_down, v_ffn_w_down, gsh["w_down"])}
    res = {}
    for nm, (w, m, v, g) in big.items():
        d, nm_, nv_ = _adamw(w[0], g, m[0], v[0], "adamw_" + nm)
        res[nm] = (g[None], d[None], nm_[None], nv_[None])

    names = ["norm1_w", "norm2_w", "norm_f_w", "dn_norm_w", "dn_A_log", "dn_dt_bias"]
    given = {"norm1_w": (norm1_w, m_norm1_w, v_norm1_w), "norm2_w": (norm2_w, m_norm2_w, v_norm2_w),
             "norm_f_w": (norm_f_w, m_norm_f_w, v_norm_f_w), "dn_norm_w": (dn_norm_w, m_dn_norm_w, v_dn_norm_w),
             "dn_A_log": (dn_A_log, m_dn_A_log, v_dn_A_log), "dn_dt_bias": (dn_dt_bias, m_dn_dt_bias, v_dn_dt_bias)}

    def stack(k, fill):
        rows = [jnp.pad(given[nm][k].reshape(1, -1), ((0, 0), (0, D_MODEL - given[nm][k].size)),
                        constant_values=fill) for nm in names]
        return jnp.concatenate(rows + [jnp.full((SMALL_ROWS - len(names), D_MODEL), fill, F32)], axis=0)

    g_rows = jnp.concatenate(
        [jnp.pad(g_small[nm].reshape(1, -1), ((0, 0), (0, D_MODEL - g_small[nm].size))) for nm in names]
        + [jnp.zeros((SMALL_ROWS - len(names), D_MODEL), F32)], axis=0)
    d_s, m_s, v_s = _adamw(stack(0, 0.0), g_rows, stack(1, 0.0), stack(2, 1.0), "adamw_small")
    for r, nm in enumerate(names):
        shape = given[nm][0].shape
        n = given[nm][0].size
        res[nm] = (g_small[nm].reshape(shape), d_s[r, :n].reshape(shape), m_s[r, :n].reshape(shape),
                   v_s[r, :n].reshape(shape))

    order = ["norm1_w", "w_in", "dn_conv_w", "dn_A_log", "dn_dt_bias", "dn_norm_w", "w_proj_dn", "w_proj_sb",
             "w_out", "norm2_w", "ffn_w_up", "ffn_conv_w", "ffn_w_down", "norm_f_w"]
    outs = [loss, grad_x[None]]
    for k in range(4):
        outs += [res[nm][k] for nm in order]
    return tuple(outs)
```

```python
import functools

import jax
import jax.numpy as jnp
from jax import lax
from jax.experimental import pallas as pl
from jax.experimental.pallas import tpu as pltpu

F32 = jnp.float32
BF16 = jnp.bfloat16
HIGHEST = lax.Precision.HIGHEST
MESH = pl.DeviceIdType.MESH

EPS = 1e-6
D_MODEL = 1024
N_HEADS = 8
HEAD_DIM = 128
DN_CONV = 4
DN_CHUNK = 64
D_FF = 2816
FFN_CONV = 3
ADAM_LR, ADAM_B1, ADAM_B2, ADAM_EPS, ADAM_WD, ADAM_STEP = 0.001, 0.9, 0.999, 1e-08, 0.01, 10

N_CHIPS = 4
LANES = 128
HALO = 8
VMEM_LIMIT = 48 * 1024 * 1024
PACK_ROWS = 5248
SMALL_ROWS = 8


def _params(sem=None):
    return pltpu.CompilerParams(dimension_semantics=sem, vmem_limit_bytes=VMEM_LIMIT)


def _pick(n, target):
    best = None
    for b in range(LANES, min(n, target) + 1, LANES):
        if n % b == 0:
            best = b
    return best or n


def _rows(t, target=256):
    return min(t, target)


def _dot(a, b, precision=None):
    return lax.dot_general(a, b, (((1,), (0,)), ((), ())), precision=precision, preferred_element_type=F32)


def _dot_nt(a, b, precision=None):
    return lax.dot_general(a, b, (((1,), (1,)), ((), ())), precision=precision, preferred_element_type=F32)


def _dot_tn(a, b, precision=None):
    return lax.dot_general(a, b, (((0,), (0,)), ((), ())), precision=precision, preferred_element_type=F32)


def _rms(x, w):
    return x * lax.rsqrt(jnp.mean(x * x, axis=-1, keepdims=True) + EPS) * w


def _silu(x):
    return x * jax.nn.sigmoid(x)


def _softplus(x):
    return jnp.maximum(x, 0.0) + jnp.log(1.0 + jnp.exp(-jnp.abs(x)))


def _mm(a, b, *, ta=False, tb=False, add=None, out_dtype=F32, name, bm=512, bn=512, bk=1024):
    m = a.shape[1] if ta else a.shape[0]
    k = a.shape[0] if ta else a.shape[1]
    n = b.shape[0] if tb else b.shape[1]
    bm, bn, bk = _pick(m, bm), _pick(n, bn), _pick(k, bk)
    nk = k // bk
    dims = (((0 if ta else 1,), (1 if tb else 0,)), ((), ()))

    def body(*refs):
        if add is None:
            a_ref, b_ref, o_ref, acc = refs
        else:
            a_ref, b_ref, c_ref, o_ref, acc = refs
        kk = pl.program_id(2)

        @pl.when(kk == 0)
        def _():
            acc[...] = jnp.zeros_like(acc)

        acc[...] += lax.dot_general(a_ref[...].astype(BF16), b_ref[...].astype(BF16), dims,
                                    preferred_element_type=F32)

        @pl.when(kk == nk - 1)
        def _():
            r = acc[...]
            if add is not None:
                r = r + c_ref[...].astype(F32)
            o_ref[...] = r.astype(out_dtype)

    a_spec = (pl.BlockSpec((bk, bm), lambda i, j, kk: (kk, i)) if ta
              else pl.BlockSpec((bm, bk), lambda i, j, kk: (i, kk)))
    b_spec = (pl.BlockSpec((bn, bk), lambda i, j, kk: (j, kk)) if tb
              else pl.BlockSpec((bk, bn), lambda i, j, kk: (kk, j)))
    o_spec = pl.BlockSpec((bm, bn), lambda i, j, kk: (i, j))
    in_specs = [a_spec, b_spec] + ([o_spec] if add is not None else [])
    args = (a, b) + ((add,) if add is not None else ())
    return pl.pallas_call(
        body, name=name, grid=(m // bm, n // bn, nk),
        in_specs=in_specs, out_specs=o_spec,
        out_shape=jax.ShapeDtypeStruct((m, n), out_dtype),
        scratch_shapes=[pltpu.VMEM((bm, bn), F32)],
        compiler_params=_params(("parallel", "parallel", "arbitrary")),
    )(*args)


def _norm1_fwd(x, w, w_ab):
    t = x.shape[0]
    tb = _rows(t)

    def body(x_ref, w_ref, wab_ref, n_ref, hab_ref):
        n = _rms(x_ref[...], w_ref[...]).astype(BF16)
        n_ref[...] = n
        hab_ref[...] = _dot(n, wab_ref[...])

    return pl.pallas_call(
        body, name="norm1_fwd", grid=(t // tb,),
        in_specs=[pl.BlockSpec((tb, D_MODEL), lambda i: (i, 0)),
                  pl.BlockSpec((1, D_MODEL), lambda i: (0, 0)),
                  pl.BlockSpec((D_MODEL, LANES), lambda i: (0, 0))],
        out_specs=[pl.BlockSpec((tb, D_MODEL), lambda i: (i, 0)),
                   pl.BlockSpec((tb, LANES), lambda i: (i, 0))],
        out_shape=[jax.ShapeDtypeStruct((t, D_MODEL), BF16), jax.ShapeDtypeStruct((t, LANES), F32)],
        compiler_params=_params(("arbitrary",)),
    )(x, w, w_ab)


def _norm1_bwd(x, w, dn, dres, dab, w_ab):
    t = x.shape[0]
    tb = _rows(t)

    def body(x_ref, w_ref, dn_ref, dres_ref, dab_ref, wab_ref, dx_ref, dw_ref):
        i = pl.program_id(0)
        g = dn_ref[...] + _dot_nt(dab_ref[...].astype(BF16), wab_ref[...])
        _, vjp = jax.vjp(_rms, x_ref[...], w_ref[...])
        dx, dw = vjp(g)
        dx_ref[...] = dres_ref[...] + dx

        @pl.when(i == 0)
        def _():
            dw_ref[...] = jnp.zeros_like(dw_ref)

        dw_ref[...] += dw

    row = pl.BlockSpec((tb, D_MODEL), lambda i: (i, 0))
    vec = pl.BlockSpec((1, D_MODEL), lambda i: (0, 0))
    return pl.pallas_call(
        body, name="norm1_bwd", grid=(t // tb,),
        in_specs=[row, vec, row, row, pl.BlockSpec((tb, LANES), lambda i: (i, 0)),
                  pl.BlockSpec((D_MODEL, LANES), lambda i: (0, 0))],
        out_specs=[row, vec],
        out_shape=[jax.ShapeDtypeStruct((t, D_MODEL), F32), jax.ShapeDtypeStruct((1, D_MODEL), F32)],
        compiler_params=_params(("arbitrary",)),
    )(x, w, dn, dres, dab, w_ab)


def _conv_fwd(x, w, name):
    t, c = x.shape
    kk = w.shape[0]
    tb, cb = _rows(t), _pick(c, 512)
    per = tb // HALO

    def body(x_ref, halo_ref, w_ref, y_ref, buf):
        i = pl.program_id(0)
        buf[pl.ds(HALO, tb), :] = x_ref[...]
        buf[pl.ds(0, HALO), :] = jnp.where(i == 0, 0.0, halo_ref[...])
        y = w_ref[0:1, :] * buf[pl.ds(HALO - (kk - 1), tb), :]
        for s in range(1, kk):
            y = y + w_ref[s:s + 1, :] * buf[pl.ds(HALO - (kk - 1) + s, tb), :]
        y_ref[...] = y

    return pl.pallas_call(
        body, name=name, grid=(t // tb, c // cb),
        in_specs=[pl.BlockSpec((tb, cb), lambda i, j: (i, j)),
                  pl.BlockSpec((HALO, cb), lambda i, j: (jnp.maximum(i * per - 1, 0), j)),
                  pl.BlockSpec((kk, cb), lambda i, j: (0, j))],
        out_specs=pl.BlockSpec((tb, cb), lambda i, j: (i, j)),
        out_shape=jax.ShapeDtypeStruct((t, c), F32),
        scratch_shapes=[pltpu.VMEM((tb + HALO, cb), F32)],
        compiler_params=_params(("parallel", "parallel")),
    )(x, x, w)


def _conv_bwd(dy, x, w, name, dx_dtype):
    t, c = x.shape
    kk = w.shape[0]
    tb, cb = _rows(t), _pick(c, 512)
    per = tb // HALO
    nblk = t // tb

    def body(dy_ref, after_ref, x_ref, before_ref, w_ref, dx_ref, dw_ref, dbuf, xbuf):
        i = pl.program_id(1)
        dy = dy_ref[...]
        dbuf[pl.ds(0, tb), :] = dy
        dbuf[pl.ds(tb, HALO), :] = jnp.where(i == nblk - 1, 0.0, after_ref[...])
        xbuf[pl.ds(HALO, tb), :] = x_ref[...]
        xbuf[pl.ds(0, HALO), :] = jnp.where(i == 0, 0.0, before_ref[...])
        dx = w_ref[0:1, :] * dbuf[pl.ds(kk - 1, tb), :]
        for s in range(1, kk):
            dx = dx + w_ref[s:s + 1, :] * dbuf[pl.ds(kk - 1 - s, tb), :]
        dx_ref[...] = dx.astype(dx_dtype)

        @pl.when(i == 0)
        def _():
            dw_ref[...] = jnp.zeros_like(dw_ref)

        for s in range(kk):
            part = jnp.sum(dy * xbuf[pl.ds(HALO - (kk - 1) + s, tb), :], axis=0, keepdims=True)
            dw_ref[s:s + 1, :] += part

    blk = pl.BlockSpec((tb, cb), lambda j, i: (i, j))
    return pl.pallas_call(
        body, name=name, grid=(c // cb, nblk),
        in_specs=[blk,
                  pl.BlockSpec((HALO, cb), lambda j, i: (jnp.minimum((i + 1) * per, t // HALO - 1), j)),
                  blk,
                  pl.BlockSpec((HALO, cb), lambda j, i: (jnp.maximum(i * per - 1, 0), j)),
                  pl.BlockSpec((kk, cb), lambda j, i: (0, j))],
        out_specs=[blk, pl.BlockSpec((HALO, cb), lambda j, i: (0, j))],
        out_shape=[jax.ShapeDtypeStruct((t, c), dx_dtype), jax.ShapeDtypeStruct((HALO, c), F32)],
        scratch_shapes=[pltpu.VMEM((tb + HALO, cb), F32), pltpu.VMEM((tb + HALO, cb), F32)],
        compiler_params=_params(("parallel", "arbitrary")),
    )(dy, dy, x, x, w)


def _dn_prep_fn(c, hab, alog, dtb):
    s = _silu(c)
    heads = []
    for h in range(2 * N_HEADS):
        sh = s[:, h * HEAD_DIM:(h + 1) * HEAD_DIM]
        heads.append(sh * lax.rsqrt(jnp.sum(sh * sh, axis=-1, keepdims=True) + EPS))
    qn = jnp.concatenate(heads[:N_HEADS], axis=1)
    kn = jnp.concatenate(heads[N_HEADS:], axis=1)
    v = s[:, 2 * D_MODEL:]
    lane = lax.broadcasted_iota(jnp.int32, hab.shape, 1)
    g = -jnp.exp(alog) * _softplus(hab + dtb)
    beta = jax.nn.sigmoid(hab)
    gb = jnp.where(lane < N_HEADS, g, jnp.where(lane < 2 * N_HEADS, beta, 0.0))
    return qn, kn, v, gb


def _to_heads(ref, val):
    for h in range(N_HEADS):
        ref[h] = val[:, h * HEAD_DIM:(h + 1) * HEAD_DIM]


def _from_heads(ref):
    return jnp.concatenate([ref[h] for h in range(N_HEADS)], axis=1)


def _dn_prep_fwd(c, hab, alog, dtb):
    t = c.shape[0]
    tb = _rows(t)

    def body(c_ref, hab_ref, alog_ref, dtb_ref, q_ref, k_ref, v_ref, gb_ref):
        qn, kn, v, gb = _dn_prep_fn(c_ref[...], hab_ref[...], alog_ref[...], dtb_ref[...])
        _to_heads(q_ref, qn)
        _to_heads(k_ref, kn)
        _to_heads(v_ref, v)
        gb_ref[...] = gb

    hm = pl.BlockSpec((N_HEADS, tb, HEAD_DIM), lambda i: (0, i, 0))
    nar = pl.BlockSpec((tb, LANES), lambda i: (i, 0))
    vec = pl.BlockSpec((1, LANES), lambda i: (0, 0))
    return pl.pallas_call(
        body, name="dn_prep_fwd", grid=(t // tb,),
        in_specs=[pl.BlockSpec((tb, 3 * D_MODEL), lambda i: (i, 0)), nar, vec, vec],
        out_specs=[hm, hm, hm, nar],
        out_shape=[jax.ShapeDtypeStruct((N_HEADS, t, HEAD_DIM), F32)] * 3 + [jax.ShapeDtypeStruct((t, LANES), F32)],
        compiler_params=_params(("parallel",)),
    )(c, hab, alog, dtb)


def _dn_prep_bwd(c, hab, alog, dtb, dq, dk, dv, dgb):
    t = c.shape[0]
    tb = _rows(t)

    def body(c_ref, hab_ref, alog_ref, dtb_ref, dq_ref, dk_ref, dv_ref, dgb_ref,
             dc_ref, dhab_ref, dalog_ref, ddtb_ref):
        i = pl.program_id(0)
        _, vjp = jax.vjp(_dn_prep_fn, c_ref[...], hab_ref[...], alog_ref[...], dtb_ref[...])
        dc, dhab, dalog, ddtb = vjp((_from_heads(dq_ref), _from_heads(dk_ref), _from_heads(dv_ref), dgb_ref[...]))
        dc_ref[...] = dc
        dhab_ref[...] = dhab

        @pl.when(i == 0)
        def _():
            dalog_ref[...] = jnp.zeros_like(dalog_ref)
            ddtb_ref[...] = jnp.zeros_like(ddtb_ref)

        dalog_ref[...] += dalog
        ddtb_ref[...] += ddtb

    hm = pl.BlockSpec((N_HEADS, tb, HEAD_DIM), lambda i: (0, i, 0))
    wide = pl.BlockSpec((tb, 3 * D_MODEL), lambda i: (i, 0))
    nar = pl.BlockSpec((tb, LANES), lambda i: (i, 0))
    vec = pl.BlockSpec((1, LANES), lambda i: (0, 0))
    return pl.pallas_call(
        body, name="dn_prep_bwd", grid=(t // tb,),
        in_specs=[wide, nar, vec, vec, hm, hm, hm, nar],
        out_specs=[wide, nar, vec, vec],
        out_shape=[jax.ShapeDtypeStruct((t, 3 * D_MODEL), F32), jax.ShapeDtypeStruct((t, LANES), F32),
                   jax.ShapeDtypeStruct((1, LANES), F32), jax.ShapeDtypeStruct((1, LANES), F32)],
        compiler_params=_params(("arbitrary",)),
    )(c, hab, alog, dtb, dq, dk, dv, dgb)


DN_PREC = lax.Precision.HIGH
DN_GROUP = 8


def _bdot(a, b):
    return lax.dot_general(a, b, (((2,), (1,)), ((0,), (0,))), precision=DN_PREC, preferred_element_type=F32)


def _bdot_nt(a, b):
    return lax.dot_general(a, b, (((2,), (2,)), ((0,), (0,))), precision=DN_PREC, preferred_element_type=F32)


def _bdot_tn(a, b):
    return lax.dot_general(a, b, (((1,), (1,)), ((0,), (0,))), precision=DN_PREC, preferred_element_type=F32)


def _unit_lower_inverse(lmat):
    c = lmat.shape[-1]
    ri = lax.broadcasted_iota(jnp.int32, (c, c), 0)
    ci = lax.broadcasted_iota(jnp.int32, (c, c), 1)
    p = -lmat
    tinv = jnp.where(ri == ci, 1.0, 0.0) + p
    for _ in range(max(c.bit_length() - 2, 0)):
        p = _bdot(p, p)
        tinv = tinv + _bdot(tinv, p)
    return tinv


@jax.custom_vjp
def _solve_with(lmat, rhs, tinv):
    return _bdot(tinv, rhs)


def _solve_with_fwd(lmat, rhs, tinv):
    sol = _bdot(tinv, rhs)
    return sol, (sol, tinv)


def _solve_with_bwd(res, dsol):
    sol, tinv = res
    drhs = _bdot_tn(tinv, dsol)
    return -_bdot_nt(drhs, sol), drhs, jnp.zeros_like(tinv)


_solve_with.defvjp(_solve_with_fwd, _solve_with_bwd)


def _dn_local(q, k, v, gcol, grow, bcol, tinv):
    g, c, _ = q.shape
    ri = lax.broadcasted_iota(jnp.int32, (c, c), 0)
    ci = lax.broadcasted_iota(jnp.int32, (c, c), 1)
    lower = ri >= ci
    gc_col = jnp.sum(jnp.where(lower, jnp.broadcast_to(grow, (g, c, c)), 0.0), axis=2, keepdims=True)
    gc_row = jnp.sum(jnp.where(ri <= ci, jnp.broadcast_to(gcol, (g, c, c)), 0.0), axis=1, keepdims=True)
    qs = q * (HEAD_DIM ** -0.5)
    kb = k * bcol
    vb = v * bcol
    decay = jnp.where(lower, jnp.exp(jnp.where(lower, gc_col - gc_row, 0.0)), 0.0)
    lmat = jnp.where(ri > ci, _bdot_nt(kb, k) * decay, 0.0)
    eg = jnp.exp(gc_col)
    rhs = jnp.concatenate([vb, kb * eg], axis=2)
    if tinv is None:
        tinv = _unit_lower_inverse(lmat)
    sol = _solve_with(lmat, rhs, tinv)
    a_qk = jnp.where(lower, _bdot_nt(qs, k) * decay, 0.0)
    g_last = jnp.sum(grow, axis=2, keepdims=True)
    kdec = k * jnp.exp(g_last - gc_col)
    egl = jnp.broadcast_to(jnp.exp(g_last), (g, 1, HEAD_DIM))
    return sol[:, :, :HEAD_DIM], sol[:, :, HEAD_DIM:], a_qk, qs * eg, kdec, egl, tinv


def _dn_seq(u, w, a_qk, qe, kdec, egl, s_in):
    v_new = u - _bdot(w, s_in)
    o = _bdot(qe, s_in) + _bdot(a_qk, v_new)
    return o, s_in * egl + _bdot_tn(kdec, v_new)


def _dn_local_specs(t):
    grp = min(DN_GROUP, t // DN_CHUNK)
    rows = grp * DN_CHUNK
    blk = pl.BlockSpec((1, rows, HEAD_DIM), lambda h, i: (h, i, 0))
    col = pl.BlockSpec((1, grp, DN_CHUNK, 1), lambda h, i: (h, i, 0, 0))
    row = pl.BlockSpec((1, grp, 1, DN_CHUNK), lambda h, i: (h, i, 0, 0))
    sq = pl.BlockSpec((1, grp, DN_CHUNK, DN_CHUNK), lambda h, i: (h, i, 0, 0))
    lane = pl.BlockSpec((1, grp, 1, HEAD_DIM), lambda h, i: (h, i, 0, 0))
    return grp, blk, col, row, sq, lane


def _dn_shapes(t):
    nchunk = t // DN_CHUNK
    big = jax.ShapeDtypeStruct((N_HEADS, t, HEAD_DIM), F32)
    col = jax.ShapeDtypeStruct((N_HEADS, nchunk, DN_CHUNK, 1), F32)
    row = jax.ShapeDtypeStruct((N_HEADS, nchunk, 1, DN_CHUNK), F32)
    sq = jax.ShapeDtypeStruct((N_HEADS, nchunk, DN_CHUNK, DN_CHUNK), F32)
    lane = jax.ShapeDtypeStruct((N_HEADS, nchunk, 1, HEAD_DIM), F32)
    return big, col, row, sq, lane


def _dn_local_fwd(q, k, v, gcol, grow, bcol):
    t = q.shape[1]
    grp, blk, col, row, sq, lane = _dn_local_specs(t)
    big, _, _, sqs, lanes = _dn_shapes(t)

    def body(q_ref, k_ref, v_ref, gc_ref, gr_ref, bc_ref, u_ref, w_ref, a_ref, qe_ref, kd_ref, egl_ref, t_ref):
        split = lambda r: r[0].reshape(grp, DN_CHUNK, HEAD_DIM)
        u, w, a_qk, qe, kdec, egl, tinv = _dn_local(split(q_ref), split(k_ref), split(v_ref), gc_ref[0],
                                                     gr_ref[0], bc_ref[0], None)
        for ref, val in ((u_ref, u), (w_ref, w), (qe_ref, qe), (kd_ref, kdec)):
            ref[0] = val.reshape(grp * DN_CHUNK, HEAD_DIM)
        a_ref[0] = a_qk
        egl_ref[0] = egl
        t_ref[0] = tinv

    return pl.pallas_call(
        body, name="dn_local_fwd", grid=(N_HEADS, t // (grp * DN_CHUNK)),
        in_specs=[blk, blk, blk, col, row, col],
        out_specs=[blk, blk, sq, blk, blk, lane, sq],
        out_shape=[big, big, sqs, big, big, lanes, sqs],
        compiler_params=_params(("parallel", "parallel")),
    )(q, k, v, gcol, grow, bcol)


def _dn_local_bwd(q, k, v, gcol, grow, bcol, tinv, du, dw, da, dqe, dkd, degl):
    t = q.shape[1]
    grp, blk, col, row, sq, lane = _dn_local_specs(t)
    big, cols, rows_, _, _ = _dn_shapes(t)

    def body(q_ref, k_ref, v_ref, gc_ref, gr_ref, bc_ref, t_ref, du_ref, dw_ref, da_ref, dqe_ref, dkd_ref,
             degl_ref, dq_ref, dk_ref, dv_ref, dgc_ref, dgr_ref, dbc_ref):
        split = lambda r: r[0].reshape(grp, DN_CHUNK, HEAD_DIM)
        tinv_v = t_ref[0]
        fn = lambda q_, k_, v_, gc_, gr_, bc_: _dn_local(q_, k_, v_, gc_, gr_, bc_, tinv_v)[:6]
        _, vjp = jax.vjp(fn, split(q_ref), split(k_ref), split(v_ref), gc_ref[0], gr_ref[0], bc_ref[0])
        dq, dk, dv, dgc, dgr, dbc = vjp((split(du_ref), split(dw_ref), da_ref[0], split(dqe_ref), split(dkd_ref),
                                         degl_ref[0]))
        for ref, val in ((dq_ref, dq), (dk_ref, dk), (dv_ref, dv)):
            ref[0] = val.reshape(grp * DN_CHUNK, HEAD_DIM)
        dgc_ref[0] = dgc
        dgr_ref[0] = dgr
        dbc_ref[0] = dbc

    return pl.pallas_call(
        body, name="dn_local_bwd", grid=(N_HEADS, t // (grp * DN_CHUNK)),
        in_specs=[blk, blk, blk, col, row, col, sq, blk, blk, sq, blk, blk, lane],
        out_specs=[blk, blk, blk, col, row, col],
        out_shape=[big, big, big, cols, rows_, cols],
        compiler_params=_params(("parallel", "parallel")),
    )(q, k, v, gcol, grow, bcol, tinv, du, dw, da, dqe, dkd, degl)


def _dn_seq_specs(nchunk, rev):
    def idx(n):
        return nchunk - 1 - n if rev else n

    blk = pl.BlockSpec((N_HEADS, DN_CHUNK, HEAD_DIM), lambda n: (0, idx(n), 0))
    sq = pl.BlockSpec((N_HEADS, 1, DN_CHUNK, DN_CHUNK), lambda n: (0, idx(n), 0, 0))
    lane = pl.BlockSpec((N_HEADS, 1, 1, HEAD_DIM), lambda n: (0, idx(n), 0, 0))
    st = pl.BlockSpec((N_HEADS, 1, HEAD_DIM, HEAD_DIM), lambda n: (0, idx(n), 0, 0))
    return blk, sq, lane, st


def _dn_seq_fwd(u, w, a_qk, qe, kdec, egl):
    t = u.shape[1]
    nchunk = t // DN_CHUNK
    blk, sq, lane, st = _dn_seq_specs(nchunk, False)

    def body(u_ref, w_ref, a_ref, qe_ref, kd_ref, egl_ref, o_ref, s_ref, state):
        @pl.when(pl.program_id(0) == 0)
        def _():
            state[...] = jnp.zeros_like(state)

        s_in = state[...]
        s_ref[:, 0] = s_in
        o, s_out = _dn_seq(u_ref[...], w_ref[...], a_ref[:, 0], qe_ref[...], kd_ref[...], egl_ref[:, 0], s_in)
        o_ref[...] = o
        state[...] = s_out

    return pl.pallas_call(
        body, name="dn_seq_fwd", grid=(nchunk,),
        in_specs=[blk, blk, sq, blk, blk, lane],
        out_specs=[blk, st],
        out_shape=[jax.ShapeDtypeStruct((N_HEADS, t, HEAD_DIM), F32),
                   jax.ShapeDtypeStruct((N_HEADS, nchunk, HEAD_DIM, HEAD_DIM), F32)],
        scratch_shapes=[pltpu.VMEM((N_HEADS, HEAD_DIM, HEAD_DIM), F32)],
        compiler_params=_params(("arbitrary",)),
    )(u, w, a_qk, qe, kdec, egl)


def _dn_seq_bwd(u, w, a_qk, qe, kdec, egl, states, do):
    t = u.shape[1]
    nchunk = t // DN_CHUNK
    blk, sq, lane, st = _dn_seq_specs(nchunk, True)
    big, _, _, sqs, lanes = _dn_shapes(t)

    def body(u_ref, w_ref, a_ref, qe_ref, kd_ref, egl_ref, s_ref, do_ref,
             du_ref, dw_ref, da_ref, dqe_ref, dkd_ref, degl_ref, dstate):
        @pl.when(pl.program_id(0) == 0)
        def _():
            dstate[...] = jnp.zeros_like(dstate)

        _, vjp = jax.vjp(_dn_seq, u_ref[...], w_ref[...], a_ref[:, 0], qe_ref[...], kd_ref[...], egl_ref[:, 0],
                         s_ref[:, 0])
        du, dw, da, dqe, dkd, degl, ds = vjp((do_ref[...], dstate[...]))
        du_ref[...] = du
        dw_ref[...] = dw
        da_ref[:, 0] = da
        dqe_ref[...] = dqe
        dkd_ref[...] = dkd
        degl_ref[:, 0] = degl
        dstate[...] = ds

    return pl.pallas_call(
        body, name="dn_seq_bwd", grid=(nchunk,),
        in_specs=[blk, blk, sq, blk, blk, lane, st, blk],
        out_specs=[blk, blk, sq, blk, blk, lane],
        out_shape=[big, big, sqs, big, big, lanes],
        scratch_shapes=[pltpu.VMEM((N_HEADS, HEAD_DIM, HEAD_DIM), F32)],
        compiler_params=_params(("arbitrary",)),
    )(u, w, a_qk, qe, kdec, egl, states, do)


def _dn_post_fn(o, gate, w):
    outs = []
    for h in range(N_HEADS):
        sl = slice(h * HEAD_DIM, (h + 1) * HEAD_DIM)
        outs.append(_rms(o[:, sl], w) * _silu(gate[:, sl]))
    return jnp.concatenate(outs, axis=1)


def _dn_post_fwd(o, gate, w):
    t = gate.shape[0]
    tb = _rows(t)

    def body(o_ref, g_ref, w_ref, y_ref):
        y_ref[...] = _dn_post_fn(_from_heads(o_ref), g_ref[...], w_ref[...]).astype(BF16)

    row = pl.BlockSpec((tb, D_MODEL), lambda i: (i, 0))
    hm = pl.BlockSpec((N_HEADS, tb, HEAD_DIM), lambda i: (0, i, 0))
    return pl.pallas_call(
        body, name="dn_post_fwd", grid=(t // tb,),
        in_specs=[hm, row, pl.BlockSpec((1, HEAD_DIM), lambda i: (0, 0))],
        out_specs=row, out_shape=jax.ShapeDtypeStruct((t, D_MODEL), BF16),
        compiler_params=_params(("parallel",)),
    )(o, gate, w)


def _dn_post_bwd(o, gate, w, dy):
    t = gate.shape[0]
    tb = _rows(t)

    def body(o_ref, g_ref, w_ref, dy_ref, do_ref, dg_ref, dw_ref):
        i = pl.program_id(0)
        _, vjp = jax.vjp(_dn_post_fn, _from_heads(o_ref), g_ref[...], w_ref[...])
        do, dg, dw = vjp(dy_ref[...])
        _to_heads(do_ref, do)
        dg_ref[...] = dg.astype(BF16)

        @pl.when(i == 0)
        def _():
            dw_ref[...] = jnp.zeros_like(dw_ref)

        dw_ref[...] += dw

    row = pl.BlockSpec((tb, D_MODEL), lambda i: (i, 0))
    hm = pl.BlockSpec((N_HEADS, tb, HEAD_DIM), lambda i: (0, i, 0))
    vec = pl.BlockSpec((1, HEAD_DIM), lambda i: (0, 0))
    return pl.pallas_call(
        body, name="dn_post_bwd", grid=(t // tb,),
        in_specs=[hm, row, vec, row],
        out_specs=[hm, row, vec],
        out_shape=[jax.ShapeDtypeStruct((N_HEADS, t, HEAD_DIM), F32), jax.ShapeDtypeStruct((t, D_MODEL), BF16),
                   jax.ShapeDtypeStruct((1, HEAD_DIM), F32)],
        compiler_params=_params(("arbitrary",)),
    )(o, gate, w, dy)


def _split_bf16(x):
    hi = x.astype(BF16)
    lo = (x - hi.astype(F32)).astype(BF16)
    return hi, lo


def _sb_logits(q, kb, off, tpos, scale):
    z = _dot_nt(q, kb) * scale
    ls = jnp.minimum(z, 0.0) - jnp.log(1.0 + jnp.exp(-jnp.abs(z)))
    spos = off + lax.broadcasted_iota(jnp.int32, z.shape, 1)
    mask = spos < tpos
    lk = jnp.where(mask, ls - z, 0.0)
    return ls, lk, mask


def _sb_fwd(qkv, blk=256):
    t = qkv.shape[0]
    blk = min(blk, t)
    scale = HEAD_DIM ** -0.5

    def body(q_ref, k_ref, v_ref, o_ref, tot_ref):
        i = pl.program_id(1)
        q = q_ref[...]
        rj = lax.broadcasted_iota(jnp.int32, (blk, blk), 0)
        cj = lax.broadcasted_iota(jnp.int32, (blk, blk), 1)
        after = (rj > cj).astype(BF16)
        tpos = i * blk + rj

        def step(it, carry):
            run, acc = carry
            j = i - it
            off = pl.multiple_of(j * blk, blk)
            kb = k_ref[pl.ds(off, blk), :]
            vb = v_ref[pl.ds(off, blk), :]
            ls, lk, mask = _sb_logits(q, kb, off, tpos, scale)
            hi, lo = _split_bf16(lk)
            between = _dot(hi, after) + _dot(lo, after) + run
            a = jnp.where(mask, jnp.exp(ls + between), 0.0)
            acc = acc + _dot(a.astype(BF16), vb)
            run = run + jnp.sum(lk, axis=1, keepdims=True)
            return run, acc

        run, acc = lax.fori_loop(0, i + 1, step, (jnp.zeros((blk, 1), F32), jnp.zeros((blk, HEAD_DIM), F32)))
        o_ref[...] = acc.astype(BF16)
        tot_ref[...] = jnp.broadcast_to(run, (blk, HEAD_DIM))

    qs = pl.BlockSpec((blk, HEAD_DIM), lambda h, i: (i, h))
    ks = pl.BlockSpec((t, HEAD_DIM), lambda h, i: (0, N_HEADS + h))
    vs = pl.BlockSpec((t, HEAD_DIM), lambda h, i: (0, 2 * N_HEADS + h))
    return pl.pallas_call(
        body, name="sb_fwd", grid=(N_HEADS, t // blk),
        in_specs=[qs, ks, vs], out_specs=[qs, qs],
        out_shape=[jax.ShapeDtypeStruct((t, D_MODEL), BF16), jax.ShapeDtypeStruct((t, D_MODEL), F32)],
        compiler_params=_params(("parallel", "arbitrary")),
    )(qkv, qkv, qkv)


def _sb_bwd(qkv, tot, do, blk=256):
    t = qkv.shape[0]
    blk = min(blk, t)
    scale = HEAD_DIM ** -0.5

    def body(q_ref, k_ref, v_ref, tot_ref, do_ref, dq_ref, dk_ref, dv_ref):
        i = pl.program_id(1)

        @pl.when(i == 0)
        def _():
            dk_ref[...] = jnp.zeros_like(dk_ref)
            dv_ref[...] = jnp.zeros_like(dv_ref)

        q = q_ref[...]
        do = do_ref[...]
        total = tot_ref[:, 0:1]
        rj = lax.broadcasted_iota(jnp.int32, (blk, blk), 0)
        cj = lax.broadcasted_iota(jnp.int32, (blk, blk), 1)
        upto = (rj <= cj).astype(BF16)
        before = (rj < cj).astype(BF16)
        tpos = i * blk + rj

        def step(j, carry):
            run_k, run_e, dq = carry
            off = pl.multiple_of(j * blk, blk)
            kb = k_ref[pl.ds(off, blk), :]
            vb = v_ref[pl.ds(off, blk), :]
            ls, lk, mask = _sb_logits(q, kb, off, tpos, scale)
            hi, lo = _split_bf16(lk)
            between = total - (_dot(hi, upto) + _dot(lo, upto) + run_k)
            a = jnp.where(mask, jnp.exp(ls + between), 0.0)
            e = a * _dot_nt(do, vb)
            ehi, elo = _split_bf16(e)
            pre = _dot(ehi, before) + _dot(elo, before) + run_e
            sig = jnp.exp(ls)
            dz = (jnp.where(mask, e * (1.0 - sig) - pre * sig, 0.0) * scale).astype(BF16)
            dq = dq + _dot(dz, kb)
            dk_ref[pl.ds(off, blk), :] += _dot_tn(dz, q)
            dv_ref[pl.ds(off, blk), :] += _dot_tn(a.astype(BF16), do)
            return (run_k + jnp.sum(lk, axis=1, keepdims=True),
                    run_e + jnp.sum(e, axis=1, keepdims=True), dq)

        zero = jnp.zeros((blk, 1), F32)
        _, _, dq = lax.fori_loop(0, i + 1, step, (zero, zero, jnp.zeros((blk, HEAD_DIM), F32)))
        dq_ref[...] = dq

    qs = pl.BlockSpec((blk, HEAD_DIM), lambda h, i: (i, h))
    ks = pl.BlockSpec((t, HEAD_DIM), lambda h, i: (0, N_HEADS + h))
    vs = pl.BlockSpec((t, HEAD_DIM), lambda h, i: (0, 2 * N_HEADS + h))
    full = pl.BlockSpec((t, HEAD_DIM), lambda h, i: (0, h))
    big = jax.ShapeDtypeStruct((t, D_MODEL), F32)
    return pl.pallas_call(
        body, name="sb_bwd", grid=(N_HEADS, t // blk),
        in_specs=[qs, ks, vs, qs, qs], out_specs=[qs, full, full],
        out_shape=[big, big, big],
        compiler_params=_params(("parallel", "arbitrary")),
    )(qkv, qkv, qkv, tot, do)


def _merge_fwd(o_dn, o_sb, gl, x, wp_dn, wp_sb, w_out, w2):
    t = x.shape[0]
    tb = _rows(t)

    def body(odn_ref, osb_ref, gl_ref, x_ref, wpd_ref, wps_ref, wo_ref, w2_ref,
             pdn_ref, psb_ref, mix_ref, x1_ref, n2_ref):
        pdn = _dot(odn_ref[...], wpd_ref[...])
        psb = _dot(osb_ref[...], wps_ref[...])
        gates = jax.nn.sigmoid(gl_ref[...])
        mixed = (gates[:, :D_MODEL] * pdn + gates[:, D_MODEL:] * psb).astype(BF16)
        x1 = x_ref[...] + _dot(mixed, wo_ref[...])
        pdn_ref[...] = pdn
        psb_ref[...] = psb
        mix_ref[...] = mixed
        x1_ref[...] = x1
        n2_ref[...] = _rms(x1, w2_ref[...]).astype(BF16)

    row = pl.BlockSpec((tb, D_MODEL), lambda i: (i, 0))
    sq = pl.BlockSpec((D_MODEL, D_MODEL), lambda i: (0, 0))
    f = jax.ShapeDtypeStruct((t, D_MODEL), F32)
    b = jax.ShapeDtypeStruct((t, D_MODEL), BF16)
    return pl.pallas_call(
        body, name="merge_fwd", grid=(t // tb,),
        in_specs=[row, row, pl.BlockSpec((tb, 2 * D_MODEL), lambda i: (i, 0)), row, sq, sq, sq,
                  pl.BlockSpec((1, D_MODEL), lambda i: (0, 0))],
        out_specs=[row] * 5, out_shape=[f, f, b, f, b],
        compiler_params=_params(("parallel",)),
    )(o_dn, o_sb, gl, x, wp_dn, wp_sb, w_out, w2)


def _merge_bwd(dx2, dn2, x1, w2, gl, pdn, psb, wp_dn, wp_sb, w_out):
    t = x1.shape[0]
    tb = _rows(t)

    def body(dx2_ref, dn2_ref, x1_ref, w2_ref, gl_ref, pdn_ref, psb_ref, wpd_ref, wps_ref, wo_ref,
             dx1_ref, dw2_ref, dgl_ref, dpdn_ref, dpsb_ref, dodn_ref, dosb_ref):
        i = pl.program_id(0)
        _, vjp = jax.vjp(_rms, x1_ref[...], w2_ref[...])
        dxn, dw2 = vjp(dn2_ref[...])
        dx1 = dx2_ref[...] + dxn
        dx1_ref[...] = dx1

        @pl.when(i == 0)
        def _():
            dw2_ref[...] = jnp.zeros_like(dw2_ref)

        dw2_ref[...] += dw2
        dmix = _dot_nt(dx1.astype(BF16), wo_ref[...])
        gates = jax.nn.sigmoid(gl_ref[...])
        g_dn, g_sb = gates[:, :D_MODEL], gates[:, D_MODEL:]
        dpdn = (dmix * g_dn).astype(BF16)
        dpsb = (dmix * g_sb).astype(BF16)
        dgl_ref[:, :D_MODEL] = (dmix * pdn_ref[...] * g_dn * (1.0 - g_dn)).astype(BF16)
        dgl_ref[:, D_MODEL:] = (dmix * psb_ref[...] * g_sb * (1.0 - g_sb)).astype(BF16)
        dpdn_ref[...] = dpdn
        dpsb_ref[...] = dpsb
        dodn_ref[...] = _dot_nt(dpdn, wpd_ref[...])
        dosb_ref[...] = _dot_nt(dpsb, wps_ref[...]).astype(BF16)

    row = pl.BlockSpec((tb, D_MODEL), lambda i: (i, 0))
    wide = pl.BlockSpec((tb, 2 * D_MODEL), lambda i: (i, 0))
    sq = pl.BlockSpec((D_MODEL, D_MODEL), lambda i: (0, 0))
    vec = pl.BlockSpec((1, D_MODEL), lambda i: (0, 0))
    f = jax.ShapeDtypeStruct((t, D_MODEL), F32)
    b = jax.ShapeDtypeStruct((t, D_MODEL), BF16)
    return pl.pallas_call(
        body, name="merge_bwd", grid=(t // tb,),
        in_specs=[row, row, row, vec, wide, row, row, sq, sq, sq],
        out_specs=[row, vec, wide, row, row, row, row],
        out_shape=[f, jax.ShapeDtypeStruct((1, D_MODEL), F32), jax.ShapeDtypeStruct((t, 2 * D_MODEL), BF16),
                   b, b, f, b],
        compiler_params=_params(("arbitrary",)),
    )(dx2, dn2, x1, w2, gl, pdn, psb, wp_dn, wp_sb, w_out)


def _swiglu_fwd(ug, uu):
    t, c = ug.shape
    tb, cb = _rows(t), _pick(c, 512)

    def body(g_ref, u_ref, a_ref):
        a_ref[...] = (_silu(g_ref[...]) * u_ref[...]).astype(BF16)

    blk = pl.BlockSpec((tb, cb), lambda i, j: (i, j))
    return pl.pallas_call(
        body, name="swiglu_fwd", grid=(t // tb, c // cb), in_specs=[blk, blk], out_specs=blk,
        out_shape=jax.ShapeDtypeStruct((t, c), BF16), compiler_params=_params(("parallel", "parallel")),
    )(ug, uu)


def _swiglu_bwd(ug, uu, da):
    t, c = ug.shape
    tb, cb = _rows(t), _pick(c, 512)

    def body(g_ref, u_ref, da_ref, dg_ref, du_ref):
        _, vjp = jax.vjp(lambda g, u: _silu(g) * u, g_ref[...], u_ref[...])
        dg, du = vjp(da_ref[...])
        dg_ref[...] = dg
        du_ref[...] = du

    blk = pl.BlockSpec((tb, cb), lambda i, j: (i, j))
    f = jax.ShapeDtypeStruct((t, c), F32)
    return pl.pallas_call(
        body, name="swiglu_bwd", grid=(t // tb, c // cb), in_specs=[blk, blk, blk], out_specs=[blk, blk],
        out_shape=[f, f], compiler_params=_params(("parallel", "parallel")),
    )(ug, uu, da)


def _down_loss(a, w_down, x1, wf, target):
    t = x1.shape[0]
    tb = _rows(t)

    def body(a_ref, wd_ref, x1_ref, wf_ref, tgt_ref, dx2_ref, dwf_ref, loss_ref):
        i = pl.program_id(0)
        x2 = x1_ref[...] + _dot(a_ref[...], wd_ref[...])
        y, vjp = jax.vjp(_rms, x2, wf_ref[...])
        err = y - tgt_ref[...]
        dx2, dwf = vjp(err * (1.0 / D_MODEL))
        dx2_ref[...] = dx2
        part = jnp.sum(jnp.sum(err * err, axis=1, keepdims=True), axis=0, keepdims=True) * (0.5 / D_MODEL)

        @pl.when(i == 0)
        def _():
            dwf_ref[...] = jnp.zeros_like(dwf_ref)
            loss_ref[...] = jnp.zeros_like(loss_ref)

        dwf_ref[...] += dwf
        loss_ref[...] += jnp.broadcast_to(part, loss_ref.shape)

    row = pl.BlockSpec((tb, D_MODEL), lambda i: (i, 0))
    vec = pl.BlockSpec((1, D_MODEL), lambda i: (0, 0))
    return pl.pallas_call(
        body, name="down_loss", grid=(t // tb,),
        in_specs=[pl.BlockSpec((tb, D_FF), lambda i: (i, 0)), pl.BlockSpec((D_FF, D_MODEL), lambda i: (0, 0)),
                  row, vec, row],
        out_specs=[row, vec, pl.BlockSpec((1, LANES), lambda i: (0, 0))],
        out_shape=[jax.ShapeDtypeStruct((t, D_MODEL), F32), jax.ShapeDtypeStruct((1, D_MODEL), F32),
                   jax.ShapeDtypeStruct((1, LANES), F32)],
        compiler_params=_params(("arbitrary",)),
    )(a, w_down, x1, wf, target)


def _local_step(x, target, wts):
    t = x.shape[0]
    nchunk = t // DN_CHUNK

    n1, hab = _norm1_fwd(x, wts["norm1"], wts["w_ab"])
    dnqkv = _mm(n1, wts["w_dnqkv"], name="h_dnqkv")
    dngate = _mm(n1, wts["w_dngate"], name="h_dngate")
    sbqkv = _mm(n1, wts["w_sbqkv"], out_dtype=BF16, name="h_sbqkv")
    gl = _mm(n1, wts["w_gl"], name="h_gl")

    cdn = _conv_fwd(dnqkv, wts["dn_conv"], "dn_conv_fwd")
    qn, kn, vv, gb = _dn_prep_fwd(cdn, hab, wts["alog"], wts["dtb"])
    per_head = gb[:, :2 * N_HEADS].T.reshape(2 * N_HEADS, nchunk, DN_CHUNK)
    gcol, bcol = per_head[:N_HEADS, :, :, None], per_head[N_HEADS:, :, :, None]
    grow = per_head[:N_HEADS, :, None, :]
    u_dn, w_dn, a_qk, qe, kdec, egl, tinv = _dn_local_fwd(qn, kn, vv, gcol, grow, bcol)
    o_raw, states = _dn_seq_fwd(u_dn, w_dn, a_qk, qe, kdec, egl)
    o_dn = _dn_post_fwd(o_raw, dngate, wts["dn_norm"])

    o_sb, tot = _sb_fwd(sbqkv)

    pdn, psb, mixed, x1, n2 = _merge_fwd(o_dn, o_sb, gl, x, wts["wp_dn"], wts["wp_sb"], wts["w_out"],
                                         wts["norm2"])
    pre_g = _mm(n2, wts["w_up_g"], name="ffn_up_g")
    pre_u = _mm(n2, wts["w_up_u"], name="ffn_up_u")
    ug = _conv_fwd(pre_g, wts["ffn_conv_g"], "ffn_conv_g_fwd")
    uu = _conv_fwd(pre_u, wts["ffn_conv_u"], "ffn_conv_u_fwd")
    act = _swiglu_fwd(ug, uu)
    dx2, d_normf, loss_part = _down_loss(act, wts["w_down"], x1, wts["normf"], target)

    grads = {"normf": d_normf}
    da = _mm(dx2, wts["w_down"], tb=True, name="d_act")
    grads["w_down"] = _mm(act, dx2, ta=True, name="dw_down")
    dug, duu = _swiglu_bwd(ug, uu, da)
    dpre_g, dcw_g = _conv_bwd(dug, pre_g, wts["ffn_conv_g"], "ffn_conv_g_bwd", BF16)
    dpre_u, dcw_u = _conv_bwd(duu, pre_u, wts["ffn_conv_u"], "ffn_conv_u_bwd", BF16)
    grads["ffn_conv"] = jnp.concatenate([dcw_g[:FFN_CONV], dcw_u[:FFN_CONV]], axis=1)
    dn2 = _mm(dpre_g, wts["w_up_g"], tb=True, name="dn2_g")
    dn2 = _mm(dpre_u, wts["w_up_u"], tb=True, add=dn2, name="dn2_u")
    grads["w_up"] = jnp.concatenate([_mm(n2, dpre_g, ta=True, name="dw_up_g"),
                                     _mm(n2, dpre_u, ta=True, name="dw_up_u")], axis=1)

    dx1, grads["norm2"], dgl, dpdn, dpsb, do_dn, do_sb = _merge_bwd(
        dx2, dn2, x1, wts["norm2"], gl, pdn, psb, wts["wp_dn"], wts["wp_sb"], wts["w_out"])
    grads["w_out"] = _mm(mixed, dx1, ta=True, name="dw_out")
    grads["wp_dn"] = _mm(o_dn, dpdn, ta=True, name="dw_proj_dn")
    grads["wp_sb"] = _mm(o_sb, dpsb, ta=True, name="dw_proj_sb")

    dsq, dsk, dsv = _sb_bwd(sbqkv, tot, do_sb)
    dsbqkv = jnp.concatenate([dsq, dsk, dsv], axis=1).astype(BF16)

    do_raw, ddngate, grads["dn_norm"] = _dn_post_bwd(o_raw, dngate, wts["dn_norm"], do_dn)
    seq_grads = _dn_seq_bwd(u_dn, w_dn, a_qk, qe, kdec, egl, states, do_raw)
    dqn, dkn, dvv, dgcol, dgrow, dbcol = _dn_local_bwd(qn, kn, vv, gcol, grow, bcol, tinv, *seq_grads)
    dg = (dgcol[..., 0] + dgrow[:, :, 0, :]).reshape(N_HEADS, t)
    dgb = jnp.concatenate([dg, dbcol[..., 0].reshape(N_HEADS, t)], axis=0).T
    dgb = jnp.pad(dgb, ((0, 0), (0, LANES - 2 * N_HEADS)))
    dcdn, dhab, grads["alog"], grads["dtb"] = _dn_prep_bwd(cdn, hab, wts["alog"], wts["dtb"], dqn, dkn, dvv, dgb)
    ddnqkv, dcw_dn = _conv_bwd(dcdn, dnqkv, wts["dn_conv"], "dn_conv_bwd", BF16)
    grads["dn_conv"] = dcw_dn[:DN_CONV]

    dn1 = _mm(ddnqkv, wts["w_dnqkv"], tb=True, name="dn1_dnqkv")
    dn1 = _mm(ddngate, wts["w_dngate"], tb=True, add=dn1, name="dn1_dngate")
    dn1 = _mm(dsbqkv, wts["w_sbqkv"], tb=True, add=dn1, name="dn1_sbqkv")
    dn1 = _mm(dgl, wts["w_gl"], tb=True, add=dn1, name="dn1_gl")
    grads["w_dnqkv"] = _mm(n1, ddnqkv, ta=True, name="dw_dnqkv")
    grads["w_dngate"] = _mm(n1, ddngate, ta=True, name="dw_dngate")
    grads["w_sbqkv"] = _mm(n1, dsbqkv, ta=True, name="dw_sbqkv")
    grads["w_gl"] = _mm(n1, dgl, ta=True, name="dw_gl")
    grads["w_ab"] = _mm(n1, dhab, ta=True, name="dw_ab")
    grad_x, grads["norm1"] = _norm1_bwd(x, wts["norm1"], dn1, dx1, dhab, wts["w_ab"])
    return loss_part, grad_x, grads


def _place():
    return lax.axis_index("x"), lax.axis_index("y"), lax.axis_index("c")


def _gather_shards(shard):
    rows, cols = shard.shape
    half = rows // 2

    def body(in_ref, out_ref, send_sems, recv_sems, local_sem):
        x, y, c = _place()
        me = 2 * x + y
        sibling = (x, y, 1 - c)
        chips = [(1 - x, y), (x, 1 - y), (1 - x, 1 - y)]

        def slab(chip_index, part):
            return out_ref.at[chip_index, pl.ds(part * half, half), :]

        def copy(k, src, dst, to):
            return pltpu.make_async_remote_copy(src_ref=src, dst_ref=dst, send_sem=send_sems.at[k],
                                                recv_sem=recv_sems.at[k], device_id=to, device_id_type=MESH)

        mine = pltpu.make_async_copy(in_ref, out_ref.at[me], local_sem)
        mine.start()
        my_half = in_ref.at[pl.ds(c * half, half), :]
        first = [copy(j, my_half, slab(me, c), (px, py, c)) for j, (px, py) in enumerate(chips)]
        for cp in first:
            cp.start()
        passed = []
        for j, (px, py) in enumerate(chips):
            landed = slab(2 * px + py, c)
            copy(j, landed, landed, (px, py, c)).wait_recv()
            fwd = copy(3 + j, landed, landed, sibling)
            fwd.start()
            passed.append(fwd)
        for j, (px, py) in enumerate(chips):
            there = slab(2 * px + py, 1 - c)
            copy(3 + j, there, there, sibling).wait_recv()
        for cp in first + passed:
            cp.wait_send()
        mine.wait()

    return pl.pallas_call(
        body, name="gather_weights",
        in_specs=[pl.BlockSpec(memory_space=pltpu.HBM)],
        out_specs=pl.BlockSpec(memory_space=pltpu.HBM),
        out_shape=jax.ShapeDtypeStruct((N_CHIPS, rows, cols), shard.dtype),
        scratch_shapes=[pltpu.SemaphoreType.DMA((6,)), pltpu.SemaphoreType.DMA((6,)), pltpu.SemaphoreType.DMA],
    )(shard)


def _pair_exchange_halves(g):
    nsh, rows, cols = g.shape
    half = rows // 2

    def body(in_ref, out_ref, send_sem, recv_sem):
        x, y, c = _place()
        src = in_ref.at[:, pl.ds((1 - c) * half, half), :]
        cp = pltpu.make_async_remote_copy(src_ref=src, dst_ref=out_ref, send_sem=send_sem, recv_sem=recv_sem,
                                          device_id=(x, y, 1 - c), device_id_type=MESH)
        cp.start()
        cp.wait()

    return pl.pallas_call(
        body, name="grad_pair_exchange",
        in_specs=[pl.BlockSpec(memory_space=pltpu.HBM)],
        out_specs=pl.BlockSpec(memory_space=pltpu.HBM),
        out_shape=jax.ShapeDtypeStruct((nsh, half, cols), g.dtype),
        scratch_shapes=[pltpu.SemaphoreType.DMA, pltpu.SemaphoreType.DMA],
    )(g)


def _pair_add(g, got, c_idx):
    nsh, rows, cols = g.shape
    half = rows // 2
    rb = _pick_rows(half)

    def body(c_ref, g_ref, got_ref, o_ref):
        o_ref[...] = (g_ref[...].astype(F32) + got_ref[...].astype(F32)).astype(BF16)

    nb = half // rb
    grid_spec = pltpu.PrefetchScalarGridSpec(
        num_scalar_prefetch=1, grid=(nsh, nb),
        in_specs=[pl.BlockSpec((1, rb, cols), lambda s, i, c_ref: (s, c_ref[0] * nb + i, 0)),
                  pl.BlockSpec((1, rb, cols), lambda s, i, c_ref: (s, i, 0))],
        out_specs=pl.BlockSpec((1, rb, cols), lambda s, i, c_ref: (s, i, 0)))
    return pl.pallas_call(
        body, name="grad_pair_add", grid_spec=grid_spec,
        out_shape=jax.ShapeDtypeStruct((nsh, half, cols), BF16),
        compiler_params=_params(("parallel", "parallel")),
    )(c_idx, g, got)


def _pick_rows(n, target=512):
    best = 16
    for b in range(16, min(n, target) + 1, 16):
        if n % b == 0:
            best = b
    return best


def _chip_exchange(p):
    nsh, half, cols = p.shape

    def body(in_ref, out_ref, send_sems, recv_sems, local_sem):
        x, y, c = _place()
        me = 2 * x + y
        chips = [(1 - x, y), (x, 1 - y), (1 - x, 1 - y)]
        mine = pltpu.make_async_copy(in_ref.at[me], out_ref.at[me], local_sem)
        mine.start()
        sends = []
        for j, (px, py) in enumerate(chips):
            cp = pltpu.make_async_remote_copy(src_ref=in_ref.at[2 * px + py], dst_ref=out_ref.at[me],
                                              send_sem=send_sems.at[j], recv_sem=recv_sems.at[j],
                                              device_id=(px, py, c), device_id_type=MESH)
            cp.start()
            sends.append(cp)
        for j, (px, py) in enumerate(chips):
            there = out_ref.at[2 * px + py]
            pltpu.make_async_remote_copy(src_ref=there, dst_ref=there, send_sem=send_sems.at[j],
                                         recv_sem=recv_sems.at[j], device_id=(px, py, c),
                                         device_id_type=MESH).wait_recv()
        for cp in sends:
            cp.wait_send()
        mine.wait()

    return pl.pallas_call(
        body, name="grad_chip_exchange",
        in_specs=[pl.BlockSpec(memory_space=pltpu.HBM)],
        out_specs=pl.BlockSpec(memory_space=pltpu.HBM),
        out_shape=jax.ShapeDtypeStruct((nsh, half, cols), p.dtype),
        scratch_shapes=[pltpu.SemaphoreType.DMA((3,)), pltpu.SemaphoreType.DMA((3,)), pltpu.SemaphoreType.DMA],
    )(p)


def _sum_slots(parts):
    nsh, half, cols = parts.shape
    rb = _pick_rows(half)

    def body(p_ref, o_ref):
        acc = p_ref[0].astype(F32)
        for s in range(1, nsh):
            acc = acc + p_ref[s].astype(F32)
        o_ref[...] = acc

    return pl.pallas_call(
        body, name="grad_sum_chips", grid=(half // rb,),
        in_specs=[pl.BlockSpec((nsh, rb, cols), lambda i: (0, i, 0))],
        out_specs=pl.BlockSpec((rb, cols), lambda i: (i, 0)),
        out_shape=jax.ShapeDtypeStruct((half, cols), F32),
        compiler_params=_params(("parallel",)),
    )(parts)


def _pair_share(r):
    half, cols = r.shape

    def body(in_ref, out_ref, send_sem, recv_sem, local_sem):
        x, y, c = _place()
        mine = pltpu.make_async_copy(in_ref, out_ref.at[c], local_sem)
        mine.start()
        cp = pltpu.make_async_remote_copy(src_ref=in_ref, dst_ref=out_ref.at[c], send_sem=send_sem,
                                          recv_sem=recv_sem, device_id=(x, y, 1 - c), device_id_type=MESH)
        cp.start()
        there = out_ref.at[1 - c]
        pltpu.make_async_remote_copy(src_ref=there, dst_ref=there, send_sem=send_sem, recv_sem=recv_sem,
                                     device_id=(x, y, 1 - c), device_id_type=MESH).wait_recv()
        cp.wait_send()
        mine.wait()

    return pl.pallas_call(
        body, name="grad_pair_share",
        in_specs=[pl.BlockSpec(memory_space=pltpu.HBM)],
        out_specs=pl.BlockSpec(memory_space=pltpu.HBM),
        out_shape=jax.ShapeDtypeStruct((2, half, cols), r.dtype),
        scratch_shapes=[pltpu.SemaphoreType.DMA, pltpu.SemaphoreType.DMA, pltpu.SemaphoreType.DMA],
    )(r)


def _small_allreduce(v):
    rows, cols = v.shape
    ndev = 8

    def body(in_ref, out_ref, slots, send_sems, recv_sems):
        x, y, c = _place()
        me = 4 * x + 2 * y + c
        slots[me] = in_ref[...]
        sends = []
        for k in range(1, ndev):
            peer = (x ^ (k >> 2), y ^ ((k >> 1) & 1), c ^ (k & 1))
            cp = pltpu.make_async_remote_copy(src_ref=in_ref, dst_ref=slots.at[me], send_sem=send_sems.at[k - 1],
                                              recv_sem=recv_sems.at[k - 1], device_id=peer, device_id_type=MESH)
            cp.start()
            sends.append(cp)
        for k in range(1, ndev):
            there = slots.at[me ^ k]
            pltpu.make_async_remote_copy(src_ref=there, dst_ref=there, send_sem=send_sems.at[k - 1],
                                         recv_sem=recv_sems.at[k - 1], device_id=(x, y, c),
                                         device_id_type=MESH).wait_recv()
        for cp in sends:
            cp.wait_send()
        acc = slots[0]
        for s in range(1, ndev):
            acc = acc + slots[s]
        out_ref[...] = acc

    return pl.pallas_call(
        body, name="small_allreduce",
        in_specs=[pl.BlockSpec(memory_space=pltpu.VMEM)],
        out_specs=pl.BlockSpec(memory_space=pltpu.VMEM),
        out_shape=jax.ShapeDtypeStruct((rows, cols), F32),
        scratch_shapes=[pltpu.VMEM((ndev, rows, cols), F32), pltpu.SemaphoreType.DMA((ndev - 1,)),
                        pltpu.SemaphoreType.DMA((ndev - 1,))],
    )(v)


def _adamw(w, g, m, v, name):
    r, c = w.shape
    rb = r if r <= 128 else _pick_rows_8(r, 128)
    c1 = 1.0 - ADAM_B1 ** ADAM_STEP
    c2 = 1.0 - ADAM_B2 ** ADAM_STEP

    def body(w_ref, g_ref, m_ref, v_ref, d_ref, nm_ref, nv_ref):
        gg = g_ref[...]
        nm = ADAM_B1 * m_ref[...] + (1.0 - ADAM_B1) * gg
        nv = ADAM_B2 * v_ref[...] + (1.0 - ADAM_B2) * (gg * gg)
        d_ref[...] = -ADAM_LR * ((nm / c1) / (jnp.sqrt(nv / c2) + ADAM_EPS) + ADAM_WD * w_ref[...])
        nm_ref[...] = nm
        nv_ref[...] = nv

    blk = pl.BlockSpec((rb, c), lambda i: (i, 0))
    shp = jax.ShapeDtypeStruct((r, c), F32)
    return pl.pallas_call(
        body, name=name, grid=(r // rb,), in_specs=[blk] * 4, out_specs=[blk] * 3, out_shape=[shp] * 3,
        compiler_params=_params(("parallel",)),
    )(w, g, m, v)


def _pick_rows_8(n, target):
    best = n
    for b in range(8, min(n, target) + 1, 8):
        if n % b == 0:
            best = b
    return best


W_IN_COLS = 2308
W_UP_COLS = 1408
W_DOWN_ROWS = 704
DN_CONV_COLS = 768
FFN_CONV_COLS = 1408
PROJ_ROWS = 256
ROW_TILE = 16
SEG = [("w_in", W_IN_COLS), ("wp_dn", PROJ_ROWS), ("wp_sb", PROJ_ROWS), ("w_out", PROJ_ROWS),
       ("w_up", W_UP_COLS), ("w_down", W_DOWN_ROWS), ("dn_conv", ROW_TILE), ("ffn_conv", ROW_TILE)]


def _seg_offsets():
    offs, at = {}, 0
    for nm, n in SEG:
        offs[nm] = (at, n)
        at += -(-n // ROW_TILE) * ROW_TILE
    assert at <= PACK_ROWS and PACK_ROWS % (2 * ROW_TILE) == 0
    return offs, at


PACK_OFFS, PACK_USED = _seg_offsets()


def _tile_rows(a, axis):
    n = a.shape[axis]
    pad = [(0, 0)] * a.ndim
    pad[axis] = (0, -(-n // ROW_TILE) * ROW_TILE - n)
    return jnp.pad(a, pad)


def _flat_rows(a, nrows):
    flat = a.reshape(-1)
    return jnp.pad(flat, (0, nrows * D_MODEL - flat.shape[0])).reshape(nrows, D_MODEL)


def _pack_weight_shard(w_in, wp_dn, wp_sb, w_out, w_up, w_down, dn_conv, ffn_conv):
    parts = [w_in.astype(BF16).reshape(W_IN_COLS, D_MODEL), wp_dn.astype(BF16), wp_sb.astype(BF16),
             w_out.astype(BF16), w_up.astype(BF16).reshape(W_UP_COLS, D_MODEL), w_down.astype(BF16),
             _flat_rows(lax.bitcast_convert_type(dn_conv, BF16), ROW_TILE),
             _flat_rows(lax.bitcast_convert_type(ffn_conv, BF16), ROW_TILE),
             jnp.zeros((PACK_ROWS - PACK_USED, D_MODEL), BF16)]
    return jnp.concatenate([_tile_rows(p, 0) for p in parts], axis=0)


def _unpack_weights(g):
    def seg(nm):
        at, n = PACK_OFFS[nm]
        return g[:, at:at + n, :]

    def cols(nm, ncols):
        return seg(nm).reshape(N_CHIPS, D_MODEL, ncols).transpose(1, 0, 2).reshape(D_MODEL, N_CHIPS * ncols)

    def f32_rows(nm, k, ncols):
        raw = seg(nm).reshape(N_CHIPS, -1)[:, :2 * k * ncols].reshape(N_CHIPS, k * ncols, 2)
        vals = lax.bitcast_convert_type(raw, F32).reshape(N_CHIPS, k, ncols)
        return vals.transpose(1, 0, 2).reshape(k, N_CHIPS * ncols)

    w_in = cols("w_in", W_IN_COLS)
    w_up = cols("w_up", W_UP_COLS)
    ffn_conv = f32_rows("ffn_conv", FFN_CONV, FFN_CONV_COLS)
    q_end, a_end, g_end, s_end = 3 * D_MODEL, 3 * D_MODEL + 2 * N_HEADS, 4 * D_MODEL + 2 * N_HEADS, 7 * D_MODEL + 2 * N_HEADS
    return {
        "w_dnqkv": w_in[:, :q_end],
        "w_ab": jnp.pad(w_in[:, q_end:a_end], ((0, 0), (0, LANES - 2 * N_HEADS))),
        "w_dngate": w_in[:, a_end:g_end],
        "w_sbqkv": w_in[:, g_end:s_end],
        "w_gl": w_in[:, s_end:],
        "wp_dn": seg("wp_dn").reshape(D_MODEL, D_MODEL),
        "wp_sb": seg("wp_sb").reshape(D_MODEL, D_MODEL),
        "w_out": seg("w_out").reshape(D_MODEL, D_MODEL),
        "w_up_g": w_up[:, :D_FF], "w_up_u": w_up[:, D_FF:],
        "w_down": seg("w_down").reshape(D_FF, D_MODEL),
        "dn_conv": f32_rows("dn_conv", DN_CONV, DN_CONV_COLS),
        "ffn_conv_g": ffn_conv[:, :D_FF], "ffn_conv_u": ffn_conv[:, D_FF:],
    }


def _pack_grads(gr):
    w_in = jnp.concatenate([gr["w_dnqkv"], gr["w_ab"][:, :2 * N_HEADS], gr["w_dngate"], gr["w_sbqkv"], gr["w_gl"]],
                           axis=1)

    def cols(a, ncols):
        return a.reshape(a.shape[0], N_CHIPS, ncols).transpose(1, 0, 2)

    def rows(a, nrows):
        return a.reshape(N_CHIPS, nrows, a.shape[1])

    def flat(a, nrows):
        a = a.reshape(N_CHIPS, -1)
        return jnp.pad(a, ((0, 0), (0, nrows * D_MODEL - a.shape[1]))).reshape(N_CHIPS, nrows, D_MODEL)

    parts = [cols(w_in, W_IN_COLS).reshape(N_CHIPS, W_IN_COLS, D_MODEL),
             rows(gr["wp_dn"], PROJ_ROWS), rows(gr["wp_sb"], PROJ_ROWS), rows(gr["w_out"], PROJ_ROWS),
             cols(gr["w_up"], W_UP_COLS).reshape(N_CHIPS, W_UP_COLS, D_MODEL),
             rows(gr["w_down"], W_DOWN_ROWS),
             flat(cols(gr["dn_conv"], DN_CONV_COLS), ROW_TILE), flat(cols(gr["ffn_conv"], FFN_CONV_COLS), ROW_TILE),
             jnp.zeros((N_CHIPS, PACK_ROWS - PACK_USED, D_MODEL), F32)]
    return jnp.concatenate([_tile_rows(p, 1) for p in parts], axis=1).astype(BF16)


def _unpack_grad_shard(r):
    def seg(nm):
        at, n = PACK_OFFS[nm]
        return r[at:at + n, :]

    return {
        "w_in": seg("w_in").reshape(D_MODEL, W_IN_COLS),
        "wp_dn": seg("wp_dn"), "wp_sb": seg("wp_sb"), "w_out": seg("w_out"),
        "w_up": seg("w_up").reshape(D_MODEL, W_UP_COLS),
        "w_down": seg("w_down"),
        "dn_conv": seg("dn_conv").reshape(-1)[:DN_CONV * DN_CONV_COLS].reshape(DN_CONV, DN_CONV_COLS),
        "ffn_conv": seg("ffn_conv").reshape(-1)[:FFN_CONV * FFN_CONV_COLS].reshape(FFN_CONV, FFN_CONV_COLS),
    }


def _lane_row(v):
    return jnp.pad(v.reshape(1, -1), ((0, 0), (0, LANES - v.size)))


def kernel(x, norm1_w, w_in, dn_conv_w, dn_A_log, dn_dt_bias, dn_norm_w, w_proj_dn, w_proj_sb, w_out, norm2_w, ffn_w_up, ffn_conv_w, ffn_w_down, norm_f_w, loss_target, m_norm1_w, m_w_in, m_dn_conv_w, m_dn_A_log, m_dn_dt_bias, m_dn_norm_w, m_w_proj_dn, m_w_proj_sb, m_w_out, m_norm2_w, m_ffn_w_up, m_ffn_conv_w, m_ffn_w_down, m_norm_f_w, v_norm1_w, v_w_in, v_dn_conv_w, v_dn_A_log, v_dn_dt_bias, v_dn_norm_w, v_w_proj_dn, v_w_proj_sb, v_w_out, v_norm2_w, v_ffn_w_up, v_ffn_conv_w, v_ffn_w_down, v_norm_f_w):
    shard = _pack_weight_shard(w_in[0], w_proj_dn[0], w_proj_sb[0], w_out[0], ffn_w_up[0], ffn_w_down[0],
                               dn_conv_w[0], ffn_conv_w[0])
    wts = _unpack_weights(_gather_shards(shard))
    wts.update(norm1=norm1_w, norm2=norm2_w, normf=norm_f_w.reshape(1, D_MODEL), dn_norm=dn_norm_w,
               alog=_lane_row(dn_A_log), dtb=_lane_row(dn_dt_bias))

    loss_part, grad_x, gr = _local_step(x[0], loss_target[0], wts)

    c_idx = lax.axis_index("c").astype(jnp.int32).reshape(1)
    packed = _pack_grads(gr)
    partial_sum = _pair_add(packed, _pair_exchange_halves(packed), c_idx)
    reduced_half = _sum_slots(_chip_exchange(partial_sum))
    gsh = _unpack_grad_shard(_pair_share(reduced_half).reshape(PACK_ROWS, D_MODEL))

    tail = jnp.concatenate([gr["dn_norm"], gr["alog"][:, :N_HEADS], gr["dtb"][:, :N_HEADS], loss_part[:, :1]], axis=1)
    small = jnp.concatenate([gr["norm1"], gr["norm2"], gr["normf"],
                             jnp.pad(tail, ((0, 0), (0, D_MODEL - tail.shape[1]))),
                             jnp.zeros((SMALL_ROWS - 4, D_MODEL), F32)], axis=0)
    small = _small_allreduce(small)
    at = HEAD_DIM
    g_small = {"norm1_w": small[0:1], "norm2_w": small[1:2], "norm_f_w": small[2],
               "dn_norm_w": small[3:4, :at], "dn_A_log": small[3:4, at:at + N_HEADS],
               "dn_dt_bias": small[3:4, at + N_HEADS:at + 2 * N_HEADS]}
    loss = small[3, at + 2 * N_HEADS]

    big = {"w_in": (w_in, m_w_in, v_w_in, gsh["w_in"]), "dn_conv_w": (dn_conv_w, m_dn_conv_w, v_dn_conv_w, gsh["dn_conv"]),
           "w_proj_dn": (w_proj_dn, m_w_proj_dn, v_w_proj_dn, gsh["wp_dn"]),
           "w_proj_sb": (w_proj_sb, m_w_proj_sb, v_w_proj_sb, gsh["wp_sb"]),
           "w_out": (w_out, m_w_out, v_w_out, gsh["w_out"]),
           "ffn_w_up": (ffn_w_up, m_ffn_w_up, v_ffn_w_up, gsh["w_up"]),
           "ffn_conv_w": (ffn_conv_w, m_ffn_conv_w, v_ffn_conv_w, gsh["ffn_conv"]),
           "ffn_w_down": (ffn_w_down, m_ffn_w_down, v_ffn_w_down, gsh["w_down"])}
    res = {}
    for nm, (w, m, v, g) in big.items():
        d, nm_, nv_ = _adamw(w[0], g, m[0], v[0], "adamw_" + nm)
        res[nm] = (g[None], d[None], nm_[None], nv_[None])

    names = ["norm1_w", "norm2_w", "norm_f_w", "dn_norm_w", "dn_A_log", "dn_dt_bias"]
    given = {"norm1_w": (norm1_w, m_norm1_w, v_norm1_w), "norm2_w": (norm2_w, m_norm2_w, v_norm2_w),
             "norm_f_w": (norm_f_w, m_norm_f_w, v_norm_f_w), "dn_norm_w": (dn_norm_w, m_dn_norm_w, v_dn_norm_w),
             "dn_A_log": (dn_A_log, m_dn_A_log, v_dn_A_log), "dn_dt_bias": (dn_dt_bias, m_dn_dt_bias, v_dn_dt_bias)}

    def stack(k, fill):
        rows = [jnp.pad(given[nm][k].reshape(1, -1), ((0, 0), (0, D_MODEL - given[nm][k].size)),
                        constant_values=fill) for nm in names]
        return jnp.concatenate(rows + [jnp.full((SMALL_ROWS - len(names), D_MODEL), fill, F32)], axis=0)

    g_rows = jnp.concatenate(
        [jnp.pad(g_small[nm].reshape(1, -1), ((0, 0), (0, D_MODEL - g_small[nm].size))) for nm in names]
        + [jnp.zeros((SMALL_ROWS - len(names), D_MODEL), F32)], axis=0)
    d_s, m_s, v_s = _adamw(stack(0, 0.0), g_rows, stack(1, 0.0), stack(2, 1.0), "adamw_small")
    for r, nm in enumerate(names):
        shape = given[nm][0].shape
        n = given[nm][0].size
        res[nm] = (g_small[nm].reshape(shape), d_s[r, :n].reshape(shape), m_s[r, :n].reshape(shape),
                   v_s[r, :n].reshape(shape))

    order = ["norm1_w", "w_in", "dn_conv_w", "dn_A_log", "dn_dt_bias", "dn_norm_w", "w_proj_dn", "w_proj_sb",
             "w_out", "norm2_w", "ffn_w_up", "ffn_conv_w", "ffn_w_down", "norm_f_w"]
    outs = [loss, grad_x[None]]
    for k in range(4):
        outs += [res[nm][k] for nm in order]
    return tuple(outs)
```

```python
import functools

import jax
import jax.numpy as jnp
from jax import lax
from jax.experimental import pallas as pl
from jax.experimental.pallas import tpu as pltpu

F32 = jnp.float32
BF16 = jnp.bfloat16
HIGHEST = lax.Precision.HIGHEST
MESH = pl.DeviceIdType.MESH

EPS = 1e-6
D_MODEL = 1024
N_HEADS = 8
HEAD_DIM = 128
DN_CONV = 4
DN_CHUNK = 64
D_FF = 2816
FFN_CONV = 3
ADAM_LR, ADAM_B1, ADAM_B2, ADAM_EPS, ADAM_WD, ADAM_STEP = 0.001, 0.9, 0.999, 1e-08, 0.01, 10

N_CHIPS = 4
LANES = 128
HALO = 8
VMEM_LIMIT = 48 * 1024 * 1024
PACK_ROWS = 5248
SMALL_ROWS = 8


def _params(sem=None):
    return pltpu.CompilerParams(dimension_semantics=sem, vmem_limit_bytes=VMEM_LIMIT)


def _pick(n, target):
    best = None
    for b in range(LANES, min(n, target) + 1, LANES):
        if n % b == 0:
            best = b
    return best or n


def _rows(t, target=256):
    return min(t, target)


def _dot(a, b, precision=None):
    return lax.dot_general(a, b, (((1,), (0,)), ((), ())), precision=precision, preferred_element_type=F32)


def _dot_nt(a, b, precision=None):
    return lax.dot_general(a, b, (((1,), (1,)), ((), ())), precision=precision, preferred_element_type=F32)


def _dot_tn(a, b, precision=None):
    return lax.dot_general(a, b, (((0,), (0,)), ((), ())), precision=precision, preferred_element_type=F32)


def _rms(x, w):
    return x * lax.rsqrt(jnp.mean(x * x, axis=-1, keepdims=True) + EPS) * w


def _silu(x):
    return x * jax.nn.sigmoid(x)


def _softplus(x):
    return jnp.maximum(x, 0.0) + jnp.log(1.0 + jnp.exp(-jnp.abs(x)))


def _mm(a, b, *, ta=False, tb=False, add=None, out_dtype=F32, name, bm=512, bn=512, bk=1024):
    m = a.shape[1] if ta else a.shape[0]
    k = a.shape[0] if ta else a.shape[1]
    n = b.shape[0] if tb else b.shape[1]
    bm, bn, bk = _pick(m, bm), _pick(n, bn), _pick(k, bk)
    nk = k // bk
    dims = (((0 if ta else 1,), (1 if tb else 0,)), ((), ()))

    def body(*refs):
        if add is None:
            a_ref, b_ref, o_ref, acc = refs
        else:
            a_ref, b_ref, c_ref, o_ref, acc = refs
        kk = pl.program_id(2)

        @pl.when(kk == 0)
        def _():
            acc[...] = jnp.zeros_like(acc)

        acc[...] += lax.dot_general(a_ref[...].astype(BF16), b_ref[...].astype(BF16), dims,
                                    preferred_element_type=F32)

        @pl.when(kk == nk - 1)
        def _():
            r = acc[...]
            if add is not None:
                r = r + c_ref[...].astype(F32)
            o_ref[...] = r.astype(out_dtype)

    a_spec = (pl.BlockSpec((bk, bm), lambda i, j, kk: (kk, i)) if ta
              else pl.BlockSpec((bm, bk), lambda i, j, kk: (i, kk)))
    b_spec = (pl.BlockSpec((bn, bk), lambda i, j, kk: (j, kk)) if tb
              else pl.BlockSpec((bk, bn), lambda i, j, kk: (kk, j)))
    o_spec = pl.BlockSpec((bm, bn), lambda i, j, kk: (i, j))
    in_specs = [a_spec, b_spec] + ([o_spec] if add is not None else [])
    args = (a, b) + ((add,) if add is not None else ())
    return pl.pallas_call(
        body, name=name, grid=(m // bm, n // bn, nk),
        in_specs=in_specs, out_specs=o_spec,
        out_shape=jax.ShapeDtypeStruct((m, n), out_dtype),
        scratch_shapes=[pltpu.VMEM((bm, bn), F32)],
        compiler_params=_params(("parallel", "parallel", "arbitrary")),
    )(*args)


def _norm1_fwd(x, w, w_ab):
    t = x.shape[0]
    tb = _rows(t)

    def body(x_ref, w_ref, wab_ref, n_ref, hab_ref):
        n = _rms(x_ref[...], w_ref[...]).astype(BF16)
        n_ref[...] = n
        hab_ref[...] = _dot(n, wab_ref[...])

    return pl.pallas_call(
        body, name="norm1_fwd", grid=(t // tb,),
        in_specs=[pl.BlockSpec((tb, D_MODEL), lambda i: (i, 0)),
                  pl.BlockSpec((1, D_MODEL), lambda i: (0, 0)),
                  pl.BlockSpec((D_MODEL, LANES), lambda i: (0, 0))],
        out_specs=[pl.BlockSpec((tb, D_MODEL), lambda i: (i, 0)),
                   pl.BlockSpec((tb, LANES), lambda i: (i, 0))],
        out_shape=[jax.ShapeDtypeStruct((t, D_MODEL), BF16), jax.ShapeDtypeStruct((t, LANES), F32)],
        compiler_params=_params(("arbitrary",)),
    )(x, w, w_ab)


def _norm1_bwd(x, w, dn, dres, dab, w_ab):
    t = x.shape[0]
    tb = _rows(t)

    def body(x_ref, w_ref, dn_ref, dres_ref, dab_ref, wab_ref, dx_ref, dw_ref):
        i = pl.program_id(0)
        g = dn_ref[...] + _dot_nt(dab_ref[...].astype(BF16), wab_ref[...])
        _, vjp = jax.vjp(_rms, x_ref[...], w_ref[...])
        dx, dw = vjp(g)
        dx_ref[...] = dres_ref[...] + dx

        @pl.when(i == 0)
        def _():
            dw_ref[...] = jnp.zeros_like(dw_ref)

        dw_ref[...] += dw

    row = pl.BlockSpec((tb, D_MODEL), lambda i: (i, 0))
    vec = pl.BlockSpec((1, D_MODEL), lambda i: (0, 0))
    return pl.pallas_call(
        body, name="norm1_bwd", grid=(t // tb,),
        in_specs=[row, vec, row, row, pl.BlockSpec((tb, LANES), lambda i: (i, 0)),
                  pl.BlockSpec((D_MODEL, LANES), lambda i: (0, 0))],
        out_specs=[row, vec],
        out_shape=[jax.ShapeDtypeStruct((t, D_MODEL), F32), jax.ShapeDtypeStruct((1, D_MODEL), F32)],
        compiler_params=_params(("arbitrary",)),
    )(x, w, dn, dres, dab, w_ab)


def _conv_fwd(x, w, name):
    t, c = x.shape
    kk = w.shape[0]
    tb, cb = _rows(t), _pick(c, 512)
    per = tb // HALO

    def body(x_ref, halo_ref, w_ref, y_ref, buf):
        i = pl.program_id(0)
        buf[pl.ds(HALO, tb), :] = x_ref[...]
        buf[pl.ds(0, HALO), :] = jnp.where(i == 0, 0.0, halo_ref[...])
        y = w_ref[0:1, :] * buf[pl.ds(HALO - (kk - 1), tb), :]
        for s in range(1, kk):
            y = y + w_ref[s:s + 1, :] * buf[pl.ds(HALO - (kk - 1) + s, tb), :]
        y_ref[...] = y

    return pl.pallas_call(
        body, name=name, grid=(t // tb, c // cb),
        in_specs=[pl.BlockSpec((tb, cb), lambda i, j: (i, j)),
                  pl.BlockSpec((HALO, cb), lambda i, j: (jnp.maximum(i * per - 1, 0), j)),
                  pl.BlockSpec((kk, cb), lambda i, j: (0, j))],
        out_specs=pl.BlockSpec((tb, cb), lambda i, j: (i, j)),
        out_shape=jax.ShapeDtypeStruct((t, c), F32),
        scratch_shapes=[pltpu.VMEM((tb + HALO, cb), F32)],
        compiler_params=_params(("parallel", "parallel")),
    )(x, x, w)


def _conv_bwd(dy, x, w, name, dx_dtype):
    t, c = x.shape
    kk = w.shape[0]
    tb, cb = _rows(t), _pick(c, 512)
    per = tb // HALO
    nblk = t // tb

    def body(dy_ref, after_ref, x_ref, before_ref, w_ref, dx_ref, dw_ref, dbuf, xbuf):
        i = pl.program_id(1)
        dy = dy_ref[...]
        dbuf[pl.ds(0, tb), :] = dy
        dbuf[pl.ds(tb, HALO), :] = jnp.where(i == nblk - 1, 0.0, after_ref[...])
        xbuf[pl.ds(HALO, tb), :] = x_ref[...]
        xbuf[pl.ds(0, HALO), :] = jnp.where(i == 0, 0.0, before_ref[...])
        dx = w_ref[0:1, :] * dbuf[pl.ds(kk - 1, tb), :]
        for s in range(1, kk):
            dx = dx + w_ref[s:s + 1, :] * dbuf[pl.ds(kk - 1 - s, tb), :]
        dx_ref[...] = dx.astype(dx_dtype)

        @pl.when(i == 0)
        def _():
            dw_ref[...] = jnp.zeros_like(dw_ref)

        for s in range(kk):
            part = jnp.sum(dy * xbuf[pl.ds(HALO - (kk - 1) + s, tb), :], axis=0, keepdims=True)
            dw_ref[s:s + 1, :] += part

    blk = pl.BlockSpec((tb, cb), lambda j, i: (i, j))
    return pl.pallas_call(
        body, name=name, grid=(c // cb, nblk),
        in_specs=[blk,
                  pl.BlockSpec((HALO, cb), lambda j, i: (jnp.minimum((i + 1) * per, t // HALO - 1), j)),
                  blk,
                  pl.BlockSpec((HALO, cb), lambda j, i: (jnp.maximum(i * per - 1, 0), j)),
                  pl.BlockSpec((kk, cb), lambda j, i: (0, j))],
        out_specs=[blk, pl.BlockSpec((HALO, cb), lambda j, i: (0, j))],
        out_shape=[jax.ShapeDtypeStruct((t, c), dx_dtype), jax.ShapeDtypeStruct((HALO, c), F32)],
        scratch_shapes=[pltpu.VMEM((tb + HALO, cb), F32), pltpu.VMEM((tb + HALO, cb), F32)],
        compiler_params=_params(("parallel", "arbitrary")),
    )(dy, dy, x, x, w)


def _dn_prep_fn(c, hab, alog, dtb):
    s = _silu(c)
    heads = []
    for h in range(2 * N_HEADS):
        sh = s[:, h * HEAD_DIM:(h + 1) * HEAD_DIM]
        heads.append(sh * lax.rsqrt(jnp.sum(sh * sh, axis=-1, keepdims=True) + EPS))
    qn = jnp.concatenate(heads[:N_HEADS], axis=1)
    kn = jnp.concatenate(heads[N_HEADS:], axis=1)
    v = s[:, 2 * D_MODEL:]
    lane = lax.broadcasted_iota(jnp.int32, hab.shape, 1)
    g = -jnp.exp(alog) * _softplus(hab + dtb)
    beta = jax.nn.sigmoid(hab)
    gb = jnp.where(lane < N_HEADS, g, jnp.where(lane < 2 * N_HEADS, beta, 0.0))
    return qn, kn, v, gb


def _to_heads(ref, val):
    for h in range(N_HEADS):
        ref[h] = val[:, h * HEAD_DIM:(h + 1) * HEAD_DIM]


def _from_heads(ref):
    return jnp.concatenate([ref[h] for h in range(N_HEADS)], axis=1)


def _dn_prep_fwd(c, hab, alog, dtb):
    t = c.shape[0]
    tb = _rows(t)

    def body(c_ref, hab_ref, alog_ref, dtb_ref, q_ref, k_ref, v_ref, gb_ref):
        qn, kn, v, gb = _dn_prep_fn(c_ref[...], hab_ref[...], alog_ref[...], dtb_ref[...])
        _to_heads(q_ref, qn)
        _to_heads(k_ref, kn)
        _to_heads(v_ref, v)
        gb_ref[...] = gb

    hm = pl.BlockSpec((N_HEADS, tb, HEAD_DIM), lambda i: (0, i, 0))
    nar = pl.BlockSpec((tb, LANES), lambda i: (i, 0))
    vec = pl.BlockSpec((1, LANES), lambda i: (0, 0))
    return pl.pallas_call(
        body, name="dn_prep_fwd", grid=(t // tb,),
        in_specs=[pl.BlockSpec((tb, 3 * D_MODEL), lambda i: (i, 0)), nar, vec, vec],
        out_specs=[hm, hm, hm, nar],
        out_shape=[jax.ShapeDtypeStruct((N_HEADS, t, HEAD_DIM), F32)] * 3 + [jax.ShapeDtypeStruct((t, LANES), F32)],
        compiler_params=_params(("parallel",)),
    )(c, hab, alog, dtb)


def _dn_prep_bwd(c, hab, alog, dtb, dq, dk, dv, dgb):
    t = c.shape[0]
    tb = _rows(t)

    def body(c_ref, hab_ref, alog_ref, dtb_ref, dq_ref, dk_ref, dv_ref, dgb_ref,
             dc_ref, dhab_ref, dalog_ref, ddtb_ref):
        i = pl.program_id(0)
        _, vjp = jax.vjp(_dn_prep_fn, c_ref[...], hab_ref[...], alog_ref[...], dtb_ref[...])
        dc, dhab, dalog, ddtb = vjp((_from_heads(dq_ref), _from_heads(dk_ref), _from_heads(dv_ref), dgb_ref[...]))
        dc_ref[...] = dc
        dhab_ref[...] = dhab

        @pl.when(i == 0)
        def _():
            dalog_ref[...] = jnp.zeros_like(dalog_ref)
            ddtb_ref[...] = jnp.zeros_like(ddtb_ref)

        dalog_ref[...] += dalog
        ddtb_ref[...] += ddtb

    hm = pl.BlockSpec((N_HEADS, tb, HEAD_DIM), lambda i: (0, i, 0))
    wide = pl.BlockSpec((tb, 3 * D_MODEL), lambda i: (i, 0))
    nar = pl.BlockSpec((tb, LANES), lambda i: (i, 0))
    vec = pl.BlockSpec((1, LANES), lambda i: (0, 0))
    return pl.pallas_call(
        body, name="dn_prep_bwd", grid=(t // tb,),
        in_specs=[wide, nar, vec, vec, hm, hm, hm, nar],
        out_specs=[wide, nar, vec, vec],
        out_shape=[jax.ShapeDtypeStruct((t, 3 * D_MODEL), F32), jax.ShapeDtypeStruct((t, LANES), F32),
                   jax.ShapeDtypeStruct((1, LANES), F32), jax.ShapeDtypeStruct((1, LANES), F32)],
        compiler_params=_params(("arbitrary",)),
    )(c, hab, alog, dtb, dq, dk, dv, dgb)


DN_PREC = lax.Precision.HIGH
DN_GROUP = 8


def _bdot(a, b):
    return lax.dot_general(a, b, (((2,), (1,)), ((0,), (0,))), precision=DN_PREC, preferred_element_type=F32)


def _bdot_nt(a, b):
    return lax.dot_general(a, b, (((2,), (2,)), ((0,), (0,))), precision=DN_PREC, preferred_element_type=F32)


def _bdot_tn(a, b):
    return lax.dot_general(a, b, (((1,), (1,)), ((0,), (0,))), precision=DN_PREC, preferred_element_type=F32)


def _unit_lower_inverse(lmat):
    c = lmat.shape[-1]
    ri = lax.broadcasted_iota(jnp.int32, (c, c), 0)
    ci = lax.broadcasted_iota(jnp.int32, (c, c), 1)
    p = -lmat
    tinv = jnp.where(ri == ci, 1.0, 0.0) + p
    for _ in range(max(c.bit_length() - 2, 0)):
        p = _bdot(p, p)
        tinv = tinv + _bdot(tinv, p)
    return tinv


@jax.custom_vjp
def _solve_with(lmat, rhs, tinv):
    return _bdot(tinv, rhs)


def _solve_with_fwd(lmat, rhs, tinv):
    sol = _bdot(tinv, rhs)
    return sol, (sol, tinv)


def _solve_with_bwd(res, dsol):
    sol, tinv = res
    drhs = _bdot_tn(tinv, dsol)
    return -_bdot_nt(drhs, sol), drhs, jnp.zeros_like(tinv)


_solve_with.defvjp(_solve_with_fwd, _solve_with_bwd)


def _dn_local(q, k, v, gcol, grow, bcol, tinv):
    g, c, _ = q.shape
    ri = lax.broadcasted_iota(jnp.int32, (c, c), 0)
    ci = lax.broadcasted_iota(jnp.int32, (c, c), 1)
    lower = ri >= ci
    gc_col = jnp.sum(jnp.where(lower, jnp.broadcast_to(grow, (g, c, c)), 0.0), axis=2, keepdims=True)
    gc_row = jnp.sum(jnp.where(ri <= ci, jnp.broadcast_to(gcol, (g, c, c)), 0.0), axis=1, keepdims=True)
    qs = q * (HEAD_DIM ** -0.5)
    kb = k * bcol
    vb = v * bcol
    decay = jnp.where(lower, jnp.exp(jnp.where(lower, gc_col - gc_row, 0.0)), 0.0)
    lmat = jnp.where(ri > ci, _bdot_nt(kb, k) * decay, 0.0)
    eg = jnp.exp(gc_col)
    rhs = jnp.concatenate([vb, kb * eg], axis=2)
    if tinv is None:
        tinv = _unit_lower_inverse(lmat)
    sol = _solve_with(lmat, rhs, tinv)
    a_qk = jnp.where(lower, _bdot_nt(qs, k) * decay, 0.0)
    g_last = jnp.sum(grow, axis=2, keepdims=True)
    kdec = k * jnp.exp(g_last - gc_col)
    egl = jnp.broadcast_to(jnp.exp(g_last), (g, 1, HEAD_DIM))
    return sol[:, :, :HEAD_DIM], sol[:, :, HEAD_DIM:], a_qk, qs * eg, kdec, egl, tinv


def _dn_seq(u, w, a_qk, qe, kdec, egl, s_in):
    v_new = u - _bdot(w, s_in)
    o = _bdot(qe, s_in) + _bdot(a_qk, v_new)
    return o, s_in * egl + _bdot_tn(kdec, v_new)


def _dn_local_specs(t):
    grp = min(DN_GROUP, t // DN_CHUNK)
    rows = grp * DN_CHUNK
    blk = pl.BlockSpec((1, rows, HEAD_DIM), lambda h, i: (h, i, 0))
    col = pl.BlockSpec((1, grp, DN_CHUNK, 1), lambda h, i: (h, i, 0, 0))
    row = pl.BlockSpec((1, grp, 1, DN_CHUNK), lambda h, i: (h, i, 0, 0))
    sq = pl.BlockSpec((1, grp, DN_CHUNK, DN_CHUNK), lambda h, i: (h, i, 0, 0))
    lane = pl.BlockSpec((1, grp, 1, HEAD_DIM), lambda h, i: (h, i, 0, 0))
    return grp, blk, col, row, sq, lane


def _dn_shapes(t):
    nchunk = t // DN_CHUNK
    big = jax.ShapeDtypeStruct((N_HEADS, t, HEAD_DIM), F32)
    col = jax.ShapeDtypeStruct((N_HEADS, nchunk, DN_CHUNK, 1), F32)
    row = jax.ShapeDtypeStruct((N_HEADS, nchunk, 1, DN_CHUNK), F32)
    sq = jax.ShapeDtypeStruct((N_HEADS, nchunk, DN_CHUNK, DN_CHUNK), F32)
    lane = jax.ShapeDtypeStruct((N_HEADS, nchunk, 1, HEAD_DIM), F32)
    return big, col, row, sq, lane


def _dn_local_fwd(q, k, v, gcol, grow, bcol):
    t = q.shape[1]
    grp, blk, col, row, sq, lane = _dn_local_specs(t)
    big, _, _, sqs, lanes = _dn_shapes(t)

    def body(q_ref, k_ref, v_ref, gc_ref, gr_ref, bc_ref, u_ref, w_ref, a_ref, qe_ref, kd_ref, egl_ref, t_ref):
        split = lambda r: r[0].reshape(grp, DN_CHUNK, HEAD_DIM)
        u, w, a_qk, qe, kdec, egl, tinv = _dn_local(split(q_ref), split(k_ref), split(v_ref), gc_ref[0],
                                                     gr_ref[0], bc_ref[0], None)
        for ref, val in ((u_ref, u), (w_ref, w), (qe_ref, qe), (kd_ref, kdec)):
            ref[0] = val.reshape(grp * DN_CHUNK, HEAD_DIM)
        a_ref[0] = a_qk
        egl_ref[0] = egl
        t_ref[0] = tinv

    return pl.pallas_call(
        body, name="dn_local_fwd", grid=(N_HEADS, t // (grp * DN_CHUNK)),
        in_specs=[blk, blk, blk, col, row, col],
        out_specs=[blk, blk, sq, blk, blk, lane, sq],
        out_shape=[big, big, sqs, big, big, lanes, sqs],
        compiler_params=_params(("parallel", "parallel")),
    )(q, k, v, gcol, grow, bcol)


def _dn_local_bwd(q, k, v, gcol, grow, bcol, tinv, du, dw, da, dqe, dkd, degl):
    t = q.shape[1]
    grp, blk, col, row, sq, lane = _dn_local_specs(t)
    big, cols, rows_, _, _ = _dn_shapes(t)

    def body(q_ref, k_ref, v_ref, gc_ref, gr_ref, bc_ref, t_ref, du_ref, dw_ref, da_ref, dqe_ref, dkd_ref,
             degl_ref, dq_ref, dk_ref, dv_ref, dgc_ref, dgr_ref, dbc_ref):
        split = lambda r: r[0].reshape(grp, DN_CHUNK, HEAD_DIM)
        tinv_v = t_ref[0]
        fn = lambda q_, k_, v_, gc_, gr_, bc_: _dn_local(q_, k_, v_, gc_, gr_, bc_, tinv_v)[:6]
        _, vjp = jax.vjp(fn, split(q_ref), split(k_ref), split(v_ref), gc_ref[0], gr_ref[0], bc_ref[0])
        dq, dk, dv, dgc, dgr, dbc = vjp((split(du_ref), split(dw_ref), da_ref[0], split(dqe_ref), split(dkd_ref),
                                         degl_ref[0]))
        for ref, val in ((dq_ref, dq), (dk_ref, dk), (dv_ref, dv)):
            ref[0] = val.reshape(grp * DN_CHUNK, HEAD_DIM)
        dgc_ref[0] = dgc
        dgr_ref[0] = dgr
        dbc_ref[0] = dbc

    return pl.pallas_call(
        body, name="dn_local_bwd", grid=(N_HEADS, t // (grp * DN_CHUNK)),
        in_specs=[blk, blk, blk, col, row, col, sq, blk, blk, sq, blk, blk, lane],
        out_specs=[blk, blk, blk, col, row, col],
        out_shape=[big, big, big, cols, rows_, cols],
        compiler_params=_params(("parallel", "parallel")),
    )(q, k, v, gcol, grow, bcol, tinv, du, dw, da, dqe, dkd, degl)


def _dn_seq_specs(nchunk, rev):
    def idx(n):
        return nchunk - 1 - n if rev else n

    blk = pl.BlockSpec((N_HEADS, DN_CHUNK, HEAD_DIM), lambda n: (0, idx(n), 0))
    sq = pl.BlockSpec((N_HEADS, 1, DN_CHUNK, DN_CHUNK), lambda n: (0, idx(n), 0, 0))
    lane = pl.BlockSpec((N_HEADS, 1, 1, HEAD_DIM), lambda n: (0, idx(n), 0, 0))
    st = pl.BlockSpec((N_HEADS, 1, HEAD_DIM, HEAD_DIM), lambda n: (0, idx(n), 0, 0))
    return blk, sq, lane, st


def _dn_seq_fwd(u, w, a_qk, qe, kdec, egl):
    t = u.shape[1]
    nchunk = t // DN_CHUNK
    blk, sq, lane, st = _dn_seq_specs(nchunk, False)

    def body(u_ref, w_ref, a_ref, qe_ref, kd_ref, egl_ref, o_ref, s_ref, state):
        @pl.when(pl.program_id(0) == 0)
        def _():
            state[...] = jnp.zeros_like(state)

        s_in = state[...]
        s_ref[:, 0] = s_in
        o, s_out = _dn_seq(u_ref[...], w_ref[...], a_ref[:, 0], qe_ref[...], kd_ref[...], egl_ref[:, 0], s_in)
        o_ref[...] = o
        state[...] = s_out

    return pl.pallas_call(
        body, name="dn_seq_fwd", grid=(nchunk,),
        in_specs=[blk, blk, sq, blk, blk, lane],
        out_specs=[blk, st],
        out_shape=[jax.ShapeDtypeStruct((N_HEADS, t, HEAD_DIM), F32),
                   jax.ShapeDtypeStruct((N_HEADS, nchunk, HEAD_DIM, HEAD_DIM), F32)],
        scratch_shapes=[pltpu.VMEM((N_HEADS, HEAD_DIM, HEAD_DIM), F32)],
        compiler_params=_params(("arbitrary",)),
    )(u, w, a_qk, qe, kdec, egl)


def _dn_seq_bwd(u, w, a_qk, qe, kdec, egl, states, do):
    t = u.shape[1]
    nchunk = t // DN_CHUNK
    blk, sq, lane, st = _dn_seq_specs(nchunk, True)
    big, _, _, sqs, lanes = _dn_shapes(t)

    def body(u_ref, w_ref, a_ref, qe_ref, kd_ref, egl_ref, s_ref, do_ref,
             du_ref, dw_ref, da_ref, dqe_ref, dkd_ref, degl_ref, dstate):
        @pl.when(pl.program_id(0) == 0)
        def _():
            dstate[...] = jnp.zeros_like(dstate)

        _, vjp = jax.vjp(_dn_seq, u_ref[...], w_ref[...], a_ref[:, 0], qe_ref[...], kd_ref[...], egl_ref[:, 0],
                         s_ref[:, 0])
        du, dw, da, dqe, dkd, degl, ds = vjp((do_ref[...], dstate[...]))
        du_ref[...] = du
        dw_ref[...] = dw
        da_ref[:, 0] = da
        dqe_ref[...] = dqe
        dkd_ref[...] = dkd
        degl_ref[:, 0] = degl
        dstate[...] = ds

    return pl.pallas_call(
        body, name="dn_seq_bwd", grid=(nchunk,),
        in_specs=[blk, blk, sq, blk, blk, lane, st, blk],
        out_specs=[blk, blk, sq, blk, blk, lane],
        out_shape=[big, big, sqs, big, big, lanes],
        scratch_shapes=[pltpu.VMEM((N_HEADS, HEAD_DIM, HEAD_DIM), F32)],
        compiler_params=_params(("arbitrary",)),
    )(u, w, a_qk, qe, kdec, egl, states, do)


def _dn_post_fn(o, gate, w):
    outs = []
    for h in range(N_HEADS):
        sl = slice(h * HEAD_DIM, (h + 1) * HEAD_DIM)
        outs.append(_rms(o[:, sl], w) * _silu(gate[:, sl]))
    return jnp.concatenate(outs, axis=1)


def _dn_post_fwd(o, gate, w):
    t = gate.shape[0]
    tb = _rows(t)

    def body(o_ref, g_ref, w_ref, y_ref):
        y_ref[...] = _dn_post_fn(_from_heads(o_ref), g_ref[...], w_ref[...]).astype(BF16)

    row = pl.BlockSpec((tb, D_MODEL), lambda i: (i, 0))
    hm = pl.BlockSpec((N_HEADS, tb, HEAD_DIM), lambda i: (0, i, 0))
    return pl.pallas_call(
        body, name="dn_post_fwd", grid=(t // tb,),
        in_specs=[hm, row, pl.BlockSpec((1, HEAD_DIM), lambda i: (0, 0))],
        out_specs=row, out_shape=jax.ShapeDtypeStruct((t, D_MODEL), BF16),
        compiler_params=_params(("parallel",)),
    )(o, gate, w)


def _dn_post_bwd(o, gate, w, dy):
    t = gate.shape[0]
    tb = _rows(t)

    def body(o_ref, g_ref, w_ref, dy_ref, do_ref, dg_ref, dw_ref):
        i = pl.program_id(0)
        _, vjp = jax.vjp(_dn_post_fn, _from_heads(o_ref), g_ref[...], w_ref[...])
        do, dg, dw = vjp(dy_ref[...])
        _to_heads(do_ref, do)
        dg_ref[...] = dg.astype(BF16)

        @pl.when(i == 0)
        def _():
            dw_ref[...] = jnp.zeros_like(dw_ref)

        dw_ref[...] += dw

    row = pl.BlockSpec((tb, D_MODEL), lambda i: (i, 0))
    hm = pl.BlockSpec((N_HEADS, tb, HEAD_DIM), lambda i: (0, i, 0))
    vec = pl.BlockSpec((1, HEAD_DIM), lambda i: (0, 0))
    return pl.pallas_call(
        body, name="dn_post_bwd", grid=(t // tb,),
        in_specs=[hm, row, vec, row],
        out_specs=[hm, row, vec],
        out_shape=[jax.ShapeDtypeStruct((N_HEADS, t, HEAD_DIM), F32), jax.ShapeDtypeStruct((t, D_MODEL), BF16),
                   jax.ShapeDtypeStruct((1, HEAD_DIM), F32)],
        compiler_params=_params(("arbitrary",)),
    )(o, gate, w, dy)


def _split_bf16(x):
    hi = x.astype(BF16)
    lo = (x - hi.astype(F32)).astype(BF16)
    return hi, lo


SB_Q_BLOCK = 512
SB_K_BLOCK = 256


def _sb_logits(q, kb, mask, scale):
    z = _dot_nt(q, kb) * scale
    ls = jnp.minimum(z, 0.0) - jnp.log(1.0 + jnp.exp(-jnp.abs(z)))
    lk = ls - z
    if mask is not None:
        lk = jnp.where(mask, lk, 0.0)
    return ls, lk


def _sb_blocks(t):
    bq = min(SB_Q_BLOCK, t)
    bk = min(SB_K_BLOCK, bq)
    return bq, bk, bq // bk


def _sb_fwd(qkv):
    t = qkv.shape[0]
    bq, bk, nd = _sb_blocks(t)
    scale = HEAD_DIM ** -0.5

    def body(q_ref, k_ref, v_ref, o_ref, tot_ref):
        i = pl.program_id(1)
        q = q_ref[...]
        rj = lax.broadcasted_iota(jnp.int32, (bk, bk), 0)
        cj = lax.broadcasted_iota(jnp.int32, (bk, bk), 1)
        after = (rj > cj).astype(BF16)
        trow = lax.broadcasted_iota(jnp.int32, (bq, bk), 0)
        scol = lax.broadcasted_iota(jnp.int32, (bq, bk), 1)

        def tile(j, run, acc, mask):
            off = pl.multiple_of(j * bk, bk)
            kb = k_ref[pl.ds(off, bk), :]
            vb = v_ref[pl.ds(off, bk), :]
            ls, lk = _sb_logits(q, kb, mask, scale)
            hi, lo = _split_bf16(lk)
            between = _dot(hi, after) + _dot(lo, after) + run
            a = jnp.exp(ls + between)
            if mask is not None:
                a = jnp.where(mask, a, 0.0)
            acc = acc + _dot(a.astype(BF16), vb)
            return run + jnp.sum(lk, axis=1, keepdims=True), acc

        run, acc = jnp.zeros((bq, 1), F32), jnp.zeros((bq, HEAD_DIM), F32)
        for d in reversed(range(nd)):
            run, acc = tile(i * nd + d, run, acc, scol + d * bk < trow)
        run, acc = lax.fori_loop(0, i * nd, lambda it, c: tile(i * nd - 1 - it, c[0], c[1], None), (run, acc))
        o_ref[...] = acc.astype(BF16)
        tot_ref[...] = jnp.broadcast_to(run, (bq, HEAD_DIM))

    qs = pl.BlockSpec((bq, HEAD_DIM), lambda h, i: (i, h))
    ks = pl.BlockSpec((t, HEAD_DIM), lambda h, i: (0, N_HEADS + h))
    vs = pl.BlockSpec((t, HEAD_DIM), lambda h, i: (0, 2 * N_HEADS + h))
    return pl.pallas_call(
        body, name="sb_fwd", grid=(N_HEADS, t // bq),
        in_specs=[qs, ks, vs], out_specs=[qs, qs],
        out_shape=[jax.ShapeDtypeStruct((t, D_MODEL), BF16), jax.ShapeDtypeStruct((t, D_MODEL), F32)],
        compiler_params=_params(("parallel", "arbitrary")),
    )(qkv, qkv, qkv)


def _sb_bwd(qkv, tot, do):
    t = qkv.shape[0]
    bq, bk, nd = _sb_blocks(t)
    scale = HEAD_DIM ** -0.5

    def body(q_ref, k_ref, v_ref, tot_ref, do_ref, dq_ref, dk_ref, dv_ref):
        i = pl.program_id(1)

        @pl.when(i == 0)
        def _():
            dk_ref[...] = jnp.zeros_like(dk_ref)
            dv_ref[...] = jnp.zeros_like(dv_ref)

        q = q_ref[...]
        do = do_ref[...]
        total = tot_ref[:, 0:1]
        rj = lax.broadcasted_iota(jnp.int32, (bk, bk), 0)
        cj = lax.broadcasted_iota(jnp.int32, (bk, bk), 1)
        upto = (rj <= cj).astype(BF16)
        before = (rj < cj).astype(BF16)
        trow = lax.broadcasted_iota(jnp.int32, (bq, bk), 0)
        scol = lax.broadcasted_iota(jnp.int32, (bq, bk), 1)

        def tile(j, run_k, run_e, dq, mask):
            off = pl.multiple_of(j * bk, bk)
            kb = k_ref[pl.ds(off, bk), :]
            vb = v_ref[pl.ds(off, bk), :]
            ls, lk = _sb_logits(q, kb, mask, scale)
            hi, lo = _split_bf16(lk)
            between = total - (_dot(hi, upto) + _dot(lo, upto) + run_k)
            a = jnp.exp(ls + between)
            if mask is not None:
                a = jnp.where(mask, a, 0.0)
            e = a * _dot_nt(do, vb)
            ehi, elo = _split_bf16(e)
            pre = _dot(ehi, before) + _dot(elo, before) + run_e
            sig = jnp.exp(ls)
            dz = e * (1.0 - sig) - pre * sig
            if mask is not None:
                dz = jnp.where(mask, dz, 0.0)
            dz = (dz * scale).astype(BF16)
            dq = dq + _dot(dz, kb)
            dk_ref[pl.ds(off, bk), :] += _dot_tn(dz, q)
            dv_ref[pl.ds(off, bk), :] += _dot_tn(a.astype(BF16), do)
            return (run_k + jnp.sum(lk, axis=1, keepdims=True),
                    run_e + jnp.sum(e, axis=1, keepdims=True), dq)

        zero = jnp.zeros((bq, 1), F32)
        carry = lax.fori_loop(0, i * nd, lambda j, c: tile(j, c[0], c[1], c[2], None),
                              (zero, zero, jnp.zeros((bq, HEAD_DIM), F32)))
        for d in range(nd):
            carry = tile(i * nd + d, *carry, scol + d * bk < trow)
        dq_ref[...] = carry[2]

    qs = pl.BlockSpec((bq, HEAD_DIM), lambda h, i: (i, h))
    ks = pl.BlockSpec((t, HEAD_DIM), lambda h, i: (0, N_HEADS + h))
    vs = pl.BlockSpec((t, HEAD_DIM), lambda h, i: (0, 2 * N_HEADS + h))
    full = pl.BlockSpec((t, HEAD_DIM), lambda h, i: (0, h))
    big = jax.ShapeDtypeStruct((t, D_MODEL), F32)
    return pl.pallas_call(
        body, name="sb_bwd", grid=(N_HEADS, t // bq),
        in_specs=[qs, ks, vs, qs, qs], out_specs=[qs, full, full],
        out_shape=[big, big, big],
        compiler_params=_params(("parallel", "arbitrary")),
    )(qkv, qkv, qkv, tot, do)


def _merge_fwd(o_dn, o_sb, gl, x, wp_dn, wp_sb, w_out, w2):
    t = x.shape[0]
    tb = _rows(t)

    def body(odn_ref, osb_ref, gl_ref, x_ref, wpd_ref, wps_ref, wo_ref, w2_ref,
             pdn_ref, psb_ref, mix_ref, x1_ref, n2_ref):
        pdn = _dot(odn_ref[...], wpd_ref[...])
        psb = _dot(osb_ref[...], wps_ref[...])
        gates = jax.nn.sigmoid(gl_ref[...])
        mixed = (gates[:, :D_MODEL] * pdn + gates[:, D_MODEL:] * psb).astype(BF16)
        x1 = x_ref[...] + _dot(mixed, wo_ref[...])
        pdn_ref[...] = pdn
        psb_ref[...] = psb
        mix_ref[...] = mixed
        x1_ref[...] = x1
        n2_ref[...] = _rms(x1, w2_ref[...]).astype(BF16)

    row = pl.BlockSpec((tb, D_MODEL), lambda i: (i, 0))
    sq = pl.BlockSpec((D_MODEL, D_MODEL), lambda i: (0, 0))
    f = jax.ShapeDtypeStruct((t, D_MODEL), F32)
    b = jax.ShapeDtypeStruct((t, D_MODEL), BF16)
    return pl.pallas_call(
        body, name="merge_fwd", grid=(t // tb,),
        in_specs=[row, row, pl.BlockSpec((tb, 2 * D_MODEL), lambda i: (i, 0)), row, sq, sq, sq,
                  pl.BlockSpec((1, D_MODEL), lambda i: (0, 0))],
        out_specs=[row] * 5, out_shape=[f, f, b, f, b],
        compiler_params=_params(("parallel",)),
    )(o_dn, o_sb, gl, x, wp_dn, wp_sb, w_out, w2)


def _merge_bwd(dx2, dn2, x1, w2, gl, pdn, psb, wp_dn, wp_sb, w_out):
    t = x1.shape[0]
    tb = _rows(t)

    def body(dx2_ref, dn2_ref, x1_ref, w2_ref, gl_ref, pdn_ref, psb_ref, wpd_ref, wps_ref, wo_ref,
             dx1_ref, dw2_ref, dgl_ref, dpdn_ref, dpsb_ref, dodn_ref, dosb_ref):
        i = pl.program_id(0)
        _, vjp = jax.vjp(_rms, x1_ref[...], w2_ref[...])
        dxn, dw2 = vjp(dn2_ref[...])
        dx1 = dx2_ref[...] + dxn
        dx1_ref[...] = dx1

        @pl.when(i == 0)
        def _():
            dw2_ref[...] = jnp.zeros_like(dw2_ref)

        dw2_ref[...] += dw2
        dmix = _dot_nt(dx1.astype(BF16), wo_ref[...])
        gates = jax.nn.sigmoid(gl_ref[...])
        g_dn, g_sb = gates[:, :D_MODEL], gates[:, D_MODEL:]
        dpdn = (dmix * g_dn).astype(BF16)
        dpsb = (dmix * g_sb).astype(BF16)
        dgl_ref[:, :D_MODEL] = (dmix * pdn_ref[...] * g_dn * (1.0 - g_dn)).astype(BF16)
        dgl_ref[:, D_MODEL:] = (dmix * psb_ref[...] * g_sb * (1.0 - g_sb)).astype(BF16)
        dpdn_ref[...] = dpdn
        dpsb_ref[...] = dpsb
        dodn_ref[...] = _dot_nt(dpdn, wpd_ref[...])
        dosb_ref[...] = _dot_nt(dpsb, wps_ref[...]).astype(BF16)

    row = pl.BlockSpec((tb, D_MODEL), lambda i: (i, 0))
    wide = pl.BlockSpec((tb, 2 * D_MODEL), lambda i: (i, 0))
    sq = pl.BlockSpec((D_MODEL, D_MODEL), lambda i: (0, 0))
    vec = pl.BlockSpec((1, D_MODEL), lambda i: (0, 0))
    f = jax.ShapeDtypeStruct((t, D_MODEL), F32)
    b = jax.ShapeDtypeStruct((t, D_MODEL), BF16)
    return pl.pallas_call(
        body, name="merge_bwd", grid=(t // tb,),
        in_specs=[row, row, row, vec, wide, row, row, sq, sq, sq],
        out_specs=[row, vec, wide, row, row, row, row],
        out_shape=[f, jax.ShapeDtypeStruct((1, D_MODEL), F32), jax.ShapeDtypeStruct((t, 2 * D_MODEL), BF16),
                   b, b, f, b],
        compiler_params=_params(("arbitrary",)),
    )(dx2, dn2, x1, w2, gl, pdn, psb, wp_dn, wp_sb, w_out)


def _swiglu_fwd(ug, uu):
    t, c = ug.shape
    tb, cb = _rows(t), _pick(c, 512)

    def body(g_ref, u_ref, a_ref):
        a_ref[...] = (_silu(g_ref[...]) * u_ref[...]).astype(BF16)

    blk = pl.BlockSpec((tb, cb), lambda i, j: (i, j))
    return pl.pallas_call(
        body, name="swiglu_fwd", grid=(t // tb, c // cb), in_specs=[blk, blk], out_specs=blk,
        out_shape=jax.ShapeDtypeStruct((t, c), BF16), compiler_params=_params(("parallel", "parallel")),
    )(ug, uu)


def _swiglu_bwd(ug, uu, da):
    t, c = ug.shape
    tb, cb = _rows(t), _pick(c, 512)

    def body(g_ref, u_ref, da_ref, dg_ref, du_ref):
        _, vjp = jax.vjp(lambda g, u: _silu(g) * u, g_ref[...], u_ref[...])
        dg, du = vjp(da_ref[...])
        dg_ref[...] = dg
        du_ref[...] = du

    blk = pl.BlockSpec((tb, cb), lambda i, j: (i, j))
    f = jax.ShapeDtypeStruct((t, c), F32)
    return pl.pallas_call(
        body, name="swiglu_bwd", grid=(t // tb, c // cb), in_specs=[blk, blk, blk], out_specs=[blk, blk],
        out_shape=[f, f], compiler_params=_params(("parallel", "parallel")),
    )(ug, uu, da)


def _down_loss(a, w_down, x1, wf, target):
    t = x1.shape[0]
    tb = _rows(t)

    def body(a_ref, wd_ref, x1_ref, wf_ref, tgt_ref, dx2_ref, dwf_ref, loss_ref):
        i = pl.program_id(0)
        x2 = x1_ref[...] + _dot(a_ref[...], wd_ref[...])
        y, vjp = jax.vjp(_rms, x2, wf_ref[...])
        err = y - tgt_ref[...]
        dx2, dwf = vjp(err * (1.0 / D_MODEL))
        dx2_ref[...] = dx2
        part = jnp.sum(jnp.sum(err * err, axis=1, keepdims=True), axis=0, keepdims=True) * (0.5 / D_MODEL)

        @pl.when(i == 0)
        def _():
            dwf_ref[...] = jnp.zeros_like(dwf_ref)
            loss_ref[...] = jnp.zeros_like(loss_ref)

        dwf_ref[...] += dwf
        loss_ref[...] += jnp.broadcast_to(part, loss_ref.shape)

    row = pl.BlockSpec((tb, D_MODEL), lambda i: (i, 0))
    vec = pl.BlockSpec((1, D_MODEL), lambda i: (0, 0))
    return pl.pallas_call(
        body, name="down_loss", grid=(t // tb,),
        in_specs=[pl.BlockSpec((tb, D_FF), lambda i: (i, 0)), pl.BlockSpec((D_FF, D_MODEL), lambda i: (0, 0)),
                  row, vec, row],
        out_specs=[row, vec, pl.BlockSpec((1, LANES), lambda i: (0, 0))],
        out_shape=[jax.ShapeDtypeStruct((t, D_MODEL), F32), jax.ShapeDtypeStruct((1, D_MODEL), F32),
                   jax.ShapeDtypeStruct((1, LANES), F32)],
        compiler_params=_params(("arbitrary",)),
    )(a, w_down, x1, wf, target)


def _local_step(x, target, wts):
    t = x.shape[0]
    nchunk = t // DN_CHUNK

    n1, hab = _norm1_fwd(x, wts["norm1"], wts["w_ab"])
    dnqkv = _mm(n1, wts["w_dnqkv"], name="h_dnqkv")
    dngate = _mm(n1, wts["w_dngate"], name="h_dngate")
    sbqkv = _mm(n1, wts["w_sbqkv"], out_dtype=BF16, name="h_sbqkv")
    gl = _mm(n1, wts["w_gl"], name="h_gl")

    cdn = _conv_fwd(dnqkv, wts["dn_conv"], "dn_conv_fwd")
    qn, kn, vv, gb = _dn_prep_fwd(cdn, hab, wts["alog"], wts["dtb"])
    per_head = gb[:, :2 * N_HEADS].T.reshape(2 * N_HEADS, nchunk, DN_CHUNK)
    gcol, bcol = per_head[:N_HEADS, :, :, None], per_head[N_HEADS:, :, :, None]
    grow = per_head[:N_HEADS, :, None, :]
    u_dn, w_dn, a_qk, qe, kdec, egl, tinv = _dn_local_fwd(qn, kn, vv, gcol, grow, bcol)
    o_raw, states = _dn_seq_fwd(u_dn, w_dn, a_qk, qe, kdec, egl)
    o_dn = _dn_post_fwd(o_raw, dngate, wts["dn_norm"])

    o_sb, tot = _sb_fwd(sbqkv)

    pdn, psb, mixed, x1, n2 = _merge_fwd(o_dn, o_sb, gl, x, wts["wp_dn"], wts["wp_sb"], wts["w_out"],
                                         wts["norm2"])
    pre_g = _mm(n2, wts["w_up_g"], name="ffn_up_g")
    pre_u = _mm(n2, wts["w_up_u"], name="ffn_up_u")
    ug = _conv_fwd(pre_g, wts["ffn_conv_g"], "ffn_conv_g_fwd")
    uu = _conv_fwd(pre_u, wts["ffn_conv_u"], "ffn_conv_u_fwd")
    act = _swiglu_fwd(ug, uu)
    dx2, d_normf, loss_part = _down_loss(act, wts["w_down"], x1, wts["normf"], target)

    grads = {"normf": d_normf}
    da = _mm(dx2, wts["w_down"], tb=True, name="d_act")
    grads["w_down"] = _mm(act, dx2, ta=True, name="dw_down")
    dug, duu = _swiglu_bwd(ug, uu, da)
    dpre_g, dcw_g = _conv_bwd(dug, pre_g, wts["ffn_conv_g"], "ffn_conv_g_bwd", BF16)
    dpre_u, dcw_u = _conv_bwd(duu, pre_u, wts["ffn_conv_u"], "ffn_conv_u_bwd", BF16)
    grads["ffn_conv"] = jnp.concatenate([dcw_g[:FFN_CONV], dcw_u[:FFN_CONV]], axis=1)
    dn2 = _mm(dpre_g, wts["w_up_g"], tb=True, name="dn2_g")
    dn2 = _mm(dpre_u, wts["w_up_u"], tb=True, add=dn2, name="dn2_u")
    grads["w_up"] = jnp.concatenate([_mm(n2, dpre_g, ta=True, name="dw_up_g"),
                                     _mm(n2, dpre_u, ta=True, name="dw_up_u")], axis=1)

    dx1, grads["norm2"], dgl, dpdn, dpsb, do_dn, do_sb = _merge_bwd(
        dx2, dn2, x1, wts["norm2"], gl, pdn, psb, wts["wp_dn"], wts["wp_sb"], wts["w_out"])
    grads["w_out"] = _mm(mixed, dx1, ta=True, name="dw_out")
    grads["wp_dn"] = _mm(o_dn, dpdn, ta=True, name="dw_proj_dn")
    grads["wp_sb"] = _mm(o_sb, dpsb, ta=True, name="dw_proj_sb")

    dsq, dsk, dsv = _sb_bwd(sbqkv, tot, do_sb)
    dsbqkv = jnp.concatenate([dsq, dsk, dsv], axis=1).astype(BF16)

    do_raw, ddngate, grads["dn_norm"] = _dn_post_bwd(o_raw, dngate, wts["dn_norm"], do_dn)
    seq_grads = _dn_seq_bwd(u_dn, w_dn, a_qk, qe, kdec, egl, states, do_raw)
    dqn, dkn, dvv, dgcol, dgrow, dbcol = _dn_local_bwd(qn, kn, vv, gcol, grow, bcol, tinv, *seq_grads)
    dg = (dgcol[..., 0] + dgrow[:, :, 0, :]).reshape(N_HEADS, t)
    dgb = jnp.concatenate([dg, dbcol[..., 0].reshape(N_HEADS, t)], axis=0).T
    dgb = jnp.pad(dgb, ((0, 0), (0, LANES - 2 * N_HEADS)))
    dcdn, dhab, grads["alog"], grads["dtb"] = _dn_prep_bwd(cdn, hab, wts["alog"], wts["dtb"], dqn, dkn, dvv, dgb)
    ddnqkv, dcw_dn = _conv_bwd(dcdn, dnqkv, wts["dn_conv"], "dn_conv_bwd", BF16)
    grads["dn_conv"] = dcw_dn[:DN_CONV]

    dn1 = _mm(ddnqkv, wts["w_dnqkv"], tb=True, name="dn1_dnqkv")
    dn1 = _mm(ddngate, wts["w_dngate"], tb=True, add=dn1, name="dn1_dngate")
    dn1 = _mm(dsbqkv, wts["w_sbqkv"], tb=True, add=dn1, name="dn1_sbqkv")
    dn1 = _mm(dgl, wts["w_gl"], tb=True, add=dn1, name="dn1_gl")
    grads["w_dnqkv"] = _mm(n1, ddnqkv, ta=True, name="dw_dnqkv")
    grads["w_dngate"] = _mm(n1, ddngate, ta=True, name="dw_dngate")
    grads["w_sbqkv"] = _mm(n1, dsbqkv, ta=True, name="dw_sbqkv")
    grads["w_gl"] = _mm(n1, dgl, ta=True, name="dw_gl")
    grads["w_ab"] = _mm(n1, dhab, ta=True, name="dw_ab")
    grad_x, grads["norm1"] = _norm1_bwd(x, wts["norm1"], dn1, dx1, dhab, wts["w_ab"])
    return loss_part, grad_x, grads


def _place():
    return lax.axis_index("x"), lax.axis_index("y"), lax.axis_index("c")


def _gather_shards(shard):
    rows, cols = shard.shape
    half = rows // 2

    def body(in_ref, out_ref, send_sems, recv_sems):
        x, y, c = _place()
        me = 2 * x + y
        sibling = (x, y, 1 - c)
        chips = [(1 - x, y), (x, 1 - y), (1 - x, 1 - y)]

        def slab(chip_index, part):
            return out_ref.at[chip_index, pl.ds(part * half, half), :]

        def copy(k, src, dst, to):
            return pltpu.make_async_remote_copy(src_ref=src, dst_ref=dst, send_sem=send_sems.at[k],
                                                recv_sem=recv_sems.at[k], device_id=to, device_id_type=MESH)

        my_half = in_ref.at[pl.ds(c * half, half), :]
        first = [copy(j, my_half, slab(me, c), (px, py, c)) for j, (px, py) in enumerate(chips)]
        for cp in first:
            cp.start()
        passed = []
        for j, (px, py) in enumerate(chips):
            landed = slab(2 * px + py, c)
            copy(j, landed, landed, (px, py, c)).wait_recv()
            fwd = copy(3 + j, landed, landed, sibling)
            fwd.start()
            passed.append(fwd)
        for j, (px, py) in enumerate(chips):
            there = slab(2 * px + py, 1 - c)
            copy(3 + j, there, there, sibling).wait_recv()
        for cp in first + passed:
            cp.wait_send()

    return pl.pallas_call(
        body, name="gather_weights",
        in_specs=[pl.BlockSpec(memory_space=pltpu.HBM)],
        out_specs=pl.BlockSpec(memory_space=pltpu.HBM),
        out_shape=jax.ShapeDtypeStruct((N_CHIPS, rows, cols), shard.dtype),
        scratch_shapes=[pltpu.SemaphoreType.DMA((6,)), pltpu.SemaphoreType.DMA((6,))],
    )(shard)


def _pair_exchange_halves(g):
    nsh, rows, cols = g.shape
    half = rows // 2

    def body(in_ref, out_ref, send_sem, recv_sem):
        x, y, c = _place()
        src = in_ref.at[:, pl.ds((1 - c) * half, half), :]
        cp = pltpu.make_async_remote_copy(src_ref=src, dst_ref=out_ref, send_sem=send_sem, recv_sem=recv_sem,
                                          device_id=(x, y, 1 - c), device_id_type=MESH)
        cp.start()
        cp.wait()

    return pl.pallas_call(
        body, name="grad_pair_exchange",
        in_specs=[pl.BlockSpec(memory_space=pltpu.HBM)],
        out_specs=pl.BlockSpec(memory_space=pltpu.HBM),
        out_shape=jax.ShapeDtypeStruct((nsh, half, cols), g.dtype),
        scratch_shapes=[pltpu.SemaphoreType.DMA, pltpu.SemaphoreType.DMA],
    )(g)


def _pair_add(g, got, c_idx):
    nsh, rows, cols = g.shape
    half = rows // 2
    rb = _pick_rows(half)

    def body(c_ref, g_ref, got_ref, o_ref):
        o_ref[...] = (g_ref[...].astype(F32) + got_ref[...].astype(F32)).astype(BF16)

    nb = half // rb
    grid_spec = pltpu.PrefetchScalarGridSpec(
        num_scalar_prefetch=1, grid=(nsh, nb),
        in_specs=[pl.BlockSpec((1, rb, cols), lambda s, i, c_ref: (s, c_ref[0] * nb + i, 0)),
                  pl.BlockSpec((1, rb, cols), lambda s, i, c_ref: (s, i, 0))],
        out_specs=pl.BlockSpec((1, rb, cols), lambda s, i, c_ref: (s, i, 0)))
    return pl.pallas_call(
        body, name="grad_pair_add", grid_spec=grid_spec,
        out_shape=jax.ShapeDtypeStruct((nsh, half, cols), BF16),
        compiler_params=_params(("parallel", "parallel")),
    )(c_idx, g, got)


def _pick_rows(n, target=512):
    best = 16
    for b in range(16, min(n, target) + 1, 16):
        if n % b == 0:
            best = b
    return best


def _chip_exchange(p):
    nsh, half, cols = p.shape

    def body(in_ref, out_ref, send_sems, recv_sems):
        x, y, c = _place()
        chips = [(1 - x, y), (x, 1 - y), (1 - x, 1 - y)]
        sends = []
        for j, (px, py) in enumerate(chips):
            cp = pltpu.make_async_remote_copy(src_ref=in_ref.at[2 * px + py], dst_ref=out_ref.at[j],
                                              send_sem=send_sems.at[j], recv_sem=recv_sems.at[j],
                                              device_id=(px, py, c), device_id_type=MESH)
            cp.start()
            sends.append(cp)
        for cp in sends:
            cp.wait_recv()
        for cp in sends:
            cp.wait_send()

    return pl.pallas_call(
        body, name="grad_chip_exchange",
        in_specs=[pl.BlockSpec(memory_space=pltpu.HBM)],
        out_specs=pl.BlockSpec(memory_space=pltpu.HBM),
        out_shape=jax.ShapeDtypeStruct((N_CHIPS - 1, half, cols), p.dtype),
        scratch_shapes=[pltpu.SemaphoreType.DMA((3,)), pltpu.SemaphoreType.DMA((3,))],
    )(p)


def _sum_partials(p, got, chip_idx):
    nsh, half, cols = got.shape
    rb = _pick_rows(half)

    def body(me_ref, p_ref, got_ref, o_ref):
        acc = p_ref[0].astype(F32)
        for s in range(nsh):
            acc = acc + got_ref[s].astype(F32)
        o_ref[...] = acc

    grid_spec = pltpu.PrefetchScalarGridSpec(
        num_scalar_prefetch=1, grid=(half // rb,),
        in_specs=[pl.BlockSpec((1, rb, cols), lambda i, me_ref: (me_ref[0], i, 0)),
                  pl.BlockSpec((nsh, rb, cols), lambda i, me_ref: (0, i, 0))],
        out_specs=pl.BlockSpec((rb, cols), lambda i, me_ref: (i, 0)))
    return pl.pallas_call(
        body, name="grad_sum_chips", grid_spec=grid_spec,
        out_shape=jax.ShapeDtypeStruct((half, cols), F32),
        compiler_params=_params(("parallel",)),
    )(chip_idx, p, got)


def _pair_share(r):
    half, cols = r.shape

    def body(in_ref, out_ref, send_sem, recv_sem):
        x, y, c = _place()
        cp = pltpu.make_async_remote_copy(src_ref=in_ref, dst_ref=out_ref, send_sem=send_sem,
                                          recv_sem=recv_sem, device_id=(x, y, 1 - c), device_id_type=MESH)
        cp.start()
        cp.wait()

    return pl.pallas_call(
        body, name="grad_pair_share",
        in_specs=[pl.BlockSpec(memory_space=pltpu.HBM)],
        out_specs=pl.BlockSpec(memory_space=pltpu.HBM),
        out_shape=jax.ShapeDtypeStruct((half, cols), r.dtype),
        scratch_shapes=[pltpu.SemaphoreType.DMA, pltpu.SemaphoreType.DMA],
    )(r)


def _small_allreduce(v):
    rows, cols = v.shape
    ndev = 8

    def body(in_ref, out_ref, slots, send_sems, recv_sems):
        x, y, c = _place()
        me = 4 * x + 2 * y + c
        slots[me] = in_ref[...]
        sends = []
        for k in range(1, ndev):
            peer = (x ^ (k >> 2), y ^ ((k >> 1) & 1), c ^ (k & 1))
            cp = pltpu.make_async_remote_copy(src_ref=in_ref, dst_ref=slots.at[me], send_sem=send_sems.at[k - 1],
                                              recv_sem=recv_sems.at[k - 1], device_id=peer, device_id_type=MESH)
            cp.start()
            sends.append(cp)
        for k in range(1, ndev):
            there = slots.at[me ^ k]
            pltpu.make_async_remote_copy(src_ref=there, dst_ref=there, send_sem=send_sems.at[k - 1],
                                         recv_sem=recv_sems.at[k - 1], device_id=(x, y, c),
                                         device_id_type=MESH).wait_recv()
        for cp in sends:
            cp.wait_send()
        acc = slots[0]
        for s in range(1, ndev):
            acc = acc + slots[s]
        out_ref[...] = acc

    return pl.pallas_call(
        body, name="small_allreduce",
        in_specs=[pl.BlockSpec(memory_space=pltpu.VMEM)],
        out_specs=pl.BlockSpec(memory_space=pltpu.VMEM),
        out_shape=jax.ShapeDtypeStruct((rows, cols), F32),
        scratch_shapes=[pltpu.VMEM((ndev, rows, cols), F32), pltpu.SemaphoreType.DMA((ndev - 1,)),
                        pltpu.SemaphoreType.DMA((ndev - 1,))],
    )(v)


def _adamw(w, g, m, v, name):
    r, c = w.shape
    rb = r if r <= 128 else _pick_rows_8(r, 128)
    c1 = 1.0 - ADAM_B1 ** ADAM_STEP
    c2 = 1.0 - ADAM_B2 ** ADAM_STEP

    def body(w_ref, g_ref, m_ref, v_ref, d_ref, nm_ref, nv_ref):
        gg = g_ref[...]
        nm = ADAM_B1 * m_ref[...] + (1.0 - ADAM_B1) * gg
        nv = ADAM_B2 * v_ref[...] + (1.0 - ADAM_B2) * (gg * gg)
        d_ref[...] = -ADAM_LR * ((nm / c1) / (jnp.sqrt(nv / c2) + ADAM_EPS) + ADAM_WD * w_ref[...])
        nm_ref[...] = nm
        nv_ref[...] = nv

    blk = pl.BlockSpec((rb, c), lambda i: (i, 0))
    shp = jax.ShapeDtypeStruct((r, c), F32)
    return pl.pallas_call(
        body, name=name, grid=(r // rb,), in_specs=[blk] * 4, out_specs=[blk] * 3, out_shape=[shp] * 3,
        compiler_params=_params(("parallel",)),
    )(w, g, m, v)


def _pick_rows_8(n, target):
    best = n
    for b in range(8, min(n, target) + 1, 8):
        if n % b == 0:
            best = b
    return best


W_IN_COLS = 2308
W_UP_COLS = 1408
W_DOWN_ROWS = 704
DN_CONV_COLS = 768
FFN_CONV_COLS = 1408
PROJ_ROWS = 256
ROW_TILE = 16
SEG = [("w_in", W_IN_COLS), ("wp_dn", PROJ_ROWS), ("wp_sb", PROJ_ROWS), ("w_out", PROJ_ROWS),
       ("w_up", W_UP_COLS), ("w_down", W_DOWN_ROWS), ("dn_conv", ROW_TILE), ("ffn_conv", ROW_TILE)]


def _seg_offsets():
    offs, at = {}, 0
    for nm, n in SEG:
        offs[nm] = (at, n)
        at += -(-n // ROW_TILE) * ROW_TILE
    assert at <= PACK_ROWS and PACK_ROWS % (2 * ROW_TILE) == 0
    return offs, at


PACK_OFFS, PACK_USED = _seg_offsets()


def _tile_rows(a, axis):
    n = a.shape[axis]
    pad = [(0, 0)] * a.ndim
    pad[axis] = (0, -(-n // ROW_TILE) * ROW_TILE - n)
    return jnp.pad(a, pad)


def _flat_rows(a, nrows):
    flat = a.reshape(-1)
    return jnp.pad(flat, (0, nrows * D_MODEL - flat.shape[0])).reshape(nrows, D_MODEL)


def _pack_weight_shard(w_in, wp_dn, wp_sb, w_out, w_up, w_down, dn_conv, ffn_conv):
    parts = [w_in.astype(BF16).reshape(W_IN_COLS, D_MODEL), wp_dn.astype(BF16), wp_sb.astype(BF16),
             w_out.astype(BF16), w_up.astype(BF16).reshape(W_UP_COLS, D_MODEL), w_down.astype(BF16),
             _flat_rows(lax.bitcast_convert_type(dn_conv, BF16), ROW_TILE),
             _flat_rows(lax.bitcast_convert_type(ffn_conv, BF16), ROW_TILE),
             jnp.zeros((PACK_ROWS - PACK_USED, D_MODEL), BF16)]
    return jnp.concatenate([_tile_rows(p, 0) for p in parts], axis=0)


def _unpack_weights(g):
    def seg(nm):
        at, n = PACK_OFFS[nm]
        return g[:, at:at + n, :]

    def cols(nm, ncols):
        return seg(nm).reshape(N_CHIPS, D_MODEL, ncols).transpose(1, 0, 2).reshape(D_MODEL, N_CHIPS * ncols)

    def f32_rows(nm, k, ncols):
        raw = seg(nm).reshape(N_CHIPS, -1)[:, :2 * k * ncols].reshape(N_CHIPS, k * ncols, 2)
        vals = lax.bitcast_convert_type(raw, F32).reshape(N_CHIPS, k, ncols)
        return vals.transpose(1, 0, 2).reshape(k, N_CHIPS * ncols)

    w_in = cols("w_in", W_IN_COLS)
    w_up = cols("w_up", W_UP_COLS)
    ffn_conv = f32_rows("ffn_conv", FFN_CONV, FFN_CONV_COLS)
    q_end, a_end, g_end, s_end = 3 * D_MODEL, 3 * D_MODEL + 2 * N_HEADS, 4 * D_MODEL + 2 * N_HEADS, 7 * D_MODEL + 2 * N_HEADS
    return {
        "w_dnqkv": w_in[:, :q_end],
        "w_ab": jnp.pad(w_in[:, q_end:a_end], ((0, 0), (0, LANES - 2 * N_HEADS))),
        "w_dngate": w_in[:, a_end:g_end],
        "w_sbqkv": w_in[:, g_end:s_end],
        "w_gl": w_in[:, s_end:],
        "wp_dn": seg("wp_dn").reshape(D_MODEL, D_MODEL),
        "wp_sb": seg("wp_sb").reshape(D_MODEL, D_MODEL),
        "w_out": seg("w_out").reshape(D_MODEL, D_MODEL),
        "w_up_g": w_up[:, :D_FF], "w_up_u": w_up[:, D_FF:],
        "w_down": seg("w_down").reshape(D_FF, D_MODEL),
        "dn_conv": f32_rows("dn_conv", DN_CONV, DN_CONV_COLS),
        "ffn_conv_g": ffn_conv[:, :D_FF], "ffn_conv_u": ffn_conv[:, D_FF:],
    }


def _pack_grads(gr):
    w_in = jnp.concatenate([gr["w_dnqkv"], gr["w_ab"][:, :2 * N_HEADS], gr["w_dngate"], gr["w_sbqkv"], gr["w_gl"]],
                           axis=1)

    def cols(a, ncols):
        return a.reshape(a.shape[0], N_CHIPS, ncols).transpose(1, 0, 2)

    def rows(a, nrows):
        return a.reshape(N_CHIPS, nrows, a.shape[1])

    def flat(a, nrows):
        a = a.reshape(N_CHIPS, -1)
        return jnp.pad(a, ((0, 0), (0, nrows * D_MODEL - a.shape[1]))).reshape(N_CHIPS, nrows, D_MODEL)

    parts = [cols(w_in, W_IN_COLS).reshape(N_CHIPS, W_IN_COLS, D_MODEL),
             rows(gr["wp_dn"], PROJ_ROWS), rows(gr["wp_sb"], PROJ_ROWS), rows(gr["w_out"], PROJ_ROWS),
             cols(gr["w_up"], W_UP_COLS).reshape(N_CHIPS, W_UP_COLS, D_MODEL),
             rows(gr["w_down"], W_DOWN_ROWS),
             flat(cols(gr["dn_conv"], DN_CONV_COLS), ROW_TILE), flat(cols(gr["ffn_conv"], FFN_CONV_COLS), ROW_TILE),
             jnp.zeros((N_CHIPS, PACK_ROWS - PACK_USED, D_MODEL), F32)]
    return jnp.concatenate([_tile_rows(p, 1) for p in parts], axis=1).astype(BF16)


def _unpack_grad_shard(r):
    def seg(nm):
        at, n = PACK_OFFS[nm]
        return r[at:at + n, :]

    return {
        "w_in": seg("w_in").reshape(D_MODEL, W_IN_COLS),
        "wp_dn": seg("wp_dn"), "wp_sb": seg("wp_sb"), "w_out": seg("w_out"),
        "w_up": seg("w_up").reshape(D_MODEL, W_UP_COLS),
        "w_down": seg("w_down"),
        "dn_conv": seg("dn_conv").reshape(-1)[:DN_CONV * DN_CONV_COLS].reshape(DN_CONV, DN_CONV_COLS),
        "ffn_conv": seg("ffn_conv").reshape(-1)[:FFN_CONV * FFN_CONV_COLS].reshape(FFN_CONV, FFN_CONV_COLS),
    }


def _lane_row(v):
    return jnp.pad(v.reshape(1, -1), ((0, 0), (0, LANES - v.size)))


def kernel(x, norm1_w, w_in, dn_conv_w, dn_A_log, dn_dt_bias, dn_norm_w, w_proj_dn, w_proj_sb, w_out, norm2_w, ffn_w_up, ffn_conv_w, ffn_w_down, norm_f_w, loss_target, m_norm1_w, m_w_in, m_dn_conv_w, m_dn_A_log, m_dn_dt_bias, m_dn_norm_w, m_w_proj_dn, m_w_proj_sb, m_w_out, m_norm2_w, m_ffn_w_up, m_ffn_conv_w, m_ffn_w_down, m_norm_f_w, v_norm1_w, v_w_in, v_dn_conv_w, v_dn_A_log, v_dn_dt_bias, v_dn_norm_w, v_w_proj_dn, v_w_proj_sb, v_w_out, v_norm2_w, v_ffn_w_up, v_ffn_conv_w, v_ffn_w_down, v_norm_f_w):
    shard = _pack_weight_shard(w_in[0], w_proj_dn[0], w_proj_sb[0], w_out[0], ffn_w_up[0], ffn_w_down[0],
                               dn_conv_w[0], ffn_conv_w[0])
    chip_idx = (2 * lax.axis_index("x") + lax.axis_index("y")).astype(jnp.int32)
    gathered = lax.dynamic_update_slice(_gather_shards(shard), shard[None], (chip_idx, 0, 0))
    wts = _unpack_weights(gathered)
    wts.update(norm1=norm1_w, norm2=norm2_w, normf=norm_f_w.reshape(1, D_MODEL), dn_norm=dn_norm_w,
               alog=_lane_row(dn_A_log), dtb=_lane_row(dn_dt_bias))

    loss_part, grad_x, gr = _local_step(x[0], loss_target[0], wts)

    c_idx = lax.axis_index("c").astype(jnp.int32).reshape(1)
    packed = _pack_grads(gr)
    partial_sum = _pair_add(packed, _pair_exchange_halves(packed), c_idx)
    reduced_half = _sum_partials(partial_sum, _chip_exchange(partial_sum), chip_idx.reshape(1))
    other_half = _pair_share(reduced_half)
    is_south = lax.axis_index("c") == 0
    gsh = _unpack_grad_shard(jnp.concatenate([jnp.where(is_south, reduced_half, other_half),
                                              jnp.where(is_south, other_half, reduced_half)], axis=0))

    tail = jnp.concatenate([gr["dn_norm"], gr["alog"][:, :N_HEADS], gr["dtb"][:, :N_HEADS], loss_part[:, :1]], axis=1)
    small = jnp.concatenate([gr["norm1"], gr["norm2"], gr["normf"],
                             jnp.pad(tail, ((0, 0), (0, D_MODEL - tail.shape[1]))),
                             jnp.zeros((SMALL_ROWS - 4, D_MODEL), F32)], axis=0)
    small = _small_allreduce(small)
    at = HEAD_DIM
    g_small = {"norm1_w": small[0:1], "norm2_w": small[1:2], "norm_f_w": small[2],
               "dn_norm_w": small[3:4, :at], "dn_A_log": small[3:4, at:at + N_HEADS],
               "dn_dt_bias": small[3:4, at + N_HEADS:at + 2 * N_HEADS]}
    loss = small[3, at + 2 * N_HEADS]

    big = {"w_in": (w_in, m_w_in, v_w_in, gsh["w_in"]), "dn_conv_w": (dn_conv_w, m_dn_conv_w, v_dn_conv_w, gsh["dn_conv"]),
           "w_proj_dn": (w_proj_dn, m_w_proj_dn, v_w_proj_dn, gsh["wp_dn"]),
           "w_proj_sb": (w_proj_sb, m_w_proj_sb, v_w_proj_sb, gsh["wp_sb"]),
           "w_out": (w_out, m_w_out, v_w_out, gsh["w_out"]),
           "ffn_w_up": (ffn_w_up, m_ffn_w_up, v_ffn_w_up, gsh["w_up"]),
           "ffn_conv_w": (ffn_conv_w, m_ffn_conv_w, v_ffn_conv_w, gsh["ffn_conv"]),
           "ffn_w_down": (ffn_w_down, m_ffn_w_down, v_ffn_w_down, gsh["w_down"])}
    res = {}
    for nm, (w, m, v, g) in big.items():
        d, nm_, nv_ = _adamw(w[0], g, m[0], v[0], "adamw_" + nm)
        res[nm] = (g[None], d[None], nm_[None], nv_[None])

    names = ["norm1_w", "norm2_w", "norm_f_w", "dn_norm_w", "dn_A_log", "dn_dt_bias"]
    given = {"norm1_w": (norm1_w, m_norm1_w, v_norm1_w), "norm2_w": (norm2_w, m_norm2_w, v_norm2_w),
             "norm_f_w": (norm_f_w, m_norm_f_w, v_norm_f_w), "dn_norm_w": (dn_norm_w, m_dn_norm_w, v_dn_norm_w),
             "dn_A_log": (dn_A_log, m_dn_A_log, v_dn_A_log), "dn_dt_bias": (dn_dt_bias, m_dn_dt_bias, v_dn_dt_bias)}

    def stack(k, fill):
        rows = [jnp.pad(given[nm][k].reshape(1, -1), ((0, 0), (0, D_MODEL - given[nm][k].size)),
                        constant_values=fill) for nm in names]
        return jnp.concatenate(rows + [jnp.full((SMALL_ROWS - len(names), D_MODEL), fill, F32)], axis=0)

    g_rows = jnp.concatenate(
        [jnp.pad(g_small[nm].reshape(1, -1), ((0, 0), (0, D_MODEL - g_small[nm].size))) for nm in names]
        + [jnp.zeros((SMALL_ROWS - len(names), D_MODEL), F32)], axis=0)
    d_s, m_s, v_s = _adamw(stack(0, 0.0), g_rows, stack(1, 0.0), stack(2, 1.0), "adamw_small")
    for r, nm in enumerate(names):
        shape = given[nm][0].shape
        n = given[nm][0].size
        res[nm] = (g_small[nm].reshape(shape), d_s[r, :n].reshape(shape), m_s[r, :n].reshape(shape),
                   v_s[r, :n].reshape(shape))

    order = ["norm1_w", "w_in", "dn_conv_w", "dn_A_log", "dn_dt_bias", "dn_norm_w", "w_proj_dn", "w_proj_sb",
             "w_out", "norm2_w", "ffn_w_up", "ffn_conv_w", "ffn_w_down", "norm_f_w"]
    outs = [loss, grad_x[None]]
    for k in range(4):
        outs += [res[nm][k] for nm in order]
    return tuple(outs)
```

```python
import functools

import jax
import jax.numpy as jnp
from jax import lax
from jax.experimental import pallas as pl
from jax.experimental.pallas import tpu as pltpu

F32 = jnp.float32
BF16 = jnp.bfloat16
HIGHEST = lax.Precision.HIGHEST
MESH = pl.DeviceIdType.MESH

EPS = 1e-6
D_MODEL = 1024
N_HEADS = 8
HEAD_DIM = 128
DN_CONV = 4
DN_CHUNK = 64
D_FF = 2816
FFN_CONV = 3
ADAM_LR, ADAM_B1, ADAM_B2, ADAM_EPS, ADAM_WD, ADAM_STEP = 0.001, 0.9, 0.999, 1e-08, 0.01, 10

N_CHIPS = 4
LANES = 128
HALO = 8
VMEM_LIMIT = 48 * 1024 * 1024
PACK_ROWS = 5248
SMALL_ROWS = 8


def _params(sem=None):
    return pltpu.CompilerParams(dimension_semantics=sem, vmem_limit_bytes=VMEM_LIMIT)


def _pick(n, target):
    best = None
    for b in range(LANES, min(n, target) + 1, LANES):
        if n % b == 0:
            best = b
    return best or n


ELEMENTWISE_COLS = 1408


def _rows(t, target=256):
    return min(t, target)


def _dot(a, b, precision=None):
    return lax.dot_general(a, b, (((1,), (0,)), ((), ())), precision=precision, preferred_element_type=F32)


def _dot_nt(a, b, precision=None):
    return lax.dot_general(a, b, (((1,), (1,)), ((), ())), precision=precision, preferred_element_type=F32)


def _dot_tn(a, b, precision=None):
    return lax.dot_general(a, b, (((0,), (0,)), ((), ())), precision=precision, preferred_element_type=F32)


def _rms(x, w):
    return x * lax.rsqrt(jnp.mean(x * x, axis=-1, keepdims=True) + EPS) * w


def _silu(x):
    return x * jax.nn.sigmoid(x)


def _softplus(x):
    return jnp.maximum(x, 0.0) + jnp.log(1.0 + jnp.exp(-jnp.abs(x)))


MM_BLOCK = 1408


def _mm(a, b, *, ta=False, tb=False, add=None, out_dtype=F32, name, bm=MM_BLOCK, bn=MM_BLOCK, bk=MM_BLOCK):
    m = a.shape[1] if ta else a.shape[0]
    k = a.shape[0] if ta else a.shape[1]
    n = b.shape[0] if tb else b.shape[1]
    bm, bn, bk = _pick(m, bm), _pick(n, bn), _pick(k, bk)
    nk = k // bk
    dims = (((0 if ta else 1,), (1 if tb else 0,)), ((), ()))

    def body(*refs):
        a_ref, b_ref = refs[:2]
        c_ref = refs[2] if add is not None else None
        o_ref = refs[3] if add is not None else refs[2]
        acc = refs[-1]
        kk = pl.program_id(2)
        part = lax.dot_general(a_ref[...].astype(BF16), b_ref[...].astype(BF16), dims, preferred_element_type=F32)

        def finish(r):
            if add is not None:
                r = r + c_ref[...].astype(F32)
            o_ref[...] = r.astype(out_dtype)

        if nk == 1:
            finish(part)
            return

        @pl.when(kk == 0)
        def _():
            acc[...] = part

        @pl.when(jnp.logical_and(kk > 0, kk < nk - 1))
        def _():
            acc[...] += part

        @pl.when(kk == nk - 1)
        def _():
            finish(acc[...] + part)

    a_spec = (pl.BlockSpec((bk, bm), lambda i, j, kk: (kk, i)) if ta
              else pl.BlockSpec((bm, bk), lambda i, j, kk: (i, kk)))
    b_spec = (pl.BlockSpec((bn, bk), lambda i, j, kk: (j, kk)) if tb
              else pl.BlockSpec((bk, bn), lambda i, j, kk: (kk, j)))
    o_spec = pl.BlockSpec((bm, bn), lambda i, j, kk: (i, j))
    in_specs = [a_spec, b_spec] + ([o_spec] if add is not None else [])
    args = (a, b) + ((add,) if add is not None else ())
    return pl.pallas_call(
        body, name=name, grid=(m // bm, n // bn, nk),
        in_specs=in_specs, out_specs=o_spec,
        out_shape=jax.ShapeDtypeStruct((m, n), out_dtype),
        scratch_shapes=[pltpu.VMEM((bm, bn), F32)] if nk > 1 else [],
        compiler_params=_params(("parallel", "parallel", "arbitrary")),
    )(*args)


def _norm1_fwd(x, w, w_ab):
    t = x.shape[0]
    tb = _rows(t)

    def body(x_ref, w_ref, wab_ref, n_ref, hab_ref):
        n = _rms(x_ref[...], w_ref[...]).astype(BF16)
        n_ref[...] = n
        hab_ref[...] = _dot(n, wab_ref[...])

    return pl.pallas_call(
        body, name="norm1_fwd", grid=(t // tb,),
        in_specs=[pl.BlockSpec((tb, D_MODEL), lambda i: (i, 0)),
                  pl.BlockSpec((1, D_MODEL), lambda i: (0, 0)),
                  pl.BlockSpec((D_MODEL, LANES), lambda i: (0, 0))],
        out_specs=[pl.BlockSpec((tb, D_MODEL), lambda i: (i, 0)),
                   pl.BlockSpec((tb, LANES), lambda i: (i, 0))],
        out_shape=[jax.ShapeDtypeStruct((t, D_MODEL), BF16), jax.ShapeDtypeStruct((t, LANES), F32)],
        compiler_params=_params(("arbitrary",)),
    )(x, w, w_ab)


def _norm1_bwd(x, w, dn, dres, dab, w_ab):
    t = x.shape[0]
    tb = _rows(t)

    def body(x_ref, w_ref, dn_ref, dres_ref, dab_ref, wab_ref, dx_ref, dw_ref):
        i = pl.program_id(0)
        g = dn_ref[...] + _dot_nt(dab_ref[...].astype(BF16), wab_ref[...])
        _, vjp = jax.vjp(_rms, x_ref[...], w_ref[...])
        dx, dw = vjp(g)
        dx_ref[...] = dres_ref[...] + dx

        @pl.when(i == 0)
        def _():
            dw_ref[...] = jnp.zeros_like(dw_ref)

        dw_ref[...] += dw

    row = pl.BlockSpec((tb, D_MODEL), lambda i: (i, 0))
    vec = pl.BlockSpec((1, D_MODEL), lambda i: (0, 0))
    return pl.pallas_call(
        body, name="norm1_bwd", grid=(t // tb,),
        in_specs=[row, vec, row, row, pl.BlockSpec((tb, LANES), lambda i: (i, 0)),
                  pl.BlockSpec((D_MODEL, LANES), lambda i: (0, 0))],
        out_specs=[row, vec],
        out_shape=[jax.ShapeDtypeStruct((t, D_MODEL), F32), jax.ShapeDtypeStruct((1, D_MODEL), F32)],
        compiler_params=_params(("arbitrary",)),
    )(x, w, dn, dres, dab, w_ab)


def _conv_fwd(x, w, name):
    t, c = x.shape
    kk = w.shape[0]
    tb, cb = _rows(t, 512), _pick(c, ELEMENTWISE_COLS)
    per = tb // HALO

    def body(x_ref, halo_ref, w_ref, y_ref, buf):
        i = pl.program_id(0)
        buf[pl.ds(HALO, tb), :] = x_ref[...]
        buf[pl.ds(0, HALO), :] = jnp.where(i == 0, 0.0, halo_ref[...])
        y = w_ref[0:1, :] * buf[pl.ds(HALO - (kk - 1), tb), :]
        for s in range(1, kk):
            y = y + w_ref[s:s + 1, :] * buf[pl.ds(HALO - (kk - 1) + s, tb), :]
        y_ref[...] = y

    return pl.pallas_call(
        body, name=name, grid=(t // tb, c // cb),
        in_specs=[pl.BlockSpec((tb, cb), lambda i, j: (i, j)),
                  pl.BlockSpec((HALO, cb), lambda i, j: (jnp.maximum(i * per - 1, 0), j)),
                  pl.BlockSpec((kk, cb), lambda i, j: (0, j))],
        out_specs=pl.BlockSpec((tb, cb), lambda i, j: (i, j)),
        out_shape=jax.ShapeDtypeStruct((t, c), F32),
        scratch_shapes=[pltpu.VMEM((tb + HALO, cb), F32)],
        compiler_params=_params(("parallel", "parallel")),
    )(x, x, w)


def _conv_bwd(dy, x, w, name, dx_dtype):
    t, c = x.shape
    kk = w.shape[0]
    tb, cb = _rows(t, 512), _pick(c, ELEMENTWISE_COLS)
    per = tb // HALO
    nblk = t // tb

    def body(dy_ref, after_ref, x_ref, before_ref, w_ref, dx_ref, dw_ref, dbuf, xbuf):
        i = pl.program_id(1)
        dy = dy_ref[...]
        dbuf[pl.ds(0, tb), :] = dy
        dbuf[pl.ds(tb, HALO), :] = jnp.where(i == nblk - 1, 0.0, after_ref[...])
        xbuf[pl.ds(HALO, tb), :] = x_ref[...]
        xbuf[pl.ds(0, HALO), :] = jnp.where(i == 0, 0.0, before_ref[...])
        dx = w_ref[0:1, :] * dbuf[pl.ds(kk - 1, tb), :]
        for s in range(1, kk):
            dx = dx + w_ref[s:s + 1, :] * dbuf[pl.ds(kk - 1 - s, tb), :]
        dx_ref[...] = dx.astype(dx_dtype)

        @pl.when(i == 0)
        def _():
            dw_ref[...] = jnp.zeros_like(dw_ref)

        for s in range(kk):
            part = jnp.sum(dy * xbuf[pl.ds(HALO - (kk - 1) + s, tb), :], axis=0, keepdims=True)
            dw_ref[s:s + 1, :] += part

    blk = pl.BlockSpec((tb, cb), lambda j, i: (i, j))
    return pl.pallas_call(
        body, name=name, grid=(c // cb, nblk),
        in_specs=[blk,
                  pl.BlockSpec((HALO, cb), lambda j, i: (jnp.minimum((i + 1) * per, t // HALO - 1), j)),
                  blk,
                  pl.BlockSpec((HALO, cb), lambda j, i: (jnp.maximum(i * per - 1, 0), j)),
                  pl.BlockSpec((kk, cb), lambda j, i: (0, j))],
        out_specs=[blk, pl.BlockSpec((HALO, cb), lambda j, i: (0, j))],
        out_shape=[jax.ShapeDtypeStruct((t, c), dx_dtype), jax.ShapeDtypeStruct((HALO, c), F32)],
        scratch_shapes=[pltpu.VMEM((tb + HALO, cb), F32), pltpu.VMEM((tb + HALO, cb), F32)],
        compiler_params=_params(("parallel", "arbitrary")),
    )(dy, dy, x, x, w)


def _dn_prep_fn(c, hab, alog, dtb):
    s = _silu(c)
    heads = []
    for h in range(2 * N_HEADS):
        sh = s[:, h * HEAD_DIM:(h + 1) * HEAD_DIM]
        heads.append(sh * lax.rsqrt(jnp.sum(sh * sh, axis=-1, keepdims=True) + EPS))
    qn = jnp.concatenate(heads[:N_HEADS], axis=1)
    kn = jnp.concatenate(heads[N_HEADS:], axis=1)
    v = s[:, 2 * D_MODEL:]
    lane = lax.broadcasted_iota(jnp.int32, hab.shape, 1)
    g = -jnp.exp(alog) * _softplus(hab + dtb)
    beta = jax.nn.sigmoid(hab)
    gb = jnp.where(lane < N_HEADS, g, jnp.where(lane < 2 * N_HEADS, beta, 0.0))
    return qn, kn, v, gb


def _to_heads(ref, val):
    for h in range(N_HEADS):
        ref[h] = val[:, h * HEAD_DIM:(h + 1) * HEAD_DIM]


def _from_heads(ref):
    return jnp.concatenate([ref[h] for h in range(N_HEADS)], axis=1)


def _dn_prep_fwd(c, hab, alog, dtb):
    t = c.shape[0]
    tb = _rows(t)

    def body(c_ref, hab_ref, alog_ref, dtb_ref, q_ref, k_ref, v_ref, gb_ref):
        qn, kn, v, gb = _dn_prep_fn(c_ref[...], hab_ref[...], alog_ref[...], dtb_ref[...])
        _to_heads(q_ref, qn)
        _to_heads(k_ref, kn)
        _to_heads(v_ref, v)
        gb_ref[...] = gb

    hm = pl.BlockSpec((N_HEADS, tb, HEAD_DIM), lambda i: (0, i, 0))
    nar = pl.BlockSpec((tb, LANES), lambda i: (i, 0))
    vec = pl.BlockSpec((1, LANES), lambda i: (0, 0))
    return pl.pallas_call(
        body, name="dn_prep_fwd", grid=(t // tb,),
        in_specs=[pl.BlockSpec((tb, 3 * D_MODEL), lambda i: (i, 0)), nar, vec, vec],
        out_specs=[hm, hm, hm, nar],
        out_shape=[jax.ShapeDtypeStruct((N_HEADS, t, HEAD_DIM), F32)] * 3 + [jax.ShapeDtypeStruct((t, LANES), F32)],
        compiler_params=_params(("parallel",)),
    )(c, hab, alog, dtb)


def _dn_prep_bwd(c, hab, alog, dtb, dq, dk, dv, dgb):
    t = c.shape[0]
    tb = _rows(t)

    def body(c_ref, hab_ref, alog_ref, dtb_ref, dq_ref, dk_ref, dv_ref, dgb_ref,
             dc_ref, dhab_ref, dalog_ref, ddtb_ref):
        i = pl.program_id(0)
        _, vjp = jax.vjp(_dn_prep_fn, c_ref[...], hab_ref[...], alog_ref[...], dtb_ref[...])
        dc, dhab, dalog, ddtb = vjp((_from_heads(dq_ref), _from_heads(dk_ref), _from_heads(dv_ref), dgb_ref[...]))
        dc_ref[...] = dc
        dhab_ref[...] = dhab

        @pl.when(i == 0)
        def _():
            dalog_ref[...] = jnp.zeros_like(dalog_ref)
            ddtb_ref[...] = jnp.zeros_like(ddtb_ref)

        dalog_ref[...] += dalog
        ddtb_ref[...] += ddtb

    hm = pl.BlockSpec((N_HEADS, tb, HEAD_DIM), lambda i: (0, i, 0))
    wide = pl.BlockSpec((tb, 3 * D_MODEL), lambda i: (i, 0))
    nar = pl.BlockSpec((tb, LANES), lambda i: (i, 0))
    vec = pl.BlockSpec((1, LANES), lambda i: (0, 0))
    return pl.pallas_call(
        body, name="dn_prep_bwd", grid=(t // tb,),
        in_specs=[wide, nar, vec, vec, hm, hm, hm, nar],
        out_specs=[wide, nar, vec, vec],
        out_shape=[jax.ShapeDtypeStruct((t, 3 * D_MODEL), F32), jax.ShapeDtypeStruct((t, LANES), F32),
                   jax.ShapeDtypeStruct((1, LANES), F32), jax.ShapeDtypeStruct((1, LANES), F32)],
        compiler_params=_params(("arbitrary",)),
    )(c, hab, alog, dtb, dq, dk, dv, dgb)


DN_PREC = lax.Precision.HIGH
DN_GROUP = 8


def _bdot(a, b):
    return lax.dot_general(a, b, (((2,), (1,)), ((0,), (0,))), precision=DN_PREC, preferred_element_type=F32)


def _bdot_nt(a, b):
    return lax.dot_general(a, b, (((2,), (2,)), ((0,), (0,))), precision=DN_PREC, preferred_element_type=F32)


def _bdot_tn(a, b):
    return lax.dot_general(a, b, (((1,), (1,)), ((0,), (0,))), precision=DN_PREC, preferred_element_type=F32)


def _unit_lower_inverse(lmat):
    c = lmat.shape[-1]
    ri = lax.broadcasted_iota(jnp.int32, (c, c), 0)
    ci = lax.broadcasted_iota(jnp.int32, (c, c), 1)
    p = -lmat
    tinv = jnp.where(ri == ci, 1.0, 0.0) + p
    for _ in range(max(c.bit_length() - 2, 0)):
        p = _bdot(p, p)
        tinv = tinv + _bdot(tinv, p)
    return tinv


@jax.custom_vjp
def _solve_with(lmat, rhs, tinv):
    return _bdot(tinv, rhs)


def _solve_with_fwd(lmat, rhs, tinv):
    sol = _bdot(tinv, rhs)
    return sol, (sol, tinv)


def _solve_with_bwd(res, dsol):
    sol, tinv = res
    drhs = _bdot_tn(tinv, dsol)
    return -_bdot_nt(drhs, sol), drhs, jnp.zeros_like(tinv)


_solve_with.defvjp(_solve_with_fwd, _solve_with_bwd)


def _dn_local(q, k, v, gcol, grow, bcol, tinv):
    g, c, _ = q.shape
    ri = lax.broadcasted_iota(jnp.int32, (c, c), 0)
    ci = lax.broadcasted_iota(jnp.int32, (c, c), 1)
    lower = ri >= ci
    gc_col = jnp.sum(jnp.where(lower, jnp.broadcast_to(grow, (g, c, c)), 0.0), axis=2, keepdims=True)
    gc_row = jnp.sum(jnp.where(ri <= ci, jnp.broadcast_to(gcol, (g, c, c)), 0.0), axis=1, keepdims=True)
    qs = q * (HEAD_DIM ** -0.5)
    kb = k * bcol
    vb = v * bcol
    decay = jnp.where(lower, jnp.exp(jnp.where(lower, gc_col - gc_row, 0.0)), 0.0)
    lmat = jnp.where(ri > ci, _bdot_nt(kb, k) * decay, 0.0)
    eg = jnp.exp(gc_col)
    rhs = jnp.concatenate([vb, kb * eg], axis=2)
    if tinv is None:
        tinv = _unit_lower_inverse(lmat)
    sol = _solve_with(lmat, rhs, tinv)
    a_qk = jnp.where(lower, _bdot_nt(qs, k) * decay, 0.0)
    g_last = jnp.sum(grow, axis=2, keepdims=True)
    kdec = k * jnp.exp(g_last - gc_col)
    egl = jnp.broadcast_to(jnp.exp(g_last), (g, 1, HEAD_DIM))
    return sol[:, :, :HEAD_DIM], sol[:, :, HEAD_DIM:], a_qk, qs * eg, kdec, egl, tinv


def _dn_seq(u, w, a_qk, qe, kdec, egl, s_in):
    v_new = u - _bdot(w, s_in)
    o = _bdot(qe, s_in) + _bdot(a_qk, v_new)
    return o, s_in * egl + _bdot_tn(kdec, v_new)


def _dn_local_specs(t):
    grp = min(DN_GROUP, t // DN_CHUNK)
    rows = grp * DN_CHUNK
    blk = pl.BlockSpec((1, rows, HEAD_DIM), lambda h, i: (h, i, 0))
    col = pl.BlockSpec((1, grp, DN_CHUNK, 1), lambda h, i: (h, i, 0, 0))
    row = pl.BlockSpec((1, grp, 1, DN_CHUNK), lambda h, i: (h, i, 0, 0))
    sq = pl.BlockSpec((1, grp, DN_CHUNK, DN_CHUNK), lambda h, i: (h, i, 0, 0))
    lane = pl.BlockSpec((1, grp, 1, HEAD_DIM), lambda h, i: (h, i, 0, 0))
    return grp, blk, col, row, sq, lane


def _dn_shapes(t):
    nchunk = t // DN_CHUNK
    big = jax.ShapeDtypeStruct((N_HEADS, t, HEAD_DIM), F32)
    col = jax.ShapeDtypeStruct((N_HEADS, nchunk, DN_CHUNK, 1), F32)
    row = jax.ShapeDtypeStruct((N_HEADS, nchunk, 1, DN_CHUNK), F32)
    sq = jax.ShapeDtypeStruct((N_HEADS, nchunk, DN_CHUNK, DN_CHUNK), F32)
    lane = jax.ShapeDtypeStruct((N_HEADS, nchunk, 1, HEAD_DIM), F32)
    return big, col, row, sq, lane


def _dn_local_fwd(q, k, v, gcol, grow, bcol):
    t = q.shape[1]
    grp, blk, col, row, sq, lane = _dn_local_specs(t)
    big, _, _, sqs, lanes = _dn_shapes(t)

    def body(q_ref, k_ref, v_ref, gc_ref, gr_ref, bc_ref, u_ref, w_ref, a_ref, qe_ref, kd_ref, egl_ref, t_ref):
        split = lambda r: r[0].reshape(grp, DN_CHUNK, HEAD_DIM)
        u, w, a_qk, qe, kdec, egl, tinv = _dn_local(split(q_ref), split(k_ref), split(v_ref), gc_ref[0],
                                                     gr_ref[0], bc_ref[0], None)
        for ref, val in ((u_ref, u), (w_ref, w), (qe_ref, qe), (kd_ref, kdec)):
            ref[0] = val.reshape(grp * DN_CHUNK, HEAD_DIM)
        a_ref[0] = a_qk
        egl_ref[0] = egl
        t_ref[0] = tinv

    return pl.pallas_call(
        body, name="dn_local_fwd", grid=(N_HEADS, t // (grp * DN_CHUNK)),
        in_specs=[blk, blk, blk, col, row, col],
        out_specs=[blk, blk, sq, blk, blk, lane, sq],
        out_shape=[big, big, sqs, big, big, lanes, sqs],
        compiler_params=_params(("parallel", "parallel")),
    )(q, k, v, gcol, grow, bcol)


def _dn_local_bwd(q, k, v, gcol, grow, bcol, tinv, du, dw, da, dqe, dkd, degl):
    t = q.shape[1]
    grp, blk, col, row, sq, lane = _dn_local_specs(t)
    big, cols, rows_, _, _ = _dn_shapes(t)

    def body(q_ref, k_ref, v_ref, gc_ref, gr_ref, bc_ref, t_ref, du_ref, dw_ref, da_ref, dqe_ref, dkd_ref,
             degl_ref, dq_ref, dk_ref, dv_ref, dgc_ref, dgr_ref, dbc_ref):
        split = lambda r: r[0].reshape(grp, DN_CHUNK, HEAD_DIM)
        tinv_v = t_ref[0]
        fn = lambda q_, k_, v_, gc_, gr_, bc_: _dn_local(q_, k_, v_, gc_, gr_, bc_, tinv_v)[:6]
        _, vjp = jax.vjp(fn, split(q_ref), split(k_ref), split(v_ref), gc_ref[0], gr_ref[0], bc_ref[0])
        dq, dk, dv, dgc, dgr, dbc = vjp((split(du_ref), split(dw_ref), da_ref[0], split(dqe_ref), split(dkd_ref),
                                         degl_ref[0]))
        for ref, val in ((dq_ref, dq), (dk_ref, dk), (dv_ref, dv)):
            ref[0] = val.reshape(grp * DN_CHUNK, HEAD_DIM)
        dgc_ref[0] = dgc
        dgr_ref[0] = dgr
        dbc_ref[0] = dbc

    return pl.pallas_call(
        body, name="dn_local_bwd", grid=(N_HEADS, t // (grp * DN_CHUNK)),
        in_specs=[blk, blk, blk, col, row, col, sq, blk, blk, sq, blk, blk, lane],
        out_specs=[blk, blk, blk, col, row, col],
        out_shape=[big, big, big, cols, rows_, cols],
        compiler_params=_params(("parallel", "parallel")),
    )(q, k, v, gcol, grow, bcol, tinv, du, dw, da, dqe, dkd, degl)


def _dn_seq_specs(nchunk, rev):
    def idx(n):
        return nchunk - 1 - n if rev else n

    blk = pl.BlockSpec((N_HEADS, DN_CHUNK, HEAD_DIM), lambda n: (0, idx(n), 0))
    sq = pl.BlockSpec((N_HEADS, 1, DN_CHUNK, DN_CHUNK), lambda n: (0, idx(n), 0, 0))
    lane = pl.BlockSpec((N_HEADS, 1, 1, HEAD_DIM), lambda n: (0, idx(n), 0, 0))
    st = pl.BlockSpec((N_HEADS, 1, HEAD_DIM, HEAD_DIM), lambda n: (0, idx(n), 0, 0))
    return blk, sq, lane, st


def _dn_seq_fwd(u, w, a_qk, qe, kdec, egl):
    t = u.shape[1]
    nchunk = t // DN_CHUNK
    blk, sq, lane, st = _dn_seq_specs(nchunk, False)

    def body(u_ref, w_ref, a_ref, qe_ref, kd_ref, egl_ref, o_ref, s_ref, state):
        @pl.when(pl.program_id(0) == 0)
        def _():
            state[...] = jnp.zeros_like(state)

        s_in = state[...]
        s_ref[:, 0] = s_in
        o, s_out = _dn_seq(u_ref[...], w_ref[...], a_ref[:, 0], qe_ref[...], kd_ref[...], egl_ref[:, 0], s_in)
        o_ref[...] = o
        state[...] = s_out

    return pl.pallas_call(
        body, name="dn_seq_fwd", grid=(nchunk,),
        in_specs=[blk, blk, sq, blk, blk, lane],
        out_specs=[blk, st],
        out_shape=[jax.ShapeDtypeStruct((N_HEADS, t, HEAD_DIM), F32),
                   jax.ShapeDtypeStruct((N_HEADS, nchunk, HEAD_DIM, HEAD_DIM), F32)],
        scratch_shapes=[pltpu.VMEM((N_HEADS, HEAD_DIM, HEAD_DIM), F32)],
        compiler_params=_params(("arbitrary",)),
    )(u, w, a_qk, qe, kdec, egl)


def _dn_seq_bwd(u, w, a_qk, qe, kdec, egl, states, do):
    t = u.shape[1]
    nchunk = t // DN_CHUNK
    blk, sq, lane, st = _dn_seq_specs(nchunk, True)
    big, _, _, sqs, lanes = _dn_shapes(t)

    def body(u_ref, w_ref, a_ref, qe_ref, kd_ref, egl_ref, s_ref, do_ref,
             du_ref, dw_ref, da_ref, dqe_ref, dkd_ref, degl_ref, dstate):
        @pl.when(pl.program_id(0) == 0)
        def _():
            dstate[...] = jnp.zeros_like(dstate)

        _, vjp = jax.vjp(_dn_seq, u_ref[...], w_ref[...], a_ref[:, 0], qe_ref[...], kd_ref[...], egl_ref[:, 0],
                         s_ref[:, 0])
        du, dw, da, dqe, dkd, degl, ds = vjp((do_ref[...], dstate[...]))
        du_ref[...] = du
        dw_ref[...] = dw
        da_ref[:, 0] = da
        dqe_ref[...] = dqe
        dkd_ref[...] = dkd
        degl_ref[:, 0] = degl
        dstate[...] = ds

    return pl.pallas_call(
        body, name="dn_seq_bwd", grid=(nchunk,),
        in_specs=[blk, blk, sq, blk, blk, lane, st, blk],
        out_specs=[blk, blk, sq, blk, blk, lane],
        out_shape=[big, big, sqs, big, big, lanes],
        scratch_shapes=[pltpu.VMEM((N_HEADS, HEAD_DIM, HEAD_DIM), F32)],
        compiler_params=_params(("arbitrary",)),
    )(u, w, a_qk, qe, kdec, egl, states, do)


def _dn_post_fn(o, gate, w):
    outs = []
    for h in range(N_HEADS):
        sl = slice(h * HEAD_DIM, (h + 1) * HEAD_DIM)
        outs.append(_rms(o[:, sl], w) * _silu(gate[:, sl]))
    return jnp.concatenate(outs, axis=1)


def _dn_post_fwd(o, gate, w):
    t = gate.shape[0]
    tb = _rows(t)

    def body(o_ref, g_ref, w_ref, y_ref):
        y_ref[...] = _dn_post_fn(_from_heads(o_ref), g_ref[...], w_ref[...]).astype(BF16)

    row = pl.BlockSpec((tb, D_MODEL), lambda i: (i, 0))
    hm = pl.BlockSpec((N_HEADS, tb, HEAD_DIM), lambda i: (0, i, 0))
    return pl.pallas_call(
        body, name="dn_post_fwd", grid=(t // tb,),
        in_specs=[hm, row, pl.BlockSpec((1, HEAD_DIM), lambda i: (0, 0))],
        out_specs=row, out_shape=jax.ShapeDtypeStruct((t, D_MODEL), BF16),
        compiler_params=_params(("parallel",)),
    )(o, gate, w)


def _dn_post_bwd(o, gate, w, dy):
    t = gate.shape[0]
    tb = _rows(t)

    def body(o_ref, g_ref, w_ref, dy_ref, do_ref, dg_ref, dw_ref):
        i = pl.program_id(0)
        _, vjp = jax.vjp(_dn_post_fn, _from_heads(o_ref), g_ref[...], w_ref[...])
        do, dg, dw = vjp(dy_ref[...])
        _to_heads(do_ref, do)
        dg_ref[...] = dg.astype(BF16)

        @pl.when(i == 0)
        def _():
            dw_ref[...] = jnp.zeros_like(dw_ref)

        dw_ref[...] += dw

    row = pl.BlockSpec((tb, D_MODEL), lambda i: (i, 0))
    hm = pl.BlockSpec((N_HEADS, tb, HEAD_DIM), lambda i: (0, i, 0))
    vec = pl.BlockSpec((1, HEAD_DIM), lambda i: (0, 0))
    return pl.pallas_call(
        body, name="dn_post_bwd", grid=(t // tb,),
        in_specs=[hm, row, vec, row],
        out_specs=[hm, row, vec],
        out_shape=[jax.ShapeDtypeStruct((N_HEADS, t, HEAD_DIM), F32), jax.ShapeDtypeStruct((t, D_MODEL), BF16),
                   jax.ShapeDtypeStruct((1, HEAD_DIM), F32)],
        compiler_params=_params(("arbitrary",)),
    )(o, gate, w, dy)


def _split_bf16(x):
    hi = x.astype(BF16)
    lo = (x - hi.astype(F32)).astype(BF16)
    return hi, lo


SB_Q_BLOCK = 512
SB_K_BLOCK = 256


def _sb_logits(q, kb, mask, scale):
    z = _dot_nt(q, kb) * scale
    ls = jnp.minimum(z, 0.0) - jnp.log(1.0 + jnp.exp(-jnp.abs(z)))
    lk = ls - z
    if mask is not None:
        lk = jnp.where(mask, lk, 0.0)
    return ls, lk


def _sb_blocks(t):
    bq = min(SB_Q_BLOCK, t)
    bk = min(SB_K_BLOCK, bq)
    return bq, bk, bq // bk


def _sb_fwd(qkv):
    t = qkv.shape[0]
    bq, bk, nd = _sb_blocks(t)
    scale = HEAD_DIM ** -0.5

    def body(q_ref, k_ref, v_ref, o_ref, tot_ref):
        i = pl.program_id(1)
        q = q_ref[...]
        rj = lax.broadcasted_iota(jnp.int32, (bk, bk), 0)
        cj = lax.broadcasted_iota(jnp.int32, (bk, bk), 1)
        after = (rj > cj).astype(BF16)
        trow = lax.broadcasted_iota(jnp.int32, (bq, bk), 0)
        scol = lax.broadcasted_iota(jnp.int32, (bq, bk), 1)

        def tile(j, run, acc, mask):
            off = pl.multiple_of(j * bk, bk)
            kb = k_ref[pl.ds(off, bk), :]
            vb = v_ref[pl.ds(off, bk), :]
            ls, lk = _sb_logits(q, kb, mask, scale)
            hi, lo = _split_bf16(lk)
            between = _dot(hi, after) + _dot(lo, after) + run
            a = jnp.exp(ls + between)
            if mask is not None:
                a = jnp.where(mask, a, 0.0)
            acc = acc + _dot(a.astype(BF16), vb)
            return run + jnp.sum(lk, axis=1, keepdims=True), acc

        run, acc = jnp.zeros((bq, 1), F32), jnp.zeros((bq, HEAD_DIM), F32)
        for d in reversed(range(nd)):
            run, acc = tile(i * nd + d, run, acc, scol + d * bk < trow)
        run, acc = lax.fori_loop(0, i * nd, lambda it, c: tile(i * nd - 1 - it, c[0], c[1], None), (run, acc))
        o_ref[...] = acc.astype(BF16)
        tot_ref[...] = jnp.broadcast_to(run, (bq, HEAD_DIM))

    qs = pl.BlockSpec((bq, HEAD_DIM), lambda h, i: (i, h))
    ks = pl.BlockSpec((t, HEAD_DIM), lambda h, i: (0, N_HEADS + h))
    vs = pl.BlockSpec((t, HEAD_DIM), lambda h, i: (0, 2 * N_HEADS + h))
    return pl.pallas_call(
        body, name="sb_fwd", grid=(N_HEADS, t // bq),
        in_specs=[qs, ks, vs], out_specs=[qs, qs],
        out_shape=[jax.ShapeDtypeStruct((t, D_MODEL), BF16), jax.ShapeDtypeStruct((t, D_MODEL), F32)],
        compiler_params=_params(("parallel", "arbitrary")),
    )(qkv, qkv, qkv)


def _sb_bwd(qkv, tot, do):
    t = qkv.shape[0]
    bq, bk, nd = _sb_blocks(t)
    scale = HEAD_DIM ** -0.5

    def body(q_ref, k_ref, v_ref, tot_ref, do_ref, dq_ref, dk_ref, dv_ref):
        i = pl.program_id(1)

        @pl.when(i == 0)
        def _():
            dk_ref[...] = jnp.zeros_like(dk_ref)
            dv_ref[...] = jnp.zeros_like(dv_ref)

        q = q_ref[...]
        do = do_ref[...]
        total = tot_ref[:, 0:1]
        rj = lax.broadcasted_iota(jnp.int32, (bk, bk), 0)
        cj = lax.broadcasted_iota(jnp.int32, (bk, bk), 1)
        upto = (rj <= cj).astype(BF16)
        before = (rj < cj).astype(BF16)
        trow = lax.broadcasted_iota(jnp.int32, (bq, bk), 0)
        scol = lax.broadcasted_iota(jnp.int32, (bq, bk), 1)

        def tile(j, run_k, run_e, dq, mask):
            off = pl.multiple_of(j * bk, bk)
            kb = k_ref[pl.ds(off, bk), :]
            vb = v_ref[pl.ds(off, bk), :]
            ls, lk = _sb_logits(q, kb, mask, scale)
            hi, lo = _split_bf16(lk)
            between = total - (_dot(hi, upto) + _dot(lo, upto) + run_k)
            a = jnp.exp(ls + between)
            if mask is not None:
                a = jnp.where(mask, a, 0.0)
            e = a * _dot_nt(do, vb)
            ehi, elo = _split_bf16(e)
            pre = _dot(ehi, before) + _dot(elo, before) + run_e
            sig = jnp.exp(ls)
            dz = e * (1.0 - sig) - pre * sig
            if mask is not None:
                dz = jnp.where(mask, dz, 0.0)
            dz = (dz * scale).astype(BF16)
            dq = dq + _dot(dz, kb)
            dk_ref[pl.ds(off, bk), :] += _dot_tn(dz, q)
            dv_ref[pl.ds(off, bk), :] += _dot_tn(a.astype(BF16), do)
            return (run_k + jnp.sum(lk, axis=1, keepdims=True),
                    run_e + jnp.sum(e, axis=1, keepdims=True), dq)

        zero = jnp.zeros((bq, 1), F32)
        carry = lax.fori_loop(0, i * nd, lambda j, c: tile(j, c[0], c[1], c[2], None),
                              (zero, zero, jnp.zeros((bq, HEAD_DIM), F32)))
        for d in range(nd):
            carry = tile(i * nd + d, *carry, scol + d * bk < trow)
        dq_ref[...] = carry[2]

    qs = pl.BlockSpec((bq, HEAD_DIM), lambda h, i: (i, h))
    ks = pl.BlockSpec((t, HEAD_DIM), lambda h, i: (0, N_HEADS + h))
    vs = pl.BlockSpec((t, HEAD_DIM), lambda h, i: (0, 2 * N_HEADS + h))
    full = pl.BlockSpec((t, HEAD_DIM), lambda h, i: (0, h))
    big = jax.ShapeDtypeStruct((t, D_MODEL), F32)
    return pl.pallas_call(
        body, name="sb_bwd", grid=(N_HEADS, t // bq),
        in_specs=[qs, ks, vs, qs, qs], out_specs=[qs, full, full],
        out_shape=[big, big, big],
        compiler_params=_params(("parallel", "arbitrary")),
    )(qkv, qkv, qkv, tot, do)


def _merge_fwd(o_dn, o_sb, gl, x, wp_dn, wp_sb, w_out, w2):
    t = x.shape[0]
    tb = _rows(t)

    def body(odn_ref, osb_ref, gl_ref, x_ref, wpd_ref, wps_ref, wo_ref, w2_ref,
             pdn_ref, psb_ref, mix_ref, x1_ref, n2_ref):
        pdn = _dot(odn_ref[...], wpd_ref[...])
        psb = _dot(osb_ref[...], wps_ref[...])
        gates = jax.nn.sigmoid(gl_ref[...])
        mixed = (gates[:, :D_MODEL] * pdn + gates[:, D_MODEL:] * psb).astype(BF16)
        x1 = x_ref[...] + _dot(mixed, wo_ref[...])
        pdn_ref[...] = pdn
        psb_ref[...] = psb
        mix_ref[...] = mixed
        x1_ref[...] = x1
        n2_ref[...] = _rms(x1, w2_ref[...]).astype(BF16)

    row = pl.BlockSpec((tb, D_MODEL), lambda i: (i, 0))
    sq = pl.BlockSpec((D_MODEL, D_MODEL), lambda i: (0, 0))
    f = jax.ShapeDtypeStruct((t, D_MODEL), F32)
    b = jax.ShapeDtypeStruct((t, D_MODEL), BF16)
    return pl.pallas_call(
        body, name="merge_fwd", grid=(t // tb,),
        in_specs=[row, row, pl.BlockSpec((tb, 2 * D_MODEL), lambda i: (i, 0)), row, sq, sq, sq,
                  pl.BlockSpec((1, D_MODEL), lambda i: (0, 0))],
        out_specs=[row] * 5, out_shape=[f, f, b, f, b],
        compiler_params=_params(("parallel",)),
    )(o_dn, o_sb, gl, x, wp_dn, wp_sb, w_out, w2)


def _merge_bwd(dx2, dn2, x1, w2, gl, pdn, psb, wp_dn, wp_sb, w_out):
    t = x1.shape[0]
    tb = _rows(t)

    def body(dx2_ref, dn2_ref, x1_ref, w2_ref, gl_ref, pdn_ref, psb_ref, wpd_ref, wps_ref, wo_ref,
             dx1_ref, dw2_ref, dgl_ref, dpdn_ref, dpsb_ref, dodn_ref, dosb_ref):
        i = pl.program_id(0)
        _, vjp = jax.vjp(_rms, x1_ref[...], w2_ref[...])
        dxn, dw2 = vjp(dn2_ref[...])
        dx1 = dx2_ref[...] + dxn
        dx1_ref[...] = dx1

        @pl.when(i == 0)
        def _():
            dw2_ref[...] = jnp.zeros_like(dw2_ref)

        dw2_ref[...] += dw2
        dmix = _dot_nt(dx1.astype(BF16), wo_ref[...])
        gates = jax.nn.sigmoid(gl_ref[...])
        g_dn, g_sb = gates[:, :D_MODEL], gates[:, D_MODEL:]
        dpdn = (dmix * g_dn).astype(BF16)
        dpsb = (dmix * g_sb).astype(BF16)
        dgl_ref[:, :D_MODEL] = (dmix * pdn_ref[...] * g_dn * (1.0 - g_dn)).astype(BF16)
        dgl_ref[:, D_MODEL:] = (dmix * psb_ref[...] * g_sb * (1.0 - g_sb)).astype(BF16)
        dpdn_ref[...] = dpdn
        dpsb_ref[...] = dpsb
        dodn_ref[...] = _dot_nt(dpdn, wpd_ref[...])
        dosb_ref[...] = _dot_nt(dpsb, wps_ref[...]).astype(BF16)

    row = pl.BlockSpec((tb, D_MODEL), lambda i: (i, 0))
    wide = pl.BlockSpec((tb, 2 * D_MODEL), lambda i: (i, 0))
    sq = pl.BlockSpec((D_MODEL, D_MODEL), lambda i: (0, 0))
    vec = pl.BlockSpec((1, D_MODEL), lambda i: (0, 0))
    f = jax.ShapeDtypeStruct((t, D_MODEL), F32)
    b = jax.ShapeDtypeStruct((t, D_MODEL), BF16)
    return pl.pallas_call(
        body, name="merge_bwd", grid=(t // tb,),
        in_specs=[row, row, row, vec, wide, row, row, sq, sq, sq],
        out_specs=[row, vec, wide, row, row, row, row],
        out_shape=[f, jax.ShapeDtypeStruct((1, D_MODEL), F32), jax.ShapeDtypeStruct((t, 2 * D_MODEL), BF16),
                   b, b, f, b],
        compiler_params=_params(("arbitrary",)),
    )(dx2, dn2, x1, w2, gl, pdn, psb, wp_dn, wp_sb, w_out)


def _swiglu_fwd(ug, uu):
    t, c = ug.shape
    tb, cb = _rows(t, 512), _pick(c, ELEMENTWISE_COLS)

    def body(g_ref, u_ref, a_ref):
        a_ref[...] = (_silu(g_ref[...]) * u_ref[...]).astype(BF16)

    blk = pl.BlockSpec((tb, cb), lambda i, j: (i, j))
    return pl.pallas_call(
        body, name="swiglu_fwd", grid=(t // tb, c // cb), in_specs=[blk, blk], out_specs=blk,
        out_shape=jax.ShapeDtypeStruct((t, c), BF16), compiler_params=_params(("parallel", "parallel")),
    )(ug, uu)


def _swiglu_bwd(ug, uu, da):
    t, c = ug.shape
    tb, cb = _rows(t, 512), _pick(c, ELEMENTWISE_COLS)

    def body(g_ref, u_ref, da_ref, dg_ref, du_ref):
        _, vjp = jax.vjp(lambda g, u: _silu(g) * u, g_ref[...], u_ref[...])
        dg, du = vjp(da_ref[...])
        dg_ref[...] = dg
        du_ref[...] = du

    blk = pl.BlockSpec((tb, cb), lambda i, j: (i, j))
    f = jax.ShapeDtypeStruct((t, c), F32)
    return pl.pallas_call(
        body, name="swiglu_bwd", grid=(t // tb, c // cb), in_specs=[blk, blk, blk], out_specs=[blk, blk],
        out_shape=[f, f], compiler_params=_params(("parallel", "parallel")),
    )(ug, uu, da)


def _down_loss(a, w_down, x1, wf, target):
    t = x1.shape[0]
    tb = _rows(t)

    def body(a_ref, wd_ref, x1_ref, wf_ref, tgt_ref, dx2_ref, dwf_ref, loss_ref):
        i = pl.program_id(0)
        x2 = x1_ref[...] + _dot(a_ref[...], wd_ref[...])
        y, vjp = jax.vjp(_rms, x2, wf_ref[...])
        err = y - tgt_ref[...]
        dx2, dwf = vjp(err * (1.0 / D_MODEL))
        dx2_ref[...] = dx2
        part = jnp.sum(jnp.sum(err * err, axis=1, keepdims=True), axis=0, keepdims=True) * (0.5 / D_MODEL)

        @pl.when(i == 0)
        def _():
            dwf_ref[...] = jnp.zeros_like(dwf_ref)
            loss_ref[...] = jnp.zeros_like(loss_ref)

        dwf_ref[...] += dwf
        loss_ref[...] += jnp.broadcast_to(part, loss_ref.shape)

    row = pl.BlockSpec((tb, D_MODEL), lambda i: (i, 0))
    vec = pl.BlockSpec((1, D_MODEL), lambda i: (0, 0))
    return pl.pallas_call(
        body, name="down_loss", grid=(t // tb,),
        in_specs=[pl.BlockSpec((tb, D_FF), lambda i: (i, 0)), pl.BlockSpec((D_FF, D_MODEL), lambda i: (0, 0)),
                  row, vec, row],
        out_specs=[row, vec, pl.BlockSpec((1, LANES), lambda i: (0, 0))],
        out_shape=[jax.ShapeDtypeStruct((t, D_MODEL), F32), jax.ShapeDtypeStruct((1, D_MODEL), F32),
                   jax.ShapeDtypeStruct((1, LANES), F32)],
        compiler_params=_params(("arbitrary",)),
    )(a, w_down, x1, wf, target)


def _local_step(x, target, wts):
    t = x.shape[0]
    nchunk = t // DN_CHUNK

    n1, hab = _norm1_fwd(x, wts["norm1"], wts["w_ab"])
    dnqkv = _mm(n1, wts["w_dnqkv"], name="h_dnqkv")
    dngate = _mm(n1, wts["w_dngate"], name="h_dngate")
    sbqkv = _mm(n1, wts["w_sbqkv"], out_dtype=BF16, name="h_sbqkv")
    gl = _mm(n1, wts["w_gl"], name="h_gl")

    cdn = _conv_fwd(dnqkv, wts["dn_conv"], "dn_conv_fwd")
    qn, kn, vv, gb = _dn_prep_fwd(cdn, hab, wts["alog"], wts["dtb"])
    per_head = gb[:, :2 * N_HEADS].T.reshape(2 * N_HEADS, nchunk, DN_CHUNK)
    gcol, bcol = per_head[:N_HEADS, :, :, None], per_head[N_HEADS:, :, :, None]
    grow = per_head[:N_HEADS, :, None, :]
    u_dn, w_dn, a_qk, qe, kdec, egl, tinv = _dn_local_fwd(qn, kn, vv, gcol, grow, bcol)
    o_raw, states = _dn_seq_fwd(u_dn, w_dn, a_qk, qe, kdec, egl)
    o_dn = _dn_post_fwd(o_raw, dngate, wts["dn_norm"])

    o_sb, tot = _sb_fwd(sbqkv)

    pdn, psb, mixed, x1, n2 = _merge_fwd(o_dn, o_sb, gl, x, wts["wp_dn"], wts["wp_sb"], wts["w_out"],
                                         wts["norm2"])
    pre_g = _mm(n2, wts["w_up_g"], name="ffn_up_g")
    pre_u = _mm(n2, wts["w_up_u"], name="ffn_up_u")
    ug = _conv_fwd(pre_g, wts["ffn_conv_g"], "ffn_conv_g_fwd")
    uu = _conv_fwd(pre_u, wts["ffn_conv_u"], "ffn_conv_u_fwd")
    act = _swiglu_fwd(ug, uu)
    dx2, d_normf, loss_part = _down_loss(act, wts["w_down"], x1, wts["normf"], target)

    grads = {"normf": d_normf}
    da = _mm(dx2, wts["w_down"], tb=True, name="d_act")
    grads["w_down"] = _mm(act, dx2, ta=True, name="dw_down")
    dug, duu = _swiglu_bwd(ug, uu, da)
    dpre_g, dcw_g = _conv_bwd(dug, pre_g, wts["ffn_conv_g"], "ffn_conv_g_bwd", BF16)
    dpre_u, dcw_u = _conv_bwd(duu, pre_u, wts["ffn_conv_u"], "ffn_conv_u_bwd", BF16)
    grads["ffn_conv"] = jnp.concatenate([dcw_g[:FFN_CONV], dcw_u[:FFN_CONV]], axis=1)
    dn2 = _mm(dpre_g, wts["w_up_g"], tb=True, name="dn2_g")
    dn2 = _mm(dpre_u, wts["w_up_u"], tb=True, add=dn2, name="dn2_u")
    grads["w_up"] = jnp.concatenate([_mm(n2, dpre_g, ta=True, name="dw_up_g"),
                                     _mm(n2, dpre_u, ta=True, name="dw_up_u")], axis=1)

    dx1, grads["norm2"], dgl, dpdn, dpsb, do_dn, do_sb = _merge_bwd(
        dx2, dn2, x1, wts["norm2"], gl, pdn, psb, wts["wp_dn"], wts["wp_sb"], wts["w_out"])
    grads["w_out"] = _mm(mixed, dx1, ta=True, name="dw_out")
    grads["wp_dn"] = _mm(o_dn, dpdn, ta=True, name="dw_proj_dn")
    grads["wp_sb"] = _mm(o_sb, dpsb, ta=True, name="dw_proj_sb")

    dsq, dsk, dsv = _sb_bwd(sbqkv, tot, do_sb)
    dsbqkv = jnp.concatenate([dsq, dsk, dsv], axis=1).astype(BF16)

    do_raw, ddngate, grads["dn_norm"] = _dn_post_bwd(o_raw, dngate, wts["dn_norm"], do_dn)
    seq_grads = _dn_seq_bwd(u_dn, w_dn, a_qk, qe, kdec, egl, states, do_raw)
    dqn, dkn, dvv, dgcol, dgrow, dbcol = _dn_local_bwd(qn, kn, vv, gcol, grow, bcol, tinv, *seq_grads)
    dg = (dgcol[..., 0] + dgrow[:, :, 0, :]).reshape(N_HEADS, t)
    dgb = jnp.concatenate([dg, dbcol[..., 0].reshape(N_HEADS, t)], axis=0).T
    dgb = jnp.pad(dgb, ((0, 0), (0, LANES - 2 * N_HEADS)))
    dcdn, dhab, grads["alog"], grads["dtb"] = _dn_prep_bwd(cdn, hab, wts["alog"], wts["dtb"], dqn, dkn, dvv, dgb)
    ddnqkv, dcw_dn = _conv_bwd(dcdn, dnqkv, wts["dn_conv"], "dn_conv_bwd", BF16)
    grads["dn_conv"] = dcw_dn[:DN_CONV]

    dn1 = _mm(ddnqkv, wts["w_dnqkv"], tb=True, name="dn1_dnqkv")
    dn1 = _mm(ddngate, wts["w_dngate"], tb=True, add=dn1, name="dn1_dngate")
    dn1 = _mm(dsbqkv, wts["w_sbqkv"], tb=True, add=dn1, name="dn1_sbqkv")
    dn1 = _mm(dgl, wts["w_gl"], tb=True, add=dn1, name="dn1_gl")
    grads["w_dnqkv"] = _mm(n1, ddnqkv, ta=True, name="dw_dnqkv")
    grads["w_dngate"] = _mm(n1, ddngate, ta=True, name="dw_dngate")
    grads["w_sbqkv"] = _mm(n1, dsbqkv, ta=True, name="dw_sbqkv")
    grads["w_gl"] = _mm(n1, dgl, ta=True, name="dw_gl")
    grads["w_ab"] = _mm(n1, dhab, ta=True, name="dw_ab")
    grad_x, grads["norm1"] = _norm1_bwd(x, wts["norm1"], dn1, dx1, dhab, wts["w_ab"])
    return loss_part, grad_x, grads


def _place():
    return lax.axis_index("x"), lax.axis_index("y"), lax.axis_index("c")


def _gather_shards(shard):
    rows, cols = shard.shape
    half = rows // 2

    def body(in_ref, out_ref, send_sems, recv_sems):
        x, y, c = _place()
        me = 2 * x + y
        sibling = (x, y, 1 - c)
        chips = [(1 - x, y), (x, 1 - y), (1 - x, 1 - y)]

        def slab(chip_index, part):
            return out_ref.at[chip_index, pl.ds(part * half, half), :]

        def copy(k, src, dst, to):
            return pltpu.make_async_remote_copy(src_ref=src, dst_ref=dst, send_sem=send_sems.at[k],
                                                recv_sem=recv_sems.at[k], device_id=to, device_id_type=MESH)

        my_half = in_ref.at[pl.ds(c * half, half), :]
        first = [copy(j, my_half, slab(me, c), (px, py, c)) for j, (px, py) in enumerate(chips)]
        for cp in first:
            cp.start()
        passed = []
        for j, (px, py) in enumerate(chips):
            landed = slab(2 * px + py, c)
            copy(j, landed, landed, (px, py, c)).wait_recv()
            fwd = copy(3 + j, landed, landed, sibling)
            fwd.start()
            passed.append(fwd)
        for j, (px, py) in enumerate(chips):
            there = slab(2 * px + py, 1 - c)
            copy(3 + j, there, there, sibling).wait_recv()
        for cp in first + passed:
            cp.wait_send()

    return pl.pallas_call(
        body, name="gather_weights",
        in_specs=[pl.BlockSpec(memory_space=pltpu.HBM)],
        out_specs=pl.BlockSpec(memory_space=pltpu.HBM),
        out_shape=jax.ShapeDtypeStruct((N_CHIPS, rows, cols), shard.dtype),
        scratch_shapes=[pltpu.SemaphoreType.DMA((6,)), pltpu.SemaphoreType.DMA((6,))],
    )(shard)


def _pair_exchange_halves(g):
    nsh, rows, cols = g.shape
    half = rows // 2

    def body(in_ref, out_ref, send_sem, recv_sem):
        x, y, c = _place()
        src = in_ref.at[:, pl.ds((1 - c) * half, half), :]
        cp = pltpu.make_async_remote_copy(src_ref=src, dst_ref=out_ref, send_sem=send_sem, recv_sem=recv_sem,
                                          device_id=(x, y, 1 - c), device_id_type=MESH)
        cp.start()
        cp.wait()

    return pl.pallas_call(
        body, name="grad_pair_exchange",
        in_specs=[pl.BlockSpec(memory_space=pltpu.HBM)],
        out_specs=pl.BlockSpec(memory_space=pltpu.HBM),
        out_shape=jax.ShapeDtypeStruct((nsh, half, cols), g.dtype),
        scratch_shapes=[pltpu.SemaphoreType.DMA, pltpu.SemaphoreType.DMA],
    )(g)


def _pair_add(g, got, c_idx):
    nsh, rows, cols = g.shape
    half = rows // 2
    rb = _pick_rows(half)

    def body(c_ref, g_ref, got_ref, o_ref):
        o_ref[...] = (g_ref[...].astype(F32) + got_ref[...].astype(F32)).astype(BF16)

    nb = half // rb
    grid_spec = pltpu.PrefetchScalarGridSpec(
        num_scalar_prefetch=1, grid=(nsh, nb),
        in_specs=[pl.BlockSpec((1, rb, cols), lambda s, i, c_ref: (s, c_ref[0] * nb + i, 0)),
                  pl.BlockSpec((1, rb, cols), lambda s, i, c_ref: (s, i, 0))],
        out_specs=pl.BlockSpec((1, rb, cols), lambda s, i, c_ref: (s, i, 0)))
    return pl.pallas_call(
        body, name="grad_pair_add", grid_spec=grid_spec,
        out_shape=jax.ShapeDtypeStruct((nsh, half, cols), BF16),
        compiler_params=_params(("parallel", "parallel")),
    )(c_idx, g, got)


def _pick_rows(n, target=512):
    best = 16
    for b in range(16, min(n, target) + 1, 16):
        if n % b == 0:
            best = b
    return best


def _chip_exchange(p):
    nsh, half, cols = p.shape

    def body(in_ref, out_ref, send_sems, recv_sems):
        x, y, c = _place()
        chips = [(1 - x, y), (x, 1 - y), (1 - x, 1 - y)]
        sends = []
        for j, (px, py) in enumerate(chips):
            cp = pltpu.make_async_remote_copy(src_ref=in_ref.at[2 * px + py], dst_ref=out_ref.at[j],
                                              send_sem=send_sems.at[j], recv_sem=recv_sems.at[j],
                                              device_id=(px, py, c), device_id_type=MESH)
            cp.start()
            sends.append(cp)
        for cp in sends:
            cp.wait_recv()
        for cp in sends:
            cp.wait_send()

    return pl.pallas_call(
        body, name="grad_chip_exchange",
        in_specs=[pl.BlockSpec(memory_space=pltpu.HBM)],
        out_specs=pl.BlockSpec(memory_space=pltpu.HBM),
        out_shape=jax.ShapeDtypeStruct((N_CHIPS - 1, half, cols), p.dtype),
        scratch_shapes=[pltpu.SemaphoreType.DMA((3,)), pltpu.SemaphoreType.DMA((3,))],
    )(p)


def _sum_partials(p, got, chip_idx):
    nsh, half, cols = got.shape
    rb = _pick_rows(half)

    def body(me_ref, p_ref, got_ref, o_ref):
        acc = p_ref[0].astype(F32)
        for s in range(nsh):
            acc = acc + got_ref[s].astype(F32)
        o_ref[...] = acc

    grid_spec = pltpu.PrefetchScalarGridSpec(
        num_scalar_prefetch=1, grid=(half // rb,),
        in_specs=[pl.BlockSpec((1, rb, cols), lambda i, me_ref: (me_ref[0], i, 0)),
                  pl.BlockSpec((nsh, rb, cols), lambda i, me_ref: (0, i, 0))],
        out_specs=pl.BlockSpec((rb, cols), lambda i, me_ref: (i, 0)))
    return pl.pallas_call(
        body, name="grad_sum_chips", grid_spec=grid_spec,
        out_shape=jax.ShapeDtypeStruct((half, cols), F32),
        compiler_params=_params(("parallel",)),
    )(chip_idx, p, got)


def _pair_share(r):
    half, cols = r.shape

    def body(in_ref, out_ref, send_sem, recv_sem):
        x, y, c = _place()
        cp = pltpu.make_async_remote_copy(src_ref=in_ref, dst_ref=out_ref, send_sem=send_sem,
                                          recv_sem=recv_sem, device_id=(x, y, 1 - c), device_id_type=MESH)
        cp.start()
        cp.wait()

    return pl.pallas_call(
        body, name="grad_pair_share",
        in_specs=[pl.BlockSpec(memory_space=pltpu.HBM)],
        out_specs=pl.BlockSpec(memory_space=pltpu.HBM),
        out_shape=jax.ShapeDtypeStruct((half, cols), r.dtype),
        scratch_shapes=[pltpu.SemaphoreType.DMA, pltpu.SemaphoreType.DMA],
    )(r)


def _small_allreduce(v):
    rows, cols = v.shape
    ndev = 8

    def body(in_ref, out_ref, slots, send_sems, recv_sems):
        x, y, c = _place()
        me = 4 * x + 2 * y + c
        slots[me] = in_ref[...]
        sends = []
        for k in range(1, ndev):
            peer = (x ^ (k >> 2), y ^ ((k >> 1) & 1), c ^ (k & 1))
            cp = pltpu.make_async_remote_copy(src_ref=in_ref, dst_ref=slots.at[me], send_sem=send_sems.at[k - 1],
                                              recv_sem=recv_sems.at[k - 1], device_id=peer, device_id_type=MESH)
            cp.start()
            sends.append(cp)
        for k in range(1, ndev):
            there = slots.at[me ^ k]
            pltpu.make_async_remote_copy(src_ref=there, dst_ref=there, send_sem=send_sems.at[k - 1],
                                         recv_sem=recv_sems.at[k - 1], device_id=(x, y, c),
                                         device_id_type=MESH).wait_recv()
        for cp in sends:
            cp.wait_send()
        acc = slots[0]
        for s in range(1, ndev):
            acc = acc + slots[s]
        out_ref[...] = acc

    return pl.pallas_call(
        body, name="small_allreduce",
        in_specs=[pl.BlockSpec(memory_space=pltpu.VMEM)],
        out_specs=pl.BlockSpec(memory_space=pltpu.VMEM),
        out_shape=jax.ShapeDtypeStruct((rows, cols), F32),
        scratch_shapes=[pltpu.VMEM((ndev, rows, cols), F32), pltpu.SemaphoreType.DMA((ndev - 1,)),
                        pltpu.SemaphoreType.DMA((ndev - 1,))],
    )(v)


def _adamw(w, g, m, v, name):
    r, c = w.shape
    rb = r if r <= 128 else _pick_rows_8(r, 128)
    c1 = 1.0 - ADAM_B1 ** ADAM_STEP
    c2 = 1.0 - ADAM_B2 ** ADAM_STEP

    def body(w_ref, g_ref, m_ref, v_ref, d_ref, nm_ref, nv_ref):
        gg = g_ref[...]
        nm = ADAM_B1 * m_ref[...] + (1.0 - ADAM_B1) * gg
        nv = ADAM_B2 * v_ref[...] + (1.0 - ADAM_B2) * (gg * gg)
        d_ref[...] = -ADAM_LR * ((nm / c1) / (jnp.sqrt(nv / c2) + ADAM_EPS) + ADAM_WD * w_ref[...])
        nm_ref[...] = nm
        nv_ref[...] = nv

    blk = pl.BlockSpec((rb, c), lambda i: (i, 0))
    shp = jax.ShapeDtypeStruct((r, c), F32)
    return pl.pallas_call(
        body, name=name, grid=(r // rb,), in_specs=[blk] * 4, out_specs=[blk] * 3, out_shape=[shp] * 3,
        compiler_params=_params(("parallel",)),
    )(w, g, m, v)


def _pick_rows_8(n, target):
    best = n
    for b in range(8, min(n, target) + 1, 8):
        if n % b == 0:
            best = b
    return best


W_IN_COLS = 2308
W_UP_COLS = 1408
W_DOWN_ROWS = 704
DN_CONV_COLS = 768
FFN_CONV_COLS = 1408
PROJ_ROWS = 256
ROW_TILE = 16
SEG = [("w_in", W_IN_COLS), ("wp_dn", PROJ_ROWS), ("wp_sb", PROJ_ROWS), ("w_out", PROJ_ROWS),
       ("w_up", W_UP_COLS), ("w_down", W_DOWN_ROWS), ("dn_conv", ROW_TILE), ("ffn_conv", ROW_TILE)]


def _seg_offsets():
    offs, at = {}, 0
    for nm, n in SEG:
        offs[nm] = (at, n)
        at += -(-n // ROW_TILE) * ROW_TILE
    assert at <= PACK_ROWS and PACK_ROWS % (2 * ROW_TILE) == 0
    return offs, at


PACK_OFFS, PACK_USED = _seg_offsets()


def _tile_rows(a, axis):
    n = a.shape[axis]
    pad = [(0, 0)] * a.ndim
    pad[axis] = (0, -(-n // ROW_TILE) * ROW_TILE - n)
    return jnp.pad(a, pad)


def _flat_rows(a, nrows):
    flat = a.reshape(-1)
    return jnp.pad(flat, (0, nrows * D_MODEL - flat.shape[0])).reshape(nrows, D_MODEL)


def _pack_weight_shard(w_in, wp_dn, wp_sb, w_out, w_up, w_down, dn_conv, ffn_conv):
    parts = [w_in.astype(BF16).reshape(W_IN_COLS, D_MODEL), wp_dn.astype(BF16), wp_sb.astype(BF16),
             w_out.astype(BF16), w_up.astype(BF16).reshape(W_UP_COLS, D_MODEL), w_down.astype(BF16),
             _flat_rows(lax.bitcast_convert_type(dn_conv, BF16), ROW_TILE),
             _flat_rows(lax.bitcast_convert_type(ffn_conv, BF16), ROW_TILE),
             jnp.zeros((PACK_ROWS - PACK_USED, D_MODEL), BF16)]
    return jnp.concatenate([_tile_rows(p, 0) for p in parts], axis=0)


def _unpack_weights(g):
    def seg(nm):
        at, n = PACK_OFFS[nm]
        return g[:, at:at + n, :]

    def cols(nm, ncols):
        return seg(nm).reshape(N_CHIPS, D_MODEL, ncols).transpose(1, 0, 2).reshape(D_MODEL, N_CHIPS * ncols)

    def f32_rows(nm, k, ncols):
        raw = seg(nm).reshape(N_CHIPS, -1)[:, :2 * k * ncols].reshape(N_CHIPS, k * ncols, 2)
        vals = lax.bitcast_convert_type(raw, F32).reshape(N_CHIPS, k, ncols)
        return vals.transpose(1, 0, 2).reshape(k, N_CHIPS * ncols)

    w_in = cols("w_in", W_IN_COLS)
    w_up = cols("w_up", W_UP_COLS)
    ffn_conv = f32_rows("ffn_conv", FFN_CONV, FFN_CONV_COLS)
    q_end, a_end, g_end, s_end = 3 * D_MODEL, 3 * D_MODEL + 2 * N_HEADS, 4 * D_MODEL + 2 * N_HEADS, 7 * D_MODEL + 2 * N_HEADS
    return {
        "w_dnqkv": w_in[:, :q_end],
        "w_ab": jnp.pad(w_in[:, q_end:a_end], ((0, 0), (0, LANES - 2 * N_HEADS))),
        "w_dngate": w_in[:, a_end:g_end],
        "w_sbqkv": w_in[:, g_end:s_end],
        "w_gl": w_in[:, s_end:],
        "wp_dn": seg("wp_dn").reshape(D_MODEL, D_MODEL),
        "wp_sb": seg("wp_sb").reshape(D_MODEL, D_MODEL),
        "w_out": seg("w_out").reshape(D_MODEL, D_MODEL),
        "w_up_g": w_up[:, :D_FF], "w_up_u": w_up[:, D_FF:],
        "w_down": seg("w_down").reshape(D_FF, D_MODEL),
        "dn_conv": f32_rows("dn_conv", DN_CONV, DN_CONV_COLS),
        "ffn_conv_g": ffn_conv[:, :D_FF], "ffn_conv_u": ffn_conv[:, D_FF:],
    }


def _pack_grads(gr):
    w_in = jnp.concatenate([gr["w_dnqkv"], gr["w_ab"][:, :2 * N_HEADS], gr["w_dngate"], gr["w_sbqkv"], gr["w_gl"]],
                           axis=1)

    def cols(a, ncols):
        return a.reshape(a.shape[0], N_CHIPS, ncols).transpose(1, 0, 2)

    def rows(a, nrows):
        return a.reshape(N_CHIPS, nrows, a.shape[1])

    def flat(a, nrows):
        a = a.reshape(N_CHIPS, -1)
        return jnp.pad(a, ((0, 0), (0, nrows * D_MODEL - a.shape[1]))).reshape(N_CHIPS, nrows, D_MODEL)

    parts = [cols(w_in, W_IN_COLS).reshape(N_CHIPS, W_IN_COLS, D_MODEL),
             rows(gr["wp_dn"], PROJ_ROWS), rows(gr["wp_sb"], PROJ_ROWS), rows(gr["w_out"], PROJ_ROWS),
             cols(gr["w_up"], W_UP_COLS).reshape(N_CHIPS, W_UP_COLS, D_MODEL),
             rows(gr["w_down"], W_DOWN_ROWS),
             flat(cols(gr["dn_conv"], DN_CONV_COLS), ROW_TILE), flat(cols(gr["ffn_conv"], FFN_CONV_COLS), ROW_TILE),
             jnp.zeros((N_CHIPS, PACK_ROWS - PACK_USED, D_MODEL), F32)]
    return jnp.concatenate([_tile_rows(p, 1) for p in parts], axis=1).astype(BF16)


def _unpack_grad_shard(r):
    def seg(nm):
        at, n = PACK_OFFS[nm]
        return r[at:at + n, :]

    return {
        "w_in": seg("w_in").reshape(D_MODEL, W_IN_COLS),
        "wp_dn": seg("wp_dn"), "wp_sb": seg("wp_sb"), "w_out": seg("w_out"),
        "w_up": seg("w_up").reshape(D_MODEL, W_UP_COLS),
        "w_down": seg("w_down"),
        "dn_conv": seg("dn_conv").reshape(-1)[:DN_CONV * DN_CONV_COLS].reshape(DN_CONV, DN_CONV_COLS),
        "ffn_conv": seg("ffn_conv").reshape(-1)[:FFN_CONV * FFN_CONV_COLS].reshape(FFN_CONV, FFN_CONV_COLS),
    }


def _lane_row(v):
    return jnp.pad(v.reshape(1, -1), ((0, 0), (0, LANES - v.size)))


def kernel(x, norm1_w, w_in, dn_conv_w, dn_A_log, dn_dt_bias, dn_norm_w, w_proj_dn, w_proj_sb, w_out, norm2_w, ffn_w_up, ffn_conv_w, ffn_w_down, norm_f_w, loss_target, m_norm1_w, m_w_in, m_dn_conv_w, m_dn_A_log, m_dn_dt_bias, m_dn_norm_w, m_w_proj_dn, m_w_proj_sb, m_w_out, m_norm2_w, m_ffn_w_up, m_ffn_conv_w, m_ffn_w_down, m_norm_f_w, v_norm1_w, v_w_in, v_dn_conv_w, v_dn_A_log, v_dn_dt_bias, v_dn_norm_w, v_w_proj_dn, v_w_proj_sb, v_w_out, v_norm2_w, v_ffn_w_up, v_ffn_conv_w, v_ffn_w_down, v_norm_f_w):
    shard = _pack_weight_shard(w_in[0], w_proj_dn[0], w_proj_sb[0], w_out[0], ffn_w_up[0], ffn_w_down[0],
                               dn_conv_w[0], ffn_conv_w[0])
    chip_idx = (2 * lax.axis_index("x") + lax.axis_index("y")).astype(jnp.int32)
    gathered = lax.dynamic_update_slice(_gather_shards(shard), shard[None], (chip_idx, 0, 0))
    wts = _unpack_weights(gathered)
    wts.update(norm1=norm1_w, norm2=norm2_w, normf=norm_f_w.reshape(1, D_MODEL), dn_norm=dn_norm_w,
               alog=_lane_row(dn_A_log), dtb=_lane_row(dn_dt_bias))

    loss_part, grad_x, gr = _local_step(x[0], loss_target[0], wts)

    c_idx = lax.axis_index("c").astype(jnp.int32).reshape(1)
    packed = _pack_grads(gr)
    partial_sum = _pair_add(packed, _pair_exchange_halves(packed), c_idx)
    reduced_half = _sum_partials(partial_sum, _chip_exchange(partial_sum), chip_idx.reshape(1))
    other_half = _pair_share(reduced_half)
    is_south = lax.axis_index("c") == 0
    gsh = _unpack_grad_shard(jnp.concatenate([jnp.where(is_south, reduced_half, other_half),
                                              jnp.where(is_south, other_half, reduced_half)], axis=0))

    tail = jnp.concatenate([gr["dn_norm"], gr["alog"][:, :N_HEADS], gr["dtb"][:, :N_HEADS], loss_part[:, :1]], axis=1)
    small = jnp.concatenate([gr["norm1"], gr["norm2"], gr["normf"],
                             jnp.pad(tail, ((0, 0), (0, D_MODEL - tail.shape[1]))),
                             jnp.zeros((SMALL_ROWS - 4, D_MODEL), F32)], axis=0)
    small = _small_allreduce(small)
    at = HEAD_DIM
    g_small = {"norm1_w": small[0:1], "norm2_w": small[1:2], "norm_f_w": small[2],
               "dn_norm_w": small[3:4, :at], "dn_A_log": small[3:4, at:at + N_HEADS],
               "dn_dt_bias": small[3:4, at + N_HEADS:at + 2 * N_HEADS]}
    loss = small[3, at + 2 * N_HEADS]

    big = {"w_in": (w_in, m_w_in, v_w_in, gsh["w_in"]), "dn_conv_w": (dn_conv_w, m_dn_conv_w, v_dn_conv_w, gsh["dn_conv"]),
           "w_proj_dn": (w_proj_dn, m_w_proj_dn, v_w_proj_dn, gsh["wp_dn"]),
           "w_proj_sb": (w_proj_sb, m_w_proj_sb, v_w_proj_sb, gsh["wp_sb"]),
           "w_out": (w_out, m_w_out, v_w_out, gsh["w_out"]),
           "ffn_w_up": (ffn_w_up, m_ffn_w_up, v_ffn_w_up, gsh["w_up"]),
           "ffn_conv_w": (ffn_conv_w, m_ffn_conv_w, v_ffn_conv_w, gsh["ffn_conv"]),
           "ffn_w_down": (ffn_w_down, m_ffn_w_down, v_ffn_w_down, gsh["w_down"])}
    res = {}
    for nm, (w, m, v, g) in big.items():
        d, nm_, nv_ = _adamw(w[0], g, m[0], v[0], "adamw_" + nm)
        res[nm] = (g[None], d[None], nm_[None], nv_[None])

    names = ["norm1_w", "norm2_w", "norm_f_w", "dn_norm_w", "dn_A_log", "dn_dt_bias"]
    given = {"norm1_w": (norm1_w, m_norm1_w, v_norm1_w), "norm2_w": (norm2_w, m_norm2_w, v_norm2_w),
             "norm_f_w": (norm_f_w, m_norm_f_w, v_norm_f_w), "dn_norm_w": (dn_norm_w, m_dn_norm_w, v_dn_norm_w),
             "dn_A_log": (dn_A_log, m_dn_A_log, v_dn_A_log), "dn_dt_bias": (dn_dt_bias, m_dn_dt_bias, v_dn_dt_bias)}

    def stack(k, fill):
        rows = [jnp.pad(given[nm][k].reshape(1, -1), ((0, 0), (0, D_MODEL - given[nm][k].size)),
                        constant_values=fill) for nm in names]
        return jnp.concatenate(rows + [jnp.full((SMALL_ROWS - len(names), D_MODEL), fill, F32)], axis=0)

    g_rows = jnp.concatenate(
        [jnp.pad(g_small[nm].reshape(1, -1), ((0, 0), (0, D_MODEL - g_small[nm].size))) for nm in names]
        + [jnp.zeros((SMALL_ROWS - len(names), D_MODEL), F32)], axis=0)
    d_s, m_s, v_s = _adamw(stack(0, 0.0), g_rows, stack(1, 0.0), stack(2, 1.0), "adamw_small")
    for r, nm in enumerate(names):
        shape = given[nm][0].shape
        n = given[nm][0].size
        res[nm] = (g_small[nm].reshape(shape), d_s[r, :n].reshape(shape), m_s[r, :n].reshape(shape),
                   v_s[r, :n].reshape(shape))

    order = ["norm1_w", "w_in", "dn_conv_w", "dn_A_log", "dn_dt_bias", "dn_norm_w", "w_proj_dn", "w_proj_sb",
             "w_out", "norm2_w", "ffn_w_up", "ffn_conv_w", "ffn_w_down", "norm_f_w"]
    outs = [loss, grad_x[None]]
    for k in range(4):
        outs += [res[nm][k] for nm in order]
    return tuple(outs)
```

```python
import functools

import jax
import jax.numpy as jnp
from jax import lax
from jax.experimental import pallas as pl
from jax.experimental.pallas import tpu as pltpu

F32 = jnp.float32
BF16 = jnp.bfloat16
HIGHEST = lax.Precision.HIGHEST
MESH = pl.DeviceIdType.MESH

EPS = 1e-6
D_MODEL = 1024
N_HEADS = 8
HEAD_DIM = 128
DN_CONV = 4
DN_CHUNK = 64
D_FF = 2816
FFN_CONV = 3
ADAM_LR, ADAM_B1, ADAM_B2, ADAM_EPS, ADAM_WD, ADAM_STEP = 0.001, 0.9, 0.999, 1e-08, 0.01, 10

N_CHIPS = 4
LANES = 128
HALO = 8
VMEM_LIMIT = 48 * 1024 * 1024
PACK_ROWS = 5248
SMALL_ROWS = 8


def _params(sem=None):
    return pltpu.CompilerParams(dimension_semantics=sem, vmem_limit_bytes=VMEM_LIMIT)


def _pick(n, target):
    best = None
    for b in range(LANES, min(n, target) + 1, LANES):
        if n % b == 0:
            best = b
    return best or n


ELEMENTWISE_COLS = 1408


def _rows(t, target=256):
    return min(t, target)


def _dot(a, b, precision=None):
    return lax.dot_general(a, b, (((1,), (0,)), ((), ())), precision=precision, preferred_element_type=F32)


def _dot_nt(a, b, precision=None):
    return lax.dot_general(a, b, (((1,), (1,)), ((), ())), precision=precision, preferred_element_type=F32)


def _dot_tn(a, b, precision=None):
    return lax.dot_general(a, b, (((0,), (0,)), ((), ())), precision=precision, preferred_element_type=F32)


def _rms(x, w):
    return x * lax.rsqrt(jnp.mean(x * x, axis=-1, keepdims=True) + EPS) * w


def _silu(x):
    return x * jax.nn.sigmoid(x)


def _softplus(x):
    return jnp.maximum(x, 0.0) + jnp.log(1.0 + jnp.exp(-jnp.abs(x)))


MM_BLOCK = 1408


def _mm(a, b, *, ta=False, tb=False, add=None, out_dtype=F32, name, bm=MM_BLOCK, bn=MM_BLOCK, bk=MM_BLOCK):
    m = a.shape[1] if ta else a.shape[0]
    k = a.shape[0] if ta else a.shape[1]
    n = b.shape[0] if tb else b.shape[1]
    bm, bn, bk = _pick(m, bm), _pick(n, bn), _pick(k, bk)
    nk = k // bk
    dims = (((0 if ta else 1,), (1 if tb else 0,)), ((), ()))

    def body(*refs):
        a_ref, b_ref = refs[:2]
        c_ref = refs[2] if add is not None else None
        o_ref = refs[3] if add is not None else refs[2]
        acc = refs[-1]
        kk = pl.program_id(2)
        part = lax.dot_general(a_ref[...].astype(BF16), b_ref[...].astype(BF16), dims, preferred_element_type=F32)

        def finish(r):
            if add is not None:
                r = r + c_ref[...].astype(F32)
            o_ref[...] = r.astype(out_dtype)

        if nk == 1:
            finish(part)
            return

        @pl.when(kk == 0)
        def _():
            acc[...] = part

        @pl.when(jnp.logical_and(kk > 0, kk < nk - 1))
        def _():
            acc[...] += part

        @pl.when(kk == nk - 1)
        def _():
            finish(acc[...] + part)

    a_spec = (pl.BlockSpec((bk, bm), lambda i, j, kk: (kk, i)) if ta
              else pl.BlockSpec((bm, bk), lambda i, j, kk: (i, kk)))
    b_spec = (pl.BlockSpec((bn, bk), lambda i, j, kk: (j, kk)) if tb
              else pl.BlockSpec((bk, bn), lambda i, j, kk: (kk, j)))
    o_spec = pl.BlockSpec((bm, bn), lambda i, j, kk: (i, j))
    in_specs = [a_spec, b_spec] + ([o_spec] if add is not None else [])
    args = (a, b) + ((add,) if add is not None else ())
    return pl.pallas_call(
        body, name=name, grid=(m // bm, n // bn, nk),
        in_specs=in_specs, out_specs=o_spec,
        out_shape=jax.ShapeDtypeStruct((m, n), out_dtype),
        scratch_shapes=[pltpu.VMEM((bm, bn), F32)] if nk > 1 else [],
        compiler_params=_params(("parallel", "parallel", "arbitrary")),
    )(*args)


def _norm1_fwd(x, w, w_ab):
    t = x.shape[0]
    tb = _rows(t)

    def body(x_ref, w_ref, wab_ref, n_ref, hab_ref):
        n = _rms(x_ref[...], w_ref[...]).astype(BF16)
        n_ref[...] = n
        hab_ref[...] = _dot(n, wab_ref[...])

    return pl.pallas_call(
        body, name="norm1_fwd", grid=(t // tb,),
        in_specs=[pl.BlockSpec((tb, D_MODEL), lambda i: (i, 0)),
                  pl.BlockSpec((1, D_MODEL), lambda i: (0, 0)),
                  pl.BlockSpec((D_MODEL, LANES), lambda i: (0, 0))],
        out_specs=[pl.BlockSpec((tb, D_MODEL), lambda i: (i, 0)),
                   pl.BlockSpec((tb, LANES), lambda i: (i, 0))],
        out_shape=[jax.ShapeDtypeStruct((t, D_MODEL), BF16), jax.ShapeDtypeStruct((t, LANES), F32)],
        compiler_params=_params(("arbitrary",)),
    )(x, w, w_ab)


def _norm1_bwd(x, w, dn, dres, dab, w_ab):
    t = x.shape[0]
    tb = _rows(t)

    def body(x_ref, w_ref, dn_ref, dres_ref, dab_ref, wab_ref, dx_ref, dw_ref):
        i = pl.program_id(0)
        g = dn_ref[...] + _dot_nt(dab_ref[...].astype(BF16), wab_ref[...])
        _, vjp = jax.vjp(_rms, x_ref[...], w_ref[...])
        dx, dw = vjp(g)
        dx_ref[...] = dres_ref[...] + dx

        @pl.when(i == 0)
        def _():
            dw_ref[...] = jnp.zeros_like(dw_ref)

        dw_ref[...] += dw

    row = pl.BlockSpec((tb, D_MODEL), lambda i: (i, 0))
    vec = pl.BlockSpec((1, D_MODEL), lambda i: (0, 0))
    return pl.pallas_call(
        body, name="norm1_bwd", grid=(t // tb,),
        in_specs=[row, vec, row, row, pl.BlockSpec((tb, LANES), lambda i: (i, 0)),
                  pl.BlockSpec((D_MODEL, LANES), lambda i: (0, 0))],
        out_specs=[row, vec],
        out_shape=[jax.ShapeDtypeStruct((t, D_MODEL), F32), jax.ShapeDtypeStruct((1, D_MODEL), F32)],
        compiler_params=_params(("arbitrary",)),
    )(x, w, dn, dres, dab, w_ab)


def _conv_fwd(x, w, name):
    t, c = x.shape
    kk = w.shape[0]
    tb, cb = _rows(t, 512), _pick(c, ELEMENTWISE_COLS)
    per = tb // HALO

    def body(x_ref, halo_ref, w_ref, y_ref, buf):
        i = pl.program_id(0)
        buf[pl.ds(HALO, tb), :] = x_ref[...]
        buf[pl.ds(0, HALO), :] = jnp.where(i == 0, 0.0, halo_ref[...])
        y = w_ref[0:1, :] * buf[pl.ds(HALO - (kk - 1), tb), :]
        for s in range(1, kk):
            y = y + w_ref[s:s + 1, :] * buf[pl.ds(HALO - (kk - 1) + s, tb), :]
        y_ref[...] = y

    return pl.pallas_call(
        body, name=name, grid=(t // tb, c // cb),
        in_specs=[pl.BlockSpec((tb, cb), lambda i, j: (i, j)),
                  pl.BlockSpec((HALO, cb), lambda i, j: (jnp.maximum(i * per - 1, 0), j)),
                  pl.BlockSpec((kk, cb), lambda i, j: (0, j))],
        out_specs=pl.BlockSpec((tb, cb), lambda i, j: (i, j)),
        out_shape=jax.ShapeDtypeStruct((t, c), F32),
        scratch_shapes=[pltpu.VMEM((tb + HALO, cb), F32)],
        compiler_params=_params(("parallel", "parallel")),
    )(x, x, w)


def _conv_bwd(dy, x, w, name, dx_dtype):
    t, c = x.shape
    kk = w.shape[0]
    tb, cb = _rows(t, 512), _pick(c, ELEMENTWISE_COLS)
    per = tb // HALO
    nblk = t // tb

    def body(dy_ref, after_ref, x_ref, before_ref, w_ref, dx_ref, dw_ref, dbuf, xbuf):
        i = pl.program_id(1)
        dy = dy_ref[...]
        dbuf[pl.ds(0, tb), :] = dy
        dbuf[pl.ds(tb, HALO), :] = jnp.where(i == nblk - 1, 0.0, after_ref[...])
        xbuf[pl.ds(HALO, tb), :] = x_ref[...]
        xbuf[pl.ds(0, HALO), :] = jnp.where(i == 0, 0.0, before_ref[...])
        dx = w_ref[0:1, :] * dbuf[pl.ds(kk - 1, tb), :]
        for s in range(1, kk):
            dx = dx + w_ref[s:s + 1, :] * dbuf[pl.ds(kk - 1 - s, tb), :]
        dx_ref[...] = dx.astype(dx_dtype)

        @pl.when(i == 0)
        def _():
            dw_ref[...] = jnp.zeros_like(dw_ref)

        for s in range(kk):
            part = jnp.sum(dy * xbuf[pl.ds(HALO - (kk - 1) + s, tb), :], axis=0, keepdims=True)
            dw_ref[s:s + 1, :] += part

    blk = pl.BlockSpec((tb, cb), lambda j, i: (i, j))
    return pl.pallas_call(
        body, name=name, grid=(c // cb, nblk),
        in_specs=[blk,
                  pl.BlockSpec((HALO, cb), lambda j, i: (jnp.minimum((i + 1) * per, t // HALO - 1), j)),
                  blk,
                  pl.BlockSpec((HALO, cb), lambda j, i: (jnp.maximum(i * per - 1, 0), j)),
                  pl.BlockSpec((kk, cb), lambda j, i: (0, j))],
        out_specs=[blk, pl.BlockSpec((HALO, cb), lambda j, i: (0, j))],
        out_shape=[jax.ShapeDtypeStruct((t, c), dx_dtype), jax.ShapeDtypeStruct((HALO, c), F32)],
        scratch_shapes=[pltpu.VMEM((tb + HALO, cb), F32), pltpu.VMEM((tb + HALO, cb), F32)],
        compiler_params=_params(("parallel", "arbitrary")),
    )(dy, dy, x, x, w)


def _dn_prep_fn(c, hab, alog, dtb):
    s = _silu(c)
    heads = []
    for h in range(2 * N_HEADS):
        sh = s[:, h * HEAD_DIM:(h + 1) * HEAD_DIM]
        heads.append(sh * lax.rsqrt(jnp.sum(sh * sh, axis=-1, keepdims=True) + EPS))
    qn = jnp.concatenate(heads[:N_HEADS], axis=1)
    kn = jnp.concatenate(heads[N_HEADS:], axis=1)
    v = s[:, 2 * D_MODEL:]
    lane = lax.broadcasted_iota(jnp.int32, hab.shape, 1)
    g = -jnp.exp(alog) * _softplus(hab + dtb)
    beta = jax.nn.sigmoid(hab)
    gb = jnp.where(lane < N_HEADS, g, jnp.where(lane < 2 * N_HEADS, beta, 0.0))
    return qn, kn, v, gb


def _to_heads(ref, val):
    for h in range(N_HEADS):
        ref[h] = val[:, h * HEAD_DIM:(h + 1) * HEAD_DIM]


def _from_heads(ref):
    return jnp.concatenate([ref[h] for h in range(N_HEADS)], axis=1)


def _dn_prep_fwd(c, hab, alog, dtb):
    t = c.shape[0]
    tb = _rows(t)

    def body(c_ref, hab_ref, alog_ref, dtb_ref, q_ref, k_ref, v_ref, gb_ref):
        qn, kn, v, gb = _dn_prep_fn(c_ref[...], hab_ref[...], alog_ref[...], dtb_ref[...])
        _to_heads(q_ref, qn)
        _to_heads(k_ref, kn)
        _to_heads(v_ref, v)
        gb_ref[...] = gb

    hm = pl.BlockSpec((N_HEADS, tb, HEAD_DIM), lambda i: (0, i, 0))
    nar = pl.BlockSpec((tb, LANES), lambda i: (i, 0))
    vec = pl.BlockSpec((1, LANES), lambda i: (0, 0))
    return pl.pallas_call(
        body, name="dn_prep_fwd", grid=(t // tb,),
        in_specs=[pl.BlockSpec((tb, 3 * D_MODEL), lambda i: (i, 0)), nar, vec, vec],
        out_specs=[hm, hm, hm, nar],
        out_shape=[jax.ShapeDtypeStruct((N_HEADS, t, HEAD_DIM), F32)] * 3 + [jax.ShapeDtypeStruct((t, LANES), F32)],
        compiler_params=_params(("parallel",)),
    )(c, hab, alog, dtb)


def _dn_prep_bwd(c, hab, alog, dtb, dq, dk, dv, dgb):
    t = c.shape[0]
    tb = _rows(t)

    def body(c_ref, hab_ref, alog_ref, dtb_ref, dq_ref, dk_ref, dv_ref, dgb_ref,
             dc_ref, dhab_ref, dalog_ref, ddtb_ref):
        i = pl.program_id(0)
        _, vjp = jax.vjp(_dn_prep_fn, c_ref[...], hab_ref[...], alog_ref[...], dtb_ref[...])
        dc, dhab, dalog, ddtb = vjp((_from_heads(dq_ref), _from_heads(dk_ref), _from_heads(dv_ref), dgb_ref[...]))
        dc_ref[...] = dc
        dhab_ref[...] = dhab

        @pl.when(i == 0)
        def _():
            dalog_ref[...] = jnp.zeros_like(dalog_ref)
            ddtb_ref[...] = jnp.zeros_like(ddtb_ref)

        dalog_ref[...] += dalog
        ddtb_ref[...] += ddtb

    hm = pl.BlockSpec((N_HEADS, tb, HEAD_DIM), lambda i: (0, i, 0))
    wide = pl.BlockSpec((tb, 3 * D_MODEL), lambda i: (i, 0))
    nar = pl.BlockSpec((tb, LANES), lambda i: (i, 0))
    vec = pl.BlockSpec((1, LANES), lambda i: (0, 0))
    return pl.pallas_call(
        body, name="dn_prep_bwd", grid=(t // tb,),
        in_specs=[wide, nar, vec, vec, hm, hm, hm, nar],
        out_specs=[wide, nar, vec, vec],
        out_shape=[jax.ShapeDtypeStruct((t, 3 * D_MODEL), F32), jax.ShapeDtypeStruct((t, LANES), F32),
                   jax.ShapeDtypeStruct((1, LANES), F32), jax.ShapeDtypeStruct((1, LANES), F32)],
        compiler_params=_params(("arbitrary",)),
    )(c, hab, alog, dtb, dq, dk, dv, dgb)


DN_PREC = lax.Precision.HIGH
DN_GROUP = 8


def _dn_prec(a):
    return DN_PREC if a.dtype == F32 else None


def _bdot(a, b):
    return lax.dot_general(a, b, (((2,), (1,)), ((0,), (0,))), precision=_dn_prec(a), preferred_element_type=F32)


def _bdot_nt(a, b):
    return lax.dot_general(a, b, (((2,), (2,)), ((0,), (0,))), precision=_dn_prec(a), preferred_element_type=F32)


def _bdot_tn(a, b):
    return lax.dot_general(a, b, (((1,), (1,)), ((0,), (0,))), precision=_dn_prec(a), preferred_element_type=F32)


def _unit_lower_inverse(lmat):
    c = lmat.shape[-1]
    ri = lax.broadcasted_iota(jnp.int32, (c, c), 0)
    ci = lax.broadcasted_iota(jnp.int32, (c, c), 1)
    p = -lmat
    tinv = jnp.where(ri == ci, 1.0, 0.0) + p
    for _ in range(max(c.bit_length() - 2, 0)):
        p = _bdot(p, p)
        tinv = tinv + _bdot(tinv, p)
    return tinv


@jax.custom_vjp
def _solve_with(lmat, rhs, tinv):
    return _bdot(tinv, rhs)


def _solve_with_fwd(lmat, rhs, tinv):
    sol = _bdot(tinv, rhs)
    return sol, (sol, tinv)


def _solve_with_bwd(res, dsol):
    sol, tinv = res
    drhs = _bdot_tn(tinv, dsol)
    return -_bdot_nt(drhs, sol), drhs, jnp.zeros_like(tinv)


_solve_with.defvjp(_solve_with_fwd, _solve_with_bwd)


def _dn_local(q, k, v, gcol, grow, bcol, tinv):
    g, c, _ = q.shape
    ri = lax.broadcasted_iota(jnp.int32, (c, c), 0)
    ci = lax.broadcasted_iota(jnp.int32, (c, c), 1)
    lower = ri >= ci
    gc_col = jnp.sum(jnp.where(lower, jnp.broadcast_to(grow, (g, c, c)), 0.0), axis=2, keepdims=True)
    gc_row = jnp.sum(jnp.where(ri <= ci, jnp.broadcast_to(gcol, (g, c, c)), 0.0), axis=1, keepdims=True)
    qs = q * (HEAD_DIM ** -0.5)
    kb = k * bcol
    vb = v * bcol
    decay = jnp.where(lower, jnp.exp(jnp.where(lower, gc_col - gc_row, 0.0)), 0.0)
    lmat = jnp.where(ri > ci, _bdot_nt(kb, k) * decay, 0.0)
    eg = jnp.exp(gc_col)
    rhs = jnp.concatenate([vb, kb * eg], axis=2)
    if tinv is None:
        tinv = _unit_lower_inverse(lmat)
    sol = _solve_with(lmat, rhs, tinv)
    a_qk = jnp.where(lower, _bdot_nt(qs, k) * decay, 0.0)
    g_last = jnp.sum(grow, axis=2, keepdims=True)
    kdec = k * jnp.exp(g_last - gc_col)
    egl = jnp.broadcast_to(jnp.exp(g_last), (g, 1, HEAD_DIM))
    return sol[:, :, :HEAD_DIM], sol[:, :, HEAD_DIM:], a_qk, qs * eg, kdec, egl, tinv


def _dn_seq(u, w, a_qk, qe, kdec, egl, s_in):
    b16 = lambda x: x.astype(BF16)
    v_new = u - _bdot(b16(w), b16(s_in))
    o = _bdot(b16(qe), b16(s_in)) + _bdot(b16(a_qk), b16(v_new))
    return o, s_in * egl + _bdot_tn(b16(kdec), b16(v_new))


def _dn_local_specs(t):
    grp = min(DN_GROUP, t // DN_CHUNK)
    rows = grp * DN_CHUNK
    blk = pl.BlockSpec((1, rows, HEAD_DIM), lambda h, i: (h, i, 0))
    col = pl.BlockSpec((1, grp, DN_CHUNK, 1), lambda h, i: (h, i, 0, 0))
    row = pl.BlockSpec((1, grp, 1, DN_CHUNK), lambda h, i: (h, i, 0, 0))
    sq = pl.BlockSpec((1, grp, DN_CHUNK, DN_CHUNK), lambda h, i: (h, i, 0, 0))
    lane = pl.BlockSpec((1, grp, 1, HEAD_DIM), lambda h, i: (h, i, 0, 0))
    return grp, blk, col, row, sq, lane


def _dn_shapes(t):
    nchunk = t // DN_CHUNK
    big = jax.ShapeDtypeStruct((N_HEADS, t, HEAD_DIM), F32)
    col = jax.ShapeDtypeStruct((N_HEADS, nchunk, DN_CHUNK, 1), F32)
    row = jax.ShapeDtypeStruct((N_HEADS, nchunk, 1, DN_CHUNK), F32)
    sq = jax.ShapeDtypeStruct((N_HEADS, nchunk, DN_CHUNK, DN_CHUNK), F32)
    lane = jax.ShapeDtypeStruct((N_HEADS, nchunk, 1, HEAD_DIM), F32)
    return big, col, row, sq, lane


def _dn_local_fwd(q, k, v, gcol, grow, bcol):
    t = q.shape[1]
    grp, blk, col, row, sq, lane = _dn_local_specs(t)
    big, _, _, sqs, lanes = _dn_shapes(t)

    def body(q_ref, k_ref, v_ref, gc_ref, gr_ref, bc_ref, u_ref, w_ref, a_ref, qe_ref, kd_ref, egl_ref, t_ref):
        split = lambda r: r[0].reshape(grp, DN_CHUNK, HEAD_DIM)
        u, w, a_qk, qe, kdec, egl, tinv = _dn_local(split(q_ref), split(k_ref), split(v_ref), gc_ref[0],
                                                     gr_ref[0], bc_ref[0], None)
        for ref, val in ((u_ref, u), (w_ref, w), (qe_ref, qe), (kd_ref, kdec)):
            ref[0] = val.reshape(grp * DN_CHUNK, HEAD_DIM)
        a_ref[0] = a_qk
        egl_ref[0] = egl
        t_ref[0] = tinv

    return pl.pallas_call(
        body, name="dn_local_fwd", grid=(N_HEADS, t // (grp * DN_CHUNK)),
        in_specs=[blk, blk, blk, col, row, col],
        out_specs=[blk, blk, sq, blk, blk, lane, sq],
        out_shape=[big, big, sqs, big, big, lanes, sqs],
        compiler_params=_params(("parallel", "parallel")),
    )(q, k, v, gcol, grow, bcol)


def _dn_local_bwd(q, k, v, gcol, grow, bcol, tinv, du, dw, da, dqe, dkd, degl):
    t = q.shape[1]
    grp, blk, col, row, sq, lane = _dn_local_specs(t)
    big, cols, rows_, _, _ = _dn_shapes(t)

    def body(q_ref, k_ref, v_ref, gc_ref, gr_ref, bc_ref, t_ref, du_ref, dw_ref, da_ref, dqe_ref, dkd_ref,
             degl_ref, dq_ref, dk_ref, dv_ref, dgc_ref, dgr_ref, dbc_ref):
        split = lambda r: r[0].reshape(grp, DN_CHUNK, HEAD_DIM)
        tinv_v = t_ref[0]
        fn = lambda q_, k_, v_, gc_, gr_, bc_: _dn_local(q_, k_, v_, gc_, gr_, bc_, tinv_v)[:6]
        _, vjp = jax.vjp(fn, split(q_ref), split(k_ref), split(v_ref), gc_ref[0], gr_ref[0], bc_ref[0])
        dq, dk, dv, dgc, dgr, dbc = vjp((split(du_ref), split(dw_ref), da_ref[0], split(dqe_ref), split(dkd_ref),
                                         degl_ref[0]))
        for ref, val in ((dq_ref, dq), (dk_ref, dk), (dv_ref, dv)):
            ref[0] = val.reshape(grp * DN_CHUNK, HEAD_DIM)
        dgc_ref[0] = dgc
        dgr_ref[0] = dgr
        dbc_ref[0] = dbc

    return pl.pallas_call(
        body, name="dn_local_bwd", grid=(N_HEADS, t // (grp * DN_CHUNK)),
        in_specs=[blk, blk, blk, col, row, col, sq, blk, blk, sq, blk, blk, lane],
        out_specs=[blk, blk, blk, col, row, col],
        out_shape=[big, big, big, cols, rows_, cols],
        compiler_params=_params(("parallel", "parallel")),
    )(q, k, v, gcol, grow, bcol, tinv, du, dw, da, dqe, dkd, degl)


def _dn_seq_specs(nchunk, rev):
    def idx(n):
        return nchunk - 1 - n if rev else n

    blk = pl.BlockSpec((N_HEADS, DN_CHUNK, HEAD_DIM), lambda n: (0, idx(n), 0))
    sq = pl.BlockSpec((N_HEADS, 1, DN_CHUNK, DN_CHUNK), lambda n: (0, idx(n), 0, 0))
    lane = pl.BlockSpec((N_HEADS, 1, 1, HEAD_DIM), lambda n: (0, idx(n), 0, 0))
    st = pl.BlockSpec((N_HEADS, 1, HEAD_DIM, HEAD_DIM), lambda n: (0, idx(n), 0, 0))
    return blk, sq, lane, st


def _dn_seq_fwd(u, w, a_qk, qe, kdec, egl):
    t = u.shape[1]
    nchunk = t // DN_CHUNK
    blk, sq, lane, st = _dn_seq_specs(nchunk, False)

    def body(u_ref, w_ref, a_ref, qe_ref, kd_ref, egl_ref, o_ref, s_ref, state):
        @pl.when(pl.program_id(0) == 0)
        def _():
            state[...] = jnp.zeros_like(state)

        s_in = state[...]
        s_ref[:, 0] = s_in
        o, s_out = _dn_seq(u_ref[...], w_ref[...], a_ref[:, 0], qe_ref[...], kd_ref[...], egl_ref[:, 0], s_in)
        o_ref[...] = o
        state[...] = s_out

    return pl.pallas_call(
        body, name="dn_seq_fwd", grid=(nchunk,),
        in_specs=[blk, blk, sq, blk, blk, lane],
        out_specs=[blk, st],
        out_shape=[jax.ShapeDtypeStruct((N_HEADS, t, HEAD_DIM), F32),
                   jax.ShapeDtypeStruct((N_HEADS, nchunk, HEAD_DIM, HEAD_DIM), F32)],
        scratch_shapes=[pltpu.VMEM((N_HEADS, HEAD_DIM, HEAD_DIM), F32)],
        compiler_params=_params(("arbitrary",)),
    )(u, w, a_qk, qe, kdec, egl)


def _dn_seq_bwd(u, w, a_qk, qe, kdec, egl, states, do):
    t = u.shape[1]
    nchunk = t // DN_CHUNK
    blk, sq, lane, st = _dn_seq_specs(nchunk, True)
    big, _, _, sqs, lanes = _dn_shapes(t)

    def body(u_ref, w_ref, a_ref, qe_ref, kd_ref, egl_ref, s_ref, do_ref,
             du_ref, dw_ref, da_ref, dqe_ref, dkd_ref, degl_ref, dstate):
        @pl.when(pl.program_id(0) == 0)
        def _():
            dstate[...] = jnp.zeros_like(dstate)

        _, vjp = jax.vjp(_dn_seq, u_ref[...], w_ref[...], a_ref[:, 0], qe_ref[...], kd_ref[...], egl_ref[:, 0],
                         s_ref[:, 0])
        du, dw, da, dqe, dkd, degl, ds = vjp((do_ref[...], dstate[...]))
        du_ref[...] = du
        dw_ref[...] = dw
        da_ref[:, 0] = da
        dqe_ref[...] = dqe
        dkd_ref[...] = dkd
        degl_ref[:, 0] = degl
        dstate[...] = ds

    return pl.pallas_call(
        body, name="dn_seq_bwd", grid=(nchunk,),
        in_specs=[blk, blk, sq, blk, blk, lane, st, blk],
        out_specs=[blk, blk, sq, blk, blk, lane],
        out_shape=[big, big, sqs, big, big, lanes],
        scratch_shapes=[pltpu.VMEM((N_HEADS, HEAD_DIM, HEAD_DIM), F32)],
        compiler_params=_params(("arbitrary",)),
    )(u, w, a_qk, qe, kdec, egl, states, do)


def _dn_post_fn(o, gate, w):
    outs = []
    for h in range(N_HEADS):
        sl = slice(h * HEAD_DIM, (h + 1) * HEAD_DIM)
        outs.append(_rms(o[:, sl], w) * _silu(gate[:, sl]))
    return jnp.concatenate(outs, axis=1)


def _dn_post_fwd(o, gate, w):
    t = gate.shape[0]
    tb = _rows(t)

    def body(o_ref, g_ref, w_ref, y_ref):
        y_ref[...] = _dn_post_fn(_from_heads(o_ref), g_ref[...], w_ref[...]).astype(BF16)

    row = pl.BlockSpec((tb, D_MODEL), lambda i: (i, 0))
    hm = pl.BlockSpec((N_HEADS, tb, HEAD_DIM), lambda i: (0, i, 0))
    return pl.pallas_call(
        body, name="dn_post_fwd", grid=(t // tb,),
        in_specs=[hm, row, pl.BlockSpec((1, HEAD_DIM), lambda i: (0, 0))],
        out_specs=row, out_shape=jax.ShapeDtypeStruct((t, D_MODEL), BF16),
        compiler_params=_params(("parallel",)),
    )(o, gate, w)


def _dn_post_bwd(o, gate, w, dy):
    t = gate.shape[0]
    tb = _rows(t)

    def body(o_ref, g_ref, w_ref, dy_ref, do_ref, dg_ref, dw_ref):
        i = pl.program_id(0)
        _, vjp = jax.vjp(_dn_post_fn, _from_heads(o_ref), g_ref[...], w_ref[...])
        do, dg, dw = vjp(dy_ref[...])
        _to_heads(do_ref, do)
        dg_ref[...] = dg.astype(BF16)

        @pl.when(i == 0)
        def _():
            dw_ref[...] = jnp.zeros_like(dw_ref)

        dw_ref[...] += dw

    row = pl.BlockSpec((tb, D_MODEL), lambda i: (i, 0))
    hm = pl.BlockSpec((N_HEADS, tb, HEAD_DIM), lambda i: (0, i, 0))
    vec = pl.BlockSpec((1, HEAD_DIM), lambda i: (0, 0))
    return pl.pallas_call(
        body, name="dn_post_bwd", grid=(t // tb,),
        in_specs=[hm, row, vec, row],
        out_specs=[hm, row, vec],
        out_shape=[jax.ShapeDtypeStruct((N_HEADS, t, HEAD_DIM), F32), jax.ShapeDtypeStruct((t, D_MODEL), BF16),
                   jax.ShapeDtypeStruct((1, HEAD_DIM), F32)],
        compiler_params=_params(("arbitrary",)),
    )(o, gate, w, dy)


def _split_bf16(x):
    hi = x.astype(BF16)
    lo = (x - hi.astype(F32)).astype(BF16)
    return hi, lo


SB_Q_BLOCK = 512
SB_K_BLOCK = 256


def _sb_logits(q, kb, mask, scale):
    z = _dot_nt(q, kb) * scale
    ls = jnp.minimum(z, 0.0) - jnp.log(1.0 + jnp.exp(-jnp.abs(z)))
    lk = ls - z
    if mask is not None:
        lk = jnp.where(mask, lk, 0.0)
    return ls, lk


def _sb_blocks(t):
    bq = min(SB_Q_BLOCK, t)
    bk = min(SB_K_BLOCK, bq)
    return bq, bk, bq // bk


def _sb_fwd(qkv):
    t = qkv.shape[0]
    bq, bk, nd = _sb_blocks(t)
    scale = HEAD_DIM ** -0.5

    def body(q_ref, k_ref, v_ref, o_ref, tot_ref):
        i = pl.program_id(1)
        q = q_ref[...]
        rj = lax.broadcasted_iota(jnp.int32, (bk, bk), 0)
        cj = lax.broadcasted_iota(jnp.int32, (bk, bk), 1)
        after = (rj > cj).astype(BF16)
        trow = lax.broadcasted_iota(jnp.int32, (bq, bk), 0)
        scol = lax.broadcasted_iota(jnp.int32, (bq, bk), 1)

        def tile(j, run, acc, mask):
            off = pl.multiple_of(j * bk, bk)
            kb = k_ref[pl.ds(off, bk), :]
            vb = v_ref[pl.ds(off, bk), :]
            ls, lk = _sb_logits(q, kb, mask, scale)
            hi, lo = _split_bf16(lk)
            between = _dot(hi, after) + _dot(lo, after) + run
            a = jnp.exp(ls + between)
            if mask is not None:
                a = jnp.where(mask, a, 0.0)
            acc = acc + _dot(a.astype(BF16), vb)
            return run + jnp.sum(lk, axis=1, keepdims=True), acc

        run, acc = jnp.zeros((bq, 1), F32), jnp.zeros((bq, HEAD_DIM), F32)
        for d in reversed(range(nd)):
            run, acc = tile(i * nd + d, run, acc, scol + d * bk < trow)
        run, acc = lax.fori_loop(0, i * nd, lambda it, c: tile(i * nd - 1 - it, c[0], c[1], None), (run, acc))
        o_ref[...] = acc.astype(BF16)
        tot_ref[...] = jnp.broadcast_to(run, (bq, HEAD_DIM))

    qs = pl.BlockSpec((bq, HEAD_DIM), lambda h, i: (i, h))
    ks = pl.BlockSpec((t, HEAD_DIM), lambda h, i: (0, N_HEADS + h))
    vs = pl.BlockSpec((t, HEAD_DIM), lambda h, i: (0, 2 * N_HEADS + h))
    return pl.pallas_call(
        body, name="sb_fwd", grid=(N_HEADS, t // bq),
        in_specs=[qs, ks, vs], out_specs=[qs, qs],
        out_shape=[jax.ShapeDtypeStruct((t, D_MODEL), BF16), jax.ShapeDtypeStruct((t, D_MODEL), F32)],
        compiler_params=_params(("parallel", "arbitrary")),
    )(qkv, qkv, qkv)


def _sb_bwd(qkv, tot, do):
    t = qkv.shape[0]
    bq, bk, nd = _sb_blocks(t)
    scale = HEAD_DIM ** -0.5

    def body(q_ref, k_ref, v_ref, tot_ref, do_ref, dq_ref, dk_ref, dv_ref):
        i = pl.program_id(1)

        @pl.when(i == 0)
        def _():
            dk_ref[...] = jnp.zeros_like(dk_ref)
            dv_ref[...] = jnp.zeros_like(dv_ref)

        q = q_ref[...]
        do = do_ref[...]
        total = tot_ref[:, 0:1]
        rj = lax.broadcasted_iota(jnp.int32, (bk, bk), 0)
        cj = lax.broadcasted_iota(jnp.int32, (bk, bk), 1)
        upto = (rj <= cj).astype(BF16)
        before = (rj < cj).astype(BF16)
        trow = lax.broadcasted_iota(jnp.int32, (bq, bk), 0)
        scol = lax.broadcasted_iota(jnp.int32, (bq, bk), 1)

        def tile(j, run_k, run_e, dq, mask):
            off = pl.multiple_of(j * bk, bk)
            kb = k_ref[pl.ds(off, bk), :]
            vb = v_ref[pl.ds(off, bk), :]
            ls, lk = _sb_logits(q, kb, mask, scale)
            hi, lo = _split_bf16(lk)
            between = total - (_dot(hi, upto) + _dot(lo, upto) + run_k)
            a = jnp.exp(ls + between)
            if mask is not None:
                a = jnp.where(mask, a, 0.0)
            e = a * _dot_nt(do, vb)
            ehi, elo = _split_bf16(e)
            pre = _dot(ehi, before) + _dot(elo, before) + run_e
            sig = jnp.exp(ls)
            dz = e * (1.0 - sig) - pre * sig
            if mask is not None:
                dz = jnp.where(mask, dz, 0.0)
            dz = (dz * scale).astype(BF16)
            dq = dq + _dot(dz, kb)
            dk_ref[pl.ds(off, bk), :] += _dot_tn(dz, q)
            dv_ref[pl.ds(off, bk), :] += _dot_tn(a.astype(BF16), do)
            return (run_k + jnp.sum(lk, axis=1, keepdims=True),
                    run_e + jnp.sum(e, axis=1, keepdims=True), dq)

        zero = jnp.zeros((bq, 1), F32)
        carry = lax.fori_loop(0, i * nd, lambda j, c: tile(j, c[0], c[1], c[2], None),
                              (zero, zero, jnp.zeros((bq, HEAD_DIM), F32)))
        for d in range(nd):
            carry = tile(i * nd + d, *carry, scol + d * bk < trow)
        dq_ref[...] = carry[2]

    qs = pl.BlockSpec((bq, HEAD_DIM), lambda h, i: (i, h))
    ks = pl.BlockSpec((t, HEAD_DIM), lambda h, i: (0, N_HEADS + h))
    vs = pl.BlockSpec((t, HEAD_DIM), lambda h, i: (0, 2 * N_HEADS + h))
    full = pl.BlockSpec((t, HEAD_DIM), lambda h, i: (0, h))
    big = jax.ShapeDtypeStruct((t, D_MODEL), F32)
    return pl.pallas_call(
        body, name="sb_bwd", grid=(N_HEADS, t // bq),
        in_specs=[qs, ks, vs, qs, qs], out_specs=[qs, full, full],
        out_shape=[big, big, big],
        compiler_params=_params(("parallel", "arbitrary")),
    )(qkv, qkv, qkv, tot, do)


def _merge_fwd(o_dn, o_sb, gl, x, wp_dn, wp_sb, w_out, w2):
    t = x.shape[0]
    tb = _rows(t)

    def body(odn_ref, osb_ref, gl_ref, x_ref, wpd_ref, wps_ref, wo_ref, w2_ref,
             pdn_ref, psb_ref, mix_ref, x1_ref, n2_ref):
        pdn = _dot(odn_ref[...], wpd_ref[...])
        psb = _dot(osb_ref[...], wps_ref[...])
        gates = jax.nn.sigmoid(gl_ref[...])
        mixed = (gates[:, :D_MODEL] * pdn + gates[:, D_MODEL:] * psb).astype(BF16)
        x1 = x_ref[...] + _dot(mixed, wo_ref[...])
        pdn_ref[...] = pdn
        psb_ref[...] = psb
        mix_ref[...] = mixed
        x1_ref[...] = x1
        n2_ref[...] = _rms(x1, w2_ref[...]).astype(BF16)

    row = pl.BlockSpec((tb, D_MODEL), lambda i: (i, 0))
    sq = pl.BlockSpec((D_MODEL, D_MODEL), lambda i: (0, 0))
    f = jax.ShapeDtypeStruct((t, D_MODEL), F32)
    b = jax.ShapeDtypeStruct((t, D_MODEL), BF16)
    return pl.pallas_call(
        body, name="merge_fwd", grid=(t // tb,),
        in_specs=[row, row, pl.BlockSpec((tb, 2 * D_MODEL), lambda i: (i, 0)), row, sq, sq, sq,
                  pl.BlockSpec((1, D_MODEL), lambda i: (0, 0))],
        out_specs=[row] * 5, out_shape=[f, f, b, f, b],
        compiler_params=_params(("parallel",)),
    )(o_dn, o_sb, gl, x, wp_dn, wp_sb, w_out, w2)


def _merge_bwd(dx2, dn2, x1, w2, gl, pdn, psb, wp_dn, wp_sb, w_out):
    t = x1.shape[0]
    tb = _rows(t)

    def body(dx2_ref, dn2_ref, x1_ref, w2_ref, gl_ref, pdn_ref, psb_ref, wpd_ref, wps_ref, wo_ref,
             dx1_ref, dw2_ref, dgl_ref, dpdn_ref, dpsb_ref, dodn_ref, dosb_ref):
        i = pl.program_id(0)
        _, vjp = jax.vjp(_rms, x1_ref[...], w2_ref[...])
        dxn, dw2 = vjp(dn2_ref[...])
        dx1 = dx2_ref[...] + dxn
        dx1_ref[...] = dx1

        @pl.when(i == 0)
        def _():
            dw2_ref[...] = jnp.zeros_like(dw2_ref)

        dw2_ref[...] += dw2
        dmix = _dot_nt(dx1.astype(BF16), wo_ref[...])
        gates = jax.nn.sigmoid(gl_ref[...])
        g_dn, g_sb = gates[:, :D_MODEL], gates[:, D_MODEL:]
        dpdn = (dmix * g_dn).astype(BF16)
        dpsb = (dmix * g_sb).astype(BF16)
        dgl_ref[:, :D_MODEL] = (dmix * pdn_ref[...] * g_dn * (1.0 - g_dn)).astype(BF16)
        dgl_ref[:, D_MODEL:] = (dmix * psb_ref[...] * g_sb * (1.0 - g_sb)).astype(BF16)
        dpdn_ref[...] = dpdn
        dpsb_ref[...] = dpsb
        dodn_ref[...] = _dot_nt(dpdn, wpd_ref[...])
        dosb_ref[...] = _dot_nt(dpsb, wps_ref[...]).astype(BF16)

    row = pl.BlockSpec((tb, D_MODEL), lambda i: (i, 0))
    wide = pl.BlockSpec((tb, 2 * D_MODEL), lambda i: (i, 0))
    sq = pl.BlockSpec((D_MODEL, D_MODEL), lambda i: (0, 0))
    vec = pl.BlockSpec((1, D_MODEL), lambda i: (0, 0))
    f = jax.ShapeDtypeStruct((t, D_MODEL), F32)
    b = jax.ShapeDtypeStruct((t, D_MODEL), BF16)
    return pl.pallas_call(
        body, name="merge_bwd", grid=(t // tb,),
        in_specs=[row, row, row, vec, wide, row, row, sq, sq, sq],
        out_specs=[row, vec, wide, row, row, row, row],
        out_shape=[f, jax.ShapeDtypeStruct((1, D_MODEL), F32), jax.ShapeDtypeStruct((t, 2 * D_MODEL), BF16),
                   b, b, f, b],
        compiler_params=_params(("arbitrary",)),
    )(dx2, dn2, x1, w2, gl, pdn, psb, wp_dn, wp_sb, w_out)


def _conv_taps(buf, w_ref, first, rows):
    y = w_ref[0:1, :] * buf[pl.ds(first, rows), :]
    for s in range(1, w_ref.shape[0]):
        y = y + w_ref[s:s + 1, :] * buf[pl.ds(first + s, rows), :]
    return y


def _ffn_mid_fwd(pre_g, pre_u, wg, wu):
    t, c = pre_g.shape
    kk = wg.shape[0]
    tb, cb = _rows(t), _pick(c, ELEMENTWISE_COLS)
    per = tb // HALO

    def body(g_ref, gh_ref, u_ref, uh_ref, wg_ref, wu_ref, a_ref, gbuf, ubuf):
        i = pl.program_id(0)
        for buf, ref, halo in ((gbuf, g_ref, gh_ref), (ubuf, u_ref, uh_ref)):
            buf[pl.ds(HALO, tb), :] = ref[...]
            buf[pl.ds(0, HALO), :] = jnp.where(i == 0, 0.0, halo[...])
        ug = _conv_taps(gbuf, wg_ref, HALO - (kk - 1), tb)
        uu = _conv_taps(ubuf, wu_ref, HALO - (kk - 1), tb)
        a_ref[...] = (_silu(ug) * uu).astype(BF16)

    blk = pl.BlockSpec((tb, cb), lambda i, j: (i, j))
    halo = pl.BlockSpec((HALO, cb), lambda i, j: (jnp.maximum(i * per - 1, 0), j))
    wspec = pl.BlockSpec((kk, cb), lambda i, j: (0, j))
    return pl.pallas_call(
        body, name="ffn_mid_fwd", grid=(t // tb, c // cb),
        in_specs=[blk, halo, blk, halo, wspec, wspec], out_specs=blk,
        out_shape=jax.ShapeDtypeStruct((t, c), BF16),
        scratch_shapes=[pltpu.VMEM((tb + HALO, cb), F32)] * 2,
        compiler_params=_params(("parallel", "parallel")),
    )(pre_g, pre_g, pre_u, pre_u, wg, wu)


def _ffn_mid_bwd(pre_g, pre_u, wg, wu, da):
    t, c = pre_g.shape
    kk = wg.shape[0]
    tb, cb = _rows(t), _pick(c, ELEMENTWISE_COLS)
    per = tb // HALO
    nblk = t // tb
    ext = tb + HALO

    def body(g_ref, gb_ref, ga_ref, u_ref, ub_ref, ua_ref, da_ref, daa_ref, wg_ref, wu_ref,
             dg_ref, du_ref, dwg_ref, dwu_ref, gbuf, ubuf, dabuf, dgbuf, dubuf):
        i = pl.program_id(1)
        last = i == nblk - 1
        for buf, ref, before, after in ((gbuf, g_ref, gb_ref, ga_ref), (ubuf, u_ref, ub_ref, ua_ref)):
            buf[pl.ds(0, HALO), :] = jnp.where(i == 0, 0.0, before[...])
            buf[pl.ds(HALO, tb), :] = ref[...]
            buf[pl.ds(HALO + tb, HALO), :] = jnp.where(last, 0.0, after[...])
        dabuf[pl.ds(0, tb), :] = da_ref[...]
        dabuf[pl.ds(tb, HALO), :] = jnp.where(last, 0.0, daa_ref[...])
        ug = _conv_taps(gbuf, wg_ref, HALO - (kk - 1), ext)
        uu = _conv_taps(ubuf, wu_ref, HALO - (kk - 1), ext)
        _, vjp = jax.vjp(lambda g, u: _silu(g) * u, ug, uu)
        dgbuf[...], dubuf[...] = vjp(dabuf[...])

        @pl.when(i == 0)
        def _():
            dwg_ref[...] = jnp.zeros_like(dwg_ref)
            dwu_ref[...] = jnp.zeros_like(dwu_ref)

        for dbuf, xbuf, w_ref, dx_ref, dw_ref in ((dgbuf, gbuf, wg_ref, dg_ref, dwg_ref),
                                                  (dubuf, ubuf, wu_ref, du_ref, dwu_ref)):
            dx = w_ref[0:1, :] * dbuf[pl.ds(kk - 1, tb), :]
            for s in range(1, kk):
                dx = dx + w_ref[s:s + 1, :] * dbuf[pl.ds(kk - 1 - s, tb), :]
            dx_ref[...] = dx.astype(BF16)
            dy = dbuf[pl.ds(0, tb), :]
            for s in range(kk):
                dw_ref[s:s + 1, :] += jnp.sum(dy * xbuf[pl.ds(HALO - (kk - 1) + s, tb), :], axis=0, keepdims=True)

    blk = pl.BlockSpec((tb, cb), lambda j, i: (i, j))
    before = pl.BlockSpec((HALO, cb), lambda j, i: (jnp.maximum(i * per - 1, 0), j))
    after = pl.BlockSpec((HALO, cb), lambda j, i: (jnp.minimum((i + 1) * per, t // HALO - 1), j))
    wspec = pl.BlockSpec((kk, cb), lambda j, i: (0, j))
    dwspec = pl.BlockSpec((HALO, cb), lambda j, i: (0, j))
    half = jax.ShapeDtypeStruct((t, c), BF16)
    dwshape = jax.ShapeDtypeStruct((HALO, c), F32)
    return pl.pallas_call(
        body, name="ffn_mid_bwd", grid=(c // cb, nblk),
        in_specs=[blk, before, after, blk, before, after, blk, after, wspec, wspec],
        out_specs=[blk, blk, dwspec, dwspec],
        out_shape=[half, half, dwshape, dwshape],
        scratch_shapes=[pltpu.VMEM((ext + HALO, cb), F32)] * 2 + [pltpu.VMEM((ext, cb), F32)] * 3,
        compiler_params=_params(("parallel", "arbitrary")),
    )(pre_g, pre_g, pre_g, pre_u, pre_u, pre_u, da, da, wg, wu)


def _down_loss(a, w_down, x1, wf, target):
    t = x1.shape[0]
    tb = _rows(t)

    def body(a_ref, wd_ref, x1_ref, wf_ref, tgt_ref, dx2_ref, dwf_ref, loss_ref):
        i = pl.program_id(0)
        x2 = x1_ref[...] + _dot(a_ref[...], wd_ref[...])
        y, vjp = jax.vjp(_rms, x2, wf_ref[...])
        err = y - tgt_ref[...]
        dx2, dwf = vjp(err * (1.0 / D_MODEL))
        dx2_ref[...] = dx2
        part = jnp.sum(jnp.sum(err * err, axis=1, keepdims=True), axis=0, keepdims=True) * (0.5 / D_MODEL)

        @pl.when(i == 0)
        def _():
            dwf_ref[...] = jnp.zeros_like(dwf_ref)
            loss_ref[...] = jnp.zeros_like(loss_ref)

        dwf_ref[...] += dwf
        loss_ref[...] += jnp.broadcast_to(part, loss_ref.shape)

    row = pl.BlockSpec((tb, D_MODEL), lambda i: (i, 0))
    vec = pl.BlockSpec((1, D_MODEL), lambda i: (0, 0))
    return pl.pallas_call(
        body, name="down_loss", grid=(t // tb,),
        in_specs=[pl.BlockSpec((tb, D_FF), lambda i: (i, 0)), pl.BlockSpec((D_FF, D_MODEL), lambda i: (0, 0)),
                  row, vec, row],
        out_specs=[row, vec, pl.BlockSpec((1, LANES), lambda i: (0, 0))],
        out_shape=[jax.ShapeDtypeStruct((t, D_MODEL), F32), jax.ShapeDtypeStruct((1, D_MODEL), F32),
                   jax.ShapeDtypeStruct((1, LANES), F32)],
        compiler_params=_params(("arbitrary",)),
    )(a, w_down, x1, wf, target)


def _local_step(x, target, wts):
    t = x.shape[0]
    nchunk = t // DN_CHUNK

    n1, hab = _norm1_fwd(x, wts["norm1"], wts["w_ab"])
    dnqkv = _mm(n1, wts["w_dnqkv"], name="h_dnqkv")
    dngate = _mm(n1, wts["w_dngate"], name="h_dngate")
    sbqkv = _mm(n1, wts["w_sbqkv"], out_dtype=BF16, name="h_sbqkv")
    gl = _mm(n1, wts["w_gl"], name="h_gl")

    cdn = _conv_fwd(dnqkv, wts["dn_conv"], "dn_conv_fwd")
    qn, kn, vv, gb = _dn_prep_fwd(cdn, hab, wts["alog"], wts["dtb"])
    per_head = gb[:, :2 * N_HEADS].T.reshape(2 * N_HEADS, nchunk, DN_CHUNK)
    gcol, bcol = per_head[:N_HEADS, :, :, None], per_head[N_HEADS:, :, :, None]
    grow = per_head[:N_HEADS, :, None, :]
    u_dn, w_dn, a_qk, qe, kdec, egl, tinv = _dn_local_fwd(qn, kn, vv, gcol, grow, bcol)
    o_raw, states = _dn_seq_fwd(u_dn, w_dn, a_qk, qe, kdec, egl)
    o_dn = _dn_post_fwd(o_raw, dngate, wts["dn_norm"])

    o_sb, tot = _sb_fwd(sbqkv)

    pdn, psb, mixed, x1, n2 = _merge_fwd(o_dn, o_sb, gl, x, wts["wp_dn"], wts["wp_sb"], wts["w_out"],
                                         wts["norm2"])
    pre_g = _mm(n2, wts["w_up_g"], name="ffn_up_g")
    pre_u = _mm(n2, wts["w_up_u"], name="ffn_up_u")
    act = _ffn_mid_fwd(pre_g, pre_u, wts["ffn_conv_g"], wts["ffn_conv_u"])
    dx2, d_normf, loss_part = _down_loss(act, wts["w_down"], x1, wts["normf"], target)

    grads = {"normf": d_normf}
    da = _mm(dx2, wts["w_down"], tb=True, name="d_act")
    grads["w_down"] = _mm(act, dx2, ta=True, name="dw_down")
    dpre_g, dpre_u, dcw_g, dcw_u = _ffn_mid_bwd(pre_g, pre_u, wts["ffn_conv_g"], wts["ffn_conv_u"], da)
    grads["ffn_conv"] = jnp.concatenate([dcw_g[:FFN_CONV], dcw_u[:FFN_CONV]], axis=1)
    dn2 = _mm(dpre_g, wts["w_up_g"], tb=True, name="dn2_g")
    dn2 = _mm(dpre_u, wts["w_up_u"], tb=True, add=dn2, name="dn2_u")
    grads["w_up"] = jnp.concatenate([_mm(n2, dpre_g, ta=True, name="dw_up_g"),
                                     _mm(n2, dpre_u, ta=True, name="dw_up_u")], axis=1)

    dx1, grads["norm2"], dgl, dpdn, dpsb, do_dn, do_sb = _merge_bwd(
        dx2, dn2, x1, wts["norm2"], gl, pdn, psb, wts["wp_dn"], wts["wp_sb"], wts["w_out"])
    grads["w_out"] = _mm(mixed, dx1, ta=True, name="dw_out")
    grads["wp_dn"] = _mm(o_dn, dpdn, ta=True, name="dw_proj_dn")
    grads["wp_sb"] = _mm(o_sb, dpsb, ta=True, name="dw_proj_sb")

    dsq, dsk, dsv = _sb_bwd(sbqkv, tot, do_sb)
    dsbqkv = jnp.concatenate([dsq, dsk, dsv], axis=1).astype(BF16)

    do_raw, ddngate, grads["dn_norm"] = _dn_post_bwd(o_raw, dngate, wts["dn_norm"], do_dn)
    seq_grads = _dn_seq_bwd(u_dn, w_dn, a_qk, qe, kdec, egl, states, do_raw)
    dqn, dkn, dvv, dgcol, dgrow, dbcol = _dn_local_bwd(qn, kn, vv, gcol, grow, bcol, tinv, *seq_grads)
    dg = (dgcol[..., 0] + dgrow[:, :, 0, :]).reshape(N_HEADS, t)
    dgb = jnp.concatenate([dg, dbcol[..., 0].reshape(N_HEADS, t)], axis=0).T
    dgb = jnp.pad(dgb, ((0, 0), (0, LANES - 2 * N_HEADS)))
    dcdn, dhab, grads["alog"], grads["dtb"] = _dn_prep_bwd(cdn, hab, wts["alog"], wts["dtb"], dqn, dkn, dvv, dgb)
    ddnqkv, dcw_dn = _conv_bwd(dcdn, dnqkv, wts["dn_conv"], "dn_conv_bwd", BF16)
    grads["dn_conv"] = dcw_dn[:DN_CONV]

    dn1 = _mm(ddnqkv, wts["w_dnqkv"], tb=True, name="dn1_dnqkv")
    dn1 = _mm(ddngate, wts["w_dngate"], tb=True, add=dn1, name="dn1_dngate")
    dn1 = _mm(dsbqkv, wts["w_sbqkv"], tb=True, add=dn1, name="dn1_sbqkv")
    dn1 = _mm(dgl, wts["w_gl"], tb=True, add=dn1, name="dn1_gl")
    grads["w_dnqkv"] = _mm(n1, ddnqkv, ta=True, name="dw_dnqkv")
    grads["w_dngate"] = _mm(n1, ddngate, ta=True, name="dw_dngate")
    grads["w_sbqkv"] = _mm(n1, dsbqkv, ta=True, name="dw_sbqkv")
    grads["w_gl"] = _mm(n1, dgl, ta=True, name="dw_gl")
    grads["w_ab"] = _mm(n1, dhab, ta=True, name="dw_ab")
    grad_x, grads["norm1"] = _norm1_bwd(x, wts["norm1"], dn1, dx1, dhab, wts["w_ab"])
    return loss_part, grad_x, grads


def _place():
    return lax.axis_index("x"), lax.axis_index("y"), lax.axis_index("c")


def _gather_shards(shard):
    rows, cols = shard.shape
    half = rows // 2

    def body(in_ref, out_ref, send_sems, recv_sems):
        x, y, c = _place()
        me = 2 * x + y
        sibling = (x, y, 1 - c)
        chips = [(1 - x, y), (x, 1 - y), (1 - x, 1 - y)]

        def slab(chip_index, part):
            return out_ref.at[chip_index, pl.ds(part * half, half), :]

        def copy(k, src, dst, to):
            return pltpu.make_async_remote_copy(src_ref=src, dst_ref=dst, send_sem=send_sems.at[k],
                                                recv_sem=recv_sems.at[k], device_id=to, device_id_type=MESH)

        my_half = in_ref.at[pl.ds(c * half, half), :]
        first = [copy(j, my_half, slab(me, c), (px, py, c)) for j, (px, py) in enumerate(chips)]
        for cp in first:
            cp.start()
        passed = []
        for j, (px, py) in enumerate(chips):
            landed = slab(2 * px + py, c)
            copy(j, landed, landed, (px, py, c)).wait_recv()
            fwd = copy(3 + j, landed, landed, sibling)
            fwd.start()
            passed.append(fwd)
        for j, (px, py) in enumerate(chips):
            there = slab(2 * px + py, 1 - c)
            copy(3 + j, there, there, sibling).wait_recv()
        for cp in first + passed:
            cp.wait_send()

    return pl.pallas_call(
        body, name="gather_weights",
        in_specs=[pl.BlockSpec(memory_space=pltpu.HBM)],
        out_specs=pl.BlockSpec(memory_space=pltpu.HBM),
        out_shape=jax.ShapeDtypeStruct((N_CHIPS, rows, cols), shard.dtype),
        scratch_shapes=[pltpu.SemaphoreType.DMA((6,)), pltpu.SemaphoreType.DMA((6,))],
    )(shard)


def _pair_exchange_halves(g):
    nsh, rows, cols = g.shape
    half = rows // 2

    def body(in_ref, out_ref, send_sem, recv_sem):
        x, y, c = _place()
        src = in_ref.at[:, pl.ds((1 - c) * half, half), :]
        cp = pltpu.make_async_remote_copy(src_ref=src, dst_ref=out_ref, send_sem=send_sem, recv_sem=recv_sem,
                                          device_id=(x, y, 1 - c), device_id_type=MESH)
        cp.start()
        cp.wait()

    return pl.pallas_call(
        body, name="grad_pair_exchange",
        in_specs=[pl.BlockSpec(memory_space=pltpu.HBM)],
        out_specs=pl.BlockSpec(memory_space=pltpu.HBM),
        out_shape=jax.ShapeDtypeStruct((nsh, half, cols), g.dtype),
        scratch_shapes=[pltpu.SemaphoreType.DMA, pltpu.SemaphoreType.DMA],
    )(g)


def _pair_add(g, got, c_idx):
    nsh, rows, cols = g.shape
    half = rows // 2
    rb = _pick_rows(half)

    def body(c_ref, g_ref, got_ref, o_ref):
        o_ref[...] = (g_ref[...].astype(F32) + got_ref[...].astype(F32)).astype(BF16)

    nb = half // rb
    grid_spec = pltpu.PrefetchScalarGridSpec(
        num_scalar_prefetch=1, grid=(nsh, nb),
        in_specs=[pl.BlockSpec((1, rb, cols), lambda s, i, c_ref: (s, c_ref[0] * nb + i, 0)),
                  pl.BlockSpec((1, rb, cols), lambda s, i, c_ref: (s, i, 0))],
        out_specs=pl.BlockSpec((1, rb, cols), lambda s, i, c_ref: (s, i, 0)))
    return pl.pallas_call(
        body, name="grad_pair_add", grid_spec=grid_spec,
        out_shape=jax.ShapeDtypeStruct((nsh, half, cols), BF16),
        compiler_params=_params(("parallel", "parallel")),
    )(c_idx, g, got)


def _pick_rows(n, target=1024):
    best = 16
    for b in range(16, min(n, target) + 1, 16):
        if n % b == 0:
            best = b
    return best


def _chip_exchange(p):
    nsh, half, cols = p.shape

    def body(in_ref, out_ref, send_sems, recv_sems):
        x, y, c = _place()
        chips = [(1 - x, y), (x, 1 - y), (1 - x, 1 - y)]
        sends = []
        for j, (px, py) in enumerate(chips):
            cp = pltpu.make_async_remote_copy(src_ref=in_ref.at[2 * px + py], dst_ref=out_ref.at[j],
                                              send_sem=send_sems.at[j], recv_sem=recv_sems.at[j],
                                              device_id=(px, py, c), device_id_type=MESH)
            cp.start()
            sends.append(cp)
        for cp in sends:
            cp.wait_recv()
        for cp in sends:
            cp.wait_send()

    return pl.pallas_call(
        body, name="grad_chip_exchange",
        in_specs=[pl.BlockSpec(memory_space=pltpu.HBM)],
        out_specs=pl.BlockSpec(memory_space=pltpu.HBM),
        out_shape=jax.ShapeDtypeStruct((N_CHIPS - 1, half, cols), p.dtype),
        scratch_shapes=[pltpu.SemaphoreType.DMA((3,)), pltpu.SemaphoreType.DMA((3,))],
    )(p)


def _sum_partials(p, got, chip_idx):
    nsh, half, cols = got.shape
    rb = _pick_rows(half)

    def body(me_ref, p_ref, got_ref, o_ref):
        acc = p_ref[0].astype(F32)
        for s in range(nsh):
            acc = acc + got_ref[s].astype(F32)
        o_ref[...] = acc

    grid_spec = pltpu.PrefetchScalarGridSpec(
        num_scalar_prefetch=1, grid=(half // rb,),
        in_specs=[pl.BlockSpec((1, rb, cols), lambda i, me_ref: (me_ref[0], i, 0)),
                  pl.BlockSpec((nsh, rb, cols), lambda i, me_ref: (0, i, 0))],
        out_specs=pl.BlockSpec((rb, cols), lambda i, me_ref: (i, 0)))
    return pl.pallas_call(
        body, name="grad_sum_chips", grid_spec=grid_spec,
        out_shape=jax.ShapeDtypeStruct((half, cols), F32),
        compiler_params=_params(("parallel",)),
    )(chip_idx, p, got)


def _pair_share(r):
    half, cols = r.shape

    def body(in_ref, out_ref, send_sem, recv_sem):
        x, y, c = _place()
        cp = pltpu.make_async_remote_copy(src_ref=in_ref, dst_ref=out_ref, send_sem=send_sem,
                                          recv_sem=recv_sem, device_id=(x, y, 1 - c), device_id_type=MESH)
        cp.start()
        cp.wait()

    return pl.pallas_call(
        body, name="grad_pair_share",
        in_specs=[pl.BlockSpec(memory_space=pltpu.HBM)],
        out_specs=pl.BlockSpec(memory_space=pltpu.HBM),
        out_shape=jax.ShapeDtypeStruct((half, cols), r.dtype),
        scratch_shapes=[pltpu.SemaphoreType.DMA, pltpu.SemaphoreType.DMA],
    )(r)


def _small_allreduce(v):
    rows, cols = v.shape
    ndev = 8

    def body(in_ref, out_ref, slots, send_sems, recv_sems):
        x, y, c = _place()
        me = 4 * x + 2 * y + c
        slots[me] = in_ref[...]
        sends = []
        for k in range(1, ndev):
            peer = (x ^ (k >> 2), y ^ ((k >> 1) & 1), c ^ (k & 1))
            cp = pltpu.make_async_remote_copy(src_ref=in_ref, dst_ref=slots.at[me], send_sem=send_sems.at[k - 1],
                                              recv_sem=recv_sems.at[k - 1], device_id=peer, device_id_type=MESH)
            cp.start()
            sends.append(cp)
        for k in range(1, ndev):
            there = slots.at[me ^ k]
            pltpu.make_async_remote_copy(src_ref=there, dst_ref=there, send_sem=send_sems.at[k - 1],
                                         recv_sem=recv_sems.at[k - 1], device_id=(x, y, c),
                                         device_id_type=MESH).wait_recv()
        for cp in sends:
            cp.wait_send()
        acc = slots[0]
        for s in range(1, ndev):
            acc = acc + slots[s]
        out_ref[...] = acc

    return pl.pallas_call(
        body, name="small_allreduce",
        in_specs=[pl.BlockSpec(memory_space=pltpu.VMEM)],
        out_specs=pl.BlockSpec(memory_space=pltpu.VMEM),
        out_shape=jax.ShapeDtypeStruct((rows, cols), F32),
        scratch_shapes=[pltpu.VMEM((ndev, rows, cols), F32), pltpu.SemaphoreType.DMA((ndev - 1,)),
                        pltpu.SemaphoreType.DMA((ndev - 1,))],
    )(v)


def _adamw(w, g, m, v, name):
    r, c = w.shape
    rb = r if r <= 128 else _pick_rows_8(r, 128)
    c1 = 1.0 - ADAM_B1 ** ADAM_STEP
    c2 = 1.0 - ADAM_B2 ** ADAM_STEP

    def body(w_ref, g_ref, m_ref, v_ref, d_ref, nm_ref, nv_ref):
        gg = g_ref[...]
        nm = ADAM_B1 * m_ref[...] + (1.0 - ADAM_B1) * gg
        nv = ADAM_B2 * v_ref[...] + (1.0 - ADAM_B2) * (gg * gg)
        d_ref[...] = -ADAM_LR * ((nm / c1) / (jnp.sqrt(nv / c2) + ADAM_EPS) + ADAM_WD * w_ref[...])
        nm_ref[...] = nm
        nv_ref[...] = nv

    blk = pl.BlockSpec((rb, c), lambda i: (i, 0))
    shp = jax.ShapeDtypeStruct((r, c), F32)
    return pl.pallas_call(
        body, name=name, grid=(r // rb,), in_specs=[blk] * 4, out_specs=[blk] * 3, out_shape=[shp] * 3,
        compiler_params=_params(("parallel",)),
    )(w, g, m, v)


def _pick_rows_8(n, target):
    best = n
    for b in range(8, min(n, target) + 1, 8):
        if n % b == 0:
            best = b
    return best


W_IN_COLS = 2308
W_UP_COLS = 1408
W_DOWN_ROWS = 704
DN_CONV_COLS = 768
FFN_CONV_COLS = 1408
PROJ_ROWS = 256
ROW_TILE = 16
SEG = [("w_in", W_IN_COLS), ("wp_dn", PROJ_ROWS), ("wp_sb", PROJ_ROWS), ("w_out", PROJ_ROWS),
       ("w_up", W_UP_COLS), ("w_down", W_DOWN_ROWS), ("dn_conv", ROW_TILE), ("ffn_conv", ROW_TILE)]


def _seg_offsets():
    offs, at = {}, 0
    for nm, n in SEG:
        offs[nm] = (at, n)
        at += -(-n // ROW_TILE) * ROW_TILE
    assert at <= PACK_ROWS and PACK_ROWS % (2 * ROW_TILE) == 0
    return offs, at


PACK_OFFS, PACK_USED = _seg_offsets()


def _tile_rows(a, axis):
    n = a.shape[axis]
    pad = [(0, 0)] * a.ndim
    pad[axis] = (0, -(-n // ROW_TILE) * ROW_TILE - n)
    return jnp.pad(a, pad)


def _flat_rows(a, nrows):
    flat = a.reshape(-1)
    return jnp.pad(flat, (0, nrows * D_MODEL - flat.shape[0])).reshape(nrows, D_MODEL)


def _pack_weight_shard(w_in, wp_dn, wp_sb, w_out, w_up, w_down, dn_conv, ffn_conv):
    parts = [w_in.astype(BF16).reshape(W_IN_COLS, D_MODEL), wp_dn.astype(BF16), wp_sb.astype(BF16),
             w_out.astype(BF16), w_up.astype(BF16).reshape(W_UP_COLS, D_MODEL), w_down.astype(BF16),
             _flat_rows(lax.bitcast_convert_type(dn_conv, BF16), ROW_TILE),
             _flat_rows(lax.bitcast_convert_type(ffn_conv, BF16), ROW_TILE),
             jnp.zeros((PACK_ROWS - PACK_USED, D_MODEL), BF16)]
    return jnp.concatenate([_tile_rows(p, 0) for p in parts], axis=0)


def _unpack_weights(g):
    def seg(nm):
        at, n = PACK_OFFS[nm]
        return g[:, at:at + n, :]

    def cols(nm, ncols):
        return seg(nm).reshape(N_CHIPS, D_MODEL, ncols).transpose(1, 0, 2).reshape(D_MODEL, N_CHIPS * ncols)

    def f32_rows(nm, k, ncols):
        raw = seg(nm).reshape(N_CHIPS, -1)[:, :2 * k * ncols].reshape(N_CHIPS, k * ncols, 2)
        vals = lax.bitcast_convert_type(raw, F32).reshape(N_CHIPS, k, ncols)
        return vals.transpose(1, 0, 2).reshape(k, N_CHIPS * ncols)

    w_in = cols("w_in", W_IN_COLS)
    w_up = cols("w_up", W_UP_COLS)
    ffn_conv = f32_rows("ffn_conv", FFN_CONV, FFN_CONV_COLS)
    q_end, a_end, g_end, s_end = 3 * D_MODEL, 3 * D_MODEL + 2 * N_HEADS, 4 * D_MODEL + 2 * N_HEADS, 7 * D_MODEL + 2 * N_HEADS
    return {
        "w_dnqkv": w_in[:, :q_end],
        "w_ab": jnp.pad(w_in[:, q_end:a_end], ((0, 0), (0, LANES - 2 * N_HEADS))),
        "w_dngate": w_in[:, a_end:g_end],
        "w_sbqkv": w_in[:, g_end:s_end],
        "w_gl": w_in[:, s_end:],
        "wp_dn": seg("wp_dn").reshape(D_MODEL, D_MODEL),
        "wp_sb": seg("wp_sb").reshape(D_MODEL, D_MODEL),
        "w_out": seg("w_out").reshape(D_MODEL, D_MODEL),
        "w_up_g": w_up[:, :D_FF], "w_up_u": w_up[:, D_FF:],
        "w_down": seg("w_down").reshape(D_FF, D_MODEL),
        "dn_conv": f32_rows("dn_conv", DN_CONV, DN_CONV_COLS),
        "ffn_conv_g": ffn_conv[:, :D_FF], "ffn_conv_u": ffn_conv[:, D_FF:],
    }


def _pack_grads(gr):
    w_in = jnp.concatenate([gr["w_dnqkv"], gr["w_ab"][:, :2 * N_HEADS], gr["w_dngate"], gr["w_sbqkv"], gr["w_gl"]],
                           axis=1)

    def cols(a, ncols):
        return a.reshape(a.shape[0], N_CHIPS, ncols).transpose(1, 0, 2)

    def rows(a, nrows):
        return a.reshape(N_CHIPS, nrows, a.shape[1])

    def flat(a, nrows):
        a = a.reshape(N_CHIPS, -1)
        return jnp.pad(a, ((0, 0), (0, nrows * D_MODEL - a.shape[1]))).reshape(N_CHIPS, nrows, D_MODEL)

    parts = [cols(w_in, W_IN_COLS).reshape(N_CHIPS, W_IN_COLS, D_MODEL),
             rows(gr["wp_dn"], PROJ_ROWS), rows(gr["wp_sb"], PROJ_ROWS), rows(gr["w_out"], PROJ_ROWS),
             cols(gr["w_up"], W_UP_COLS).reshape(N_CHIPS, W_UP_COLS, D_MODEL),
             rows(gr["w_down"], W_DOWN_ROWS),
             flat(cols(gr["dn_conv"], DN_CONV_COLS), ROW_TILE), flat(cols(gr["ffn_conv"], FFN_CONV_COLS), ROW_TILE),
             jnp.zeros((N_CHIPS, PACK_ROWS - PACK_USED, D_MODEL), F32)]
    return jnp.concatenate([_tile_rows(p, 1) for p in parts], axis=1).astype(BF16)


def _unpack_grad_shard(r):
    def seg(nm):
        at, n = PACK_OFFS[nm]
        return r[at:at + n, :]

    return {
        "w_in": seg("w_in").reshape(D_MODEL, W_IN_COLS),
        "wp_dn": seg("wp_dn"), "wp_sb": seg("wp_sb"), "w_out": seg("w_out"),
        "w_up": seg("w_up").reshape(D_MODEL, W_UP_COLS),
        "w_down": seg("w_down"),
        "dn_conv": seg("dn_conv").reshape(-1)[:DN_CONV * DN_CONV_COLS].reshape(DN_CONV, DN_CONV_COLS),
        "ffn_conv": seg("ffn_conv").reshape(-1)[:FFN_CONV * FFN_CONV_COLS].reshape(FFN_CONV, FFN_CONV_COLS),
    }


def _lane_row(v):
    return jnp.pad(v.reshape(1, -1), ((0, 0), (0, LANES - v.size)))


def kernel(x, norm1_w, w_in, dn_conv_w, dn_A_log, dn_dt_bias, dn_norm_w, w_proj_dn, w_proj_sb, w_out, norm2_w, ffn_w_up, ffn_conv_w, ffn_w_down, norm_f_w, loss_target, m_norm1_w, m_w_in, m_dn_conv_w, m_dn_A_log, m_dn_dt_bias, m_dn_norm_w, m_w_proj_dn, m_w_proj_sb, m_w_out, m_norm2_w, m_ffn_w_up, m_ffn_conv_w, m_ffn_w_down, m_norm_f_w, v_norm1_w, v_w_in, v_dn_conv_w, v_dn_A_log, v_dn_dt_bias, v_dn_norm_w, v_w_proj_dn, v_w_proj_sb, v_w_out, v_norm2_w, v_ffn_w_up, v_ffn_conv_w, v_ffn_w_down, v_norm_f_w):
    shard = _pack_weight_shard(w_in[0], w_proj_dn[0], w_proj_sb[0], w_out[0], ffn_w_up[0], ffn_w_down[0],
                               dn_conv_w[0], ffn_conv_w[0])
    chip_idx = (2 * lax.axis_index("x") + lax.axis_index("y")).astype(jnp.int32)
    gathered = lax.dynamic_update_slice(_gather_shards(shard), shard[None], (chip_idx, 0, 0))
    wts = _unpack_weights(gathered)
    wts.update(norm1=norm1_w, norm2=norm2_w, normf=norm_f_w.reshape(1, D_MODEL), dn_norm=dn_norm_w,
               alog=_lane_row(dn_A_log), dtb=_lane_row(dn_dt_bias))

    loss_part, grad_x, gr = _local_step(x[0], loss_target[0], wts)

    c_idx = lax.axis_index("c").astype(jnp.int32).reshape(1)
    packed = _pack_grads(gr)
    partial_sum = _pair_add(packed, _pair_exchange_halves(packed), c_idx)
    reduced_half = _sum_partials(partial_sum, _chip_exchange(partial_sum), chip_idx.reshape(1))
    other_half = _pair_share(reduced_half)
    is_south = lax.axis_index("c") == 0
    gsh = _unpack_grad_shard(jnp.concatenate([jnp.where(is_south, reduced_half, other_half),
                                              jnp.where(is_south, other_half, reduced_half)], axis=0))

    tail = jnp.concatenate([gr["dn_norm"], gr["alog"][:, :N_HEADS], gr["dtb"][:, :N_HEADS], loss_part[:, :1]], axis=1)
    small = jnp.concatenate([gr["norm1"], gr["norm2"], gr["normf"],
                             jnp.pad(tail, ((0, 0), (0, D_MODEL - tail.shape[1]))),
                             jnp.zeros((SMALL_ROWS - 4, D_MODEL), F32)], axis=0)
    small = _small_allreduce(small)
    at = HEAD_DIM
    g_small = {"norm1_w": small[0:1], "norm2_w": small[1:2], "norm_f_w": small[2],
               "dn_norm_w": small[3:4, :at], "dn_A_log": small[3:4, at:at + N_HEADS],
               "dn_dt_bias": small[3:4, at + N_HEADS:at + 2 * N_HEADS]}
    loss = small[3, at + 2 * N_HEADS]

    big = {"w_in": (w_in, m_w_in, v_w_in, gsh["w_in"]), "dn_conv_w": (dn_conv_w, m_dn_conv_w, v_dn_conv_w, gsh["dn_conv"]),
           "w_proj_dn": (w_proj_dn, m_w_proj_dn, v_w_proj_dn, gsh["wp_dn"]),
           "w_proj_sb": (w_proj_sb, m_w_proj_sb, v_w_proj_sb, gsh["wp_sb"]),
           "w_out": (w_out, m_w_out, v_w_out, gsh["w_out"]),
           "ffn_w_up": (ffn_w_up, m_ffn_w_up, v_ffn_w_up, gsh["w_up"]),
           "ffn_conv_w": (ffn_conv_w, m_ffn_conv_w, v_ffn_conv_w, gsh["ffn_conv"]),
           "ffn_w_down": (ffn_w_down, m_ffn_w_down, v_ffn_w_down, gsh["w_down"])}
    res = {}
    for nm, (w, m, v, g) in big.items():
        d, nm_, nv_ = _adamw(w[0], g, m[0], v[0], "adamw_" + nm)
        res[nm] = (g[None], d[None], nm_[None], nv_[None])

    names = ["norm1_w", "norm2_w", "norm_f_w", "dn_norm_w", "dn_A_log", "dn_dt_bias"]
    given = {"norm1_w": (norm1_w, m_norm1_w, v_norm1_w), "norm2_w": (norm2_w, m_norm2_w, v_norm2_w),
             "norm_f_w": (norm_f_w, m_norm_f_w, v_norm_f_w), "dn_norm_w": (dn_norm_w, m_dn_norm_w, v_dn_norm_w),
             "dn_A_log": (dn_A_log, m_dn_A_log, v_dn_A_log), "dn_dt_bias": (dn_dt_bias, m_dn_dt_bias, v_dn_dt_bias)}

    def stack(k, fill):
        rows = [jnp.pad(given[nm][k].reshape(1, -1), ((0, 0), (0, D_MODEL - given[nm][k].size)),
                        constant_values=fill) for nm in names]
        return jnp.concatenate(rows + [jnp.full((SMALL_ROWS - len(names), D_MODEL), fill, F32)], axis=0)

    g_rows = jnp.concatenate(
        [jnp.pad(g_small[nm].reshape(1, -1), ((0, 0), (0, D_MODEL - g_small[nm].size))) for nm in names]
        + [jnp.zeros((SMALL_ROWS - len(names), D_MODEL), F32)], axis=0)
    d_s, m_s, v_s = _adamw(stack(0, 0.0), g_rows, stack(1, 0.0), stack(2, 1.0), "adamw_small")
    for r, nm in enumerate(names):
        shape = given[nm][0].shape
        n = given[nm][0].size
        res[nm] = (g_small[nm].reshape(shape), d_s[r, :n].reshape(shape), m_s[r, :n].reshape(shape),
                   v_s[r, :n].reshape(shape))

    order = ["norm1_w", "w_in", "dn_conv_w", "dn_A_log", "dn_dt_bias", "dn_norm_w", "w_proj_dn", "w_proj_sb",
             "w_out", "norm2_w", "ffn_w_up", "ffn_conv_w", "ffn_w_down", "norm_f_w"]
    outs = [loss, grad_x[None]]
    for k in range(4):
        outs += [res[nm][k] for nm in order]
    return tuple(outs)
```

```python
import functools

import jax
import jax.numpy as jnp
from jax import lax
from jax.experimental import pallas as pl
from jax.experimental.pallas import tpu as pltpu

F32 = jnp.float32
BF16 = jnp.bfloat16
HIGHEST = lax.Precision.HIGHEST
MESH = pl.DeviceIdType.MESH

EPS = 1e-6
D_MODEL = 1024
N_HEADS = 8
HEAD_DIM = 128
DN_CONV = 4
DN_CHUNK = 64
D_FF = 2816
FFN_CONV = 3
ADAM_LR, ADAM_B1, ADAM_B2, ADAM_EPS, ADAM_WD, ADAM_STEP = 0.001, 0.9, 0.999, 1e-08, 0.01, 10

N_CHIPS = 4
LANES = 128
HALO = 8
VMEM_LIMIT = 48 * 1024 * 1024
PACK_ROWS = 5248
SMALL_ROWS = 8


def _params(sem=None):
    return pltpu.CompilerParams(dimension_semantics=sem, vmem_limit_bytes=VMEM_LIMIT)


def _pick(n, target):
    best = None
    for b in range(LANES, min(n, target) + 1, LANES):
        if n % b == 0:
            best = b
    return best or n


ELEMENTWISE_COLS = 1408


def _rows(t, target=256):
    return min(t, target)


def _dot(a, b, precision=None):
    return lax.dot_general(a, b, (((1,), (0,)), ((), ())), precision=precision, preferred_element_type=F32)


def _dot_nt(a, b, precision=None):
    return lax.dot_general(a, b, (((1,), (1,)), ((), ())), precision=precision, preferred_element_type=F32)


def _dot_tn(a, b, precision=None):
    return lax.dot_general(a, b, (((0,), (0,)), ((), ())), precision=precision, preferred_element_type=F32)


def _rms(x, w):
    return x * lax.rsqrt(jnp.mean(x * x, axis=-1, keepdims=True) + EPS) * w


def _silu(x):
    return x * jax.nn.sigmoid(x)


def _softplus(x):
    return jnp.maximum(x, 0.0) + jnp.log(1.0 + jnp.exp(-jnp.abs(x)))


MM_BLOCK = 1408


def _mm(a, b, *, ta=False, tb=False, add=None, out_dtype=F32, name, bm=MM_BLOCK, bn=MM_BLOCK, bk=MM_BLOCK):
    m = a.shape[1] if ta else a.shape[0]
    k = a.shape[0] if ta else a.shape[1]
    n = b.shape[0] if tb else b.shape[1]
    bm, bn, bk = _pick(m, bm), _pick(n, bn), _pick(k, bk)
    nk = k // bk
    dims = (((0 if ta else 1,), (1 if tb else 0,)), ((), ()))

    def body(*refs):
        a_ref, b_ref = refs[:2]
        c_ref = refs[2] if add is not None else None
        o_ref = refs[3] if add is not None else refs[2]
        acc = refs[-1]
        kk = pl.program_id(2)
        part = lax.dot_general(a_ref[...].astype(BF16), b_ref[...].astype(BF16), dims, preferred_element_type=F32)

        def finish(r):
            if add is not None:
                r = r + c_ref[...].astype(F32)
            o_ref[...] = r.astype(out_dtype)

        if nk == 1:
            finish(part)
            return

        @pl.when(kk == 0)
        def _():
            acc[...] = part

        @pl.when(jnp.logical_and(kk > 0, kk < nk - 1))
        def _():
            acc[...] += part

        @pl.when(kk == nk - 1)
        def _():
            finish(acc[...] + part)

    a_spec = (pl.BlockSpec((bk, bm), lambda i, j, kk: (kk, i)) if ta
              else pl.BlockSpec((bm, bk), lambda i, j, kk: (i, kk)))
    b_spec = (pl.BlockSpec((bn, bk), lambda i, j, kk: (j, kk)) if tb
              else pl.BlockSpec((bk, bn), lambda i, j, kk: (kk, j)))
    o_spec = pl.BlockSpec((bm, bn), lambda i, j, kk: (i, j))
    in_specs = [a_spec, b_spec] + ([o_spec] if add is not None else [])
    args = (a, b) + ((add,) if add is not None else ())
    return pl.pallas_call(
        body, name=name, grid=(m // bm, n // bn, nk),
        in_specs=in_specs, out_specs=o_spec,
        out_shape=jax.ShapeDtypeStruct((m, n), out_dtype),
        scratch_shapes=[pltpu.VMEM((bm, bn), F32)] if nk > 1 else [],
        compiler_params=_params(("parallel", "parallel", "arbitrary")),
    )(*args)


def _norm1_fwd(x, w, w_ab):
    t = x.shape[0]
    tb = _rows(t)

    def body(x_ref, w_ref, wab_ref, n_ref, hab_ref):
        n = _rms(x_ref[...], w_ref[...]).astype(BF16)
        n_ref[...] = n
        hab_ref[...] = _dot(n, wab_ref[...])

    return pl.pallas_call(
        body, name="norm1_fwd", grid=(t // tb,),
        in_specs=[pl.BlockSpec((tb, D_MODEL), lambda i: (i, 0)),
                  pl.BlockSpec((1, D_MODEL), lambda i: (0, 0)),
                  pl.BlockSpec((D_MODEL, LANES), lambda i: (0, 0))],
        out_specs=[pl.BlockSpec((tb, D_MODEL), lambda i: (i, 0)),
                   pl.BlockSpec((tb, LANES), lambda i: (i, 0))],
        out_shape=[jax.ShapeDtypeStruct((t, D_MODEL), BF16), jax.ShapeDtypeStruct((t, LANES), F32)],
        compiler_params=_params(("arbitrary",)),
    )(x, w, w_ab)


def _norm1_bwd(x, w, dn, dres, dab, w_ab):
    t = x.shape[0]
    tb = _rows(t)

    def body(x_ref, w_ref, dn_ref, dres_ref, dab_ref, wab_ref, dx_ref, dw_ref):
        i = pl.program_id(0)
        g = dn_ref[...] + _dot_nt(dab_ref[...].astype(BF16), wab_ref[...])
        _, vjp = jax.vjp(_rms, x_ref[...], w_ref[...])
        dx, dw = vjp(g)
        dx_ref[...] = dres_ref[...] + dx

        @pl.when(i == 0)
        def _():
            dw_ref[...] = jnp.zeros_like(dw_ref)

        dw_ref[...] += dw

    row = pl.BlockSpec((tb, D_MODEL), lambda i: (i, 0))
    vec = pl.BlockSpec((1, D_MODEL), lambda i: (0, 0))
    return pl.pallas_call(
        body, name="norm1_bwd", grid=(t // tb,),
        in_specs=[row, vec, row, row, pl.BlockSpec((tb, LANES), lambda i: (i, 0)),
                  pl.BlockSpec((D_MODEL, LANES), lambda i: (0, 0))],
        out_specs=[row, vec],
        out_shape=[jax.ShapeDtypeStruct((t, D_MODEL), F32), jax.ShapeDtypeStruct((1, D_MODEL), F32)],
        compiler_params=_params(("arbitrary",)),
    )(x, w, dn, dres, dab, w_ab)


def _conv_fwd(x, w, name):
    t, c = x.shape
    kk = w.shape[0]
    tb, cb = _rows(t, 512), _pick(c, ELEMENTWISE_COLS)
    per = tb // HALO

    def body(x_ref, halo_ref, w_ref, y_ref, buf):
        i = pl.program_id(0)
        buf[pl.ds(HALO, tb), :] = x_ref[...]
        buf[pl.ds(0, HALO), :] = jnp.where(i == 0, 0.0, halo_ref[...])
        y = w_ref[0:1, :] * buf[pl.ds(HALO - (kk - 1), tb), :]
        for s in range(1, kk):
            y = y + w_ref[s:s + 1, :] * buf[pl.ds(HALO - (kk - 1) + s, tb), :]
        y_ref[...] = y

    return pl.pallas_call(
        body, name=name, grid=(t // tb, c // cb),
        in_specs=[pl.BlockSpec((tb, cb), lambda i, j: (i, j)),
                  pl.BlockSpec((HALO, cb), lambda i, j: (jnp.maximum(i * per - 1, 0), j)),
                  pl.BlockSpec((kk, cb), lambda i, j: (0, j))],
        out_specs=pl.BlockSpec((tb, cb), lambda i, j: (i, j)),
        out_shape=jax.ShapeDtypeStruct((t, c), F32),
        scratch_shapes=[pltpu.VMEM((tb + HALO, cb), F32)],
        compiler_params=_params(("parallel", "parallel")),
    )(x, x, w)


def _conv_bwd(dy, x, w, name, dx_dtype):
    t, c = x.shape
    kk = w.shape[0]
    tb, cb = _rows(t, 512), _pick(c, ELEMENTWISE_COLS)
    per = tb // HALO
    nblk = t // tb

    def body(dy_ref, after_ref, x_ref, before_ref, w_ref, dx_ref, dw_ref, dbuf, xbuf):
        i = pl.program_id(1)
        dy = dy_ref[...]
        dbuf[pl.ds(0, tb), :] = dy
        dbuf[pl.ds(tb, HALO), :] = jnp.where(i == nblk - 1, 0.0, after_ref[...])
        xbuf[pl.ds(HALO, tb), :] = x_ref[...]
        xbuf[pl.ds(0, HALO), :] = jnp.where(i == 0, 0.0, before_ref[...])
        dx = w_ref[0:1, :] * dbuf[pl.ds(kk - 1, tb), :]
        for s in range(1, kk):
            dx = dx + w_ref[s:s + 1, :] * dbuf[pl.ds(kk - 1 - s, tb), :]
        dx_ref[...] = dx.astype(dx_dtype)

        @pl.when(i == 0)
        def _():
            dw_ref[...] = jnp.zeros_like(dw_ref)

        for s in range(kk):
            part = jnp.sum(dy * xbuf[pl.ds(HALO - (kk - 1) + s, tb), :], axis=0, keepdims=True)
            dw_ref[s:s + 1, :] += part

    blk = pl.BlockSpec((tb, cb), lambda j, i: (i, j))
    return pl.pallas_call(
        body, name=name, grid=(c // cb, nblk),
        in_specs=[blk,
                  pl.BlockSpec((HALO, cb), lambda j, i: (jnp.minimum((i + 1) * per, t // HALO - 1), j)),
                  blk,
                  pl.BlockSpec((HALO, cb), lambda j, i: (jnp.maximum(i * per - 1, 0), j)),
                  pl.BlockSpec((kk, cb), lambda j, i: (0, j))],
        out_specs=[blk, pl.BlockSpec((HALO, cb), lambda j, i: (0, j))],
        out_shape=[jax.ShapeDtypeStruct((t, c), dx_dtype), jax.ShapeDtypeStruct((HALO, c), F32)],
        scratch_shapes=[pltpu.VMEM((tb + HALO, cb), F32), pltpu.VMEM((tb + HALO, cb), F32)],
        compiler_params=_params(("parallel", "arbitrary")),
    )(dy, dy, x, x, w)


def _dn_prep_fn(c, hab, alog, dtb):
    s = _silu(c)
    heads = []
    for h in range(2 * N_HEADS):
        sh = s[:, h * HEAD_DIM:(h + 1) * HEAD_DIM]
        heads.append(sh * lax.rsqrt(jnp.sum(sh * sh, axis=-1, keepdims=True) + EPS))
    qn = jnp.concatenate(heads[:N_HEADS], axis=1)
    kn = jnp.concatenate(heads[N_HEADS:], axis=1)
    v = s[:, 2 * D_MODEL:]
    lane = lax.broadcasted_iota(jnp.int32, hab.shape, 1)
    g = -jnp.exp(alog) * _softplus(hab + dtb)
    beta = jax.nn.sigmoid(hab)
    gb = jnp.where(lane < N_HEADS, g, jnp.where(lane < 2 * N_HEADS, beta, 0.0))
    return qn, kn, v, gb


def _to_heads(ref, val):
    for h in range(N_HEADS):
        ref[h] = val[:, h * HEAD_DIM:(h + 1) * HEAD_DIM]


def _from_heads(ref):
    return jnp.concatenate([ref[h] for h in range(N_HEADS)], axis=1)


def _dn_prep_fwd(c, hab, alog, dtb):
    t = c.shape[0]
    tb = _rows(t)

    def body(c_ref, hab_ref, alog_ref, dtb_ref, q_ref, k_ref, v_ref, gb_ref):
        qn, kn, v, gb = _dn_prep_fn(c_ref[...], hab_ref[...], alog_ref[...], dtb_ref[...])
        _to_heads(q_ref, qn)
        _to_heads(k_ref, kn)
        _to_heads(v_ref, v)
        gb_ref[...] = gb

    hm = pl.BlockSpec((N_HEADS, tb, HEAD_DIM), lambda i: (0, i, 0))
    nar = pl.BlockSpec((tb, LANES), lambda i: (i, 0))
    vec = pl.BlockSpec((1, LANES), lambda i: (0, 0))
    return pl.pallas_call(
        body, name="dn_prep_fwd", grid=(t // tb,),
        in_specs=[pl.BlockSpec((tb, 3 * D_MODEL), lambda i: (i, 0)), nar, vec, vec],
        out_specs=[hm, hm, hm, nar],
        out_shape=[jax.ShapeDtypeStruct((N_HEADS, t, HEAD_DIM), F32)] * 3 + [jax.ShapeDtypeStruct((t, LANES), F32)],
        compiler_params=_params(("parallel",)),
    )(c, hab, alog, dtb)


def _dn_prep_bwd(c, hab, alog, dtb, dq, dk, dv, dgb):
    t = c.shape[0]
    tb = _rows(t)

    def body(c_ref, hab_ref, alog_ref, dtb_ref, dq_ref, dk_ref, dv_ref, dgb_ref,
             dc_ref, dhab_ref, dalog_ref, ddtb_ref):
        i = pl.program_id(0)
        _, vjp = jax.vjp(_dn_prep_fn, c_ref[...], hab_ref[...], alog_ref[...], dtb_ref[...])
        dc, dhab, dalog, ddtb = vjp((_from_heads(dq_ref), _from_heads(dk_ref), _from_heads(dv_ref), dgb_ref[...]))
        dc_ref[...] = dc
        dhab_ref[...] = dhab

        @pl.when(i == 0)
        def _():
            dalog_ref[...] = jnp.zeros_like(dalog_ref)
            ddtb_ref[...] = jnp.zeros_like(ddtb_ref)

        dalog_ref[...] += dalog
        ddtb_ref[...] += ddtb

    hm = pl.BlockSpec((N_HEADS, tb, HEAD_DIM), lambda i: (0, i, 0))
    wide = pl.BlockSpec((tb, 3 * D_MODEL), lambda i: (i, 0))
    nar = pl.BlockSpec((tb, LANES), lambda i: (i, 0))
    vec = pl.BlockSpec((1, LANES), lambda i: (0, 0))
    return pl.pallas_call(
        body, name="dn_prep_bwd", grid=(t // tb,),
        in_specs=[wide, nar, vec, vec, hm, hm, hm, nar],
        out_specs=[wide, nar, vec, vec],
        out_shape=[jax.ShapeDtypeStruct((t, 3 * D_MODEL), F32), jax.ShapeDtypeStruct((t, LANES), F32),
                   jax.ShapeDtypeStruct((1, LANES), F32), jax.ShapeDtypeStruct((1, LANES), F32)],
        compiler_params=_params(("arbitrary",)),
    )(c, hab, alog, dtb, dq, dk, dv, dgb)


DN_PREC = lax.Precision.HIGH
DN_GROUP = 8


def _dn_prec(a):
    return DN_PREC if a.dtype == F32 else None


def _bdot(a, b):
    return lax.dot_general(a, b, (((2,), (1,)), ((0,), (0,))), precision=_dn_prec(a), preferred_element_type=F32)


def _bdot_nt(a, b):
    return lax.dot_general(a, b, (((2,), (2,)), ((0,), (0,))), precision=_dn_prec(a), preferred_element_type=F32)


def _bdot_tn(a, b):
    return lax.dot_general(a, b, (((1,), (1,)), ((0,), (0,))), precision=_dn_prec(a), preferred_element_type=F32)


def _unit_lower_inverse(lmat):
    c = lmat.shape[-1]
    ri = lax.broadcasted_iota(jnp.int32, (c, c), 0)
    ci = lax.broadcasted_iota(jnp.int32, (c, c), 1)
    p = -lmat
    tinv = jnp.where(ri == ci, 1.0, 0.0) + p
    for _ in range(max(c.bit_length() - 2, 0)):
        p = _bdot(p, p)
        tinv = tinv + _bdot(tinv, p)
    return tinv


@jax.custom_vjp
def _solve_with(lmat, rhs, tinv):
    return _bdot(tinv, rhs)


def _solve_with_fwd(lmat, rhs, tinv):
    sol = _bdot(tinv, rhs)
    return sol, (sol, tinv)


def _solve_with_bwd(res, dsol):
    sol, tinv = res
    drhs = _bdot_tn(tinv, dsol)
    return -_bdot_nt(drhs, sol), drhs, jnp.zeros_like(tinv)


_solve_with.defvjp(_solve_with_fwd, _solve_with_bwd)


def _dn_local(q, k, v, gcol, grow, bcol, tinv):
    g, c, _ = q.shape
    ri = lax.broadcasted_iota(jnp.int32, (c, c), 0)
    ci = lax.broadcasted_iota(jnp.int32, (c, c), 1)
    lower = ri >= ci
    gc_col = jnp.sum(jnp.where(lower, jnp.broadcast_to(grow, (g, c, c)), 0.0), axis=2, keepdims=True)
    gc_row = jnp.sum(jnp.where(ri <= ci, jnp.broadcast_to(gcol, (g, c, c)), 0.0), axis=1, keepdims=True)
    qs = q * (HEAD_DIM ** -0.5)
    kb = k * bcol
    vb = v * bcol
    decay = jnp.where(lower, jnp.exp(jnp.where(lower, gc_col - gc_row, 0.0)), 0.0)
    lmat = jnp.where(ri > ci, _bdot_nt(kb, k) * decay, 0.0)
    eg = jnp.exp(gc_col)
    rhs = jnp.concatenate([vb, kb * eg], axis=2)
    if tinv is None:
        tinv = _unit_lower_inverse(lmat)
    sol = _solve_with(lmat, rhs, tinv)
    a_qk = jnp.where(lower, _bdot_nt(qs, k) * decay, 0.0)
    g_last = jnp.sum(grow, axis=2, keepdims=True)
    kdec = k * jnp.exp(g_last - gc_col)
    egl = jnp.broadcast_to(jnp.exp(g_last), (g, 1, HEAD_DIM))
    return sol[:, :, :HEAD_DIM], sol[:, :, HEAD_DIM:], a_qk, qs * eg, kdec, egl, tinv


def _dn_seq(u, w, a_qk, qe, kdec, egl, s_in):
    b16 = lambda x: x.astype(BF16)
    v_new = u - _bdot(b16(w), b16(s_in))
    o = _bdot(b16(qe), b16(s_in)) + _bdot(b16(a_qk), b16(v_new))
    return o, s_in * egl + _bdot_tn(b16(kdec), b16(v_new))


def _dn_local_specs(t):
    grp = min(DN_GROUP, t // DN_CHUNK)
    rows = grp * DN_CHUNK
    blk = pl.BlockSpec((1, rows, HEAD_DIM), lambda h, i: (h, i, 0))
    col = pl.BlockSpec((1, grp, DN_CHUNK, 1), lambda h, i: (h, i, 0, 0))
    row = pl.BlockSpec((1, grp, 1, DN_CHUNK), lambda h, i: (h, i, 0, 0))
    sq = pl.BlockSpec((1, grp, DN_CHUNK, DN_CHUNK), lambda h, i: (h, i, 0, 0))
    lane = pl.BlockSpec((1, grp, 1, HEAD_DIM), lambda h, i: (h, i, 0, 0))
    return grp, blk, col, row, sq, lane


def _dn_shapes(t):
    nchunk = t // DN_CHUNK
    big = jax.ShapeDtypeStruct((N_HEADS, t, HEAD_DIM), F32)
    col = jax.ShapeDtypeStruct((N_HEADS, nchunk, DN_CHUNK, 1), F32)
    row = jax.ShapeDtypeStruct((N_HEADS, nchunk, 1, DN_CHUNK), F32)
    sq = jax.ShapeDtypeStruct((N_HEADS, nchunk, DN_CHUNK, DN_CHUNK), F32)
    lane = jax.ShapeDtypeStruct((N_HEADS, nchunk, 1, HEAD_DIM), F32)
    return big, col, row, sq, lane


def _dn_local_fwd(q, k, v, gcol, grow, bcol):
    t = q.shape[1]
    grp, blk, col, row, sq, lane = _dn_local_specs(t)
    big, _, _, sqs, lanes = _dn_shapes(t)

    def body(q_ref, k_ref, v_ref, gc_ref, gr_ref, bc_ref, u_ref, w_ref, a_ref, qe_ref, kd_ref, egl_ref, t_ref):
        split = lambda r: r[0].reshape(grp, DN_CHUNK, HEAD_DIM)
        u, w, a_qk, qe, kdec, egl, tinv = _dn_local(split(q_ref), split(k_ref), split(v_ref), gc_ref[0],
                                                     gr_ref[0], bc_ref[0], None)
        for ref, val in ((u_ref, u), (w_ref, w), (qe_ref, qe), (kd_ref, kdec)):
            ref[0] = val.reshape(grp * DN_CHUNK, HEAD_DIM)
        a_ref[0] = a_qk
        egl_ref[0] = egl
        t_ref[0] = tinv

    return pl.pallas_call(
        body, name="dn_local_fwd", grid=(N_HEADS, t // (grp * DN_CHUNK)),
        in_specs=[blk, blk, blk, col, row, col],
        out_specs=[blk, blk, sq, blk, blk, lane, sq],
        out_shape=[big, big, sqs, big, big, lanes, sqs],
        compiler_params=_params(("parallel", "parallel")),
    )(q, k, v, gcol, grow, bcol)


def _dn_local_bwd(q, k, v, gcol, grow, bcol, tinv, du, dw, da, dqe, dkd, degl):
    t = q.shape[1]
    grp, blk, col, row, sq, lane = _dn_local_specs(t)
    big, cols, rows_, _, _ = _dn_shapes(t)

    def body(q_ref, k_ref, v_ref, gc_ref, gr_ref, bc_ref, t_ref, du_ref, dw_ref, da_ref, dqe_ref, dkd_ref,
             degl_ref, dq_ref, dk_ref, dv_ref, dgc_ref, dgr_ref, dbc_ref):
        split = lambda r: r[0].reshape(grp, DN_CHUNK, HEAD_DIM)
        tinv_v = t_ref[0]
        fn = lambda q_, k_, v_, gc_, gr_, bc_: _dn_local(q_, k_, v_, gc_, gr_, bc_, tinv_v)[:6]
        _, vjp = jax.vjp(fn, split(q_ref), split(k_ref), split(v_ref), gc_ref[0], gr_ref[0], bc_ref[0])
        dq, dk, dv, dgc, dgr, dbc = vjp((split(du_ref), split(dw_ref), da_ref[0], split(dqe_ref), split(dkd_ref),
                                         degl_ref[0]))
        for ref, val in ((dq_ref, dq), (dk_ref, dk), (dv_ref, dv)):
            ref[0] = val.reshape(grp * DN_CHUNK, HEAD_DIM)
        dgc_ref[0] = dgc
        dgr_ref[0] = dgr
        dbc_ref[0] = dbc

    return pl.pallas_call(
        body, name="dn_local_bwd", grid=(N_HEADS, t // (grp * DN_CHUNK)),
        in_specs=[blk, blk, blk, col, row, col, sq, blk, blk, sq, blk, blk, lane],
        out_specs=[blk, blk, blk, col, row, col],
        out_shape=[big, big, big, cols, rows_, cols],
        compiler_params=_params(("parallel", "parallel")),
    )(q, k, v, gcol, grow, bcol, tinv, du, dw, da, dqe, dkd, degl)


def _dn_seq_specs(nchunk, rev):
    def idx(n):
        return nchunk - 1 - n if rev else n

    blk = pl.BlockSpec((N_HEADS, DN_CHUNK, HEAD_DIM), lambda n: (0, idx(n), 0))
    sq = pl.BlockSpec((N_HEADS, 1, DN_CHUNK, DN_CHUNK), lambda n: (0, idx(n), 0, 0))
    lane = pl.BlockSpec((N_HEADS, 1, 1, HEAD_DIM), lambda n: (0, idx(n), 0, 0))
    st = pl.BlockSpec((N_HEADS, 1, HEAD_DIM, HEAD_DIM), lambda n: (0, idx(n), 0, 0))
    return blk, sq, lane, st


def _dn_seq_fwd(u, w, a_qk, qe, kdec, egl):
    t = u.shape[1]
    nchunk = t // DN_CHUNK
    blk, sq, lane, st = _dn_seq_specs(nchunk, False)

    def body(u_ref, w_ref, a_ref, qe_ref, kd_ref, egl_ref, o_ref, s_ref, state):
        @pl.when(pl.program_id(0) == 0)
        def _():
            state[...] = jnp.zeros_like(state)

        s_in = state[...]
        s_ref[:, 0] = s_in
        o, s_out = _dn_seq(u_ref[...], w_ref[...], a_ref[:, 0], qe_ref[...], kd_ref[...], egl_ref[:, 0], s_in)
        o_ref[...] = o
        state[...] = s_out

    return pl.pallas_call(
        body, name="dn_seq_fwd", grid=(nchunk,),
        in_specs=[blk, blk, sq, blk, blk, lane],
        out_specs=[blk, st],
        out_shape=[jax.ShapeDtypeStruct((N_HEADS, t, HEAD_DIM), F32),
                   jax.ShapeDtypeStruct((N_HEADS, nchunk, HEAD_DIM, HEAD_DIM), F32)],
        scratch_shapes=[pltpu.VMEM((N_HEADS, HEAD_DIM, HEAD_DIM), F32)],
        compiler_params=_params(("arbitrary",)),
    )(u, w, a_qk, qe, kdec, egl)


def _dn_seq_bwd(u, w, a_qk, qe, kdec, egl, states, do):
    t = u.shape[1]
    nchunk = t // DN_CHUNK
    blk, sq, lane, st = _dn_seq_specs(nchunk, True)
    big, _, _, sqs, lanes = _dn_shapes(t)

    def body(u_ref, w_ref, a_ref, qe_ref, kd_ref, egl_ref, s_ref, do_ref,
             du_ref, dw_ref, da_ref, dqe_ref, dkd_ref, degl_ref, dstate):
        @pl.when(pl.program_id(0) == 0)
        def _():
            dstate[...] = jnp.zeros_like(dstate)

        _, vjp = jax.vjp(_dn_seq, u_ref[...], w_ref[...], a_ref[:, 0], qe_ref[...], kd_ref[...], egl_ref[:, 0],
                         s_ref[:, 0])
        du, dw, da, dqe, dkd, degl, ds = vjp((do_ref[...], dstate[...]))
        du_ref[...] = du
        dw_ref[...] = dw
        da_ref[:, 0] = da
        dqe_ref[...] = dqe
        dkd_ref[...] = dkd
        degl_ref[:, 0] = degl
        dstate[...] = ds

    return pl.pallas_call(
        body, name="dn_seq_bwd", grid=(nchunk,),
        in_specs=[blk, blk, sq, blk, blk, lane, st, blk],
        out_specs=[blk, blk, sq, blk, blk, lane],
        out_shape=[big, big, sqs, big, big, lanes],
        scratch_shapes=[pltpu.VMEM((N_HEADS, HEAD_DIM, HEAD_DIM), F32)],
        compiler_params=_params(("arbitrary",)),
    )(u, w, a_qk, qe, kdec, egl, states, do)


def _dn_post_fn(o, gate, w):
    outs = []
    for h in range(N_HEADS):
        sl = slice(h * HEAD_DIM, (h + 1) * HEAD_DIM)
        outs.append(_rms(o[:, sl], w) * _silu(gate[:, sl]))
    return jnp.concatenate(outs, axis=1)


def _dn_post_fwd(o, gate, w):
    t = gate.shape[0]
    tb = _rows(t)

    def body(o_ref, g_ref, w_ref, y_ref):
        y_ref[...] = _dn_post_fn(_from_heads(o_ref), g_ref[...], w_ref[...]).astype(BF16)

    row = pl.BlockSpec((tb, D_MODEL), lambda i: (i, 0))
    hm = pl.BlockSpec((N_HEADS, tb, HEAD_DIM), lambda i: (0, i, 0))
    return pl.pallas_call(
        body, name="dn_post_fwd", grid=(t // tb,),
        in_specs=[hm, row, pl.BlockSpec((1, HEAD_DIM), lambda i: (0, 0))],
        out_specs=row, out_shape=jax.ShapeDtypeStruct((t, D_MODEL), BF16),
        compiler_params=_params(("parallel",)),
    )(o, gate, w)


def _dn_post_bwd(o, gate, w, dy):
    t = gate.shape[0]
    tb = _rows(t)

    def body(o_ref, g_ref, w_ref, dy_ref, do_ref, dg_ref, dw_ref):
        i = pl.program_id(0)
        _, vjp = jax.vjp(_dn_post_fn, _from_heads(o_ref), g_ref[...], w_ref[...])
        do, dg, dw = vjp(dy_ref[...])
        _to_heads(do_ref, do)
        dg_ref[...] = dg.astype(BF16)

        @pl.when(i == 0)
        def _():
            dw_ref[...] = jnp.zeros_like(dw_ref)

        dw_ref[...] += dw

    row = pl.BlockSpec((tb, D_MODEL), lambda i: (i, 0))
    hm = pl.BlockSpec((N_HEADS, tb, HEAD_DIM), lambda i: (0, i, 0))
    vec = pl.BlockSpec((1, HEAD_DIM), lambda i: (0, 0))
    return pl.pallas_call(
        body, name="dn_post_bwd", grid=(t // tb,),
        in_specs=[hm, row, vec, row],
        out_specs=[hm, row, vec],
        out_shape=[jax.ShapeDtypeStruct((N_HEADS, t, HEAD_DIM), F32), jax.ShapeDtypeStruct((t, D_MODEL), BF16),
                   jax.ShapeDtypeStruct((1, HEAD_DIM), F32)],
        compiler_params=_params(("arbitrary",)),
    )(o, gate, w, dy)


def _split_bf16(x):
    hi = x.astype(BF16)
    lo = (x - hi.astype(F32)).astype(BF16)
    return hi, lo


SB_Q_BLOCK = 512
SB_K_BLOCK = 256
SB_NEGLIGIBLE = -60.0


def _sb_logits(q, kb, mask, scale):
    z = _dot_nt(q, kb) * scale
    ls = jnp.minimum(z, 0.0) - jnp.log(1.0 + jnp.exp(-jnp.abs(z)))
    lk = ls - z
    if mask is not None:
        lk = jnp.where(mask, lk, 0.0)
    return ls, lk


def _sb_blocks(t):
    bq = min(SB_Q_BLOCK, t)
    bk = min(SB_K_BLOCK, bq)
    return bq, bk, bq // bk


def _sb_fwd(qkv):
    t = qkv.shape[0]
    bq, bk, nd = _sb_blocks(t)
    scale = HEAD_DIM ** -0.5

    def body(q_ref, k_ref, v_ref, o_ref, tot_ref, used_ref):
        i = pl.program_id(1)
        q = q_ref[...]
        rj = lax.broadcasted_iota(jnp.int32, (bk, bk), 0)
        cj = lax.broadcasted_iota(jnp.int32, (bk, bk), 1)
        after = (rj > cj).astype(BF16)
        trow = lax.broadcasted_iota(jnp.int32, (bq, bk), 0)
        scol = lax.broadcasted_iota(jnp.int32, (bq, bk), 1)

        def tile(j, run, acc, mask):
            off = pl.multiple_of(j * bk, bk)
            kb = k_ref[pl.ds(off, bk), :]
            vb = v_ref[pl.ds(off, bk), :]
            ls, lk = _sb_logits(q, kb, mask, scale)
            hi, lo = _split_bf16(lk)
            between = _dot(hi, after) + _dot(lo, after) + run
            a = jnp.exp(ls + between)
            if mask is not None:
                a = jnp.where(mask, a, 0.0)
            acc = acc + _dot(a.astype(BF16), vb)
            return run + jnp.sum(lk, axis=1, keepdims=True), acc

        run, acc = jnp.zeros((bq, 1), F32), jnp.zeros((bq, HEAD_DIM), F32)
        for d in reversed(range(nd)):
            run, acc = tile(i * nd + d, run, acc, scol + d * bk < trow)
        def more(c):
            return jnp.logical_and(c[0] < i * nd, jnp.max(c[1]) > SB_NEGLIGIBLE)

        def far(c):
            run_, acc_ = tile(i * nd - 1 - c[0], c[1], c[2], None)
            return c[0] + 1, run_, acc_

        used, run, acc = lax.while_loop(more, far, (jnp.int32(0), run, acc))
        o_ref[...] = acc.astype(BF16)
        tot_ref[...] = jnp.broadcast_to(run, (bq, HEAD_DIM))
        used_ref[...] = jnp.full(used_ref.shape, used, F32)

    qs = pl.BlockSpec((bq, HEAD_DIM), lambda h, i: (i, h))
    ks = pl.BlockSpec((t, HEAD_DIM), lambda h, i: (0, N_HEADS + h))
    vs = pl.BlockSpec((t, HEAD_DIM), lambda h, i: (0, 2 * N_HEADS + h))
    return pl.pallas_call(
        body, name="sb_fwd", grid=(N_HEADS, t // bq),
        in_specs=[qs, ks, vs], out_specs=[qs, qs, pl.BlockSpec((1, 1, 1, LANES), lambda h, i: (h, i, 0, 0))],
        out_shape=[jax.ShapeDtypeStruct((t, D_MODEL), BF16), jax.ShapeDtypeStruct((t, D_MODEL), F32),
                   jax.ShapeDtypeStruct((N_HEADS, t // bq, 1, LANES), F32)],
        compiler_params=_params(("parallel", "arbitrary")),
    )(qkv, qkv, qkv)


def _sb_bwd(qkv, tot, used, do):
    t = qkv.shape[0]
    bq, bk, nd = _sb_blocks(t)
    scale = HEAD_DIM ** -0.5

    def body(q_ref, k_ref, v_ref, tot_ref, used_ref, do_ref, dq_ref, dk_ref, dv_ref):
        i = pl.program_id(1)

        @pl.when(i == 0)
        def _():
            dk_ref[...] = jnp.zeros_like(dk_ref)
            dv_ref[...] = jnp.zeros_like(dv_ref)

        q = q_ref[...]
        do = do_ref[...]
        total = tot_ref[:, 0:1]
        rj = lax.broadcasted_iota(jnp.int32, (bk, bk), 0)
        cj = lax.broadcasted_iota(jnp.int32, (bk, bk), 1)
        upto = (rj <= cj).astype(BF16)
        before = (rj < cj).astype(BF16)
        trow = lax.broadcasted_iota(jnp.int32, (bq, bk), 0)
        scol = lax.broadcasted_iota(jnp.int32, (bq, bk), 1)

        def tile(j, run_k, run_e, dq, mask):
            off = pl.multiple_of(j * bk, bk)
            kb = k_ref[pl.ds(off, bk), :]
            vb = v_ref[pl.ds(off, bk), :]
            ls, lk = _sb_logits(q, kb, mask, scale)
            hi, lo = _split_bf16(lk)
            between = total - (_dot(hi, upto) + _dot(lo, upto) + run_k)
            a = jnp.exp(ls + between)
            if mask is not None:
                a = jnp.where(mask, a, 0.0)
            e = a * _dot_nt(do, vb)
            ehi, elo = _split_bf16(e)
            pre = _dot(ehi, before) + _dot(elo, before) + run_e
            sig = jnp.exp(ls)
            dz = e * (1.0 - sig) - pre * sig
            if mask is not None:
                dz = jnp.where(mask, dz, 0.0)
            dz = (dz * scale).astype(BF16)
            dq = dq + _dot(dz, kb)
            dk_ref[pl.ds(off, bk), :] += _dot_tn(dz, q)
            dv_ref[pl.ds(off, bk), :] += _dot_tn(a.astype(BF16), do)
            return (run_k + jnp.sum(lk, axis=1, keepdims=True),
                    run_e + jnp.sum(e, axis=1, keepdims=True), dq)

        zero = jnp.zeros((bq, 1), F32)
        visited = jnp.clip(jnp.max(used_ref[...]).astype(jnp.int32), 0, i * nd)
        carry = lax.fori_loop(i * nd - visited, i * nd, lambda j, c: tile(j, c[0], c[1], c[2], None),
                              (zero, zero, jnp.zeros((bq, HEAD_DIM), F32)))
        for d in range(nd):
            carry = tile(i * nd + d, *carry, scol + d * bk < trow)
        dq_ref[...] = carry[2]

    qs = pl.BlockSpec((bq, HEAD_DIM), lambda h, i: (i, h))
    ks = pl.BlockSpec((t, HEAD_DIM), lambda h, i: (0, N_HEADS + h))
    vs = pl.BlockSpec((t, HEAD_DIM), lambda h, i: (0, 2 * N_HEADS + h))
    full = pl.BlockSpec((t, HEAD_DIM), lambda h, i: (0, h))
    big = jax.ShapeDtypeStruct((t, D_MODEL), F32)
    return pl.pallas_call(
        body, name="sb_bwd", grid=(N_HEADS, t // bq),
        in_specs=[qs, ks, vs, qs, pl.BlockSpec((1, 1, 1, LANES), lambda h, i: (h, i, 0, 0)), qs],
        out_specs=[qs, full, full],
        out_shape=[big, big, big],
        compiler_params=_params(("parallel", "arbitrary")),
    )(qkv, qkv, qkv, tot, used, do)


def _merge_fwd(o_dn, o_sb, gl, x, wp_dn, wp_sb, w_out, w2):
    t = x.shape[0]
    tb = _rows(t)

    def body(odn_ref, osb_ref, gl_ref, x_ref, wpd_ref, wps_ref, wo_ref, w2_ref,
             pdn_ref, psb_ref, mix_ref, x1_ref, n2_ref):
        pdn = _dot(odn_ref[...], wpd_ref[...])
        psb = _dot(osb_ref[...], wps_ref[...])
        gates = jax.nn.sigmoid(gl_ref[...])
        mixed = (gates[:, :D_MODEL] * pdn + gates[:, D_MODEL:] * psb).astype(BF16)
        x1 = x_ref[...] + _dot(mixed, wo_ref[...])
        pdn_ref[...] = pdn
        psb_ref[...] = psb
        mix_ref[...] = mixed
        x1_ref[...] = x1
        n2_ref[...] = _rms(x1, w2_ref[...]).astype(BF16)

    row = pl.BlockSpec((tb, D_MODEL), lambda i: (i, 0))
    sq = pl.BlockSpec((D_MODEL, D_MODEL), lambda i: (0, 0))
    f = jax.ShapeDtypeStruct((t, D_MODEL), F32)
    b = jax.ShapeDtypeStruct((t, D_MODEL), BF16)
    return pl.pallas_call(
        body, name="merge_fwd", grid=(t // tb,),
        in_specs=[row, row, pl.BlockSpec((tb, 2 * D_MODEL), lambda i: (i, 0)), row, sq, sq, sq,
                  pl.BlockSpec((1, D_MODEL), lambda i: (0, 0))],
        out_specs=[row] * 5, out_shape=[f, f, b, f, b],
        compiler_params=_params(("parallel",)),
    )(o_dn, o_sb, gl, x, wp_dn, wp_sb, w_out, w2)


def _merge_bwd(dx2, dn2, x1, w2, gl, pdn, psb, wp_dn, wp_sb, w_out):
    t = x1.shape[0]
    tb = _rows(t)

    def body(dx2_ref, dn2_ref, x1_ref, w2_ref, gl_ref, pdn_ref, psb_ref, wpd_ref, wps_ref, wo_ref,
             dx1_ref, dw2_ref, dgl_ref, dpdn_ref, dpsb_ref, dodn_ref, dosb_ref):
        i = pl.program_id(0)
        _, vjp = jax.vjp(_rms, x1_ref[...], w2_ref[...])
        dxn, dw2 = vjp(dn2_ref[...])
        dx1 = dx2_ref[...] + dxn
        dx1_ref[...] = dx1

        @pl.when(i == 0)
        def _():
            dw2_ref[...] = jnp.zeros_like(dw2_ref)

        dw2_ref[...] += dw2
        dmix = _dot_nt(dx1.astype(BF16), wo_ref[...])
        gates = jax.nn.sigmoid(gl_ref[...])
        g_dn, g_sb = gates[:, :D_MODEL], gates[:, D_MODEL:]
        dpdn = (dmix * g_dn).astype(BF16)
        dpsb = (dmix * g_sb).astype(BF16)
        dgl_ref[:, :D_MODEL] = (dmix * pdn_ref[...] * g_dn * (1.0 - g_dn)).astype(BF16)
        dgl_ref[:, D_MODEL:] = (dmix * psb_ref[...] * g_sb * (1.0 - g_sb)).astype(BF16)
        dpdn_ref[...] = dpdn
        dpsb_ref[...] = dpsb
        dodn_ref[...] = _dot_nt(dpdn, wpd_ref[...])
        dosb_ref[...] = _dot_nt(dpsb, wps_ref[...]).astype(BF16)

    row = pl.BlockSpec((tb, D_MODEL), lambda i: (i, 0))
    wide = pl.BlockSpec((tb, 2 * D_MODEL), lambda i: (i, 0))
    sq = pl.BlockSpec((D_MODEL, D_MODEL), lambda i: (0, 0))
    vec = pl.BlockSpec((1, D_MODEL), lambda i: (0, 0))
    f = jax.ShapeDtypeStruct((t, D_MODEL), F32)
    b = jax.ShapeDtypeStruct((t, D_MODEL), BF16)
    return pl.pallas_call(
        body, name="merge_bwd", grid=(t // tb,),
        in_specs=[row, row, row, vec, wide, row, row, sq, sq, sq],
        out_specs=[row, vec, wide, row, row, row, row],
        out_shape=[f, jax.ShapeDtypeStruct((1, D_MODEL), F32), jax.ShapeDtypeStruct((t, 2 * D_MODEL), BF16),
                   b, b, f, b],
        compiler_params=_params(("arbitrary",)),
    )(dx2, dn2, x1, w2, gl, pdn, psb, wp_dn, wp_sb, w_out)


def _conv_taps(buf, w_ref, first, rows):
    y = w_ref[0:1, :] * buf[pl.ds(first, rows), :]
    for s in range(1, w_ref.shape[0]):
        y = y + w_ref[s:s + 1, :] * buf[pl.ds(first + s, rows), :]
    return y


def _ffn_mid_fwd(pre_g, pre_u, wg, wu):
    t, c = pre_g.shape
    kk = wg.shape[0]
    tb, cb = _rows(t), _pick(c, ELEMENTWISE_COLS)
    per = tb // HALO

    def body(g_ref, gh_ref, u_ref, uh_ref, wg_ref, wu_ref, a_ref, gbuf, ubuf):
        i = pl.program_id(0)
        for buf, ref, halo in ((gbuf, g_ref, gh_ref), (ubuf, u_ref, uh_ref)):
            buf[pl.ds(HALO, tb), :] = ref[...]
            buf[pl.ds(0, HALO), :] = jnp.where(i == 0, 0.0, halo[...])
        ug = _conv_taps(gbuf, wg_ref, HALO - (kk - 1), tb)
        uu = _conv_taps(ubuf, wu_ref, HALO - (kk - 1), tb)
        a_ref[...] = (_silu(ug) * uu).astype(BF16)

    blk = pl.BlockSpec((tb, cb), lambda i, j: (i, j))
    halo = pl.BlockSpec((HALO, cb), lambda i, j: (jnp.maximum(i * per - 1, 0), j))
    wspec = pl.BlockSpec((kk, cb), lambda i, j: (0, j))
    return pl.pallas_call(
        body, name="ffn_mid_fwd", grid=(t // tb, c // cb),
        in_specs=[blk, halo, blk, halo, wspec, wspec], out_specs=blk,
        out_shape=jax.ShapeDtypeStruct((t, c), BF16),
        scratch_shapes=[pltpu.VMEM((tb + HALO, cb), F32)] * 2,
        compiler_params=_params(("parallel", "parallel")),
    )(pre_g, pre_g, pre_u, pre_u, wg, wu)


def _ffn_mid_bwd(pre_g, pre_u, wg, wu, da):
    t, c = pre_g.shape
    kk = wg.shape[0]
    tb, cb = _rows(t), _pick(c, ELEMENTWISE_COLS)
    per = tb // HALO
    nblk = t // tb
    ext = tb + HALO

    def body(g_ref, gb_ref, ga_ref, u_ref, ub_ref, ua_ref, da_ref, daa_ref, wg_ref, wu_ref,
             dg_ref, du_ref, dwg_ref, dwu_ref, gbuf, ubuf, dabuf, dgbuf, dubuf):
        i = pl.program_id(1)
        last = i == nblk - 1
        for buf, ref, before, after in ((gbuf, g_ref, gb_ref, ga_ref), (ubuf, u_ref, ub_ref, ua_ref)):
            buf[pl.ds(0, HALO), :] = jnp.where(i == 0, 0.0, before[...])
            buf[pl.ds(HALO, tb), :] = ref[...]
            buf[pl.ds(HALO + tb, HALO), :] = jnp.where(last, 0.0, after[...])
        dabuf[pl.ds(0, tb), :] = da_ref[...]
        dabuf[pl.ds(tb, HALO), :] = jnp.where(last, 0.0, daa_ref[...])
        ug = _conv_taps(gbuf, wg_ref, HALO - (kk - 1), ext)
        uu = _conv_taps(ubuf, wu_ref, HALO - (kk - 1), ext)
        _, vjp = jax.vjp(lambda g, u: _silu(g) * u, ug, uu)
        dgbuf[...], dubuf[...] = vjp(dabuf[...])

        @pl.when(i == 0)
        def _():
            dwg_ref[...] = jnp.zeros_like(dwg_ref)
            dwu_ref[...] = jnp.zeros_like(dwu_ref)

        for dbuf, xbuf, w_ref, dx_ref, dw_ref in ((dgbuf, gbuf, wg_ref, dg_ref, dwg_ref),
                                                  (dubuf, ubuf, wu_ref, du_ref, dwu_ref)):
            dx = w_ref[0:1, :] * dbuf[pl.ds(kk - 1, tb), :]
            for s in range(1, kk):
                dx = dx + w_ref[s:s + 1, :] * dbuf[pl.ds(kk - 1 - s, tb), :]
            dx_ref[...] = dx.astype(BF16)
            dy = dbuf[pl.ds(0, tb), :]
            for s in range(kk):
                dw_ref[s:s + 1, :] += jnp.sum(dy * xbuf[pl.ds(HALO - (kk - 1) + s, tb), :], axis=0, keepdims=True)

    blk = pl.BlockSpec((tb, cb), lambda j, i: (i, j))
    before = pl.BlockSpec((HALO, cb), lambda j, i: (jnp.maximum(i * per - 1, 0), j))
    after = pl.BlockSpec((HALO, cb), lambda j, i: (jnp.minimum((i + 1) * per, t // HALO - 1), j))
    wspec = pl.BlockSpec((kk, cb), lambda j, i: (0, j))
    dwspec = pl.BlockSpec((HALO, cb), lambda j, i: (0, j))
    half = jax.ShapeDtypeStruct((t, c), BF16)
    dwshape = jax.ShapeDtypeStruct((HALO, c), F32)
    return pl.pallas_call(
        body, name="ffn_mid_bwd", grid=(c // cb, nblk),
        in_specs=[blk, before, after, blk, before, after, blk, after, wspec, wspec],
        out_specs=[blk, blk, dwspec, dwspec],
        out_shape=[half, half, dwshape, dwshape],
        scratch_shapes=[pltpu.VMEM((ext + HALO, cb), F32)] * 2 + [pltpu.VMEM((ext, cb), F32)] * 3,
        compiler_params=_params(("parallel", "arbitrary")),
    )(pre_g, pre_g, pre_g, pre_u, pre_u, pre_u, da, da, wg, wu)


def _down_loss(a, w_down, x1, wf, target):
    t = x1.shape[0]
    tb = _rows(t)

    def body(a_ref, wd_ref, x1_ref, wf_ref, tgt_ref, dx2_ref, dwf_ref, loss_ref):
        i = pl.program_id(0)
        x2 = x1_ref[...] + _dot(a_ref[...], wd_ref[...])
        y, vjp = jax.vjp(_rms, x2, wf_ref[...])
        err = y - tgt_ref[...]
        dx2, dwf = vjp(err * (1.0 / D_MODEL))
        dx2_ref[...] = dx2
        part = jnp.sum(jnp.sum(err * err, axis=1, keepdims=True), axis=0, keepdims=True) * (0.5 / D_MODEL)

        @pl.when(i == 0)
        def _():
            dwf_ref[...] = jnp.zeros_like(dwf_ref)
            loss_ref[...] = jnp.zeros_like(loss_ref)

        dwf_ref[...] += dwf
        loss_ref[...] += jnp.broadcast_to(part, loss_ref.shape)

    row = pl.BlockSpec((tb, D_MODEL), lambda i: (i, 0))
    vec = pl.BlockSpec((1, D_MODEL), lambda i: (0, 0))
    return pl.pallas_call(
        body, name="down_loss", grid=(t // tb,),
        in_specs=[pl.BlockSpec((tb, D_FF), lambda i: (i, 0)), pl.BlockSpec((D_FF, D_MODEL), lambda i: (0, 0)),
                  row, vec, row],
        out_specs=[row, vec, pl.BlockSpec((1, LANES), lambda i: (0, 0))],
        out_shape=[jax.ShapeDtypeStruct((t, D_MODEL), F32), jax.ShapeDtypeStruct((1, D_MODEL), F32),
                   jax.ShapeDtypeStruct((1, LANES), F32)],
        compiler_params=_params(("arbitrary",)),
    )(a, w_down, x1, wf, target)


def _local_step(x, target, wts):
    t = x.shape[0]
    nchunk = t // DN_CHUNK

    n1, hab = _norm1_fwd(x, wts["norm1"], wts["w_ab"])
    dnqkv = _mm(n1, wts["w_dnqkv"], name="h_dnqkv")
    dngate = _mm(n1, wts["w_dngate"], name="h_dngate")
    sbqkv = _mm(n1, wts["w_sbqkv"], out_dtype=BF16, name="h_sbqkv")
    gl = _mm(n1, wts["w_gl"], name="h_gl")

    cdn = _conv_fwd(dnqkv, wts["dn_conv"], "dn_conv_fwd")
    qn, kn, vv, gb = _dn_prep_fwd(cdn, hab, wts["alog"], wts["dtb"])
    per_head = gb[:, :2 * N_HEADS].T.reshape(2 * N_HEADS, nchunk, DN_CHUNK)
    gcol, bcol = per_head[:N_HEADS, :, :, None], per_head[N_HEADS:, :, :, None]
    grow = per_head[:N_HEADS, :, None, :]
    u_dn, w_dn, a_qk, qe, kdec, egl, tinv = _dn_local_fwd(qn, kn, vv, gcol, grow, bcol)
    o_raw, states = _dn_seq_fwd(u_dn, w_dn, a_qk, qe, kdec, egl)
    o_dn = _dn_post_fwd(o_raw, dngate, wts["dn_norm"])

    o_sb, tot, sb_used = _sb_fwd(sbqkv)

    pdn, psb, mixed, x1, n2 = _merge_fwd(o_dn, o_sb, gl, x, wts["wp_dn"], wts["wp_sb"], wts["w_out"],
                                         wts["norm2"])
    pre_g = _mm(n2, wts["w_up_g"], name="ffn_up_g")
    pre_u = _mm(n2, wts["w_up_u"], name="ffn_up_u")
    act = _ffn_mid_fwd(pre_g, pre_u, wts["ffn_conv_g"], wts["ffn_conv_u"])
    dx2, d_normf, loss_part = _down_loss(act, wts["w_down"], x1, wts["normf"], target)

    grads = {"normf": d_normf}
    da = _mm(dx2, wts["w_down"], tb=True, name="d_act")
    grads["w_down"] = _mm(act, dx2, ta=True, name="dw_down")
    dpre_g, dpre_u, dcw_g, dcw_u = _ffn_mid_bwd(pre_g, pre_u, wts["ffn_conv_g"], wts["ffn_conv_u"], da)
    grads["ffn_conv"] = jnp.concatenate([dcw_g[:FFN_CONV], dcw_u[:FFN_CONV]], axis=1)
    dn2 = _mm(dpre_g, wts["w_up_g"], tb=True, name="dn2_g")
    dn2 = _mm(dpre_u, wts["w_up_u"], tb=True, add=dn2, name="dn2_u")
    grads["w_up"] = jnp.concatenate([_mm(n2, dpre_g, ta=True, name="dw_up_g"),
                                     _mm(n2, dpre_u, ta=True, name="dw_up_u")], axis=1)

    dx1, grads["norm2"], dgl, dpdn, dpsb, do_dn, do_sb = _merge_bwd(
        dx2, dn2, x1, wts["norm2"], gl, pdn, psb, wts["wp_dn"], wts["wp_sb"], wts["w_out"])
    grads["w_out"] = _mm(mixed, dx1, ta=True, name="dw_out")
    grads["wp_dn"] = _mm(o_dn, dpdn, ta=True, name="dw_proj_dn")
    grads["wp_sb"] = _mm(o_sb, dpsb, ta=True, name="dw_proj_sb")

    dsq, dsk, dsv = _sb_bwd(sbqkv, tot, sb_used, do_sb)
    dsbqkv = jnp.concatenate([dsq, dsk, dsv], axis=1).astype(BF16)

    do_raw, ddngate, grads["dn_norm"] = _dn_post_bwd(o_raw, dngate, wts["dn_norm"], do_dn)
    seq_grads = _dn_seq_bwd(u_dn, w_dn, a_qk, qe, kdec, egl, states, do_raw)
    dqn, dkn, dvv, dgcol, dgrow, dbcol = _dn_local_bwd(qn, kn, vv, gcol, grow, bcol, tinv, *seq_grads)
    dg = (dgcol[..., 0] + dgrow[:, :, 0, :]).reshape(N_HEADS, t)
    dgb = jnp.concatenate([dg, dbcol[..., 0].reshape(N_HEADS, t)], axis=0).T
    dgb = jnp.pad(dgb, ((0, 0), (0, LANES - 2 * N_HEADS)))
    dcdn, dhab, grads["alog"], grads["dtb"] = _dn_prep_bwd(cdn, hab, wts["alog"], wts["dtb"], dqn, dkn, dvv, dgb)
    ddnqkv, dcw_dn = _conv_bwd(dcdn, dnqkv, wts["dn_conv"], "dn_conv_bwd", BF16)
    grads["dn_conv"] = dcw_dn[:DN_CONV]

    dn1 = _mm(ddnqkv, wts["w_dnqkv"], tb=True, name="dn1_dnqkv")
    dn1 = _mm(ddngate, wts["w_dngate"], tb=True, add=dn1, name="dn1_dngate")
    dn1 = _mm(dsbqkv, wts["w_sbqkv"], tb=True, add=dn1, name="dn1_sbqkv")
    dn1 = _mm(dgl, wts["w_gl"], tb=True, add=dn1, name="dn1_gl")
    grads["w_dnqkv"] = _mm(n1, ddnqkv, ta=True, name="dw_dnqkv")
    grads["w_dngate"] = _mm(n1, ddngate, ta=True, name="dw_dngate")
    grads["w_sbqkv"] = _mm(n1, dsbqkv, ta=True, name="dw_sbqkv")
    grads["w_gl"] = _mm(n1, dgl, ta=True, name="dw_gl")
    grads["w_ab"] = _mm(n1, dhab, ta=True, name="dw_ab")
    grad_x, grads["norm1"] = _norm1_bwd(x, wts["norm1"], dn1, dx1, dhab, wts["w_ab"])
    return loss_part, grad_x, grads


def _place():
    return lax.axis_index("x"), lax.axis_index("y"), lax.axis_index("c")


def _gather_shards(shard):
    rows, cols = shard.shape
    half = rows // 2

    def body(in_ref, out_ref, send_sems, recv_sems):
        x, y, c = _place()
        me = 2 * x + y
        sibling = (x, y, 1 - c)
        chips = [(1 - x, y), (x, 1 - y), (1 - x, 1 - y)]

        def slab(chip_index, part):
            return out_ref.at[chip_index, pl.ds(part * half, half), :]

        def copy(k, src, dst, to):
            return pltpu.make_async_remote_copy(src_ref=src, dst_ref=dst, send_sem=send_sems.at[k],
                                                recv_sem=recv_sems.at[k], device_id=to, device_id_type=MESH)

        my_half = in_ref.at[pl.ds(c * half, half), :]
        first = [copy(j, my_half, slab(me, c), (px, py, c)) for j, (px, py) in enumerate(chips)]
        for cp in first:
            cp.start()
        passed = []
        for j, (px, py) in enumerate(chips):
            landed = slab(2 * px + py, c)
            copy(j, landed, landed, (px, py, c)).wait_recv()
            fwd = copy(3 + j, landed, landed, sibling)
            fwd.start()
            passed.append(fwd)
        for j, (px, py) in enumerate(chips):
            there = slab(2 * px + py, 1 - c)
            copy(3 + j, there, there, sibling).wait_recv()
        for cp in first + passed:
            cp.wait_send()

    return pl.pallas_call(
        body, name="gather_weights",
        in_specs=[pl.BlockSpec(memory_space=pltpu.HBM)],
        out_specs=pl.BlockSpec(memory_space=pltpu.HBM),
        out_shape=jax.ShapeDtypeStruct((N_CHIPS, rows, cols), shard.dtype),
        scratch_shapes=[pltpu.SemaphoreType.DMA((6,)), pltpu.SemaphoreType.DMA((6,))],
    )(shard)


def _pair_exchange_halves(g):
    nsh, rows, cols = g.shape
    half = rows // 2

    def body(in_ref, out_ref, send_sem, recv_sem):
        x, y, c = _place()
        src = in_ref.at[:, pl.ds((1 - c) * half, half), :]
        cp = pltpu.make_async_remote_copy(src_ref=src, dst_ref=out_ref, send_sem=send_sem, recv_sem=recv_sem,
                                          device_id=(x, y, 1 - c), device_id_type=MESH)
        cp.start()
        cp.wait()

    return pl.pallas_call(
        body, name="grad_pair_exchange",
        in_specs=[pl.BlockSpec(memory_space=pltpu.HBM)],
        out_specs=pl.BlockSpec(memory_space=pltpu.HBM),
        out_shape=jax.ShapeDtypeStruct((nsh, half, cols), g.dtype),
        scratch_shapes=[pltpu.SemaphoreType.DMA, pltpu.SemaphoreType.DMA],
    )(g)


def _pair_add(g, got, c_idx):
    nsh, rows, cols = g.shape
    half = rows // 2
    rb = _pick_rows(half)

    def body(c_ref, g_ref, got_ref, o_ref):
        o_ref[...] = (g_ref[...].astype(F32) + got_ref[...].astype(F32)).astype(BF16)

    nb = half // rb
    grid_spec = pltpu.PrefetchScalarGridSpec(
        num_scalar_prefetch=1, grid=(nsh, nb),
        in_specs=[pl.BlockSpec((1, rb, cols), lambda s, i, c_ref: (s, c_ref[0] * nb + i, 0)),
                  pl.BlockSpec((1, rb, cols), lambda s, i, c_ref: (s, i, 0))],
        out_specs=pl.BlockSpec((1, rb, cols), lambda s, i, c_ref: (s, i, 0)))
    return pl.pallas_call(
        body, name="grad_pair_add", grid_spec=grid_spec,
        out_shape=jax.ShapeDtypeStruct((nsh, half, cols), BF16),
        compiler_params=_params(("parallel", "parallel")),
    )(c_idx, g, got)


def _pick_rows(n, target=1024):
    best = 16
    for b in range(16, min(n, target) + 1, 16):
        if n % b == 0:
            best = b
    return best


def _chip_exchange(p):
    nsh, half, cols = p.shape

    def body(in_ref, out_ref, send_sems, recv_sems):
        x, y, c = _place()
        chips = [(1 - x, y), (x, 1 - y), (1 - x, 1 - y)]
        sends = []
        for j, (px, py) in enumerate(chips):
            cp = pltpu.make_async_remote_copy(src_ref=in_ref.at[2 * px + py], dst_ref=out_ref.at[j],
                                              send_sem=send_sems.at[j], recv_sem=recv_sems.at[j],
                                              device_id=(px, py, c), device_id_type=MESH)
            cp.start()
            sends.append(cp)
        for cp in sends:
            cp.wait_recv()
        for cp in sends:
            cp.wait_send()

    return pl.pallas_call(
        body, name="grad_chip_exchange",
        in_specs=[pl.BlockSpec(memory_space=pltpu.HBM)],
        out_specs=pl.BlockSpec(memory_space=pltpu.HBM),
        out_shape=jax.ShapeDtypeStruct((N_CHIPS - 1, half, cols), p.dtype),
        scratch_shapes=[pltpu.SemaphoreType.DMA((3,)), pltpu.SemaphoreType.DMA((3,))],
    )(p)


def _sum_partials(p, got, chip_idx):
    nsh, half, cols = got.shape
    rb = _pick_rows(half)

    def body(me_ref, p_ref, got_ref, o_ref):
        acc = p_ref[0].astype(F32)
        for s in range(nsh):
            acc = acc + got_ref[s].astype(F32)
        o_ref[...] = acc

    grid_spec = pltpu.PrefetchScalarGridSpec(
        num_scalar_prefetch=1, grid=(half // rb,),
        in_specs=[pl.BlockSpec((1, rb, cols), lambda i, me_ref: (me_ref[0], i, 0)),
                  pl.BlockSpec((nsh, rb, cols), lambda i, me_ref: (0, i, 0))],
        out_specs=pl.BlockSpec((rb, cols), lambda i, me_ref: (i, 0)))
    return pl.pallas_call(
        body, name="grad_sum_chips", grid_spec=grid_spec,
        out_shape=jax.ShapeDtypeStruct((half, cols), F32),
        compiler_params=_params(("parallel",)),
    )(chip_idx, p, got)


def _pair_share(r):
    half, cols = r.shape

    def body(in_ref, out_ref, send_sem, recv_sem):
        x, y, c = _place()
        cp = pltpu.make_async_remote_copy(src_ref=in_ref, dst_ref=out_ref, send_sem=send_sem,
                                          recv_sem=recv_sem, device_id=(x, y, 1 - c), device_id_type=MESH)
        cp.start()
        cp.wait()

    return pl.pallas_call(
        body, name="grad_pair_share",
        in_specs=[pl.BlockSpec(memory_space=pltpu.HBM)],
        out_specs=pl.BlockSpec(memory_space=pltpu.HBM),
        out_shape=jax.ShapeDtypeStruct((half, cols), r.dtype),
        scratch_shapes=[pltpu.SemaphoreType.DMA, pltpu.SemaphoreType.DMA],
    )(r)


def _small_allreduce(v):
    rows, cols = v.shape
    ndev = 8

    def body(in_ref, out_ref, slots, send_sems, recv_sems):
        x, y, c = _place()
        me = 4 * x + 2 * y + c
        slots[me] = in_ref[...]
        sends = []
        for k in range(1, ndev):
            peer = (x ^ (k >> 2), y ^ ((k >> 1) & 1), c ^ (k & 1))
            cp = pltpu.make_async_remote_copy(src_ref=in_ref, dst_ref=slots.at[me], send_sem=send_sems.at[k - 1],
                                              recv_sem=recv_sems.at[k - 1], device_id=peer, device_id_type=MESH)
            cp.start()
            sends.append(cp)
        for k in range(1, ndev):
            there = slots.at[me ^ k]
            pltpu.make_async_remote_copy(src_ref=there, dst_ref=there, send_sem=send_sems.at[k - 1],
                                         recv_sem=recv_sems.at[k - 1], device_id=(x, y, c),
                                         device_id_type=MESH).wait_recv()
        for cp in sends:
            cp.wait_send()
        acc = slots[0]
        for s in range(1, ndev):
            acc = acc + slots[s]
        out_ref[...] = acc

    return pl.pallas_call(
        body, name="small_allreduce",
        in_specs=[pl.BlockSpec(memory_space=pltpu.VMEM)],
        out_specs=pl.BlockSpec(memory_space=pltpu.VMEM),
        out_shape=jax.ShapeDtypeStruct((rows, cols), F32),
        scratch_shapes=[pltpu.VMEM((ndev, rows, cols), F32), pltpu.SemaphoreType.DMA((ndev - 1,)),
                        pltpu.SemaphoreType.DMA((ndev - 1,))],
    )(v)


def _adamw(w, g, m, v, name):
    r, c = w.shape
    rb = r if r <= 128 else _pick_rows_8(r, 128)
    c1 = 1.0 - ADAM_B1 ** ADAM_STEP
    c2 = 1.0 - ADAM_B2 ** ADAM_STEP

    def body(w_ref, g_ref, m_ref, v_ref, d_ref, nm_ref, nv_ref):
        gg = g_ref[...]
        nm = ADAM_B1 * m_ref[...] + (1.0 - ADAM_B1) * gg
        nv = ADAM_B2 * v_ref[...] + (1.0 - ADAM_B2) * (gg * gg)
        d_ref[...] = -ADAM_LR * ((nm / c1) / (jnp.sqrt(nv / c2) + ADAM_EPS) + ADAM_WD * w_ref[...])
        nm_ref[...] = nm
        nv_ref[...] = nv

    blk = pl.BlockSpec((rb, c), lambda i: (i, 0))
    shp = jax.ShapeDtypeStruct((r, c), F32)
    return pl.pallas_call(
        body, name=name, grid=(r // rb,), in_specs=[blk] * 4, out_specs=[blk] * 3, out_shape=[shp] * 3,
        compiler_params=_params(("parallel",)),
    )(w, g, m, v)


def _pick_rows_8(n, target):
    best = n
    for b in range(8, min(n, target) + 1, 8):
        if n % b == 0:
            best = b
    return best


W_IN_COLS = 2308
W_UP_COLS = 1408
W_DOWN_ROWS = 704
DN_CONV_COLS = 768
FFN_CONV_COLS = 1408
PROJ_ROWS = 256
ROW_TILE = 16
SEG = [("w_in", W_IN_COLS), ("wp_dn", PROJ_ROWS), ("wp_sb", PROJ_ROWS), ("w_out", PROJ_ROWS),
       ("w_up", W_UP_COLS), ("w_down", W_DOWN_ROWS), ("dn_conv", ROW_TILE), ("ffn_conv", ROW_TILE)]


def _seg_offsets():
    offs, at = {}, 0
    for nm, n in SEG:
        offs[nm] = (at, n)
        at += -(-n // ROW_TILE) * ROW_TILE
    assert at <= PACK_ROWS and PACK_ROWS % (2 * ROW_TILE) == 0
    return offs, at


PACK_OFFS, PACK_USED = _seg_offsets()


def _tile_rows(a, axis):
    n = a.shape[axis]
    pad = [(0, 0)] * a.ndim
    pad[axis] = (0, -(-n // ROW_TILE) * ROW_TILE - n)
    return jnp.pad(a, pad)


def _flat_rows(a, nrows):
    flat = a.reshape(-1)
    return jnp.pad(flat, (0, nrows * D_MODEL - flat.shape[0])).reshape(nrows, D_MODEL)


def _pack_weight_shard(w_in, wp_dn, wp_sb, w_out, w_up, w_down, dn_conv, ffn_conv):
    parts = [w_in.astype(BF16).reshape(W_IN_COLS, D_MODEL), wp_dn.astype(BF16), wp_sb.astype(BF16),
             w_out.astype(BF16), w_up.astype(BF16).reshape(W_UP_COLS, D_MODEL), w_down.astype(BF16),
             _flat_rows(lax.bitcast_convert_type(dn_conv, BF16), ROW_TILE),
             _flat_rows(lax.bitcast_convert_type(ffn_conv, BF16), ROW_TILE),
             jnp.zeros((PACK_ROWS - PACK_USED, D_MODEL), BF16)]
    return jnp.concatenate([_tile_rows(p, 0) for p in parts], axis=0)


def _unpack_weights(g):
    def seg(nm):
        at, n = PACK_OFFS[nm]
        return g[:, at:at + n, :]

    def cols(nm, ncols):
        return seg(nm).reshape(N_CHIPS, D_MODEL, ncols).transpose(1, 0, 2).reshape(D_MODEL, N_CHIPS * ncols)

    def f32_rows(nm, k, ncols):
        raw = seg(nm).reshape(N_CHIPS, -1)[:, :2 * k * ncols].reshape(N_CHIPS, k * ncols, 2)
        vals = lax.bitcast_convert_type(raw, F32).reshape(N_CHIPS, k, ncols)
        return vals.transpose(1, 0, 2).reshape(k, N_CHIPS * ncols)

    w_in = cols("w_in", W_IN_COLS)
    w_up = cols("w_up", W_UP_COLS)
    ffn_conv = f32_rows("ffn_conv", FFN_CONV, FFN_CONV_COLS)
    q_end, a_end, g_end, s_end = 3 * D_MODEL, 3 * D_MODEL + 2 * N_HEADS, 4 * D_MODEL + 2 * N_HEADS, 7 * D_MODEL + 2 * N_HEADS
    return {
        "w_dnqkv": w_in[:, :q_end],
        "w_ab": jnp.pad(w_in[:, q_end:a_end], ((0, 0), (0, LANES - 2 * N_HEADS))),
        "w_dngate": w_in[:, a_end:g_end],
        "w_sbqkv": w_in[:, g_end:s_end],
        "w_gl": w_in[:, s_end:],
        "wp_dn": seg("wp_dn").reshape(D_MODEL, D_MODEL),
        "wp_sb": seg("wp_sb").reshape(D_MODEL, D_MODEL),
        "w_out": seg("w_out").reshape(D_MODEL, D_MODEL),
        "w_up_g": w_up[:, :D_FF], "w_up_u": w_up[:, D_FF:],
        "w_down": seg("w_down").reshape(D_FF, D_MODEL),
        "dn_conv": f32_rows("dn_conv", DN_CONV, DN_CONV_COLS),
        "ffn_conv_g": ffn_conv[:, :D_FF], "ffn_conv_u": ffn_conv[:, D_FF:],
    }


def _pack_grads(gr):
    w_in = jnp.concatenate([gr["w_dnqkv"], gr["w_ab"][:, :2 * N_HEADS], gr["w_dngate"], gr["w_sbqkv"], gr["w_gl"]],
                           axis=1)

    def cols(a, ncols):
        return a.reshape(a.shape[0], N_CHIPS, ncols).transpose(1, 0, 2)

    def rows(a, nrows):
        return a.reshape(N_CHIPS, nrows, a.shape[1])

    def flat(a, nrows):
        a = a.reshape(N_CHIPS, -1)
        return jnp.pad(a, ((0, 0), (0, nrows * D_MODEL - a.shape[1]))).reshape(N_CHIPS, nrows, D_MODEL)

    parts = [cols(w_in, W_IN_COLS).reshape(N_CHIPS, W_IN_COLS, D_MODEL),
             rows(gr["wp_dn"], PROJ_ROWS), rows(gr["wp_sb"], PROJ_ROWS), rows(gr["w_out"], PROJ_ROWS),
             cols(gr["w_up"], W_UP_COLS).reshape(N_CHIPS, W_UP_COLS, D_MODEL),
             rows(gr["w_down"], W_DOWN_ROWS),
             flat(cols(gr["dn_conv"], DN_CONV_COLS), ROW_TILE), flat(cols(gr["ffn_conv"], FFN_CONV_COLS), ROW_TILE),
             jnp.zeros((N_CHIPS, PACK_ROWS - PACK_USED, D_MODEL), F32)]
    return jnp.concatenate([_tile_rows(p, 1) for p in parts], axis=1).astype(BF16)


def _unpack_grad_shard(r):
    def seg(nm):
        at, n = PACK_OFFS[nm]
        return r[at:at + n, :]

    return {
        "w_in": seg("w_in").reshape(D_MODEL, W_IN_COLS),
        "wp_dn": seg("wp_dn"), "wp_sb": seg("wp_sb"), "w_out": seg("w_out"),
        "w_up": seg("w_up").reshape(D_MODEL, W_UP_COLS),
        "w_down": seg("w_down"),
        "dn_conv": seg("dn_conv").reshape(-1)[:DN_CONV * DN_CONV_COLS].reshape(DN_CONV, DN_CONV_COLS),
        "ffn_conv": seg("ffn_conv").reshape(-1)[:FFN_CONV * FFN_CONV_COLS].reshape(FFN_CONV, FFN_CONV_COLS),
    }


def _lane_row(v):
    return jnp.pad(v.reshape(1, -1), ((0, 0), (0, LANES - v.size)))


def kernel(x, norm1_w, w_in, dn_conv_w, dn_A_log, dn_dt_bias, dn_norm_w, w_proj_dn, w_proj_sb, w_out, norm2_w, ffn_w_up, ffn_conv_w, ffn_w_down, norm_f_w, loss_target, m_norm1_w, m_w_in, m_dn_conv_w, m_dn_A_log, m_dn_dt_bias, m_dn_norm_w, m_w_proj_dn, m_w_proj_sb, m_w_out, m_norm2_w, m_ffn_w_up, m_ffn_conv_w, m_ffn_w_down, m_norm_f_w, v_norm1_w, v_w_in, v_dn_conv_w, v_dn_A_log, v_dn_dt_bias, v_dn_norm_w, v_w_proj_dn, v_w_proj_sb, v_w_out, v_norm2_w, v_ffn_w_up, v_ffn_conv_w, v_ffn_w_down, v_norm_f_w):
    shard = _pack_weight_shard(w_in[0], w_proj_dn[0], w_proj_sb[0], w_out[0], ffn_w_up[0], ffn_w_down[0],
                               dn_conv_w[0], ffn_conv_w[0])
    chip_idx = (2 * lax.axis_index("x") + lax.axis_index("y")).astype(jnp.int32)
    gathered = lax.dynamic_update_slice(_gather_shards(shard), shard[None], (chip_idx, 0, 0))
    wts = _unpack_weights(gathered)
    wts.update(norm1=norm1_w, norm2=norm2_w, normf=norm_f_w.reshape(1, D_MODEL), dn_norm=dn_norm_w,
               alog=_lane_row(dn_A_log), dtb=_lane_row(dn_dt_bias))

    loss_part, grad_x, gr = _local_step(x[0], loss_target[0], wts)

    c_idx = lax.axis_index("c").astype(jnp.int32).reshape(1)
    packed = _pack_grads(gr)
    partial_sum = _pair_add(packed, _pair_exchange_halves(packed), c_idx)
    reduced_half = _sum_partials(partial_sum, _chip_exchange(partial_sum), chip_idx.reshape(1))
    other_half = _pair_share(reduced_half)
    is_south = lax.axis_index("c") == 0
    gsh = _unpack_grad_shard(jnp.concatenate([jnp.where(is_south, reduced_half, other_half),
                                              jnp.where(is_south, other_half, reduced_half)], axis=0))

    tail = jnp.concatenate([gr["dn_norm"], gr["alog"][:, :N_HEADS], gr["dtb"][:, :N_HEADS], loss_part[:, :1]], axis=1)
    small = jnp.concatenate([gr["norm1"], gr["norm2"], gr["normf"],
                             jnp.pad(tail, ((0, 0), (0, D_MODEL - tail.shape[1]))),
                             jnp.zeros((SMALL_ROWS - 4, D_MODEL), F32)], axis=0)
    small = _small_allreduce(small)
    at = HEAD_DIM
    g_small = {"norm1_w": small[0:1], "norm2_w": small[1:2], "norm_f_w": small[2],
               "dn_norm_w": small[3:4, :at], "dn_A_log": small[3:4, at:at + N_HEADS],
               "dn_dt_bias": small[3:4, at + N_HEADS:at + 2 * N_HEADS]}
    loss = small[3, at + 2 * N_HEADS]

    big = {"w_in": (w_in, m_w_in, v_w_in, gsh["w_in"]), "dn_conv_w": (dn_conv_w, m_dn_conv_w, v_dn_conv_w, gsh["dn_conv"]),
           "w_proj_dn": (w_proj_dn, m_w_proj_dn, v_w_proj_dn, gsh["wp_dn"]),
           "w_proj_sb": (w_proj_sb, m_w_proj_sb, v_w_proj_sb, gsh["wp_sb"]),
           "w_out": (w_out, m_w_out, v_w_out, gsh["w_out"]),
           "ffn_w_up": (ffn_w_up, m_ffn_w_up, v_ffn_w_up, gsh["w_up"]),
           "ffn_conv_w": (ffn_conv_w, m_ffn_conv_w, v_ffn_conv_w, gsh["ffn_conv"]),
           "ffn_w_down": (ffn_w_down, m_ffn_w_down, v_ffn_w_down, gsh["w_down"])}
    res = {}
    for nm, (w, m, v, g) in big.items():
        d, nm_, nv_ = _adamw(w[0], g, m[0], v[0], "adamw_" + nm)
        res[nm] = (g[None], d[None], nm_[None], nv_[None])

    names = ["norm1_w", "norm2_w", "norm_f_w", "dn_norm_w", "dn_A_log", "dn_dt_bias"]
    given = {"norm1_w": (norm1_w, m_norm1_w, v_norm1_w), "norm2_w": (norm2_w, m_norm2_w, v_norm2_w),
             "norm_f_w": (norm_f_w, m_norm_f_w, v_norm_f_w), "dn_norm_w": (dn_norm_w, m_dn_norm_w, v_dn_norm_w),
             "dn_A_log": (dn_A_log, m_dn_A_log, v_dn_A_log), "dn_dt_bias": (dn_dt_bias, m_dn_dt_bias, v_dn_dt_bias)}

    def stack(k, fill):
        rows = [jnp.pad(given[nm][k].reshape(1, -1), ((0, 0), (0, D_MODEL - given[nm][k].size)),
                        constant_values=fill) for nm in names]
        return jnp.concatenate(rows + [jnp.full((SMALL_ROWS - len(names), D_MODEL), fill, F32)], axis=0)

    g_rows = jnp.concatenate(
        [jnp.pad(g_small[nm].reshape(1, -1), ((0, 0), (0, D_MODEL - g_small[nm].size))) for nm in names]
        + [jnp.zeros((SMALL_ROWS - len(names), D_MODEL), F32)], axis=0)
    d_s, m_s, v_s = _adamw(stack(0, 0.0), g_rows, stack(1, 0.0), stack(2, 1.0), "adamw_small")
    for r, nm in enumerate(names):
        shape = given[nm][0].shape
        n = given[nm][0].size
        res[nm] = (g_small[nm].reshape(shape), d_s[r, :n].reshape(shape), m_s[r, :n].reshape(shape),
                   v_s[r, :n].reshape(shape))

    order = ["norm1_w", "w_in", "dn_conv_w", "dn_A_log", "dn_dt_bias", "dn_norm_w", "w_proj_dn", "w_proj_sb",
             "w_out", "norm2_w", "ffn_w_up", "ffn_conv_w", "ffn_w_down", "norm_f_w"]
    outs = [loss, grad_x[None]]
    for k in range(4):
        outs += [res[nm][k] for nm in order]
    return tuple(outs)
```

```python
import functools

import jax
import jax.numpy as jnp
from jax import lax
from jax.experimental import pallas as pl
from jax.experimental.pallas import tpu as pltpu

F32 = jnp.float32
BF16 = jnp.bfloat16
HIGHEST = lax.Precision.HIGHEST
MESH = pl.DeviceIdType.MESH

EPS = 1e-6
D_MODEL = 1024
N_HEADS = 8
HEAD_DIM = 128
DN_CONV = 4
DN_CHUNK = 64
D_FF = 2816
FFN_CONV = 3
ADAM_LR, ADAM_B1, ADAM_B2, ADAM_EPS, ADAM_WD, ADAM_STEP = 0.001, 0.9, 0.999, 1e-08, 0.01, 10

N_CHIPS = 4
LANES = 128
HALO = 8
VMEM_LIMIT = 48 * 1024 * 1024
SMALL_ROWS = 8


def _params(sem=None):
    return pltpu.CompilerParams(dimension_semantics=sem, vmem_limit_bytes=VMEM_LIMIT)


def _pick(n, target):
    best = None
    for b in range(LANES, min(n, target) + 1, LANES):
        if n % b == 0:
            best = b
    return best or n


ELEMENTWISE_COLS = 1408


def _rows(t, target=256):
    return min(t, target)


def _dot(a, b, precision=None):
    return lax.dot_general(a, b, (((1,), (0,)), ((), ())), precision=precision, preferred_element_type=F32)


def _dot_nt(a, b, precision=None):
    return lax.dot_general(a, b, (((1,), (1,)), ((), ())), precision=precision, preferred_element_type=F32)


def _dot_tn(a, b, precision=None):
    return lax.dot_general(a, b, (((0,), (0,)), ((), ())), precision=precision, preferred_element_type=F32)


def _rms(x, w):
    return x * lax.rsqrt(jnp.mean(x * x, axis=-1, keepdims=True) + EPS) * w


def _silu(x):
    return x * jax.nn.sigmoid(x)


def _softplus(x):
    return jnp.maximum(x, 0.0) + jnp.log(1.0 + jnp.exp(-jnp.abs(x)))


MM_BLOCK = 1408


def _mm(a, b, *, ta=False, tb=False, add=None, out_dtype=F32, name, bm=MM_BLOCK, bn=MM_BLOCK, bk=MM_BLOCK):
    m = a.shape[1] if ta else a.shape[0]
    k = a.shape[0] if ta else a.shape[1]
    n = b.shape[0] if tb else b.shape[1]
    bm, bn, bk = _pick(m, bm), _pick(n, bn), _pick(k, bk)
    nk = k // bk
    dims = (((0 if ta else 1,), (1 if tb else 0,)), ((), ()))

    def body(*refs):
        a_ref, b_ref = refs[:2]
        c_ref = refs[2] if add is not None else None
        o_ref = refs[3] if add is not None else refs[2]
        acc = refs[-1]
        kk = pl.program_id(2)
        part = lax.dot_general(a_ref[...].astype(BF16), b_ref[...].astype(BF16), dims, preferred_element_type=F32)

        def finish(r):
            if add is not None:
                r = r + c_ref[...].astype(F32)
            o_ref[...] = r.astype(out_dtype)

        if nk == 1:
            finish(part)
            return

        @pl.when(kk == 0)
        def _():
            acc[...] = part

        @pl.when(jnp.logical_and(kk > 0, kk < nk - 1))
        def _():
            acc[...] += part

        @pl.when(kk == nk - 1)
        def _():
            finish(acc[...] + part)

    a_spec = (pl.BlockSpec((bk, bm), lambda i, j, kk: (kk, i)) if ta
              else pl.BlockSpec((bm, bk), lambda i, j, kk: (i, kk)))
    b_spec = (pl.BlockSpec((bn, bk), lambda i, j, kk: (j, kk)) if tb
              else pl.BlockSpec((bk, bn), lambda i, j, kk: (kk, j)))
    o_spec = pl.BlockSpec((bm, bn), lambda i, j, kk: (i, j))
    in_specs = [a_spec, b_spec] + ([o_spec] if add is not None else [])
    args = (a, b) + ((add,) if add is not None else ())
    return pl.pallas_call(
        body, name=name, grid=(m // bm, n // bn, nk),
        in_specs=in_specs, out_specs=o_spec,
        out_shape=jax.ShapeDtypeStruct((m, n), out_dtype),
        scratch_shapes=[pltpu.VMEM((bm, bn), F32)] if nk > 1 else [],
        compiler_params=_params(("parallel", "parallel", "arbitrary")),
    )(*args)


def _norm1_fwd(x, w, w_ab):
    t = x.shape[0]
    tb = _rows(t)

    def body(x_ref, w_ref, wab_ref, n_ref, hab_ref):
        n = _rms(x_ref[...], w_ref[...]).astype(BF16)
        n_ref[...] = n
        hab_ref[...] = _dot(n, wab_ref[...])

    return pl.pallas_call(
        body, name="norm1_fwd", grid=(t // tb,),
        in_specs=[pl.BlockSpec((tb, D_MODEL), lambda i: (i, 0)),
                  pl.BlockSpec((1, D_MODEL), lambda i: (0, 0)),
                  pl.BlockSpec((D_MODEL, LANES), lambda i: (0, 0))],
        out_specs=[pl.BlockSpec((tb, D_MODEL), lambda i: (i, 0)),
                   pl.BlockSpec((tb, LANES), lambda i: (i, 0))],
        out_shape=[jax.ShapeDtypeStruct((t, D_MODEL), BF16), jax.ShapeDtypeStruct((t, LANES), F32)],
        compiler_params=_params(("arbitrary",)),
    )(x, w, w_ab)


def _norm1_bwd(x, w, dn, dres, dab, w_ab):
    t = x.shape[0]
    tb = _rows(t)

    def body(x_ref, w_ref, dn_ref, dres_ref, dab_ref, wab_ref, dx_ref, dw_ref):
        i = pl.program_id(0)
        g = dn_ref[...] + _dot_nt(dab_ref[...].astype(BF16), wab_ref[...])
        _, vjp = jax.vjp(_rms, x_ref[...], w_ref[...])
        dx, dw = vjp(g)
        dx_ref[...] = dres_ref[...] + dx

        @pl.when(i == 0)
        def _():
            dw_ref[...] = jnp.zeros_like(dw_ref)

        dw_ref[...] += dw

    row = pl.BlockSpec((tb, D_MODEL), lambda i: (i, 0))
    vec = pl.BlockSpec((1, D_MODEL), lambda i: (0, 0))
    return pl.pallas_call(
        body, name="norm1_bwd", grid=(t // tb,),
        in_specs=[row, vec, row, row, pl.BlockSpec((tb, LANES), lambda i: (i, 0)),
                  pl.BlockSpec((D_MODEL, LANES), lambda i: (0, 0))],
        out_specs=[row, vec],
        out_shape=[jax.ShapeDtypeStruct((t, D_MODEL), F32), jax.ShapeDtypeStruct((1, D_MODEL), F32)],
        compiler_params=_params(("arbitrary",)),
    )(x, w, dn, dres, dab, w_ab)


def _conv_fwd(x, w, name):
    t, c = x.shape
    kk = w.shape[0]
    tb, cb = _rows(t, 512), _pick(c, ELEMENTWISE_COLS)
    per = tb // HALO

    def body(x_ref, halo_ref, w_ref, y_ref, buf):
        i = pl.program_id(0)
        buf[pl.ds(HALO, tb), :] = x_ref[...]
        buf[pl.ds(0, HALO), :] = jnp.where(i == 0, 0.0, halo_ref[...])
        y = w_ref[0:1, :] * buf[pl.ds(HALO - (kk - 1), tb), :]
        for s in range(1, kk):
            y = y + w_ref[s:s + 1, :] * buf[pl.ds(HALO - (kk - 1) + s, tb), :]
        y_ref[...] = y

    return pl.pallas_call(
        body, name=name, grid=(t // tb, c // cb),
        in_specs=[pl.BlockSpec((tb, cb), lambda i, j: (i, j)),
                  pl.BlockSpec((HALO, cb), lambda i, j: (jnp.maximum(i * per - 1, 0), j)),
                  pl.BlockSpec((kk, cb), lambda i, j: (0, j))],
        out_specs=pl.BlockSpec((tb, cb), lambda i, j: (i, j)),
        out_shape=jax.ShapeDtypeStruct((t, c), F32),
        scratch_shapes=[pltpu.VMEM((tb + HALO, cb), F32)],
        compiler_params=_params(("parallel", "parallel")),
    )(x, x, w)


def _conv_bwd(dy, x, w, name, dx_dtype):
    t, c = x.shape
    kk = w.shape[0]
    tb, cb = _rows(t, 512), _pick(c, ELEMENTWISE_COLS)
    per = tb // HALO
    nblk = t // tb

    def body(dy_ref, after_ref, x_ref, before_ref, w_ref, dx_ref, dw_ref, dbuf, xbuf):
        i = pl.program_id(1)
        dy = dy_ref[...]
        dbuf[pl.ds(0, tb), :] = dy
        dbuf[pl.ds(tb, HALO), :] = jnp.where(i == nblk - 1, 0.0, after_ref[...])
        xbuf[pl.ds(HALO, tb), :] = x_ref[...]
        xbuf[pl.ds(0, HALO), :] = jnp.where(i == 0, 0.0, before_ref[...])
        dx = w_ref[0:1, :] * dbuf[pl.ds(kk - 1, tb), :]
        for s in range(1, kk):
            dx = dx + w_ref[s:s + 1, :] * dbuf[pl.ds(kk - 1 - s, tb), :]
        dx_ref[...] = dx.astype(dx_dtype)

        @pl.when(i == 0)
        def _():
            dw_ref[...] = jnp.zeros_like(dw_ref)

        for s in range(kk):
            part = jnp.sum(dy * xbuf[pl.ds(HALO - (kk - 1) + s, tb), :], axis=0, keepdims=True)
            dw_ref[s:s + 1, :] += part

    blk = pl.BlockSpec((tb, cb), lambda j, i: (i, j))
    return pl.pallas_call(
        body, name=name, grid=(c // cb, nblk),
        in_specs=[blk,
                  pl.BlockSpec((HALO, cb), lambda j, i: (jnp.minimum((i + 1) * per, t // HALO - 1), j)),
                  blk,
                  pl.BlockSpec((HALO, cb), lambda j, i: (jnp.maximum(i * per - 1, 0), j)),
                  pl.BlockSpec((kk, cb), lambda j, i: (0, j))],
        out_specs=[blk, pl.BlockSpec((HALO, cb), lambda j, i: (0, j))],
        out_shape=[jax.ShapeDtypeStruct((t, c), dx_dtype), jax.ShapeDtypeStruct((HALO, c), F32)],
        scratch_shapes=[pltpu.VMEM((tb + HALO, cb), F32), pltpu.VMEM((tb + HALO, cb), F32)],
        compiler_params=_params(("parallel", "arbitrary")),
    )(dy, dy, x, x, w)


def _dn_prep_fn(c, hab, alog, dtb):
    s = _silu(c)
    heads = []
    for h in range(2 * N_HEADS):
        sh = s[:, h * HEAD_DIM:(h + 1) * HEAD_DIM]
        heads.append(sh * lax.rsqrt(jnp.sum(sh * sh, axis=-1, keepdims=True) + EPS))
    qn = jnp.concatenate(heads[:N_HEADS], axis=1)
    kn = jnp.concatenate(heads[N_HEADS:], axis=1)
    v = s[:, 2 * D_MODEL:]
    lane = lax.broadcasted_iota(jnp.int32, hab.shape, 1)
    g = -jnp.exp(alog) * _softplus(hab + dtb)
    beta = jax.nn.sigmoid(hab)
    gb = jnp.where(lane < N_HEADS, g, jnp.where(lane < 2 * N_HEADS, beta, 0.0))
    return qn, kn, v, gb


def _to_heads(ref, val):
    for h in range(N_HEADS):
        ref[h] = val[:, h * HEAD_DIM:(h + 1) * HEAD_DIM]


def _from_heads(ref):
    return jnp.concatenate([ref[h] for h in range(N_HEADS)], axis=1)


def _dn_prep_fwd(c, hab, alog, dtb):
    t = c.shape[0]
    tb = _rows(t)

    def body(c_ref, hab_ref, alog_ref, dtb_ref, q_ref, k_ref, v_ref, gb_ref):
        qn, kn, v, gb = _dn_prep_fn(c_ref[...], hab_ref[...], alog_ref[...], dtb_ref[...])
        _to_heads(q_ref, qn)
        _to_heads(k_ref, kn)
        _to_heads(v_ref, v)
        gb_ref[...] = gb

    hm = pl.BlockSpec((N_HEADS, tb, HEAD_DIM), lambda i: (0, i, 0))
    nar = pl.BlockSpec((tb, LANES), lambda i: (i, 0))
    vec = pl.BlockSpec((1, LANES), lambda i: (0, 0))
    return pl.pallas_call(
        body, name="dn_prep_fwd", grid=(t // tb,),
        in_specs=[pl.BlockSpec((tb, 3 * D_MODEL), lambda i: (i, 0)), nar, vec, vec],
        out_specs=[hm, hm, hm, nar],
        out_shape=[jax.ShapeDtypeStruct((N_HEADS, t, HEAD_DIM), F32)] * 3 + [jax.ShapeDtypeStruct((t, LANES), F32)],
        compiler_params=_params(("parallel",)),
    )(c, hab, alog, dtb)


def _dn_prep_bwd(c, hab, alog, dtb, dq, dk, dv, dgb):
    t = c.shape[0]
    tb = _rows(t)

    def body(c_ref, hab_ref, alog_ref, dtb_ref, dq_ref, dk_ref, dv_ref, dgb_ref,
             dc_ref, dhab_ref, dalog_ref, ddtb_ref):
        i = pl.program_id(0)
        _, vjp = jax.vjp(_dn_prep_fn, c_ref[...], hab_ref[...], alog_ref[...], dtb_ref[...])
        dc, dhab, dalog, ddtb = vjp((_from_heads(dq_ref), _from_heads(dk_ref), _from_heads(dv_ref), dgb_ref[...]))
        dc_ref[...] = dc
        dhab_ref[...] = dhab

        @pl.when(i == 0)
        def _():
            dalog_ref[...] = jnp.zeros_like(dalog_ref)
            ddtb_ref[...] = jnp.zeros_like(ddtb_ref)

        dalog_ref[...] += dalog
        ddtb_ref[...] += ddtb

    hm = pl.BlockSpec((N_HEADS, tb, HEAD_DIM), lambda i: (0, i, 0))
    wide = pl.BlockSpec((tb, 3 * D_MODEL), lambda i: (i, 0))
    nar = pl.BlockSpec((tb, LANES), lambda i: (i, 0))
    vec = pl.BlockSpec((1, LANES), lambda i: (0, 0))
    return pl.pallas_call(
        body, name="dn_prep_bwd", grid=(t // tb,),
        in_specs=[wide, nar, vec, vec, hm, hm, hm, nar],
        out_specs=[wide, nar, vec, vec],
        out_shape=[jax.ShapeDtypeStruct((t, 3 * D_MODEL), F32), jax.ShapeDtypeStruct((t, LANES), F32),
                   jax.ShapeDtypeStruct((1, LANES), F32), jax.ShapeDtypeStruct((1, LANES), F32)],
        compiler_params=_params(("arbitrary",)),
    )(c, hab, alog, dtb, dq, dk, dv, dgb)


DN_PREC = lax.Precision.HIGH
DN_GROUP = 8


def _dn_prec(a):
    return DN_PREC if a.dtype == F32 else None


def _bdot(a, b):
    return lax.dot_general(a, b, (((2,), (1,)), ((0,), (0,))), precision=_dn_prec(a), preferred_element_type=F32)


def _bdot_nt(a, b):
    return lax.dot_general(a, b, (((2,), (2,)), ((0,), (0,))), precision=_dn_prec(a), preferred_element_type=F32)


def _bdot_tn(a, b):
    return lax.dot_general(a, b, (((1,), (1,)), ((0,), (0,))), precision=_dn_prec(a), preferred_element_type=F32)


def _unit_lower_inverse(lmat):
    c = lmat.shape[-1]
    ri = lax.broadcasted_iota(jnp.int32, (c, c), 0)
    ci = lax.broadcasted_iota(jnp.int32, (c, c), 1)
    p = -lmat
    tinv = jnp.where(ri == ci, 1.0, 0.0) + p
    for _ in range(max(c.bit_length() - 2, 0)):
        p = _bdot(p, p)
        tinv = tinv + _bdot(tinv, p)
    return tinv


@jax.custom_vjp
def _solve_with(lmat, rhs, tinv):
    return _bdot(tinv, rhs)


def _solve_with_fwd(lmat, rhs, tinv):
    sol = _bdot(tinv, rhs)
    return sol, (sol, tinv)


def _solve_with_bwd(res, dsol):
    sol, tinv = res
    drhs = _bdot_tn(tinv, dsol)
    return -_bdot_nt(drhs, sol), drhs, jnp.zeros_like(tinv)


_solve_with.defvjp(_solve_with_fwd, _solve_with_bwd)


def _dn_local(q, k, v, grow, brow, tinv):
    g, c, _ = q.shape
    ri = lax.broadcasted_iota(jnp.int32, (c, c), 0)
    ci = lax.broadcasted_iota(jnp.int32, (c, c), 1)
    lower = ri >= ci
    as_col = lambda r: jnp.sum(jnp.where(ri == ci, jnp.broadcast_to(r, (g, c, c)), 0.0), axis=2, keepdims=True)
    gcol, bcol = as_col(grow), as_col(brow)
    gc_col = jnp.sum(jnp.where(lower, jnp.broadcast_to(grow, (g, c, c)), 0.0), axis=2, keepdims=True)
    gc_row = jnp.sum(jnp.where(ri <= ci, jnp.broadcast_to(gcol, (g, c, c)), 0.0), axis=1, keepdims=True)
    qs = q * (HEAD_DIM ** -0.5)
    kb = k * bcol
    vb = v * bcol
    decay = jnp.where(lower, jnp.exp(jnp.where(lower, gc_col - gc_row, 0.0)), 0.0)
    lmat = jnp.where(ri > ci, _bdot_nt(kb, k) * decay, 0.0)
    eg = jnp.exp(gc_col)
    rhs = jnp.concatenate([vb, kb * eg], axis=2)
    if tinv is None:
        tinv = _unit_lower_inverse(lmat)
    sol = _solve_with(lmat, rhs, tinv)
    a_qk = jnp.where(lower, _bdot_nt(qs, k) * decay, 0.0)
    g_last = jnp.sum(grow, axis=2, keepdims=True)
    kdec = k * jnp.exp(g_last - gc_col)
    egl = jnp.broadcast_to(jnp.exp(g_last), (g, 1, HEAD_DIM))
    return sol[:, :, :HEAD_DIM], sol[:, :, HEAD_DIM:], a_qk, qs * eg, kdec, egl, tinv


def _dn_seq(u, w, a_qk, qe, kdec, egl, s_in):
    b16 = lambda x: x.astype(BF16)
    v_new = u - _bdot(b16(w), b16(s_in))
    o = _bdot(b16(qe), b16(s_in)) + _bdot(b16(a_qk), b16(v_new))
    return o, s_in * egl + _bdot_tn(b16(kdec), b16(v_new))


def _dn_local_specs(t):
    grp = min(DN_GROUP, t // DN_CHUNK)
    rows = grp * DN_CHUNK
    blk = pl.BlockSpec((1, rows, HEAD_DIM), lambda h, i: (h, i, 0))
    row = pl.BlockSpec((1, grp, 1, DN_CHUNK), lambda h, i: (h, i, 0, 0))
    sq = pl.BlockSpec((1, grp, DN_CHUNK, DN_CHUNK), lambda h, i: (h, i, 0, 0))
    lane = pl.BlockSpec((1, grp, 1, HEAD_DIM), lambda h, i: (h, i, 0, 0))
    return grp, blk, row, sq, lane


def _dn_shapes(t):
    nchunk = t // DN_CHUNK
    big = jax.ShapeDtypeStruct((N_HEADS, t, HEAD_DIM), F32)
    row = jax.ShapeDtypeStruct((N_HEADS, nchunk, 1, DN_CHUNK), F32)
    sq = jax.ShapeDtypeStruct((N_HEADS, nchunk, DN_CHUNK, DN_CHUNK), F32)
    lane = jax.ShapeDtypeStruct((N_HEADS, nchunk, 1, HEAD_DIM), F32)
    return big, row, sq, lane


def _dn_local_fwd(q, k, v, grow, brow):
    t = q.shape[1]
    grp, blk, row, sq, lane = _dn_local_specs(t)
    big, _, sqs, lanes = _dn_shapes(t)

    def body(q_ref, k_ref, v_ref, gr_ref, br_ref, u_ref, w_ref, a_ref, qe_ref, kd_ref, egl_ref, t_ref):
        split = lambda r: r[0].reshape(grp, DN_CHUNK, HEAD_DIM)
        u, w, a_qk, qe, kdec, egl, tinv = _dn_local(split(q_ref), split(k_ref), split(v_ref), gr_ref[0],
                                                     br_ref[0], None)
        for ref, val in ((u_ref, u), (w_ref, w), (qe_ref, qe), (kd_ref, kdec)):
            ref[0] = val.reshape(grp * DN_CHUNK, HEAD_DIM)
        a_ref[0] = a_qk
        egl_ref[0] = egl
        t_ref[0] = tinv

    return pl.pallas_call(
        body, name="dn_local_fwd", grid=(N_HEADS, t // (grp * DN_CHUNK)),
        in_specs=[blk, blk, blk, row, row],
        out_specs=[blk, blk, sq, blk, blk, lane, sq],
        out_shape=[big, big, sqs, big, big, lanes, sqs],
        compiler_params=_params(("parallel", "parallel")),
    )(q, k, v, grow, brow)


def _dn_local_bwd(q, k, v, grow, brow, tinv, du, dw, da, dqe, dkd, degl):
    t = q.shape[1]
    grp, blk, row, sq, lane = _dn_local_specs(t)
    big, rows_, _, _ = _dn_shapes(t)

    def body(q_ref, k_ref, v_ref, gr_ref, br_ref, t_ref, du_ref, dw_ref, da_ref, dqe_ref, dkd_ref,
             degl_ref, dq_ref, dk_ref, dv_ref, dgr_ref, dbr_ref):
        split = lambda r: r[0].reshape(grp, DN_CHUNK, HEAD_DIM)
        tinv_v = t_ref[0]
        fn = lambda q_, k_, v_, gr_, br_: _dn_local(q_, k_, v_, gr_, br_, tinv_v)[:6]
        _, vjp = jax.vjp(fn, split(q_ref), split(k_ref), split(v_ref), gr_ref[0], br_ref[0])
        dq, dk, dv, dgr, dbr = vjp((split(du_ref), split(dw_ref), da_ref[0], split(dqe_ref), split(dkd_ref),
                                    degl_ref[0]))
        for ref, val in ((dq_ref, dq), (dk_ref, dk), (dv_ref, dv)):
            ref[0] = val.reshape(grp * DN_CHUNK, HEAD_DIM)
        dgr_ref[0] = dgr
        dbr_ref[0] = dbr

    return pl.pallas_call(
        body, name="dn_local_bwd", grid=(N_HEADS, t // (grp * DN_CHUNK)),
        in_specs=[blk, blk, blk, row, row, sq, blk, blk, sq, blk, blk, lane],
        out_specs=[blk, blk, blk, row, row],
        out_shape=[big, big, big, rows_, rows_],
        compiler_params=_params(("parallel", "parallel")),
    )(q, k, v, grow, brow, tinv, du, dw, da, dqe, dkd, degl)


def _dn_seq_specs(nchunk, rev):
    def idx(n):
        return nchunk - 1 - n if rev else n

    blk = pl.BlockSpec((N_HEADS, DN_CHUNK, HEAD_DIM), lambda n: (0, idx(n), 0))
    sq = pl.BlockSpec((N_HEADS, 1, DN_CHUNK, DN_CHUNK), lambda n: (0, idx(n), 0, 0))
    lane = pl.BlockSpec((N_HEADS, 1, 1, HEAD_DIM), lambda n: (0, idx(n), 0, 0))
    st = pl.BlockSpec((N_HEADS, 1, HEAD_DIM, HEAD_DIM), lambda n: (0, idx(n), 0, 0))
    return blk, sq, lane, st


def _dn_seq_fwd(u, w, a_qk, qe, kdec, egl):
    t = u.shape[1]
    nchunk = t // DN_CHUNK
    blk, sq, lane, st = _dn_seq_specs(nchunk, False)

    def body(u_ref, w_ref, a_ref, qe_ref, kd_ref, egl_ref, o_ref, s_ref, state):
        @pl.when(pl.program_id(0) == 0)
        def _():
            state[...] = jnp.zeros_like(state)

        s_in = state[...]
        s_ref[:, 0] = s_in
        o, s_out = _dn_seq(u_ref[...], w_ref[...], a_ref[:, 0], qe_ref[...], kd_ref[...], egl_ref[:, 0], s_in)
        o_ref[...] = o
        state[...] = s_out

    return pl.pallas_call(
        body, name="dn_seq_fwd", grid=(nchunk,),
        in_specs=[blk, blk, sq, blk, blk, lane],
        out_specs=[blk, st],
        out_shape=[jax.ShapeDtypeStruct((N_HEADS, t, HEAD_DIM), F32),
                   jax.ShapeDtypeStruct((N_HEADS, nchunk, HEAD_DIM, HEAD_DIM), F32)],
        scratch_shapes=[pltpu.VMEM((N_HEADS, HEAD_DIM, HEAD_DIM), F32)],
        compiler_params=_params(("arbitrary",)),
    )(u, w, a_qk, qe, kdec, egl)


def _dn_seq_bwd(u, w, a_qk, qe, kdec, egl, states, do):
    t = u.shape[1]
    nchunk = t // DN_CHUNK
    blk, sq, lane, st = _dn_seq_specs(nchunk, True)
    big, _, sqs, lanes = _dn_shapes(t)

    def body(u_ref, w_ref, a_ref, qe_ref, kd_ref, egl_ref, s_ref, do_ref,
             du_ref, dw_ref, da_ref, dqe_ref, dkd_ref, degl_ref, dstate):
        @pl.when(pl.program_id(0) == 0)
        def _():
            dstate[...] = jnp.zeros_like(dstate)

        _, vjp = jax.vjp(_dn_seq, u_ref[...], w_ref[...], a_ref[:, 0], qe_ref[...], kd_ref[...], egl_ref[:, 0],
                         s_ref[:, 0])
        du, dw, da, dqe, dkd, degl, ds = vjp((do_ref[...], dstate[...]))
        du_ref[...] = du
        dw_ref[...] = dw
        da_ref[:, 0] = da
        dqe_ref[...] = dqe
        dkd_ref[...] = dkd
        degl_ref[:, 0] = degl
        dstate[...] = ds

    return pl.pallas_call(
        body, name="dn_seq_bwd", grid=(nchunk,),
        in_specs=[blk, blk, sq, blk, blk, lane, st, blk],
        out_specs=[blk, blk, sq, blk, blk, lane],
        out_shape=[big, big, sqs, big, big, lanes],
        scratch_shapes=[pltpu.VMEM((N_HEADS, HEAD_DIM, HEAD_DIM), F32)],
        compiler_params=_params(("arbitrary",)),
    )(u, w, a_qk, qe, kdec, egl, states, do)


def _dn_post_fn(o, gate, w):
    outs = []
    for h in range(N_HEADS):
        sl = slice(h * HEAD_DIM, (h + 1) * HEAD_DIM)
        outs.append(_rms(o[:, sl], w) * _silu(gate[:, sl]))
    return jnp.concatenate(outs, axis=1)


def _dn_post_fwd(o, gate, w):
    t = gate.shape[0]
    tb = _rows(t)

    def body(o_ref, g_ref, w_ref, y_ref):
        y_ref[...] = _dn_post_fn(_from_heads(o_ref), g_ref[...], w_ref[...]).astype(BF16)

    row = pl.BlockSpec((tb, D_MODEL), lambda i: (i, 0))
    hm = pl.BlockSpec((N_HEADS, tb, HEAD_DIM), lambda i: (0, i, 0))
    return pl.pallas_call(
        body, name="dn_post_fwd", grid=(t // tb,),
        in_specs=[hm, row, pl.BlockSpec((1, HEAD_DIM), lambda i: (0, 0))],
        out_specs=row, out_shape=jax.ShapeDtypeStruct((t, D_MODEL), BF16),
        compiler_params=_params(("parallel",)),
    )(o, gate, w)


def _dn_post_bwd(o, gate, w, dy):
    t = gate.shape[0]
    tb = _rows(t)

    def body(o_ref, g_ref, w_ref, dy_ref, do_ref, dg_ref, dw_ref):
        i = pl.program_id(0)
        _, vjp = jax.vjp(_dn_post_fn, _from_heads(o_ref), g_ref[...], w_ref[...])
        do, dg, dw = vjp(dy_ref[...])
        _to_heads(do_ref, do)
        dg_ref[...] = dg.astype(BF16)

        @pl.when(i == 0)
        def _():
            dw_ref[...] = jnp.zeros_like(dw_ref)

        dw_ref[...] += dw

    row = pl.BlockSpec((tb, D_MODEL), lambda i: (i, 0))
    hm = pl.BlockSpec((N_HEADS, tb, HEAD_DIM), lambda i: (0, i, 0))
    vec = pl.BlockSpec((1, HEAD_DIM), lambda i: (0, 0))
    return pl.pallas_call(
        body, name="dn_post_bwd", grid=(t // tb,),
        in_specs=[hm, row, vec, row],
        out_specs=[hm, row, vec],
        out_shape=[jax.ShapeDtypeStruct((N_HEADS, t, HEAD_DIM), F32), jax.ShapeDtypeStruct((t, D_MODEL), BF16),
                   jax.ShapeDtypeStruct((1, HEAD_DIM), F32)],
        compiler_params=_params(("arbitrary",)),
    )(o, gate, w, dy)


def _split_bf16(x):
    hi = x.astype(BF16)
    lo = (x - hi.astype(F32)).astype(BF16)
    return hi, lo


SB_Q_BLOCK = 512
SB_K_BLOCK = 256
SB_NEGLIGIBLE = -60.0


def _sb_logits(q, kb, mask, scale):
    z = _dot_nt(q, kb) * scale
    ls = jnp.minimum(z, 0.0) - jnp.log(1.0 + jnp.exp(-jnp.abs(z)))
    lk = ls - z
    if mask is not None:
        lk = jnp.where(mask, lk, 0.0)
    return ls, lk


def _sb_blocks(t):
    bq = min(SB_Q_BLOCK, t)
    bk = min(SB_K_BLOCK, bq)
    return bq, bk, bq // bk


def _sb_fwd(qkv):
    t = qkv.shape[0]
    bq, bk, nd = _sb_blocks(t)
    scale = HEAD_DIM ** -0.5

    def body(q_ref, k_ref, v_ref, o_ref, tot_ref, used_ref):
        i = pl.program_id(1)
        q = q_ref[...]
        rj = lax.broadcasted_iota(jnp.int32, (bk, bk), 0)
        cj = lax.broadcasted_iota(jnp.int32, (bk, bk), 1)
        after = (rj > cj).astype(BF16)
        trow = lax.broadcasted_iota(jnp.int32, (bq, bk), 0)
        scol = lax.broadcasted_iota(jnp.int32, (bq, bk), 1)

        def tile(j, run, acc, mask):
            off = pl.multiple_of(j * bk, bk)
            kb = k_ref[pl.ds(off, bk), :]
            vb = v_ref[pl.ds(off, bk), :]
            ls, lk = _sb_logits(q, kb, mask, scale)
            hi, lo = _split_bf16(lk)
            between = _dot(hi, after) + _dot(lo, after) + run
            a = jnp.exp(ls + between)
            if mask is not None:
                a = jnp.where(mask, a, 0.0)
            acc = acc + _dot(a.astype(BF16), vb)
            return run + jnp.sum(lk, axis=1, keepdims=True), acc

        run, acc = jnp.zeros((bq, 1), F32), jnp.zeros((bq, HEAD_DIM), F32)
        for d in reversed(range(nd)):
            run, acc = tile(i * nd + d, run, acc, scol + d * bk < trow)
        def more(c):
            return jnp.logical_and(c[0] < i * nd, jnp.max(c[1]) > SB_NEGLIGIBLE)

        def far(c):
            run_, acc_ = tile(i * nd - 1 - c[0], c[1], c[2], None)
            return c[0] + 1, run_, acc_

        used, run, acc = lax.while_loop(more, far, (jnp.int32(0), run, acc))
        o_ref[...] = acc.astype(BF16)
        tot_ref[...] = jnp.broadcast_to(run, (bq, HEAD_DIM))
        used_ref[...] = jnp.full(used_ref.shape, used, F32)

    qs = pl.BlockSpec((bq, HEAD_DIM), lambda h, i: (i, h))
    ks = pl.BlockSpec((t, HEAD_DIM), lambda h, i: (0, N_HEADS + h))
    vs = pl.BlockSpec((t, HEAD_DIM), lambda h, i: (0, 2 * N_HEADS + h))
    return pl.pallas_call(
        body, name="sb_fwd", grid=(N_HEADS, t // bq),
        in_specs=[qs, ks, vs], out_specs=[qs, qs, pl.BlockSpec((1, 1, 1, LANES), lambda h, i: (h, i, 0, 0))],
        out_shape=[jax.ShapeDtypeStruct((t, D_MODEL), BF16), jax.ShapeDtypeStruct((t, D_MODEL), F32),
                   jax.ShapeDtypeStruct((N_HEADS, t // bq, 1, LANES), F32)],
        compiler_params=_params(("parallel", "arbitrary")),
    )(qkv, qkv, qkv)


def _sb_bwd(qkv, tot, used, do):
    t = qkv.shape[0]
    bq, bk, nd = _sb_blocks(t)
    scale = HEAD_DIM ** -0.5

    def body(q_ref, k_ref, v_ref, tot_ref, used_ref, do_ref, dq_ref, dk_ref, dv_ref):
        i = pl.program_id(1)

        @pl.when(i == 0)
        def _():
            dk_ref[...] = jnp.zeros_like(dk_ref)
            dv_ref[...] = jnp.zeros_like(dv_ref)

        q = q_ref[...]
        do = do_ref[...]
        total = tot_ref[:, 0:1]
        rj = lax.broadcasted_iota(jnp.int32, (bk, bk), 0)
        cj = lax.broadcasted_iota(jnp.int32, (bk, bk), 1)
        upto = (rj <= cj).astype(BF16)
        before = (rj < cj).astype(BF16)
        trow = lax.broadcasted_iota(jnp.int32, (bq, bk), 0)
        scol = lax.broadcasted_iota(jnp.int32, (bq, bk), 1)

        def tile(j, run_k, run_e, dq, mask):
            off = pl.multiple_of(j * bk, bk)
            kb = k_ref[pl.ds(off, bk), :]
            vb = v_ref[pl.ds(off, bk), :]
            ls, lk = _sb_logits(q, kb, mask, scale)
            hi, lo = _split_bf16(lk)
            between = total - (_dot(hi, upto) + _dot(lo, upto) + run_k)
            a = jnp.exp(ls + between)
            if mask is not None:
                a = jnp.where(mask, a, 0.0)
            e = a * _dot_nt(do, vb)
            ehi, elo = _split_bf16(e)
            pre = _dot(ehi, before) + _dot(elo, before) + run_e
            sig = jnp.exp(ls)
            dz = e * (1.0 - sig) - pre * sig
            if mask is not None:
                dz = jnp.where(mask, dz, 0.0)
            dz = (dz * scale).astype(BF16)
            dq = dq + _dot(dz, kb)
            dk_ref[pl.ds(off, bk), :] += _dot_tn(dz, q)
            dv_ref[pl.ds(off, bk), :] += _dot_tn(a.astype(BF16), do)
            return (run_k + jnp.sum(lk, axis=1, keepdims=True),
                    run_e + jnp.sum(e, axis=1, keepdims=True), dq)

        zero = jnp.zeros((bq, 1), F32)
        visited = jnp.clip(jnp.max(used_ref[...]).astype(jnp.int32), 0, i * nd)
        carry = lax.fori_loop(i * nd - visited, i * nd, lambda j, c: tile(j, c[0], c[1], c[2], None),
                              (zero, zero, jnp.zeros((bq, HEAD_DIM), F32)))
        for d in range(nd):
            carry = tile(i * nd + d, *carry, scol + d * bk < trow)
        dq_ref[...] = carry[2]

    qs = pl.BlockSpec((bq, HEAD_DIM), lambda h, i: (i, h))
    ks = pl.BlockSpec((t, HEAD_DIM), lambda h, i: (0, N_HEADS + h))
    vs = pl.BlockSpec((t, HEAD_DIM), lambda h, i: (0, 2 * N_HEADS + h))
    full = pl.BlockSpec((t, HEAD_DIM), lambda h, i: (0, h))
    big = jax.ShapeDtypeStruct((t, D_MODEL), F32)
    return pl.pallas_call(
        body, name="sb_bwd", grid=(N_HEADS, t // bq),
        in_specs=[qs, ks, vs, qs, pl.BlockSpec((1, 1, 1, LANES), lambda h, i: (h, i, 0, 0)), qs],
        out_specs=[qs, full, full],
        out_shape=[big, big, big],
        compiler_params=_params(("parallel", "arbitrary")),
    )(qkv, qkv, qkv, tot, used, do)


def _merge_fwd(o_dn, o_sb, gl, x, wp_dn, wp_sb, w_out, w2):
    t = x.shape[0]
    tb = _rows(t)

    def body(odn_ref, osb_ref, gl_ref, x_ref, wpd_ref, wps_ref, wo_ref, w2_ref,
             pdn_ref, psb_ref, mix_ref, x1_ref, n2_ref):
        pdn = _dot(odn_ref[...], wpd_ref[...])
        psb = _dot(osb_ref[...], wps_ref[...])
        gates = jax.nn.sigmoid(gl_ref[...])
        mixed = (gates[:, :D_MODEL] * pdn + gates[:, D_MODEL:] * psb).astype(BF16)
        x1 = x_ref[...] + _dot(mixed, wo_ref[...])
        pdn_ref[...] = pdn
        psb_ref[...] = psb
        mix_ref[...] = mixed
        x1_ref[...] = x1
        n2_ref[...] = _rms(x1, w2_ref[...]).astype(BF16)

    row = pl.BlockSpec((tb, D_MODEL), lambda i: (i, 0))
    sq = pl.BlockSpec((D_MODEL, D_MODEL), lambda i: (0, 0))
    f = jax.ShapeDtypeStruct((t, D_MODEL), F32)
    b = jax.ShapeDtypeStruct((t, D_MODEL), BF16)
    return pl.pallas_call(
        body, name="merge_fwd", grid=(t // tb,),
        in_specs=[row, row, pl.BlockSpec((tb, 2 * D_MODEL), lambda i: (i, 0)), row, sq, sq, sq,
                  pl.BlockSpec((1, D_MODEL), lambda i: (0, 0))],
        out_specs=[row] * 5, out_shape=[f, f, b, f, b],
        compiler_params=_params(("parallel",)),
    )(o_dn, o_sb, gl, x, wp_dn, wp_sb, w_out, w2)


def _merge_bwd(dx2, dn2, x1, w2, gl, pdn, psb, wp_dn, wp_sb, w_out):
    t = x1.shape[0]
    tb = _rows(t)

    def body(dx2_ref, dn2_ref, x1_ref, w2_ref, gl_ref, pdn_ref, psb_ref, wpd_ref, wps_ref, wo_ref,
             dx1_ref, dw2_ref, dgl_ref, dpdn_ref, dpsb_ref, dodn_ref, dosb_ref):
        i = pl.program_id(0)
        _, vjp = jax.vjp(_rms, x1_ref[...], w2_ref[...])
        dxn, dw2 = vjp(dn2_ref[...])
        dx1 = dx2_ref[...] + dxn
        dx1_ref[...] = dx1

        @pl.when(i == 0)
        def _():
            dw2_ref[...] = jnp.zeros_like(dw2_ref)

        dw2_ref[...] += dw2
        dmix = _dot_nt(dx1.astype(BF16), wo_ref[...])
        gates = jax.nn.sigmoid(gl_ref[...])
        g_dn, g_sb = gates[:, :D_MODEL], gates[:, D_MODEL:]
        dpdn = (dmix * g_dn).astype(BF16)
        dpsb = (dmix * g_sb).astype(BF16)
        dgl_ref[:, :D_MODEL] = (dmix * pdn_ref[...] * g_dn * (1.0 - g_dn)).astype(BF16)
        dgl_ref[:, D_MODEL:] = (dmix * psb_ref[...] * g_sb * (1.0 - g_sb)).astype(BF16)
        dpdn_ref[...] = dpdn
        dpsb_ref[...] = dpsb
        dodn_ref[...] = _dot_nt(dpdn, wpd_ref[...])
        dosb_ref[...] = _dot_nt(dpsb, wps_ref[...]).astype(BF16)

    row = pl.BlockSpec((tb, D_MODEL), lambda i: (i, 0))
    wide = pl.BlockSpec((tb, 2 * D_MODEL), lambda i: (i, 0))
    sq = pl.BlockSpec((D_MODEL, D_MODEL), lambda i: (0, 0))
    vec = pl.BlockSpec((1, D_MODEL), lambda i: (0, 0))
    f = jax.ShapeDtypeStruct((t, D_MODEL), F32)
    b = jax.ShapeDtypeStruct((t, D_MODEL), BF16)
    return pl.pallas_call(
        body, name="merge_bwd", grid=(t // tb,),
        in_specs=[row, row, row, vec, wide, row, row, sq, sq, sq],
        out_specs=[row, vec, wide, row, row, row, row],
        out_shape=[f, jax.ShapeDtypeStruct((1, D_MODEL), F32), jax.ShapeDtypeStruct((t, 2 * D_MODEL), BF16),
                   b, b, f, b],
        compiler_params=_params(("arbitrary",)),
    )(dx2, dn2, x1, w2, gl, pdn, psb, wp_dn, wp_sb, w_out)


def _conv_taps(buf, w_ref, first, rows):
    y = w_ref[0:1, :] * buf[pl.ds(first, rows), :]
    for s in range(1, w_ref.shape[0]):
        y = y + w_ref[s:s + 1, :] * buf[pl.ds(first + s, rows), :]
    return y


def _ffn_mid_fwd(pre_g, pre_u, wg, wu):
    t, c = pre_g.shape
    kk = wg.shape[0]
    tb, cb = _rows(t), _pick(c, ELEMENTWISE_COLS)
    per = tb // HALO

    def body(g_ref, gh_ref, u_ref, uh_ref, wg_ref, wu_ref, a_ref, gbuf, ubuf):
        i = pl.program_id(0)
        for buf, ref, halo in ((gbuf, g_ref, gh_ref), (ubuf, u_ref, uh_ref)):
            buf[pl.ds(HALO, tb), :] = ref[...]
            buf[pl.ds(0, HALO), :] = jnp.where(i == 0, 0.0, halo[...])
        ug = _conv_taps(gbuf, wg_ref, HALO - (kk - 1), tb)
        uu = _conv_taps(ubuf, wu_ref, HALO - (kk - 1), tb)
        a_ref[...] = (_silu(ug) * uu).astype(BF16)

    blk = pl.BlockSpec((tb, cb), lambda i, j: (i, j))
    halo = pl.BlockSpec((HALO, cb), lambda i, j: (jnp.maximum(i * per - 1, 0), j))
    wspec = pl.BlockSpec((kk, cb), lambda i, j: (0, j))
    return pl.pallas_call(
        body, name="ffn_mid_fwd", grid=(t // tb, c // cb),
        in_specs=[blk, halo, blk, halo, wspec, wspec], out_specs=blk,
        out_shape=jax.ShapeDtypeStruct((t, c), BF16),
        scratch_shapes=[pltpu.VMEM((tb + HALO, cb), F32)] * 2,
        compiler_params=_params(("parallel", "parallel")),
    )(pre_g, pre_g, pre_u, pre_u, wg, wu)


def _ffn_mid_bwd(pre_g, pre_u, wg, wu, da):
    t, c = pre_g.shape
    kk = wg.shape[0]
    tb, cb = _rows(t), _pick(c, ELEMENTWISE_COLS)
    per = tb // HALO
    nblk = t // tb
    ext = tb + HALO

    def body(g_ref, gb_ref, ga_ref, u_ref, ub_ref, ua_ref, da_ref, daa_ref, wg_ref, wu_ref,
             dg_ref, du_ref, dwg_ref, dwu_ref, gbuf, ubuf, dabuf, dgbuf, dubuf):
        i = pl.program_id(1)
        last = i == nblk - 1
        for buf, ref, before, after in ((gbuf, g_ref, gb_ref, ga_ref), (ubuf, u_ref, ub_ref, ua_ref)):
            buf[pl.ds(0, HALO), :] = jnp.where(i == 0, 0.0, before[...])
            buf[pl.ds(HALO, tb), :] = ref[...]
            buf[pl.ds(HALO + tb, HALO), :] = jnp.where(last, 0.0, after[...])
        dabuf[pl.ds(0, tb), :] = da_ref[...]
        dabuf[pl.ds(tb, HALO), :] = jnp.where(last, 0.0, daa_ref[...])
        ug = _conv_taps(gbuf, wg_ref, HALO - (kk - 1), ext)
        uu = _conv_taps(ubuf, wu_ref, HALO - (kk - 1), ext)
        _, vjp = jax.vjp(lambda g, u: _silu(g) * u, ug, uu)
        dgbuf[...], dubuf[...] = vjp(dabuf[...])

        @pl.when(i == 0)
        def _():
            dwg_ref[...] = jnp.zeros_like(dwg_ref)
            dwu_ref[...] = jnp.zeros_like(dwu_ref)

        for dbuf, xbuf, w_ref, dx_ref, dw_ref in ((dgbuf, gbuf, wg_ref, dg_ref, dwg_ref),
                                                  (dubuf, ubuf, wu_ref, du_ref, dwu_ref)):
            dx = w_ref[0:1, :] * dbuf[pl.ds(kk - 1, tb), :]
            for s in range(1, kk):
                dx = dx + w_ref[s:s + 1, :] * dbuf[pl.ds(kk - 1 - s, tb), :]
            dx_ref[...] = dx.astype(BF16)
            dy = dbuf[pl.ds(0, tb), :]
            for s in range(kk):
                dw_ref[s:s + 1, :] += jnp.sum(dy * xbuf[pl.ds(HALO - (kk - 1) + s, tb), :], axis=0, keepdims=True)

    blk = pl.BlockSpec((tb, cb), lambda j, i: (i, j))
    before = pl.BlockSpec((HALO, cb), lambda j, i: (jnp.maximum(i * per - 1, 0), j))
    after = pl.BlockSpec((HALO, cb), lambda j, i: (jnp.minimum((i + 1) * per, t // HALO - 1), j))
    wspec = pl.BlockSpec((kk, cb), lambda j, i: (0, j))
    dwspec = pl.BlockSpec((HALO, cb), lambda j, i: (0, j))
    half = jax.ShapeDtypeStruct((t, c), BF16)
    dwshape = jax.ShapeDtypeStruct((HALO, c), F32)
    return pl.pallas_call(
        body, name="ffn_mid_bwd", grid=(c // cb, nblk),
        in_specs=[blk, before, after, blk, before, after, blk, after, wspec, wspec],
        out_specs=[blk, blk, dwspec, dwspec],
        out_shape=[half, half, dwshape, dwshape],
        scratch_shapes=[pltpu.VMEM((ext + HALO, cb), F32)] * 2 + [pltpu.VMEM((ext, cb), F32)] * 3,
        compiler_params=_params(("parallel", "arbitrary")),
    )(pre_g, pre_g, pre_g, pre_u, pre_u, pre_u, da, da, wg, wu)


def _down_loss(a, w_down, x1, wf, target):
    t = x1.shape[0]
    tb = _rows(t)

    def body(a_ref, wd_ref, x1_ref, wf_ref, tgt_ref, dx2_ref, dwf_ref, loss_ref):
        i = pl.program_id(0)
        x2 = x1_ref[...] + _dot(a_ref[...], wd_ref[...])
        y, vjp = jax.vjp(_rms, x2, wf_ref[...])
        err = y - tgt_ref[...]
        dx2, dwf = vjp(err * (1.0 / D_MODEL))
        dx2_ref[...] = dx2
        part = jnp.sum(jnp.sum(err * err, axis=1, keepdims=True), axis=0, keepdims=True) * (0.5 / D_MODEL)

        @pl.when(i == 0)
        def _():
            dwf_ref[...] = jnp.zeros_like(dwf_ref)
            loss_ref[...] = jnp.zeros_like(loss_ref)

        dwf_ref[...] += dwf
        loss_ref[...] += jnp.broadcast_to(part, loss_ref.shape)

    row = pl.BlockSpec((tb, D_MODEL), lambda i: (i, 0))
    vec = pl.BlockSpec((1, D_MODEL), lambda i: (0, 0))
    return pl.pallas_call(
        body, name="down_loss", grid=(t // tb,),
        in_specs=[pl.BlockSpec((tb, D_FF), lambda i: (i, 0)), pl.BlockSpec((D_FF, D_MODEL), lambda i: (0, 0)),
                  row, vec, row],
        out_specs=[row, vec, pl.BlockSpec((1, LANES), lambda i: (0, 0))],
        out_shape=[jax.ShapeDtypeStruct((t, D_MODEL), F32), jax.ShapeDtypeStruct((1, D_MODEL), F32),
                   jax.ShapeDtypeStruct((1, LANES), F32)],
        compiler_params=_params(("arbitrary",)),
    )(a, w_down, x1, wf, target)


def _local_step(x, target, wts):
    t = x.shape[0]
    nchunk = t // DN_CHUNK

    n1, hab = _norm1_fwd(x, wts["norm1"], wts["w_ab"])
    dnqkv = _mm(n1, wts["w_dnqkv"], name="h_dnqkv")
    dngate = _mm(n1, wts["w_dngate"], name="h_dngate")
    sbqkv = _mm(n1, wts["w_sbqkv"], out_dtype=BF16, name="h_sbqkv")
    gl = _mm(n1, wts["w_gl"], name="h_gl")

    cdn = _conv_fwd(dnqkv, wts["dn_conv"], "dn_conv_fwd")
    qn, kn, vv, gb = _dn_prep_fwd(cdn, hab, wts["alog"], wts["dtb"])
    per_head = gb[:, :2 * N_HEADS].T.reshape(2 * N_HEADS, nchunk, DN_CHUNK)
    grow, brow = per_head[:N_HEADS, :, None, :], per_head[N_HEADS:, :, None, :]
    u_dn, w_dn, a_qk, qe, kdec, egl, tinv = _dn_local_fwd(qn, kn, vv, grow, brow)
    o_raw, states = _dn_seq_fwd(u_dn, w_dn, a_qk, qe, kdec, egl)
    o_dn = _dn_post_fwd(o_raw, dngate, wts["dn_norm"])

    o_sb, tot, sb_used = _sb_fwd(sbqkv)

    pdn, psb, mixed, x1, n2 = _merge_fwd(o_dn, o_sb, gl, x, wts["wp_dn"], wts["wp_sb"], wts["w_out"],
                                         wts["norm2"])
    pre_g = _mm(n2, wts["w_up_g"], name="ffn_up_g")
    pre_u = _mm(n2, wts["w_up_u"], name="ffn_up_u")
    act = _ffn_mid_fwd(pre_g, pre_u, wts["ffn_conv_g"], wts["ffn_conv_u"])
    dx2, d_normf, loss_part = _down_loss(act, wts["w_down"], x1, wts["normf"], target)

    grads = {"normf": d_normf}
    da = _mm(dx2, wts["w_down"], tb=True, name="d_act")
    grads["w_down"] = _mm(act, dx2, ta=True, out_dtype=BF16, name="dw_down")
    dpre_g, dpre_u, dcw_g, dcw_u = _ffn_mid_bwd(pre_g, pre_u, wts["ffn_conv_g"], wts["ffn_conv_u"], da)
    grads["ffn_conv"] = jnp.concatenate([dcw_g[:FFN_CONV], dcw_u[:FFN_CONV]], axis=1)
    dn2 = _mm(dpre_g, wts["w_up_g"], tb=True, name="dn2_g")
    dn2 = _mm(dpre_u, wts["w_up_u"], tb=True, add=dn2, name="dn2_u")
    grads["w_up_g"] = _mm(n2, dpre_g, ta=True, out_dtype=BF16, name="dw_up_g")
    grads["w_up_u"] = _mm(n2, dpre_u, ta=True, out_dtype=BF16, name="dw_up_u")

    dx1, grads["norm2"], dgl, dpdn, dpsb, do_dn, do_sb = _merge_bwd(
        dx2, dn2, x1, wts["norm2"], gl, pdn, psb, wts["wp_dn"], wts["wp_sb"], wts["w_out"])
    grads["w_out"] = _mm(mixed, dx1, ta=True, out_dtype=BF16, name="dw_out")
    grads["wp_dn"] = _mm(o_dn, dpdn, ta=True, out_dtype=BF16, name="dw_proj_dn")
    grads["wp_sb"] = _mm(o_sb, dpsb, ta=True, out_dtype=BF16, name="dw_proj_sb")

    dsq, dsk, dsv = _sb_bwd(sbqkv, tot, sb_used, do_sb)
    dsbqkv = jnp.concatenate([dsq, dsk, dsv], axis=1).astype(BF16)

    do_raw, ddngate, grads["dn_norm"] = _dn_post_bwd(o_raw, dngate, wts["dn_norm"], do_dn)
    seq_grads = _dn_seq_bwd(u_dn, w_dn, a_qk, qe, kdec, egl, states, do_raw)
    dqn, dkn, dvv, dgrow, dbrow = _dn_local_bwd(qn, kn, vv, grow, brow, tinv, *seq_grads)
    dgb = jnp.concatenate([dgrow.reshape(N_HEADS, t), dbrow.reshape(N_HEADS, t)], axis=0).T
    dgb = jnp.pad(dgb, ((0, 0), (0, LANES - 2 * N_HEADS)))
    dcdn, dhab, grads["alog"], grads["dtb"] = _dn_prep_bwd(cdn, hab, wts["alog"], wts["dtb"], dqn, dkn, dvv, dgb)
    ddnqkv, dcw_dn = _conv_bwd(dcdn, dnqkv, wts["dn_conv"], "dn_conv_bwd", BF16)
    grads["dn_conv"] = dcw_dn[:DN_CONV]

    dn1 = _mm(ddnqkv, wts["w_dnqkv"], tb=True, name="dn1_dnqkv")
    dn1 = _mm(ddngate, wts["w_dngate"], tb=True, add=dn1, name="dn1_dngate")
    dn1 = _mm(dsbqkv, wts["w_sbqkv"], tb=True, add=dn1, name="dn1_sbqkv")
    dn1 = _mm(dgl, wts["w_gl"], tb=True, add=dn1, name="dn1_gl")
    grads["w_dnqkv"] = _mm(n1, ddnqkv, ta=True, out_dtype=BF16, name="dw_dnqkv")
    grads["w_dngate"] = _mm(n1, ddngate, ta=True, out_dtype=BF16, name="dw_dngate")
    grads["w_sbqkv"] = _mm(n1, dsbqkv, ta=True, out_dtype=BF16, name="dw_sbqkv")
    grads["w_gl"] = _mm(n1, dgl, ta=True, out_dtype=BF16, name="dw_gl")
    grads["w_ab"] = _mm(n1, dhab, ta=True, out_dtype=BF16, name="dw_ab")
    grad_x, grads["norm1"] = _norm1_bwd(x, wts["norm1"], dn1, dx1, dhab, wts["w_ab"])
    return loss_part, grad_x, grads


def _place():
    return lax.axis_index("x"), lax.axis_index("y"), lax.axis_index("c")


def _hbm_specs(n):
    return [pl.BlockSpec(memory_space=pltpu.HBM)] * n


def _gather_shards(shards):
    n = len(shards)

    def body(*refs):
        ins, outs, (send_sems, recv_sems) = refs[:n], refs[n:2 * n], refs[2 * n:]
        x, y, c = _place()
        me = 2 * x + y
        sibling = (x, y, 1 - c)
        chips = [(1 - x, y), (x, 1 - y), (1 - x, 1 - y)]

        def slab(a, chip_index, part):
            half = ins[a].shape[0] // 2
            return outs[a].at[chip_index, pl.ds(part * half, half), :]

        def copy(k, src, dst, to):
            return pltpu.make_async_remote_copy(src_ref=src, dst_ref=dst, send_sem=send_sems.at[k],
                                                recv_sem=recv_sems.at[k], device_id=to, device_id_type=MESH)

        first, passed = [], []
        for a in range(n):
            half = ins[a].shape[0] // 2
            my_half = ins[a].at[pl.ds(c * half, half), :]
            for j, (px, py) in enumerate(chips):
                cp = copy(6 * a + j, my_half, slab(a, me, c), (px, py, c))
                cp.start()
                first.append(cp)
        for a in range(n):
            for j, (px, py) in enumerate(chips):
                landed = slab(a, 2 * px + py, c)
                copy(6 * a + j, landed, landed, (px, py, c)).wait_recv()
                fwd = copy(6 * a + 3 + j, landed, landed, sibling)
                fwd.start()
                passed.append(fwd)
        for a in range(n):
            for j, (px, py) in enumerate(chips):
                there = slab(a, 2 * px + py, 1 - c)
                copy(6 * a + 3 + j, there, there, sibling).wait_recv()
        for cp in first + passed:
            cp.wait_send()

    return pl.pallas_call(
        body, name="gather_weights", in_specs=_hbm_specs(n), out_specs=_hbm_specs(n),
        out_shape=[jax.ShapeDtypeStruct((N_CHIPS,) + s.shape, s.dtype) for s in shards],
        scratch_shapes=[pltpu.SemaphoreType.DMA((6 * n,)), pltpu.SemaphoreType.DMA((6 * n,))],
    )(*shards)


def _pair_exchange_halves(gs):
    n = len(gs)

    def body(*refs):
        ins, outs, (send_sems, recv_sems) = refs[:n], refs[n:2 * n], refs[2 * n:]
        x, y, c = _place()
        cps = []
        for a in range(n):
            half = ins[a].shape[1] // 2
            cp = pltpu.make_async_remote_copy(src_ref=ins[a].at[:, pl.ds((1 - c) * half, half), :], dst_ref=outs[a],
                                              send_sem=send_sems.at[a], recv_sem=recv_sems.at[a],
                                              device_id=(x, y, 1 - c), device_id_type=MESH)
            cp.start()
            cps.append(cp)
        for cp in cps:
            cp.wait()

    return pl.pallas_call(
        body, name="grad_pair_exchange", in_specs=_hbm_specs(n), out_specs=_hbm_specs(n),
        out_shape=[jax.ShapeDtypeStruct((g.shape[0], g.shape[1] // 2, g.shape[2]), g.dtype) for g in gs],
        scratch_shapes=[pltpu.SemaphoreType.DMA((n,)), pltpu.SemaphoreType.DMA((n,))],
    )(*gs)


def _pick_rows(n, target=1024):
    best = 16
    for b in range(16, min(n, target) + 1, 16):
        if n % b == 0:
            best = b
    return best


def _pair_add(g, got, c_idx, tag):
    nsh, rows, cols = g.shape
    half = rows // 2
    rb = _pick_rows(half)

    def body(c_ref, g_ref, got_ref, o_ref):
        o_ref[...] = (g_ref[...].astype(F32) + got_ref[...].astype(F32)).astype(BF16)

    nb = half // rb
    grid_spec = pltpu.PrefetchScalarGridSpec(
        num_scalar_prefetch=1, grid=(nsh, nb),
        in_specs=[pl.BlockSpec((1, rb, cols), lambda s, i, c_ref: (s, c_ref[0] * nb + i, 0)),
                  pl.BlockSpec((1, rb, cols), lambda s, i, c_ref: (s, i, 0))],
        out_specs=pl.BlockSpec((1, rb, cols), lambda s, i, c_ref: (s, i, 0)))
    return pl.pallas_call(
        body, name="grad_pair_add_" + tag, grid_spec=grid_spec,
        out_shape=jax.ShapeDtypeStruct((nsh, half, cols), BF16),
        compiler_params=_params(("parallel", "parallel")),
    )(c_idx, g, got)


def _chip_exchange(ps):
    n = len(ps)

    def body(*refs):
        ins, outs, (send_sems, recv_sems) = refs[:n], refs[n:2 * n], refs[2 * n:]
        x, y, c = _place()
        chips = [(1 - x, y), (x, 1 - y), (1 - x, 1 - y)]
        sends = []
        for a in range(n):
            for j, (px, py) in enumerate(chips):
                cp = pltpu.make_async_remote_copy(src_ref=ins[a].at[2 * px + py], dst_ref=outs[a].at[j],
                                                  send_sem=send_sems.at[3 * a + j], recv_sem=recv_sems.at[3 * a + j],
                                                  device_id=(px, py, c), device_id_type=MESH)
                cp.start()
                sends.append(cp)
        for cp in sends:
            cp.wait_recv()
        for cp in sends:
            cp.wait_send()

    return pl.pallas_call(
        body, name="grad_chip_exchange", in_specs=_hbm_specs(n), out_specs=_hbm_specs(n),
        out_shape=[jax.ShapeDtypeStruct((N_CHIPS - 1,) + p.shape[1:], p.dtype) for p in ps],
        scratch_shapes=[pltpu.SemaphoreType.DMA((3 * n,)), pltpu.SemaphoreType.DMA((3 * n,))],
    )(*ps)


def _sum_partials(p, got, chip_idx, tag):
    nsh, half, cols = got.shape
    rb = _pick_rows(half)

    def body(me_ref, p_ref, got_ref, o_ref):
        acc = p_ref[0].astype(F32)
        for s in range(nsh):
            acc = acc + got_ref[s].astype(F32)
        o_ref[...] = acc

    grid_spec = pltpu.PrefetchScalarGridSpec(
        num_scalar_prefetch=1, grid=(half // rb,),
        in_specs=[pl.BlockSpec((1, rb, cols), lambda i, me_ref: (me_ref[0], i, 0)),
                  pl.BlockSpec((nsh, rb, cols), lambda i, me_ref: (0, i, 0))],
        out_specs=pl.BlockSpec((rb, cols), lambda i, me_ref: (i, 0)))
    return pl.pallas_call(
        body, name="grad_sum_chips_" + tag, grid_spec=grid_spec,
        out_shape=jax.ShapeDtypeStruct((half, cols), F32),
        compiler_params=_params(("parallel",)),
    )(chip_idx, p, got)


def _pair_share(rs):
    n = len(rs)

    def body(*refs):
        ins, outs, (send_sems, recv_sems) = refs[:n], refs[n:2 * n], refs[2 * n:]
        x, y, c = _place()
        cps = []
        for a in range(n):
            cp = pltpu.make_async_remote_copy(src_ref=ins[a], dst_ref=outs[a], send_sem=send_sems.at[a],
                                              recv_sem=recv_sems.at[a], device_id=(x, y, 1 - c),
                                              device_id_type=MESH)
            cp.start()
            cps.append(cp)
        for cp in cps:
            cp.wait()

    return pl.pallas_call(
        body, name="grad_pair_share", in_specs=_hbm_specs(n), out_specs=_hbm_specs(n),
        out_shape=[jax.ShapeDtypeStruct(r.shape, r.dtype) for r in rs],
        scratch_shapes=[pltpu.SemaphoreType.DMA((n,)), pltpu.SemaphoreType.DMA((n,))],
    )(*rs)


def _small_allreduce(v):
    rows, cols = v.shape
    ndev = 8

    def body(in_ref, out_ref, slots, send_sems, recv_sems):
        x, y, c = _place()
        me = 4 * x + 2 * y + c
        slots[me] = in_ref[...]
        sends = []
        for k in range(1, ndev):
            peer = (x ^ (k >> 2), y ^ ((k >> 1) & 1), c ^ (k & 1))
            cp = pltpu.make_async_remote_copy(src_ref=in_ref, dst_ref=slots.at[me], send_sem=send_sems.at[k - 1],
                                              recv_sem=recv_sems.at[k - 1], device_id=peer, device_id_type=MESH)
            cp.start()
            sends.append(cp)
        for k in range(1, ndev):
            there = slots.at[me ^ k]
            pltpu.make_async_remote_copy(src_ref=there, dst_ref=there, send_sem=send_sems.at[k - 1],
                                         recv_sem=recv_sems.at[k - 1], device_id=(x, y, c),
                                         device_id_type=MESH).wait_recv()
        for cp in sends:
            cp.wait_send()
        acc = slots[0]
        for s in range(1, ndev):
            acc = acc + slots[s]
        out_ref[...] = acc

    return pl.pallas_call(
        body, name="small_allreduce",
        in_specs=[pl.BlockSpec(memory_space=pltpu.VMEM)],
        out_specs=pl.BlockSpec(memory_space=pltpu.VMEM),
        out_shape=jax.ShapeDtypeStruct((rows, cols), F32),
        scratch_shapes=[pltpu.VMEM((ndev, rows, cols), F32), pltpu.SemaphoreType.DMA((ndev - 1,)),
                        pltpu.SemaphoreType.DMA((ndev - 1,))],
    )(v)


def _adamw(w, g, m, v, name):
    r, c = w.shape
    rb = r if r <= 128 else _pick_rows_8(r, 128)
    c1 = 1.0 - ADAM_B1 ** ADAM_STEP
    c2 = 1.0 - ADAM_B2 ** ADAM_STEP

    def body(w_ref, g_ref, m_ref, v_ref, d_ref, nm_ref, nv_ref):
        gg = g_ref[...]
        nm = ADAM_B1 * m_ref[...] + (1.0 - ADAM_B1) * gg
        nv = ADAM_B2 * v_ref[...] + (1.0 - ADAM_B2) * (gg * gg)
        d_ref[...] = -ADAM_LR * ((nm / c1) / (jnp.sqrt(nv / c2) + ADAM_EPS) + ADAM_WD * w_ref[...])
        nm_ref[...] = nm
        nv_ref[...] = nv

    blk = pl.BlockSpec((rb, c), lambda i: (i, 0))
    shp = jax.ShapeDtypeStruct((r, c), F32)
    return pl.pallas_call(
        body, name=name, grid=(r // rb,), in_specs=[blk] * 4, out_specs=[blk] * 3, out_shape=[shp] * 3,
        compiler_params=_params(("parallel",)),
    )(w, g, m, v)


def _pick_rows_8(n, target):
    best = n
    for b in range(8, min(n, target) + 1, 8):
        if n % b == 0:
            best = b
    return best


W_IN_COLS = 2308
W_UP_COLS = 1408
W_DOWN_ROWS = 704
DN_CONV_COLS = 768
FFN_CONV_COLS = 1408
PROJ_ROWS = 256
ROW_TILE = 16
ROW_SEGS = [("wp_dn", PROJ_ROWS), ("wp_sb", PROJ_ROWS), ("w_out", PROJ_ROWS), ("w_down", W_DOWN_ROWS),
            ("dn_conv", ROW_TILE), ("ffn_conv", ROW_TILE)]
ROW_OFFS = {nm: (sum(n for _, n in ROW_SEGS[:i]), n) for i, (nm, n) in enumerate(ROW_SEGS)}
STACK_ROWS = sum(n for _, n in ROW_SEGS)
assert all(n % ROW_TILE == 0 for _, n in ROW_SEGS) and STACK_ROWS % (2 * ROW_TILE) == 0
Q_END, A_END, G_END, S_END = 3 * D_MODEL, 3 * D_MODEL + 2 * N_HEADS, 4 * D_MODEL + 2 * N_HEADS, 7 * D_MODEL + 2 * N_HEADS


def _flat_rows(a, nrows):
    flat = a.reshape(-1)
    return jnp.pad(flat, (0, nrows * D_MODEL - flat.shape[0])).reshape(nrows, D_MODEL)


def _weight_wire(w_in, wp_dn, wp_sb, w_out, w_up, w_down, dn_conv, ffn_conv):
    stack = jnp.concatenate([wp_dn.astype(BF16), wp_sb.astype(BF16), w_out.astype(BF16), w_down.astype(BF16),
                             _flat_rows(lax.bitcast_convert_type(dn_conv, BF16), ROW_TILE),
                             _flat_rows(lax.bitcast_convert_type(ffn_conv, BF16), ROW_TILE)], axis=0)
    return [w_in.astype(BF16), w_up.astype(BF16), stack]


def _col_range(g, lo, hi, width):
    parts = []
    for s in range(g.shape[0]):
        a, b = max(lo, s * width), min(hi, (s + 1) * width)
        if a < b:
            parts.append(g[s][:, a - s * width:b - s * width])
    return parts[0] if len(parts) == 1 else jnp.concatenate(parts, axis=1)


def _unpack_weights(g_in, g_up, g_stack):
    def seg(nm):
        at, n = ROW_OFFS[nm]
        return g_stack[:, at:at + n, :]

    def f32_rows(nm, k, ncols):
        raw = seg(nm).reshape(N_CHIPS, -1)[:, :2 * k * ncols].reshape(N_CHIPS, k * ncols, 2)
        vals = lax.bitcast_convert_type(raw, F32).reshape(N_CHIPS, k, ncols)
        return vals.transpose(1, 0, 2).reshape(k, N_CHIPS * ncols)

    ffn_conv = f32_rows("ffn_conv", FFN_CONV, FFN_CONV_COLS)
    return {
        "w_dnqkv": _col_range(g_in, 0, Q_END, W_IN_COLS),
        "w_ab": jnp.pad(_col_range(g_in, Q_END, A_END, W_IN_COLS), ((0, 0), (0, LANES - 2 * N_HEADS))),
        "w_dngate": _col_range(g_in, A_END, G_END, W_IN_COLS),
        "w_sbqkv": _col_range(g_in, G_END, S_END, W_IN_COLS),
        "w_gl": _col_range(g_in, S_END, N_CHIPS * W_IN_COLS, W_IN_COLS),
        "wp_dn": seg("wp_dn").reshape(D_MODEL, D_MODEL),
        "wp_sb": seg("wp_sb").reshape(D_MODEL, D_MODEL),
        "w_out": seg("w_out").reshape(D_MODEL, D_MODEL),
        "w_up_g": _col_range(g_up, 0, D_FF, W_UP_COLS), "w_up_u": _col_range(g_up, D_FF, 2 * D_FF, W_UP_COLS),
        "w_down": seg("w_down").reshape(D_FF, D_MODEL),
        "dn_conv": f32_rows("dn_conv", DN_CONV, DN_CONV_COLS),
        "ffn_conv_g": ffn_conv[:, :D_FF], "ffn_conv_u": ffn_conv[:, D_FF:],
    }


def _grad_wire(gr):
    pieces = [(gr["w_dnqkv"], 0), (gr["w_ab"][:, :2 * N_HEADS], Q_END), (gr["w_dngate"], A_END),
              (gr["w_sbqkv"], G_END), (gr["w_gl"], S_END)]

    def in_block(s):
        lo, hi = s * W_IN_COLS, (s + 1) * W_IN_COLS
        parts = []
        for a, at in pieces:
            b0, b1 = max(lo, at), min(hi, at + a.shape[1])
            if b0 < b1:
                parts.append(a[:, b0 - at:b1 - at].astype(BF16))
        return parts[0] if len(parts) == 1 else jnp.concatenate(parts, axis=1)

    def cols(a, ncols):
        return a.reshape(a.shape[0], N_CHIPS, ncols).transpose(1, 0, 2)

    def rows(a, nrows):
        return a.astype(BF16).reshape(N_CHIPS, nrows, a.shape[1])

    def flat(a, nrows):
        a = a.astype(BF16).reshape(N_CHIPS, -1)
        return jnp.pad(a, ((0, 0), (0, nrows * D_MODEL - a.shape[1]))).reshape(N_CHIPS, nrows, D_MODEL)

    g_in = jnp.stack([in_block(s) for s in range(N_CHIPS)])
    up = [gr["w_up_g"], gr["w_up_u"]]
    g_up = jnp.stack([up[s // 2][:, (s % 2) * W_UP_COLS:(s % 2 + 1) * W_UP_COLS].astype(BF16) for s in range(N_CHIPS)])
    g_stack = jnp.concatenate([rows(gr["wp_dn"], PROJ_ROWS), rows(gr["wp_sb"], PROJ_ROWS), rows(gr["w_out"], PROJ_ROWS),
                               rows(gr["w_down"], W_DOWN_ROWS), flat(cols(gr["dn_conv"], DN_CONV_COLS), ROW_TILE),
                               flat(cols(gr["ffn_conv"], FFN_CONV_COLS), ROW_TILE)], axis=1)
    return [g_in, g_up, g_stack]


def _unpack_grad_shard(r_in, r_up, r_stack):
    def seg(nm):
        at, n = ROW_OFFS[nm]
        return r_stack[at:at + n, :]

    return {
        "w_in": r_in, "w_up": r_up,
        "wp_dn": seg("wp_dn"), "wp_sb": seg("wp_sb"), "w_out": seg("w_out"), "w_down": seg("w_down"),
        "dn_conv": seg("dn_conv").reshape(-1)[:DN_CONV * DN_CONV_COLS].reshape(DN_CONV, DN_CONV_COLS),
        "ffn_conv": seg("ffn_conv").reshape(-1)[:FFN_CONV * FFN_CONV_COLS].reshape(FFN_CONV, FFN_CONV_COLS),
    }


def _lane_row(v):
    return jnp.pad(v.reshape(1, -1), ((0, 0), (0, LANES - v.size)))


def kernel(x, norm1_w, w_in, dn_conv_w, dn_A_log, dn_dt_bias, dn_norm_w, w_proj_dn, w_proj_sb, w_out, norm2_w, ffn_w_up, ffn_conv_w, ffn_w_down, norm_f_w, loss_target, m_norm1_w, m_w_in, m_dn_conv_w, m_dn_A_log, m_dn_dt_bias, m_dn_norm_w, m_w_proj_dn, m_w_proj_sb, m_w_out, m_norm2_w, m_ffn_w_up, m_ffn_conv_w, m_ffn_w_down, m_norm_f_w, v_norm1_w, v_w_in, v_dn_conv_w, v_dn_A_log, v_dn_dt_bias, v_dn_norm_w, v_w_proj_dn, v_w_proj_sb, v_w_out, v_norm2_w, v_ffn_w_up, v_ffn_conv_w, v_ffn_w_down, v_norm_f_w):
    wire = _weight_wire(w_in[0], w_proj_dn[0], w_proj_sb[0], w_out[0], ffn_w_up[0], ffn_w_down[0],
                        dn_conv_w[0], ffn_conv_w[0])
    chip_idx = (2 * lax.axis_index("x") + lax.axis_index("y")).astype(jnp.int32)
    gathered = [lax.dynamic_update_slice(g, mine[None], (chip_idx, 0, 0))
                for g, mine in zip(_gather_shards(wire), wire)]
    wts = _unpack_weights(*gathered)
    wts.update(norm1=norm1_w, norm2=norm2_w, normf=norm_f_w.reshape(1, D_MODEL), dn_norm=dn_norm_w,
               alog=_lane_row(dn_A_log), dtb=_lane_row(dn_dt_bias))

    loss_part, grad_x, gr = _local_step(x[0], loss_target[0], wts)

    c_idx = lax.axis_index("c").astype(jnp.int32).reshape(1)
    tags = ["w_in", "w_up", "rows"]
    wire_g = _grad_wire(gr)
    partial_sums = [_pair_add(g, got, c_idx, tag) for g, got, tag in zip(wire_g, _pair_exchange_halves(wire_g), tags)]
    reduced = [_sum_partials(p, got, chip_idx.reshape(1), tag)
               for p, got, tag in zip(partial_sums, _chip_exchange(partial_sums), tags)]
    is_south = lax.axis_index("c") == 0
    gsh = _unpack_grad_shard(*[jnp.concatenate([jnp.where(is_south, mine, other), jnp.where(is_south, other, mine)],
                                               axis=0) for mine, other in zip(reduced, _pair_share(reduced))])

    tail = jnp.concatenate([gr["dn_norm"], gr["alog"][:, :N_HEADS], gr["dtb"][:, :N_HEADS], loss_part[:, :1]], axis=1)
    small = jnp.concatenate([gr["norm1"], gr["norm2"], gr["normf"],
                             jnp.pad(tail, ((0, 0), (0, D_MODEL - tail.shape[1]))),
                             jnp.zeros((SMALL_ROWS - 4, D_MODEL), F32)], axis=0)
    small = _small_allreduce(small)
    at = HEAD_DIM
    g_small = {"norm1_w": small[0:1], "norm2_w": small[1:2], "norm_f_w": small[2],
               "dn_norm_w": small[3:4, :at], "dn_A_log": small[3:4, at:at + N_HEADS],
               "dn_dt_bias": small[3:4, at + N_HEADS:at + 2 * N_HEADS]}
    loss = small[3, at + 2 * N_HEADS]

    big = {"w_in": (w_in, m_w_in, v_w_in, gsh["w_in"]), "dn_conv_w": (dn_conv_w, m_dn_conv_w, v_dn_conv_w, gsh["dn_conv"]),
           "w_proj_dn": (w_proj_dn, m_w_proj_dn, v_w_proj_dn, gsh["wp_dn"]),
           "w_proj_sb": (w_proj_sb, m_w_proj_sb, v_w_proj_sb, gsh["wp_sb"]),
           "w_out": (w_out, m_w_out, v_w_out, gsh["w_out"]),
           "ffn_w_up": (ffn_w_up, m_ffn_w_up, v_ffn_w_up, gsh["w_up"]),
           "ffn_conv_w": (ffn_conv_w, m_ffn_conv_w, v_ffn_conv_w, gsh["ffn_conv"]),
           "ffn_w_down": (ffn_w_down, m_ffn_w_down, v_ffn_w_down, gsh["w_down"])}
    res = {}
    for nm, (w, m, v, g) in big.items():
        d, nm_, nv_ = _adamw(w[0], g, m[0], v[0], "adamw_" + nm)
        res[nm] = (g[None], d[None], nm_[None], nv_[None])

    names = ["norm1_w", "norm2_w", "norm_f_w", "dn_norm_w", "dn_A_log", "dn_dt_bias"]
    given = {"norm1_w": (norm1_w, m_norm1_w, v_norm1_w), "norm2_w": (norm2_w, m_norm2_w, v_norm2_w),
             "norm_f_w": (norm_f_w, m_norm_f_w, v_norm_f_w), "dn_norm_w": (dn_norm_w, m_dn_norm_w, v_dn_norm_w),
             "dn_A_log": (dn_A_log, m_dn_A_log, v_dn_A_log), "dn_dt_bias": (dn_dt_bias, m_dn_dt_bias, v_dn_dt_bias)}

    def stack(k, fill):
        rows = [jnp.pad(given[nm][k].reshape(1, -1), ((0, 0), (0, D_MODEL - given[nm][k].size)),
                        constant_values=fill) for nm in names]
        return jnp.concatenate(rows + [jnp.full((SMALL_ROWS - len(names), D_MODEL), fill, F32)], axis=0)

    g_rows = jnp.concatenate(
        [jnp.pad(g_small[nm].reshape(1, -1), ((0, 0), (0, D_MODEL - g_small[nm].size))) for nm in names]
        + [jnp.zeros((SMALL_ROWS - len(names), D_MODEL), F32)], axis=0)
    d_s, m_s, v_s = _adamw(stack(0, 0.0), g_rows, stack(1, 0.0), stack(2, 1.0), "adamw_small")
    for r, nm in enumerate(names):
        shape = given[nm][0].shape
        n = given[nm][0].size
        res[nm] = (g_small[nm].reshape(shape), d_s[r, :n].reshape(shape), m_s[r, :n].reshape(shape),
                   v_s[r, :n].reshape(shape))

    order = ["norm1_w", "w_in", "dn_conv_w", "dn_A_log", "dn_dt_bias", "dn_norm_w", "w_proj_dn", "w_proj_sb",
             "w_out", "norm2_w", "ffn_w_up", "ffn_conv_w", "ffn_w_down", "norm_f_w"]
    outs = [loss, grad_x[None]]
    for k in range(4):
        outs += [res[nm][k] for nm in order]
    return tuple(outs)
```

```python
import functools

import jax
import jax.numpy as jnp
from jax import lax
from jax.experimental import pallas as pl
from jax.experimental.pallas import tpu as pltpu

F32 = jnp.float32
BF16 = jnp.bfloat16
HIGHEST = lax.Precision.HIGHEST
MESH = pl.DeviceIdType.MESH

EPS = 1e-6
D_MODEL = 1024
N_HEADS = 8
HEAD_DIM = 128
DN_CONV = 4
DN_CHUNK = 64
D_FF = 2816
FFN_CONV = 3
ADAM_LR, ADAM_B1, ADAM_B2, ADAM_EPS, ADAM_WD, ADAM_STEP = 0.001, 0.9, 0.999, 1e-08, 0.01, 10

N_CHIPS = 4
LANES = 128
HALO = 8
VMEM_LIMIT = 48 * 1024 * 1024
SMALL_ROWS = 8


def _params(sem=None):
    return pltpu.CompilerParams(dimension_semantics=sem, vmem_limit_bytes=VMEM_LIMIT)


def _pick(n, target):
    best = None
    for b in range(LANES, min(n, target) + 1, LANES):
        if n % b == 0:
            best = b
    return best or n


ELEMENTWISE_COLS = 1408


def _rows(t, target=256):
    return min(t, target)


def _dot(a, b, precision=None):
    return lax.dot_general(a, b, (((1,), (0,)), ((), ())), precision=precision, preferred_element_type=F32)


def _dot_nt(a, b, precision=None):
    return lax.dot_general(a, b, (((1,), (1,)), ((), ())), precision=precision, preferred_element_type=F32)


def _dot_tn(a, b, precision=None):
    return lax.dot_general(a, b, (((0,), (0,)), ((), ())), precision=precision, preferred_element_type=F32)


def _rms(x, w):
    return x * lax.rsqrt(jnp.mean(x * x, axis=-1, keepdims=True) + EPS) * w


def _silu(x):
    return x * jax.nn.sigmoid(x)


def _softplus(x):
    return jnp.maximum(x, 0.0) + jnp.log(1.0 + jnp.exp(-jnp.abs(x)))


MM_BLOCK = 1408


def _mm(a, b, *, ta=False, tb=False, add=None, out_dtype=F32, name, bm=MM_BLOCK, bn=MM_BLOCK, bk=MM_BLOCK):
    m = a.shape[1] if ta else a.shape[0]
    k = a.shape[0] if ta else a.shape[1]
    n = b.shape[0] if tb else b.shape[1]
    bm, bn, bk = _pick(m, bm), _pick(n, bn), _pick(k, bk)
    nk = k // bk
    dims = (((0 if ta else 1,), (1 if tb else 0,)), ((), ()))

    def body(*refs):
        a_ref, b_ref = refs[:2]
        c_ref = refs[2] if add is not None else None
        o_ref = refs[3] if add is not None else refs[2]
        acc = refs[-1]
        kk = pl.program_id(2)
        part = lax.dot_general(a_ref[...].astype(BF16), b_ref[...].astype(BF16), dims, preferred_element_type=F32)

        def finish(r):
            if add is not None:
                r = r + c_ref[...].astype(F32)
            o_ref[...] = r.astype(out_dtype)

        if nk == 1:
            finish(part)
            return

        @pl.when(kk == 0)
        def _():
            acc[...] = part

        @pl.when(jnp.logical_and(kk > 0, kk < nk - 1))
        def _():
            acc[...] += part

        @pl.when(kk == nk - 1)
        def _():
            finish(acc[...] + part)

    a_spec = (pl.BlockSpec((bk, bm), lambda i, j, kk: (kk, i)) if ta
              else pl.BlockSpec((bm, bk), lambda i, j, kk: (i, kk)))
    b_spec = (pl.BlockSpec((bn, bk), lambda i, j, kk: (j, kk)) if tb
              else pl.BlockSpec((bk, bn), lambda i, j, kk: (kk, j)))
    o_spec = pl.BlockSpec((bm, bn), lambda i, j, kk: (i, j))
    in_specs = [a_spec, b_spec] + ([o_spec] if add is not None else [])
    args = (a, b) + ((add,) if add is not None else ())
    return pl.pallas_call(
        body, name=name, grid=(m // bm, n // bn, nk),
        in_specs=in_specs, out_specs=o_spec,
        out_shape=jax.ShapeDtypeStruct((m, n), out_dtype),
        scratch_shapes=[pltpu.VMEM((bm, bn), F32)] if nk > 1 else [],
        compiler_params=_params(("parallel", "parallel", "arbitrary")),
    )(*args)


def _norm1_fwd(x, w, w_ab):
    t = x.shape[0]
    tb = _rows(t)

    def body(x_ref, w_ref, wab_ref, n_ref, hab_ref):
        n = _rms(x_ref[...], w_ref[...]).astype(BF16)
        n_ref[...] = n
        hab_ref[...] = _dot(n, wab_ref[...])

    return pl.pallas_call(
        body, name="norm1_fwd", grid=(t // tb,),
        in_specs=[pl.BlockSpec((tb, D_MODEL), lambda i: (i, 0)),
                  pl.BlockSpec((1, D_MODEL), lambda i: (0, 0)),
                  pl.BlockSpec((D_MODEL, LANES), lambda i: (0, 0))],
        out_specs=[pl.BlockSpec((tb, D_MODEL), lambda i: (i, 0)),
                   pl.BlockSpec((tb, LANES), lambda i: (i, 0))],
        out_shape=[jax.ShapeDtypeStruct((t, D_MODEL), BF16), jax.ShapeDtypeStruct((t, LANES), F32)],
        compiler_params=_params(("arbitrary",)),
    )(x, w, w_ab)


def _norm1_bwd(x, w, dn, dres, dab, w_ab):
    t = x.shape[0]
    tb = _rows(t)

    def body(x_ref, w_ref, dn_ref, dres_ref, dab_ref, wab_ref, dx_ref, dw_ref):
        i = pl.program_id(0)
        g = dn_ref[...] + _dot_nt(dab_ref[...].astype(BF16), wab_ref[...])
        _, vjp = jax.vjp(_rms, x_ref[...], w_ref[...])
        dx, dw = vjp(g)
        dx_ref[...] = dres_ref[...] + dx

        @pl.when(i == 0)
        def _():
            dw_ref[...] = jnp.zeros_like(dw_ref)

        dw_ref[...] += dw

    row = pl.BlockSpec((tb, D_MODEL), lambda i: (i, 0))
    vec = pl.BlockSpec((1, D_MODEL), lambda i: (0, 0))
    return pl.pallas_call(
        body, name="norm1_bwd", grid=(t // tb,),
        in_specs=[row, vec, row, row, pl.BlockSpec((tb, LANES), lambda i: (i, 0)),
                  pl.BlockSpec((D_MODEL, LANES), lambda i: (0, 0))],
        out_specs=[row, vec],
        out_shape=[jax.ShapeDtypeStruct((t, D_MODEL), F32), jax.ShapeDtypeStruct((1, D_MODEL), F32)],
        compiler_params=_params(("arbitrary",)),
    )(x, w, dn, dres, dab, w_ab)


def _conv_fwd(x, w, name):
    t, c = x.shape
    kk = w.shape[0]
    tb, cb = _rows(t, 512), _pick(c, ELEMENTWISE_COLS)
    per = tb // HALO

    def body(x_ref, halo_ref, w_ref, y_ref, buf):
        i = pl.program_id(0)
        buf[pl.ds(HALO, tb), :] = x_ref[...]
        buf[pl.ds(0, HALO), :] = jnp.where(i == 0, 0.0, halo_ref[...])
        y = w_ref[0:1, :] * buf[pl.ds(HALO - (kk - 1), tb), :]
        for s in range(1, kk):
            y = y + w_ref[s:s + 1, :] * buf[pl.ds(HALO - (kk - 1) + s, tb), :]
        y_ref[...] = y

    return pl.pallas_call(
        body, name=name, grid=(t // tb, c // cb),
        in_specs=[pl.BlockSpec((tb, cb), lambda i, j: (i, j)),
                  pl.BlockSpec((HALO, cb), lambda i, j: (jnp.maximum(i * per - 1, 0), j)),
                  pl.BlockSpec((kk, cb), lambda i, j: (0, j))],
        out_specs=pl.BlockSpec((tb, cb), lambda i, j: (i, j)),
        out_shape=jax.ShapeDtypeStruct((t, c), F32),
        scratch_shapes=[pltpu.VMEM((tb + HALO, cb), F32)],
        compiler_params=_params(("parallel", "parallel")),
    )(x, x, w)


def _conv_bwd(dy, x, w, name, dx_dtype):
    t, c = x.shape
    kk = w.shape[0]
    tb, cb = _rows(t, 512), _pick(c, ELEMENTWISE_COLS)
    per = tb // HALO
    nblk = t // tb

    def body(dy_ref, after_ref, x_ref, w_ref, dx_ref, dw_ref, dbuf):
        i = pl.program_id(1)
        dbuf[pl.ds(0, tb), :] = dy_ref[...]
        dbuf[pl.ds(tb, HALO), :] = jnp.where(i == nblk - 1, 0.0, after_ref[...])

        @pl.when(i == 0)
        def _():
            dw_ref[...] = jnp.zeros_like(dw_ref)

        x = x_ref[...]
        dx = None
        for s in range(kk):
            shifted = dbuf[pl.ds(kk - 1 - s, tb), :]
            term = w_ref[s:s + 1, :] * shifted
            dx = term if dx is None else dx + term
            dw_ref[s:s + 1, :] += jnp.sum(shifted * x, axis=0, keepdims=True)
        dx_ref[...] = dx.astype(dx_dtype)

    blk = pl.BlockSpec((tb, cb), lambda j, i: (i, j))
    return pl.pallas_call(
        body, name=name, grid=(c // cb, nblk),
        in_specs=[blk,
                  pl.BlockSpec((HALO, cb), lambda j, i: (jnp.minimum((i + 1) * per, t // HALO - 1), j)),
                  blk,
                  pl.BlockSpec((kk, cb), lambda j, i: (0, j))],
        out_specs=[blk, pl.BlockSpec((HALO, cb), lambda j, i: (0, j))],
        out_shape=[jax.ShapeDtypeStruct((t, c), dx_dtype), jax.ShapeDtypeStruct((HALO, c), F32)],
        scratch_shapes=[pltpu.VMEM((tb + HALO, cb), F32)],
        compiler_params=_params(("parallel", "arbitrary")),
    )(dy, dy, x, w)


def _dn_prep_fn(c, hab, alog, dtb):
    s = _silu(c)
    heads = []
    for h in range(2 * N_HEADS):
        sh = s[:, h * HEAD_DIM:(h + 1) * HEAD_DIM]
        heads.append(sh * lax.rsqrt(jnp.sum(sh * sh, axis=-1, keepdims=True) + EPS))
    qn = jnp.concatenate(heads[:N_HEADS], axis=1)
    kn = jnp.concatenate(heads[N_HEADS:], axis=1)
    v = s[:, 2 * D_MODEL:]
    lane = lax.broadcasted_iota(jnp.int32, hab.shape, 1)
    g = -jnp.exp(alog) * _softplus(hab + dtb)
    beta = jax.nn.sigmoid(hab)
    gb = jnp.where(lane < N_HEADS, g, jnp.where(lane < 2 * N_HEADS, beta, 0.0))
    return qn, kn, v, gb


def _to_heads(ref, val):
    for h in range(N_HEADS):
        ref[h] = val[:, h * HEAD_DIM:(h + 1) * HEAD_DIM]


def _from_heads(ref):
    return jnp.concatenate([ref[h] for h in range(N_HEADS)], axis=1)


def _dn_prep_fwd(c, hab, alog, dtb):
    t = c.shape[0]
    tb = _rows(t)

    def body(c_ref, hab_ref, alog_ref, dtb_ref, q_ref, k_ref, v_ref, gb_ref):
        qn, kn, v, gb = _dn_prep_fn(c_ref[...], hab_ref[...], alog_ref[...], dtb_ref[...])
        _to_heads(q_ref, qn)
        _to_heads(k_ref, kn)
        _to_heads(v_ref, v)
        gb_ref[...] = gb

    hm = pl.BlockSpec((N_HEADS, tb, HEAD_DIM), lambda i: (0, i, 0))
    nar = pl.BlockSpec((tb, LANES), lambda i: (i, 0))
    vec = pl.BlockSpec((1, LANES), lambda i: (0, 0))
    return pl.pallas_call(
        body, name="dn_prep_fwd", grid=(t // tb,),
        in_specs=[pl.BlockSpec((tb, 3 * D_MODEL), lambda i: (i, 0)), nar, vec, vec],
        out_specs=[hm, hm, hm, nar],
        out_shape=[jax.ShapeDtypeStruct((N_HEADS, t, HEAD_DIM), F32)] * 3 + [jax.ShapeDtypeStruct((t, LANES), F32)],
        compiler_params=_params(("parallel",)),
    )(c, hab, alog, dtb)


def _dn_prep_bwd(c, hab, alog, dtb, dq, dk, dv, dgb):
    t = c.shape[0]
    tb = _rows(t)

    def body(c_ref, hab_ref, alog_ref, dtb_ref, dq_ref, dk_ref, dv_ref, dgb_ref,
             dc_ref, dhab_ref, dalog_ref, ddtb_ref):
        i = pl.program_id(0)
        _, vjp = jax.vjp(_dn_prep_fn, c_ref[...], hab_ref[...], alog_ref[...], dtb_ref[...])
        dc, dhab, dalog, ddtb = vjp((_from_heads(dq_ref), _from_heads(dk_ref), _from_heads(dv_ref), dgb_ref[...]))
        dc_ref[...] = dc
        dhab_ref[...] = dhab

        @pl.when(i == 0)
        def _():
            dalog_ref[...] = jnp.zeros_like(dalog_ref)
            ddtb_ref[...] = jnp.zeros_like(ddtb_ref)

        dalog_ref[...] += dalog
        ddtb_ref[...] += ddtb

    hm = pl.BlockSpec((N_HEADS, tb, HEAD_DIM), lambda i: (0, i, 0))
    wide = pl.BlockSpec((tb, 3 * D_MODEL), lambda i: (i, 0))
    nar = pl.BlockSpec((tb, LANES), lambda i: (i, 0))
    vec = pl.BlockSpec((1, LANES), lambda i: (0, 0))
    return pl.pallas_call(
        body, name="dn_prep_bwd", grid=(t // tb,),
        in_specs=[wide, nar, vec, vec, hm, hm, hm, nar],
        out_specs=[wide, nar, vec, vec],
        out_shape=[jax.ShapeDtypeStruct((t, 3 * D_MODEL), F32), jax.ShapeDtypeStruct((t, LANES), F32),
                   jax.ShapeDtypeStruct((1, LANES), F32), jax.ShapeDtypeStruct((1, LANES), F32)],
        compiler_params=_params(("arbitrary",)),
    )(c, hab, alog, dtb, dq, dk, dv, dgb)


DN_PREC = lax.Precision.HIGH
DN_GROUP = 8


def _dn_prec(a):
    return DN_PREC if a.dtype == F32 else None


def _bdot(a, b):
    return lax.dot_general(a, b, (((2,), (1,)), ((0,), (0,))), precision=_dn_prec(a), preferred_element_type=F32)


def _bdot_nt(a, b):
    return lax.dot_general(a, b, (((2,), (2,)), ((0,), (0,))), precision=_dn_prec(a), preferred_element_type=F32)


def _bdot_tn(a, b):
    return lax.dot_general(a, b, (((1,), (1,)), ((0,), (0,))), precision=_dn_prec(a), preferred_element_type=F32)


def _unit_lower_inverse(lmat):
    c = lmat.shape[-1]
    ri = lax.broadcasted_iota(jnp.int32, (c, c), 0)
    ci = lax.broadcasted_iota(jnp.int32, (c, c), 1)
    p = -lmat
    tinv = jnp.where(ri == ci, 1.0, 0.0) + p
    for _ in range(max(c.bit_length() - 2, 0)):
        p = _bdot(p, p)
        tinv = tinv + _bdot(tinv, p)
    return tinv


@jax.custom_vjp
def _solve_with(lmat, rhs, tinv):
    return _bdot(tinv, rhs)


def _solve_with_fwd(lmat, rhs, tinv):
    sol = _bdot(tinv, rhs)
    return sol, (sol, tinv)


def _solve_with_bwd(res, dsol):
    sol, tinv = res
    drhs = _bdot_tn(tinv, dsol)
    return -_bdot_nt(drhs, sol), drhs, jnp.zeros_like(tinv)


_solve_with.defvjp(_solve_with_fwd, _solve_with_bwd)


def _dn_local(q, k, v, grow, brow, tinv):
    g, c, _ = q.shape
    ri = lax.broadcasted_iota(jnp.int32, (c, c), 0)
    ci = lax.broadcasted_iota(jnp.int32, (c, c), 1)
    lower = ri >= ci
    as_col = lambda r: jnp.sum(jnp.where(ri == ci, jnp.broadcast_to(r, (g, c, c)), 0.0), axis=2, keepdims=True)
    gcol, bcol = as_col(grow), as_col(brow)
    gc_col = jnp.sum(jnp.where(lower, jnp.broadcast_to(grow, (g, c, c)), 0.0), axis=2, keepdims=True)
    gc_row = jnp.sum(jnp.where(ri <= ci, jnp.broadcast_to(gcol, (g, c, c)), 0.0), axis=1, keepdims=True)
    qs = q * (HEAD_DIM ** -0.5)
    kb = k * bcol
    vb = v * bcol
    decay = jnp.where(lower, jnp.exp(jnp.where(lower, gc_col - gc_row, 0.0)), 0.0)
    lmat = jnp.where(ri > ci, _bdot_nt(kb.astype(BF16), k.astype(BF16)) * decay, 0.0)
    eg = jnp.exp(gc_col)
    rhs = jnp.concatenate([vb, kb * eg], axis=2)
    if tinv is None:
        tinv = _unit_lower_inverse(lmat)
    sol = _solve_with(lmat, rhs, tinv)
    a_qk = jnp.where(lower, _bdot_nt(qs.astype(BF16), k.astype(BF16)) * decay, 0.0)
    g_last = jnp.sum(grow, axis=2, keepdims=True)
    kdec = k * jnp.exp(g_last - gc_col)
    egl = jnp.broadcast_to(jnp.exp(g_last), (g, 1, HEAD_DIM))
    return sol[:, :, :HEAD_DIM], sol[:, :, HEAD_DIM:], a_qk, qs * eg, kdec, egl, tinv


def _dn_seq(u, w, a_qk, qe, kdec, egl, s_in):
    b16 = lambda x: x.astype(BF16)
    v_new = u - _bdot(b16(w), b16(s_in))
    o = _bdot(b16(qe), b16(s_in)) + _bdot(b16(a_qk), b16(v_new))
    return o, s_in * egl + _bdot_tn(b16(kdec), b16(v_new))


def _dn_local_specs(t):
    grp = min(DN_GROUP, t // DN_CHUNK)
    rows = grp * DN_CHUNK
    blk = pl.BlockSpec((1, rows, HEAD_DIM), lambda h, i: (h, i, 0))
    row = pl.BlockSpec((1, grp, 1, DN_CHUNK), lambda h, i: (h, i, 0, 0))
    sq = pl.BlockSpec((1, grp, DN_CHUNK, DN_CHUNK), lambda h, i: (h, i, 0, 0))
    lane = pl.BlockSpec((1, grp, 1, HEAD_DIM), lambda h, i: (h, i, 0, 0))
    return grp, blk, row, sq, lane


def _dn_shapes(t):
    nchunk = t // DN_CHUNK
    big = jax.ShapeDtypeStruct((N_HEADS, t, HEAD_DIM), F32)
    row = jax.ShapeDtypeStruct((N_HEADS, nchunk, 1, DN_CHUNK), F32)
    sq = jax.ShapeDtypeStruct((N_HEADS, nchunk, DN_CHUNK, DN_CHUNK), F32)
    lane = jax.ShapeDtypeStruct((N_HEADS, nchunk, 1, HEAD_DIM), F32)
    return big, row, sq, lane


def _dn_local_fwd(q, k, v, grow, brow):
    t = q.shape[1]
    grp, blk, row, sq, lane = _dn_local_specs(t)
    big, _, sqs, lanes = _dn_shapes(t)

    def body(q_ref, k_ref, v_ref, gr_ref, br_ref, u_ref, w_ref, a_ref, qe_ref, kd_ref, egl_ref, t_ref):
        split = lambda r: r[0].reshape(grp, DN_CHUNK, HEAD_DIM)
        u, w, a_qk, qe, kdec, egl, tinv = _dn_local(split(q_ref), split(k_ref), split(v_ref), gr_ref[0],
                                                     br_ref[0], None)
        for ref, val in ((u_ref, u), (w_ref, w), (qe_ref, qe), (kd_ref, kdec)):
            ref[0] = val.reshape(grp * DN_CHUNK, HEAD_DIM)
        a_ref[0] = a_qk
        egl_ref[0] = egl
        t_ref[0] = tinv

    return pl.pallas_call(
        body, name="dn_local_fwd", grid=(N_HEADS, t // (grp * DN_CHUNK)),
        in_specs=[blk, blk, blk, row, row],
        out_specs=[blk, blk, sq, blk, blk, lane, sq],
        out_shape=[big, big, sqs, big, big, lanes, sqs],
        compiler_params=_params(("parallel", "parallel")),
    )(q, k, v, grow, brow)


def _dn_local_bwd(q, k, v, grow, brow, tinv, du, dw, da, dqe, dkd, degl):
    t = q.shape[1]
    grp, blk, row, sq, lane = _dn_local_specs(t)
    big, rows_, _, _ = _dn_shapes(t)

    def body(q_ref, k_ref, v_ref, gr_ref, br_ref, t_ref, du_ref, dw_ref, da_ref, dqe_ref, dkd_ref,
             degl_ref, dq_ref, dk_ref, dv_ref, dgr_ref, dbr_ref):
        split = lambda r: r[0].reshape(grp, DN_CHUNK, HEAD_DIM)
        tinv_v = t_ref[0]
        fn = lambda q_, k_, v_, gr_, br_: _dn_local(q_, k_, v_, gr_, br_, tinv_v)[:6]
        _, vjp = jax.vjp(fn, split(q_ref), split(k_ref), split(v_ref), gr_ref[0], br_ref[0])
        dq, dk, dv, dgr, dbr = vjp((split(du_ref), split(dw_ref), da_ref[0], split(dqe_ref), split(dkd_ref),
                                    degl_ref[0]))
        for ref, val in ((dq_ref, dq), (dk_ref, dk), (dv_ref, dv)):
            ref[0] = val.reshape(grp * DN_CHUNK, HEAD_DIM)
        dgr_ref[0] = dgr
        dbr_ref[0] = dbr

    return pl.pallas_call(
        body, name="dn_local_bwd", grid=(N_HEADS, t // (grp * DN_CHUNK)),
        in_specs=[blk, blk, blk, row, row, sq, blk, blk, sq, blk, blk, lane],
        out_specs=[blk, blk, blk, row, row],
        out_shape=[big, big, big, rows_, rows_],
        compiler_params=_params(("parallel", "parallel")),
    )(q, k, v, grow, brow, tinv, du, dw, da, dqe, dkd, degl)


def _dn_seq_specs(nchunk, rev):
    def idx(n):
        return nchunk - 1 - n if rev else n

    blk = pl.BlockSpec((N_HEADS, DN_CHUNK, HEAD_DIM), lambda n: (0, idx(n), 0))
    sq = pl.BlockSpec((N_HEADS, 1, DN_CHUNK, DN_CHUNK), lambda n: (0, idx(n), 0, 0))
    lane = pl.BlockSpec((N_HEADS, 1, 1, HEAD_DIM), lambda n: (0, idx(n), 0, 0))
    st = pl.BlockSpec((N_HEADS, 1, HEAD_DIM, HEAD_DIM), lambda n: (0, idx(n), 0, 0))
    return blk, sq, lane, st


def _dn_seq_fwd(u, w, a_qk, qe, kdec, egl):
    t = u.shape[1]
    nchunk = t // DN_CHUNK
    blk, sq, lane, st = _dn_seq_specs(nchunk, False)

    def body(u_ref, w_ref, a_ref, qe_ref, kd_ref, egl_ref, o_ref, s_ref, state):
        @pl.when(pl.program_id(0) == 0)
        def _():
            state[...] = jnp.zeros_like(state)

        s_in = state[...]
        s_ref[:, 0] = s_in
        o, s_out = _dn_seq(u_ref[...], w_ref[...], a_ref[:, 0], qe_ref[...], kd_ref[...], egl_ref[:, 0], s_in)
        o_ref[...] = o
        state[...] = s_out

    return pl.pallas_call(
        body, name="dn_seq_fwd", grid=(nchunk,),
        in_specs=[blk, blk, sq, blk, blk, lane],
        out_specs=[blk, st],
        out_shape=[jax.ShapeDtypeStruct((N_HEADS, t, HEAD_DIM), F32),
                   jax.ShapeDtypeStruct((N_HEADS, nchunk, HEAD_DIM, HEAD_DIM), F32)],
        scratch_shapes=[pltpu.VMEM((N_HEADS, HEAD_DIM, HEAD_DIM), F32)],
        compiler_params=_params(("arbitrary",)),
    )(u, w, a_qk, qe, kdec, egl)


def _dn_seq_bwd(u, w, a_qk, qe, kdec, egl, states, do):
    t = u.shape[1]
    nchunk = t // DN_CHUNK
    blk, sq, lane, st = _dn_seq_specs(nchunk, True)
    big, _, sqs, lanes = _dn_shapes(t)

    def body(u_ref, w_ref, a_ref, qe_ref, kd_ref, egl_ref, s_ref, do_ref,
             du_ref, dw_ref, da_ref, dqe_ref, dkd_ref, degl_ref, dstate):
        @pl.when(pl.program_id(0) == 0)
        def _():
            dstate[...] = jnp.zeros_like(dstate)

        _, vjp = jax.vjp(_dn_seq, u_ref[...], w_ref[...], a_ref[:, 0], qe_ref[...], kd_ref[...], egl_ref[:, 0],
                         s_ref[:, 0])
        du, dw, da, dqe, dkd, degl, ds = vjp((do_ref[...], dstate[...]))
        du_ref[...] = du
        dw_ref[...] = dw
        da_ref[:, 0] = da
        dqe_ref[...] = dqe
        dkd_ref[...] = dkd
        degl_ref[:, 0] = degl
        dstate[...] = ds

    return pl.pallas_call(
        body, name="dn_seq_bwd", grid=(nchunk,),
        in_specs=[blk, blk, sq, blk, blk, lane, st, blk],
        out_specs=[blk, blk, sq, blk, blk, lane],
        out_shape=[big, big, sqs, big, big, lanes],
        scratch_shapes=[pltpu.VMEM((N_HEADS, HEAD_DIM, HEAD_DIM), F32)],
        compiler_params=_params(("arbitrary",)),
    )(u, w, a_qk, qe, kdec, egl, states, do)


def _dn_post_fn(o, gate, w):
    outs = []
    for h in range(N_HEADS):
        sl = slice(h * HEAD_DIM, (h + 1) * HEAD_DIM)
        outs.append(_rms(o[:, sl], w) * _silu(gate[:, sl]))
    return jnp.concatenate(outs, axis=1)


def _dn_post_fwd(o, gate, w):
    t = gate.shape[0]
    tb = _rows(t)

    def body(o_ref, g_ref, w_ref, y_ref):
        y_ref[...] = _dn_post_fn(_from_heads(o_ref), g_ref[...], w_ref[...]).astype(BF16)

    row = pl.BlockSpec((tb, D_MODEL), lambda i: (i, 0))
    hm = pl.BlockSpec((N_HEADS, tb, HEAD_DIM), lambda i: (0, i, 0))
    return pl.pallas_call(
        body, name="dn_post_fwd", grid=(t // tb,),
        in_specs=[hm, row, pl.BlockSpec((1, HEAD_DIM), lambda i: (0, 0))],
        out_specs=row, out_shape=jax.ShapeDtypeStruct((t, D_MODEL), BF16),
        compiler_params=_params(("parallel",)),
    )(o, gate, w)


def _dn_post_bwd(o, gate, w, dy):
    t = gate.shape[0]
    tb = _rows(t)

    def body(o_ref, g_ref, w_ref, dy_ref, do_ref, dg_ref, dw_ref):
        i = pl.program_id(0)
        _, vjp = jax.vjp(_dn_post_fn, _from_heads(o_ref), g_ref[...], w_ref[...])
        do, dg, dw = vjp(dy_ref[...])
        _to_heads(do_ref, do)
        dg_ref[...] = dg.astype(BF16)

        @pl.when(i == 0)
        def _():
            dw_ref[...] = jnp.zeros_like(dw_ref)

        dw_ref[...] += dw

    row = pl.BlockSpec((tb, D_MODEL), lambda i: (i, 0))
    hm = pl.BlockSpec((N_HEADS, tb, HEAD_DIM), lambda i: (0, i, 0))
    vec = pl.BlockSpec((1, HEAD_DIM), lambda i: (0, 0))
    return pl.pallas_call(
        body, name="dn_post_bwd", grid=(t // tb,),
        in_specs=[hm, row, vec, row],
        out_specs=[hm, row, vec],
        out_shape=[jax.ShapeDtypeStruct((N_HEADS, t, HEAD_DIM), F32), jax.ShapeDtypeStruct((t, D_MODEL), BF16),
                   jax.ShapeDtypeStruct((1, HEAD_DIM), F32)],
        compiler_params=_params(("arbitrary",)),
    )(o, gate, w, dy)


def _split_bf16(x):
    hi = x.astype(BF16)
    lo = (x - hi.astype(F32)).astype(BF16)
    return hi, lo


SB_Q_BLOCK = 512
SB_K_BLOCK = 256
SB_NEGLIGIBLE = -60.0


def _sb_logits(q, kb, mask, scale):
    z = _dot_nt(q, kb) * scale
    ls = jnp.minimum(z, 0.0) - jnp.log(1.0 + jnp.exp(-jnp.abs(z)))
    lk = ls - z
    if mask is not None:
        lk = jnp.where(mask, lk, 0.0)
    return ls, lk


def _sb_blocks(t):
    bq = min(SB_Q_BLOCK, t)
    bk = min(SB_K_BLOCK, bq)
    return bq, bk, bq // bk


def _sb_fwd(qkv):
    t = qkv.shape[0]
    bq, bk, nd = _sb_blocks(t)
    scale = HEAD_DIM ** -0.5

    def body(q_ref, k_ref, v_ref, o_ref, tot_ref, used_ref):
        i = pl.program_id(1)
        q = q_ref[...]
        rj = lax.broadcasted_iota(jnp.int32, (bk, bk), 0)
        cj = lax.broadcasted_iota(jnp.int32, (bk, bk), 1)
        after = (rj > cj).astype(BF16)
        trow = lax.broadcasted_iota(jnp.int32, (bq, bk), 0)
        scol = lax.broadcasted_iota(jnp.int32, (bq, bk), 1)

        def tile(j, run, acc, mask):
            off = pl.multiple_of(j * bk, bk)
            kb = k_ref[pl.ds(off, bk), :]
            vb = v_ref[pl.ds(off, bk), :]
            ls, lk = _sb_logits(q, kb, mask, scale)
            hi, lo = _split_bf16(lk)
            between = _dot(hi, after) + _dot(lo, after) + run
            a = jnp.exp(ls + between)
            if mask is not None:
                a = jnp.where(mask, a, 0.0)
            acc = acc + _dot(a.astype(BF16), vb)
            return run + jnp.sum(lk, axis=1, keepdims=True), acc

        run, acc = jnp.zeros((bq, 1), F32), jnp.zeros((bq, HEAD_DIM), F32)
        for d in reversed(range(nd)):
            run, acc = tile(i * nd + d, run, acc, scol + d * bk < trow)
        def more(c):
            return jnp.logical_and(c[0] < i * nd, jnp.max(c[1]) > SB_NEGLIGIBLE)

        def far(c):
            run_, acc_ = tile(i * nd - 1 - c[0], c[1], c[2], None)
            return c[0] + 1, run_, acc_

        used, run, acc = lax.while_loop(more, far, (jnp.int32(0), run, acc))
        o_ref[...] = acc.astype(BF16)
        tot_ref[...] = jnp.broadcast_to(run, (bq, HEAD_DIM))
        used_ref[...] = jnp.full(used_ref.shape, used, F32)

    qs = pl.BlockSpec((bq, HEAD_DIM), lambda h, i: (i, h))
    ks = pl.BlockSpec((t, HEAD_DIM), lambda h, i: (0, N_HEADS + h))
    vs = pl.BlockSpec((t, HEAD_DIM), lambda h, i: (0, 2 * N_HEADS + h))
    return pl.pallas_call(
        body, name="sb_fwd", grid=(N_HEADS, t // bq),
        in_specs=[qs, ks, vs], out_specs=[qs, qs, pl.BlockSpec((1, 1, 1, LANES), lambda h, i: (h, i, 0, 0))],
        out_shape=[jax.ShapeDtypeStruct((t, D_MODEL), BF16), jax.ShapeDtypeStruct((t, D_MODEL), F32),
                   jax.ShapeDtypeStruct((N_HEADS, t // bq, 1, LANES), F32)],
        compiler_params=_params(("parallel", "arbitrary")),
    )(qkv, qkv, qkv)


def _sb_bwd(qkv, tot, used, do):
    t = qkv.shape[0]
    bq, bk, nd = _sb_blocks(t)
    scale = HEAD_DIM ** -0.5

    def body(q_ref, k_ref, v_ref, tot_ref, used_ref, do_ref, dq_ref, dk_ref, dv_ref):
        i = pl.program_id(1)

        @pl.when(i == 0)
        def _():
            dk_ref[...] = jnp.zeros_like(dk_ref)
            dv_ref[...] = jnp.zeros_like(dv_ref)

        q = q_ref[...]
        do = do_ref[...]
        total = tot_ref[:, 0:1]
        rj = lax.broadcasted_iota(jnp.int32, (bk, bk), 0)
        cj = lax.broadcasted_iota(jnp.int32, (bk, bk), 1)
        upto = (rj <= cj).astype(BF16)
        before = (rj < cj).astype(BF16)
        trow = lax.broadcasted_iota(jnp.int32, (bq, bk), 0)
        scol = lax.broadcasted_iota(jnp.int32, (bq, bk), 1)

        def tile(j, run_k, run_e, dq, mask):
            off = pl.multiple_of(j * bk, bk)
            kb = k_ref[pl.ds(off, bk), :]
            vb = v_ref[pl.ds(off, bk), :]
            ls, lk = _sb_logits(q, kb, mask, scale)
            hi, lo = _split_bf16(lk)
            between = total - (_dot(hi, upto) + _dot(lo, upto) + run_k)
            a = jnp.exp(ls + between)
            if mask is not None:
                a = jnp.where(mask, a, 0.0)
            e = a * _dot_nt(do, vb)
            ehi, elo = _split_bf16(e)
            pre = _dot(ehi, before) + _dot(elo, before) + run_e
            sig = jnp.exp(ls)
            dz = e * (1.0 - sig) - pre * sig
            if mask is not None:
                dz = jnp.where(mask, dz, 0.0)
            dz = (dz * scale).astype(BF16)
            dq = dq + _dot(dz, kb)
            dk_ref[pl.ds(off, bk), :] += _dot_tn(dz, q)
            dv_ref[pl.ds(off, bk), :] += _dot_tn(a.astype(BF16), do)
            return (run_k + jnp.sum(lk, axis=1, keepdims=True),
                    run_e + jnp.sum(e, axis=1, keepdims=True), dq)

        zero = jnp.zeros((bq, 1), F32)
        visited = jnp.clip(jnp.max(used_ref[...]).astype(jnp.int32), 0, i * nd)
        carry = lax.fori_loop(i * nd - visited, i * nd, lambda j, c: tile(j, c[0], c[1], c[2], None),
                              (zero, zero, jnp.zeros((bq, HEAD_DIM), F32)))
        for d in range(nd):
            carry = tile(i * nd + d, *carry, scol + d * bk < trow)
        dq_ref[...] = carry[2]

    qs = pl.BlockSpec((bq, HEAD_DIM), lambda h, i: (i, h))
    ks = pl.BlockSpec((t, HEAD_DIM), lambda h, i: (0, N_HEADS + h))
    vs = pl.BlockSpec((t, HEAD_DIM), lambda h, i: (0, 2 * N_HEADS + h))
    full = pl.BlockSpec((t, HEAD_DIM), lambda h, i: (0, h))
    big = jax.ShapeDtypeStruct((t, D_MODEL), F32)
    return pl.pallas_call(
        body, name="sb_bwd", grid=(N_HEADS, t // bq),
        in_specs=[qs, ks, vs, qs, pl.BlockSpec((1, 1, 1, LANES), lambda h, i: (h, i, 0, 0)), qs],
        out_specs=[qs, full, full],
        out_shape=[big, big, big],
        compiler_params=_params(("parallel", "arbitrary")),
    )(qkv, qkv, qkv, tot, used, do)


def _merge_fwd(o_dn, o_sb, gl, x, wp_dn, wp_sb, w_out, w2):
    t = x.shape[0]
    tb = _rows(t)

    def body(odn_ref, osb_ref, gl_ref, x_ref, wpd_ref, wps_ref, wo_ref, w2_ref,
             pdn_ref, psb_ref, mix_ref, x1_ref, n2_ref):
        pdn = _dot(odn_ref[...], wpd_ref[...])
        psb = _dot(osb_ref[...], wps_ref[...])
        gates = jax.nn.sigmoid(gl_ref[...])
        mixed = (gates[:, :D_MODEL] * pdn + gates[:, D_MODEL:] * psb).astype(BF16)
        x1 = x_ref[...] + _dot(mixed, wo_ref[...])
        pdn_ref[...] = pdn
        psb_ref[...] = psb
        mix_ref[...] = mixed
        x1_ref[...] = x1
        n2_ref[...] = _rms(x1, w2_ref[...]).astype(BF16)

    row = pl.BlockSpec((tb, D_MODEL), lambda i: (i, 0))
    sq = pl.BlockSpec((D_MODEL, D_MODEL), lambda i: (0, 0))
    f = jax.ShapeDtypeStruct((t, D_MODEL), F32)
    b = jax.ShapeDtypeStruct((t, D_MODEL), BF16)
    return pl.pallas_call(
        body, name="merge_fwd", grid=(t // tb,),
        in_specs=[row, row, pl.BlockSpec((tb, 2 * D_MODEL), lambda i: (i, 0)), row, sq, sq, sq,
                  pl.BlockSpec((1, D_MODEL), lambda i: (0, 0))],
        out_specs=[row] * 5, out_shape=[f, f, b, f, b],
        compiler_params=_params(("parallel",)),
    )(o_dn, o_sb, gl, x, wp_dn, wp_sb, w_out, w2)


def _merge_bwd(dx2, dn2, x1, w2, gl, pdn, psb, wp_dn, wp_sb, w_out):
    t = x1.shape[0]
    tb = _rows(t)

    def body(dx2_ref, dn2_ref, x1_ref, w2_ref, gl_ref, pdn_ref, psb_ref, wpd_ref, wps_ref, wo_ref,
             dx1_ref, dw2_ref, dgl_ref, dpdn_ref, dpsb_ref, dodn_ref, dosb_ref):
        i = pl.program_id(0)
        _, vjp = jax.vjp(_rms, x1_ref[...], w2_ref[...])
        dxn, dw2 = vjp(dn2_ref[...])
        dx1 = dx2_ref[...] + dxn
        dx1_ref[...] = dx1

        @pl.when(i == 0)
        def _():
            dw2_ref[...] = jnp.zeros_like(dw2_ref)

        dw2_ref[...] += dw2
        dmix = _dot_nt(dx1.astype(BF16), wo_ref[...])
        gates = jax.nn.sigmoid(gl_ref[...])
        g_dn, g_sb = gates[:, :D_MODEL], gates[:, D_MODEL:]
        dpdn = (dmix * g_dn).astype(BF16)
        dpsb = (dmix * g_sb).astype(BF16)
        dgl_ref[:, :D_MODEL] = (dmix * pdn_ref[...] * g_dn * (1.0 - g_dn)).astype(BF16)
        dgl_ref[:, D_MODEL:] = (dmix * psb_ref[...] * g_sb * (1.0 - g_sb)).astype(BF16)
        dpdn_ref[...] = dpdn
        dpsb_ref[...] = dpsb
        dodn_ref[...] = _dot_nt(dpdn, wpd_ref[...])
        dosb_ref[...] = _dot_nt(dpsb, wps_ref[...]).astype(BF16)

    row = pl.BlockSpec((tb, D_MODEL), lambda i: (i, 0))
    wide = pl.BlockSpec((tb, 2 * D_MODEL), lambda i: (i, 0))
    sq = pl.BlockSpec((D_MODEL, D_MODEL), lambda i: (0, 0))
    vec = pl.BlockSpec((1, D_MODEL), lambda i: (0, 0))
    f = jax.ShapeDtypeStruct((t, D_MODEL), F32)
    b = jax.ShapeDtypeStruct((t, D_MODEL), BF16)
    return pl.pallas_call(
        body, name="merge_bwd", grid=(t // tb,),
        in_specs=[row, row, row, vec, wide, row, row, sq, sq, sq],
        out_specs=[row, vec, wide, row, row, row, row],
        out_shape=[f, jax.ShapeDtypeStruct((1, D_MODEL), F32), jax.ShapeDtypeStruct((t, 2 * D_MODEL), BF16),
                   b, b, f, b],
        compiler_params=_params(("arbitrary",)),
    )(dx2, dn2, x1, w2, gl, pdn, psb, wp_dn, wp_sb, w_out)


def _conv_taps(buf, w_ref, first, rows):
    y = w_ref[0:1, :] * buf[pl.ds(first, rows), :]
    for s in range(1, w_ref.shape[0]):
        y = y + w_ref[s:s + 1, :] * buf[pl.ds(first + s, rows), :]
    return y


def _ffn_mid_fwd(pre_g, pre_u, wg, wu):
    t, c = pre_g.shape
    kk = wg.shape[0]
    tb, cb = _rows(t), _pick(c, ELEMENTWISE_COLS)
    per = tb // HALO

    def body(g_ref, gh_ref, u_ref, uh_ref, wg_ref, wu_ref, a_ref, gbuf, ubuf):
        i = pl.program_id(0)
        for buf, ref, halo in ((gbuf, g_ref, gh_ref), (ubuf, u_ref, uh_ref)):
            buf[pl.ds(HALO, tb), :] = ref[...]
            buf[pl.ds(0, HALO), :] = jnp.where(i == 0, 0.0, halo[...])
        ug = _conv_taps(gbuf, wg_ref, HALO - (kk - 1), tb)
        uu = _conv_taps(ubuf, wu_ref, HALO - (kk - 1), tb)
        a_ref[...] = (_silu(ug) * uu).astype(BF16)

    blk = pl.BlockSpec((tb, cb), lambda i, j: (i, j))
    halo = pl.BlockSpec((HALO, cb), lambda i, j: (jnp.maximum(i * per - 1, 0), j))
    wspec = pl.BlockSpec((kk, cb), lambda i, j: (0, j))
    return pl.pallas_call(
        body, name="ffn_mid_fwd", grid=(t // tb, c // cb),
        in_specs=[blk, halo, blk, halo, wspec, wspec], out_specs=blk,
        out_shape=jax.ShapeDtypeStruct((t, c), BF16),
        scratch_shapes=[pltpu.VMEM((tb + HALO, cb), F32)] * 2,
        compiler_params=_params(("parallel", "parallel")),
    )(pre_g, pre_g, pre_u, pre_u, wg, wu)


def _ffn_mid_bwd(pre_g, pre_u, wg, wu, da):
    t, c = pre_g.shape
    kk = wg.shape[0]
    tb, cb = _rows(t), _pick(c, ELEMENTWISE_COLS)
    per = tb // HALO
    nblk = t // tb
    ext = tb + HALO

    def body(g_ref, gb_ref, ga_ref, u_ref, ub_ref, ua_ref, da_ref, daa_ref, wg_ref, wu_ref,
             dg_ref, du_ref, dwg_ref, dwu_ref, gbuf, ubuf, dabuf, dgbuf, dubuf):
        i = pl.program_id(1)
        last = i == nblk - 1
        for buf, ref, before, after in ((gbuf, g_ref, gb_ref, ga_ref), (ubuf, u_ref, ub_ref, ua_ref)):
            buf[pl.ds(0, HALO), :] = jnp.where(i == 0, 0.0, before[...])
            buf[pl.ds(HALO, tb), :] = ref[...]
            buf[pl.ds(HALO + tb, HALO), :] = jnp.where(last, 0.0, after[...])
        dabuf[pl.ds(0, tb), :] = da_ref[...]
        dabuf[pl.ds(tb, HALO), :] = jnp.where(last, 0.0, daa_ref[...])
        ug = _conv_taps(gbuf, wg_ref, HALO - (kk - 1), ext)
        uu = _conv_taps(ubuf, wu_ref, HALO - (kk - 1), ext)
        _, vjp = jax.vjp(lambda g, u: _silu(g) * u, ug, uu)
        dgbuf[...], dubuf[...] = vjp(dabuf[...])

        @pl.when(i == 0)
        def _():
            dwg_ref[...] = jnp.zeros_like(dwg_ref)
            dwu_ref[...] = jnp.zeros_like(dwu_ref)

        for dbuf, xbuf, w_ref, dx_ref, dw_ref in ((dgbuf, gbuf, wg_ref, dg_ref, dwg_ref),
                                                  (dubuf, ubuf, wu_ref, du_ref, dwu_ref)):
            x = xbuf[pl.ds(HALO, tb), :]
            dx = None
            for s in range(kk):
                shifted = dbuf[pl.ds(kk - 1 - s, tb), :]
                term = w_ref[s:s + 1, :] * shifted
                dx = term if dx is None else dx + term
                dw_ref[s:s + 1, :] += jnp.sum(shifted * x, axis=0, keepdims=True)
            dx_ref[...] = dx.astype(BF16)

    blk = pl.BlockSpec((tb, cb), lambda j, i: (i, j))
    before = pl.BlockSpec((HALO, cb), lambda j, i: (jnp.maximum(i * per - 1, 0), j))
    after = pl.BlockSpec((HALO, cb), lambda j, i: (jnp.minimum((i + 1) * per, t // HALO - 1), j))
    wspec = pl.BlockSpec((kk, cb), lambda j, i: (0, j))
    dwspec = pl.BlockSpec((HALO, cb), lambda j, i: (0, j))
    half = jax.ShapeDtypeStruct((t, c), BF16)
    dwshape = jax.ShapeDtypeStruct((HALO, c), F32)
    return pl.pallas_call(
        body, name="ffn_mid_bwd", grid=(c // cb, nblk),
        in_specs=[blk, before, after, blk, before, after, blk, after, wspec, wspec],
        out_specs=[blk, blk, dwspec, dwspec],
        out_shape=[half, half, dwshape, dwshape],
        scratch_shapes=[pltpu.VMEM((ext + HALO, cb), F32)] * 2 + [pltpu.VMEM((ext, cb), F32)] * 3,
        compiler_params=_params(("parallel", "arbitrary")),
    )(pre_g, pre_g, pre_g, pre_u, pre_u, pre_u, da, da, wg, wu)


def _down_loss(a, w_down, x1, wf, target):
    t = x1.shape[0]
    tb = _rows(t)

    def body(a_ref, wd_ref, x1_ref, wf_ref, tgt_ref, dx2_ref, dwf_ref, loss_ref):
        i = pl.program_id(0)
        x2 = x1_ref[...] + _dot(a_ref[...], wd_ref[...])
        y, vjp = jax.vjp(_rms, x2, wf_ref[...])
        err = y - tgt_ref[...]
        dx2, dwf = vjp(err * (1.0 / D_MODEL))
        dx2_ref[...] = dx2
        part = jnp.sum(jnp.sum(err * err, axis=1, keepdims=True), axis=0, keepdims=True) * (0.5 / D_MODEL)

        @pl.when(i == 0)
        def _():
            dwf_ref[...] = jnp.zeros_like(dwf_ref)
            loss_ref[...] = jnp.zeros_like(loss_ref)

        dwf_ref[...] += dwf
        loss_ref[...] += jnp.broadcast_to(part, loss_ref.shape)

    row = pl.BlockSpec((tb, D_MODEL), lambda i: (i, 0))
    vec = pl.BlockSpec((1, D_MODEL), lambda i: (0, 0))
    return pl.pallas_call(
        body, name="down_loss", grid=(t // tb,),
        in_specs=[pl.BlockSpec((tb, D_FF), lambda i: (i, 0)), pl.BlockSpec((D_FF, D_MODEL), lambda i: (0, 0)),
                  row, vec, row],
        out_specs=[row, vec, pl.BlockSpec((1, LANES), lambda i: (0, 0))],
        out_shape=[jax.ShapeDtypeStruct((t, D_MODEL), F32), jax.ShapeDtypeStruct((1, D_MODEL), F32),
                   jax.ShapeDtypeStruct((1, LANES), F32)],
        compiler_params=_params(("arbitrary",)),
    )(a, w_down, x1, wf, target)


def _local_step(x, target, wts):
    t = x.shape[0]
    nchunk = t // DN_CHUNK

    n1, hab = _norm1_fwd(x, wts["norm1"], wts["w_ab"])
    dnqkv = _mm(n1, wts["w_dnqkv"], name="h_dnqkv")
    dngate = _mm(n1, wts["w_dngate"], name="h_dngate")
    sbqkv = _mm(n1, wts["w_sbqkv"], out_dtype=BF16, name="h_sbqkv")
    gl = _mm(n1, wts["w_gl"], name="h_gl")

    cdn = _conv_fwd(dnqkv, wts["dn_conv"], "dn_conv_fwd")
    qn, kn, vv, gb = _dn_prep_fwd(cdn, hab, wts["alog"], wts["dtb"])
    per_head = gb[:, :2 * N_HEADS].T.reshape(2 * N_HEADS, nchunk, DN_CHUNK)
    grow, brow = per_head[:N_HEADS, :, None, :], per_head[N_HEADS:, :, None, :]
    u_dn, w_dn, a_qk, qe, kdec, egl, tinv = _dn_local_fwd(qn, kn, vv, grow, brow)
    o_raw, states = _dn_seq_fwd(u_dn, w_dn, a_qk, qe, kdec, egl)
    o_dn = _dn_post_fwd(o_raw, dngate, wts["dn_norm"])

    o_sb, tot, sb_used = _sb_fwd(sbqkv)

    pdn, psb, mixed, x1, n2 = _merge_fwd(o_dn, o_sb, gl, x, wts["wp_dn"], wts["wp_sb"], wts["w_out"],
                                         wts["norm2"])
    pre_g = _mm(n2, wts["w_up_g"], name="ffn_up_g")
    pre_u = _mm(n2, wts["w_up_u"], name="ffn_up_u")
    act = _ffn_mid_fwd(pre_g, pre_u, wts["ffn_conv_g"], wts["ffn_conv_u"])
    dx2, d_normf, loss_part = _down_loss(act, wts["w_down"], x1, wts["normf"], target)

    grads = {"normf": d_normf}
    da = _mm(dx2, wts["w_down"], tb=True, name="d_act")
    grads["w_down"] = _mm(act, dx2, ta=True, out_dtype=BF16, name="dw_down")
    dpre_g, dpre_u, dcw_g, dcw_u = _ffn_mid_bwd(pre_g, pre_u, wts["ffn_conv_g"], wts["ffn_conv_u"], da)
    grads["ffn_conv"] = jnp.concatenate([dcw_g[:FFN_CONV], dcw_u[:FFN_CONV]], axis=1)
    dn2 = _mm(dpre_g, wts["w_up_g"], tb=True, name="dn2_g")
    dn2 = _mm(dpre_u, wts["w_up_u"], tb=True, add=dn2, name="dn2_u")
    grads["w_up_g"] = _mm(n2, dpre_g, ta=True, out_dtype=BF16, name="dw_up_g")
    grads["w_up_u"] = _mm(n2, dpre_u, ta=True, out_dtype=BF16, name="dw_up_u")

    dx1, grads["norm2"], dgl, dpdn, dpsb, do_dn, do_sb = _merge_bwd(
        dx2, dn2, x1, wts["norm2"], gl, pdn, psb, wts["wp_dn"], wts["wp_sb"], wts["w_out"])
    grads["w_out"] = _mm(mixed, dx1, ta=True, out_dtype=BF16, name="dw_out")
    grads["wp_dn"] = _mm(o_dn, dpdn, ta=True, out_dtype=BF16, name="dw_proj_dn")
    grads["wp_sb"] = _mm(o_sb, dpsb, ta=True, out_dtype=BF16, name="dw_proj_sb")

    dsq, dsk, dsv = _sb_bwd(sbqkv, tot, sb_used, do_sb)
    dsbqkv = jnp.concatenate([dsq, dsk, dsv], axis=1).astype(BF16)

    do_raw, ddngate, grads["dn_norm"] = _dn_post_bwd(o_raw, dngate, wts["dn_norm"], do_dn)
    seq_grads = _dn_seq_bwd(u_dn, w_dn, a_qk, qe, kdec, egl, states, do_raw)
    dqn, dkn, dvv, dgrow, dbrow = _dn_local_bwd(qn, kn, vv, grow, brow, tinv, *seq_grads)
    dgb = jnp.concatenate([dgrow.reshape(N_HEADS, t), dbrow.reshape(N_HEADS, t)], axis=0).T
    dgb = jnp.pad(dgb, ((0, 0), (0, LANES - 2 * N_HEADS)))
    dcdn, dhab, grads["alog"], grads["dtb"] = _dn_prep_bwd(cdn, hab, wts["alog"], wts["dtb"], dqn, dkn, dvv, dgb)
    ddnqkv, dcw_dn = _conv_bwd(dcdn, dnqkv, wts["dn_conv"], "dn_conv_bwd", BF16)
    grads["dn_conv"] = dcw_dn[:DN_CONV]

    dn1 = _mm(ddnqkv, wts["w_dnqkv"], tb=True, name="dn1_dnqkv")
    dn1 = _mm(ddngate, wts["w_dngate"], tb=True, add=dn1, name="dn1_dngate")
    dn1 = _mm(dsbqkv, wts["w_sbqkv"], tb=True, add=dn1, name="dn1_sbqkv")
    dn1 = _mm(dgl, wts["w_gl"], tb=True, add=dn1, name="dn1_gl")
    grads["w_dnqkv"] = _mm(n1, ddnqkv, ta=True, out_dtype=BF16, name="dw_dnqkv")
    grads["w_dngate"] = _mm(n1, ddngate, ta=True, out_dtype=BF16, name="dw_dngate")
    grads["w_sbqkv"] = _mm(n1, dsbqkv, ta=True, out_dtype=BF16, name="dw_sbqkv")
    grads["w_gl"] = _mm(n1, dgl, ta=True, out_dtype=BF16, name="dw_gl")
    grads["w_ab"] = _mm(n1, dhab, ta=True, out_dtype=BF16, name="dw_ab")
    grad_x, grads["norm1"] = _norm1_bwd(x, wts["norm1"], dn1, dx1, dhab, wts["w_ab"])
    return loss_part, grad_x, grads


def _place():
    return lax.axis_index("x"), lax.axis_index("y"), lax.axis_index("c")


def _hbm_specs(n):
    return [pl.BlockSpec(memory_space=pltpu.HBM)] * n


def _gather_shards(shards):
    n = len(shards)
    per = 8

    def body(*refs):
        ins, outs, (send_sems, recv_sems) = refs[:n], refs[n:2 * n], refs[2 * n:]
        x, y, c = _place()
        me = 2 * x + y
        sibling = (x, y, 1 - c)
        xn, yn, dg = (1 - x, y), (x, 1 - y), (1 - x, 1 - y)
        idx = lambda chip: 2 * chip[0] + chip[1]

        def part(a, chip_index, core, quarter=None):
            half = ins[a].shape[0] // 2
            if quarter is None:
                return outs[a].at[chip_index, pl.ds(core * half, half), :]
            return outs[a].at[chip_index, pl.ds(core * half + quarter * (half // 2), half // 2), :]

        def copy(a, k, src, dst, to):
            return pltpu.make_async_remote_copy(src_ref=src, dst_ref=dst, send_sem=send_sems.at[per * a + k],
                                                recv_sem=recv_sems.at[per * a + k], device_id=to, device_id_type=MESH)

        started = []

        def start(cp):
            cp.start()
            started.append(cp)

        for a in range(n):
            half = ins[a].shape[0] // 2
            my_half = ins[a].at[pl.ds(c * half, half), :]
            start(copy(a, 0, my_half, part(a, me, c), (*xn, c)))
            start(copy(a, 1, my_half, part(a, me, c), (*yn, c)))
        for a in range(n):
            landed = part(a, idx(xn), c)
            copy(a, 0, landed, landed, (*xn, c)).wait_recv()
            start(copy(a, 2, part(a, idx(xn), c, 0), part(a, idx(xn), c, 0), (*yn, c)))
            start(copy(a, 4, landed, landed, sibling))
            landed = part(a, idx(yn), c)
            copy(a, 1, landed, landed, (*yn, c)).wait_recv()
            start(copy(a, 3, part(a, idx(yn), c, 1), part(a, idx(yn), c, 1), (*xn, c)))
            start(copy(a, 5, landed, landed, sibling))
        for a in range(n):
            for k, q, via in ((2, 0, yn), (3, 1, xn)):
                landed = part(a, idx(dg), c, q)
                copy(a, k, landed, landed, (*via, c)).wait_recv()
                start(copy(a, 4 + k, landed, landed, sibling))
        for a in range(n):
            for k, there in ((4, part(a, idx(xn), 1 - c)), (5, part(a, idx(yn), 1 - c)),
                             (6, part(a, idx(dg), 1 - c, 0)), (7, part(a, idx(dg), 1 - c, 1))):
                copy(a, k, there, there, sibling).wait_recv()
        for cp in started:
            cp.wait_send()

    return pl.pallas_call(
        body, name="gather_weights", in_specs=_hbm_specs(n), out_specs=_hbm_specs(n),
        out_shape=[jax.ShapeDtypeStruct((N_CHIPS,) + s.shape, s.dtype) for s in shards],
        scratch_shapes=[pltpu.SemaphoreType.DMA((per * n,)), pltpu.SemaphoreType.DMA((per * n,))],
    )(*shards)


def _pair_exchange_halves(gs):
    n = len(gs)

    def body(*refs):
        ins, outs, (send_sems, recv_sems) = refs[:n], refs[n:2 * n], refs[2 * n:]
        x, y, c = _place()
        cps = []
        for a in range(n):
            half = ins[a].shape[1] // 2
            cp = pltpu.make_async_remote_copy(src_ref=ins[a].at[:, pl.ds((1 - c) * half, half), :], dst_ref=outs[a],
                                              send_sem=send_sems.at[a], recv_sem=recv_sems.at[a],
                                              device_id=(x, y, 1 - c), device_id_type=MESH)
            cp.start()
            cps.append(cp)
        for cp in cps:
            cp.wait()

    return pl.pallas_call(
        body, name="grad_pair_exchange", in_specs=_hbm_specs(n), out_specs=_hbm_specs(n),
        out_shape=[jax.ShapeDtypeStruct((g.shape[0], g.shape[1] // 2, g.shape[2]), g.dtype) for g in gs],
        scratch_shapes=[pltpu.SemaphoreType.DMA((n,)), pltpu.SemaphoreType.DMA((n,))],
    )(*gs)


def _pick_rows(n, target=1024):
    best = 16
    for b in range(16, min(n, target) + 1, 16):
        if n % b == 0:
            best = b
    return best


def _pair_add(g, got, c_idx, tag):
    nsh, rows, cols = g.shape
    half = rows // 2
    rb = _pick_rows(half)

    def body(c_ref, g_ref, got_ref, o_ref):
        o_ref[...] = (g_ref[...].astype(F32) + got_ref[...].astype(F32)).astype(BF16)

    nb = half // rb
    grid_spec = pltpu.PrefetchScalarGridSpec(
        num_scalar_prefetch=1, grid=(nsh, nb),
        in_specs=[pl.BlockSpec((1, rb, cols), lambda s, i, c_ref: (s, c_ref[0] * nb + i, 0)),
                  pl.BlockSpec((1, rb, cols), lambda s, i, c_ref: (s, i, 0))],
        out_specs=pl.BlockSpec((1, rb, cols), lambda s, i, c_ref: (s, i, 0)))
    return pl.pallas_call(
        body, name="grad_pair_add_" + tag, grid_spec=grid_spec,
        out_shape=jax.ShapeDtypeStruct((nsh, half, cols), BF16),
        compiler_params=_params(("parallel", "parallel")),
    )(c_idx, g, got)


def _chip_exchange(ps):
    n = len(ps)

    def body(*refs):
        ins, outs, (send_sems, recv_sems) = refs[:n], refs[n:2 * n], refs[2 * n:]
        x, y, c = _place()
        chips = [(1 - x, y), (x, 1 - y), (1 - x, 1 - y)]
        sends = []
        for a in range(n):
            for j, (px, py) in enumerate(chips):
                cp = pltpu.make_async_remote_copy(src_ref=ins[a].at[2 * px + py], dst_ref=outs[a].at[j],
                                                  send_sem=send_sems.at[3 * a + j], recv_sem=recv_sems.at[3 * a + j],
                                                  device_id=(px, py, c), device_id_type=MESH)
                cp.start()
                sends.append(cp)
        for cp in sends:
            cp.wait_recv()
        for cp in sends:
            cp.wait_send()

    return pl.pallas_call(
        body, name="grad_chip_exchange", in_specs=_hbm_specs(n), out_specs=_hbm_specs(n),
        out_shape=[jax.ShapeDtypeStruct((N_CHIPS - 1,) + p.shape[1:], p.dtype) for p in ps],
        scratch_shapes=[pltpu.SemaphoreType.DMA((3 * n,)), pltpu.SemaphoreType.DMA((3 * n,))],
    )(*ps)


def _sum_partials(p, got, chip_idx, tag):
    nsh, half, cols = got.shape
    rb = _pick_rows(half)

    def body(me_ref, p_ref, got_ref, o_ref):
        acc = p_ref[0].astype(F32)
        for s in range(nsh):
            acc = acc + got_ref[s].astype(F32)
        o_ref[...] = acc

    grid_spec = pltpu.PrefetchScalarGridSpec(
        num_scalar_prefetch=1, grid=(half // rb,),
        in_specs=[pl.BlockSpec((1, rb, cols), lambda i, me_ref: (me_ref[0], i, 0)),
                  pl.BlockSpec((nsh, rb, cols), lambda i, me_ref: (0, i, 0))],
        out_specs=pl.BlockSpec((rb, cols), lambda i, me_ref: (i, 0)))
    return pl.pallas_call(
        body, name="grad_sum_chips_" + tag, grid_spec=grid_spec,
        out_shape=jax.ShapeDtypeStruct((half, cols), F32),
        compiler_params=_params(("parallel",)),
    )(chip_idx, p, got)


def _pair_share(rs):
    n = len(rs)

    def body(*refs):
        ins, outs, (send_sems, recv_sems) = refs[:n], refs[n:2 * n], refs[2 * n:]
        x, y, c = _place()
        cps = []
        for a in range(n):
            cp = pltpu.make_async_remote_copy(src_ref=ins[a], dst_ref=outs[a], send_sem=send_sems.at[a],
                                              recv_sem=recv_sems.at[a], device_id=(x, y, 1 - c),
                                              device_id_type=MESH)
            cp.start()
            cps.append(cp)
        for cp in cps:
            cp.wait()

    return pl.pallas_call(
        body, name="grad_pair_share", in_specs=_hbm_specs(n), out_specs=_hbm_specs(n),
        out_shape=[jax.ShapeDtypeStruct(r.shape, r.dtype) for r in rs],
        scratch_shapes=[pltpu.SemaphoreType.DMA((n,)), pltpu.SemaphoreType.DMA((n,))],
    )(*rs)


def _small_allreduce(v):
    rows, cols = v.shape
    ndev = 8

    def body(in_ref, out_ref, slots, send_sems, recv_sems):
        x, y, c = _place()
        me = 4 * x + 2 * y + c
        slots[me] = in_ref[...]
        sends = []
        for k in range(1, ndev):
            peer = (x ^ (k >> 2), y ^ ((k >> 1) & 1), c ^ (k & 1))
            cp = pltpu.make_async_remote_copy(src_ref=in_ref, dst_ref=slots.at[me], send_sem=send_sems.at[k - 1],
                                              recv_sem=recv_sems.at[k - 1], device_id=peer, device_id_type=MESH)
            cp.start()
            sends.append(cp)
        for k in range(1, ndev):
            there = slots.at[me ^ k]
            pltpu.make_async_remote_copy(src_ref=there, dst_ref=there, send_sem=send_sems.at[k - 1],
                                         recv_sem=recv_sems.at[k - 1], device_id=(x, y, c),
                                         device_id_type=MESH).wait_recv()
        for cp in sends:
            cp.wait_send()
        acc = slots[0]
        for s in range(1, ndev):
            acc = acc + slots[s]
        out_ref[...] = acc

    return pl.pallas_call(
        body, name="small_allreduce",
        in_specs=[pl.BlockSpec(memory_space=pltpu.VMEM)],
        out_specs=pl.BlockSpec(memory_space=pltpu.VMEM),
        out_shape=jax.ShapeDtypeStruct((rows, cols), F32),
        scratch_shapes=[pltpu.VMEM((ndev, rows, cols), F32), pltpu.SemaphoreType.DMA((ndev - 1,)),
                        pltpu.SemaphoreType.DMA((ndev - 1,))],
    )(v)


def _adamw(w, g, m, v, name):
    r, c = w.shape
    rb = r if r <= 128 else _pick_rows_8(r, 128)
    c1 = 1.0 - ADAM_B1 ** ADAM_STEP
    c2 = 1.0 - ADAM_B2 ** ADAM_STEP

    def body(w_ref, g_ref, m_ref, v_ref, d_ref, nm_ref, nv_ref):
        gg = g_ref[...]
        nm = ADAM_B1 * m_ref[...] + (1.0 - ADAM_B1) * gg
        nv = ADAM_B2 * v_ref[...] + (1.0 - ADAM_B2) * (gg * gg)
        d_ref[...] = -ADAM_LR * ((nm / c1) / (jnp.sqrt(nv / c2) + ADAM_EPS) + ADAM_WD * w_ref[...])
        nm_ref[...] = nm
        nv_ref[...] = nv

    blk = pl.BlockSpec((rb, c), lambda i: (i, 0))
    shp = jax.ShapeDtypeStruct((r, c), F32)
    return pl.pallas_call(
        body, name=name, grid=(r // rb,), in_specs=[blk] * 4, out_specs=[blk] * 3, out_shape=[shp] * 3,
        compiler_params=_params(("parallel",)),
    )(w, g, m, v)


def _pick_rows_8(n, target):
    best = n
    for b in range(8, min(n, target) + 1, 8):
        if n % b == 0:
            best = b
    return best


W_IN_COLS = 2308
W_UP_COLS = 1408
W_DOWN_ROWS = 704
DN_CONV_COLS = 768
FFN_CONV_COLS = 1408
PROJ_ROWS = 256
ROW_TILE = 16
ROW_SEGS = [("wp_dn", PROJ_ROWS), ("wp_sb", PROJ_ROWS), ("w_out", PROJ_ROWS), ("w_down", W_DOWN_ROWS),
            ("dn_conv", ROW_TILE), ("ffn_conv", ROW_TILE), ("spare", 2 * ROW_TILE)]
ROW_OFFS = {nm: (sum(n for _, n in ROW_SEGS[:i]), n) for i, (nm, n) in enumerate(ROW_SEGS)}
STACK_ROWS = sum(n for _, n in ROW_SEGS)
assert all(n % ROW_TILE == 0 for _, n in ROW_SEGS) and STACK_ROWS % (4 * ROW_TILE) == 0
Q_END, A_END, G_END, S_END = 3 * D_MODEL, 3 * D_MODEL + 2 * N_HEADS, 4 * D_MODEL + 2 * N_HEADS, 7 * D_MODEL + 2 * N_HEADS


def _flat_rows(a, nrows):
    flat = a.reshape(-1)
    return jnp.pad(flat, (0, nrows * D_MODEL - flat.shape[0])).reshape(nrows, D_MODEL)


def _weight_wire(w_in, wp_dn, wp_sb, w_out, w_up, w_down, dn_conv, ffn_conv):
    stack = jnp.concatenate([wp_dn.astype(BF16), wp_sb.astype(BF16), w_out.astype(BF16), w_down.astype(BF16),
                             _flat_rows(lax.bitcast_convert_type(dn_conv, BF16), ROW_TILE),
                             _flat_rows(lax.bitcast_convert_type(ffn_conv, BF16), ROW_TILE),
                             jnp.zeros((ROW_OFFS["spare"][1], D_MODEL), BF16)], axis=0)
    return [w_in.astype(BF16), w_up.astype(BF16), stack]


def _col_range(g, lo, hi, width):
    parts = []
    for s in range(g.shape[0]):
        a, b = max(lo, s * width), min(hi, (s + 1) * width)
        if a < b:
            parts.append(g[s][:, a - s * width:b - s * width])
    return parts[0] if len(parts) == 1 else jnp.concatenate(parts, axis=1)


def _unpack_weights(g_in, g_up, g_stack):
    def seg(nm):
        at, n = ROW_OFFS[nm]
        return g_stack[:, at:at + n, :]

    def f32_rows(nm, k, ncols):
        raw = seg(nm).reshape(N_CHIPS, -1)[:, :2 * k * ncols].reshape(N_CHIPS, k * ncols, 2)
        vals = lax.bitcast_convert_type(raw, F32).reshape(N_CHIPS, k, ncols)
        return vals.transpose(1, 0, 2).reshape(k, N_CHIPS * ncols)

    ffn_conv = f32_rows("ffn_conv", FFN_CONV, FFN_CONV_COLS)
    return {
        "w_dnqkv": _col_range(g_in, 0, Q_END, W_IN_COLS),
        "w_ab": jnp.pad(_col_range(g_in, Q_END, A_END, W_IN_COLS), ((0, 0), (0, LANES - 2 * N_HEADS))),
        "w_dngate": _col_range(g_in, A_END, G_END, W_IN_COLS),
        "w_sbqkv": _col_range(g_in, G_END, S_END, W_IN_COLS),
        "w_gl": _col_range(g_in, S_END, N_CHIPS * W_IN_COLS, W_IN_COLS),
        "wp_dn": seg("wp_dn").reshape(D_MODEL, D_MODEL),
        "wp_sb": seg("wp_sb").reshape(D_MODEL, D_MODEL),
        "w_out": seg("w_out").reshape(D_MODEL, D_MODEL),
        "w_up_g": _col_range(g_up, 0, D_FF, W_UP_COLS), "w_up_u": _col_range(g_up, D_FF, 2 * D_FF, W_UP_COLS),
        "w_down": seg("w_down").reshape(D_FF, D_MODEL),
        "dn_conv": f32_rows("dn_conv", DN_CONV, DN_CONV_COLS),
        "ffn_conv_g": ffn_conv[:, :D_FF], "ffn_conv_u": ffn_conv[:, D_FF:],
    }


def _grad_wire(gr):
    pieces = [(gr["w_dnqkv"], 0), (gr["w_ab"][:, :2 * N_HEADS], Q_END), (gr["w_dngate"], A_END),
              (gr["w_sbqkv"], G_END), (gr["w_gl"], S_END)]

    def in_block(s):
        lo, hi = s * W_IN_COLS, (s + 1) * W_IN_COLS
        parts = []
        for a, at in pieces:
            b0, b1 = max(lo, at), min(hi, at + a.shape[1])
            if b0 < b1:
                parts.append(a[:, b0 - at:b1 - at].astype(BF16))
        return parts[0] if len(parts) == 1 else jnp.concatenate(parts, axis=1)

    def cols(a, ncols):
        return a.reshape(a.shape[0], N_CHIPS, ncols).transpose(1, 0, 2)

    def rows(a, nrows):
        return a.astype(BF16).reshape(N_CHIPS, nrows, a.shape[1])

    def flat(a, nrows):
        a = a.astype(BF16).reshape(N_CHIPS, -1)
        return jnp.pad(a, ((0, 0), (0, nrows * D_MODEL - a.shape[1]))).reshape(N_CHIPS, nrows, D_MODEL)

    g_in = jnp.stack([in_block(s) for s in range(N_CHIPS)])
    up = [gr["w_up_g"], gr["w_up_u"]]
    g_up = jnp.stack([up[s // 2][:, (s % 2) * W_UP_COLS:(s % 2 + 1) * W_UP_COLS].astype(BF16) for s in range(N_CHIPS)])
    g_stack = jnp.concatenate([rows(gr["wp_dn"], PROJ_ROWS), rows(gr["wp_sb"], PROJ_ROWS), rows(gr["w_out"], PROJ_ROWS),
                               rows(gr["w_down"], W_DOWN_ROWS), flat(cols(gr["dn_conv"], DN_CONV_COLS), ROW_TILE),
                               flat(cols(gr["ffn_conv"], FFN_CONV_COLS), ROW_TILE),
                               jnp.zeros((N_CHIPS, ROW_OFFS["spare"][1], D_MODEL), BF16)], axis=1)
    return [g_in, g_up, g_stack]


def _unpack_grad_shard(r_in, r_up, r_stack):
    def seg(nm):
        at, n = ROW_OFFS[nm]
        return r_stack[at:at + n, :]

    return {
        "w_in": r_in, "w_up": r_up,
        "wp_dn": seg("wp_dn"), "wp_sb": seg("wp_sb"), "w_out": seg("w_out"), "w_down": seg("w_down"),
        "dn_conv": seg("dn_conv").reshape(-1)[:DN_CONV * DN_CONV_COLS].reshape(DN_CONV, DN_CONV_COLS),
        "ffn_conv": seg("ffn_conv").reshape(-1)[:FFN_CONV * FFN_CONV_COLS].reshape(FFN_CONV, FFN_CONV_COLS),
    }


def _lane_row(v):
    return jnp.pad(v.reshape(1, -1), ((0, 0), (0, LANES - v.size)))


def kernel(x, norm1_w, w_in, dn_conv_w, dn_A_log, dn_dt_bias, dn_norm_w, w_proj_dn, w_proj_sb, w_out, norm2_w, ffn_w_up, ffn_conv_w, ffn_w_down, norm_f_w, loss_target, m_norm1_w, m_w_in, m_dn_conv_w, m_dn_A_log, m_dn_dt_bias, m_dn_norm_w, m_w_proj_dn, m_w_proj_sb, m_w_out, m_norm2_w, m_ffn_w_up, m_ffn_conv_w, m_ffn_w_down, m_norm_f_w, v_norm1_w, v_w_in, v_dn_conv_w, v_dn_A_log, v_dn_dt_bias, v_dn_norm_w, v_w_proj_dn, v_w_proj_sb, v_w_out, v_norm2_w, v_ffn_w_up, v_ffn_conv_w, v_ffn_w_down, v_norm_f_w):
    wire = _weight_wire(w_in[0], w_proj_dn[0], w_proj_sb[0], w_out[0], ffn_w_up[0], ffn_w_down[0],
                        dn_conv_w[0], ffn_conv_w[0])
    chip_idx = (2 * lax.axis_index("x") + lax.axis_index("y")).astype(jnp.int32)
    gathered = [lax.dynamic_update_slice(g, mine[None], (chip_idx, 0, 0))
                for g, mine in zip(_gather_shards(wire), wire)]
    wts = _unpack_weights(*gathered)
    wts.update(norm1=norm1_w, norm2=norm2_w, normf=norm_f_w.reshape(1, D_MODEL), dn_norm=dn_norm_w,
               alog=_lane_row(dn_A_log), dtb=_lane_row(dn_dt_bias))

    loss_part, grad_x, gr = _local_step(x[0], loss_target[0], wts)

    c_idx = lax.axis_index("c").astype(jnp.int32).reshape(1)
    tags = ["w_in", "w_up", "rows"]
    wire_g = _grad_wire(gr)
    partial_sums = [_pair_add(g, got, c_idx, tag) for g, got, tag in zip(wire_g, _pair_exchange_halves(wire_g), tags)]
    reduced = [_sum_partials(p, got, chip_idx.reshape(1), tag)
               for p, got, tag in zip(partial_sums, _chip_exchange(partial_sums), tags)]
    is_south = lax.axis_index("c") == 0
    gsh = _unpack_grad_shard(*[jnp.concatenate([jnp.where(is_south, mine, other), jnp.where(is_south, other, mine)],
                                               axis=0) for mine, other in zip(reduced, _pair_share(reduced))])

    tail = jnp.concatenate([gr["dn_norm"], gr["alog"][:, :N_HEADS], gr["dtb"][:, :N_HEADS], loss_part[:, :1]], axis=1)
    small = jnp.concatenate([gr["norm1"], gr["norm2"], gr["normf"],
                             jnp.pad(tail, ((0, 0), (0, D_MODEL - tail.shape[1]))),
                             jnp.zeros((SMALL_ROWS - 4, D_MODEL), F32)], axis=0)
    small = _small_allreduce(small)
    at = HEAD_DIM
    g_small = {"norm1_w": small[0:1], "norm2_w": small[1:2], "norm_f_w": small[2],
               "dn_norm_w": small[3:4, :at], "dn_A_log": small[3:4, at:at + N_HEADS],
               "dn_dt_bias": small[3:4, at + N_HEADS:at + 2 * N_HEADS]}
    loss = small[3, at + 2 * N_HEADS]

    big = {"w_in": (w_in, m_w_in, v_w_in, gsh["w_in"]), "dn_conv_w": (dn_conv_w, m_dn_conv_w, v_dn_conv_w, gsh["dn_conv"]),
           "w_proj_dn": (w_proj_dn, m_w_proj_dn, v_w_proj_dn, gsh["wp_dn"]),
           "w_proj_sb": (w_proj_sb, m_w_proj_sb, v_w_proj_sb, gsh["wp_sb"]),
           "w_out": (w_out, m_w_out, v_w_out, gsh["w_out"]),
           "ffn_w_up": (ffn_w_up, m_ffn_w_up, v_ffn_w_up, gsh["w_up"]),
           "ffn_conv_w": (ffn_conv_w, m_ffn_conv_w, v_ffn_conv_w, gsh["ffn_conv"]),
           "ffn_w_down": (ffn_w_down, m_ffn_w_down, v_ffn_w_down, gsh["w_down"])}
    res = {}
    for nm, (w, m, v, g) in big.items():
        d, nm_, nv_ = _adamw(w[0], g, m[0], v[0], "adamw_" + nm)
        res[nm] = (g[None], d[None], nm_[None], nv_[None])

    names = ["norm1_w", "norm2_w", "norm_f_w", "dn_norm_w", "dn_A_log", "dn_dt_bias"]
    given = {"norm1_w": (norm1_w, m_norm1_w, v_norm1_w), "norm2_w": (norm2_w, m_norm2_w, v_norm2_w),
             "norm_f_w": (norm_f_w, m_norm_f_w, v_norm_f_w), "dn_norm_w": (dn_norm_w, m_dn_norm_w, v_dn_norm_w),
             "dn_A_log": (dn_A_log, m_dn_A_log, v_dn_A_log), "dn_dt_bias": (dn_dt_bias, m_dn_dt_bias, v_dn_dt_bias)}

    def stack(k, fill):
        rows = [jnp.pad(given[nm][k].reshape(1, -1), ((0, 0), (0, D_MODEL - given[nm][k].size)),
                        constant_values=fill) for nm in names]
        return jnp.concatenate(rows + [jnp.full((SMALL_ROWS - len(names), D_MODEL), fill, F32)], axis=0)

    g_rows = jnp.concatenate(
        [jnp.pad(g_small[nm].reshape(1, -1), ((0, 0), (0, D_MODEL - g_small[nm].size))) for nm in names]
        + [jnp.zeros((SMALL_ROWS - len(names), D_MODEL), F32)], axis=0)
    d_s, m_s, v_s = _adamw(stack(0, 0.0), g_rows, stack(1, 0.0), stack(2, 1.0), "adamw_small")
    for r, nm in enumerate(names):
        shape = given[nm][0].shape
        n = given[nm][0].size
        res[nm] = (g_small[nm].reshape(shape), d_s[r, :n].reshape(shape), m_s[r, :n].reshape(shape),
                   v_s[r, :n].reshape(shape))

    order = ["norm1_w", "w_in", "dn_conv_w", "dn_A_log", "dn_dt_bias", "dn_norm_w", "w_proj_dn", "w_proj_sb",
             "w_out", "norm2_w", "ffn_w_up", "ffn_conv_w", "ffn_w_down", "norm_f_w"]
    outs = [loss, grad_x[None]]
    for k in range(4):
        outs += [res[nm][k] for nm in order]
    return tuple(outs)
```

```python
import functools

import jax
import jax.numpy as jnp
from jax import lax
from jax.experimental import pallas as pl
from jax.experimental.pallas import tpu as pltpu

F32 = jnp.float32
BF16 = jnp.bfloat16
HIGHEST = lax.Precision.HIGHEST
MESH = pl.DeviceIdType.MESH

EPS = 1e-6
D_MODEL = 1024
N_HEADS = 8
HEAD_DIM = 128
DN_CONV = 4
DN_CHUNK = 64
D_FF = 2816
FFN_CONV = 3
ADAM_LR, ADAM_B1, ADAM_B2, ADAM_EPS, ADAM_WD, ADAM_STEP = 0.001, 0.9, 0.999, 1e-08, 0.01, 10

N_CHIPS = 4
LANES = 128
HALO = 8
VMEM_LIMIT = 48 * 1024 * 1024
SMALL_ROWS = 8


def _params(sem=None):
    return pltpu.CompilerParams(dimension_semantics=sem, vmem_limit_bytes=VMEM_LIMIT)


def _pick(n, target):
    best = None
    for b in range(LANES, min(n, target) + 1, LANES):
        if n % b == 0:
            best = b
    return best or n


ELEMENTWISE_COLS = 1408


def _rows(t, target=256):
    return min(t, target)


def _dot(a, b, precision=None):
    return lax.dot_general(a, b, (((1,), (0,)), ((), ())), precision=precision, preferred_element_type=F32)


def _dot_nt(a, b, precision=None):
    return lax.dot_general(a, b, (((1,), (1,)), ((), ())), precision=precision, preferred_element_type=F32)


def _dot_tn(a, b, precision=None):
    return lax.dot_general(a, b, (((0,), (0,)), ((), ())), precision=precision, preferred_element_type=F32)


def _rms(x, w):
    return x * lax.rsqrt(jnp.mean(x * x, axis=-1, keepdims=True) + EPS) * w


def _silu(x):
    return x * jax.nn.sigmoid(x)


def _softplus(x):
    return jnp.maximum(x, 0.0) + jnp.log(1.0 + jnp.exp(-jnp.abs(x)))


MM_BLOCK = 1408


def _mm(a, b, *, ta=False, tb=False, add=None, out_dtype=F32, name, bm=MM_BLOCK, bn=MM_BLOCK, bk=MM_BLOCK):
    m = a.shape[1] if ta else a.shape[0]
    k = a.shape[0] if ta else a.shape[1]
    n = b.shape[0] if tb else b.shape[1]
    bm, bn, bk = _pick(m, bm), _pick(n, bn), _pick(k, bk)
    nk = k // bk
    dims = (((0 if ta else 1,), (1 if tb else 0,)), ((), ()))

    def body(*refs):
        a_ref, b_ref = refs[:2]
        c_ref = refs[2] if add is not None else None
        o_ref = refs[3] if add is not None else refs[2]
        acc = refs[-1]
        kk = pl.program_id(2)
        part = lax.dot_general(a_ref[...].astype(BF16), b_ref[...].astype(BF16), dims, preferred_element_type=F32)

        def finish(r):
            if add is not None:
                r = r + c_ref[...].astype(F32)
            o_ref[...] = r.astype(out_dtype)

        if nk == 1:
            finish(part)
            return

        @pl.when(kk == 0)
        def _():
            acc[...] = part

        @pl.when(jnp.logical_and(kk > 0, kk < nk - 1))
        def _():
            acc[...] += part

        @pl.when(kk == nk - 1)
        def _():
            finish(acc[...] + part)

    a_spec = (pl.BlockSpec((bk, bm), lambda i, j, kk: (kk, i)) if ta
              else pl.BlockSpec((bm, bk), lambda i, j, kk: (i, kk)))
    b_spec = (pl.BlockSpec((bn, bk), lambda i, j, kk: (j, kk)) if tb
              else pl.BlockSpec((bk, bn), lambda i, j, kk: (kk, j)))
    o_spec = pl.BlockSpec((bm, bn), lambda i, j, kk: (i, j))
    in_specs = [a_spec, b_spec] + ([o_spec] if add is not None else [])
    args = (a, b) + ((add,) if add is not None else ())
    return pl.pallas_call(
        body, name=name, grid=(m // bm, n // bn, nk),
        in_specs=in_specs, out_specs=o_spec,
        out_shape=jax.ShapeDtypeStruct((m, n), out_dtype),
        scratch_shapes=[pltpu.VMEM((bm, bn), F32)] if nk > 1 else [],
        compiler_params=_params(("parallel", "parallel", "arbitrary")),
    )(*args)


def _norm1_fwd(x, w, w_ab):
    t = x.shape[0]
    tb = _rows(t)

    def body(x_ref, w_ref, wab_ref, n_ref, hab_ref):
        n = _rms(x_ref[...], w_ref[...]).astype(BF16)
        n_ref[...] = n
        hab_ref[...] = _dot(n, wab_ref[...])

    return pl.pallas_call(
        body, name="norm1_fwd", grid=(t // tb,),
        in_specs=[pl.BlockSpec((tb, D_MODEL), lambda i: (i, 0)),
                  pl.BlockSpec((1, D_MODEL), lambda i: (0, 0)),
                  pl.BlockSpec((D_MODEL, LANES), lambda i: (0, 0))],
        out_specs=[pl.BlockSpec((tb, D_MODEL), lambda i: (i, 0)),
                   pl.BlockSpec((tb, LANES), lambda i: (i, 0))],
        out_shape=[jax.ShapeDtypeStruct((t, D_MODEL), BF16), jax.ShapeDtypeStruct((t, LANES), F32)],
        compiler_params=_params(("arbitrary",)),
    )(x, w, w_ab)


def _norm1_bwd(x, w, dn, dres, dab, w_ab):
    t = x.shape[0]
    tb = _rows(t)

    def body(x_ref, w_ref, dn_ref, dres_ref, dab_ref, wab_ref, dx_ref, dw_ref):
        i = pl.program_id(0)
        g = dn_ref[...] + _dot_nt(dab_ref[...].astype(BF16), wab_ref[...])
        _, vjp = jax.vjp(_rms, x_ref[...], w_ref[...])
        dx, dw = vjp(g)
        dx_ref[...] = dres_ref[...] + dx

        @pl.when(i == 0)
        def _():
            dw_ref[...] = jnp.zeros_like(dw_ref)

        dw_ref[...] += dw

    row = pl.BlockSpec((tb, D_MODEL), lambda i: (i, 0))
    vec = pl.BlockSpec((1, D_MODEL), lambda i: (0, 0))
    return pl.pallas_call(
        body, name="norm1_bwd", grid=(t // tb,),
        in_specs=[row, vec, row, row, pl.BlockSpec((tb, LANES), lambda i: (i, 0)),
                  pl.BlockSpec((D_MODEL, LANES), lambda i: (0, 0))],
        out_specs=[row, vec],
        out_shape=[jax.ShapeDtypeStruct((t, D_MODEL), F32), jax.ShapeDtypeStruct((1, D_MODEL), F32)],
        compiler_params=_params(("arbitrary",)),
    )(x, w, dn, dres, dab, w_ab)


def _conv_fwd(x, w, name):
    t, c = x.shape
    kk = w.shape[0]
    tb, cb = _rows(t, 512), _pick(c, ELEMENTWISE_COLS)
    per = tb // HALO

    def body(x_ref, halo_ref, w_ref, y_ref, buf):
        i = pl.program_id(0)
        buf[pl.ds(HALO, tb), :] = x_ref[...]
        buf[pl.ds(0, HALO), :] = jnp.where(i == 0, 0.0, halo_ref[...])
        y = w_ref[0:1, :] * buf[pl.ds(HALO - (kk - 1), tb), :]
        for s in range(1, kk):
            y = y + w_ref[s:s + 1, :] * buf[pl.ds(HALO - (kk - 1) + s, tb), :]
        y_ref[...] = y

    return pl.pallas_call(
        body, name=name, grid=(t // tb, c // cb),
        in_specs=[pl.BlockSpec((tb, cb), lambda i, j: (i, j)),
                  pl.BlockSpec((HALO, cb), lambda i, j: (jnp.maximum(i * per - 1, 0), j)),
                  pl.BlockSpec((kk, cb), lambda i, j: (0, j))],
        out_specs=pl.BlockSpec((tb, cb), lambda i, j: (i, j)),
        out_shape=jax.ShapeDtypeStruct((t, c), F32),
        scratch_shapes=[pltpu.VMEM((tb + HALO, cb), F32)],
        compiler_params=_params(("parallel", "parallel")),
    )(x, x, w)


def _conv_bwd(dy, x, w, name, dx_dtype):
    t, c = x.shape
    kk = w.shape[0]
    tb, cb = _rows(t, 512), _pick(c, ELEMENTWISE_COLS)
    per = tb // HALO
    nblk = t // tb

    def body(dy_ref, after_ref, x_ref, w_ref, dx_ref, dw_ref, dbuf):
        i = pl.program_id(1)
        dbuf[pl.ds(0, tb), :] = dy_ref[...]
        dbuf[pl.ds(tb, HALO), :] = jnp.where(i == nblk - 1, 0.0, after_ref[...])

        @pl.when(i == 0)
        def _():
            dw_ref[...] = jnp.zeros_like(dw_ref)

        x = x_ref[...]
        dx = None
        for s in range(kk):
            shifted = dbuf[pl.ds(kk - 1 - s, tb), :]
            term = w_ref[s:s + 1, :] * shifted
            dx = term if dx is None else dx + term
            dw_ref[s:s + 1, :] += jnp.sum(shifted * x, axis=0, keepdims=True)
        dx_ref[...] = dx.astype(dx_dtype)

    blk = pl.BlockSpec((tb, cb), lambda j, i: (i, j))
    return pl.pallas_call(
        body, name=name, grid=(c // cb, nblk),
        in_specs=[blk,
                  pl.BlockSpec((HALO, cb), lambda j, i: (jnp.minimum((i + 1) * per, t // HALO - 1), j)),
                  blk,
                  pl.BlockSpec((kk, cb), lambda j, i: (0, j))],
        out_specs=[blk, pl.BlockSpec((HALO, cb), lambda j, i: (0, j))],
        out_shape=[jax.ShapeDtypeStruct((t, c), dx_dtype), jax.ShapeDtypeStruct((HALO, c), F32)],
        scratch_shapes=[pltpu.VMEM((tb + HALO, cb), F32)],
        compiler_params=_params(("parallel", "arbitrary")),
    )(dy, dy, x, w)


def _dn_prep_fn(c, hab, alog, dtb):
    s = _silu(c)
    heads = []
    for h in range(2 * N_HEADS):
        sh = s[:, h * HEAD_DIM:(h + 1) * HEAD_DIM]
        heads.append(sh * lax.rsqrt(jnp.sum(sh * sh, axis=-1, keepdims=True) + EPS))
    qn = jnp.concatenate(heads[:N_HEADS], axis=1)
    kn = jnp.concatenate(heads[N_HEADS:], axis=1)
    v = s[:, 2 * D_MODEL:]
    lane = lax.broadcasted_iota(jnp.int32, hab.shape, 1)
    g = -jnp.exp(alog) * _softplus(hab + dtb)
    beta = jax.nn.sigmoid(hab)
    gb = jnp.where(lane < N_HEADS, g, jnp.where(lane < 2 * N_HEADS, beta, 0.0))
    return qn, kn, v, gb


def _to_heads(ref, val):
    for h in range(N_HEADS):
        ref[h] = val[:, h * HEAD_DIM:(h + 1) * HEAD_DIM]


def _from_heads(ref):
    return jnp.concatenate([ref[h] for h in range(N_HEADS)], axis=1)


def _dn_prep_fwd(c, hab, alog, dtb):
    t = c.shape[0]
    tb = _rows(t)

    def body(c_ref, hab_ref, alog_ref, dtb_ref, q_ref, k_ref, v_ref, gb_ref):
        qn, kn, v, gb = _dn_prep_fn(c_ref[...], hab_ref[...], alog_ref[...], dtb_ref[...])
        _to_heads(q_ref, qn)
        _to_heads(k_ref, kn)
        _to_heads(v_ref, v)
        gb_ref[...] = gb

    hm = pl.BlockSpec((N_HEADS, tb, HEAD_DIM), lambda i: (0, i, 0))
    nar = pl.BlockSpec((tb, LANES), lambda i: (i, 0))
    vec = pl.BlockSpec((1, LANES), lambda i: (0, 0))
    return pl.pallas_call(
        body, name="dn_prep_fwd", grid=(t // tb,),
        in_specs=[pl.BlockSpec((tb, 3 * D_MODEL), lambda i: (i, 0)), nar, vec, vec],
        out_specs=[hm, hm, hm, nar],
        out_shape=[jax.ShapeDtypeStruct((N_HEADS, t, HEAD_DIM), F32)] * 3 + [jax.ShapeDtypeStruct((t, LANES), F32)],
        compiler_params=_params(("parallel",)),
    )(c, hab, alog, dtb)


def _dn_prep_bwd(c, hab, alog, dtb, dq, dk, dv, dgb):
    t = c.shape[0]
    tb = _rows(t)

    def body(c_ref, hab_ref, alog_ref, dtb_ref, dq_ref, dk_ref, dv_ref, dgb_ref,
             dc_ref, dhab_ref, dalog_ref, ddtb_ref):
        i = pl.program_id(0)
        _, vjp = jax.vjp(_dn_prep_fn, c_ref[...], hab_ref[...], alog_ref[...], dtb_ref[...])
        dc, dhab, dalog, ddtb = vjp((_from_heads(dq_ref), _from_heads(dk_ref), _from_heads(dv_ref), dgb_ref[...]))
        dc_ref[...] = dc
        dhab_ref[...] = dhab

        @pl.when(i == 0)
        def _():
            dalog_ref[...] = jnp.zeros_like(dalog_ref)
            ddtb_ref[...] = jnp.zeros_like(ddtb_ref)

        dalog_ref[...] += dalog
        ddtb_ref[...] += ddtb

    hm = pl.BlockSpec((N_HEADS, tb, HEAD_DIM), lambda i: (0, i, 0))
    wide = pl.BlockSpec((tb, 3 * D_MODEL), lambda i: (i, 0))
    nar = pl.BlockSpec((tb, LANES), lambda i: (i, 0))
    vec = pl.BlockSpec((1, LANES), lambda i: (0, 0))
    return pl.pallas_call(
        body, name="dn_prep_bwd", grid=(t // tb,),
        in_specs=[wide, nar, vec, vec, hm, hm, hm, nar],
        out_specs=[wide, nar, vec, vec],
        out_shape=[jax.ShapeDtypeStruct((t, 3 * D_MODEL), F32), jax.ShapeDtypeStruct((t, LANES), F32),
                   jax.ShapeDtypeStruct((1, LANES), F32), jax.ShapeDtypeStruct((1, LANES), F32)],
        compiler_params=_params(("arbitrary",)),
    )(c, hab, alog, dtb, dq, dk, dv, dgb)


DN_PREC = lax.Precision.HIGH
DN_GROUP = 8


def _dn_prec(a):
    return DN_PREC if a.dtype == F32 else None


def _bdot(a, b):
    return lax.dot_general(a, b, (((2,), (1,)), ((0,), (0,))), precision=_dn_prec(a), preferred_element_type=F32)


def _bdot_nt(a, b):
    return lax.dot_general(a, b, (((2,), (2,)), ((0,), (0,))), precision=_dn_prec(a), preferred_element_type=F32)


def _bdot_tn(a, b):
    return lax.dot_general(a, b, (((1,), (1,)), ((0,), (0,))), precision=_dn_prec(a), preferred_element_type=F32)


def _unit_lower_inverse(lmat):
    c = lmat.shape[-1]
    ri = lax.broadcasted_iota(jnp.int32, (c, c), 0)
    ci = lax.broadcasted_iota(jnp.int32, (c, c), 1)
    p = -lmat
    tinv = jnp.where(ri == ci, 1.0, 0.0) + p
    for _ in range(max(c.bit_length() - 2, 0)):
        p = _bdot(p, p)
        tinv = tinv + _bdot(tinv, p)
    return tinv


@jax.custom_vjp
def _solve_with(lmat, rhs, tinv):
    return _bdot(tinv, rhs)


def _solve_with_fwd(lmat, rhs, tinv):
    sol = _bdot(tinv, rhs)
    return sol, (sol, tinv)


def _solve_with_bwd(res, dsol):
    sol, tinv = res
    drhs = _bdot_tn(tinv, dsol)
    return -_bdot_nt(drhs, sol), drhs, jnp.zeros_like(tinv)


_solve_with.defvjp(_solve_with_fwd, _solve_with_bwd)


def _dn_local(q, k, v, grow, brow, tinv):
    g, c, _ = q.shape
    ri = lax.broadcasted_iota(jnp.int32, (c, c), 0)
    ci = lax.broadcasted_iota(jnp.int32, (c, c), 1)
    lower = ri >= ci
    as_col = lambda r: jnp.sum(jnp.where(ri == ci, jnp.broadcast_to(r, (g, c, c)), 0.0), axis=2, keepdims=True)
    gcol, bcol = as_col(grow), as_col(brow)
    gc_col = jnp.sum(jnp.where(lower, jnp.broadcast_to(grow, (g, c, c)), 0.0), axis=2, keepdims=True)
    gc_row = jnp.sum(jnp.where(ri <= ci, jnp.broadcast_to(gcol, (g, c, c)), 0.0), axis=1, keepdims=True)
    qs = q * (HEAD_DIM ** -0.5)
    kb = k * bcol
    vb = v * bcol
    decay = jnp.where(lower, jnp.exp(jnp.where(lower, gc_col - gc_row, 0.0)), 0.0)
    lmat = jnp.where(ri > ci, _bdot_nt(kb.astype(BF16), k.astype(BF16)) * decay, 0.0)
    eg = jnp.exp(gc_col)
    rhs = jnp.concatenate([vb, kb * eg], axis=2)
    if tinv is None:
        tinv = _unit_lower_inverse(lmat)
    sol = _solve_with(lmat, rhs, tinv)
    a_qk = jnp.where(lower, _bdot_nt(qs.astype(BF16), k.astype(BF16)) * decay, 0.0)
    g_last = jnp.sum(grow, axis=2, keepdims=True)
    kdec = k * jnp.exp(g_last - gc_col)
    egl = jnp.broadcast_to(jnp.exp(g_last), (g, 1, HEAD_DIM))
    return sol[:, :, :HEAD_DIM], sol[:, :, HEAD_DIM:], a_qk, qs * eg, kdec, egl, tinv


def _dn_seq(u, w, a_qk, qe, kdec, egl, s_in):
    b16 = lambda x: x.astype(BF16)
    v_new = u - _bdot(b16(w), b16(s_in))
    o = _bdot(b16(qe), b16(s_in)) + _bdot(b16(a_qk), b16(v_new))
    return o, s_in * egl + _bdot_tn(b16(kdec), b16(v_new))


def _dn_local_specs(t):
    grp = min(DN_GROUP, t // DN_CHUNK)
    rows = grp * DN_CHUNK
    blk = pl.BlockSpec((1, rows, HEAD_DIM), lambda h, i: (h, i, 0))
    row = pl.BlockSpec((1, grp, 1, DN_CHUNK), lambda h, i: (h, i, 0, 0))
    sq = pl.BlockSpec((1, grp, DN_CHUNK, DN_CHUNK), lambda h, i: (h, i, 0, 0))
    lane = pl.BlockSpec((1, grp, 1, HEAD_DIM), lambda h, i: (h, i, 0, 0))
    return grp, blk, row, sq, lane


def _dn_shapes(t):
    nchunk = t // DN_CHUNK
    big = jax.ShapeDtypeStruct((N_HEADS, t, HEAD_DIM), F32)
    row = jax.ShapeDtypeStruct((N_HEADS, nchunk, 1, DN_CHUNK), F32)
    sq = jax.ShapeDtypeStruct((N_HEADS, nchunk, DN_CHUNK, DN_CHUNK), F32)
    lane = jax.ShapeDtypeStruct((N_HEADS, nchunk, 1, HEAD_DIM), F32)
    return big, row, sq, lane


def _dn_local_fwd(q, k, v, grow, brow, wire=()):
    t = q.shape[1]
    grp, blk, row, sq, lane = _dn_local_specs(t)
    big, _, sqs, lanes = _dn_shapes(t)
    n = len(wire)
    groups = t // (grp * DN_CHUNK)
    steps = N_HEADS * groups

    def body(q_ref, k_ref, v_ref, gr_ref, br_ref, *rest):
        u_ref, w_ref, a_ref, qe_ref, kd_ref, egl_ref, t_ref = rest[n:n + 7]
        if n:
            begin, middle, end = _gather_protocol(rest[:n], rest[n + 7:2 * n + 7], *rest[2 * n + 7:])
            step = pl.program_id(0) * groups + pl.program_id(1)
            pl.when(step == 0)(begin)
            pl.when(step == (5 * steps) // 8)(middle)
        split = lambda r: r[0].reshape(grp, DN_CHUNK, HEAD_DIM)
        u, w, a_qk, qe, kdec, egl, tinv = _dn_local(split(q_ref), split(k_ref), split(v_ref), gr_ref[0],
                                                     br_ref[0], None)
        for ref, val in ((u_ref, u), (w_ref, w), (qe_ref, qe), (kd_ref, kdec)):
            ref[0] = val.reshape(grp * DN_CHUNK, HEAD_DIM)
        a_ref[0] = a_qk
        egl_ref[0] = egl
        t_ref[0] = tinv
        if n:
            pl.when(step == steps - 1)(end)

    assert n == 0 or steps >= 3
    return pl.pallas_call(
        body, name="dn_local_fwd", grid=(N_HEADS, groups),
        in_specs=[blk, blk, blk, row, row] + _hbm_specs(n),
        out_specs=[blk, blk, sq, blk, blk, lane, sq] + _hbm_specs(n),
        out_shape=[big, big, sqs, big, big, lanes, sqs] + _gather_out_shapes(wire),
        scratch_shapes=_gather_sems(n) if n else [],
        compiler_params=_params(("arbitrary", "arbitrary")),
    )(q, k, v, grow, brow, *wire)


def _dn_local_bwd(q, k, v, grow, brow, tinv, du, dw, da, dqe, dkd, degl):
    t = q.shape[1]
    grp, blk, row, sq, lane = _dn_local_specs(t)
    big, rows_, _, _ = _dn_shapes(t)

    def body(q_ref, k_ref, v_ref, gr_ref, br_ref, t_ref, du_ref, dw_ref, da_ref, dqe_ref, dkd_ref,
             degl_ref, dq_ref, dk_ref, dv_ref, dgr_ref, dbr_ref):
        split = lambda r: r[0].reshape(grp, DN_CHUNK, HEAD_DIM)
        tinv_v = t_ref[0]
        fn = lambda q_, k_, v_, gr_, br_: _dn_local(q_, k_, v_, gr_, br_, tinv_v)[:6]
        _, vjp = jax.vjp(fn, split(q_ref), split(k_ref), split(v_ref), gr_ref[0], br_ref[0])
        dq, dk, dv, dgr, dbr = vjp((split(du_ref), split(dw_ref), da_ref[0], split(dqe_ref), split(dkd_ref),
                                    degl_ref[0]))
        for ref, val in ((dq_ref, dq), (dk_ref, dk), (dv_ref, dv)):
            ref[0] = val.reshape(grp * DN_CHUNK, HEAD_DIM)
        dgr_ref[0] = dgr
        dbr_ref[0] = dbr

    return pl.pallas_call(
        body, name="dn_local_bwd", grid=(N_HEADS, t // (grp * DN_CHUNK)),
        in_specs=[blk, blk, blk, row, row, sq, blk, blk, sq, blk, blk, lane],
        out_specs=[blk, blk, blk, row, row],
        out_shape=[big, big, big, rows_, rows_],
        compiler_params=_params(("parallel", "parallel")),
    )(q, k, v, grow, brow, tinv, du, dw, da, dqe, dkd, degl)


def _dn_seq_specs(nchunk, rev):
    def idx(n):
        return nchunk - 1 - n if rev else n

    blk = pl.BlockSpec((N_HEADS, DN_CHUNK, HEAD_DIM), lambda n: (0, idx(n), 0))
    sq = pl.BlockSpec((N_HEADS, 1, DN_CHUNK, DN_CHUNK), lambda n: (0, idx(n), 0, 0))
    lane = pl.BlockSpec((N_HEADS, 1, 1, HEAD_DIM), lambda n: (0, idx(n), 0, 0))
    st = pl.BlockSpec((N_HEADS, 1, HEAD_DIM, HEAD_DIM), lambda n: (0, idx(n), 0, 0))
    return blk, sq, lane, st


def _dn_seq_fwd(u, w, a_qk, qe, kdec, egl):
    t = u.shape[1]
    nchunk = t // DN_CHUNK
    blk, sq, lane, st = _dn_seq_specs(nchunk, False)

    def body(u_ref, w_ref, a_ref, qe_ref, kd_ref, egl_ref, o_ref, s_ref, state):
        @pl.when(pl.program_id(0) == 0)
        def _():
            state[...] = jnp.zeros_like(state)

        s_in = state[...]
        s_ref[:, 0] = s_in
        o, s_out = _dn_seq(u_ref[...], w_ref[...], a_ref[:, 0], qe_ref[...], kd_ref[...], egl_ref[:, 0], s_in)
        o_ref[...] = o
        state[...] = s_out

    return pl.pallas_call(
        body, name="dn_seq_fwd", grid=(nchunk,),
        in_specs=[blk, blk, sq, blk, blk, lane],
        out_specs=[blk, st],
        out_shape=[jax.ShapeDtypeStruct((N_HEADS, t, HEAD_DIM), F32),
                   jax.ShapeDtypeStruct((N_HEADS, nchunk, HEAD_DIM, HEAD_DIM), F32)],
        scratch_shapes=[pltpu.VMEM((N_HEADS, HEAD_DIM, HEAD_DIM), F32)],
        compiler_params=_params(("arbitrary",)),
    )(u, w, a_qk, qe, kdec, egl)


def _dn_seq_bwd(u, w, a_qk, qe, kdec, egl, states, do):
    t = u.shape[1]
    nchunk = t // DN_CHUNK
    blk, sq, lane, st = _dn_seq_specs(nchunk, True)
    big, _, sqs, lanes = _dn_shapes(t)

    def body(u_ref, w_ref, a_ref, qe_ref, kd_ref, egl_ref, s_ref, do_ref,
             du_ref, dw_ref, da_ref, dqe_ref, dkd_ref, degl_ref, dstate):
        @pl.when(pl.program_id(0) == 0)
        def _():
            dstate[...] = jnp.zeros_like(dstate)

        _, vjp = jax.vjp(_dn_seq, u_ref[...], w_ref[...], a_ref[:, 0], qe_ref[...], kd_ref[...], egl_ref[:, 0],
                         s_ref[:, 0])
        du, dw, da, dqe, dkd, degl, ds = vjp((do_ref[...], dstate[...]))
        du_ref[...] = du
        dw_ref[...] = dw
        da_ref[:, 0] = da
        dqe_ref[...] = dqe
        dkd_ref[...] = dkd
        degl_ref[:, 0] = degl
        dstate[...] = ds

    return pl.pallas_call(
        body, name="dn_seq_bwd", grid=(nchunk,),
        in_specs=[blk, blk, sq, blk, blk, lane, st, blk],
        out_specs=[blk, blk, sq, blk, blk, lane],
        out_shape=[big, big, sqs, big, big, lanes],
        scratch_shapes=[pltpu.VMEM((N_HEADS, HEAD_DIM, HEAD_DIM), F32)],
        compiler_params=_params(("arbitrary",)),
    )(u, w, a_qk, qe, kdec, egl, states, do)


def _dn_post_fn(o, gate, w):
    outs = []
    for h in range(N_HEADS):
        sl = slice(h * HEAD_DIM, (h + 1) * HEAD_DIM)
        outs.append(_rms(o[:, sl], w) * _silu(gate[:, sl]))
    return jnp.concatenate(outs, axis=1)


def _dn_post_fwd(o, gate, w):
    t = gate.shape[0]
    tb = _rows(t)

    def body(o_ref, g_ref, w_ref, y_ref):
        y_ref[...] = _dn_post_fn(_from_heads(o_ref), g_ref[...], w_ref[...]).astype(BF16)

    row = pl.BlockSpec((tb, D_MODEL), lambda i: (i, 0))
    hm = pl.BlockSpec((N_HEADS, tb, HEAD_DIM), lambda i: (0, i, 0))
    return pl.pallas_call(
        body, name="dn_post_fwd", grid=(t // tb,),
        in_specs=[hm, row, pl.BlockSpec((1, HEAD_DIM), lambda i: (0, 0))],
        out_specs=row, out_shape=jax.ShapeDtypeStruct((t, D_MODEL), BF16),
        compiler_params=_params(("parallel",)),
    )(o, gate, w)


def _dn_post_bwd(o, gate, w, dy):
    t = gate.shape[0]
    tb = _rows(t)

    def body(o_ref, g_ref, w_ref, dy_ref, do_ref, dg_ref, dw_ref):
        i = pl.program_id(0)
        _, vjp = jax.vjp(_dn_post_fn, _from_heads(o_ref), g_ref[...], w_ref[...])
        do, dg, dw = vjp(dy_ref[...])
        _to_heads(do_ref, do)
        dg_ref[...] = dg.astype(BF16)

        @pl.when(i == 0)
        def _():
            dw_ref[...] = jnp.zeros_like(dw_ref)

        dw_ref[...] += dw

    row = pl.BlockSpec((tb, D_MODEL), lambda i: (i, 0))
    hm = pl.BlockSpec((N_HEADS, tb, HEAD_DIM), lambda i: (0, i, 0))
    vec = pl.BlockSpec((1, HEAD_DIM), lambda i: (0, 0))
    return pl.pallas_call(
        body, name="dn_post_bwd", grid=(t // tb,),
        in_specs=[hm, row, vec, row],
        out_specs=[hm, row, vec],
        out_shape=[jax.ShapeDtypeStruct((N_HEADS, t, HEAD_DIM), F32), jax.ShapeDtypeStruct((t, D_MODEL), BF16),
                   jax.ShapeDtypeStruct((1, HEAD_DIM), F32)],
        compiler_params=_params(("arbitrary",)),
    )(o, gate, w, dy)


def _split_bf16(x):
    hi = x.astype(BF16)
    lo = (x - hi.astype(F32)).astype(BF16)
    return hi, lo


SB_Q_BLOCK = 512
SB_K_BLOCK = 256
SB_NEGLIGIBLE = -60.0


def _sb_logits(q, kb, mask, scale):
    z = _dot_nt(q, kb) * scale
    ls = jnp.minimum(z, 0.0) - jnp.log(1.0 + jnp.exp(-jnp.abs(z)))
    lk = ls - z
    if mask is not None:
        lk = jnp.where(mask, lk, 0.0)
    return ls, lk


def _sb_blocks(t):
    bq = min(SB_Q_BLOCK, t)
    bk = min(SB_K_BLOCK, bq)
    return bq, bk, bq // bk


def _sb_fwd(qkv):
    t = qkv.shape[0]
    bq, bk, nd = _sb_blocks(t)
    scale = HEAD_DIM ** -0.5

    def body(q_ref, k_ref, v_ref, o_ref, tot_ref, used_ref):
        i = pl.program_id(1)
        q = q_ref[...]
        rj = lax.broadcasted_iota(jnp.int32, (bk, bk), 0)
        cj = lax.broadcasted_iota(jnp.int32, (bk, bk), 1)
        after = (rj > cj).astype(BF16)
        trow = lax.broadcasted_iota(jnp.int32, (bq, bk), 0)
        scol = lax.broadcasted_iota(jnp.int32, (bq, bk), 1)

        def tile(j, run, acc, mask):
            off = pl.multiple_of(j * bk, bk)
            kb = k_ref[pl.ds(off, bk), :]
            vb = v_ref[pl.ds(off, bk), :]
            ls, lk = _sb_logits(q, kb, mask, scale)
            hi, lo = _split_bf16(lk)
            between = _dot(hi, after) + _dot(lo, after) + run
            a = jnp.exp(ls + between)
            if mask is not None:
                a = jnp.where(mask, a, 0.0)
            acc = acc + _dot(a.astype(BF16), vb)
            return run + jnp.sum(lk, axis=1, keepdims=True), acc

        run, acc = jnp.zeros((bq, 1), F32), jnp.zeros((bq, HEAD_DIM), F32)
        for d in reversed(range(nd)):
            run, acc = tile(i * nd + d, run, acc, scol + d * bk < trow)
        def more(c):
            return jnp.logical_and(c[0] < i * nd, jnp.max(c[1]) > SB_NEGLIGIBLE)

        def far(c):
            run_, acc_ = tile(i * nd - 1 - c[0], c[1], c[2], None)
            return c[0] + 1, run_, acc_

        used, run, acc = lax.while_loop(more, far, (jnp.int32(0), run, acc))
        o_ref[...] = acc.astype(BF16)
        tot_ref[...] = jnp.broadcast_to(run, (bq, HEAD_DIM))
        used_ref[...] = jnp.full(used_ref.shape, used, F32)

    qs = pl.BlockSpec((bq, HEAD_DIM), lambda h, i: (i, h))
    ks = pl.BlockSpec((t, HEAD_DIM), lambda h, i: (0, N_HEADS + h))
    vs = pl.BlockSpec((t, HEAD_DIM), lambda h, i: (0, 2 * N_HEADS + h))
    return pl.pallas_call(
        body, name="sb_fwd", grid=(N_HEADS, t // bq),
        in_specs=[qs, ks, vs], out_specs=[qs, qs, pl.BlockSpec((1, 1, 1, LANES), lambda h, i: (h, i, 0, 0))],
        out_shape=[jax.ShapeDtypeStruct((t, D_MODEL), BF16), jax.ShapeDtypeStruct((t, D_MODEL), F32),
                   jax.ShapeDtypeStruct((N_HEADS, t // bq, 1, LANES), F32)],
        compiler_params=_params(("parallel", "arbitrary")),
    )(qkv, qkv, qkv)


def _sb_bwd(qkv, tot, used, do):
    t = qkv.shape[0]
    bq, bk, nd = _sb_blocks(t)
    scale = HEAD_DIM ** -0.5

    def body(q_ref, k_ref, v_ref, tot_ref, used_ref, do_ref, dq_ref, dk_ref, dv_ref):
        i = pl.program_id(1)

        @pl.when(i == 0)
        def _():
            dk_ref[...] = jnp.zeros_like(dk_ref)
            dv_ref[...] = jnp.zeros_like(dv_ref)

        q = q_ref[...]
        do = do_ref[...]
        total = tot_ref[:, 0:1]
        rj = lax.broadcasted_iota(jnp.int32, (bk, bk), 0)
        cj = lax.broadcasted_iota(jnp.int32, (bk, bk), 1)
        upto = (rj <= cj).astype(BF16)
        before = (rj < cj).astype(BF16)
        trow = lax.broadcasted_iota(jnp.int32, (bq, bk), 0)
        scol = lax.broadcasted_iota(jnp.int32, (bq, bk), 1)

        def tile(j, run_k, run_e, dq, mask):
            off = pl.multiple_of(j * bk, bk)
            kb = k_ref[pl.ds(off, bk), :]
            vb = v_ref[pl.ds(off, bk), :]
            ls, lk = _sb_logits(q, kb, mask, scale)
            hi, lo = _split_bf16(lk)
            between = total - (_dot(hi, upto) + _dot(lo, upto) + run_k)
            a = jnp.exp(ls + between)
            if mask is not None:
                a = jnp.where(mask, a, 0.0)
            e = a * _dot_nt(do, vb)
            ehi, elo = _split_bf16(e)
            pre = _dot(ehi, before) + _dot(elo, before) + run_e
            sig = jnp.exp(ls)
            dz = e * (1.0 - sig) - pre * sig
            if mask is not None:
                dz = jnp.where(mask, dz, 0.0)
            dz = (dz * scale).astype(BF16)
            dq = dq + _dot(dz, kb)
            dk_ref[pl.ds(off, bk), :] += _dot_tn(dz, q)
            dv_ref[pl.ds(off, bk), :] += _dot_tn(a.astype(BF16), do)
            return (run_k + jnp.sum(lk, axis=1, keepdims=True),
                    run_e + jnp.sum(e, axis=1, keepdims=True), dq)

        zero = jnp.zeros((bq, 1), F32)
        visited = jnp.clip(jnp.max(used_ref[...]).astype(jnp.int32), 0, i * nd)
        carry = lax.fori_loop(i * nd - visited, i * nd, lambda j, c: tile(j, c[0], c[1], c[2], None),
                              (zero, zero, jnp.zeros((bq, HEAD_DIM), F32)))
        for d in range(nd):
            carry = tile(i * nd + d, *carry, scol + d * bk < trow)
        dq_ref[...] = carry[2]

    qs = pl.BlockSpec((bq, HEAD_DIM), lambda h, i: (i, h))
    ks = pl.BlockSpec((t, HEAD_DIM), lambda h, i: (0, N_HEADS + h))
    vs = pl.BlockSpec((t, HEAD_DIM), lambda h, i: (0, 2 * N_HEADS + h))
    full = pl.BlockSpec((t, HEAD_DIM), lambda h, i: (0, h))
    big = jax.ShapeDtypeStruct((t, D_MODEL), F32)
    return pl.pallas_call(
        body, name="sb_bwd", grid=(N_HEADS, t // bq),
        in_specs=[qs, ks, vs, qs, pl.BlockSpec((1, 1, 1, LANES), lambda h, i: (h, i, 0, 0)), qs],
        out_specs=[qs, full, full],
        out_shape=[big, big, big],
        compiler_params=_params(("parallel", "arbitrary")),
    )(qkv, qkv, qkv, tot, used, do)


def _merge_fwd(o_dn, o_sb, gl, x, wp_dn, wp_sb, w_out, w2):
    t = x.shape[0]
    tb = _rows(t)

    def body(odn_ref, osb_ref, gl_ref, x_ref, wpd_ref, wps_ref, wo_ref, w2_ref,
             pdn_ref, psb_ref, mix_ref, x1_ref, n2_ref):
        pdn = _dot(odn_ref[...], wpd_ref[...])
        psb = _dot(osb_ref[...], wps_ref[...])
        gates = jax.nn.sigmoid(gl_ref[...])
        mixed = (gates[:, :D_MODEL] * pdn + gates[:, D_MODEL:] * psb).astype(BF16)
        x1 = x_ref[...] + _dot(mixed, wo_ref[...])
        pdn_ref[...] = pdn
        psb_ref[...] = psb
        mix_ref[...] = mixed
        x1_ref[...] = x1
        n2_ref[...] = _rms(x1, w2_ref[...]).astype(BF16)

    row = pl.BlockSpec((tb, D_MODEL), lambda i: (i, 0))
    sq = pl.BlockSpec((D_MODEL, D_MODEL), lambda i: (0, 0))
    f = jax.ShapeDtypeStruct((t, D_MODEL), F32)
    b = jax.ShapeDtypeStruct((t, D_MODEL), BF16)
    return pl.pallas_call(
        body, name="merge_fwd", grid=(t // tb,),
        in_specs=[row, row, pl.BlockSpec((tb, 2 * D_MODEL), lambda i: (i, 0)), row, sq, sq, sq,
                  pl.BlockSpec((1, D_MODEL), lambda i: (0, 0))],
        out_specs=[row] * 5, out_shape=[f, f, b, f, b],
        compiler_params=_params(("parallel",)),
    )(o_dn, o_sb, gl, x, wp_dn, wp_sb, w_out, w2)


def _merge_bwd(dx2, dn2, x1, w2, gl, pdn, psb, wp_dn, wp_sb, w_out):
    t = x1.shape[0]
    tb = _rows(t)

    def body(dx2_ref, dn2_ref, x1_ref, w2_ref, gl_ref, pdn_ref, psb_ref, wpd_ref, wps_ref, wo_ref,
             dx1_ref, dw2_ref, dgl_ref, dpdn_ref, dpsb_ref, dodn_ref, dosb_ref):
        i = pl.program_id(0)
        _, vjp = jax.vjp(_rms, x1_ref[...], w2_ref[...])
        dxn, dw2 = vjp(dn2_ref[...])
        dx1 = dx2_ref[...] + dxn
        dx1_ref[...] = dx1

        @pl.when(i == 0)
        def _():
            dw2_ref[...] = jnp.zeros_like(dw2_ref)

        dw2_ref[...] += dw2
        dmix = _dot_nt(dx1.astype(BF16), wo_ref[...])
        gates = jax.nn.sigmoid(gl_ref[...])
        g_dn, g_sb = gates[:, :D_MODEL], gates[:, D_MODEL:]
        dpdn = (dmix * g_dn).astype(BF16)
        dpsb = (dmix * g_sb).astype(BF16)
        dgl_ref[:, :D_MODEL] = (dmix * pdn_ref[...] * g_dn * (1.0 - g_dn)).astype(BF16)
        dgl_ref[:, D_MODEL:] = (dmix * psb_ref[...] * g_sb * (1.0 - g_sb)).astype(BF16)
        dpdn_ref[...] = dpdn
        dpsb_ref[...] = dpsb
        dodn_ref[...] = _dot_nt(dpdn, wpd_ref[...])
        dosb_ref[...] = _dot_nt(dpsb, wps_ref[...]).astype(BF16)

    row = pl.BlockSpec((tb, D_MODEL), lambda i: (i, 0))
    wide = pl.BlockSpec((tb, 2 * D_MODEL), lambda i: (i, 0))
    sq = pl.BlockSpec((D_MODEL, D_MODEL), lambda i: (0, 0))
    vec = pl.BlockSpec((1, D_MODEL), lambda i: (0, 0))
    f = jax.ShapeDtypeStruct((t, D_MODEL), F32)
    b = jax.ShapeDtypeStruct((t, D_MODEL), BF16)
    return pl.pallas_call(
        body, name="merge_bwd", grid=(t // tb,),
        in_specs=[row, row, row, vec, wide, row, row, sq, sq, sq],
        out_specs=[row, vec, wide, row, row, row, row],
        out_shape=[f, jax.ShapeDtypeStruct((1, D_MODEL), F32), jax.ShapeDtypeStruct((t, 2 * D_MODEL), BF16),
                   b, b, f, b],
        compiler_params=_params(("arbitrary",)),
    )(dx2, dn2, x1, w2, gl, pdn, psb, wp_dn, wp_sb, w_out)


def _conv_taps(buf, w_ref, first, rows):
    y = w_ref[0:1, :] * buf[pl.ds(first, rows), :]
    for s in range(1, w_ref.shape[0]):
        y = y + w_ref[s:s + 1, :] * buf[pl.ds(first + s, rows), :]
    return y


def _ffn_mid_fwd(pre_g, pre_u, wg, wu):
    t, c = pre_g.shape
    kk = wg.shape[0]
    tb, cb = _rows(t), _pick(c, ELEMENTWISE_COLS)
    per = tb // HALO

    def body(g_ref, gh_ref, u_ref, uh_ref, wg_ref, wu_ref, a_ref, gbuf, ubuf):
        i = pl.program_id(0)
        for buf, ref, halo in ((gbuf, g_ref, gh_ref), (ubuf, u_ref, uh_ref)):
            buf[pl.ds(HALO, tb), :] = ref[...]
            buf[pl.ds(0, HALO), :] = jnp.where(i == 0, 0.0, halo[...])
        ug = _conv_taps(gbuf, wg_ref, HALO - (kk - 1), tb)
        uu = _conv_taps(ubuf, wu_ref, HALO - (kk - 1), tb)
        a_ref[...] = (_silu(ug) * uu).astype(BF16)

    blk = pl.BlockSpec((tb, cb), lambda i, j: (i, j))
    halo = pl.BlockSpec((HALO, cb), lambda i, j: (jnp.maximum(i * per - 1, 0), j))
    wspec = pl.BlockSpec((kk, cb), lambda i, j: (0, j))
    return pl.pallas_call(
        body, name="ffn_mid_fwd", grid=(t // tb, c // cb),
        in_specs=[blk, halo, blk, halo, wspec, wspec], out_specs=blk,
        out_shape=jax.ShapeDtypeStruct((t, c), BF16),
        scratch_shapes=[pltpu.VMEM((tb + HALO, cb), F32)] * 2,
        compiler_params=_params(("parallel", "parallel")),
    )(pre_g, pre_g, pre_u, pre_u, wg, wu)


def _ffn_mid_bwd(pre_g, pre_u, wg, wu, da):
    t, c = pre_g.shape
    kk = wg.shape[0]
    tb, cb = _rows(t), _pick(c, ELEMENTWISE_COLS)
    per = tb // HALO
    nblk = t // tb
    ext = tb + HALO

    def body(g_ref, gb_ref, ga_ref, u_ref, ub_ref, ua_ref, da_ref, daa_ref, wg_ref, wu_ref,
             dg_ref, du_ref, dwg_ref, dwu_ref, gbuf, ubuf, dabuf, dgbuf, dubuf):
        i = pl.program_id(1)
        last = i == nblk - 1
        for buf, ref, before, after in ((gbuf, g_ref, gb_ref, ga_ref), (ubuf, u_ref, ub_ref, ua_ref)):
            buf[pl.ds(0, HALO), :] = jnp.where(i == 0, 0.0, before[...])
            buf[pl.ds(HALO, tb), :] = ref[...]
            buf[pl.ds(HALO + tb, HALO), :] = jnp.where(last, 0.0, after[...])
        dabuf[pl.ds(0, tb), :] = da_ref[...]
        dabuf[pl.ds(tb, HALO), :] = jnp.where(last, 0.0, daa_ref[...])
        ug = _conv_taps(gbuf, wg_ref, HALO - (kk - 1), ext)
        uu = _conv_taps(ubuf, wu_ref, HALO - (kk - 1), ext)
        _, vjp = jax.vjp(lambda g, u: _silu(g) * u, ug, uu)
        dgbuf[...], dubuf[...] = vjp(dabuf[...])

        @pl.when(i == 0)
        def _():
            dwg_ref[...] = jnp.zeros_like(dwg_ref)
            dwu_ref[...] = jnp.zeros_like(dwu_ref)

        for dbuf, xbuf, w_ref, dx_ref, dw_ref in ((dgbuf, gbuf, wg_ref, dg_ref, dwg_ref),
                                                  (dubuf, ubuf, wu_ref, du_ref, dwu_ref)):
            x = xbuf[pl.ds(HALO, tb), :]
            dx = None
            for s in range(kk):
                shifted = dbuf[pl.ds(kk - 1 - s, tb), :]
                term = w_ref[s:s + 1, :] * shifted
                dx = term if dx is None else dx + term
                dw_ref[s:s + 1, :] += jnp.sum(shifted * x, axis=0, keepdims=True)
            dx_ref[...] = dx.astype(BF16)

    blk = pl.BlockSpec((tb, cb), lambda j, i: (i, j))
    before = pl.BlockSpec((HALO, cb), lambda j, i: (jnp.maximum(i * per - 1, 0), j))
    after = pl.BlockSpec((HALO, cb), lambda j, i: (jnp.minimum((i + 1) * per, t // HALO - 1), j))
    wspec = pl.BlockSpec((kk, cb), lambda j, i: (0, j))
    dwspec = pl.BlockSpec((HALO, cb), lambda j, i: (0, j))
    half = jax.ShapeDtypeStruct((t, c), BF16)
    dwshape = jax.ShapeDtypeStruct((HALO, c), F32)
    return pl.pallas_call(
        body, name="ffn_mid_bwd", grid=(c // cb, nblk),
        in_specs=[blk, before, after, blk, before, after, blk, after, wspec, wspec],
        out_specs=[blk, blk, dwspec, dwspec],
        out_shape=[half, half, dwshape, dwshape],
        scratch_shapes=[pltpu.VMEM((ext + HALO, cb), F32)] * 2 + [pltpu.VMEM((ext, cb), F32)] * 3,
        compiler_params=_params(("parallel", "arbitrary")),
    )(pre_g, pre_g, pre_g, pre_u, pre_u, pre_u, da, da, wg, wu)


def _down_loss(a, w_down, x1, wf, target):
    t = x1.shape[0]
    tb = _rows(t)

    def body(a_ref, wd_ref, x1_ref, wf_ref, tgt_ref, dx2_ref, dwf_ref, loss_ref):
        i = pl.program_id(0)
        x2 = x1_ref[...] + _dot(a_ref[...], wd_ref[...])
        y, vjp = jax.vjp(_rms, x2, wf_ref[...])
        err = y - tgt_ref[...]
        dx2, dwf = vjp(err * (1.0 / D_MODEL))
        dx2_ref[...] = dx2
        part = jnp.sum(jnp.sum(err * err, axis=1, keepdims=True), axis=0, keepdims=True) * (0.5 / D_MODEL)

        @pl.when(i == 0)
        def _():
            dwf_ref[...] = jnp.zeros_like(dwf_ref)
            loss_ref[...] = jnp.zeros_like(loss_ref)

        dwf_ref[...] += dwf
        loss_ref[...] += jnp.broadcast_to(part, loss_ref.shape)

    row = pl.BlockSpec((tb, D_MODEL), lambda i: (i, 0))
    vec = pl.BlockSpec((1, D_MODEL), lambda i: (0, 0))
    return pl.pallas_call(
        body, name="down_loss", grid=(t // tb,),
        in_specs=[pl.BlockSpec((tb, D_FF), lambda i: (i, 0)), pl.BlockSpec((D_FF, D_MODEL), lambda i: (0, 0)),
                  row, vec, row],
        out_specs=[row, vec, pl.BlockSpec((1, LANES), lambda i: (0, 0))],
        out_shape=[jax.ShapeDtypeStruct((t, D_MODEL), F32), jax.ShapeDtypeStruct((1, D_MODEL), F32),
                   jax.ShapeDtypeStruct((1, LANES), F32)],
        compiler_params=_params(("arbitrary",)),
    )(a, w_down, x1, wf, target)


def _local_step(x, target, wts, late_wire=(), late_weights=None):
    t = x.shape[0]
    nchunk = t // DN_CHUNK

    n1, hab = _norm1_fwd(x, wts["norm1"], wts["w_ab"])
    dnqkv = _mm(n1, wts["w_dnqkv"], name="h_dnqkv")
    dngate = _mm(n1, wts["w_dngate"], name="h_dngate")
    sbqkv = _mm(n1, wts["w_sbqkv"], out_dtype=BF16, name="h_sbqkv")
    gl = _mm(n1, wts["w_gl"], name="h_gl")

    cdn = _conv_fwd(dnqkv, wts["dn_conv"], "dn_conv_fwd")
    qn, kn, vv, gb = _dn_prep_fwd(cdn, hab, wts["alog"], wts["dtb"])
    per_head = gb[:, :2 * N_HEADS].T.reshape(2 * N_HEADS, nchunk, DN_CHUNK)
    grow, brow = per_head[:N_HEADS, :, None, :], per_head[N_HEADS:, :, None, :]
    u_dn, w_dn, a_qk, qe, kdec, egl, tinv, *late = _dn_local_fwd(qn, kn, vv, grow, brow, late_wire)
    if late_wire:
        wts = {**wts, **late_weights(late)}
    o_raw, states = _dn_seq_fwd(u_dn, w_dn, a_qk, qe, kdec, egl)
    o_dn = _dn_post_fwd(o_raw, dngate, wts["dn_norm"])

    o_sb, tot, sb_used = _sb_fwd(sbqkv)

    pdn, psb, mixed, x1, n2 = _merge_fwd(o_dn, o_sb, gl, x, wts["wp_dn"], wts["wp_sb"], wts["w_out"],
                                         wts["norm2"])
    pre_g = _mm(n2, wts["w_up_g"], name="ffn_up_g")
    pre_u = _mm(n2, wts["w_up_u"], name="ffn_up_u")
    act = _ffn_mid_fwd(pre_g, pre_u, wts["ffn_conv_g"], wts["ffn_conv_u"])
    dx2, d_normf, loss_part = _down_loss(act, wts["w_down"], x1, wts["normf"], target)

    grads = {"normf": d_normf}
    da = _mm(dx2, wts["w_down"], tb=True, name="d_act")
    grads["w_down"] = _mm(act, dx2, ta=True, out_dtype=BF16, name="dw_down")
    dpre_g, dpre_u, dcw_g, dcw_u = _ffn_mid_bwd(pre_g, pre_u, wts["ffn_conv_g"], wts["ffn_conv_u"], da)
    grads["ffn_conv"] = jnp.concatenate([dcw_g[:FFN_CONV], dcw_u[:FFN_CONV]], axis=1)
    dn2 = _mm(dpre_g, wts["w_up_g"], tb=True, name="dn2_g")
    dn2 = _mm(dpre_u, wts["w_up_u"], tb=True, add=dn2, name="dn2_u")
    grads["w_up_g"] = _mm(n2, dpre_g, ta=True, out_dtype=BF16, name="dw_up_g")
    grads["w_up_u"] = _mm(n2, dpre_u, ta=True, out_dtype=BF16, name="dw_up_u")

    dx1, grads["norm2"], dgl, dpdn, dpsb, do_dn, do_sb = _merge_bwd(
        dx2, dn2, x1, wts["norm2"], gl, pdn, psb, wts["wp_dn"], wts["wp_sb"], wts["w_out"])
    grads["w_out"] = _mm(mixed, dx1, ta=True, out_dtype=BF16, name="dw_out")
    grads["wp_dn"] = _mm(o_dn, dpdn, ta=True, out_dtype=BF16, name="dw_proj_dn")
    grads["wp_sb"] = _mm(o_sb, dpsb, ta=True, out_dtype=BF16, name="dw_proj_sb")

    dsq, dsk, dsv = _sb_bwd(sbqkv, tot, sb_used, do_sb)
    dsbqkv = jnp.concatenate([dsq, dsk, dsv], axis=1).astype(BF16)

    do_raw, ddngate, grads["dn_norm"] = _dn_post_bwd(o_raw, dngate, wts["dn_norm"], do_dn)
    seq_grads = _dn_seq_bwd(u_dn, w_dn, a_qk, qe, kdec, egl, states, do_raw)
    dqn, dkn, dvv, dgrow, dbrow = _dn_local_bwd(qn, kn, vv, grow, brow, tinv, *seq_grads)
    dgb = jnp.concatenate([dgrow.reshape(N_HEADS, t), dbrow.reshape(N_HEADS, t)], axis=0).T
    dgb = jnp.pad(dgb, ((0, 0), (0, LANES - 2 * N_HEADS)))
    dcdn, dhab, grads["alog"], grads["dtb"] = _dn_prep_bwd(cdn, hab, wts["alog"], wts["dtb"], dqn, dkn, dvv, dgb)
    ddnqkv, dcw_dn = _conv_bwd(dcdn, dnqkv, wts["dn_conv"], "dn_conv_bwd", BF16)
    grads["dn_conv"] = dcw_dn[:DN_CONV]

    dn1 = _mm(ddnqkv, wts["w_dnqkv"], tb=True, name="dn1_dnqkv")
    dn1 = _mm(ddngate, wts["w_dngate"], tb=True, add=dn1, name="dn1_dngate")
    dn1 = _mm(dsbqkv, wts["w_sbqkv"], tb=True, add=dn1, name="dn1_sbqkv")
    dn1 = _mm(dgl, wts["w_gl"], tb=True, add=dn1, name="dn1_gl")
    grads["w_dnqkv"] = _mm(n1, ddnqkv, ta=True, out_dtype=BF16, name="dw_dnqkv")
    grads["w_dngate"] = _mm(n1, ddngate, ta=True, out_dtype=BF16, name="dw_dngate")
    grads["w_sbqkv"] = _mm(n1, dsbqkv, ta=True, out_dtype=BF16, name="dw_sbqkv")
    grads["w_gl"] = _mm(n1, dgl, ta=True, out_dtype=BF16, name="dw_gl")
    grads["w_ab"] = _mm(n1, dhab, ta=True, out_dtype=BF16, name="dw_ab")
    grad_x, grads["norm1"] = _norm1_bwd(x, wts["norm1"], dn1, dx1, dhab, wts["w_ab"])
    return loss_part, grad_x, grads


def _place():
    return lax.axis_index("x"), lax.axis_index("y"), lax.axis_index("c")


def _hbm_specs(n):
    return [pl.BlockSpec(memory_space=pltpu.HBM)] * n


GATHER_SEMS = 8


def _gather_protocol(ins, outs, send_sems, recv_sems):
    n = len(ins)
    x, y, c = _place()
    me = 2 * x + y
    sibling = (x, y, 1 - c)
    xn, yn, dg = (1 - x, y), (x, 1 - y), (1 - x, 1 - y)
    idx = lambda chip: 2 * chip[0] + chip[1]

    def part(a, chip_index, core, quarter=None):
        half = ins[a].shape[0] // 2
        if quarter is None:
            return outs[a].at[chip_index, pl.ds(core * half, half), :]
        return outs[a].at[chip_index, pl.ds(core * half + quarter * (half // 2), half // 2), :]

    def copy(a, k, src, dst, to):
        return pltpu.make_async_remote_copy(src_ref=src, dst_ref=dst, send_sem=send_sems.at[GATHER_SEMS * a + k],
                                            recv_sem=recv_sems.at[GATHER_SEMS * a + k], device_id=to,
                                            device_id_type=MESH)

    def sent(a, k):
        half = ins[a].shape[0] // 2
        my_half = ins[a].at[pl.ds(c * half, half), :]
        if k < 2:
            return copy(a, k, my_half, part(a, me, c), (*(xn, yn)[k], c))
        if k < 4:
            src = part(a, idx((xn, yn)[k - 2]), c, k - 2)
            return copy(a, k, src, src, (*(yn, xn)[k - 2], c))
        src = (part(a, idx(xn), c), part(a, idx(yn), c), part(a, idx(dg), c, 0), part(a, idx(dg), c, 1))[k - 4]
        return copy(a, k, src, src, sibling)

    def landed(a, k):
        dst = (part(a, idx(xn), c), part(a, idx(yn), c), part(a, idx(dg), c, 0), part(a, idx(dg), c, 1),
               part(a, idx(xn), 1 - c), part(a, idx(yn), 1 - c), part(a, idx(dg), 1 - c, 0),
               part(a, idx(dg), 1 - c, 1))[k]
        return copy(a, k, dst, dst, sibling)

    def begin():
        for a in range(n):
            sent(a, 0).start()
            sent(a, 1).start()

    def middle():
        for a in range(n):
            for k in range(2):
                landed(a, k).wait_recv()
                sent(a, 2 + k).start()
                sent(a, 4 + k).start()

    def end():
        for a in range(n):
            for k in (2, 3):
                landed(a, k).wait_recv()
                sent(a, 4 + k).start()
        for a in range(n):
            for k in range(4, GATHER_SEMS):
                landed(a, k).wait_recv()
        for a in range(n):
            for k in range(GATHER_SEMS):
                sent(a, k).wait_send()

    return begin, middle, end


def _gather_out_shapes(shards):
    return [jax.ShapeDtypeStruct((N_CHIPS,) + s.shape, s.dtype) for s in shards]


def _gather_sems(n):
    return [pltpu.SemaphoreType.DMA((GATHER_SEMS * n,)), pltpu.SemaphoreType.DMA((GATHER_SEMS * n,))]


def _gather_shards(shards):
    n = len(shards)

    def body(*refs):
        begin, middle, end = _gather_protocol(refs[:n], refs[n:2 * n], *refs[2 * n:])
        begin()
        middle()
        end()

    return pl.pallas_call(
        body, name="gather_weights", in_specs=_hbm_specs(n), out_specs=_hbm_specs(n),
        out_shape=_gather_out_shapes(shards), scratch_shapes=_gather_sems(n),
    )(*shards)


def _pair_exchange_halves(gs):
    n = len(gs)

    def body(*refs):
        ins, outs, (send_sems, recv_sems) = refs[:n], refs[n:2 * n], refs[2 * n:]
        x, y, c = _place()
        cps = []
        for a in range(n):
            half = ins[a].shape[1] // 2
            cp = pltpu.make_async_remote_copy(src_ref=ins[a].at[:, pl.ds((1 - c) * half, half), :], dst_ref=outs[a],
                                              send_sem=send_sems.at[a], recv_sem=recv_sems.at[a],
                                              device_id=(x, y, 1 - c), device_id_type=MESH)
            cp.start()
            cps.append(cp)
        for cp in cps:
            cp.wait()

    return pl.pallas_call(
        body, name="grad_pair_exchange", in_specs=_hbm_specs(n), out_specs=_hbm_specs(n),
        out_shape=[jax.ShapeDtypeStruct((g.shape[0], g.shape[1] // 2, g.shape[2]), g.dtype) for g in gs],
        scratch_shapes=[pltpu.SemaphoreType.DMA((n,)), pltpu.SemaphoreType.DMA((n,))],
    )(*gs)


def _pick_rows(n, target=1024):
    best = 16
    for b in range(16, min(n, target) + 1, 16):
        if n % b == 0:
            best = b
    return best


def _pair_add(g, got, c_idx, tag):
    nsh, rows, cols = g.shape
    half = rows // 2
    rb = _pick_rows(half)

    def body(c_ref, g_ref, got_ref, o_ref):
        o_ref[...] = (g_ref[...].astype(F32) + got_ref[...].astype(F32)).astype(BF16)

    nb = half // rb
    grid_spec = pltpu.PrefetchScalarGridSpec(
        num_scalar_prefetch=1, grid=(nsh, nb),
        in_specs=[pl.BlockSpec((1, rb, cols), lambda s, i, c_ref: (s, c_ref[0] * nb + i, 0)),
                  pl.BlockSpec((1, rb, cols), lambda s, i, c_ref: (s, i, 0))],
        out_specs=pl.BlockSpec((1, rb, cols), lambda s, i, c_ref: (s, i, 0)))
    return pl.pallas_call(
        body, name="grad_pair_add_" + tag, grid_spec=grid_spec,
        out_shape=jax.ShapeDtypeStruct((nsh, half, cols), BF16),
        compiler_params=_params(("parallel", "parallel")),
    )(c_idx, g, got)


def _chip_exchange(ps):
    n = len(ps)

    def body(*refs):
        ins, outs, (send_sems, recv_sems) = refs[:n], refs[n:2 * n], refs[2 * n:]
        x, y, c = _place()
        chips = [(1 - x, y), (x, 1 - y), (1 - x, 1 - y)]
        sends = []
        for a in range(n):
            for j, (px, py) in enumerate(chips):
                cp = pltpu.make_async_remote_copy(src_ref=ins[a].at[2 * px + py], dst_ref=outs[a].at[j],
                                                  send_sem=send_sems.at[3 * a + j], recv_sem=recv_sems.at[3 * a + j],
                                                  device_id=(px, py, c), device_id_type=MESH)
                cp.start()
                sends.append(cp)
        for cp in sends:
            cp.wait_recv()
        for cp in sends:
            cp.wait_send()

    return pl.pallas_call(
        body, name="grad_chip_exchange", in_specs=_hbm_specs(n), out_specs=_hbm_specs(n),
        out_shape=[jax.ShapeDtypeStruct((N_CHIPS - 1,) + p.shape[1:], p.dtype) for p in ps],
        scratch_shapes=[pltpu.SemaphoreType.DMA((3 * n,)), pltpu.SemaphoreType.DMA((3 * n,))],
    )(*ps)


def _sum_partials(p, got, chip_idx, tag):
    nsh, half, cols = got.shape
    rb = _pick_rows(half)

    def body(me_ref, p_ref, got_ref, o_ref):
        acc = p_ref[0].astype(F32)
        for s in range(nsh):
            acc = acc + got_ref[s].astype(F32)
        o_ref[...] = acc

    grid_spec = pltpu.PrefetchScalarGridSpec(
        num_scalar_prefetch=1, grid=(half // rb,),
        in_specs=[pl.BlockSpec((1, rb, cols), lambda i, me_ref: (me_ref[0], i, 0)),
                  pl.BlockSpec((nsh, rb, cols), lambda i, me_ref: (0, i, 0))],
        out_specs=pl.BlockSpec((rb, cols), lambda i, me_ref: (i, 0)))
    return pl.pallas_call(
        body, name="grad_sum_chips_" + tag, grid_spec=grid_spec,
        out_shape=jax.ShapeDtypeStruct((half, cols), F32),
        compiler_params=_params(("parallel",)),
    )(chip_idx, p, got)


def _pair_share(rs):
    n = len(rs)

    def body(*refs):
        ins, outs, (send_sems, recv_sems) = refs[:n], refs[n:2 * n], refs[2 * n:]
        x, y, c = _place()
        cps = []
        for a in range(n):
            cp = pltpu.make_async_remote_copy(src_ref=ins[a], dst_ref=outs[a], send_sem=send_sems.at[a],
                                              recv_sem=recv_sems.at[a], device_id=(x, y, 1 - c),
                                              device_id_type=MESH)
            cp.start()
            cps.append(cp)
        for cp in cps:
            cp.wait()

    return pl.pallas_call(
        body, name="grad_pair_share", in_specs=_hbm_specs(n), out_specs=_hbm_specs(n),
        out_shape=[jax.ShapeDtypeStruct(r.shape, r.dtype) for r in rs],
        scratch_shapes=[pltpu.SemaphoreType.DMA((n,)), pltpu.SemaphoreType.DMA((n,))],
    )(*rs)


def _small_allreduce(v):
    rows, cols = v.shape
    ndev = 8

    def body(in_ref, out_ref, slots, send_sems, recv_sems):
        x, y, c = _place()
        me = 4 * x + 2 * y + c
        slots[me] = in_ref[...]
        sends = []
        for k in range(1, ndev):
            peer = (x ^ (k >> 2), y ^ ((k >> 1) & 1), c ^ (k & 1))
            cp = pltpu.make_async_remote_copy(src_ref=in_ref, dst_ref=slots.at[me], send_sem=send_sems.at[k - 1],
                                              recv_sem=recv_sems.at[k - 1], device_id=peer, device_id_type=MESH)
            cp.start()
            sends.append(cp)
        for k in range(1, ndev):
            there = slots.at[me ^ k]
            pltpu.make_async_remote_copy(src_ref=there, dst_ref=there, send_sem=send_sems.at[k - 1],
                                         recv_sem=recv_sems.at[k - 1], device_id=(x, y, c),
                                         device_id_type=MESH).wait_recv()
        for cp in sends:
            cp.wait_send()
        acc = slots[0]
        for s in range(1, ndev):
            acc = acc + slots[s]
        out_ref[...] = acc

    return pl.pallas_call(
        body, name="small_allreduce",
        in_specs=[pl.BlockSpec(memory_space=pltpu.VMEM)],
        out_specs=pl.BlockSpec(memory_space=pltpu.VMEM),
        out_shape=jax.ShapeDtypeStruct((rows, cols), F32),
        scratch_shapes=[pltpu.VMEM((ndev, rows, cols), F32), pltpu.SemaphoreType.DMA((ndev - 1,)),
                        pltpu.SemaphoreType.DMA((ndev - 1,))],
    )(v)


def _adamw(w, g, m, v, name):
    r, c = w.shape
    rb = r if r <= 128 else _pick_rows_8(r, 128)
    c1 = 1.0 - ADAM_B1 ** ADAM_STEP
    c2 = 1.0 - ADAM_B2 ** ADAM_STEP

    def body(w_ref, g_ref, m_ref, v_ref, d_ref, nm_ref, nv_ref):
        gg = g_ref[...]
        nm = ADAM_B1 * m_ref[...] + (1.0 - ADAM_B1) * gg
        nv = ADAM_B2 * v_ref[...] + (1.0 - ADAM_B2) * (gg * gg)
        d_ref[...] = -ADAM_LR * ((nm / c1) / (jnp.sqrt(nv / c2) + ADAM_EPS) + ADAM_WD * w_ref[...])
        nm_ref[...] = nm
        nv_ref[...] = nv

    blk = pl.BlockSpec((rb, c), lambda i: (i, 0))
    shp = jax.ShapeDtypeStruct((r, c), F32)
    return pl.pallas_call(
        body, name=name, grid=(r // rb,), in_specs=[blk] * 4, out_specs=[blk] * 3, out_shape=[shp] * 3,
        compiler_params=_params(("parallel",)),
    )(w, g, m, v)


def _pick_rows_8(n, target):
    best = n
    for b in range(8, min(n, target) + 1, 8):
        if n % b == 0:
            best = b
    return best


W_IN_COLS = 2308
W_UP_COLS = 1408
W_DOWN_ROWS = 704
DN_CONV_COLS = 768
FFN_CONV_COLS = 1408
PROJ_ROWS = 256
ROW_TILE = 16
ROW_SEGS = [("wp_dn", PROJ_ROWS), ("wp_sb", PROJ_ROWS), ("w_out", PROJ_ROWS), ("w_down", W_DOWN_ROWS),
            ("dn_conv", ROW_TILE), ("ffn_conv", ROW_TILE), ("spare", 2 * ROW_TILE)]
ROW_OFFS = {nm: (sum(n for _, n in ROW_SEGS[:i]), n) for i, (nm, n) in enumerate(ROW_SEGS)}
STACK_ROWS = sum(n for _, n in ROW_SEGS)
assert all(n % ROW_TILE == 0 for _, n in ROW_SEGS) and STACK_ROWS % (4 * ROW_TILE) == 0
Q_END, A_END, G_END, S_END = 3 * D_MODEL, 3 * D_MODEL + 2 * N_HEADS, 4 * D_MODEL + 2 * N_HEADS, 7 * D_MODEL + 2 * N_HEADS


def _flat_rows(a, nrows):
    flat = a.reshape(-1)
    return jnp.pad(flat, (0, nrows * D_MODEL - flat.shape[0])).reshape(nrows, D_MODEL)


IN_EXTRA_ROWS = 64


def _weight_wire(w_in, wp_dn, wp_sb, w_out, w_up, w_down, dn_conv, ffn_conv):
    bits = lax.bitcast_convert_type(dn_conv, BF16).reshape(-1)
    extra = jnp.pad(bits, (0, IN_EXTRA_ROWS * W_IN_COLS - bits.shape[0])).reshape(IN_EXTRA_ROWS, W_IN_COLS)
    stack = jnp.concatenate([wp_dn.astype(BF16), wp_sb.astype(BF16), w_out.astype(BF16), w_down.astype(BF16),
                             jnp.zeros((ROW_TILE, D_MODEL), BF16),
                             _flat_rows(lax.bitcast_convert_type(ffn_conv, BF16), ROW_TILE),
                             jnp.zeros((ROW_OFFS["spare"][1], D_MODEL), BF16)], axis=0)
    return [jnp.concatenate([w_in.astype(BF16), extra], axis=0)], [w_up.astype(BF16), stack]


def _col_range(g, lo, hi, width):
    parts = []
    for s in range(g.shape[0]):
        a, b = max(lo, s * width), min(hi, (s + 1) * width)
        if a < b:
            parts.append(g[s][:, a - s * width:b - s * width])
    return parts[0] if len(parts) == 1 else jnp.concatenate(parts, axis=1)


def _f32_rows(raw, k, ncols):
    raw = raw.reshape(N_CHIPS, -1)[:, :2 * k * ncols].reshape(N_CHIPS, k * ncols, 2)
    vals = lax.bitcast_convert_type(raw, F32).reshape(N_CHIPS, k, ncols)
    return vals.transpose(1, 0, 2).reshape(k, N_CHIPS * ncols)


def _unpack_early(g_in):
    w = g_in[:, :D_MODEL, :]
    return {
        "w_dnqkv": _col_range(w, 0, Q_END, W_IN_COLS),
        "w_ab": jnp.pad(_col_range(w, Q_END, A_END, W_IN_COLS), ((0, 0), (0, LANES - 2 * N_HEADS))),
        "w_dngate": _col_range(w, A_END, G_END, W_IN_COLS),
        "w_sbqkv": _col_range(w, G_END, S_END, W_IN_COLS),
        "w_gl": _col_range(w, S_END, N_CHIPS * W_IN_COLS, W_IN_COLS),
        "dn_conv": _f32_rows(g_in[:, D_MODEL:, :], DN_CONV, DN_CONV_COLS),
    }


def _unpack_late(g_up, g_stack):
    def seg(nm):
        at, n = ROW_OFFS[nm]
        return g_stack[:, at:at + n, :]

    ffn_conv = _f32_rows(seg("ffn_conv"), FFN_CONV, FFN_CONV_COLS)
    return {
        "wp_dn": seg("wp_dn").reshape(D_MODEL, D_MODEL),
        "wp_sb": seg("wp_sb").reshape(D_MODEL, D_MODEL),
        "w_out": seg("w_out").reshape(D_MODEL, D_MODEL),
        "w_up_g": _col_range(g_up, 0, D_FF, W_UP_COLS), "w_up_u": _col_range(g_up, D_FF, 2 * D_FF, W_UP_COLS),
        "w_down": seg("w_down").reshape(D_FF, D_MODEL),
        "ffn_conv_g": ffn_conv[:, :D_FF], "ffn_conv_u": ffn_conv[:, D_FF:],
    }


def _grad_wire(gr):
    pieces = [(gr["w_dnqkv"], 0), (gr["w_ab"][:, :2 * N_HEADS], Q_END), (gr["w_dngate"], A_END),
              (gr["w_sbqkv"], G_END), (gr["w_gl"], S_END)]

    def in_block(s):
        lo, hi = s * W_IN_COLS, (s + 1) * W_IN_COLS
        parts = []
        for a, at in pieces:
            b0, b1 = max(lo, at), min(hi, at + a.shape[1])
            if b0 < b1:
                parts.append(a[:, b0 - at:b1 - at].astype(BF16))
        return parts[0] if len(parts) == 1 else jnp.concatenate(parts, axis=1)

    def cols(a, ncols):
        return a.reshape(a.shape[0], N_CHIPS, ncols).transpose(1, 0, 2)

    def rows(a, nrows):
        return a.astype(BF16).reshape(N_CHIPS, nrows, a.shape[1])

    def flat(a, nrows):
        a = a.astype(BF16).reshape(N_CHIPS, -1)
        return jnp.pad(a, ((0, 0), (0, nrows * D_MODEL - a.shape[1]))).reshape(N_CHIPS, nrows, D_MODEL)

    g_in = jnp.stack([in_block(s) for s in range(N_CHIPS)])
    up = [gr["w_up_g"], gr["w_up_u"]]
    g_up = jnp.stack([up[s // 2][:, (s % 2) * W_UP_COLS:(s % 2 + 1) * W_UP_COLS].astype(BF16) for s in range(N_CHIPS)])
    g_stack = jnp.concatenate([rows(gr["wp_dn"], PROJ_ROWS), rows(gr["wp_sb"], PROJ_ROWS), rows(gr["w_out"], PROJ_ROWS),
                               rows(gr["w_down"], W_DOWN_ROWS), flat(cols(gr["dn_conv"], DN_CONV_COLS), ROW_TILE),
                               flat(cols(gr["ffn_conv"], FFN_CONV_COLS), ROW_TILE),
                               jnp.zeros((N_CHIPS, ROW_OFFS["spare"][1], D_MODEL), BF16)], axis=1)
    return [g_in, g_up, g_stack]


def _unpack_grad_shard(r_in, r_up, r_stack):
    def seg(nm):
        at, n = ROW_OFFS[nm]
        return r_stack[at:at + n, :]

    return {
        "w_in": r_in, "w_up": r_up,
        "wp_dn": seg("wp_dn"), "wp_sb": seg("wp_sb"), "w_out": seg("w_out"), "w_down": seg("w_down"),
        "dn_conv": seg("dn_conv").reshape(-1)[:DN_CONV * DN_CONV_COLS].reshape(DN_CONV, DN_CONV_COLS),
        "ffn_conv": seg("ffn_conv").reshape(-1)[:FFN_CONV * FFN_CONV_COLS].reshape(FFN_CONV, FFN_CONV_COLS),
    }


def _lane_row(v):
    return jnp.pad(v.reshape(1, -1), ((0, 0), (0, LANES - v.size)))


def kernel(x, norm1_w, w_in, dn_conv_w, dn_A_log, dn_dt_bias, dn_norm_w, w_proj_dn, w_proj_sb, w_out, norm2_w, ffn_w_up, ffn_conv_w, ffn_w_down, norm_f_w, loss_target, m_norm1_w, m_w_in, m_dn_conv_w, m_dn_A_log, m_dn_dt_bias, m_dn_norm_w, m_w_proj_dn, m_w_proj_sb, m_w_out, m_norm2_w, m_ffn_w_up, m_ffn_conv_w, m_ffn_w_down, m_norm_f_w, v_norm1_w, v_w_in, v_dn_conv_w, v_dn_A_log, v_dn_dt_bias, v_dn_norm_w, v_w_proj_dn, v_w_proj_sb, v_w_out, v_norm2_w, v_ffn_w_up, v_ffn_conv_w, v_ffn_w_down, v_norm_f_w):
    early, late = _weight_wire(w_in[0], w_proj_dn[0], w_proj_sb[0], w_out[0], ffn_w_up[0], ffn_w_down[0],
                               dn_conv_w[0], ffn_conv_w[0])
    chip_idx = (2 * lax.axis_index("x") + lax.axis_index("y")).astype(jnp.int32)

    def with_mine(gathered, wire):
        return [lax.dynamic_update_slice(g, mine[None], (chip_idx, 0, 0)) for g, mine in zip(gathered, wire)]

    wts = _unpack_early(*with_mine(_gather_shards(early), early))
    wts.update(norm1=norm1_w, norm2=norm2_w, normf=norm_f_w.reshape(1, D_MODEL), dn_norm=dn_norm_w,
               alog=_lane_row(dn_A_log), dtb=_lane_row(dn_dt_bias))

    loss_part, grad_x, gr = _local_step(x[0], loss_target[0], wts, late,
                                        lambda gathered: _unpack_late(*with_mine(gathered, late)))

    c_idx = lax.axis_index("c").astype(jnp.int32).reshape(1)
    tags = ["w_in", "w_up", "rows"]
    wire_g = _grad_wire(gr)
    partial_sums = [_pair_add(g, got, c_idx, tag) for g, got, tag in zip(wire_g, _pair_exchange_halves(wire_g), tags)]
    reduced = [_sum_partials(p, got, chip_idx.reshape(1), tag)
               for p, got, tag in zip(partial_sums, _chip_exchange(partial_sums), tags)]
    is_south = lax.axis_index("c") == 0
    gsh = _unpack_grad_shard(*[jnp.concatenate([jnp.where(is_south, mine, other), jnp.where(is_south, other, mine)],
                                               axis=0) for mine, other in zip(reduced, _pair_share(reduced))])

    tail = jnp.concatenate([gr["dn_norm"], gr["alog"][:, :N_HEADS], gr["dtb"][:, :N_HEADS], loss_part[:, :1]], axis=1)
    small = jnp.concatenate([gr["norm1"], gr["norm2"], gr["normf"],
                             jnp.pad(tail, ((0, 0), (0, D_MODEL - tail.shape[1]))),
                             jnp.zeros((SMALL_ROWS - 4, D_MODEL), F32)], axis=0)
    small = _small_allreduce(small)
    at = HEAD_DIM
    g_small = {"norm1_w": small[0:1], "norm2_w": small[1:2], "norm_f_w": small[2],
               "dn_norm_w": small[3:4, :at], "dn_A_log": small[3:4, at:at + N_HEADS],
               "dn_dt_bias": small[3:4, at + N_HEADS:at + 2 * N_HEADS]}
    loss = small[3, at + 2 * N_HEADS]

    big = {"w_in": (w_in, m_w_in, v_w_in, gsh["w_in"]), "dn_conv_w": (dn_conv_w, m_dn_conv_w, v_dn_conv_w, gsh["dn_conv"]),
           "w_proj_dn": (w_proj_dn, m_w_proj_dn, v_w_proj_dn, gsh["wp_dn"]),
           "w_proj_sb": (w_proj_sb, m_w_proj_sb, v_w_proj_sb, gsh["wp_sb"]),
           "w_out": (w_out, m_w_out, v_w_out, gsh["w_out"]),
           "ffn_w_up": (ffn_w_up, m_ffn_w_up, v_ffn_w_up, gsh["w_up"]),
           "ffn_conv_w": (ffn_conv_w, m_ffn_conv_w, v_ffn_conv_w, gsh["ffn_conv"]),
           "ffn_w_down": (ffn_w_down, m_ffn_w_down, v_ffn_w_down, gsh["w_down"])}
    res = {}
    for nm, (w, m, v, g) in big.items():
        d, nm_, nv_ = _adamw(w[0], g, m[0], v[0], "adamw_" + nm)
        res[nm] = (g[None], d[None], nm_[None], nv_[None])

    names = ["norm1_w", "norm2_w", "norm_f_w", "dn_norm_w", "dn_A_log", "dn_dt_bias"]
    given = {"norm1_w": (norm1_w, m_norm1_w, v_norm1_w), "norm2_w": (norm2_w, m_norm2_w, v_norm2_w),
             "norm_f_w": (norm_f_w, m_norm_f_w, v_norm_f_w), "dn_norm_w": (dn_norm_w, m_dn_norm_w, v_dn_norm_w),
             "dn_A_log": (dn_A_log, m_dn_A_log, v_dn_A_log), "dn_dt_bias": (dn_dt_bias, m_dn_dt_bias, v_dn_dt_bias)}

    def stack(k, fill):
        rows = [jnp.pad(given[nm][k].reshape(1, -1), ((0, 0), (0, D_MODEL - given[nm][k].size)),
                        constant_values=fill) for nm in names]
        return jnp.concatenate(rows + [jnp.full((SMALL_ROWS - len(names), D_MODEL), fill, F32)], axis=0)

    g_rows = jnp.concatenate(
        [jnp.pad(g_small[nm].reshape(1, -1), ((0, 0), (0, D_MODEL - g_small[nm].size))) for nm in names]
        + [jnp.zeros((SMALL_ROWS - len(names), D_MODEL), F32)], axis=0)
    d_s, m_s, v_s = _adamw(stack(0, 0.0), g_rows, stack(1, 0.0), stack(2, 1.0), "adamw_small")
    for r, nm in enumerate(names):
        shape = given[nm][0].shape
        n = given[nm][0].size
        res[nm] = (g_small[nm].reshape(shape), d_s[r, :n].reshape(shape), m_s[r, :n].reshape(shape),
                   v_s[r, :n].reshape(shape))

    order = ["norm1_w", "w_in", "dn_conv_w", "dn_A_log", "dn_dt_bias", "dn_norm_w", "w_proj_dn", "w_proj_sb",
             "w_out", "norm2_w", "ffn_w_up", "ffn_conv_w", "ffn_w_down", "norm_f_w"]
    outs = [loss, grad_x[None]]
    for k in range(4):
        outs += [res[nm][k] for nm in order]
    return tuple(outs)
```

```python
import functools

import jax
import jax.numpy as jnp
from jax import lax
from jax.experimental import pallas as pl
from jax.experimental.pallas import tpu as pltpu

F32 = jnp.float32
BF16 = jnp.bfloat16
HIGHEST = lax.Precision.HIGHEST
MESH = pl.DeviceIdType.MESH

EPS = 1e-6
D_MODEL = 1024
N_HEADS = 8
HEAD_DIM = 128
DN_CONV = 4
DN_CHUNK = 64
D_FF = 2816
FFN_CONV = 3
ADAM_LR, ADAM_B1, ADAM_B2, ADAM_EPS, ADAM_WD, ADAM_STEP = 0.001, 0.9, 0.999, 1e-08, 0.01, 10

N_CHIPS = 4
LANES = 128
HALO = 8
VMEM_LIMIT = 48 * 1024 * 1024
SMALL_ROWS = 8


def _params(sem=None):
    return pltpu.CompilerParams(dimension_semantics=sem, vmem_limit_bytes=VMEM_LIMIT)


def _pick(n, target):
    best = None
    for b in range(LANES, min(n, target) + 1, LANES):
        if n % b == 0:
            best = b
    return best or n


ELEMENTWISE_COLS = 1408


def _rows(t, target=256):
    return min(t, target)


def _dot(a, b, precision=None):
    return lax.dot_general(a, b, (((1,), (0,)), ((), ())), precision=precision, preferred_element_type=F32)


def _dot_nt(a, b, precision=None):
    return lax.dot_general(a, b, (((1,), (1,)), ((), ())), precision=precision, preferred_element_type=F32)


def _dot_tn(a, b, precision=None):
    return lax.dot_general(a, b, (((0,), (0,)), ((), ())), precision=precision, preferred_element_type=F32)


def _rms(x, w):
    return x * lax.rsqrt(jnp.mean(x * x, axis=-1, keepdims=True) + EPS) * w


def _silu(x):
    return x * jax.nn.sigmoid(x)


def _softplus(x):
    return jnp.maximum(x, 0.0) + jnp.log(1.0 + jnp.exp(-jnp.abs(x)))


MM_BLOCK = 1408


def _mm(a, b, *, ta=False, tb=False, add=None, out_dtype=F32, name, bm=MM_BLOCK, bn=MM_BLOCK, bk=MM_BLOCK):
    m = a.shape[1] if ta else a.shape[0]
    k = a.shape[0] if ta else a.shape[1]
    n = b.shape[0] if tb else b.shape[1]
    bm, bn, bk = _pick(m, bm), _pick(n, bn), _pick(k, bk)
    nk = k // bk
    dims = (((0 if ta else 1,), (1 if tb else 0,)), ((), ()))

    def body(*refs):
        a_ref, b_ref = refs[:2]
        c_ref = refs[2] if add is not None else None
        o_ref = refs[3] if add is not None else refs[2]
        acc = refs[-1]
        kk = pl.program_id(2)
        part = lax.dot_general(a_ref[...].astype(BF16), b_ref[...].astype(BF16), dims, preferred_element_type=F32)

        def finish(r):
            if add is not None:
                r = r + c_ref[...].astype(F32)
            o_ref[...] = r.astype(out_dtype)

        if nk == 1:
            finish(part)
            return

        @pl.when(kk == 0)
        def _():
            acc[...] = part

        @pl.when(jnp.logical_and(kk > 0, kk < nk - 1))
        def _():
            acc[...] += part

        @pl.when(kk == nk - 1)
        def _():
            finish(acc[...] + part)

    a_spec = (pl.BlockSpec((bk, bm), lambda i, j, kk: (kk, i)) if ta
              else pl.BlockSpec((bm, bk), lambda i, j, kk: (i, kk)))
    b_spec = (pl.BlockSpec((bn, bk), lambda i, j, kk: (j, kk)) if tb
              else pl.BlockSpec((bk, bn), lambda i, j, kk: (kk, j)))
    o_spec = pl.BlockSpec((bm, bn), lambda i, j, kk: (i, j))
    in_specs = [a_spec, b_spec] + ([o_spec] if add is not None else [])
    args = (a, b) + ((add,) if add is not None else ())
    return pl.pallas_call(
        body, name=name, grid=(m // bm, n // bn, nk),
        in_specs=in_specs, out_specs=o_spec,
        out_shape=jax.ShapeDtypeStruct((m, n), out_dtype),
        scratch_shapes=[pltpu.VMEM((bm, bn), F32)] if nk > 1 else [],
        compiler_params=_params(("parallel", "parallel", "arbitrary")),
    )(*args)


def _norm1_fwd(x, w, w_ab):
    t = x.shape[0]
    tb = _rows(t)

    def body(x_ref, w_ref, wab_ref, n_ref, hab_ref):
        n = _rms(x_ref[...], w_ref[...]).astype(BF16)
        n_ref[...] = n
        hab_ref[...] = _dot(n, wab_ref[...])

    return pl.pallas_call(
        body, name="norm1_fwd", grid=(t // tb,),
        in_specs=[pl.BlockSpec((tb, D_MODEL), lambda i: (i, 0)),
                  pl.BlockSpec((1, D_MODEL), lambda i: (0, 0)),
                  pl.BlockSpec((D_MODEL, LANES), lambda i: (0, 0))],
        out_specs=[pl.BlockSpec((tb, D_MODEL), lambda i: (i, 0)),
                   pl.BlockSpec((tb, LANES), lambda i: (i, 0))],
        out_shape=[jax.ShapeDtypeStruct((t, D_MODEL), BF16), jax.ShapeDtypeStruct((t, LANES), F32)],
        compiler_params=_params(("arbitrary",)),
    )(x, w, w_ab)


def _norm1_bwd(x, w, dn, dres, dab, w_ab):
    t = x.shape[0]
    tb = _rows(t)

    def body(x_ref, w_ref, dn_ref, dres_ref, dab_ref, wab_ref, dx_ref, dw_ref):
        i = pl.program_id(0)
        g = dn_ref[...] + _dot_nt(dab_ref[...].astype(BF16), wab_ref[...])
        _, vjp = jax.vjp(_rms, x_ref[...], w_ref[...])
        dx, dw = vjp(g)
        dx_ref[...] = dres_ref[...] + dx

        @pl.when(i == 0)
        def _():
            dw_ref[...] = jnp.zeros_like(dw_ref)

        dw_ref[...] += dw

    row = pl.BlockSpec((tb, D_MODEL), lambda i: (i, 0))
    vec = pl.BlockSpec((1, D_MODEL), lambda i: (0, 0))
    return pl.pallas_call(
        body, name="norm1_bwd", grid=(t // tb,),
        in_specs=[row, vec, row, row, pl.BlockSpec((tb, LANES), lambda i: (i, 0)),
                  pl.BlockSpec((D_MODEL, LANES), lambda i: (0, 0))],
        out_specs=[row, vec],
        out_shape=[jax.ShapeDtypeStruct((t, D_MODEL), F32), jax.ShapeDtypeStruct((1, D_MODEL), F32)],
        compiler_params=_params(("arbitrary",)),
    )(x, w, dn, dres, dab, w_ab)


def _conv_fwd(x, w, name):
    t, c = x.shape
    kk = w.shape[0]
    tb, cb = _rows(t, 512), _pick(c, ELEMENTWISE_COLS)
    per = tb // HALO

    def body(x_ref, halo_ref, w_ref, y_ref, buf):
        i = pl.program_id(0)
        buf[pl.ds(HALO, tb), :] = x_ref[...]
        buf[pl.ds(0, HALO), :] = jnp.where(i == 0, 0.0, halo_ref[...])
        y = w_ref[0:1, :] * buf[pl.ds(HALO - (kk - 1), tb), :]
        for s in range(1, kk):
            y = y + w_ref[s:s + 1, :] * buf[pl.ds(HALO - (kk - 1) + s, tb), :]
        y_ref[...] = y

    return pl.pallas_call(
        body, name=name, grid=(t // tb, c // cb),
        in_specs=[pl.BlockSpec((tb, cb), lambda i, j: (i, j)),
                  pl.BlockSpec((HALO, cb), lambda i, j: (jnp.maximum(i * per - 1, 0), j)),
                  pl.BlockSpec((kk, cb), lambda i, j: (0, j))],
        out_specs=pl.BlockSpec((tb, cb), lambda i, j: (i, j)),
        out_shape=jax.ShapeDtypeStruct((t, c), F32),
        scratch_shapes=[pltpu.VMEM((tb + HALO, cb), F32)],
        compiler_params=_params(("parallel", "parallel")),
    )(x, x, w)


def _conv_bwd(dy, x, w, name, dx_dtype):
    t, c = x.shape
    kk = w.shape[0]
    tb, cb = _rows(t, 512), _pick(c, ELEMENTWISE_COLS)
    per = tb // HALO
    nblk = t // tb

    def body(dy_ref, after_ref, x_ref, w_ref, dx_ref, dw_ref, dbuf):
        i = pl.program_id(1)
        dbuf[pl.ds(0, tb), :] = dy_ref[...]
        dbuf[pl.ds(tb, HALO), :] = jnp.where(i == nblk - 1, 0.0, after_ref[...])

        @pl.when(i == 0)
        def _():
            dw_ref[...] = jnp.zeros_like(dw_ref)

        x = x_ref[...]
        dx = None
        for s in range(kk):
            shifted = dbuf[pl.ds(kk - 1 - s, tb), :]
            term = w_ref[s:s + 1, :] * shifted
            dx = term if dx is None else dx + term
            dw_ref[s:s + 1, :] += jnp.sum(shifted * x, axis=0, keepdims=True)
        dx_ref[...] = dx.astype(dx_dtype)

    blk = pl.BlockSpec((tb, cb), lambda j, i: (i, j))
    return pl.pallas_call(
        body, name=name, grid=(c // cb, nblk),
        in_specs=[blk,
                  pl.BlockSpec((HALO, cb), lambda j, i: (jnp.minimum((i + 1) * per, t // HALO - 1), j)),
                  blk,
                  pl.BlockSpec((kk, cb), lambda j, i: (0, j))],
        out_specs=[blk, pl.BlockSpec((HALO, cb), lambda j, i: (0, j))],
        out_shape=[jax.ShapeDtypeStruct((t, c), dx_dtype), jax.ShapeDtypeStruct((HALO, c), F32)],
        scratch_shapes=[pltpu.VMEM((tb + HALO, cb), F32)],
        compiler_params=_params(("parallel", "arbitrary")),
    )(dy, dy, x, w)


def _dn_prep_fn(c, hab, alog, dtb):
    s = _silu(c)
    heads = []
    for h in range(2 * N_HEADS):
        sh = s[:, h * HEAD_DIM:(h + 1) * HEAD_DIM]
        heads.append(sh * lax.rsqrt(jnp.sum(sh * sh, axis=-1, keepdims=True) + EPS))
    qn = jnp.concatenate(heads[:N_HEADS], axis=1)
    kn = jnp.concatenate(heads[N_HEADS:], axis=1)
    v = s[:, 2 * D_MODEL:]
    lane = lax.broadcasted_iota(jnp.int32, hab.shape, 1)
    g = -jnp.exp(alog) * _softplus(hab + dtb)
    beta = jax.nn.sigmoid(hab)
    gb = jnp.where(lane < N_HEADS, g, jnp.where(lane < 2 * N_HEADS, beta, 0.0))
    return qn, kn, v, gb


def _to_heads(ref, val):
    for h in range(N_HEADS):
        ref[h] = val[:, h * HEAD_DIM:(h + 1) * HEAD_DIM]


def _from_heads(ref):
    return jnp.concatenate([ref[h] for h in range(N_HEADS)], axis=1)


def _dn_prep_fwd(c, hab, alog, dtb):
    t = c.shape[0]
    tb = _rows(t)

    def body(c_ref, hab_ref, alog_ref, dtb_ref, q_ref, k_ref, v_ref, gb_ref):
        qn, kn, v, gb = _dn_prep_fn(c_ref[...], hab_ref[...], alog_ref[...], dtb_ref[...])
        _to_heads(q_ref, qn)
        _to_heads(k_ref, kn)
        _to_heads(v_ref, v)
        gb_ref[...] = gb

    hm = pl.BlockSpec((N_HEADS, tb, HEAD_DIM), lambda i: (0, i, 0))
    nar = pl.BlockSpec((tb, LANES), lambda i: (i, 0))
    vec = pl.BlockSpec((1, LANES), lambda i: (0, 0))
    return pl.pallas_call(
        body, name="dn_prep_fwd", grid=(t // tb,),
        in_specs=[pl.BlockSpec((tb, 3 * D_MODEL), lambda i: (i, 0)), nar, vec, vec],
        out_specs=[hm, hm, hm, nar],
        out_shape=[jax.ShapeDtypeStruct((N_HEADS, t, HEAD_DIM), F32)] * 3 + [jax.ShapeDtypeStruct((t, LANES), F32)],
        compiler_params=_params(("parallel",)),
    )(c, hab, alog, dtb)


def _dn_prep_bwd(c, hab, alog, dtb, dq, dk, dv, dgb):
    t = c.shape[0]
    tb = _rows(t)

    def body(c_ref, hab_ref, alog_ref, dtb_ref, dq_ref, dk_ref, dv_ref, dgb_ref,
             dc_ref, dhab_ref, dalog_ref, ddtb_ref):
        i = pl.program_id(0)
        _, vjp = jax.vjp(_dn_prep_fn, c_ref[...], hab_ref[...], alog_ref[...], dtb_ref[...])
        dc, dhab, dalog, ddtb = vjp((_from_heads(dq_ref), _from_heads(dk_ref), _from_heads(dv_ref), dgb_ref[...]))
        dc_ref[...] = dc
        dhab_ref[...] = dhab

        @pl.when(i == 0)
        def _():
            dalog_ref[...] = jnp.zeros_like(dalog_ref)
            ddtb_ref[...] = jnp.zeros_like(ddtb_ref)

        dalog_ref[...] += dalog
        ddtb_ref[...] += ddtb

    hm = pl.BlockSpec((N_HEADS, tb, HEAD_DIM), lambda i: (0, i, 0))
    wide = pl.BlockSpec((tb, 3 * D_MODEL), lambda i: (i, 0))
    nar = pl.BlockSpec((tb, LANES), lambda i: (i, 0))
    vec = pl.BlockSpec((1, LANES), lambda i: (0, 0))
    return pl.pallas_call(
        body, name="dn_prep_bwd", grid=(t // tb,),
        in_specs=[wide, nar, vec, vec, hm, hm, hm, nar],
        out_specs=[wide, nar, vec, vec],
        out_shape=[jax.ShapeDtypeStruct((t, 3 * D_MODEL), F32), jax.ShapeDtypeStruct((t, LANES), F32),
                   jax.ShapeDtypeStruct((1, LANES), F32), jax.ShapeDtypeStruct((1, LANES), F32)],
        compiler_params=_params(("arbitrary",)),
    )(c, hab, alog, dtb, dq, dk, dv, dgb)


DN_PREC = lax.Precision.HIGH
DN_GROUP = 8


def _dn_prec(a):
    return DN_PREC if a.dtype == F32 else None


def _bdot(a, b):
    return lax.dot_general(a, b, (((2,), (1,)), ((0,), (0,))), precision=_dn_prec(a), preferred_element_type=F32)


def _bdot_nt(a, b):
    return lax.dot_general(a, b, (((2,), (2,)), ((0,), (0,))), precision=_dn_prec(a), preferred_element_type=F32)


def _bdot_tn(a, b):
    return lax.dot_general(a, b, (((1,), (1,)), ((0,), (0,))), precision=_dn_prec(a), preferred_element_type=F32)


def _unit_lower_inverse(lmat):
    c = lmat.shape[-1]
    ri = lax.broadcasted_iota(jnp.int32, (c, c), 0)
    ci = lax.broadcasted_iota(jnp.int32, (c, c), 1)
    p = -lmat
    tinv = jnp.where(ri == ci, 1.0, 0.0) + p
    for _ in range(max(c.bit_length() - 2, 0)):
        p = _bdot(p, p)
        tinv = tinv + _bdot(tinv, p)
    return tinv


@jax.custom_vjp
def _solve_with(lmat, rhs, tinv):
    return _bdot(tinv, rhs)


def _solve_with_fwd(lmat, rhs, tinv):
    sol = _bdot(tinv, rhs)
    return sol, (sol, tinv)


def _solve_with_bwd(res, dsol):
    sol, tinv = res
    drhs = _bdot_tn(tinv, dsol)
    return -_bdot_nt(drhs, sol), drhs, jnp.zeros_like(tinv)


_solve_with.defvjp(_solve_with_fwd, _solve_with_bwd)


def _dn_local(q, k, v, grow, brow, tinv):
    g, c, _ = q.shape
    ri = lax.broadcasted_iota(jnp.int32, (c, c), 0)
    ci = lax.broadcasted_iota(jnp.int32, (c, c), 1)
    lower = ri >= ci
    as_col = lambda r: jnp.sum(jnp.where(ri == ci, jnp.broadcast_to(r, (g, c, c)), 0.0), axis=2, keepdims=True)
    gcol, bcol = as_col(grow), as_col(brow)
    gc_col = jnp.sum(jnp.where(lower, jnp.broadcast_to(grow, (g, c, c)), 0.0), axis=2, keepdims=True)
    gc_row = jnp.sum(jnp.where(ri <= ci, jnp.broadcast_to(gcol, (g, c, c)), 0.0), axis=1, keepdims=True)
    qs = q * (HEAD_DIM ** -0.5)
    kb = k * bcol
    vb = v * bcol
    decay = jnp.where(lower, jnp.exp(jnp.where(lower, gc_col - gc_row, 0.0)), 0.0)
    lmat = jnp.where(ri > ci, _bdot_nt(kb.astype(BF16), k.astype(BF16)) * decay, 0.0)
    eg = jnp.exp(gc_col)
    rhs = jnp.concatenate([vb, kb * eg], axis=2)
    if tinv is None:
        tinv = _unit_lower_inverse(lmat)
    sol = _solve_with(lmat, rhs, tinv)
    a_qk = jnp.where(lower, _bdot_nt(qs.astype(BF16), k.astype(BF16)) * decay, 0.0)
    g_last = jnp.sum(grow, axis=2, keepdims=True)
    kdec = k * jnp.exp(g_last - gc_col)
    egl = jnp.broadcast_to(jnp.exp(g_last), (g, 1, HEAD_DIM))
    return sol[:, :, :HEAD_DIM], sol[:, :, HEAD_DIM:], a_qk, qs * eg, kdec, egl, tinv


def _dn_seq(u, w, a_qk, qe, kdec, egl, s_in):
    b16 = lambda x: x.astype(BF16)
    v_new = u - _bdot(b16(w), b16(s_in))
    o = _bdot(b16(qe), b16(s_in)) + _bdot(b16(a_qk), b16(v_new))
    return o, s_in * egl + _bdot_tn(b16(kdec), b16(v_new))


def _dn_local_specs(t):
    grp = min(DN_GROUP, t // DN_CHUNK)
    rows = grp * DN_CHUNK
    blk = pl.BlockSpec((1, rows, HEAD_DIM), lambda h, i: (h, i, 0))
    row = pl.BlockSpec((1, grp, 1, DN_CHUNK), lambda h, i: (h, i, 0, 0))
    sq = pl.BlockSpec((1, grp, DN_CHUNK, DN_CHUNK), lambda h, i: (h, i, 0, 0))
    lane = pl.BlockSpec((1, grp, 1, HEAD_DIM), lambda h, i: (h, i, 0, 0))
    return grp, blk, row, sq, lane


def _dn_shapes(t):
    nchunk = t // DN_CHUNK
    big = jax.ShapeDtypeStruct((N_HEADS, t, HEAD_DIM), F32)
    row = jax.ShapeDtypeStruct((N_HEADS, nchunk, 1, DN_CHUNK), F32)
    sq = jax.ShapeDtypeStruct((N_HEADS, nchunk, DN_CHUNK, DN_CHUNK), F32)
    lane = jax.ShapeDtypeStruct((N_HEADS, nchunk, 1, HEAD_DIM), F32)
    return big, row, sq, lane


def _dn_local_fwd(q, k, v, grow, brow, wire=()):
    t = q.shape[1]
    grp, blk, row, sq, lane = _dn_local_specs(t)
    big, _, sqs, lanes = _dn_shapes(t)
    n = len(wire)
    groups = t // (grp * DN_CHUNK)
    steps = N_HEADS * groups

    def body(q_ref, k_ref, v_ref, gr_ref, br_ref, *rest):
        u_ref, w_ref, a_ref, qe_ref, kd_ref, egl_ref, t_ref = rest[n:n + 7]
        if n:
            begin, middle, end = _gather_protocol(rest[:n], rest[n + 7:2 * n + 7], *rest[2 * n + 7:])
            step = pl.program_id(0) * groups + pl.program_id(1)
            pl.when(step == 0)(begin)
            pl.when(step == (5 * steps) // 8)(middle)
        split = lambda r: r[0].reshape(grp, DN_CHUNK, HEAD_DIM)
        u, w, a_qk, qe, kdec, egl, tinv = _dn_local(split(q_ref), split(k_ref), split(v_ref), gr_ref[0],
                                                     br_ref[0], None)
        for ref, val in ((u_ref, u), (w_ref, w), (qe_ref, qe), (kd_ref, kdec)):
            ref[0] = val.reshape(grp * DN_CHUNK, HEAD_DIM)
        a_ref[0] = a_qk
        egl_ref[0] = egl
        t_ref[0] = tinv
        if n:
            pl.when(step == steps - 1)(end)

    assert n == 0 or steps >= 3
    return pl.pallas_call(
        body, name="dn_local_fwd", grid=(N_HEADS, groups),
        in_specs=[blk, blk, blk, row, row] + _hbm_specs(n),
        out_specs=[blk, blk, sq, blk, blk, lane, sq] + _hbm_specs(n),
        out_shape=[big, big, sqs, big, big, lanes, sqs] + _gather_out_shapes(wire),
        scratch_shapes=_gather_sems(n) if n else [],
        compiler_params=_params(("arbitrary", "arbitrary")),
    )(q, k, v, grow, brow, *wire)


def _dn_local_bwd(q, k, v, grow, brow, tinv, du, dw, da, dqe, dkd, degl, partials=()):
    t = q.shape[1]
    grp, blk, row, sq, lane = _dn_local_specs(t)
    big, rows_, _, _ = _dn_shapes(t)
    n = len(partials)
    groups = t // (grp * DN_CHUNK)
    steps = N_HEADS * groups

    def body(q_ref, k_ref, v_ref, gr_ref, br_ref, t_ref, du_ref, dw_ref, da_ref, dqe_ref, dkd_ref,
             degl_ref, *rest):
        dq_ref, dk_ref, dv_ref, dgr_ref, dbr_ref = rest[n:n + 5]
        if n:
            begin, end = _chip_exchange_protocol(rest[:n], rest[n + 5:2 * n + 5], *rest[2 * n + 5:])
            step = pl.program_id(0) * groups + pl.program_id(1)
            pl.when(step == 0)(begin)
        split = lambda r: r[0].reshape(grp, DN_CHUNK, HEAD_DIM)
        tinv_v = t_ref[0]
        fn = lambda q_, k_, v_, gr_, br_: _dn_local(q_, k_, v_, gr_, br_, tinv_v)[:6]
        _, vjp = jax.vjp(fn, split(q_ref), split(k_ref), split(v_ref), gr_ref[0], br_ref[0])
        dq, dk, dv, dgr, dbr = vjp((split(du_ref), split(dw_ref), da_ref[0], split(dqe_ref), split(dkd_ref),
                                    degl_ref[0]))
        for ref, val in ((dq_ref, dq), (dk_ref, dk), (dv_ref, dv)):
            ref[0] = val.reshape(grp * DN_CHUNK, HEAD_DIM)
        dgr_ref[0] = dgr
        dbr_ref[0] = dbr
        if n:
            pl.when(step == steps - 1)(end)

    assert n == 0 or steps >= 2
    return pl.pallas_call(
        body, name="dn_local_bwd", grid=(N_HEADS, groups),
        in_specs=[blk, blk, blk, row, row, sq, blk, blk, sq, blk, blk, lane] + _hbm_specs(n),
        out_specs=[blk, blk, blk, row, row] + _hbm_specs(n),
        out_shape=[big, big, big, rows_, rows_] + _chip_exchange_shapes(partials),
        scratch_shapes=_chip_exchange_sems(n) if n else [],
        compiler_params=_params(("arbitrary", "arbitrary")),
    )(q, k, v, grow, brow, tinv, du, dw, da, dqe, dkd, degl, *partials)


def _dn_seq_specs(nchunk, rev):
    def idx(n):
        return nchunk - 1 - n if rev else n

    blk = pl.BlockSpec((N_HEADS, DN_CHUNK, HEAD_DIM), lambda n: (0, idx(n), 0))
    sq = pl.BlockSpec((N_HEADS, 1, DN_CHUNK, DN_CHUNK), lambda n: (0, idx(n), 0, 0))
    lane = pl.BlockSpec((N_HEADS, 1, 1, HEAD_DIM), lambda n: (0, idx(n), 0, 0))
    st = pl.BlockSpec((N_HEADS, 1, HEAD_DIM, HEAD_DIM), lambda n: (0, idx(n), 0, 0))
    return blk, sq, lane, st


def _dn_seq_fwd(u, w, a_qk, qe, kdec, egl):
    t = u.shape[1]
    nchunk = t // DN_CHUNK
    blk, sq, lane, st = _dn_seq_specs(nchunk, False)

    def body(u_ref, w_ref, a_ref, qe_ref, kd_ref, egl_ref, o_ref, s_ref, state):
        @pl.when(pl.program_id(0) == 0)
        def _():
            state[...] = jnp.zeros_like(state)

        s_in = state[...]
        s_ref[:, 0] = s_in
        o, s_out = _dn_seq(u_ref[...], w_ref[...], a_ref[:, 0], qe_ref[...], kd_ref[...], egl_ref[:, 0], s_in)
        o_ref[...] = o
        state[...] = s_out

    return pl.pallas_call(
        body, name="dn_seq_fwd", grid=(nchunk,),
        in_specs=[blk, blk, sq, blk, blk, lane],
        out_specs=[blk, st],
        out_shape=[jax.ShapeDtypeStruct((N_HEADS, t, HEAD_DIM), F32),
                   jax.ShapeDtypeStruct((N_HEADS, nchunk, HEAD_DIM, HEAD_DIM), F32)],
        scratch_shapes=[pltpu.VMEM((N_HEADS, HEAD_DIM, HEAD_DIM), F32)],
        compiler_params=_params(("arbitrary",)),
    )(u, w, a_qk, qe, kdec, egl)


def _dn_seq_bwd(u, w, a_qk, qe, kdec, egl, states, do):
    t = u.shape[1]
    nchunk = t // DN_CHUNK
    blk, sq, lane, st = _dn_seq_specs(nchunk, True)
    big, _, sqs, lanes = _dn_shapes(t)

    def body(u_ref, w_ref, a_ref, qe_ref, kd_ref, egl_ref, s_ref, do_ref,
             du_ref, dw_ref, da_ref, dqe_ref, dkd_ref, degl_ref, dstate):
        @pl.when(pl.program_id(0) == 0)
        def _():
            dstate[...] = jnp.zeros_like(dstate)

        _, vjp = jax.vjp(_dn_seq, u_ref[...], w_ref[...], a_ref[:, 0], qe_ref[...], kd_ref[...], egl_ref[:, 0],
                         s_ref[:, 0])
        du, dw, da, dqe, dkd, degl, ds = vjp((do_ref[...], dstate[...]))
        du_ref[...] = du
        dw_ref[...] = dw
        da_ref[:, 0] = da
        dqe_ref[...] = dqe
        dkd_ref[...] = dkd
        degl_ref[:, 0] = degl
        dstate[...] = ds

    return pl.pallas_call(
        body, name="dn_seq_bwd", grid=(nchunk,),
        in_specs=[blk, blk, sq, blk, blk, lane, st, blk],
        out_specs=[blk, blk, sq, blk, blk, lane],
        out_shape=[big, big, sqs, big, big, lanes],
        scratch_shapes=[pltpu.VMEM((N_HEADS, HEAD_DIM, HEAD_DIM), F32)],
        compiler_params=_params(("arbitrary",)),
    )(u, w, a_qk, qe, kdec, egl, states, do)


def _dn_post_fn(o, gate, w):
    outs = []
    for h in range(N_HEADS):
        sl = slice(h * HEAD_DIM, (h + 1) * HEAD_DIM)
        outs.append(_rms(o[:, sl], w) * _silu(gate[:, sl]))
    return jnp.concatenate(outs, axis=1)


def _dn_post_fwd(o, gate, w):
    t = gate.shape[0]
    tb = _rows(t)

    def body(o_ref, g_ref, w_ref, y_ref):
        y_ref[...] = _dn_post_fn(_from_heads(o_ref), g_ref[...], w_ref[...]).astype(BF16)

    row = pl.BlockSpec((tb, D_MODEL), lambda i: (i, 0))
    hm = pl.BlockSpec((N_HEADS, tb, HEAD_DIM), lambda i: (0, i, 0))
    return pl.pallas_call(
        body, name="dn_post_fwd", grid=(t // tb,),
        in_specs=[hm, row, pl.BlockSpec((1, HEAD_DIM), lambda i: (0, 0))],
        out_specs=row, out_shape=jax.ShapeDtypeStruct((t, D_MODEL), BF16),
        compiler_params=_params(("parallel",)),
    )(o, gate, w)


def _dn_post_bwd(o, gate, w, dy):
    t = gate.shape[0]
    tb = _rows(t)

    def body(o_ref, g_ref, w_ref, dy_ref, do_ref, dg_ref, dw_ref):
        i = pl.program_id(0)
        _, vjp = jax.vjp(_dn_post_fn, _from_heads(o_ref), g_ref[...], w_ref[...])
        do, dg, dw = vjp(dy_ref[...])
        _to_heads(do_ref, do)
        dg_ref[...] = dg.astype(BF16)

        @pl.when(i == 0)
        def _():
            dw_ref[...] = jnp.zeros_like(dw_ref)

        dw_ref[...] += dw

    row = pl.BlockSpec((tb, D_MODEL), lambda i: (i, 0))
    hm = pl.BlockSpec((N_HEADS, tb, HEAD_DIM), lambda i: (0, i, 0))
    vec = pl.BlockSpec((1, HEAD_DIM), lambda i: (0, 0))
    return pl.pallas_call(
        body, name="dn_post_bwd", grid=(t // tb,),
        in_specs=[hm, row, vec, row],
        out_specs=[hm, row, vec],
        out_shape=[jax.ShapeDtypeStruct((N_HEADS, t, HEAD_DIM), F32), jax.ShapeDtypeStruct((t, D_MODEL), BF16),
                   jax.ShapeDtypeStruct((1, HEAD_DIM), F32)],
        compiler_params=_params(("arbitrary",)),
    )(o, gate, w, dy)


def _split_bf16(x):
    hi = x.astype(BF16)
    lo = (x - hi.astype(F32)).astype(BF16)
    return hi, lo


SB_Q_BLOCK = 512
SB_K_BLOCK = 256
SB_NEGLIGIBLE = -60.0


def _sb_logits(q, kb, mask, scale):
    z = _dot_nt(q, kb) * scale
    ls = jnp.minimum(z, 0.0) - jnp.log(1.0 + jnp.exp(-jnp.abs(z)))
    lk = ls - z
    if mask is not None:
        lk = jnp.where(mask, lk, 0.0)
    return ls, lk


def _sb_blocks(t):
    bq = min(SB_Q_BLOCK, t)
    bk = min(SB_K_BLOCK, bq)
    return bq, bk, bq // bk


def _sb_fwd(qkv):
    t = qkv.shape[0]
    bq, bk, nd = _sb_blocks(t)
    scale = HEAD_DIM ** -0.5

    def body(q_ref, k_ref, v_ref, o_ref, tot_ref, used_ref):
        i = pl.program_id(1)
        q = q_ref[...]
        rj = lax.broadcasted_iota(jnp.int32, (bk, bk), 0)
        cj = lax.broadcasted_iota(jnp.int32, (bk, bk), 1)
        after = (rj > cj).astype(BF16)
        trow = lax.broadcasted_iota(jnp.int32, (bq, bk), 0)
        scol = lax.broadcasted_iota(jnp.int32, (bq, bk), 1)

        def tile(j, run, acc, mask):
            off = pl.multiple_of(j * bk, bk)
            kb = k_ref[pl.ds(off, bk), :]
            vb = v_ref[pl.ds(off, bk), :]
            ls, lk = _sb_logits(q, kb, mask, scale)
            hi, lo = _split_bf16(lk)
            between = _dot(hi, after) + _dot(lo, after) + run
            a = jnp.exp(ls + between)
            if mask is not None:
                a = jnp.where(mask, a, 0.0)
            acc = acc + _dot(a.astype(BF16), vb)
            return run + jnp.sum(lk, axis=1, keepdims=True), acc

        run, acc = jnp.zeros((bq, 1), F32), jnp.zeros((bq, HEAD_DIM), F32)
        for d in reversed(range(nd)):
            run, acc = tile(i * nd + d, run, acc, scol + d * bk < trow)
        def more(c):
            return jnp.logical_and(c[0] < i * nd, jnp.max(c[1]) > SB_NEGLIGIBLE)

        def far(c):
            run_, acc_ = tile(i * nd - 1 - c[0], c[1], c[2], None)
            return c[0] + 1, run_, acc_

        used, run, acc = lax.while_loop(more, far, (jnp.int32(0), run, acc))
        o_ref[...] = acc.astype(BF16)
        tot_ref[...] = jnp.broadcast_to(run, (bq, HEAD_DIM))
        used_ref[...] = jnp.full(used_ref.shape, used, F32)

    qs = pl.BlockSpec((bq, HEAD_DIM), lambda h, i: (i, h))
    ks = pl.BlockSpec((t, HEAD_DIM), lambda h, i: (0, N_HEADS + h))
    vs = pl.BlockSpec((t, HEAD_DIM), lambda h, i: (0, 2 * N_HEADS + h))
    return pl.pallas_call(
        body, name="sb_fwd", grid=(N_HEADS, t // bq),
        in_specs=[qs, ks, vs], out_specs=[qs, qs, pl.BlockSpec((1, 1, 1, LANES), lambda h, i: (h, i, 0, 0))],
        out_shape=[jax.ShapeDtypeStruct((t, D_MODEL), BF16), jax.ShapeDtypeStruct((t, D_MODEL), F32),
                   jax.ShapeDtypeStruct((N_HEADS, t // bq, 1, LANES), F32)],
        compiler_params=_params(("parallel", "arbitrary")),
    )(qkv, qkv, qkv)


def _sb_bwd(qkv, tot, used, do):
    t = qkv.shape[0]
    bq, bk, nd = _sb_blocks(t)
    scale = HEAD_DIM ** -0.5

    def body(q_ref, k_ref, v_ref, tot_ref, used_ref, do_ref, dq_ref, dk_ref, dv_ref):
        i = pl.program_id(1)

        @pl.when(i == 0)
        def _():
            dk_ref[...] = jnp.zeros_like(dk_ref)
            dv_ref[...] = jnp.zeros_like(dv_ref)

        q = q_ref[...]
        do = do_ref[...]
        total = tot_ref[:, 0:1]
        rj = lax.broadcasted_iota(jnp.int32, (bk, bk), 0)
        cj = lax.broadcasted_iota(jnp.int32, (bk, bk), 1)
        upto = (rj <= cj).astype(BF16)
        before = (rj < cj).astype(BF16)
        trow = lax.broadcasted_iota(jnp.int32, (bq, bk), 0)
        scol = lax.broadcasted_iota(jnp.int32, (bq, bk), 1)

        def tile(j, run_k, run_e, dq, mask):
            off = pl.multiple_of(j * bk, bk)
            kb = k_ref[pl.ds(off, bk), :]
            vb = v_ref[pl.ds(off, bk), :]
            ls, lk = _sb_logits(q, kb, mask, scale)
            hi, lo = _split_bf16(lk)
            between = total - (_dot(hi, upto) + _dot(lo, upto) + run_k)
            a = jnp.exp(ls + between)
            if mask is not None:
                a = jnp.where(mask, a, 0.0)
            e = a * _dot_nt(do, vb)
            ehi, elo = _split_bf16(e)
            pre = _dot(ehi, before) + _dot(elo, before) + run_e
            sig = jnp.exp(ls)
            dz = e * (1.0 - sig) - pre * sig
            if mask is not None:
                dz = jnp.where(mask, dz, 0.0)
            dz = (dz * scale).astype(BF16)
            dq = dq + _dot(dz, kb)
            dk_ref[pl.ds(off, bk), :] += _dot_tn(dz, q)
            dv_ref[pl.ds(off, bk), :] += _dot_tn(a.astype(BF16), do)
            return (run_k + jnp.sum(lk, axis=1, keepdims=True),
                    run_e + jnp.sum(e, axis=1, keepdims=True), dq)

        zero = jnp.zeros((bq, 1), F32)
        visited = jnp.clip(jnp.max(used_ref[...]).astype(jnp.int32), 0, i * nd)
        carry = lax.fori_loop(i * nd - visited, i * nd, lambda j, c: tile(j, c[0], c[1], c[2], None),
                              (zero, zero, jnp.zeros((bq, HEAD_DIM), F32)))
        for d in range(nd):
            carry = tile(i * nd + d, *carry, scol + d * bk < trow)
        dq_ref[...] = carry[2]

    qs = pl.BlockSpec((bq, HEAD_DIM), lambda h, i: (i, h))
    ks = pl.BlockSpec((t, HEAD_DIM), lambda h, i: (0, N_HEADS + h))
    vs = pl.BlockSpec((t, HEAD_DIM), lambda h, i: (0, 2 * N_HEADS + h))
    full = pl.BlockSpec((t, HEAD_DIM), lambda h, i: (0, h))
    big = jax.ShapeDtypeStruct((t, D_MODEL), F32)
    return pl.pallas_call(
        body, name="sb_bwd", grid=(N_HEADS, t // bq),
        in_specs=[qs, ks, vs, qs, pl.BlockSpec((1, 1, 1, LANES), lambda h, i: (h, i, 0, 0)), qs],
        out_specs=[qs, full, full],
        out_shape=[big, big, big],
        compiler_params=_params(("parallel", "arbitrary")),
    )(qkv, qkv, qkv, tot, used, do)


def _merge_fwd(o_dn, o_sb, gl, x, wp_dn, wp_sb, w_out, w2):
    t = x.shape[0]
    tb = _rows(t)

    def body(odn_ref, osb_ref, gl_ref, x_ref, wpd_ref, wps_ref, wo_ref, w2_ref,
             pdn_ref, psb_ref, mix_ref, x1_ref, n2_ref):
        pdn = _dot(odn_ref[...], wpd_ref[...])
        psb = _dot(osb_ref[...], wps_ref[...])
        gates = jax.nn.sigmoid(gl_ref[...])
        mixed = (gates[:, :D_MODEL] * pdn + gates[:, D_MODEL:] * psb).astype(BF16)
        x1 = x_ref[...] + _dot(mixed, wo_ref[...])
        pdn_ref[...] = pdn
        psb_ref[...] = psb
        mix_ref[...] = mixed
        x1_ref[...] = x1
        n2_ref[...] = _rms(x1, w2_ref[...]).astype(BF16)

    row = pl.BlockSpec((tb, D_MODEL), lambda i: (i, 0))
    sq = pl.BlockSpec((D_MODEL, D_MODEL), lambda i: (0, 0))
    f = jax.ShapeDtypeStruct((t, D_MODEL), F32)
    b = jax.ShapeDtypeStruct((t, D_MODEL), BF16)
    return pl.pallas_call(
        body, name="merge_fwd", grid=(t // tb,),
        in_specs=[row, row, pl.BlockSpec((tb, 2 * D_MODEL), lambda i: (i, 0)), row, sq, sq, sq,
                  pl.BlockSpec((1, D_MODEL), lambda i: (0, 0))],
        out_specs=[row] * 5, out_shape=[f, f, b, f, b],
        compiler_params=_params(("parallel",)),
    )(o_dn, o_sb, gl, x, wp_dn, wp_sb, w_out, w2)


def _merge_bwd(dx2, dn2, x1, w2, gl, pdn, psb, wp_dn, wp_sb, w_out):
    t = x1.shape[0]
    tb = _rows(t)

    def body(dx2_ref, dn2_ref, x1_ref, w2_ref, gl_ref, pdn_ref, psb_ref, wpd_ref, wps_ref, wo_ref,
             dx1_ref, dw2_ref, dgl_ref, dpdn_ref, dpsb_ref, dodn_ref, dosb_ref):
        i = pl.program_id(0)
        _, vjp = jax.vjp(_rms, x1_ref[...], w2_ref[...])
        dxn, dw2 = vjp(dn2_ref[...])
        dx1 = dx2_ref[...] + dxn
        dx1_ref[...] = dx1

        @pl.when(i == 0)
        def _():
            dw2_ref[...] = jnp.zeros_like(dw2_ref)

        dw2_ref[...] += dw2
        dmix = _dot_nt(dx1.astype(BF16), wo_ref[...])
        gates = jax.nn.sigmoid(gl_ref[...])
        g_dn, g_sb = gates[:, :D_MODEL], gates[:, D_MODEL:]
        dpdn = (dmix * g_dn).astype(BF16)
        dpsb = (dmix * g_sb).astype(BF16)
        dgl_ref[:, :D_MODEL] = (dmix * pdn_ref[...] * g_dn * (1.0 - g_dn)).astype(BF16)
        dgl_ref[:, D_MODEL:] = (dmix * psb_ref[...] * g_sb * (1.0 - g_sb)).astype(BF16)
        dpdn_ref[...] = dpdn
        dpsb_ref[...] = dpsb
        dodn_ref[...] = _dot_nt(dpdn, wpd_ref[...])
        dosb_ref[...] = _dot_nt(dpsb, wps_ref[...]).astype(BF16)

    row = pl.BlockSpec((tb, D_MODEL), lambda i: (i, 0))
    wide = pl.BlockSpec((tb, 2 * D_MODEL), lambda i: (i, 0))
    sq = pl.BlockSpec((D_MODEL, D_MODEL), lambda i: (0, 0))
    vec = pl.BlockSpec((1, D_MODEL), lambda i: (0, 0))
    f = jax.ShapeDtypeStruct((t, D_MODEL), F32)
    b = jax.ShapeDtypeStruct((t, D_MODEL), BF16)
    return pl.pallas_call(
        body, name="merge_bwd", grid=(t // tb,),
        in_specs=[row, row, row, vec, wide, row, row, sq, sq, sq],
        out_specs=[row, vec, wide, row, row, row, row],
        out_shape=[f, jax.ShapeDtypeStruct((1, D_MODEL), F32), jax.ShapeDtypeStruct((t, 2 * D_MODEL), BF16),
                   b, b, f, b],
        compiler_params=_params(("arbitrary",)),
    )(dx2, dn2, x1, w2, gl, pdn, psb, wp_dn, wp_sb, w_out)


def _conv_taps(buf, w_ref, first, rows):
    y = w_ref[0:1, :] * buf[pl.ds(first, rows), :]
    for s in range(1, w_ref.shape[0]):
        y = y + w_ref[s:s + 1, :] * buf[pl.ds(first + s, rows), :]
    return y


def _ffn_mid_fwd(pre_g, pre_u, wg, wu):
    t, c = pre_g.shape
    kk = wg.shape[0]
    tb, cb = _rows(t), _pick(c, ELEMENTWISE_COLS)
    per = tb // HALO

    def body(g_ref, gh_ref, u_ref, uh_ref, wg_ref, wu_ref, a_ref, gbuf, ubuf):
        i = pl.program_id(0)
        for buf, ref, halo in ((gbuf, g_ref, gh_ref), (ubuf, u_ref, uh_ref)):
            buf[pl.ds(HALO, tb), :] = ref[...]
            buf[pl.ds(0, HALO), :] = jnp.where(i == 0, 0.0, halo[...])
        ug = _conv_taps(gbuf, wg_ref, HALO - (kk - 1), tb)
        uu = _conv_taps(ubuf, wu_ref, HALO - (kk - 1), tb)
        a_ref[...] = (_silu(ug) * uu).astype(BF16)

    blk = pl.BlockSpec((tb, cb), lambda i, j: (i, j))
    halo = pl.BlockSpec((HALO, cb), lambda i, j: (jnp.maximum(i * per - 1, 0), j))
    wspec = pl.BlockSpec((kk, cb), lambda i, j: (0, j))
    return pl.pallas_call(
        body, name="ffn_mid_fwd", grid=(t // tb, c // cb),
        in_specs=[blk, halo, blk, halo, wspec, wspec], out_specs=blk,
        out_shape=jax.ShapeDtypeStruct((t, c), BF16),
        scratch_shapes=[pltpu.VMEM((tb + HALO, cb), F32)] * 2,
        compiler_params=_params(("parallel", "parallel")),
    )(pre_g, pre_g, pre_u, pre_u, wg, wu)


def _ffn_mid_bwd(pre_g, pre_u, wg, wu, da):
    t, c = pre_g.shape
    kk = wg.shape[0]
    tb, cb = _rows(t), _pick(c, ELEMENTWISE_COLS)
    per = tb // HALO
    nblk = t // tb
    ext = tb + HALO

    def body(g_ref, gb_ref, ga_ref, u_ref, ub_ref, ua_ref, da_ref, daa_ref, wg_ref, wu_ref,
             dg_ref, du_ref, dwg_ref, dwu_ref, gbuf, ubuf, dabuf, dgbuf, dubuf):
        i = pl.program_id(1)
        last = i == nblk - 1
        for buf, ref, before, after in ((gbuf, g_ref, gb_ref, ga_ref), (ubuf, u_ref, ub_ref, ua_ref)):
            buf[pl.ds(0, HALO), :] = jnp.where(i == 0, 0.0, before[...])
            buf[pl.ds(HALO, tb), :] = ref[...]
            buf[pl.ds(HALO + tb, HALO), :] = jnp.where(last, 0.0, after[...])
        dabuf[pl.ds(0, tb), :] = da_ref[...]
        dabuf[pl.ds(tb, HALO), :] = jnp.where(last, 0.0, daa_ref[...])
        ug = _conv_taps(gbuf, wg_ref, HALO - (kk - 1), ext)
        uu = _conv_taps(ubuf, wu_ref, HALO - (kk - 1), ext)
        _, vjp = jax.vjp(lambda g, u: _silu(g) * u, ug, uu)
        dgbuf[...], dubuf[...] = vjp(dabuf[...])

        @pl.when(i == 0)
        def _():
            dwg_ref[...] = jnp.zeros_like(dwg_ref)
            dwu_ref[...] = jnp.zeros_like(dwu_ref)

        for dbuf, xbuf, w_ref, dx_ref, dw_ref in ((dgbuf, gbuf, wg_ref, dg_ref, dwg_ref),
                                                  (dubuf, ubuf, wu_ref, du_ref, dwu_ref)):
            x = xbuf[pl.ds(HALO, tb), :]
            dx = None
            for s in range(kk):
                shifted = dbuf[pl.ds(kk - 1 - s, tb), :]
                term = w_ref[s:s + 1, :] * shifted
                dx = term if dx is None else dx + term
                dw_ref[s:s + 1, :] += jnp.sum(shifted * x, axis=0, keepdims=True)
            dx_ref[...] = dx.astype(BF16)

    blk = pl.BlockSpec((tb, cb), lambda j, i: (i, j))
    before = pl.BlockSpec((HALO, cb), lambda j, i: (jnp.maximum(i * per - 1, 0), j))
    after = pl.BlockSpec((HALO, cb), lambda j, i: (jnp.minimum((i + 1) * per, t // HALO - 1), j))
    wspec = pl.BlockSpec((kk, cb), lambda j, i: (0, j))
    dwspec = pl.BlockSpec((HALO, cb), lambda j, i: (0, j))
    half = jax.ShapeDtypeStruct((t, c), BF16)
    dwshape = jax.ShapeDtypeStruct((HALO, c), F32)
    return pl.pallas_call(
        body, name="ffn_mid_bwd", grid=(c // cb, nblk),
        in_specs=[blk, before, after, blk, before, after, blk, after, wspec, wspec],
        out_specs=[blk, blk, dwspec, dwspec],
        out_shape=[half, half, dwshape, dwshape],
        scratch_shapes=[pltpu.VMEM((ext + HALO, cb), F32)] * 2 + [pltpu.VMEM((ext, cb), F32)] * 3,
        compiler_params=_params(("parallel", "arbitrary")),
    )(pre_g, pre_g, pre_g, pre_u, pre_u, pre_u, da, da, wg, wu)


def _down_loss(a, w_down, x1, wf, target):
    t = x1.shape[0]
    tb = _rows(t)

    def body(a_ref, wd_ref, x1_ref, wf_ref, tgt_ref, dx2_ref, dwf_ref, loss_ref):
        i = pl.program_id(0)
        x2 = x1_ref[...] + _dot(a_ref[...], wd_ref[...])
        y, vjp = jax.vjp(_rms, x2, wf_ref[...])
        err = y - tgt_ref[...]
        dx2, dwf = vjp(err * (1.0 / D_MODEL))
        dx2_ref[...] = dx2
        part = jnp.sum(jnp.sum(err * err, axis=1, keepdims=True), axis=0, keepdims=True) * (0.5 / D_MODEL)

        @pl.when(i == 0)
        def _():
            dwf_ref[...] = jnp.zeros_like(dwf_ref)
            loss_ref[...] = jnp.zeros_like(loss_ref)

        dwf_ref[...] += dwf
        loss_ref[...] += jnp.broadcast_to(part, loss_ref.shape)

    row = pl.BlockSpec((tb, D_MODEL), lambda i: (i, 0))
    vec = pl.BlockSpec((1, D_MODEL), lambda i: (0, 0))
    return pl.pallas_call(
        body, name="down_loss", grid=(t // tb,),
        in_specs=[pl.BlockSpec((tb, D_FF), lambda i: (i, 0)), pl.BlockSpec((D_FF, D_MODEL), lambda i: (0, 0)),
                  row, vec, row],
        out_specs=[row, vec, pl.BlockSpec((1, LANES), lambda i: (0, 0))],
        out_shape=[jax.ShapeDtypeStruct((t, D_MODEL), F32), jax.ShapeDtypeStruct((1, D_MODEL), F32),
                   jax.ShapeDtypeStruct((1, LANES), F32)],
        compiler_params=_params(("arbitrary",)),
    )(a, w_down, x1, wf, target)


def _local_step(x, target, wts, late_wire=(), late_weights=None, early_partials=None):
    t = x.shape[0]
    nchunk = t // DN_CHUNK

    n1, hab = _norm1_fwd(x, wts["norm1"], wts["w_ab"])
    dnqkv = _mm(n1, wts["w_dnqkv"], name="h_dnqkv")
    dngate = _mm(n1, wts["w_dngate"], name="h_dngate")
    sbqkv = _mm(n1, wts["w_sbqkv"], out_dtype=BF16, name="h_sbqkv")
    gl = _mm(n1, wts["w_gl"], name="h_gl")

    cdn = _conv_fwd(dnqkv, wts["dn_conv"], "dn_conv_fwd")
    qn, kn, vv, gb = _dn_prep_fwd(cdn, hab, wts["alog"], wts["dtb"])
    per_head = gb[:, :2 * N_HEADS].T.reshape(2 * N_HEADS, nchunk, DN_CHUNK)
    grow, brow = per_head[:N_HEADS, :, None, :], per_head[N_HEADS:, :, None, :]
    u_dn, w_dn, a_qk, qe, kdec, egl, tinv, *late = _dn_local_fwd(qn, kn, vv, grow, brow, late_wire)
    if late_wire:
        wts = {**wts, **late_weights(late)}
    o_raw, states = _dn_seq_fwd(u_dn, w_dn, a_qk, qe, kdec, egl)
    o_dn = _dn_post_fwd(o_raw, dngate, wts["dn_norm"])

    o_sb, tot, sb_used = _sb_fwd(sbqkv)

    pdn, psb, mixed, x1, n2 = _merge_fwd(o_dn, o_sb, gl, x, wts["wp_dn"], wts["wp_sb"], wts["w_out"],
                                         wts["norm2"])
    pre_g = _mm(n2, wts["w_up_g"], name="ffn_up_g")
    pre_u = _mm(n2, wts["w_up_u"], name="ffn_up_u")
    act = _ffn_mid_fwd(pre_g, pre_u, wts["ffn_conv_g"], wts["ffn_conv_u"])
    dx2, d_normf, loss_part = _down_loss(act, wts["w_down"], x1, wts["normf"], target)

    grads = {"normf": d_normf}
    da = _mm(dx2, wts["w_down"], tb=True, name="d_act")
    grads["w_down"] = _mm(act, dx2, ta=True, out_dtype=BF16, name="dw_down")
    dpre_g, dpre_u, dcw_g, dcw_u = _ffn_mid_bwd(pre_g, pre_u, wts["ffn_conv_g"], wts["ffn_conv_u"], da)
    grads["ffn_conv"] = jnp.concatenate([dcw_g[:FFN_CONV], dcw_u[:FFN_CONV]], axis=1)
    dn2 = _mm(dpre_g, wts["w_up_g"], tb=True, name="dn2_g")
    dn2 = _mm(dpre_u, wts["w_up_u"], tb=True, add=dn2, name="dn2_u")
    grads["w_up_g"] = _mm(n2, dpre_g, ta=True, out_dtype=BF16, name="dw_up_g")
    grads["w_up_u"] = _mm(n2, dpre_u, ta=True, out_dtype=BF16, name="dw_up_u")

    dx1, grads["norm2"], dgl, dpdn, dpsb, do_dn, do_sb = _merge_bwd(
        dx2, dn2, x1, wts["norm2"], gl, pdn, psb, wts["wp_dn"], wts["wp_sb"], wts["w_out"])
    grads["w_out"] = _mm(mixed, dx1, ta=True, out_dtype=BF16, name="dw_out")
    grads["wp_dn"] = _mm(o_dn, dpdn, ta=True, out_dtype=BF16, name="dw_proj_dn")
    grads["wp_sb"] = _mm(o_sb, dpsb, ta=True, out_dtype=BF16, name="dw_proj_sb")

    partials = early_partials(grads) if early_partials else ()
    dsq, dsk, dsv = _sb_bwd(sbqkv, tot, sb_used, do_sb)
    dsbqkv = jnp.concatenate([dsq, dsk, dsv], axis=1).astype(BF16)

    do_raw, ddngate, grads["dn_norm"] = _dn_post_bwd(o_raw, dngate, wts["dn_norm"], do_dn)
    seq_grads = _dn_seq_bwd(u_dn, w_dn, a_qk, qe, kdec, egl, states, do_raw)
    dqn, dkn, dvv, dgrow, dbrow, *arrived = _dn_local_bwd(qn, kn, vv, grow, brow, tinv, *seq_grads,
                                                          partials=partials)
    dgb = jnp.concatenate([dgrow.reshape(N_HEADS, t), dbrow.reshape(N_HEADS, t)], axis=0).T
    dgb = jnp.pad(dgb, ((0, 0), (0, LANES - 2 * N_HEADS)))
    dcdn, dhab, grads["alog"], grads["dtb"] = _dn_prep_bwd(cdn, hab, wts["alog"], wts["dtb"], dqn, dkn, dvv, dgb)
    ddnqkv, dcw_dn = _conv_bwd(dcdn, dnqkv, wts["dn_conv"], "dn_conv_bwd", BF16)
    grads["dn_conv"] = dcw_dn[:DN_CONV]

    dn1 = _mm(ddnqkv, wts["w_dnqkv"], tb=True, name="dn1_dnqkv")
    dn1 = _mm(ddngate, wts["w_dngate"], tb=True, add=dn1, name="dn1_dngate")
    dn1 = _mm(dsbqkv, wts["w_sbqkv"], tb=True, add=dn1, name="dn1_sbqkv")
    dn1 = _mm(dgl, wts["w_gl"], tb=True, add=dn1, name="dn1_gl")
    grads["w_dnqkv"] = _mm(n1, ddnqkv, ta=True, out_dtype=BF16, name="dw_dnqkv")
    grads["w_dngate"] = _mm(n1, ddngate, ta=True, out_dtype=BF16, name="dw_dngate")
    grads["w_sbqkv"] = _mm(n1, dsbqkv, ta=True, out_dtype=BF16, name="dw_sbqkv")
    grads["w_gl"] = _mm(n1, dgl, ta=True, out_dtype=BF16, name="dw_gl")
    grads["w_ab"] = _mm(n1, dhab, ta=True, out_dtype=BF16, name="dw_ab")
    grad_x, grads["norm1"] = _norm1_bwd(x, wts["norm1"], dn1, dx1, dhab, wts["w_ab"])
    return loss_part, grad_x, grads, (list(partials), arrived)


def _place():
    return lax.axis_index("x"), lax.axis_index("y"), lax.axis_index("c")


def _hbm_specs(n):
    return [pl.BlockSpec(memory_space=pltpu.HBM)] * n


GATHER_SEMS = 8


def _gather_protocol(ins, outs, send_sems, recv_sems):
    n = len(ins)
    x, y, c = _place()
    me = 2 * x + y
    sibling = (x, y, 1 - c)
    xn, yn, dg = (1 - x, y), (x, 1 - y), (1 - x, 1 - y)
    idx = lambda chip: 2 * chip[0] + chip[1]

    def part(a, chip_index, core, quarter=None):
        half = ins[a].shape[0] // 2
        if quarter is None:
            return outs[a].at[chip_index, pl.ds(core * half, half), :]
        return outs[a].at[chip_index, pl.ds(core * half + quarter * (half // 2), half // 2), :]

    def copy(a, k, src, dst, to):
        return pltpu.make_async_remote_copy(src_ref=src, dst_ref=dst, send_sem=send_sems.at[GATHER_SEMS * a + k],
                                            recv_sem=recv_sems.at[GATHER_SEMS * a + k], device_id=to,
                                            device_id_type=MESH)

    def sent(a, k):
        half = ins[a].shape[0] // 2
        my_half = ins[a].at[pl.ds(c * half, half), :]
        if k < 2:
            return copy(a, k, my_half, part(a, me, c), (*(xn, yn)[k], c))
        if k < 4:
            src = part(a, idx((xn, yn)[k - 2]), c, k - 2)
            return copy(a, k, src, src, (*(yn, xn)[k - 2], c))
        src = (part(a, idx(xn), c), part(a, idx(yn), c), part(a, idx(dg), c, 0), part(a, idx(dg), c, 1))[k - 4]
        return copy(a, k, src, src, sibling)

    def landed(a, k):
        dst = (part(a, idx(xn), c), part(a, idx(yn), c), part(a, idx(dg), c, 0), part(a, idx(dg), c, 1),
               part(a, idx(xn), 1 - c), part(a, idx(yn), 1 - c), part(a, idx(dg), 1 - c, 0),
               part(a, idx(dg), 1 - c, 1))[k]
        return copy(a, k, dst, dst, sibling)

    def begin():
        for a in range(n):
            sent(a, 0).start()
            sent(a, 1).start()

    def middle():
        for a in range(n):
            for k in range(2):
                landed(a, k).wait_recv()
                sent(a, 2 + k).start()
                sent(a, 4 + k).start()

    def end():
        for a in range(n):
            for k in (2, 3):
                landed(a, k).wait_recv()
                sent(a, 4 + k).start()
        for a in range(n):
            for k in range(4, GATHER_SEMS):
                landed(a, k).wait_recv()
        for a in range(n):
            for k in range(GATHER_SEMS):
                sent(a, k).wait_send()

    return begin, middle, end


def _gather_out_shapes(shards):
    return [jax.ShapeDtypeStruct((N_CHIPS,) + s.shape, s.dtype) for s in shards]


def _gather_sems(n):
    return [pltpu.SemaphoreType.DMA((GATHER_SEMS * n,)), pltpu.SemaphoreType.DMA((GATHER_SEMS * n,))]


def _gather_shards(shards):
    n = len(shards)

    def body(*refs):
        begin, middle, end = _gather_protocol(refs[:n], refs[n:2 * n], *refs[2 * n:])
        begin()
        middle()
        end()

    return pl.pallas_call(
        body, name="gather_weights", in_specs=_hbm_specs(n), out_specs=_hbm_specs(n),
        out_shape=_gather_out_shapes(shards), scratch_shapes=_gather_sems(n),
    )(*shards)


def _pair_exchange_halves(gs, tag):
    n = len(gs)

    def body(*refs):
        ins, outs, (send_sems, recv_sems) = refs[:n], refs[n:2 * n], refs[2 * n:]
        x, y, c = _place()
        cps = []
        for a in range(n):
            half = ins[a].shape[1] // 2
            cp = pltpu.make_async_remote_copy(src_ref=ins[a].at[:, pl.ds((1 - c) * half, half), :], dst_ref=outs[a],
                                              send_sem=send_sems.at[a], recv_sem=recv_sems.at[a],
                                              device_id=(x, y, 1 - c), device_id_type=MESH)
            cp.start()
            cps.append(cp)
        for cp in cps:
            cp.wait()

    return pl.pallas_call(
        body, name="grad_pair_exchange_" + tag, in_specs=_hbm_specs(n), out_specs=_hbm_specs(n),
        out_shape=[jax.ShapeDtypeStruct((g.shape[0], g.shape[1] // 2, g.shape[2]), g.dtype) for g in gs],
        scratch_shapes=[pltpu.SemaphoreType.DMA((n,)), pltpu.SemaphoreType.DMA((n,))],
    )(*gs)


def _pick_rows(n, target=1024):
    best = 16
    for b in range(16, min(n, target) + 1, 16):
        if n % b == 0:
            best = b
    return best


def _pair_add(g, got, c_idx, tag):
    nsh, rows, cols = g.shape
    half = rows // 2
    rb = _pick_rows(half)

    def body(c_ref, g_ref, got_ref, o_ref):
        o_ref[...] = (g_ref[...].astype(F32) + got_ref[...].astype(F32)).astype(BF16)

    nb = half // rb
    grid_spec = pltpu.PrefetchScalarGridSpec(
        num_scalar_prefetch=1, grid=(nsh, nb),
        in_specs=[pl.BlockSpec((1, rb, cols), lambda s, i, c_ref: (s, c_ref[0] * nb + i, 0)),
                  pl.BlockSpec((1, rb, cols), lambda s, i, c_ref: (s, i, 0))],
        out_specs=pl.BlockSpec((1, rb, cols), lambda s, i, c_ref: (s, i, 0)))
    return pl.pallas_call(
        body, name="grad_pair_add_" + tag, grid_spec=grid_spec,
        out_shape=jax.ShapeDtypeStruct((nsh, half, cols), BF16),
        compiler_params=_params(("parallel", "parallel")),
    )(c_idx, g, got)


def _chip_exchange_protocol(ins, outs, send_sems, recv_sems):
    x, y, c = _place()
    chips = [(1 - x, y), (x, 1 - y), (1 - x, 1 - y)]

    def copies():
        return [pltpu.make_async_remote_copy(src_ref=ins[a].at[2 * px + py], dst_ref=outs[a].at[j],
                                             send_sem=send_sems.at[3 * a + j], recv_sem=recv_sems.at[3 * a + j],
                                             device_id=(px, py, c), device_id_type=MESH)
                for a in range(len(ins)) for j, (px, py) in enumerate(chips)]

    def begin():
        for cp in copies():
            cp.start()

    def end():
        for cp in copies():
            cp.wait_recv()
        for cp in copies():
            cp.wait_send()

    return begin, end


def _chip_exchange_shapes(ps):
    return [jax.ShapeDtypeStruct((N_CHIPS - 1,) + p.shape[1:], p.dtype) for p in ps]


def _chip_exchange_sems(n):
    return [pltpu.SemaphoreType.DMA((3 * n,)), pltpu.SemaphoreType.DMA((3 * n,))]


def _chip_exchange(ps):
    n = len(ps)

    def body(*refs):
        begin, end = _chip_exchange_protocol(refs[:n], refs[n:2 * n], *refs[2 * n:])
        begin()
        end()

    return pl.pallas_call(
        body, name="grad_chip_exchange", in_specs=_hbm_specs(n), out_specs=_hbm_specs(n),
        out_shape=_chip_exchange_shapes(ps), scratch_shapes=_chip_exchange_sems(n),
    )(*ps)


def _sum_partials(p, got, chip_idx, tag):
    nsh, half, cols = got.shape
    rb = _pick_rows(half)

    def body(me_ref, p_ref, got_ref, o_ref):
        acc = p_ref[0].astype(F32)
        for s in range(nsh):
            acc = acc + got_ref[s].astype(F32)
        o_ref[...] = acc

    grid_spec = pltpu.PrefetchScalarGridSpec(
        num_scalar_prefetch=1, grid=(half // rb,),
        in_specs=[pl.BlockSpec((1, rb, cols), lambda i, me_ref: (me_ref[0], i, 0)),
                  pl.BlockSpec((nsh, rb, cols), lambda i, me_ref: (0, i, 0))],
        out_specs=pl.BlockSpec((rb, cols), lambda i, me_ref: (i, 0)))
    return pl.pallas_call(
        body, name="grad_sum_chips_" + tag, grid_spec=grid_spec,
        out_shape=jax.ShapeDtypeStruct((half, cols), F32),
        compiler_params=_params(("parallel",)),
    )(chip_idx, p, got)


def _pair_share(rs):
    n = len(rs)

    def body(*refs):
        ins, outs, (send_sems, recv_sems) = refs[:n], refs[n:2 * n], refs[2 * n:]
        x, y, c = _place()
        cps = []
        for a in range(n):
            cp = pltpu.make_async_remote_copy(src_ref=ins[a], dst_ref=outs[a], send_sem=send_sems.at[a],
                                              recv_sem=recv_sems.at[a], device_id=(x, y, 1 - c),
                                              device_id_type=MESH)
            cp.start()
            cps.append(cp)
        for cp in cps:
            cp.wait()

    return pl.pallas_call(
        body, name="grad_pair_share", in_specs=_hbm_specs(n), out_specs=_hbm_specs(n),
        out_shape=[jax.ShapeDtypeStruct(r.shape, r.dtype) for r in rs],
        scratch_shapes=[pltpu.SemaphoreType.DMA((n,)), pltpu.SemaphoreType.DMA((n,))],
    )(*rs)


def _small_allreduce(v):
    rows, cols = v.shape
    ndev = 8

    def body(in_ref, out_ref, slots, send_sems, recv_sems):
        x, y, c = _place()
        me = 4 * x + 2 * y + c
        slots[me] = in_ref[...]
        sends = []
        for k in range(1, ndev):
            peer = (x ^ (k >> 2), y ^ ((k >> 1) & 1), c ^ (k & 1))
            cp = pltpu.make_async_remote_copy(src_ref=in_ref, dst_ref=slots.at[me], send_sem=send_sems.at[k - 1],
                                              recv_sem=recv_sems.at[k - 1], device_id=peer, device_id_type=MESH)
            cp.start()
            sends.append(cp)
        for k in range(1, ndev):
            there = slots.at[me ^ k]
            pltpu.make_async_remote_copy(src_ref=there, dst_ref=there, send_sem=send_sems.at[k - 1],
                                         recv_sem=recv_sems.at[k - 1], device_id=(x, y, c),
                                         device_id_type=MESH).wait_recv()
        for cp in sends:
            cp.wait_send()
        acc = slots[0]
        for s in range(1, ndev):
            acc = acc + slots[s]
        out_ref[...] = acc

    return pl.pallas_call(
        body, name="small_allreduce",
        in_specs=[pl.BlockSpec(memory_space=pltpu.VMEM)],
        out_specs=pl.BlockSpec(memory_space=pltpu.VMEM),
        out_shape=jax.ShapeDtypeStruct((rows, cols), F32),
        scratch_shapes=[pltpu.VMEM((ndev, rows, cols), F32), pltpu.SemaphoreType.DMA((ndev - 1,)),
                        pltpu.SemaphoreType.DMA((ndev - 1,))],
    )(v)


def _adamw(w, g, m, v, name):
    r, c = w.shape
    rb = r if r <= 128 else _pick_rows_8(r, 128)
    c1 = 1.0 - ADAM_B1 ** ADAM_STEP
    c2 = 1.0 - ADAM_B2 ** ADAM_STEP

    def body(w_ref, g_ref, m_ref, v_ref, d_ref, nm_ref, nv_ref):
        gg = g_ref[...]
        nm = ADAM_B1 * m_ref[...] + (1.0 - ADAM_B1) * gg
        nv = ADAM_B2 * v_ref[...] + (1.0 - ADAM_B2) * (gg * gg)
        d_ref[...] = -ADAM_LR * ((nm / c1) / (jnp.sqrt(nv / c2) + ADAM_EPS) + ADAM_WD * w_ref[...])
        nm_ref[...] = nm
        nv_ref[...] = nv

    blk = pl.BlockSpec((rb, c), lambda i: (i, 0))
    shp = jax.ShapeDtypeStruct((r, c), F32)
    return pl.pallas_call(
        body, name=name, grid=(r // rb,), in_specs=[blk] * 4, out_specs=[blk] * 3, out_shape=[shp] * 3,
        compiler_params=_params(("parallel",)),
    )(w, g, m, v)


def _pick_rows_8(n, target):
    best = n
    for b in range(8, min(n, target) + 1, 8):
        if n % b == 0:
            best = b
    return best


W_IN_COLS = 2308
W_UP_COLS = 1408
W_DOWN_ROWS = 704
DN_CONV_COLS = 768
FFN_CONV_COLS = 1408
PROJ_ROWS = 256
ROW_TILE = 16
ROW_SEGS = [("wp_dn", PROJ_ROWS), ("wp_sb", PROJ_ROWS), ("w_out", PROJ_ROWS), ("w_down", W_DOWN_ROWS),
            ("dn_conv", ROW_TILE), ("ffn_conv", ROW_TILE), ("spare", 2 * ROW_TILE)]
ROW_OFFS = {nm: (sum(n for _, n in ROW_SEGS[:i]), n) for i, (nm, n) in enumerate(ROW_SEGS)}
STACK_ROWS = sum(n for _, n in ROW_SEGS)
assert all(n % ROW_TILE == 0 for _, n in ROW_SEGS) and STACK_ROWS % (4 * ROW_TILE) == 0
Q_END, A_END, G_END, S_END = 3 * D_MODEL, 3 * D_MODEL + 2 * N_HEADS, 4 * D_MODEL + 2 * N_HEADS, 7 * D_MODEL + 2 * N_HEADS


def _flat_rows(a, nrows):
    flat = a.reshape(-1)
    return jnp.pad(flat, (0, nrows * D_MODEL - flat.shape[0])).reshape(nrows, D_MODEL)


IN_EXTRA_ROWS = 64


def _weight_wire(w_in, wp_dn, wp_sb, w_out, w_up, w_down, dn_conv, ffn_conv):
    bits = lax.bitcast_convert_type(dn_conv, BF16).reshape(-1)
    extra = jnp.pad(bits, (0, IN_EXTRA_ROWS * W_IN_COLS - bits.shape[0])).reshape(IN_EXTRA_ROWS, W_IN_COLS)
    stack = jnp.concatenate([wp_dn.astype(BF16), wp_sb.astype(BF16), w_out.astype(BF16), w_down.astype(BF16),
                             jnp.zeros((ROW_TILE, D_MODEL), BF16),
                             _flat_rows(lax.bitcast_convert_type(ffn_conv, BF16), ROW_TILE),
                             jnp.zeros((ROW_OFFS["spare"][1], D_MODEL), BF16)], axis=0)
    return [jnp.concatenate([w_in.astype(BF16), extra], axis=0)], [w_up.astype(BF16), stack]


def _col_range(g, lo, hi, width):
    parts = []
    for s in range(g.shape[0]):
        a, b = max(lo, s * width), min(hi, (s + 1) * width)
        if a < b:
            parts.append(g[s][:, a - s * width:b - s * width])
    return parts[0] if len(parts) == 1 else jnp.concatenate(parts, axis=1)


def _f32_rows(raw, k, ncols):
    raw = raw.reshape(N_CHIPS, -1)[:, :2 * k * ncols].reshape(N_CHIPS, k * ncols, 2)
    vals = lax.bitcast_convert_type(raw, F32).reshape(N_CHIPS, k, ncols)
    return vals.transpose(1, 0, 2).reshape(k, N_CHIPS * ncols)


def _unpack_early(g_in):
    w = g_in[:, :D_MODEL, :]
    return {
        "w_dnqkv": _col_range(w, 0, Q_END, W_IN_COLS),
        "w_ab": jnp.pad(_col_range(w, Q_END, A_END, W_IN_COLS), ((0, 0), (0, LANES - 2 * N_HEADS))),
        "w_dngate": _col_range(w, A_END, G_END, W_IN_COLS),
        "w_sbqkv": _col_range(w, G_END, S_END, W_IN_COLS),
        "w_gl": _col_range(w, S_END, N_CHIPS * W_IN_COLS, W_IN_COLS),
        "dn_conv": _f32_rows(g_in[:, D_MODEL:, :], DN_CONV, DN_CONV_COLS),
    }


def _unpack_late(g_up, g_stack):
    def seg(nm):
        at, n = ROW_OFFS[nm]
        return g_stack[:, at:at + n, :]

    ffn_conv = _f32_rows(seg("ffn_conv"), FFN_CONV, FFN_CONV_COLS)
    return {
        "wp_dn": seg("wp_dn").reshape(D_MODEL, D_MODEL),
        "wp_sb": seg("wp_sb").reshape(D_MODEL, D_MODEL),
        "w_out": seg("w_out").reshape(D_MODEL, D_MODEL),
        "w_up_g": _col_range(g_up, 0, D_FF, W_UP_COLS), "w_up_u": _col_range(g_up, D_FF, 2 * D_FF, W_UP_COLS),
        "w_down": seg("w_down").reshape(D_FF, D_MODEL),
        "ffn_conv_g": ffn_conv[:, :D_FF], "ffn_conv_u": ffn_conv[:, D_FF:],
    }


def _grad_wire_early(gr):
    def cols(a, ncols):
        return a.reshape(a.shape[0], N_CHIPS, ncols).transpose(1, 0, 2)

    def rows(a, nrows):
        return a.astype(BF16).reshape(N_CHIPS, nrows, a.shape[1])

    def flat(a, nrows):
        a = a.astype(BF16).reshape(N_CHIPS, -1)
        return jnp.pad(a, ((0, 0), (0, nrows * D_MODEL - a.shape[1]))).reshape(N_CHIPS, nrows, D_MODEL)

    up = [gr["w_up_g"], gr["w_up_u"]]
    g_up = jnp.stack([up[s // 2][:, (s % 2) * W_UP_COLS:(s % 2 + 1) * W_UP_COLS].astype(BF16) for s in range(N_CHIPS)])
    g_stack = jnp.concatenate([rows(gr["wp_dn"], PROJ_ROWS), rows(gr["wp_sb"], PROJ_ROWS), rows(gr["w_out"], PROJ_ROWS),
                               rows(gr["w_down"], W_DOWN_ROWS), jnp.zeros((N_CHIPS, ROW_TILE, D_MODEL), BF16),
                               flat(cols(gr["ffn_conv"], FFN_CONV_COLS), ROW_TILE),
                               jnp.zeros((N_CHIPS, ROW_OFFS["spare"][1], D_MODEL), BF16)], axis=1)
    return [g_up, g_stack]


def _grad_wire_late(gr):
    pieces = [(gr["w_dnqkv"], 0), (gr["w_ab"][:, :2 * N_HEADS], Q_END), (gr["w_dngate"], A_END),
              (gr["w_sbqkv"], G_END), (gr["w_gl"], S_END)]
    conv = gr["dn_conv"].reshape(DN_CONV, N_CHIPS, DN_CONV_COLS).transpose(1, 0, 2).reshape(N_CHIPS, -1)

    def block(s):
        lo, hi = s * W_IN_COLS, (s + 1) * W_IN_COLS
        parts = []
        for a, at in pieces:
            b0, b1 = max(lo, at), min(hi, at + a.shape[1])
            if b0 < b1:
                parts.append(a[:, b0 - at:b1 - at].astype(BF16))
        w = parts[0] if len(parts) == 1 else jnp.concatenate(parts, axis=1)
        extra = jnp.pad(conv[s].astype(BF16), (0, IN_EXTRA_ROWS * W_IN_COLS - conv.shape[1]))
        return jnp.concatenate([w, extra.reshape(IN_EXTRA_ROWS, W_IN_COLS)], axis=0)

    return [jnp.stack([block(s) for s in range(N_CHIPS)])]


def _unpack_grad_shard(r_in, r_up, r_stack):
    def seg(nm):
        at, n = ROW_OFFS[nm]
        return r_stack[at:at + n, :]

    return {
        "w_in": r_in[:D_MODEL], "w_up": r_up,
        "wp_dn": seg("wp_dn"), "wp_sb": seg("wp_sb"), "w_out": seg("w_out"), "w_down": seg("w_down"),
        "dn_conv": r_in[D_MODEL:].reshape(-1)[:DN_CONV * DN_CONV_COLS].reshape(DN_CONV, DN_CONV_COLS),
        "ffn_conv": seg("ffn_conv").reshape(-1)[:FFN_CONV * FFN_CONV_COLS].reshape(FFN_CONV, FFN_CONV_COLS),
    }


def _lane_row(v):
    return jnp.pad(v.reshape(1, -1), ((0, 0), (0, LANES - v.size)))


def kernel(x, norm1_w, w_in, dn_conv_w, dn_A_log, dn_dt_bias, dn_norm_w, w_proj_dn, w_proj_sb, w_out, norm2_w, ffn_w_up, ffn_conv_w, ffn_w_down, norm_f_w, loss_target, m_norm1_w, m_w_in, m_dn_conv_w, m_dn_A_log, m_dn_dt_bias, m_dn_norm_w, m_w_proj_dn, m_w_proj_sb, m_w_out, m_norm2_w, m_ffn_w_up, m_ffn_conv_w, m_ffn_w_down, m_norm_f_w, v_norm1_w, v_w_in, v_dn_conv_w, v_dn_A_log, v_dn_dt_bias, v_dn_norm_w, v_w_proj_dn, v_w_proj_sb, v_w_out, v_norm2_w, v_ffn_w_up, v_ffn_conv_w, v_ffn_w_down, v_norm_f_w):
    early, late = _weight_wire(w_in[0], w_proj_dn[0], w_proj_sb[0], w_out[0], ffn_w_up[0], ffn_w_down[0],
                               dn_conv_w[0], ffn_conv_w[0])
    chip_idx = (2 * lax.axis_index("x") + lax.axis_index("y")).astype(jnp.int32)

    def with_mine(gathered, wire):
        return [lax.dynamic_update_slice(g, mine[None], (chip_idx, 0, 0)) for g, mine in zip(gathered, wire)]

    wts = _unpack_early(*with_mine(_gather_shards(early), early))
    wts.update(norm1=norm1_w, norm2=norm2_w, normf=norm_f_w.reshape(1, D_MODEL), dn_norm=dn_norm_w,
               alog=_lane_row(dn_A_log), dtb=_lane_row(dn_dt_bias))

    c_idx = lax.axis_index("c").astype(jnp.int32).reshape(1)

    def pair_sums(wire_g, tags, when):
        return [_pair_add(g, got, c_idx, tag) for g, got, tag in zip(wire_g, _pair_exchange_halves(wire_g, when), tags)]

    loss_part, grad_x, gr, (early_sums, early_arrived) = _local_step(
        x[0], loss_target[0], wts, late, lambda gathered: _unpack_late(*with_mine(gathered, late)),
        lambda grads: pair_sums(_grad_wire_early(grads), ["w_up", "rows"], "early"))

    late_sums = pair_sums(_grad_wire_late(gr), ["w_in"], "late")
    tags = ["w_in", "w_up", "rows"]
    reduced = [_sum_partials(p, got, chip_idx.reshape(1), tag)
               for p, got, tag in zip(late_sums + early_sums, list(_chip_exchange(late_sums)) + list(early_arrived), tags)]
    is_south = lax.axis_index("c") == 0
    gsh = _unpack_grad_shard(*[jnp.concatenate([jnp.where(is_south, mine, other), jnp.where(is_south, other, mine)],
                                               axis=0) for mine, other in zip(reduced, _pair_share(reduced))])

    tail = jnp.concatenate([gr["dn_norm"], gr["alog"][:, :N_HEADS], gr["dtb"][:, :N_HEADS], loss_part[:, :1]], axis=1)
    small = jnp.concatenate([gr["norm1"], gr["norm2"], gr["normf"],
                             jnp.pad(tail, ((0, 0), (0, D_MODEL - tail.shape[1]))),
                             jnp.zeros((SMALL_ROWS - 4, D_MODEL), F32)], axis=0)
    small = _small_allreduce(small)
    at = HEAD_DIM
    g_small = {"norm1_w": small[0:1], "norm2_w": small[1:2], "norm_f_w": small[2],
               "dn_norm_w": small[3:4, :at], "dn_A_log": small[3:4, at:at + N_HEADS],
               "dn_dt_bias": small[3:4, at + N_HEADS:at + 2 * N_HEADS]}
    loss = small[3, at + 2 * N_HEADS]

    big = {"w_in": (w_in, m_w_in, v_w_in, gsh["w_in"]), "dn_conv_w": (dn_conv_w, m_dn_conv_w, v_dn_conv_w, gsh["dn_conv"]),
           "w_proj_dn": (w_proj_dn, m_w_proj_dn, v_w_proj_dn, gsh["wp_dn"]),
           "w_proj_sb": (w_proj_sb, m_w_proj_sb, v_w_proj_sb, gsh["wp_sb"]),
           "w_out": (w_out, m_w_out, v_w_out, gsh["w_out"]),
           "ffn_w_up": (ffn_w_up, m_ffn_w_up, v_ffn_w_up, gsh["w_up"]),
           "ffn_conv_w": (ffn_conv_w, m_ffn_conv_w, v_ffn_conv_w, gsh["ffn_conv"]),
           "ffn_w_down": (ffn_w_down, m_ffn_w_down, v_ffn_w_down, gsh["w_down"])}
    res = {}
    for nm, (w, m, v, g) in big.items():
        d, nm_, nv_ = _adamw(w[0], g, m[0], v[0], "adamw_" + nm)
        res[nm] = (g[None], d[None], nm_[None], nv_[None])

    names = ["norm1_w", "norm2_w", "norm_f_w", "dn_norm_w", "dn_A_log", "dn_dt_bias"]
    given = {"norm1_w": (norm1_w, m_norm1_w, v_norm1_w), "norm2_w": (norm2_w, m_norm2_w, v_norm2_w),
             "norm_f_w": (norm_f_w, m_norm_f_w, v_norm_f_w), "dn_norm_w": (dn_norm_w, m_dn_norm_w, v_dn_norm_w),
             "dn_A_log": (dn_A_log, m_dn_A_log, v_dn_A_log), "dn_dt_bias": (dn_dt_bias, m_dn_dt_bias, v_dn_dt_bias)}

    def stack(k, fill):
        rows = [jnp.pad(given[nm][k].reshape(1, -1), ((0, 0), (0, D_MODEL - given[nm][k].size)),
                        constant_values=fill) for nm in names]
        return jnp.concatenate(rows + [jnp.full((SMALL_ROWS - len(names), D_MODEL), fill, F32)], axis=0)

    g_rows = jnp.concatenate(
        [jnp.pad(g_small[nm].reshape(1, -1), ((0, 0), (0, D_MODEL - g_small[nm].size))) for nm in names]
        + [jnp.zeros((SMALL_ROWS - len(names), D_MODEL), F32)], axis=0)
    d_s, m_s, v_s = _adamw(stack(0, 0.0), g_rows, stack(1, 0.0), stack(2, 1.0), "adamw_small")
    for r, nm in enumerate(names):
        shape = given[nm][0].shape
        n = given[nm][0].size
        res[nm] = (g_small[nm].reshape(shape), d_s[r, :n].reshape(shape), m_s[r, :n].reshape(shape),
                   v_s[r, :n].reshape(shape))

    order = ["norm1_w", "w_in", "dn_conv_w", "dn_A_log", "dn_dt_bias", "dn_norm_w", "w_proj_dn", "w_proj_sb",
             "w_out", "norm2_w", "ffn_w_up", "ffn_conv_w", "ffn_w_down", "norm_f_w"]
    outs = [loss, grad_x[None]]
    for k in range(4):
        outs += [res[nm][k] for nm in order]
    return tuple(outs)
```

```python
import functools

import jax
import jax.numpy as jnp
from jax import lax
from jax.experimental import pallas as pl
from jax.experimental.pallas import tpu as pltpu

F32 = jnp.float32
BF16 = jnp.bfloat16
HIGHEST = lax.Precision.HIGHEST
MESH = pl.DeviceIdType.MESH

EPS = 1e-6
D_MODEL = 1024
N_HEADS = 8
HEAD_DIM = 128
DN_CONV = 4
DN_CHUNK = 64
D_FF = 2816
FFN_CONV = 3
ADAM_LR, ADAM_B1, ADAM_B2, ADAM_EPS, ADAM_WD, ADAM_STEP = 0.001, 0.9, 0.999, 1e-08, 0.01, 10

N_CHIPS = 4
LANES = 128
HALO = 8
VMEM_LIMIT = 48 * 1024 * 1024
SMALL_ROWS = 8


def _params(sem=None):
    return pltpu.CompilerParams(dimension_semantics=sem, vmem_limit_bytes=VMEM_LIMIT)


def _pick(n, target):
    best = None
    for b in range(LANES, min(n, target) + 1, LANES):
        if n % b == 0:
            best = b
    return best or n


ELEMENTWISE_COLS = 1408


def _rows(t, target=256):
    return min(t, target)


def _dot(a, b, precision=None):
    return lax.dot_general(a, b, (((1,), (0,)), ((), ())), precision=precision, preferred_element_type=F32)


def _dot_nt(a, b, precision=None):
    return lax.dot_general(a, b, (((1,), (1,)), ((), ())), precision=precision, preferred_element_type=F32)


def _dot_tn(a, b, precision=None):
    return lax.dot_general(a, b, (((0,), (0,)), ((), ())), precision=precision, preferred_element_type=F32)


def _rms(x, w):
    return x * lax.rsqrt(jnp.mean(x * x, axis=-1, keepdims=True) + EPS) * w


def _silu(x):
    return x * jax.nn.sigmoid(x)


def _softplus(x):
    return jnp.maximum(x, 0.0) + jnp.log(1.0 + jnp.exp(-jnp.abs(x)))


MM_BLOCK = 1408


def _mm(a, b, *, ta=False, tb=False, add=None, out_dtype=F32, name, bm=MM_BLOCK, bn=MM_BLOCK, bk=MM_BLOCK):
    m = a.shape[1] if ta else a.shape[0]
    k = a.shape[0] if ta else a.shape[1]
    n = b.shape[0] if tb else b.shape[1]
    bm, bn, bk = _pick(m, bm), _pick(n, bn), _pick(k, bk)
    nk = k // bk
    dims = (((0 if ta else 1,), (1 if tb else 0,)), ((), ()))

    def body(*refs):
        a_ref, b_ref = refs[:2]
        c_ref = refs[2] if add is not None else None
        o_ref = refs[3] if add is not None else refs[2]
        acc = refs[-1]
        kk = pl.program_id(2)
        part = lax.dot_general(a_ref[...].astype(BF16), b_ref[...].astype(BF16), dims, preferred_element_type=F32)

        def finish(r):
            if add is not None:
                r = r + c_ref[...].astype(F32)
            o_ref[...] = r.astype(out_dtype)

        if nk == 1:
            finish(part)
            return

        @pl.when(kk == 0)
        def _():
            acc[...] = part

        @pl.when(jnp.logical_and(kk > 0, kk < nk - 1))
        def _():
            acc[...] += part

        @pl.when(kk == nk - 1)
        def _():
            finish(acc[...] + part)

    a_spec = (pl.BlockSpec((bk, bm), lambda i, j, kk: (kk, i)) if ta
              else pl.BlockSpec((bm, bk), lambda i, j, kk: (i, kk)))
    b_spec = (pl.BlockSpec((bn, bk), lambda i, j, kk: (j, kk)) if tb
              else pl.BlockSpec((bk, bn), lambda i, j, kk: (kk, j)))
    o_spec = pl.BlockSpec((bm, bn), lambda i, j, kk: (i, j))
    in_specs = [a_spec, b_spec] + ([o_spec] if add is not None else [])
    args = (a, b) + ((add,) if add is not None else ())
    return pl.pallas_call(
        body, name=name, grid=(m // bm, n // bn, nk),
        in_specs=in_specs, out_specs=o_spec,
        out_shape=jax.ShapeDtypeStruct((m, n), out_dtype),
        scratch_shapes=[pltpu.VMEM((bm, bn), F32)] if nk > 1 else [],
        compiler_params=_params(("parallel", "parallel", "arbitrary")),
    )(*args)


def _norm1_fwd(x, w, w_ab):
    t = x.shape[0]
    tb = _rows(t)

    def body(x_ref, w_ref, wab_ref, n_ref, hab_ref):
        n = _rms(x_ref[...], w_ref[...]).astype(BF16)
        n_ref[...] = n
        hab_ref[...] = _dot(n, wab_ref[...])

    return pl.pallas_call(
        body, name="norm1_fwd", grid=(t // tb,),
        in_specs=[pl.BlockSpec((tb, D_MODEL), lambda i: (i, 0)),
                  pl.BlockSpec((1, D_MODEL), lambda i: (0, 0)),
                  pl.BlockSpec((D_MODEL, LANES), lambda i: (0, 0))],
        out_specs=[pl.BlockSpec((tb, D_MODEL), lambda i: (i, 0)),
                   pl.BlockSpec((tb, LANES), lambda i: (i, 0))],
        out_shape=[jax.ShapeDtypeStruct((t, D_MODEL), BF16), jax.ShapeDtypeStruct((t, LANES), F32)],
        compiler_params=_params(("arbitrary",)),
    )(x, w, w_ab)


def _norm1_bwd(x, w, dn, dres, dab, w_ab):
    t = x.shape[0]
    tb = _rows(t)

    def body(x_ref, w_ref, dn_ref, dres_ref, dab_ref, wab_ref, dx_ref, dw_ref):
        i = pl.program_id(0)
        g = dn_ref[...] + _dot_nt(dab_ref[...].astype(BF16), wab_ref[...])
        _, vjp = jax.vjp(_rms, x_ref[...], w_ref[...])
        dx, dw = vjp(g)
        dx_ref[...] = dres_ref[...] + dx

        @pl.when(i == 0)
        def _():
            dw_ref[...] = jnp.zeros_like(dw_ref)

        dw_ref[...] += dw

    row = pl.BlockSpec((tb, D_MODEL), lambda i: (i, 0))
    vec = pl.BlockSpec((1, D_MODEL), lambda i: (0, 0))
    return pl.pallas_call(
        body, name="norm1_bwd", grid=(t // tb,),
        in_specs=[row, vec, row, row, pl.BlockSpec((tb, LANES), lambda i: (i, 0)),
                  pl.BlockSpec((D_MODEL, LANES), lambda i: (0, 0))],
        out_specs=[row, vec],
        out_shape=[jax.ShapeDtypeStruct((t, D_MODEL), F32), jax.ShapeDtypeStruct((1, D_MODEL), F32)],
        compiler_params=_params(("arbitrary",)),
    )(x, w, dn, dres, dab, w_ab)


def _conv_fwd(x, w, name):
    t, c = x.shape
    kk = w.shape[0]
    tb, cb = _rows(t, 512), _pick(c, ELEMENTWISE_COLS)
    per = tb // HALO

    def body(x_ref, halo_ref, w_ref, y_ref, buf):
        i = pl.program_id(0)
        buf[pl.ds(HALO, tb), :] = x_ref[...]
        buf[pl.ds(0, HALO), :] = jnp.where(i == 0, 0.0, halo_ref[...])
        y_ref[...] = _conv_taps(buf, w_ref, HALO - (kk - 1), tb)

    return pl.pallas_call(
        body, name=name, grid=(t // tb, c // cb),
        in_specs=[pl.BlockSpec((tb, cb), lambda i, j: (i, j)),
                  pl.BlockSpec((HALO, cb), lambda i, j: (jnp.maximum(i * per - 1, 0), j)),
                  pl.BlockSpec((kk, cb), lambda i, j: (0, j))],
        out_specs=pl.BlockSpec((tb, cb), lambda i, j: (i, j)),
        out_shape=jax.ShapeDtypeStruct((t, c), F32),
        scratch_shapes=[pltpu.VMEM((tb + HALO, cb), F32)],
        compiler_params=_params(("parallel", "parallel")),
    )(x, x, w)


def _conv_bwd(dy, x, w, name, dx_dtype):
    t, c = x.shape
    kk = w.shape[0]
    tb, cb = _rows(t, 512), _pick(c, ELEMENTWISE_COLS)
    per = tb // HALO
    nblk = t // tb

    def body(dy_ref, after_ref, x_ref, w_ref, dx_ref, dw_ref, dbuf):
        i = pl.program_id(1)
        dbuf[pl.ds(0, tb), :] = dy_ref[...]
        dbuf[pl.ds(tb, HALO), :] = jnp.where(i == nblk - 1, 0.0, after_ref[...])

        @pl.when(i == 0)
        def _():
            dw_ref[...] = jnp.zeros_like(dw_ref)

        for j in range(cb // LANES):
            sl = pl.ds(j * LANES, LANES)
            x = x_ref[:, sl]
            dx = None
            for s in range(kk):
                shifted = dbuf[pl.ds(kk - 1 - s, tb), sl]
                term = w_ref[s:s + 1, sl] * shifted
                dx = term if dx is None else dx + term
                dw_ref[s:s + 1, sl] += jnp.sum(shifted * x, axis=0, keepdims=True)
            dx_ref[:, sl] = dx.astype(dx_dtype)

    blk = pl.BlockSpec((tb, cb), lambda j, i: (i, j))
    return pl.pallas_call(
        body, name=name, grid=(c // cb, nblk),
        in_specs=[blk,
                  pl.BlockSpec((HALO, cb), lambda j, i: (jnp.minimum((i + 1) * per, t // HALO - 1), j)),
                  blk,
                  pl.BlockSpec((kk, cb), lambda j, i: (0, j))],
        out_specs=[blk, pl.BlockSpec((HALO, cb), lambda j, i: (0, j))],
        out_shape=[jax.ShapeDtypeStruct((t, c), dx_dtype), jax.ShapeDtypeStruct((HALO, c), F32)],
        scratch_shapes=[pltpu.VMEM((tb + HALO, cb), F32)],
        compiler_params=_params(("parallel", "arbitrary")),
    )(dy, dy, x, w)


def _dn_prep_fn(c, hab, alog, dtb):
    s = _silu(c)
    heads = []
    for h in range(2 * N_HEADS):
        sh = s[:, h * HEAD_DIM:(h + 1) * HEAD_DIM]
        heads.append(sh * lax.rsqrt(jnp.sum(sh * sh, axis=-1, keepdims=True) + EPS))
    qn = jnp.concatenate(heads[:N_HEADS], axis=1)
    kn = jnp.concatenate(heads[N_HEADS:], axis=1)
    v = s[:, 2 * D_MODEL:]
    lane = lax.broadcasted_iota(jnp.int32, hab.shape, 1)
    g = -jnp.exp(alog) * _softplus(hab + dtb)
    beta = jax.nn.sigmoid(hab)
    gb = jnp.where(lane < N_HEADS, g, jnp.where(lane < 2 * N_HEADS, beta, 0.0))
    return qn, kn, v, gb


def _to_heads(ref, val):
    for h in range(N_HEADS):
        ref[h] = val[:, h * HEAD_DIM:(h + 1) * HEAD_DIM]


def _from_heads(ref):
    return jnp.concatenate([ref[h] for h in range(N_HEADS)], axis=1)


def _dn_prep_fwd(c, hab, alog, dtb):
    t = c.shape[0]
    tb = _rows(t)

    def body(c_ref, hab_ref, alog_ref, dtb_ref, q_ref, k_ref, v_ref, gb_ref):
        qn, kn, v, gb = _dn_prep_fn(c_ref[...], hab_ref[...], alog_ref[...], dtb_ref[...])
        _to_heads(q_ref, qn)
        _to_heads(k_ref, kn)
        _to_heads(v_ref, v)
        gb_ref[...] = gb

    hm = pl.BlockSpec((N_HEADS, tb, HEAD_DIM), lambda i: (0, i, 0))
    nar = pl.BlockSpec((tb, LANES), lambda i: (i, 0))
    vec = pl.BlockSpec((1, LANES), lambda i: (0, 0))
    return pl.pallas_call(
        body, name="dn_prep_fwd", grid=(t // tb,),
        in_specs=[pl.BlockSpec((tb, 3 * D_MODEL), lambda i: (i, 0)), nar, vec, vec],
        out_specs=[hm, hm, hm, nar],
        out_shape=[jax.ShapeDtypeStruct((N_HEADS, t, HEAD_DIM), F32)] * 3 + [jax.ShapeDtypeStruct((t, LANES), F32)],
        compiler_params=_params(("parallel",)),
    )(c, hab, alog, dtb)


def _dn_prep_bwd(c, hab, alog, dtb, dq, dk, dv, dgb):
    t = c.shape[0]
    tb = _rows(t)

    def body(c_ref, hab_ref, alog_ref, dtb_ref, dq_ref, dk_ref, dv_ref, dgb_ref,
             dc_ref, dhab_ref, dalog_ref, ddtb_ref):
        i = pl.program_id(0)
        _, vjp = jax.vjp(_dn_prep_fn, c_ref[...], hab_ref[...], alog_ref[...], dtb_ref[...])
        dc, dhab, dalog, ddtb = vjp((_from_heads(dq_ref), _from_heads(dk_ref), _from_heads(dv_ref), dgb_ref[...]))
        dc_ref[...] = dc
        dhab_ref[...] = dhab

        @pl.when(i == 0)
        def _():
            dalog_ref[...] = jnp.zeros_like(dalog_ref)
            ddtb_ref[...] = jnp.zeros_like(ddtb_ref)

        dalog_ref[...] += dalog
        ddtb_ref[...] += ddtb

    hm = pl.BlockSpec((N_HEADS, tb, HEAD_DIM), lambda i: (0, i, 0))
    wide = pl.BlockSpec((tb, 3 * D_MODEL), lambda i: (i, 0))
    nar = pl.BlockSpec((tb, LANES), lambda i: (i, 0))
    vec = pl.BlockSpec((1, LANES), lambda i: (0, 0))
    return pl.pallas_call(
        body, name="dn_prep_bwd", grid=(t // tb,),
        in_specs=[wide, nar, vec, vec, hm, hm, hm, nar],
        out_specs=[wide, nar, vec, vec],
        out_shape=[jax.ShapeDtypeStruct((t, 3 * D_MODEL), F32), jax.ShapeDtypeStruct((t, LANES), F32),
                   jax.ShapeDtypeStruct((1, LANES), F32), jax.ShapeDtypeStruct((1, LANES), F32)],
        compiler_params=_params(("arbitrary",)),
    )(c, hab, alog, dtb, dq, dk, dv, dgb)


DN_PREC = lax.Precision.HIGH
DN_GROUP = 8


def _dn_prec(a):
    return DN_PREC if a.dtype == F32 else None


def _bdot(a, b):
    return lax.dot_general(a, b, (((2,), (1,)), ((0,), (0,))), precision=_dn_prec(a), preferred_element_type=F32)


def _bdot_nt(a, b):
    return lax.dot_general(a, b, (((2,), (2,)), ((0,), (0,))), precision=_dn_prec(a), preferred_element_type=F32)


def _bdot_tn(a, b):
    return lax.dot_general(a, b, (((1,), (1,)), ((0,), (0,))), precision=_dn_prec(a), preferred_element_type=F32)


def _unit_lower_inverse(lmat):
    c = lmat.shape[-1]
    ri = lax.broadcasted_iota(jnp.int32, (c, c), 0)
    ci = lax.broadcasted_iota(jnp.int32, (c, c), 1)
    p = -lmat
    tinv = jnp.where(ri == ci, 1.0, 0.0) + p
    for _ in range(max(c.bit_length() - 2, 0)):
        p = _bdot(p, p)
        tinv = tinv + _bdot(tinv, p)
    return tinv


@jax.custom_vjp
def _solve_with(lmat, rhs, tinv):
    return _bdot(tinv, rhs)


def _solve_with_fwd(lmat, rhs, tinv):
    sol = _bdot(tinv, rhs)
    return sol, (sol, tinv)


def _solve_with_bwd(res, dsol):
    sol, tinv = res
    drhs = _bdot_tn(tinv, dsol)
    return -_bdot_nt(drhs, sol), drhs, jnp.zeros_like(tinv)


_solve_with.defvjp(_solve_with_fwd, _solve_with_bwd)


def _dn_local(q, k, v, grow, brow, tinv):
    g, c, _ = q.shape
    ri = lax.broadcasted_iota(jnp.int32, (c, c), 0)
    ci = lax.broadcasted_iota(jnp.int32, (c, c), 1)
    lower = ri >= ci
    as_col = lambda r: jnp.sum(jnp.where(ri == ci, jnp.broadcast_to(r, (g, c, c)), 0.0), axis=2, keepdims=True)
    gcol, bcol = as_col(grow), as_col(brow)
    gc_col = jnp.sum(jnp.where(lower, jnp.broadcast_to(grow, (g, c, c)), 0.0), axis=2, keepdims=True)
    gc_row = jnp.sum(jnp.where(ri <= ci, jnp.broadcast_to(gcol, (g, c, c)), 0.0), axis=1, keepdims=True)
    qs = q * (HEAD_DIM ** -0.5)
    kb = k * bcol
    vb = v * bcol
    decay = jnp.where(lower, jnp.exp(jnp.where(lower, gc_col - gc_row, 0.0)), 0.0)
    lmat = jnp.where(ri > ci, _bdot_nt(kb.astype(BF16), k.astype(BF16)) * decay, 0.0)
    eg = jnp.exp(gc_col)
    rhs = jnp.concatenate([vb, kb * eg], axis=2)
    if tinv is None:
        tinv = _unit_lower_inverse(lmat)
    sol = _solve_with(lmat, rhs, tinv)
    a_qk = jnp.where(lower, _bdot_nt(qs.astype(BF16), k.astype(BF16)) * decay, 0.0)
    g_last = jnp.sum(grow, axis=2, keepdims=True)
    kdec = k * jnp.exp(g_last - gc_col)
    egl = jnp.broadcast_to(jnp.exp(g_last), (g, 1, HEAD_DIM))
    return sol[:, :, :HEAD_DIM], sol[:, :, HEAD_DIM:], a_qk, qs * eg, kdec, egl, tinv


def _dn_seq(u, w, a_qk, qe, kdec, egl, s_in):
    b16 = lambda x: x.astype(BF16)
    v_new = u - _bdot(b16(w), b16(s_in))
    o = _bdot(b16(qe), b16(s_in)) + _bdot(b16(a_qk), b16(v_new))
    return o, s_in * egl + _bdot_tn(b16(kdec), b16(v_new))


def _dn_local_specs(t):
    grp = min(DN_GROUP, t // DN_CHUNK)
    rows = grp * DN_CHUNK
    blk = pl.BlockSpec((1, rows, HEAD_DIM), lambda h, i: (h, i, 0))
    row = pl.BlockSpec((1, grp, 1, DN_CHUNK), lambda h, i: (h, i, 0, 0))
    sq = pl.BlockSpec((1, grp, DN_CHUNK, DN_CHUNK), lambda h, i: (h, i, 0, 0))
    lane = pl.BlockSpec((1, grp, 1, HEAD_DIM), lambda h, i: (h, i, 0, 0))
    return grp, blk, row, sq, lane


def _dn_shapes(t):
    nchunk = t // DN_CHUNK
    big = jax.ShapeDtypeStruct((N_HEADS, t, HEAD_DIM), F32)
    row = jax.ShapeDtypeStruct((N_HEADS, nchunk, 1, DN_CHUNK), F32)
    sq = jax.ShapeDtypeStruct((N_HEADS, nchunk, DN_CHUNK, DN_CHUNK), F32)
    lane = jax.ShapeDtypeStruct((N_HEADS, nchunk, 1, HEAD_DIM), F32)
    return big, row, sq, lane


def _dn_local_fwd(q, k, v, grow, brow, wire=()):
    t = q.shape[1]
    grp, blk, row, sq, lane = _dn_local_specs(t)
    big, _, sqs, lanes = _dn_shapes(t)
    n = len(wire)
    groups = t // (grp * DN_CHUNK)
    steps = N_HEADS * groups

    def body(q_ref, k_ref, v_ref, gr_ref, br_ref, *rest):
        u_ref, w_ref, a_ref, qe_ref, kd_ref, egl_ref, t_ref = rest[n:n + 7]
        if n:
            begin, middle, end = _gather_protocol(rest[:n], rest[n + 7:2 * n + 7], *rest[2 * n + 7:])
            step = pl.program_id(0) * groups + pl.program_id(1)
            pl.when(step == 0)(begin)
            pl.when(step == (5 * steps) // 8)(middle)
        split = lambda r: r[0].reshape(grp, DN_CHUNK, HEAD_DIM)
        u, w, a_qk, qe, kdec, egl, tinv = _dn_local(split(q_ref), split(k_ref), split(v_ref), gr_ref[0],
                                                     br_ref[0], None)
        for ref, val in ((u_ref, u), (w_ref, w), (qe_ref, qe), (kd_ref, kdec)):
            ref[0] = val.reshape(grp * DN_CHUNK, HEAD_DIM)
        a_ref[0] = a_qk
        egl_ref[0] = egl
        t_ref[0] = tinv
        if n:
            pl.when(step == steps - 1)(end)

    assert n == 0 or steps >= 3
    return pl.pallas_call(
        body, name="dn_local_fwd", grid=(N_HEADS, groups),
        in_specs=[blk, blk, blk, row, row] + _hbm_specs(n),
        out_specs=[blk, blk, sq, blk, blk, lane, sq] + _hbm_specs(n),
        out_shape=[big, big, sqs, big, big, lanes, sqs] + _gather_out_shapes(wire),
        scratch_shapes=_gather_sems(n) if n else [],
        compiler_params=_params(("arbitrary", "arbitrary")),
    )(q, k, v, grow, brow, *wire)


def _dn_local_bwd(q, k, v, grow, brow, tinv, du, dw, da, dqe, dkd, degl, partials=()):
    t = q.shape[1]
    grp, blk, row, sq, lane = _dn_local_specs(t)
    big, rows_, _, _ = _dn_shapes(t)
    n = len(partials)
    groups = t // (grp * DN_CHUNK)
    steps = N_HEADS * groups

    def body(q_ref, k_ref, v_ref, gr_ref, br_ref, t_ref, du_ref, dw_ref, da_ref, dqe_ref, dkd_ref,
             degl_ref, *rest):
        dq_ref, dk_ref, dv_ref, dgr_ref, dbr_ref = rest[n:n + 5]
        if n:
            begin, end = _chip_exchange_protocol(rest[:n], rest[n + 5:2 * n + 5], *rest[2 * n + 5:])
            step = pl.program_id(0) * groups + pl.program_id(1)
            pl.when(step == 0)(begin)
        split = lambda r: r[0].reshape(grp, DN_CHUNK, HEAD_DIM)
        tinv_v = t_ref[0]
        fn = lambda q_, k_, v_, gr_, br_: _dn_local(q_, k_, v_, gr_, br_, tinv_v)[:6]
        _, vjp = jax.vjp(fn, split(q_ref), split(k_ref), split(v_ref), gr_ref[0], br_ref[0])
        dq, dk, dv, dgr, dbr = vjp((split(du_ref), split(dw_ref), da_ref[0], split(dqe_ref), split(dkd_ref),
                                    degl_ref[0]))
        for ref, val in ((dq_ref, dq), (dk_ref, dk), (dv_ref, dv)):
            ref[0] = val.reshape(grp * DN_CHUNK, HEAD_DIM)
        dgr_ref[0] = dgr
        dbr_ref[0] = dbr
        if n:
            pl.when(step == steps - 1)(end)

    assert n == 0 or steps >= 2
    return pl.pallas_call(
        body, name="dn_local_bwd", grid=(N_HEADS, groups),
        in_specs=[blk, blk, blk, row, row, sq, blk, blk, sq, blk, blk, lane] + _hbm_specs(n),
        out_specs=[blk, blk, blk, row, row] + _hbm_specs(n),
        out_shape=[big, big, big, rows_, rows_] + _chip_exchange_shapes(partials),
        scratch_shapes=_chip_exchange_sems(n) if n else [],
        compiler_params=_params(("arbitrary", "arbitrary")),
    )(q, k, v, grow, brow, tinv, du, dw, da, dqe, dkd, degl, *partials)


def _dn_seq_specs(nchunk, rev):
    def idx(n):
        return nchunk - 1 - n if rev else n

    blk = pl.BlockSpec((N_HEADS, DN_CHUNK, HEAD_DIM), lambda n: (0, idx(n), 0))
    sq = pl.BlockSpec((N_HEADS, 1, DN_CHUNK, DN_CHUNK), lambda n: (0, idx(n), 0, 0))
    lane = pl.BlockSpec((N_HEADS, 1, 1, HEAD_DIM), lambda n: (0, idx(n), 0, 0))
    st = pl.BlockSpec((N_HEADS, 1, HEAD_DIM, HEAD_DIM), lambda n: (0, idx(n), 0, 0))
    return blk, sq, lane, st


def _dn_seq_fwd(u, w, a_qk, qe, kdec, egl):
    t = u.shape[1]
    nchunk = t // DN_CHUNK
    blk, sq, lane, st = _dn_seq_specs(nchunk, False)

    def body(u_ref, w_ref, a_ref, qe_ref, kd_ref, egl_ref, o_ref, s_ref, state):
        @pl.when(pl.program_id(0) == 0)
        def _():
            state[...] = jnp.zeros_like(state)

        s_in = state[...]
        s_ref[:, 0] = s_in
        o, s_out = _dn_seq(u_ref[...], w_ref[...], a_ref[:, 0], qe_ref[...], kd_ref[...], egl_ref[:, 0], s_in)
        o_ref[...] = o
        state[...] = s_out

    return pl.pallas_call(
        body, name="dn_seq_fwd", grid=(nchunk,),
        in_specs=[blk, blk, sq, blk, blk, lane],
        out_specs=[blk, st],
        out_shape=[jax.ShapeDtypeStruct((N_HEADS, t, HEAD_DIM), F32),
                   jax.ShapeDtypeStruct((N_HEADS, nchunk, HEAD_DIM, HEAD_DIM), F32)],
        scratch_shapes=[pltpu.VMEM((N_HEADS, HEAD_DIM, HEAD_DIM), F32)],
        compiler_params=_params(("arbitrary",)),
    )(u, w, a_qk, qe, kdec, egl)


def _dn_seq_bwd(u, w, a_qk, qe, kdec, egl, states, do):
    t = u.shape[1]
    nchunk = t // DN_CHUNK
    blk, sq, lane, st = _dn_seq_specs(nchunk, True)
    big, _, sqs, lanes = _dn_shapes(t)

    def body(u_ref, w_ref, a_ref, qe_ref, kd_ref, egl_ref, s_ref, do_ref,
             du_ref, dw_ref, da_ref, dqe_ref, dkd_ref, degl_ref, dstate):
        @pl.when(pl.program_id(0) == 0)
        def _():
            dstate[...] = jnp.zeros_like(dstate)

        _, vjp = jax.vjp(_dn_seq, u_ref[...], w_ref[...], a_ref[:, 0], qe_ref[...], kd_ref[...], egl_ref[:, 0],
                         s_ref[:, 0])
        du, dw, da, dqe, dkd, degl, ds = vjp((do_ref[...], dstate[...]))
        du_ref[...] = du
        dw_ref[...] = dw
        da_ref[:, 0] = da
        dqe_ref[...] = dqe
        dkd_ref[...] = dkd
        degl_ref[:, 0] = degl
        dstate[...] = ds

    return pl.pallas_call(
        body, name="dn_seq_bwd", grid=(nchunk,),
        in_specs=[blk, blk, sq, blk, blk, lane, st, blk],
        out_specs=[blk, blk, sq, blk, blk, lane],
        out_shape=[big, big, sqs, big, big, lanes],
        scratch_shapes=[pltpu.VMEM((N_HEADS, HEAD_DIM, HEAD_DIM), F32)],
        compiler_params=_params(("arbitrary",)),
    )(u, w, a_qk, qe, kdec, egl, states, do)


def _dn_post_fn(o, gate, w):
    outs = []
    for h in range(N_HEADS):
        sl = slice(h * HEAD_DIM, (h + 1) * HEAD_DIM)
        outs.append(_rms(o[:, sl], w) * _silu(gate[:, sl]))
    return jnp.concatenate(outs, axis=1)


def _dn_post_fwd(o, gate, w):
    t = gate.shape[0]
    tb = _rows(t)

    def body(o_ref, g_ref, w_ref, y_ref):
        y_ref[...] = _dn_post_fn(_from_heads(o_ref), g_ref[...], w_ref[...]).astype(BF16)

    row = pl.BlockSpec((tb, D_MODEL), lambda i: (i, 0))
    hm = pl.BlockSpec((N_HEADS, tb, HEAD_DIM), lambda i: (0, i, 0))
    return pl.pallas_call(
        body, name="dn_post_fwd", grid=(t // tb,),
        in_specs=[hm, row, pl.BlockSpec((1, HEAD_DIM), lambda i: (0, 0))],
        out_specs=row, out_shape=jax.ShapeDtypeStruct((t, D_MODEL), BF16),
        compiler_params=_params(("parallel",)),
    )(o, gate, w)


def _dn_post_bwd(o, gate, w, dy):
    t = gate.shape[0]
    tb = _rows(t)

    def body(o_ref, g_ref, w_ref, dy_ref, do_ref, dg_ref, dw_ref):
        i = pl.program_id(0)
        _, vjp = jax.vjp(_dn_post_fn, _from_heads(o_ref), g_ref[...], w_ref[...])
        do, dg, dw = vjp(dy_ref[...])
        _to_heads(do_ref, do)
        dg_ref[...] = dg.astype(BF16)

        @pl.when(i == 0)
        def _():
            dw_ref[...] = jnp.zeros_like(dw_ref)

        dw_ref[...] += dw

    row = pl.BlockSpec((tb, D_MODEL), lambda i: (i, 0))
    hm = pl.BlockSpec((N_HEADS, tb, HEAD_DIM), lambda i: (0, i, 0))
    vec = pl.BlockSpec((1, HEAD_DIM), lambda i: (0, 0))
    return pl.pallas_call(
        body, name="dn_post_bwd", grid=(t // tb,),
        in_specs=[hm, row, vec, row],
        out_specs=[hm, row, vec],
        out_shape=[jax.ShapeDtypeStruct((N_HEADS, t, HEAD_DIM), F32), jax.ShapeDtypeStruct((t, D_MODEL), BF16),
                   jax.ShapeDtypeStruct((1, HEAD_DIM), F32)],
        compiler_params=_params(("arbitrary",)),
    )(o, gate, w, dy)


def _split_bf16(x):
    hi = x.astype(BF16)
    lo = (x - hi.astype(F32)).astype(BF16)
    return hi, lo


SB_Q_BLOCK = 512
SB_K_BLOCK = 256
SB_NEGLIGIBLE = -60.0


def _sb_logits(q, kb, mask, scale):
    z = _dot_nt(q, kb) * scale
    ls = jnp.minimum(z, 0.0) - jnp.log(1.0 + jnp.exp(-jnp.abs(z)))
    lk = ls - z
    if mask is not None:
        lk = jnp.where(mask, lk, 0.0)
    return ls, lk


def _sb_blocks(t):
    bq = min(SB_Q_BLOCK, t)
    bk = min(SB_K_BLOCK, bq)
    return bq, bk, bq // bk


def _sb_fwd(qkv):
    t = qkv.shape[0]
    bq, bk, nd = _sb_blocks(t)
    scale = HEAD_DIM ** -0.5

    def body(q_ref, k_ref, v_ref, o_ref, tot_ref, used_ref):
        i = pl.program_id(1)
        q = q_ref[...]
        rj = lax.broadcasted_iota(jnp.int32, (bk, bk), 0)
        cj = lax.broadcasted_iota(jnp.int32, (bk, bk), 1)
        after = (rj > cj).astype(BF16)
        trow = lax.broadcasted_iota(jnp.int32, (bq, bk), 0)
        scol = lax.broadcasted_iota(jnp.int32, (bq, bk), 1)

        def tile(j, run, acc, mask):
            off = pl.multiple_of(j * bk, bk)
            kb = k_ref[pl.ds(off, bk), :]
            vb = v_ref[pl.ds(off, bk), :]
            ls, lk = _sb_logits(q, kb, mask, scale)
            hi, lo = _split_bf16(lk)
            between = _dot(hi, after) + _dot(lo, after) + run
            a = jnp.exp(ls + between)
            if mask is not None:
                a = jnp.where(mask, a, 0.0)
            acc = acc + _dot(a.astype(BF16), vb)
            return run + jnp.sum(lk, axis=1, keepdims=True), acc

        run, acc = jnp.zeros((bq, 1), F32), jnp.zeros((bq, HEAD_DIM), F32)
        for d in reversed(range(nd)):
            run, acc = tile(i * nd + d, run, acc, scol + d * bk < trow)
        def more(c):
            return jnp.logical_and(c[0] < i * nd, jnp.max(c[1]) > SB_NEGLIGIBLE)

        def far(c):
            run_, acc_ = tile(i * nd - 1 - c[0], c[1], c[2], None)
            return c[0] + 1, run_, acc_

        used, run, acc = lax.while_loop(more, far, (jnp.int32(0), run, acc))
        o_ref[...] = acc.astype(BF16)
        tot_ref[...] = jnp.broadcast_to(run, (bq, HEAD_DIM))
        used_ref[...] = jnp.full(used_ref.shape, used, F32)

    qs = pl.BlockSpec((bq, HEAD_DIM), lambda h, i: (i, h))
    ks = pl.BlockSpec((t, HEAD_DIM), lambda h, i: (0, N_HEADS + h))
    vs = pl.BlockSpec((t, HEAD_DIM), lambda h, i: (0, 2 * N_HEADS + h))
    return pl.pallas_call(
        body, name="sb_fwd", grid=(N_HEADS, t // bq),
        in_specs=[qs, ks, vs], out_specs=[qs, qs, pl.BlockSpec((1, 1, 1, LANES), lambda h, i: (h, i, 0, 0))],
        out_shape=[jax.ShapeDtypeStruct((t, D_MODEL), BF16), jax.ShapeDtypeStruct((t, D_MODEL), F32),
                   jax.ShapeDtypeStruct((N_HEADS, t // bq, 1, LANES), F32)],
        compiler_params=_params(("parallel", "arbitrary")),
    )(qkv, qkv, qkv)


def _sb_bwd(qkv, tot, used, do):
    t = qkv.shape[0]
    bq, bk, nd = _sb_blocks(t)
    scale = HEAD_DIM ** -0.5

    def body(q_ref, k_ref, v_ref, tot_ref, used_ref, do_ref, dq_ref, dk_ref, dv_ref):
        i = pl.program_id(1)

        @pl.when(i == 0)
        def _():
            dk_ref[...] = jnp.zeros_like(dk_ref)
            dv_ref[...] = jnp.zeros_like(dv_ref)

        q = q_ref[...]
        do = do_ref[...]
        total = tot_ref[:, 0:1]
        rj = lax.broadcasted_iota(jnp.int32, (bk, bk), 0)
        cj = lax.broadcasted_iota(jnp.int32, (bk, bk), 1)
        upto = (rj <= cj).astype(BF16)
        before = (rj < cj).astype(BF16)
        trow = lax.broadcasted_iota(jnp.int32, (bq, bk), 0)
        scol = lax.broadcasted_iota(jnp.int32, (bq, bk), 1)

        def tile(j, run_k, run_e, dq, mask):
            off = pl.multiple_of(j * bk, bk)
            kb = k_ref[pl.ds(off, bk), :]
            vb = v_ref[pl.ds(off, bk), :]
            ls, lk = _sb_logits(q, kb, mask, scale)
            hi, lo = _split_bf16(lk)
            between = total - (_dot(hi, upto) + _dot(lo, upto) + run_k)
            a = jnp.exp(ls + between)
            if mask is not None:
                a = jnp.where(mask, a, 0.0)
            e = a * _dot_nt(do, vb)
            ehi, elo = _split_bf16(e)
            pre = _dot(ehi, before) + _dot(elo, before) + run_e
            sig = jnp.exp(ls)
            dz = e * (1.0 - sig) - pre * sig
            if mask is not None:
                dz = jnp.where(mask, dz, 0.0)
            dz = (dz * scale).astype(BF16)
            dq = dq + _dot(dz, kb)
            dk_ref[pl.ds(off, bk), :] += _dot_tn(dz, q)
            dv_ref[pl.ds(off, bk), :] += _dot_tn(a.astype(BF16), do)
            return (run_k + jnp.sum(lk, axis=1, keepdims=True),
                    run_e + jnp.sum(e, axis=1, keepdims=True), dq)

        zero = jnp.zeros((bq, 1), F32)
        visited = jnp.clip(jnp.max(used_ref[...]).astype(jnp.int32), 0, i * nd)
        carry = lax.fori_loop(i * nd - visited, i * nd, lambda j, c: tile(j, c[0], c[1], c[2], None),
                              (zero, zero, jnp.zeros((bq, HEAD_DIM), F32)))
        for d in range(nd):
            carry = tile(i * nd + d, *carry, scol + d * bk < trow)
        dq_ref[...] = carry[2]

    qs = pl.BlockSpec((bq, HEAD_DIM), lambda h, i: (i, h))
    ks = pl.BlockSpec((t, HEAD_DIM), lambda h, i: (0, N_HEADS + h))
    vs = pl.BlockSpec((t, HEAD_DIM), lambda h, i: (0, 2 * N_HEADS + h))
    full = pl.BlockSpec((t, HEAD_DIM), lambda h, i: (0, h))
    big = jax.ShapeDtypeStruct((t, D_MODEL), F32)
    return pl.pallas_call(
        body, name="sb_bwd", grid=(N_HEADS, t // bq),
        in_specs=[qs, ks, vs, qs, pl.BlockSpec((1, 1, 1, LANES), lambda h, i: (h, i, 0, 0)), qs],
        out_specs=[qs, full, full],
        out_shape=[big, big, big],
        compiler_params=_params(("parallel", "arbitrary")),
    )(qkv, qkv, qkv, tot, used, do)


def _merge_fwd(o_dn, o_sb, gl, x, wp_dn, wp_sb, w_out, w2):
    t = x.shape[0]
    tb = _rows(t)

    def body(odn_ref, osb_ref, gl_ref, x_ref, wpd_ref, wps_ref, wo_ref, w2_ref,
             pdn_ref, psb_ref, mix_ref, x1_ref, n2_ref):
        pdn = _dot(odn_ref[...], wpd_ref[...])
        psb = _dot(osb_ref[...], wps_ref[...])
        gates = jax.nn.sigmoid(gl_ref[...])
        mixed = (gates[:, :D_MODEL] * pdn + gates[:, D_MODEL:] * psb).astype(BF16)
        x1 = x_ref[...] + _dot(mixed, wo_ref[...])
        pdn_ref[...] = pdn
        psb_ref[...] = psb
        mix_ref[...] = mixed
        x1_ref[...] = x1
        n2_ref[...] = _rms(x1, w2_ref[...]).astype(BF16)

    row = pl.BlockSpec((tb, D_MODEL), lambda i: (i, 0))
    sq = pl.BlockSpec((D_MODEL, D_MODEL), lambda i: (0, 0))
    f = jax.ShapeDtypeStruct((t, D_MODEL), F32)
    b = jax.ShapeDtypeStruct((t, D_MODEL), BF16)
    return pl.pallas_call(
        body, name="merge_fwd", grid=(t // tb,),
        in_specs=[row, row, pl.BlockSpec((tb, 2 * D_MODEL), lambda i: (i, 0)), row, sq, sq, sq,
                  pl.BlockSpec((1, D_MODEL), lambda i: (0, 0))],
        out_specs=[row] * 5, out_shape=[f, f, b, f, b],
        compiler_params=_params(("parallel",)),
    )(o_dn, o_sb, gl, x, wp_dn, wp_sb, w_out, w2)


def _merge_bwd(dx2, dn2, x1, w2, gl, pdn, psb, wp_dn, wp_sb, w_out):
    t = x1.shape[0]
    tb = _rows(t)

    def body(dx2_ref, dn2_ref, x1_ref, w2_ref, gl_ref, pdn_ref, psb_ref, wpd_ref, wps_ref, wo_ref,
             dx1_ref, dw2_ref, dgl_ref, dpdn_ref, dpsb_ref, dodn_ref, dosb_ref):
        i = pl.program_id(0)
        _, vjp = jax.vjp(_rms, x1_ref[...], w2_ref[...])
        dxn, dw2 = vjp(dn2_ref[...])
        dx1 = dx2_ref[...] + dxn
        dx1_ref[...] = dx1

        @pl.when(i == 0)
        def _():
            dw2_ref[...] = jnp.zeros_like(dw2_ref)

        dw2_ref[...] += dw2
        dmix = _dot_nt(dx1.astype(BF16), wo_ref[...])
        gates = jax.nn.sigmoid(gl_ref[...])
        g_dn, g_sb = gates[:, :D_MODEL], gates[:, D_MODEL:]
        dpdn = (dmix * g_dn).astype(BF16)
        dpsb = (dmix * g_sb).astype(BF16)
        dgl_ref[:, :D_MODEL] = (dmix * pdn_ref[...] * g_dn * (1.0 - g_dn)).astype(BF16)
        dgl_ref[:, D_MODEL:] = (dmix * psb_ref[...] * g_sb * (1.0 - g_sb)).astype(BF16)
        dpdn_ref[...] = dpdn
        dpsb_ref[...] = dpsb
        dodn_ref[...] = _dot_nt(dpdn, wpd_ref[...])
        dosb_ref[...] = _dot_nt(dpsb, wps_ref[...]).astype(BF16)

    row = pl.BlockSpec((tb, D_MODEL), lambda i: (i, 0))
    wide = pl.BlockSpec((tb, 2 * D_MODEL), lambda i: (i, 0))
    sq = pl.BlockSpec((D_MODEL, D_MODEL), lambda i: (0, 0))
    vec = pl.BlockSpec((1, D_MODEL), lambda i: (0, 0))
    f = jax.ShapeDtypeStruct((t, D_MODEL), F32)
    b = jax.ShapeDtypeStruct((t, D_MODEL), BF16)
    return pl.pallas_call(
        body, name="merge_bwd", grid=(t // tb,),
        in_specs=[row, row, row, vec, wide, row, row, sq, sq, sq],
        out_specs=[row, vec, wide, row, row, row, row],
        out_shape=[f, jax.ShapeDtypeStruct((1, D_MODEL), F32), jax.ShapeDtypeStruct((t, 2 * D_MODEL), BF16),
                   b, b, f, b],
        compiler_params=_params(("arbitrary",)),
    )(dx2, dn2, x1, w2, gl, pdn, psb, wp_dn, wp_sb, w_out)


def _conv_taps(buf, w_ref, first, rows, cols=slice(None)):
    y = w_ref[0:1, cols] * buf[pl.ds(first, rows), cols]
    for s in range(1, w_ref.shape[0]):
        y = y + w_ref[s:s + 1, cols] * buf[pl.ds(first + s, rows), cols]
    return y


def _ffn_mid_fwd(pre_g, pre_u, wg, wu):
    t, c = pre_g.shape
    kk = wg.shape[0]
    tb, cb = _rows(t), _pick(c, ELEMENTWISE_COLS)
    per = tb // HALO

    def body(g_ref, gh_ref, u_ref, uh_ref, wg_ref, wu_ref, a_ref, gbuf, ubuf):
        i = pl.program_id(0)
        for buf, ref, halo in ((gbuf, g_ref, gh_ref), (ubuf, u_ref, uh_ref)):
            buf[pl.ds(HALO, tb), :] = ref[...]
            buf[pl.ds(0, HALO), :] = jnp.where(i == 0, 0.0, halo[...])
        for j in range(cb // LANES):
            sl = pl.ds(j * LANES, LANES)
            ug = _conv_taps(gbuf, wg_ref, HALO - (kk - 1), tb, sl)
            uu = _conv_taps(ubuf, wu_ref, HALO - (kk - 1), tb, sl)
            a_ref[:, sl] = (_silu(ug) * uu).astype(BF16)

    blk = pl.BlockSpec((tb, cb), lambda i, j: (i, j))
    halo = pl.BlockSpec((HALO, cb), lambda i, j: (jnp.maximum(i * per - 1, 0), j))
    wspec = pl.BlockSpec((kk, cb), lambda i, j: (0, j))
    return pl.pallas_call(
        body, name="ffn_mid_fwd", grid=(t // tb, c // cb),
        in_specs=[blk, halo, blk, halo, wspec, wspec], out_specs=blk,
        out_shape=jax.ShapeDtypeStruct((t, c), BF16),
        scratch_shapes=[pltpu.VMEM((tb + HALO, cb), F32)] * 2,
        compiler_params=_params(("parallel", "parallel")),
    )(pre_g, pre_g, pre_u, pre_u, wg, wu)


def _ffn_mid_bwd(pre_g, pre_u, wg, wu, da):
    t, c = pre_g.shape
    kk = wg.shape[0]
    tb, cb = _rows(t), _pick(c, ELEMENTWISE_COLS)
    per = tb // HALO
    nblk = t // tb
    ext = tb + HALO

    def body(g_ref, gb_ref, ga_ref, u_ref, ub_ref, ua_ref, da_ref, daa_ref, wg_ref, wu_ref,
             dg_ref, du_ref, dwg_ref, dwu_ref, gbuf, ubuf, dabuf, dgbuf, dubuf):
        i = pl.program_id(1)
        last = i == nblk - 1
        for buf, ref, before, after in ((gbuf, g_ref, gb_ref, ga_ref), (ubuf, u_ref, ub_ref, ua_ref)):
            buf[pl.ds(0, HALO), :] = jnp.where(i == 0, 0.0, before[...])
            buf[pl.ds(HALO, tb), :] = ref[...]
            buf[pl.ds(HALO + tb, HALO), :] = jnp.where(last, 0.0, after[...])
        dabuf[pl.ds(0, tb), :] = da_ref[...]
        dabuf[pl.ds(tb, HALO), :] = jnp.where(last, 0.0, daa_ref[...])

        @pl.when(i == 0)
        def _():
            dwg_ref[...] = jnp.zeros_like(dwg_ref)
            dwu_ref[...] = jnp.zeros_like(dwu_ref)

        for j in range(cb // LANES):
            sl = pl.ds(j * LANES, LANES)
            ug = _conv_taps(gbuf, wg_ref, HALO - (kk - 1), ext, sl)
            uu = _conv_taps(ubuf, wu_ref, HALO - (kk - 1), ext, sl)
            _, vjp = jax.vjp(lambda g, u: _silu(g) * u, ug, uu)
            dgbuf[:, sl], dubuf[:, sl] = vjp(dabuf[:, sl])
            for dbuf, xbuf, w_ref, dx_ref, dw_ref in ((dgbuf, gbuf, wg_ref, dg_ref, dwg_ref),
                                                      (dubuf, ubuf, wu_ref, du_ref, dwu_ref)):
                x = xbuf[pl.ds(HALO, tb), sl]
                dx = None
                for s in range(kk):
                    shifted = dbuf[pl.ds(kk - 1 - s, tb), sl]
                    term = w_ref[s:s + 1, sl] * shifted
                    dx = term if dx is None else dx + term
                    dw_ref[s:s + 1, sl] += jnp.sum(shifted * x, axis=0, keepdims=True)
                dx_ref[:, sl] = dx.astype(BF16)

    blk = pl.BlockSpec((tb, cb), lambda j, i: (i, j))
    before = pl.BlockSpec((HALO, cb), lambda j, i: (jnp.maximum(i * per - 1, 0), j))
    after = pl.BlockSpec((HALO, cb), lambda j, i: (jnp.minimum((i + 1) * per, t // HALO - 1), j))
    wspec = pl.BlockSpec((kk, cb), lambda j, i: (0, j))
    dwspec = pl.BlockSpec((HALO, cb), lambda j, i: (0, j))
    half = jax.ShapeDtypeStruct((t, c), BF16)
    dwshape = jax.ShapeDtypeStruct((HALO, c), F32)
    return pl.pallas_call(
        body, name="ffn_mid_bwd", grid=(c // cb, nblk),
        in_specs=[blk, before, after, blk, before, after, blk, after, wspec, wspec],
        out_specs=[blk, blk, dwspec, dwspec],
        out_shape=[half, half, dwshape, dwshape],
        scratch_shapes=[pltpu.VMEM((ext + HALO, cb), F32)] * 2 + [pltpu.VMEM((ext, cb), F32)] * 3,
        compiler_params=_params(("parallel", "arbitrary")),
    )(pre_g, pre_g, pre_g, pre_u, pre_u, pre_u, da, da, wg, wu)


def _down_loss(a, w_down, x1, wf, target):
    t = x1.shape[0]
    tb = _rows(t)

    def body(a_ref, wd_ref, x1_ref, wf_ref, tgt_ref, dx2_ref, dwf_ref, loss_ref):
        i = pl.program_id(0)
        x2 = x1_ref[...] + _dot(a_ref[...], wd_ref[...])
        y, vjp = jax.vjp(_rms, x2, wf_ref[...])
        err = y - tgt_ref[...]
        dx2, dwf = vjp(err * (1.0 / D_MODEL))
        dx2_ref[...] = dx2
        part = jnp.sum(jnp.sum(err * err, axis=1, keepdims=True), axis=0, keepdims=True) * (0.5 / D_MODEL)

        @pl.when(i == 0)
        def _():
            dwf_ref[...] = jnp.zeros_like(dwf_ref)
            loss_ref[...] = jnp.zeros_like(loss_ref)

        dwf_ref[...] += dwf
        loss_ref[...] += jnp.broadcast_to(part, loss_ref.shape)

    row = pl.BlockSpec((tb, D_MODEL), lambda i: (i, 0))
    vec = pl.BlockSpec((1, D_MODEL), lambda i: (0, 0))
    return pl.pallas_call(
        body, name="down_loss", grid=(t // tb,),
        in_specs=[pl.BlockSpec((tb, D_FF), lambda i: (i, 0)), pl.BlockSpec((D_FF, D_MODEL), lambda i: (0, 0)),
                  row, vec, row],
        out_specs=[row, vec, pl.BlockSpec((1, LANES), lambda i: (0, 0))],
        out_shape=[jax.ShapeDtypeStruct((t, D_MODEL), F32), jax.ShapeDtypeStruct((1, D_MODEL), F32),
                   jax.ShapeDtypeStruct((1, LANES), F32)],
        compiler_params=_params(("arbitrary",)),
    )(a, w_down, x1, wf, target)


def _local_step(x, target, wts, late_wire=(), late_weights=None, early_partials=None):
    t = x.shape[0]
    nchunk = t // DN_CHUNK

    n1, hab = _norm1_fwd(x, wts["norm1"], wts["w_ab"])
    dnqkv = _mm(n1, wts["w_dnqkv"], name="h_dnqkv")
    dngate = _mm(n1, wts["w_dngate"], name="h_dngate")
    sbqkv = _mm(n1, wts["w_sbqkv"], out_dtype=BF16, name="h_sbqkv")
    gl = _mm(n1, wts["w_gl"], name="h_gl")

    cdn = _conv_fwd(dnqkv, wts["dn_conv"], "dn_conv_fwd")
    qn, kn, vv, gb = _dn_prep_fwd(cdn, hab, wts["alog"], wts["dtb"])
    per_head = gb[:, :2 * N_HEADS].T.reshape(2 * N_HEADS, nchunk, DN_CHUNK)
    grow, brow = per_head[:N_HEADS, :, None, :], per_head[N_HEADS:, :, None, :]
    u_dn, w_dn, a_qk, qe, kdec, egl, tinv, *late = _dn_local_fwd(qn, kn, vv, grow, brow, late_wire)
    if late_wire:
        wts = {**wts, **late_weights(late)}
    o_raw, states = _dn_seq_fwd(u_dn, w_dn, a_qk, qe, kdec, egl)
    o_dn = _dn_post_fwd(o_raw, dngate, wts["dn_norm"])

    o_sb, tot, sb_used = _sb_fwd(sbqkv)

    pdn, psb, mixed, x1, n2 = _merge_fwd(o_dn, o_sb, gl, x, wts["wp_dn"], wts["wp_sb"], wts["w_out"],
                                         wts["norm2"])
    pre_g = _mm(n2, wts["w_up_g"], name="ffn_up_g")
    pre_u = _mm(n2, wts["w_up_u"], name="ffn_up_u")
    act = _ffn_mid_fwd(pre_g, pre_u, wts["ffn_conv_g"], wts["ffn_conv_u"])
    dx2, d_normf, loss_part = _down_loss(act, wts["w_down"], x1, wts["normf"], target)

    grads = {"normf": d_normf}
    da = _mm(dx2, wts["w_down"], tb=True, name="d_act")
    grads["w_down"] = _mm(act, dx2, ta=True, out_dtype=BF16, name="dw_down")
    dpre_g, dpre_u, dcw_g, dcw_u = _ffn_mid_bwd(pre_g, pre_u, wts["ffn_conv_g"], wts["ffn_conv_u"], da)
    grads["ffn_conv"] = jnp.concatenate([dcw_g[:FFN_CONV], dcw_u[:FFN_CONV]], axis=1)
    dn2 = _mm(dpre_g, wts["w_up_g"], tb=True, name="dn2_g")
    dn2 = _mm(dpre_u, wts["w_up_u"], tb=True, add=dn2, name="dn2_u")
    grads["w_up_g"] = _mm(n2, dpre_g, ta=True, out_dtype=BF16, name="dw_up_g")
    grads["w_up_u"] = _mm(n2, dpre_u, ta=True, out_dtype=BF16, name="dw_up_u")

    dx1, grads["norm2"], dgl, dpdn, dpsb, do_dn, do_sb = _merge_bwd(
        dx2, dn2, x1, wts["norm2"], gl, pdn, psb, wts["wp_dn"], wts["wp_sb"], wts["w_out"])
    grads["w_out"] = _mm(mixed, dx1, ta=True, out_dtype=BF16, name="dw_out")
    grads["wp_dn"] = _mm(o_dn, dpdn, ta=True, out_dtype=BF16, name="dw_proj_dn")
    grads["wp_sb"] = _mm(o_sb, dpsb, ta=True, out_dtype=BF16, name="dw_proj_sb")

    partials = early_partials(grads) if early_partials else ()
    dsq, dsk, dsv = _sb_bwd(sbqkv, tot, sb_used, do_sb)
    dsbqkv = jnp.concatenate([dsq, dsk, dsv], axis=1).astype(BF16)

    do_raw, ddngate, grads["dn_norm"] = _dn_post_bwd(o_raw, dngate, wts["dn_norm"], do_dn)
    seq_grads = _dn_seq_bwd(u_dn, w_dn, a_qk, qe, kdec, egl, states, do_raw)
    dqn, dkn, dvv, dgrow, dbrow, *arrived = _dn_local_bwd(qn, kn, vv, grow, brow, tinv, *seq_grads,
                                                          partials=partials)
    dgb = jnp.concatenate([dgrow.reshape(N_HEADS, t), dbrow.reshape(N_HEADS, t)], axis=0).T
    dgb = jnp.pad(dgb, ((0, 0), (0, LANES - 2 * N_HEADS)))
    dcdn, dhab, grads["alog"], grads["dtb"] = _dn_prep_bwd(cdn, hab, wts["alog"], wts["dtb"], dqn, dkn, dvv, dgb)
    ddnqkv, dcw_dn = _conv_bwd(dcdn, dnqkv, wts["dn_conv"], "dn_conv_bwd", BF16)
    grads["dn_conv"] = dcw_dn[:DN_CONV]

    dn1 = _mm(ddnqkv, wts["w_dnqkv"], tb=True, name="dn1_dnqkv")
    dn1 = _mm(ddngate, wts["w_dngate"], tb=True, add=dn1, name="dn1_dngate")
    dn1 = _mm(dsbqkv, wts["w_sbqkv"], tb=True, add=dn1, name="dn1_sbqkv")
    dn1 = _mm(dgl, wts["w_gl"], tb=True, add=dn1, name="dn1_gl")
    grads["w_dnqkv"] = _mm(n1, ddnqkv, ta=True, out_dtype=BF16, name="dw_dnqkv")
    grads["w_dngate"] = _mm(n1, ddngate, ta=True, out_dtype=BF16, name="dw_dngate")
    grads["w_sbqkv"] = _mm(n1, dsbqkv, ta=True, out_dtype=BF16, name="dw_sbqkv")
    grads["w_gl"] = _mm(n1, dgl, ta=True, out_dtype=BF16, name="dw_gl")
    grads["w_ab"] = _mm(n1, dhab, ta=True, out_dtype=BF16, name="dw_ab")
    grad_x, grads["norm1"] = _norm1_bwd(x, wts["norm1"], dn1, dx1, dhab, wts["w_ab"])
    return loss_part, grad_x, grads, (list(partials), arrived)


def _place():
    return lax.axis_index("x"), lax.axis_index("y"), lax.axis_index("c")


def _hbm_specs(n):
    return [pl.BlockSpec(memory_space=pltpu.HBM)] * n


GATHER_SEMS = 8


def _gather_protocol(ins, outs, send_sems, recv_sems):
    n = len(ins)
    x, y, c = _place()
    me = 2 * x + y
    sibling = (x, y, 1 - c)
    xn, yn, dg = (1 - x, y), (x, 1 - y), (1 - x, 1 - y)
    idx = lambda chip: 2 * chip[0] + chip[1]

    def part(a, chip_index, core, quarter=None):
        half = ins[a].shape[0] // 2
        if quarter is None:
            return outs[a].at[chip_index, pl.ds(core * half, half), :]
        return outs[a].at[chip_index, pl.ds(core * half + quarter * (half // 2), half // 2), :]

    def copy(a, k, src, dst, to):
        return pltpu.make_async_remote_copy(src_ref=src, dst_ref=dst, send_sem=send_sems.at[GATHER_SEMS * a + k],
                                            recv_sem=recv_sems.at[GATHER_SEMS * a + k], device_id=to,
                                            device_id_type=MESH)

    def sent(a, k):
        half = ins[a].shape[0] // 2
        my_half = ins[a].at[pl.ds(c * half, half), :]
        if k < 2:
            return copy(a, k, my_half, part(a, me, c), (*(xn, yn)[k], c))
        if k < 4:
            src = part(a, idx((xn, yn)[k - 2]), c, k - 2)
            return copy(a, k, src, src, (*(yn, xn)[k - 2], c))
        src = (part(a, idx(xn), c), part(a, idx(yn), c), part(a, idx(dg), c, 0), part(a, idx(dg), c, 1))[k - 4]
        return copy(a, k, src, src, sibling)

    def landed(a, k):
        dst = (part(a, idx(xn), c), part(a, idx(yn), c), part(a, idx(dg), c, 0), part(a, idx(dg), c, 1),
               part(a, idx(xn), 1 - c), part(a, idx(yn), 1 - c), part(a, idx(dg), 1 - c, 0),
               part(a, idx(dg), 1 - c, 1))[k]
        return copy(a, k, dst, dst, sibling)

    def begin():
        for a in range(n):
            sent(a, 0).start()
            sent(a, 1).start()

    def middle():
        for a in range(n):
            for k in range(2):
                landed(a, k).wait_recv()
                sent(a, 2 + k).start()
                sent(a, 4 + k).start()

    def end():
        for a in range(n):
            for k in (2, 3):
                landed(a, k).wait_recv()
                sent(a, 4 + k).start()
        for a in range(n):
            for k in range(4, GATHER_SEMS):
                landed(a, k).wait_recv()
        for a in range(n):
            for k in range(GATHER_SEMS):
                sent(a, k).wait_send()

    return begin, middle, end


def _gather_out_shapes(shards):
    return [jax.ShapeDtypeStruct((N_CHIPS,) + s.shape, s.dtype) for s in shards]


def _gather_sems(n):
    return [pltpu.SemaphoreType.DMA((GATHER_SEMS * n,)), pltpu.SemaphoreType.DMA((GATHER_SEMS * n,))]


def _gather_shards(shards):
    n = len(shards)

    def body(*refs):
        begin, middle, end = _gather_protocol(refs[:n], refs[n:2 * n], *refs[2 * n:])
        begin()
        middle()
        end()

    return pl.pallas_call(
        body, name="gather_weights", in_specs=_hbm_specs(n), out_specs=_hbm_specs(n),
        out_shape=_gather_out_shapes(shards), scratch_shapes=_gather_sems(n),
    )(*shards)


def _pair_exchange_halves(gs, tag):
    n = len(gs)

    def body(*refs):
        ins, outs, (send_sems, recv_sems) = refs[:n], refs[n:2 * n], refs[2 * n:]
        x, y, c = _place()
        cps = []
        for a in range(n):
            half = ins[a].shape[1] // 2
            cp = pltpu.make_async_remote_copy(src_ref=ins[a].at[:, pl.ds((1 - c) * half, half), :], dst_ref=outs[a],
                                              send_sem=send_sems.at[a], recv_sem=recv_sems.at[a],
                                              device_id=(x, y, 1 - c), device_id_type=MESH)
            cp.start()
            cps.append(cp)
        for cp in cps:
            cp.wait()

    return pl.pallas_call(
        body, name="grad_pair_exchange_" + tag, in_specs=_hbm_specs(n), out_specs=_hbm_specs(n),
        out_shape=[jax.ShapeDtypeStruct((g.shape[0], g.shape[1] // 2, g.shape[2]), g.dtype) for g in gs],
        scratch_shapes=[pltpu.SemaphoreType.DMA((n,)), pltpu.SemaphoreType.DMA((n,))],
    )(*gs)


def _pick_rows(n, target=1024):
    best = 16
    for b in range(16, min(n, target) + 1, 16):
        if n % b == 0:
            best = b
    return best


def _pair_add(g, got, c_idx, tag):
    nsh, rows, cols = g.shape
    half = rows // 2
    rb = _pick_rows(half)

    def body(c_ref, g_ref, got_ref, o_ref):
        o_ref[...] = (g_ref[...].astype(F32) + got_ref[...].astype(F32)).astype(BF16)

    nb = half // rb
    grid_spec = pltpu.PrefetchScalarGridSpec(
        num_scalar_prefetch=1, grid=(nsh, nb),
        in_specs=[pl.BlockSpec((1, rb, cols), lambda s, i, c_ref: (s, c_ref[0] * nb + i, 0)),
                  pl.BlockSpec((1, rb, cols), lambda s, i, c_ref: (s, i, 0))],
        out_specs=pl.BlockSpec((1, rb, cols), lambda s, i, c_ref: (s, i, 0)))
    return pl.pallas_call(
        body, name="grad_pair_add_" + tag, grid_spec=grid_spec,
        out_shape=jax.ShapeDtypeStruct((nsh, half, cols), BF16),
        compiler_params=_params(("parallel", "parallel")),
    )(c_idx, g, got)


def _chip_exchange_protocol(ins, outs, send_sems, recv_sems):
    x, y, c = _place()
    chips = [(1 - x, y), (x, 1 - y), (1 - x, 1 - y)]

    def copies():
        return [pltpu.make_async_remote_copy(src_ref=ins[a].at[2 * px + py], dst_ref=outs[a].at[j],
                                             send_sem=send_sems.at[3 * a + j], recv_sem=recv_sems.at[3 * a + j],
                                             device_id=(px, py, c), device_id_type=MESH)
                for a in range(len(ins)) for j, (px, py) in enumerate(chips)]

    def begin():
        for cp in copies():
            cp.start()

    def end():
        for cp in copies():
            cp.wait_recv()
        for cp in copies():
            cp.wait_send()

    return begin, end


def _chip_exchange_shapes(ps):
    return [jax.ShapeDtypeStruct((N_CHIPS - 1,) + p.shape[1:], p.dtype) for p in ps]


def _chip_exchange_sems(n):
    return [pltpu.SemaphoreType.DMA((3 * n,)), pltpu.SemaphoreType.DMA((3 * n,))]


def _chip_exchange(ps):
    n = len(ps)

    def body(*refs):
        begin, end = _chip_exchange_protocol(refs[:n], refs[n:2 * n], *refs[2 * n:])
        begin()
        end()

    return pl.pallas_call(
        body, name="grad_chip_exchange", in_specs=_hbm_specs(n), out_specs=_hbm_specs(n),
        out_shape=_chip_exchange_shapes(ps), scratch_shapes=_chip_exchange_sems(n),
    )(*ps)


def _sum_partials(p, got, chip_idx, tag):
    nsh, half, cols = got.shape
    rb = _pick_rows(half)

    def body(me_ref, p_ref, got_ref, o_ref):
        acc = p_ref[0].astype(F32)
        for s in range(nsh):
            acc = acc + got_ref[s].astype(F32)
        o_ref[...] = acc

    grid_spec = pltpu.PrefetchScalarGridSpec(
        num_scalar_prefetch=1, grid=(half // rb,),
        in_specs=[pl.BlockSpec((1, rb, cols), lambda i, me_ref: (me_ref[0], i, 0)),
                  pl.BlockSpec((nsh, rb, cols), lambda i, me_ref: (0, i, 0))],
        out_specs=pl.BlockSpec((rb, cols), lambda i, me_ref: (i, 0)))
    return pl.pallas_call(
        body, name="grad_sum_chips_" + tag, grid_spec=grid_spec,
        out_shape=jax.ShapeDtypeStruct((half, cols), F32),
        compiler_params=_params(("parallel",)),
    )(chip_idx, p, got)


def _pair_share(rs):
    n = len(rs)

    def body(*refs):
        ins, outs, (send_sems, recv_sems) = refs[:n], refs[n:2 * n], refs[2 * n:]
        x, y, c = _place()
        cps = []
        for a in range(n):
            cp = pltpu.make_async_remote_copy(src_ref=ins[a], dst_ref=outs[a], send_sem=send_sems.at[a],
                                              recv_sem=recv_sems.at[a], device_id=(x, y, 1 - c),
                                              device_id_type=MESH)
            cp.start()
            cps.append(cp)
        for cp in cps:
            cp.wait()

    return pl.pallas_call(
        body, name="grad_pair_share", in_specs=_hbm_specs(n), out_specs=_hbm_specs(n),
        out_shape=[jax.ShapeDtypeStruct(r.shape, r.dtype) for r in rs],
        scratch_shapes=[pltpu.SemaphoreType.DMA((n,)), pltpu.SemaphoreType.DMA((n,))],
    )(*rs)


def _small_allreduce(v):
    rows, cols = v.shape
    ndev = 8

    def body(in_ref, out_ref, slots, send_sems, recv_sems):
        x, y, c = _place()
        me = 4 * x + 2 * y + c
        slots[me] = in_ref[...]
        sends = []
        for k in range(1, ndev):
            peer = (x ^ (k >> 2), y ^ ((k >> 1) & 1), c ^ (k & 1))
            cp = pltpu.make_async_remote_copy(src_ref=in_ref, dst_ref=slots.at[me], send_sem=send_sems.at[k - 1],
                                              recv_sem=recv_sems.at[k - 1], device_id=peer, device_id_type=MESH)
            cp.start()
            sends.append(cp)
        for k in range(1, ndev):
            there = slots.at[me ^ k]
            pltpu.make_async_remote_copy(src_ref=there, dst_ref=there, send_sem=send_sems.at[k - 1],
                                         recv_sem=recv_sems.at[k - 1], device_id=(x, y, c),
                                         device_id_type=MESH).wait_recv()
        for cp in sends:
            cp.wait_send()
        acc = slots[0]
        for s in range(1, ndev):
            acc = acc + slots[s]
        out_ref[...] = acc

    return pl.pallas_call(
        body, name="small_allreduce",
        in_specs=[pl.BlockSpec(memory_space=pltpu.VMEM)],
        out_specs=pl.BlockSpec(memory_space=pltpu.VMEM),
        out_shape=jax.ShapeDtypeStruct((rows, cols), F32),
        scratch_shapes=[pltpu.VMEM((ndev, rows, cols), F32), pltpu.SemaphoreType.DMA((ndev - 1,)),
                        pltpu.SemaphoreType.DMA((ndev - 1,))],
    )(v)


def _adamw(w, g, m, v, name):
    r, c = w.shape
    rb = r if r <= 128 else _pick_rows_8(r, 128)
    c1 = 1.0 - ADAM_B1 ** ADAM_STEP
    c2 = 1.0 - ADAM_B2 ** ADAM_STEP

    def body(w_ref, g_ref, m_ref, v_ref, d_ref, nm_ref, nv_ref):
        gg = g_ref[...]
        nm = ADAM_B1 * m_ref[...] + (1.0 - ADAM_B1) * gg
        nv = ADAM_B2 * v_ref[...] + (1.0 - ADAM_B2) * (gg * gg)
        d_ref[...] = -ADAM_LR * ((nm / c1) / (jnp.sqrt(nv / c2) + ADAM_EPS) + ADAM_WD * w_ref[...])
        nm_ref[...] = nm
        nv_ref[...] = nv

    blk = pl.BlockSpec((rb, c), lambda i: (i, 0))
    shp = jax.ShapeDtypeStruct((r, c), F32)
    return pl.pallas_call(
        body, name=name, grid=(r // rb,), in_specs=[blk] * 4, out_specs=[blk] * 3, out_shape=[shp] * 3,
        compiler_params=_params(("parallel",)),
    )(w, g, m, v)


def _pick_rows_8(n, target):
    best = n
    for b in range(8, min(n, target) + 1, 8):
        if n % b == 0:
            best = b
    return best


W_IN_COLS = 2308
W_UP_COLS = 1408
W_DOWN_ROWS = 704
DN_CONV_COLS = 768
FFN_CONV_COLS = 1408
PROJ_ROWS = 256
ROW_TILE = 16
ROW_SEGS = [("wp_dn", PROJ_ROWS), ("wp_sb", PROJ_ROWS), ("w_out", PROJ_ROWS), ("w_down", W_DOWN_ROWS),
            ("dn_conv", ROW_TILE), ("ffn_conv", ROW_TILE), ("spare", 2 * ROW_TILE)]
ROW_OFFS = {nm: (sum(n for _, n in ROW_SEGS[:i]), n) for i, (nm, n) in enumerate(ROW_SEGS)}
STACK_ROWS = sum(n for _, n in ROW_SEGS)
assert all(n % ROW_TILE == 0 for _, n in ROW_SEGS) and STACK_ROWS % (4 * ROW_TILE) == 0
Q_END, A_END, G_END, S_END = 3 * D_MODEL, 3 * D_MODEL + 2 * N_HEADS, 4 * D_MODEL + 2 * N_HEADS, 7 * D_MODEL + 2 * N_HEADS


def _flat_rows(a, nrows):
    flat = a.reshape(-1)
    return jnp.pad(flat, (0, nrows * D_MODEL - flat.shape[0])).reshape(nrows, D_MODEL)


IN_EXTRA_ROWS = 64


def _weight_wire(w_in, wp_dn, wp_sb, w_out, w_up, w_down, dn_conv, ffn_conv):
    bits = lax.bitcast_convert_type(dn_conv, BF16).reshape(-1)
    extra = jnp.pad(bits, (0, IN_EXTRA_ROWS * W_IN_COLS - bits.shape[0])).reshape(IN_EXTRA_ROWS, W_IN_COLS)
    stack = jnp.concatenate([wp_dn.astype(BF16), wp_sb.astype(BF16), w_out.astype(BF16), w_down.astype(BF16),
                             jnp.zeros((ROW_TILE, D_MODEL), BF16),
                             _flat_rows(lax.bitcast_convert_type(ffn_conv, BF16), ROW_TILE),
                             jnp.zeros((ROW_OFFS["spare"][1], D_MODEL), BF16)], axis=0)
    return [jnp.concatenate([w_in.astype(BF16), extra], axis=0)], [w_up.astype(BF16), stack]


def _col_range(g, lo, hi, width):
    parts = []
    for s in range(g.shape[0]):
        a, b = max(lo, s * width), min(hi, (s + 1) * width)
        if a < b:
            parts.append(g[s][:, a - s * width:b - s * width])
    return parts[0] if len(parts) == 1 else jnp.concatenate(parts, axis=1)


def _f32_rows(raw, k, ncols):
    raw = raw.reshape(N_CHIPS, -1)[:, :2 * k * ncols].reshape(N_CHIPS, k * ncols, 2)
    vals = lax.bitcast_convert_type(raw, F32).reshape(N_CHIPS, k, ncols)
    return vals.transpose(1, 0, 2).reshape(k, N_CHIPS * ncols)


def _unpack_early(g_in):
    w = g_in[:, :D_MODEL, :]
    return {
        "w_dnqkv": _col_range(w, 0, Q_END, W_IN_COLS),
        "w_ab": jnp.pad(_col_range(w, Q_END, A_END, W_IN_COLS), ((0, 0), (0, LANES - 2 * N_HEADS))),
        "w_dngate": _col_range(w, A_END, G_END, W_IN_COLS),
        "w_sbqkv": _col_range(w, G_END, S_END, W_IN_COLS),
        "w_gl": _col_range(w, S_END, N_CHIPS * W_IN_COLS, W_IN_COLS),
        "dn_conv": _f32_rows(g_in[:, D_MODEL:, :], DN_CONV, DN_CONV_COLS),
    }


def _unpack_late(g_up, g_stack):
    def seg(nm):
        at, n = ROW_OFFS[nm]
        return g_stack[:, at:at + n, :]

    ffn_conv = _f32_rows(seg("ffn_conv"), FFN_CONV, FFN_CONV_COLS)
    return {
        "wp_dn": seg("wp_dn").reshape(D_MODEL, D_MODEL),
        "wp_sb": seg("wp_sb").reshape(D_MODEL, D_MODEL),
        "w_out": seg("w_out").reshape(D_MODEL, D_MODEL),
        "w_up_g": _col_range(g_up, 0, D_FF, W_UP_COLS), "w_up_u": _col_range(g_up, D_FF, 2 * D_FF, W_UP_COLS),
        "w_down": seg("w_down").reshape(D_FF, D_MODEL),
        "ffn_conv_g": ffn_conv[:, :D_FF], "ffn_conv_u": ffn_conv[:, D_FF:],
    }


def _grad_wire_early(gr):
    def cols(a, ncols):
        return a.reshape(a.shape[0], N_CHIPS, ncols).transpose(1, 0, 2)

    def rows(a, nrows):
        return a.astype(BF16).reshape(N_CHIPS, nrows, a.shape[1])

    def flat(a, nrows):
        a = a.astype(BF16).reshape(N_CHIPS, -1)
        return jnp.pad(a, ((0, 0), (0, nrows * D_MODEL - a.shape[1]))).reshape(N_CHIPS, nrows, D_MODEL)

    up = [gr["w_up_g"], gr["w_up_u"]]
    g_up = jnp.stack([up[s // 2][:, (s % 2) * W_UP_COLS:(s % 2 + 1) * W_UP_COLS].astype(BF16) for s in range(N_CHIPS)])
    g_stack = jnp.concatenate([rows(gr["wp_dn"], PROJ_ROWS), rows(gr["wp_sb"], PROJ_ROWS), rows(gr["w_out"], PROJ_ROWS),
                               rows(gr["w_down"], W_DOWN_ROWS), jnp.zeros((N_CHIPS, ROW_TILE, D_MODEL), BF16),
                               flat(cols(gr["ffn_conv"], FFN_CONV_COLS), ROW_TILE),
                               jnp.zeros((N_CHIPS, ROW_OFFS["spare"][1], D_MODEL), BF16)], axis=1)
    return [g_up, g_stack]


def _grad_wire_late(gr):
    pieces = [(gr["w_dnqkv"], 0), (gr["w_ab"][:, :2 * N_HEADS], Q_END), (gr["w_dngate"], A_END),
              (gr["w_sbqkv"], G_END), (gr["w_gl"], S_END)]
    conv = gr["dn_conv"].reshape(DN_CONV, N_CHIPS, DN_CONV_COLS).transpose(1, 0, 2).reshape(N_CHIPS, -1)

    def block(s):
        lo, hi = s * W_IN_COLS, (s + 1) * W_IN_COLS
        parts = []
        for a, at in pieces:
            b0, b1 = max(lo, at), min(hi, at + a.shape[1])
            if b0 < b1:
                parts.append(a[:, b0 - at:b1 - at].astype(BF16))
        w = parts[0] if len(parts) == 1 else jnp.concatenate(parts, axis=1)
        extra = jnp.pad(conv[s].astype(BF16), (0, IN_EXTRA_ROWS * W_IN_COLS - conv.shape[1]))
        return jnp.concatenate([w, extra.reshape(IN_EXTRA_ROWS, W_IN_COLS)], axis=0)

    return [jnp.stack([block(s) for s in range(N_CHIPS)])]


def _unpack_grad_shard(r_in, r_up, r_stack):
    def seg(nm):
        at, n = ROW_OFFS[nm]
        return r_stack[at:at + n, :]

    return {
        "w_in": r_in[:D_MODEL], "w_up": r_up,
        "wp_dn": seg("wp_dn"), "wp_sb": seg("wp_sb"), "w_out": seg("w_out"), "w_down": seg("w_down"),
        "dn_conv": r_in[D_MODEL:].reshape(-1)[:DN_CONV * DN_CONV_COLS].reshape(DN_CONV, DN_CONV_COLS),
        "ffn_conv": seg("ffn_conv").reshape(-1)[:FFN_CONV * FFN_CONV_COLS].reshape(FFN_CONV, FFN_CONV_COLS),
    }


def _lane_row(v):
    return jnp.pad(v.reshape(1, -1), ((0, 0), (0, LANES - v.size)))


def kernel(x, norm1_w, w_in, dn_conv_w, dn_A_log, dn_dt_bias, dn_norm_w, w_proj_dn, w_proj_sb, w_out, norm2_w, ffn_w_up, ffn_conv_w, ffn_w_down, norm_f_w, loss_target, m_norm1_w, m_w_in, m_dn_conv_w, m_dn_A_log, m_dn_dt_bias, m_dn_norm_w, m_w_proj_dn, m_w_proj_sb, m_w_out, m_norm2_w, m_ffn_w_up, m_ffn_conv_w, m_ffn_w_down, m_norm_f_w, v_norm1_w, v_w_in, v_dn_conv_w, v_dn_A_log, v_dn_dt_bias, v_dn_norm_w, v_w_proj_dn, v_w_proj_sb, v_w_out, v_norm2_w, v_ffn_w_up, v_ffn_conv_w, v_ffn_w_down, v_norm_f_w):
    early, late = _weight_wire(w_in[0], w_proj_dn[0], w_proj_sb[0], w_out[0], ffn_w_up[0], ffn_w_down[0],
                               dn_conv_w[0], ffn_conv_w[0])
    chip_idx = (2 * lax.axis_index("x") + lax.axis_index("y")).astype(jnp.int32)

    def with_mine(gathered, wire):
        return [lax.dynamic_update_slice(g, mine[None], (chip_idx, 0, 0)) for g, mine in zip(gathered, wire)]

    wts = _unpack_early(*with_mine(_gather_shards(early), early))
    wts.update(norm1=norm1_w, norm2=norm2_w, normf=norm_f_w.reshape(1, D_MODEL), dn_norm=dn_norm_w,
               alog=_lane_row(dn_A_log), dtb=_lane_row(dn_dt_bias))

    c_idx = lax.axis_index("c").astype(jnp.int32).reshape(1)

    def pair_sums(wire_g, tags, when):
        return [_pair_add(g, got, c_idx, tag) for g, got, tag in zip(wire_g, _pair_exchange_halves(wire_g, when), tags)]

    loss_part, grad_x, gr, (early_sums, early_arrived) = _local_step(
        x[0], loss_target[0], wts, late, lambda gathered: _unpack_late(*with_mine(gathered, late)),
        lambda grads: pair_sums(_grad_wire_early(grads), ["w_up", "rows"], "early"))

    late_sums = pair_sums(_grad_wire_late(gr), ["w_in"], "late")
    tags = ["w_in", "w_up", "rows"]
    reduced = [_sum_partials(p, got, chip_idx.reshape(1), tag)
               for p, got, tag in zip(late_sums + early_sums, list(_chip_exchange(late_sums)) + list(early_arrived), tags)]
    is_south = lax.axis_index("c") == 0
    gsh = _unpack_grad_shard(*[jnp.concatenate([jnp.where(is_south, mine, other), jnp.where(is_south, other, mine)],
                                               axis=0) for mine, other in zip(reduced, _pair_share(reduced))])

    tail = jnp.concatenate([gr["dn_norm"], gr["alog"][:, :N_HEADS], gr["dtb"][:, :N_HEADS], loss_part[:, :1]], axis=1)
    small = jnp.concatenate([gr["norm1"], gr["norm2"], gr["normf"],
                             jnp.pad(tail, ((0, 0), (0, D_MODEL - tail.shape[1]))),
                             jnp.zeros((SMALL_ROWS - 4, D_MODEL), F32)], axis=0)
    small = _small_allreduce(small)
    at = HEAD_DIM
    g_small = {"norm1_w": small[0:1], "norm2_w": small[1:2], "norm_f_w": small[2],
               "dn_norm_w": small[3:4, :at], "dn_A_log": small[3:4, at:at + N_HEADS],
               "dn_dt_bias": small[3:4, at + N_HEADS:at + 2 * N_HEADS]}
    loss = small[3, at + 2 * N_HEADS]

    big = {"w_in": (w_in, m_w_in, v_w_in, gsh["w_in"]), "dn_conv_w": (dn_conv_w, m_dn_conv_w, v_dn_conv_w, gsh["dn_conv"]),
           "w_proj_dn": (w_proj_dn, m_w_proj_dn, v_w_proj_dn, gsh["wp_dn"]),
           "w_proj_sb": (w_proj_sb, m_w_proj_sb, v_w_proj_sb, gsh["wp_sb"]),
           "w_out": (w_out, m_w_out, v_w_out, gsh["w_out"]),
           "ffn_w_up": (ffn_w_up, m_ffn_w_up, v_ffn_w_up, gsh["w_up"]),
           "ffn_conv_w": (ffn_conv_w, m_ffn_conv_w, v_ffn_conv_w, gsh["ffn_conv"]),
           "ffn_w_down": (ffn_w_down, m_ffn_w_down, v_ffn_w_down, gsh["w_down"])}
    res = {}
    for nm, (w, m, v, g) in big.items():
        d, nm_, nv_ = _adamw(w[0], g, m[0], v[0], "adamw_" + nm)
        res[nm] = (g[None], d[None], nm_[None], nv_[None])

    names = ["norm1_w", "norm2_w", "norm_f_w", "dn_norm_w", "dn_A_log", "dn_dt_bias"]
    given = {"norm1_w": (norm1_w, m_norm1_w, v_norm1_w), "norm2_w": (norm2_w, m_norm2_w, v_norm2_w),
             "norm_f_w": (norm_f_w, m_norm_f_w, v_norm_f_w), "dn_norm_w": (dn_norm_w, m_dn_norm_w, v_dn_norm_w),
             "dn_A_log": (dn_A_log, m_dn_A_log, v_dn_A_log), "dn_dt_bias": (dn_dt_bias, m_dn_dt_bias, v_dn_dt_bias)}

    def stack(k, fill):
        rows = [jnp.pad(given[nm][k].reshape(1, -1), ((0, 0), (0, D_MODEL - given[nm][k].size)),
                        constant_values=fill) for nm in names]
        return jnp.concatenate(rows + [jnp.full((SMALL_ROWS - len(names), D_MODEL), fill, F32)], axis=0)

    g_rows = jnp.concatenate(
        [jnp.pad(g_small[nm].reshape(1, -1), ((0, 0), (0, D_MODEL - g_small[nm].size))) for nm in names]
        + [jnp.zeros((SMALL_ROWS - len(names), D_MODEL), F32)], axis=0)
    d_s, m_s, v_s = _adamw(stack(0, 0.0), g_rows, stack(1, 0.0), stack(2, 1.0), "adamw_small")
    for r, nm in enumerate(names):
        shape = given[nm][0].shape
        n = given[nm][0].size
        res[nm] = (g_small[nm].reshape(shape), d_s[r, :n].reshape(shape), m_s[r, :n].reshape(shape),
                   v_s[r, :n].reshape(shape))

    order = ["norm1_w", "w_in", "dn_conv_w", "dn_A_log", "dn_dt_bias", "dn_norm_w", "w_proj_dn", "w_proj_sb",
             "w_out", "norm2_w", "ffn_w_up", "ffn_conv_w", "ffn_w_down", "norm_f_w"]
    outs = [loss, grad_x[None]]
    for k in range(4):
        outs += [res[nm][k] for nm in order]
    return tuple(outs)
```

```python
import functools

import jax
import jax.numpy as jnp
from jax import lax
from jax.experimental import pallas as pl
from jax.experimental.pallas import tpu as pltpu

F32 = jnp.float32
BF16 = jnp.bfloat16
HIGHEST = lax.Precision.HIGHEST
MESH = pl.DeviceIdType.MESH

EPS = 1e-6
D_MODEL = 1024
N_HEADS = 8
HEAD_DIM = 128
DN_CONV = 4
DN_CHUNK = 64
D_FF = 2816
FFN_CONV = 3
ADAM_LR, ADAM_B1, ADAM_B2, ADAM_EPS, ADAM_WD, ADAM_STEP = 0.001, 0.9, 0.999, 1e-08, 0.01, 10

N_CHIPS = 4
LANES = 128
HALO = 8
VMEM_LIMIT = 48 * 1024 * 1024
SMALL_ROWS = 8


def _params(sem=None):
    return pltpu.CompilerParams(dimension_semantics=sem, vmem_limit_bytes=VMEM_LIMIT)


def _pick(n, target):
    best = None
    for b in range(LANES, min(n, target) + 1, LANES):
        if n % b == 0:
            best = b
    return best or n


ELEMENTWISE_COLS = 1408


def _rows(t, target=256):
    return min(t, target)


def _dot(a, b, precision=None):
    return lax.dot_general(a, b, (((1,), (0,)), ((), ())), precision=precision, preferred_element_type=F32)


def _dot_nt(a, b, precision=None):
    return lax.dot_general(a, b, (((1,), (1,)), ((), ())), precision=precision, preferred_element_type=F32)


def _dot_tn(a, b, precision=None):
    return lax.dot_general(a, b, (((0,), (0,)), ((), ())), precision=precision, preferred_element_type=F32)


def _rms(x, w):
    return x * lax.rsqrt(jnp.mean(x * x, axis=-1, keepdims=True) + EPS) * w


def _silu(x):
    return x * jax.nn.sigmoid(x)


def _softplus(x):
    return jnp.maximum(x, 0.0) + jnp.log(1.0 + jnp.exp(-jnp.abs(x)))


MM_BLOCK = 1408
MM_VMEM_BUDGET = 38 * 1024 * 1024


def _mm(a, b, *, ta=False, tb=False, add=None, out_dtype=F32, name, bm=MM_BLOCK, bn=MM_BLOCK, bk=MM_BLOCK):
    m = a.shape[1] if ta else a.shape[0]
    k = a.shape[0] if ta else a.shape[1]
    n = b.shape[0] if tb else b.shape[1]
    bm, bn = _pick(m, bm), _pick(n, bn)

    def vmem_need(bk_):
        need = 2 * (bm * bk_ * a.dtype.itemsize + bk_ * bn * b.dtype.itemsize) + 2 * bm * bn * jnp.dtype(out_dtype).itemsize
        need += 2 * bm * bn * add.dtype.itemsize if add is not None else 0
        return need + (bm * bn * 4 if bk_ < k else 0)

    bk = max((d for d in range(LANES, k + 1, LANES) if k % d == 0 and vmem_need(d) <= MM_VMEM_BUDGET),
             default=_pick(k, bk))
    nk = k // bk
    dims = (((0 if ta else 1,), (1 if tb else 0,)), ((), ()))

    def body(*refs):
        a_ref, b_ref = refs[:2]
        c_ref = refs[2] if add is not None else None
        o_ref = refs[3] if add is not None else refs[2]
        acc = refs[-1]
        kk = pl.program_id(2)
        part = lax.dot_general(a_ref[...].astype(BF16), b_ref[...].astype(BF16), dims, preferred_element_type=F32)

        def finish(r):
            if add is not None:
                r = r + c_ref[...].astype(F32)
            o_ref[...] = r.astype(out_dtype)

        if nk == 1:
            finish(part)
            return

        @pl.when(kk == 0)
        def _():
            acc[...] = part

        @pl.when(jnp.logical_and(kk > 0, kk < nk - 1))
        def _():
            acc[...] += part

        @pl.when(kk == nk - 1)
        def _():
            finish(acc[...] + part)

    a_spec = (pl.BlockSpec((bk, bm), lambda i, j, kk: (kk, i)) if ta
              else pl.BlockSpec((bm, bk), lambda i, j, kk: (i, kk)))
    b_spec = (pl.BlockSpec((bn, bk), lambda i, j, kk: (j, kk)) if tb
              else pl.BlockSpec((bk, bn), lambda i, j, kk: (kk, j)))
    o_spec = pl.BlockSpec((bm, bn), lambda i, j, kk: (i, j))
    in_specs = [a_spec, b_spec] + ([o_spec] if add is not None else [])
    args = (a, b) + ((add,) if add is not None else ())
    return pl.pallas_call(
        body, name=name, grid=(m // bm, n // bn, nk),
        in_specs=in_specs, out_specs=o_spec,
        out_shape=jax.ShapeDtypeStruct((m, n), out_dtype),
        scratch_shapes=[pltpu.VMEM((bm, bn), F32)] if nk > 1 else [],
        compiler_params=_params(("parallel", "parallel", "arbitrary")),
    )(*args)


def _norm1_fwd(x, w, w_ab):
    t = x.shape[0]
    tb = _rows(t)

    def body(x_ref, w_ref, wab_ref, n_ref, hab_ref):
        n = _rms(x_ref[...], w_ref[...]).astype(BF16)
        n_ref[...] = n
        hab_ref[...] = _dot(n, wab_ref[...])

    return pl.pallas_call(
        body, name="norm1_fwd", grid=(t // tb,),
        in_specs=[pl.BlockSpec((tb, D_MODEL), lambda i: (i, 0)),
                  pl.BlockSpec((1, D_MODEL), lambda i: (0, 0)),
                  pl.BlockSpec((D_MODEL, LANES), lambda i: (0, 0))],
        out_specs=[pl.BlockSpec((tb, D_MODEL), lambda i: (i, 0)),
                   pl.BlockSpec((tb, LANES), lambda i: (i, 0))],
        out_shape=[jax.ShapeDtypeStruct((t, D_MODEL), BF16), jax.ShapeDtypeStruct((t, LANES), F32)],
        compiler_params=_params(("arbitrary",)),
    )(x, w, w_ab)


def _norm1_bwd(x, w, dn, dres, dab, w_ab):
    t = x.shape[0]
    tb = _rows(t)

    def body(x_ref, w_ref, dn_ref, dres_ref, dab_ref, wab_ref, dx_ref, dw_ref):
        i = pl.program_id(0)
        g = dn_ref[...] + _dot_nt(dab_ref[...].astype(BF16), wab_ref[...])
        _, vjp = jax.vjp(_rms, x_ref[...], w_ref[...])
        dx, dw = vjp(g)
        dx_ref[...] = dres_ref[...] + dx

        @pl.when(i == 0)
        def _():
            dw_ref[...] = jnp.zeros_like(dw_ref)

        dw_ref[...] += dw

    row = pl.BlockSpec((tb, D_MODEL), lambda i: (i, 0))
    vec = pl.BlockSpec((1, D_MODEL), lambda i: (0, 0))
    return pl.pallas_call(
        body, name="norm1_bwd", grid=(t // tb,),
        in_specs=[row, vec, row, row, pl.BlockSpec((tb, LANES), lambda i: (i, 0)),
                  pl.BlockSpec((D_MODEL, LANES), lambda i: (0, 0))],
        out_specs=[row, vec],
        out_shape=[jax.ShapeDtypeStruct((t, D_MODEL), F32), jax.ShapeDtypeStruct((1, D_MODEL), F32)],
        compiler_params=_params(("arbitrary",)),
    )(x, w, dn, dres, dab, w_ab)


def _conv_fwd(x, w, name):
    t, c = x.shape
    kk = w.shape[0]
    tb, cb = _rows(t, 512), _pick(c, ELEMENTWISE_COLS)
    per = tb // HALO

    def body(x_ref, halo_ref, w_ref, y_ref, buf):
        i = pl.program_id(0)
        buf[pl.ds(HALO, tb), :] = x_ref[...]
        buf[pl.ds(0, HALO), :] = jnp.where(i == 0, 0.0, halo_ref[...])
        y_ref[...] = _conv_taps(buf, w_ref, HALO - (kk - 1), tb)

    return pl.pallas_call(
        body, name=name, grid=(t // tb, c // cb),
        in_specs=[pl.BlockSpec((tb, cb), lambda i, j: (i, j)),
                  pl.BlockSpec((HALO, cb), lambda i, j: (jnp.maximum(i * per - 1, 0), j)),
                  pl.BlockSpec((kk, cb), lambda i, j: (0, j))],
        out_specs=pl.BlockSpec((tb, cb), lambda i, j: (i, j)),
        out_shape=jax.ShapeDtypeStruct((t, c), F32),
        scratch_shapes=[pltpu.VMEM((tb + HALO, cb), F32)],
        compiler_params=_params(("parallel", "parallel")),
    )(x, x, w)


def _conv_bwd(dy, x, w, name, dx_dtype):
    t, c = x.shape
    kk = w.shape[0]
    tb, cb = _rows(t, 512), _pick(c, ELEMENTWISE_COLS)
    per = tb // HALO
    nblk = t // tb

    def body(dy_ref, after_ref, x_ref, w_ref, dx_ref, dw_ref, dbuf):
        i = pl.program_id(1)
        dbuf[pl.ds(0, tb), :] = dy_ref[...]
        dbuf[pl.ds(tb, HALO), :] = jnp.where(i == nblk - 1, 0.0, after_ref[...])

        @pl.when(i == 0)
        def _():
            dw_ref[...] = jnp.zeros_like(dw_ref)

        for j in range(cb // LANES):
            sl = pl.ds(j * LANES, LANES)
            x = x_ref[:, sl]
            dx = None
            for s in range(kk):
                shifted = dbuf[pl.ds(kk - 1 - s, tb), sl]
                term = w_ref[s:s + 1, sl] * shifted
                dx = term if dx is None else dx + term
                dw_ref[s:s + 1, sl] += jnp.sum(shifted * x, axis=0, keepdims=True)
            dx_ref[:, sl] = dx.astype(dx_dtype)

    blk = pl.BlockSpec((tb, cb), lambda j, i: (i, j))
    return pl.pallas_call(
        body, name=name, grid=(c // cb, nblk),
        in_specs=[blk,
                  pl.BlockSpec((HALO, cb), lambda j, i: (jnp.minimum((i + 1) * per, t // HALO - 1), j)),
                  blk,
                  pl.BlockSpec((kk, cb), lambda j, i: (0, j))],
        out_specs=[blk, pl.BlockSpec((HALO, cb), lambda j, i: (0, j))],
        out_shape=[jax.ShapeDtypeStruct((t, c), dx_dtype), jax.ShapeDtypeStruct((HALO, c), F32)],
        scratch_shapes=[pltpu.VMEM((tb + HALO, cb), F32)],
        compiler_params=_params(("parallel", "arbitrary")),
    )(dy, dy, x, w)


def _dn_prep_fn(c, hab, alog, dtb):
    s = _silu(c)
    heads = []
    for h in range(2 * N_HEADS):
        sh = s[:, h * HEAD_DIM:(h + 1) * HEAD_DIM]
        heads.append(sh * lax.rsqrt(jnp.sum(sh * sh, axis=-1, keepdims=True) + EPS))
    qn = jnp.concatenate(heads[:N_HEADS], axis=1)
    kn = jnp.concatenate(heads[N_HEADS:], axis=1)
    v = s[:, 2 * D_MODEL:]
    lane = lax.broadcasted_iota(jnp.int32, hab.shape, 1)
    g = -jnp.exp(alog) * _softplus(hab + dtb)
    beta = jax.nn.sigmoid(hab)
    gb = jnp.where(lane < N_HEADS, g, jnp.where(lane < 2 * N_HEADS, beta, 0.0))
    return qn, kn, v, gb


def _to_heads(ref, val):
    for h in range(N_HEADS):
        ref[h] = val[:, h * HEAD_DIM:(h + 1) * HEAD_DIM]


def _from_heads(ref):
    return jnp.concatenate([ref[h] for h in range(N_HEADS)], axis=1)


def _dn_prep_fwd(c, hab, alog, dtb):
    t = c.shape[0]
    tb = _rows(t)

    def body(c_ref, hab_ref, alog_ref, dtb_ref, q_ref, k_ref, v_ref, gb_ref):
        qn, kn, v, gb = _dn_prep_fn(c_ref[...], hab_ref[...], alog_ref[...], dtb_ref[...])
        _to_heads(q_ref, qn)
        _to_heads(k_ref, kn)
        _to_heads(v_ref, v)
        gb_ref[...] = gb

    hm = pl.BlockSpec((N_HEADS, tb, HEAD_DIM), lambda i: (0, i, 0))
    nar = pl.BlockSpec((tb, LANES), lambda i: (i, 0))
    vec = pl.BlockSpec((1, LANES), lambda i: (0, 0))
    return pl.pallas_call(
        body, name="dn_prep_fwd", grid=(t // tb,),
        in_specs=[pl.BlockSpec((tb, 3 * D_MODEL), lambda i: (i, 0)), nar, vec, vec],
        out_specs=[hm, hm, hm, nar],
        out_shape=[jax.ShapeDtypeStruct((N_HEADS, t, HEAD_DIM), F32)] * 3 + [jax.ShapeDtypeStruct((t, LANES), F32)],
        compiler_params=_params(("parallel",)),
    )(c, hab, alog, dtb)


def _dn_prep_bwd(c, hab, alog, dtb, dq, dk, dv, dgb):
    t = c.shape[0]
    tb = _rows(t)

    def body(c_ref, hab_ref, alog_ref, dtb_ref, dq_ref, dk_ref, dv_ref, dgb_ref,
             dc_ref, dhab_ref, dalog_ref, ddtb_ref):
        i = pl.program_id(0)
        _, vjp = jax.vjp(_dn_prep_fn, c_ref[...], hab_ref[...], alog_ref[...], dtb_ref[...])
        dc, dhab, dalog, ddtb = vjp((_from_heads(dq_ref), _from_heads(dk_ref), _from_heads(dv_ref), dgb_ref[...]))
        dc_ref[...] = dc
        dhab_ref[...] = dhab

        @pl.when(i == 0)
        def _():
            dalog_ref[...] = jnp.zeros_like(dalog_ref)
            ddtb_ref[...] = jnp.zeros_like(ddtb_ref)

        dalog_ref[...] += dalog
        ddtb_ref[...] += ddtb

    hm = pl.BlockSpec((N_HEADS, tb, HEAD_DIM), lambda i: (0, i, 0))
    wide = pl.BlockSpec((tb, 3 * D_MODEL), lambda i: (i, 0))
    nar = pl.BlockSpec((tb, LANES), lambda i: (i, 0))
    vec = pl.BlockSpec((1, LANES), lambda i: (0, 0))
    return pl.pallas_call(
        body, name="dn_prep_bwd", grid=(t // tb,),
        in_specs=[wide, nar, vec, vec, hm, hm, hm, nar],
        out_specs=[wide, nar, vec, vec],
        out_shape=[jax.ShapeDtypeStruct((t, 3 * D_MODEL), F32), jax.ShapeDtypeStruct((t, LANES), F32),
                   jax.ShapeDtypeStruct((1, LANES), F32), jax.ShapeDtypeStruct((1, LANES), F32)],
        compiler_params=_params(("arbitrary",)),
    )(c, hab, alog, dtb, dq, dk, dv, dgb)


DN_PREC = lax.Precision.HIGH
DN_GROUP = 8


def _dn_prec(a):
    return DN_PREC if a.dtype == F32 else None


def _bdot(a, b):
    return lax.dot_general(a, b, (((2,), (1,)), ((0,), (0,))), precision=_dn_prec(a), preferred_element_type=F32)


def _bdot_nt(a, b):
    return lax.dot_general(a, b, (((2,), (2,)), ((0,), (0,))), precision=_dn_prec(a), preferred_element_type=F32)


def _bdot_tn(a, b):
    return lax.dot_general(a, b, (((1,), (1,)), ((0,), (0,))), precision=_dn_prec(a), preferred_element_type=F32)


def _unit_lower_inverse(lmat):
    c = lmat.shape[-1]
    ri = lax.broadcasted_iota(jnp.int32, (c, c), 0)
    ci = lax.broadcasted_iota(jnp.int32, (c, c), 1)
    p = -lmat
    tinv = jnp.where(ri == ci, 1.0, 0.0) + p
    for _ in range(max(c.bit_length() - 2, 0)):
        p = _bdot(p, p)
        tinv = tinv + _bdot(tinv, p)
    return tinv


@jax.custom_vjp
def _solve_with(lmat, rhs, tinv):
    return _bdot(tinv, rhs)


def _solve_with_fwd(lmat, rhs, tinv):
    sol = _bdot(tinv, rhs)
    return sol, (sol, tinv)


def _solve_with_bwd(res, dsol):
    sol, tinv = res
    drhs = _bdot_tn(tinv, dsol)
    return -_bdot_nt(drhs, sol), drhs, jnp.zeros_like(tinv)


_solve_with.defvjp(_solve_with_fwd, _solve_with_bwd)


def _dn_local(q, k, v, grow, brow, tinv):
    g, c, _ = q.shape
    ri = lax.broadcasted_iota(jnp.int32, (c, c), 0)
    ci = lax.broadcasted_iota(jnp.int32, (c, c), 1)
    lower = ri >= ci
    as_col = lambda r: jnp.sum(jnp.where(ri == ci, jnp.broadcast_to(r, (g, c, c)), 0.0), axis=2, keepdims=True)
    gcol, bcol = as_col(grow), as_col(brow)
    gc_col = jnp.sum(jnp.where(lower, jnp.broadcast_to(grow, (g, c, c)), 0.0), axis=2, keepdims=True)
    gc_row = jnp.sum(jnp.where(ri <= ci, jnp.broadcast_to(gcol, (g, c, c)), 0.0), axis=1, keepdims=True)
    qs = q * (HEAD_DIM ** -0.5)
    kb = k * bcol
    vb = v * bcol
    decay = jnp.where(lower, jnp.exp(jnp.where(lower, gc_col - gc_row, 0.0)), 0.0)
    lmat = jnp.where(ri > ci, _bdot_nt(kb.astype(BF16), k.astype(BF16)) * decay, 0.0)
    eg = jnp.exp(gc_col)
    rhs = jnp.concatenate([vb, kb * eg], axis=2)
    if tinv is None:
        tinv = _unit_lower_inverse(lmat)
    sol = _solve_with(lmat, rhs, tinv)
    a_qk = jnp.where(lower, _bdot_nt(qs.astype(BF16), k.astype(BF16)) * decay, 0.0)
    g_last = jnp.sum(grow, axis=2, keepdims=True)
    kdec = k * jnp.exp(g_last - gc_col)
    egl = jnp.broadcast_to(jnp.exp(g_last), (g, 1, HEAD_DIM))
    return sol[:, :, :HEAD_DIM], sol[:, :, HEAD_DIM:], a_qk, qs * eg, kdec, egl, tinv


def _dn_seq(u, w, a_qk, qe, kdec, egl, s_in):
    b16 = lambda x: x.astype(BF16)
    v_new = u - _bdot(b16(w), b16(s_in))
    o = _bdot(b16(qe), b16(s_in)) + _bdot(b16(a_qk), b16(v_new))
    return o, s_in * egl + _bdot_tn(b16(kdec), b16(v_new))


def _dn_local_specs(t):
    grp = min(DN_GROUP, t // DN_CHUNK)
    rows = grp * DN_CHUNK
    blk = pl.BlockSpec((1, rows, HEAD_DIM), lambda h, i: (h, i, 0))
    row = pl.BlockSpec((1, grp, 1, DN_CHUNK), lambda h, i: (h, i, 0, 0))
    sq = pl.BlockSpec((1, grp, DN_CHUNK, DN_CHUNK), lambda h, i: (h, i, 0, 0))
    lane = pl.BlockSpec((1, grp, 1, HEAD_DIM), lambda h, i: (h, i, 0, 0))
    return grp, blk, row, sq, lane


def _dn_shapes(t):
    nchunk = t // DN_CHUNK
    big = jax.ShapeDtypeStruct((N_HEADS, t, HEAD_DIM), F32)
    row = jax.ShapeDtypeStruct((N_HEADS, nchunk, 1, DN_CHUNK), F32)
    sq = jax.ShapeDtypeStruct((N_HEADS, nchunk, DN_CHUNK, DN_CHUNK), F32)
    lane = jax.ShapeDtypeStruct((N_HEADS, nchunk, 1, HEAD_DIM), F32)
    return big, row, sq, lane


def _dn_local_fwd(q, k, v, grow, brow, wire=()):
    t = q.shape[1]
    grp, blk, row, sq, lane = _dn_local_specs(t)
    big, _, sqs, lanes = _dn_shapes(t)
    n = len(wire)
    groups = t // (grp * DN_CHUNK)
    steps = N_HEADS * groups

    def body(q_ref, k_ref, v_ref, gr_ref, br_ref, *rest):
        u_ref, w_ref, a_ref, qe_ref, kd_ref, egl_ref, t_ref = rest[n:n + 7]
        if n:
            begin, middle, end = _gather_protocol(rest[:n], rest[n + 7:2 * n + 7], *rest[2 * n + 7:])
            step = pl.program_id(0) * groups + pl.program_id(1)
            pl.when(step == 0)(begin)
            pl.when(step == (5 * steps) // 8)(middle)
        split = lambda r: r[0].reshape(grp, DN_CHUNK, HEAD_DIM)
        u, w, a_qk, qe, kdec, egl, tinv = _dn_local(split(q_ref), split(k_ref), split(v_ref), gr_ref[0],
                                                     br_ref[0], None)
        for ref, val in ((u_ref, u), (w_ref, w), (qe_ref, qe), (kd_ref, kdec)):
            ref[0] = val.reshape(grp * DN_CHUNK, HEAD_DIM)
        a_ref[0] = a_qk
        egl_ref[0] = egl
        t_ref[0] = tinv
        if n:
            pl.when(step == steps - 1)(end)

    assert n == 0 or steps >= 3
    return pl.pallas_call(
        body, name="dn_local_fwd", grid=(N_HEADS, groups),
        in_specs=[blk, blk, blk, row, row] + _hbm_specs(n),
        out_specs=[blk, blk, sq, blk, blk, lane, sq] + _hbm_specs(n),
        out_shape=[big, big, sqs, big, big, lanes, sqs] + _gather_out_shapes(wire),
        scratch_shapes=_gather_sems(n) if n else [],
        compiler_params=_params(("arbitrary", "arbitrary")),
    )(q, k, v, grow, brow, *wire)


def _dn_local_bwd(q, k, v, grow, brow, tinv, du, dw, da, dqe, dkd, degl, partials=()):
    t = q.shape[1]
    grp, blk, row, sq, lane = _dn_local_specs(t)
    big, rows_, _, _ = _dn_shapes(t)
    n = len(partials)
    groups = t // (grp * DN_CHUNK)
    steps = N_HEADS * groups

    def body(q_ref, k_ref, v_ref, gr_ref, br_ref, t_ref, du_ref, dw_ref, da_ref, dqe_ref, dkd_ref,
             degl_ref, *rest):
        dq_ref, dk_ref, dv_ref, dgr_ref, dbr_ref = rest[n:n + 5]
        if n:
            begin, end = _chip_exchange_protocol(rest[:n], rest[n + 5:2 * n + 5], *rest[2 * n + 5:])
            step = pl.program_id(0) * groups + pl.program_id(1)
            pl.when(step == 0)(begin)
        split = lambda r: r[0].reshape(grp, DN_CHUNK, HEAD_DIM)
        tinv_v = t_ref[0]
        fn = lambda q_, k_, v_, gr_, br_: _dn_local(q_, k_, v_, gr_, br_, tinv_v)[:6]
        _, vjp = jax.vjp(fn, split(q_ref), split(k_ref), split(v_ref), gr_ref[0], br_ref[0])
        dq, dk, dv, dgr, dbr = vjp((split(du_ref), split(dw_ref), da_ref[0], split(dqe_ref), split(dkd_ref),
                                    degl_ref[0]))
        for ref, val in ((dq_ref, dq), (dk_ref, dk), (dv_ref, dv)):
            ref[0] = val.reshape(grp * DN_CHUNK, HEAD_DIM)
        dgr_ref[0] = dgr
        dbr_ref[0] = dbr
        if n:
            pl.when(step == steps - 1)(end)

    assert n == 0 or steps >= 2
    return pl.pallas_call(
        body, name="dn_local_bwd", grid=(N_HEADS, groups),
        in_specs=[blk, blk, blk, row, row, sq, blk, blk, sq, blk, blk, lane] + _hbm_specs(n),
        out_specs=[blk, blk, blk, row, row] + _hbm_specs(n),
        out_shape=[big, big, big, rows_, rows_] + _chip_exchange_shapes(partials),
        scratch_shapes=_chip_exchange_sems(n) if n else [],
        compiler_params=_params(("arbitrary", "arbitrary")),
    )(q, k, v, grow, brow, tinv, du, dw, da, dqe, dkd, degl, *partials)


def _dn_seq_specs(nchunk, rev):
    def idx(n):
        return nchunk - 1 - n if rev else n

    blk = pl.BlockSpec((N_HEADS, DN_CHUNK, HEAD_DIM), lambda n: (0, idx(n), 0))
    sq = pl.BlockSpec((N_HEADS, 1, DN_CHUNK, DN_CHUNK), lambda n: (0, idx(n), 0, 0))
    lane = pl.BlockSpec((N_HEADS, 1, 1, HEAD_DIM), lambda n: (0, idx(n), 0, 0))
    st = pl.BlockSpec((N_HEADS, 1, HEAD_DIM, HEAD_DIM), lambda n: (0, idx(n), 0, 0))
    return blk, sq, lane, st


def _dn_seq_fwd(u, w, a_qk, qe, kdec, egl):
    t = u.shape[1]
    nchunk = t // DN_CHUNK
    blk, sq, lane, st = _dn_seq_specs(nchunk, False)

    def body(u_ref, w_ref, a_ref, qe_ref, kd_ref, egl_ref, o_ref, s_ref, state):
        @pl.when(pl.program_id(0) == 0)
        def _():
            state[...] = jnp.zeros_like(state)

        s_in = state[...]
        s_ref[:, 0] = s_in
        o, s_out = _dn_seq(u_ref[...], w_ref[...], a_ref[:, 0], qe_ref[...], kd_ref[...], egl_ref[:, 0], s_in)
        o_ref[...] = o
        state[...] = s_out

    return pl.pallas_call(
        body, name="dn_seq_fwd", grid=(nchunk,),
        in_specs=[blk, blk, sq, blk, blk, lane],
        out_specs=[blk, st],
        out_shape=[jax.ShapeDtypeStruct((N_HEADS, t, HEAD_DIM), F32),
                   jax.ShapeDtypeStruct((N_HEADS, nchunk, HEAD_DIM, HEAD_DIM), F32)],
        scratch_shapes=[pltpu.VMEM((N_HEADS, HEAD_DIM, HEAD_DIM), F32)],
        compiler_params=_params(("arbitrary",)),
    )(u, w, a_qk, qe, kdec, egl)


def _dn_seq_bwd(u, w, a_qk, qe, kdec, egl, states, do):
    t = u.shape[1]
    nchunk = t // DN_CHUNK
    blk, sq, lane, st = _dn_seq_specs(nchunk, True)
    big, _, sqs, lanes = _dn_shapes(t)

    def body(u_ref, w_ref, a_ref, qe_ref, kd_ref, egl_ref, s_ref, do_ref,
             du_ref, dw_ref, da_ref, dqe_ref, dkd_ref, degl_ref, dstate):
        @pl.when(pl.program_id(0) == 0)
        def _():
            dstate[...] = jnp.zeros_like(dstate)

        _, vjp = jax.vjp(_dn_seq, u_ref[...], w_ref[...], a_ref[:, 0], qe_ref[...], kd_ref[...], egl_ref[:, 0],
                         s_ref[:, 0])
        du, dw, da, dqe, dkd, degl, ds = vjp((do_ref[...], dstate[...]))
        du_ref[...] = du
        dw_ref[...] = dw
        da_ref[:, 0] = da
        dqe_ref[...] = dqe
        dkd_ref[...] = dkd
        degl_ref[:, 0] = degl
        dstate[...] = ds

    return pl.pallas_call(
        body, name="dn_seq_bwd", grid=(nchunk,),
        in_specs=[blk, blk, sq, blk, blk, lane, st, blk],
        out_specs=[blk, blk, sq, blk, blk, lane],
        out_shape=[big, big, sqs, big, big, lanes],
        scratch_shapes=[pltpu.VMEM((N_HEADS, HEAD_DIM, HEAD_DIM), F32)],
        compiler_params=_params(("arbitrary",)),
    )(u, w, a_qk, qe, kdec, egl, states, do)


def _dn_post_fn(o, gate, w):
    outs = []
    for h in range(N_HEADS):
        sl = slice(h * HEAD_DIM, (h + 1) * HEAD_DIM)
        outs.append(_rms(o[:, sl], w) * _silu(gate[:, sl]))
    return jnp.concatenate(outs, axis=1)


def _dn_post_fwd(o, gate, w):
    t = gate.shape[0]
    tb = _rows(t)

    def body(o_ref, g_ref, w_ref, y_ref):
        y_ref[...] = _dn_post_fn(_from_heads(o_ref), g_ref[...], w_ref[...]).astype(BF16)

    row = pl.BlockSpec((tb, D_MODEL), lambda i: (i, 0))
    hm = pl.BlockSpec((N_HEADS, tb, HEAD_DIM), lambda i: (0, i, 0))
    return pl.pallas_call(
        body, name="dn_post_fwd", grid=(t // tb,),
        in_specs=[hm, row, pl.BlockSpec((1, HEAD_DIM), lambda i: (0, 0))],
        out_specs=row, out_shape=jax.ShapeDtypeStruct((t, D_MODEL), BF16),
        compiler_params=_params(("parallel",)),
    )(o, gate, w)


def _dn_post_bwd(o, gate, w, dy):
    t = gate.shape[0]
    tb = _rows(t)

    def body(o_ref, g_ref, w_ref, dy_ref, do_ref, dg_ref, dw_ref):
        i = pl.program_id(0)
        _, vjp = jax.vjp(_dn_post_fn, _from_heads(o_ref), g_ref[...], w_ref[...])
        do, dg, dw = vjp(dy_ref[...])
        _to_heads(do_ref, do)
        dg_ref[...] = dg.astype(BF16)

        @pl.when(i == 0)
        def _():
            dw_ref[...] = jnp.zeros_like(dw_ref)

        dw_ref[...] += dw

    row = pl.BlockSpec((tb, D_MODEL), lambda i: (i, 0))
    hm = pl.BlockSpec((N_HEADS, tb, HEAD_DIM), lambda i: (0, i, 0))
    vec = pl.BlockSpec((1, HEAD_DIM), lambda i: (0, 0))
    return pl.pallas_call(
        body, name="dn_post_bwd", grid=(t // tb,),
        in_specs=[hm, row, vec, row],
        out_specs=[hm, row, vec],
        out_shape=[jax.ShapeDtypeStruct((N_HEADS, t, HEAD_DIM), F32), jax.ShapeDtypeStruct((t, D_MODEL), BF16),
                   jax.ShapeDtypeStruct((1, HEAD_DIM), F32)],
        compiler_params=_params(("arbitrary",)),
    )(o, gate, w, dy)


def _split_bf16(x):
    hi = x.astype(BF16)
    lo = (x - hi.astype(F32)).astype(BF16)
    return hi, lo


SB_Q_BLOCK = 512
SB_K_BLOCK = 256
SB_NEGLIGIBLE = -60.0


def _sb_logits(q, kb, mask, scale):
    z = _dot_nt(q, kb) * scale
    ls = jnp.minimum(z, 0.0) - jnp.log(1.0 + jnp.exp(-jnp.abs(z)))
    lk = ls - z
    if mask is not None:
        lk = jnp.where(mask, lk, 0.0)
    return ls, lk


def _sb_blocks(t):
    bq = min(SB_Q_BLOCK, t)
    bk = min(SB_K_BLOCK, bq)
    return bq, bk, bq // bk


def _sb_fwd(qkv):
    t = qkv.shape[0]
    bq, bk, nd = _sb_blocks(t)
    scale = HEAD_DIM ** -0.5

    def body(q_ref, k_ref, v_ref, o_ref, tot_ref, used_ref):
        i = pl.program_id(1)
        q = q_ref[...]
        rj = lax.broadcasted_iota(jnp.int32, (bk, bk), 0)
        cj = lax.broadcasted_iota(jnp.int32, (bk, bk), 1)
        after = (rj > cj).astype(BF16)
        trow = lax.broadcasted_iota(jnp.int32, (bq, bk), 0)
        scol = lax.broadcasted_iota(jnp.int32, (bq, bk), 1)

        def tile(j, run, acc, mask):
            off = pl.multiple_of(j * bk, bk)
            kb = k_ref[pl.ds(off, bk), :]
            vb = v_ref[pl.ds(off, bk), :]
            ls, lk = _sb_logits(q, kb, mask, scale)
            hi, lo = _split_bf16(lk)
            between = _dot(hi, after) + _dot(lo, after) + run
            a = jnp.exp(ls + between)
            if mask is not None:
                a = jnp.where(mask, a, 0.0)
            acc = acc + _dot(a.astype(BF16), vb)
            return run + jnp.sum(lk, axis=1, keepdims=True), acc

        run, acc = jnp.zeros((bq, 1), F32), jnp.zeros((bq, HEAD_DIM), F32)
        for d in reversed(range(nd)):
            run, acc = tile(i * nd + d, run, acc, scol + d * bk < trow)
        def more(c):
            return jnp.logical_and(c[0] < i * nd, jnp.max(c[1]) > SB_NEGLIGIBLE)

        def far(c):
            run_, acc_ = tile(i * nd - 1 - c[0], c[1], c[2], None)
            return c[0] + 1, run_, acc_

        used, run, acc = lax.while_loop(more, far, (jnp.int32(0), run, acc))
        o_ref[...] = acc.astype(BF16)
        tot_ref[...] = jnp.broadcast_to(run, (bq, HEAD_DIM))
        used_ref[...] = jnp.full(used_ref.shape, used, F32)

    qs = pl.BlockSpec((bq, HEAD_DIM), lambda h, i: (i, h))
    ks = pl.BlockSpec((t, HEAD_DIM), lambda h, i: (0, N_HEADS + h))
    vs = pl.BlockSpec((t, HEAD_DIM), lambda h, i: (0, 2 * N_HEADS + h))
    return pl.pallas_call(
        body, name="sb_fwd", grid=(N_HEADS, t // bq),
        in_specs=[qs, ks, vs], out_specs=[qs, qs, pl.BlockSpec((1, 1, 1, LANES), lambda h, i: (h, i, 0, 0))],
        out_shape=[jax.ShapeDtypeStruct((t, D_MODEL), BF16), jax.ShapeDtypeStruct((t, D_MODEL), F32),
                   jax.ShapeDtypeStruct((N_HEADS, t // bq, 1, LANES), F32)],
        compiler_params=_params(("parallel", "arbitrary")),
    )(qkv, qkv, qkv)


def _sb_bwd(qkv, tot, used, do):
    t = qkv.shape[0]
    bq, bk, nd = _sb_blocks(t)
    scale = HEAD_DIM ** -0.5

    def body(q_ref, k_ref, v_ref, tot_ref, used_ref, do_ref, dq_ref, dk_ref, dv_ref):
        i = pl.program_id(1)

        @pl.when(i == 0)
        def _():
            dk_ref[...] = jnp.zeros_like(dk_ref)
            dv_ref[...] = jnp.zeros_like(dv_ref)

        q = q_ref[...]
        do = do_ref[...]
        total = tot_ref[:, 0:1]
        rj = lax.broadcasted_iota(jnp.int32, (bk, bk), 0)
        cj = lax.broadcasted_iota(jnp.int32, (bk, bk), 1)
        upto = (rj <= cj).astype(BF16)
        before = (rj < cj).astype(BF16)
        trow = lax.broadcasted_iota(jnp.int32, (bq, bk), 0)
        scol = lax.broadcasted_iota(jnp.int32, (bq, bk), 1)

        def tile(j, run_k, run_e, dq, mask):
            off = pl.multiple_of(j * bk, bk)
            kb = k_ref[pl.ds(off, bk), :]
            vb = v_ref[pl.ds(off, bk), :]
            ls, lk = _sb_logits(q, kb, mask, scale)
            hi, lo = _split_bf16(lk)
            between = total - (_dot(hi, upto) + _dot(lo, upto) + run_k)
            a = jnp.exp(ls + between)
            if mask is not None:
                a = jnp.where(mask, a, 0.0)
            e = a * _dot_nt(do, vb)
            ehi, elo = _split_bf16(e)
            pre = _dot(ehi, before) + _dot(elo, before) + run_e
            sig = jnp.exp(ls)
            dz = e * (1.0 - sig) - pre * sig
            if mask is not None:
                dz = jnp.where(mask, dz, 0.0)
            dz = (dz * scale).astype(BF16)
            dq = dq + _dot(dz, kb)
            dk_ref[pl.ds(off, bk), :] += _dot_tn(dz, q)
            dv_ref[pl.ds(off, bk), :] += _dot_tn(a.astype(BF16), do)
            return (run_k + jnp.sum(lk, axis=1, keepdims=True),
                    run_e + jnp.sum(e, axis=1, keepdims=True), dq)

        zero = jnp.zeros((bq, 1), F32)
        visited = jnp.clip(jnp.max(used_ref[...]).astype(jnp.int32), 0, i * nd)
        carry = lax.fori_loop(i * nd - visited, i * nd, lambda j, c: tile(j, c[0], c[1], c[2], None),
                              (zero, zero, jnp.zeros((bq, HEAD_DIM), F32)))
        for d in range(nd):
            carry = tile(i * nd + d, *carry, scol + d * bk < trow)
        dq_ref[...] = carry[2]

    qs = pl.BlockSpec((bq, HEAD_DIM), lambda h, i: (i, h))
    ks = pl.BlockSpec((t, HEAD_DIM), lambda h, i: (0, N_HEADS + h))
    vs = pl.BlockSpec((t, HEAD_DIM), lambda h, i: (0, 2 * N_HEADS + h))
    full = pl.BlockSpec((t, HEAD_DIM), lambda h, i: (0, h))
    big = jax.ShapeDtypeStruct((t, D_MODEL), F32)
    return pl.pallas_call(
        body, name="sb_bwd", grid=(N_HEADS, t // bq),
        in_specs=[qs, ks, vs, qs, pl.BlockSpec((1, 1, 1, LANES), lambda h, i: (h, i, 0, 0)), qs],
        out_specs=[qs, full, full],
        out_shape=[big, big, big],
        compiler_params=_params(("parallel", "arbitrary")),
    )(qkv, qkv, qkv, tot, used, do)


def _merge_fwd(o_dn, o_sb, gl, x, wp_dn, wp_sb, w_out, w2):
    t = x.shape[0]
    tb = _rows(t)

    def body(odn_ref, osb_ref, gl_ref, x_ref, wpd_ref, wps_ref, wo_ref, w2_ref,
             pdn_ref, psb_ref, mix_ref, x1_ref, n2_ref):
        pdn = _dot(odn_ref[...], wpd_ref[...])
        psb = _dot(osb_ref[...], wps_ref[...])
        gates = jax.nn.sigmoid(gl_ref[...])
        mixed = (gates[:, :D_MODEL] * pdn + gates[:, D_MODEL:] * psb).astype(BF16)
        x1 = x_ref[...] + _dot(mixed, wo_ref[...])
        pdn_ref[...] = pdn
        psb_ref[...] = psb
        mix_ref[...] = mixed
        x1_ref[...] = x1
        n2_ref[...] = _rms(x1, w2_ref[...]).astype(BF16)

    row = pl.BlockSpec((tb, D_MODEL), lambda i: (i, 0))
    sq = pl.BlockSpec((D_MODEL, D_MODEL), lambda i: (0, 0))
    f = jax.ShapeDtypeStruct((t, D_MODEL), F32)
    b = jax.ShapeDtypeStruct((t, D_MODEL), BF16)
    return pl.pallas_call(
        body, name="merge_fwd", grid=(t // tb,),
        in_specs=[row, row, pl.BlockSpec((tb, 2 * D_MODEL), lambda i: (i, 0)), row, sq, sq, sq,
                  pl.BlockSpec((1, D_MODEL), lambda i: (0, 0))],
        out_specs=[row] * 5, out_shape=[f, f, b, f, b],
        compiler_params=_params(("parallel",)),
    )(o_dn, o_sb, gl, x, wp_dn, wp_sb, w_out, w2)


def _merge_bwd(dx2, dn2, x1, w2, gl, pdn, psb, wp_dn, wp_sb, w_out):
    t = x1.shape[0]
    tb = _rows(t)

    def body(dx2_ref, dn2_ref, x1_ref, w2_ref, gl_ref, pdn_ref, psb_ref, wpd_ref, wps_ref, wo_ref,
             dx1_ref, dw2_ref, dgl_ref, dpdn_ref, dpsb_ref, dodn_ref, dosb_ref):
        i = pl.program_id(0)
        _, vjp = jax.vjp(_rms, x1_ref[...], w2_ref[...])
        dxn, dw2 = vjp(dn2_ref[...])
        dx1 = dx2_ref[...] + dxn
        dx1_ref[...] = dx1

        @pl.when(i == 0)
        def _():
            dw2_ref[...] = jnp.zeros_like(dw2_ref)

        dw2_ref[...] += dw2
        dmix = _dot_nt(dx1.astype(BF16), wo_ref[...])
        gates = jax.nn.sigmoid(gl_ref[...])
        g_dn, g_sb = gates[:, :D_MODEL], gates[:, D_MODEL:]
        dpdn = (dmix * g_dn).astype(BF16)
        dpsb = (dmix * g_sb).astype(BF16)
        dgl_ref[:, :D_MODEL] = (dmix * pdn_ref[...] * g_dn * (1.0 - g_dn)).astype(BF16)
        dgl_ref[:, D_MODEL:] = (dmix * psb_ref[...] * g_sb * (1.0 - g_sb)).astype(BF16)
        dpdn_ref[...] = dpdn
        dpsb_ref[...] = dpsb
        dodn_ref[...] = _dot_nt(dpdn, wpd_ref[...])
        dosb_ref[...] = _dot_nt(dpsb, wps_ref[...]).astype(BF16)

    row = pl.BlockSpec((tb, D_MODEL), lambda i: (i, 0))
    wide = pl.BlockSpec((tb, 2 * D_MODEL), lambda i: (i, 0))
    sq = pl.BlockSpec((D_MODEL, D_MODEL), lambda i: (0, 0))
    vec = pl.BlockSpec((1, D_MODEL), lambda i: (0, 0))
    f = jax.ShapeDtypeStruct((t, D_MODEL), F32)
    b = jax.ShapeDtypeStruct((t, D_MODEL), BF16)
    return pl.pallas_call(
        body, name="merge_bwd", grid=(t // tb,),
        in_specs=[row, row, row, vec, wide, row, row, sq, sq, sq],
        out_specs=[row, vec, wide, row, row, row, row],
        out_shape=[f, jax.ShapeDtypeStruct((1, D_MODEL), F32), jax.ShapeDtypeStruct((t, 2 * D_MODEL), BF16),
                   b, b, f, b],
        compiler_params=_params(("arbitrary",)),
    )(dx2, dn2, x1, w2, gl, pdn, psb, wp_dn, wp_sb, w_out)


def _conv_taps(buf, w_ref, first, rows, cols=slice(None)):
    y = w_ref[0:1, cols] * buf[pl.ds(first, rows), cols]
    for s in range(1, w_ref.shape[0]):
        y = y + w_ref[s:s + 1, cols] * buf[pl.ds(first + s, rows), cols]
    return y


def _ffn_mid_fwd(pre_g, pre_u, wg, wu):
    t, c = pre_g.shape
    kk = wg.shape[0]
    tb, cb = _rows(t), _pick(c, ELEMENTWISE_COLS)
    per = tb // HALO

    def body(g_ref, gh_ref, u_ref, uh_ref, wg_ref, wu_ref, a_ref, gbuf, ubuf):
        i = pl.program_id(0)
        for buf, ref, halo in ((gbuf, g_ref, gh_ref), (ubuf, u_ref, uh_ref)):
            buf[pl.ds(HALO, tb), :] = ref[...]
            buf[pl.ds(0, HALO), :] = jnp.where(i == 0, 0.0, halo[...])
        for j in range(cb // LANES):
            sl = pl.ds(j * LANES, LANES)
            ug = _conv_taps(gbuf, wg_ref, HALO - (kk - 1), tb, sl)
            uu = _conv_taps(ubuf, wu_ref, HALO - (kk - 1), tb, sl)
            a_ref[:, sl] = (_silu(ug) * uu).astype(BF16)

    blk = pl.BlockSpec((tb, cb), lambda i, j: (i, j))
    halo = pl.BlockSpec((HALO, cb), lambda i, j: (jnp.maximum(i * per - 1, 0), j))
    wspec = pl.BlockSpec((kk, cb), lambda i, j: (0, j))
    return pl.pallas_call(
        body, name="ffn_mid_fwd", grid=(t // tb, c // cb),
        in_specs=[blk, halo, blk, halo, wspec, wspec], out_specs=blk,
        out_shape=jax.ShapeDtypeStruct((t, c), BF16),
        scratch_shapes=[pltpu.VMEM((tb + HALO, cb), F32)] * 2,
        compiler_params=_params(("parallel", "parallel")),
    )(pre_g, pre_g, pre_u, pre_u, wg, wu)


def _ffn_mid_bwd(pre_g, pre_u, wg, wu, da):
    t, c = pre_g.shape
    kk = wg.shape[0]
    tb, cb = _rows(t), _pick(c, ELEMENTWISE_COLS)
    per = tb // HALO
    nblk = t // tb
    ext = tb + HALO

    def body(g_ref, gb_ref, ga_ref, u_ref, ub_ref, ua_ref, da_ref, daa_ref, wg_ref, wu_ref,
             dg_ref, du_ref, dwg_ref, dwu_ref, gbuf, ubuf, dabuf, dgbuf, dubuf):
        i = pl.program_id(1)
        last = i == nblk - 1
        for buf, ref, before, after in ((gbuf, g_ref, gb_ref, ga_ref), (ubuf, u_ref, ub_ref, ua_ref)):
            buf[pl.ds(0, HALO), :] = jnp.where(i == 0, 0.0, before[...])
            buf[pl.ds(HALO, tb), :] = ref[...]
            buf[pl.ds(HALO + tb, HALO), :] = jnp.where(last, 0.0, after[...])
        dabuf[pl.ds(0, tb), :] = da_ref[...]
        dabuf[pl.ds(tb, HALO), :] = jnp.where(last, 0.0, daa_ref[...])

        @pl.when(i == 0)
        def _():
            dwg_ref[...] = jnp.zeros_like(dwg_ref)
            dwu_ref[...] = jnp.zeros_like(dwu_ref)

        for j in range(cb // LANES):
            sl = pl.ds(j * LANES, LANES)
            ug = _conv_taps(gbuf, wg_ref, HALO - (kk - 1), ext, sl)
            uu = _conv_taps(ubuf, wu_ref, HALO - (kk - 1), ext, sl)
            _, vjp = jax.vjp(lambda g, u: _silu(g) * u, ug, uu)
            dgbuf[:, sl], dubuf[:, sl] = vjp(dabuf[:, sl])
            for dbuf, xbuf, w_ref, dx_ref, dw_ref in ((dgbuf, gbuf, wg_ref, dg_ref, dwg_ref),
                                                      (dubuf, ubuf, wu_ref, du_ref, dwu_ref)):
                x = xbuf[pl.ds(HALO, tb), sl]
                dx = None
                for s in range(kk):
                    shifted = dbuf[pl.ds(kk - 1 - s, tb), sl]
                    term = w_ref[s:s + 1, sl] * shifted
                    dx = term if dx is None else dx + term
                    dw_ref[s:s + 1, sl] += jnp.sum(shifted * x, axis=0, keepdims=True)
                dx_ref[:, sl] = dx.astype(BF16)

    blk = pl.BlockSpec((tb, cb), lambda j, i: (i, j))
    before = pl.BlockSpec((HALO, cb), lambda j, i: (jnp.maximum(i * per - 1, 0), j))
    after = pl.BlockSpec((HALO, cb), lambda j, i: (jnp.minimum((i + 1) * per, t // HALO - 1), j))
    wspec = pl.BlockSpec((kk, cb), lambda j, i: (0, j))
    dwspec = pl.BlockSpec((HALO, cb), lambda j, i: (0, j))
    half = jax.ShapeDtypeStruct((t, c), BF16)
    dwshape = jax.ShapeDtypeStruct((HALO, c), F32)
    return pl.pallas_call(
        body, name="ffn_mid_bwd", grid=(c // cb, nblk),
        in_specs=[blk, before, after, blk, before, after, blk, after, wspec, wspec],
        out_specs=[blk, blk, dwspec, dwspec],
        out_shape=[half, half, dwshape, dwshape],
        scratch_shapes=[pltpu.VMEM((ext + HALO, cb), F32)] * 2 + [pltpu.VMEM((ext, cb), F32)] * 3,
        compiler_params=_params(("parallel", "arbitrary")),
    )(pre_g, pre_g, pre_g, pre_u, pre_u, pre_u, da, da, wg, wu)


def _down_loss(a, w_down, x1, wf, target):
    t = x1.shape[0]
    tb = _rows(t)

    def body(a_ref, wd_ref, x1_ref, wf_ref, tgt_ref, dx2_ref, dwf_ref, loss_ref):
        i = pl.program_id(0)
        x2 = x1_ref[...] + _dot(a_ref[...], wd_ref[...])
        y, vjp = jax.vjp(_rms, x2, wf_ref[...])
        err = y - tgt_ref[...]
        dx2, dwf = vjp(err * (1.0 / D_MODEL))
        dx2_ref[...] = dx2
        part = jnp.sum(jnp.sum(err * err, axis=1, keepdims=True), axis=0, keepdims=True) * (0.5 / D_MODEL)

        @pl.when(i == 0)
        def _():
            dwf_ref[...] = jnp.zeros_like(dwf_ref)
            loss_ref[...] = jnp.zeros_like(loss_ref)

        dwf_ref[...] += dwf
        loss_ref[...] += jnp.broadcast_to(part, loss_ref.shape)

    row = pl.BlockSpec((tb, D_MODEL), lambda i: (i, 0))
    vec = pl.BlockSpec((1, D_MODEL), lambda i: (0, 0))
    return pl.pallas_call(
        body, name="down_loss", grid=(t // tb,),
        in_specs=[pl.BlockSpec((tb, D_FF), lambda i: (i, 0)), pl.BlockSpec((D_FF, D_MODEL), lambda i: (0, 0)),
                  row, vec, row],
        out_specs=[row, vec, pl.BlockSpec((1, LANES), lambda i: (0, 0))],
        out_shape=[jax.ShapeDtypeStruct((t, D_MODEL), F32), jax.ShapeDtypeStruct((1, D_MODEL), F32),
                   jax.ShapeDtypeStruct((1, LANES), F32)],
        compiler_params=_params(("arbitrary",)),
    )(a, w_down, x1, wf, target)


def _local_step(x, target, wts, late_wire=(), late_weights=None, early_partials=None):
    t = x.shape[0]
    nchunk = t // DN_CHUNK

    n1, hab = _norm1_fwd(x, wts["norm1"], wts["w_ab"])
    dnqkv = _mm(n1, wts["w_dnqkv"], name="h_dnqkv")
    dngate = _mm(n1, wts["w_dngate"], name="h_dngate")
    sbqkv = _mm(n1, wts["w_sbqkv"], out_dtype=BF16, name="h_sbqkv")
    gl = _mm(n1, wts["w_gl"], name="h_gl")

    cdn = _conv_fwd(dnqkv, wts["dn_conv"], "dn_conv_fwd")
    qn, kn, vv, gb = _dn_prep_fwd(cdn, hab, wts["alog"], wts["dtb"])
    per_head = gb[:, :2 * N_HEADS].T.reshape(2 * N_HEADS, nchunk, DN_CHUNK)
    grow, brow = per_head[:N_HEADS, :, None, :], per_head[N_HEADS:, :, None, :]
    u_dn, w_dn, a_qk, qe, kdec, egl, tinv, *late = _dn_local_fwd(qn, kn, vv, grow, brow, late_wire)
    if late_wire:
        wts = {**wts, **late_weights(late)}
    o_raw, states = _dn_seq_fwd(u_dn, w_dn, a_qk, qe, kdec, egl)
    o_dn = _dn_post_fwd(o_raw, dngate, wts["dn_norm"])

    o_sb, tot, sb_used = _sb_fwd(sbqkv)

    pdn, psb, mixed, x1, n2 = _merge_fwd(o_dn, o_sb, gl, x, wts["wp_dn"], wts["wp_sb"], wts["w_out"],
                                         wts["norm2"])
    pre_g = _mm(n2, wts["w_up_g"], name="ffn_up_g")
    pre_u = _mm(n2, wts["w_up_u"], name="ffn_up_u")
    act = _ffn_mid_fwd(pre_g, pre_u, wts["ffn_conv_g"], wts["ffn_conv_u"])
    dx2, d_normf, loss_part = _down_loss(act, wts["w_down"], x1, wts["normf"], target)

    grads = {"normf": d_normf}
    da = _mm(dx2, wts["w_down"], tb=True, name="d_act")
    grads["w_down"] = _mm(act, dx2, ta=True, out_dtype=BF16, name="dw_down")
    dpre_g, dpre_u, dcw_g, dcw_u = _ffn_mid_bwd(pre_g, pre_u, wts["ffn_conv_g"], wts["ffn_conv_u"], da)
    grads["ffn_conv"] = jnp.concatenate([dcw_g[:FFN_CONV], dcw_u[:FFN_CONV]], axis=1)
    dn2 = _mm(dpre_g, wts["w_up_g"], tb=True, name="dn2_g")
    dn2 = _mm(dpre_u, wts["w_up_u"], tb=True, add=dn2, name="dn2_u")
    grads["w_up_g"] = _mm(n2, dpre_g, ta=True, out_dtype=BF16, name="dw_up_g")
    grads["w_up_u"] = _mm(n2, dpre_u, ta=True, out_dtype=BF16, name="dw_up_u")

    dx1, grads["norm2"], dgl, dpdn, dpsb, do_dn, do_sb = _merge_bwd(
        dx2, dn2, x1, wts["norm2"], gl, pdn, psb, wts["wp_dn"], wts["wp_sb"], wts["w_out"])
    grads["w_out"] = _mm(mixed, dx1, ta=True, out_dtype=BF16, name="dw_out")
    grads["wp_dn"] = _mm(o_dn, dpdn, ta=True, out_dtype=BF16, name="dw_proj_dn")
    grads["wp_sb"] = _mm(o_sb, dpsb, ta=True, out_dtype=BF16, name="dw_proj_sb")

    partials = early_partials(grads) if early_partials else ()
    dsq, dsk, dsv = _sb_bwd(sbqkv, tot, sb_used, do_sb)
    dsbqkv = jnp.concatenate([dsq, dsk, dsv], axis=1).astype(BF16)

    do_raw, ddngate, grads["dn_norm"] = _dn_post_bwd(o_raw, dngate, wts["dn_norm"], do_dn)
    seq_grads = _dn_seq_bwd(u_dn, w_dn, a_qk, qe, kdec, egl, states, do_raw)
    dqn, dkn, dvv, dgrow, dbrow, *arrived = _dn_local_bwd(qn, kn, vv, grow, brow, tinv, *seq_grads,
                                                          partials=partials)
    dgb = jnp.concatenate([dgrow.reshape(N_HEADS, t), dbrow.reshape(N_HEADS, t)], axis=0).T
    dgb = jnp.pad(dgb, ((0, 0), (0, LANES - 2 * N_HEADS)))
    dcdn, dhab, grads["alog"], grads["dtb"] = _dn_prep_bwd(cdn, hab, wts["alog"], wts["dtb"], dqn, dkn, dvv, dgb)
    ddnqkv, dcw_dn = _conv_bwd(dcdn, dnqkv, wts["dn_conv"], "dn_conv_bwd", BF16)
    grads["dn_conv"] = dcw_dn[:DN_CONV]

    dn1 = _mm(ddnqkv, wts["w_dnqkv"], tb=True, name="dn1_dnqkv")
    dn1 = _mm(ddngate, wts["w_dngate"], tb=True, add=dn1, name="dn1_dngate")
    dn1 = _mm(dsbqkv, wts["w_sbqkv"], tb=True, add=dn1, name="dn1_sbqkv")
    dn1 = _mm(dgl, wts["w_gl"], tb=True, add=dn1, name="dn1_gl")
    grads["w_dnqkv"] = _mm(n1, ddnqkv, ta=True, out_dtype=BF16, name="dw_dnqkv")
    grads["w_dngate"] = _mm(n1, ddngate, ta=True, out_dtype=BF16, name="dw_dngate")
    grads["w_sbqkv"] = _mm(n1, dsbqkv, ta=True, out_dtype=BF16, name="dw_sbqkv")
    grads["w_gl"] = _mm(n1, dgl, ta=True, out_dtype=BF16, name="dw_gl")
    grads["w_ab"] = _mm(n1, dhab, ta=True, out_dtype=BF16, name="dw_ab")
    grad_x, grads["norm1"] = _norm1_bwd(x, wts["norm1"], dn1, dx1, dhab, wts["w_ab"])
    return loss_part, grad_x, grads, (list(partials), arrived)


def _place():
    return lax.axis_index("x"), lax.axis_index("y"), lax.axis_index("c")


def _hbm_specs(n):
    return [pl.BlockSpec(memory_space=pltpu.HBM)] * n


GATHER_SEMS = 8


def _gather_protocol(ins, outs, send_sems, recv_sems):
    n = len(ins)
    x, y, c = _place()
    me = 2 * x + y
    sibling = (x, y, 1 - c)
    xn, yn, dg = (1 - x, y), (x, 1 - y), (1 - x, 1 - y)
    idx = lambda chip: 2 * chip[0] + chip[1]

    def part(a, chip_index, core, quarter=None):
        half = ins[a].shape[0] // 2
        if quarter is None:
            return outs[a].at[chip_index, pl.ds(core * half, half), :]
        return outs[a].at[chip_index, pl.ds(core * half + quarter * (half // 2), half // 2), :]

    def copy(a, k, src, dst, to):
        return pltpu.make_async_remote_copy(src_ref=src, dst_ref=dst, send_sem=send_sems.at[GATHER_SEMS * a + k],
                                            recv_sem=recv_sems.at[GATHER_SEMS * a + k], device_id=to,
                                            device_id_type=MESH)

    def sent(a, k):
        half = ins[a].shape[0] // 2
        my_half = ins[a].at[pl.ds(c * half, half), :]
        if k < 2:
            return copy(a, k, my_half, part(a, me, c), (*(xn, yn)[k], c))
        if k < 4:
            src = part(a, idx((xn, yn)[k - 2]), c, k - 2)
            return copy(a, k, src, src, (*(yn, xn)[k - 2], c))
        src = (part(a, idx(xn), c), part(a, idx(yn), c), part(a, idx(dg), c, 0), part(a, idx(dg), c, 1))[k - 4]
        return copy(a, k, src, src, sibling)

    def landed(a, k):
        dst = (part(a, idx(xn), c), part(a, idx(yn), c), part(a, idx(dg), c, 0), part(a, idx(dg), c, 1),
               part(a, idx(xn), 1 - c), part(a, idx(yn), 1 - c), part(a, idx(dg), 1 - c, 0),
               part(a, idx(dg), 1 - c, 1))[k]
        return copy(a, k, dst, dst, sibling)

    def begin():
        for a in range(n):
            sent(a, 0).start()
            sent(a, 1).start()

    def middle():
        for a in range(n):
            for k in range(2):
                landed(a, k).wait_recv()
                sent(a, 2 + k).start()
                sent(a, 4 + k).start()

    def end():
        for a in range(n):
            for k in (2, 3):
                landed(a, k).wait_recv()
                sent(a, 4 + k).start()
        for a in range(n):
            for k in range(4, GATHER_SEMS):
                landed(a, k).wait_recv()
        for a in range(n):
            for k in range(GATHER_SEMS):
                sent(a, k).wait_send()

    return begin, middle, end


def _gather_out_shapes(shards):
    return [jax.ShapeDtypeStruct((N_CHIPS,) + s.shape, s.dtype) for s in shards]


def _gather_sems(n):
    return [pltpu.SemaphoreType.DMA((GATHER_SEMS * n,)), pltpu.SemaphoreType.DMA((GATHER_SEMS * n,))]


def _gather_shards(shards):
    n = len(shards)

    def body(*refs):
        begin, middle, end = _gather_protocol(refs[:n], refs[n:2 * n], *refs[2 * n:])
        begin()
        middle()
        end()

    return pl.pallas_call(
        body, name="gather_weights", in_specs=_hbm_specs(n), out_specs=_hbm_specs(n),
        out_shape=_gather_out_shapes(shards), scratch_shapes=_gather_sems(n),
    )(*shards)


def _pair_exchange_halves(gs, tag):
    n = len(gs)

    def body(*refs):
        ins, outs, (send_sems, recv_sems) = refs[:n], refs[n:2 * n], refs[2 * n:]
        x, y, c = _place()
        cps = []
        for a in range(n):
            half = ins[a].shape[1] // 2
            cp = pltpu.make_async_remote_copy(src_ref=ins[a].at[:, pl.ds((1 - c) * half, half), :], dst_ref=outs[a],
                                              send_sem=send_sems.at[a], recv_sem=recv_sems.at[a],
                                              device_id=(x, y, 1 - c), device_id_type=MESH)
            cp.start()
            cps.append(cp)
        for cp in cps:
            cp.wait()

    return pl.pallas_call(
        body, name="grad_pair_exchange_" + tag, in_specs=_hbm_specs(n), out_specs=_hbm_specs(n),
        out_shape=[jax.ShapeDtypeStruct((g.shape[0], g.shape[1] // 2, g.shape[2]), g.dtype) for g in gs],
        scratch_shapes=[pltpu.SemaphoreType.DMA((n,)), pltpu.SemaphoreType.DMA((n,))],
    )(*gs)


def _pick_rows(n, target=1024):
    best = 16
    for b in range(16, min(n, target) + 1, 16):
        if n % b == 0:
            best = b
    return best


def _pair_add(g, got, c_idx, tag):
    nsh, rows, cols = g.shape
    half = rows // 2
    rb = _pick_rows(half)

    def body(c_ref, g_ref, got_ref, o_ref):
        o_ref[...] = (g_ref[...].astype(F32) + got_ref[...].astype(F32)).astype(BF16)

    nb = half // rb
    grid_spec = pltpu.PrefetchScalarGridSpec(
        num_scalar_prefetch=1, grid=(nsh, nb),
        in_specs=[pl.BlockSpec((1, rb, cols), lambda s, i, c_ref: (s, c_ref[0] * nb + i, 0)),
                  pl.BlockSpec((1, rb, cols), lambda s, i, c_ref: (s, i, 0))],
        out_specs=pl.BlockSpec((1, rb, cols), lambda s, i, c_ref: (s, i, 0)))
    return pl.pallas_call(
        body, name="grad_pair_add_" + tag, grid_spec=grid_spec,
        out_shape=jax.ShapeDtypeStruct((nsh, half, cols), BF16),
        compiler_params=_params(("parallel", "parallel")),
    )(c_idx, g, got)


def _chip_exchange_protocol(ins, outs, send_sems, recv_sems):
    x, y, c = _place()
    chips = [(1 - x, y), (x, 1 - y), (1 - x, 1 - y)]

    def copies():
        return [pltpu.make_async_remote_copy(src_ref=ins[a].at[2 * px + py], dst_ref=outs[a].at[j],
                                             send_sem=send_sems.at[3 * a + j], recv_sem=recv_sems.at[3 * a + j],
                                             device_id=(px, py, c), device_id_type=MESH)
                for a in range(len(ins)) for j, (px, py) in enumerate(chips)]

    def begin():
        for cp in copies():
            cp.start()

    def end():
        for cp in copies():
            cp.wait_recv()
        for cp in copies():
            cp.wait_send()

    return begin, end


def _chip_exchange_shapes(ps):
    return [jax.ShapeDtypeStruct((N_CHIPS - 1,) + p.shape[1:], p.dtype) for p in ps]


def _chip_exchange_sems(n):
    return [pltpu.SemaphoreType.DMA((3 * n,)), pltpu.SemaphoreType.DMA((3 * n,))]


def _chip_exchange(ps):
    n = len(ps)

    def body(*refs):
        begin, end = _chip_exchange_protocol(refs[:n], refs[n:2 * n], *refs[2 * n:])
        begin()
        end()

    return pl.pallas_call(
        body, name="grad_chip_exchange", in_specs=_hbm_specs(n), out_specs=_hbm_specs(n),
        out_shape=_chip_exchange_shapes(ps), scratch_shapes=_chip_exchange_sems(n),
    )(*ps)


def _sum_partials(p, got, chip_idx, tag):
    nsh, half, cols = got.shape
    rb = _pick_rows(half)

    def body(me_ref, p_ref, got_ref, o_ref):
        acc = p_ref[0].astype(F32)
        for s in range(nsh):
            acc = acc + got_ref[s].astype(F32)
        o_ref[...] = acc

    grid_spec = pltpu.PrefetchScalarGridSpec(
        num_scalar_prefetch=1, grid=(half // rb,),
        in_specs=[pl.BlockSpec((1, rb, cols), lambda i, me_ref: (me_ref[0], i, 0)),
                  pl.BlockSpec((nsh, rb, cols), lambda i, me_ref: (0, i, 0))],
        out_specs=pl.BlockSpec((rb, cols), lambda i, me_ref: (i, 0)))
    return pl.pallas_call(
        body, name="grad_sum_chips_" + tag, grid_spec=grid_spec,
        out_shape=jax.ShapeDtypeStruct((half, cols), F32),
        compiler_params=_params(("parallel",)),
    )(chip_idx, p, got)


def _pair_share(rs):
    n = len(rs)

    def body(*refs):
        ins, outs, (send_sems, recv_sems) = refs[:n], refs[n:2 * n], refs[2 * n:]
        x, y, c = _place()
        cps = []
        for a in range(n):
            cp = pltpu.make_async_remote_copy(src_ref=ins[a], dst_ref=outs[a], send_sem=send_sems.at[a],
                                              recv_sem=recv_sems.at[a], device_id=(x, y, 1 - c),
                                              device_id_type=MESH)
            cp.start()
            cps.append(cp)
        for cp in cps:
            cp.wait()

    return pl.pallas_call(
        body, name="grad_pair_share", in_specs=_hbm_specs(n), out_specs=_hbm_specs(n),
        out_shape=[jax.ShapeDtypeStruct(r.shape, r.dtype) for r in rs],
        scratch_shapes=[pltpu.SemaphoreType.DMA((n,)), pltpu.SemaphoreType.DMA((n,))],
    )(*rs)


def _small_allreduce(v):
    rows, cols = v.shape
    ndev = 8

    def body(in_ref, out_ref, slots, send_sems, recv_sems):
        x, y, c = _place()
        me = 4 * x + 2 * y + c
        slots[me] = in_ref[...]
        sends = []
        for k in range(1, ndev):
            peer = (x ^ (k >> 2), y ^ ((k >> 1) & 1), c ^ (k & 1))
            cp = pltpu.make_async_remote_copy(src_ref=in_ref, dst_ref=slots.at[me], send_sem=send_sems.at[k - 1],
                                              recv_sem=recv_sems.at[k - 1], device_id=peer, device_id_type=MESH)
            cp.start()
            sends.append(cp)
        for k in range(1, ndev):
            there = slots.at[me ^ k]
            pltpu.make_async_remote_copy(src_ref=there, dst_ref=there, send_sem=send_sems.at[k - 1],
                                         recv_sem=recv_sems.at[k - 1], device_id=(x, y, c),
                                         device_id_type=MESH).wait_recv()
        for cp in sends:
            cp.wait_send()
        acc = slots[0]
        for s in range(1, ndev):
            acc = acc + slots[s]
        out_ref[...] = acc

    return pl.pallas_call(
        body, name="small_allreduce",
        in_specs=[pl.BlockSpec(memory_space=pltpu.VMEM)],
        out_specs=pl.BlockSpec(memory_space=pltpu.VMEM),
        out_shape=jax.ShapeDtypeStruct((rows, cols), F32),
        scratch_shapes=[pltpu.VMEM((ndev, rows, cols), F32), pltpu.SemaphoreType.DMA((ndev - 1,)),
                        pltpu.SemaphoreType.DMA((ndev - 1,))],
    )(v)


def _adamw(w, g, m, v, name):
    r, c = w.shape
    rb = r if r <= 128 else _pick_rows_8(r, 128)
    c1 = 1.0 - ADAM_B1 ** ADAM_STEP
    c2 = 1.0 - ADAM_B2 ** ADAM_STEP

    def body(w_ref, g_ref, m_ref, v_ref, d_ref, nm_ref, nv_ref):
        gg = g_ref[...]
        nm = ADAM_B1 * m_ref[...] + (1.0 - ADAM_B1) * gg
        nv = ADAM_B2 * v_ref[...] + (1.0 - ADAM_B2) * (gg * gg)
        d_ref[...] = -ADAM_LR * ((nm / c1) / (jnp.sqrt(nv / c2) + ADAM_EPS) + ADAM_WD * w_ref[...])
        nm_ref[...] = nm
        nv_ref[...] = nv

    blk = pl.BlockSpec((rb, c), lambda i: (i, 0))
    shp = jax.ShapeDtypeStruct((r, c), F32)
    return pl.pallas_call(
        body, name=name, grid=(r // rb,), in_specs=[blk] * 4, out_specs=[blk] * 3, out_shape=[shp] * 3,
        compiler_params=_params(("parallel",)),
    )(w, g, m, v)


def _pick_rows_8(n, target):
    best = n
    for b in range(8, min(n, target) + 1, 8):
        if n % b == 0:
            best = b
    return best


W_IN_COLS = 2308
W_UP_COLS = 1408
W_DOWN_ROWS = 704
DN_CONV_COLS = 768
FFN_CONV_COLS = 1408
PROJ_ROWS = 256
ROW_TILE = 16
ROW_SEGS = [("wp_dn", PROJ_ROWS), ("wp_sb", PROJ_ROWS), ("w_out", PROJ_ROWS), ("w_down", W_DOWN_ROWS),
            ("dn_conv", ROW_TILE), ("ffn_conv", ROW_TILE), ("spare", 2 * ROW_TILE)]
ROW_OFFS = {nm: (sum(n for _, n in ROW_SEGS[:i]), n) for i, (nm, n) in enumerate(ROW_SEGS)}
STACK_ROWS = sum(n for _, n in ROW_SEGS)
assert all(n % ROW_TILE == 0 for _, n in ROW_SEGS) and STACK_ROWS % (4 * ROW_TILE) == 0
Q_END, A_END, G_END, S_END = 3 * D_MODEL, 3 * D_MODEL + 2 * N_HEADS, 4 * D_MODEL + 2 * N_HEADS, 7 * D_MODEL + 2 * N_HEADS


def _flat_rows(a, nrows):
    flat = a.reshape(-1)
    return jnp.pad(flat, (0, nrows * D_MODEL - flat.shape[0])).reshape(nrows, D_MODEL)


IN_EXTRA_ROWS = 64


def _weight_wire(w_in, wp_dn, wp_sb, w_out, w_up, w_down, dn_conv, ffn_conv):
    bits = lax.bitcast_convert_type(dn_conv, BF16).reshape(-1)
    extra = jnp.pad(bits, (0, IN_EXTRA_ROWS * W_IN_COLS - bits.shape[0])).reshape(IN_EXTRA_ROWS, W_IN_COLS)
    stack = jnp.concatenate([wp_dn.astype(BF16), wp_sb.astype(BF16), w_out.astype(BF16), w_down.astype(BF16),
                             jnp.zeros((ROW_TILE, D_MODEL), BF16),
                             _flat_rows(lax.bitcast_convert_type(ffn_conv, BF16), ROW_TILE),
                             jnp.zeros((ROW_OFFS["spare"][1], D_MODEL), BF16)], axis=0)
    return [jnp.concatenate([w_in.astype(BF16), extra], axis=0)], [w_up.astype(BF16), stack]


def _col_range(g, lo, hi, width):
    parts = []
    for s in range(g.shape[0]):
        a, b = max(lo, s * width), min(hi, (s + 1) * width)
        if a < b:
            parts.append(g[s][:, a - s * width:b - s * width])
    return parts[0] if len(parts) == 1 else jnp.concatenate(parts, axis=1)


def _f32_rows(raw, k, ncols):
    raw = raw.reshape(N_CHIPS, -1)[:, :2 * k * ncols].reshape(N_CHIPS, k * ncols, 2)
    vals = lax.bitcast_convert_type(raw, F32).reshape(N_CHIPS, k, ncols)
    return vals.transpose(1, 0, 2).reshape(k, N_CHIPS * ncols)


def _unpack_early(g_in):
    w = g_in[:, :D_MODEL, :]
    return {
        "w_dnqkv": _col_range(w, 0, Q_END, W_IN_COLS),
        "w_ab": jnp.pad(_col_range(w, Q_END, A_END, W_IN_COLS), ((0, 0), (0, LANES - 2 * N_HEADS))),
        "w_dngate": _col_range(w, A_END, G_END, W_IN_COLS),
        "w_sbqkv": _col_range(w, G_END, S_END, W_IN_COLS),
        "w_gl": _col_range(w, S_END, N_CHIPS * W_IN_COLS, W_IN_COLS),
        "dn_conv": _f32_rows(g_in[:, D_MODEL:, :], DN_CONV, DN_CONV_COLS),
    }


def _unpack_late(g_up, g_stack):
    def seg(nm):
        at, n = ROW_OFFS[nm]
        return g_stack[:, at:at + n, :]

    ffn_conv = _f32_rows(seg("ffn_conv"), FFN_CONV, FFN_CONV_COLS)
    return {
        "wp_dn": seg("wp_dn").reshape(D_MODEL, D_MODEL),
        "wp_sb": seg("wp_sb").reshape(D_MODEL, D_MODEL),
        "w_out": seg("w_out").reshape(D_MODEL, D_MODEL),
        "w_up_g": _col_range(g_up, 0, D_FF, W_UP_COLS), "w_up_u": _col_range(g_up, D_FF, 2 * D_FF, W_UP_COLS),
        "w_down": seg("w_down").reshape(D_FF, D_MODEL),
        "ffn_conv_g": ffn_conv[:, :D_FF], "ffn_conv_u": ffn_conv[:, D_FF:],
    }


def _grad_wire_early(gr):
    def cols(a, ncols):
        return a.reshape(a.shape[0], N_CHIPS, ncols).transpose(1, 0, 2)

    def rows(a, nrows):
        return a.astype(BF16).reshape(N_CHIPS, nrows, a.shape[1])

    def flat(a, nrows):
        a = a.astype(BF16).reshape(N_CHIPS, -1)
        return jnp.pad(a, ((0, 0), (0, nrows * D_MODEL - a.shape[1]))).reshape(N_CHIPS, nrows, D_MODEL)

    up = [gr["w_up_g"], gr["w_up_u"]]
    g_up = jnp.stack([up[s // 2][:, (s % 2) * W_UP_COLS:(s % 2 + 1) * W_UP_COLS].astype(BF16) for s in range(N_CHIPS)])
    g_stack = jnp.concatenate([rows(gr["wp_dn"], PROJ_ROWS), rows(gr["wp_sb"], PROJ_ROWS), rows(gr["w_out"], PROJ_ROWS),
                               rows(gr["w_down"], W_DOWN_ROWS), jnp.zeros((N_CHIPS, ROW_TILE, D_MODEL), BF16),
                               flat(cols(gr["ffn_conv"], FFN_CONV_COLS), ROW_TILE),
                               jnp.zeros((N_CHIPS, ROW_OFFS["spare"][1], D_MODEL), BF16)], axis=1)
    return [g_up, g_stack]


def _grad_wire_late(gr):
    pieces = [(gr["w_dnqkv"], 0), (gr["w_ab"][:, :2 * N_HEADS], Q_END), (gr["w_dngate"], A_END),
              (gr["w_sbqkv"], G_END), (gr["w_gl"], S_END)]
    conv = gr["dn_conv"].reshape(DN_CONV, N_CHIPS, DN_CONV_COLS).transpose(1, 0, 2).reshape(N_CHIPS, -1)

    def block(s):
        lo, hi = s * W_IN_COLS, (s + 1) * W_IN_COLS
        parts = []
        for a, at in pieces:
            b0, b1 = max(lo, at), min(hi, at + a.shape[1])
            if b0 < b1:
                parts.append(a[:, b0 - at:b1 - at].astype(BF16))
        w = parts[0] if len(parts) == 1 else jnp.concatenate(parts, axis=1)
        extra = jnp.pad(conv[s].astype(BF16), (0, IN_EXTRA_ROWS * W_IN_COLS - conv.shape[1]))
        return jnp.concatenate([w, extra.reshape(IN_EXTRA_ROWS, W_IN_COLS)], axis=0)

    return [jnp.stack([block(s) for s in range(N_CHIPS)])]


def _unpack_grad_shard(r_in, r_up, r_stack):
    def seg(nm):
        at, n = ROW_OFFS[nm]
        return r_stack[at:at + n, :]

    return {
        "w_in": r_in[:D_MODEL], "w_up": r_up,
        "wp_dn": seg("wp_dn"), "wp_sb": seg("wp_sb"), "w_out": seg("w_out"), "w_down": seg("w_down"),
        "dn_conv": r_in[D_MODEL:].reshape(-1)[:DN_CONV * DN_CONV_COLS].reshape(DN_CONV, DN_CONV_COLS),
        "ffn_conv": seg("ffn_conv").reshape(-1)[:FFN_CONV * FFN_CONV_COLS].reshape(FFN_CONV, FFN_CONV_COLS),
    }


def _lane_row(v):
    return jnp.pad(v.reshape(1, -1), ((0, 0), (0, LANES - v.size)))


def kernel(x, norm1_w, w_in, dn_conv_w, dn_A_log, dn_dt_bias, dn_norm_w, w_proj_dn, w_proj_sb, w_out, norm2_w, ffn_w_up, ffn_conv_w, ffn_w_down, norm_f_w, loss_target, m_norm1_w, m_w_in, m_dn_conv_w, m_dn_A_log, m_dn_dt_bias, m_dn_norm_w, m_w_proj_dn, m_w_proj_sb, m_w_out, m_norm2_w, m_ffn_w_up, m_ffn_conv_w, m_ffn_w_down, m_norm_f_w, v_norm1_w, v_w_in, v_dn_conv_w, v_dn_A_log, v_dn_dt_bias, v_dn_norm_w, v_w_proj_dn, v_w_proj_sb, v_w_out, v_norm2_w, v_ffn_w_up, v_ffn_conv_w, v_ffn_w_down, v_norm_f_w):
    early, late = _weight_wire(w_in[0], w_proj_dn[0], w_proj_sb[0], w_out[0], ffn_w_up[0], ffn_w_down[0],
                               dn_conv_w[0], ffn_conv_w[0])
    chip_idx = (2 * lax.axis_index("x") + lax.axis_index("y")).astype(jnp.int32)

    def with_mine(gathered, wire):
        return [lax.dynamic_update_slice(g, mine[None], (chip_idx, 0, 0)) for g, mine in zip(gathered, wire)]

    wts = _unpack_early(*with_mine(_gather_shards(early), early))
    wts.update(norm1=norm1_w, norm2=norm2_w, normf=norm_f_w.reshape(1, D_MODEL), dn_norm=dn_norm_w,
               alog=_lane_row(dn_A_log), dtb=_lane_row(dn_dt_bias))

    c_idx = lax.axis_index("c").astype(jnp.int32).reshape(1)

    def pair_sums(wire_g, tags, when):
        return [_pair_add(g, got, c_idx, tag) for g, got, tag in zip(wire_g, _pair_exchange_halves(wire_g, when), tags)]

    loss_part, grad_x, gr, (early_sums, early_arrived) = _local_step(
        x[0], loss_target[0], wts, late, lambda gathered: _unpack_late(*with_mine(gathered, late)),
        lambda grads: pair_sums(_grad_wire_early(grads), ["w_up", "rows"], "early"))

    late_sums = pair_sums(_grad_wire_late(gr), ["w_in"], "late")
    tags = ["w_in", "w_up", "rows"]
    reduced = [_sum_partials(p, got, chip_idx.reshape(1), tag)
               for p, got, tag in zip(late_sums + early_sums, list(_chip_exchange(late_sums)) + list(early_arrived), tags)]
    is_south = lax.axis_index("c") == 0
    gsh = _unpack_grad_shard(*[jnp.concatenate([jnp.where(is_south, mine, other), jnp.where(is_south, other, mine)],
                                               axis=0) for mine, other in zip(reduced, _pair_share(reduced))])

    tail = jnp.concatenate([gr["dn_norm"], gr["alog"][:, :N_HEADS], gr["dtb"][:, :N_HEADS], loss_part[:, :1]], axis=1)
    small = jnp.concatenate([gr["norm1"], gr["norm2"], gr["normf"],
                             jnp.pad(tail, ((0, 0), (0, D_MODEL - tail.shape[1]))),
                             jnp.zeros((SMALL_ROWS - 4, D_MODEL), F32)], axis=0)
    small = _small_allreduce(small)
    at = HEAD_DIM
    g_small = {"norm1_w": small[0:1], "norm2_w": small[1:2], "norm_f_w": small[2],
               "dn_norm_w": small[3:4, :at], "dn_A_log": small[3:4, at:at + N_HEADS],
               "dn_dt_bias": small[3:4, at + N_HEADS:at + 2 * N_HEADS]}
    loss = small[3, at + 2 * N_HEADS]

    big = {"w_in": (w_in, m_w_in, v_w_in, gsh["w_in"]), "dn_conv_w": (dn_conv_w, m_dn_conv_w, v_dn_conv_w, gsh["dn_conv"]),
           "w_proj_dn": (w_proj_dn, m_w_proj_dn, v_w_proj_dn, gsh["wp_dn"]),
           "w_proj_sb": (w_proj_sb, m_w_proj_sb, v_w_proj_sb, gsh["wp_sb"]),
           "w_out": (w_out, m_w_out, v_w_out, gsh["w_out"]),
           "ffn_w_up": (ffn_w_up, m_ffn_w_up, v_ffn_w_up, gsh["w_up"]),
           "ffn_conv_w": (ffn_conv_w, m_ffn_conv_w, v_ffn_conv_w, gsh["ffn_conv"]),
           "ffn_w_down": (ffn_w_down, m_ffn_w_down, v_ffn_w_down, gsh["w_down"])}
    res = {}
    for nm, (w, m, v, g) in big.items():
        d, nm_, nv_ = _adamw(w[0], g, m[0], v[0], "adamw_" + nm)
        res[nm] = (g[None], d[None], nm_[None], nv_[None])

    names = ["norm1_w", "norm2_w", "norm_f_w", "dn_norm_w", "dn_A_log", "dn_dt_bias"]
    given = {"norm1_w": (norm1_w, m_norm1_w, v_norm1_w), "norm2_w": (norm2_w, m_norm2_w, v_norm2_w),
             "norm_f_w": (norm_f_w, m_norm_f_w, v_norm_f_w), "dn_norm_w": (dn_norm_w, m_dn_norm_w, v_dn_norm_w),
             "dn_A_log": (dn_A_log, m_dn_A_log, v_dn_A_log), "dn_dt_bias": (dn_dt_bias, m_dn_dt_bias, v_dn_dt_bias)}

    def stack(k, fill):
        rows = [jnp.pad(given[nm][k].reshape(1, -1), ((0, 0), (0, D_MODEL - given[nm][k].size)),
                        constant_values=fill) for nm in names]
        return jnp.concatenate(rows + [jnp.full((SMALL_ROWS - len(names), D_MODEL), fill, F32)], axis=0)

    g_rows = jnp.concatenate(
        [jnp.pad(g_small[nm].reshape(1, -1), ((0, 0), (0, D_MODEL - g_small[nm].size))) for nm in names]
        + [jnp.zeros((SMALL_ROWS - len(names), D_MODEL), F32)], axis=0)
    d_s, m_s, v_s = _adamw(stack(0, 0.0), g_rows, stack(1, 0.0), stack(2, 1.0), "adamw_small")
    for r, nm in enumerate(names):
        shape = given[nm][0].shape
        n = given[nm][0].size
        res[nm] = (g_small[nm].reshape(shape), d_s[r, :n].reshape(shape), m_s[r, :n].reshape(shape),
                   v_s[r, :n].reshape(shape))

    order = ["norm1_w", "w_in", "dn_conv_w", "dn_A_log", "dn_dt_bias", "dn_norm_w", "w_proj_dn", "w_proj_sb",
             "w_out", "norm2_w", "ffn_w_up", "ffn_conv_w", "ffn_w_down", "norm_f_w"]
    outs = [loss, grad_x[None]]
    for k in range(4):
        outs += [res[nm][k] for nm in order]
    return tuple(outs)
```

```python
import functools

import jax
import jax.numpy as jnp
from jax import lax
from jax.experimental import pallas as pl
from jax.experimental.pallas import tpu as pltpu

F32 = jnp.float32
BF16 = jnp.bfloat16
HIGHEST = lax.Precision.HIGHEST
MESH = pl.DeviceIdType.MESH

EPS = 1e-6
D_MODEL = 1024
N_HEADS = 8
HEAD_DIM = 128
DN_CONV = 4
DN_CHUNK = 64
D_FF = 2816
FFN_CONV = 3
ADAM_LR, ADAM_B1, ADAM_B2, ADAM_EPS, ADAM_WD, ADAM_STEP = 0.001, 0.9, 0.999, 1e-08, 0.01, 10

N_CHIPS = 4
LANES = 128
HALO = 8
VMEM_LIMIT = 48 * 1024 * 1024
SMALL_ROWS = 8


def _params(sem=None):
    return pltpu.CompilerParams(dimension_semantics=sem, vmem_limit_bytes=VMEM_LIMIT)


def _pick(n, target):
    best = None
    for b in range(LANES, min(n, target) + 1, LANES):
        if n % b == 0:
            best = b
    return best or n


ELEMENTWISE_COLS = 1408


def _rows(t, target=256):
    return min(t, target)


def _dot(a, b, precision=None):
    return lax.dot_general(a, b, (((1,), (0,)), ((), ())), precision=precision, preferred_element_type=F32)


def _dot_nt(a, b, precision=None):
    return lax.dot_general(a, b, (((1,), (1,)), ((), ())), precision=precision, preferred_element_type=F32)


def _dot_tn(a, b, precision=None):
    return lax.dot_general(a, b, (((0,), (0,)), ((), ())), precision=precision, preferred_element_type=F32)


def _rms(x, w):
    return x * lax.rsqrt(jnp.mean(x * x, axis=-1, keepdims=True) + EPS) * w


def _silu(x):
    return x * jax.nn.sigmoid(x)


def _softplus(x):
    return jnp.maximum(x, 0.0) + jnp.log(1.0 + jnp.exp(-jnp.abs(x)))


MM_BLOCK = 1408
MM_VMEM_BUDGET = 38 * 1024 * 1024


def _mm(a, b, *, ta=False, tb=False, add=None, out_dtype=F32, name, bm=MM_BLOCK, bn=MM_BLOCK, bk=MM_BLOCK):
    m = a.shape[1] if ta else a.shape[0]
    k = a.shape[0] if ta else a.shape[1]
    n = b.shape[0] if tb else b.shape[1]
    bm, bn = _pick(m, bm), _pick(n, bn)

    def vmem_need(bk_):
        need = 2 * (bm * bk_ * a.dtype.itemsize + bk_ * bn * b.dtype.itemsize) + 2 * bm * bn * jnp.dtype(out_dtype).itemsize
        need += 2 * bm * bn * add.dtype.itemsize if add is not None else 0
        return need + (bm * bn * 4 if bk_ < k else 0)

    bk = max((d for d in range(LANES, k + 1, LANES) if k % d == 0 and vmem_need(d) <= MM_VMEM_BUDGET),
             default=_pick(k, bk))
    nk = k // bk
    dims = (((0 if ta else 1,), (1 if tb else 0,)), ((), ()))

    def body(*refs):
        a_ref, b_ref = refs[:2]
        c_ref = refs[2] if add is not None else None
        o_ref = refs[3] if add is not None else refs[2]
        acc = refs[-1]
        kk = pl.program_id(2)
        part = lax.dot_general(a_ref[...].astype(BF16), b_ref[...].astype(BF16), dims, preferred_element_type=F32)

        def finish(r):
            if add is not None:
                r = r + c_ref[...].astype(F32)
            o_ref[...] = r.astype(out_dtype)

        if nk == 1:
            finish(part)
            return

        @pl.when(kk == 0)
        def _():
            acc[...] = part

        @pl.when(jnp.logical_and(kk > 0, kk < nk - 1))
        def _():
            acc[...] += part

        @pl.when(kk == nk - 1)
        def _():
            finish(acc[...] + part)

    a_spec = (pl.BlockSpec((bk, bm), lambda i, j, kk: (kk, i)) if ta
              else pl.BlockSpec((bm, bk), lambda i, j, kk: (i, kk)))
    b_spec = (pl.BlockSpec((bn, bk), lambda i, j, kk: (j, kk)) if tb
              else pl.BlockSpec((bk, bn), lambda i, j, kk: (kk, j)))
    o_spec = pl.BlockSpec((bm, bn), lambda i, j, kk: (i, j))
    in_specs = [a_spec, b_spec] + ([o_spec] if add is not None else [])
    args = (a, b) + ((add,) if add is not None else ())
    return pl.pallas_call(
        body, name=name, grid=(m // bm, n // bn, nk),
        in_specs=in_specs, out_specs=o_spec,
        out_shape=jax.ShapeDtypeStruct((m, n), out_dtype),
        scratch_shapes=[pltpu.VMEM((bm, bn), F32)] if nk > 1 else [],
        compiler_params=_params(("parallel", "parallel", "arbitrary")),
    )(*args)


def _norm1_fwd(x, w, w_ab):
    t = x.shape[0]
    tb = _rows(t)

    def body(x_ref, w_ref, wab_ref, n_ref, hab_ref):
        n = _rms(x_ref[...], w_ref[...]).astype(BF16)
        n_ref[...] = n
        hab_ref[...] = _dot(n, wab_ref[...])

    return pl.pallas_call(
        body, name="norm1_fwd", grid=(t // tb,),
        in_specs=[pl.BlockSpec((tb, D_MODEL), lambda i: (i, 0)),
                  pl.BlockSpec((1, D_MODEL), lambda i: (0, 0)),
                  pl.BlockSpec((D_MODEL, LANES), lambda i: (0, 0))],
        out_specs=[pl.BlockSpec((tb, D_MODEL), lambda i: (i, 0)),
                   pl.BlockSpec((tb, LANES), lambda i: (i, 0))],
        out_shape=[jax.ShapeDtypeStruct((t, D_MODEL), BF16), jax.ShapeDtypeStruct((t, LANES), F32)],
        compiler_params=_params(("arbitrary",)),
    )(x, w, w_ab)


def _norm1_bwd(x, w, dn, dres, dab, w_ab):
    t = x.shape[0]
    tb = _rows(t)

    def body(x_ref, w_ref, dn_ref, dres_ref, dab_ref, wab_ref, dx_ref, dw_ref):
        i = pl.program_id(0)
        g = dn_ref[...] + _dot_nt(dab_ref[...].astype(BF16), wab_ref[...])
        _, vjp = jax.vjp(_rms, x_ref[...], w_ref[...])
        dx, dw = vjp(g)
        dx_ref[...] = dres_ref[...] + dx

        @pl.when(i == 0)
        def _():
            dw_ref[...] = jnp.zeros_like(dw_ref)

        dw_ref[...] += dw

    row = pl.BlockSpec((tb, D_MODEL), lambda i: (i, 0))
    vec = pl.BlockSpec((1, D_MODEL), lambda i: (0, 0))
    return pl.pallas_call(
        body, name="norm1_bwd", grid=(t // tb,),
        in_specs=[row, vec, row, row, pl.BlockSpec((tb, LANES), lambda i: (i, 0)),
                  pl.BlockSpec((D_MODEL, LANES), lambda i: (0, 0))],
        out_specs=[row, vec],
        out_shape=[jax.ShapeDtypeStruct((t, D_MODEL), F32), jax.ShapeDtypeStruct((1, D_MODEL), F32)],
        compiler_params=_params(("arbitrary",)),
    )(x, w, dn, dres, dab, w_ab)


def _conv_fwd(x, w, name):
    t, c = x.shape
    kk = w.shape[0]
    tb, cb = _rows(t, 512), _pick(c, ELEMENTWISE_COLS)
    per = tb // HALO

    def body(x_ref, halo_ref, w_ref, y_ref, buf):
        i = pl.program_id(0)
        buf[pl.ds(HALO, tb), :] = x_ref[...]
        buf[pl.ds(0, HALO), :] = jnp.where(i == 0, 0.0, halo_ref[...])
        y_ref[...] = _conv_taps(buf, w_ref, HALO - (kk - 1), tb)

    return pl.pallas_call(
        body, name=name, grid=(t // tb, c // cb),
        in_specs=[pl.BlockSpec((tb, cb), lambda i, j: (i, j)),
                  pl.BlockSpec((HALO, cb), lambda i, j: (jnp.maximum(i * per - 1, 0), j)),
                  pl.BlockSpec((kk, cb), lambda i, j: (0, j))],
        out_specs=pl.BlockSpec((tb, cb), lambda i, j: (i, j)),
        out_shape=jax.ShapeDtypeStruct((t, c), F32),
        scratch_shapes=[pltpu.VMEM((tb + HALO, cb), F32)],
        compiler_params=_params(("parallel", "parallel")),
    )(x, x, w)


def _conv_bwd(dy, x, w, name, dx_dtype):
    t, c = x.shape
    kk = w.shape[0]
    tb, cb = _rows(t, 512), _pick(c, ELEMENTWISE_COLS)
    per = tb // HALO
    nblk = t // tb

    def body(dy_ref, after_ref, x_ref, w_ref, dx_ref, dw_ref, dbuf):
        i = pl.program_id(1)
        dbuf[pl.ds(0, tb), :] = dy_ref[...]
        dbuf[pl.ds(tb, HALO), :] = jnp.where(i == nblk - 1, 0.0, after_ref[...])

        @pl.when(i == 0)
        def _():
            dw_ref[...] = jnp.zeros_like(dw_ref)

        for j in range(cb // LANES):
            sl = pl.ds(j * LANES, LANES)
            x = x_ref[:, sl]
            dx = None
            for s in range(kk):
                shifted = dbuf[pl.ds(kk - 1 - s, tb), sl]
                term = w_ref[s:s + 1, sl] * shifted
                dx = term if dx is None else dx + term
                dw_ref[s:s + 1, sl] += jnp.sum(shifted * x, axis=0, keepdims=True)
            dx_ref[:, sl] = dx.astype(dx_dtype)

    blk = pl.BlockSpec((tb, cb), lambda j, i: (i, j))
    return pl.pallas_call(
        body, name=name, grid=(c // cb, nblk),
        in_specs=[blk,
                  pl.BlockSpec((HALO, cb), lambda j, i: (jnp.minimum((i + 1) * per, t // HALO - 1), j)),
                  blk,
                  pl.BlockSpec((kk, cb), lambda j, i: (0, j))],
        out_specs=[blk, pl.BlockSpec((HALO, cb), lambda j, i: (0, j))],
        out_shape=[jax.ShapeDtypeStruct((t, c), dx_dtype), jax.ShapeDtypeStruct((HALO, c), F32)],
        scratch_shapes=[pltpu.VMEM((tb + HALO, cb), F32)],
        compiler_params=_params(("parallel", "arbitrary")),
    )(dy, dy, x, w)


def _dn_head(c, normed):
    s = _silu(c)
    return s * lax.rsqrt(jnp.sum(s * s, axis=-1, keepdims=True) + EPS) if normed else s


def _dn_gates(hab, alog, dtb):
    lane = lax.broadcasted_iota(jnp.int32, hab.shape, 1)
    g = -jnp.exp(alog) * _softplus(hab + dtb)
    beta = jax.nn.sigmoid(hab)
    return jnp.where(lane < N_HEADS, g, jnp.where(lane < 2 * N_HEADS, beta, 0.0))


def _dn_head_slices(q_ref, k_ref, v_ref):
    return [(pl.ds((part * N_HEADS + h) * HEAD_DIM, HEAD_DIM), ref, h, part < 2)
            for part, ref in enumerate((q_ref, k_ref, v_ref)) for h in range(N_HEADS)]


def _dn_prep_fwd(c, hab, alog, dtb):
    t = c.shape[0]
    tb = _rows(t)

    def body(c_ref, hab_ref, alog_ref, dtb_ref, q_ref, k_ref, v_ref, gb_ref):
        for sl, ref, h, normed in _dn_head_slices(q_ref, k_ref, v_ref):
            ref[h] = _dn_head(c_ref[:, sl], normed)
        gb_ref[...] = _dn_gates(hab_ref[...], alog_ref[...], dtb_ref[...])

    hm = pl.BlockSpec((N_HEADS, tb, HEAD_DIM), lambda i: (0, i, 0))
    nar = pl.BlockSpec((tb, LANES), lambda i: (i, 0))
    vec = pl.BlockSpec((1, LANES), lambda i: (0, 0))
    return pl.pallas_call(
        body, name="dn_prep_fwd", grid=(t // tb,),
        in_specs=[pl.BlockSpec((tb, 3 * D_MODEL), lambda i: (i, 0)), nar, vec, vec],
        out_specs=[hm, hm, hm, nar],
        out_shape=[jax.ShapeDtypeStruct((N_HEADS, t, HEAD_DIM), F32)] * 3 + [jax.ShapeDtypeStruct((t, LANES), F32)],
        compiler_params=_params(("parallel",)),
    )(c, hab, alog, dtb)


def _dn_prep_bwd(c, hab, alog, dtb, dq, dk, dv, dgb):
    t = c.shape[0]
    tb = _rows(t)

    def body(c_ref, hab_ref, alog_ref, dtb_ref, dq_ref, dk_ref, dv_ref, dgb_ref,
             dc_ref, dhab_ref, dalog_ref, ddtb_ref):
        i = pl.program_id(0)
        for sl, ref, h, normed in _dn_head_slices(dq_ref, dk_ref, dv_ref):
            _, vjp = jax.vjp(functools.partial(_dn_head, normed=normed), c_ref[:, sl])
            dc_ref[:, sl] = vjp(ref[h])[0]
        _, vjp = jax.vjp(_dn_gates, hab_ref[...], alog_ref[...], dtb_ref[...])
        dhab, dalog, ddtb = vjp(dgb_ref[...])
        dhab_ref[...] = dhab

        @pl.when(i == 0)
        def _():
            dalog_ref[...] = jnp.zeros_like(dalog_ref)
            ddtb_ref[...] = jnp.zeros_like(ddtb_ref)

        dalog_ref[...] += dalog
        ddtb_ref[...] += ddtb

    hm = pl.BlockSpec((N_HEADS, tb, HEAD_DIM), lambda i: (0, i, 0))
    wide = pl.BlockSpec((tb, 3 * D_MODEL), lambda i: (i, 0))
    nar = pl.BlockSpec((tb, LANES), lambda i: (i, 0))
    vec = pl.BlockSpec((1, LANES), lambda i: (0, 0))
    return pl.pallas_call(
        body, name="dn_prep_bwd", grid=(t // tb,),
        in_specs=[wide, nar, vec, vec, hm, hm, hm, nar],
        out_specs=[wide, nar, vec, vec],
        out_shape=[jax.ShapeDtypeStruct((t, 3 * D_MODEL), F32), jax.ShapeDtypeStruct((t, LANES), F32),
                   jax.ShapeDtypeStruct((1, LANES), F32), jax.ShapeDtypeStruct((1, LANES), F32)],
        compiler_params=_params(("arbitrary",)),
    )(c, hab, alog, dtb, dq, dk, dv, dgb)


DN_PREC = lax.Precision.HIGH
DN_GROUP = 8


def _dn_prec(a):
    return DN_PREC if a.dtype == F32 else None


def _bdot(a, b):
    return lax.dot_general(a, b, (((2,), (1,)), ((0,), (0,))), precision=_dn_prec(a), preferred_element_type=F32)


def _bdot_nt(a, b):
    return lax.dot_general(a, b, (((2,), (2,)), ((0,), (0,))), precision=_dn_prec(a), preferred_element_type=F32)


def _bdot_tn(a, b):
    return lax.dot_general(a, b, (((1,), (1,)), ((0,), (0,))), precision=_dn_prec(a), preferred_element_type=F32)


def _unit_lower_inverse(lmat):
    c = lmat.shape[-1]
    ri = lax.broadcasted_iota(jnp.int32, (c, c), 0)
    ci = lax.broadcasted_iota(jnp.int32, (c, c), 1)
    p = -lmat
    tinv = jnp.where(ri == ci, 1.0, 0.0) + p
    for _ in range(max(c.bit_length() - 2, 0)):
        p = _bdot(p, p)
        tinv = tinv + _bdot(tinv, p)
    return tinv


@jax.custom_vjp
def _solve_with(lmat, rhs, tinv):
    return _bdot(tinv, rhs)


def _solve_with_fwd(lmat, rhs, tinv):
    sol = _bdot(tinv, rhs)
    return sol, (sol, tinv)


def _solve_with_bwd(res, dsol):
    sol, tinv = res
    drhs = _bdot_tn(tinv, dsol)
    return -_bdot_nt(drhs, sol), drhs, jnp.zeros_like(tinv)


_solve_with.defvjp(_solve_with_fwd, _solve_with_bwd)


def _dn_local(q, k, v, grow, brow, tinv):
    g, c, _ = q.shape
    ri = lax.broadcasted_iota(jnp.int32, (c, c), 0)
    ci = lax.broadcasted_iota(jnp.int32, (c, c), 1)
    lower = ri >= ci
    as_col = lambda r: jnp.sum(jnp.where(ri == ci, jnp.broadcast_to(r, (g, c, c)), 0.0), axis=2, keepdims=True)
    gcol, bcol = as_col(grow), as_col(brow)
    gc_col = jnp.sum(jnp.where(lower, jnp.broadcast_to(grow, (g, c, c)), 0.0), axis=2, keepdims=True)
    gc_row = jnp.sum(jnp.where(ri <= ci, jnp.broadcast_to(gcol, (g, c, c)), 0.0), axis=1, keepdims=True)
    qs = q * (HEAD_DIM ** -0.5)
    kb = k * bcol
    vb = v * bcol
    decay = jnp.where(lower, jnp.exp(jnp.where(lower, gc_col - gc_row, 0.0)), 0.0)
    lmat = jnp.where(ri > ci, _bdot_nt(kb.astype(BF16), k.astype(BF16)) * decay, 0.0)
    eg = jnp.exp(gc_col)
    rhs = jnp.concatenate([vb, kb * eg], axis=2)
    if tinv is None:
        tinv = _unit_lower_inverse(lmat)
    sol = _solve_with(lmat, rhs, tinv)
    a_qk = jnp.where(lower, _bdot_nt(qs.astype(BF16), k.astype(BF16)) * decay, 0.0)
    g_last = jnp.sum(grow, axis=2, keepdims=True)
    kdec = k * jnp.exp(g_last - gc_col)
    egl = jnp.broadcast_to(jnp.exp(g_last), (g, 1, HEAD_DIM))
    return sol[:, :, :HEAD_DIM], sol[:, :, HEAD_DIM:], a_qk, qs * eg, kdec, egl, tinv


def _dn_seq(u, w, a_qk, qe, kdec, egl, s_in):
    b16 = lambda x: x.astype(BF16)
    v_new = u - _bdot(b16(w), b16(s_in))
    o = _bdot(b16(qe), b16(s_in)) + _bdot(b16(a_qk), b16(v_new))
    return o, s_in * egl + _bdot_tn(b16(kdec), b16(v_new))


def _dn_local_specs(t):
    grp = min(DN_GROUP, t // DN_CHUNK)
    rows = grp * DN_CHUNK
    blk = pl.BlockSpec((1, rows, HEAD_DIM), lambda h, i: (h, i, 0))
    row = pl.BlockSpec((1, grp, 1, DN_CHUNK), lambda h, i: (h, i, 0, 0))
    sq = pl.BlockSpec((1, grp, DN_CHUNK, DN_CHUNK), lambda h, i: (h, i, 0, 0))
    lane = pl.BlockSpec((1, grp, 1, HEAD_DIM), lambda h, i: (h, i, 0, 0))
    return grp, blk, row, sq, lane


def _dn_shapes(t):
    nchunk = t // DN_CHUNK
    big = jax.ShapeDtypeStruct((N_HEADS, t, HEAD_DIM), F32)
    row = jax.ShapeDtypeStruct((N_HEADS, nchunk, 1, DN_CHUNK), F32)
    sq = jax.ShapeDtypeStruct((N_HEADS, nchunk, DN_CHUNK, DN_CHUNK), F32)
    lane = jax.ShapeDtypeStruct((N_HEADS, nchunk, 1, HEAD_DIM), F32)
    return big, row, sq, lane


def _dn_local_fwd(q, k, v, grow, brow, wire=()):
    t = q.shape[1]
    grp, blk, row, sq, lane = _dn_local_specs(t)
    big, _, sqs, lanes = _dn_shapes(t)
    n = len(wire)
    groups = t // (grp * DN_CHUNK)
    steps = N_HEADS * groups

    def body(q_ref, k_ref, v_ref, gr_ref, br_ref, *rest):
        u_ref, w_ref, a_ref, qe_ref, kd_ref, egl_ref, t_ref = rest[n:n + 7]
        if n:
            begin, middle, end = _gather_protocol(rest[:n], rest[n + 7:2 * n + 7], *rest[2 * n + 7:])
            step = pl.program_id(0) * groups + pl.program_id(1)
            pl.when(step == 0)(begin)
            pl.when(step == (5 * steps) // 8)(middle)
        split = lambda r: r[0].reshape(grp, DN_CHUNK, HEAD_DIM)
        u, w, a_qk, qe, kdec, egl, tinv = _dn_local(split(q_ref), split(k_ref), split(v_ref), gr_ref[0],
                                                     br_ref[0], None)
        for ref, val in ((u_ref, u), (w_ref, w), (qe_ref, qe), (kd_ref, kdec)):
            ref[0] = val.reshape(grp * DN_CHUNK, HEAD_DIM)
        a_ref[0] = a_qk
        egl_ref[0] = egl
        t_ref[0] = tinv
        if n:
            pl.when(step == steps - 1)(end)

    assert n == 0 or steps >= 3
    return pl.pallas_call(
        body, name="dn_local_fwd", grid=(N_HEADS, groups),
        in_specs=[blk, blk, blk, row, row] + _hbm_specs(n),
        out_specs=[blk, blk, sq, blk, blk, lane, sq] + _hbm_specs(n),
        out_shape=[big, big, sqs, big, big, lanes, sqs] + _gather_out_shapes(wire),
        scratch_shapes=_gather_sems(n) if n else [],
        compiler_params=_params(("arbitrary", "arbitrary")),
    )(q, k, v, grow, brow, *wire)


def _dn_local_bwd(q, k, v, grow, brow, tinv, du, dw, da, dqe, dkd, degl, partials=()):
    t = q.shape[1]
    grp, blk, row, sq, lane = _dn_local_specs(t)
    big, rows_, _, _ = _dn_shapes(t)
    n = len(partials)
    groups = t // (grp * DN_CHUNK)
    steps = N_HEADS * groups

    def body(q_ref, k_ref, v_ref, gr_ref, br_ref, t_ref, du_ref, dw_ref, da_ref, dqe_ref, dkd_ref,
             degl_ref, *rest):
        dq_ref, dk_ref, dv_ref, dgr_ref, dbr_ref = rest[n:n + 5]
        if n:
            begin, end = _chip_exchange_protocol(rest[:n], rest[n + 5:2 * n + 5], *rest[2 * n + 5:])
            step = pl.program_id(0) * groups + pl.program_id(1)
            pl.when(step == 0)(begin)
        split = lambda r: r[0].reshape(grp, DN_CHUNK, HEAD_DIM)
        tinv_v = t_ref[0]
        fn = lambda q_, k_, v_, gr_, br_: _dn_local(q_, k_, v_, gr_, br_, tinv_v)[:6]
        _, vjp = jax.vjp(fn, split(q_ref), split(k_ref), split(v_ref), gr_ref[0], br_ref[0])
        dq, dk, dv, dgr, dbr = vjp((split(du_ref), split(dw_ref), da_ref[0], split(dqe_ref), split(dkd_ref),
                                    degl_ref[0]))
        for ref, val in ((dq_ref, dq), (dk_ref, dk), (dv_ref, dv)):
            ref[0] = val.reshape(grp * DN_CHUNK, HEAD_DIM)
        dgr_ref[0] = dgr
        dbr_ref[0] = dbr
        if n:
            pl.when(step == steps - 1)(end)

    assert n == 0 or steps >= 2
    return pl.pallas_call(
        body, name="dn_local_bwd", grid=(N_HEADS, groups),
        in_specs=[blk, blk, blk, row, row, sq, blk, blk, sq, blk, blk, lane] + _hbm_specs(n),
        out_specs=[blk, blk, blk, row, row] + _hbm_specs(n),
        out_shape=[big, big, big, rows_, rows_] + _chip_exchange_shapes(partials),
        scratch_shapes=_chip_exchange_sems(n) if n else [],
        compiler_params=_params(("arbitrary", "arbitrary")),
    )(q, k, v, grow, brow, tinv, du, dw, da, dqe, dkd, degl, *partials)


def _dn_seq_specs(nchunk, rev):
    def idx(n):
        return nchunk - 1 - n if rev else n

    blk = pl.BlockSpec((N_HEADS, DN_CHUNK, HEAD_DIM), lambda n: (0, idx(n), 0))
    sq = pl.BlockSpec((N_HEADS, 1, DN_CHUNK, DN_CHUNK), lambda n: (0, idx(n), 0, 0))
    lane = pl.BlockSpec((N_HEADS, 1, 1, HEAD_DIM), lambda n: (0, idx(n), 0, 0))
    st = pl.BlockSpec((N_HEADS, 1, HEAD_DIM, HEAD_DIM), lambda n: (0, idx(n), 0, 0))
    return blk, sq, lane, st


def _dn_seq_fwd(u, w, a_qk, qe, kdec, egl):
    t = u.shape[1]
    nchunk = t // DN_CHUNK
    blk, sq, lane, st = _dn_seq_specs(nchunk, False)

    def body(u_ref, w_ref, a_ref, qe_ref, kd_ref, egl_ref, o_ref, s_ref, state):
        @pl.when(pl.program_id(0) == 0)
        def _():
            state[...] = jnp.zeros_like(state)

        s_in = state[...]
        s_ref[:, 0] = s_in
        o, s_out = _dn_seq(u_ref[...], w_ref[...], a_ref[:, 0], qe_ref[...], kd_ref[...], egl_ref[:, 0], s_in)
        o_ref[...] = o
        state[...] = s_out

    return pl.pallas_call(
        body, name="dn_seq_fwd", grid=(nchunk,),
        in_specs=[blk, blk, sq, blk, blk, lane],
        out_specs=[blk, st],
        out_shape=[jax.ShapeDtypeStruct((N_HEADS, t, HEAD_DIM), F32),
                   jax.ShapeDtypeStruct((N_HEADS, nchunk, HEAD_DIM, HEAD_DIM), F32)],
        scratch_shapes=[pltpu.VMEM((N_HEADS, HEAD_DIM, HEAD_DIM), F32)],
        compiler_params=_params(("arbitrary",)),
    )(u, w, a_qk, qe, kdec, egl)


def _dn_seq_bwd(u, w, a_qk, qe, kdec, egl, states, do):
    t = u.shape[1]
    nchunk = t // DN_CHUNK
    blk, sq, lane, st = _dn_seq_specs(nchunk, True)
    big, _, sqs, lanes = _dn_shapes(t)

    def body(u_ref, w_ref, a_ref, qe_ref, kd_ref, egl_ref, s_ref, do_ref,
             du_ref, dw_ref, da_ref, dqe_ref, dkd_ref, degl_ref, dstate):
        @pl.when(pl.program_id(0) == 0)
        def _():
            dstate[...] = jnp.zeros_like(dstate)

        _, vjp = jax.vjp(_dn_seq, u_ref[...], w_ref[...], a_ref[:, 0], qe_ref[...], kd_ref[...], egl_ref[:, 0],
                         s_ref[:, 0])
        du, dw, da, dqe, dkd, degl, ds = vjp((do_ref[...], dstate[...]))
        du_ref[...] = du
        dw_ref[...] = dw
        da_ref[:, 0] = da
        dqe_ref[...] = dqe
        dkd_ref[...] = dkd
        degl_ref[:, 0] = degl
        dstate[...] = ds

    return pl.pallas_call(
        body, name="dn_seq_bwd", grid=(nchunk,),
        in_specs=[blk, blk, sq, blk, blk, lane, st, blk],
        out_specs=[blk, blk, sq, blk, blk, lane],
        out_shape=[big, big, sqs, big, big, lanes],
        scratch_shapes=[pltpu.VMEM((N_HEADS, HEAD_DIM, HEAD_DIM), F32)],
        compiler_params=_params(("arbitrary",)),
    )(u, w, a_qk, qe, kdec, egl, states, do)


def _dn_post_head(o, gate, w):
    return _rms(o, w) * _silu(gate)


def _dn_post_fwd(o, gate, w):
    t = gate.shape[0]
    tb = _rows(t)

    def body(o_ref, g_ref, w_ref, y_ref):
        for h in range(N_HEADS):
            sl = pl.ds(h * HEAD_DIM, HEAD_DIM)
            y_ref[:, sl] = _dn_post_head(o_ref[h], g_ref[:, sl], w_ref[...]).astype(BF16)

    row = pl.BlockSpec((tb, D_MODEL), lambda i: (i, 0))
    hm = pl.BlockSpec((N_HEADS, tb, HEAD_DIM), lambda i: (0, i, 0))
    return pl.pallas_call(
        body, name="dn_post_fwd", grid=(t // tb,),
        in_specs=[hm, row, pl.BlockSpec((1, HEAD_DIM), lambda i: (0, 0))],
        out_specs=row, out_shape=jax.ShapeDtypeStruct((t, D_MODEL), BF16),
        compiler_params=_params(("parallel",)),
    )(o, gate, w)


def _dn_post_bwd(o, gate, w, dy):
    t = gate.shape[0]
    tb = _rows(t)

    def body(o_ref, g_ref, w_ref, dy_ref, do_ref, dg_ref, dw_ref):
        i = pl.program_id(0)
        @pl.when(i == 0)
        def _():
            dw_ref[...] = jnp.zeros_like(dw_ref)

        for h in range(N_HEADS):
            sl = pl.ds(h * HEAD_DIM, HEAD_DIM)
            _, vjp = jax.vjp(_dn_post_head, o_ref[h], g_ref[:, sl], w_ref[...])
            do_ref[h], dg, dw = vjp(dy_ref[:, sl])
            dg_ref[:, sl] = dg.astype(BF16)
            dw_ref[...] += dw

    row = pl.BlockSpec((tb, D_MODEL), lambda i: (i, 0))
    hm = pl.BlockSpec((N_HEADS, tb, HEAD_DIM), lambda i: (0, i, 0))
    vec = pl.BlockSpec((1, HEAD_DIM), lambda i: (0, 0))
    return pl.pallas_call(
        body, name="dn_post_bwd", grid=(t // tb,),
        in_specs=[hm, row, vec, row],
        out_specs=[hm, row, vec],
        out_shape=[jax.ShapeDtypeStruct((N_HEADS, t, HEAD_DIM), F32), jax.ShapeDtypeStruct((t, D_MODEL), BF16),
                   jax.ShapeDtypeStruct((1, HEAD_DIM), F32)],
        compiler_params=_params(("arbitrary",)),
    )(o, gate, w, dy)


def _split_bf16(x):
    hi = x.astype(BF16)
    lo = (x - hi.astype(F32)).astype(BF16)
    return hi, lo


SB_Q_BLOCK = 512
SB_K_BLOCK = 256
SB_NEGLIGIBLE = -60.0


def _sb_logits(q, kb, mask, scale):
    z = _dot_nt(q, kb) * scale
    ls = jnp.minimum(z, 0.0) - jnp.log(1.0 + jnp.exp(-jnp.abs(z)))
    lk = ls - z
    if mask is not None:
        lk = jnp.where(mask, lk, 0.0)
    return ls, lk


def _sb_blocks(t):
    bq = min(SB_Q_BLOCK, t)
    bk = min(SB_K_BLOCK, bq)
    return bq, bk, bq // bk


def _sb_fwd(qkv):
    t = qkv.shape[0]
    bq, bk, nd = _sb_blocks(t)
    scale = HEAD_DIM ** -0.5

    def body(q_ref, k_ref, v_ref, o_ref, tot_ref, used_ref):
        i = pl.program_id(1)
        q = q_ref[...]
        rj = lax.broadcasted_iota(jnp.int32, (bk, bk), 0)
        cj = lax.broadcasted_iota(jnp.int32, (bk, bk), 1)
        after = (rj > cj).astype(BF16)
        trow = lax.broadcasted_iota(jnp.int32, (bq, bk), 0)
        scol = lax.broadcasted_iota(jnp.int32, (bq, bk), 1)

        def tile(j, run, acc, mask):
            off = pl.multiple_of(j * bk, bk)
            kb = k_ref[pl.ds(off, bk), :]
            vb = v_ref[pl.ds(off, bk), :]
            ls, lk = _sb_logits(q, kb, mask, scale)
            hi, lo = _split_bf16(lk)
            between = _dot(hi, after) + _dot(lo, after) + run
            a = jnp.exp(ls + between)
            if mask is not None:
                a = jnp.where(mask, a, 0.0)
            acc = acc + _dot(a.astype(BF16), vb)
            return run + jnp.sum(lk, axis=1, keepdims=True), acc

        run, acc = jnp.zeros((bq, 1), F32), jnp.zeros((bq, HEAD_DIM), F32)
        for d in reversed(range(nd)):
            run, acc = tile(i * nd + d, run, acc, scol + d * bk < trow)
        def more(c):
            return jnp.logical_and(c[0] < i * nd, jnp.max(c[1]) > SB_NEGLIGIBLE)

        def far(c):
            run_, acc_ = tile(i * nd - 1 - c[0], c[1], c[2], None)
            return c[0] + 1, run_, acc_

        used, run, acc = lax.while_loop(more, far, (jnp.int32(0), run, acc))
        o_ref[...] = acc.astype(BF16)
        tot_ref[...] = jnp.broadcast_to(run, (bq, HEAD_DIM))
        used_ref[...] = jnp.full(used_ref.shape, used, F32)

    qs = pl.BlockSpec((bq, HEAD_DIM), lambda h, i: (i, h))
    ks = pl.BlockSpec((t, HEAD_DIM), lambda h, i: (0, N_HEADS + h))
    vs = pl.BlockSpec((t, HEAD_DIM), lambda h, i: (0, 2 * N_HEADS + h))
    return pl.pallas_call(
        body, name="sb_fwd", grid=(N_HEADS, t // bq),
        in_specs=[qs, ks, vs], out_specs=[qs, qs, pl.BlockSpec((1, 1, 1, LANES), lambda h, i: (h, i, 0, 0))],
        out_shape=[jax.ShapeDtypeStruct((t, D_MODEL), BF16), jax.ShapeDtypeStruct((t, D_MODEL), F32),
                   jax.ShapeDtypeStruct((N_HEADS, t // bq, 1, LANES), F32)],
        compiler_params=_params(("parallel", "arbitrary")),
    )(qkv, qkv, qkv)


def _sb_bwd(qkv, tot, used, do):
    t = qkv.shape[0]
    bq, bk, nd = _sb_blocks(t)
    scale = HEAD_DIM ** -0.5

    def body(q_ref, k_ref, v_ref, tot_ref, used_ref, do_ref, dq_ref, dk_ref, dv_ref):
        i = pl.program_id(1)

        @pl.when(i == 0)
        def _():
            dk_ref[...] = jnp.zeros_like(dk_ref)
            dv_ref[...] = jnp.zeros_like(dv_ref)

        q = q_ref[...]
        do = do_ref[...]
        total = tot_ref[:, 0:1]
        rj = lax.broadcasted_iota(jnp.int32, (bk, bk), 0)
        cj = lax.broadcasted_iota(jnp.int32, (bk, bk), 1)
        upto = (rj <= cj).astype(BF16)
        before = (rj < cj).astype(BF16)
        trow = lax.broadcasted_iota(jnp.int32, (bq, bk), 0)
        scol = lax.broadcasted_iota(jnp.int32, (bq, bk), 1)

        def tile(j, run_k, run_e, dq, mask):
            off = pl.multiple_of(j * bk, bk)
            kb = k_ref[pl.ds(off, bk), :]
            vb = v_ref[pl.ds(off, bk), :]
            ls, lk = _sb_logits(q, kb, mask, scale)
            hi, lo = _split_bf16(lk)
            between = total - (_dot(hi, upto) + _dot(lo, upto) + run_k)
            a = jnp.exp(ls + between)
            if mask is not None:
                a = jnp.where(mask, a, 0.0)
            e = a * _dot_nt(do, vb)
            ehi, elo = _split_bf16(e)
            pre = _dot(ehi, before) + _dot(elo, before) + run_e
            sig = jnp.exp(ls)
            dz = e * (1.0 - sig) - pre * sig
            if mask is not None:
                dz = jnp.where(mask, dz, 0.0)
            dz = (dz * scale).astype(BF16)
            dq = dq + _dot(dz, kb)
            dk_ref[pl.ds(off, bk), :] += _dot_tn(dz, q)
            dv_ref[pl.ds(off, bk), :] += _dot_tn(a.astype(BF16), do)
            return (run_k + jnp.sum(lk, axis=1, keepdims=True),
                    run_e + jnp.sum(e, axis=1, keepdims=True), dq)

        zero = jnp.zeros((bq, 1), F32)
        visited = jnp.clip(jnp.max(used_ref[...]).astype(jnp.int32), 0, i * nd)
        carry = lax.fori_loop(i * nd - visited, i * nd, lambda j, c: tile(j, c[0], c[1], c[2], None),
                              (zero, zero, jnp.zeros((bq, HEAD_DIM), F32)))
        for d in range(nd):
            carry = tile(i * nd + d, *carry, scol + d * bk < trow)
        dq_ref[...] = carry[2]

    qs = pl.BlockSpec((bq, HEAD_DIM), lambda h, i: (i, h))
    ks = pl.BlockSpec((t, HEAD_DIM), lambda h, i: (0, N_HEADS + h))
    vs = pl.BlockSpec((t, HEAD_DIM), lambda h, i: (0, 2 * N_HEADS + h))
    full = pl.BlockSpec((t, HEAD_DIM), lambda h, i: (0, h))
    big = jax.ShapeDtypeStruct((t, D_MODEL), F32)
    return pl.pallas_call(
        body, name="sb_bwd", grid=(N_HEADS, t // bq),
        in_specs=[qs, ks, vs, qs, pl.BlockSpec((1, 1, 1, LANES), lambda h, i: (h, i, 0, 0)), qs],
        out_specs=[qs, full, full],
        out_shape=[big, big, big],
        compiler_params=_params(("parallel", "arbitrary")),
    )(qkv, qkv, qkv, tot, used, do)


def _merge_fwd(o_dn, o_sb, gl, x, wp_dn, wp_sb, w_out, w2):
    t = x.shape[0]
    tb = _rows(t)

    def body(odn_ref, osb_ref, gl_ref, x_ref, wpd_ref, wps_ref, wo_ref, w2_ref,
             pdn_ref, psb_ref, mix_ref, x1_ref, n2_ref):
        pdn = _dot(odn_ref[...], wpd_ref[...])
        psb = _dot(osb_ref[...], wps_ref[...])
        gates = jax.nn.sigmoid(gl_ref[...])
        mixed = (gates[:, :D_MODEL] * pdn + gates[:, D_MODEL:] * psb).astype(BF16)
        x1 = x_ref[...] + _dot(mixed, wo_ref[...])
        pdn_ref[...] = pdn
        psb_ref[...] = psb
        mix_ref[...] = mixed
        x1_ref[...] = x1
        n2_ref[...] = _rms(x1, w2_ref[...]).astype(BF16)

    row = pl.BlockSpec((tb, D_MODEL), lambda i: (i, 0))
    sq = pl.BlockSpec((D_MODEL, D_MODEL), lambda i: (0, 0))
    f = jax.ShapeDtypeStruct((t, D_MODEL), F32)
    b = jax.ShapeDtypeStruct((t, D_MODEL), BF16)
    return pl.pallas_call(
        body, name="merge_fwd", grid=(t // tb,),
        in_specs=[row, row, pl.BlockSpec((tb, 2 * D_MODEL), lambda i: (i, 0)), row, sq, sq, sq,
                  pl.BlockSpec((1, D_MODEL), lambda i: (0, 0))],
        out_specs=[row] * 5, out_shape=[f, f, b, f, b],
        compiler_params=_params(("parallel",)),
    )(o_dn, o_sb, gl, x, wp_dn, wp_sb, w_out, w2)


def _merge_bwd(dx2, dn2, x1, w2, gl, pdn, psb, wp_dn, wp_sb, w_out):
    t = x1.shape[0]
    tb = _rows(t)

    def body(dx2_ref, dn2_ref, x1_ref, w2_ref, gl_ref, pdn_ref, psb_ref, wpd_ref, wps_ref, wo_ref,
             dx1_ref, dw2_ref, dgl_ref, dpdn_ref, dpsb_ref, dodn_ref, dosb_ref):
        i = pl.program_id(0)
        _, vjp = jax.vjp(_rms, x1_ref[...], w2_ref[...])
        dxn, dw2 = vjp(dn2_ref[...])
        dx1 = dx2_ref[...] + dxn
        dx1_ref[...] = dx1

        @pl.when(i == 0)
        def _():
            dw2_ref[...] = jnp.zeros_like(dw2_ref)

        dw2_ref[...] += dw2
        dmix = _dot_nt(dx1.astype(BF16), wo_ref[...])
        gates = jax.nn.sigmoid(gl_ref[...])
        g_dn, g_sb = gates[:, :D_MODEL], gates[:, D_MODEL:]
        dpdn = (dmix * g_dn).astype(BF16)
        dpsb = (dmix * g_sb).astype(BF16)
        dgl_ref[:, :D_MODEL] = (dmix * pdn_ref[...] * g_dn * (1.0 - g_dn)).astype(BF16)
        dgl_ref[:, D_MODEL:] = (dmix * psb_ref[...] * g_sb * (1.0 - g_sb)).astype(BF16)
        dpdn_ref[...] = dpdn
        dpsb_ref[...] = dpsb
        dodn_ref[...] = _dot_nt(dpdn, wpd_ref[...])
        dosb_ref[...] = _dot_nt(dpsb, wps_ref[...]).astype(BF16)

    row = pl.BlockSpec((tb, D_MODEL), lambda i: (i, 0))
    wide = pl.BlockSpec((tb, 2 * D_MODEL), lambda i: (i, 0))
    sq = pl.BlockSpec((D_MODEL, D_MODEL), lambda i: (0, 0))
    vec = pl.BlockSpec((1, D_MODEL), lambda i: (0, 0))
    f = jax.ShapeDtypeStruct((t, D_MODEL), F32)
    b = jax.ShapeDtypeStruct((t, D_MODEL), BF16)
    return pl.pallas_call(
        body, name="merge_bwd", grid=(t // tb,),
        in_specs=[row, row, row, vec, wide, row, row, sq, sq, sq],
        out_specs=[row, vec, wide, row, row, row, row],
        out_shape=[f, jax.ShapeDtypeStruct((1, D_MODEL), F32), jax.ShapeDtypeStruct((t, 2 * D_MODEL), BF16),
                   b, b, f, b],
        compiler_params=_params(("arbitrary",)),
    )(dx2, dn2, x1, w2, gl, pdn, psb, wp_dn, wp_sb, w_out)


def _conv_taps(buf, w_ref, first, rows, cols=slice(None)):
    y = w_ref[0:1, cols] * buf[pl.ds(first, rows), cols]
    for s in range(1, w_ref.shape[0]):
        y = y + w_ref[s:s + 1, cols] * buf[pl.ds(first + s, rows), cols]
    return y


def _ffn_mid_fwd(pre_g, pre_u, wg, wu):
    t, c = pre_g.shape
    kk = wg.shape[0]
    tb, cb = _rows(t), _pick(c, ELEMENTWISE_COLS)
    per = tb // HALO

    def body(g_ref, gh_ref, u_ref, uh_ref, wg_ref, wu_ref, a_ref, gbuf, ubuf):
        i = pl.program_id(0)
        for buf, ref, halo in ((gbuf, g_ref, gh_ref), (ubuf, u_ref, uh_ref)):
            buf[pl.ds(HALO, tb), :] = ref[...]
            buf[pl.ds(0, HALO), :] = jnp.where(i == 0, 0.0, halo[...])
        for j in range(cb // LANES):
            sl = pl.ds(j * LANES, LANES)
            ug = _conv_taps(gbuf, wg_ref, HALO - (kk - 1), tb, sl)
            uu = _conv_taps(ubuf, wu_ref, HALO - (kk - 1), tb, sl)
            a_ref[:, sl] = (_silu(ug) * uu).astype(BF16)

    blk = pl.BlockSpec((tb, cb), lambda i, j: (i, j))
    halo = pl.BlockSpec((HALO, cb), lambda i, j: (jnp.maximum(i * per - 1, 0), j))
    wspec = pl.BlockSpec((kk, cb), lambda i, j: (0, j))
    return pl.pallas_call(
        body, name="ffn_mid_fwd", grid=(t // tb, c // cb),
        in_specs=[blk, halo, blk, halo, wspec, wspec], out_specs=blk,
        out_shape=jax.ShapeDtypeStruct((t, c), BF16),
        scratch_shapes=[pltpu.VMEM((tb + HALO, cb), F32)] * 2,
        compiler_params=_params(("parallel", "parallel")),
    )(pre_g, pre_g, pre_u, pre_u, wg, wu)


def _ffn_mid_bwd(pre_g, pre_u, wg, wu, da):
    t, c = pre_g.shape
    kk = wg.shape[0]
    tb, cb = _rows(t), _pick(c, ELEMENTWISE_COLS)
    per = tb // HALO
    nblk = t // tb
    ext = tb + HALO

    def body(g_ref, gb_ref, ga_ref, u_ref, ub_ref, ua_ref, da_ref, daa_ref, wg_ref, wu_ref,
             dg_ref, du_ref, dwg_ref, dwu_ref, gbuf, ubuf, dabuf, dgbuf, dubuf):
        i = pl.program_id(1)
        last = i == nblk - 1
        for buf, ref, before, after in ((gbuf, g_ref, gb_ref, ga_ref), (ubuf, u_ref, ub_ref, ua_ref)):
            buf[pl.ds(0, HALO), :] = jnp.where(i == 0, 0.0, before[...])
            buf[pl.ds(HALO, tb), :] = ref[...]
            buf[pl.ds(HALO + tb, HALO), :] = jnp.where(last, 0.0, after[...])
        dabuf[pl.ds(0, tb), :] = da_ref[...]
        dabuf[pl.ds(tb, HALO), :] = jnp.where(last, 0.0, daa_ref[...])

        @pl.when(i == 0)
        def _():
            dwg_ref[...] = jnp.zeros_like(dwg_ref)
            dwu_ref[...] = jnp.zeros_like(dwu_ref)

        for j in range(cb // LANES):
            sl = pl.ds(j * LANES, LANES)
            ug = _conv_taps(gbuf, wg_ref, HALO - (kk - 1), ext, sl)
            uu = _conv_taps(ubuf, wu_ref, HALO - (kk - 1), ext, sl)
            _, vjp = jax.vjp(lambda g, u: _silu(g) * u, ug, uu)
            dgbuf[:, sl], dubuf[:, sl] = vjp(dabuf[:, sl])
            for dbuf, xbuf, w_ref, dx_ref, dw_ref in ((dgbuf, gbuf, wg_ref, dg_ref, dwg_ref),
                                                      (dubuf, ubuf, wu_ref, du_ref, dwu_ref)):
                x = xbuf[pl.ds(HALO, tb), sl]
                dx = None
                for s in range(kk):
                    shifted = dbuf[pl.ds(kk - 1 - s, tb), sl]
                    term = w_ref[s:s + 1, sl] * shifted
                    dx = term if dx is None else dx + term
                    dw_ref[s:s + 1, sl] += jnp.sum(shifted * x, axis=0, keepdims=True)
                dx_ref[:, sl] = dx.astype(BF16)

    blk = pl.BlockSpec((tb, cb), lambda j, i: (i, j))
    before = pl.BlockSpec((HALO, cb), lambda j, i: (jnp.maximum(i * per - 1, 0), j))
    after = pl.BlockSpec((HALO, cb), lambda j, i: (jnp.minimum((i + 1) * per, t // HALO - 1), j))
    wspec = pl.BlockSpec((kk, cb), lambda j, i: (0, j))
    dwspec = pl.BlockSpec((HALO, cb), lambda j, i: (0, j))
    half = jax.ShapeDtypeStruct((t, c), BF16)
    dwshape = jax.ShapeDtypeStruct((HALO, c), F32)
    return pl.pallas_call(
        body, name="ffn_mid_bwd", grid=(c // cb, nblk),
        in_specs=[blk, before, after, blk, before, after, blk, after, wspec, wspec],
        out_specs=[blk, blk, dwspec, dwspec],
        out_shape=[half, half, dwshape, dwshape],
        scratch_shapes=[pltpu.VMEM((ext + HALO, cb), F32)] * 2 + [pltpu.VMEM((ext, cb), F32)] * 3,
        compiler_params=_params(("parallel", "arbitrary")),
    )(pre_g, pre_g, pre_g, pre_u, pre_u, pre_u, da, da, wg, wu)


def _down_loss(a, w_down, x1, wf, target):
    t = x1.shape[0]
    tb = _rows(t)

    def body(a_ref, wd_ref, x1_ref, wf_ref, tgt_ref, dx2_ref, dwf_ref, loss_ref):
        i = pl.program_id(0)
        x2 = x1_ref[...] + _dot(a_ref[...], wd_ref[...])
        y, vjp = jax.vjp(_rms, x2, wf_ref[...])
        err = y - tgt_ref[...]
        dx2, dwf = vjp(err * (1.0 / D_MODEL))
        dx2_ref[...] = dx2
        part = jnp.sum(jnp.sum(err * err, axis=1, keepdims=True), axis=0, keepdims=True) * (0.5 / D_MODEL)

        @pl.when(i == 0)
        def _():
            dwf_ref[...] = jnp.zeros_like(dwf_ref)
            loss_ref[...] = jnp.zeros_like(loss_ref)

        dwf_ref[...] += dwf
        loss_ref[...] += jnp.broadcast_to(part, loss_ref.shape)

    row = pl.BlockSpec((tb, D_MODEL), lambda i: (i, 0))
    vec = pl.BlockSpec((1, D_MODEL), lambda i: (0, 0))
    return pl.pallas_call(
        body, name="down_loss", grid=(t // tb,),
        in_specs=[pl.BlockSpec((tb, D_FF), lambda i: (i, 0)), pl.BlockSpec((D_FF, D_MODEL), lambda i: (0, 0)),
                  row, vec, row],
        out_specs=[row, vec, pl.BlockSpec((1, LANES), lambda i: (0, 0))],
        out_shape=[jax.ShapeDtypeStruct((t, D_MODEL), F32), jax.ShapeDtypeStruct((1, D_MODEL), F32),
                   jax.ShapeDtypeStruct((1, LANES), F32)],
        compiler_params=_params(("arbitrary",)),
    )(a, w_down, x1, wf, target)


def _local_step(x, target, wts, late_wire=(), late_weights=None, early_partials=None):
    t = x.shape[0]
    nchunk = t // DN_CHUNK

    n1, hab = _norm1_fwd(x, wts["norm1"], wts["w_ab"])
    dnqkv = _mm(n1, wts["w_dnqkv"], name="h_dnqkv")
    dngate = _mm(n1, wts["w_dngate"], name="h_dngate")
    sbqkv = _mm(n1, wts["w_sbqkv"], out_dtype=BF16, name="h_sbqkv")
    gl = _mm(n1, wts["w_gl"], name="h_gl")

    cdn = _conv_fwd(dnqkv, wts["dn_conv"], "dn_conv_fwd")
    qn, kn, vv, gb = _dn_prep_fwd(cdn, hab, wts["alog"], wts["dtb"])
    per_head = gb[:, :2 * N_HEADS].T.reshape(2 * N_HEADS, nchunk, DN_CHUNK)
    grow, brow = per_head[:N_HEADS, :, None, :], per_head[N_HEADS:, :, None, :]
    u_dn, w_dn, a_qk, qe, kdec, egl, tinv, *late = _dn_local_fwd(qn, kn, vv, grow, brow, late_wire)
    if late_wire:
        wts = {**wts, **late_weights(late)}
    o_raw, states = _dn_seq_fwd(u_dn, w_dn, a_qk, qe, kdec, egl)
    o_dn = _dn_post_fwd(o_raw, dngate, wts["dn_norm"])

    o_sb, tot, sb_used = _sb_fwd(sbqkv)

    pdn, psb, mixed, x1, n2 = _merge_fwd(o_dn, o_sb, gl, x, wts["wp_dn"], wts["wp_sb"], wts["w_out"],
                                         wts["norm2"])
    pre_g = _mm(n2, wts["w_up_g"], name="ffn_up_g")
    pre_u = _mm(n2, wts["w_up_u"], name="ffn_up_u")
    act = _ffn_mid_fwd(pre_g, pre_u, wts["ffn_conv_g"], wts["ffn_conv_u"])
    dx2, d_normf, loss_part = _down_loss(act, wts["w_down"], x1, wts["normf"], target)

    grads = {"normf": d_normf}
    da = _mm(dx2, wts["w_down"], tb=True, name="d_act")
    grads["w_down"] = _mm(act, dx2, ta=True, out_dtype=BF16, name="dw_down")
    dpre_g, dpre_u, dcw_g, dcw_u = _ffn_mid_bwd(pre_g, pre_u, wts["ffn_conv_g"], wts["ffn_conv_u"], da)
    grads["ffn_conv"] = jnp.concatenate([dcw_g[:FFN_CONV], dcw_u[:FFN_CONV]], axis=1)
    dn2 = _mm(dpre_g, wts["w_up_g"], tb=True, name="dn2_g")
    dn2 = _mm(dpre_u, wts["w_up_u"], tb=True, add=dn2, name="dn2_u")
    grads["w_up_g"] = _mm(n2, dpre_g, ta=True, out_dtype=BF16, name="dw_up_g")
    grads["w_up_u"] = _mm(n2, dpre_u, ta=True, out_dtype=BF16, name="dw_up_u")

    dx1, grads["norm2"], dgl, dpdn, dpsb, do_dn, do_sb = _merge_bwd(
        dx2, dn2, x1, wts["norm2"], gl, pdn, psb, wts["wp_dn"], wts["wp_sb"], wts["w_out"])
    grads["w_out"] = _mm(mixed, dx1, ta=True, out_dtype=BF16, name="dw_out")
    grads["wp_dn"] = _mm(o_dn, dpdn, ta=True, out_dtype=BF16, name="dw_proj_dn")
    grads["wp_sb"] = _mm(o_sb, dpsb, ta=True, out_dtype=BF16, name="dw_proj_sb")

    partials = early_partials(grads) if early_partials else ()
    dsq, dsk, dsv = _sb_bwd(sbqkv, tot, sb_used, do_sb)
    dsbqkv = jnp.concatenate([dsq, dsk, dsv], axis=1).astype(BF16)

    do_raw, ddngate, grads["dn_norm"] = _dn_post_bwd(o_raw, dngate, wts["dn_norm"], do_dn)
    seq_grads = _dn_seq_bwd(u_dn, w_dn, a_qk, qe, kdec, egl, states, do_raw)
    dqn, dkn, dvv, dgrow, dbrow, *arrived = _dn_local_bwd(qn, kn, vv, grow, brow, tinv, *seq_grads,
                                                          partials=partials)
    dgb = jnp.concatenate([dgrow.reshape(N_HEADS, t), dbrow.reshape(N_HEADS, t)], axis=0).T
    dgb = jnp.pad(dgb, ((0, 0), (0, LANES - 2 * N_HEADS)))
    dcdn, dhab, grads["alog"], grads["dtb"] = _dn_prep_bwd(cdn, hab, wts["alog"], wts["dtb"], dqn, dkn, dvv, dgb)
    ddnqkv, dcw_dn = _conv_bwd(dcdn, dnqkv, wts["dn_conv"], "dn_conv_bwd", BF16)
    grads["dn_conv"] = dcw_dn[:DN_CONV]

    dn1 = _mm(ddnqkv, wts["w_dnqkv"], tb=True, name="dn1_dnqkv")
    dn1 = _mm(ddngate, wts["w_dngate"], tb=True, add=dn1, name="dn1_dngate")
    dn1 = _mm(dsbqkv, wts["w_sbqkv"], tb=True, add=dn1, name="dn1_sbqkv")
    dn1 = _mm(dgl, wts["w_gl"], tb=True, add=dn1, name="dn1_gl")
    grads["w_dnqkv"] = _mm(n1, ddnqkv, ta=True, out_dtype=BF16, name="dw_dnqkv")
    grads["w_dngate"] = _mm(n1, ddngate, ta=True, out_dtype=BF16, name="dw_dngate")
    grads["w_sbqkv"] = _mm(n1, dsbqkv, ta=True, out_dtype=BF16, name="dw_sbqkv")
    grads["w_gl"] = _mm(n1, dgl, ta=True, out_dtype=BF16, name="dw_gl")
    grads["w_ab"] = _mm(n1, dhab, ta=True, out_dtype=BF16, name="dw_ab")
    grad_x, grads["norm1"] = _norm1_bwd(x, wts["norm1"], dn1, dx1, dhab, wts["w_ab"])
    return loss_part, grad_x, grads, (list(partials), arrived)


def _place():
    return lax.axis_index("x"), lax.axis_index("y"), lax.axis_index("c")


def _hbm_specs(n):
    return [pl.BlockSpec(memory_space=pltpu.HBM)] * n


GATHER_SEMS = 8


def _gather_protocol(ins, outs, send_sems, recv_sems):
    n = len(ins)
    x, y, c = _place()
    me = 2 * x + y
    sibling = (x, y, 1 - c)
    xn, yn, dg = (1 - x, y), (x, 1 - y), (1 - x, 1 - y)
    idx = lambda chip: 2 * chip[0] + chip[1]

    def part(a, chip_index, core, quarter=None):
        half = ins[a].shape[0] // 2
        if quarter is None:
            return outs[a].at[chip_index, pl.ds(core * half, half), :]
        return outs[a].at[chip_index, pl.ds(core * half + quarter * (half // 2), half // 2), :]

    def copy(a, k, src, dst, to):
        return pltpu.make_async_remote_copy(src_ref=src, dst_ref=dst, send_sem=send_sems.at[GATHER_SEMS * a + k],
                                            recv_sem=recv_sems.at[GATHER_SEMS * a + k], device_id=to,
                                            device_id_type=MESH)

    def sent(a, k):
        half = ins[a].shape[0] // 2
        my_half = ins[a].at[pl.ds(c * half, half), :]
        if k < 2:
            return copy(a, k, my_half, part(a, me, c), (*(xn, yn)[k], c))
        if k < 4:
            src = part(a, idx((xn, yn)[k - 2]), c, k - 2)
            return copy(a, k, src, src, (*(yn, xn)[k - 2], c))
        src = (part(a, idx(xn), c), part(a, idx(yn), c), part(a, idx(dg), c, 0), part(a, idx(dg), c, 1))[k - 4]
        return copy(a, k, src, src, sibling)

    def landed(a, k):
        dst = (part(a, idx(xn), c), part(a, idx(yn), c), part(a, idx(dg), c, 0), part(a, idx(dg), c, 1),
               part(a, idx(xn), 1 - c), part(a, idx(yn), 1 - c), part(a, idx(dg), 1 - c, 0),
               part(a, idx(dg), 1 - c, 1))[k]
        return copy(a, k, dst, dst, sibling)

    def begin():
        for a in range(n):
            sent(a, 0).start()
            sent(a, 1).start()

    def middle():
        for a in range(n):
            for k in range(2):
                landed(a, k).wait_recv()
                sent(a, 2 + k).start()
                sent(a, 4 + k).start()

    def end():
        for a in range(n):
            for k in (2, 3):
                landed(a, k).wait_recv()
                sent(a, 4 + k).start()
        for a in range(n):
            for k in range(4, GATHER_SEMS):
                landed(a, k).wait_recv()
        for a in range(n):
            for k in range(GATHER_SEMS):
                sent(a, k).wait_send()

    return begin, middle, end


def _gather_out_shapes(shards):
    return [jax.ShapeDtypeStruct((N_CHIPS,) + s.shape, s.dtype) for s in shards]


def _gather_sems(n):
    return [pltpu.SemaphoreType.DMA((GATHER_SEMS * n,)), pltpu.SemaphoreType.DMA((GATHER_SEMS * n,))]


def _gather_shards(shards):
    n = len(shards)

    def body(*refs):
        begin, middle, end = _gather_protocol(refs[:n], refs[n:2 * n], *refs[2 * n:])
        begin()
        middle()
        end()

    return pl.pallas_call(
        body, name="gather_weights", in_specs=_hbm_specs(n), out_specs=_hbm_specs(n),
        out_shape=_gather_out_shapes(shards), scratch_shapes=_gather_sems(n),
    )(*shards)


def _pair_exchange_halves(gs, tag):
    n = len(gs)

    def body(*refs):
        ins, outs, (send_sems, recv_sems) = refs[:n], refs[n:2 * n], refs[2 * n:]
        x, y, c = _place()
        cps = []
        for a in range(n):
            half = ins[a].shape[1] // 2
            cp = pltpu.make_async_remote_copy(src_ref=ins[a].at[:, pl.ds((1 - c) * half, half), :], dst_ref=outs[a],
                                              send_sem=send_sems.at[a], recv_sem=recv_sems.at[a],
                                              device_id=(x, y, 1 - c), device_id_type=MESH)
            cp.start()
            cps.append(cp)
        for cp in cps:
            cp.wait()

    return pl.pallas_call(
        body, name="grad_pair_exchange_" + tag, in_specs=_hbm_specs(n), out_specs=_hbm_specs(n),
        out_shape=[jax.ShapeDtypeStruct((g.shape[0], g.shape[1] // 2, g.shape[2]), g.dtype) for g in gs],
        scratch_shapes=[pltpu.SemaphoreType.DMA((n,)), pltpu.SemaphoreType.DMA((n,))],
    )(*gs)


def _pick_rows(n, target=1024):
    best = 16
    for b in range(16, min(n, target) + 1, 16):
        if n % b == 0:
            best = b
    return best


def _pair_add(g, got, c_idx, tag):
    nsh, rows, cols = g.shape
    half = rows // 2
    rb = _pick_rows(half)

    def body(c_ref, g_ref, got_ref, o_ref):
        o_ref[...] = (g_ref[...].astype(F32) + got_ref[...].astype(F32)).astype(BF16)

    nb = half // rb
    grid_spec = pltpu.PrefetchScalarGridSpec(
        num_scalar_prefetch=1, grid=(nsh, nb),
        in_specs=[pl.BlockSpec((1, rb, cols), lambda s, i, c_ref: (s, c_ref[0] * nb + i, 0)),
                  pl.BlockSpec((1, rb, cols), lambda s, i, c_ref: (s, i, 0))],
        out_specs=pl.BlockSpec((1, rb, cols), lambda s, i, c_ref: (s, i, 0)))
    return pl.pallas_call(
        body, name="grad_pair_add_" + tag, grid_spec=grid_spec,
        out_shape=jax.ShapeDtypeStruct((nsh, half, cols), BF16),
        compiler_params=_params(("parallel", "parallel")),
    )(c_idx, g, got)


def _chip_exchange_protocol(ins, outs, send_sems, recv_sems):
    x, y, c = _place()
    chips = [(1 - x, y), (x, 1 - y), (1 - x, 1 - y)]

    def copies():
        return [pltpu.make_async_remote_copy(src_ref=ins[a].at[2 * px + py], dst_ref=outs[a].at[j],
                                             send_sem=send_sems.at[3 * a + j], recv_sem=recv_sems.at[3 * a + j],
                                             device_id=(px, py, c), device_id_type=MESH)
                for a in range(len(ins)) for j, (px, py) in enumerate(chips)]

    def begin():
        for cp in copies():
            cp.start()

    def end():
        for cp in copies():
            cp.wait_recv()
        for cp in copies():
            cp.wait_send()

    return begin, end


def _chip_exchange_shapes(ps):
    return [jax.ShapeDtypeStruct((N_CHIPS - 1,) + p.shape[1:], p.dtype) for p in ps]


def _chip_exchange_sems(n):
    return [pltpu.SemaphoreType.DMA((3 * n,)), pltpu.SemaphoreType.DMA((3 * n,))]


def _chip_exchange(ps):
    n = len(ps)

    def body(*refs):
        begin, end = _chip_exchange_protocol(refs[:n], refs[n:2 * n], *refs[2 * n:])
        begin()
        end()

    return pl.pallas_call(
        body, name="grad_chip_exchange", in_specs=_hbm_specs(n), out_specs=_hbm_specs(n),
        out_shape=_chip_exchange_shapes(ps), scratch_shapes=_chip_exchange_sems(n),
    )(*ps)


def _sum_partials(p, got, chip_idx, tag):
    nsh, half, cols = got.shape
    rb = _pick_rows(half)

    def body(me_ref, p_ref, got_ref, o_ref):
        acc = p_ref[0].astype(F32)
        for s in range(nsh):
            acc = acc + got_ref[s].astype(F32)
        o_ref[...] = acc

    grid_spec = pltpu.PrefetchScalarGridSpec(
        num_scalar_prefetch=1, grid=(half // rb,),
        in_specs=[pl.BlockSpec((1, rb, cols), lambda i, me_ref: (me_ref[0], i, 0)),
                  pl.BlockSpec((nsh, rb, cols), lambda i, me_ref: (0, i, 0))],
        out_specs=pl.BlockSpec((rb, cols), lambda i, me_ref: (i, 0)))
    return pl.pallas_call(
        body, name="grad_sum_chips_" + tag, grid_spec=grid_spec,
        out_shape=jax.ShapeDtypeStruct((half, cols), F32),
        compiler_params=_params(("parallel",)),
    )(chip_idx, p, got)


def _pair_share(rs):
    n = len(rs)

    def body(*refs):
        ins, outs, (send_sems, recv_sems) = refs[:n], refs[n:2 * n], refs[2 * n:]
        x, y, c = _place()
        cps = []
        for a in range(n):
            cp = pltpu.make_async_remote_copy(src_ref=ins[a], dst_ref=outs[a], send_sem=send_sems.at[a],
                                              recv_sem=recv_sems.at[a], device_id=(x, y, 1 - c),
                                              device_id_type=MESH)
            cp.start()
            cps.append(cp)
        for cp in cps:
            cp.wait()

    return pl.pallas_call(
        body, name="grad_pair_share", in_specs=_hbm_specs(n), out_specs=_hbm_specs(n),
        out_shape=[jax.ShapeDtypeStruct(r.shape, r.dtype) for r in rs],
        scratch_shapes=[pltpu.SemaphoreType.DMA((n,)), pltpu.SemaphoreType.DMA((n,))],
    )(*rs)


def _small_allreduce(v):
    rows, cols = v.shape
    ndev = 8

    def body(in_ref, out_ref, slots, send_sems, recv_sems):
        x, y, c = _place()
        me = 4 * x + 2 * y + c
        slots[me] = in_ref[...]
        sends = []
        for k in range(1, ndev):
            peer = (x ^ (k >> 2), y ^ ((k >> 1) & 1), c ^ (k & 1))
            cp = pltpu.make_async_remote_copy(src_ref=in_ref, dst_ref=slots.at[me], send_sem=send_sems.at[k - 1],
                                              recv_sem=recv_sems.at[k - 1], device_id=peer, device_id_type=MESH)
            cp.start()
            sends.append(cp)
        for k in range(1, ndev):
            there = slots.at[me ^ k]
            pltpu.make_async_remote_copy(src_ref=there, dst_ref=there, send_sem=send_sems.at[k - 1],
                                         recv_sem=recv_sems.at[k - 1], device_id=(x, y, c),
                                         device_id_type=MESH).wait_recv()
        for cp in sends:
            cp.wait_send()
        acc = slots[0]
        for s in range(1, ndev):
            acc = acc + slots[s]
        out_ref[...] = acc

    return pl.pallas_call(
        body, name="small_allreduce",
        in_specs=[pl.BlockSpec(memory_space=pltpu.VMEM)],
        out_specs=pl.BlockSpec(memory_space=pltpu.VMEM),
        out_shape=jax.ShapeDtypeStruct((rows, cols), F32),
        scratch_shapes=[pltpu.VMEM((ndev, rows, cols), F32), pltpu.SemaphoreType.DMA((ndev - 1,)),
                        pltpu.SemaphoreType.DMA((ndev - 1,))],
    )(v)


def _adamw(w, g, m, v, name):
    r, c = w.shape
    rb = r if r <= 128 else _pick_rows_8(r, 128)
    c1 = 1.0 - ADAM_B1 ** ADAM_STEP
    c2 = 1.0 - ADAM_B2 ** ADAM_STEP

    def body(w_ref, g_ref, m_ref, v_ref, d_ref, nm_ref, nv_ref):
        gg = g_ref[...]
        nm = ADAM_B1 * m_ref[...] + (1.0 - ADAM_B1) * gg
        nv = ADAM_B2 * v_ref[...] + (1.0 - ADAM_B2) * (gg * gg)
        d_ref[...] = -ADAM_LR * ((nm / c1) / (jnp.sqrt(nv / c2) + ADAM_EPS) + ADAM_WD * w_ref[...])
        nm_ref[...] = nm
        nv_ref[...] = nv

    blk = pl.BlockSpec((rb, c), lambda i: (i, 0))
    shp = jax.ShapeDtypeStruct((r, c), F32)
    return pl.pallas_call(
        body, name=name, grid=(r // rb,), in_specs=[blk] * 4, out_specs=[blk] * 3, out_shape=[shp] * 3,
        compiler_params=_params(("parallel",)),
    )(w, g, m, v)


def _pick_rows_8(n, target):
    best = n
    for b in range(8, min(n, target) + 1, 8):
        if n % b == 0:
            best = b
    return best


W_IN_COLS = 2308
W_UP_COLS = 1408
W_DOWN_ROWS = 704
DN_CONV_COLS = 768
FFN_CONV_COLS = 1408
PROJ_ROWS = 256
ROW_TILE = 16
ROW_SEGS = [("wp_dn", PROJ_ROWS), ("wp_sb", PROJ_ROWS), ("w_out", PROJ_ROWS), ("w_down", W_DOWN_ROWS),
            ("dn_conv", ROW_TILE), ("ffn_conv", ROW_TILE), ("spare", 2 * ROW_TILE)]
ROW_OFFS = {nm: (sum(n for _, n in ROW_SEGS[:i]), n) for i, (nm, n) in enumerate(ROW_SEGS)}
STACK_ROWS = sum(n for _, n in ROW_SEGS)
assert all(n % ROW_TILE == 0 for _, n in ROW_SEGS) and STACK_ROWS % (4 * ROW_TILE) == 0
Q_END, A_END, G_END, S_END = 3 * D_MODEL, 3 * D_MODEL + 2 * N_HEADS, 4 * D_MODEL + 2 * N_HEADS, 7 * D_MODEL + 2 * N_HEADS


def _flat_rows(a, nrows):
    flat = a.reshape(-1)
    return jnp.pad(flat, (0, nrows * D_MODEL - flat.shape[0])).reshape(nrows, D_MODEL)


IN_EXTRA_ROWS = 64


def _weight_wire(w_in, wp_dn, wp_sb, w_out, w_up, w_down, dn_conv, ffn_conv):
    bits = lax.bitcast_convert_type(dn_conv, BF16).reshape(-1)
    extra = jnp.pad(bits, (0, IN_EXTRA_ROWS * W_IN_COLS - bits.shape[0])).reshape(IN_EXTRA_ROWS, W_IN_COLS)
    stack = jnp.concatenate([wp_dn.astype(BF16), wp_sb.astype(BF16), w_out.astype(BF16), w_down.astype(BF16),
                             jnp.zeros((ROW_TILE, D_MODEL), BF16),
                             _flat_rows(lax.bitcast_convert_type(ffn_conv, BF16), ROW_TILE),
                             jnp.zeros((ROW_OFFS["spare"][1], D_MODEL), BF16)], axis=0)
    return [jnp.concatenate([w_in.astype(BF16), extra], axis=0)], [w_up.astype(BF16), stack]


def _col_range(g, lo, hi, width):
    parts = []
    for s in range(g.shape[0]):
        a, b = max(lo, s * width), min(hi, (s + 1) * width)
        if a < b:
            parts.append(g[s][:, a - s * width:b - s * width])
    return parts[0] if len(parts) == 1 else jnp.concatenate(parts, axis=1)


def _f32_rows(raw, k, ncols):
    raw = raw.reshape(N_CHIPS, -1)[:, :2 * k * ncols].reshape(N_CHIPS, k * ncols, 2)
    vals = lax.bitcast_convert_type(raw, F32).reshape(N_CHIPS, k, ncols)
    return vals.transpose(1, 0, 2).reshape(k, N_CHIPS * ncols)


def _unpack_early(g_in):
    w = g_in[:, :D_MODEL, :]
    return {
        "w_dnqkv": _col_range(w, 0, Q_END, W_IN_COLS),
        "w_ab": jnp.pad(_col_range(w, Q_END, A_END, W_IN_COLS), ((0, 0), (0, LANES - 2 * N_HEADS))),
        "w_dngate": _col_range(w, A_END, G_END, W_IN_COLS),
        "w_sbqkv": _col_range(w, G_END, S_END, W_IN_COLS),
        "w_gl": _col_range(w, S_END, N_CHIPS * W_IN_COLS, W_IN_COLS),
        "dn_conv": _f32_rows(g_in[:, D_MODEL:, :], DN_CONV, DN_CONV_COLS),
    }


def _unpack_late(g_up, g_stack):
    def seg(nm):
        at, n = ROW_OFFS[nm]
        return g_stack[:, at:at + n, :]

    ffn_conv = _f32_rows(seg("ffn_conv"), FFN_CONV, FFN_CONV_COLS)
    return {
        "wp_dn": seg("wp_dn").reshape(D_MODEL, D_MODEL),
        "wp_sb": seg("wp_sb").reshape(D_MODEL, D_MODEL),
        "w_out": seg("w_out").reshape(D_MODEL, D_MODEL),
        "w_up_g": _col_range(g_up, 0, D_FF, W_UP_COLS), "w_up_u": _col_range(g_up, D_FF, 2 * D_FF, W_UP_COLS),
        "w_down": seg("w_down").reshape(D_FF, D_MODEL),
        "ffn_conv_g": ffn_conv[:, :D_FF], "ffn_conv_u": ffn_conv[:, D_FF:],
    }


def _grad_wire_early(gr):
    def cols(a, ncols):
        return a.reshape(a.shape[0], N_CHIPS, ncols).transpose(1, 0, 2)

    def rows(a, nrows):
        return a.astype(BF16).reshape(N_CHIPS, nrows, a.shape[1])

    def flat(a, nrows):
        a = a.astype(BF16).reshape(N_CHIPS, -1)
        return jnp.pad(a, ((0, 0), (0, nrows * D_MODEL - a.shape[1]))).reshape(N_CHIPS, nrows, D_MODEL)

    up = [gr["w_up_g"], gr["w_up_u"]]
    g_up = jnp.stack([up[s // 2][:, (s % 2) * W_UP_COLS:(s % 2 + 1) * W_UP_COLS].astype(BF16) for s in range(N_CHIPS)])
    g_stack = jnp.concatenate([rows(gr["wp_dn"], PROJ_ROWS), rows(gr["wp_sb"], PROJ_ROWS), rows(gr["w_out"], PROJ_ROWS),
                               rows(gr["w_down"], W_DOWN_ROWS), jnp.zeros((N_CHIPS, ROW_TILE, D_MODEL), BF16),
                               flat(cols(gr["ffn_conv"], FFN_CONV_COLS), ROW_TILE),
                               jnp.zeros((N_CHIPS, ROW_OFFS["spare"][1], D_MODEL), BF16)], axis=1)
    return [g_up, g_stack]


def _grad_wire_late(gr):
    pieces = [(gr["w_dnqkv"], 0), (gr["w_ab"][:, :2 * N_HEADS], Q_END), (gr["w_dngate"], A_END),
              (gr["w_sbqkv"], G_END), (gr["w_gl"], S_END)]
    conv = gr["dn_conv"].reshape(DN_CONV, N_CHIPS, DN_CONV_COLS).transpose(1, 0, 2).reshape(N_CHIPS, -1)

    def block(s):
        lo, hi = s * W_IN_COLS, (s + 1) * W_IN_COLS
        parts = []
        for a, at in pieces:
            b0, b1 = max(lo, at), min(hi, at + a.shape[1])
            if b0 < b1:
                parts.append(a[:, b0 - at:b1 - at].astype(BF16))
        w = parts[0] if len(parts) == 1 else jnp.concatenate(parts, axis=1)
        extra = jnp.pad(conv[s].astype(BF16), (0, IN_EXTRA_ROWS * W_IN_COLS - conv.shape[1]))
        return jnp.concatenate([w, extra.reshape(IN_EXTRA_ROWS, W_IN_COLS)], axis=0)

    return [jnp.stack([block(s) for s in range(N_CHIPS)])]


def _unpack_grad_shard(r_in, r_up, r_stack):
    def seg(nm):
        at, n = ROW_OFFS[nm]
        return r_stack[at:at + n, :]

    return {
        "w_in": r_in[:D_MODEL], "w_up": r_up,
        "wp_dn": seg("wp_dn"), "wp_sb": seg("wp_sb"), "w_out": seg("w_out"), "w_down": seg("w_down"),
        "dn_conv": r_in[D_MODEL:].reshape(-1)[:DN_CONV * DN_CONV_COLS].reshape(DN_CONV, DN_CONV_COLS),
        "ffn_conv": seg("ffn_conv").reshape(-1)[:FFN_CONV * FFN_CONV_COLS].reshape(FFN_CONV, FFN_CONV_COLS),
    }


def _lane_row(v):
    return jnp.pad(v.reshape(1, -1), ((0, 0), (0, LANES - v.size)))


def kernel(x, norm1_w, w_in, dn_conv_w, dn_A_log, dn_dt_bias, dn_norm_w, w_proj_dn, w_proj_sb, w_out, norm2_w, ffn_w_up, ffn_conv_w, ffn_w_down, norm_f_w, loss_target, m_norm1_w, m_w_in, m_dn_conv_w, m_dn_A_log, m_dn_dt_bias, m_dn_norm_w, m_w_proj_dn, m_w_proj_sb, m_w_out, m_norm2_w, m_ffn_w_up, m_ffn_conv_w, m_ffn_w_down, m_norm_f_w, v_norm1_w, v_w_in, v_dn_conv_w, v_dn_A_log, v_dn_dt_bias, v_dn_norm_w, v_w_proj_dn, v_w_proj_sb, v_w_out, v_norm2_w, v_ffn_w_up, v_ffn_conv_w, v_ffn_w_down, v_norm_f_w):
    early, late = _weight_wire(w_in[0], w_proj_dn[0], w_proj_sb[0], w_out[0], ffn_w_up[0], ffn_w_down[0],
                               dn_conv_w[0], ffn_conv_w[0])
    chip_idx = (2 * lax.axis_index("x") + lax.axis_index("y")).astype(jnp.int32)

    def with_mine(gathered, wire):
        return [lax.dynamic_update_slice(g, mine[None], (chip_idx, 0, 0)) for g, mine in zip(gathered, wire)]

    wts = _unpack_early(*with_mine(_gather_shards(early), early))
    wts.update(norm1=norm1_w, norm2=norm2_w, normf=norm_f_w.reshape(1, D_MODEL), dn_norm=dn_norm_w,
               alog=_lane_row(dn_A_log), dtb=_lane_row(dn_dt_bias))

    c_idx = lax.axis_index("c").astype(jnp.int32).reshape(1)

    def pair_sums(wire_g, tags, when):
        return [_pair_add(g, got, c_idx, tag) for g, got, tag in zip(wire_g, _pair_exchange_halves(wire_g, when), tags)]

    loss_part, grad_x, gr, (early_sums, early_arrived) = _local_step(
        x[0], loss_target[0], wts, late, lambda gathered: _unpack_late(*with_mine(gathered, late)),
        lambda grads: pair_sums(_grad_wire_early(grads), ["w_up", "rows"], "early"))

    late_sums = pair_sums(_grad_wire_late(gr), ["w_in"], "late")
    tags = ["w_in", "w_up", "rows"]
    reduced = [_sum_partials(p, got, chip_idx.reshape(1), tag)
               for p, got, tag in zip(late_sums + early_sums, list(_chip_exchange(late_sums)) + list(early_arrived), tags)]
    is_south = lax.axis_index("c") == 0
    gsh = _unpack_grad_shard(*[jnp.concatenate([jnp.where(is_south, mine, other), jnp.where(is_south, other, mine)],
                                               axis=0) for mine, other in zip(reduced, _pair_share(reduced))])

    tail = jnp.concatenate([gr["dn_norm"], gr["alog"][:, :N_HEADS], gr["dtb"][:, :N_HEADS], loss_part[:, :1]], axis=1)
    small = jnp.concatenate([gr["norm1"], gr["norm2"], gr["normf"],
                             jnp.pad(tail, ((0, 0), (0, D_MODEL - tail.shape[1]))),
                             jnp.zeros((SMALL_ROWS - 4, D_MODEL), F32)], axis=0)
    small = _small_allreduce(small)
    at = HEAD_DIM
    g_small = {"norm1_w": small[0:1], "norm2_w": small[1:2], "norm_f_w": small[2],
               "dn_norm_w": small[3:4, :at], "dn_A_log": small[3:4, at:at + N_HEADS],
               "dn_dt_bias": small[3:4, at + N_HEADS:at + 2 * N_HEADS]}
    loss = small[3, at + 2 * N_HEADS]

    big = {"w_in": (w_in, m_w_in, v_w_in, gsh["w_in"]), "dn_conv_w": (dn_conv_w, m_dn_conv_w, v_dn_conv_w, gsh["dn_conv"]),
           "w_proj_dn": (w_proj_dn, m_w_proj_dn, v_w_proj_dn, gsh["wp_dn"]),
           "w_proj_sb": (w_proj_sb, m_w_proj_sb, v_w_proj_sb, gsh["wp_sb"]),
           "w_out": (w_out, m_w_out, v_w_out, gsh["w_out"]),
           "ffn_w_up": (ffn_w_up, m_ffn_w_up, v_ffn_w_up, gsh["w_up"]),
           "ffn_conv_w": (ffn_conv_w, m_ffn_conv_w, v_ffn_conv_w, gsh["ffn_conv"]),
           "ffn_w_down": (ffn_w_down, m_ffn_w_down, v_ffn_w_down, gsh["w_down"])}
    res = {}
    for nm, (w, m, v, g) in big.items():
        d, nm_, nv_ = _adamw(w[0], g, m[0], v[0], "adamw_" + nm)
        res[nm] = (g[None], d[None], nm_[None], nv_[None])

    names = ["norm1_w", "norm2_w", "norm_f_w", "dn_norm_w", "dn_A_log", "dn_dt_bias"]
    given = {"norm1_w": (norm1_w, m_norm1_w, v_norm1_w), "norm2_w": (norm2_w, m_norm2_w, v_norm2_w),
             "norm_f_w": (norm_f_w, m_norm_f_w, v_norm_f_w), "dn_norm_w": (dn_norm_w, m_dn_norm_w, v_dn_norm_w),
             "dn_A_log": (dn_A_log, m_dn_A_log, v_dn_A_log), "dn_dt_bias": (dn_dt_bias, m_dn_dt_bias, v_dn_dt_bias)}

    def stack(k, fill):
        rows = [jnp.pad(given[nm][k].reshape(1, -1), ((0, 0), (0, D_MODEL - given[nm][k].size)),
                        constant_values=fill) for nm in names]
        return jnp.concatenate(rows + [jnp.full((SMALL_ROWS - len(names), D_MODEL), fill, F32)], axis=0)

    g_rows = jnp.concatenate(
        [jnp.pad(g_small[nm].reshape(1, -1), ((0, 0), (0, D_MODEL - g_small[nm].size))) for nm in names]
        + [jnp.zeros((SMALL_ROWS - len(names), D_MODEL), F32)], axis=0)
    d_s, m_s, v_s = _adamw(stack(0, 0.0), g_rows, stack(1, 0.0), stack(2, 1.0), "adamw_small")
    for r, nm in enumerate(names):
        shape = given[nm][0].shape
        n = given[nm][0].size
        res[nm] = (g_small[nm].reshape(shape), d_s[r, :n].reshape(shape), m_s[r, :n].reshape(shape),
                   v_s[r, :n].reshape(shape))

    order = ["norm1_w", "w_in", "dn_conv_w", "dn_A_log", "dn_dt_bias", "dn_norm_w", "w_proj_dn", "w_proj_sb",
             "w_out", "norm2_w", "ffn_w_up", "ffn_conv_w", "ffn_w_down", "norm_f_w"]
    outs = [loss, grad_x[None]]
    for k in range(4):
        outs += [res[nm][k] for nm in order]
    return tuple(outs)
```

```python
import functools

import jax
import jax.numpy as jnp
from jax import lax
from jax.experimental import pallas as pl
from jax.experimental.pallas import tpu as pltpu

F32 = jnp.float32
BF16 = jnp.bfloat16
HIGHEST = lax.Precision.HIGHEST
MESH = pl.DeviceIdType.MESH

EPS = 1e-6
D_MODEL = 1024
N_HEADS = 8
HEAD_DIM = 128
DN_CONV = 4
DN_CHUNK = 64
D_FF = 2816
FFN_CONV = 3
ADAM_LR, ADAM_B1, ADAM_B2, ADAM_EPS, ADAM_WD, ADAM_STEP = 0.001, 0.9, 0.999, 1e-08, 0.01, 10

N_CHIPS = 4
LANES = 128
HALO = 8
VMEM_LIMIT = 48 * 1024 * 1024
SMALL_ROWS = 8


def _params(sem=None):
    return pltpu.CompilerParams(dimension_semantics=sem, vmem_limit_bytes=VMEM_LIMIT)


def _pick(n, target):
    best = None
    for b in range(LANES, min(n, target) + 1, LANES):
        if n % b == 0:
            best = b
    return best or n


ELEMENTWISE_COLS = 1408


def _rows(t, target=256):
    return min(t, target)


def _dot(a, b, precision=None):
    return lax.dot_general(a, b, (((1,), (0,)), ((), ())), precision=precision, preferred_element_type=F32)


def _dot_nt(a, b, precision=None):
    return lax.dot_general(a, b, (((1,), (1,)), ((), ())), precision=precision, preferred_element_type=F32)


def _dot_tn(a, b, precision=None):
    return lax.dot_general(a, b, (((0,), (0,)), ((), ())), precision=precision, preferred_element_type=F32)


def _rms(x, w):
    return x * lax.rsqrt(jnp.mean(x * x, axis=-1, keepdims=True) + EPS) * w


def _silu(x):
    return x * jax.nn.sigmoid(x)


def _softplus(x):
    return jnp.maximum(x, 0.0) + jnp.log(1.0 + jnp.exp(-jnp.abs(x)))


MM_BLOCK = 1408
MM_VMEM_BUDGET = 38 * 1024 * 1024


def _mm(a, b, *, ta=False, tb=False, add=None, out_dtype=F32, name, bm=MM_BLOCK, bn=MM_BLOCK, bk=MM_BLOCK):
    m = a.shape[1] if ta else a.shape[0]
    k = a.shape[0] if ta else a.shape[1]
    n = b.shape[0] if tb else b.shape[1]
    bm, bn = _pick(m, bm), _pick(n, bn)

    def vmem_need(bk_):
        need = 2 * (bm * bk_ * a.dtype.itemsize + bk_ * bn * b.dtype.itemsize) + 2 * bm * bn * jnp.dtype(out_dtype).itemsize
        need += 2 * bm * bn * add.dtype.itemsize if add is not None else 0
        return need + (bm * bn * 4 if bk_ < k else 0)

    bk = max((d for d in range(LANES, k + 1, LANES) if k % d == 0 and vmem_need(d) <= MM_VMEM_BUDGET),
             default=_pick(k, bk))
    nk = k // bk
    dims = (((0 if ta else 1,), (1 if tb else 0,)), ((), ()))

    def body(*refs):
        a_ref, b_ref = refs[:2]
        c_ref = refs[2] if add is not None else None
        o_ref = refs[3] if add is not None else refs[2]
        acc = refs[-1]
        kk = pl.program_id(2)
        part = lax.dot_general(a_ref[...].astype(BF16), b_ref[...].astype(BF16), dims, preferred_element_type=F32)

        def finish(r):
            if add is not None:
                r = r + c_ref[...].astype(F32)
            o_ref[...] = r.astype(out_dtype)

        if nk == 1:
            finish(part)
            return

        @pl.when(kk == 0)
        def _():
            acc[...] = part

        @pl.when(jnp.logical_and(kk > 0, kk < nk - 1))
        def _():
            acc[...] += part

        @pl.when(kk == nk - 1)
        def _():
            finish(acc[...] + part)

    a_spec = (pl.BlockSpec((bk, bm), lambda i, j, kk: (kk, i)) if ta
              else pl.BlockSpec((bm, bk), lambda i, j, kk: (i, kk)))
    b_spec = (pl.BlockSpec((bn, bk), lambda i, j, kk: (j, kk)) if tb
              else pl.BlockSpec((bk, bn), lambda i, j, kk: (kk, j)))
    o_spec = pl.BlockSpec((bm, bn), lambda i, j, kk: (i, j))
    in_specs = [a_spec, b_spec] + ([o_spec] if add is not None else [])
    args = (a, b) + ((add,) if add is not None else ())
    return pl.pallas_call(
        body, name=name, grid=(m // bm, n // bn, nk),
        in_specs=in_specs, out_specs=o_spec,
        out_shape=jax.ShapeDtypeStruct((m, n), out_dtype),
        scratch_shapes=[pltpu.VMEM((bm, bn), F32)] if nk > 1 else [],
        compiler_params=_params(("parallel", "parallel", "arbitrary")),
    )(*args)


def _norm1_fwd(x, w, w_ab):
    t = x.shape[0]
    tb = _rows(t)

    def body(x_ref, w_ref, wab_ref, n_ref, hab_ref):
        n = _rms(x_ref[...], w_ref[...]).astype(BF16)
        n_ref[...] = n
        hab_ref[...] = _dot(n, wab_ref[...])

    return pl.pallas_call(
        body, name="norm1_fwd", grid=(t // tb,),
        in_specs=[pl.BlockSpec((tb, D_MODEL), lambda i: (i, 0)),
                  pl.BlockSpec((1, D_MODEL), lambda i: (0, 0)),
                  pl.BlockSpec((D_MODEL, LANES), lambda i: (0, 0))],
        out_specs=[pl.BlockSpec((tb, D_MODEL), lambda i: (i, 0)),
                   pl.BlockSpec((tb, LANES), lambda i: (i, 0))],
        out_shape=[jax.ShapeDtypeStruct((t, D_MODEL), BF16), jax.ShapeDtypeStruct((t, LANES), F32)],
        compiler_params=_params(("arbitrary",)),
    )(x, w, w_ab)


def _norm1_bwd(x, w, dn, dres, dab, w_ab):
    t = x.shape[0]
    tb = _rows(t)

    def body(x_ref, w_ref, dn_ref, dres_ref, dab_ref, wab_ref, dx_ref, dw_ref):
        i = pl.program_id(0)
        g = dn_ref[...] + _dot_nt(dab_ref[...].astype(BF16), wab_ref[...])
        _, vjp = jax.vjp(_rms, x_ref[...], w_ref[...])
        dx, dw = vjp(g)
        dx_ref[...] = dres_ref[...] + dx

        @pl.when(i == 0)
        def _():
            dw_ref[...] = jnp.zeros_like(dw_ref)

        dw_ref[...] += dw

    row = pl.BlockSpec((tb, D_MODEL), lambda i: (i, 0))
    vec = pl.BlockSpec((1, D_MODEL), lambda i: (0, 0))
    return pl.pallas_call(
        body, name="norm1_bwd", grid=(t // tb,),
        in_specs=[row, vec, row, row, pl.BlockSpec((tb, LANES), lambda i: (i, 0)),
                  pl.BlockSpec((D_MODEL, LANES), lambda i: (0, 0))],
        out_specs=[row, vec],
        out_shape=[jax.ShapeDtypeStruct((t, D_MODEL), F32), jax.ShapeDtypeStruct((1, D_MODEL), F32)],
        compiler_params=_params(("arbitrary",)),
    )(x, w, dn, dres, dab, w_ab)


def _conv_fwd(x, w, name):
    t, c = x.shape
    kk = w.shape[0]
    tb, cb = _rows(t, 512), _pick(c, ELEMENTWISE_COLS)
    per = tb // HALO

    def body(x_ref, halo_ref, w_ref, y_ref, buf):
        i = pl.program_id(0)
        buf[pl.ds(HALO, tb), :] = x_ref[...]
        buf[pl.ds(0, HALO), :] = jnp.where(i == 0, 0.0, halo_ref[...])
        y_ref[...] = _conv_taps(buf, w_ref, HALO - (kk - 1), tb)

    return pl.pallas_call(
        body, name=name, grid=(t // tb, c // cb),
        in_specs=[pl.BlockSpec((tb, cb), lambda i, j: (i, j)),
                  pl.BlockSpec((HALO, cb), lambda i, j: (jnp.maximum(i * per - 1, 0), j)),
                  pl.BlockSpec((kk, cb), lambda i, j: (0, j))],
        out_specs=pl.BlockSpec((tb, cb), lambda i, j: (i, j)),
        out_shape=jax.ShapeDtypeStruct((t, c), F32),
        scratch_shapes=[pltpu.VMEM((tb + HALO, cb), F32)],
        compiler_params=_params(("parallel", "parallel")),
    )(x, x, w)


def _conv_bwd(dy, x, w, name, dx_dtype):
    t, c = x.shape
    kk = w.shape[0]
    tb, cb = _rows(t, 512), _pick(c, ELEMENTWISE_COLS)
    per = tb // HALO
    nblk = t // tb

    def body(dy_ref, after_ref, x_ref, w_ref, dx_ref, dw_ref, dbuf):
        i = pl.program_id(1)
        dbuf[pl.ds(0, tb), :] = dy_ref[...]
        dbuf[pl.ds(tb, HALO), :] = jnp.where(i == nblk - 1, 0.0, after_ref[...])

        @pl.when(i == 0)
        def _():
            dw_ref[...] = jnp.zeros_like(dw_ref)

        for j in range(cb // LANES):
            sl = pl.ds(j * LANES, LANES)
            x = x_ref[:, sl]
            dx = None
            for s in range(kk):
                shifted = dbuf[pl.ds(kk - 1 - s, tb), sl]
                term = w_ref[s:s + 1, sl] * shifted
                dx = term if dx is None else dx + term
                dw_ref[s:s + 1, sl] += jnp.sum(shifted * x, axis=0, keepdims=True)
            dx_ref[:, sl] = dx.astype(dx_dtype)

    blk = pl.BlockSpec((tb, cb), lambda j, i: (i, j))
    return pl.pallas_call(
        body, name=name, grid=(c // cb, nblk),
        in_specs=[blk,
                  pl.BlockSpec((HALO, cb), lambda j, i: (jnp.minimum((i + 1) * per, t // HALO - 1), j)),
                  blk,
                  pl.BlockSpec((kk, cb), lambda j, i: (0, j))],
        out_specs=[blk, pl.BlockSpec((HALO, cb), lambda j, i: (0, j))],
        out_shape=[jax.ShapeDtypeStruct((t, c), dx_dtype), jax.ShapeDtypeStruct((HALO, c), F32)],
        scratch_shapes=[pltpu.VMEM((tb + HALO, cb), F32)],
        compiler_params=_params(("parallel", "arbitrary")),
    )(dy, dy, x, w)


def _dn_head(c, normed):
    s = _silu(c)
    return s * lax.rsqrt(jnp.sum(s * s, axis=-1, keepdims=True) + EPS) if normed else s


def _dn_gates(hab, alog, dtb):
    lane = lax.broadcasted_iota(jnp.int32, hab.shape, 1)
    g = -jnp.exp(alog) * _softplus(hab + dtb)
    beta = jax.nn.sigmoid(hab)
    return jnp.where(lane < N_HEADS, g, jnp.where(lane < 2 * N_HEADS, beta, 0.0))


def _dn_head_slices(q_ref, k_ref, v_ref):
    return [(pl.ds((part * N_HEADS + h) * HEAD_DIM, HEAD_DIM), ref, h, part < 2)
            for part, ref in enumerate((q_ref, k_ref, v_ref)) for h in range(N_HEADS)]


def _dn_prep_fwd(c, hab, alog, dtb):
    t = c.shape[0]
    tb = _rows(t)

    def body(c_ref, hab_ref, alog_ref, dtb_ref, q_ref, k_ref, v_ref, gb_ref):
        for sl, ref, h, normed in _dn_head_slices(q_ref, k_ref, v_ref):
            ref[h] = _dn_head(c_ref[:, sl], normed)
        gb_ref[...] = _dn_gates(hab_ref[...], alog_ref[...], dtb_ref[...])

    hm = pl.BlockSpec((N_HEADS, tb, HEAD_DIM), lambda i: (0, i, 0))
    nar = pl.BlockSpec((tb, LANES), lambda i: (i, 0))
    vec = pl.BlockSpec((1, LANES), lambda i: (0, 0))
    return pl.pallas_call(
        body, name="dn_prep_fwd", grid=(t // tb,),
        in_specs=[pl.BlockSpec((tb, 3 * D_MODEL), lambda i: (i, 0)), nar, vec, vec],
        out_specs=[hm, hm, hm, nar],
        out_shape=[jax.ShapeDtypeStruct((N_HEADS, t, HEAD_DIM), F32)] * 3 + [jax.ShapeDtypeStruct((t, LANES), F32)],
        compiler_params=_params(("parallel",)),
    )(c, hab, alog, dtb)


def _dn_prep_bwd(c, hab, alog, dtb, dq, dk, dv, dgb):
    t = c.shape[0]
    tb = _rows(t)

    def body(c_ref, hab_ref, alog_ref, dtb_ref, dq_ref, dk_ref, dv_ref, dgb_ref,
             dc_ref, dhab_ref, dalog_ref, ddtb_ref):
        i = pl.program_id(0)
        for sl, ref, h, normed in _dn_head_slices(dq_ref, dk_ref, dv_ref):
            _, vjp = jax.vjp(functools.partial(_dn_head, normed=normed), c_ref[:, sl])
            dc_ref[:, sl] = vjp(ref[h])[0]
        _, vjp = jax.vjp(_dn_gates, hab_ref[...], alog_ref[...], dtb_ref[...])
        dhab, dalog, ddtb = vjp(dgb_ref[...])
        dhab_ref[...] = dhab

        @pl.when(i == 0)
        def _():
            dalog_ref[...] = jnp.zeros_like(dalog_ref)
            ddtb_ref[...] = jnp.zeros_like(ddtb_ref)

        dalog_ref[...] += dalog
        ddtb_ref[...] += ddtb

    hm = pl.BlockSpec((N_HEADS, tb, HEAD_DIM), lambda i: (0, i, 0))
    wide = pl.BlockSpec((tb, 3 * D_MODEL), lambda i: (i, 0))
    nar = pl.BlockSpec((tb, LANES), lambda i: (i, 0))
    vec = pl.BlockSpec((1, LANES), lambda i: (0, 0))
    return pl.pallas_call(
        body, name="dn_prep_bwd", grid=(t // tb,),
        in_specs=[wide, nar, vec, vec, hm, hm, hm, nar],
        out_specs=[wide, nar, vec, vec],
        out_shape=[jax.ShapeDtypeStruct((t, 3 * D_MODEL), F32), jax.ShapeDtypeStruct((t, LANES), F32),
                   jax.ShapeDtypeStruct((1, LANES), F32), jax.ShapeDtypeStruct((1, LANES), F32)],
        compiler_params=_params(("arbitrary",)),
    )(c, hab, alog, dtb, dq, dk, dv, dgb)


DN_PREC = lax.Precision.HIGH
DN_GROUP = 8


def _dn_prec(a):
    return DN_PREC if a.dtype == F32 else None


def _bdot(a, b):
    return lax.dot_general(a, b, (((2,), (1,)), ((0,), (0,))), precision=_dn_prec(a), preferred_element_type=F32)


def _bdot_nt(a, b):
    return lax.dot_general(a, b, (((2,), (2,)), ((0,), (0,))), precision=_dn_prec(a), preferred_element_type=F32)


def _bdot_tn(a, b):
    return lax.dot_general(a, b, (((1,), (1,)), ((0,), (0,))), precision=_dn_prec(a), preferred_element_type=F32)


def _unit_lower_inverse(lmat):
    c = lmat.shape[-1]
    ri = lax.broadcasted_iota(jnp.int32, (c, c), 0)
    ci = lax.broadcasted_iota(jnp.int32, (c, c), 1)
    p = -lmat
    tinv = jnp.where(ri == ci, 1.0, 0.0) + p
    for _ in range(max(c.bit_length() - 2, 0)):
        p = _bdot(p, p)
        tinv = tinv + _bdot(tinv, p)
    return tinv


@jax.custom_vjp
def _solve_with(lmat, rhs, tinv):
    return _bdot(tinv, rhs)


def _solve_with_fwd(lmat, rhs, tinv):
    sol = _bdot(tinv, rhs)
    return sol, (sol, tinv)


def _solve_with_bwd(res, dsol):
    sol, tinv = res
    drhs = _bdot_tn(tinv, dsol)
    return -_bdot_nt(drhs, sol), drhs, jnp.zeros_like(tinv)


_solve_with.defvjp(_solve_with_fwd, _solve_with_bwd)


def _dn_local(q, k, v, grow, brow, tinv):
    g, c, _ = q.shape
    ri = lax.broadcasted_iota(jnp.int32, (c, c), 0)
    ci = lax.broadcasted_iota(jnp.int32, (c, c), 1)
    lower = ri >= ci
    as_col = lambda r: jnp.sum(jnp.where(ri == ci, jnp.broadcast_to(r, (g, c, c)), 0.0), axis=2, keepdims=True)
    gcol, bcol = as_col(grow), as_col(brow)
    gc_col = jnp.sum(jnp.where(lower, jnp.broadcast_to(grow, (g, c, c)), 0.0), axis=2, keepdims=True)
    gc_row = jnp.sum(jnp.where(ri <= ci, jnp.broadcast_to(gcol, (g, c, c)), 0.0), axis=1, keepdims=True)
    qs = q * (HEAD_DIM ** -0.5)
    kb = k * bcol
    vb = v * bcol
    decay = jnp.where(lower, jnp.exp(jnp.where(lower, gc_col - gc_row, 0.0)), 0.0)
    lmat = jnp.where(ri > ci, _bdot_nt(kb.astype(BF16), k.astype(BF16)) * decay, 0.0)
    eg = jnp.exp(gc_col)
    rhs = jnp.concatenate([vb, kb * eg], axis=2)
    if tinv is None:
        tinv = _unit_lower_inverse(lmat)
    sol = _solve_with(lmat, rhs, tinv)
    a_qk = jnp.where(lower, _bdot_nt(qs.astype(BF16), k.astype(BF16)) * decay, 0.0)
    g_last = jnp.sum(grow, axis=2, keepdims=True)
    kdec = k * jnp.exp(g_last - gc_col)
    egl = jnp.broadcast_to(jnp.exp(g_last), (g, 1, HEAD_DIM))
    return sol[:, :, :HEAD_DIM], sol[:, :, HEAD_DIM:], a_qk, qs * eg, kdec, egl, tinv


def _dn_seq(u, w, a_qk, qe, kdec, egl, s_in):
    b16 = lambda x: x.astype(BF16)
    v_new = u - _bdot(b16(w), b16(s_in))
    o = _bdot(b16(qe), b16(s_in)) + _bdot(b16(a_qk), b16(v_new))
    return o, s_in * egl + _bdot_tn(b16(kdec), b16(v_new))


def _dn_local_specs(t):
    grp = min(DN_GROUP, t // DN_CHUNK)
    rows = grp * DN_CHUNK
    blk = pl.BlockSpec((1, rows, HEAD_DIM), lambda h, i: (h, i, 0))
    row = pl.BlockSpec((1, grp, 1, DN_CHUNK), lambda h, i: (h, i, 0, 0))
    sq = pl.BlockSpec((1, grp, DN_CHUNK, DN_CHUNK), lambda h, i: (h, i, 0, 0))
    lane = pl.BlockSpec((1, grp, 1, HEAD_DIM), lambda h, i: (h, i, 0, 0))
    return grp, blk, row, sq, lane


def _dn_shapes(t):
    nchunk = t // DN_CHUNK
    big = jax.ShapeDtypeStruct((N_HEADS, t, HEAD_DIM), F32)
    row = jax.ShapeDtypeStruct((N_HEADS, nchunk, 1, DN_CHUNK), F32)
    sq = jax.ShapeDtypeStruct((N_HEADS, nchunk, DN_CHUNK, DN_CHUNK), F32)
    lane = jax.ShapeDtypeStruct((N_HEADS, nchunk, 1, HEAD_DIM), F32)
    return big, row, sq, lane


def _dn_local_fwd(q, k, v, grow, brow, wire=()):
    t = q.shape[1]
    grp, blk, row, sq, lane = _dn_local_specs(t)
    big, _, sqs, lanes = _dn_shapes(t)
    n = len(wire)
    groups = t // (grp * DN_CHUNK)
    steps = N_HEADS * groups

    def body(q_ref, k_ref, v_ref, gr_ref, br_ref, *rest):
        u_ref, w_ref, a_ref, qe_ref, kd_ref, egl_ref, t_ref = rest[n:n + 7]
        if n:
            begin, middle, end = _gather_protocol(rest[:n], rest[n + 7:2 * n + 7], *rest[2 * n + 7:])
            step = pl.program_id(0) * groups + pl.program_id(1)
            pl.when(step == 0)(begin)
            pl.when(step == (5 * steps) // 8)(middle)
        split = lambda r: r[0].reshape(grp, DN_CHUNK, HEAD_DIM)
        u, w, a_qk, qe, kdec, egl, tinv = _dn_local(split(q_ref), split(k_ref), split(v_ref), gr_ref[0],
                                                     br_ref[0], None)
        for ref, val in ((u_ref, u), (w_ref, w), (qe_ref, qe), (kd_ref, kdec)):
            ref[0] = val.reshape(grp * DN_CHUNK, HEAD_DIM)
        a_ref[0] = a_qk
        egl_ref[0] = egl
        t_ref[0] = tinv
        if n:
            pl.when(step == steps - 1)(end)

    assert n == 0 or steps >= 3
    return pl.pallas_call(
        body, name="dn_local_fwd", grid=(N_HEADS, groups),
        in_specs=[blk, blk, blk, row, row] + _hbm_specs(n),
        out_specs=[blk, blk, sq, blk, blk, lane, sq] + _hbm_specs(n),
        out_shape=[big, big, sqs, big, big, lanes, sqs] + _gather_out_shapes(wire),
        scratch_shapes=_gather_sems(n) if n else [],
        compiler_params=_params(("arbitrary", "arbitrary")),
    )(q, k, v, grow, brow, *wire)


def _dn_local_bwd(q, k, v, grow, brow, tinv, du, dw, da, dqe, dkd, degl, partials=()):
    t = q.shape[1]
    grp, blk, row, sq, lane = _dn_local_specs(t)
    big, rows_, _, _ = _dn_shapes(t)
    n = len(partials)
    groups = t // (grp * DN_CHUNK)
    steps = N_HEADS * groups

    def body(q_ref, k_ref, v_ref, gr_ref, br_ref, t_ref, du_ref, dw_ref, da_ref, dqe_ref, dkd_ref,
             degl_ref, *rest):
        dq_ref, dk_ref, dv_ref, dgr_ref, dbr_ref = rest[n:n + 5]
        if n:
            begin, end = _chip_exchange_protocol(rest[:n], rest[n + 5:2 * n + 5], *rest[2 * n + 5:])
            step = pl.program_id(0) * groups + pl.program_id(1)
            pl.when(step == 0)(begin)
        split = lambda r: r[0].reshape(grp, DN_CHUNK, HEAD_DIM)
        tinv_v = t_ref[0]
        fn = lambda q_, k_, v_, gr_, br_: _dn_local(q_, k_, v_, gr_, br_, tinv_v)[:6]
        _, vjp = jax.vjp(fn, split(q_ref), split(k_ref), split(v_ref), gr_ref[0], br_ref[0])
        dq, dk, dv, dgr, dbr = vjp((split(du_ref), split(dw_ref), da_ref[0], split(dqe_ref), split(dkd_ref),
                                    degl_ref[0]))
        for ref, val in ((dq_ref, dq), (dk_ref, dk), (dv_ref, dv)):
            ref[0] = val.reshape(grp * DN_CHUNK, HEAD_DIM)
        dgr_ref[0] = dgr
        dbr_ref[0] = dbr
        if n:
            pl.when(step == steps - 1)(end)

    assert n == 0 or steps >= 2
    return pl.pallas_call(
        body, name="dn_local_bwd", grid=(N_HEADS, groups),
        in_specs=[blk, blk, blk, row, row, sq, blk, blk, sq, blk, blk, lane] + _hbm_specs(n),
        out_specs=[blk, blk, blk, row, row] + _hbm_specs(n),
        out_shape=[big, big, big, rows_, rows_] + _chip_exchange_shapes(partials),
        scratch_shapes=_chip_exchange_sems(n) if n else [],
        compiler_params=_params(("arbitrary", "arbitrary")),
    )(q, k, v, grow, brow, tinv, du, dw, da, dqe, dkd, degl, *partials)


def _dn_seq_specs(nchunk, rev):
    def idx(n):
        return nchunk - 1 - n if rev else n

    blk = pl.BlockSpec((N_HEADS, DN_CHUNK, HEAD_DIM), lambda n: (0, idx(n), 0))
    sq = pl.BlockSpec((N_HEADS, 1, DN_CHUNK, DN_CHUNK), lambda n: (0, idx(n), 0, 0))
    lane = pl.BlockSpec((N_HEADS, 1, 1, HEAD_DIM), lambda n: (0, idx(n), 0, 0))
    st = pl.BlockSpec((N_HEADS, 1, HEAD_DIM, HEAD_DIM), lambda n: (0, idx(n), 0, 0))
    return blk, sq, lane, st


def _dn_seq_fwd(u, w, a_qk, qe, kdec, egl):
    t = u.shape[1]
    nchunk = t // DN_CHUNK
    blk, sq, lane, st = _dn_seq_specs(nchunk, False)

    def body(u_ref, w_ref, a_ref, qe_ref, kd_ref, egl_ref, o_ref, s_ref, state):
        @pl.when(pl.program_id(0) == 0)
        def _():
            state[...] = jnp.zeros_like(state)

        s_in = state[...]
        s_ref[:, 0] = s_in
        o, s_out = _dn_seq(u_ref[...], w_ref[...], a_ref[:, 0], qe_ref[...], kd_ref[...], egl_ref[:, 0], s_in)
        o_ref[...] = o
        state[...] = s_out

    return pl.pallas_call(
        body, name="dn_seq_fwd", grid=(nchunk,),
        in_specs=[blk, blk, sq, blk, blk, lane],
        out_specs=[blk, st],
        out_shape=[jax.ShapeDtypeStruct((N_HEADS, t, HEAD_DIM), F32),
                   jax.ShapeDtypeStruct((N_HEADS, nchunk, HEAD_DIM, HEAD_DIM), F32)],
        scratch_shapes=[pltpu.VMEM((N_HEADS, HEAD_DIM, HEAD_DIM), F32)],
        compiler_params=_params(("arbitrary",)),
    )(u, w, a_qk, qe, kdec, egl)


def _dn_seq_bwd(u, w, a_qk, qe, kdec, egl, states, do):
    t = u.shape[1]
    nchunk = t // DN_CHUNK
    blk, sq, lane, st = _dn_seq_specs(nchunk, True)
    big, _, sqs, lanes = _dn_shapes(t)

    def body(u_ref, w_ref, a_ref, qe_ref, kd_ref, egl_ref, s_ref, do_ref,
             du_ref, dw_ref, da_ref, dqe_ref, dkd_ref, degl_ref, dstate):
        @pl.when(pl.program_id(0) == 0)
        def _():
            dstate[...] = jnp.zeros_like(dstate)

        _, vjp = jax.vjp(_dn_seq, u_ref[...], w_ref[...], a_ref[:, 0], qe_ref[...], kd_ref[...], egl_ref[:, 0],
                         s_ref[:, 0])
        du, dw, da, dqe, dkd, degl, ds = vjp((do_ref[...], dstate[...]))
        du_ref[...] = du
        dw_ref[...] = dw
        da_ref[:, 0] = da
        dqe_ref[...] = dqe
        dkd_ref[...] = dkd
        degl_ref[:, 0] = degl
        dstate[...] = ds

    return pl.pallas_call(
        body, name="dn_seq_bwd", grid=(nchunk,),
        in_specs=[blk, blk, sq, blk, blk, lane, st, blk],
        out_specs=[blk, blk, sq, blk, blk, lane],
        out_shape=[big, big, sqs, big, big, lanes],
        scratch_shapes=[pltpu.VMEM((N_HEADS, HEAD_DIM, HEAD_DIM), F32)],
        compiler_params=_params(("arbitrary",)),
    )(u, w, a_qk, qe, kdec, egl, states, do)


def _dn_post_head(o, gate, w):
    return _rms(o, w) * _silu(gate)


def _dn_post_fwd(o, gate, w):
    t = gate.shape[0]
    tb = _rows(t)

    def body(o_ref, g_ref, w_ref, y_ref):
        for h in range(N_HEADS):
            sl = pl.ds(h * HEAD_DIM, HEAD_DIM)
            y_ref[:, sl] = _dn_post_head(o_ref[h], g_ref[:, sl], w_ref[...]).astype(BF16)

    row = pl.BlockSpec((tb, D_MODEL), lambda i: (i, 0))
    hm = pl.BlockSpec((N_HEADS, tb, HEAD_DIM), lambda i: (0, i, 0))
    return pl.pallas_call(
        body, name="dn_post_fwd", grid=(t // tb,),
        in_specs=[hm, row, pl.BlockSpec((1, HEAD_DIM), lambda i: (0, 0))],
        out_specs=row, out_shape=jax.ShapeDtypeStruct((t, D_MODEL), BF16),
        compiler_params=_params(("parallel",)),
    )(o, gate, w)


def _dn_post_bwd(o, gate, w, dy):
    t = gate.shape[0]
    tb = _rows(t)

    def body(o_ref, g_ref, w_ref, dy_ref, do_ref, dg_ref, dw_ref):
        i = pl.program_id(0)
        @pl.when(i == 0)
        def _():
            dw_ref[...] = jnp.zeros_like(dw_ref)

        for h in range(N_HEADS):
            sl = pl.ds(h * HEAD_DIM, HEAD_DIM)
            _, vjp = jax.vjp(_dn_post_head, o_ref[h], g_ref[:, sl], w_ref[...])
            do_ref[h], dg, dw = vjp(dy_ref[:, sl])
            dg_ref[:, sl] = dg.astype(BF16)
            dw_ref[...] += dw

    row = pl.BlockSpec((tb, D_MODEL), lambda i: (i, 0))
    hm = pl.BlockSpec((N_HEADS, tb, HEAD_DIM), lambda i: (0, i, 0))
    vec = pl.BlockSpec((1, HEAD_DIM), lambda i: (0, 0))
    return pl.pallas_call(
        body, name="dn_post_bwd", grid=(t // tb,),
        in_specs=[hm, row, vec, row],
        out_specs=[hm, row, vec],
        out_shape=[jax.ShapeDtypeStruct((N_HEADS, t, HEAD_DIM), F32), jax.ShapeDtypeStruct((t, D_MODEL), BF16),
                   jax.ShapeDtypeStruct((1, HEAD_DIM), F32)],
        compiler_params=_params(("arbitrary",)),
    )(o, gate, w, dy)


def _split_bf16(x):
    hi = x.astype(BF16)
    lo = (x - hi.astype(F32)).astype(BF16)
    return hi, lo


SB_Q_BLOCK = 512
SB_K_BLOCK = 256
SB_NEGLIGIBLE = -60.0


def _sb_logits(q, kb, mask, scale):
    z = _dot_nt(q, kb) * scale
    ls = jnp.minimum(z, 0.0) - jnp.log(1.0 + jnp.exp(-jnp.abs(z)))
    lk = ls - z
    if mask is not None:
        lk = jnp.where(mask, lk, 0.0)
    return ls, lk


def _sb_blocks(t):
    bq = min(SB_Q_BLOCK, t)
    bk = min(SB_K_BLOCK, bq)
    return bq, bk, bq // bk


def _sb_fwd(qkv):
    t = qkv.shape[0]
    bq, bk, nd = _sb_blocks(t)
    scale = HEAD_DIM ** -0.5

    def body(q_ref, k_ref, v_ref, o_ref, tot_ref, used_ref):
        i = pl.program_id(1)
        q = q_ref[...]
        rj = lax.broadcasted_iota(jnp.int32, (bk, bk), 0)
        cj = lax.broadcasted_iota(jnp.int32, (bk, bk), 1)
        after = (rj > cj).astype(BF16)
        trow = lax.broadcasted_iota(jnp.int32, (bq, bk), 0)
        scol = lax.broadcasted_iota(jnp.int32, (bq, bk), 1)

        def tile(j, run, acc, mask):
            off = pl.multiple_of(j * bk, bk)
            kb = k_ref[pl.ds(off, bk), :]
            vb = v_ref[pl.ds(off, bk), :]
            ls, lk = _sb_logits(q, kb, mask, scale)
            hi, lo = _split_bf16(lk)
            between = _dot(hi, after) + _dot(lo, after) + run
            a = jnp.exp(ls + between)
            if mask is not None:
                a = jnp.where(mask, a, 0.0)
            acc = acc + _dot(a.astype(BF16), vb)
            return run + jnp.sum(lk, axis=1, keepdims=True), acc

        run, acc = jnp.zeros((bq, 1), F32), jnp.zeros((bq, HEAD_DIM), F32)
        for d in reversed(range(nd)):
            run, acc = tile(i * nd + d, run, acc, scol + d * bk < trow)
        used = i * nd
        run, acc = lax.fori_loop(0, used, lambda it, c: tile(i * nd - 1 - it, c[0], c[1], None), (run, acc))
        o_ref[...] = acc.astype(BF16)
        tot_ref[...] = jnp.broadcast_to(run, (bq, HEAD_DIM))
        used_ref[...] = jnp.full(used_ref.shape, used, F32)

    qs = pl.BlockSpec((bq, HEAD_DIM), lambda h, i: (i, h))
    ks = pl.BlockSpec((t, HEAD_DIM), lambda h, i: (0, N_HEADS + h))
    vs = pl.BlockSpec((t, HEAD_DIM), lambda h, i: (0, 2 * N_HEADS + h))
    return pl.pallas_call(
        body, name="sb_fwd", grid=(N_HEADS, t // bq),
        in_specs=[qs, ks, vs], out_specs=[qs, qs, pl.BlockSpec((1, 1, 1, LANES), lambda h, i: (h, i, 0, 0))],
        out_shape=[jax.ShapeDtypeStruct((t, D_MODEL), BF16), jax.ShapeDtypeStruct((t, D_MODEL), F32),
                   jax.ShapeDtypeStruct((N_HEADS, t // bq, 1, LANES), F32)],
        compiler_params=_params(("parallel", "arbitrary")),
    )(qkv, qkv, qkv)


def _sb_bwd(qkv, tot, used, do):
    t = qkv.shape[0]
    bq, bk, nd = _sb_blocks(t)
    scale = HEAD_DIM ** -0.5

    def body(q_ref, k_ref, v_ref, tot_ref, used_ref, do_ref, dq_ref, dk_ref, dv_ref):
        i = pl.program_id(1)

        @pl.when(i == 0)
        def _():
            dk_ref[...] = jnp.zeros_like(dk_ref)
            dv_ref[...] = jnp.zeros_like(dv_ref)

        q = q_ref[...]
        do = do_ref[...]
        total = tot_ref[:, 0:1]
        rj = lax.broadcasted_iota(jnp.int32, (bk, bk), 0)
        cj = lax.broadcasted_iota(jnp.int32, (bk, bk), 1)
        upto = (rj <= cj).astype(BF16)
        before = (rj < cj).astype(BF16)
        trow = lax.broadcasted_iota(jnp.int32, (bq, bk), 0)
        scol = lax.broadcasted_iota(jnp.int32, (bq, bk), 1)

        def tile(j, run_k, run_e, dq, mask):
            off = pl.multiple_of(j * bk, bk)
            kb = k_ref[pl.ds(off, bk), :]
            vb = v_ref[pl.ds(off, bk), :]
            ls, lk = _sb_logits(q, kb, mask, scale)
            hi, lo = _split_bf16(lk)
            between = total - (_dot(hi, upto) + _dot(lo, upto) + run_k)
            a = jnp.exp(ls + between)
            if mask is not None:
                a = jnp.where(mask, a, 0.0)
            e = a * _dot_nt(do, vb)
            ehi, elo = _split_bf16(e)
            pre = _dot(ehi, before) + _dot(elo, before) + run_e
            sig = jnp.exp(ls)
            dz = e * (1.0 - sig) - pre * sig
            if mask is not None:
                dz = jnp.where(mask, dz, 0.0)
            dz = (dz * scale).astype(BF16)
            dq = dq + _dot(dz, kb)
            dk_ref[pl.ds(off, bk), :] += _dot_tn(dz, q)
            dv_ref[pl.ds(off, bk), :] += _dot_tn(a.astype(BF16), do)
            return (run_k + jnp.sum(lk, axis=1, keepdims=True),
                    run_e + jnp.sum(e, axis=1, keepdims=True), dq)

        zero = jnp.zeros((bq, 1), F32)
        visited = i * nd
        carry = lax.fori_loop(i * nd - visited, i * nd, lambda j, c: tile(j, c[0], c[1], c[2], None),
                              (zero, zero, jnp.zeros((bq, HEAD_DIM), F32)))
        for d in range(nd):
            carry = tile(i * nd + d, *carry, scol + d * bk < trow)
        dq_ref[...] = carry[2]

    qs = pl.BlockSpec((bq, HEAD_DIM), lambda h, i: (i, h))
    ks = pl.BlockSpec((t, HEAD_DIM), lambda h, i: (0, N_HEADS + h))
    vs = pl.BlockSpec((t, HEAD_DIM), lambda h, i: (0, 2 * N_HEADS + h))
    full = pl.BlockSpec((t, HEAD_DIM), lambda h, i: (0, h))
    big = jax.ShapeDtypeStruct((t, D_MODEL), F32)
    return pl.pallas_call(
        body, name="sb_bwd", grid=(N_HEADS, t // bq),
        in_specs=[qs, ks, vs, qs, pl.BlockSpec((1, 1, 1, LANES), lambda h, i: (h, i, 0, 0)), qs],
        out_specs=[qs, full, full],
        out_shape=[big, big, big],
        compiler_params=_params(("parallel", "arbitrary")),
    )(qkv, qkv, qkv, tot, used, do)


def _merge_fwd(o_dn, o_sb, gl, x, wp_dn, wp_sb, w_out, w2):
    t = x.shape[0]
    tb = _rows(t)

    def body(odn_ref, osb_ref, gl_ref, x_ref, wpd_ref, wps_ref, wo_ref, w2_ref,
             pdn_ref, psb_ref, mix_ref, x1_ref, n2_ref):
        pdn = _dot(odn_ref[...], wpd_ref[...])
        psb = _dot(osb_ref[...], wps_ref[...])
        gates = jax.nn.sigmoid(gl_ref[...])
        mixed = (gates[:, :D_MODEL] * pdn + gates[:, D_MODEL:] * psb).astype(BF16)
        x1 = x_ref[...] + _dot(mixed, wo_ref[...])
        pdn_ref[...] = pdn
        psb_ref[...] = psb
        mix_ref[...] = mixed
        x1_ref[...] = x1
        n2_ref[...] = _rms(x1, w2_ref[...]).astype(BF16)

    row = pl.BlockSpec((tb, D_MODEL), lambda i: (i, 0))
    sq = pl.BlockSpec((D_MODEL, D_MODEL), lambda i: (0, 0))
    f = jax.ShapeDtypeStruct((t, D_MODEL), F32)
    b = jax.ShapeDtypeStruct((t, D_MODEL), BF16)
    return pl.pallas_call(
        body, name="merge_fwd", grid=(t // tb,),
        in_specs=[row, row, pl.BlockSpec((tb, 2 * D_MODEL), lambda i: (i, 0)), row, sq, sq, sq,
                  pl.BlockSpec((1, D_MODEL), lambda i: (0, 0))],
        out_specs=[row] * 5, out_shape=[f, f, b, f, b],
        compiler_params=_params(("parallel",)),
    )(o_dn, o_sb, gl, x, wp_dn, wp_sb, w_out, w2)


def _merge_bwd(dx2, dn2, x1, w2, gl, pdn, psb, wp_dn, wp_sb, w_out):
    t = x1.shape[0]
    tb = _rows(t)

    def body(dx2_ref, dn2_ref, x1_ref, w2_ref, gl_ref, pdn_ref, psb_ref, wpd_ref, wps_ref, wo_ref,
             dx1_ref, dw2_ref, dgl_ref, dpdn_ref, dpsb_ref, dodn_ref, dosb_ref):
        i = pl.program_id(0)
        _, vjp = jax.vjp(_rms, x1_ref[...], w2_ref[...])
        dxn, dw2 = vjp(dn2_ref[...])
        dx1 = dx2_ref[...] + dxn
        dx1_ref[...] = dx1

        @pl.when(i == 0)
        def _():
            dw2_ref[...] = jnp.zeros_like(dw2_ref)

        dw2_ref[...] += dw2
        dmix = _dot_nt(dx1.astype(BF16), wo_ref[...])
        gates = jax.nn.sigmoid(gl_ref[...])
        g_dn, g_sb = gates[:, :D_MODEL], gates[:, D_MODEL:]
        dpdn = (dmix * g_dn).astype(BF16)
        dpsb = (dmix * g_sb).astype(BF16)
        dgl_ref[:, :D_MODEL] = (dmix * pdn_ref[...] * g_dn * (1.0 - g_dn)).astype(BF16)
        dgl_ref[:, D_MODEL:] = (dmix * psb_ref[...] * g_sb * (1.0 - g_sb)).astype(BF16)
        dpdn_ref[...] = dpdn
        dpsb_ref[...] = dpsb
        dodn_ref[...] = _dot_nt(dpdn, wpd_ref[...])
        dosb_ref[...] = _dot_nt(dpsb, wps_ref[...]).astype(BF16)

    row = pl.BlockSpec((tb, D_MODEL), lambda i: (i, 0))
    wide = pl.BlockSpec((tb, 2 * D_MODEL), lambda i: (i, 0))
    sq = pl.BlockSpec((D_MODEL, D_MODEL), lambda i: (0, 0))
    vec = pl.BlockSpec((1, D_MODEL), lambda i: (0, 0))
    f = jax.ShapeDtypeStruct((t, D_MODEL), F32)
    b = jax.ShapeDtypeStruct((t, D_MODEL), BF16)
    return pl.pallas_call(
        body, name="merge_bwd", grid=(t // tb,),
        in_specs=[row, row, row, vec, wide, row, row, sq, sq, sq],
        out_specs=[row, vec, wide, row, row, row, row],
        out_shape=[f, jax.ShapeDtypeStruct((1, D_MODEL), F32), jax.ShapeDtypeStruct((t, 2 * D_MODEL), BF16),
                   b, b, f, b],
        compiler_params=_params(("arbitrary",)),
    )(dx2, dn2, x1, w2, gl, pdn, psb, wp_dn, wp_sb, w_out)


def _conv_taps(buf, w_ref, first, rows, cols=slice(None)):
    y = w_ref[0:1, cols] * buf[pl.ds(first, rows), cols]
    for s in range(1, w_ref.shape[0]):
        y = y + w_ref[s:s + 1, cols] * buf[pl.ds(first + s, rows), cols]
    return y


def _ffn_mid_fwd(pre_g, pre_u, wg, wu):
    t, c = pre_g.shape
    kk = wg.shape[0]
    tb, cb = _rows(t), _pick(c, ELEMENTWISE_COLS)
    per = tb // HALO

    def body(g_ref, gh_ref, u_ref, uh_ref, wg_ref, wu_ref, a_ref, gbuf, ubuf):
        i = pl.program_id(0)
        for buf, ref, halo in ((gbuf, g_ref, gh_ref), (ubuf, u_ref, uh_ref)):
            buf[pl.ds(HALO, tb), :] = ref[...]
            buf[pl.ds(0, HALO), :] = jnp.where(i == 0, 0.0, halo[...])
        for j in range(cb // LANES):
            sl = pl.ds(j * LANES, LANES)
            ug = _conv_taps(gbuf, wg_ref, HALO - (kk - 1), tb, sl)
            uu = _conv_taps(ubuf, wu_ref, HALO - (kk - 1), tb, sl)
            a_ref[:, sl] = (_silu(ug) * uu).astype(BF16)

    blk = pl.BlockSpec((tb, cb), lambda i, j: (i, j))
    halo = pl.BlockSpec((HALO, cb), lambda i, j: (jnp.maximum(i * per - 1, 0), j))
    wspec = pl.BlockSpec((kk, cb), lambda i, j: (0, j))
    return pl.pallas_call(
        body, name="ffn_mid_fwd", grid=(t // tb, c // cb),
        in_specs=[blk, halo, blk, halo, wspec, wspec], out_specs=blk,
        out_shape=jax.ShapeDtypeStruct((t, c), BF16),
        scratch_shapes=[pltpu.VMEM((tb + HALO, cb), F32)] * 2,
        compiler_params=_params(("parallel", "parallel")),
    )(pre_g, pre_g, pre_u, pre_u, wg, wu)


def _ffn_mid_bwd(pre_g, pre_u, wg, wu, da):
    t, c = pre_g.shape
    kk = wg.shape[0]
    tb, cb = _rows(t), _pick(c, ELEMENTWISE_COLS)
    per = tb // HALO
    nblk = t // tb
    ext = tb + HALO

    def body(g_ref, gb_ref, ga_ref, u_ref, ub_ref, ua_ref, da_ref, daa_ref, wg_ref, wu_ref,
             dg_ref, du_ref, dwg_ref, dwu_ref, gbuf, ubuf, dabuf, dgbuf, dubuf):
        i = pl.program_id(1)
        last = i == nblk - 1
        for buf, ref, before, after in ((gbuf, g_ref, gb_ref, ga_ref), (ubuf, u_ref, ub_ref, ua_ref)):
            buf[pl.ds(0, HALO), :] = jnp.where(i == 0, 0.0, before[...])
            buf[pl.ds(HALO, tb), :] = ref[...]
            buf[pl.ds(HALO + tb, HALO), :] = jnp.where(last, 0.0, after[...])
        dabuf[pl.ds(0, tb), :] = da_ref[...]
        dabuf[pl.ds(tb, HALO), :] = jnp.where(last, 0.0, daa_ref[...])

        @pl.when(i == 0)
        def _():
            dwg_ref[...] = jnp.zeros_like(dwg_ref)
            dwu_ref[...] = jnp.zeros_like(dwu_ref)

        for j in range(cb // LANES):
            sl = pl.ds(j * LANES, LANES)
            ug = _conv_taps(gbuf, wg_ref, HALO - (kk - 1), ext, sl)
            uu = _conv_taps(ubuf, wu_ref, HALO - (kk - 1), ext, sl)
            _, vjp = jax.vjp(lambda g, u: _silu(g) * u, ug, uu)
            dgbuf[:, sl], dubuf[:, sl] = vjp(dabuf[:, sl])
            for dbuf, xbuf, w_ref, dx_ref, dw_ref in ((dgbuf, gbuf, wg_ref, dg_ref, dwg_ref),
                                                      (dubuf, ubuf, wu_ref, du_ref, dwu_ref)):
                x = xbuf[pl.ds(HALO, tb), sl]
                dx = None
                for s in range(kk):
                    shifted = dbuf[pl.ds(kk - 1 - s, tb), sl]
                    term = w_ref[s:s + 1, sl] * shifted
                    dx = term if dx is None else dx + term
                    dw_ref[s:s + 1, sl] += jnp.sum(shifted * x, axis=0, keepdims=True)
                dx_ref[:, sl] = dx.astype(BF16)

    blk = pl.BlockSpec((tb, cb), lambda j, i: (i, j))
    before = pl.BlockSpec((HALO, cb), lambda j, i: (jnp.maximum(i * per - 1, 0), j))
    after = pl.BlockSpec((HALO, cb), lambda j, i: (jnp.minimum((i + 1) * per, t // HALO - 1), j))
    wspec = pl.BlockSpec((kk, cb), lambda j, i: (0, j))
    dwspec = pl.BlockSpec((HALO, cb), lambda j, i: (0, j))
    half = jax.ShapeDtypeStruct((t, c), BF16)
    dwshape = jax.ShapeDtypeStruct((HALO, c), F32)
    return pl.pallas_call(
        body, name="ffn_mid_bwd", grid=(c // cb, nblk),
        in_specs=[blk, before, after, blk, before, after, blk, after, wspec, wspec],
        out_specs=[blk, blk, dwspec, dwspec],
        out_shape=[half, half, dwshape, dwshape],
        scratch_shapes=[pltpu.VMEM((ext + HALO, cb), F32)] * 2 + [pltpu.VMEM((ext, cb), F32)] * 3,
        compiler_params=_params(("parallel", "arbitrary")),
    )(pre_g, pre_g, pre_g, pre_u, pre_u, pre_u, da, da, wg, wu)


def _down_loss(a, w_down, x1, wf, target):
    t = x1.shape[0]
    tb = _rows(t)

    def body(a_ref, wd_ref, x1_ref, wf_ref, tgt_ref, dx2_ref, dwf_ref, loss_ref):
        i = pl.program_id(0)
        x2 = x1_ref[...] + _dot(a_ref[...], wd_ref[...])
        y, vjp = jax.vjp(_rms, x2, wf_ref[...])
        err = y - tgt_ref[...]
        dx2, dwf = vjp(err * (1.0 / D_MODEL))
        dx2_ref[...] = dx2
        part = jnp.sum(jnp.sum(err * err, axis=1, keepdims=True), axis=0, keepdims=True) * (0.5 / D_MODEL)

        @pl.when(i == 0)
        def _():
            dwf_ref[...] = jnp.zeros_like(dwf_ref)
            loss_ref[...] = jnp.zeros_like(loss_ref)

        dwf_ref[...] += dwf
        loss_ref[...] += jnp.broadcast_to(part, loss_ref.shape)

    row = pl.BlockSpec((tb, D_MODEL), lambda i: (i, 0))
    vec = pl.BlockSpec((1, D_MODEL), lambda i: (0, 0))
    return pl.pallas_call(
        body, name="down_loss", grid=(t // tb,),
        in_specs=[pl.BlockSpec((tb, D_FF), lambda i: (i, 0)), pl.BlockSpec((D_FF, D_MODEL), lambda i: (0, 0)),
                  row, vec, row],
        out_specs=[row, vec, pl.BlockSpec((1, LANES), lambda i: (0, 0))],
        out_shape=[jax.ShapeDtypeStruct((t, D_MODEL), F32), jax.ShapeDtypeStruct((1, D_MODEL), F32),
                   jax.ShapeDtypeStruct((1, LANES), F32)],
        compiler_params=_params(("arbitrary",)),
    )(a, w_down, x1, wf, target)


def _local_step(x, target, wts, late_wire=(), late_weights=None, early_partials=None):
    t = x.shape[0]
    nchunk = t // DN_CHUNK

    n1, hab = _norm1_fwd(x, wts["norm1"], wts["w_ab"])
    dnqkv = _mm(n1, wts["w_dnqkv"], name="h_dnqkv")
    dngate = _mm(n1, wts["w_dngate"], name="h_dngate")
    sbqkv = _mm(n1, wts["w_sbqkv"], out_dtype=BF16, name="h_sbqkv")
    gl = _mm(n1, wts["w_gl"], name="h_gl")

    cdn = _conv_fwd(dnqkv, wts["dn_conv"], "dn_conv_fwd")
    qn, kn, vv, gb = _dn_prep_fwd(cdn, hab, wts["alog"], wts["dtb"])
    per_head = gb[:, :2 * N_HEADS].T.reshape(2 * N_HEADS, nchunk, DN_CHUNK)
    grow, brow = per_head[:N_HEADS, :, None, :], per_head[N_HEADS:, :, None, :]
    u_dn, w_dn, a_qk, qe, kdec, egl, tinv, *late = _dn_local_fwd(qn, kn, vv, grow, brow, late_wire)
    if late_wire:
        wts = {**wts, **late_weights(late)}
    o_raw, states = _dn_seq_fwd(u_dn, w_dn, a_qk, qe, kdec, egl)
    o_dn = _dn_post_fwd(o_raw, dngate, wts["dn_norm"])

    o_sb, tot, sb_used = _sb_fwd(sbqkv)

    pdn, psb, mixed, x1, n2 = _merge_fwd(o_dn, o_sb, gl, x, wts["wp_dn"], wts["wp_sb"], wts["w_out"],
                                         wts["norm2"])
    pre_g = _mm(n2, wts["w_up_g"], name="ffn_up_g")
    pre_u = _mm(n2, wts["w_up_u"], name="ffn_up_u")
    act = _ffn_mid_fwd(pre_g, pre_u, wts["ffn_conv_g"], wts["ffn_conv_u"])
    dx2, d_normf, loss_part = _down_loss(act, wts["w_down"], x1, wts["normf"], target)

    grads = {"normf": d_normf}
    da = _mm(dx2, wts["w_down"], tb=True, name="d_act")
    grads["w_down"] = _mm(act, dx2, ta=True, out_dtype=BF16, name="dw_down")
    dpre_g, dpre_u, dcw_g, dcw_u = _ffn_mid_bwd(pre_g, pre_u, wts["ffn_conv_g"], wts["ffn_conv_u"], da)
    grads["ffn_conv"] = jnp.concatenate([dcw_g[:FFN_CONV], dcw_u[:FFN_CONV]], axis=1)
    dn2 = _mm(dpre_g, wts["w_up_g"], tb=True, name="dn2_g")
    dn2 = _mm(dpre_u, wts["w_up_u"], tb=True, add=dn2, name="dn2_u")
    grads["w_up_g"] = _mm(n2, dpre_g, ta=True, out_dtype=BF16, name="dw_up_g")
    grads["w_up_u"] = _mm(n2, dpre_u, ta=True, out_dtype=BF16, name="dw_up_u")

    dx1, grads["norm2"], dgl, dpdn, dpsb, do_dn, do_sb = _merge_bwd(
        dx2, dn2, x1, wts["norm2"], gl, pdn, psb, wts["wp_dn"], wts["wp_sb"], wts["w_out"])
    grads["w_out"] = _mm(mixed, dx1, ta=True, out_dtype=BF16, name="dw_out")
    grads["wp_dn"] = _mm(o_dn, dpdn, ta=True, out_dtype=BF16, name="dw_proj_dn")
    grads["wp_sb"] = _mm(o_sb, dpsb, ta=True, out_dtype=BF16, name="dw_proj_sb")

    partials = early_partials(grads) if early_partials else ()
    dsq, dsk, dsv = _sb_bwd(sbqkv, tot, sb_used, do_sb)
    dsbqkv = jnp.concatenate([dsq, dsk, dsv], axis=1).astype(BF16)

    do_raw, ddngate, grads["dn_norm"] = _dn_post_bwd(o_raw, dngate, wts["dn_norm"], do_dn)
    seq_grads = _dn_seq_bwd(u_dn, w_dn, a_qk, qe, kdec, egl, states, do_raw)
    dqn, dkn, dvv, dgrow, dbrow, *arrived = _dn_local_bwd(qn, kn, vv, grow, brow, tinv, *seq_grads,
                                                          partials=partials)
    dgb = jnp.concatenate([dgrow.reshape(N_HEADS, t), dbrow.reshape(N_HEADS, t)], axis=0).T
    dgb = jnp.pad(dgb, ((0, 0), (0, LANES - 2 * N_HEADS)))
    dcdn, dhab, grads["alog"], grads["dtb"] = _dn_prep_bwd(cdn, hab, wts["alog"], wts["dtb"], dqn, dkn, dvv, dgb)
    ddnqkv, dcw_dn = _conv_bwd(dcdn, dnqkv, wts["dn_conv"], "dn_conv_bwd", BF16)
    grads["dn_conv"] = dcw_dn[:DN_CONV]

    dn1 = _mm(ddnqkv, wts["w_dnqkv"], tb=True, name="dn1_dnqkv")
    dn1 = _mm(ddngate, wts["w_dngate"], tb=True, add=dn1, name="dn1_dngate")
    dn1 = _mm(dsbqkv, wts["w_sbqkv"], tb=True, add=dn1, name="dn1_sbqkv")
    dn1 = _mm(dgl, wts["w_gl"], tb=True, add=dn1, name="dn1_gl")
    grads["w_dnqkv"] = _mm(n1, ddnqkv, ta=True, out_dtype=BF16, name="dw_dnqkv")
    grads["w_dngate"] = _mm(n1, ddngate, ta=True, out_dtype=BF16, name="dw_dngate")
    grads["w_sbqkv"] = _mm(n1, dsbqkv, ta=True, out_dtype=BF16, name="dw_sbqkv")
    grads["w_gl"] = _mm(n1, dgl, ta=True, out_dtype=BF16, name="dw_gl")
    grads["w_ab"] = _mm(n1, dhab, ta=True, out_dtype=BF16, name="dw_ab")
    grad_x, grads["norm1"] = _norm1_bwd(x, wts["norm1"], dn1, dx1, dhab, wts["w_ab"])
    return loss_part, grad_x, grads, (list(partials), arrived)


def _place():
    return lax.axis_index("x"), lax.axis_index("y"), lax.axis_index("c")


def _hbm_specs(n):
    return [pl.BlockSpec(memory_space=pltpu.HBM)] * n


GATHER_SEMS = 8


def _gather_protocol(ins, outs, send_sems, recv_sems):
    n = len(ins)
    x, y, c = _place()
    me = 2 * x + y
    sibling = (x, y, 1 - c)
    xn, yn, dg = (1 - x, y), (x, 1 - y), (1 - x, 1 - y)
    idx = lambda chip: 2 * chip[0] + chip[1]

    def part(a, chip_index, core, quarter=None):
        half = ins[a].shape[0] // 2
        if quarter is None:
            return outs[a].at[chip_index, pl.ds(core * half, half), :]
        return outs[a].at[chip_index, pl.ds(core * half + quarter * (half // 2), half // 2), :]

    def copy(a, k, src, dst, to):
        return pltpu.make_async_remote_copy(src_ref=src, dst_ref=dst, send_sem=send_sems.at[GATHER_SEMS * a + k],
                                            recv_sem=recv_sems.at[GATHER_SEMS * a + k], device_id=to,
                                            device_id_type=MESH)

    def sent(a, k):
        half = ins[a].shape[0] // 2
        my_half = ins[a].at[pl.ds(c * half, half), :]
        if k < 2:
            return copy(a, k, my_half, part(a, me, c), (*(xn, yn)[k], c))
        if k < 4:
            src = part(a, idx((xn, yn)[k - 2]), c, k - 2)
            return copy(a, k, src, src, (*(yn, xn)[k - 2], c))
        src = (part(a, idx(xn), c), part(a, idx(yn), c), part(a, idx(dg), c, 0), part(a, idx(dg), c, 1))[k - 4]
        return copy(a, k, src, src, sibling)

    def landed(a, k):
        dst = (part(a, idx(xn), c), part(a, idx(yn), c), part(a, idx(dg), c, 0), part(a, idx(dg), c, 1),
               part(a, idx(xn), 1 - c), part(a, idx(yn), 1 - c), part(a, idx(dg), 1 - c, 0),
               part(a, idx(dg), 1 - c, 1))[k]
        return copy(a, k, dst, dst, sibling)

    def begin():
        for a in range(n):
            sent(a, 0).start()
            sent(a, 1).start()

    def middle():
        for a in range(n):
            for k in range(2):
                landed(a, k).wait_recv()
                sent(a, 2 + k).start()
                sent(a, 4 + k).start()

    def end():
        for a in range(n):
            for k in (2, 3):
                landed(a, k).wait_recv()
                sent(a, 4 + k).start()
        for a in range(n):
            for k in range(4, GATHER_SEMS):
                landed(a, k).wait_recv()
        for a in range(n):
            for k in range(GATHER_SEMS):
                sent(a, k).wait_send()

    return begin, middle, end


def _gather_out_shapes(shards):
    return [jax.ShapeDtypeStruct((N_CHIPS,) + s.shape, s.dtype) for s in shards]


def _gather_sems(n):
    return [pltpu.SemaphoreType.DMA((GATHER_SEMS * n,)), pltpu.SemaphoreType.DMA((GATHER_SEMS * n,))]


def _gather_shards(shards):
    n = len(shards)

    def body(*refs):
        begin, middle, end = _gather_protocol(refs[:n], refs[n:2 * n], *refs[2 * n:])
        begin()
        middle()
        end()

    return pl.pallas_call(
        body, name="gather_weights", in_specs=_hbm_specs(n), out_specs=_hbm_specs(n),
        out_shape=_gather_out_shapes(shards), scratch_shapes=_gather_sems(n),
    )(*shards)


def _pair_exchange_halves(gs, tag):
    n = len(gs)

    def body(*refs):
        ins, outs, (send_sems, recv_sems) = refs[:n], refs[n:2 * n], refs[2 * n:]
        x, y, c = _place()
        cps = []
        for a in range(n):
            half = ins[a].shape[1] // 2
            cp = pltpu.make_async_remote_copy(src_ref=ins[a].at[:, pl.ds((1 - c) * half, half), :], dst_ref=outs[a],
                                              send_sem=send_sems.at[a], recv_sem=recv_sems.at[a],
                                              device_id=(x, y, 1 - c), device_id_type=MESH)
            cp.start()
            cps.append(cp)
        for cp in cps:
            cp.wait()

    return pl.pallas_call(
        body, name="grad_pair_exchange_" + tag, in_specs=_hbm_specs(n), out_specs=_hbm_specs(n),
        out_shape=[jax.ShapeDtypeStruct((g.shape[0], g.shape[1] // 2, g.shape[2]), g.dtype) for g in gs],
        scratch_shapes=[pltpu.SemaphoreType.DMA((n,)), pltpu.SemaphoreType.DMA((n,))],
    )(*gs)


def _pick_rows(n, target=1024):
    best = 16
    for b in range(16, min(n, target) + 1, 16):
        if n % b == 0:
            best = b
    return best


def _pair_add(g, got, c_idx, tag):
    nsh, rows, cols = g.shape
    half = rows // 2
    rb = _pick_rows(half)

    def body(c_ref, g_ref, got_ref, o_ref):
        o_ref[...] = (g_ref[...].astype(F32) + got_ref[...].astype(F32)).astype(BF16)

    nb = half // rb
    grid_spec = pltpu.PrefetchScalarGridSpec(
        num_scalar_prefetch=1, grid=(nsh, nb),
        in_specs=[pl.BlockSpec((1, rb, cols), lambda s, i, c_ref: (s, c_ref[0] * nb + i, 0)),
                  pl.BlockSpec((1, rb, cols), lambda s, i, c_ref: (s, i, 0))],
        out_specs=pl.BlockSpec((1, rb, cols), lambda s, i, c_ref: (s, i, 0)))
    return pl.pallas_call(
        body, name="grad_pair_add_" + tag, grid_spec=grid_spec,
        out_shape=jax.ShapeDtypeStruct((nsh, half, cols), BF16),
        compiler_params=_params(("parallel", "parallel")),
    )(c_idx, g, got)


def _chip_exchange_protocol(ins, outs, send_sems, recv_sems):
    x, y, c = _place()
    chips = [(1 - x, y), (x, 1 - y), (1 - x, 1 - y)]

    def copies():
        return [pltpu.make_async_remote_copy(src_ref=ins[a].at[2 * px + py], dst_ref=outs[a].at[j],
                                             send_sem=send_sems.at[3 * a + j], recv_sem=recv_sems.at[3 * a + j],
                                             device_id=(px, py, c), device_id_type=MESH)
                for a in range(len(ins)) for j, (px, py) in enumerate(chips)]

    def begin():
        for cp in copies():
            cp.start()

    def end():
        for cp in copies():
            cp.wait_recv()
        for cp in copies():
            cp.wait_send()

    return begin, end


def _chip_exchange_shapes(ps):
    return [jax.ShapeDtypeStruct((N_CHIPS - 1,) + p.shape[1:], p.dtype) for p in ps]


def _chip_exchange_sems(n):
    return [pltpu.SemaphoreType.DMA((3 * n,)), pltpu.SemaphoreType.DMA((3 * n,))]


def _chip_exchange(ps):
    n = len(ps)

    def body(*refs):
        begin, end = _chip_exchange_protocol(refs[:n], refs[n:2 * n], *refs[2 * n:])
        begin()
        end()

    return pl.pallas_call(
        body, name="grad_chip_exchange", in_specs=_hbm_specs(n), out_specs=_hbm_specs(n),
        out_shape=_chip_exchange_shapes(ps), scratch_shapes=_chip_exchange_sems(n),
    )(*ps)


def _sum_partials(p, got, chip_idx, tag):
    nsh, half, cols = got.shape
    rb = _pick_rows(half)

    def body(me_ref, p_ref, got_ref, o_ref):
        acc = p_ref[0].astype(F32)
        for s in range(nsh):
            acc = acc + got_ref[s].astype(F32)
        o_ref[...] = acc

    grid_spec = pltpu.PrefetchScalarGridSpec(
        num_scalar_prefetch=1, grid=(half // rb,),
        in_specs=[pl.BlockSpec((1, rb, cols), lambda i, me_ref: (me_ref[0], i, 0)),
                  pl.BlockSpec((nsh, rb, cols), lambda i, me_ref: (0, i, 0))],
        out_specs=pl.BlockSpec((rb, cols), lambda i, me_ref: (i, 0)))
    return pl.pallas_call(
        body, name="grad_sum_chips_" + tag, grid_spec=grid_spec,
        out_shape=jax.ShapeDtypeStruct((half, cols), F32),
        compiler_params=_params(("parallel",)),
    )(chip_idx, p, got)


def _pair_share(rs):
    n = len(rs)

    def body(*refs):
        ins, outs, (send_sems, recv_sems) = refs[:n], refs[n:2 * n], refs[2 * n:]
        x, y, c = _place()
        cps = []
        for a in range(n):
            cp = pltpu.make_async_remote_copy(src_ref=ins[a], dst_ref=outs[a], send_sem=send_sems.at[a],
                                              recv_sem=recv_sems.at[a], device_id=(x, y, 1 - c),
                                              device_id_type=MESH)
            cp.start()
            cps.append(cp)
        for cp in cps:
            cp.wait()

    return pl.pallas_call(
        body, name="grad_pair_share", in_specs=_hbm_specs(n), out_specs=_hbm_specs(n),
        out_shape=[jax.ShapeDtypeStruct(r.shape, r.dtype) for r in rs],
        scratch_shapes=[pltpu.SemaphoreType.DMA((n,)), pltpu.SemaphoreType.DMA((n,))],
    )(*rs)


def _small_allreduce(v):
    rows, cols = v.shape
    ndev = 8

    def body(in_ref, out_ref, slots, send_sems, recv_sems):
        x, y, c = _place()
        me = 4 * x + 2 * y + c
        slots[me] = in_ref[...]
        sends = []
        for k in range(1, ndev):
            peer = (x ^ (k >> 2), y ^ ((k >> 1) & 1), c ^ (k & 1))
            cp = pltpu.make_async_remote_copy(src_ref=in_ref, dst_ref=slots.at[me], send_sem=send_sems.at[k - 1],
                                              recv_sem=recv_sems.at[k - 1], device_id=peer, device_id_type=MESH)
            cp.start()
            sends.append(cp)
        for k in range(1, ndev):
            there = slots.at[me ^ k]
            pltpu.make_async_remote_copy(src_ref=there, dst_ref=there, send_sem=send_sems.at[k - 1],
                                         recv_sem=recv_sems.at[k - 1], device_id=(x, y, c),
                                         device_id_type=MESH).wait_recv()
        for cp in sends:
            cp.wait_send()
        acc = slots[0]
        for s in range(1, ndev):
            acc = acc + slots[s]
        out_ref[...] = acc

    return pl.pallas_call(
        body, name="small_allreduce",
        in_specs=[pl.BlockSpec(memory_space=pltpu.VMEM)],
        out_specs=pl.BlockSpec(memory_space=pltpu.VMEM),
        out_shape=jax.ShapeDtypeStruct((rows, cols), F32),
        scratch_shapes=[pltpu.VMEM((ndev, rows, cols), F32), pltpu.SemaphoreType.DMA((ndev - 1,)),
                        pltpu.SemaphoreType.DMA((ndev - 1,))],
    )(v)


def _adamw(w, g, m, v, name):
    r, c = w.shape
    rb = r if r <= 128 else _pick_rows_8(r, 128)
    c1 = 1.0 - ADAM_B1 ** ADAM_STEP
    c2 = 1.0 - ADAM_B2 ** ADAM_STEP

    def body(w_ref, g_ref, m_ref, v_ref, d_ref, nm_ref, nv_ref):
        gg = g_ref[...]
        nm = ADAM_B1 * m_ref[...] + (1.0 - ADAM_B1) * gg
        nv = ADAM_B2 * v_ref[...] + (1.0 - ADAM_B2) * (gg * gg)
        d_ref[...] = -ADAM_LR * ((nm / c1) / (jnp.sqrt(nv / c2) + ADAM_EPS) + ADAM_WD * w_ref[...])
        nm_ref[...] = nm
        nv_ref[...] = nv

    blk = pl.BlockSpec((rb, c), lambda i: (i, 0))
    shp = jax.ShapeDtypeStruct((r, c), F32)
    return pl.pallas_call(
        body, name=name, grid=(r // rb,), in_specs=[blk] * 4, out_specs=[blk] * 3, out_shape=[shp] * 3,
        compiler_params=_params(("parallel",)),
    )(w, g, m, v)


def _pick_rows_8(n, target):
    best = n
    for b in range(8, min(n, target) + 1, 8):
        if n % b == 0:
            best = b
    return best


W_IN_COLS = 2308
W_UP_COLS = 1408
W_DOWN_ROWS = 704
DN_CONV_COLS = 768
FFN_CONV_COLS = 1408
PROJ_ROWS = 256
ROW_TILE = 16
ROW_SEGS = [("wp_dn", PROJ_ROWS), ("wp_sb", PROJ_ROWS), ("w_out", PROJ_ROWS), ("w_down", W_DOWN_ROWS),
            ("dn_conv", ROW_TILE), ("ffn_conv", ROW_TILE), ("spare", 2 * ROW_TILE)]
ROW_OFFS = {nm: (sum(n for _, n in ROW_SEGS[:i]), n) for i, (nm, n) in enumerate(ROW_SEGS)}
STACK_ROWS = sum(n for _, n in ROW_SEGS)
assert all(n % ROW_TILE == 0 for _, n in ROW_SEGS) and STACK_ROWS % (4 * ROW_TILE) == 0
Q_END, A_END, G_END, S_END = 3 * D_MODEL, 3 * D_MODEL + 2 * N_HEADS, 4 * D_MODEL + 2 * N_HEADS, 7 * D_MODEL + 2 * N_HEADS


def _flat_rows(a, nrows):
    flat = a.reshape(-1)
    return jnp.pad(flat, (0, nrows * D_MODEL - flat.shape[0])).reshape(nrows, D_MODEL)


IN_EXTRA_ROWS = 64


def _weight_wire(w_in, wp_dn, wp_sb, w_out, w_up, w_down, dn_conv, ffn_conv):
    bits = lax.bitcast_convert_type(dn_conv, BF16).reshape(-1)
    extra = jnp.pad(bits, (0, IN_EXTRA_ROWS * W_IN_COLS - bits.shape[0])).reshape(IN_EXTRA_ROWS, W_IN_COLS)
    stack = jnp.concatenate([wp_dn.astype(BF16), wp_sb.astype(BF16), w_out.astype(BF16), w_down.astype(BF16),
                             jnp.zeros((ROW_TILE, D_MODEL), BF16),
                             _flat_rows(lax.bitcast_convert_type(ffn_conv, BF16), ROW_TILE),
                             jnp.zeros((ROW_OFFS["spare"][1], D_MODEL), BF16)], axis=0)
    return [jnp.concatenate([w_in.astype(BF16), extra], axis=0)], [w_up.astype(BF16), stack]


def _col_range(g, lo, hi, width):
    parts = []
    for s in range(g.shape[0]):
        a, b = max(lo, s * width), min(hi, (s + 1) * width)
        if a < b:
            parts.append(g[s][:, a - s * width:b - s * width])
    return parts[0] if len(parts) == 1 else jnp.concatenate(parts, axis=1)


def _f32_rows(raw, k, ncols):
    raw = raw.reshape(N_CHIPS, -1)[:, :2 * k * ncols].reshape(N_CHIPS, k * ncols, 2)
    vals = lax.bitcast_convert_type(raw, F32).reshape(N_CHIPS, k, ncols)
    return vals.transpose(1, 0, 2).reshape(k, N_CHIPS * ncols)


def _unpack_early(g_in):
    w = g_in[:, :D_MODEL, :]
    return {
        "w_dnqkv": _col_range(w, 0, Q_END, W_IN_COLS),
        "w_ab": jnp.pad(_col_range(w, Q_END, A_END, W_IN_COLS), ((0, 0), (0, LANES - 2 * N_HEADS))),
        "w_dngate": _col_range(w, A_END, G_END, W_IN_COLS),
        "w_sbqkv": _col_range(w, G_END, S_END, W_IN_COLS),
        "w_gl": _col_range(w, S_END, N_CHIPS * W_IN_COLS, W_IN_COLS),
        "dn_conv": _f32_rows(g_in[:, D_MODEL:, :], DN_CONV, DN_CONV_COLS),
    }


def _unpack_late(g_up, g_stack):
    def seg(nm):
        at, n = ROW_OFFS[nm]
        return g_stack[:, at:at + n, :]

    ffn_conv = _f32_rows(seg("ffn_conv"), FFN_CONV, FFN_CONV_COLS)
    return {
        "wp_dn": seg("wp_dn").reshape(D_MODEL, D_MODEL),
        "wp_sb": seg("wp_sb").reshape(D_MODEL, D_MODEL),
        "w_out": seg("w_out").reshape(D_MODEL, D_MODEL),
        "w_up_g": _col_range(g_up, 0, D_FF, W_UP_COLS), "w_up_u": _col_range(g_up, D_FF, 2 * D_FF, W_UP_COLS),
        "w_down": seg("w_down").reshape(D_FF, D_MODEL),
        "ffn_conv_g": ffn_conv[:, :D_FF], "ffn_conv_u": ffn_conv[:, D_FF:],
    }


def _grad_wire_early(gr):
    def cols(a, ncols):
        return a.reshape(a.shape[0], N_CHIPS, ncols).transpose(1, 0, 2)

    def rows(a, nrows):
        return a.astype(BF16).reshape(N_CHIPS, nrows, a.shape[1])

    def flat(a, nrows):
        a = a.astype(BF16).reshape(N_CHIPS, -1)
        return jnp.pad(a, ((0, 0), (0, nrows * D_MODEL - a.shape[1]))).reshape(N_CHIPS, nrows, D_MODEL)

    up = [gr["w_up_g"], gr["w_up_u"]]
    g_up = jnp.stack([up[s // 2][:, (s % 2) * W_UP_COLS:(s % 2 + 1) * W_UP_COLS].astype(BF16) for s in range(N_CHIPS)])
    g_stack = jnp.concatenate([rows(gr["wp_dn"], PROJ_ROWS), rows(gr["wp_sb"], PROJ_ROWS), rows(gr["w_out"], PROJ_ROWS),
                               rows(gr["w_down"], W_DOWN_ROWS), jnp.zeros((N_CHIPS, ROW_TILE, D_MODEL), BF16),
                               flat(cols(gr["ffn_conv"], FFN_CONV_COLS), ROW_TILE),
                               jnp.zeros((N_CHIPS, ROW_OFFS["spare"][1], D_MODEL), BF16)], axis=1)
    return [g_up, g_stack]


def _grad_wire_late(gr):
    pieces = [(gr["w_dnqkv"], 0), (gr["w_ab"][:, :2 * N_HEADS], Q_END), (gr["w_dngate"], A_END),
              (gr["w_sbqkv"], G_END), (gr["w_gl"], S_END)]
    conv = gr["dn_conv"].reshape(DN_CONV, N_CHIPS, DN_CONV_COLS).transpose(1, 0, 2).reshape(N_CHIPS, -1)

    def block(s):
        lo, hi = s * W_IN_COLS, (s + 1) * W_IN_COLS
        parts = []
        for a, at in pieces:
            b0, b1 = max(lo, at), min(hi, at + a.shape[1])
            if b0 < b1:
                parts.append(a[:, b0 - at:b1 - at].astype(BF16))
        w = parts[0] if len(parts) == 1 else jnp.concatenate(parts, axis=1)
        extra = jnp.pad(conv[s].astype(BF16), (0, IN_EXTRA_ROWS * W_IN_COLS - conv.shape[1]))
        return jnp.concatenate([w, extra.reshape(IN_EXTRA_ROWS, W_IN_COLS)], axis=0)

    return [jnp.stack([block(s) for s in range(N_CHIPS)])]


def _unpack_grad_shard(r_in, r_up, r_stack):
    def seg(nm):
        at, n = ROW_OFFS[nm]
        return r_stack[at:at + n, :]

    return {
        "w_in": r_in[:D_MODEL], "w_up": r_up,
        "wp_dn": seg("wp_dn"), "wp_sb": seg("wp_sb"), "w_out": seg("w_out"), "w_down": seg("w_down"),
        "dn_conv": r_in[D_MODEL:].reshape(-1)[:DN_CONV * DN_CONV_COLS].reshape(DN_CONV, DN_CONV_COLS),
        "ffn_conv": seg("ffn_conv").reshape(-1)[:FFN_CONV * FFN_CONV_COLS].reshape(FFN_CONV, FFN_CONV_COLS),
    }


def _lane_row(v):
    return jnp.pad(v.reshape(1, -1), ((0, 0), (0, LANES - v.size)))


def kernel(x, norm1_w, w_in, dn_conv_w, dn_A_log, dn_dt_bias, dn_norm_w, w_proj_dn, w_proj_sb, w_out, norm2_w, ffn_w_up, ffn_conv_w, ffn_w_down, norm_f_w, loss_target, m_norm1_w, m_w_in, m_dn_conv_w, m_dn_A_log, m_dn_dt_bias, m_dn_norm_w, m_w_proj_dn, m_w_proj_sb, m_w_out, m_norm2_w, m_ffn_w_up, m_ffn_conv_w, m_ffn_w_down, m_norm_f_w, v_norm1_w, v_w_in, v_dn_conv_w, v_dn_A_log, v_dn_dt_bias, v_dn_norm_w, v_w_proj_dn, v_w_proj_sb, v_w_out, v_norm2_w, v_ffn_w_up, v_ffn_conv_w, v_ffn_w_down, v_norm_f_w):
    early, late = _weight_wire(w_in[0], w_proj_dn[0], w_proj_sb[0], w_out[0], ffn_w_up[0], ffn_w_down[0],
                               dn_conv_w[0], ffn_conv_w[0])
    chip_idx = (2 * lax.axis_index("x") + lax.axis_index("y")).astype(jnp.int32)

    def with_mine(gathered, wire):
        return [lax.dynamic_update_slice(g, mine[None], (chip_idx, 0, 0)) for g, mine in zip(gathered, wire)]

    wts = _unpack_early(*with_mine(_gather_shards(early), early))
    wts.update(norm1=norm1_w, norm2=norm2_w, normf=norm_f_w.reshape(1, D_MODEL), dn_norm=dn_norm_w,
               alog=_lane_row(dn_A_log), dtb=_lane_row(dn_dt_bias))

    c_idx = lax.axis_index("c").astype(jnp.int32).reshape(1)

    def pair_sums(wire_g, tags, when):
        return [_pair_add(g, got, c_idx, tag) for g, got, tag in zip(wire_g, _pair_exchange_halves(wire_g, when), tags)]

    loss_part, grad_x, gr, (early_sums, early_arrived) = _local_step(
        x[0], loss_target[0], wts, late, lambda gathered: _unpack_late(*with_mine(gathered, late)),
        lambda grads: pair_sums(_grad_wire_early(grads), ["w_up", "rows"], "early"))

    late_sums = pair_sums(_grad_wire_late(gr), ["w_in"], "late")
    tags = ["w_in", "w_up", "rows"]
    reduced = [_sum_partials(p, got, chip_idx.reshape(1), tag)
               for p, got, tag in zip(late_sums + early_sums, list(_chip_exchange(late_sums)) + list(early_arrived), tags)]
    is_south = lax.axis_index("c") == 0
    gsh = _unpack_grad_shard(*[jnp.concatenate([jnp.where(is_south, mine, other), jnp.where(is_south, other, mine)],
                                               axis=0) for mine, other in zip(reduced, _pair_share(reduced))])

    tail = jnp.concatenate([gr["dn_norm"], gr["alog"][:, :N_HEADS], gr["dtb"][:, :N_HEADS], loss_part[:, :1]], axis=1)
    small = jnp.concatenate([gr["norm1"], gr["norm2"], gr["normf"],
                             jnp.pad(tail, ((0, 0), (0, D_MODEL - tail.shape[1]))),
                             jnp.zeros((SMALL_ROWS - 4, D_MODEL), F32)], axis=0)
    small = _small_allreduce(small)
    at = HEAD_DIM
    g_small = {"norm1_w": small[0:1], "norm2_w": small[1:2], "norm_f_w": small[2],
               "dn_norm_w": small[3:4, :at], "dn_A_log": small[3:4, at:at + N_HEADS],
               "dn_dt_bias": small[3:4, at + N_HEADS:at + 2 * N_HEADS]}
    loss = small[3, at + 2 * N_HEADS]

    big = {"w_in": (w_in, m_w_in, v_w_in, gsh["w_in"]), "dn_conv_w": (dn_conv_w, m_dn_conv_w, v_dn_conv_w, gsh["dn_conv"]),
           "w_proj_dn": (w_proj_dn, m_w_proj_dn, v_w_proj_dn, gsh["wp_dn"]),
           "w_proj_sb": (w_proj_sb, m_w_proj_sb, v_w_proj_sb, gsh["wp_sb"]),
           "w_out": (w_out, m_w_out, v_w_out, gsh["w_out"]),
           "ffn_w_up": (ffn_w_up, m_ffn_w_up, v_ffn_w_up, gsh["w_up"]),
           "ffn_conv_w": (ffn_conv_w, m_ffn_conv_w, v_ffn_conv_w, gsh["ffn_conv"]),
           "ffn_w_down": (ffn_w_down, m_ffn_w_down, v_ffn_w_down, gsh["w_down"])}
    res = {}
    for nm, (w, m, v, g) in big.items():
        d, nm_, nv_ = _adamw(w[0], g, m[0], v[0], "adamw_" + nm)
        res[nm] = (g[None], d[None], nm_[None], nv_[None])

    names = ["norm1_w", "norm2_w", "norm_f_w", "dn_norm_w", "dn_A_log", "dn_dt_bias"]
    given = {"norm1_w": (norm1_w, m_norm1_w, v_norm1_w), "norm2_w": (norm2_w, m_norm2_w, v_norm2_w),
             "norm_f_w": (norm_f_w, m_norm_f_w, v_norm_f_w), "dn_norm_w": (dn_norm_w, m_dn_norm_w, v_dn_norm_w),
             "dn_A_log": (dn_A_log, m_dn_A_log, v_dn_A_log), "dn_dt_bias": (dn_dt_bias, m_dn_dt_bias, v_dn_dt_bias)}

    def stack(k, fill):
        rows = [jnp.pad(given[nm][k].reshape(1, -1), ((0, 0), (0, D_MODEL - given[nm][k].size)),
                        constant_values=fill) for nm in names]
        return jnp.concatenate(rows + [jnp.full((SMALL_ROWS - len(names), D_MODEL), fill, F32)], axis=0)

    g_rows = jnp.concatenate(
        [jnp.pad(g_small[nm].reshape(1, -1), ((0, 0), (0, D_MODEL - g_small[nm].size))) for nm in names]
        + [jnp.zeros((SMALL_ROWS - len(names), D_MODEL), F32)], axis=0)
    d_s, m_s, v_s = _adamw(stack(0, 0.0), g_rows, stack(1, 0.0), stack(2, 1.0), "adamw_small")
    for r, nm in enumerate(names):
        shape = given[nm][0].shape
        n = given[nm][0].size
        res[nm] = (g_small[nm].reshape(shape), d_s[r, :n].reshape(shape), m_s[r, :n].reshape(shape),
                   v_s[r, :n].reshape(shape))

    order = ["norm1_w", "w_in", "dn_conv_w", "dn_A_log", "dn_dt_bias", "dn_norm_w", "w_proj_dn", "w_proj_sb",
             "w_out", "norm2_w", "ffn_w_up", "ffn_conv_w", "ffn_w_down", "norm_f_w"]
    outs = [loss, grad_x[None]]
    for k in range(4):
        outs += [res[nm][k] for nm in order]
    return tuple(outs)
```

```python
import functools

import jax
import jax.numpy as jnp
from jax import lax
from jax.experimental import pallas as pl
from jax.experimental.pallas import tpu as pltpu

F32 = jnp.float32
BF16 = jnp.bfloat16
MESH = pl.DeviceIdType.MESH

EPS = 1e-6
D_MODEL = 1024
N_HEADS = 8
HEAD_DIM = 128
DN_CONV = 4
DN_CHUNK = 64
D_FF = 2816
FFN_CONV = 3
ADAM_LR, ADAM_B1, ADAM_B2, ADAM_EPS, ADAM_WD, ADAM_STEP = 0.001, 0.9, 0.999, 1e-08, 0.01, 10

N_CHIPS = 4
LANES = 128
HALO = 8
VMEM_LIMIT = 48 * 1024 * 1024
SMALL_ROWS = 8


def _params(sem=None):
    return pltpu.CompilerParams(dimension_semantics=sem, vmem_limit_bytes=VMEM_LIMIT)


def _pick(n, target):
    best = None
    for b in range(LANES, min(n, target) + 1, LANES):
        if n % b == 0:
            best = b
    return best or n


ELEMENTWISE_COLS = 1408


def _rows(t, target=256):
    return min(t, target)


def _dot(a, b, precision=None):
    return lax.dot_general(a, b, (((1,), (0,)), ((), ())), precision=precision, preferred_element_type=F32)


def _dot_nt(a, b, precision=None):
    return lax.dot_general(a, b, (((1,), (1,)), ((), ())), precision=precision, preferred_element_type=F32)


def _dot_tn(a, b, precision=None):
    return lax.dot_general(a, b, (((0,), (0,)), ((), ())), precision=precision, preferred_element_type=F32)


def _rms(x, w):
    return x * lax.rsqrt(jnp.mean(x * x, axis=-1, keepdims=True) + EPS) * w


def _silu(x):
    return x * jax.nn.sigmoid(x)


def _softplus(x):
    return jnp.maximum(x, 0.0) + jnp.log(1.0 + jnp.exp(-jnp.abs(x)))


MM_BLOCK = 1408
MM_VMEM_BUDGET = 38 * 1024 * 1024


def _mm(a, b, *, ta=False, tb=False, add=None, out_dtype=F32, name, bm=MM_BLOCK, bn=MM_BLOCK, bk=MM_BLOCK):
    m = a.shape[1] if ta else a.shape[0]
    k = a.shape[0] if ta else a.shape[1]
    n = b.shape[0] if tb else b.shape[1]
    bm, bn = _pick(m, bm), _pick(n, bn)

    def vmem_need(bk_):
        need = 2 * (bm * bk_ * a.dtype.itemsize + bk_ * bn * b.dtype.itemsize) + 2 * bm * bn * jnp.dtype(out_dtype).itemsize
        need += 2 * bm * bn * add.dtype.itemsize if add is not None else 0
        return need + (bm * bn * 4 if bk_ < k else 0)

    bk = max((d for d in range(LANES, k + 1, LANES) if k % d == 0 and vmem_need(d) <= MM_VMEM_BUDGET),
             default=_pick(k, bk))
    nk = k // bk
    dims = (((0 if ta else 1,), (1 if tb else 0,)), ((), ()))

    def body(*refs):
        a_ref, b_ref = refs[:2]
        c_ref = refs[2] if add is not None else None
        o_ref = refs[3] if add is not None else refs[2]
        acc = refs[-1]
        kk = pl.program_id(2)
        part = lax.dot_general(a_ref[...].astype(BF16), b_ref[...].astype(BF16), dims, preferred_element_type=F32)

        def finish(r):
            if add is not None:
                r = r + c_ref[...].astype(F32)
            o_ref[...] = r.astype(out_dtype)

        if nk == 1:
            finish(part)
            return

        @pl.when(kk == 0)
        def _():
            acc[...] = part

        @pl.when(jnp.logical_and(kk > 0, kk < nk - 1))
        def _():
            acc[...] += part

        @pl.when(kk == nk - 1)
        def _():
            finish(acc[...] + part)

    a_spec = (pl.BlockSpec((bk, bm), lambda i, j, kk: (kk, i)) if ta
              else pl.BlockSpec((bm, bk), lambda i, j, kk: (i, kk)))
    b_spec = (pl.BlockSpec((bn, bk), lambda i, j, kk: (j, kk)) if tb
              else pl.BlockSpec((bk, bn), lambda i, j, kk: (kk, j)))
    o_spec = pl.BlockSpec((bm, bn), lambda i, j, kk: (i, j))
    in_specs = [a_spec, b_spec] + ([o_spec] if add is not None else [])
    args = (a, b) + ((add,) if add is not None else ())
    return pl.pallas_call(
        body, name=name, grid=(m // bm, n // bn, nk),
        in_specs=in_specs, out_specs=o_spec,
        out_shape=jax.ShapeDtypeStruct((m, n), out_dtype),
        scratch_shapes=[pltpu.VMEM((bm, bn), F32)] if nk > 1 else [],
        compiler_params=_params(("parallel", "parallel", "arbitrary")),
    )(*args)


def _norm1_fwd(x, w, w_ab):
    t = x.shape[0]
    tb = _rows(t)

    def body(x_ref, w_ref, wab_ref, n_ref, hab_ref):
        n = _rms(x_ref[...], w_ref[...]).astype(BF16)
        n_ref[...] = n
        hab_ref[...] = _dot(n, wab_ref[...])

    return pl.pallas_call(
        body, name="norm1_fwd", grid=(t // tb,),
        in_specs=[pl.BlockSpec((tb, D_MODEL), lambda i: (i, 0)),
                  pl.BlockSpec((1, D_MODEL), lambda i: (0, 0)),
                  pl.BlockSpec((D_MODEL, LANES), lambda i: (0, 0))],
        out_specs=[pl.BlockSpec((tb, D_MODEL), lambda i: (i, 0)),
                   pl.BlockSpec((tb, LANES), lambda i: (i, 0))],
        out_shape=[jax.ShapeDtypeStruct((t, D_MODEL), BF16), jax.ShapeDtypeStruct((t, LANES), F32)],
        compiler_params=_params(("arbitrary",)),
    )(x, w, w_ab)


def _norm1_bwd(x, w, dn, dres, dab, w_ab):
    t = x.shape[0]
    tb = _rows(t)

    def body(x_ref, w_ref, dn_ref, dres_ref, dab_ref, wab_ref, dx_ref, dw_ref):
        i = pl.program_id(0)
        g = dn_ref[...] + _dot_nt(dab_ref[...].astype(BF16), wab_ref[...])
        _, vjp = jax.vjp(_rms, x_ref[...], w_ref[...])
        dx, dw = vjp(g)
        dx_ref[...] = dres_ref[...] + dx

        @pl.when(i == 0)
        def _():
            dw_ref[...] = jnp.zeros_like(dw_ref)

        dw_ref[...] += dw

    row = pl.BlockSpec((tb, D_MODEL), lambda i: (i, 0))
    vec = pl.BlockSpec((1, D_MODEL), lambda i: (0, 0))
    return pl.pallas_call(
        body, name="norm1_bwd", grid=(t // tb,),
        in_specs=[row, vec, row, row, pl.BlockSpec((tb, LANES), lambda i: (i, 0)),
                  pl.BlockSpec((D_MODEL, LANES), lambda i: (0, 0))],
        out_specs=[row, vec],
        out_shape=[jax.ShapeDtypeStruct((t, D_MODEL), F32), jax.ShapeDtypeStruct((1, D_MODEL), F32)],
        compiler_params=_params(("arbitrary",)),
    )(x, w, dn, dres, dab, w_ab)


def _conv_fwd(x, w, name):
    t, c = x.shape
    kk = w.shape[0]
    tb, cb = _rows(t, 512), _pick(c, ELEMENTWISE_COLS)
    per = tb // HALO

    def body(x_ref, halo_ref, w_ref, y_ref, buf):
        i = pl.program_id(0)
        buf[pl.ds(HALO, tb), :] = x_ref[...]
        buf[pl.ds(0, HALO), :] = jnp.where(i == 0, 0.0, halo_ref[...])
        y_ref[...] = _conv_taps(buf, w_ref, HALO - (kk - 1), tb)

    return pl.pallas_call(
        body, name=name, grid=(t // tb, c // cb),
        in_specs=[pl.BlockSpec((tb, cb), lambda i, j: (i, j)),
                  pl.BlockSpec((HALO, cb), lambda i, j: (jnp.maximum(i * per - 1, 0), j)),
                  pl.BlockSpec((kk, cb), lambda i, j: (0, j))],
        out_specs=pl.BlockSpec((tb, cb), lambda i, j: (i, j)),
        out_shape=jax.ShapeDtypeStruct((t, c), F32),
        scratch_shapes=[pltpu.VMEM((tb + HALO, cb), F32)],
        compiler_params=_params(("parallel", "parallel")),
    )(x, x, w)


def _conv_bwd(dy, x, w, name, dx_dtype):
    t, c = x.shape
    kk = w.shape[0]
    tb, cb = _rows(t, 512), _pick(c, ELEMENTWISE_COLS)
    per = tb // HALO
    nblk = t // tb

    def body(dy_ref, after_ref, x_ref, w_ref, dx_ref, dw_ref, dbuf):
        i = pl.program_id(1)
        dbuf[pl.ds(0, tb), :] = dy_ref[...]
        dbuf[pl.ds(tb, HALO), :] = jnp.where(i == nblk - 1, 0.0, after_ref[...])

        @pl.when(i == 0)
        def _():
            dw_ref[...] = jnp.zeros_like(dw_ref)

        for j in range(cb // LANES):
            sl = pl.ds(j * LANES, LANES)
            x = x_ref[:, sl]
            dx = None
            for s in range(kk):
                shifted = dbuf[pl.ds(kk - 1 - s, tb), sl]
                term = w_ref[s:s + 1, sl] * shifted
                dx = term if dx is None else dx + term
                dw_ref[s:s + 1, sl] += jnp.sum(shifted * x, axis=0, keepdims=True)
            dx_ref[:, sl] = dx.astype(dx_dtype)

    blk = pl.BlockSpec((tb, cb), lambda j, i: (i, j))
    return pl.pallas_call(
        body, name=name, grid=(c // cb, nblk),
        in_specs=[blk,
                  pl.BlockSpec((HALO, cb), lambda j, i: (jnp.minimum((i + 1) * per, t // HALO - 1), j)),
                  blk,
                  pl.BlockSpec((kk, cb), lambda j, i: (0, j))],
        out_specs=[blk, pl.BlockSpec((HALO, cb), lambda j, i: (0, j))],
        out_shape=[jax.ShapeDtypeStruct((t, c), dx_dtype), jax.ShapeDtypeStruct((HALO, c), F32)],
        scratch_shapes=[pltpu.VMEM((tb + HALO, cb), F32)],
        compiler_params=_params(("parallel", "arbitrary")),
    )(dy, dy, x, w)


def _dn_head(c, normed):
    s = _silu(c)
    return s * lax.rsqrt(jnp.sum(s * s, axis=-1, keepdims=True) + EPS) if normed else s


def _dn_gates(hab, alog, dtb):
    lane = lax.broadcasted_iota(jnp.int32, hab.shape, 1)
    g = -jnp.exp(alog) * _softplus(hab + dtb)
    beta = jax.nn.sigmoid(hab)
    return jnp.where(lane < N_HEADS, g, jnp.where(lane < 2 * N_HEADS, beta, 0.0))


def _dn_head_slices(q_ref, k_ref, v_ref):
    return [(pl.ds((part * N_HEADS + h) * HEAD_DIM, HEAD_DIM), ref, h, part < 2)
            for part, ref in enumerate((q_ref, k_ref, v_ref)) for h in range(N_HEADS)]


def _dn_prep_fwd(c, hab, alog, dtb):
    t = c.shape[0]
    tb = _rows(t)

    def body(c_ref, hab_ref, alog_ref, dtb_ref, q_ref, k_ref, v_ref, gb_ref):
        for sl, ref, h, normed in _dn_head_slices(q_ref, k_ref, v_ref):
            ref[h] = _dn_head(c_ref[:, sl], normed)
        gb_ref[...] = _dn_gates(hab_ref[...], alog_ref[...], dtb_ref[...])

    hm = pl.BlockSpec((N_HEADS, tb, HEAD_DIM), lambda i: (0, i, 0))
    nar = pl.BlockSpec((tb, LANES), lambda i: (i, 0))
    vec = pl.BlockSpec((1, LANES), lambda i: (0, 0))
    return pl.pallas_call(
        body, name="dn_prep_fwd", grid=(t // tb,),
        in_specs=[pl.BlockSpec((tb, 3 * D_MODEL), lambda i: (i, 0)), nar, vec, vec],
        out_specs=[hm, hm, hm, nar],
        out_shape=[jax.ShapeDtypeStruct((N_HEADS, t, HEAD_DIM), F32)] * 3 + [jax.ShapeDtypeStruct((t, LANES), F32)],
        compiler_params=_params(("parallel",)),
    )(c, hab, alog, dtb)


def _dn_prep_bwd(c, hab, alog, dtb, dq, dk, dv, dgb):
    t = c.shape[0]
    tb = _rows(t)

    def body(c_ref, hab_ref, alog_ref, dtb_ref, dq_ref, dk_ref, dv_ref, dgb_ref,
             dc_ref, dhab_ref, dalog_ref, ddtb_ref):
        i = pl.program_id(0)
        for sl, ref, h, normed in _dn_head_slices(dq_ref, dk_ref, dv_ref):
            _, vjp = jax.vjp(functools.partial(_dn_head, normed=normed), c_ref[:, sl])
            dc_ref[:, sl] = vjp(ref[h])[0]
        _, vjp = jax.vjp(_dn_gates, hab_ref[...], alog_ref[...], dtb_ref[...])
        dhab, dalog, ddtb = vjp(dgb_ref[...])
        dhab_ref[...] = dhab

        @pl.when(i == 0)
        def _():
            dalog_ref[...] = jnp.zeros_like(dalog_ref)
            ddtb_ref[...] = jnp.zeros_like(ddtb_ref)

        dalog_ref[...] += dalog
        ddtb_ref[...] += ddtb

    hm = pl.BlockSpec((N_HEADS, tb, HEAD_DIM), lambda i: (0, i, 0))
    wide = pl.BlockSpec((tb, 3 * D_MODEL), lambda i: (i, 0))
    nar = pl.BlockSpec((tb, LANES), lambda i: (i, 0))
    vec = pl.BlockSpec((1, LANES), lambda i: (0, 0))
    return pl.pallas_call(
        body, name="dn_prep_bwd", grid=(t // tb,),
        in_specs=[wide, nar, vec, vec, hm, hm, hm, nar],
        out_specs=[wide, nar, vec, vec],
        out_shape=[jax.ShapeDtypeStruct((t, 3 * D_MODEL), F32), jax.ShapeDtypeStruct((t, LANES), F32),
                   jax.ShapeDtypeStruct((1, LANES), F32), jax.ShapeDtypeStruct((1, LANES), F32)],
        compiler_params=_params(("arbitrary",)),
    )(c, hab, alog, dtb, dq, dk, dv, dgb)


DN_PREC = lax.Precision.HIGH
DN_GROUP = 8


def _dn_prec(a):
    return DN_PREC if a.dtype == F32 else None


def _bdot(a, b):
    return lax.dot_general(a, b, (((2,), (1,)), ((0,), (0,))), precision=_dn_prec(a), preferred_element_type=F32)


def _bdot_nt(a, b):
    return lax.dot_general(a, b, (((2,), (2,)), ((0,), (0,))), precision=_dn_prec(a), preferred_element_type=F32)


def _bdot_tn(a, b):
    return lax.dot_general(a, b, (((1,), (1,)), ((0,), (0,))), precision=_dn_prec(a), preferred_element_type=F32)


def _unit_lower_inverse(lmat):
    c = lmat.shape[-1]
    ri = lax.broadcasted_iota(jnp.int32, (c, c), 0)
    ci = lax.broadcasted_iota(jnp.int32, (c, c), 1)
    p = -lmat
    tinv = jnp.where(ri == ci, 1.0, 0.0) + p
    for _ in range(max(c.bit_length() - 2, 0)):
        p = _bdot(p, p)
        tinv = tinv + _bdot(tinv, p)
    return tinv


@jax.custom_vjp
def _solve_with(lmat, rhs, tinv):
    return _bdot(tinv, rhs)


def _solve_with_fwd(lmat, rhs, tinv):
    sol = _bdot(tinv, rhs)
    return sol, (sol, tinv)


def _solve_with_bwd(res, dsol):
    sol, tinv = res
    drhs = _bdot_tn(tinv, dsol)
    return -_bdot_nt(drhs, sol), drhs, jnp.zeros_like(tinv)


_solve_with.defvjp(_solve_with_fwd, _solve_with_bwd)


def _dn_local(q, k, v, grow, brow, tinv):
    g, c, _ = q.shape
    ri = lax.broadcasted_iota(jnp.int32, (c, c), 0)
    ci = lax.broadcasted_iota(jnp.int32, (c, c), 1)
    lower = ri >= ci
    as_col = lambda r: jnp.sum(jnp.where(ri == ci, jnp.broadcast_to(r, (g, c, c)), 0.0), axis=2, keepdims=True)
    gcol, bcol = as_col(grow), as_col(brow)
    gc_col = jnp.sum(jnp.where(lower, jnp.broadcast_to(grow, (g, c, c)), 0.0), axis=2, keepdims=True)
    gc_row = jnp.sum(jnp.where(ri <= ci, jnp.broadcast_to(gcol, (g, c, c)), 0.0), axis=1, keepdims=True)
    qs = q * (HEAD_DIM ** -0.5)
    kb = k * bcol
    vb = v * bcol
    decay = jnp.where(lower, jnp.exp(jnp.where(lower, gc_col - gc_row, 0.0)), 0.0)
    lmat = jnp.where(ri > ci, _bdot_nt(kb.astype(BF16), k.astype(BF16)) * decay, 0.0)
    eg = jnp.exp(gc_col)
    rhs = jnp.concatenate([vb, kb * eg], axis=2)
    if tinv is None:
        tinv = _unit_lower_inverse(lmat)
    sol = _solve_with(lmat, rhs, tinv)
    a_qk = jnp.where(lower, _bdot_nt(qs.astype(BF16), k.astype(BF16)) * decay, 0.0)
    g_last = jnp.sum(grow, axis=2, keepdims=True)
    kdec = k * jnp.exp(g_last - gc_col)
    egl = jnp.broadcast_to(jnp.exp(g_last), (g, 1, HEAD_DIM))
    return sol[:, :, :HEAD_DIM], sol[:, :, HEAD_DIM:], a_qk, qs * eg, kdec, egl, tinv


def _dn_seq(u, w, a_qk, qe, kdec, egl, s_in):
    b16 = lambda x: x.astype(BF16)
    v_new = u - _bdot(b16(w), b16(s_in))
    o = _bdot(b16(qe), b16(s_in)) + _bdot(b16(a_qk), b16(v_new))
    return o, s_in * egl + _bdot_tn(b16(kdec), b16(v_new))


def _dn_local_specs(t):
    grp = min(DN_GROUP, t // DN_CHUNK)
    rows = grp * DN_CHUNK
    blk = pl.BlockSpec((1, rows, HEAD_DIM), lambda h, i: (h, i, 0))
    row = pl.BlockSpec((1, grp, 1, DN_CHUNK), lambda h, i: (h, i, 0, 0))
    sq = pl.BlockSpec((1, grp, DN_CHUNK, DN_CHUNK), lambda h, i: (h, i, 0, 0))
    lane = pl.BlockSpec((1, grp, 1, HEAD_DIM), lambda h, i: (h, i, 0, 0))
    return grp, blk, row, sq, lane


def _dn_shapes(t):
    nchunk = t // DN_CHUNK
    big = jax.ShapeDtypeStruct((N_HEADS, t, HEAD_DIM), F32)
    row = jax.ShapeDtypeStruct((N_HEADS, nchunk, 1, DN_CHUNK), F32)
    sq = jax.ShapeDtypeStruct((N_HEADS, nchunk, DN_CHUNK, DN_CHUNK), F32)
    lane = jax.ShapeDtypeStruct((N_HEADS, nchunk, 1, HEAD_DIM), F32)
    return big, row, sq, lane


def _dn_local_fwd(q, k, v, grow, brow, wire=()):
    t = q.shape[1]
    grp, blk, row, sq, lane = _dn_local_specs(t)
    big, _, sqs, lanes = _dn_shapes(t)
    n = len(wire)
    groups = t // (grp * DN_CHUNK)
    steps = N_HEADS * groups

    def body(q_ref, k_ref, v_ref, gr_ref, br_ref, *rest):
        u_ref, w_ref, a_ref, qe_ref, kd_ref, egl_ref, t_ref = rest[n:n + 7]
        if n:
            begin, middle, end = _gather_protocol(rest[:n], rest[n + 7:2 * n + 7], *rest[2 * n + 7:])
            step = pl.program_id(0) * groups + pl.program_id(1)
            pl.when(step == 0)(begin)
            pl.when(step == (GATHER_FORWARD_AT * steps) // 8)(middle)
        split = lambda r: r[0].reshape(grp, DN_CHUNK, HEAD_DIM)
        u, w, a_qk, qe, kdec, egl, tinv = _dn_local(split(q_ref), split(k_ref), split(v_ref), gr_ref[0],
                                                     br_ref[0], None)
        for ref, val in ((u_ref, u), (w_ref, w), (qe_ref, qe), (kd_ref, kdec)):
            ref[0] = val.reshape(grp * DN_CHUNK, HEAD_DIM)
        a_ref[0] = a_qk
        egl_ref[0] = egl
        t_ref[0] = tinv
        if n:
            pl.when(step == steps - 1)(end)

    assert n == 0 or steps >= 3
    return pl.pallas_call(
        body, name="dn_local_fwd", grid=(N_HEADS, groups),
        in_specs=[blk, blk, blk, row, row] + _hbm_specs(n),
        out_specs=[blk, blk, sq, blk, blk, lane, sq] + _hbm_specs(n),
        out_shape=[big, big, sqs, big, big, lanes, sqs] + _gather_out_shapes(wire),
        scratch_shapes=_gather_sems(n) if n else [],
        compiler_params=_params(("arbitrary", "arbitrary")),
    )(q, k, v, grow, brow, *wire)


def _dn_local_bwd(q, k, v, grow, brow, tinv, du, dw, da, dqe, dkd, degl, partials=()):
    t = q.shape[1]
    grp, blk, row, sq, lane = _dn_local_specs(t)
    big, rows_, _, _ = _dn_shapes(t)
    n = len(partials)
    groups = t // (grp * DN_CHUNK)
    steps = N_HEADS * groups

    def body(q_ref, k_ref, v_ref, gr_ref, br_ref, t_ref, du_ref, dw_ref, da_ref, dqe_ref, dkd_ref,
             degl_ref, *rest):
        dq_ref, dk_ref, dv_ref, dgr_ref, dbr_ref = rest[n:n + 5]
        if n:
            begin, end = _chip_exchange_protocol(rest[:n], rest[n + 5:2 * n + 5], *rest[2 * n + 5:])
            step = pl.program_id(0) * groups + pl.program_id(1)
            pl.when(step == 0)(begin)
        split = lambda r: r[0].reshape(grp, DN_CHUNK, HEAD_DIM)
        tinv_v = t_ref[0]
        fn = lambda q_, k_, v_, gr_, br_: _dn_local(q_, k_, v_, gr_, br_, tinv_v)[:6]
        _, vjp = jax.vjp(fn, split(q_ref), split(k_ref), split(v_ref), gr_ref[0], br_ref[0])
        dq, dk, dv, dgr, dbr = vjp((split(du_ref), split(dw_ref), da_ref[0], split(dqe_ref), split(dkd_ref),
                                    degl_ref[0]))
        for ref, val in ((dq_ref, dq), (dk_ref, dk), (dv_ref, dv)):
            ref[0] = val.reshape(grp * DN_CHUNK, HEAD_DIM)
        dgr_ref[0] = dgr
        dbr_ref[0] = dbr
        if n:
            pl.when(step == steps - 1)(end)

    assert n == 0 or steps >= 2
    return pl.pallas_call(
        body, name="dn_local_bwd", grid=(N_HEADS, groups),
        in_specs=[blk, blk, blk, row, row, sq, blk, blk, sq, blk, blk, lane] + _hbm_specs(n),
        out_specs=[blk, blk, blk, row, row] + _hbm_specs(n),
        out_shape=[big, big, big, rows_, rows_] + _chip_exchange_shapes(partials),
        scratch_shapes=_chip_exchange_sems(n) if n else [],
        compiler_params=_params(("arbitrary", "arbitrary")),
    )(q, k, v, grow, brow, tinv, du, dw, da, dqe, dkd, degl, *partials)


DN_SEQ_CHUNKS = 4


def _dn_seq_specs(nchunk, rev):
    per = min(DN_SEQ_CHUNKS, nchunk)
    nstep = nchunk // per

    def idx(n):
        return nstep - 1 - n if rev else n

    blk = pl.BlockSpec((N_HEADS, per * DN_CHUNK, HEAD_DIM), lambda n: (0, idx(n), 0))
    sq = pl.BlockSpec((N_HEADS, per, DN_CHUNK, DN_CHUNK), lambda n: (0, idx(n), 0, 0))
    lane = pl.BlockSpec((N_HEADS, per, 1, HEAD_DIM), lambda n: (0, idx(n), 0, 0))
    st = pl.BlockSpec((N_HEADS, per, HEAD_DIM, HEAD_DIM), lambda n: (0, idx(n), 0, 0))
    return per, nstep, blk, sq, lane, st


def _dn_seq_fwd(u, w, a_qk, qe, kdec, egl):
    t = u.shape[1]
    nchunk = t // DN_CHUNK
    per, nstep, blk, sq, lane, st = _dn_seq_specs(nchunk, False)

    def body(u_ref, w_ref, a_ref, qe_ref, kd_ref, egl_ref, o_ref, s_ref, state):
        @pl.when(pl.program_id(0) == 0)
        def _():
            state[...] = jnp.zeros_like(state)

        for c in range(per):
            rows = pl.ds(c * DN_CHUNK, DN_CHUNK)
            s_in = state[...]
            s_ref[:, c] = s_in
            o_ref[:, rows], state[...] = _dn_seq(u_ref[:, rows], w_ref[:, rows], a_ref[:, c], qe_ref[:, rows],
                                                 kd_ref[:, rows], egl_ref[:, c], s_in)

    return pl.pallas_call(
        body, name="dn_seq_fwd", grid=(nstep,),
        in_specs=[blk, blk, sq, blk, blk, lane],
        out_specs=[blk, st],
        out_shape=[jax.ShapeDtypeStruct((N_HEADS, t, HEAD_DIM), F32),
                   jax.ShapeDtypeStruct((N_HEADS, nchunk, HEAD_DIM, HEAD_DIM), F32)],
        scratch_shapes=[pltpu.VMEM((N_HEADS, HEAD_DIM, HEAD_DIM), F32)],
        compiler_params=_params(("arbitrary",)),
    )(u, w, a_qk, qe, kdec, egl)


def _dn_seq_bwd(u, w, a_qk, qe, kdec, egl, states, do):
    t = u.shape[1]
    nchunk = t // DN_CHUNK
    per, nstep, blk, sq, lane, st = _dn_seq_specs(nchunk, True)
    big, _, sqs, lanes = _dn_shapes(t)

    def body(u_ref, w_ref, a_ref, qe_ref, kd_ref, egl_ref, s_ref, do_ref,
             du_ref, dw_ref, da_ref, dqe_ref, dkd_ref, degl_ref, dstate):
        @pl.when(pl.program_id(0) == 0)
        def _():
            dstate[...] = jnp.zeros_like(dstate)

        for c in reversed(range(per)):
            rows = pl.ds(c * DN_CHUNK, DN_CHUNK)
            _, vjp = jax.vjp(_dn_seq, u_ref[:, rows], w_ref[:, rows], a_ref[:, c], qe_ref[:, rows], kd_ref[:, rows],
                             egl_ref[:, c], s_ref[:, c])
            (du_ref[:, rows], dw_ref[:, rows], da_ref[:, c], dqe_ref[:, rows], dkd_ref[:, rows], degl_ref[:, c],
             dstate[...]) = vjp((do_ref[:, rows], dstate[...]))

    return pl.pallas_call(
        body, name="dn_seq_bwd", grid=(nstep,),
        in_specs=[blk, blk, sq, blk, blk, lane, st, blk],
        out_specs=[blk, blk, sq, blk, blk, lane],
        out_shape=[big, big, sqs, big, big, lanes],
        scratch_shapes=[pltpu.VMEM((N_HEADS, HEAD_DIM, HEAD_DIM), F32)],
        compiler_params=_params(("arbitrary",)),
    )(u, w, a_qk, qe, kdec, egl, states, do)


def _dn_post_head(o, gate, w):
    return _rms(o, w) * _silu(gate)


def _dn_post_fwd(o, gate, w):
    t = gate.shape[0]
    tb = _rows(t)

    def body(o_ref, g_ref, w_ref, y_ref):
        for h in range(N_HEADS):
            sl = pl.ds(h * HEAD_DIM, HEAD_DIM)
            y_ref[:, sl] = _dn_post_head(o_ref[h], g_ref[:, sl], w_ref[...]).astype(BF16)

    row = pl.BlockSpec((tb, D_MODEL), lambda i: (i, 0))
    hm = pl.BlockSpec((N_HEADS, tb, HEAD_DIM), lambda i: (0, i, 0))
    return pl.pallas_call(
        body, name="dn_post_fwd", grid=(t // tb,),
        in_specs=[hm, row, pl.BlockSpec((1, HEAD_DIM), lambda i: (0, 0))],
        out_specs=row, out_shape=jax.ShapeDtypeStruct((t, D_MODEL), BF16),
        compiler_params=_params(("parallel",)),
    )(o, gate, w)


def _dn_post_bwd(o, gate, w, dy):
    t = gate.shape[0]
    tb = _rows(t)

    def body(o_ref, g_ref, w_ref, dy_ref, do_ref, dg_ref, dw_ref):
        i = pl.program_id(0)
        @pl.when(i == 0)
        def _():
            dw_ref[...] = jnp.zeros_like(dw_ref)

        for h in range(N_HEADS):
            sl = pl.ds(h * HEAD_DIM, HEAD_DIM)
            _, vjp = jax.vjp(_dn_post_head, o_ref[h], g_ref[:, sl], w_ref[...])
            do_ref[h], dg, dw = vjp(dy_ref[:, sl])
            dg_ref[:, sl] = dg.astype(BF16)
            dw_ref[...] += dw

    row = pl.BlockSpec((tb, D_MODEL), lambda i: (i, 0))
    hm = pl.BlockSpec((N_HEADS, tb, HEAD_DIM), lambda i: (0, i, 0))
    vec = pl.BlockSpec((1, HEAD_DIM), lambda i: (0, 0))
    return pl.pallas_call(
        body, name="dn_post_bwd", grid=(t // tb,),
        in_specs=[hm, row, vec, row],
        out_specs=[hm, row, vec],
        out_shape=[jax.ShapeDtypeStruct((N_HEADS, t, HEAD_DIM), F32), jax.ShapeDtypeStruct((t, D_MODEL), BF16),
                   jax.ShapeDtypeStruct((1, HEAD_DIM), F32)],
        compiler_params=_params(("arbitrary",)),
    )(o, gate, w, dy)


def _split_bf16(x):
    hi = x.astype(BF16)
    lo = (x - hi.astype(F32)).astype(BF16)
    return hi, lo


SB_Q_BLOCK = 512
SB_K_BLOCK = 256
SB_NEGLIGIBLE = -60.0


def _sb_logits(q, kb, mask, scale):
    z = _dot_nt(q, kb) * scale
    ls = jnp.minimum(z, 0.0) - jnp.log(1.0 + jnp.exp(-jnp.abs(z)))
    lk = ls - z
    if mask is not None:
        lk = jnp.where(mask, lk, 0.0)
    return ls, lk


def _sb_blocks(t):
    bq = min(SB_Q_BLOCK, t)
    bk = min(SB_K_BLOCK, bq)
    return bq, bk, bq // bk


def _sb_fwd(qkv):
    t = qkv.shape[0]
    bq, bk, nd = _sb_blocks(t)
    scale = HEAD_DIM ** -0.5

    def body(q_ref, k_ref, v_ref, o_ref, tot_ref, used_ref):
        i = pl.program_id(1)
        q = q_ref[...]
        rj = lax.broadcasted_iota(jnp.int32, (bk, bk), 0)
        cj = lax.broadcasted_iota(jnp.int32, (bk, bk), 1)
        after = (rj > cj).astype(BF16)
        trow = lax.broadcasted_iota(jnp.int32, (bq, bk), 0)
        scol = lax.broadcasted_iota(jnp.int32, (bq, bk), 1)

        def tile(j, run, acc, mask):
            off = pl.multiple_of(j * bk, bk)
            kb = k_ref[pl.ds(off, bk), :]
            vb = v_ref[pl.ds(off, bk), :]
            ls, lk = _sb_logits(q, kb, mask, scale)
            hi, lo = _split_bf16(lk)
            between = _dot(hi, after) + _dot(lo, after) + run
            a = jnp.exp(ls + between)
            if mask is not None:
                a = jnp.where(mask, a, 0.0)
            acc = acc + _dot(a.astype(BF16), vb)
            return run + jnp.sum(lk, axis=1, keepdims=True), acc

        run, acc = jnp.zeros((bq, 1), F32), jnp.zeros((bq, HEAD_DIM), F32)
        for d in reversed(range(nd)):
            run, acc = tile(i * nd + d, run, acc, scol + d * bk < trow)
        def more(c):
            return jnp.logical_and(c[0] < i * nd, jnp.max(c[1]) > SB_NEGLIGIBLE)

        def far(c):
            run_, acc_ = tile(i * nd - 1 - c[0], c[1], c[2], None)
            return c[0] + 1, run_, acc_

        used, run, acc = lax.while_loop(more, far, (jnp.int32(0), run, acc))
        o_ref[...] = acc.astype(BF16)
        tot_ref[...] = jnp.broadcast_to(run, (bq, HEAD_DIM))
        used_ref[...] = jnp.full(used_ref.shape, used, F32)

    qs = pl.BlockSpec((bq, HEAD_DIM), lambda h, i: (i, h))
    ks = pl.BlockSpec((t, HEAD_DIM), lambda h, i: (0, N_HEADS + h))
    vs = pl.BlockSpec((t, HEAD_DIM), lambda h, i: (0, 2 * N_HEADS + h))
    return pl.pallas_call(
        body, name="sb_fwd", grid=(N_HEADS, t // bq),
        in_specs=[qs, ks, vs], out_specs=[qs, qs, pl.BlockSpec((1, 1, 1, LANES), lambda h, i: (h, i, 0, 0))],
        out_shape=[jax.ShapeDtypeStruct((t, D_MODEL), BF16), jax.ShapeDtypeStruct((t, D_MODEL), F32),
                   jax.ShapeDtypeStruct((N_HEADS, t // bq, 1, LANES), F32)],
        compiler_params=_params(("parallel", "arbitrary")),
    )(qkv, qkv, qkv)


def _sb_bwd(qkv, tot, used, do):
    t = qkv.shape[0]
    bq, bk, nd = _sb_blocks(t)
    scale = HEAD_DIM ** -0.5

    def body(q_ref, k_ref, v_ref, tot_ref, used_ref, do_ref, dq_ref, dk_ref, dv_ref):
        i = pl.program_id(1)

        @pl.when(i == 0)
        def _():
            dk_ref[...] = jnp.zeros_like(dk_ref)
            dv_ref[...] = jnp.zeros_like(dv_ref)

        q = q_ref[...]
        do = do_ref[...]
        total = tot_ref[:, 0:1]
        rj = lax.broadcasted_iota(jnp.int32, (bk, bk), 0)
        cj = lax.broadcasted_iota(jnp.int32, (bk, bk), 1)
        upto = (rj <= cj).astype(BF16)
        before = (rj < cj).astype(BF16)
        trow = lax.broadcasted_iota(jnp.int32, (bq, bk), 0)
        scol = lax.broadcasted_iota(jnp.int32, (bq, bk), 1)

        def tile(j, run_k, run_e, dq, mask):
            off = pl.multiple_of(j * bk, bk)
            kb = k_ref[pl.ds(off, bk), :]
            vb = v_ref[pl.ds(off, bk), :]
            ls, lk = _sb_logits(q, kb, mask, scale)
            hi, lo = _split_bf16(lk)
            between = total - (_dot(hi, upto) + _dot(lo, upto) + run_k)
            a = jnp.exp(ls + between)
            if mask is not None:
                a = jnp.where(mask, a, 0.0)
            e = a * _dot_nt(do, vb)
            ehi, elo = _split_bf16(e)
            pre = _dot(ehi, before) + _dot(elo, before) + run_e
            sig = jnp.exp(ls)
            dz = e * (1.0 - sig) - pre * sig
            if mask is not None:
                dz = jnp.where(mask, dz, 0.0)
            dz = (dz * scale).astype(BF16)
            dq = dq + _dot(dz, kb)
            dk_ref[pl.ds(off, bk), :] += _dot_tn(dz, q)
            dv_ref[pl.ds(off, bk), :] += _dot_tn(a.astype(BF16), do)
            return (run_k + jnp.sum(lk, axis=1, keepdims=True),
                    run_e + jnp.sum(e, axis=1, keepdims=True), dq)

        zero = jnp.zeros((bq, 1), F32)
        visited = jnp.clip(jnp.max(used_ref[...]).astype(jnp.int32), 0, i * nd)
        carry = lax.fori_loop(i * nd - visited, i * nd, lambda j, c: tile(j, c[0], c[1], c[2], None),
                              (zero, zero, jnp.zeros((bq, HEAD_DIM), F32)))
        for d in range(nd):
            carry = tile(i * nd + d, *carry, scol + d * bk < trow)
        dq_ref[...] = carry[2]

    qs = pl.BlockSpec((bq, HEAD_DIM), lambda h, i: (i, h))
    ks = pl.BlockSpec((t, HEAD_DIM), lambda h, i: (0, N_HEADS + h))
    vs = pl.BlockSpec((t, HEAD_DIM), lambda h, i: (0, 2 * N_HEADS + h))
    full = pl.BlockSpec((t, HEAD_DIM), lambda h, i: (0, h))
    big = jax.ShapeDtypeStruct((t, D_MODEL), F32)
    return pl.pallas_call(
        body, name="sb_bwd", grid=(N_HEADS, t // bq),
        in_specs=[qs, ks, vs, qs, pl.BlockSpec((1, 1, 1, LANES), lambda h, i: (h, i, 0, 0)), qs],
        out_specs=[qs, full, full],
        out_shape=[big, big, big],
        compiler_params=_params(("parallel", "arbitrary")),
    )(qkv, qkv, qkv, tot, used, do)


def _merge_fwd(o_dn, o_sb, gl, x, wp_dn, wp_sb, w_out, w2):
    t = x.shape[0]
    tb = _rows(t)

    def body(odn_ref, osb_ref, gl_ref, x_ref, wpd_ref, wps_ref, wo_ref, w2_ref,
             pdn_ref, psb_ref, mix_ref, x1_ref, n2_ref):
        pdn = _dot(odn_ref[...], wpd_ref[...])
        psb = _dot(osb_ref[...], wps_ref[...])
        gates = jax.nn.sigmoid(gl_ref[...])
        mixed = (gates[:, :D_MODEL] * pdn + gates[:, D_MODEL:] * psb).astype(BF16)
        x1 = x_ref[...] + _dot(mixed, wo_ref[...])
        pdn_ref[...] = pdn.astype(BF16)
        psb_ref[...] = psb.astype(BF16)
        mix_ref[...] = mixed
        x1_ref[...] = x1
        n2_ref[...] = _rms(x1, w2_ref[...]).astype(BF16)

    row = pl.BlockSpec((tb, D_MODEL), lambda i: (i, 0))
    sq = pl.BlockSpec((D_MODEL, D_MODEL), lambda i: (0, 0))
    f = jax.ShapeDtypeStruct((t, D_MODEL), F32)
    b = jax.ShapeDtypeStruct((t, D_MODEL), BF16)
    return pl.pallas_call(
        body, name="merge_fwd", grid=(t // tb,),
        in_specs=[row, row, pl.BlockSpec((tb, 2 * D_MODEL), lambda i: (i, 0)), row, sq, sq, sq,
                  pl.BlockSpec((1, D_MODEL), lambda i: (0, 0))],
        out_specs=[row] * 5, out_shape=[b, b, b, f, b],
        compiler_params=_params(("parallel",)),
    )(o_dn, o_sb, gl, x, wp_dn, wp_sb, w_out, w2)


def _merge_bwd(dx2, dn2, x1, w2, gl, pdn, psb, wp_dn, wp_sb, w_out):
    t = x1.shape[0]
    tb = _rows(t)

    def body(dx2_ref, dn2_ref, x1_ref, w2_ref, gl_ref, pdn_ref, psb_ref, wpd_ref, wps_ref, wo_ref,
             dx1_ref, dw2_ref, dgl_ref, dpdn_ref, dpsb_ref, dodn_ref, dosb_ref):
        i = pl.program_id(0)
        _, vjp = jax.vjp(_rms, x1_ref[...], w2_ref[...])
        dxn, dw2 = vjp(dn2_ref[...])
        dx1 = dx2_ref[...] + dxn
        dx1_ref[...] = dx1

        @pl.when(i == 0)
        def _():
            dw2_ref[...] = jnp.zeros_like(dw2_ref)

        dw2_ref[...] += dw2
        dmix = _dot_nt(dx1.astype(BF16), wo_ref[...])
        gates = jax.nn.sigmoid(gl_ref[...])
        g_dn, g_sb = gates[:, :D_MODEL], gates[:, D_MODEL:]
        dpdn = (dmix * g_dn).astype(BF16)
        dpsb = (dmix * g_sb).astype(BF16)
        dgl_ref[:, :D_MODEL] = (dmix * pdn_ref[...].astype(F32) * g_dn * (1.0 - g_dn)).astype(BF16)
        dgl_ref[:, D_MODEL:] = (dmix * psb_ref[...].astype(F32) * g_sb * (1.0 - g_sb)).astype(BF16)
        dpdn_ref[...] = dpdn
        dpsb_ref[...] = dpsb
        dodn_ref[...] = _dot_nt(dpdn, wpd_ref[...])
        dosb_ref[...] = _dot_nt(dpsb, wps_ref[...]).astype(BF16)

    row = pl.BlockSpec((tb, D_MODEL), lambda i: (i, 0))
    wide = pl.BlockSpec((tb, 2 * D_MODEL), lambda i: (i, 0))
    sq = pl.BlockSpec((D_MODEL, D_MODEL), lambda i: (0, 0))
    vec = pl.BlockSpec((1, D_MODEL), lambda i: (0, 0))
    f = jax.ShapeDtypeStruct((t, D_MODEL), F32)
    b = jax.ShapeDtypeStruct((t, D_MODEL), BF16)
    return pl.pallas_call(
        body, name="merge_bwd", grid=(t // tb,),
        in_specs=[row, row, row, vec, wide, row, row, sq, sq, sq],
        out_specs=[row, vec, wide, row, row, row, row],
        out_shape=[f, jax.ShapeDtypeStruct((1, D_MODEL), F32), jax.ShapeDtypeStruct((t, 2 * D_MODEL), BF16),
                   b, b, f, b],
        compiler_params=_params(("arbitrary",)),
    )(dx2, dn2, x1, w2, gl, pdn, psb, wp_dn, wp_sb, w_out)


def _conv_taps(buf, w_ref, first, rows, cols=slice(None)):
    y = w_ref[0:1, cols] * buf[pl.ds(first, rows), cols]
    for s in range(1, w_ref.shape[0]):
        y = y + w_ref[s:s + 1, cols] * buf[pl.ds(first + s, rows), cols]
    return y


def _ffn_mid_fwd(pre_g, pre_u, wg, wu):
    t, c = pre_g.shape
    kk = wg.shape[0]
    tb, cb = _rows(t), _pick(c, ELEMENTWISE_COLS)
    per = tb // HALO

    def body(g_ref, gh_ref, u_ref, uh_ref, wg_ref, wu_ref, a_ref, gbuf, ubuf):
        i = pl.program_id(0)
        for buf, ref, halo in ((gbuf, g_ref, gh_ref), (ubuf, u_ref, uh_ref)):
            buf[pl.ds(HALO, tb), :] = ref[...]
            buf[pl.ds(0, HALO), :] = jnp.where(i == 0, 0.0, halo[...])
        for j in range(cb // LANES):
            sl = pl.ds(j * LANES, LANES)
            ug = _conv_taps(gbuf, wg_ref, HALO - (kk - 1), tb, sl)
            uu = _conv_taps(ubuf, wu_ref, HALO - (kk - 1), tb, sl)
            a_ref[:, sl] = (_silu(ug) * uu).astype(BF16)

    blk = pl.BlockSpec((tb, cb), lambda i, j: (i, j))
    halo = pl.BlockSpec((HALO, cb), lambda i, j: (jnp.maximum(i * per - 1, 0), j))
    wspec = pl.BlockSpec((kk, cb), lambda i, j: (0, j))
    return pl.pallas_call(
        body, name="ffn_mid_fwd", grid=(t // tb, c // cb),
        in_specs=[blk, halo, blk, halo, wspec, wspec], out_specs=blk,
        out_shape=jax.ShapeDtypeStruct((t, c), BF16),
        scratch_shapes=[pltpu.VMEM((tb + HALO, cb), F32)] * 2,
        compiler_params=_params(("parallel", "parallel")),
    )(pre_g, pre_g, pre_u, pre_u, wg, wu)


def _ffn_mid_bwd(pre_g, pre_u, wg, wu, da):
    t, c = pre_g.shape
    kk = wg.shape[0]
    tb, cb = _rows(t), _pick(c, ELEMENTWISE_COLS)
    per = tb // HALO
    nblk = t // tb
    ext = tb + HALO

    def body(g_ref, gb_ref, ga_ref, u_ref, ub_ref, ua_ref, da_ref, daa_ref, wg_ref, wu_ref,
             dg_ref, du_ref, dwg_ref, dwu_ref, gbuf, ubuf, dabuf, dgbuf, dubuf):
        i = pl.program_id(1)
        last = i == nblk - 1
        for buf, ref, before, after in ((gbuf, g_ref, gb_ref, ga_ref), (ubuf, u_ref, ub_ref, ua_ref)):
            buf[pl.ds(0, HALO), :] = jnp.where(i == 0, 0.0, before[...])
            buf[pl.ds(HALO, tb), :] = ref[...]
            buf[pl.ds(HALO + tb, HALO), :] = jnp.where(last, 0.0, after[...])
        dabuf[pl.ds(0, tb), :] = da_ref[...]
        dabuf[pl.ds(tb, HALO), :] = jnp.where(last, 0.0, daa_ref[...])

        @pl.when(i == 0)
        def _():
            dwg_ref[...] = jnp.zeros_like(dwg_ref)
            dwu_ref[...] = jnp.zeros_like(dwu_ref)

        for j in range(cb // LANES):
            sl = pl.ds(j * LANES, LANES)
            ug = _conv_taps(gbuf, wg_ref, HALO - (kk - 1), ext, sl)
            uu = _conv_taps(ubuf, wu_ref, HALO - (kk - 1), ext, sl)
            _, vjp = jax.vjp(lambda g, u: _silu(g) * u, ug, uu)
            dgbuf[:, sl], dubuf[:, sl] = vjp(dabuf[:, sl])
            for dbuf, xbuf, w_ref, dx_ref, dw_ref in ((dgbuf, gbuf, wg_ref, dg_ref, dwg_ref),
                                                      (dubuf, ubuf, wu_ref, du_ref, dwu_ref)):
                x = xbuf[pl.ds(HALO, tb), sl]
                dx = None
                for s in range(kk):
                    shifted = dbuf[pl.ds(kk - 1 - s, tb), sl]
                    term = w_ref[s:s + 1, sl] * shifted
                    dx = term if dx is None else dx + term
                    dw_ref[s:s + 1, sl] += jnp.sum(shifted * x, axis=0, keepdims=True)
                dx_ref[:, sl] = dx.astype(BF16)

    blk = pl.BlockSpec((tb, cb), lambda j, i: (i, j))
    before = pl.BlockSpec((HALO, cb), lambda j, i: (jnp.maximum(i * per - 1, 0), j))
    after = pl.BlockSpec((HALO, cb), lambda j, i: (jnp.minimum((i + 1) * per, t // HALO - 1), j))
    wspec = pl.BlockSpec((kk, cb), lambda j, i: (0, j))
    dwspec = pl.BlockSpec((HALO, cb), lambda j, i: (0, j))
    half = jax.ShapeDtypeStruct((t, c), BF16)
    dwshape = jax.ShapeDtypeStruct((HALO, c), F32)
    return pl.pallas_call(
        body, name="ffn_mid_bwd", grid=(c // cb, nblk),
        in_specs=[blk, before, after, blk, before, after, blk, after, wspec, wspec],
        out_specs=[blk, blk, dwspec, dwspec],
        out_shape=[half, half, dwshape, dwshape],
        scratch_shapes=[pltpu.VMEM((ext + HALO, cb), F32)] * 2 + [pltpu.VMEM((ext, cb), F32)] * 3,
        compiler_params=_params(("parallel", "arbitrary")),
    )(pre_g, pre_g, pre_g, pre_u, pre_u, pre_u, da, da, wg, wu)


def _down_loss(a, w_down, x1, wf, target):
    t = x1.shape[0]
    tb = _rows(t)

    def body(a_ref, wd_ref, x1_ref, wf_ref, tgt_ref, dx2_ref, dwf_ref, loss_ref):
        i = pl.program_id(0)
        x2 = x1_ref[...] + _dot(a_ref[...], wd_ref[...])
        y, vjp = jax.vjp(_rms, x2, wf_ref[...])
        err = y - tgt_ref[...]
        dx2, dwf = vjp(err * (1.0 / D_MODEL))
        dx2_ref[...] = dx2
        part = jnp.sum(jnp.sum(err * err, axis=1, keepdims=True), axis=0, keepdims=True) * (0.5 / D_MODEL)

        @pl.when(i == 0)
        def _():
            dwf_ref[...] = jnp.zeros_like(dwf_ref)
            loss_ref[...] = jnp.zeros_like(loss_ref)

        dwf_ref[...] += dwf
        loss_ref[...] += jnp.broadcast_to(part, loss_ref.shape)

    row = pl.BlockSpec((tb, D_MODEL), lambda i: (i, 0))
    vec = pl.BlockSpec((1, D_MODEL), lambda i: (0, 0))
    return pl.pallas_call(
        body, name="down_loss", grid=(t // tb,),
        in_specs=[pl.BlockSpec((tb, D_FF), lambda i: (i, 0)), pl.BlockSpec((D_FF, D_MODEL), lambda i: (0, 0)),
                  row, vec, row],
        out_specs=[row, vec, pl.BlockSpec((1, LANES), lambda i: (0, 0))],
        out_shape=[jax.ShapeDtypeStruct((t, D_MODEL), F32), jax.ShapeDtypeStruct((1, D_MODEL), F32),
                   jax.ShapeDtypeStruct((1, LANES), F32)],
        compiler_params=_params(("arbitrary",)),
    )(a, w_down, x1, wf, target)


def _local_step(x, target, wts, late_wire=(), late_weights=None, early_partials=None):
    t = x.shape[0]
    nchunk = t // DN_CHUNK

    n1, hab = _norm1_fwd(x, wts["norm1"], wts["w_ab"])
    dnqkv = _mm(n1, wts["w_dnqkv"], name="h_dnqkv")
    dngate = _mm(n1, wts["w_dngate"], name="h_dngate")
    sbqkv = _mm(n1, wts["w_sbqkv"], out_dtype=BF16, name="h_sbqkv")
    gl = _mm(n1, wts["w_gl"], name="h_gl")

    cdn = _conv_fwd(dnqkv, wts["dn_conv"], "dn_conv_fwd")
    qn, kn, vv, gb = _dn_prep_fwd(cdn, hab, wts["alog"], wts["dtb"])
    per_head = gb[:, :2 * N_HEADS].T.reshape(2 * N_HEADS, nchunk, DN_CHUNK)
    grow, brow = per_head[:N_HEADS, :, None, :], per_head[N_HEADS:, :, None, :]
    u_dn, w_dn, a_qk, qe, kdec, egl, tinv, *late = _dn_local_fwd(qn, kn, vv, grow, brow, late_wire)
    if late_wire:
        wts = {**wts, **late_weights(late)}
    o_raw, states = _dn_seq_fwd(u_dn, w_dn, a_qk, qe, kdec, egl)
    o_dn = _dn_post_fwd(o_raw, dngate, wts["dn_norm"])

    o_sb, tot, sb_used = _sb_fwd(sbqkv)

    pdn, psb, mixed, x1, n2 = _merge_fwd(o_dn, o_sb, gl, x, wts["wp_dn"], wts["wp_sb"], wts["w_out"],
                                         wts["norm2"])
    pre_g = _mm(n2, wts["w_up_g"], name="ffn_up_g")
    pre_u = _mm(n2, wts["w_up_u"], name="ffn_up_u")
    act = _ffn_mid_fwd(pre_g, pre_u, wts["ffn_conv_g"], wts["ffn_conv_u"])
    dx2, d_normf, loss_part = _down_loss(act, wts["w_down"], x1, wts["normf"], target)

    grads = {"normf": d_normf}
    da = _mm(dx2, wts["w_down"], tb=True, name="d_act")
    grads["w_down"] = _mm(act, dx2, ta=True, out_dtype=BF16, name="dw_down")
    dpre_g, dpre_u, dcw_g, dcw_u = _ffn_mid_bwd(pre_g, pre_u, wts["ffn_conv_g"], wts["ffn_conv_u"], da)
    grads["ffn_conv"] = jnp.concatenate([dcw_g[:FFN_CONV], dcw_u[:FFN_CONV]], axis=1)
    dn2 = _mm(dpre_g, wts["w_up_g"], tb=True, name="dn2_g")
    dn2 = _mm(dpre_u, wts["w_up_u"], tb=True, add=dn2, name="dn2_u")
    grads["w_up_g"] = _mm(n2, dpre_g, ta=True, out_dtype=BF16, name="dw_up_g")
    grads["w_up_u"] = _mm(n2, dpre_u, ta=True, out_dtype=BF16, name="dw_up_u")

    dx1, grads["norm2"], dgl, dpdn, dpsb, do_dn, do_sb = _merge_bwd(
        dx2, dn2, x1, wts["norm2"], gl, pdn, psb, wts["wp_dn"], wts["wp_sb"], wts["w_out"])
    grads["w_out"] = _mm(mixed, dx1, ta=True, out_dtype=BF16, name="dw_out")
    grads["wp_dn"] = _mm(o_dn, dpdn, ta=True, out_dtype=BF16, name="dw_proj_dn")
    grads["wp_sb"] = _mm(o_sb, dpsb, ta=True, out_dtype=BF16, name="dw_proj_sb")

    partials = early_partials(grads) if early_partials else ()
    dsq, dsk, dsv = _sb_bwd(sbqkv, tot, sb_used, do_sb)
    dsbqkv = jnp.concatenate([dsq, dsk, dsv], axis=1).astype(BF16)

    do_raw, ddngate, grads["dn_norm"] = _dn_post_bwd(o_raw, dngate, wts["dn_norm"], do_dn)
    seq_grads = _dn_seq_bwd(u_dn, w_dn, a_qk, qe, kdec, egl, states, do_raw)
    dqn, dkn, dvv, dgrow, dbrow, *arrived = _dn_local_bwd(qn, kn, vv, grow, brow, tinv, *seq_grads,
                                                          partials=partials)
    dgb = jnp.concatenate([dgrow.reshape(N_HEADS, t), dbrow.reshape(N_HEADS, t)], axis=0).T
    dgb = jnp.pad(dgb, ((0, 0), (0, LANES - 2 * N_HEADS)))
    dcdn, dhab, grads["alog"], grads["dtb"] = _dn_prep_bwd(cdn, hab, wts["alog"], wts["dtb"], dqn, dkn, dvv, dgb)
    ddnqkv, dcw_dn = _conv_bwd(dcdn, dnqkv, wts["dn_conv"], "dn_conv_bwd", BF16)
    grads["dn_conv"] = dcw_dn[:DN_CONV]

    dn1 = _mm(ddnqkv, wts["w_dnqkv"], tb=True, name="dn1_dnqkv")
    dn1 = _mm(ddngate, wts["w_dngate"], tb=True, add=dn1, name="dn1_dngate")
    dn1 = _mm(dsbqkv, wts["w_sbqkv"], tb=True, add=dn1, name="dn1_sbqkv")
    dn1 = _mm(dgl, wts["w_gl"], tb=True, add=dn1, name="dn1_gl")
    grads["w_dnqkv"] = _mm(n1, ddnqkv, ta=True, out_dtype=BF16, name="dw_dnqkv")
    grads["w_dngate"] = _mm(n1, ddngate, ta=True, out_dtype=BF16, name="dw_dngate")
    grads["w_sbqkv"] = _mm(n1, dsbqkv, ta=True, out_dtype=BF16, name="dw_sbqkv")
    grads["w_gl"] = _mm(n1, dgl, ta=True, out_dtype=BF16, name="dw_gl")
    grads["w_ab"] = _mm(n1, dhab, ta=True, out_dtype=BF16, name="dw_ab")
    grad_x, grads["norm1"] = _norm1_bwd(x, wts["norm1"], dn1, dx1, dhab, wts["w_ab"])
    return loss_part, grad_x, grads, (list(partials), arrived)


def _place():
    return lax.axis_index("x"), lax.axis_index("y"), lax.axis_index("c")


def _hbm_specs(n):
    return [pl.BlockSpec(memory_space=pltpu.HBM)] * n


GATHER_SEMS = 8
GATHER_FORWARD_AT = 5


def _gather_protocol(ins, outs, send_sems, recv_sems):
    n = len(ins)
    x, y, c = _place()
    me = 2 * x + y
    sibling = (x, y, 1 - c)
    xn, yn, dg = (1 - x, y), (x, 1 - y), (1 - x, 1 - y)
    idx = lambda chip: 2 * chip[0] + chip[1]

    def part(a, chip_index, core, quarter=None):
        half = ins[a].shape[0] // 2
        if quarter is None:
            return outs[a].at[chip_index, pl.ds(core * half, half), :]
        return outs[a].at[chip_index, pl.ds(core * half + quarter * (half // 2), half // 2), :]

    def copy(a, k, src, dst, to):
        return pltpu.make_async_remote_copy(src_ref=src, dst_ref=dst, send_sem=send_sems.at[GATHER_SEMS * a + k],
                                            recv_sem=recv_sems.at[GATHER_SEMS * a + k], device_id=to,
                                            device_id_type=MESH)

    def sent(a, k):
        half = ins[a].shape[0] // 2
        my_half = ins[a].at[pl.ds(c * half, half), :]
        if k < 2:
            return copy(a, k, my_half, part(a, me, c), (*(xn, yn)[k], c))
        if k < 4:
            src = part(a, idx((xn, yn)[k - 2]), c, k - 2)
            return copy(a, k, src, src, (*(yn, xn)[k - 2], c))
        src = (part(a, idx(xn), c), part(a, idx(yn), c), part(a, idx(dg), c, 0), part(a, idx(dg), c, 1))[k - 4]
        return copy(a, k, src, src, sibling)

    def landed(a, k):
        dst = (part(a, idx(xn), c), part(a, idx(yn), c), part(a, idx(dg), c, 0), part(a, idx(dg), c, 1),
               part(a, idx(xn), 1 - c), part(a, idx(yn), 1 - c), part(a, idx(dg), 1 - c, 0),
               part(a, idx(dg), 1 - c, 1))[k]
        return copy(a, k, dst, dst, sibling)

    def begin():
        for a in range(n):
            sent(a, 0).start()
            sent(a, 1).start()

    def middle():
        for a in range(n):
            for k in range(2):
                landed(a, k).wait_recv()
                sent(a, 2 + k).start()
                sent(a, 4 + k).start()

    def end():
        for a in range(n):
            for k in (2, 3):
                landed(a, k).wait_recv()
                sent(a, 4 + k).start()
        for a in range(n):
            for k in range(4, GATHER_SEMS):
                landed(a, k).wait_recv()
        for a in range(n):
            for k in range(GATHER_SEMS):
                sent(a, k).wait_send()

    return begin, middle, end


def _gather_out_shapes(shards):
    return [jax.ShapeDtypeStruct((N_CHIPS,) + s.shape, s.dtype) for s in shards]


def _gather_sems(n):
    return [pltpu.SemaphoreType.DMA((GATHER_SEMS * n,)), pltpu.SemaphoreType.DMA((GATHER_SEMS * n,))]


def _gather_shards(shards):
    n = len(shards)

    def body(*refs):
        begin, middle, end = _gather_protocol(refs[:n], refs[n:2 * n], *refs[2 * n:])
        begin()
        middle()
        end()

    return pl.pallas_call(
        body, name="gather_weights", in_specs=_hbm_specs(n), out_specs=_hbm_specs(n),
        out_shape=_gather_out_shapes(shards), scratch_shapes=_gather_sems(n),
    )(*shards)


def _pair_exchange_halves(gs, tag):
    n = len(gs)

    def body(*refs):
        ins, outs, (send_sems, recv_sems) = refs[:n], refs[n:2 * n], refs[2 * n:]
        x, y, c = _place()
        cps = []
        for a in range(n):
            half = ins[a].shape[1] // 2
            cp = pltpu.make_async_remote_copy(src_ref=ins[a].at[:, pl.ds((1 - c) * half, half), :], dst_ref=outs[a],
                                              send_sem=send_sems.at[a], recv_sem=recv_sems.at[a],
                                              device_id=(x, y, 1 - c), device_id_type=MESH)
            cp.start()
            cps.append(cp)
        for cp in cps:
            cp.wait()

    return pl.pallas_call(
        body, name="grad_pair_exchange_" + tag, in_specs=_hbm_specs(n), out_specs=_hbm_specs(n),
        out_shape=[jax.ShapeDtypeStruct((g.shape[0], g.shape[1] // 2, g.shape[2]), g.dtype) for g in gs],
        scratch_shapes=[pltpu.SemaphoreType.DMA((n,)), pltpu.SemaphoreType.DMA((n,))],
    )(*gs)


def _pick_rows(n, target=1024):
    best = 16
    for b in range(16, min(n, target) + 1, 16):
        if n % b == 0:
            best = b
    return best


def _pair_add(g, got, c_idx, tag):
    nsh, rows, cols = g.shape
    half = rows // 2
    rb = _pick_rows(half)

    def body(c_ref, g_ref, got_ref, o_ref):
        o_ref[...] = (g_ref[...].astype(F32) + got_ref[...].astype(F32)).astype(BF16)

    nb = half // rb
    grid_spec = pltpu.PrefetchScalarGridSpec(
        num_scalar_prefetch=1, grid=(nsh, nb),
        in_specs=[pl.BlockSpec((1, rb, cols), lambda s, i, c_ref: (s, c_ref[0] * nb + i, 0)),
                  pl.BlockSpec((1, rb, cols), lambda s, i, c_ref: (s, i, 0))],
        out_specs=pl.BlockSpec((1, rb, cols), lambda s, i, c_ref: (s, i, 0)))
    return pl.pallas_call(
        body, name="grad_pair_add_" + tag, grid_spec=grid_spec,
        out_shape=jax.ShapeDtypeStruct((nsh, half, cols), BF16),
        compiler_params=_params(("parallel", "parallel")),
    )(c_idx, g, got)


def _chip_exchange_protocol(ins, outs, send_sems, recv_sems):
    x, y, c = _place()
    chips = [(1 - x, y), (x, 1 - y), (1 - x, 1 - y)]

    def copies():
        return [pltpu.make_async_remote_copy(src_ref=ins[a].at[2 * px + py], dst_ref=outs[a].at[j],
                                             send_sem=send_sems.at[3 * a + j], recv_sem=recv_sems.at[3 * a + j],
                                             device_id=(px, py, c), device_id_type=MESH)
                for a in range(len(ins)) for j, (px, py) in enumerate(chips)]

    def begin():
        for cp in copies():
            cp.start()

    def end():
        for cp in copies():
            cp.wait_recv()
        for cp in copies():
            cp.wait_send()

    return begin, end


def _chip_exchange_shapes(ps):
    return [jax.ShapeDtypeStruct((N_CHIPS - 1,) + p.shape[1:], p.dtype) for p in ps]


def _chip_exchange_sems(n):
    return [pltpu.SemaphoreType.DMA((3 * n,)), pltpu.SemaphoreType.DMA((3 * n,))]


def _chip_exchange(ps):
    n = len(ps)

    def body(*refs):
        begin, end = _chip_exchange_protocol(refs[:n], refs[n:2 * n], *refs[2 * n:])
        begin()
        end()

    return pl.pallas_call(
        body, name="grad_chip_exchange", in_specs=_hbm_specs(n), out_specs=_hbm_specs(n),
        out_shape=_chip_exchange_shapes(ps), scratch_shapes=_chip_exchange_sems(n),
    )(*ps)


def _sum_partials(p, got, chip_idx, tag):
    nsh, half, cols = got.shape
    rb = _pick_rows(half)

    def body(me_ref, p_ref, got_ref, o_ref):
        acc = p_ref[0].astype(F32)
        for s in range(nsh):
            acc = acc + got_ref[s].astype(F32)
        o_ref[...] = acc

    grid_spec = pltpu.PrefetchScalarGridSpec(
        num_scalar_prefetch=1, grid=(half // rb,),
        in_specs=[pl.BlockSpec((1, rb, cols), lambda i, me_ref: (me_ref[0], i, 0)),
                  pl.BlockSpec((nsh, rb, cols), lambda i, me_ref: (0, i, 0))],
        out_specs=pl.BlockSpec((rb, cols), lambda i, me_ref: (i, 0)))
    return pl.pallas_call(
        body, name="grad_sum_chips_" + tag, grid_spec=grid_spec,
        out_shape=jax.ShapeDtypeStruct((half, cols), F32),
        compiler_params=_params(("parallel",)),
    )(chip_idx, p, got)


def _pair_share(rs):
    n = len(rs)

    def body(*refs):
        ins, outs, (send_sems, recv_sems) = refs[:n], refs[n:2 * n], refs[2 * n:]
        x, y, c = _place()
        cps = []
        for a in range(n):
            cp = pltpu.make_async_remote_copy(src_ref=ins[a], dst_ref=outs[a], send_sem=send_sems.at[a],
                                              recv_sem=recv_sems.at[a], device_id=(x, y, 1 - c),
                                              device_id_type=MESH)
            cp.start()
            cps.append(cp)
        for cp in cps:
            cp.wait()

    return pl.pallas_call(
        body, name="grad_pair_share", in_specs=_hbm_specs(n), out_specs=_hbm_specs(n),
        out_shape=[jax.ShapeDtypeStruct(r.shape, r.dtype) for r in rs],
        scratch_shapes=[pltpu.SemaphoreType.DMA((n,)), pltpu.SemaphoreType.DMA((n,))],
    )(*rs)


def _small_allreduce(v):
    rows, cols = v.shape
    ndev = 8

    def body(in_ref, out_ref, slots, send_sems, recv_sems):
        x, y, c = _place()
        me = 4 * x + 2 * y + c
        slots[me] = in_ref[...]
        sends = []
        for k in range(1, ndev):
            peer = (x ^ (k >> 2), y ^ ((k >> 1) & 1), c ^ (k & 1))
            cp = pltpu.make_async_remote_copy(src_ref=in_ref, dst_ref=slots.at[me], send_sem=send_sems.at[k - 1],
                                              recv_sem=recv_sems.at[k - 1], device_id=peer, device_id_type=MESH)
            cp.start()
            sends.append(cp)
        for k in range(1, ndev):
            there = slots.at[me ^ k]
            pltpu.make_async_remote_copy(src_ref=there, dst_ref=there, send_sem=send_sems.at[k - 1],
                                         recv_sem=recv_sems.at[k - 1], device_id=(x, y, c),
                                         device_id_type=MESH).wait_recv()
        for cp in sends:
            cp.wait_send()
        acc = slots[0]
        for s in range(1, ndev):
            acc = acc + slots[s]
        out_ref[...] = acc

    return pl.pallas_call(
        body, name="small_allreduce",
        in_specs=[pl.BlockSpec(memory_space=pltpu.VMEM)],
        out_specs=pl.BlockSpec(memory_space=pltpu.VMEM),
        out_shape=jax.ShapeDtypeStruct((rows, cols), F32),
        scratch_shapes=[pltpu.VMEM((ndev, rows, cols), F32), pltpu.SemaphoreType.DMA((ndev - 1,)),
                        pltpu.SemaphoreType.DMA((ndev - 1,))],
    )(v)


def _adamw(w, g, m, v, name):
    r, c = w.shape
    rb = r if r <= 128 else _pick_rows_8(r, 128)
    c1 = 1.0 - ADAM_B1 ** ADAM_STEP
    c2 = 1.0 - ADAM_B2 ** ADAM_STEP

    def body(w_ref, g_ref, m_ref, v_ref, d_ref, nm_ref, nv_ref):
        gg = g_ref[...]
        nm = ADAM_B1 * m_ref[...] + (1.0 - ADAM_B1) * gg
        nv = ADAM_B2 * v_ref[...] + (1.0 - ADAM_B2) * (gg * gg)
        d_ref[...] = -ADAM_LR * ((nm / c1) / (jnp.sqrt(nv / c2) + ADAM_EPS) + ADAM_WD * w_ref[...])
        nm_ref[...] = nm
        nv_ref[...] = nv

    blk = pl.BlockSpec((rb, c), lambda i: (i, 0))
    shp = jax.ShapeDtypeStruct((r, c), F32)
    return pl.pallas_call(
        body, name=name, grid=(r // rb,), in_specs=[blk] * 4, out_specs=[blk] * 3, out_shape=[shp] * 3,
        compiler_params=_params(("parallel",)),
    )(w, g, m, v)


def _pick_rows_8(n, target):
    best = n
    for b in range(8, min(n, target) + 1, 8):
        if n % b == 0:
            best = b
    return best


W_IN_COLS = 2308
W_UP_COLS = 1408
W_DOWN_ROWS = 704
DN_CONV_COLS = 768
FFN_CONV_COLS = 1408
PROJ_ROWS = 256
ROW_TILE = 16
ROW_SEGS = [("wp_dn", PROJ_ROWS), ("wp_sb", PROJ_ROWS), ("w_out", PROJ_ROWS), ("w_down", W_DOWN_ROWS),
            ("dn_conv", ROW_TILE), ("ffn_conv", ROW_TILE), ("spare", 2 * ROW_TILE)]
ROW_OFFS = {nm: (sum(n for _, n in ROW_SEGS[:i]), n) for i, (nm, n) in enumerate(ROW_SEGS)}
STACK_ROWS = sum(n for _, n in ROW_SEGS)
assert all(n % ROW_TILE == 0 for _, n in ROW_SEGS) and STACK_ROWS % (4 * ROW_TILE) == 0
Q_END, A_END, G_END, S_END = 3 * D_MODEL, 3 * D_MODEL + 2 * N_HEADS, 4 * D_MODEL + 2 * N_HEADS, 7 * D_MODEL + 2 * N_HEADS


def _flat_rows(a, nrows):
    flat = a.reshape(-1)
    return jnp.pad(flat, (0, nrows * D_MODEL - flat.shape[0])).reshape(nrows, D_MODEL)


IN_EXTRA_ROWS = 64


def _weight_wire(w_in, wp_dn, wp_sb, w_out, w_up, w_down, dn_conv, ffn_conv):
    bits = lax.bitcast_convert_type(dn_conv, BF16).reshape(-1)
    extra = jnp.pad(bits, (0, IN_EXTRA_ROWS * W_IN_COLS - bits.shape[0])).reshape(IN_EXTRA_ROWS, W_IN_COLS)
    stack = jnp.concatenate([wp_dn.astype(BF16), wp_sb.astype(BF16), w_out.astype(BF16), w_down.astype(BF16),
                             jnp.zeros((ROW_TILE, D_MODEL), BF16),
                             _flat_rows(lax.bitcast_convert_type(ffn_conv, BF16), ROW_TILE),
                             jnp.zeros((ROW_OFFS["spare"][1], D_MODEL), BF16)], axis=0)
    return [jnp.concatenate([w_in.astype(BF16), extra], axis=0)], [w_up.astype(BF16), stack]


def _col_range(g, lo, hi, width):
    parts = []
    for s in range(g.shape[0]):
        a, b = max(lo, s * width), min(hi, (s + 1) * width)
        if a < b:
            parts.append(g[s][:, a - s * width:b - s * width])
    return parts[0] if len(parts) == 1 else jnp.concatenate(parts, axis=1)


def _f32_rows(raw, k, ncols):
    raw = raw.reshape(N_CHIPS, -1)[:, :2 * k * ncols].reshape(N_CHIPS, k * ncols, 2)
    vals = lax.bitcast_convert_type(raw, F32).reshape(N_CHIPS, k, ncols)
    return vals.transpose(1, 0, 2).reshape(k, N_CHIPS * ncols)


def _unpack_early(g_in):
    w = g_in[:, :D_MODEL, :]
    return {
        "w_dnqkv": _col_range(w, 0, Q_END, W_IN_COLS),
        "w_ab": jnp.pad(_col_range(w, Q_END, A_END, W_IN_COLS), ((0, 0), (0, LANES - 2 * N_HEADS))),
        "w_dngate": _col_range(w, A_END, G_END, W_IN_COLS),
        "w_sbqkv": _col_range(w, G_END, S_END, W_IN_COLS),
        "w_gl": _col_range(w, S_END, N_CHIPS * W_IN_COLS, W_IN_COLS),
        "dn_conv": _f32_rows(g_in[:, D_MODEL:, :], DN_CONV, DN_CONV_COLS),
    }


def _unpack_late(g_up, g_stack):
    def seg(nm):
        at, n = ROW_OFFS[nm]
        return g_stack[:, at:at + n, :]

    ffn_conv = _f32_rows(seg("ffn_conv"), FFN_CONV, FFN_CONV_COLS)
    return {
        "wp_dn": seg("wp_dn").reshape(D_MODEL, D_MODEL),
        "wp_sb": seg("wp_sb").reshape(D_MODEL, D_MODEL),
        "w_out": seg("w_out").reshape(D_MODEL, D_MODEL),
        "w_up_g": _col_range(g_up, 0, D_FF, W_UP_COLS), "w_up_u": _col_range(g_up, D_FF, 2 * D_FF, W_UP_COLS),
        "w_down": seg("w_down").reshape(D_FF, D_MODEL),
        "ffn_conv_g": ffn_conv[:, :D_FF], "ffn_conv_u": ffn_conv[:, D_FF:],
    }


def _grad_wire_early(gr):
    def cols(a, ncols):
        return a.reshape(a.shape[0], N_CHIPS, ncols).transpose(1, 0, 2)

    def rows(a, nrows):
        return a.astype(BF16).reshape(N_CHIPS, nrows, a.shape[1])

    def flat(a, nrows):
        a = a.astype(BF16).reshape(N_CHIPS, -1)
        return jnp.pad(a, ((0, 0), (0, nrows * D_MODEL - a.shape[1]))).reshape(N_CHIPS, nrows, D_MODEL)

    up = [gr["w_up_g"], gr["w_up_u"]]
    g_up = jnp.stack([up[s // 2][:, (s % 2) * W_UP_COLS:(s % 2 + 1) * W_UP_COLS].astype(BF16) for s in range(N_CHIPS)])
    g_stack = jnp.concatenate([rows(gr["wp_dn"], PROJ_ROWS), rows(gr["wp_sb"], PROJ_ROWS), rows(gr["w_out"], PROJ_ROWS),
                               rows(gr["w_down"], W_DOWN_ROWS), jnp.zeros((N_CHIPS, ROW_TILE, D_MODEL), BF16),
                               flat(cols(gr["ffn_conv"], FFN_CONV_COLS), ROW_TILE),
                               jnp.zeros((N_CHIPS, ROW_OFFS["spare"][1], D_MODEL), BF16)], axis=1)
    return [g_up, g_stack]


def _grad_wire_late(gr):
    pieces = [(gr["w_dnqkv"], 0), (gr["w_ab"][:, :2 * N_HEADS], Q_END), (gr["w_dngate"], A_END),
              (gr["w_sbqkv"], G_END), (gr["w_gl"], S_END)]
    conv = gr["dn_conv"].reshape(DN_CONV, N_CHIPS, DN_CONV_COLS).transpose(1, 0, 2).reshape(N_CHIPS, -1)

    def block(s):
        lo, hi = s * W_IN_COLS, (s + 1) * W_IN_COLS
        parts = []
        for a, at in pieces:
            b0, b1 = max(lo, at), min(hi, at + a.shape[1])
            if b0 < b1:
                parts.append(a[:, b0 - at:b1 - at].astype(BF16))
        w = parts[0] if len(parts) == 1 else jnp.concatenate(parts, axis=1)
        extra = jnp.pad(conv[s].astype(BF16), (0, IN_EXTRA_ROWS * W_IN_COLS - conv.shape[1]))
        return jnp.concatenate([w, extra.reshape(IN_EXTRA_ROWS, W_IN_COLS)], axis=0)

    return [jnp.stack([block(s) for s in range(N_CHIPS)])]


def _unpack_grad_shard(r_in, r_up, r_stack):
    def seg(nm):
        at, n = ROW_OFFS[nm]
        return r_stack[at:at + n, :]

    return {
        "w_in": r_in[:D_MODEL], "w_up": r_up,
        "wp_dn": seg("wp_dn"), "wp_sb": seg("wp_sb"), "w_out": seg("w_out"), "w_down": seg("w_down"),
        "dn_conv": r_in[D_MODEL:].reshape(-1)[:DN_CONV * DN_CONV_COLS].reshape(DN_CONV, DN_CONV_COLS),
        "ffn_conv": seg("ffn_conv").reshape(-1)[:FFN_CONV * FFN_CONV_COLS].reshape(FFN_CONV, FFN_CONV_COLS),
    }


def _lane_row(v):
    return jnp.pad(v.reshape(1, -1), ((0, 0), (0, LANES - v.size)))


def kernel(x, norm1_w, w_in, dn_conv_w, dn_A_log, dn_dt_bias, dn_norm_w, w_proj_dn, w_proj_sb, w_out, norm2_w, ffn_w_up, ffn_conv_w, ffn_w_down, norm_f_w, loss_target, m_norm1_w, m_w_in, m_dn_conv_w, m_dn_A_log, m_dn_dt_bias, m_dn_norm_w, m_w_proj_dn, m_w_proj_sb, m_w_out, m_norm2_w, m_ffn_w_up, m_ffn_conv_w, m_ffn_w_down, m_norm_f_w, v_norm1_w, v_w_in, v_dn_conv_w, v_dn_A_log, v_dn_dt_bias, v_dn_norm_w, v_w_proj_dn, v_w_proj_sb, v_w_out, v_norm2_w, v_ffn_w_up, v_ffn_conv_w, v_ffn_w_down, v_norm_f_w):
    early, late = _weight_wire(w_in[0], w_proj_dn[0], w_proj_sb[0], w_out[0], ffn_w_up[0], ffn_w_down[0],
                               dn_conv_w[0], ffn_conv_w[0])
    chip_idx = (2 * lax.axis_index("x") + lax.axis_index("y")).astype(jnp.int32)

    def with_mine(gathered, wire):
        return [lax.dynamic_update_slice(g, mine[None], (chip_idx, 0, 0)) for g, mine in zip(gathered, wire)]

    wts = _unpack_early(*with_mine(_gather_shards(early), early))
    wts.update(norm1=norm1_w, norm2=norm2_w, normf=norm_f_w.reshape(1, D_MODEL), dn_norm=dn_norm_w,
               alog=_lane_row(dn_A_log), dtb=_lane_row(dn_dt_bias))

    c_idx = lax.axis_index("c").astype(jnp.int32).reshape(1)

    def pair_sums(wire_g, tags, when):
        return [_pair_add(g, got, c_idx, tag) for g, got, tag in zip(wire_g, _pair_exchange_halves(wire_g, when), tags)]

    loss_part, grad_x, gr, (early_sums, early_arrived) = _local_step(
        x[0], loss_target[0], wts, late, lambda gathered: _unpack_late(*with_mine(gathered, late)),
        lambda grads: pair_sums(_grad_wire_early(grads), ["w_up", "rows"], "early"))

    late_sums = pair_sums(_grad_wire_late(gr), ["w_in"], "late")
    tags = ["w_in", "w_up", "rows"]
    reduced = [_sum_partials(p, got, chip_idx.reshape(1), tag)
               for p, got, tag in zip(late_sums + early_sums, list(_chip_exchange(late_sums)) + list(early_arrived), tags)]
    is_south = lax.axis_index("c") == 0
    gsh = _unpack_grad_shard(*[jnp.concatenate([jnp.where(is_south, mine, other), jnp.where(is_south, other, mine)],
                                               axis=0) for mine, other in zip(reduced, _pair_share(reduced))])

    tail = jnp.concatenate([gr["dn_norm"], gr["alog"][:, :N_HEADS], gr["dtb"][:, :N_HEADS], loss_part[:, :1]], axis=1)
    small = jnp.concatenate([gr["norm1"], gr["norm2"], gr["normf"],
                             jnp.pad(tail, ((0, 0), (0, D_MODEL - tail.shape[1]))),
                             jnp.zeros((SMALL_ROWS - 4, D_MODEL), F32)], axis=0)
    small = _small_allreduce(small)
    at = HEAD_DIM
    g_small = {"norm1_w": small[0:1], "norm2_w": small[1:2], "norm_f_w": small[2],
               "dn_norm_w": small[3:4, :at], "dn_A_log": small[3:4, at:at + N_HEADS],
               "dn_dt_bias": small[3:4, at + N_HEADS:at + 2 * N_HEADS]}
    loss = small[3, at + 2 * N_HEADS]

    big = {"w_in": (w_in, m_w_in, v_w_in, gsh["w_in"]), "dn_conv_w": (dn_conv_w, m_dn_conv_w, v_dn_conv_w, gsh["dn_conv"]),
           "w_proj_dn": (w_proj_dn, m_w_proj_dn, v_w_proj_dn, gsh["wp_dn"]),
           "w_proj_sb": (w_proj_sb, m_w_proj_sb, v_w_proj_sb, gsh["wp_sb"]),
           "w_out": (w_out, m_w_out, v_w_out, gsh["w_out"]),
           "ffn_w_up": (ffn_w_up, m_ffn_w_up, v_ffn_w_up, gsh["w_up"]),
           "ffn_conv_w": (ffn_conv_w, m_ffn_conv_w, v_ffn_conv_w, gsh["ffn_conv"]),
           "ffn_w_down": (ffn_w_down, m_ffn_w_down, v_ffn_w_down, gsh["w_down"])}
    res = {}
    for nm, (w, m, v, g) in big.items():
        d, nm_, nv_ = _adamw(w[0], g, m[0], v[0], "adamw_" + nm)
        res[nm] = (g[None], d[None], nm_[None], nv_[None])

    names = ["norm1_w", "norm2_w", "norm_f_w", "dn_norm_w", "dn_A_log", "dn_dt_bias"]
    given = {"norm1_w": (norm1_w, m_norm1_w, v_norm1_w), "norm2_w": (norm2_w, m_norm2_w, v_norm2_w),
             "norm_f_w": (norm_f_w, m_norm_f_w, v_norm_f_w), "dn_norm_w": (dn_norm_w, m_dn_norm_w, v_dn_norm_w),
             "dn_A_log": (dn_A_log, m_dn_A_log, v_dn_A_log), "dn_dt_bias": (dn_dt_bias, m_dn_dt_bias, v_dn_dt_bias)}

    def stack(k, fill):
        rows = [jnp.pad(given[nm][k].reshape(1, -1), ((0, 0), (0, D_MODEL - given[nm][k].size)),
                        constant_values=fill) for nm in names]
        return jnp.concatenate(rows + [jnp.full((SMALL_ROWS - len(names), D_MODEL), fill, F32)], axis=0)

    g_rows = jnp.concatenate(
        [jnp.pad(g_small[nm].reshape(1, -1), ((0, 0), (0, D_MODEL - g_small[nm].size))) for nm in names]
        + [jnp.zeros((SMALL_ROWS - len(names), D_MODEL), F32)], axis=0)
    d_s, m_s, v_s = _adamw(stack(0, 0.0), g_rows, stack(1, 0.0), stack(2, 1.0), "adamw_small")
    for r, nm in enumerate(names):
        shape = given[nm][0].shape
        n = given[nm][0].size
        res[nm] = (g_small[nm].reshape(shape), d_s[r, :n].reshape(shape), m_s[r, :n].reshape(shape),
                   v_s[r, :n].reshape(shape))

    order = ["norm1_w", "w_in", "dn_conv_w", "dn_A_log", "dn_dt_bias", "dn_norm_w", "w_proj_dn", "w_proj_sb",
             "w_out", "norm2_w", "ffn_w_up", "ffn_conv_w", "ffn_w_down", "norm_f_w"]
    outs = [loss, grad_x[None]]
    for k in range(4):
        outs += [res[nm][k] for nm in order]
    return tuple(outs)
```

```python
import functools

import jax
import jax.numpy as jnp
from jax import lax
from jax.experimental import pallas as pl
from jax.experimental.pallas import tpu as pltpu

F32 = jnp.float32
BF16 = jnp.bfloat16
MESH = pl.DeviceIdType.MESH

EPS = 1e-6
D_MODEL = 1024
N_HEADS = 8
HEAD_DIM = 128
DN_CONV = 4
DN_CHUNK = 64
D_FF = 2816
FFN_CONV = 3
ADAM_LR, ADAM_B1, ADAM_B2, ADAM_EPS, ADAM_WD, ADAM_STEP = 0.001, 0.9, 0.999, 1e-08, 0.01, 10

N_CHIPS = 4
LANES = 128
HALO = 8
VMEM_LIMIT = 48 * 1024 * 1024
SMALL_ROWS = 8


def _params(sem=None):
    return pltpu.CompilerParams(dimension_semantics=sem, vmem_limit_bytes=VMEM_LIMIT)


def _pick(n, target):
    best = None
    for b in range(LANES, min(n, target) + 1, LANES):
        if n % b == 0:
            best = b
    return best or n


ELEMENTWISE_COLS = 1408


def _rows(t, target=256):
    return min(t, target)


def _dot(a, b, precision=None):
    return lax.dot_general(a, b, (((1,), (0,)), ((), ())), precision=precision, preferred_element_type=F32)


def _dot_nt(a, b, precision=None):
    return lax.dot_general(a, b, (((1,), (1,)), ((), ())), precision=precision, preferred_element_type=F32)


def _dot_tn(a, b, precision=None):
    return lax.dot_general(a, b, (((0,), (0,)), ((), ())), precision=precision, preferred_element_type=F32)


def _rms(x, w):
    return x * lax.rsqrt(jnp.mean(x * x, axis=-1, keepdims=True) + EPS) * w


def _silu(x):
    return x * jax.nn.sigmoid(x)


def _softplus(x):
    return jnp.maximum(x, 0.0) + jnp.log(1.0 + jnp.exp(-jnp.abs(x)))


MM_BLOCK = 1408
MM_VMEM_BUDGET = 38 * 1024 * 1024


def _mm(a, b, *, ta=False, tb=False, add=None, out_dtype=F32, name, bm=MM_BLOCK, bn=MM_BLOCK, bk=MM_BLOCK):
    m = a.shape[1] if ta else a.shape[0]
    k = a.shape[0] if ta else a.shape[1]
    n = b.shape[0] if tb else b.shape[1]
    bm, bn = _pick(m, bm), _pick(n, bn)

    def vmem_need(bk_):
        need = 2 * (bm * bk_ * a.dtype.itemsize + bk_ * bn * b.dtype.itemsize) + 2 * bm * bn * jnp.dtype(out_dtype).itemsize
        need += 2 * bm * bn * add.dtype.itemsize if add is not None else 0
        return need + (bm * bn * 4 if bk_ < k else 0)

    bk = max((d for d in range(LANES, k + 1, LANES) if k % d == 0 and vmem_need(d) <= MM_VMEM_BUDGET),
             default=_pick(k, bk))
    nk = k // bk
    dims = (((0 if ta else 1,), (1 if tb else 0,)), ((), ()))

    def body(*refs):
        a_ref, b_ref = refs[:2]
        c_ref = refs[2] if add is not None else None
        o_ref = refs[3] if add is not None else refs[2]
        acc = refs[-1]
        kk = pl.program_id(2)
        part = lax.dot_general(a_ref[...].astype(BF16), b_ref[...].astype(BF16), dims, preferred_element_type=F32)

        def finish(r):
            if add is not None:
                r = r + c_ref[...].astype(F32)
            o_ref[...] = r.astype(out_dtype)

        if nk == 1:
            finish(part)
            return

        @pl.when(kk == 0)
        def _():
            acc[...] = part

        @pl.when(jnp.logical_and(kk > 0, kk < nk - 1))
        def _():
            acc[...] += part

        @pl.when(kk == nk - 1)
        def _():
            finish(acc[...] + part)

    a_spec = (pl.BlockSpec((bk, bm), lambda i, j, kk: (kk, i)) if ta
              else pl.BlockSpec((bm, bk), lambda i, j, kk: (i, kk)))
    b_spec = (pl.BlockSpec((bn, bk), lambda i, j, kk: (j, kk)) if tb
              else pl.BlockSpec((bk, bn), lambda i, j, kk: (kk, j)))
    o_spec = pl.BlockSpec((bm, bn), lambda i, j, kk: (i, j))
    in_specs = [a_spec, b_spec] + ([o_spec] if add is not None else [])
    args = (a, b) + ((add,) if add is not None else ())
    return pl.pallas_call(
        body, name=name, grid=(m // bm, n // bn, nk),
        in_specs=in_specs, out_specs=o_spec,
        out_shape=jax.ShapeDtypeStruct((m, n), out_dtype),
        scratch_shapes=[pltpu.VMEM((bm, bn), F32)] if nk > 1 else [],
        compiler_params=_params(("parallel", "parallel", "arbitrary")),
    )(*args)


def _norm1_fwd(x, w, w_ab):
    t = x.shape[0]
    tb = _rows(t)

    def body(x_ref, w_ref, wab_ref, n_ref, hab_ref):
        n = _rms(x_ref[...], w_ref[...]).astype(BF16)
        n_ref[...] = n
        hab_ref[...] = _dot(n, wab_ref[...])

    return pl.pallas_call(
        body, name="norm1_fwd", grid=(t // tb,),
        in_specs=[pl.BlockSpec((tb, D_MODEL), lambda i: (i, 0)),
                  pl.BlockSpec((1, D_MODEL), lambda i: (0, 0)),
                  pl.BlockSpec((D_MODEL, LANES), lambda i: (0, 0))],
        out_specs=[pl.BlockSpec((tb, D_MODEL), lambda i: (i, 0)),
                   pl.BlockSpec((tb, LANES), lambda i: (i, 0))],
        out_shape=[jax.ShapeDtypeStruct((t, D_MODEL), BF16), jax.ShapeDtypeStruct((t, LANES), F32)],
        compiler_params=_params(("arbitrary",)),
    )(x, w, w_ab)


def _norm1_bwd(x, w, dn, dres, dab, w_ab):
    t = x.shape[0]
    tb = _rows(t)

    def body(x_ref, w_ref, dn_ref, dres_ref, dab_ref, wab_ref, dx_ref, dw_ref):
        i = pl.program_id(0)
        g = dn_ref[...] + _dot_nt(dab_ref[...].astype(BF16), wab_ref[...])
        _, vjp = jax.vjp(_rms, x_ref[...], w_ref[...])
        dx, dw = vjp(g)
        dx_ref[...] = dres_ref[...] + dx

        @pl.when(i == 0)
        def _():
            dw_ref[...] = jnp.zeros_like(dw_ref)

        dw_ref[...] += dw

    row = pl.BlockSpec((tb, D_MODEL), lambda i: (i, 0))
    vec = pl.BlockSpec((1, D_MODEL), lambda i: (0, 0))
    return pl.pallas_call(
        body, name="norm1_bwd", grid=(t // tb,),
        in_specs=[row, vec, row, row, pl.BlockSpec((tb, LANES), lambda i: (i, 0)),
                  pl.BlockSpec((D_MODEL, LANES), lambda i: (0, 0))],
        out_specs=[row, vec],
        out_shape=[jax.ShapeDtypeStruct((t, D_MODEL), F32), jax.ShapeDtypeStruct((1, D_MODEL), F32)],
        compiler_params=_params(("arbitrary",)),
    )(x, w, dn, dres, dab, w_ab)


def _conv_fwd(x, w, name):
    t, c = x.shape
    kk = w.shape[0]
    tb, cb = _rows(t, 512), _pick(c, ELEMENTWISE_COLS)
    per = tb // HALO

    def body(x_ref, halo_ref, w_ref, y_ref, buf):
        i = pl.program_id(0)
        buf[pl.ds(HALO, tb), :] = x_ref[...]
        buf[pl.ds(0, HALO), :] = jnp.where(i == 0, 0.0, halo_ref[...])
        y_ref[...] = _conv_taps(buf, w_ref, HALO - (kk - 1), tb)

    return pl.pallas_call(
        body, name=name, grid=(t // tb, c // cb),
        in_specs=[pl.BlockSpec((tb, cb), lambda i, j: (i, j)),
                  pl.BlockSpec((HALO, cb), lambda i, j: (jnp.maximum(i * per - 1, 0), j)),
                  pl.BlockSpec((kk, cb), lambda i, j: (0, j))],
        out_specs=pl.BlockSpec((tb, cb), lambda i, j: (i, j)),
        out_shape=jax.ShapeDtypeStruct((t, c), F32),
        scratch_shapes=[pltpu.VMEM((tb + HALO, cb), F32)],
        compiler_params=_params(("parallel", "parallel")),
    )(x, x, w)


def _conv_bwd(dy, x, w, name, dx_dtype):
    t, c = x.shape
    kk = w.shape[0]
    tb, cb = _rows(t, 512), _pick(c, ELEMENTWISE_COLS)
    per = tb // HALO
    nblk = t // tb

    def body(dy_ref, after_ref, x_ref, w_ref, dx_ref, dw_ref, dbuf):
        i = pl.program_id(1)
        dbuf[pl.ds(0, tb), :] = dy_ref[...]
        dbuf[pl.ds(tb, HALO), :] = jnp.where(i == nblk - 1, 0.0, after_ref[...])

        @pl.when(i == 0)
        def _():
            dw_ref[...] = jnp.zeros_like(dw_ref)

        for j in range(cb // LANES):
            sl = pl.ds(j * LANES, LANES)
            x = x_ref[:, sl]
            dx = None
            for s in range(kk):
                shifted = dbuf[pl.ds(kk - 1 - s, tb), sl]
                term = w_ref[s:s + 1, sl] * shifted
                dx = term if dx is None else dx + term
                dw_ref[s:s + 1, sl] += jnp.sum(shifted * x, axis=0, keepdims=True)
            dx_ref[:, sl] = dx.astype(dx_dtype)

    blk = pl.BlockSpec((tb, cb), lambda j, i: (i, j))
    return pl.pallas_call(
        body, name=name, grid=(c // cb, nblk),
        in_specs=[blk,
                  pl.BlockSpec((HALO, cb), lambda j, i: (jnp.minimum((i + 1) * per, t // HALO - 1), j)),
                  blk,
                  pl.BlockSpec((kk, cb), lambda j, i: (0, j))],
        out_specs=[blk, pl.BlockSpec((HALO, cb), lambda j, i: (0, j))],
        out_shape=[jax.ShapeDtypeStruct((t, c), dx_dtype), jax.ShapeDtypeStruct((HALO, c), F32)],
        scratch_shapes=[pltpu.VMEM((tb + HALO, cb), F32)],
        compiler_params=_params(("parallel", "arbitrary")),
    )(dy, dy, x, w)


def _dn_head(c, normed):
    s = _silu(c)
    return s * lax.rsqrt(jnp.sum(s * s, axis=-1, keepdims=True) + EPS) if normed else s


def _dn_gates(hab, alog, dtb):
    lane = lax.broadcasted_iota(jnp.int32, hab.shape, 1)
    g = -jnp.exp(alog) * _softplus(hab + dtb)
    beta = jax.nn.sigmoid(hab)
    return jnp.where(lane < N_HEADS, g, jnp.where(lane < 2 * N_HEADS, beta, 0.0))


def _dn_head_slices(q_ref, k_ref, v_ref):
    return [(pl.ds((part * N_HEADS + h) * HEAD_DIM, HEAD_DIM), ref, h, part < 2)
            for part, ref in enumerate((q_ref, k_ref, v_ref)) for h in range(N_HEADS)]


def _dn_prep_fwd(c, hab, alog, dtb):
    t = c.shape[0]
    tb = _rows(t)

    def body(c_ref, hab_ref, alog_ref, dtb_ref, q_ref, k_ref, v_ref, gb_ref):
        for sl, ref, h, normed in _dn_head_slices(q_ref, k_ref, v_ref):
            ref[h] = _dn_head(c_ref[:, sl], normed)
        gb_ref[...] = _dn_gates(hab_ref[...], alog_ref[...], dtb_ref[...])

    hm = pl.BlockSpec((N_HEADS, tb, HEAD_DIM), lambda i: (0, i, 0))
    nar = pl.BlockSpec((tb, LANES), lambda i: (i, 0))
    vec = pl.BlockSpec((1, LANES), lambda i: (0, 0))
    return pl.pallas_call(
        body, name="dn_prep_fwd", grid=(t // tb,),
        in_specs=[pl.BlockSpec((tb, 3 * D_MODEL), lambda i: (i, 0)), nar, vec, vec],
        out_specs=[hm, hm, hm, nar],
        out_shape=[jax.ShapeDtypeStruct((N_HEADS, t, HEAD_DIM), F32)] * 3 + [jax.ShapeDtypeStruct((t, LANES), F32)],
        compiler_params=_params(("parallel",)),
    )(c, hab, alog, dtb)


def _dn_prep_bwd(c, hab, alog, dtb, dq, dk, dv, dgb):
    t = c.shape[0]
    tb = _rows(t)

    def body(c_ref, hab_ref, alog_ref, dtb_ref, dq_ref, dk_ref, dv_ref, dgb_ref,
             dc_ref, dhab_ref, dalog_ref, ddtb_ref):
        i = pl.program_id(0)
        for sl, ref, h, normed in _dn_head_slices(dq_ref, dk_ref, dv_ref):
            _, vjp = jax.vjp(functools.partial(_dn_head, normed=normed), c_ref[:, sl])
            dc_ref[:, sl] = vjp(ref[h])[0]
        _, vjp = jax.vjp(_dn_gates, hab_ref[...], alog_ref[...], dtb_ref[...])
        dhab, dalog, ddtb = vjp(dgb_ref[...])
        dhab_ref[...] = dhab

        @pl.when(i == 0)
        def _():
            dalog_ref[...] = jnp.zeros_like(dalog_ref)
            ddtb_ref[...] = jnp.zeros_like(ddtb_ref)

        dalog_ref[...] += dalog
        ddtb_ref[...] += ddtb

    hm = pl.BlockSpec((N_HEADS, tb, HEAD_DIM), lambda i: (0, i, 0))
    wide = pl.BlockSpec((tb, 3 * D_MODEL), lambda i: (i, 0))
    nar = pl.BlockSpec((tb, LANES), lambda i: (i, 0))
    vec = pl.BlockSpec((1, LANES), lambda i: (0, 0))
    return pl.pallas_call(
        body, name="dn_prep_bwd", grid=(t // tb,),
        in_specs=[wide, nar, vec, vec, hm, hm, hm, nar],
        out_specs=[wide, nar, vec, vec],
        out_shape=[jax.ShapeDtypeStruct((t, 3 * D_MODEL), F32), jax.ShapeDtypeStruct((t, LANES), F32),
                   jax.ShapeDtypeStruct((1, LANES), F32), jax.ShapeDtypeStruct((1, LANES), F32)],
        compiler_params=_params(("arbitrary",)),
    )(c, hab, alog, dtb, dq, dk, dv, dgb)


DN_PREC = lax.Precision.HIGH
DN_GROUP = 16


def _dn_prec(a):
    return DN_PREC if a.dtype == F32 else None


def _bdot(a, b):
    return lax.dot_general(a, b, (((2,), (1,)), ((0,), (0,))), precision=_dn_prec(a), preferred_element_type=F32)


def _bdot_nt(a, b):
    return lax.dot_general(a, b, (((2,), (2,)), ((0,), (0,))), precision=_dn_prec(a), preferred_element_type=F32)


def _bdot_tn(a, b):
    return lax.dot_general(a, b, (((1,), (1,)), ((0,), (0,))), precision=_dn_prec(a), preferred_element_type=F32)


def _unit_lower_inverse(lmat):
    c = lmat.shape[-1]
    ri = lax.broadcasted_iota(jnp.int32, (c, c), 0)
    ci = lax.broadcasted_iota(jnp.int32, (c, c), 1)
    p = -lmat
    tinv = jnp.where(ri == ci, 1.0, 0.0) + p
    for _ in range(max(c.bit_length() - 2, 0)):
        p = _bdot(p, p)
        tinv = tinv + _bdot(tinv, p)
    return tinv


@jax.custom_vjp
def _solve_with(lmat, rhs, tinv):
    return _bdot(tinv, rhs)


def _solve_with_fwd(lmat, rhs, tinv):
    sol = _bdot(tinv, rhs)
    return sol, (sol, tinv)


def _solve_with_bwd(res, dsol):
    sol, tinv = res
    drhs = _bdot_tn(tinv, dsol)
    return -_bdot_nt(drhs, sol), drhs, jnp.zeros_like(tinv)


_solve_with.defvjp(_solve_with_fwd, _solve_with_bwd)


def _dn_local(q, k, v, grow, brow, tinv):
    g, c, _ = q.shape
    ri = lax.broadcasted_iota(jnp.int32, (c, c), 0)
    ci = lax.broadcasted_iota(jnp.int32, (c, c), 1)
    lower = ri >= ci
    as_col = lambda r: jnp.sum(jnp.where(ri == ci, jnp.broadcast_to(r, (g, c, c)), 0.0), axis=2, keepdims=True)
    gcol, bcol = as_col(grow), as_col(brow)
    gc_col = jnp.sum(jnp.where(lower, jnp.broadcast_to(grow, (g, c, c)), 0.0), axis=2, keepdims=True)
    gc_row = jnp.sum(jnp.where(ri <= ci, jnp.broadcast_to(gcol, (g, c, c)), 0.0), axis=1, keepdims=True)
    qs = q * (HEAD_DIM ** -0.5)
    kb = k * bcol
    vb = v * bcol
    decay = jnp.where(lower, jnp.exp(jnp.where(lower, gc_col - gc_row, 0.0)), 0.0)
    lmat = jnp.where(ri > ci, _bdot_nt(kb.astype(BF16), k.astype(BF16)) * decay, 0.0)
    eg = jnp.exp(gc_col)
    rhs = jnp.concatenate([vb, kb * eg], axis=2)
    if tinv is None:
        tinv = _unit_lower_inverse(lmat)
    sol = _solve_with(lmat, rhs, tinv)
    a_qk = jnp.where(lower, _bdot_nt(qs.astype(BF16), k.astype(BF16)) * decay, 0.0)
    g_last = jnp.sum(grow, axis=2, keepdims=True)
    kdec = k * jnp.exp(g_last - gc_col)
    egl = jnp.broadcast_to(jnp.exp(g_last), (g, 1, HEAD_DIM))
    return sol[:, :, :HEAD_DIM], sol[:, :, HEAD_DIM:], a_qk, qs * eg, kdec, egl, tinv


def _dn_seq(u, w, a_qk, qe, kdec, egl, s_in):
    b16 = lambda x: x.astype(BF16)
    v_new = u - _bdot(b16(w), b16(s_in))
    o = _bdot(b16(qe), b16(s_in)) + _bdot(b16(a_qk), b16(v_new))
    return o, s_in * egl + _bdot_tn(b16(kdec), b16(v_new))


def _dn_local_specs(t):
    grp = min(DN_GROUP, t // DN_CHUNK)
    rows = grp * DN_CHUNK
    blk = pl.BlockSpec((1, rows, HEAD_DIM), lambda h, i: (h, i, 0))
    row = pl.BlockSpec((1, grp, 1, DN_CHUNK), lambda h, i: (h, i, 0, 0))
    sq = pl.BlockSpec((1, grp, DN_CHUNK, DN_CHUNK), lambda h, i: (h, i, 0, 0))
    lane = pl.BlockSpec((1, grp, 1, HEAD_DIM), lambda h, i: (h, i, 0, 0))
    return grp, blk, row, sq, lane


def _dn_shapes(t):
    nchunk = t // DN_CHUNK
    big = jax.ShapeDtypeStruct((N_HEADS, t, HEAD_DIM), F32)
    row = jax.ShapeDtypeStruct((N_HEADS, nchunk, 1, DN_CHUNK), F32)
    sq = jax.ShapeDtypeStruct((N_HEADS, nchunk, DN_CHUNK, DN_CHUNK), F32)
    lane = jax.ShapeDtypeStruct((N_HEADS, nchunk, 1, HEAD_DIM), F32)
    return big, row, sq, lane


def _dn_local_fwd(q, k, v, grow, brow, wire=()):
    t = q.shape[1]
    grp, blk, row, sq, lane = _dn_local_specs(t)
    big, _, sqs, lanes = _dn_shapes(t)
    n = len(wire)
    groups = t // (grp * DN_CHUNK)
    steps = N_HEADS * groups

    def body(q_ref, k_ref, v_ref, gr_ref, br_ref, *rest):
        u_ref, w_ref, a_ref, qe_ref, kd_ref, egl_ref, t_ref = rest[n:n + 7]
        if n:
            begin, middle, end = _gather_protocol(rest[:n], rest[n + 7:2 * n + 7], *rest[2 * n + 7:])
            step = pl.program_id(0) * groups + pl.program_id(1)
            pl.when(step == 0)(begin)
            pl.when(step == (GATHER_FORWARD_AT * steps) // 8)(middle)
        split = lambda r: r[0].reshape(grp, DN_CHUNK, HEAD_DIM)
        u, w, a_qk, qe, kdec, egl, tinv = _dn_local(split(q_ref), split(k_ref), split(v_ref), gr_ref[0],
                                                     br_ref[0], None)
        for ref, val in ((u_ref, u), (w_ref, w), (qe_ref, qe), (kd_ref, kdec)):
            ref[0] = val.reshape(grp * DN_CHUNK, HEAD_DIM)
        a_ref[0] = a_qk
        egl_ref[0] = egl
        t_ref[0] = tinv
        if n:
            pl.when(step == steps - 1)(end)

    assert n == 0 or steps >= 3
    return pl.pallas_call(
        body, name="dn_local_fwd", grid=(N_HEADS, groups),
        in_specs=[blk, blk, blk, row, row] + _hbm_specs(n),
        out_specs=[blk, blk, sq, blk, blk, lane, sq] + _hbm_specs(n),
        out_shape=[big, big, sqs, big, big, lanes, sqs] + _gather_out_shapes(wire),
        scratch_shapes=_gather_sems(n) if n else [],
        compiler_params=_params(("arbitrary", "arbitrary")),
    )(q, k, v, grow, brow, *wire)


def _dn_local_bwd(q, k, v, grow, brow, tinv, du, dw, da, dqe, dkd, degl, partials=()):
    t = q.shape[1]
    grp, blk, row, sq, lane = _dn_local_specs(t)
    big, rows_, _, _ = _dn_shapes(t)
    n = len(partials)
    groups = t // (grp * DN_CHUNK)
    steps = N_HEADS * groups

    def body(q_ref, k_ref, v_ref, gr_ref, br_ref, t_ref, du_ref, dw_ref, da_ref, dqe_ref, dkd_ref,
             degl_ref, *rest):
        dq_ref, dk_ref, dv_ref, dgr_ref, dbr_ref = rest[n:n + 5]
        if n:
            begin, end = _chip_exchange_protocol(rest[:n], rest[n + 5:2 * n + 5], *rest[2 * n + 5:])
            step = pl.program_id(0) * groups + pl.program_id(1)
            pl.when(step == 0)(begin)
        split = lambda r: r[0].reshape(grp, DN_CHUNK, HEAD_DIM)
        tinv_v = t_ref[0]
        fn = lambda q_, k_, v_, gr_, br_: _dn_local(q_, k_, v_, gr_, br_, tinv_v)[:6]
        _, vjp = jax.vjp(fn, split(q_ref), split(k_ref), split(v_ref), gr_ref[0], br_ref[0])
        dq, dk, dv, dgr, dbr = vjp((split(du_ref), split(dw_ref), da_ref[0], split(dqe_ref), split(dkd_ref),
                                    degl_ref[0]))
        for ref, val in ((dq_ref, dq), (dk_ref, dk), (dv_ref, dv)):
            ref[0] = val.reshape(grp * DN_CHUNK, HEAD_DIM)
        dgr_ref[0] = dgr
        dbr_ref[0] = dbr
        if n:
            pl.when(step == steps - 1)(end)

    assert n == 0 or steps >= 2
    return pl.pallas_call(
        body, name="dn_local_bwd", grid=(N_HEADS, groups),
        in_specs=[blk, blk, blk, row, row, sq, blk, blk, sq, blk, blk, lane] + _hbm_specs(n),
        out_specs=[blk, blk, blk, row, row] + _hbm_specs(n),
        out_shape=[big, big, big, rows_, rows_] + _chip_exchange_shapes(partials),
        scratch_shapes=_chip_exchange_sems(n) if n else [],
        compiler_params=_params(("arbitrary", "arbitrary")),
    )(q, k, v, grow, brow, tinv, du, dw, da, dqe, dkd, degl, *partials)


DN_SEQ_CHUNKS = 4


def _dn_seq_specs(nchunk, rev):
    per = min(DN_SEQ_CHUNKS, nchunk)
    nstep = nchunk // per

    def idx(n):
        return nstep - 1 - n if rev else n

    blk = pl.BlockSpec((N_HEADS, per * DN_CHUNK, HEAD_DIM), lambda n: (0, idx(n), 0))
    sq = pl.BlockSpec((N_HEADS, per, DN_CHUNK, DN_CHUNK), lambda n: (0, idx(n), 0, 0))
    lane = pl.BlockSpec((N_HEADS, per, 1, HEAD_DIM), lambda n: (0, idx(n), 0, 0))
    st = pl.BlockSpec((N_HEADS, per, HEAD_DIM, HEAD_DIM), lambda n: (0, idx(n), 0, 0))
    return per, nstep, blk, sq, lane, st


def _dn_seq_fwd(u, w, a_qk, qe, kdec, egl):
    t = u.shape[1]
    nchunk = t // DN_CHUNK
    per, nstep, blk, sq, lane, st = _dn_seq_specs(nchunk, False)

    def body(u_ref, w_ref, a_ref, qe_ref, kd_ref, egl_ref, o_ref, s_ref, state):
        @pl.when(pl.program_id(0) == 0)
        def _():
            state[...] = jnp.zeros_like(state)

        for c in range(per):
            rows = pl.ds(c * DN_CHUNK, DN_CHUNK)
            s_in = state[...]
            s_ref[:, c] = s_in
            o_ref[:, rows], state[...] = _dn_seq(u_ref[:, rows], w_ref[:, rows], a_ref[:, c], qe_ref[:, rows],
                                                 kd_ref[:, rows], egl_ref[:, c], s_in)

    return pl.pallas_call(
        body, name="dn_seq_fwd", grid=(nstep,),
        in_specs=[blk, blk, sq, blk, blk, lane],
        out_specs=[blk, st],
        out_shape=[jax.ShapeDtypeStruct((N_HEADS, t, HEAD_DIM), F32),
                   jax.ShapeDtypeStruct((N_HEADS, nchunk, HEAD_DIM, HEAD_DIM), F32)],
        scratch_shapes=[pltpu.VMEM((N_HEADS, HEAD_DIM, HEAD_DIM), F32)],
        compiler_params=_params(("arbitrary",)),
    )(u, w, a_qk, qe, kdec, egl)


def _dn_seq_bwd(u, w, a_qk, qe, kdec, egl, states, do):
    t = u.shape[1]
    nchunk = t // DN_CHUNK
    per, nstep, blk, sq, lane, st = _dn_seq_specs(nchunk, True)
    big, _, sqs, lanes = _dn_shapes(t)

    def body(u_ref, w_ref, a_ref, qe_ref, kd_ref, egl_ref, s_ref, do_ref,
             du_ref, dw_ref, da_ref, dqe_ref, dkd_ref, degl_ref, dstate):
        @pl.when(pl.program_id(0) == 0)
        def _():
            dstate[...] = jnp.zeros_like(dstate)

        for c in reversed(range(per)):
            rows = pl.ds(c * DN_CHUNK, DN_CHUNK)
            _, vjp = jax.vjp(_dn_seq, u_ref[:, rows], w_ref[:, rows], a_ref[:, c], qe_ref[:, rows], kd_ref[:, rows],
                             egl_ref[:, c], s_ref[:, c])
            (du_ref[:, rows], dw_ref[:, rows], da_ref[:, c], dqe_ref[:, rows], dkd_ref[:, rows], degl_ref[:, c],
             dstate[...]) = vjp((do_ref[:, rows], dstate[...]))

    return pl.pallas_call(
        body, name="dn_seq_bwd", grid=(nstep,),
        in_specs=[blk, blk, sq, blk, blk, lane, st, blk],
        out_specs=[blk, blk, sq, blk, blk, lane],
        out_shape=[big, big, sqs, big, big, lanes],
        scratch_shapes=[pltpu.VMEM((N_HEADS, HEAD_DIM, HEAD_DIM), F32)],
        compiler_params=_params(("arbitrary",)),
    )(u, w, a_qk, qe, kdec, egl, states, do)


def _dn_post_head(o, gate, w):
    return _rms(o, w) * _silu(gate)


def _dn_post_fwd(o, gate, w):
    t = gate.shape[0]
    tb = _rows(t)

    def body(o_ref, g_ref, w_ref, y_ref):
        for h in range(N_HEADS):
            sl = pl.ds(h * HEAD_DIM, HEAD_DIM)
            y_ref[:, sl] = _dn_post_head(o_ref[h], g_ref[:, sl], w_ref[...]).astype(BF16)

    row = pl.BlockSpec((tb, D_MODEL), lambda i: (i, 0))
    hm = pl.BlockSpec((N_HEADS, tb, HEAD_DIM), lambda i: (0, i, 0))
    return pl.pallas_call(
        body, name="dn_post_fwd", grid=(t // tb,),
        in_specs=[hm, row, pl.BlockSpec((1, HEAD_DIM), lambda i: (0, 0))],
        out_specs=row, out_shape=jax.ShapeDtypeStruct((t, D_MODEL), BF16),
        compiler_params=_params(("parallel",)),
    )(o, gate, w)


def _dn_post_bwd(o, gate, w, dy):
    t = gate.shape[0]
    tb = _rows(t)

    def body(o_ref, g_ref, w_ref, dy_ref, do_ref, dg_ref, dw_ref):
        i = pl.program_id(0)
        @pl.when(i == 0)
        def _():
            dw_ref[...] = jnp.zeros_like(dw_ref)

        for h in range(N_HEADS):
            sl = pl.ds(h * HEAD_DIM, HEAD_DIM)
            _, vjp = jax.vjp(_dn_post_head, o_ref[h], g_ref[:, sl], w_ref[...])
            do_ref[h], dg, dw = vjp(dy_ref[:, sl])
            dg_ref[:, sl] = dg.astype(BF16)
            dw_ref[...] += dw

    row = pl.BlockSpec((tb, D_MODEL), lambda i: (i, 0))
    hm = pl.BlockSpec((N_HEADS, tb, HEAD_DIM), lambda i: (0, i, 0))
    vec = pl.BlockSpec((1, HEAD_DIM), lambda i: (0, 0))
    return pl.pallas_call(
        body, name="dn_post_bwd", grid=(t // tb,),
        in_specs=[hm, row, vec, row],
        out_specs=[hm, row, vec],
        out_shape=[jax.ShapeDtypeStruct((N_HEADS, t, HEAD_DIM), F32), jax.ShapeDtypeStruct((t, D_MODEL), BF16),
                   jax.ShapeDtypeStruct((1, HEAD_DIM), F32)],
        compiler_params=_params(("arbitrary",)),
    )(o, gate, w, dy)


def _split_bf16(x):
    hi = x.astype(BF16)
    lo = (x - hi.astype(F32)).astype(BF16)
    return hi, lo


SB_Q_BLOCK = 512
SB_K_BLOCK = 256
SB_NEGLIGIBLE = -60.0


def _sb_logits(q, kb, mask, scale):
    z = _dot_nt(q, kb) * scale
    ls = jnp.minimum(z, 0.0) - jnp.log(1.0 + jnp.exp(-jnp.abs(z)))
    lk = ls - z
    if mask is not None:
        lk = jnp.where(mask, lk, 0.0)
    return ls, lk


def _sb_blocks(t):
    bq = min(SB_Q_BLOCK, t)
    bk = min(SB_K_BLOCK, bq)
    return bq, bk, bq // bk


def _sb_fwd(qkv):
    t = qkv.shape[0]
    bq, bk, nd = _sb_blocks(t)
    scale = HEAD_DIM ** -0.5

    def body(q_ref, k_ref, v_ref, o_ref, tot_ref, used_ref):
        i = pl.program_id(1)
        q = q_ref[...]
        rj = lax.broadcasted_iota(jnp.int32, (bk, bk), 0)
        cj = lax.broadcasted_iota(jnp.int32, (bk, bk), 1)
        after = (rj > cj).astype(BF16)
        trow = lax.broadcasted_iota(jnp.int32, (bq, bk), 0)
        scol = lax.broadcasted_iota(jnp.int32, (bq, bk), 1)

        def tile(j, run, acc, mask):
            off = pl.multiple_of(j * bk, bk)
            kb = k_ref[pl.ds(off, bk), :]
            vb = v_ref[pl.ds(off, bk), :]
            ls, lk = _sb_logits(q, kb, mask, scale)
            hi, lo = _split_bf16(lk)
            between = _dot(hi, after) + _dot(lo, after) + run
            a = jnp.exp(ls + between)
            if mask is not None:
                a = jnp.where(mask, a, 0.0)
            acc = acc + _dot(a.astype(BF16), vb)
            return run + jnp.sum(lk, axis=1, keepdims=True), acc

        run, acc = jnp.zeros((bq, 1), F32), jnp.zeros((bq, HEAD_DIM), F32)
        for d in reversed(range(nd)):
            run, acc = tile(i * nd + d, run, acc, scol + d * bk < trow)
        def more(c):
            return jnp.logical_and(c[0] < i * nd, jnp.max(c[1]) > SB_NEGLIGIBLE)

        def far(c):
            run_, acc_ = tile(i * nd - 1 - c[0], c[1], c[2], None)
            return c[0] + 1, run_, acc_

        used, run, acc = lax.while_loop(more, far, (jnp.int32(0), run, acc))
        o_ref[...] = acc.astype(BF16)
        tot_ref[...] = jnp.broadcast_to(run, (bq, HEAD_DIM))
        used_ref[...] = jnp.full(used_ref.shape, used, F32)

    qs = pl.BlockSpec((bq, HEAD_DIM), lambda h, i: (i, h))
    ks = pl.BlockSpec((t, HEAD_DIM), lambda h, i: (0, N_HEADS + h))
    vs = pl.BlockSpec((t, HEAD_DIM), lambda h, i: (0, 2 * N_HEADS + h))
    return pl.pallas_call(
        body, name="sb_fwd", grid=(N_HEADS, t // bq),
        in_specs=[qs, ks, vs], out_specs=[qs, qs, pl.BlockSpec((1, 1, 1, LANES), lambda h, i: (h, i, 0, 0))],
        out_shape=[jax.ShapeDtypeStruct((t, D_MODEL), BF16), jax.ShapeDtypeStruct((t, D_MODEL), F32),
                   jax.ShapeDtypeStruct((N_HEADS, t // bq, 1, LANES), F32)],
        compiler_params=_params(("parallel", "arbitrary")),
    )(qkv, qkv, qkv)


def _sb_bwd(qkv, tot, used, do):
    t = qkv.shape[0]
    bq, bk, nd = _sb_blocks(t)
    scale = HEAD_DIM ** -0.5

    def body(q_ref, k_ref, v_ref, tot_ref, used_ref, do_ref, dq_ref, dk_ref, dv_ref):
        i = pl.program_id(1)

        @pl.when(i == 0)
        def _():
            dk_ref[...] = jnp.zeros_like(dk_ref)
            dv_ref[...] = jnp.zeros_like(dv_ref)

        q = q_ref[...]
        do = do_ref[...]
        total = tot_ref[:, 0:1]
        rj = lax.broadcasted_iota(jnp.int32, (bk, bk), 0)
        cj = lax.broadcasted_iota(jnp.int32, (bk, bk), 1)
        upto = (rj <= cj).astype(BF16)
        before = (rj < cj).astype(BF16)
        trow = lax.broadcasted_iota(jnp.int32, (bq, bk), 0)
        scol = lax.broadcasted_iota(jnp.int32, (bq, bk), 1)

        def tile(j, run_k, run_e, dq, mask):
            off = pl.multiple_of(j * bk, bk)
            kb = k_ref[pl.ds(off, bk), :]
            vb = v_ref[pl.ds(off, bk), :]
            ls, lk = _sb_logits(q, kb, mask, scale)
            hi, lo = _split_bf16(lk)
            between = total - (_dot(hi, upto) + _dot(lo, upto) + run_k)
            a = jnp.exp(ls + between)
            if mask is not None:
                a = jnp.where(mask, a, 0.0)
            e = a * _dot_nt(do, vb)
            ehi, elo = _split_bf16(e)
            pre = _dot(ehi, before) + _dot(elo, before) + run_e
            sig = jnp.exp(ls)
            dz = e * (1.0 - sig) - pre * sig
            if mask is not None:
                dz = jnp.where(mask, dz, 0.0)
            dz = (dz * scale).astype(BF16)
            dq = dq + _dot(dz, kb)
            dk_ref[pl.ds(off, bk), :] += _dot_tn(dz, q)
            dv_ref[pl.ds(off, bk), :] += _dot_tn(a.astype(BF16), do)
            return (run_k + jnp.sum(lk, axis=1, keepdims=True),
                    run_e + jnp.sum(e, axis=1, keepdims=True), dq)

        zero = jnp.zeros((bq, 1), F32)
        visited = jnp.clip(jnp.max(used_ref[...]).astype(jnp.int32), 0, i * nd)
        carry = lax.fori_loop(i * nd - visited, i * nd, lambda j, c: tile(j, c[0], c[1], c[2], None),
                              (zero, zero, jnp.zeros((bq, HEAD_DIM), F32)))
        for d in range(nd):
            carry = tile(i * nd + d, *carry, scol + d * bk < trow)
        dq_ref[...] = carry[2]

    qs = pl.BlockSpec((bq, HEAD_DIM), lambda h, i: (i, h))
    ks = pl.BlockSpec((t, HEAD_DIM), lambda h, i: (0, N_HEADS + h))
    vs = pl.BlockSpec((t, HEAD_DIM), lambda h, i: (0, 2 * N_HEADS + h))
    full = pl.BlockSpec((t, HEAD_DIM), lambda h, i: (0, h))
    big = jax.ShapeDtypeStruct((t, D_MODEL), F32)
    return pl.pallas_call(
        body, name="sb_bwd", grid=(N_HEADS, t // bq),
        in_specs=[qs, ks, vs, qs, pl.BlockSpec((1, 1, 1, LANES), lambda h, i: (h, i, 0, 0)), qs],
        out_specs=[qs, full, full],
        out_shape=[big, big, big],
        compiler_params=_params(("parallel", "arbitrary")),
    )(qkv, qkv, qkv, tot, used, do)


def _merge_fwd(o_dn, o_sb, gl, x, wp_dn, wp_sb, w_out, w2):
    t = x.shape[0]
    tb = _rows(t)

    def body(odn_ref, osb_ref, gl_ref, x_ref, wpd_ref, wps_ref, wo_ref, w2_ref,
             pdn_ref, psb_ref, mix_ref, x1_ref, n2_ref):
        pdn = _dot(odn_ref[...], wpd_ref[...])
        psb = _dot(osb_ref[...], wps_ref[...])
        gates = jax.nn.sigmoid(gl_ref[...])
        mixed = (gates[:, :D_MODEL] * pdn + gates[:, D_MODEL:] * psb).astype(BF16)
        x1 = x_ref[...] + _dot(mixed, wo_ref[...])
        pdn_ref[...] = pdn.astype(BF16)
        psb_ref[...] = psb.astype(BF16)
        mix_ref[...] = mixed
        x1_ref[...] = x1
        n2_ref[...] = _rms(x1, w2_ref[...]).astype(BF16)

    row = pl.BlockSpec((tb, D_MODEL), lambda i: (i, 0))
    sq = pl.BlockSpec((D_MODEL, D_MODEL), lambda i: (0, 0))
    f = jax.ShapeDtypeStruct((t, D_MODEL), F32)
    b = jax.ShapeDtypeStruct((t, D_MODEL), BF16)
    return pl.pallas_call(
        body, name="merge_fwd", grid=(t // tb,),
        in_specs=[row, row, pl.BlockSpec((tb, 2 * D_MODEL), lambda i: (i, 0)), row, sq, sq, sq,
                  pl.BlockSpec((1, D_MODEL), lambda i: (0, 0))],
        out_specs=[row] * 5, out_shape=[b, b, b, f, b],
        compiler_params=_params(("parallel",)),
    )(o_dn, o_sb, gl, x, wp_dn, wp_sb, w_out, w2)


def _merge_bwd(dx2, dn2, x1, w2, gl, pdn, psb, wp_dn, wp_sb, w_out):
    t = x1.shape[0]
    tb = _rows(t)

    def body(dx2_ref, dn2_ref, x1_ref, w2_ref, gl_ref, pdn_ref, psb_ref, wpd_ref, wps_ref, wo_ref,
             dx1_ref, dw2_ref, dgl_ref, dpdn_ref, dpsb_ref, dodn_ref, dosb_ref):
        i = pl.program_id(0)
        _, vjp = jax.vjp(_rms, x1_ref[...], w2_ref[...])
        dxn, dw2 = vjp(dn2_ref[...])
        dx1 = dx2_ref[...] + dxn
        dx1_ref[...] = dx1

        @pl.when(i == 0)
        def _():
            dw2_ref[...] = jnp.zeros_like(dw2_ref)

        dw2_ref[...] += dw2
        dmix = _dot_nt(dx1.astype(BF16), wo_ref[...])
        gates = jax.nn.sigmoid(gl_ref[...])
        g_dn, g_sb = gates[:, :D_MODEL], gates[:, D_MODEL:]
        dpdn = (dmix * g_dn).astype(BF16)
        dpsb = (dmix * g_sb).astype(BF16)
        dgl_ref[:, :D_MODEL] = (dmix * pdn_ref[...].astype(F32) * g_dn * (1.0 - g_dn)).astype(BF16)
        dgl_ref[:, D_MODEL:] = (dmix * psb_ref[...].astype(F32) * g_sb * (1.0 - g_sb)).astype(BF16)
        dpdn_ref[...] = dpdn
        dpsb_ref[...] = dpsb
        dodn_ref[...] = _dot_nt(dpdn, wpd_ref[...])
        dosb_ref[...] = _dot_nt(dpsb, wps_ref[...]).astype(BF16)

    row = pl.BlockSpec((tb, D_MODEL), lambda i: (i, 0))
    wide = pl.BlockSpec((tb, 2 * D_MODEL), lambda i: (i, 0))
    sq = pl.BlockSpec((D_MODEL, D_MODEL), lambda i: (0, 0))
    vec = pl.BlockSpec((1, D_MODEL), lambda i: (0, 0))
    f = jax.ShapeDtypeStruct((t, D_MODEL), F32)
    b = jax.ShapeDtypeStruct((t, D_MODEL), BF16)
    return pl.pallas_call(
        body, name="merge_bwd", grid=(t // tb,),
        in_specs=[row, row, row, vec, wide, row, row, sq, sq, sq],
        out_specs=[row, vec, wide, row, row, row, row],
        out_shape=[f, jax.ShapeDtypeStruct((1, D_MODEL), F32), jax.ShapeDtypeStruct((t, 2 * D_MODEL), BF16),
                   b, b, f, b],
        compiler_params=_params(("arbitrary",)),
    )(dx2, dn2, x1, w2, gl, pdn, psb, wp_dn, wp_sb, w_out)


def _conv_taps(buf, w_ref, first, rows, cols=slice(None)):
    y = w_ref[0:1, cols] * buf[pl.ds(first, rows), cols]
    for s in range(1, w_ref.shape[0]):
        y = y + w_ref[s:s + 1, cols] * buf[pl.ds(first + s, rows), cols]
    return y


def _ffn_mid_fwd(pre_g, pre_u, wg, wu):
    t, c = pre_g.shape
    kk = wg.shape[0]
    tb, cb = _rows(t), _pick(c, ELEMENTWISE_COLS)
    per = tb // HALO

    def body(g_ref, gh_ref, u_ref, uh_ref, wg_ref, wu_ref, a_ref, gbuf, ubuf):
        i = pl.program_id(0)
        for buf, ref, halo in ((gbuf, g_ref, gh_ref), (ubuf, u_ref, uh_ref)):
            buf[pl.ds(HALO, tb), :] = ref[...]
            buf[pl.ds(0, HALO), :] = jnp.where(i == 0, 0.0, halo[...])
        for j in range(cb // LANES):
            sl = pl.ds(j * LANES, LANES)
            ug = _conv_taps(gbuf, wg_ref, HALO - (kk - 1), tb, sl)
            uu = _conv_taps(ubuf, wu_ref, HALO - (kk - 1), tb, sl)
            a_ref[:, sl] = (_silu(ug) * uu).astype(BF16)

    blk = pl.BlockSpec((tb, cb), lambda i, j: (i, j))
    halo = pl.BlockSpec((HALO, cb), lambda i, j: (jnp.maximum(i * per - 1, 0), j))
    wspec = pl.BlockSpec((kk, cb), lambda i, j: (0, j))
    return pl.pallas_call(
        body, name="ffn_mid_fwd", grid=(t // tb, c // cb),
        in_specs=[blk, halo, blk, halo, wspec, wspec], out_specs=blk,
        out_shape=jax.ShapeDtypeStruct((t, c), BF16),
        scratch_shapes=[pltpu.VMEM((tb + HALO, cb), F32)] * 2,
        compiler_params=_params(("parallel", "parallel")),
    )(pre_g, pre_g, pre_u, pre_u, wg, wu)


def _ffn_mid_bwd(pre_g, pre_u, wg, wu, da):
    t, c = pre_g.shape
    kk = wg.shape[0]
    tb, cb = _rows(t), _pick(c, ELEMENTWISE_COLS)
    per = tb // HALO
    nblk = t // tb
    ext = tb + HALO

    def body(g_ref, gb_ref, ga_ref, u_ref, ub_ref, ua_ref, da_ref, daa_ref, wg_ref, wu_ref,
             dg_ref, du_ref, dwg_ref, dwu_ref, gbuf, ubuf, dabuf, dgbuf, dubuf):
        i = pl.program_id(1)
        last = i == nblk - 1
        for buf, ref, before, after in ((gbuf, g_ref, gb_ref, ga_ref), (ubuf, u_ref, ub_ref, ua_ref)):
            buf[pl.ds(0, HALO), :] = jnp.where(i == 0, 0.0, before[...])
            buf[pl.ds(HALO, tb), :] = ref[...]
            buf[pl.ds(HALO + tb, HALO), :] = jnp.where(last, 0.0, after[...])
        dabuf[pl.ds(0, tb), :] = da_ref[...]
        dabuf[pl.ds(tb, HALO), :] = jnp.where(last, 0.0, daa_ref[...])

        @pl.when(i == 0)
        def _():
            dwg_ref[...] = jnp.zeros_like(dwg_ref)
            dwu_ref[...] = jnp.zeros_like(dwu_ref)

        for j in range(cb // LANES):
            sl = pl.ds(j * LANES, LANES)
            ug = _conv_taps(gbuf, wg_ref, HALO - (kk - 1), ext, sl)
            uu = _conv_taps(ubuf, wu_ref, HALO - (kk - 1), ext, sl)
            _, vjp = jax.vjp(lambda g, u: _silu(g) * u, ug, uu)
            dgbuf[:, sl], dubuf[:, sl] = vjp(dabuf[:, sl])
            for dbuf, xbuf, w_ref, dx_ref, dw_ref in ((dgbuf, gbuf, wg_ref, dg_ref, dwg_ref),
                                                      (dubuf, ubuf, wu_ref, du_ref, dwu_ref)):
                x = xbuf[pl.ds(HALO, tb), sl]
                dx = None
                for s in range(kk):
                    shifted = dbuf[pl.ds(kk - 1 - s, tb), sl]
                    term = w_ref[s:s + 1, sl] * shifted
                    dx = term if dx is None else dx + term
                    dw_ref[s:s + 1, sl] += jnp.sum(shifted * x, axis=0, keepdims=True)
                dx_ref[:, sl] = dx.astype(BF16)

    blk = pl.BlockSpec((tb, cb), lambda j, i: (i, j))
    before = pl.BlockSpec((HALO, cb), lambda j, i: (jnp.maximum(i * per - 1, 0), j))
    after = pl.BlockSpec((HALO, cb), lambda j, i: (jnp.minimum((i + 1) * per, t // HALO - 1), j))
    wspec = pl.BlockSpec((kk, cb), lambda j, i: (0, j))
    dwspec = pl.BlockSpec((HALO, cb), lambda j, i: (0, j))
    half = jax.ShapeDtypeStruct((t, c), BF16)
    dwshape = jax.ShapeDtypeStruct((HALO, c), F32)
    return pl.pallas_call(
        body, name="ffn_mid_bwd", grid=(c // cb, nblk),
        in_specs=[blk, before, after, blk, before, after, blk, after, wspec, wspec],
        out_specs=[blk, blk, dwspec, dwspec],
        out_shape=[half, half, dwshape, dwshape],
        scratch_shapes=[pltpu.VMEM((ext + HALO, cb), F32)] * 2 + [pltpu.VMEM((ext, cb), F32)] * 3,
        compiler_params=_params(("parallel", "arbitrary")),
    )(pre_g, pre_g, pre_g, pre_u, pre_u, pre_u, da, da, wg, wu)


def _down_loss(a, w_down, x1, wf, target):
    t = x1.shape[0]
    tb = _rows(t)

    def body(a_ref, wd_ref, x1_ref, wf_ref, tgt_ref, dx2_ref, dwf_ref, loss_ref):
        i = pl.program_id(0)
        x2 = x1_ref[...] + _dot(a_ref[...], wd_ref[...])
        y, vjp = jax.vjp(_rms, x2, wf_ref[...])
        err = y - tgt_ref[...]
        dx2, dwf = vjp(err * (1.0 / D_MODEL))
        dx2_ref[...] = dx2
        part = jnp.sum(jnp.sum(err * err, axis=1, keepdims=True), axis=0, keepdims=True) * (0.5 / D_MODEL)

        @pl.when(i == 0)
        def _():
            dwf_ref[...] = jnp.zeros_like(dwf_ref)
            loss_ref[...] = jnp.zeros_like(loss_ref)

        dwf_ref[...] += dwf
        loss_ref[...] += jnp.broadcast_to(part, loss_ref.shape)

    row = pl.BlockSpec((tb, D_MODEL), lambda i: (i, 0))
    vec = pl.BlockSpec((1, D_MODEL), lambda i: (0, 0))
    return pl.pallas_call(
        body, name="down_loss", grid=(t // tb,),
        in_specs=[pl.BlockSpec((tb, D_FF), lambda i: (i, 0)), pl.BlockSpec((D_FF, D_MODEL), lambda i: (0, 0)),
                  row, vec, row],
        out_specs=[row, vec, pl.BlockSpec((1, LANES), lambda i: (0, 0))],
        out_shape=[jax.ShapeDtypeStruct((t, D_MODEL), F32), jax.ShapeDtypeStruct((1, D_MODEL), F32),
                   jax.ShapeDtypeStruct((1, LANES), F32)],
        compiler_params=_params(("arbitrary",)),
    )(a, w_down, x1, wf, target)


def _local_step(x, target, wts, late_wire=(), late_weights=None, early_partials=None):
    t = x.shape[0]
    nchunk = t // DN_CHUNK

    n1, hab = _norm1_fwd(x, wts["norm1"], wts["w_ab"])
    dnqkv = _mm(n1, wts["w_dnqkv"], name="h_dnqkv")
    dngate = _mm(n1, wts["w_dngate"], name="h_dngate")
    sbqkv = _mm(n1, wts["w_sbqkv"], out_dtype=BF16, name="h_sbqkv")
    gl = _mm(n1, wts["w_gl"], name="h_gl")

    cdn = _conv_fwd(dnqkv, wts["dn_conv"], "dn_conv_fwd")
    qn, kn, vv, gb = _dn_prep_fwd(cdn, hab, wts["alog"], wts["dtb"])
    per_head = gb[:, :2 * N_HEADS].T.reshape(2 * N_HEADS, nchunk, DN_CHUNK)
    grow, brow = per_head[:N_HEADS, :, None, :], per_head[N_HEADS:, :, None, :]
    u_dn, w_dn, a_qk, qe, kdec, egl, tinv, *late = _dn_local_fwd(qn, kn, vv, grow, brow, late_wire)
    if late_wire:
        wts = {**wts, **late_weights(late)}
    o_raw, states = _dn_seq_fwd(u_dn, w_dn, a_qk, qe, kdec, egl)
    o_dn = _dn_post_fwd(o_raw, dngate, wts["dn_norm"])

    o_sb, tot, sb_used = _sb_fwd(sbqkv)

    pdn, psb, mixed, x1, n2 = _merge_fwd(o_dn, o_sb, gl, x, wts["wp_dn"], wts["wp_sb"], wts["w_out"],
                                         wts["norm2"])
    pre_g = _mm(n2, wts["w_up_g"], name="ffn_up_g")
    pre_u = _mm(n2, wts["w_up_u"], name="ffn_up_u")
    act = _ffn_mid_fwd(pre_g, pre_u, wts["ffn_conv_g"], wts["ffn_conv_u"])
    dx2, d_normf, loss_part = _down_loss(act, wts["w_down"], x1, wts["normf"], target)

    grads = {"normf": d_normf}
    da = _mm(dx2, wts["w_down"], tb=True, name="d_act")
    grads["w_down"] = _mm(act, dx2, ta=True, out_dtype=BF16, name="dw_down")
    dpre_g, dpre_u, dcw_g, dcw_u = _ffn_mid_bwd(pre_g, pre_u, wts["ffn_conv_g"], wts["ffn_conv_u"], da)
    grads["ffn_conv"] = jnp.concatenate([dcw_g[:FFN_CONV], dcw_u[:FFN_CONV]], axis=1)
    dn2 = _mm(dpre_g, wts["w_up_g"], tb=True, name="dn2_g")
    dn2 = _mm(dpre_u, wts["w_up_u"], tb=True, add=dn2, name="dn2_u")
    grads["w_up_g"] = _mm(n2, dpre_g, ta=True, out_dtype=BF16, name="dw_up_g")
    grads["w_up_u"] = _mm(n2, dpre_u, ta=True, out_dtype=BF16, name="dw_up_u")

    dx1, grads["norm2"], dgl, dpdn, dpsb, do_dn, do_sb = _merge_bwd(
        dx2, dn2, x1, wts["norm2"], gl, pdn, psb, wts["wp_dn"], wts["wp_sb"], wts["w_out"])
    grads["w_out"] = _mm(mixed, dx1, ta=True, out_dtype=BF16, name="dw_out")
    grads["wp_dn"] = _mm(o_dn, dpdn, ta=True, out_dtype=BF16, name="dw_proj_dn")
    grads["wp_sb"] = _mm(o_sb, dpsb, ta=True, out_dtype=BF16, name="dw_proj_sb")

    partials = early_partials(grads) if early_partials else ()
    dsq, dsk, dsv = _sb_bwd(sbqkv, tot, sb_used, do_sb)
    dsbqkv = jnp.concatenate([dsq, dsk, dsv], axis=1).astype(BF16)

    do_raw, ddngate, grads["dn_norm"] = _dn_post_bwd(o_raw, dngate, wts["dn_norm"], do_dn)
    seq_grads = _dn_seq_bwd(u_dn, w_dn, a_qk, qe, kdec, egl, states, do_raw)
    dqn, dkn, dvv, dgrow, dbrow, *arrived = _dn_local_bwd(qn, kn, vv, grow, brow, tinv, *seq_grads,
                                                          partials=partials)
    dgb = jnp.concatenate([dgrow.reshape(N_HEADS, t), dbrow.reshape(N_HEADS, t)], axis=0).T
    dgb = jnp.pad(dgb, ((0, 0), (0, LANES - 2 * N_HEADS)))
    dcdn, dhab, grads["alog"], grads["dtb"] = _dn_prep_bwd(cdn, hab, wts["alog"], wts["dtb"], dqn, dkn, dvv, dgb)
    ddnqkv, dcw_dn = _conv_bwd(dcdn, dnqkv, wts["dn_conv"], "dn_conv_bwd", BF16)
    grads["dn_conv"] = dcw_dn[:DN_CONV]

    dn1 = _mm(ddnqkv, wts["w_dnqkv"], tb=True, name="dn1_dnqkv")
    dn1 = _mm(ddngate, wts["w_dngate"], tb=True, add=dn1, name="dn1_dngate")
    dn1 = _mm(dsbqkv, wts["w_sbqkv"], tb=True, add=dn1, name="dn1_sbqkv")
    dn1 = _mm(dgl, wts["w_gl"], tb=True, add=dn1, name="dn1_gl")
    grads["w_dnqkv"] = _mm(n1, ddnqkv, ta=True, out_dtype=BF16, name="dw_dnqkv")
    grads["w_dngate"] = _mm(n1, ddngate, ta=True, out_dtype=BF16, name="dw_dngate")
    grads["w_sbqkv"] = _mm(n1, dsbqkv, ta=True, out_dtype=BF16, name="dw_sbqkv")
    grads["w_gl"] = _mm(n1, dgl, ta=True, out_dtype=BF16, name="dw_gl")
    grads["w_ab"] = _mm(n1, dhab, ta=True, out_dtype=BF16, name="dw_ab")
    grad_x, grads["norm1"] = _norm1_bwd(x, wts["norm1"], dn1, dx1, dhab, wts["w_ab"])
    return loss_part, grad_x, grads, (list(partials), arrived)


def _place():
    return lax.axis_index("x"), lax.axis_index("y"), lax.axis_index("c")


def _hbm_specs(n):
    return [pl.BlockSpec(memory_space=pltpu.HBM)] * n


GATHER_SEMS = 8
GATHER_FORWARD_AT = 5


def _gather_protocol(ins, outs, send_sems, recv_sems):
    n = len(ins)
    x, y, c = _place()
    me = 2 * x + y
    sibling = (x, y, 1 - c)
    xn, yn, dg = (1 - x, y), (x, 1 - y), (1 - x, 1 - y)
    idx = lambda chip: 2 * chip[0] + chip[1]

    def part(a, chip_index, core, quarter=None):
        half = ins[a].shape[0] // 2
        if quarter is None:
            return outs[a].at[chip_index, pl.ds(core * half, half), :]
        return outs[a].at[chip_index, pl.ds(core * half + quarter * (half // 2), half // 2), :]

    def copy(a, k, src, dst, to):
        return pltpu.make_async_remote_copy(src_ref=src, dst_ref=dst, send_sem=send_sems.at[GATHER_SEMS * a + k],
                                            recv_sem=recv_sems.at[GATHER_SEMS * a + k], device_id=to,
                                            device_id_type=MESH)

    def sent(a, k):
        half = ins[a].shape[0] // 2
        my_half = ins[a].at[pl.ds(c * half, half), :]
        if k < 2:
            return copy(a, k, my_half, part(a, me, c), (*(xn, yn)[k], c))
        if k < 4:
            src = part(a, idx((xn, yn)[k - 2]), c, k - 2)
            return copy(a, k, src, src, (*(yn, xn)[k - 2], c))
        src = (part(a, idx(xn), c), part(a, idx(yn), c), part(a, idx(dg), c, 0), part(a, idx(dg), c, 1))[k - 4]
        return copy(a, k, src, src, sibling)

    def landed(a, k):
        dst = (part(a, idx(xn), c), part(a, idx(yn), c), part(a, idx(dg), c, 0), part(a, idx(dg), c, 1),
               part(a, idx(xn), 1 - c), part(a, idx(yn), 1 - c), part(a, idx(dg), 1 - c, 0),
               part(a, idx(dg), 1 - c, 1))[k]
        return copy(a, k, dst, dst, sibling)

    def begin():
        for a in range(n):
            sent(a, 0).start()
            sent(a, 1).start()

    def middle():
        for a in range(n):
            for k in range(2):
                landed(a, k).wait_recv()
                sent(a, 2 + k).start()
                sent(a, 4 + k).start()

    def end():
        for a in range(n):
            for k in (2, 3):
                landed(a, k).wait_recv()
                sent(a, 4 + k).start()
        for a in range(n):
            for k in range(4, GATHER_SEMS):
                landed(a, k).wait_recv()
        for a in range(n):
            for k in range(GATHER_SEMS):
                sent(a, k).wait_send()

    return begin, middle, end


def _gather_out_shapes(shards):
    return [jax.ShapeDtypeStruct((N_CHIPS,) + s.shape, s.dtype) for s in shards]


def _gather_sems(n):
    return [pltpu.SemaphoreType.DMA((GATHER_SEMS * n,)), pltpu.SemaphoreType.DMA((GATHER_SEMS * n,))]


def _gather_shards(shards):
    n = len(shards)

    def body(*refs):
        begin, middle, end = _gather_protocol(refs[:n], refs[n:2 * n], *refs[2 * n:])
        begin()
        middle()
        end()

    return pl.pallas_call(
        body, name="gather_weights", in_specs=_hbm_specs(n), out_specs=_hbm_specs(n),
        out_shape=_gather_out_shapes(shards), scratch_shapes=_gather_sems(n),
    )(*shards)


def _pair_exchange_halves(gs, tag):
    n = len(gs)

    def body(*refs):
        ins, outs, (send_sems, recv_sems) = refs[:n], refs[n:2 * n], refs[2 * n:]
        x, y, c = _place()
        cps = []
        for a in range(n):
            half = ins[a].shape[1] // 2
            cp = pltpu.make_async_remote_copy(src_ref=ins[a].at[:, pl.ds((1 - c) * half, half), :], dst_ref=outs[a],
                                              send_sem=send_sems.at[a], recv_sem=recv_sems.at[a],
                                              device_id=(x, y, 1 - c), device_id_type=MESH)
            cp.start()
            cps.append(cp)
        for cp in cps:
            cp.wait()

    return pl.pallas_call(
        body, name="grad_pair_exchange_" + tag, in_specs=_hbm_specs(n), out_specs=_hbm_specs(n),
        out_shape=[jax.ShapeDtypeStruct((g.shape[0], g.shape[1] // 2, g.shape[2]), g.dtype) for g in gs],
        scratch_shapes=[pltpu.SemaphoreType.DMA((n,)), pltpu.SemaphoreType.DMA((n,))],
    )(*gs)


def _pick_rows(n, target=1024):
    best = 16
    for b in range(16, min(n, target) + 1, 16):
        if n % b == 0:
            best = b
    return best


def _pair_add(g, got, c_idx, tag):
    nsh, rows, cols = g.shape
    half = rows // 2
    rb = _pick_rows(half)

    def body(c_ref, g_ref, got_ref, o_ref):
        o_ref[...] = (g_ref[...].astype(F32) + got_ref[...].astype(F32)).astype(BF16)

    nb = half // rb
    grid_spec = pltpu.PrefetchScalarGridSpec(
        num_scalar_prefetch=1, grid=(nsh, nb),
        in_specs=[pl.BlockSpec((1, rb, cols), lambda s, i, c_ref: (s, c_ref[0] * nb + i, 0)),
                  pl.BlockSpec((1, rb, cols), lambda s, i, c_ref: (s, i, 0))],
        out_specs=pl.BlockSpec((1, rb, cols), lambda s, i, c_ref: (s, i, 0)))
    return pl.pallas_call(
        body, name="grad_pair_add_" + tag, grid_spec=grid_spec,
        out_shape=jax.ShapeDtypeStruct((nsh, half, cols), BF16),
        compiler_params=_params(("parallel", "parallel")),
    )(c_idx, g, got)


def _chip_exchange_protocol(ins, outs, send_sems, recv_sems):
    x, y, c = _place()
    chips = [(1 - x, y), (x, 1 - y), (1 - x, 1 - y)]

    def copies():
        return [pltpu.make_async_remote_copy(src_ref=ins[a].at[2 * px + py], dst_ref=outs[a].at[j],
                                             send_sem=send_sems.at[3 * a + j], recv_sem=recv_sems.at[3 * a + j],
                                             device_id=(px, py, c), device_id_type=MESH)
                for a in range(len(ins)) for j, (px, py) in enumerate(chips)]

    def begin():
        for cp in copies():
            cp.start()

    def end():
        for cp in copies():
            cp.wait_recv()
        for cp in copies():
            cp.wait_send()

    return begin, end


def _chip_exchange_shapes(ps):
    return [jax.ShapeDtypeStruct((N_CHIPS - 1,) + p.shape[1:], p.dtype) for p in ps]


def _chip_exchange_sems(n):
    return [pltpu.SemaphoreType.DMA((3 * n,)), pltpu.SemaphoreType.DMA((3 * n,))]


def _chip_exchange(ps):
    n = len(ps)

    def body(*refs):
        begin, end = _chip_exchange_protocol(refs[:n], refs[n:2 * n], *refs[2 * n:])
        begin()
        end()

    return pl.pallas_call(
        body, name="grad_chip_exchange", in_specs=_hbm_specs(n), out_specs=_hbm_specs(n),
        out_shape=_chip_exchange_shapes(ps), scratch_shapes=_chip_exchange_sems(n),
    )(*ps)


def _sum_partials(p, got, chip_idx, tag):
    nsh, half, cols = got.shape
    rb = _pick_rows(half)

    def body(me_ref, p_ref, got_ref, o_ref):
        acc = p_ref[0].astype(F32)
        for s in range(nsh):
            acc = acc + got_ref[s].astype(F32)
        o_ref[...] = acc

    grid_spec = pltpu.PrefetchScalarGridSpec(
        num_scalar_prefetch=1, grid=(half // rb,),
        in_specs=[pl.BlockSpec((1, rb, cols), lambda i, me_ref: (me_ref[0], i, 0)),
                  pl.BlockSpec((nsh, rb, cols), lambda i, me_ref: (0, i, 0))],
        out_specs=pl.BlockSpec((rb, cols), lambda i, me_ref: (i, 0)))
    return pl.pallas_call(
        body, name="grad_sum_chips_" + tag, grid_spec=grid_spec,
        out_shape=jax.ShapeDtypeStruct((half, cols), F32),
        compiler_params=_params(("parallel",)),
    )(chip_idx, p, got)


def _pair_share(rs):
    n = len(rs)

    def body(*refs):
        ins, outs, (send_sems, recv_sems) = refs[:n], refs[n:2 * n], refs[2 * n:]
        x, y, c = _place()
        cps = []
        for a in range(n):
            cp = pltpu.make_async_remote_copy(src_ref=ins[a], dst_ref=outs[a], send_sem=send_sems.at[a],
                                              recv_sem=recv_sems.at[a], device_id=(x, y, 1 - c),
                                              device_id_type=MESH)
            cp.start()
            cps.append(cp)
        for cp in cps:
            cp.wait()

    return pl.pallas_call(
        body, name="grad_pair_share", in_specs=_hbm_specs(n), out_specs=_hbm_specs(n),
        out_shape=[jax.ShapeDtypeStruct(r.shape, r.dtype) for r in rs],
        scratch_shapes=[pltpu.SemaphoreType.DMA((n,)), pltpu.SemaphoreType.DMA((n,))],
    )(*rs)


def _small_allreduce(v):
    rows, cols = v.shape
    ndev = 8

    def body(in_ref, out_ref, slots, send_sems, recv_sems):
        x, y, c = _place()
        me = 4 * x + 2 * y + c
        slots[me] = in_ref[...]
        sends = []
        for k in range(1, ndev):
            peer = (x ^ (k >> 2), y ^ ((k >> 1) & 1), c ^ (k & 1))
            cp = pltpu.make_async_remote_copy(src_ref=in_ref, dst_ref=slots.at[me], send_sem=send_sems.at[k - 1],
                                              recv_sem=recv_sems.at[k - 1], device_id=peer, device_id_type=MESH)
            cp.start()
            sends.append(cp)
        for k in range(1, ndev):
            there = slots.at[me ^ k]
            pltpu.make_async_remote_copy(src_ref=there, dst_ref=there, send_sem=send_sems.at[k - 1],
                                         recv_sem=recv_sems.at[k - 1], device_id=(x, y, c),
                                         device_id_type=MESH).wait_recv()
        for cp in sends:
            cp.wait_send()
        acc = slots[0]
        for s in range(1, ndev):
            acc = acc + slots[s]
        out_ref[...] = acc

    return pl.pallas_call(
        body, name="small_allreduce",
        in_specs=[pl.BlockSpec(memory_space=pltpu.VMEM)],
        out_specs=pl.BlockSpec(memory_space=pltpu.VMEM),
        out_shape=jax.ShapeDtypeStruct((rows, cols), F32),
        scratch_shapes=[pltpu.VMEM((ndev, rows, cols), F32), pltpu.SemaphoreType.DMA((ndev - 1,)),
                        pltpu.SemaphoreType.DMA((ndev - 1,))],
    )(v)


def _adamw(w, g, m, v, name):
    r, c = w.shape
    rb = r if r <= 128 else _pick_rows_8(r, 128)
    c1 = 1.0 - ADAM_B1 ** ADAM_STEP
    c2 = 1.0 - ADAM_B2 ** ADAM_STEP

    def body(w_ref, g_ref, m_ref, v_ref, d_ref, nm_ref, nv_ref):
        gg = g_ref[...]
        nm = ADAM_B1 * m_ref[...] + (1.0 - ADAM_B1) * gg
        nv = ADAM_B2 * v_ref[...] + (1.0 - ADAM_B2) * (gg * gg)
        d_ref[...] = -ADAM_LR * ((nm / c1) / (jnp.sqrt(nv / c2) + ADAM_EPS) + ADAM_WD * w_ref[...])
        nm_ref[...] = nm
        nv_ref[...] = nv

    blk = pl.BlockSpec((rb, c), lambda i: (i, 0))
    shp = jax.ShapeDtypeStruct((r, c), F32)
    return pl.pallas_call(
        body, name=name, grid=(r // rb,), in_specs=[blk] * 4, out_specs=[blk] * 3, out_shape=[shp] * 3,
        compiler_params=_params(("parallel",)),
    )(w, g, m, v)


def _pick_rows_8(n, target):
    best = n
    for b in range(8, min(n, target) + 1, 8):
        if n % b == 0:
            best = b
    return best


W_IN_COLS = 2308
W_UP_COLS = 1408
W_DOWN_ROWS = 704
DN_CONV_COLS = 768
FFN_CONV_COLS = 1408
PROJ_ROWS = 256
ROW_TILE = 16
ROW_SEGS = [("wp_dn", PROJ_ROWS), ("wp_sb", PROJ_ROWS), ("w_out", PROJ_ROWS), ("w_down", W_DOWN_ROWS),
            ("dn_conv", ROW_TILE), ("ffn_conv", ROW_TILE), ("spare", 2 * ROW_TILE)]
ROW_OFFS = {nm: (sum(n for _, n in ROW_SEGS[:i]), n) for i, (nm, n) in enumerate(ROW_SEGS)}
STACK_ROWS = sum(n for _, n in ROW_SEGS)
assert all(n % ROW_TILE == 0 for _, n in ROW_SEGS) and STACK_ROWS % (4 * ROW_TILE) == 0
Q_END, A_END, G_END, S_END = 3 * D_MODEL, 3 * D_MODEL + 2 * N_HEADS, 4 * D_MODEL + 2 * N_HEADS, 7 * D_MODEL + 2 * N_HEADS


def _flat_rows(a, nrows):
    flat = a.reshape(-1)
    return jnp.pad(flat, (0, nrows * D_MODEL - flat.shape[0])).reshape(nrows, D_MODEL)


IN_EXTRA_ROWS = 64


def _weight_wire(w_in, wp_dn, wp_sb, w_out, w_up, w_down, dn_conv, ffn_conv):
    bits = lax.bitcast_convert_type(dn_conv, BF16).reshape(-1)
    extra = jnp.pad(bits, (0, IN_EXTRA_ROWS * W_IN_COLS - bits.shape[0])).reshape(IN_EXTRA_ROWS, W_IN_COLS)
    stack = jnp.concatenate([wp_dn.astype(BF16), wp_sb.astype(BF16), w_out.astype(BF16), w_down.astype(BF16),
                             jnp.zeros((ROW_TILE, D_MODEL), BF16),
                             _flat_rows(lax.bitcast_convert_type(ffn_conv, BF16), ROW_TILE),
                             jnp.zeros((ROW_OFFS["spare"][1], D_MODEL), BF16)], axis=0)
    return [jnp.concatenate([w_in.astype(BF16), extra], axis=0)], [w_up.astype(BF16), stack]


def _col_range(g, lo, hi, width):
    parts = []
    for s in range(g.shape[0]):
        a, b = max(lo, s * width), min(hi, (s + 1) * width)
        if a < b:
            parts.append(g[s][:, a - s * width:b - s * width])
    return parts[0] if len(parts) == 1 else jnp.concatenate(parts, axis=1)


def _f32_rows(raw, k, ncols):
    raw = raw.reshape(N_CHIPS, -1)[:, :2 * k * ncols].reshape(N_CHIPS, k * ncols, 2)
    vals = lax.bitcast_convert_type(raw, F32).reshape(N_CHIPS, k, ncols)
    return vals.transpose(1, 0, 2).reshape(k, N_CHIPS * ncols)


def _unpack_early(g_in):
    w = g_in[:, :D_MODEL, :]
    return {
        "w_dnqkv": _col_range(w, 0, Q_END, W_IN_COLS),
        "w_ab": jnp.pad(_col_range(w, Q_END, A_END, W_IN_COLS), ((0, 0), (0, LANES - 2 * N_HEADS))),
        "w_dngate": _col_range(w, A_END, G_END, W_IN_COLS),
        "w_sbqkv": _col_range(w, G_END, S_END, W_IN_COLS),
        "w_gl": _col_range(w, S_END, N_CHIPS * W_IN_COLS, W_IN_COLS),
        "dn_conv": _f32_rows(g_in[:, D_MODEL:, :], DN_CONV, DN_CONV_COLS),
    }


def _unpack_late(g_up, g_stack):
    def seg(nm):
        at, n = ROW_OFFS[nm]
        return g_stack[:, at:at + n, :]

    ffn_conv = _f32_rows(seg("ffn_conv"), FFN_CONV, FFN_CONV_COLS)
    return {
        "wp_dn": seg("wp_dn").reshape(D_MODEL, D_MODEL),
        "wp_sb": seg("wp_sb").reshape(D_MODEL, D_MODEL),
        "w_out": seg("w_out").reshape(D_MODEL, D_MODEL),
        "w_up_g": _col_range(g_up, 0, D_FF, W_UP_COLS), "w_up_u": _col_range(g_up, D_FF, 2 * D_FF, W_UP_COLS),
        "w_down": seg("w_down").reshape(D_FF, D_MODEL),
        "ffn_conv_g": ffn_conv[:, :D_FF], "ffn_conv_u": ffn_conv[:, D_FF:],
    }


def _grad_wire_early(gr):
    def cols(a, ncols):
        return a.reshape(a.shape[0], N_CHIPS, ncols).transpose(1, 0, 2)

    def rows(a, nrows):
        return a.astype(BF16).reshape(N_CHIPS, nrows, a.shape[1])

    def flat(a, nrows):
        a = a.astype(BF16).reshape(N_CHIPS, -1)
        return jnp.pad(a, ((0, 0), (0, nrows * D_MODEL - a.shape[1]))).reshape(N_CHIPS, nrows, D_MODEL)

    up = [gr["w_up_g"], gr["w_up_u"]]
    g_up = jnp.stack([up[s // 2][:, (s % 2) * W_UP_COLS:(s % 2 + 1) * W_UP_COLS].astype(BF16) for s in range(N_CHIPS)])
    g_stack = jnp.concatenate([rows(gr["wp_dn"], PROJ_ROWS), rows(gr["wp_sb"], PROJ_ROWS), rows(gr["w_out"], PROJ_ROWS),
                               rows(gr["w_down"], W_DOWN_ROWS), jnp.zeros((N_CHIPS, ROW_TILE, D_MODEL), BF16),
                               flat(cols(gr["ffn_conv"], FFN_CONV_COLS), ROW_TILE),
                               jnp.zeros((N_CHIPS, ROW_OFFS["spare"][1], D_MODEL), BF16)], axis=1)
    return [g_up, g_stack]


def _grad_wire_late(gr):
    pieces = [(gr["w_dnqkv"], 0), (gr["w_ab"][:, :2 * N_HEADS], Q_END), (gr["w_dngate"], A_END),
              (gr["w_sbqkv"], G_END), (gr["w_gl"], S_END)]
    conv = gr["dn_conv"].reshape(DN_CONV, N_CHIPS, DN_CONV_COLS).transpose(1, 0, 2).reshape(N_CHIPS, -1)

    def block(s):
        lo, hi = s * W_IN_COLS, (s + 1) * W_IN_COLS
        parts = []
        for a, at in pieces:
            b0, b1 = max(lo, at), min(hi, at + a.shape[1])
            if b0 < b1:
                parts.append(a[:, b0 - at:b1 - at].astype(BF16))
        w = parts[0] if len(parts) == 1 else jnp.concatenate(parts, axis=1)
        extra = jnp.pad(conv[s].astype(BF16), (0, IN_EXTRA_ROWS * W_IN_COLS - conv.shape[1]))
        return jnp.concatenate([w, extra.reshape(IN_EXTRA_ROWS, W_IN_COLS)], axis=0)

    return [jnp.stack([block(s) for s in range(N_CHIPS)])]


def _unpack_grad_shard(r_in, r_up, r_stack):
    def seg(nm):
        at, n = ROW_OFFS[nm]
        return r_stack[at:at + n, :]

    return {
        "w_in": r_in[:D_MODEL], "w_up": r_up,
        "wp_dn": seg("wp_dn"), "wp_sb": seg("wp_sb"), "w_out": seg("w_out"), "w_down": seg("w_down"),
        "dn_conv": r_in[D_MODEL:].reshape(-1)[:DN_CONV * DN_CONV_COLS].reshape(DN_CONV, DN_CONV_COLS),
        "ffn_conv": seg("ffn_conv").reshape(-1)[:FFN_CONV * FFN_CONV_COLS].reshape(FFN_CONV, FFN_CONV_COLS),
    }


def _lane_row(v):
    return jnp.pad(v.reshape(1, -1), ((0, 0), (0, LANES - v.size)))


def kernel(x, norm1_w, w_in, dn_conv_w, dn_A_log, dn_dt_bias, dn_norm_w, w_proj_dn, w_proj_sb, w_out, norm2_w, ffn_w_up, ffn_conv_w, ffn_w_down, norm_f_w, loss_target, m_norm1_w, m_w_in, m_dn_conv_w, m_dn_A_log, m_dn_dt_bias, m_dn_norm_w, m_w_proj_dn, m_w_proj_sb, m_w_out, m_norm2_w, m_ffn_w_up, m_ffn_conv_w, m_ffn_w_down, m_norm_f_w, v_norm1_w, v_w_in, v_dn_conv_w, v_dn_A_log, v_dn_dt_bias, v_dn_norm_w, v_w_proj_dn, v_w_proj_sb, v_w_out, v_norm2_w, v_ffn_w_up, v_ffn_conv_w, v_ffn_w_down, v_norm_f_w):
    early, late = _weight_wire(w_in[0], w_proj_dn[0], w_proj_sb[0], w_out[0], ffn_w_up[0], ffn_w_down[0],
                               dn_conv_w[0], ffn_conv_w[0])
    chip_idx = (2 * lax.axis_index("x") + lax.axis_index("y")).astype(jnp.int32)

    def with_mine(gathered, wire):
        return [lax.dynamic_update_slice(g, mine[None], (chip_idx, 0, 0)) for g, mine in zip(gathered, wire)]

    wts = _unpack_early(*with_mine(_gather_shards(early), early))
    wts.update(norm1=norm1_w, norm2=norm2_w, normf=norm_f_w.reshape(1, D_MODEL), dn_norm=dn_norm_w,
               alog=_lane_row(dn_A_log), dtb=_lane_row(dn_dt_bias))

    c_idx = lax.axis_index("c").astype(jnp.int32).reshape(1)

    def pair_sums(wire_g, tags, when):
        return [_pair_add(g, got, c_idx, tag) for g, got, tag in zip(wire_g, _pair_exchange_halves(wire_g, when), tags)]

    loss_part, grad_x, gr, (early_sums, early_arrived) = _local_step(
        x[0], loss_target[0], wts, late, lambda gathered: _unpack_late(*with_mine(gathered, late)),
        lambda grads: pair_sums(_grad_wire_early(grads), ["w_up", "rows"], "early"))

    late_sums = pair_sums(_grad_wire_late(gr), ["w_in"], "late")
    tags = ["w_in", "w_up", "rows"]
    reduced = [_sum_partials(p, got, chip_idx.reshape(1), tag)
               for p, got, tag in zip(late_sums + early_sums, list(_chip_exchange(late_sums)) + list(early_arrived), tags)]
    is_south = lax.axis_index("c") == 0
    gsh = _unpack_grad_shard(*[jnp.concatenate([jnp.where(is_south, mine, other), jnp.where(is_south, other, mine)],
                                               axis=0) for mine, other in zip(reduced, _pair_share(reduced))])

    tail = jnp.concatenate([gr["dn_norm"], gr["alog"][:, :N_HEADS], gr["dtb"][:, :N_HEADS], loss_part[:, :1]], axis=1)
    small = jnp.concatenate([gr["norm1"], gr["norm2"], gr["normf"],
                             jnp.pad(tail, ((0, 0), (0, D_MODEL - tail.shape[1]))),
                             jnp.zeros((SMALL_ROWS - 4, D_MODEL), F32)], axis=0)
    small = _small_allreduce(small)
    at = HEAD_DIM
    g_small = {"norm1_w": small[0:1], "norm2_w": small[1:2], "norm_f_w": small[2],
               "dn_norm_w": small[3:4, :at], "dn_A_log": small[3:4, at:at + N_HEADS],
               "dn_dt_bias": small[3:4, at + N_HEADS:at + 2 * N_HEADS]}
    loss = small[3, at + 2 * N_HEADS]

    big = {"w_in": (w_in, m_w_in, v_w_in, gsh["w_in"]), "dn_conv_w": (dn_conv_w, m_dn_conv_w, v_dn_conv_w, gsh["dn_conv"]),
           "w_proj_dn": (w_proj_dn, m_w_proj_dn, v_w_proj_dn, gsh["wp_dn"]),
           "w_proj_sb": (w_proj_sb, m_w_proj_sb, v_w_proj_sb, gsh["wp_sb"]),
           "w_out": (w_out, m_w_out, v_w_out, gsh["w_out"]),
           "ffn_w_up": (ffn_w_up, m_ffn_w_up, v_ffn_w_up, gsh["w_up"]),
           "ffn_conv_w": (ffn_conv_w, m_ffn_conv_w, v_ffn_conv_w, gsh["ffn_conv"]),
           "ffn_w_down": (ffn_w_down, m_ffn_w_down, v_ffn_w_down, gsh["w_down"])}
    res = {}
    for nm, (w, m, v, g) in big.items():
        d, nm_, nv_ = _adamw(w[0], g, m[0], v[0], "adamw_" + nm)
        res[nm] = (g[None], d[None], nm_[None], nv_[None])

    names = ["norm1_w", "norm2_w", "norm_f_w", "dn_norm_w", "dn_A_log", "dn_dt_bias"]
    given = {"norm1_w": (norm1_w, m_norm1_w, v_norm1_w), "norm2_w": (norm2_w, m_norm2_w, v_norm2_w),
             "norm_f_w": (norm_f_w, m_norm_f_w, v_norm_f_w), "dn_norm_w": (dn_norm_w, m_dn_norm_w, v_dn_norm_w),
             "dn_A_log": (dn_A_log, m_dn_A_log, v_dn_A_log), "dn_dt_bias": (dn_dt_bias, m_dn_dt_bias, v_dn_dt_bias)}

    def stack(k, fill):
        rows = [jnp.pad(given[nm][k].reshape(1, -1), ((0, 0), (0, D_MODEL - given[nm][k].size)),
                        constant_values=fill) for nm in names]
        return jnp.concatenate(rows + [jnp.full((SMALL_ROWS - len(names), D_MODEL), fill, F32)], axis=0)

    g_rows = jnp.concatenate(
        [jnp.pad(g_small[nm].reshape(1, -1), ((0, 0), (0, D_MODEL - g_small[nm].size))) for nm in names]
        + [jnp.zeros((SMALL_ROWS - len(names), D_MODEL), F32)], axis=0)
    d_s, m_s, v_s = _adamw(stack(0, 0.0), g_rows, stack(1, 0.0), stack(2, 1.0), "adamw_small")
    for r, nm in enumerate(names):
        shape = given[nm][0].shape
        n = given[nm][0].size
        res[nm] = (g_small[nm].reshape(shape), d_s[r, :n].reshape(shape), m_s[r, :n].reshape(shape),
                   v_s[r, :n].reshape(shape))

    order = ["norm1_w", "w_in", "dn_conv_w", "dn_A_log", "dn_dt_bias", "dn_norm_w", "w_proj_dn", "w_proj_sb",
             "w_out", "norm2_w", "ffn_w_up", "ffn_conv_w", "ffn_w_down", "norm_f_w"]
    outs = [loss, grad_x[None]]
    for k in range(4):
        outs += [res[nm][k] for nm in order]
    return tuple(outs)
```

```python
import functools

import jax
import jax.numpy as jnp
from jax import lax
from jax.experimental import pallas as pl
from jax.experimental.pallas import tpu as pltpu

F32 = jnp.float32
BF16 = jnp.bfloat16
MESH = pl.DeviceIdType.MESH

EPS = 1e-6
D_MODEL = 1024
N_HEADS = 8
HEAD_DIM = 128
DN_CONV = 4
DN_CHUNK = 64
D_FF = 2816
FFN_CONV = 3
ADAM_LR, ADAM_B1, ADAM_B2, ADAM_EPS, ADAM_WD, ADAM_STEP = 0.001, 0.9, 0.999, 1e-08, 0.01, 10

N_CHIPS = 4
LANES = 128
HALO = 8
VMEM_LIMIT = 48 * 1024 * 1024
SMALL_ROWS = 8


def _params(sem=None):
    return pltpu.CompilerParams(dimension_semantics=sem, vmem_limit_bytes=VMEM_LIMIT)


def _pick(n, target):
    best = None
    for b in range(LANES, min(n, target) + 1, LANES):
        if n % b == 0:
            best = b
    return best or n


ELEMENTWISE_COLS = 1408


def _rows(t, target=256):
    return min(t, target)


def _dot(a, b, precision=None):
    return lax.dot_general(a, b, (((1,), (0,)), ((), ())), precision=precision, preferred_element_type=F32)


def _dot_nt(a, b, precision=None):
    return lax.dot_general(a, b, (((1,), (1,)), ((), ())), precision=precision, preferred_element_type=F32)


def _dot_tn(a, b, precision=None):
    return lax.dot_general(a, b, (((0,), (0,)), ((), ())), precision=precision, preferred_element_type=F32)


def _rms(x, w):
    return x * lax.rsqrt(jnp.mean(x * x, axis=-1, keepdims=True) + EPS) * w


def _silu(x):
    return x * jax.nn.sigmoid(x)


def _softplus(x):
    return jnp.maximum(x, 0.0) + jnp.log(1.0 + jnp.exp(-jnp.abs(x)))


MM_BLOCK = 1408
MM_VMEM_BUDGET = 38 * 1024 * 1024


def _mm(a, b, *, ta=False, tb=False, add=None, out_dtype=F32, name, bm=MM_BLOCK, bn=MM_BLOCK, bk=MM_BLOCK):
    m = a.shape[1] if ta else a.shape[0]
    k = a.shape[0] if ta else a.shape[1]
    n = b.shape[0] if tb else b.shape[1]
    bm, bn = _pick(m, bm), _pick(n, bn)

    def vmem_need(bk_):
        need = 2 * (bm * bk_ * a.dtype.itemsize + bk_ * bn * b.dtype.itemsize) + 2 * bm * bn * jnp.dtype(out_dtype).itemsize
        need += 2 * bm * bn * add.dtype.itemsize if add is not None else 0
        return need + (bm * bn * 4 if bk_ < k else 0)

    bk = max((d for d in range(LANES, k + 1, LANES) if k % d == 0 and vmem_need(d) <= MM_VMEM_BUDGET),
             default=_pick(k, bk))
    nk = k // bk
    dims = (((0 if ta else 1,), (1 if tb else 0,)), ((), ()))

    def body(*refs):
        a_ref, b_ref = refs[:2]
        c_ref = refs[2] if add is not None else None
        o_ref = refs[3] if add is not None else refs[2]
        acc = refs[-1]
        kk = pl.program_id(2)
        part = lax.dot_general(a_ref[...].astype(BF16), b_ref[...].astype(BF16), dims, preferred_element_type=F32)

        def finish(r):
            if add is not None:
                r = r + c_ref[...].astype(F32)
            o_ref[...] = r.astype(out_dtype)

        if nk == 1:
            finish(part)
            return

        @pl.when(kk == 0)
        def _():
            acc[...] = part

        @pl.when(jnp.logical_and(kk > 0, kk < nk - 1))
        def _():
            acc[...] += part

        @pl.when(kk == nk - 1)
        def _():
            finish(acc[...] + part)

    a_spec = (pl.BlockSpec((bk, bm), lambda i, j, kk: (kk, i)) if ta
              else pl.BlockSpec((bm, bk), lambda i, j, kk: (i, kk)))
    b_spec = (pl.BlockSpec((bn, bk), lambda i, j, kk: (j, kk)) if tb
              else pl.BlockSpec((bk, bn), lambda i, j, kk: (kk, j)))
    o_spec = pl.BlockSpec((bm, bn), lambda i, j, kk: (i, j))
    in_specs = [a_spec, b_spec] + ([o_spec] if add is not None else [])
    args = (a, b) + ((add,) if add is not None else ())
    return pl.pallas_call(
        body, name=name, grid=(m // bm, n // bn, nk),
        in_specs=in_specs, out_specs=o_spec,
        out_shape=jax.ShapeDtypeStruct((m, n), out_dtype),
        scratch_shapes=[pltpu.VMEM((bm, bn), F32)] if nk > 1 else [],
        compiler_params=_params(("parallel", "parallel", "arbitrary")),
    )(*args)


def _norm1_fwd(x, w, w_ab):
    t = x.shape[0]
    tb = _rows(t)

    def body(x_ref, w_ref, wab_ref, n_ref, hab_ref):
        n = _rms(x_ref[...], w_ref[...]).astype(BF16)
        n_ref[...] = n
        hab_ref[...] = _dot(n, wab_ref[...])

    return pl.pallas_call(
        body, name="norm1_fwd", grid=(t // tb,),
        in_specs=[pl.BlockSpec((tb, D_MODEL), lambda i: (i, 0)),
                  pl.BlockSpec((1, D_MODEL), lambda i: (0, 0)),
                  pl.BlockSpec((D_MODEL, LANES), lambda i: (0, 0))],
        out_specs=[pl.BlockSpec((tb, D_MODEL), lambda i: (i, 0)),
                   pl.BlockSpec((tb, LANES), lambda i: (i, 0))],
        out_shape=[jax.ShapeDtypeStruct((t, D_MODEL), BF16), jax.ShapeDtypeStruct((t, LANES), F32)],
        compiler_params=_params(("arbitrary",)),
    )(x, w, w_ab)


def _norm1_bwd(x, w, dn, dres, dab, w_ab):
    t = x.shape[0]
    tb = _rows(t)

    def body(x_ref, w_ref, dn_ref, dres_ref, dab_ref, wab_ref, dx_ref, dw_ref):
        i = pl.program_id(0)
        g = dn_ref[...] + _dot_nt(dab_ref[...].astype(BF16), wab_ref[...])
        _, vjp = jax.vjp(_rms, x_ref[...], w_ref[...])
        dx, dw = vjp(g)
        dx_ref[...] = dres_ref[...] + dx

        @pl.when(i == 0)
        def _():
            dw_ref[...] = jnp.zeros_like(dw_ref)

        dw_ref[...] += dw

    row = pl.BlockSpec((tb, D_MODEL), lambda i: (i, 0))
    vec = pl.BlockSpec((1, D_MODEL), lambda i: (0, 0))
    return pl.pallas_call(
        body, name="norm1_bwd", grid=(t // tb,),
        in_specs=[row, vec, row, row, pl.BlockSpec((tb, LANES), lambda i: (i, 0)),
                  pl.BlockSpec((D_MODEL, LANES), lambda i: (0, 0))],
        out_specs=[row, vec],
        out_shape=[jax.ShapeDtypeStruct((t, D_MODEL), F32), jax.ShapeDtypeStruct((1, D_MODEL), F32)],
        compiler_params=_params(("arbitrary",)),
    )(x, w, dn, dres, dab, w_ab)


def _conv_fwd(x, w, name):
    t, c = x.shape
    kk = w.shape[0]
    tb, cb = _rows(t, 512), _pick(c, ELEMENTWISE_COLS)
    per = tb // HALO

    def body(x_ref, halo_ref, w_ref, y_ref, buf):
        i = pl.program_id(0)
        buf[pl.ds(HALO, tb), :] = x_ref[...]
        buf[pl.ds(0, HALO), :] = jnp.where(i == 0, 0.0, halo_ref[...])
        y_ref[...] = _conv_taps(buf, w_ref, HALO - (kk - 1), tb)

    return pl.pallas_call(
        body, name=name, grid=(t // tb, c // cb),
        in_specs=[pl.BlockSpec((tb, cb), lambda i, j: (i, j)),
                  pl.BlockSpec((HALO, cb), lambda i, j: (jnp.maximum(i * per - 1, 0), j)),
                  pl.BlockSpec((kk, cb), lambda i, j: (0, j))],
        out_specs=pl.BlockSpec((tb, cb), lambda i, j: (i, j)),
        out_shape=jax.ShapeDtypeStruct((t, c), F32),
        scratch_shapes=[pltpu.VMEM((tb + HALO, cb), F32)],
        compiler_params=_params(("parallel", "parallel")),
    )(x, x, w)


def _conv_bwd(dy, x, w, name, dx_dtype):
    t, c = x.shape
    kk = w.shape[0]
    tb, cb = _rows(t, 512), _pick(c, ELEMENTWISE_COLS)
    per = tb // HALO
    nblk = t // tb

    def body(dy_ref, after_ref, x_ref, w_ref, dx_ref, dw_ref, dbuf):
        i = pl.program_id(1)
        dbuf[pl.ds(0, tb), :] = dy_ref[...]
        dbuf[pl.ds(tb, HALO), :] = jnp.where(i == nblk - 1, 0.0, after_ref[...])

        @pl.when(i == 0)
        def _():
            dw_ref[...] = jnp.zeros_like(dw_ref)

        for j in range(cb // LANES):
            sl = pl.ds(j * LANES, LANES)
            x = x_ref[:, sl]
            dx = None
            for s in range(kk):
                shifted = dbuf[pl.ds(kk - 1 - s, tb), sl]
                term = w_ref[s:s + 1, sl] * shifted
                dx = term if dx is None else dx + term
                dw_ref[s:s + 1, sl] += jnp.sum(shifted * x, axis=0, keepdims=True)
            dx_ref[:, sl] = dx.astype(dx_dtype)

    blk = pl.BlockSpec((tb, cb), lambda j, i: (i, j))
    return pl.pallas_call(
        body, name=name, grid=(c // cb, nblk),
        in_specs=[blk,
                  pl.BlockSpec((HALO, cb), lambda j, i: (jnp.minimum((i + 1) * per, t // HALO - 1), j)),
                  blk,
                  pl.BlockSpec((kk, cb), lambda j, i: (0, j))],
        out_specs=[blk, pl.BlockSpec((HALO, cb), lambda j, i: (0, j))],
        out_shape=[jax.ShapeDtypeStruct((t, c), dx_dtype), jax.ShapeDtypeStruct((HALO, c), F32)],
        scratch_shapes=[pltpu.VMEM((tb + HALO, cb), F32)],
        compiler_params=_params(("parallel", "arbitrary")),
    )(dy, dy, x, w)


def _dn_head(c, normed):
    s = _silu(c)
    return s * lax.rsqrt(jnp.sum(s * s, axis=-1, keepdims=True) + EPS) if normed else s


def _dn_gates(hab, alog, dtb):
    lane = lax.broadcasted_iota(jnp.int32, hab.shape, 1)
    g = -jnp.exp(alog) * _softplus(hab + dtb)
    beta = jax.nn.sigmoid(hab)
    return jnp.where(lane < N_HEADS, g, jnp.where(lane < 2 * N_HEADS, beta, 0.0))


def _dn_head_slices(q_ref, k_ref, v_ref):
    return [(pl.ds((part * N_HEADS + h) * HEAD_DIM, HEAD_DIM), ref, h, part < 2)
            for part, ref in enumerate((q_ref, k_ref, v_ref)) for h in range(N_HEADS)]


def _dn_prep_fwd(c, hab, alog, dtb):
    t = c.shape[0]
    tb = _rows(t)

    def body(c_ref, hab_ref, alog_ref, dtb_ref, q_ref, k_ref, v_ref, gb_ref):
        for sl, ref, h, normed in _dn_head_slices(q_ref, k_ref, v_ref):
            ref[h] = _dn_head(c_ref[:, sl], normed)
        gb_ref[...] = _dn_gates(hab_ref[...], alog_ref[...], dtb_ref[...])

    hm = pl.BlockSpec((N_HEADS, tb, HEAD_DIM), lambda i: (0, i, 0))
    nar = pl.BlockSpec((tb, LANES), lambda i: (i, 0))
    vec = pl.BlockSpec((1, LANES), lambda i: (0, 0))
    return pl.pallas_call(
        body, name="dn_prep_fwd", grid=(t // tb,),
        in_specs=[pl.BlockSpec((tb, 3 * D_MODEL), lambda i: (i, 0)), nar, vec, vec],
        out_specs=[hm, hm, hm, nar],
        out_shape=[jax.ShapeDtypeStruct((N_HEADS, t, HEAD_DIM), F32)] * 3 + [jax.ShapeDtypeStruct((t, LANES), F32)],
        compiler_params=_params(("parallel",)),
    )(c, hab, alog, dtb)


def _dn_prep_bwd(c, hab, alog, dtb, dq, dk, dv, dgb):
    t = c.shape[0]
    tb = _rows(t)

    def body(c_ref, hab_ref, alog_ref, dtb_ref, dq_ref, dk_ref, dv_ref, dgb_ref,
             dc_ref, dhab_ref, dalog_ref, ddtb_ref):
        i = pl.program_id(0)
        for sl, ref, h, normed in _dn_head_slices(dq_ref, dk_ref, dv_ref):
            _, vjp = jax.vjp(functools.partial(_dn_head, normed=normed), c_ref[:, sl])
            dc_ref[:, sl] = vjp(ref[h])[0]
        _, vjp = jax.vjp(_dn_gates, hab_ref[...], alog_ref[...], dtb_ref[...])
        dhab, dalog, ddtb = vjp(dgb_ref[...])
        dhab_ref[...] = dhab

        @pl.when(i == 0)
        def _():
            dalog_ref[...] = jnp.zeros_like(dalog_ref)
            ddtb_ref[...] = jnp.zeros_like(ddtb_ref)

        dalog_ref[...] += dalog
        ddtb_ref[...] += ddtb

    hm = pl.BlockSpec((N_HEADS, tb, HEAD_DIM), lambda i: (0, i, 0))
    wide = pl.BlockSpec((tb, 3 * D_MODEL), lambda i: (i, 0))
    nar = pl.BlockSpec((tb, LANES), lambda i: (i, 0))
    vec = pl.BlockSpec((1, LANES), lambda i: (0, 0))
    return pl.pallas_call(
        body, name="dn_prep_bwd", grid=(t // tb,),
        in_specs=[wide, nar, vec, vec, hm, hm, hm, nar],
        out_specs=[wide, nar, vec, vec],
        out_shape=[jax.ShapeDtypeStruct((t, 3 * D_MODEL), F32), jax.ShapeDtypeStruct((t, LANES), F32),
                   jax.ShapeDtypeStruct((1, LANES), F32), jax.ShapeDtypeStruct((1, LANES), F32)],
        compiler_params=_params(("arbitrary",)),
    )(c, hab, alog, dtb, dq, dk, dv, dgb)


DN_PREC = lax.Precision.HIGH
DN_GROUP = 16


def _dn_prec(a):
    return DN_PREC if a.dtype == F32 else None


def _bdot(a, b):
    return lax.dot_general(a, b, (((2,), (1,)), ((0,), (0,))), precision=_dn_prec(a), preferred_element_type=F32)


def _bdot_nt(a, b):
    return lax.dot_general(a, b, (((2,), (2,)), ((0,), (0,))), precision=_dn_prec(a), preferred_element_type=F32)


def _bdot_tn(a, b):
    return lax.dot_general(a, b, (((1,), (1,)), ((0,), (0,))), precision=_dn_prec(a), preferred_element_type=F32)


def _unit_lower_inverse(lmat):
    c = lmat.shape[-1]
    ri = lax.broadcasted_iota(jnp.int32, (c, c), 0)
    ci = lax.broadcasted_iota(jnp.int32, (c, c), 1)
    p = -lmat
    tinv = jnp.where(ri == ci, 1.0, 0.0) + p
    for _ in range(max(c.bit_length() - 2, 0)):
        p = _bdot(p, p)
        tinv = tinv + _bdot(tinv, p)
    return tinv


@jax.custom_vjp
def _solve_with(lmat, rhs, tinv):
    return _bdot(tinv, rhs)


def _solve_with_fwd(lmat, rhs, tinv):
    sol = _bdot(tinv, rhs)
    return sol, (sol, tinv)


def _solve_with_bwd(res, dsol):
    sol, tinv = res
    drhs = _bdot_tn(tinv, dsol)
    return -_bdot_nt(drhs, sol), drhs, jnp.zeros_like(tinv)


_solve_with.defvjp(_solve_with_fwd, _solve_with_bwd)


def _dn_local(q, k, v, grow, brow, tinv):
    g, c, _ = q.shape
    ri = lax.broadcasted_iota(jnp.int32, (c, c), 0)
    ci = lax.broadcasted_iota(jnp.int32, (c, c), 1)
    lower = ri >= ci
    as_col = lambda r: jnp.sum(jnp.where(ri == ci, jnp.broadcast_to(r, (g, c, c)), 0.0), axis=2, keepdims=True)
    gcol, bcol = as_col(grow), as_col(brow)
    gc_col = jnp.sum(jnp.where(lower, jnp.broadcast_to(grow, (g, c, c)), 0.0), axis=2, keepdims=True)
    gc_row = jnp.sum(jnp.where(ri <= ci, jnp.broadcast_to(gcol, (g, c, c)), 0.0), axis=1, keepdims=True)
    qs = q * (HEAD_DIM ** -0.5)
    kb = k * bcol
    vb = v * bcol
    decay = jnp.where(lower, jnp.exp(jnp.where(lower, gc_col - gc_row, 0.0)), 0.0)
    lmat = jnp.where(ri > ci, _bdot_nt(kb.astype(BF16), k.astype(BF16)) * decay, 0.0)
    eg = jnp.exp(gc_col)
    rhs = jnp.concatenate([vb, kb * eg], axis=2)
    if tinv is None:
        tinv = _unit_lower_inverse(lmat)
    sol = _solve_with(lmat, rhs, tinv)
    a_qk = jnp.where(lower, _bdot_nt(qs.astype(BF16), k.astype(BF16)) * decay, 0.0)
    g_last = jnp.sum(grow, axis=2, keepdims=True)
    kdec = k * jnp.exp(g_last - gc_col)
    egl = jnp.broadcast_to(jnp.exp(g_last), (g, 1, HEAD_DIM))
    return sol[:, :, :HEAD_DIM], sol[:, :, HEAD_DIM:], a_qk, qs * eg, kdec, egl, tinv


def _dn_seq(u, w, a_qk, qe, kdec, egl, s_in):
    b16 = lambda x: x.astype(BF16)
    v_new = u - _bdot(b16(w), b16(s_in))
    o = _bdot(b16(qe), b16(s_in)) + _bdot(b16(a_qk), b16(v_new))
    return o, s_in * egl + _bdot_tn(b16(kdec), b16(v_new))


def _dn_local_specs(t):
    grp = min(DN_GROUP, t // DN_CHUNK)
    rows = grp * DN_CHUNK
    blk = pl.BlockSpec((1, rows, HEAD_DIM), lambda h, i: (h, i, 0))
    row = pl.BlockSpec((1, grp, 1, DN_CHUNK), lambda h, i: (h, i, 0, 0))
    sq = pl.BlockSpec((1, grp, DN_CHUNK, DN_CHUNK), lambda h, i: (h, i, 0, 0))
    lane = pl.BlockSpec((1, grp, 1, HEAD_DIM), lambda h, i: (h, i, 0, 0))
    return grp, blk, row, sq, lane


def _dn_shapes(t):
    nchunk = t // DN_CHUNK
    big = jax.ShapeDtypeStruct((N_HEADS, t, HEAD_DIM), F32)
    row = jax.ShapeDtypeStruct((N_HEADS, nchunk, 1, DN_CHUNK), F32)
    sq = jax.ShapeDtypeStruct((N_HEADS, nchunk, DN_CHUNK, DN_CHUNK), F32)
    lane = jax.ShapeDtypeStruct((N_HEADS, nchunk, 1, HEAD_DIM), F32)
    return big, row, sq, lane


def _dn_local_fwd(q, k, v, grow, brow, wire=()):
    t = q.shape[1]
    grp, blk, row, sq, lane = _dn_local_specs(t)
    big, _, sqs, lanes = _dn_shapes(t)
    n = len(wire)
    groups = t // (grp * DN_CHUNK)
    steps = N_HEADS * groups

    def body(q_ref, k_ref, v_ref, gr_ref, br_ref, *rest):
        u_ref, w_ref, a_ref, qe_ref, kd_ref, egl_ref, t_ref = rest[n:n + 7]
        if n:
            begin, middle, end = _gather_protocol(rest[:n], rest[n + 7:2 * n + 7], *rest[2 * n + 7:])
            step = pl.program_id(0) * groups + pl.program_id(1)
            pl.when(step == 0)(begin)
            pl.when(step == (GATHER_FORWARD_AT * steps) // 8)(middle)
        split = lambda r: r[0].reshape(grp, DN_CHUNK, HEAD_DIM)
        u, w, a_qk, qe, kdec, egl, tinv = _dn_local(split(q_ref), split(k_ref), split(v_ref), gr_ref[0],
                                                     br_ref[0], None)
        for ref, val in ((u_ref, u), (w_ref, w), (qe_ref, qe), (kd_ref, kdec)):
            ref[0] = val.reshape(grp * DN_CHUNK, HEAD_DIM)
        a_ref[0] = a_qk
        egl_ref[0] = egl
        t_ref[0] = tinv
        if n:
            pl.when(step == steps - 1)(end)

    assert n == 0 or steps >= 3
    return pl.pallas_call(
        body, name="dn_local_fwd", grid=(N_HEADS, groups),
        in_specs=[blk, blk, blk, row, row] + _hbm_specs(n),
        out_specs=[blk, blk, sq, blk, blk, lane, sq] + _hbm_specs(n),
        out_shape=[big, big, sqs, big, big, lanes, sqs] + _gather_out_shapes(wire),
        scratch_shapes=_gather_sems(n) if n else [],
        compiler_params=_params(("arbitrary", "arbitrary")),
    )(q, k, v, grow, brow, *wire)


def _dn_local_bwd(q, k, v, grow, brow, tinv, du, dw, da, dqe, dkd, degl):
    t = q.shape[1]
    grp, blk, row, sq, lane = _dn_local_specs(t)
    big, rows_, _, _ = _dn_shapes(t)

    def body(q_ref, k_ref, v_ref, gr_ref, br_ref, t_ref, du_ref, dw_ref, da_ref, dqe_ref, dkd_ref,
             degl_ref, dq_ref, dk_ref, dv_ref, dgr_ref, dbr_ref):
        split = lambda r: r[0].reshape(grp, DN_CHUNK, HEAD_DIM)
        tinv_v = t_ref[0]
        fn = lambda q_, k_, v_, gr_, br_: _dn_local(q_, k_, v_, gr_, br_, tinv_v)[:6]
        _, vjp = jax.vjp(fn, split(q_ref), split(k_ref), split(v_ref), gr_ref[0], br_ref[0])
        dq, dk, dv, dgr, dbr = vjp((split(du_ref), split(dw_ref), da_ref[0], split(dqe_ref), split(dkd_ref),
                                    degl_ref[0]))
        for ref, val in ((dq_ref, dq), (dk_ref, dk), (dv_ref, dv)):
            ref[0] = val.reshape(grp * DN_CHUNK, HEAD_DIM)
        dgr_ref[0] = dgr
        dbr_ref[0] = dbr

    return pl.pallas_call(
        body, name="dn_local_bwd", grid=(N_HEADS, t // (grp * DN_CHUNK)),
        in_specs=[blk, blk, blk, row, row, sq, blk, blk, sq, blk, blk, lane],
        out_specs=[blk, blk, blk, row, row],
        out_shape=[big, big, big, rows_, rows_],
        compiler_params=_params(("parallel", "parallel")),
    )(q, k, v, grow, brow, tinv, du, dw, da, dqe, dkd, degl)


DN_SEQ_CHUNKS = 4


def _dn_seq_specs(nchunk, rev):
    per = min(DN_SEQ_CHUNKS, nchunk)
    nstep = nchunk // per

    def idx(n):
        return nstep - 1 - n if rev else n

    blk = pl.BlockSpec((N_HEADS, per * DN_CHUNK, HEAD_DIM), lambda n: (0, idx(n), 0))
    sq = pl.BlockSpec((N_HEADS, per, DN_CHUNK, DN_CHUNK), lambda n: (0, idx(n), 0, 0))
    lane = pl.BlockSpec((N_HEADS, per, 1, HEAD_DIM), lambda n: (0, idx(n), 0, 0))
    st = pl.BlockSpec((N_HEADS, per, HEAD_DIM, HEAD_DIM), lambda n: (0, idx(n), 0, 0))
    return per, nstep, blk, sq, lane, st


def _dn_seq_fwd(u, w, a_qk, qe, kdec, egl):
    t = u.shape[1]
    nchunk = t // DN_CHUNK
    per, nstep, blk, sq, lane, st = _dn_seq_specs(nchunk, False)

    def body(u_ref, w_ref, a_ref, qe_ref, kd_ref, egl_ref, o_ref, s_ref, state):
        @pl.when(pl.program_id(0) == 0)
        def _():
            state[...] = jnp.zeros_like(state)

        for c in range(per):
            rows = pl.ds(c * DN_CHUNK, DN_CHUNK)
            s_in = state[...]
            s_ref[:, c] = s_in
            o_ref[:, rows], state[...] = _dn_seq(u_ref[:, rows], w_ref[:, rows], a_ref[:, c], qe_ref[:, rows],
                                                 kd_ref[:, rows], egl_ref[:, c], s_in)

    return pl.pallas_call(
        body, name="dn_seq_fwd", grid=(nstep,),
        in_specs=[blk, blk, sq, blk, blk, lane],
        out_specs=[blk, st],
        out_shape=[jax.ShapeDtypeStruct((N_HEADS, t, HEAD_DIM), F32),
                   jax.ShapeDtypeStruct((N_HEADS, nchunk, HEAD_DIM, HEAD_DIM), F32)],
        scratch_shapes=[pltpu.VMEM((N_HEADS, HEAD_DIM, HEAD_DIM), F32)],
        compiler_params=_params(("arbitrary",)),
    )(u, w, a_qk, qe, kdec, egl)


def _dn_seq_bwd(u, w, a_qk, qe, kdec, egl, states, do):
    t = u.shape[1]
    nchunk = t // DN_CHUNK
    per, nstep, blk, sq, lane, st = _dn_seq_specs(nchunk, True)
    big, _, sqs, lanes = _dn_shapes(t)

    def body(u_ref, w_ref, a_ref, qe_ref, kd_ref, egl_ref, s_ref, do_ref,
             du_ref, dw_ref, da_ref, dqe_ref, dkd_ref, degl_ref, dstate):
        @pl.when(pl.program_id(0) == 0)
        def _():
            dstate[...] = jnp.zeros_like(dstate)

        for c in reversed(range(per)):
            rows = pl.ds(c * DN_CHUNK, DN_CHUNK)
            _, vjp = jax.vjp(_dn_seq, u_ref[:, rows], w_ref[:, rows], a_ref[:, c], qe_ref[:, rows], kd_ref[:, rows],
                             egl_ref[:, c], s_ref[:, c])
            (du_ref[:, rows], dw_ref[:, rows], da_ref[:, c], dqe_ref[:, rows], dkd_ref[:, rows], degl_ref[:, c],
             dstate[...]) = vjp((do_ref[:, rows], dstate[...]))

    return pl.pallas_call(
        body, name="dn_seq_bwd", grid=(nstep,),
        in_specs=[blk, blk, sq, blk, blk, lane, st, blk],
        out_specs=[blk, blk, sq, blk, blk, lane],
        out_shape=[big, big, sqs, big, big, lanes],
        scratch_shapes=[pltpu.VMEM((N_HEADS, HEAD_DIM, HEAD_DIM), F32)],
        compiler_params=_params(("arbitrary",)),
    )(u, w, a_qk, qe, kdec, egl, states, do)


def _dn_post_head(o, gate, w):
    return _rms(o, w) * _silu(gate)


def _dn_post_fwd(o, gate, w):
    t = gate.shape[0]
    tb = _rows(t)

    def body(o_ref, g_ref, w_ref, y_ref):
        for h in range(N_HEADS):
            sl = pl.ds(h * HEAD_DIM, HEAD_DIM)
            y_ref[:, sl] = _dn_post_head(o_ref[h], g_ref[:, sl], w_ref[...]).astype(BF16)

    row = pl.BlockSpec((tb, D_MODEL), lambda i: (i, 0))
    hm = pl.BlockSpec((N_HEADS, tb, HEAD_DIM), lambda i: (0, i, 0))
    return pl.pallas_call(
        body, name="dn_post_fwd", grid=(t // tb,),
        in_specs=[hm, row, pl.BlockSpec((1, HEAD_DIM), lambda i: (0, 0))],
        out_specs=row, out_shape=jax.ShapeDtypeStruct((t, D_MODEL), BF16),
        compiler_params=_params(("parallel",)),
    )(o, gate, w)


def _dn_post_bwd(o, gate, w, dy):
    t = gate.shape[0]
    tb = _rows(t)

    def body(o_ref, g_ref, w_ref, dy_ref, do_ref, dg_ref, dw_ref):
        i = pl.program_id(0)
        @pl.when(i == 0)
        def _():
            dw_ref[...] = jnp.zeros_like(dw_ref)

        for h in range(N_HEADS):
            sl = pl.ds(h * HEAD_DIM, HEAD_DIM)
            _, vjp = jax.vjp(_dn_post_head, o_ref[h], g_ref[:, sl], w_ref[...])
            do_ref[h], dg, dw = vjp(dy_ref[:, sl])
            dg_ref[:, sl] = dg.astype(BF16)
            dw_ref[...] += dw

    row = pl.BlockSpec((tb, D_MODEL), lambda i: (i, 0))
    hm = pl.BlockSpec((N_HEADS, tb, HEAD_DIM), lambda i: (0, i, 0))
    vec = pl.BlockSpec((1, HEAD_DIM), lambda i: (0, 0))
    return pl.pallas_call(
        body, name="dn_post_bwd", grid=(t // tb,),
        in_specs=[hm, row, vec, row],
        out_specs=[hm, row, vec],
        out_shape=[jax.ShapeDtypeStruct((N_HEADS, t, HEAD_DIM), F32), jax.ShapeDtypeStruct((t, D_MODEL), BF16),
                   jax.ShapeDtypeStruct((1, HEAD_DIM), F32)],
        compiler_params=_params(("arbitrary",)),
    )(o, gate, w, dy)


def _split_bf16(x):
    hi = x.astype(BF16)
    lo = (x - hi.astype(F32)).astype(BF16)
    return hi, lo


SB_Q_BLOCK = 512
SB_K_BLOCK = 256
SB_NEGLIGIBLE = -60.0


def _sb_logits(q, kb, mask, scale):
    z = _dot_nt(q, kb) * scale
    ls = jnp.minimum(z, 0.0) - jnp.log(1.0 + jnp.exp(-jnp.abs(z)))
    lk = ls - z
    if mask is not None:
        lk = jnp.where(mask, lk, 0.0)
    return ls, lk


def _sb_blocks(t):
    bq = min(SB_Q_BLOCK, t)
    bk = min(SB_K_BLOCK, bq)
    return bq, bk, bq // bk


def _sb_fwd(qkv):
    t = qkv.shape[0]
    bq, bk, nd = _sb_blocks(t)
    scale = HEAD_DIM ** -0.5

    def body(q_ref, k_ref, v_ref, o_ref, tot_ref, used_ref):
        i = pl.program_id(1)
        q = q_ref[...]
        rj = lax.broadcasted_iota(jnp.int32, (bk, bk), 0)
        cj = lax.broadcasted_iota(jnp.int32, (bk, bk), 1)
        after = (rj > cj).astype(BF16)
        trow = lax.broadcasted_iota(jnp.int32, (bq, bk), 0)
        scol = lax.broadcasted_iota(jnp.int32, (bq, bk), 1)

        def tile(j, run, acc, mask):
            off = pl.multiple_of(j * bk, bk)
            kb = k_ref[pl.ds(off, bk), :]
            vb = v_ref[pl.ds(off, bk), :]
            ls, lk = _sb_logits(q, kb, mask, scale)
            hi, lo = _split_bf16(lk)
            between = _dot(hi, after) + _dot(lo, after) + run
            a = jnp.exp(ls + between)
            if mask is not None:
                a = jnp.where(mask, a, 0.0)
            acc = acc + _dot(a.astype(BF16), vb)
            return run + jnp.sum(lk, axis=1, keepdims=True), acc

        run, acc = jnp.zeros((bq, 1), F32), jnp.zeros((bq, HEAD_DIM), F32)
        for d in reversed(range(nd)):
            run, acc = tile(i * nd + d, run, acc, scol + d * bk < trow)
        def more(c):
            return jnp.logical_and(c[0] < i * nd, jnp.max(c[1]) > SB_NEGLIGIBLE)

        def far(c):
            run_, acc_ = tile(i * nd - 1 - c[0], c[1], c[2], None)
            return c[0] + 1, run_, acc_

        used, run, acc = lax.while_loop(more, far, (jnp.int32(0), run, acc))
        o_ref[...] = acc.astype(BF16)
        tot_ref[...] = jnp.broadcast_to(run, (bq, HEAD_DIM))
        used_ref[...] = jnp.full(used_ref.shape, used, F32)

    qs = pl.BlockSpec((bq, HEAD_DIM), lambda h, i: (i, h))
    ks = pl.BlockSpec((t, HEAD_DIM), lambda h, i: (0, N_HEADS + h))
    vs = pl.BlockSpec((t, HEAD_DIM), lambda h, i: (0, 2 * N_HEADS + h))
    return pl.pallas_call(
        body, name="sb_fwd", grid=(N_HEADS, t // bq),
        in_specs=[qs, ks, vs], out_specs=[qs, qs, pl.BlockSpec((1, 1, 1, LANES), lambda h, i: (h, i, 0, 0))],
        out_shape=[jax.ShapeDtypeStruct((t, D_MODEL), BF16), jax.ShapeDtypeStruct((t, D_MODEL), F32),
                   jax.ShapeDtypeStruct((N_HEADS, t // bq, 1, LANES), F32)],
        compiler_params=_params(("parallel", "arbitrary")),
    )(qkv, qkv, qkv)


def _sb_bwd(qkv, tot, used, do, partials=()):
    t = qkv.shape[0]
    bq, bk, nd = _sb_blocks(t)
    scale = HEAD_DIM ** -0.5
    n = len(partials)
    nq = t // bq

    def body(q_ref, k_ref, v_ref, tot_ref, used_ref, do_ref, *rest):
        dq_ref, dk_ref, dv_ref = rest[n:n + 3]
        i = pl.program_id(1)
        if n:
            begin, end = _chip_exchange_protocol(rest[:n], rest[n + 3:2 * n + 3], *rest[2 * n + 3:])
            step = pl.program_id(0) * nq + i
            pl.when(step == 0)(begin)

        @pl.when(i == 0)
        def _():
            dk_ref[...] = jnp.zeros_like(dk_ref)
            dv_ref[...] = jnp.zeros_like(dv_ref)

        q = q_ref[...]
        do = do_ref[...]
        total = tot_ref[:, 0:1]
        rj = lax.broadcasted_iota(jnp.int32, (bk, bk), 0)
        cj = lax.broadcasted_iota(jnp.int32, (bk, bk), 1)
        upto = (rj <= cj).astype(BF16)
        before = (rj < cj).astype(BF16)
        trow = lax.broadcasted_iota(jnp.int32, (bq, bk), 0)
        scol = lax.broadcasted_iota(jnp.int32, (bq, bk), 1)

        def tile(j, run_k, run_e, dq, mask):
            off = pl.multiple_of(j * bk, bk)
            kb = k_ref[pl.ds(off, bk), :]
            vb = v_ref[pl.ds(off, bk), :]
            ls, lk = _sb_logits(q, kb, mask, scale)
            hi, lo = _split_bf16(lk)
            between = total - (_dot(hi, upto) + _dot(lo, upto) + run_k)
            a = jnp.exp(ls + between)
            if mask is not None:
                a = jnp.where(mask, a, 0.0)
            e = a * _dot_nt(do, vb)
            ehi, elo = _split_bf16(e)
            pre = _dot(ehi, before) + _dot(elo, before) + run_e
            sig = jnp.exp(ls)
            dz = e * (1.0 - sig) - pre * sig
            if mask is not None:
                dz = jnp.where(mask, dz, 0.0)
            dz = (dz * scale).astype(BF16)
            dq = dq + _dot(dz, kb)
            dk_ref[pl.ds(off, bk), :] += _dot_tn(dz, q)
            dv_ref[pl.ds(off, bk), :] += _dot_tn(a.astype(BF16), do)
            return (run_k + jnp.sum(lk, axis=1, keepdims=True),
                    run_e + jnp.sum(e, axis=1, keepdims=True), dq)

        zero = jnp.zeros((bq, 1), F32)
        visited = jnp.clip(jnp.max(used_ref[...]).astype(jnp.int32), 0, i * nd)
        carry = lax.fori_loop(i * nd - visited, i * nd, lambda j, c: tile(j, c[0], c[1], c[2], None),
                              (zero, zero, jnp.zeros((bq, HEAD_DIM), F32)))
        for d in range(nd):
            carry = tile(i * nd + d, *carry, scol + d * bk < trow)
        dq_ref[...] = carry[2]
        if n:
            pl.when(step == N_HEADS * nq - 1)(end)

    qs = pl.BlockSpec((bq, HEAD_DIM), lambda h, i: (i, h))
    ks = pl.BlockSpec((t, HEAD_DIM), lambda h, i: (0, N_HEADS + h))
    vs = pl.BlockSpec((t, HEAD_DIM), lambda h, i: (0, 2 * N_HEADS + h))
    full = pl.BlockSpec((t, HEAD_DIM), lambda h, i: (0, h))
    big = jax.ShapeDtypeStruct((t, D_MODEL), F32)
    return pl.pallas_call(
        body, name="sb_bwd", grid=(N_HEADS, nq),
        in_specs=[qs, ks, vs, qs, pl.BlockSpec((1, 1, 1, LANES), lambda h, i: (h, i, 0, 0)), qs] + _hbm_specs(n),
        out_specs=[qs, full, full] + _hbm_specs(n),
        out_shape=[big, big, big] + _chip_exchange_shapes(partials),
        scratch_shapes=_chip_exchange_sems(n) if n else [],
        compiler_params=_params(("arbitrary", "arbitrary")),
    )(qkv, qkv, qkv, tot, used, do, *partials)


def _merge_fwd(o_dn, o_sb, gl, x, wp_dn, wp_sb, w_out, w2):
    t = x.shape[0]
    tb = _rows(t)

    def body(odn_ref, osb_ref, gl_ref, x_ref, wpd_ref, wps_ref, wo_ref, w2_ref,
             pdn_ref, psb_ref, mix_ref, x1_ref, n2_ref):
        pdn = _dot(odn_ref[...], wpd_ref[...])
        psb = _dot(osb_ref[...], wps_ref[...])
        gates = jax.nn.sigmoid(gl_ref[...])
        mixed = (gates[:, :D_MODEL] * pdn + gates[:, D_MODEL:] * psb).astype(BF16)
        x1 = x_ref[...] + _dot(mixed, wo_ref[...])
        pdn_ref[...] = pdn.astype(BF16)
        psb_ref[...] = psb.astype(BF16)
        mix_ref[...] = mixed
        x1_ref[...] = x1
        n2_ref[...] = _rms(x1, w2_ref[...]).astype(BF16)

    row = pl.BlockSpec((tb, D_MODEL), lambda i: (i, 0))
    sq = pl.BlockSpec((D_MODEL, D_MODEL), lambda i: (0, 0))
    f = jax.ShapeDtypeStruct((t, D_MODEL), F32)
    b = jax.ShapeDtypeStruct((t, D_MODEL), BF16)
    return pl.pallas_call(
        body, name="merge_fwd", grid=(t // tb,),
        in_specs=[row, row, pl.BlockSpec((tb, 2 * D_MODEL), lambda i: (i, 0)), row, sq, sq, sq,
                  pl.BlockSpec((1, D_MODEL), lambda i: (0, 0))],
        out_specs=[row] * 5, out_shape=[b, b, b, f, b],
        compiler_params=_params(("parallel",)),
    )(o_dn, o_sb, gl, x, wp_dn, wp_sb, w_out, w2)


def _merge_bwd(dx2, dn2, x1, w2, gl, pdn, psb, wp_dn, wp_sb, w_out):
    t = x1.shape[0]
    tb = _rows(t)

    def body(dx2_ref, dn2_ref, x1_ref, w2_ref, gl_ref, pdn_ref, psb_ref, wpd_ref, wps_ref, wo_ref,
             dx1_ref, dw2_ref, dgl_ref, dpdn_ref, dpsb_ref, dodn_ref, dosb_ref):
        i = pl.program_id(0)
        _, vjp = jax.vjp(_rms, x1_ref[...], w2_ref[...])
        dxn, dw2 = vjp(dn2_ref[...])
        dx1 = dx2_ref[...] + dxn
        dx1_ref[...] = dx1

        @pl.when(i == 0)
        def _():
            dw2_ref[...] = jnp.zeros_like(dw2_ref)

        dw2_ref[...] += dw2
        dmix = _dot_nt(dx1.astype(BF16), wo_ref[...])
        gates = jax.nn.sigmoid(gl_ref[...])
        g_dn, g_sb = gates[:, :D_MODEL], gates[:, D_MODEL:]
        dpdn = (dmix * g_dn).astype(BF16)
        dpsb = (dmix * g_sb).astype(BF16)
        dgl_ref[:, :D_MODEL] = (dmix * pdn_ref[...].astype(F32) * g_dn * (1.0 - g_dn)).astype(BF16)
        dgl_ref[:, D_MODEL:] = (dmix * psb_ref[...].astype(F32) * g_sb * (1.0 - g_sb)).astype(BF16)
        dpdn_ref[...] = dpdn
        dpsb_ref[...] = dpsb
        dodn_ref[...] = _dot_nt(dpdn, wpd_ref[...])
        dosb_ref[...] = _dot_nt(dpsb, wps_ref[...]).astype(BF16)

    row = pl.BlockSpec((tb, D_MODEL), lambda i: (i, 0))
    wide = pl.BlockSpec((tb, 2 * D_MODEL), lambda i: (i, 0))
    sq = pl.BlockSpec((D_MODEL, D_MODEL), lambda i: (0, 0))
    vec = pl.BlockSpec((1, D_MODEL), lambda i: (0, 0))
    f = jax.ShapeDtypeStruct((t, D_MODEL), F32)
    b = jax.ShapeDtypeStruct((t, D_MODEL), BF16)
    return pl.pallas_call(
        body, name="merge_bwd", grid=(t // tb,),
        in_specs=[row, row, row, vec, wide, row, row, sq, sq, sq],
        out_specs=[row, vec, wide, row, row, row, row],
        out_shape=[f, jax.ShapeDtypeStruct((1, D_MODEL), F32), jax.ShapeDtypeStruct((t, 2 * D_MODEL), BF16),
                   b, b, f, b],
        compiler_params=_params(("arbitrary",)),
    )(dx2, dn2, x1, w2, gl, pdn, psb, wp_dn, wp_sb, w_out)


def _conv_taps(buf, w_ref, first, rows, cols=slice(None)):
    y = w_ref[0:1, cols] * buf[pl.ds(first, rows), cols]
    for s in range(1, w_ref.shape[0]):
        y = y + w_ref[s:s + 1, cols] * buf[pl.ds(first + s, rows), cols]
    return y


def _ffn_mid_fwd(pre_g, pre_u, wg, wu):
    t, c = pre_g.shape
    kk = wg.shape[0]
    tb, cb = _rows(t), _pick(c, ELEMENTWISE_COLS)
    per = tb // HALO

    def body(g_ref, gh_ref, u_ref, uh_ref, wg_ref, wu_ref, a_ref, gbuf, ubuf):
        i = pl.program_id(0)
        for buf, ref, halo in ((gbuf, g_ref, gh_ref), (ubuf, u_ref, uh_ref)):
            buf[pl.ds(HALO, tb), :] = ref[...]
            buf[pl.ds(0, HALO), :] = jnp.where(i == 0, 0.0, halo[...])
        for j in range(cb // LANES):
            sl = pl.ds(j * LANES, LANES)
            ug = _conv_taps(gbuf, wg_ref, HALO - (kk - 1), tb, sl)
            uu = _conv_taps(ubuf, wu_ref, HALO - (kk - 1), tb, sl)
            a_ref[:, sl] = (_silu(ug) * uu).astype(BF16)

    blk = pl.BlockSpec((tb, cb), lambda i, j: (i, j))
    halo = pl.BlockSpec((HALO, cb), lambda i, j: (jnp.maximum(i * per - 1, 0), j))
    wspec = pl.BlockSpec((kk, cb), lambda i, j: (0, j))
    return pl.pallas_call(
        body, name="ffn_mid_fwd", grid=(t // tb, c // cb),
        in_specs=[blk, halo, blk, halo, wspec, wspec], out_specs=blk,
        out_shape=jax.ShapeDtypeStruct((t, c), BF16),
        scratch_shapes=[pltpu.VMEM((tb + HALO, cb), F32)] * 2,
        compiler_params=_params(("parallel", "parallel")),
    )(pre_g, pre_g, pre_u, pre_u, wg, wu)


def _ffn_mid_bwd(pre_g, pre_u, wg, wu, da):
    t, c = pre_g.shape
    kk = wg.shape[0]
    tb, cb = _rows(t), _pick(c, ELEMENTWISE_COLS)
    per = tb // HALO
    nblk = t // tb
    ext = tb + HALO

    def body(g_ref, gb_ref, ga_ref, u_ref, ub_ref, ua_ref, da_ref, daa_ref, wg_ref, wu_ref,
             dg_ref, du_ref, dwg_ref, dwu_ref, gbuf, ubuf, dabuf, dgbuf, dubuf):
        i = pl.program_id(1)
        last = i == nblk - 1
        for buf, ref, before, after in ((gbuf, g_ref, gb_ref, ga_ref), (ubuf, u_ref, ub_ref, ua_ref)):
            buf[pl.ds(0, HALO), :] = jnp.where(i == 0, 0.0, before[...])
            buf[pl.ds(HALO, tb), :] = ref[...]
            buf[pl.ds(HALO + tb, HALO), :] = jnp.where(last, 0.0, after[...])
        dabuf[pl.ds(0, tb), :] = da_ref[...]
        dabuf[pl.ds(tb, HALO), :] = jnp.where(last, 0.0, daa_ref[...])

        @pl.when(i == 0)
        def _():
            dwg_ref[...] = jnp.zeros_like(dwg_ref)
            dwu_ref[...] = jnp.zeros_like(dwu_ref)

        for j in range(cb // LANES):
            sl = pl.ds(j * LANES, LANES)
            ug = _conv_taps(gbuf, wg_ref, HALO - (kk - 1), ext, sl)
            uu = _conv_taps(ubuf, wu_ref, HALO - (kk - 1), ext, sl)
            _, vjp = jax.vjp(lambda g, u: _silu(g) * u, ug, uu)
            dgbuf[:, sl], dubuf[:, sl] = vjp(dabuf[:, sl])
            for dbuf, xbuf, w_ref, dx_ref, dw_ref in ((dgbuf, gbuf, wg_ref, dg_ref, dwg_ref),
                                                      (dubuf, ubuf, wu_ref, du_ref, dwu_ref)):
                x = xbuf[pl.ds(HALO, tb), sl]
                dx = None
                for s in range(kk):
                    shifted = dbuf[pl.ds(kk - 1 - s, tb), sl]
                    term = w_ref[s:s + 1, sl] * shifted
                    dx = term if dx is None else dx + term
                    dw_ref[s:s + 1, sl] += jnp.sum(shifted * x, axis=0, keepdims=True)
                dx_ref[:, sl] = dx.astype(BF16)

    blk = pl.BlockSpec((tb, cb), lambda j, i: (i, j))
    before = pl.BlockSpec((HALO, cb), lambda j, i: (jnp.maximum(i * per - 1, 0), j))
    after = pl.BlockSpec((HALO, cb), lambda j, i: (jnp.minimum((i + 1) * per, t // HALO - 1), j))
    wspec = pl.BlockSpec((kk, cb), lambda j, i: (0, j))
    dwspec = pl.BlockSpec((HALO, cb), lambda j, i: (0, j))
    half = jax.ShapeDtypeStruct((t, c), BF16)
    dwshape = jax.ShapeDtypeStruct((HALO, c), F32)
    return pl.pallas_call(
        body, name="ffn_mid_bwd", grid=(c // cb, nblk),
        in_specs=[blk, before, after, blk, before, after, blk, after, wspec, wspec],
        out_specs=[blk, blk, dwspec, dwspec],
        out_shape=[half, half, dwshape, dwshape],
        scratch_shapes=[pltpu.VMEM((ext + HALO, cb), F32)] * 2 + [pltpu.VMEM((ext, cb), F32)] * 3,
        compiler_params=_params(("parallel", "arbitrary")),
    )(pre_g, pre_g, pre_g, pre_u, pre_u, pre_u, da, da, wg, wu)


def _down_loss(a, w_down, x1, wf, target):
    t = x1.shape[0]
    tb = _rows(t)

    def body(a_ref, wd_ref, x1_ref, wf_ref, tgt_ref, dx2_ref, dwf_ref, loss_ref):
        i = pl.program_id(0)
        x2 = x1_ref[...] + _dot(a_ref[...], wd_ref[...])
        y, vjp = jax.vjp(_rms, x2, wf_ref[...])
        err = y - tgt_ref[...]
        dx2, dwf = vjp(err * (1.0 / D_MODEL))
        dx2_ref[...] = dx2
        part = jnp.sum(jnp.sum(err * err, axis=1, keepdims=True), axis=0, keepdims=True) * (0.5 / D_MODEL)

        @pl.when(i == 0)
        def _():
            dwf_ref[...] = jnp.zeros_like(dwf_ref)
            loss_ref[...] = jnp.zeros_like(loss_ref)

        dwf_ref[...] += dwf
        loss_ref[...] += jnp.broadcast_to(part, loss_ref.shape)

    row = pl.BlockSpec((tb, D_MODEL), lambda i: (i, 0))
    vec = pl.BlockSpec((1, D_MODEL), lambda i: (0, 0))
    return pl.pallas_call(
        body, name="down_loss", grid=(t // tb,),
        in_specs=[pl.BlockSpec((tb, D_FF), lambda i: (i, 0)), pl.BlockSpec((D_FF, D_MODEL), lambda i: (0, 0)),
                  row, vec, row],
        out_specs=[row, vec, pl.BlockSpec((1, LANES), lambda i: (0, 0))],
        out_shape=[jax.ShapeDtypeStruct((t, D_MODEL), F32), jax.ShapeDtypeStruct((1, D_MODEL), F32),
                   jax.ShapeDtypeStruct((1, LANES), F32)],
        compiler_params=_params(("arbitrary",)),
    )(a, w_down, x1, wf, target)


def _local_step(x, target, wts, late_wire=(), late_weights=None, early_partials=None):
    t = x.shape[0]
    nchunk = t // DN_CHUNK

    n1, hab = _norm1_fwd(x, wts["norm1"], wts["w_ab"])
    dnqkv = _mm(n1, wts["w_dnqkv"], name="h_dnqkv")
    dngate = _mm(n1, wts["w_dngate"], name="h_dngate")
    sbqkv = _mm(n1, wts["w_sbqkv"], out_dtype=BF16, name="h_sbqkv")
    gl = _mm(n1, wts["w_gl"], name="h_gl")

    cdn = _conv_fwd(dnqkv, wts["dn_conv"], "dn_conv_fwd")
    qn, kn, vv, gb = _dn_prep_fwd(cdn, hab, wts["alog"], wts["dtb"])
    per_head = gb[:, :2 * N_HEADS].T.reshape(2 * N_HEADS, nchunk, DN_CHUNK)
    grow, brow = per_head[:N_HEADS, :, None, :], per_head[N_HEADS:, :, None, :]
    u_dn, w_dn, a_qk, qe, kdec, egl, tinv, *late = _dn_local_fwd(qn, kn, vv, grow, brow, late_wire)
    if late_wire:
        wts = {**wts, **late_weights(late)}
    o_raw, states = _dn_seq_fwd(u_dn, w_dn, a_qk, qe, kdec, egl)
    o_dn = _dn_post_fwd(o_raw, dngate, wts["dn_norm"])

    o_sb, tot, sb_used = _sb_fwd(sbqkv)

    pdn, psb, mixed, x1, n2 = _merge_fwd(o_dn, o_sb, gl, x, wts["wp_dn"], wts["wp_sb"], wts["w_out"],
                                         wts["norm2"])
    pre_g = _mm(n2, wts["w_up_g"], name="ffn_up_g")
    pre_u = _mm(n2, wts["w_up_u"], name="ffn_up_u")
    act = _ffn_mid_fwd(pre_g, pre_u, wts["ffn_conv_g"], wts["ffn_conv_u"])
    dx2, d_normf, loss_part = _down_loss(act, wts["w_down"], x1, wts["normf"], target)

    grads = {"normf": d_normf}
    da = _mm(dx2, wts["w_down"], tb=True, name="d_act")
    grads["w_down"] = _mm(act, dx2, ta=True, out_dtype=BF16, name="dw_down")
    dpre_g, dpre_u, dcw_g, dcw_u = _ffn_mid_bwd(pre_g, pre_u, wts["ffn_conv_g"], wts["ffn_conv_u"], da)
    grads["ffn_conv"] = jnp.concatenate([dcw_g[:FFN_CONV], dcw_u[:FFN_CONV]], axis=1)
    dn2 = _mm(dpre_g, wts["w_up_g"], tb=True, name="dn2_g")
    dn2 = _mm(dpre_u, wts["w_up_u"], tb=True, add=dn2, name="dn2_u")
    grads["w_up_g"] = _mm(n2, dpre_g, ta=True, out_dtype=BF16, name="dw_up_g")
    grads["w_up_u"] = _mm(n2, dpre_u, ta=True, out_dtype=BF16, name="dw_up_u")

    dx1, grads["norm2"], dgl, dpdn, dpsb, do_dn, do_sb = _merge_bwd(
        dx2, dn2, x1, wts["norm2"], gl, pdn, psb, wts["wp_dn"], wts["wp_sb"], wts["w_out"])
    grads["w_out"] = _mm(mixed, dx1, ta=True, out_dtype=BF16, name="dw_out")
    grads["wp_dn"] = _mm(o_dn, dpdn, ta=True, out_dtype=BF16, name="dw_proj_dn")
    grads["wp_sb"] = _mm(o_sb, dpsb, ta=True, out_dtype=BF16, name="dw_proj_sb")

    partials = early_partials(grads) if early_partials else ()
    dsq, dsk, dsv, *arrived = _sb_bwd(sbqkv, tot, sb_used, do_sb, partials)
    dsbqkv = jnp.concatenate([dsq, dsk, dsv], axis=1).astype(BF16)

    do_raw, ddngate, grads["dn_norm"] = _dn_post_bwd(o_raw, dngate, wts["dn_norm"], do_dn)
    seq_grads = _dn_seq_bwd(u_dn, w_dn, a_qk, qe, kdec, egl, states, do_raw)
    dqn, dkn, dvv, dgrow, dbrow = _dn_local_bwd(qn, kn, vv, grow, brow, tinv, *seq_grads)
    dgb = jnp.concatenate([dgrow.reshape(N_HEADS, t), dbrow.reshape(N_HEADS, t)], axis=0).T
    dgb = jnp.pad(dgb, ((0, 0), (0, LANES - 2 * N_HEADS)))
    dcdn, dhab, grads["alog"], grads["dtb"] = _dn_prep_bwd(cdn, hab, wts["alog"], wts["dtb"], dqn, dkn, dvv, dgb)
    ddnqkv, dcw_dn = _conv_bwd(dcdn, dnqkv, wts["dn_conv"], "dn_conv_bwd", BF16)
    grads["dn_conv"] = dcw_dn[:DN_CONV]

    dn1 = _mm(ddnqkv, wts["w_dnqkv"], tb=True, name="dn1_dnqkv")
    dn1 = _mm(ddngate, wts["w_dngate"], tb=True, add=dn1, name="dn1_dngate")
    dn1 = _mm(dsbqkv, wts["w_sbqkv"], tb=True, add=dn1, name="dn1_sbqkv")
    dn1 = _mm(dgl, wts["w_gl"], tb=True, add=dn1, name="dn1_gl")
    grads["w_dnqkv"] = _mm(n1, ddnqkv, ta=True, out_dtype=BF16, name="dw_dnqkv")
    grads["w_dngate"] = _mm(n1, ddngate, ta=True, out_dtype=BF16, name="dw_dngate")
    grads["w_sbqkv"] = _mm(n1, dsbqkv, ta=True, out_dtype=BF16, name="dw_sbqkv")
    grads["w_gl"] = _mm(n1, dgl, ta=True, out_dtype=BF16, name="dw_gl")
    grads["w_ab"] = _mm(n1, dhab, ta=True, out_dtype=BF16, name="dw_ab")
    grad_x, grads["norm1"] = _norm1_bwd(x, wts["norm1"], dn1, dx1, dhab, wts["w_ab"])
    return loss_part, grad_x, grads, (list(partials), arrived)


def _place():
    return lax.axis_index("x"), lax.axis_index("y"), lax.axis_index("c")


def _hbm_specs(n):
    return [pl.BlockSpec(memory_space=pltpu.HBM)] * n


GATHER_SEMS = 8
GATHER_FORWARD_AT = 5


def _gather_protocol(ins, outs, send_sems, recv_sems):
    n = len(ins)
    x, y, c = _place()
    me = 2 * x + y
    sibling = (x, y, 1 - c)
    xn, yn, dg = (1 - x, y), (x, 1 - y), (1 - x, 1 - y)
    idx = lambda chip: 2 * chip[0] + chip[1]

    def part(a, chip_index, core, quarter=None):
        half = ins[a].shape[0] // 2
        if quarter is None:
            return outs[a].at[chip_index, pl.ds(core * half, half), :]
        return outs[a].at[chip_index, pl.ds(core * half + quarter * (half // 2), half // 2), :]

    def copy(a, k, src, dst, to):
        return pltpu.make_async_remote_copy(src_ref=src, dst_ref=dst, send_sem=send_sems.at[GATHER_SEMS * a + k],
                                            recv_sem=recv_sems.at[GATHER_SEMS * a + k], device_id=to,
                                            device_id_type=MESH)

    def sent(a, k):
        half = ins[a].shape[0] // 2
        my_half = ins[a].at[pl.ds(c * half, half), :]
        if k < 2:
            return copy(a, k, my_half, part(a, me, c), (*(xn, yn)[k], c))
        if k < 4:
            src = part(a, idx((xn, yn)[k - 2]), c, k - 2)
            return copy(a, k, src, src, (*(yn, xn)[k - 2], c))
        src = (part(a, idx(xn), c), part(a, idx(yn), c), part(a, idx(dg), c, 0), part(a, idx(dg), c, 1))[k - 4]
        return copy(a, k, src, src, sibling)

    def landed(a, k):
        dst = (part(a, idx(xn), c), part(a, idx(yn), c), part(a, idx(dg), c, 0), part(a, idx(dg), c, 1),
               part(a, idx(xn), 1 - c), part(a, idx(yn), 1 - c), part(a, idx(dg), 1 - c, 0),
               part(a, idx(dg), 1 - c, 1))[k]
        return copy(a, k, dst, dst, sibling)

    def begin():
        for a in range(n):
            sent(a, 0).start()
            sent(a, 1).start()

    def middle():
        for a in range(n):
            for k in range(2):
                landed(a, k).wait_recv()
                sent(a, 2 + k).start()
                sent(a, 4 + k).start()

    def end():
        for a in range(n):
            for k in (2, 3):
                landed(a, k).wait_recv()
                sent(a, 4 + k).start()
        for a in range(n):
            for k in range(4, GATHER_SEMS):
                landed(a, k).wait_recv()
        for a in range(n):
            for k in range(GATHER_SEMS):
                sent(a, k).wait_send()

    return begin, middle, end


def _gather_out_shapes(shards):
    return [jax.ShapeDtypeStruct((N_CHIPS,) + s.shape, s.dtype) for s in shards]


def _gather_sems(n):
    return [pltpu.SemaphoreType.DMA((GATHER_SEMS * n,)), pltpu.SemaphoreType.DMA((GATHER_SEMS * n,))]


def _gather_shards(shards):
    n = len(shards)

    def body(*refs):
        begin, middle, end = _gather_protocol(refs[:n], refs[n:2 * n], *refs[2 * n:])
        begin()
        middle()
        end()

    return pl.pallas_call(
        body, name="gather_weights", in_specs=_hbm_specs(n), out_specs=_hbm_specs(n),
        out_shape=_gather_out_shapes(shards), scratch_shapes=_gather_sems(n),
    )(*shards)


def _pair_exchange_halves(gs, tag):
    n = len(gs)

    def body(*refs):
        ins, outs, (send_sems, recv_sems) = refs[:n], refs[n:2 * n], refs[2 * n:]
        x, y, c = _place()
        cps = []
        for a in range(n):
            half = ins[a].shape[1] // 2
            cp = pltpu.make_async_remote_copy(src_ref=ins[a].at[:, pl.ds((1 - c) * half, half), :], dst_ref=outs[a],
                                              send_sem=send_sems.at[a], recv_sem=recv_sems.at[a],
                                              device_id=(x, y, 1 - c), device_id_type=MESH)
            cp.start()
            cps.append(cp)
        for cp in cps:
            cp.wait()

    return pl.pallas_call(
        body, name="grad_pair_exchange_" + tag, in_specs=_hbm_specs(n), out_specs=_hbm_specs(n),
        out_shape=[jax.ShapeDtypeStruct((g.shape[0], g.shape[1] // 2, g.shape[2]), g.dtype) for g in gs],
        scratch_shapes=[pltpu.SemaphoreType.DMA((n,)), pltpu.SemaphoreType.DMA((n,))],
    )(*gs)


def _pick_rows(n, target=1024):
    best = 16
    for b in range(16, min(n, target) + 1, 16):
        if n % b == 0:
            best = b
    return best


def _pair_add(g, got, c_idx, tag):
    nsh, rows, cols = g.shape
    half = rows // 2
    rb = _pick_rows(half)

    def body(c_ref, g_ref, got_ref, o_ref):
        o_ref[...] = (g_ref[...].astype(F32) + got_ref[...].astype(F32)).astype(BF16)

    nb = half // rb
    grid_spec = pltpu.PrefetchScalarGridSpec(
        num_scalar_prefetch=1, grid=(nsh, nb),
        in_specs=[pl.BlockSpec((1, rb, cols), lambda s, i, c_ref: (s, c_ref[0] * nb + i, 0)),
                  pl.BlockSpec((1, rb, cols), lambda s, i, c_ref: (s, i, 0))],
        out_specs=pl.BlockSpec((1, rb, cols), lambda s, i, c_ref: (s, i, 0)))
    return pl.pallas_call(
        body, name="grad_pair_add_" + tag, grid_spec=grid_spec,
        out_shape=jax.ShapeDtypeStruct((nsh, half, cols), BF16),
        compiler_params=_params(("parallel", "parallel")),
    )(c_idx, g, got)


def _chip_exchange_protocol(ins, outs, send_sems, recv_sems):
    x, y, c = _place()
    chips = [(1 - x, y), (x, 1 - y), (1 - x, 1 - y)]

    def copies():
        return [pltpu.make_async_remote_copy(src_ref=ins[a].at[2 * px + py], dst_ref=outs[a].at[j],
                                             send_sem=send_sems.at[3 * a + j], recv_sem=recv_sems.at[3 * a + j],
                                             device_id=(px, py, c), device_id_type=MESH)
                for a in range(len(ins)) for j, (px, py) in enumerate(chips)]

    def begin():
        for cp in copies():
            cp.start()

    def end():
        for cp in copies():
            cp.wait_recv()
        for cp in copies():
            cp.wait_send()

    return begin, end


def _chip_exchange_shapes(ps):
    return [jax.ShapeDtypeStruct((N_CHIPS - 1,) + p.shape[1:], p.dtype) for p in ps]


def _chip_exchange_sems(n):
    return [pltpu.SemaphoreType.DMA((3 * n,)), pltpu.SemaphoreType.DMA((3 * n,))]


def _chip_exchange(ps):
    n = len(ps)

    def body(*refs):
        begin, end = _chip_exchange_protocol(refs[:n], refs[n:2 * n], *refs[2 * n:])
        begin()
        end()

    return pl.pallas_call(
        body, name="grad_chip_exchange", in_specs=_hbm_specs(n), out_specs=_hbm_specs(n),
        out_shape=_chip_exchange_shapes(ps), scratch_shapes=_chip_exchange_sems(n),
    )(*ps)


def _sum_partials(p, got, chip_idx, tag):
    nsh, half, cols = got.shape
    rb = _pick_rows(half)

    def body(me_ref, p_ref, got_ref, o_ref):
        acc = p_ref[0].astype(F32)
        for s in range(nsh):
            acc = acc + got_ref[s].astype(F32)
        o_ref[...] = acc

    grid_spec = pltpu.PrefetchScalarGridSpec(
        num_scalar_prefetch=1, grid=(half // rb,),
        in_specs=[pl.BlockSpec((1, rb, cols), lambda i, me_ref: (me_ref[0], i, 0)),
                  pl.BlockSpec((nsh, rb, cols), lambda i, me_ref: (0, i, 0))],
        out_specs=pl.BlockSpec((rb, cols), lambda i, me_ref: (i, 0)))
    return pl.pallas_call(
        body, name="grad_sum_chips_" + tag, grid_spec=grid_spec,
        out_shape=jax.ShapeDtypeStruct((half, cols), F32),
        compiler_params=_params(("parallel",)),
    )(chip_idx, p, got)


def _pair_share(rs):
    n = len(rs)

    def body(*refs):
        ins, outs, (send_sems, recv_sems) = refs[:n], refs[n:2 * n], refs[2 * n:]
        x, y, c = _place()
        cps = []
        for a in range(n):
            cp = pltpu.make_async_remote_copy(src_ref=ins[a], dst_ref=outs[a], send_sem=send_sems.at[a],
                                              recv_sem=recv_sems.at[a], device_id=(x, y, 1 - c),
                                              device_id_type=MESH)
            cp.start()
            cps.append(cp)
        for cp in cps:
            cp.wait()

    return pl.pallas_call(
        body, name="grad_pair_share", in_specs=_hbm_specs(n), out_specs=_hbm_specs(n),
        out_shape=[jax.ShapeDtypeStruct(r.shape, r.dtype) for r in rs],
        scratch_shapes=[pltpu.SemaphoreType.DMA((n,)), pltpu.SemaphoreType.DMA((n,))],
    )(*rs)


def _small_allreduce(v):
    rows, cols = v.shape
    ndev = 8

    def body(in_ref, out_ref, slots, send_sems, recv_sems):
        x, y, c = _place()
        me = 4 * x + 2 * y + c
        slots[me] = in_ref[...]
        sends = []
        for k in range(1, ndev):
            peer = (x ^ (k >> 2), y ^ ((k >> 1) & 1), c ^ (k & 1))
            cp = pltpu.make_async_remote_copy(src_ref=in_ref, dst_ref=slots.at[me], send_sem=send_sems.at[k - 1],
                                              recv_sem=recv_sems.at[k - 1], device_id=peer, device_id_type=MESH)
            cp.start()
            sends.append(cp)
        for k in range(1, ndev):
            there = slots.at[me ^ k]
            pltpu.make_async_remote_copy(src_ref=there, dst_ref=there, send_sem=send_sems.at[k - 1],
                                         recv_sem=recv_sems.at[k - 1], device_id=(x, y, c),
                                         device_id_type=MESH).wait_recv()
        for cp in sends:
            cp.wait_send()
        acc = slots[0]
        for s in range(1, ndev):
            acc = acc + slots[s]
        out_ref[...] = acc

    return pl.pallas_call(
        body, name="small_allreduce",
        in_specs=[pl.BlockSpec(memory_space=pltpu.VMEM)],
        out_specs=pl.BlockSpec(memory_space=pltpu.VMEM),
        out_shape=jax.ShapeDtypeStruct((rows, cols), F32),
        scratch_shapes=[pltpu.VMEM((ndev, rows, cols), F32), pltpu.SemaphoreType.DMA((ndev - 1,)),
                        pltpu.SemaphoreType.DMA((ndev - 1,))],
    )(v)


def _adamw(w, g, m, v, name):
    r, c = w.shape
    rb = r if r <= 128 else _pick_rows_8(r, 128)
    c1 = 1.0 - ADAM_B1 ** ADAM_STEP
    c2 = 1.0 - ADAM_B2 ** ADAM_STEP

    def body(w_ref, g_ref, m_ref, v_ref, d_ref, nm_ref, nv_ref):
        gg = g_ref[...]
        nm = ADAM_B1 * m_ref[...] + (1.0 - ADAM_B1) * gg
        nv = ADAM_B2 * v_ref[...] + (1.0 - ADAM_B2) * (gg * gg)
        d_ref[...] = -ADAM_LR * ((nm / c1) / (jnp.sqrt(nv / c2) + ADAM_EPS) + ADAM_WD * w_ref[...])
        nm_ref[...] = nm
        nv_ref[...] = nv

    blk = pl.BlockSpec((rb, c), lambda i: (i, 0))
    shp = jax.ShapeDtypeStruct((r, c), F32)
    return pl.pallas_call(
        body, name=name, grid=(r // rb,), in_specs=[blk] * 4, out_specs=[blk] * 3, out_shape=[shp] * 3,
        compiler_params=_params(("parallel",)),
    )(w, g, m, v)


def _pick_rows_8(n, target):
    best = n
    for b in range(8, min(n, target) + 1, 8):
        if n % b == 0:
            best = b
    return best


W_IN_COLS = 2308
W_UP_COLS = 1408
W_DOWN_ROWS = 704
DN_CONV_COLS = 768
FFN_CONV_COLS = 1408
PROJ_ROWS = 256
ROW_TILE = 16
ROW_SEGS = [("wp_dn", PROJ_ROWS), ("wp_sb", PROJ_ROWS), ("w_out", PROJ_ROWS), ("w_down", W_DOWN_ROWS),
            ("dn_conv", ROW_TILE), ("ffn_conv", ROW_TILE), ("spare", 2 * ROW_TILE)]
ROW_OFFS = {nm: (sum(n for _, n in ROW_SEGS[:i]), n) for i, (nm, n) in enumerate(ROW_SEGS)}
STACK_ROWS = sum(n for _, n in ROW_SEGS)
assert all(n % ROW_TILE == 0 for _, n in ROW_SEGS) and STACK_ROWS % (4 * ROW_TILE) == 0
Q_END, A_END, G_END, S_END = 3 * D_MODEL, 3 * D_MODEL + 2 * N_HEADS, 4 * D_MODEL + 2 * N_HEADS, 7 * D_MODEL + 2 * N_HEADS


def _flat_rows(a, nrows):
    flat = a.reshape(-1)
    return jnp.pad(flat, (0, nrows * D_MODEL - flat.shape[0])).reshape(nrows, D_MODEL)


IN_EXTRA_ROWS = 64


def _weight_wire(w_in, wp_dn, wp_sb, w_out, w_up, w_down, dn_conv, ffn_conv):
    bits = lax.bitcast_convert_type(dn_conv, BF16).reshape(-1)
    extra = jnp.pad(bits, (0, IN_EXTRA_ROWS * W_IN_COLS - bits.shape[0])).reshape(IN_EXTRA_ROWS, W_IN_COLS)
    stack = jnp.concatenate([wp_dn.astype(BF16), wp_sb.astype(BF16), w_out.astype(BF16), w_down.astype(BF16),
                             jnp.zeros((ROW_TILE, D_MODEL), BF16),
                             _flat_rows(lax.bitcast_convert_type(ffn_conv, BF16), ROW_TILE),
                             jnp.zeros((ROW_OFFS["spare"][1], D_MODEL), BF16)], axis=0)
    return [jnp.concatenate([w_in.astype(BF16), extra], axis=0)], [w_up.astype(BF16), stack]


def _col_range(g, lo, hi, width):
    parts = []
    for s in range(g.shape[0]):
        a, b = max(lo, s * width), min(hi, (s + 1) * width)
        if a < b:
            parts.append(g[s][:, a - s * width:b - s * width])
    return parts[0] if len(parts) == 1 else jnp.concatenate(parts, axis=1)


def _f32_rows(raw, k, ncols):
    raw = raw.reshape(N_CHIPS, -1)[:, :2 * k * ncols].reshape(N_CHIPS, k * ncols, 2)
    vals = lax.bitcast_convert_type(raw, F32).reshape(N_CHIPS, k, ncols)
    return vals.transpose(1, 0, 2).reshape(k, N_CHIPS * ncols)


def _unpack_early(g_in):
    w = g_in[:, :D_MODEL, :]
    return {
        "w_dnqkv": _col_range(w, 0, Q_END, W_IN_COLS),
        "w_ab": jnp.pad(_col_range(w, Q_END, A_END, W_IN_COLS), ((0, 0), (0, LANES - 2 * N_HEADS))),
        "w_dngate": _col_range(w, A_END, G_END, W_IN_COLS),
        "w_sbqkv": _col_range(w, G_END, S_END, W_IN_COLS),
        "w_gl": _col_range(w, S_END, N_CHIPS * W_IN_COLS, W_IN_COLS),
        "dn_conv": _f32_rows(g_in[:, D_MODEL:, :], DN_CONV, DN_CONV_COLS),
    }


def _unpack_late(g_up, g_stack):
    def seg(nm):
        at, n = ROW_OFFS[nm]
        return g_stack[:, at:at + n, :]

    ffn_conv = _f32_rows(seg("ffn_conv"), FFN_CONV, FFN_CONV_COLS)
    return {
        "wp_dn": seg("wp_dn").reshape(D_MODEL, D_MODEL),
        "wp_sb": seg("wp_sb").reshape(D_MODEL, D_MODEL),
        "w_out": seg("w_out").reshape(D_MODEL, D_MODEL),
        "w_up_g": _col_range(g_up, 0, D_FF, W_UP_COLS), "w_up_u": _col_range(g_up, D_FF, 2 * D_FF, W_UP_COLS),
        "w_down": seg("w_down").reshape(D_FF, D_MODEL),
        "ffn_conv_g": ffn_conv[:, :D_FF], "ffn_conv_u": ffn_conv[:, D_FF:],
    }


def _grad_wire_early(gr):
    def cols(a, ncols):
        return a.reshape(a.shape[0], N_CHIPS, ncols).transpose(1, 0, 2)

    def rows(a, nrows):
        return a.astype(BF16).reshape(N_CHIPS, nrows, a.shape[1])

    def flat(a, nrows):
        a = a.astype(BF16).reshape(N_CHIPS, -1)
        return jnp.pad(a, ((0, 0), (0, nrows * D_MODEL - a.shape[1]))).reshape(N_CHIPS, nrows, D_MODEL)

    up = [gr["w_up_g"], gr["w_up_u"]]
    g_up = jnp.stack([up[s // 2][:, (s % 2) * W_UP_COLS:(s % 2 + 1) * W_UP_COLS].astype(BF16) for s in range(N_CHIPS)])
    g_stack = jnp.concatenate([rows(gr["wp_dn"], PROJ_ROWS), rows(gr["wp_sb"], PROJ_ROWS), rows(gr["w_out"], PROJ_ROWS),
                               rows(gr["w_down"], W_DOWN_ROWS), jnp.zeros((N_CHIPS, ROW_TILE, D_MODEL), BF16),
                               flat(cols(gr["ffn_conv"], FFN_CONV_COLS), ROW_TILE),
                               jnp.zeros((N_CHIPS, ROW_OFFS["spare"][1], D_MODEL), BF16)], axis=1)
    return [g_up, g_stack]


def _grad_wire_late(gr):
    pieces = [(gr["w_dnqkv"], 0), (gr["w_ab"][:, :2 * N_HEADS], Q_END), (gr["w_dngate"], A_END),
              (gr["w_sbqkv"], G_END), (gr["w_gl"], S_END)]
    conv = gr["dn_conv"].reshape(DN_CONV, N_CHIPS, DN_CONV_COLS).transpose(1, 0, 2).reshape(N_CHIPS, -1)

    def block(s):
        lo, hi = s * W_IN_COLS, (s + 1) * W_IN_COLS
        parts = []
        for a, at in pieces:
            b0, b1 = max(lo, at), min(hi, at + a.shape[1])
            if b0 < b1:
                parts.append(a[:, b0 - at:b1 - at].astype(BF16))
        w = parts[0] if len(parts) == 1 else jnp.concatenate(parts, axis=1)
        extra = jnp.pad(conv[s].astype(BF16), (0, IN_EXTRA_ROWS * W_IN_COLS - conv.shape[1]))
        return jnp.concatenate([w, extra.reshape(IN_EXTRA_ROWS, W_IN_COLS)], axis=0)

    return [jnp.stack([block(s) for s in range(N_CHIPS)])]


def _unpack_grad_shard(r_in, r_up, r_stack):
    def seg(nm):
        at, n = ROW_OFFS[nm]
        return r_stack[at:at + n, :]

    return {
        "w_in": r_in[:D_MODEL], "w_up": r_up,
        "wp_dn": seg("wp_dn"), "wp_sb": seg("wp_sb"), "w_out": seg("w_out"), "w_down": seg("w_down"),
        "dn_conv": r_in[D_MODEL:].reshape(-1)[:DN_CONV * DN_CONV_COLS].reshape(DN_CONV, DN_CONV_COLS),
        "ffn_conv": seg("ffn_conv").reshape(-1)[:FFN_CONV * FFN_CONV_COLS].reshape(FFN_CONV, FFN_CONV_COLS),
    }


def _lane_row(v):
    return jnp.pad(v.reshape(1, -1), ((0, 0), (0, LANES - v.size)))


def kernel(x, norm1_w, w_in, dn_conv_w, dn_A_log, dn_dt_bias, dn_norm_w, w_proj_dn, w_proj_sb, w_out, norm2_w, ffn_w_up, ffn_conv_w, ffn_w_down, norm_f_w, loss_target, m_norm1_w, m_w_in, m_dn_conv_w, m_dn_A_log, m_dn_dt_bias, m_dn_norm_w, m_w_proj_dn, m_w_proj_sb, m_w_out, m_norm2_w, m_ffn_w_up, m_ffn_conv_w, m_ffn_w_down, m_norm_f_w, v_norm1_w, v_w_in, v_dn_conv_w, v_dn_A_log, v_dn_dt_bias, v_dn_norm_w, v_w_proj_dn, v_w_proj_sb, v_w_out, v_norm2_w, v_ffn_w_up, v_ffn_conv_w, v_ffn_w_down, v_norm_f_w):
    early, late = _weight_wire(w_in[0], w_proj_dn[0], w_proj_sb[0], w_out[0], ffn_w_up[0], ffn_w_down[0],
                               dn_conv_w[0], ffn_conv_w[0])
    chip_idx = (2 * lax.axis_index("x") + lax.axis_index("y")).astype(jnp.int32)

    def with_mine(gathered, wire):
        return [lax.dynamic_update_slice(g, mine[None], (chip_idx, 0, 0)) for g, mine in zip(gathered, wire)]

    wts = _unpack_early(*with_mine(_gather_shards(early), early))
    wts.update(norm1=norm1_w, norm2=norm2_w, normf=norm_f_w.reshape(1, D_MODEL), dn_norm=dn_norm_w,
               alog=_lane_row(dn_A_log), dtb=_lane_row(dn_dt_bias))

    c_idx = lax.axis_index("c").astype(jnp.int32).reshape(1)

    def pair_sums(wire_g, tags, when):
        return [_pair_add(g, got, c_idx, tag) for g, got, tag in zip(wire_g, _pair_exchange_halves(wire_g, when), tags)]

    loss_part, grad_x, gr, (early_sums, early_arrived) = _local_step(
        x[0], loss_target[0], wts, late, lambda gathered: _unpack_late(*with_mine(gathered, late)),
        lambda grads: pair_sums(_grad_wire_early(grads), ["w_up", "rows"], "early"))

    late_sums = pair_sums(_grad_wire_late(gr), ["w_in"], "late")
    tags = ["w_in", "w_up", "rows"]
    reduced = [_sum_partials(p, got, chip_idx.reshape(1), tag)
               for p, got, tag in zip(late_sums + early_sums, list(_chip_exchange(late_sums)) + list(early_arrived), tags)]
    is_south = lax.axis_index("c") == 0
    gsh = _unpack_grad_shard(*[jnp.concatenate([jnp.where(is_south, mine, other), jnp.where(is_south, other, mine)],
                                               axis=0) for mine, other in zip(reduced, _pair_share(reduced))])

    tail = jnp.concatenate([gr["dn_norm"], gr["alog"][:, :N_HEADS], gr["dtb"][:, :N_HEADS], loss_part[:, :1]], axis=1)
    small = jnp.concatenate([gr["norm1"], gr["norm2"], gr["normf"],
                             jnp.pad(tail, ((0, 0), (0, D_MODEL - tail.shape[1]))),
                             jnp.zeros((SMALL_ROWS - 4, D_MODEL), F32)], axis=0)
    small = _small_allreduce(small)
    at = HEAD_DIM
    g_small = {"norm1_w": small[0:1], "norm2_w": small[1:2], "norm_f_w": small[2],
               "dn_norm_w": small[3:4, :at], "dn_A_log": small[3:4, at:at + N_HEADS],
               "dn_dt_bias": small[3:4, at + N_HEADS:at + 2 * N_HEADS]}
    loss = small[3, at + 2 * N_HEADS]

    big = {"w_in": (w_in, m_w_in, v_w_in, gsh["w_in"]), "dn_conv_w": (dn_conv_w, m_dn_conv_w, v_dn_conv_w, gsh["dn_conv"]),
           "w_proj_dn": (w_proj_dn, m_w_proj_dn, v_w_proj_dn, gsh["wp_dn"]),
           "w_proj_sb": (w_proj_sb, m_w_proj_sb, v_w_proj_sb, gsh["wp_sb"]),
           "w_out": (w_out, m_w_out, v_w_out, gsh["w_out"]),
           "ffn_w_up": (ffn_w_up, m_ffn_w_up, v_ffn_w_up, gsh["w_up"]),
           "ffn_conv_w": (ffn_conv_w, m_ffn_conv_w, v_ffn_conv_w, gsh["ffn_conv"]),
           "ffn_w_down": (ffn_w_down, m_ffn_w_down, v_ffn_w_down, gsh["w_down"])}
    res = {}
    for nm, (w, m, v, g) in big.items():
        d, nm_, nv_ = _adamw(w[0], g, m[0], v[0], "adamw_" + nm)
        res[nm] = (g[None], d[None], nm_[None], nv_[None])

    names = ["norm1_w", "norm2_w", "norm_f_w", "dn_norm_w", "dn_A_log", "dn_dt_bias"]
    given = {"norm1_w": (norm1_w, m_norm1_w, v_norm1_w), "norm2_w": (norm2_w, m_norm2_w, v_norm2_w),
             "norm_f_w": (norm_f_w, m_norm_f_w, v_norm_f_w), "dn_norm_w": (dn_norm_w, m_dn_norm_w, v_dn_norm_w),
             "dn_A_log": (dn_A_log, m_dn_A_log, v_dn_A_log), "dn_dt_bias": (dn_dt_bias, m_dn_dt_bias, v_dn_dt_bias)}

    def stack(k, fill):
        rows = [jnp.pad(given[nm][k].reshape(1, -1), ((0, 0), (0, D_MODEL - given[nm][k].size)),
                        constant_values=fill) for nm in names]
        return jnp.concatenate(rows + [jnp.full((SMALL_ROWS - len(names), D_MODEL), fill, F32)], axis=0)

    g_rows = jnp.concatenate(
        [jnp.pad(g_small[nm].reshape(1, -1), ((0, 0), (0, D_MODEL - g_small[nm].size))) for nm in names]
        + [jnp.zeros((SMALL_ROWS - len(names), D_MODEL), F32)], axis=0)
    d_s, m_s, v_s = _adamw(stack(0, 0.0), g_rows, stack(1, 0.0), stack(2, 1.0), "adamw_small")
    for r, nm in enumerate(names):
        shape = given[nm][0].shape
        n = given[nm][0].size
        res[nm] = (g_small[nm].reshape(shape), d_s[r, :n].reshape(shape), m_s[r, :n].reshape(shape),
                   v_s[r, :n].reshape(shape))

    order = ["norm1_w", "w_in", "dn_conv_w", "dn_A_log", "dn_dt_bias", "dn_norm_w", "w_proj_dn", "w_proj_sb",
             "w_out", "norm2_w", "ffn_w_up", "ffn_conv_w", "ffn_w_down", "norm_f_w"]
    outs = [loss, grad_x[None]]
    for k in range(4):
        outs += [res[nm][k] for nm in order]
    return tuple(outs)
```

```python
import functools

import jax
import jax.numpy as jnp
from jax import lax
from jax.experimental import pallas as pl
from jax.experimental.pallas import tpu as pltpu

F32 = jnp.float32
BF16 = jnp.bfloat16
MESH = pl.DeviceIdType.MESH

EPS = 1e-6
D_MODEL = 1024
N_HEADS = 8
HEAD_DIM = 128
DN_CONV = 4
DN_CHUNK = 64
D_FF = 2816
FFN_CONV = 3
ADAM_LR, ADAM_B1, ADAM_B2, ADAM_EPS, ADAM_WD, ADAM_STEP = 0.001, 0.9, 0.999, 1e-08, 0.01, 10

N_CHIPS = 4
LANES = 128
HALO = 8
VMEM_LIMIT = 48 * 1024 * 1024
SMALL_ROWS = 8


def _params(sem=None):
    return pltpu.CompilerParams(dimension_semantics=sem, vmem_limit_bytes=VMEM_LIMIT)


def _pick(n, target):
    best = None
    for b in range(LANES, min(n, target) + 1, LANES):
        if n % b == 0:
            best = b
    return best or n


ELEMENTWISE_COLS = 1408


def _rows(t, target=256):
    return min(t, target)


def _dot(a, b, precision=None):
    return lax.dot_general(a, b, (((1,), (0,)), ((), ())), precision=precision, preferred_element_type=F32)


def _dot_nt(a, b, precision=None):
    return lax.dot_general(a, b, (((1,), (1,)), ((), ())), precision=precision, preferred_element_type=F32)


def _dot_tn(a, b, precision=None):
    return lax.dot_general(a, b, (((0,), (0,)), ((), ())), precision=precision, preferred_element_type=F32)


def _rms(x, w):
    return x * lax.rsqrt(jnp.mean(x * x, axis=-1, keepdims=True) + EPS) * w


def _silu(x):
    return x * jax.nn.sigmoid(x)


def _softplus(x):
    return jnp.maximum(x, 0.0) + jnp.log(1.0 + jnp.exp(-jnp.abs(x)))


MM_BLOCK = 1408
MM_VMEM_BUDGET = 38 * 1024 * 1024


def _mm(a, b, *, ta=False, tb=False, add=None, out_dtype=F32, name, bm=MM_BLOCK, bn=MM_BLOCK, bk=MM_BLOCK):
    m = a.shape[1] if ta else a.shape[0]
    k = a.shape[0] if ta else a.shape[1]
    n = b.shape[0] if tb else b.shape[1]
    bm, bn = _pick(m, bm), _pick(n, bn)

    def vmem_need(bk_):
        need = 2 * (bm * bk_ * a.dtype.itemsize + bk_ * bn * b.dtype.itemsize) + 2 * bm * bn * jnp.dtype(out_dtype).itemsize
        need += 2 * bm * bn * add.dtype.itemsize if add is not None else 0
        return need + (bm * bn * 4 if bk_ < k else 0)

    bk = max((d for d in range(LANES, k + 1, LANES) if k % d == 0 and vmem_need(d) <= MM_VMEM_BUDGET),
             default=_pick(k, bk))
    nk = k // bk
    dims = (((0 if ta else 1,), (1 if tb else 0,)), ((), ()))

    def body(*refs):
        a_ref, b_ref = refs[:2]
        c_ref = refs[2] if add is not None else None
        o_ref = refs[3] if add is not None else refs[2]
        acc = refs[-1]
        kk = pl.program_id(2)
        part = lax.dot_general(a_ref[...].astype(BF16), b_ref[...].astype(BF16), dims, preferred_element_type=F32)

        def finish(r):
            if add is not None:
                r = r + c_ref[...].astype(F32)
            o_ref[...] = r.astype(out_dtype)

        if nk == 1:
            finish(part)
            return

        @pl.when(kk == 0)
        def _():
            acc[...] = part

        @pl.when(jnp.logical_and(kk > 0, kk < nk - 1))
        def _():
            acc[...] += part

        @pl.when(kk == nk - 1)
        def _():
            finish(acc[...] + part)

    a_spec = (pl.BlockSpec((bk, bm), lambda i, j, kk: (kk, i)) if ta
              else pl.BlockSpec((bm, bk), lambda i, j, kk: (i, kk)))
    b_spec = (pl.BlockSpec((bn, bk), lambda i, j, kk: (j, kk)) if tb
              else pl.BlockSpec((bk, bn), lambda i, j, kk: (kk, j)))
    o_spec = pl.BlockSpec((bm, bn), lambda i, j, kk: (i, j))
    in_specs = [a_spec, b_spec] + ([o_spec] if add is not None else [])
    args = (a, b) + ((add,) if add is not None else ())
    return pl.pallas_call(
        body, name=name, grid=(m // bm, n // bn, nk),
        in_specs=in_specs, out_specs=o_spec,
        out_shape=jax.ShapeDtypeStruct((m, n), out_dtype),
        scratch_shapes=[pltpu.VMEM((bm, bn), F32)] if nk > 1 else [],
        compiler_params=_params(("parallel", "parallel", "arbitrary")),
    )(*args)


def _norm1_fwd(x, w, w_ab):
    t = x.shape[0]
    tb = _rows(t)

    def body(x_ref, w_ref, wab_ref, n_ref, hab_ref):
        n = _rms(x_ref[...], w_ref[...]).astype(BF16)
        n_ref[...] = n
        hab_ref[...] = _dot(n, wab_ref[...])

    return pl.pallas_call(
        body, name="norm1_fwd", grid=(t // tb,),
        in_specs=[pl.BlockSpec((tb, D_MODEL), lambda i: (i, 0)),
                  pl.BlockSpec((1, D_MODEL), lambda i: (0, 0)),
                  pl.BlockSpec((D_MODEL, LANES), lambda i: (0, 0))],
        out_specs=[pl.BlockSpec((tb, D_MODEL), lambda i: (i, 0)),
                   pl.BlockSpec((tb, LANES), lambda i: (i, 0))],
        out_shape=[jax.ShapeDtypeStruct((t, D_MODEL), BF16), jax.ShapeDtypeStruct((t, LANES), F32)],
        compiler_params=_params(("arbitrary",)),
    )(x, w, w_ab)


def _norm1_bwd(x, w, dn, dres, dab, w_ab):
    t = x.shape[0]
    tb = _rows(t)

    def body(x_ref, w_ref, dn_ref, dres_ref, dab_ref, wab_ref, dx_ref, dw_ref):
        i = pl.program_id(0)
        g = dn_ref[...] + _dot_nt(dab_ref[...].astype(BF16), wab_ref[...])
        _, vjp = jax.vjp(_rms, x_ref[...], w_ref[...])
        dx, dw = vjp(g)
        dx_ref[...] = dres_ref[...] + dx

        @pl.when(i == 0)
        def _():
            dw_ref[...] = jnp.zeros_like(dw_ref)

        dw_ref[...] += dw

    row = pl.BlockSpec((tb, D_MODEL), lambda i: (i, 0))
    vec = pl.BlockSpec((1, D_MODEL), lambda i: (0, 0))
    return pl.pallas_call(
        body, name="norm1_bwd", grid=(t // tb,),
        in_specs=[row, vec, row, row, pl.BlockSpec((tb, LANES), lambda i: (i, 0)),
                  pl.BlockSpec((D_MODEL, LANES), lambda i: (0, 0))],
        out_specs=[row, vec],
        out_shape=[jax.ShapeDtypeStruct((t, D_MODEL), F32), jax.ShapeDtypeStruct((1, D_MODEL), F32)],
        compiler_params=_params(("arbitrary",)),
    )(x, w, dn, dres, dab, w_ab)


def _conv_fwd(x, w, name):
    t, c = x.shape
    kk = w.shape[0]
    tb, cb = _rows(t, 512), _pick(c, ELEMENTWISE_COLS)
    per = tb // HALO

    def body(x_ref, halo_ref, w_ref, y_ref, buf):
        i = pl.program_id(0)
        buf[pl.ds(HALO, tb), :] = x_ref[...]
        buf[pl.ds(0, HALO), :] = jnp.where(i == 0, 0.0, halo_ref[...])
        y_ref[...] = _conv_taps(buf, w_ref, HALO - (kk - 1), tb)

    return pl.pallas_call(
        body, name=name, grid=(t // tb, c // cb),
        in_specs=[pl.BlockSpec((tb, cb), lambda i, j: (i, j)),
                  pl.BlockSpec((HALO, cb), lambda i, j: (jnp.maximum(i * per - 1, 0), j)),
                  pl.BlockSpec((kk, cb), lambda i, j: (0, j))],
        out_specs=pl.BlockSpec((tb, cb), lambda i, j: (i, j)),
        out_shape=jax.ShapeDtypeStruct((t, c), F32),
        scratch_shapes=[pltpu.VMEM((tb + HALO, cb), F32)],
        compiler_params=_params(("parallel", "parallel")),
    )(x, x, w)


def _conv_bwd(dy, x, w, name, dx_dtype):
    t, c = x.shape
    kk = w.shape[0]
    tb, cb = _rows(t, 512), _pick(c, ELEMENTWISE_COLS)
    per = tb // HALO
    nblk = t // tb

    def body(dy_ref, after_ref, x_ref, w_ref, dx_ref, dw_ref, dbuf):
        i = pl.program_id(1)
        dbuf[pl.ds(0, tb), :] = dy_ref[...]
        dbuf[pl.ds(tb, HALO), :] = jnp.where(i == nblk - 1, 0.0, after_ref[...])

        @pl.when(i == 0)
        def _():
            dw_ref[...] = jnp.zeros_like(dw_ref)

        for j in range(cb // LANES):
            sl = pl.ds(j * LANES, LANES)
            x = x_ref[:, sl]
            dx = None
            for s in range(kk):
                shifted = dbuf[pl.ds(kk - 1 - s, tb), sl]
                term = w_ref[s:s + 1, sl] * shifted
                dx = term if dx is None else dx + term
                dw_ref[s:s + 1, sl] += jnp.sum(shifted * x, axis=0, keepdims=True)
            dx_ref[:, sl] = dx.astype(dx_dtype)

    blk = pl.BlockSpec((tb, cb), lambda j, i: (i, j))
    return pl.pallas_call(
        body, name=name, grid=(c // cb, nblk),
        in_specs=[blk,
                  pl.BlockSpec((HALO, cb), lambda j, i: (jnp.minimum((i + 1) * per, t // HALO - 1), j)),
                  blk,
                  pl.BlockSpec((kk, cb), lambda j, i: (0, j))],
        out_specs=[blk, pl.BlockSpec((HALO, cb), lambda j, i: (0, j))],
        out_shape=[jax.ShapeDtypeStruct((t, c), dx_dtype), jax.ShapeDtypeStruct((HALO, c), F32)],
        scratch_shapes=[pltpu.VMEM((tb + HALO, cb), F32)],
        compiler_params=_params(("parallel", "arbitrary")),
    )(dy, dy, x, w)


def _dn_head(c, normed):
    s = _silu(c)
    return s * lax.rsqrt(jnp.sum(s * s, axis=-1, keepdims=True) + EPS) if normed else s


def _dn_gates(hab, alog, dtb):
    lane = lax.broadcasted_iota(jnp.int32, hab.shape, 1)
    g = -jnp.exp(alog) * _softplus(hab + dtb)
    beta = jax.nn.sigmoid(hab)
    return jnp.where(lane < N_HEADS, g, jnp.where(lane < 2 * N_HEADS, beta, 0.0))


def _dn_head_slices(q_ref, k_ref, v_ref):
    return [(pl.ds((part * N_HEADS + h) * HEAD_DIM, HEAD_DIM), ref, h, part < 2)
            for part, ref in enumerate((q_ref, k_ref, v_ref)) for h in range(N_HEADS)]


def _dn_prep_fwd(c, hab, alog, dtb):
    t = c.shape[0]
    tb = _rows(t)

    def body(c_ref, hab_ref, alog_ref, dtb_ref, q_ref, k_ref, v_ref, gb_ref):
        for sl, ref, h, normed in _dn_head_slices(q_ref, k_ref, v_ref):
            ref[h] = _dn_head(c_ref[:, sl], normed)
        gb_ref[...] = _dn_gates(hab_ref[...], alog_ref[...], dtb_ref[...])

    hm = pl.BlockSpec((N_HEADS, tb, HEAD_DIM), lambda i: (0, i, 0))
    nar = pl.BlockSpec((tb, LANES), lambda i: (i, 0))
    vec = pl.BlockSpec((1, LANES), lambda i: (0, 0))
    return pl.pallas_call(
        body, name="dn_prep_fwd", grid=(t // tb,),
        in_specs=[pl.BlockSpec((tb, 3 * D_MODEL), lambda i: (i, 0)), nar, vec, vec],
        out_specs=[hm, hm, hm, nar],
        out_shape=[jax.ShapeDtypeStruct((N_HEADS, t, HEAD_DIM), F32)] * 3 + [jax.ShapeDtypeStruct((t, LANES), F32)],
        compiler_params=_params(("parallel",)),
    )(c, hab, alog, dtb)


def _dn_prep_bwd(c, hab, alog, dtb, dq, dk, dv, dgb):
    t = c.shape[0]
    tb = _rows(t)

    def body(c_ref, hab_ref, alog_ref, dtb_ref, dq_ref, dk_ref, dv_ref, dgb_ref,
             dc_ref, dhab_ref, dalog_ref, ddtb_ref):
        i = pl.program_id(0)
        for sl, ref, h, normed in _dn_head_slices(dq_ref, dk_ref, dv_ref):
            _, vjp = jax.vjp(functools.partial(_dn_head, normed=normed), c_ref[:, sl])
            dc_ref[:, sl] = vjp(ref[h])[0]
        _, vjp = jax.vjp(_dn_gates, hab_ref[...], alog_ref[...], dtb_ref[...])
        dhab, dalog, ddtb = vjp(dgb_ref[...])
        dhab_ref[...] = dhab

        @pl.when(i == 0)
        def _():
            dalog_ref[...] = jnp.zeros_like(dalog_ref)
            ddtb_ref[...] = jnp.zeros_like(ddtb_ref)

        dalog_ref[...] += dalog
        ddtb_ref[...] += ddtb

    hm = pl.BlockSpec((N_HEADS, tb, HEAD_DIM), lambda i: (0, i, 0))
    wide = pl.BlockSpec((tb, 3 * D_MODEL), lambda i: (i, 0))
    nar = pl.BlockSpec((tb, LANES), lambda i: (i, 0))
    vec = pl.BlockSpec((1, LANES), lambda i: (0, 0))
    return pl.pallas_call(
        body, name="dn_prep_bwd", grid=(t // tb,),
        in_specs=[wide, nar, vec, vec, hm, hm, hm, nar],
        out_specs=[wide, nar, vec, vec],
        out_shape=[jax.ShapeDtypeStruct((t, 3 * D_MODEL), F32), jax.ShapeDtypeStruct((t, LANES), F32),
                   jax.ShapeDtypeStruct((1, LANES), F32), jax.ShapeDtypeStruct((1, LANES), F32)],
        compiler_params=_params(("arbitrary",)),
    )(c, hab, alog, dtb, dq, dk, dv, dgb)


DN_PREC = lax.Precision.HIGH
DN_GROUP = 16


def _dn_prec(a):
    return DN_PREC if a.dtype == F32 else None


def _bdot(a, b):
    return lax.dot_general(a, b, (((2,), (1,)), ((0,), (0,))), precision=_dn_prec(a), preferred_element_type=F32)


def _bdot_nt(a, b):
    return lax.dot_general(a, b, (((2,), (2,)), ((0,), (0,))), precision=_dn_prec(a), preferred_element_type=F32)


def _bdot_tn(a, b):
    return lax.dot_general(a, b, (((1,), (1,)), ((0,), (0,))), precision=_dn_prec(a), preferred_element_type=F32)


def _unit_lower_inverse(lmat):
    c = lmat.shape[-1]
    ri = lax.broadcasted_iota(jnp.int32, (c, c), 0)
    ci = lax.broadcasted_iota(jnp.int32, (c, c), 1)
    p = -lmat
    tinv = jnp.where(ri == ci, 1.0, 0.0) + p
    for _ in range(max(c.bit_length() - 2, 0)):
        p = _bdot(p, p)
        tinv = tinv + _bdot(tinv, p)
    return tinv


@jax.custom_vjp
def _solve_with(lmat, rhs, tinv):
    return _bdot(tinv, rhs)


def _solve_with_fwd(lmat, rhs, tinv):
    sol = _bdot(tinv, rhs)
    return sol, (sol, tinv)


def _solve_with_bwd(res, dsol):
    sol, tinv = res
    drhs = _bdot_tn(tinv, dsol)
    return -_bdot_nt(drhs, sol), drhs, jnp.zeros_like(tinv)


_solve_with.defvjp(_solve_with_fwd, _solve_with_bwd)


def _dn_local(q, k, v, grow, brow, tinv):
    g, c, _ = q.shape
    ri = lax.broadcasted_iota(jnp.int32, (c, c), 0)
    ci = lax.broadcasted_iota(jnp.int32, (c, c), 1)
    lower = ri >= ci
    as_col = lambda r: jnp.sum(jnp.where(ri == ci, jnp.broadcast_to(r, (g, c, c)), 0.0), axis=2, keepdims=True)
    gcol, bcol = as_col(grow), as_col(brow)
    gc_col = jnp.sum(jnp.where(lower, jnp.broadcast_to(grow, (g, c, c)), 0.0), axis=2, keepdims=True)
    gc_row = jnp.sum(jnp.where(ri <= ci, jnp.broadcast_to(gcol, (g, c, c)), 0.0), axis=1, keepdims=True)
    qs = q * (HEAD_DIM ** -0.5)
    kb = k * bcol
    vb = v * bcol
    decay = jnp.where(lower, jnp.exp(jnp.where(lower, gc_col - gc_row, 0.0)), 0.0)
    lmat = jnp.where(ri > ci, _bdot_nt(kb.astype(BF16), k.astype(BF16)) * decay, 0.0)
    eg = jnp.exp(gc_col)
    rhs = jnp.concatenate([vb, kb * eg], axis=2)
    if tinv is None:
        tinv = _unit_lower_inverse(lmat)
    sol = _solve_with(lmat, rhs, tinv)
    a_qk = jnp.where(lower, _bdot_nt(qs.astype(BF16), k.astype(BF16)) * decay, 0.0)
    g_last = jnp.sum(grow, axis=2, keepdims=True)
    kdec = k * jnp.exp(g_last - gc_col)
    egl = jnp.broadcast_to(jnp.exp(g_last), (g, 1, HEAD_DIM))
    b16 = lambda x: x.astype(BF16)
    return sol[:, :, :HEAD_DIM], b16(sol[:, :, HEAD_DIM:]), b16(a_qk), b16(qs * eg), b16(kdec), egl, tinv


def _dn_seq(u, w, a_qk, qe, kdec, egl, s_in):
    b16 = lambda x: x.astype(BF16)
    v_new = u - _bdot(b16(w), b16(s_in))
    o = _bdot(b16(qe), b16(s_in)) + _bdot(b16(a_qk), b16(v_new))
    return o, s_in * egl + _bdot_tn(b16(kdec), b16(v_new))


def _dn_local_specs(t):
    grp = min(DN_GROUP, t // DN_CHUNK)
    rows = grp * DN_CHUNK
    blk = pl.BlockSpec((1, rows, HEAD_DIM), lambda h, i: (h, i, 0))
    row = pl.BlockSpec((1, grp, 1, DN_CHUNK), lambda h, i: (h, i, 0, 0))
    sq = pl.BlockSpec((1, grp, DN_CHUNK, DN_CHUNK), lambda h, i: (h, i, 0, 0))
    lane = pl.BlockSpec((1, grp, 1, HEAD_DIM), lambda h, i: (h, i, 0, 0))
    return grp, blk, row, sq, lane


def half(shape):
    return jax.ShapeDtypeStruct(shape.shape, BF16)


def _dn_shapes(t):
    nchunk = t // DN_CHUNK
    big = jax.ShapeDtypeStruct((N_HEADS, t, HEAD_DIM), F32)
    row = jax.ShapeDtypeStruct((N_HEADS, nchunk, 1, DN_CHUNK), F32)
    sq = jax.ShapeDtypeStruct((N_HEADS, nchunk, DN_CHUNK, DN_CHUNK), F32)
    lane = jax.ShapeDtypeStruct((N_HEADS, nchunk, 1, HEAD_DIM), F32)
    return big, row, sq, lane


def _dn_local_fwd(q, k, v, grow, brow, wire=()):
    t = q.shape[1]
    grp, blk, row, sq, lane = _dn_local_specs(t)
    big, _, sqs, lanes = _dn_shapes(t)
    n = len(wire)
    groups = t // (grp * DN_CHUNK)
    steps = N_HEADS * groups

    def body(q_ref, k_ref, v_ref, gr_ref, br_ref, *rest):
        u_ref, w_ref, a_ref, qe_ref, kd_ref, egl_ref, t_ref = rest[n:n + 7]
        if n:
            begin, middle, end = _gather_protocol(rest[:n], rest[n + 7:2 * n + 7], *rest[2 * n + 7:])
            step = pl.program_id(0) * groups + pl.program_id(1)
            pl.when(step == 0)(begin)
            pl.when(step == (GATHER_FORWARD_AT * steps) // 8)(middle)
        split = lambda r: r[0].reshape(grp, DN_CHUNK, HEAD_DIM)
        u, w, a_qk, qe, kdec, egl, tinv = _dn_local(split(q_ref), split(k_ref), split(v_ref), gr_ref[0],
                                                     br_ref[0], None)
        for ref, val in ((u_ref, u), (w_ref, w), (qe_ref, qe), (kd_ref, kdec)):
            ref[0] = val.reshape(grp * DN_CHUNK, HEAD_DIM)
        a_ref[0] = a_qk
        egl_ref[0] = egl
        t_ref[0] = tinv
        if n:
            pl.when(step == steps - 1)(end)

    assert n == 0 or steps >= 3
    return pl.pallas_call(
        body, name="dn_local_fwd", grid=(N_HEADS, groups),
        in_specs=[blk, blk, blk, row, row] + _hbm_specs(n),
        out_specs=[blk, blk, sq, blk, blk, lane, sq] + _hbm_specs(n),
        out_shape=[big, half(big), half(sqs), half(big), half(big), lanes, sqs] + _gather_out_shapes(wire),
        scratch_shapes=_gather_sems(n) if n else [],
        compiler_params=_params(("arbitrary", "arbitrary")),
    )(q, k, v, grow, brow, *wire)


def _dn_local_bwd(q, k, v, grow, brow, tinv, du, dw, da, dqe, dkd, degl):
    t = q.shape[1]
    grp, blk, row, sq, lane = _dn_local_specs(t)
    big, rows_, _, _ = _dn_shapes(t)

    def body(q_ref, k_ref, v_ref, gr_ref, br_ref, t_ref, du_ref, dw_ref, da_ref, dqe_ref, dkd_ref,
             degl_ref, dq_ref, dk_ref, dv_ref, dgr_ref, dbr_ref):
        split = lambda r: r[0].reshape(grp, DN_CHUNK, HEAD_DIM)
        tinv_v = t_ref[0]
        fn = lambda q_, k_, v_, gr_, br_: _dn_local(q_, k_, v_, gr_, br_, tinv_v)[:6]
        _, vjp = jax.vjp(fn, split(q_ref), split(k_ref), split(v_ref), gr_ref[0], br_ref[0])
        dq, dk, dv, dgr, dbr = vjp((split(du_ref), split(dw_ref), da_ref[0], split(dqe_ref), split(dkd_ref),
                                    degl_ref[0]))
        for ref, val in ((dq_ref, dq), (dk_ref, dk), (dv_ref, dv)):
            ref[0] = val.reshape(grp * DN_CHUNK, HEAD_DIM)
        dgr_ref[0] = dgr
        dbr_ref[0] = dbr

    return pl.pallas_call(
        body, name="dn_local_bwd", grid=(N_HEADS, t // (grp * DN_CHUNK)),
        in_specs=[blk, blk, blk, row, row, sq, blk, blk, sq, blk, blk, lane],
        out_specs=[blk, blk, blk, row, row],
        out_shape=[big, big, big, rows_, rows_],
        compiler_params=_params(("parallel", "parallel")),
    )(q, k, v, grow, brow, tinv, du, dw, da, dqe, dkd, degl)


DN_SEQ_CHUNKS = 4


def _dn_seq_specs(nchunk, rev):
    per = min(DN_SEQ_CHUNKS, nchunk)
    nstep = nchunk // per

    def idx(n):
        return nstep - 1 - n if rev else n

    blk = pl.BlockSpec((N_HEADS, per * DN_CHUNK, HEAD_DIM), lambda n: (0, idx(n), 0))
    sq = pl.BlockSpec((N_HEADS, per, DN_CHUNK, DN_CHUNK), lambda n: (0, idx(n), 0, 0))
    lane = pl.BlockSpec((N_HEADS, per, 1, HEAD_DIM), lambda n: (0, idx(n), 0, 0))
    st = pl.BlockSpec((N_HEADS, per, HEAD_DIM, HEAD_DIM), lambda n: (0, idx(n), 0, 0))
    return per, nstep, blk, sq, lane, st


def _dn_seq_fwd(u, w, a_qk, qe, kdec, egl):
    t = u.shape[1]
    nchunk = t // DN_CHUNK
    per, nstep, blk, sq, lane, st = _dn_seq_specs(nchunk, False)

    def body(u_ref, w_ref, a_ref, qe_ref, kd_ref, egl_ref, o_ref, s_ref, state):
        @pl.when(pl.program_id(0) == 0)
        def _():
            state[...] = jnp.zeros_like(state)

        for c in range(per):
            rows = pl.ds(c * DN_CHUNK, DN_CHUNK)
            s_in = state[...]
            s_ref[:, c] = s_in
            o_ref[:, rows], state[...] = _dn_seq(u_ref[:, rows], w_ref[:, rows], a_ref[:, c], qe_ref[:, rows],
                                                 kd_ref[:, rows], egl_ref[:, c], s_in)

    return pl.pallas_call(
        body, name="dn_seq_fwd", grid=(nstep,),
        in_specs=[blk, blk, sq, blk, blk, lane],
        out_specs=[blk, st],
        out_shape=[jax.ShapeDtypeStruct((N_HEADS, t, HEAD_DIM), F32),
                   jax.ShapeDtypeStruct((N_HEADS, nchunk, HEAD_DIM, HEAD_DIM), F32)],
        scratch_shapes=[pltpu.VMEM((N_HEADS, HEAD_DIM, HEAD_DIM), F32)],
        compiler_params=_params(("arbitrary",)),
    )(u, w, a_qk, qe, kdec, egl)


def _dn_seq_bwd(u, w, a_qk, qe, kdec, egl, states, do):
    t = u.shape[1]
    nchunk = t // DN_CHUNK
    per, nstep, blk, sq, lane, st = _dn_seq_specs(nchunk, True)
    big, _, sqs, lanes = _dn_shapes(t)

    def body(u_ref, w_ref, a_ref, qe_ref, kd_ref, egl_ref, s_ref, do_ref,
             du_ref, dw_ref, da_ref, dqe_ref, dkd_ref, degl_ref, dstate):
        @pl.when(pl.program_id(0) == 0)
        def _():
            dstate[...] = jnp.zeros_like(dstate)

        for c in reversed(range(per)):
            rows = pl.ds(c * DN_CHUNK, DN_CHUNK)
            _, vjp = jax.vjp(_dn_seq, u_ref[:, rows], w_ref[:, rows], a_ref[:, c], qe_ref[:, rows], kd_ref[:, rows],
                             egl_ref[:, c], s_ref[:, c])
            (du_ref[:, rows], dw_ref[:, rows], da_ref[:, c], dqe_ref[:, rows], dkd_ref[:, rows], degl_ref[:, c],
             dstate[...]) = vjp((do_ref[:, rows], dstate[...]))

    return pl.pallas_call(
        body, name="dn_seq_bwd", grid=(nstep,),
        in_specs=[blk, blk, sq, blk, blk, lane, st, blk],
        out_specs=[blk, blk, sq, blk, blk, lane],
        out_shape=[big, half(big), half(sqs), half(big), half(big), lanes],
        scratch_shapes=[pltpu.VMEM((N_HEADS, HEAD_DIM, HEAD_DIM), F32)],
        compiler_params=_params(("arbitrary",)),
    )(u, w, a_qk, qe, kdec, egl, states, do)


def _dn_post_head(o, gate, w):
    return _rms(o, w) * _silu(gate)


def _dn_post_fwd(o, gate, w):
    t = gate.shape[0]
    tb = _rows(t)

    def body(o_ref, g_ref, w_ref, y_ref):
        for h in range(N_HEADS):
            sl = pl.ds(h * HEAD_DIM, HEAD_DIM)
            y_ref[:, sl] = _dn_post_head(o_ref[h], g_ref[:, sl], w_ref[...]).astype(BF16)

    row = pl.BlockSpec((tb, D_MODEL), lambda i: (i, 0))
    hm = pl.BlockSpec((N_HEADS, tb, HEAD_DIM), lambda i: (0, i, 0))
    return pl.pallas_call(
        body, name="dn_post_fwd", grid=(t // tb,),
        in_specs=[hm, row, pl.BlockSpec((1, HEAD_DIM), lambda i: (0, 0))],
        out_specs=row, out_shape=jax.ShapeDtypeStruct((t, D_MODEL), BF16),
        compiler_params=_params(("parallel",)),
    )(o, gate, w)


def _dn_post_bwd(o, gate, w, dy):
    t = gate.shape[0]
    tb = _rows(t)

    def body(o_ref, g_ref, w_ref, dy_ref, do_ref, dg_ref, dw_ref):
        i = pl.program_id(0)
        @pl.when(i == 0)
        def _():
            dw_ref[...] = jnp.zeros_like(dw_ref)

        for h in range(N_HEADS):
            sl = pl.ds(h * HEAD_DIM, HEAD_DIM)
            _, vjp = jax.vjp(_dn_post_head, o_ref[h], g_ref[:, sl], w_ref[...])
            do_ref[h], dg, dw = vjp(dy_ref[:, sl])
            dg_ref[:, sl] = dg.astype(BF16)
            dw_ref[...] += dw

    row = pl.BlockSpec((tb, D_MODEL), lambda i: (i, 0))
    hm = pl.BlockSpec((N_HEADS, tb, HEAD_DIM), lambda i: (0, i, 0))
    vec = pl.BlockSpec((1, HEAD_DIM), lambda i: (0, 0))
    return pl.pallas_call(
        body, name="dn_post_bwd", grid=(t // tb,),
        in_specs=[hm, row, vec, row],
        out_specs=[hm, row, vec],
        out_shape=[jax.ShapeDtypeStruct((N_HEADS, t, HEAD_DIM), F32), jax.ShapeDtypeStruct((t, D_MODEL), BF16),
                   jax.ShapeDtypeStruct((1, HEAD_DIM), F32)],
        compiler_params=_params(("arbitrary",)),
    )(o, gate, w, dy)


def _split_bf16(x):
    hi = x.astype(BF16)
    lo = (x - hi.astype(F32)).astype(BF16)
    return hi, lo


SB_Q_BLOCK = 512
SB_K_BLOCK = 256
SB_NEGLIGIBLE = -60.0


def _sb_logits(q, kb, mask, scale):
    z = _dot_nt(q, kb) * scale
    ls = jnp.minimum(z, 0.0) - jnp.log(1.0 + jnp.exp(-jnp.abs(z)))
    lk = ls - z
    if mask is not None:
        lk = jnp.where(mask, lk, 0.0)
    return ls, lk


def _sb_blocks(t):
    bq = min(SB_Q_BLOCK, t)
    bk = min(SB_K_BLOCK, bq)
    return bq, bk, bq // bk


def _sb_fwd(qkv):
    t = qkv.shape[0]
    bq, bk, nd = _sb_blocks(t)
    scale = HEAD_DIM ** -0.5

    def body(q_ref, k_ref, v_ref, o_ref, tot_ref, used_ref):
        i = pl.program_id(1)
        q = q_ref[...]
        rj = lax.broadcasted_iota(jnp.int32, (bk, bk), 0)
        cj = lax.broadcasted_iota(jnp.int32, (bk, bk), 1)
        after = (rj > cj).astype(BF16)
        trow = lax.broadcasted_iota(jnp.int32, (bq, bk), 0)
        scol = lax.broadcasted_iota(jnp.int32, (bq, bk), 1)

        def tile(j, run, acc, mask):
            off = pl.multiple_of(j * bk, bk)
            kb = k_ref[pl.ds(off, bk), :]
            vb = v_ref[pl.ds(off, bk), :]
            ls, lk = _sb_logits(q, kb, mask, scale)
            hi, lo = _split_bf16(lk)
            between = _dot(hi, after) + _dot(lo, after) + run
            a = jnp.exp(ls + between)
            if mask is not None:
                a = jnp.where(mask, a, 0.0)
            acc = acc + _dot(a.astype(BF16), vb)
            return run + jnp.sum(lk, axis=1, keepdims=True), acc

        run, acc = jnp.zeros((bq, 1), F32), jnp.zeros((bq, HEAD_DIM), F32)
        for d in reversed(range(nd)):
            run, acc = tile(i * nd + d, run, acc, scol + d * bk < trow)
        def more(c):
            return jnp.logical_and(c[0] < i * nd, jnp.max(c[1]) > SB_NEGLIGIBLE)

        def far(c):
            run_, acc_ = tile(i * nd - 1 - c[0], c[1], c[2], None)
            return c[0] + 1, run_, acc_

        used, run, acc = lax.while_loop(more, far, (jnp.int32(0), run, acc))
        o_ref[...] = acc.astype(BF16)
        tot_ref[...] = jnp.broadcast_to(run, (bq, HEAD_DIM))
        used_ref[...] = jnp.full(used_ref.shape, used, F32)

    qs = pl.BlockSpec((bq, HEAD_DIM), lambda h, i: (i, h))
    ks = pl.BlockSpec((t, HEAD_DIM), lambda h, i: (0, N_HEADS + h))
    vs = pl.BlockSpec((t, HEAD_DIM), lambda h, i: (0, 2 * N_HEADS + h))
    return pl.pallas_call(
        body, name="sb_fwd", grid=(N_HEADS, t // bq),
        in_specs=[qs, ks, vs], out_specs=[qs, qs, pl.BlockSpec((1, 1, 1, LANES), lambda h, i: (h, i, 0, 0))],
        out_shape=[jax.ShapeDtypeStruct((t, D_MODEL), BF16), jax.ShapeDtypeStruct((t, D_MODEL), F32),
                   jax.ShapeDtypeStruct((N_HEADS, t // bq, 1, LANES), F32)],
        compiler_params=_params(("parallel", "arbitrary")),
    )(qkv, qkv, qkv)


def _sb_bwd(qkv, tot, used, do, partials=()):
    t = qkv.shape[0]
    bq, bk, nd = _sb_blocks(t)
    scale = HEAD_DIM ** -0.5
    n = len(partials)
    nq = t // bq

    def body(q_ref, k_ref, v_ref, tot_ref, used_ref, do_ref, *rest):
        dq_ref, dk_ref, dv_ref = rest[n:n + 3]
        i = pl.program_id(1)
        if n:
            begin, end = _chip_exchange_protocol(rest[:n], rest[n + 3:2 * n + 3], *rest[2 * n + 3:])
            step = pl.program_id(0) * nq + i
            pl.when(step == 0)(begin)

        @pl.when(i == 0)
        def _():
            dk_ref[...] = jnp.zeros_like(dk_ref)
            dv_ref[...] = jnp.zeros_like(dv_ref)

        q = q_ref[...]
        do = do_ref[...]
        total = tot_ref[:, 0:1]
        rj = lax.broadcasted_iota(jnp.int32, (bk, bk), 0)
        cj = lax.broadcasted_iota(jnp.int32, (bk, bk), 1)
        upto = (rj <= cj).astype(BF16)
        before = (rj < cj).astype(BF16)
        trow = lax.broadcasted_iota(jnp.int32, (bq, bk), 0)
        scol = lax.broadcasted_iota(jnp.int32, (bq, bk), 1)

        def tile(j, run_k, run_e, dq, mask):
            off = pl.multiple_of(j * bk, bk)
            kb = k_ref[pl.ds(off, bk), :]
            vb = v_ref[pl.ds(off, bk), :]
            ls, lk = _sb_logits(q, kb, mask, scale)
            hi, lo = _split_bf16(lk)
            between = total - (_dot(hi, upto) + _dot(lo, upto) + run_k)
            a = jnp.exp(ls + between)
            if mask is not None:
                a = jnp.where(mask, a, 0.0)
            e = a * _dot_nt(do, vb)
            ehi, elo = _split_bf16(e)
            pre = _dot(ehi, before) + _dot(elo, before) + run_e
            sig = jnp.exp(ls)
            dz = e * (1.0 - sig) - pre * sig
            if mask is not None:
                dz = jnp.where(mask, dz, 0.0)
            dz = (dz * scale).astype(BF16)
            dq = dq + _dot(dz, kb)
            dk_ref[pl.ds(off, bk), :] += _dot_tn(dz, q)
            dv_ref[pl.ds(off, bk), :] += _dot_tn(a.astype(BF16), do)
            return (run_k + jnp.sum(lk, axis=1, keepdims=True),
                    run_e + jnp.sum(e, axis=1, keepdims=True), dq)

        zero = jnp.zeros((bq, 1), F32)
        visited = jnp.clip(jnp.max(used_ref[...]).astype(jnp.int32), 0, i * nd)
        carry = lax.fori_loop(i * nd - visited, i * nd, lambda j, c: tile(j, c[0], c[1], c[2], None),
                              (zero, zero, jnp.zeros((bq, HEAD_DIM), F32)))
        for d in range(nd):
            carry = tile(i * nd + d, *carry, scol + d * bk < trow)
        dq_ref[...] = carry[2]
        if n:
            pl.when(step == N_HEADS * nq - 1)(end)

    qs = pl.BlockSpec((bq, HEAD_DIM), lambda h, i: (i, h))
    ks = pl.BlockSpec((t, HEAD_DIM), lambda h, i: (0, N_HEADS + h))
    vs = pl.BlockSpec((t, HEAD_DIM), lambda h, i: (0, 2 * N_HEADS + h))
    full = pl.BlockSpec((t, HEAD_DIM), lambda h, i: (0, h))
    big = jax.ShapeDtypeStruct((t, D_MODEL), F32)
    return pl.pallas_call(
        body, name="sb_bwd", grid=(N_HEADS, nq),
        in_specs=[qs, ks, vs, qs, pl.BlockSpec((1, 1, 1, LANES), lambda h, i: (h, i, 0, 0)), qs] + _hbm_specs(n),
        out_specs=[qs, full, full] + _hbm_specs(n),
        out_shape=[big, big, big] + _chip_exchange_shapes(partials),
        scratch_shapes=_chip_exchange_sems(n) if n else [],
        compiler_params=_params(("arbitrary", "arbitrary")),
    )(qkv, qkv, qkv, tot, used, do, *partials)


def _merge_fwd(o_dn, o_sb, gl, x, wp_dn, wp_sb, w_out, w2):
    t = x.shape[0]
    tb = _rows(t)

    def body(odn_ref, osb_ref, gl_ref, x_ref, wpd_ref, wps_ref, wo_ref, w2_ref,
             pdn_ref, psb_ref, mix_ref, x1_ref, n2_ref):
        pdn = _dot(odn_ref[...], wpd_ref[...])
        psb = _dot(osb_ref[...], wps_ref[...])
        gates = jax.nn.sigmoid(gl_ref[...])
        mixed = (gates[:, :D_MODEL] * pdn + gates[:, D_MODEL:] * psb).astype(BF16)
        x1 = x_ref[...] + _dot(mixed, wo_ref[...])
        pdn_ref[...] = pdn.astype(BF16)
        psb_ref[...] = psb.astype(BF16)
        mix_ref[...] = mixed
        x1_ref[...] = x1
        n2_ref[...] = _rms(x1, w2_ref[...]).astype(BF16)

    row = pl.BlockSpec((tb, D_MODEL), lambda i: (i, 0))
    sq = pl.BlockSpec((D_MODEL, D_MODEL), lambda i: (0, 0))
    f = jax.ShapeDtypeStruct((t, D_MODEL), F32)
    b = jax.ShapeDtypeStruct((t, D_MODEL), BF16)
    return pl.pallas_call(
        body, name="merge_fwd", grid=(t // tb,),
        in_specs=[row, row, pl.BlockSpec((tb, 2 * D_MODEL), lambda i: (i, 0)), row, sq, sq, sq,
                  pl.BlockSpec((1, D_MODEL), lambda i: (0, 0))],
        out_specs=[row] * 5, out_shape=[b, b, b, f, b],
        compiler_params=_params(("parallel",)),
    )(o_dn, o_sb, gl, x, wp_dn, wp_sb, w_out, w2)


def _merge_bwd(dx2, dn2, x1, w2, gl, pdn, psb, wp_dn, wp_sb, w_out):
    t = x1.shape[0]
    tb = _rows(t)

    def body(dx2_ref, dn2_ref, x1_ref, w2_ref, gl_ref, pdn_ref, psb_ref, wpd_ref, wps_ref, wo_ref,
             dx1_ref, dw2_ref, dgl_ref, dpdn_ref, dpsb_ref, dodn_ref, dosb_ref):
        i = pl.program_id(0)
        _, vjp = jax.vjp(_rms, x1_ref[...], w2_ref[...])
        dxn, dw2 = vjp(dn2_ref[...])
        dx1 = dx2_ref[...] + dxn
        dx1_ref[...] = dx1

        @pl.when(i == 0)
        def _():
            dw2_ref[...] = jnp.zeros_like(dw2_ref)

        dw2_ref[...] += dw2
        dmix = _dot_nt(dx1.astype(BF16), wo_ref[...])
        gates = jax.nn.sigmoid(gl_ref[...])
        g_dn, g_sb = gates[:, :D_MODEL], gates[:, D_MODEL:]
        dpdn = (dmix * g_dn).astype(BF16)
        dpsb = (dmix * g_sb).astype(BF16)
        dgl_ref[:, :D_MODEL] = (dmix * pdn_ref[...].astype(F32) * g_dn * (1.0 - g_dn)).astype(BF16)
        dgl_ref[:, D_MODEL:] = (dmix * psb_ref[...].astype(F32) * g_sb * (1.0 - g_sb)).astype(BF16)
        dpdn_ref[...] = dpdn
        dpsb_ref[...] = dpsb
        dodn_ref[...] = _dot_nt(dpdn, wpd_ref[...])
        dosb_ref[...] = _dot_nt(dpsb, wps_ref[...]).astype(BF16)

    row = pl.BlockSpec((tb, D_MODEL), lambda i: (i, 0))
    wide = pl.BlockSpec((tb, 2 * D_MODEL), lambda i: (i, 0))
    sq = pl.BlockSpec((D_MODEL, D_MODEL), lambda i: (0, 0))
    vec = pl.BlockSpec((1, D_MODEL), lambda i: (0, 0))
    f = jax.ShapeDtypeStruct((t, D_MODEL), F32)
    b = jax.ShapeDtypeStruct((t, D_MODEL), BF16)
    return pl.pallas_call(
        body, name="merge_bwd", grid=(t // tb,),
        in_specs=[row, row, row, vec, wide, row, row, sq, sq, sq],
        out_specs=[row, vec, wide, row, row, row, row],
        out_shape=[f, jax.ShapeDtypeStruct((1, D_MODEL), F32), jax.ShapeDtypeStruct((t, 2 * D_MODEL), BF16),
                   b, b, f, b],
        compiler_params=_params(("arbitrary",)),
    )(dx2, dn2, x1, w2, gl, pdn, psb, wp_dn, wp_sb, w_out)


def _conv_taps(buf, w_ref, first, rows, cols=slice(None)):
    y = w_ref[0:1, cols] * buf[pl.ds(first, rows), cols]
    for s in range(1, w_ref.shape[0]):
        y = y + w_ref[s:s + 1, cols] * buf[pl.ds(first + s, rows), cols]
    return y


def _ffn_mid_fwd(pre_g, pre_u, wg, wu):
    t, c = pre_g.shape
    kk = wg.shape[0]
    tb, cb = _rows(t), _pick(c, ELEMENTWISE_COLS)
    per = tb // HALO

    def body(g_ref, gh_ref, u_ref, uh_ref, wg_ref, wu_ref, a_ref, gbuf, ubuf):
        i = pl.program_id(0)
        for buf, ref, halo in ((gbuf, g_ref, gh_ref), (ubuf, u_ref, uh_ref)):
            buf[pl.ds(HALO, tb), :] = ref[...]
            buf[pl.ds(0, HALO), :] = jnp.where(i == 0, 0.0, halo[...])
        for j in range(cb // LANES):
            sl = pl.ds(j * LANES, LANES)
            ug = _conv_taps(gbuf, wg_ref, HALO - (kk - 1), tb, sl)
            uu = _conv_taps(ubuf, wu_ref, HALO - (kk - 1), tb, sl)
            a_ref[:, sl] = (_silu(ug) * uu).astype(BF16)

    blk = pl.BlockSpec((tb, cb), lambda i, j: (i, j))
    halo = pl.BlockSpec((HALO, cb), lambda i, j: (jnp.maximum(i * per - 1, 0), j))
    wspec = pl.BlockSpec((kk, cb), lambda i, j: (0, j))
    return pl.pallas_call(
        body, name="ffn_mid_fwd", grid=(t // tb, c // cb),
        in_specs=[blk, halo, blk, halo, wspec, wspec], out_specs=blk,
        out_shape=jax.ShapeDtypeStruct((t, c), BF16),
        scratch_shapes=[pltpu.VMEM((tb + HALO, cb), F32)] * 2,
        compiler_params=_params(("parallel", "parallel")),
    )(pre_g, pre_g, pre_u, pre_u, wg, wu)


def _ffn_mid_bwd(pre_g, pre_u, wg, wu, da):
    t, c = pre_g.shape
    kk = wg.shape[0]
    tb, cb = _rows(t), _pick(c, ELEMENTWISE_COLS)
    per = tb // HALO
    nblk = t // tb
    ext = tb + HALO

    def body(g_ref, gb_ref, ga_ref, u_ref, ub_ref, ua_ref, da_ref, daa_ref, wg_ref, wu_ref,
             dg_ref, du_ref, dwg_ref, dwu_ref, gbuf, ubuf, dabuf, dgbuf, dubuf):
        i = pl.program_id(1)
        last = i == nblk - 1
        for buf, ref, before, after in ((gbuf, g_ref, gb_ref, ga_ref), (ubuf, u_ref, ub_ref, ua_ref)):
            buf[pl.ds(0, HALO), :] = jnp.where(i == 0, 0.0, before[...])
            buf[pl.ds(HALO, tb), :] = ref[...]
            buf[pl.ds(HALO + tb, HALO), :] = jnp.where(last, 0.0, after[...])
        dabuf[pl.ds(0, tb), :] = da_ref[...]
        dabuf[pl.ds(tb, HALO), :] = jnp.where(last, 0.0, daa_ref[...])

        @pl.when(i == 0)
        def _():
            dwg_ref[...] = jnp.zeros_like(dwg_ref)
            dwu_ref[...] = jnp.zeros_like(dwu_ref)

        for j in range(cb // LANES):
            sl = pl.ds(j * LANES, LANES)
            ug = _conv_taps(gbuf, wg_ref, HALO - (kk - 1), ext, sl)
            uu = _conv_taps(ubuf, wu_ref, HALO - (kk - 1), ext, sl)
            _, vjp = jax.vjp(lambda g, u: _silu(g) * u, ug, uu)
            dgbuf[:, sl], dubuf[:, sl] = vjp(dabuf[:, sl])
            for dbuf, xbuf, w_ref, dx_ref, dw_ref in ((dgbuf, gbuf, wg_ref, dg_ref, dwg_ref),
                                                      (dubuf, ubuf, wu_ref, du_ref, dwu_ref)):
                x = xbuf[pl.ds(HALO, tb), sl]
                dx = None
                for s in range(kk):
                    shifted = dbuf[pl.ds(kk - 1 - s, tb), sl]
                    term = w_ref[s:s + 1, sl] * shifted
                    dx = term if dx is None else dx + term
                    dw_ref[s:s + 1, sl] += jnp.sum(shifted * x, axis=0, keepdims=True)
                dx_ref[:, sl] = dx.astype(BF16)

    blk = pl.BlockSpec((tb, cb), lambda j, i: (i, j))
    before = pl.BlockSpec((HALO, cb), lambda j, i: (jnp.maximum(i * per - 1, 0), j))
    after = pl.BlockSpec((HALO, cb), lambda j, i: (jnp.minimum((i + 1) * per, t // HALO - 1), j))
    wspec = pl.BlockSpec((kk, cb), lambda j, i: (0, j))
    dwspec = pl.BlockSpec((HALO, cb), lambda j, i: (0, j))
    half = jax.ShapeDtypeStruct((t, c), BF16)
    dwshape = jax.ShapeDtypeStruct((HALO, c), F32)
    return pl.pallas_call(
        body, name="ffn_mid_bwd", grid=(c // cb, nblk),
        in_specs=[blk, before, after, blk, before, after, blk, after, wspec, wspec],
        out_specs=[blk, blk, dwspec, dwspec],
        out_shape=[half, half, dwshape, dwshape],
        scratch_shapes=[pltpu.VMEM((ext + HALO, cb), F32)] * 2 + [pltpu.VMEM((ext, cb), F32)] * 3,
        compiler_params=_params(("parallel", "arbitrary")),
    )(pre_g, pre_g, pre_g, pre_u, pre_u, pre_u, da, da, wg, wu)


def _down_loss(a, w_down, x1, wf, target):
    t = x1.shape[0]
    tb = _rows(t)

    def body(a_ref, wd_ref, x1_ref, wf_ref, tgt_ref, dx2_ref, dwf_ref, loss_ref):
        i = pl.program_id(0)
        x2 = x1_ref[...] + _dot(a_ref[...], wd_ref[...])
        y, vjp = jax.vjp(_rms, x2, wf_ref[...])
        err = y - tgt_ref[...]
        dx2, dwf = vjp(err * (1.0 / D_MODEL))
        dx2_ref[...] = dx2
        part = jnp.sum(jnp.sum(err * err, axis=1, keepdims=True), axis=0, keepdims=True) * (0.5 / D_MODEL)

        @pl.when(i == 0)
        def _():
            dwf_ref[...] = jnp.zeros_like(dwf_ref)
            loss_ref[...] = jnp.zeros_like(loss_ref)

        dwf_ref[...] += dwf
        loss_ref[...] += jnp.broadcast_to(part, loss_ref.shape)

    row = pl.BlockSpec((tb, D_MODEL), lambda i: (i, 0))
    vec = pl.BlockSpec((1, D_MODEL), lambda i: (0, 0))
    return pl.pallas_call(
        body, name="down_loss", grid=(t // tb,),
        in_specs=[pl.BlockSpec((tb, D_FF), lambda i: (i, 0)), pl.BlockSpec((D_FF, D_MODEL), lambda i: (0, 0)),
                  row, vec, row],
        out_specs=[row, vec, pl.BlockSpec((1, LANES), lambda i: (0, 0))],
        out_shape=[jax.ShapeDtypeStruct((t, D_MODEL), F32), jax.ShapeDtypeStruct((1, D_MODEL), F32),
                   jax.ShapeDtypeStruct((1, LANES), F32)],
        compiler_params=_params(("arbitrary",)),
    )(a, w_down, x1, wf, target)


def _local_step(x, target, wts, late_wire=(), late_weights=None, early_partials=None):
    t = x.shape[0]
    nchunk = t // DN_CHUNK

    n1, hab = _norm1_fwd(x, wts["norm1"], wts["w_ab"])
    dnqkv = _mm(n1, wts["w_dnqkv"], name="h_dnqkv")
    dngate = _mm(n1, wts["w_dngate"], name="h_dngate")
    sbqkv = _mm(n1, wts["w_sbqkv"], out_dtype=BF16, name="h_sbqkv")
    gl = _mm(n1, wts["w_gl"], name="h_gl")

    cdn = _conv_fwd(dnqkv, wts["dn_conv"], "dn_conv_fwd")
    qn, kn, vv, gb = _dn_prep_fwd(cdn, hab, wts["alog"], wts["dtb"])
    per_head = gb[:, :2 * N_HEADS].T.reshape(2 * N_HEADS, nchunk, DN_CHUNK)
    grow, brow = per_head[:N_HEADS, :, None, :], per_head[N_HEADS:, :, None, :]
    u_dn, w_dn, a_qk, qe, kdec, egl, tinv, *late = _dn_local_fwd(qn, kn, vv, grow, brow, late_wire)
    if late_wire:
        wts = {**wts, **late_weights(late)}
    o_raw, states = _dn_seq_fwd(u_dn, w_dn, a_qk, qe, kdec, egl)
    o_dn = _dn_post_fwd(o_raw, dngate, wts["dn_norm"])

    o_sb, tot, sb_used = _sb_fwd(sbqkv)

    pdn, psb, mixed, x1, n2 = _merge_fwd(o_dn, o_sb, gl, x, wts["wp_dn"], wts["wp_sb"], wts["w_out"],
                                         wts["norm2"])
    pre_g = _mm(n2, wts["w_up_g"], name="ffn_up_g")
    pre_u = _mm(n2, wts["w_up_u"], name="ffn_up_u")
    act = _ffn_mid_fwd(pre_g, pre_u, wts["ffn_conv_g"], wts["ffn_conv_u"])
    dx2, d_normf, loss_part = _down_loss(act, wts["w_down"], x1, wts["normf"], target)

    grads = {"normf": d_normf}
    da = _mm(dx2, wts["w_down"], tb=True, name="d_act")
    grads["w_down"] = _mm(act, dx2, ta=True, out_dtype=BF16, name="dw_down")
    dpre_g, dpre_u, dcw_g, dcw_u = _ffn_mid_bwd(pre_g, pre_u, wts["ffn_conv_g"], wts["ffn_conv_u"], da)
    grads["ffn_conv"] = jnp.concatenate([dcw_g[:FFN_CONV], dcw_u[:FFN_CONV]], axis=1)
    dn2 = _mm(dpre_g, wts["w_up_g"], tb=True, name="dn2_g")
    dn2 = _mm(dpre_u, wts["w_up_u"], tb=True, add=dn2, name="dn2_u")
    grads["w_up_g"] = _mm(n2, dpre_g, ta=True, out_dtype=BF16, name="dw_up_g")
    grads["w_up_u"] = _mm(n2, dpre_u, ta=True, out_dtype=BF16, name="dw_up_u")

    dx1, grads["norm2"], dgl, dpdn, dpsb, do_dn, do_sb = _merge_bwd(
        dx2, dn2, x1, wts["norm2"], gl, pdn, psb, wts["wp_dn"], wts["wp_sb"], wts["w_out"])
    grads["w_out"] = _mm(mixed, dx1, ta=True, out_dtype=BF16, name="dw_out")
    grads["wp_dn"] = _mm(o_dn, dpdn, ta=True, out_dtype=BF16, name="dw_proj_dn")
    grads["wp_sb"] = _mm(o_sb, dpsb, ta=True, out_dtype=BF16, name="dw_proj_sb")

    partials = early_partials(grads) if early_partials else ()
    dsq, dsk, dsv, *arrived = _sb_bwd(sbqkv, tot, sb_used, do_sb, partials)
    dsbqkv = jnp.concatenate([dsq, dsk, dsv], axis=1).astype(BF16)

    do_raw, ddngate, grads["dn_norm"] = _dn_post_bwd(o_raw, dngate, wts["dn_norm"], do_dn)
    seq_grads = _dn_seq_bwd(u_dn, w_dn, a_qk, qe, kdec, egl, states, do_raw)
    dqn, dkn, dvv, dgrow, dbrow = _dn_local_bwd(qn, kn, vv, grow, brow, tinv, *seq_grads)
    dgb = jnp.concatenate([dgrow.reshape(N_HEADS, t), dbrow.reshape(N_HEADS, t)], axis=0).T
    dgb = jnp.pad(dgb, ((0, 0), (0, LANES - 2 * N_HEADS)))
    dcdn, dhab, grads["alog"], grads["dtb"] = _dn_prep_bwd(cdn, hab, wts["alog"], wts["dtb"], dqn, dkn, dvv, dgb)
    ddnqkv, dcw_dn = _conv_bwd(dcdn, dnqkv, wts["dn_conv"], "dn_conv_bwd", BF16)
    grads["dn_conv"] = dcw_dn[:DN_CONV]

    dn1 = _mm(ddnqkv, wts["w_dnqkv"], tb=True, name="dn1_dnqkv")
    dn1 = _mm(ddngate, wts["w_dngate"], tb=True, add=dn1, name="dn1_dngate")
    dn1 = _mm(dsbqkv, wts["w_sbqkv"], tb=True, add=dn1, name="dn1_sbqkv")
    dn1 = _mm(dgl, wts["w_gl"], tb=True, add=dn1, name="dn1_gl")
    grads["w_dnqkv"] = _mm(n1, ddnqkv, ta=True, out_dtype=BF16, name="dw_dnqkv")
    grads["w_dngate"] = _mm(n1, ddngate, ta=True, out_dtype=BF16, name="dw_dngate")
    grads["w_sbqkv"] = _mm(n1, dsbqkv, ta=True, out_dtype=BF16, name="dw_sbqkv")
    grads["w_gl"] = _mm(n1, dgl, ta=True, out_dtype=BF16, name="dw_gl")
    grads["w_ab"] = _mm(n1, dhab, ta=True, out_dtype=BF16, name="dw_ab")
    grad_x, grads["norm1"] = _norm1_bwd(x, wts["norm1"], dn1, dx1, dhab, wts["w_ab"])
    return loss_part, grad_x, grads, (list(partials), arrived)


def _place():
    return lax.axis_index("x"), lax.axis_index("y"), lax.axis_index("c")


def _hbm_specs(n):
    return [pl.BlockSpec(memory_space=pltpu.HBM)] * n


GATHER_SEMS = 8
GATHER_FORWARD_AT = 5


def _gather_protocol(ins, outs, send_sems, recv_sems):
    n = len(ins)
    x, y, c = _place()
    me = 2 * x + y
    sibling = (x, y, 1 - c)
    xn, yn, dg = (1 - x, y), (x, 1 - y), (1 - x, 1 - y)
    idx = lambda chip: 2 * chip[0] + chip[1]

    def part(a, chip_index, core, quarter=None):
        half = ins[a].shape[0] // 2
        if quarter is None:
            return outs[a].at[chip_index, pl.ds(core * half, half), :]
        return outs[a].at[chip_index, pl.ds(core * half + quarter * (half // 2), half // 2), :]

    def copy(a, k, src, dst, to):
        return pltpu.make_async_remote_copy(src_ref=src, dst_ref=dst, send_sem=send_sems.at[GATHER_SEMS * a + k],
                                            recv_sem=recv_sems.at[GATHER_SEMS * a + k], device_id=to,
                                            device_id_type=MESH)

    def sent(a, k):
        half = ins[a].shape[0] // 2
        my_half = ins[a].at[pl.ds(c * half, half), :]
        if k < 2:
            return copy(a, k, my_half, part(a, me, c), (*(xn, yn)[k], c))
        if k < 4:
            src = part(a, idx((xn, yn)[k - 2]), c, k - 2)
            return copy(a, k, src, src, (*(yn, xn)[k - 2], c))
        src = (part(a, idx(xn), c), part(a, idx(yn), c), part(a, idx(dg), c, 0), part(a, idx(dg), c, 1))[k - 4]
        return copy(a, k, src, src, sibling)

    def landed(a, k):
        dst = (part(a, idx(xn), c), part(a, idx(yn), c), part(a, idx(dg), c, 0), part(a, idx(dg), c, 1),
               part(a, idx(xn), 1 - c), part(a, idx(yn), 1 - c), part(a, idx(dg), 1 - c, 0),
               part(a, idx(dg), 1 - c, 1))[k]
        return copy(a, k, dst, dst, sibling)

    def begin():
        for a in range(n):
            sent(a, 0).start()
            sent(a, 1).start()

    def middle():
        for a in range(n):
            for k in range(2):
                landed(a, k).wait_recv()
                sent(a, 2 + k).start()
                sent(a, 4 + k).start()

    def end():
        for a in range(n):
            for k in (2, 3):
                landed(a, k).wait_recv()
                sent(a, 4 + k).start()
        for a in range(n):
            for k in range(4, GATHER_SEMS):
                landed(a, k).wait_recv()
        for a in range(n):
            for k in range(GATHER_SEMS):
                sent(a, k).wait_send()

    return begin, middle, end


def _gather_out_shapes(shards):
    return [jax.ShapeDtypeStruct((N_CHIPS,) + s.shape, s.dtype) for s in shards]


def _gather_sems(n):
    return [pltpu.SemaphoreType.DMA((GATHER_SEMS * n,)), pltpu.SemaphoreType.DMA((GATHER_SEMS * n,))]


def _gather_shards(shards):
    n = len(shards)

    def body(*refs):
        begin, middle, end = _gather_protocol(refs[:n], refs[n:2 * n], *refs[2 * n:])
        begin()
        middle()
        end()

    return pl.pallas_call(
        body, name="gather_weights", in_specs=_hbm_specs(n), out_specs=_hbm_specs(n),
        out_shape=_gather_out_shapes(shards), scratch_shapes=_gather_sems(n),
    )(*shards)


def _pair_exchange_halves(gs, tag):
    n = len(gs)

    def body(*refs):
        ins, outs, (send_sems, recv_sems) = refs[:n], refs[n:2 * n], refs[2 * n:]
        x, y, c = _place()
        cps = []
        for a in range(n):
            half = ins[a].shape[1] // 2
            cp = pltpu.make_async_remote_copy(src_ref=ins[a].at[:, pl.ds((1 - c) * half, half), :], dst_ref=outs[a],
                                              send_sem=send_sems.at[a], recv_sem=recv_sems.at[a],
                                              device_id=(x, y, 1 - c), device_id_type=MESH)
            cp.start()
            cps.append(cp)
        for cp in cps:
            cp.wait()

    return pl.pallas_call(
        body, name="grad_pair_exchange_" + tag, in_specs=_hbm_specs(n), out_specs=_hbm_specs(n),
        out_shape=[jax.ShapeDtypeStruct((g.shape[0], g.shape[1] // 2, g.shape[2]), g.dtype) for g in gs],
        scratch_shapes=[pltpu.SemaphoreType.DMA((n,)), pltpu.SemaphoreType.DMA((n,))],
    )(*gs)


def _pick_rows(n, target=1024):
    best = 16
    for b in range(16, min(n, target) + 1, 16):
        if n % b == 0:
            best = b
    return best


def _pair_add(g, got, c_idx, tag):
    nsh, rows, cols = g.shape
    half = rows // 2
    rb = _pick_rows(half)

    def body(c_ref, g_ref, got_ref, o_ref):
        o_ref[...] = (g_ref[...].astype(F32) + got_ref[...].astype(F32)).astype(BF16)

    nb = half // rb
    grid_spec = pltpu.PrefetchScalarGridSpec(
        num_scalar_prefetch=1, grid=(nsh, nb),
        in_specs=[pl.BlockSpec((1, rb, cols), lambda s, i, c_ref: (s, c_ref[0] * nb + i, 0)),
                  pl.BlockSpec((1, rb, cols), lambda s, i, c_ref: (s, i, 0))],
        out_specs=pl.BlockSpec((1, rb, cols), lambda s, i, c_ref: (s, i, 0)))
    return pl.pallas_call(
        body, name="grad_pair_add_" + tag, grid_spec=grid_spec,
        out_shape=jax.ShapeDtypeStruct((nsh, half, cols), BF16),
        compiler_params=_params(("parallel", "parallel")),
    )(c_idx, g, got)


def _chip_exchange_protocol(ins, outs, send_sems, recv_sems):
    x, y, c = _place()
    chips = [(1 - x, y), (x, 1 - y), (1 - x, 1 - y)]

    def copies():
        return [pltpu.make_async_remote_copy(src_ref=ins[a].at[2 * px + py], dst_ref=outs[a].at[j],
                                             send_sem=send_sems.at[3 * a + j], recv_sem=recv_sems.at[3 * a + j],
                                             device_id=(px, py, c), device_id_type=MESH)
                for a in range(len(ins)) for j, (px, py) in enumerate(chips)]

    def begin():
        for cp in copies():
            cp.start()

    def end():
        for cp in copies():
            cp.wait_recv()
        for cp in copies():
            cp.wait_send()

    return begin, end


def _chip_exchange_shapes(ps):
    return [jax.ShapeDtypeStruct((N_CHIPS - 1,) + p.shape[1:], p.dtype) for p in ps]


def _chip_exchange_sems(n):
    return [pltpu.SemaphoreType.DMA((3 * n,)), pltpu.SemaphoreType.DMA((3 * n,))]


def _chip_exchange(ps):
    n = len(ps)

    def body(*refs):
        begin, end = _chip_exchange_protocol(refs[:n], refs[n:2 * n], *refs[2 * n:])
        begin()
        end()

    return pl.pallas_call(
        body, name="grad_chip_exchange", in_specs=_hbm_specs(n), out_specs=_hbm_specs(n),
        out_shape=_chip_exchange_shapes(ps), scratch_shapes=_chip_exchange_sems(n),
    )(*ps)


def _sum_partials(p, got, chip_idx, tag):
    nsh, half, cols = got.shape
    rb = _pick_rows(half)

    def body(me_ref, p_ref, got_ref, o_ref):
        acc = p_ref[0].astype(F32)
        for s in range(nsh):
            acc = acc + got_ref[s].astype(F32)
        o_ref[...] = acc

    grid_spec = pltpu.PrefetchScalarGridSpec(
        num_scalar_prefetch=1, grid=(half // rb,),
        in_specs=[pl.BlockSpec((1, rb, cols), lambda i, me_ref: (me_ref[0], i, 0)),
                  pl.BlockSpec((nsh, rb, cols), lambda i, me_ref: (0, i, 0))],
        out_specs=pl.BlockSpec((rb, cols), lambda i, me_ref: (i, 0)))
    return pl.pallas_call(
        body, name="grad_sum_chips_" + tag, grid_spec=grid_spec,
        out_shape=jax.ShapeDtypeStruct((half, cols), F32),
        compiler_params=_params(("parallel",)),
    )(chip_idx, p, got)


def _pair_share(rs):
    n = len(rs)

    def body(*refs):
        ins, outs, (send_sems, recv_sems) = refs[:n], refs[n:2 * n], refs[2 * n:]
        x, y, c = _place()
        cps = []
        for a in range(n):
            cp = pltpu.make_async_remote_copy(src_ref=ins[a], dst_ref=outs[a], send_sem=send_sems.at[a],
                                              recv_sem=recv_sems.at[a], device_id=(x, y, 1 - c),
                                              device_id_type=MESH)
            cp.start()
            cps.append(cp)
        for cp in cps:
            cp.wait()

    return pl.pallas_call(
        body, name="grad_pair_share", in_specs=_hbm_specs(n), out_specs=_hbm_specs(n),
        out_shape=[jax.ShapeDtypeStruct(r.shape, r.dtype) for r in rs],
        scratch_shapes=[pltpu.SemaphoreType.DMA((n,)), pltpu.SemaphoreType.DMA((n,))],
    )(*rs)


def _small_allreduce(v):
    rows, cols = v.shape
    ndev = 8

    def body(in_ref, out_ref, slots, send_sems, recv_sems):
        x, y, c = _place()
        me = 4 * x + 2 * y + c
        slots[me] = in_ref[...]
        sends = []
        for k in range(1, ndev):
            peer = (x ^ (k >> 2), y ^ ((k >> 1) & 1), c ^ (k & 1))
            cp = pltpu.make_async_remote_copy(src_ref=in_ref, dst_ref=slots.at[me], send_sem=send_sems.at[k - 1],
                                              recv_sem=recv_sems.at[k - 1], device_id=peer, device_id_type=MESH)
            cp.start()
            sends.append(cp)
        for k in range(1, ndev):
            there = slots.at[me ^ k]
            pltpu.make_async_remote_copy(src_ref=there, dst_ref=there, send_sem=send_sems.at[k - 1],
                                         recv_sem=recv_sems.at[k - 1], device_id=(x, y, c),
                                         device_id_type=MESH).wait_recv()
        for cp in sends:
            cp.wait_send()
        acc = slots[0]
        for s in range(1, ndev):
            acc = acc + slots[s]
        out_ref[...] = acc

    return pl.pallas_call(
        body, name="small_allreduce",
        in_specs=[pl.BlockSpec(memory_space=pltpu.VMEM)],
        out_specs=pl.BlockSpec(memory_space=pltpu.VMEM),
        out_shape=jax.ShapeDtypeStruct((rows, cols), F32),
        scratch_shapes=[pltpu.VMEM((ndev, rows, cols), F32), pltpu.SemaphoreType.DMA((ndev - 1,)),
                        pltpu.SemaphoreType.DMA((ndev - 1,))],
    )(v)


def _adamw(w, g, m, v, name):
    r, c = w.shape
    rb = r if r <= 128 else _pick_rows_8(r, 128)
    c1 = 1.0 - ADAM_B1 ** ADAM_STEP
    c2 = 1.0 - ADAM_B2 ** ADAM_STEP

    def body(w_ref, g_ref, m_ref, v_ref, d_ref, nm_ref, nv_ref):
        gg = g_ref[...]
        nm = ADAM_B1 * m_ref[...] + (1.0 - ADAM_B1) * gg
        nv = ADAM_B2 * v_ref[...] + (1.0 - ADAM_B2) * (gg * gg)
        d_ref[...] = -ADAM_LR * ((nm / c1) / (jnp.sqrt(nv / c2) + ADAM_EPS) + ADAM_WD * w_ref[...])
        nm_ref[...] = nm
        nv_ref[...] = nv

    blk = pl.BlockSpec((rb, c), lambda i: (i, 0))
    shp = jax.ShapeDtypeStruct((r, c), F32)
    return pl.pallas_call(
        body, name=name, grid=(r // rb,), in_specs=[blk] * 4, out_specs=[blk] * 3, out_shape=[shp] * 3,
        compiler_params=_params(("parallel",)),
    )(w, g, m, v)


def _pick_rows_8(n, target):
    best = n
    for b in range(8, min(n, target) + 1, 8):
        if n % b == 0:
            best = b
    return best


W_IN_COLS = 2308
W_UP_COLS = 1408
W_DOWN_ROWS = 704
DN_CONV_COLS = 768
FFN_CONV_COLS = 1408
PROJ_ROWS = 256
ROW_TILE = 16
ROW_SEGS = [("wp_dn", PROJ_ROWS), ("wp_sb", PROJ_ROWS), ("w_out", PROJ_ROWS), ("w_down", W_DOWN_ROWS),
            ("dn_conv", ROW_TILE), ("ffn_conv", ROW_TILE), ("spare", 2 * ROW_TILE)]
ROW_OFFS = {nm: (sum(n for _, n in ROW_SEGS[:i]), n) for i, (nm, n) in enumerate(ROW_SEGS)}
STACK_ROWS = sum(n for _, n in ROW_SEGS)
assert all(n % ROW_TILE == 0 for _, n in ROW_SEGS) and STACK_ROWS % (4 * ROW_TILE) == 0
Q_END, A_END, G_END, S_END = 3 * D_MODEL, 3 * D_MODEL + 2 * N_HEADS, 4 * D_MODEL + 2 * N_HEADS, 7 * D_MODEL + 2 * N_HEADS


def _flat_rows(a, nrows):
    flat = a.reshape(-1)
    return jnp.pad(flat, (0, nrows * D_MODEL - flat.shape[0])).reshape(nrows, D_MODEL)


IN_EXTRA_ROWS = 64


def _weight_wire(w_in, wp_dn, wp_sb, w_out, w_up, w_down, dn_conv, ffn_conv):
    bits = lax.bitcast_convert_type(dn_conv, BF16).reshape(-1)
    extra = jnp.pad(bits, (0, IN_EXTRA_ROWS * W_IN_COLS - bits.shape[0])).reshape(IN_EXTRA_ROWS, W_IN_COLS)
    stack = jnp.concatenate([wp_dn.astype(BF16), wp_sb.astype(BF16), w_out.astype(BF16), w_down.astype(BF16),
                             jnp.zeros((ROW_TILE, D_MODEL), BF16),
                             _flat_rows(lax.bitcast_convert_type(ffn_conv, BF16), ROW_TILE),
                             jnp.zeros((ROW_OFFS["spare"][1], D_MODEL), BF16)], axis=0)
    return [jnp.concatenate([w_in.astype(BF16), extra], axis=0)], [w_up.astype(BF16), stack]


def _col_range(g, lo, hi, width):
    parts = []
    for s in range(g.shape[0]):
        a, b = max(lo, s * width), min(hi, (s + 1) * width)
        if a < b:
            parts.append(g[s][:, a - s * width:b - s * width])
    return parts[0] if len(parts) == 1 else jnp.concatenate(parts, axis=1)


def _f32_rows(raw, k, ncols):
    raw = raw.reshape(N_CHIPS, -1)[:, :2 * k * ncols].reshape(N_CHIPS, k * ncols, 2)
    vals = lax.bitcast_convert_type(raw, F32).reshape(N_CHIPS, k, ncols)
    return vals.transpose(1, 0, 2).reshape(k, N_CHIPS * ncols)


def _unpack_early(g_in):
    w = g_in[:, :D_MODEL, :]
    return {
        "w_dnqkv": _col_range(w, 0, Q_END, W_IN_COLS),
        "w_ab": jnp.pad(_col_range(w, Q_END, A_END, W_IN_COLS), ((0, 0), (0, LANES - 2 * N_HEADS))),
        "w_dngate": _col_range(w, A_END, G_END, W_IN_COLS),
        "w_sbqkv": _col_range(w, G_END, S_END, W_IN_COLS),
        "w_gl": _col_range(w, S_END, N_CHIPS * W_IN_COLS, W_IN_COLS),
        "dn_conv": _f32_rows(g_in[:, D_MODEL:, :], DN_CONV, DN_CONV_COLS),
    }


def _unpack_late(g_up, g_stack):
    def seg(nm):
        at, n = ROW_OFFS[nm]
        return g_stack[:, at:at + n, :]

    ffn_conv = _f32_rows(seg("ffn_conv"), FFN_CONV, FFN_CONV_COLS)
    return {
        "wp_dn": seg("wp_dn").reshape(D_MODEL, D_MODEL),
        "wp_sb": seg("wp_sb").reshape(D_MODEL, D_MODEL),
        "w_out": seg("w_out").reshape(D_MODEL, D_MODEL),
        "w_up_g": _col_range(g_up, 0, D_FF, W_UP_COLS), "w_up_u": _col_range(g_up, D_FF, 2 * D_FF, W_UP_COLS),
        "w_down": seg("w_down").reshape(D_FF, D_MODEL),
        "ffn_conv_g": ffn_conv[:, :D_FF], "ffn_conv_u": ffn_conv[:, D_FF:],
    }


def _grad_wire_early(gr):
    def cols(a, ncols):
        return a.reshape(a.shape[0], N_CHIPS, ncols).transpose(1, 0, 2)

    def rows(a, nrows):
        return a.astype(BF16).reshape(N_CHIPS, nrows, a.shape[1])

    def flat(a, nrows):
        a = a.astype(BF16).reshape(N_CHIPS, -1)
        return jnp.pad(a, ((0, 0), (0, nrows * D_MODEL - a.shape[1]))).reshape(N_CHIPS, nrows, D_MODEL)

    up = [gr["w_up_g"], gr["w_up_u"]]
    g_up = jnp.stack([up[s // 2][:, (s % 2) * W_UP_COLS:(s % 2 + 1) * W_UP_COLS].astype(BF16) for s in range(N_CHIPS)])
    g_stack = jnp.concatenate([rows(gr["wp_dn"], PROJ_ROWS), rows(gr["wp_sb"], PROJ_ROWS), rows(gr["w_out"], PROJ_ROWS),
                               rows(gr["w_down"], W_DOWN_ROWS), jnp.zeros((N_CHIPS, ROW_TILE, D_MODEL), BF16),
                               flat(cols(gr["ffn_conv"], FFN_CONV_COLS), ROW_TILE),
                               jnp.zeros((N_CHIPS, ROW_OFFS["spare"][1], D_MODEL), BF16)], axis=1)
    return [g_up, g_stack]


def _grad_wire_late(gr):
    pieces = [(gr["w_dnqkv"], 0), (gr["w_ab"][:, :2 * N_HEADS], Q_END), (gr["w_dngate"], A_END),
              (gr["w_sbqkv"], G_END), (gr["w_gl"], S_END)]
    conv = gr["dn_conv"].reshape(DN_CONV, N_CHIPS, DN_CONV_COLS).transpose(1, 0, 2).reshape(N_CHIPS, -1)

    def block(s):
        lo, hi = s * W_IN_COLS, (s + 1) * W_IN_COLS
        parts = []
        for a, at in pieces:
            b0, b1 = max(lo, at), min(hi, at + a.shape[1])
            if b0 < b1:
                parts.append(a[:, b0 - at:b1 - at].astype(BF16))
        w = parts[0] if len(parts) == 1 else jnp.concatenate(parts, axis=1)
        extra = jnp.pad(conv[s].astype(BF16), (0, IN_EXTRA_ROWS * W_IN_COLS - conv.shape[1]))
        return jnp.concatenate([w, extra.reshape(IN_EXTRA_ROWS, W_IN_COLS)], axis=0)

    return [jnp.stack([block(s) for s in range(N_CHIPS)])]


def _unpack_grad_shard(r_in, r_up, r_stack):
    def seg(nm):
        at, n = ROW_OFFS[nm]
        return r_stack[at:at + n, :]

    return {
        "w_in": r_in[:D_MODEL], "w_up": r_up,
        "wp_dn": seg("wp_dn"), "wp_sb": seg("wp_sb"), "w_out": seg("w_out"), "w_down": seg("w_down"),
        "dn_conv": r_in[D_MODEL:].reshape(-1)[:DN_CONV * DN_CONV_COLS].reshape(DN_CONV, DN_CONV_COLS),
        "ffn_conv": seg("ffn_conv").reshape(-1)[:FFN_CONV * FFN_CONV_COLS].reshape(FFN_CONV, FFN_CONV_COLS),
    }


def _lane_row(v):
    return jnp.pad(v.reshape(1, -1), ((0, 0), (0, LANES - v.size)))


def kernel(x, norm1_w, w_in, dn_conv_w, dn_A_log, dn_dt_bias, dn_norm_w, w_proj_dn, w_proj_sb, w_out, norm2_w, ffn_w_up, ffn_conv_w, ffn_w_down, norm_f_w, loss_target, m_norm1_w, m_w_in, m_dn_conv_w, m_dn_A_log, m_dn_dt_bias, m_dn_norm_w, m_w_proj_dn, m_w_proj_sb, m_w_out, m_norm2_w, m_ffn_w_up, m_ffn_conv_w, m_ffn_w_down, m_norm_f_w, v_norm1_w, v_w_in, v_dn_conv_w, v_dn_A_log, v_dn_dt_bias, v_dn_norm_w, v_w_proj_dn, v_w_proj_sb, v_w_out, v_norm2_w, v_ffn_w_up, v_ffn_conv_w, v_ffn_w_down, v_norm_f_w):
    early, late = _weight_wire(w_in[0], w_proj_dn[0], w_proj_sb[0], w_out[0], ffn_w_up[0], ffn_w_down[0],
                               dn_conv_w[0], ffn_conv_w[0])
    chip_idx = (2 * lax.axis_index("x") + lax.axis_index("y")).astype(jnp.int32)

    def with_mine(gathered, wire):
        return [lax.dynamic_update_slice(g, mine[None], (chip_idx, 0, 0)) for g, mine in zip(gathered, wire)]

    wts = _unpack_early(*with_mine(_gather_shards(early), early))
    wts.update(norm1=norm1_w, norm2=norm2_w, normf=norm_f_w.reshape(1, D_MODEL), dn_norm=dn_norm_w,
               alog=_lane_row(dn_A_log), dtb=_lane_row(dn_dt_bias))

    c_idx = lax.axis_index("c").astype(jnp.int32).reshape(1)

    def pair_sums(wire_g, tags, when):
        return [_pair_add(g, got, c_idx, tag) for g, got, tag in zip(wire_g, _pair_exchange_halves(wire_g, when), tags)]

    loss_part, grad_x, gr, (early_sums, early_arrived) = _local_step(
        x[0], loss_target[0], wts, late, lambda gathered: _unpack_late(*with_mine(gathered, late)),
        lambda grads: pair_sums(_grad_wire_early(grads), ["w_up", "rows"], "early"))

    late_sums = pair_sums(_grad_wire_late(gr), ["w_in"], "late")
    tags = ["w_in", "w_up", "rows"]
    reduced = [_sum_partials(p, got, chip_idx.reshape(1), tag)
               for p, got, tag in zip(late_sums + early_sums, list(_chip_exchange(late_sums)) + list(early_arrived), tags)]
    is_south = lax.axis_index("c") == 0
    gsh = _unpack_grad_shard(*[jnp.concatenate([jnp.where(is_south, mine, other), jnp.where(is_south, other, mine)],
                                               axis=0) for mine, other in zip(reduced, _pair_share(reduced))])

    tail = jnp.concatenate([gr["dn_norm"], gr["alog"][:, :N_HEADS], gr["dtb"][:, :N_HEADS], loss_part[:, :1]], axis=1)
    small = jnp.concatenate([gr["norm1"], gr["norm2"], gr["normf"],
                             jnp.pad(tail, ((0, 0), (0, D_MODEL - tail.shape[1]))),
                             jnp.zeros((SMALL_ROWS - 4, D_MODEL), F32)], axis=0)
    small = _small_allreduce(small)
    at = HEAD_DIM
    g_small = {"norm1_w": small[0:1], "norm2_w": small[1:2], "norm_f_w": small[2],
               "dn_norm_w": small[3:4, :at], "dn_A_log": small[3:4, at:at + N_HEADS],
               "dn_dt_bias": small[3:4, at + N_HEADS:at + 2 * N_HEADS]}
    loss = small[3, at + 2 * N_HEADS]

    big = {"w_in": (w_in, m_w_in, v_w_in, gsh["w_in"]), "dn_conv_w": (dn_conv_w, m_dn_conv_w, v_dn_conv_w, gsh["dn_conv"]),
           "w_proj_dn": (w_proj_dn, m_w_proj_dn, v_w_proj_dn, gsh["wp_dn"]),
           "w_proj_sb": (w_proj_sb, m_w_proj_sb, v_w_proj_sb, gsh["wp_sb"]),
           "w_out": (w_out, m_w_out, v_w_out, gsh["w_out"]),
           "ffn_w_up": (ffn_w_up, m_ffn_w_up, v_ffn_w_up, gsh["w_up"]),
           "ffn_conv_w": (ffn_conv_w, m_ffn_conv_w, v_ffn_conv_w, gsh["ffn_conv"]),
           "ffn_w_down": (ffn_w_down, m_ffn_w_down, v_ffn_w_down, gsh["w_down"])}
    res = {}
    for nm, (w, m, v, g) in big.items():
        d, nm_, nv_ = _adamw(w[0], g, m[0], v[0], "adamw_" + nm)
        res[nm] = (g[None], d[None], nm_[None], nv_[None])

    names = ["norm1_w", "norm2_w", "norm_f_w", "dn_norm_w", "dn_A_log", "dn_dt_bias"]
    given = {"norm1_w": (norm1_w, m_norm1_w, v_norm1_w), "norm2_w": (norm2_w, m_norm2_w, v_norm2_w),
             "norm_f_w": (norm_f_w, m_norm_f_w, v_norm_f_w), "dn_norm_w": (dn_norm_w, m_dn_norm_w, v_dn_norm_w),
             "dn_A_log": (dn_A_log, m_dn_A_log, v_dn_A_log), "dn_dt_bias": (dn_dt_bias, m_dn_dt_bias, v_dn_dt_bias)}

    def stack(k, fill):
        rows = [jnp.pad(given[nm][k].reshape(1, -1), ((0, 0), (0, D_MODEL - given[nm][k].size)),
                        constant_values=fill) for nm in names]
        return jnp.concatenate(rows + [jnp.full((SMALL_ROWS - len(names), D_MODEL), fill, F32)], axis=0)

    g_rows = jnp.concatenate(
        [jnp.pad(g_small[nm].reshape(1, -1), ((0, 0), (0, D_MODEL - g_small[nm].size))) for nm in names]
        + [jnp.zeros((SMALL_ROWS - len(names), D_MODEL), F32)], axis=0)
    d_s, m_s, v_s = _adamw(stack(0, 0.0), g_rows, stack(1, 0.0), stack(2, 1.0), "adamw_small")
    for r, nm in enumerate(names):
        shape = given[nm][0].shape
        n = given[nm][0].size
        res[nm] = (g_small[nm].reshape(shape), d_s[r, :n].reshape(shape), m_s[r, :n].reshape(shape),
                   v_s[r, :n].reshape(shape))

    order = ["norm1_w", "w_in", "dn_conv_w", "dn_A_log", "dn_dt_bias", "dn_norm_w", "w_proj_dn", "w_proj_sb",
             "w_out", "norm2_w", "ffn_w_up", "ffn_conv_w", "ffn_w_down", "norm_f_w"]
    outs = [loss, grad_x[None]]
    for k in range(4):
        outs += [res[nm][k] for nm in order]
    return tuple(outs)
```

```python
import functools

import jax
import jax.numpy as jnp
from jax import lax
from jax.experimental import pallas as pl
from jax.experimental.pallas import tpu as pltpu

F32 = jnp.float32
BF16 = jnp.bfloat16
MESH = pl.DeviceIdType.MESH

EPS = 1e-6
D_MODEL = 1024
N_HEADS = 8
HEAD_DIM = 128
DN_CONV = 4
DN_CHUNK = 64
D_FF = 2816
FFN_CONV = 3
ADAM_LR, ADAM_B1, ADAM_B2, ADAM_EPS, ADAM_WD, ADAM_STEP = 0.001, 0.9, 0.999, 1e-08, 0.01, 10

N_CHIPS = 4
LANES = 128
HALO = 8
VMEM_LIMIT = 48 * 1024 * 1024
SMALL_ROWS = 8


def _params(sem=None):
    return pltpu.CompilerParams(dimension_semantics=sem, vmem_limit_bytes=VMEM_LIMIT)


def _pick(n, target):
    best = None
    for b in range(LANES, min(n, target) + 1, LANES):
        if n % b == 0:
            best = b
    return best or n


ELEMENTWISE_COLS = 1408


def _rows(t, target=256):
    return min(t, target)


def _dot(a, b, precision=None):
    return lax.dot_general(a, b, (((1,), (0,)), ((), ())), precision=precision, preferred_element_type=F32)


def _dot_nt(a, b, precision=None):
    return lax.dot_general(a, b, (((1,), (1,)), ((), ())), precision=precision, preferred_element_type=F32)


def _dot_tn(a, b, precision=None):
    return lax.dot_general(a, b, (((0,), (0,)), ((), ())), precision=precision, preferred_element_type=F32)


def _rms(x, w):
    return x * lax.rsqrt(jnp.mean(x * x, axis=-1, keepdims=True) + EPS) * w


def _silu(x):
    return x * jax.nn.sigmoid(x)


def _softplus(x):
    return jnp.maximum(x, 0.0) + jnp.log(1.0 + jnp.exp(-jnp.abs(x)))


MM_BLOCK = 1408
MM_VMEM_BUDGET = 38 * 1024 * 1024


def _mm(a, b, *, ta=False, tb=False, add=None, out_dtype=F32, name, bm=MM_BLOCK, bn=MM_BLOCK, bk=MM_BLOCK):
    m = a.shape[1] if ta else a.shape[0]
    k = a.shape[0] if ta else a.shape[1]
    n = b.shape[0] if tb else b.shape[1]
    bm, bn = _pick(m, bm), _pick(n, bn)

    def vmem_need(bk_):
        need = 2 * (bm * bk_ * a.dtype.itemsize + bk_ * bn * b.dtype.itemsize) + 2 * bm * bn * jnp.dtype(out_dtype).itemsize
        need += 2 * bm * bn * add.dtype.itemsize if add is not None else 0
        return need + (bm * bn * 4 if bk_ < k else 0)

    bk = max((d for d in range(LANES, k + 1, LANES) if k % d == 0 and vmem_need(d) <= MM_VMEM_BUDGET),
             default=_pick(k, bk))
    nk = k // bk
    dims = (((0 if ta else 1,), (1 if tb else 0,)), ((), ()))

    def body(*refs):
        a_ref, b_ref = refs[:2]
        c_ref = refs[2] if add is not None else None
        o_ref = refs[3] if add is not None else refs[2]
        acc = refs[-1]
        kk = pl.program_id(2)
        part = lax.dot_general(a_ref[...].astype(BF16), b_ref[...].astype(BF16), dims, preferred_element_type=F32)

        def finish(r):
            if add is not None:
                r = r + c_ref[...].astype(F32)
            o_ref[...] = r.astype(out_dtype)

        if nk == 1:
            finish(part)
            return

        @pl.when(kk == 0)
        def _():
            acc[...] = part

        @pl.when(jnp.logical_and(kk > 0, kk < nk - 1))
        def _():
            acc[...] += part

        @pl.when(kk == nk - 1)
        def _():
            finish(acc[...] + part)

    a_spec = (pl.BlockSpec((bk, bm), lambda i, j, kk: (kk, i)) if ta
              else pl.BlockSpec((bm, bk), lambda i, j, kk: (i, kk)))
    b_spec = (pl.BlockSpec((bn, bk), lambda i, j, kk: (j, kk)) if tb
              else pl.BlockSpec((bk, bn), lambda i, j, kk: (kk, j)))
    o_spec = pl.BlockSpec((bm, bn), lambda i, j, kk: (i, j))
    in_specs = [a_spec, b_spec] + ([o_spec] if add is not None else [])
    args = (a, b) + ((add,) if add is not None else ())
    return pl.pallas_call(
        body, name=name, grid=(m // bm, n // bn, nk),
        in_specs=in_specs, out_specs=o_spec,
        out_shape=jax.ShapeDtypeStruct((m, n), out_dtype),
        scratch_shapes=[pltpu.VMEM((bm, bn), F32)] if nk > 1 else [],
        compiler_params=_params(("parallel", "parallel", "arbitrary")),
    )(*args)


def _norm1_fwd(x, w, w_ab):
    t = x.shape[0]
    tb = _rows(t)

    def body(x_ref, w_ref, wab_ref, n_ref, hab_ref):
        n = _rms(x_ref[...], w_ref[...]).astype(BF16)
        n_ref[...] = n
        hab_ref[...] = _dot(n, wab_ref[...])

    return pl.pallas_call(
        body, name="norm1_fwd", grid=(t // tb,),
        in_specs=[pl.BlockSpec((tb, D_MODEL), lambda i: (i, 0)),
                  pl.BlockSpec((1, D_MODEL), lambda i: (0, 0)),
                  pl.BlockSpec((D_MODEL, LANES), lambda i: (0, 0))],
        out_specs=[pl.BlockSpec((tb, D_MODEL), lambda i: (i, 0)),
                   pl.BlockSpec((tb, LANES), lambda i: (i, 0))],
        out_shape=[jax.ShapeDtypeStruct((t, D_MODEL), BF16), jax.ShapeDtypeStruct((t, LANES), F32)],
        compiler_params=_params(("arbitrary",)),
    )(x, w, w_ab)


def _norm1_bwd(x, w, dn, dres, dab, w_ab):
    t = x.shape[0]
    tb = _rows(t)

    def body(x_ref, w_ref, dn_ref, dres_ref, dab_ref, wab_ref, dx_ref, dw_ref):
        i = pl.program_id(0)
        g = dn_ref[...] + _dot_nt(dab_ref[...].astype(BF16), wab_ref[...])
        _, vjp = jax.vjp(_rms, x_ref[...], w_ref[...])
        dx, dw = vjp(g)
        dx_ref[...] = dres_ref[...] + dx

        @pl.when(i == 0)
        def _():
            dw_ref[...] = jnp.zeros_like(dw_ref)

        dw_ref[...] += dw

    row = pl.BlockSpec((tb, D_MODEL), lambda i: (i, 0))
    vec = pl.BlockSpec((1, D_MODEL), lambda i: (0, 0))
    return pl.pallas_call(
        body, name="norm1_bwd", grid=(t // tb,),
        in_specs=[row, vec, row, row, pl.BlockSpec((tb, LANES), lambda i: (i, 0)),
                  pl.BlockSpec((D_MODEL, LANES), lambda i: (0, 0))],
        out_specs=[row, vec],
        out_shape=[jax.ShapeDtypeStruct((t, D_MODEL), F32), jax.ShapeDtypeStruct((1, D_MODEL), F32)],
        compiler_params=_params(("arbitrary",)),
    )(x, w, dn, dres, dab, w_ab)


def _conv_fwd(x, w, name):
    t, c = x.shape
    kk = w.shape[0]
    tb, cb = _rows(t, 512), _pick(c, ELEMENTWISE_COLS)
    per = tb // HALO

    def body(x_ref, halo_ref, w_ref, y_ref, buf):
        i = pl.program_id(0)
        buf[pl.ds(HALO, tb), :] = x_ref[...]
        buf[pl.ds(0, HALO), :] = jnp.where(i == 0, 0.0, halo_ref[...])
        y_ref[...] = _conv_taps(buf, w_ref, HALO - (kk - 1), tb)

    return pl.pallas_call(
        body, name=name, grid=(t // tb, c // cb),
        in_specs=[pl.BlockSpec((tb, cb), lambda i, j: (i, j)),
                  pl.BlockSpec((HALO, cb), lambda i, j: (jnp.maximum(i * per - 1, 0), j)),
                  pl.BlockSpec((kk, cb), lambda i, j: (0, j))],
        out_specs=pl.BlockSpec((tb, cb), lambda i, j: (i, j)),
        out_shape=jax.ShapeDtypeStruct((t, c), F32),
        scratch_shapes=[pltpu.VMEM((tb + HALO, cb), F32)],
        compiler_params=_params(("parallel", "parallel")),
    )(x, x, w)


def _conv_bwd(dy, x, w, name, dx_dtype):
    t, c = x.shape
    kk = w.shape[0]
    tb, cb = _rows(t, 512), _pick(c, ELEMENTWISE_COLS)
    per = tb // HALO
    nblk = t // tb

    def body(dy_ref, after_ref, x_ref, w_ref, dx_ref, dw_ref, dbuf):
        i = pl.program_id(1)
        dbuf[pl.ds(0, tb), :] = dy_ref[...]
        dbuf[pl.ds(tb, HALO), :] = jnp.where(i == nblk - 1, 0.0, after_ref[...])

        @pl.when(i == 0)
        def _():
            dw_ref[...] = jnp.zeros_like(dw_ref)

        for j in range(cb // LANES):
            sl = pl.ds(j * LANES, LANES)
            x = x_ref[:, sl]
            dx = None
            for s in range(kk):
                shifted = dbuf[pl.ds(kk - 1 - s, tb), sl]
                term = w_ref[s:s + 1, sl] * shifted
                dx = term if dx is None else dx + term
                dw_ref[s:s + 1, sl] += jnp.sum(shifted * x, axis=0, keepdims=True)
            dx_ref[:, sl] = dx.astype(dx_dtype)

    blk = pl.BlockSpec((tb, cb), lambda j, i: (i, j))
    return pl.pallas_call(
        body, name=name, grid=(c // cb, nblk),
        in_specs=[blk,
                  pl.BlockSpec((HALO, cb), lambda j, i: (jnp.minimum((i + 1) * per, t // HALO - 1), j)),
                  blk,
                  pl.BlockSpec((kk, cb), lambda j, i: (0, j))],
        out_specs=[blk, pl.BlockSpec((HALO, cb), lambda j, i: (0, j))],
        out_shape=[jax.ShapeDtypeStruct((t, c), dx_dtype), jax.ShapeDtypeStruct((HALO, c), F32)],
        scratch_shapes=[pltpu.VMEM((tb + HALO, cb), F32)],
        compiler_params=_params(("parallel", "arbitrary")),
    )(dy, dy, x, w)


def _dn_head(c, normed):
    s = _silu(c)
    return s * lax.rsqrt(jnp.sum(s * s, axis=-1, keepdims=True) + EPS) if normed else s


def _dn_gates(hab, alog, dtb):
    lane = lax.broadcasted_iota(jnp.int32, hab.shape, 1)
    g = -jnp.exp(alog) * _softplus(hab + dtb)
    beta = jax.nn.sigmoid(hab)
    return jnp.where(lane < N_HEADS, g, jnp.where(lane < 2 * N_HEADS, beta, 0.0))


def _dn_head_slices(q_ref, k_ref, v_ref):
    return [(pl.ds((part * N_HEADS + h) * HEAD_DIM, HEAD_DIM), ref, h, part < 2)
            for part, ref in enumerate((q_ref, k_ref, v_ref)) for h in range(N_HEADS)]


def _dn_prep_fwd(c, hab, alog, dtb):
    t = c.shape[0]
    tb = _rows(t)

    def body(c_ref, hab_ref, alog_ref, dtb_ref, q_ref, k_ref, v_ref, gb_ref):
        for sl, ref, h, normed in _dn_head_slices(q_ref, k_ref, v_ref):
            ref[h] = _dn_head(c_ref[:, sl], normed)
        gb_ref[...] = _dn_gates(hab_ref[...], alog_ref[...], dtb_ref[...])

    hm = pl.BlockSpec((N_HEADS, tb, HEAD_DIM), lambda i: (0, i, 0))
    nar = pl.BlockSpec((tb, LANES), lambda i: (i, 0))
    vec = pl.BlockSpec((1, LANES), lambda i: (0, 0))
    return pl.pallas_call(
        body, name="dn_prep_fwd", grid=(t // tb,),
        in_specs=[pl.BlockSpec((tb, 3 * D_MODEL), lambda i: (i, 0)), nar, vec, vec],
        out_specs=[hm, hm, hm, nar],
        out_shape=[jax.ShapeDtypeStruct((N_HEADS, t, HEAD_DIM), F32)] * 3 + [jax.ShapeDtypeStruct((t, LANES), F32)],
        compiler_params=_params(("parallel",)),
    )(c, hab, alog, dtb)


def _dn_prep_bwd(c, hab, alog, dtb, dq, dk, dv, dgb):
    t = c.shape[0]
    tb = _rows(t)

    def body(c_ref, hab_ref, alog_ref, dtb_ref, dq_ref, dk_ref, dv_ref, dgb_ref,
             dc_ref, dhab_ref, dalog_ref, ddtb_ref):
        i = pl.program_id(0)
        for sl, ref, h, normed in _dn_head_slices(dq_ref, dk_ref, dv_ref):
            _, vjp = jax.vjp(functools.partial(_dn_head, normed=normed), c_ref[:, sl])
            dc_ref[:, sl] = vjp(ref[h])[0]
        _, vjp = jax.vjp(_dn_gates, hab_ref[...], alog_ref[...], dtb_ref[...])
        dhab, dalog, ddtb = vjp(dgb_ref[...])
        dhab_ref[...] = dhab

        @pl.when(i == 0)
        def _():
            dalog_ref[...] = jnp.zeros_like(dalog_ref)
            ddtb_ref[...] = jnp.zeros_like(ddtb_ref)

        dalog_ref[...] += dalog
        ddtb_ref[...] += ddtb

    hm = pl.BlockSpec((N_HEADS, tb, HEAD_DIM), lambda i: (0, i, 0))
    wide = pl.BlockSpec((tb, 3 * D_MODEL), lambda i: (i, 0))
    nar = pl.BlockSpec((tb, LANES), lambda i: (i, 0))
    vec = pl.BlockSpec((1, LANES), lambda i: (0, 0))
    return pl.pallas_call(
        body, name="dn_prep_bwd", grid=(t // tb,),
        in_specs=[wide, nar, vec, vec, hm, hm, hm, nar],
        out_specs=[wide, nar, vec, vec],
        out_shape=[jax.ShapeDtypeStruct((t, 3 * D_MODEL), F32), jax.ShapeDtypeStruct((t, LANES), F32),
                   jax.ShapeDtypeStruct((1, LANES), F32), jax.ShapeDtypeStruct((1, LANES), F32)],
        compiler_params=_params(("arbitrary",)),
    )(c, hab, alog, dtb, dq, dk, dv, dgb)


DN_PREC = lax.Precision.HIGH
DN_GROUP = 16


def _dn_prec(a):
    return DN_PREC if a.dtype == F32 else None


def _bdot(a, b):
    return lax.dot_general(a, b, (((2,), (1,)), ((0,), (0,))), precision=_dn_prec(a), preferred_element_type=F32)


def _bdot_nt(a, b):
    return lax.dot_general(a, b, (((2,), (2,)), ((0,), (0,))), precision=_dn_prec(a), preferred_element_type=F32)


def _bdot_tn(a, b):
    return lax.dot_general(a, b, (((1,), (1,)), ((0,), (0,))), precision=_dn_prec(a), preferred_element_type=F32)


def _unit_lower_inverse(lmat):
    c = lmat.shape[-1]
    ri = lax.broadcasted_iota(jnp.int32, (c, c), 0)
    ci = lax.broadcasted_iota(jnp.int32, (c, c), 1)
    p = -lmat
    tinv = jnp.where(ri == ci, 1.0, 0.0) + p
    for _ in range(max(c.bit_length() - 2, 0)):
        p = _bdot(p, p)
        tinv = tinv + _bdot(tinv, p)
    return tinv


@jax.custom_vjp
def _solve_with(lmat, rhs, tinv):
    return _bdot(tinv, rhs)


def _solve_with_fwd(lmat, rhs, tinv):
    sol = _bdot(tinv, rhs)
    return sol, (sol, tinv)


def _solve_with_bwd(res, dsol):
    sol, tinv = res
    drhs = _bdot_tn(tinv, dsol)
    return -_bdot_nt(drhs, sol), drhs, jnp.zeros_like(tinv)


_solve_with.defvjp(_solve_with_fwd, _solve_with_bwd)


def _dn_local(q, k, v, grow, brow, tinv):
    g, c, _ = q.shape
    ri = lax.broadcasted_iota(jnp.int32, (c, c), 0)
    ci = lax.broadcasted_iota(jnp.int32, (c, c), 1)
    lower = ri >= ci
    as_col = lambda r: jnp.sum(jnp.where(ri == ci, jnp.broadcast_to(r, (g, c, c)), 0.0), axis=2, keepdims=True)
    gcol, bcol = as_col(grow), as_col(brow)
    gc_col = jnp.sum(jnp.where(lower, jnp.broadcast_to(grow, (g, c, c)), 0.0), axis=2, keepdims=True)
    gc_row = jnp.sum(jnp.where(ri <= ci, jnp.broadcast_to(gcol, (g, c, c)), 0.0), axis=1, keepdims=True)
    qs = q * (HEAD_DIM ** -0.5)
    kb = k * bcol
    vb = v * bcol
    decay = jnp.where(lower, jnp.exp(jnp.where(lower, gc_col - gc_row, 0.0)), 0.0)
    lmat = jnp.where(ri > ci, _bdot_nt(kb.astype(BF16), k.astype(BF16)) * decay, 0.0)
    eg = jnp.exp(gc_col)
    rhs = jnp.concatenate([vb, kb * eg], axis=2)
    if tinv is None:
        tinv = _unit_lower_inverse(lmat)
    sol = _solve_with(lmat, rhs, tinv)
    a_qk = jnp.where(lower, _bdot_nt(qs.astype(BF16), k.astype(BF16)) * decay, 0.0)
    g_last = jnp.sum(grow, axis=2, keepdims=True)
    kdec = k * jnp.exp(g_last - gc_col)
    egl = jnp.broadcast_to(jnp.exp(g_last), (g, 1, HEAD_DIM))
    b16 = lambda x: x.astype(BF16)
    return sol[:, :, :HEAD_DIM], b16(sol[:, :, HEAD_DIM:]), b16(a_qk), b16(qs * eg), b16(kdec), egl, tinv


def _dn_seq(u, w, a_qk, qe, kdec, egl, s_in):
    b16 = lambda x: x.astype(BF16)
    v_new = u - _bdot(b16(w), b16(s_in))
    o = _bdot(b16(qe), b16(s_in)) + _bdot(b16(a_qk), b16(v_new))
    return o, s_in * egl + _bdot_tn(b16(kdec), b16(v_new))


def _dn_local_specs(t):
    grp = min(DN_GROUP, t // DN_CHUNK)
    rows = grp * DN_CHUNK
    blk = pl.BlockSpec((1, rows, HEAD_DIM), lambda h, i: (h, i, 0))
    row = pl.BlockSpec((1, grp, 1, DN_CHUNK), lambda h, i: (h, i, 0, 0))
    sq = pl.BlockSpec((1, grp, DN_CHUNK, DN_CHUNK), lambda h, i: (h, i, 0, 0))
    lane = pl.BlockSpec((1, grp, 1, HEAD_DIM), lambda h, i: (h, i, 0, 0))
    return grp, blk, row, sq, lane


def half(shape):
    return jax.ShapeDtypeStruct(shape.shape, BF16)


def _dn_shapes(t):
    nchunk = t // DN_CHUNK
    big = jax.ShapeDtypeStruct((N_HEADS, t, HEAD_DIM), F32)
    row = jax.ShapeDtypeStruct((N_HEADS, nchunk, 1, DN_CHUNK), F32)
    sq = jax.ShapeDtypeStruct((N_HEADS, nchunk, DN_CHUNK, DN_CHUNK), F32)
    lane = jax.ShapeDtypeStruct((N_HEADS, nchunk, 1, HEAD_DIM), F32)
    return big, row, sq, lane


def _dn_local_fwd(q, k, v, grow, brow, wire=()):
    t = q.shape[1]
    grp, blk, row, sq, lane = _dn_local_specs(t)
    big, _, sqs, lanes = _dn_shapes(t)
    n = len(wire)
    groups = t // (grp * DN_CHUNK)
    steps = N_HEADS * groups

    def body(q_ref, k_ref, v_ref, gr_ref, br_ref, *rest):
        u_ref, w_ref, a_ref, qe_ref, kd_ref, egl_ref, t_ref = rest[n:n + 7]
        if n:
            begin, middle, end = _gather_protocol(rest[:n], rest[n + 7:2 * n + 7], *rest[2 * n + 7:])
            step = pl.program_id(0) * groups + pl.program_id(1)
            pl.when(step == 0)(begin)
            pl.when(step == (GATHER_FORWARD_AT * steps) // 8)(middle)
        split = lambda r: r[0].reshape(grp, DN_CHUNK, HEAD_DIM)
        u, w, a_qk, qe, kdec, egl, tinv = _dn_local(split(q_ref), split(k_ref), split(v_ref), gr_ref[0],
                                                     br_ref[0], None)
        for ref, val in ((u_ref, u), (w_ref, w), (qe_ref, qe), (kd_ref, kdec)):
            ref[0] = val.reshape(grp * DN_CHUNK, HEAD_DIM)
        a_ref[0] = a_qk
        egl_ref[0] = egl
        t_ref[0] = tinv
        if n:
            pl.when(step == steps - 1)(end)

    assert n == 0 or steps >= 3
    return pl.pallas_call(
        body, name="dn_local_fwd", grid=(N_HEADS, groups),
        in_specs=[blk, blk, blk, row, row] + _hbm_specs(n),
        out_specs=[blk, blk, sq, blk, blk, lane, sq] + _hbm_specs(n),
        out_shape=[big, half(big), half(sqs), half(big), half(big), lanes, sqs] + _gather_out_shapes(wire),
        scratch_shapes=_gather_sems(n) if n else [],
        compiler_params=_params(("arbitrary", "arbitrary")),
    )(q, k, v, grow, brow, *wire)


def _dn_local_bwd(q, k, v, grow, brow, tinv, du, dw, da, dqe, dkd, degl):
    t = q.shape[1]
    grp, blk, row, sq, lane = _dn_local_specs(t)
    big, rows_, _, _ = _dn_shapes(t)

    def body(q_ref, k_ref, v_ref, gr_ref, br_ref, t_ref, du_ref, dw_ref, da_ref, dqe_ref, dkd_ref,
             degl_ref, dq_ref, dk_ref, dv_ref, dgr_ref, dbr_ref):
        split = lambda r: r[0].reshape(grp, DN_CHUNK, HEAD_DIM)
        tinv_v = t_ref[0]
        fn = lambda q_, k_, v_, gr_, br_: _dn_local(q_, k_, v_, gr_, br_, tinv_v)[:6]
        _, vjp = jax.vjp(fn, split(q_ref), split(k_ref), split(v_ref), gr_ref[0], br_ref[0])
        dq, dk, dv, dgr, dbr = vjp((split(du_ref), split(dw_ref), da_ref[0], split(dqe_ref), split(dkd_ref),
                                    degl_ref[0]))
        for ref, val in ((dq_ref, dq), (dk_ref, dk), (dv_ref, dv)):
            ref[0] = val.reshape(grp * DN_CHUNK, HEAD_DIM)
        dgr_ref[0] = dgr
        dbr_ref[0] = dbr

    return pl.pallas_call(
        body, name="dn_local_bwd", grid=(N_HEADS, t // (grp * DN_CHUNK)),
        in_specs=[blk, blk, blk, row, row, sq, blk, blk, sq, blk, blk, lane],
        out_specs=[blk, blk, blk, row, row],
        out_shape=[big, big, big, rows_, rows_],
        compiler_params=_params(("parallel", "parallel")),
    )(q, k, v, grow, brow, tinv, du, dw, da, dqe, dkd, degl)


DN_SEQ_CHUNKS = 4


def _dn_seq_specs(nchunk, rev):
    per = min(DN_SEQ_CHUNKS, nchunk)
    nstep = nchunk // per

    def idx(n):
        return nstep - 1 - n if rev else n

    blk = pl.BlockSpec((N_HEADS, per * DN_CHUNK, HEAD_DIM), lambda n: (0, idx(n), 0))
    sq = pl.BlockSpec((N_HEADS, per, DN_CHUNK, DN_CHUNK), lambda n: (0, idx(n), 0, 0))
    lane = pl.BlockSpec((N_HEADS, per, 1, HEAD_DIM), lambda n: (0, idx(n), 0, 0))
    st = pl.BlockSpec((N_HEADS, per, HEAD_DIM, HEAD_DIM), lambda n: (0, idx(n), 0, 0))
    return per, nstep, blk, sq, lane, st


def _dn_seq_fwd(u, w, a_qk, qe, kdec, egl):
    t = u.shape[1]
    nchunk = t // DN_CHUNK
    per, nstep, blk, sq, lane, st = _dn_seq_specs(nchunk, False)

    def body(u_ref, w_ref, a_ref, qe_ref, kd_ref, egl_ref, o_ref, s_ref, state):
        @pl.when(pl.program_id(0) == 0)
        def _():
            state[...] = jnp.zeros_like(state)

        for c in range(per):
            rows = pl.ds(c * DN_CHUNK, DN_CHUNK)
            s_in = state[...]
            s_ref[:, c] = s_in.astype(BF16)
            o_ref[:, rows], state[...] = _dn_seq(u_ref[:, rows], w_ref[:, rows], a_ref[:, c], qe_ref[:, rows],
                                                 kd_ref[:, rows], egl_ref[:, c], s_in)

    return pl.pallas_call(
        body, name="dn_seq_fwd", grid=(nstep,),
        in_specs=[blk, blk, sq, blk, blk, lane],
        out_specs=[blk, st],
        out_shape=[jax.ShapeDtypeStruct((N_HEADS, t, HEAD_DIM), F32),
                   jax.ShapeDtypeStruct((N_HEADS, nchunk, HEAD_DIM, HEAD_DIM), BF16)],
        scratch_shapes=[pltpu.VMEM((N_HEADS, HEAD_DIM, HEAD_DIM), F32)],
        compiler_params=_params(("arbitrary",)),
    )(u, w, a_qk, qe, kdec, egl)


def _dn_seq_bwd(u, w, a_qk, qe, kdec, egl, states, do):
    t = u.shape[1]
    nchunk = t // DN_CHUNK
    per, nstep, blk, sq, lane, st = _dn_seq_specs(nchunk, True)
    big, _, sqs, lanes = _dn_shapes(t)

    def body(u_ref, w_ref, a_ref, qe_ref, kd_ref, egl_ref, s_ref, do_ref,
             du_ref, dw_ref, da_ref, dqe_ref, dkd_ref, degl_ref, dstate):
        @pl.when(pl.program_id(0) == 0)
        def _():
            dstate[...] = jnp.zeros_like(dstate)

        for c in reversed(range(per)):
            rows = pl.ds(c * DN_CHUNK, DN_CHUNK)
            _, vjp = jax.vjp(_dn_seq, u_ref[:, rows], w_ref[:, rows], a_ref[:, c], qe_ref[:, rows], kd_ref[:, rows],
                             egl_ref[:, c], s_ref[:, c].astype(F32))
            (du_ref[:, rows], dw_ref[:, rows], da_ref[:, c], dqe_ref[:, rows], dkd_ref[:, rows], degl_ref[:, c],
             dstate[...]) = vjp((do_ref[:, rows], dstate[...]))

    return pl.pallas_call(
        body, name="dn_seq_bwd", grid=(nstep,),
        in_specs=[blk, blk, sq, blk, blk, lane, st, blk],
        out_specs=[blk, blk, sq, blk, blk, lane],
        out_shape=[big, half(big), half(sqs), half(big), half(big), lanes],
        scratch_shapes=[pltpu.VMEM((N_HEADS, HEAD_DIM, HEAD_DIM), F32)],
        compiler_params=_params(("arbitrary",)),
    )(u, w, a_qk, qe, kdec, egl, states, do)


def _dn_post_head(o, gate, w):
    return _rms(o, w) * _silu(gate)


def _dn_post_fwd(o, gate, w):
    t = gate.shape[0]
    tb = _rows(t)

    def body(o_ref, g_ref, w_ref, y_ref):
        for h in range(N_HEADS):
            sl = pl.ds(h * HEAD_DIM, HEAD_DIM)
            y_ref[:, sl] = _dn_post_head(o_ref[h], g_ref[:, sl], w_ref[...]).astype(BF16)

    row = pl.BlockSpec((tb, D_MODEL), lambda i: (i, 0))
    hm = pl.BlockSpec((N_HEADS, tb, HEAD_DIM), lambda i: (0, i, 0))
    return pl.pallas_call(
        body, name="dn_post_fwd", grid=(t // tb,),
        in_specs=[hm, row, pl.BlockSpec((1, HEAD_DIM), lambda i: (0, 0))],
        out_specs=row, out_shape=jax.ShapeDtypeStruct((t, D_MODEL), BF16),
        compiler_params=_params(("parallel",)),
    )(o, gate, w)


def _dn_post_bwd(o, gate, w, dy):
    t = gate.shape[0]
    tb = _rows(t)

    def body(o_ref, g_ref, w_ref, dy_ref, do_ref, dg_ref, dw_ref):
        i = pl.program_id(0)
        @pl.when(i == 0)
        def _():
            dw_ref[...] = jnp.zeros_like(dw_ref)

        for h in range(N_HEADS):
            sl = pl.ds(h * HEAD_DIM, HEAD_DIM)
            _, vjp = jax.vjp(_dn_post_head, o_ref[h], g_ref[:, sl], w_ref[...])
            do_ref[h], dg, dw = vjp(dy_ref[:, sl])
            dg_ref[:, sl] = dg.astype(BF16)
            dw_ref[...] += dw

    row = pl.BlockSpec((tb, D_MODEL), lambda i: (i, 0))
    hm = pl.BlockSpec((N_HEADS, tb, HEAD_DIM), lambda i: (0, i, 0))
    vec = pl.BlockSpec((1, HEAD_DIM), lambda i: (0, 0))
    return pl.pallas_call(
        body, name="dn_post_bwd", grid=(t // tb,),
        in_specs=[hm, row, vec, row],
        out_specs=[hm, row, vec],
        out_shape=[jax.ShapeDtypeStruct((N_HEADS, t, HEAD_DIM), F32), jax.ShapeDtypeStruct((t, D_MODEL), BF16),
                   jax.ShapeDtypeStruct((1, HEAD_DIM), F32)],
        compiler_params=_params(("arbitrary",)),
    )(o, gate, w, dy)


def _split_bf16(x):
    hi = x.astype(BF16)
    lo = (x - hi.astype(F32)).astype(BF16)
    return hi, lo


SB_Q_BLOCK = 512
SB_K_BLOCK = 256
SB_NEGLIGIBLE = -60.0


def _sb_logits(q, kb, mask, scale):
    z = _dot_nt(q, kb) * scale
    ls = jnp.minimum(z, 0.0) - jnp.log(1.0 + jnp.exp(-jnp.abs(z)))
    lk = ls - z
    if mask is not None:
        lk = jnp.where(mask, lk, 0.0)
    return ls, lk


def _sb_blocks(t):
    bq = min(SB_Q_BLOCK, t)
    bk = min(SB_K_BLOCK, bq)
    return bq, bk, bq // bk


def _sb_fwd(qkv):
    t = qkv.shape[0]
    bq, bk, nd = _sb_blocks(t)
    scale = HEAD_DIM ** -0.5

    def body(q_ref, k_ref, v_ref, o_ref, tot_ref, used_ref):
        i = pl.program_id(1)
        q = q_ref[...]
        rj = lax.broadcasted_iota(jnp.int32, (bk, bk), 0)
        cj = lax.broadcasted_iota(jnp.int32, (bk, bk), 1)
        after = (rj > cj).astype(BF16)
        trow = lax.broadcasted_iota(jnp.int32, (bq, bk), 0)
        scol = lax.broadcasted_iota(jnp.int32, (bq, bk), 1)

        def tile(j, run, acc, mask):
            off = pl.multiple_of(j * bk, bk)
            kb = k_ref[pl.ds(off, bk), :]
            vb = v_ref[pl.ds(off, bk), :]
            ls, lk = _sb_logits(q, kb, mask, scale)
            hi, lo = _split_bf16(lk)
            between = _dot(hi, after) + _dot(lo, after) + run
            a = jnp.exp(ls + between)
            if mask is not None:
                a = jnp.where(mask, a, 0.0)
            acc = acc + _dot(a.astype(BF16), vb)
            return run + jnp.sum(lk, axis=1, keepdims=True), acc

        run, acc = jnp.zeros((bq, 1), F32), jnp.zeros((bq, HEAD_DIM), F32)
        for d in reversed(range(nd)):
            run, acc = tile(i * nd + d, run, acc, scol + d * bk < trow)
        def more(c):
            return jnp.logical_and(c[0] < i * nd, jnp.max(c[1]) > SB_NEGLIGIBLE)

        def far(c):
            run_, acc_ = tile(i * nd - 1 - c[0], c[1], c[2], None)
            return c[0] + 1, run_, acc_

        used, run, acc = lax.while_loop(more, far, (jnp.int32(0), run, acc))
        o_ref[...] = acc.astype(BF16)
        tot_ref[...] = jnp.broadcast_to(run, (bq, HEAD_DIM))
        used_ref[...] = jnp.full(used_ref.shape, used, F32)

    qs = pl.BlockSpec((bq, HEAD_DIM), lambda h, i: (i, h))
    ks = pl.BlockSpec((t, HEAD_DIM), lambda h, i: (0, N_HEADS + h))
    vs = pl.BlockSpec((t, HEAD_DIM), lambda h, i: (0, 2 * N_HEADS + h))
    return pl.pallas_call(
        body, name="sb_fwd", grid=(N_HEADS, t // bq),
        in_specs=[qs, ks, vs], out_specs=[qs, qs, pl.BlockSpec((1, 1, 1, LANES), lambda h, i: (h, i, 0, 0))],
        out_shape=[jax.ShapeDtypeStruct((t, D_MODEL), BF16), jax.ShapeDtypeStruct((t, D_MODEL), F32),
                   jax.ShapeDtypeStruct((N_HEADS, t // bq, 1, LANES), F32)],
        compiler_params=_params(("parallel", "arbitrary")),
    )(qkv, qkv, qkv)


def _sb_bwd(qkv, tot, used, do, partials=()):
    t = qkv.shape[0]
    bq, bk, nd = _sb_blocks(t)
    scale = HEAD_DIM ** -0.5
    n = len(partials)
    nq = t // bq

    def body(q_ref, k_ref, v_ref, tot_ref, used_ref, do_ref, *rest):
        dq_ref, dk_ref, dv_ref = rest[n:n + 3]
        i = pl.program_id(1)
        if n:
            begin, end = _chip_exchange_protocol(rest[:n], rest[n + 3:2 * n + 3], *rest[2 * n + 3:])
            step = pl.program_id(0) * nq + i
            pl.when(step == 0)(begin)

        @pl.when(i == 0)
        def _():
            dk_ref[...] = jnp.zeros_like(dk_ref)
            dv_ref[...] = jnp.zeros_like(dv_ref)

        q = q_ref[...]
        do = do_ref[...]
        total = tot_ref[:, 0:1]
        rj = lax.broadcasted_iota(jnp.int32, (bk, bk), 0)
        cj = lax.broadcasted_iota(jnp.int32, (bk, bk), 1)
        upto = (rj <= cj).astype(BF16)
        before = (rj < cj).astype(BF16)
        trow = lax.broadcasted_iota(jnp.int32, (bq, bk), 0)
        scol = lax.broadcasted_iota(jnp.int32, (bq, bk), 1)

        def tile(j, run_k, run_e, dq, mask):
            off = pl.multiple_of(j * bk, bk)
            kb = k_ref[pl.ds(off, bk), :]
            vb = v_ref[pl.ds(off, bk), :]
            ls, lk = _sb_logits(q, kb, mask, scale)
            hi, lo = _split_bf16(lk)
            between = total - (_dot(hi, upto) + _dot(lo, upto) + run_k)
            a = jnp.exp(ls + between)
            if mask is not None:
                a = jnp.where(mask, a, 0.0)
            e = a * _dot_nt(do, vb)
            ehi, elo = _split_bf16(e)
            pre = _dot(ehi, before) + _dot(elo, before) + run_e
            sig = jnp.exp(ls)
            dz = e * (1.0 - sig) - pre * sig
            if mask is not None:
                dz = jnp.where(mask, dz, 0.0)
            dz = (dz * scale).astype(BF16)
            dq = dq + _dot(dz, kb)
            dk_ref[pl.ds(off, bk), :] += _dot_tn(dz, q)
            dv_ref[pl.ds(off, bk), :] += _dot_tn(a.astype(BF16), do)
            return (run_k + jnp.sum(lk, axis=1, keepdims=True),
                    run_e + jnp.sum(e, axis=1, keepdims=True), dq)

        zero = jnp.zeros((bq, 1), F32)
        visited = jnp.clip(jnp.max(used_ref[...]).astype(jnp.int32), 0, i * nd)
        carry = lax.fori_loop(i * nd - visited, i * nd, lambda j, c: tile(j, c[0], c[1], c[2], None),
                              (zero, zero, jnp.zeros((bq, HEAD_DIM), F32)))
        for d in range(nd):
            carry = tile(i * nd + d, *carry, scol + d * bk < trow)
        dq_ref[...] = carry[2]
        if n:
            pl.when(step == N_HEADS * nq - 1)(end)

    qs = pl.BlockSpec((bq, HEAD_DIM), lambda h, i: (i, h))
    ks = pl.BlockSpec((t, HEAD_DIM), lambda h, i: (0, N_HEADS + h))
    vs = pl.BlockSpec((t, HEAD_DIM), lambda h, i: (0, 2 * N_HEADS + h))
    full = pl.BlockSpec((t, HEAD_DIM), lambda h, i: (0, h))
    big = jax.ShapeDtypeStruct((t, D_MODEL), F32)
    return pl.pallas_call(
        body, name="sb_bwd", grid=(N_HEADS, nq),
        in_specs=[qs, ks, vs, qs, pl.BlockSpec((1, 1, 1, LANES), lambda h, i: (h, i, 0, 0)), qs] + _hbm_specs(n),
        out_specs=[qs, full, full] + _hbm_specs(n),
        out_shape=[big, big, big] + _chip_exchange_shapes(partials),
        scratch_shapes=_chip_exchange_sems(n) if n else [],
        compiler_params=_params(("arbitrary", "arbitrary")),
    )(qkv, qkv, qkv, tot, used, do, *partials)


def _merge_fwd(o_dn, o_sb, gl, x, wp_dn, wp_sb, w_out, w2):
    t = x.shape[0]
    tb = _rows(t)

    def body(odn_ref, osb_ref, gl_ref, x_ref, wpd_ref, wps_ref, wo_ref, w2_ref,
             pdn_ref, psb_ref, mix_ref, x1_ref, n2_ref):
        pdn = _dot(odn_ref[...], wpd_ref[...])
        psb = _dot(osb_ref[...], wps_ref[...])
        gates = jax.nn.sigmoid(gl_ref[...])
        mixed = (gates[:, :D_MODEL] * pdn + gates[:, D_MODEL:] * psb).astype(BF16)
        x1 = x_ref[...] + _dot(mixed, wo_ref[...])
        pdn_ref[...] = pdn.astype(BF16)
        psb_ref[...] = psb.astype(BF16)
        mix_ref[...] = mixed
        x1_ref[...] = x1
        n2_ref[...] = _rms(x1, w2_ref[...]).astype(BF16)

    row = pl.BlockSpec((tb, D_MODEL), lambda i: (i, 0))
    sq = pl.BlockSpec((D_MODEL, D_MODEL), lambda i: (0, 0))
    f = jax.ShapeDtypeStruct((t, D_MODEL), F32)
    b = jax.ShapeDtypeStruct((t, D_MODEL), BF16)
    return pl.pallas_call(
        body, name="merge_fwd", grid=(t // tb,),
        in_specs=[row, row, pl.BlockSpec((tb, 2 * D_MODEL), lambda i: (i, 0)), row, sq, sq, sq,
                  pl.BlockSpec((1, D_MODEL), lambda i: (0, 0))],
        out_specs=[row] * 5, out_shape=[b, b, b, f, b],
        compiler_params=_params(("parallel",)),
    )(o_dn, o_sb, gl, x, wp_dn, wp_sb, w_out, w2)


def _merge_bwd(dx2, dn2, x1, w2, gl, pdn, psb, wp_dn, wp_sb, w_out):
    t = x1.shape[0]
    tb = _rows(t)

    def body(dx2_ref, dn2_ref, x1_ref, w2_ref, gl_ref, pdn_ref, psb_ref, wpd_ref, wps_ref, wo_ref,
             dx1_ref, dw2_ref, dgl_ref, dpdn_ref, dpsb_ref, dodn_ref, dosb_ref):
        i = pl.program_id(0)
        _, vjp = jax.vjp(_rms, x1_ref[...], w2_ref[...])
        dxn, dw2 = vjp(dn2_ref[...])
        dx1 = dx2_ref[...] + dxn
        dx1_ref[...] = dx1

        @pl.when(i == 0)
        def _():
            dw2_ref[...] = jnp.zeros_like(dw2_ref)

        dw2_ref[...] += dw2
        dmix = _dot_nt(dx1.astype(BF16), wo_ref[...])
        gates = jax.nn.sigmoid(gl_ref[...])
        g_dn, g_sb = gates[:, :D_MODEL], gates[:, D_MODEL:]
        dpdn = (dmix * g_dn).astype(BF16)
        dpsb = (dmix * g_sb).astype(BF16)
        dgl_ref[:, :D_MODEL] = (dmix * pdn_ref[...].astype(F32) * g_dn * (1.0 - g_dn)).astype(BF16)
        dgl_ref[:, D_MODEL:] = (dmix * psb_ref[...].astype(F32) * g_sb * (1.0 - g_sb)).astype(BF16)
        dpdn_ref[...] = dpdn
        dpsb_ref[...] = dpsb
        dodn_ref[...] = _dot_nt(dpdn, wpd_ref[...])
        dosb_ref[...] = _dot_nt(dpsb, wps_ref[...]).astype(BF16)

    row = pl.BlockSpec((tb, D_MODEL), lambda i: (i, 0))
    wide = pl.BlockSpec((tb, 2 * D_MODEL), lambda i: (i, 0))
    sq = pl.BlockSpec((D_MODEL, D_MODEL), lambda i: (0, 0))
    vec = pl.BlockSpec((1, D_MODEL), lambda i: (0, 0))
    f = jax.ShapeDtypeStruct((t, D_MODEL), F32)
    b = jax.ShapeDtypeStruct((t, D_MODEL), BF16)
    return pl.pallas_call(
        body, name="merge_bwd", grid=(t // tb,),
        in_specs=[row, row, row, vec, wide, row, row, sq, sq, sq],
        out_specs=[row, vec, wide, row, row, row, row],
        out_shape=[f, jax.ShapeDtypeStruct((1, D_MODEL), F32), jax.ShapeDtypeStruct((t, 2 * D_MODEL), BF16),
                   b, b, f, b],
        compiler_params=_params(("arbitrary",)),
    )(dx2, dn2, x1, w2, gl, pdn, psb, wp_dn, wp_sb, w_out)


def _conv_taps(buf, w_ref, first, rows, cols=slice(None)):
    y = w_ref[0:1, cols] * buf[pl.ds(first, rows), cols]
    for s in range(1, w_ref.shape[0]):
        y = y + w_ref[s:s + 1, cols] * buf[pl.ds(first + s, rows), cols]
    return y


def _ffn_mid_fwd(pre_g, pre_u, wg, wu):
    t, c = pre_g.shape
    kk = wg.shape[0]
    tb, cb = _rows(t), _pick(c, ELEMENTWISE_COLS)
    per = tb // HALO

    def body(g_ref, gh_ref, u_ref, uh_ref, wg_ref, wu_ref, a_ref, gbuf, ubuf):
        i = pl.program_id(0)
        for buf, ref, halo in ((gbuf, g_ref, gh_ref), (ubuf, u_ref, uh_ref)):
            buf[pl.ds(HALO, tb), :] = ref[...]
            buf[pl.ds(0, HALO), :] = jnp.where(i == 0, 0.0, halo[...])
        for j in range(cb // LANES):
            sl = pl.ds(j * LANES, LANES)
            ug = _conv_taps(gbuf, wg_ref, HALO - (kk - 1), tb, sl)
            uu = _conv_taps(ubuf, wu_ref, HALO - (kk - 1), tb, sl)
            a_ref[:, sl] = (_silu(ug) * uu).astype(BF16)

    blk = pl.BlockSpec((tb, cb), lambda i, j: (i, j))
    halo = pl.BlockSpec((HALO, cb), lambda i, j: (jnp.maximum(i * per - 1, 0), j))
    wspec = pl.BlockSpec((kk, cb), lambda i, j: (0, j))
    return pl.pallas_call(
        body, name="ffn_mid_fwd", grid=(t // tb, c // cb),
        in_specs=[blk, halo, blk, halo, wspec, wspec], out_specs=blk,
        out_shape=jax.ShapeDtypeStruct((t, c), BF16),
        scratch_shapes=[pltpu.VMEM((tb + HALO, cb), F32)] * 2,
        compiler_params=_params(("parallel", "parallel")),
    )(pre_g, pre_g, pre_u, pre_u, wg, wu)


def _ffn_mid_bwd(pre_g, pre_u, wg, wu, da):
    t, c = pre_g.shape
    kk = wg.shape[0]
    tb, cb = _rows(t), _pick(c, ELEMENTWISE_COLS)
    per = tb // HALO
    nblk = t // tb
    ext = tb + HALO

    def body(g_ref, gb_ref, ga_ref, u_ref, ub_ref, ua_ref, da_ref, daa_ref, wg_ref, wu_ref,
             dg_ref, du_ref, dwg_ref, dwu_ref, gbuf, ubuf, dabuf, dgbuf, dubuf):
        i = pl.program_id(1)
        last = i == nblk - 1
        for buf, ref, before, after in ((gbuf, g_ref, gb_ref, ga_ref), (ubuf, u_ref, ub_ref, ua_ref)):
            buf[pl.ds(0, HALO), :] = jnp.where(i == 0, 0.0, before[...])
            buf[pl.ds(HALO, tb), :] = ref[...]
            buf[pl.ds(HALO + tb, HALO), :] = jnp.where(last, 0.0, after[...])
        dabuf[pl.ds(0, tb), :] = da_ref[...]
        dabuf[pl.ds(tb, HALO), :] = jnp.where(last, 0.0, daa_ref[...])

        @pl.when(i == 0)
        def _():
            dwg_ref[...] = jnp.zeros_like(dwg_ref)
            dwu_ref[...] = jnp.zeros_like(dwu_ref)

        for j in range(cb // LANES):
            sl = pl.ds(j * LANES, LANES)
            ug = _conv_taps(gbuf, wg_ref, HALO - (kk - 1), ext, sl)
            uu = _conv_taps(ubuf, wu_ref, HALO - (kk - 1), ext, sl)
            _, vjp = jax.vjp(lambda g, u: _silu(g) * u, ug, uu)
            dgbuf[:, sl], dubuf[:, sl] = vjp(dabuf[:, sl])
            for dbuf, xbuf, w_ref, dx_ref, dw_ref in ((dgbuf, gbuf, wg_ref, dg_ref, dwg_ref),
                                                      (dubuf, ubuf, wu_ref, du_ref, dwu_ref)):
                x = xbuf[pl.ds(HALO, tb), sl]
                dx = None
                for s in range(kk):
                    shifted = dbuf[pl.ds(kk - 1 - s, tb), sl]
                    term = w_ref[s:s + 1, sl] * shifted
                    dx = term if dx is None else dx + term
                    dw_ref[s:s + 1, sl] += jnp.sum(shifted * x, axis=0, keepdims=True)
                dx_ref[:, sl] = dx.astype(BF16)

    blk = pl.BlockSpec((tb, cb), lambda j, i: (i, j))
    before = pl.BlockSpec((HALO, cb), lambda j, i: (jnp.maximum(i * per - 1, 0), j))
    after = pl.BlockSpec((HALO, cb), lambda j, i: (jnp.minimum((i + 1) * per, t // HALO - 1), j))
    wspec = pl.BlockSpec((kk, cb), lambda j, i: (0, j))
    dwspec = pl.BlockSpec((HALO, cb), lambda j, i: (0, j))
    half = jax.ShapeDtypeStruct((t, c), BF16)
    dwshape = jax.ShapeDtypeStruct((HALO, c), F32)
    return pl.pallas_call(
        body, name="ffn_mid_bwd", grid=(c // cb, nblk),
        in_specs=[blk, before, after, blk, before, after, blk, after, wspec, wspec],
        out_specs=[blk, blk, dwspec, dwspec],
        out_shape=[half, half, dwshape, dwshape],
        scratch_shapes=[pltpu.VMEM((ext + HALO, cb), F32)] * 2 + [pltpu.VMEM((ext, cb), F32)] * 3,
        compiler_params=_params(("parallel", "arbitrary")),
    )(pre_g, pre_g, pre_g, pre_u, pre_u, pre_u, da, da, wg, wu)


def _down_loss(a, w_down, x1, wf, target):
    t = x1.shape[0]
    tb = _rows(t)

    def body(a_ref, wd_ref, x1_ref, wf_ref, tgt_ref, dx2_ref, dwf_ref, loss_ref):
        i = pl.program_id(0)
        x2 = x1_ref[...] + _dot(a_ref[...], wd_ref[...])
        y, vjp = jax.vjp(_rms, x2, wf_ref[...])
        err = y - tgt_ref[...]
        dx2, dwf = vjp(err * (1.0 / D_MODEL))
        dx2_ref[...] = dx2
        part = jnp.sum(jnp.sum(err * err, axis=1, keepdims=True), axis=0, keepdims=True) * (0.5 / D_MODEL)

        @pl.when(i == 0)
        def _():
            dwf_ref[...] = jnp.zeros_like(dwf_ref)
            loss_ref[...] = jnp.zeros_like(loss_ref)

        dwf_ref[...] += dwf
        loss_ref[...] += jnp.broadcast_to(part, loss_ref.shape)

    row = pl.BlockSpec((tb, D_MODEL), lambda i: (i, 0))
    vec = pl.BlockSpec((1, D_MODEL), lambda i: (0, 0))
    return pl.pallas_call(
        body, name="down_loss", grid=(t // tb,),
        in_specs=[pl.BlockSpec((tb, D_FF), lambda i: (i, 0)), pl.BlockSpec((D_FF, D_MODEL), lambda i: (0, 0)),
                  row, vec, row],
        out_specs=[row, vec, pl.BlockSpec((1, LANES), lambda i: (0, 0))],
        out_shape=[jax.ShapeDtypeStruct((t, D_MODEL), F32), jax.ShapeDtypeStruct((1, D_MODEL), F32),
                   jax.ShapeDtypeStruct((1, LANES), F32)],
        compiler_params=_params(("arbitrary",)),
    )(a, w_down, x1, wf, target)


def _local_step(x, target, wts, late_wire=(), late_weights=None, early_partials=None):
    t = x.shape[0]
    nchunk = t // DN_CHUNK

    n1, hab = _norm1_fwd(x, wts["norm1"], wts["w_ab"])
    dnqkv = _mm(n1, wts["w_dnqkv"], name="h_dnqkv")
    dngate = _mm(n1, wts["w_dngate"], name="h_dngate")
    sbqkv = _mm(n1, wts["w_sbqkv"], out_dtype=BF16, name="h_sbqkv")
    gl = _mm(n1, wts["w_gl"], name="h_gl")

    cdn = _conv_fwd(dnqkv, wts["dn_conv"], "dn_conv_fwd")
    qn, kn, vv, gb = _dn_prep_fwd(cdn, hab, wts["alog"], wts["dtb"])
    per_head = gb[:, :2 * N_HEADS].T.reshape(2 * N_HEADS, nchunk, DN_CHUNK)
    grow, brow = per_head[:N_HEADS, :, None, :], per_head[N_HEADS:, :, None, :]
    u_dn, w_dn, a_qk, qe, kdec, egl, tinv, *late = _dn_local_fwd(qn, kn, vv, grow, brow, late_wire)
    if late_wire:
        wts = {**wts, **late_weights(late)}
    o_raw, states = _dn_seq_fwd(u_dn, w_dn, a_qk, qe, kdec, egl)
    o_dn = _dn_post_fwd(o_raw, dngate, wts["dn_norm"])

    o_sb, tot, sb_used = _sb_fwd(sbqkv)

    pdn, psb, mixed, x1, n2 = _merge_fwd(o_dn, o_sb, gl, x, wts["wp_dn"], wts["wp_sb"], wts["w_out"],
                                         wts["norm2"])
    pre_g = _mm(n2, wts["w_up_g"], name="ffn_up_g")
    pre_u = _mm(n2, wts["w_up_u"], name="ffn_up_u")
    act = _ffn_mid_fwd(pre_g, pre_u, wts["ffn_conv_g"], wts["ffn_conv_u"])
    dx2, d_normf, loss_part = _down_loss(act, wts["w_down"], x1, wts["normf"], target)

    grads = {"normf": d_normf}
    da = _mm(dx2, wts["w_down"], tb=True, name="d_act")
    grads["w_down"] = _mm(act, dx2, ta=True, out_dtype=BF16, name="dw_down")
    dpre_g, dpre_u, dcw_g, dcw_u = _ffn_mid_bwd(pre_g, pre_u, wts["ffn_conv_g"], wts["ffn_conv_u"], da)
    grads["ffn_conv"] = jnp.concatenate([dcw_g[:FFN_CONV], dcw_u[:FFN_CONV]], axis=1)
    dn2 = _mm(dpre_g, wts["w_up_g"], tb=True, name="dn2_g")
    dn2 = _mm(dpre_u, wts["w_up_u"], tb=True, add=dn2, name="dn2_u")
    grads["w_up_g"] = _mm(n2, dpre_g, ta=True, out_dtype=BF16, name="dw_up_g")
    grads["w_up_u"] = _mm(n2, dpre_u, ta=True, out_dtype=BF16, name="dw_up_u")

    dx1, grads["norm2"], dgl, dpdn, dpsb, do_dn, do_sb = _merge_bwd(
        dx2, dn2, x1, wts["norm2"], gl, pdn, psb, wts["wp_dn"], wts["wp_sb"], wts["w_out"])
    grads["w_out"] = _mm(mixed, dx1, ta=True, out_dtype=BF16, name="dw_out")
    grads["wp_dn"] = _mm(o_dn, dpdn, ta=True, out_dtype=BF16, name="dw_proj_dn")
    grads["wp_sb"] = _mm(o_sb, dpsb, ta=True, out_dtype=BF16, name="dw_proj_sb")

    partials = early_partials(grads) if early_partials else ()
    dsq, dsk, dsv, *arrived = _sb_bwd(sbqkv, tot, sb_used, do_sb, partials)
    dsbqkv = jnp.concatenate([dsq, dsk, dsv], axis=1).astype(BF16)

    do_raw, ddngate, grads["dn_norm"] = _dn_post_bwd(o_raw, dngate, wts["dn_norm"], do_dn)
    seq_grads = _dn_seq_bwd(u_dn, w_dn, a_qk, qe, kdec, egl, states, do_raw)
    dqn, dkn, dvv, dgrow, dbrow = _dn_local_bwd(qn, kn, vv, grow, brow, tinv, *seq_grads)
    dgb = jnp.concatenate([dgrow.reshape(N_HEADS, t), dbrow.reshape(N_HEADS, t)], axis=0).T
    dgb = jnp.pad(dgb, ((0, 0), (0, LANES - 2 * N_HEADS)))
    dcdn, dhab, grads["alog"], grads["dtb"] = _dn_prep_bwd(cdn, hab, wts["alog"], wts["dtb"], dqn, dkn, dvv, dgb)
    ddnqkv, dcw_dn = _conv_bwd(dcdn, dnqkv, wts["dn_conv"], "dn_conv_bwd", BF16)
    grads["dn_conv"] = dcw_dn[:DN_CONV]

    dn1 = _mm(ddnqkv, wts["w_dnqkv"], tb=True, name="dn1_dnqkv")
    dn1 = _mm(ddngate, wts["w_dngate"], tb=True, add=dn1, name="dn1_dngate")
    dn1 = _mm(dsbqkv, wts["w_sbqkv"], tb=True, add=dn1, name="dn1_sbqkv")
    dn1 = _mm(dgl, wts["w_gl"], tb=True, add=dn1, name="dn1_gl")
    grads["w_dnqkv"] = _mm(n1, ddnqkv, ta=True, out_dtype=BF16, name="dw_dnqkv")
    grads["w_dngate"] = _mm(n1, ddngate, ta=True, out_dtype=BF16, name="dw_dngate")
    grads["w_sbqkv"] = _mm(n1, dsbqkv, ta=True, out_dtype=BF16, name="dw_sbqkv")
    grads["w_gl"] = _mm(n1, dgl, ta=True, out_dtype=BF16, name="dw_gl")
    grads["w_ab"] = _mm(n1, dhab, ta=True, out_dtype=BF16, name="dw_ab")
    grad_x, grads["norm1"] = _norm1_bwd(x, wts["norm1"], dn1, dx1, dhab, wts["w_ab"])
    return loss_part, grad_x, grads, (list(partials), arrived)


def _place():
    return lax.axis_index("x"), lax.axis_index("y"), lax.axis_index("c")


def _hbm_specs(n):
    return [pl.BlockSpec(memory_space=pltpu.HBM)] * n


GATHER_SEMS = 8
GATHER_FORWARD_AT = 5


def _gather_protocol(ins, outs, send_sems, recv_sems):
    n = len(ins)
    x, y, c = _place()
    me = 2 * x + y
    sibling = (x, y, 1 - c)
    xn, yn, dg = (1 - x, y), (x, 1 - y), (1 - x, 1 - y)
    idx = lambda chip: 2 * chip[0] + chip[1]

    def part(a, chip_index, core, quarter=None):
        half = ins[a].shape[0] // 2
        if quarter is None:
            return outs[a].at[chip_index, pl.ds(core * half, half), :]
        return outs[a].at[chip_index, pl.ds(core * half + quarter * (half // 2), half // 2), :]

    def copy(a, k, src, dst, to):
        return pltpu.make_async_remote_copy(src_ref=src, dst_ref=dst, send_sem=send_sems.at[GATHER_SEMS * a + k],
                                            recv_sem=recv_sems.at[GATHER_SEMS * a + k], device_id=to,
                                            device_id_type=MESH)

    def sent(a, k):
        half = ins[a].shape[0] // 2
        my_half = ins[a].at[pl.ds(c * half, half), :]
        if k < 2:
            return copy(a, k, my_half, part(a, me, c), (*(xn, yn)[k], c))
        if k < 4:
            src = part(a, idx((xn, yn)[k - 2]), c, k - 2)
            return copy(a, k, src, src, (*(yn, xn)[k - 2], c))
        src = (part(a, idx(xn), c), part(a, idx(yn), c), part(a, idx(dg), c, 0), part(a, idx(dg), c, 1))[k - 4]
        return copy(a, k, src, src, sibling)

    def landed(a, k):
        dst = (part(a, idx(xn), c), part(a, idx(yn), c), part(a, idx(dg), c, 0), part(a, idx(dg), c, 1),
               part(a, idx(xn), 1 - c), part(a, idx(yn), 1 - c), part(a, idx(dg), 1 - c, 0),
               part(a, idx(dg), 1 - c, 1))[k]
        return copy(a, k, dst, dst, sibling)

    def begin():
        for a in range(n):
            sent(a, 0).start()
            sent(a, 1).start()

    def middle():
        for a in range(n):
            for k in range(2):
                landed(a, k).wait_recv()
                sent(a, 2 + k).start()
                sent(a, 4 + k).start()

    def end():
        for a in range(n):
            for k in (2, 3):
                landed(a, k).wait_recv()
                sent(a, 4 + k).start()
        for a in range(n):
            for k in range(4, GATHER_SEMS):
                landed(a, k).wait_recv()
        for a in range(n):
            for k in range(GATHER_SEMS):
                sent(a, k).wait_send()

    return begin, middle, end


def _gather_out_shapes(shards):
    return [jax.ShapeDtypeStruct((N_CHIPS,) + s.shape, s.dtype) for s in shards]


def _gather_sems(n):
    return [pltpu.SemaphoreType.DMA((GATHER_SEMS * n,)), pltpu.SemaphoreType.DMA((GATHER_SEMS * n,))]


def _gather_shards(shards):
    n = len(shards)

    def body(*refs):
        begin, middle, end = _gather_protocol(refs[:n], refs[n:2 * n], *refs[2 * n:])
        begin()
        middle()
        end()

    return pl.pallas_call(
        body, name="gather_weights", in_specs=_hbm_specs(n), out_specs=_hbm_specs(n),
        out_shape=_gather_out_shapes(shards), scratch_shapes=_gather_sems(n),
    )(*shards)


def _pair_exchange_halves(gs, tag):
    n = len(gs)

    def body(*refs):
        ins, outs, (send_sems, recv_sems) = refs[:n], refs[n:2 * n], refs[2 * n:]
        x, y, c = _place()
        cps = []
        for a in range(n):
            half = ins[a].shape[1] // 2
            cp = pltpu.make_async_remote_copy(src_ref=ins[a].at[:, pl.ds((1 - c) * half, half), :], dst_ref=outs[a],
                                              send_sem=send_sems.at[a], recv_sem=recv_sems.at[a],
                                              device_id=(x, y, 1 - c), device_id_type=MESH)
            cp.start()
            cps.append(cp)
        for cp in cps:
            cp.wait()

    return pl.pallas_call(
        body, name="grad_pair_exchange_" + tag, in_specs=_hbm_specs(n), out_specs=_hbm_specs(n),
        out_shape=[jax.ShapeDtypeStruct((g.shape[0], g.shape[1] // 2, g.shape[2]), g.dtype) for g in gs],
        scratch_shapes=[pltpu.SemaphoreType.DMA((n,)), pltpu.SemaphoreType.DMA((n,))],
    )(*gs)


def _pick_rows(n, target=1024):
    best = 16
    for b in range(16, min(n, target) + 1, 16):
        if n % b == 0:
            best = b
    return best


def _pair_add(g, got, c_idx, tag):
    nsh, rows, cols = g.shape
    half = rows // 2
    rb = _pick_rows(half)

    def body(c_ref, g_ref, got_ref, o_ref):
        o_ref[...] = (g_ref[...].astype(F32) + got_ref[...].astype(F32)).astype(BF16)

    nb = half // rb
    grid_spec = pltpu.PrefetchScalarGridSpec(
        num_scalar_prefetch=1, grid=(nsh, nb),
        in_specs=[pl.BlockSpec((1, rb, cols), lambda s, i, c_ref: (s, c_ref[0] * nb + i, 0)),
                  pl.BlockSpec((1, rb, cols), lambda s, i, c_ref: (s, i, 0))],
        out_specs=pl.BlockSpec((1, rb, cols), lambda s, i, c_ref: (s, i, 0)))
    return pl.pallas_call(
        body, name="grad_pair_add_" + tag, grid_spec=grid_spec,
        out_shape=jax.ShapeDtypeStruct((nsh, half, cols), BF16),
        compiler_params=_params(("parallel", "parallel")),
    )(c_idx, g, got)


def _chip_exchange_protocol(ins, outs, send_sems, recv_sems):
    x, y, c = _place()
    chips = [(1 - x, y), (x, 1 - y), (1 - x, 1 - y)]

    def copies():
        return [pltpu.make_async_remote_copy(src_ref=ins[a].at[2 * px + py], dst_ref=outs[a].at[j],
                                             send_sem=send_sems.at[3 * a + j], recv_sem=recv_sems.at[3 * a + j],
                                             device_id=(px, py, c), device_id_type=MESH)
                for a in range(len(ins)) for j, (px, py) in enumerate(chips)]

    def begin():
        for cp in copies():
            cp.start()

    def end():
        for cp in copies():
            cp.wait_recv()
        for cp in copies():
            cp.wait_send()

    return begin, end


def _chip_exchange_shapes(ps):
    return [jax.ShapeDtypeStruct((N_CHIPS - 1,) + p.shape[1:], p.dtype) for p in ps]


def _chip_exchange_sems(n):
    return [pltpu.SemaphoreType.DMA((3 * n,)), pltpu.SemaphoreType.DMA((3 * n,))]


def _chip_exchange(ps):
    n = len(ps)

    def body(*refs):
        begin, end = _chip_exchange_protocol(refs[:n], refs[n:2 * n], *refs[2 * n:])
        begin()
        end()

    return pl.pallas_call(
        body, name="grad_chip_exchange", in_specs=_hbm_specs(n), out_specs=_hbm_specs(n),
        out_shape=_chip_exchange_shapes(ps), scratch_shapes=_chip_exchange_sems(n),
    )(*ps)


def _sum_partials(p, got, chip_idx, tag):
    nsh, half, cols = got.shape
    rb = _pick_rows(half)

    def body(me_ref, p_ref, got_ref, o_ref):
        acc = p_ref[0].astype(F32)
        for s in range(nsh):
            acc = acc + got_ref[s].astype(F32)
        o_ref[...] = acc

    grid_spec = pltpu.PrefetchScalarGridSpec(
        num_scalar_prefetch=1, grid=(half // rb,),
        in_specs=[pl.BlockSpec((1, rb, cols), lambda i, me_ref: (me_ref[0], i, 0)),
                  pl.BlockSpec((nsh, rb, cols), lambda i, me_ref: (0, i, 0))],
        out_specs=pl.BlockSpec((rb, cols), lambda i, me_ref: (i, 0)))
    return pl.pallas_call(
        body, name="grad_sum_chips_" + tag, grid_spec=grid_spec,
        out_shape=jax.ShapeDtypeStruct((half, cols), F32),
        compiler_params=_params(("parallel",)),
    )(chip_idx, p, got)


def _pair_share(rs):
    n = len(rs)

    def body(*refs):
        ins, outs, (send_sems, recv_sems) = refs[:n], refs[n:2 * n], refs[2 * n:]
        x, y, c = _place()
        cps = []
        for a in range(n):
            cp = pltpu.make_async_remote_copy(src_ref=ins[a], dst_ref=outs[a], send_sem=send_sems.at[a],
                                              recv_sem=recv_sems.at[a], device_id=(x, y, 1 - c),
                                              device_id_type=MESH)
            cp.start()
            cps.append(cp)
        for cp in cps:
            cp.wait()

    return pl.pallas_call(
        body, name="grad_pair_share", in_specs=_hbm_specs(n), out_specs=_hbm_specs(n),
        out_shape=[jax.ShapeDtypeStruct(r.shape, r.dtype) for r in rs],
        scratch_shapes=[pltpu.SemaphoreType.DMA((n,)), pltpu.SemaphoreType.DMA((n,))],
    )(*rs)


def _small_allreduce(v):
    rows, cols = v.shape
    ndev = 8

    def body(in_ref, out_ref, slots, send_sems, recv_sems):
        x, y, c = _place()
        me = 4 * x + 2 * y + c
        slots[me] = in_ref[...]
        sends = []
        for k in range(1, ndev):
            peer = (x ^ (k >> 2), y ^ ((k >> 1) & 1), c ^ (k & 1))
            cp = pltpu.make_async_remote_copy(src_ref=in_ref, dst_ref=slots.at[me], send_sem=send_sems.at[k - 1],
                                              recv_sem=recv_sems.at[k - 1], device_id=peer, device_id_type=MESH)
            cp.start()
            sends.append(cp)
        for k in range(1, ndev):
            there = slots.at[me ^ k]
            pltpu.make_async_remote_copy(src_ref=there, dst_ref=there, send_sem=send_sems.at[k - 1],
                                         recv_sem=recv_sems.at[k - 1], device_id=(x, y, c),
                                         device_id_type=MESH).wait_recv()
        for cp in sends:
            cp.wait_send()
        acc = slots[0]
        for s in range(1, ndev):
            acc = acc + slots[s]
        out_ref[...] = acc

    return pl.pallas_call(
        body, name="small_allreduce",
        in_specs=[pl.BlockSpec(memory_space=pltpu.VMEM)],
        out_specs=pl.BlockSpec(memory_space=pltpu.VMEM),
        out_shape=jax.ShapeDtypeStruct((rows, cols), F32),
        scratch_shapes=[pltpu.VMEM((ndev, rows, cols), F32), pltpu.SemaphoreType.DMA((ndev - 1,)),
                        pltpu.SemaphoreType.DMA((ndev - 1,))],
    )(v)


def _adamw(w, g, m, v, name):
    r, c = w.shape
    rb = r if r <= 128 else _pick_rows_8(r, 128)
    c1 = 1.0 - ADAM_B1 ** ADAM_STEP
    c2 = 1.0 - ADAM_B2 ** ADAM_STEP

    def body(w_ref, g_ref, m_ref, v_ref, d_ref, nm_ref, nv_ref):
        gg = g_ref[...]
        nm = ADAM_B1 * m_ref[...] + (1.0 - ADAM_B1) * gg
        nv = ADAM_B2 * v_ref[...] + (1.0 - ADAM_B2) * (gg * gg)
        d_ref[...] = -ADAM_LR * ((nm / c1) / (jnp.sqrt(nv / c2) + ADAM_EPS) + ADAM_WD * w_ref[...])
        nm_ref[...] = nm
        nv_ref[...] = nv

    blk = pl.BlockSpec((rb, c), lambda i: (i, 0))
    shp = jax.ShapeDtypeStruct((r, c), F32)
    return pl.pallas_call(
        body, name=name, grid=(r // rb,), in_specs=[blk] * 4, out_specs=[blk] * 3, out_shape=[shp] * 3,
        compiler_params=_params(("parallel",)),
    )(w, g, m, v)


def _pick_rows_8(n, target):
    best = n
    for b in range(8, min(n, target) + 1, 8):
        if n % b == 0:
            best = b
    return best


W_IN_COLS = 2308
W_UP_COLS = 1408
W_DOWN_ROWS = 704
DN_CONV_COLS = 768
FFN_CONV_COLS = 1408
PROJ_ROWS = 256
ROW_TILE = 16
ROW_SEGS = [("wp_dn", PROJ_ROWS), ("wp_sb", PROJ_ROWS), ("w_out", PROJ_ROWS), ("w_down", W_DOWN_ROWS),
            ("dn_conv", ROW_TILE), ("ffn_conv", ROW_TILE), ("spare", 2 * ROW_TILE)]
ROW_OFFS = {nm: (sum(n for _, n in ROW_SEGS[:i]), n) for i, (nm, n) in enumerate(ROW_SEGS)}
STACK_ROWS = sum(n for _, n in ROW_SEGS)
assert all(n % ROW_TILE == 0 for _, n in ROW_SEGS) and STACK_ROWS % (4 * ROW_TILE) == 0
Q_END, A_END, G_END, S_END = 3 * D_MODEL, 3 * D_MODEL + 2 * N_HEADS, 4 * D_MODEL + 2 * N_HEADS, 7 * D_MODEL + 2 * N_HEADS


def _flat_rows(a, nrows):
    flat = a.reshape(-1)
    return jnp.pad(flat, (0, nrows * D_MODEL - flat.shape[0])).reshape(nrows, D_MODEL)


IN_EXTRA_ROWS = 64


def _weight_wire(w_in, wp_dn, wp_sb, w_out, w_up, w_down, dn_conv, ffn_conv):
    bits = lax.bitcast_convert_type(dn_conv, BF16).reshape(-1)
    extra = jnp.pad(bits, (0, IN_EXTRA_ROWS * W_IN_COLS - bits.shape[0])).reshape(IN_EXTRA_ROWS, W_IN_COLS)
    stack = jnp.concatenate([wp_dn.astype(BF16), wp_sb.astype(BF16), w_out.astype(BF16), w_down.astype(BF16),
                             jnp.zeros((ROW_TILE, D_MODEL), BF16),
                             _flat_rows(lax.bitcast_convert_type(ffn_conv, BF16), ROW_TILE),
                             jnp.zeros((ROW_OFFS["spare"][1], D_MODEL), BF16)], axis=0)
    return [jnp.concatenate([w_in.astype(BF16), extra], axis=0)], [w_up.astype(BF16), stack]


def _col_range(g, lo, hi, width):
    parts = []
    for s in range(g.shape[0]):
        a, b = max(lo, s * width), min(hi, (s + 1) * width)
        if a < b:
            parts.append(g[s][:, a - s * width:b - s * width])
    return parts[0] if len(parts) == 1 else jnp.concatenate(parts, axis=1)


def _f32_rows(raw, k, ncols):
    raw = raw.reshape(N_CHIPS, -1)[:, :2 * k * ncols].reshape(N_CHIPS, k * ncols, 2)
    vals = lax.bitcast_convert_type(raw, F32).reshape(N_CHIPS, k, ncols)
    return vals.transpose(1, 0, 2).reshape(k, N_CHIPS * ncols)


def _unpack_early(g_in):
    w = g_in[:, :D_MODEL, :]
    return {
        "w_dnqkv": _col_range(w, 0, Q_END, W_IN_COLS),
        "w_ab": jnp.pad(_col_range(w, Q_END, A_END, W_IN_COLS), ((0, 0), (0, LANES - 2 * N_HEADS))),
        "w_dngate": _col_range(w, A_END, G_END, W_IN_COLS),
        "w_sbqkv": _col_range(w, G_END, S_END, W_IN_COLS),
        "w_gl": _col_range(w, S_END, N_CHIPS * W_IN_COLS, W_IN_COLS),
        "dn_conv": _f32_rows(g_in[:, D_MODEL:, :], DN_CONV, DN_CONV_COLS),
    }


def _unpack_late(g_up, g_stack):
    def seg(nm):
        at, n = ROW_OFFS[nm]
        return g_stack[:, at:at + n, :]

    ffn_conv = _f32_rows(seg("ffn_conv"), FFN_CONV, FFN_CONV_COLS)
    return {
        "wp_dn": seg("wp_dn").reshape(D_MODEL, D_MODEL),
        "wp_sb": seg("wp_sb").reshape(D_MODEL, D_MODEL),
        "w_out": seg("w_out").reshape(D_MODEL, D_MODEL),
        "w_up_g": _col_range(g_up, 0, D_FF, W_UP_COLS), "w_up_u": _col_range(g_up, D_FF, 2 * D_FF, W_UP_COLS),
        "w_down": seg("w_down").reshape(D_FF, D_MODEL),
        "ffn_conv_g": ffn_conv[:, :D_FF], "ffn_conv_u": ffn_conv[:, D_FF:],
    }


def _grad_wire_early(gr):
    def cols(a, ncols):
        return a.reshape(a.shape[0], N_CHIPS, ncols).transpose(1, 0, 2)

    def rows(a, nrows):
        return a.astype(BF16).reshape(N_CHIPS, nrows, a.shape[1])

    def flat(a, nrows):
        a = a.astype(BF16).reshape(N_CHIPS, -1)
        return jnp.pad(a, ((0, 0), (0, nrows * D_MODEL - a.shape[1]))).reshape(N_CHIPS, nrows, D_MODEL)

    up = [gr["w_up_g"], gr["w_up_u"]]
    g_up = jnp.stack([up[s // 2][:, (s % 2) * W_UP_COLS:(s % 2 + 1) * W_UP_COLS].astype(BF16) for s in range(N_CHIPS)])
    g_stack = jnp.concatenate([rows(gr["wp_dn"], PROJ_ROWS), rows(gr["wp_sb"], PROJ_ROWS), rows(gr["w_out"], PROJ_ROWS),
                               rows(gr["w_down"], W_DOWN_ROWS), jnp.zeros((N_CHIPS, ROW_TILE, D_MODEL), BF16),
                               flat(cols(gr["ffn_conv"], FFN_CONV_COLS), ROW_TILE),
                               jnp.zeros((N_CHIPS, ROW_OFFS["spare"][1], D_MODEL), BF16)], axis=1)
    return [g_up, g_stack]


def _grad_wire_late(gr):
    pieces = [(gr["w_dnqkv"], 0), (gr["w_ab"][:, :2 * N_HEADS], Q_END), (gr["w_dngate"], A_END),
              (gr["w_sbqkv"], G_END), (gr["w_gl"], S_END)]
    conv = gr["dn_conv"].reshape(DN_CONV, N_CHIPS, DN_CONV_COLS).transpose(1, 0, 2).reshape(N_CHIPS, -1)

    def block(s):
        lo, hi = s * W_IN_COLS, (s + 1) * W_IN_COLS
        parts = []
        for a, at in pieces:
            b0, b1 = max(lo, at), min(hi, at + a.shape[1])
            if b0 < b1:
                parts.append(a[:, b0 - at:b1 - at].astype(BF16))
        w = parts[0] if len(parts) == 1 else jnp.concatenate(parts, axis=1)
        extra = jnp.pad(conv[s].astype(BF16), (0, IN_EXTRA_ROWS * W_IN_COLS - conv.shape[1]))
        return jnp.concatenate([w, extra.reshape(IN_EXTRA_ROWS, W_IN_COLS)], axis=0)

    return [jnp.stack([block(s) for s in range(N_CHIPS)])]


def _unpack_grad_shard(r_in, r_up, r_stack):
    def seg(nm):
        at, n = ROW_OFFS[nm]
        return r_stack[at:at + n, :]

    return {
        "w_in": r_in[:D_MODEL], "w_up": r_up,
        "wp_dn": seg("wp_dn"), "wp_sb": seg("wp_sb"), "w_out": seg("w_out"), "w_down": seg("w_down"),
        "dn_conv": r_in[D_MODEL:].reshape(-1)[:DN_CONV * DN_CONV_COLS].reshape(DN_CONV, DN_CONV_COLS),
        "ffn_conv": seg("ffn_conv").reshape(-1)[:FFN_CONV * FFN_CONV_COLS].reshape(FFN_CONV, FFN_CONV_COLS),
    }


def _lane_row(v):
    return jnp.pad(v.reshape(1, -1), ((0, 0), (0, LANES - v.size)))


def kernel(x, norm1_w, w_in, dn_conv_w, dn_A_log, dn_dt_bias, dn_norm_w, w_proj_dn, w_proj_sb, w_out, norm2_w, ffn_w_up, ffn_conv_w, ffn_w_down, norm_f_w, loss_target, m_norm1_w, m_w_in, m_dn_conv_w, m_dn_A_log, m_dn_dt_bias, m_dn_norm_w, m_w_proj_dn, m_w_proj_sb, m_w_out, m_norm2_w, m_ffn_w_up, m_ffn_conv_w, m_ffn_w_down, m_norm_f_w, v_norm1_w, v_w_in, v_dn_conv_w, v_dn_A_log, v_dn_dt_bias, v_dn_norm_w, v_w_proj_dn, v_w_proj_sb, v_w_out, v_norm2_w, v_ffn_w_up, v_ffn_conv_w, v_ffn_w_down, v_norm_f_w):
    early, late = _weight_wire(w_in[0], w_proj_dn[0], w_proj_sb[0], w_out[0], ffn_w_up[0], ffn_w_down[0],
                               dn_conv_w[0], ffn_conv_w[0])
    chip_idx = (2 * lax.axis_index("x") + lax.axis_index("y")).astype(jnp.int32)

    def with_mine(gathered, wire):
        return [lax.dynamic_update_slice(g, mine[None], (chip_idx, 0, 0)) for g, mine in zip(gathered, wire)]

    wts = _unpack_early(*with_mine(_gather_shards(early), early))
    wts.update(norm1=norm1_w, norm2=norm2_w, normf=norm_f_w.reshape(1, D_MODEL), dn_norm=dn_norm_w,
               alog=_lane_row(dn_A_log), dtb=_lane_row(dn_dt_bias))

    c_idx = lax.axis_index("c").astype(jnp.int32).reshape(1)

    def pair_sums(wire_g, tags, when):
        return [_pair_add(g, got, c_idx, tag) for g, got, tag in zip(wire_g, _pair_exchange_halves(wire_g, when), tags)]

    loss_part, grad_x, gr, (early_sums, early_arrived) = _local_step(
        x[0], loss_target[0], wts, late, lambda gathered: _unpack_late(*with_mine(gathered, late)),
        lambda grads: pair_sums(_grad_wire_early(grads), ["w_up", "rows"], "early"))

    late_sums = pair_sums(_grad_wire_late(gr), ["w_in"], "late")
    tags = ["w_in", "w_up", "rows"]
    reduced = [_sum_partials(p, got, chip_idx.reshape(1), tag)
               for p, got, tag in zip(late_sums + early_sums, list(_chip_exchange(late_sums)) + list(early_arrived), tags)]
    is_south = lax.axis_index("c") == 0
    gsh = _unpack_grad_shard(*[jnp.concatenate([jnp.where(is_south, mine, other), jnp.where(is_south, other, mine)],
                                               axis=0) for mine, other in zip(reduced, _pair_share(reduced))])

    tail = jnp.concatenate([gr["dn_norm"], gr["alog"][:, :N_HEADS], gr["dtb"][:, :N_HEADS], loss_part[:, :1]], axis=1)
    small = jnp.concatenate([gr["norm1"], gr["norm2"], gr["normf"],
                             jnp.pad(tail, ((0, 0), (0, D_MODEL - tail.shape[1]))),
                             jnp.zeros((SMALL_ROWS - 4, D_MODEL), F32)], axis=0)
    small = _small_allreduce(small)
    at = HEAD_DIM
    g_small = {"norm1_w": small[0:1], "norm2_w": small[1:2], "norm_f_w": small[2],
               "dn_norm_w": small[3:4, :at], "dn_A_log": small[3:4, at:at + N_HEADS],
               "dn_dt_bias": small[3:4, at + N_HEADS:at + 2 * N_HEADS]}
    loss = small[3, at + 2 * N_HEADS]

    big = {"w_in": (w_in, m_w_in, v_w_in, gsh["w_in"]), "dn_conv_w": (dn_conv_w, m_dn_conv_w, v_dn_conv_w, gsh["dn_conv"]),
           "w_proj_dn": (w_proj_dn, m_w_proj_dn, v_w_proj_dn, gsh["wp_dn"]),
           "w_proj_sb": (w_proj_sb, m_w_proj_sb, v_w_proj_sb, gsh["wp_sb"]),
           "w_out": (w_out, m_w_out, v_w_out, gsh["w_out"]),
           "ffn_w_up": (ffn_w_up, m_ffn_w_up, v_ffn_w_up, gsh["w_up"]),
           "ffn_conv_w": (ffn_conv_w, m_ffn_conv_w, v_ffn_conv_w, gsh["ffn_conv"]),
           "ffn_w_down": (ffn_w_down, m_ffn_w_down, v_ffn_w_down, gsh["w_down"])}
    res = {}
    for nm, (w, m, v, g) in big.items():
        d, nm_, nv_ = _adamw(w[0], g, m[0], v[0], "adamw_" + nm)
        res[nm] = (g[None], d[None], nm_[None], nv_[None])

    names = ["norm1_w", "norm2_w", "norm_f_w", "dn_norm_w", "dn_A_log", "dn_dt_bias"]
    given = {"norm1_w": (norm1_w, m_norm1_w, v_norm1_w), "norm2_w": (norm2_w, m_norm2_w, v_norm2_w),
             "norm_f_w": (norm_f_w, m_norm_f_w, v_norm_f_w), "dn_norm_w": (dn_norm_w, m_dn_norm_w, v_dn_norm_w),
             "dn_A_log": (dn_A_log, m_dn_A_log, v_dn_A_log), "dn_dt_bias": (dn_dt_bias, m_dn_dt_bias, v_dn_dt_bias)}

    def stack(k, fill):
        rows = [jnp.pad(given[nm][k].reshape(1, -1), ((0, 0), (0, D_MODEL - given[nm][k].size)),
                        constant_values=fill) for nm in names]
        return jnp.concatenate(rows + [jnp.full((SMALL_ROWS - len(names), D_MODEL), fill, F32)], axis=0)

    g_rows = jnp.concatenate(
        [jnp.pad(g_small[nm].reshape(1, -1), ((0, 0), (0, D_MODEL - g_small[nm].size))) for nm in names]
        + [jnp.zeros((SMALL_ROWS - len(names), D_MODEL), F32)], axis=0)
    d_s, m_s, v_s = _adamw(stack(0, 0.0), g_rows, stack(1, 0.0), stack(2, 1.0), "adamw_small")
    for r, nm in enumerate(names):
        shape = given[nm][0].shape
        n = given[nm][0].size
        res[nm] = (g_small[nm].reshape(shape), d_s[r, :n].reshape(shape), m_s[r, :n].reshape(shape),
                   v_s[r, :n].reshape(shape))

    order = ["norm1_w", "w_in", "dn_conv_w", "dn_A_log", "dn_dt_bias", "dn_norm_w", "w_proj_dn", "w_proj_sb",
             "w_out", "norm2_w", "ffn_w_up", "ffn_conv_w", "ffn_w_down", "norm_f_w"]
    outs = [loss, grad_x[None]]
    for k in range(4):
        outs += [res[nm][k] for nm in order]
    return tuple(outs)
```

```python
import functools

import jax
import jax.numpy as jnp
from jax import lax
from jax.experimental import pallas as pl
from jax.experimental.pallas import tpu as pltpu

F32 = jnp.float32
BF16 = jnp.bfloat16
MESH = pl.DeviceIdType.MESH

EPS = 1e-6
D_MODEL = 1024
N_HEADS = 8
HEAD_DIM = 128
DN_CONV = 4
DN_CHUNK = 64
D_FF = 2816
FFN_CONV = 3
ADAM_LR, ADAM_B1, ADAM_B2, ADAM_EPS, ADAM_WD, ADAM_STEP = 0.001, 0.9, 0.999, 1e-08, 0.01, 10

N_CHIPS = 4
LANES = 128
HALO = 8
VMEM_LIMIT = 48 * 1024 * 1024
SMALL_ROWS = 8


def _params(sem=None):
    return pltpu.CompilerParams(dimension_semantics=sem, vmem_limit_bytes=VMEM_LIMIT)


def _pick(n, target):
    best = None
    for b in range(LANES, min(n, target) + 1, LANES):
        if n % b == 0:
            best = b
    return best or n


ELEMENTWISE_COLS = 1408


def _rows(t, target=256):
    return min(t, target)


def _dot(a, b, precision=None):
    return lax.dot_general(a, b, (((1,), (0,)), ((), ())), precision=precision, preferred_element_type=F32)


def _dot_nt(a, b, precision=None):
    return lax.dot_general(a, b, (((1,), (1,)), ((), ())), precision=precision, preferred_element_type=F32)


def _dot_tn(a, b, precision=None):
    return lax.dot_general(a, b, (((0,), (0,)), ((), ())), precision=precision, preferred_element_type=F32)


def _rms(x, w):
    return x * lax.rsqrt(jnp.mean(x * x, axis=-1, keepdims=True) + EPS) * w


def _silu(x):
    return x * jax.nn.sigmoid(x)


def _softplus(x):
    return jnp.maximum(x, 0.0) + jnp.log(1.0 + jnp.exp(-jnp.abs(x)))


MM_BLOCK = 1408
MM_VMEM_BUDGET = 38 * 1024 * 1024


def _mm(a, b, *, ta=False, tb=False, add=None, out_dtype=F32, name, bm=MM_BLOCK, bn=MM_BLOCK, bk=MM_BLOCK):
    m = a.shape[1] if ta else a.shape[0]
    k = a.shape[0] if ta else a.shape[1]
    n = b.shape[0] if tb else b.shape[1]
    bm, bn = _pick(m, bm), _pick(n, bn)

    def vmem_need(bk_):
        need = 2 * (bm * bk_ * a.dtype.itemsize + bk_ * bn * b.dtype.itemsize) + 2 * bm * bn * jnp.dtype(out_dtype).itemsize
        need += 2 * bm * bn * add.dtype.itemsize if add is not None else 0
        return need + (bm * bn * 4 if bk_ < k else 0)

    bk = max((d for d in range(LANES, k + 1, LANES) if k % d == 0 and vmem_need(d) <= MM_VMEM_BUDGET),
             default=_pick(k, bk))
    nk = k // bk
    dims = (((0 if ta else 1,), (1 if tb else 0,)), ((), ()))

    def body(*refs):
        a_ref, b_ref = refs[:2]
        c_ref = refs[2] if add is not None else None
        o_ref = refs[3] if add is not None else refs[2]
        acc = refs[-1]
        kk = pl.program_id(2)
        part = lax.dot_general(a_ref[...].astype(BF16), b_ref[...].astype(BF16), dims, preferred_element_type=F32)

        def finish(r):
            if add is not None:
                r = r + c_ref[...].astype(F32)
            o_ref[...] = r.astype(out_dtype)

        if nk == 1:
            finish(part)
            return

        @pl.when(kk == 0)
        def _():
            acc[...] = part

        @pl.when(jnp.logical_and(kk > 0, kk < nk - 1))
        def _():
            acc[...] += part

        @pl.when(kk == nk - 1)
        def _():
            finish(acc[...] + part)

    a_spec = (pl.BlockSpec((bk, bm), lambda i, j, kk: (kk, i)) if ta
              else pl.BlockSpec((bm, bk), lambda i, j, kk: (i, kk)))
    b_spec = (pl.BlockSpec((bn, bk), lambda i, j, kk: (j, kk)) if tb
              else pl.BlockSpec((bk, bn), lambda i, j, kk: (kk, j)))
    o_spec = pl.BlockSpec((bm, bn), lambda i, j, kk: (i, j))
    in_specs = [a_spec, b_spec] + ([o_spec] if add is not None else [])
    args = (a, b) + ((add,) if add is not None else ())
    return pl.pallas_call(
        body, name=name, grid=(m // bm, n // bn, nk),
        in_specs=in_specs, out_specs=o_spec,
        out_shape=jax.ShapeDtypeStruct((m, n), out_dtype),
        scratch_shapes=[pltpu.VMEM((bm, bn), F32)] if nk > 1 else [],
        compiler_params=_params(("parallel", "parallel", "arbitrary")),
    )(*args)


def _norm1_fwd(x, w, w_ab):
    t = x.shape[0]
    tb = _rows(t)

    def body(x_ref, w_ref, wab_ref, n_ref, hab_ref):
        n = _rms(x_ref[...], w_ref[...]).astype(BF16)
        n_ref[...] = n
        hab_ref[...] = _dot(n, wab_ref[...])

    return pl.pallas_call(
        body, name="norm1_fwd", grid=(t // tb,),
        in_specs=[pl.BlockSpec((tb, D_MODEL), lambda i: (i, 0)),
                  pl.BlockSpec((1, D_MODEL), lambda i: (0, 0)),
                  pl.BlockSpec((D_MODEL, LANES), lambda i: (0, 0))],
        out_specs=[pl.BlockSpec((tb, D_MODEL), lambda i: (i, 0)),
                   pl.BlockSpec((tb, LANES), lambda i: (i, 0))],
        out_shape=[jax.ShapeDtypeStruct((t, D_MODEL), BF16), jax.ShapeDtypeStruct((t, LANES), F32)],
        compiler_params=_params(("arbitrary",)),
    )(x, w, w_ab)


def _norm1_bwd(x, w, dn, dres, dab, w_ab):
    t = x.shape[0]
    tb = _rows(t)

    def body(x_ref, w_ref, dn_ref, dres_ref, dab_ref, wab_ref, dx_ref, dw_ref):
        i = pl.program_id(0)
        g = dn_ref[...] + _dot_nt(dab_ref[...].astype(BF16), wab_ref[...])
        _, vjp = jax.vjp(_rms, x_ref[...], w_ref[...])
        dx, dw = vjp(g)
        dx_ref[...] = dres_ref[...] + dx

        @pl.when(i == 0)
        def _():
            dw_ref[...] = jnp.zeros_like(dw_ref)

        dw_ref[...] += dw

    row = pl.BlockSpec((tb, D_MODEL), lambda i: (i, 0))
    vec = pl.BlockSpec((1, D_MODEL), lambda i: (0, 0))
    return pl.pallas_call(
        body, name="norm1_bwd", grid=(t // tb,),
        in_specs=[row, vec, row, row, pl.BlockSpec((tb, LANES), lambda i: (i, 0)),
                  pl.BlockSpec((D_MODEL, LANES), lambda i: (0, 0))],
        out_specs=[row, vec],
        out_shape=[jax.ShapeDtypeStruct((t, D_MODEL), F32), jax.ShapeDtypeStruct((1, D_MODEL), F32)],
        compiler_params=_params(("arbitrary",)),
    )(x, w, dn, dres, dab, w_ab)


def _conv_fwd(x, w, name):
    t, c = x.shape
    kk = w.shape[0]
    tb, cb = _rows(t, 512), _pick(c, ELEMENTWISE_COLS)
    per = tb // HALO

    def body(x_ref, halo_ref, w_ref, y_ref, buf):
        i = pl.program_id(0)
        buf[pl.ds(HALO, tb), :] = x_ref[...]
        buf[pl.ds(0, HALO), :] = jnp.where(i == 0, 0.0, halo_ref[...])
        y_ref[...] = _conv_taps(buf, w_ref, HALO - (kk - 1), tb)

    return pl.pallas_call(
        body, name=name, grid=(t // tb, c // cb),
        in_specs=[pl.BlockSpec((tb, cb), lambda i, j: (i, j)),
                  pl.BlockSpec((HALO, cb), lambda i, j: (jnp.maximum(i * per - 1, 0), j)),
                  pl.BlockSpec((kk, cb), lambda i, j: (0, j))],
        out_specs=pl.BlockSpec((tb, cb), lambda i, j: (i, j)),
        out_shape=jax.ShapeDtypeStruct((t, c), F32),
        scratch_shapes=[pltpu.VMEM((tb + HALO, cb), F32)],
        compiler_params=_params(("parallel", "parallel")),
    )(x, x, w)


def _conv_bwd(dy, x, w, name, dx_dtype):
    t, c = x.shape
    kk = w.shape[0]
    tb, cb = _rows(t, 512), _pick(c, ELEMENTWISE_COLS)
    per = tb // HALO
    nblk = t // tb

    def body(dy_ref, after_ref, x_ref, w_ref, dx_ref, dw_ref, dbuf):
        i = pl.program_id(1)
        dbuf[pl.ds(0, tb), :] = dy_ref[...]
        dbuf[pl.ds(tb, HALO), :] = jnp.where(i == nblk - 1, 0.0, after_ref[...])

        @pl.when(i == 0)
        def _():
            dw_ref[...] = jnp.zeros_like(dw_ref)

        for j in range(cb // LANES):
            sl = pl.ds(j * LANES, LANES)
            x = x_ref[:, sl]
            dx = None
            for s in range(kk):
                shifted = dbuf[pl.ds(kk - 1 - s, tb), sl]
                term = w_ref[s:s + 1, sl] * shifted
                dx = term if dx is None else dx + term
                dw_ref[s:s + 1, sl] += jnp.sum(shifted * x, axis=0, keepdims=True)
            dx_ref[:, sl] = dx.astype(dx_dtype)

    blk = pl.BlockSpec((tb, cb), lambda j, i: (i, j))
    return pl.pallas_call(
        body, name=name, grid=(c // cb, nblk),
        in_specs=[blk,
                  pl.BlockSpec((HALO, cb), lambda j, i: (jnp.minimum((i + 1) * per, t // HALO - 1), j)),
                  blk,
                  pl.BlockSpec((kk, cb), lambda j, i: (0, j))],
        out_specs=[blk, pl.BlockSpec((HALO, cb), lambda j, i: (0, j))],
        out_shape=[jax.ShapeDtypeStruct((t, c), dx_dtype), jax.ShapeDtypeStruct((HALO, c), F32)],
        scratch_shapes=[pltpu.VMEM((tb + HALO, cb), F32)],
        compiler_params=_params(("parallel", "arbitrary")),
    )(dy, dy, x, w)


def _dn_head(c, normed):
    s = _silu(c)
    return s * lax.rsqrt(jnp.sum(s * s, axis=-1, keepdims=True) + EPS) if normed else s


def _dn_gates(hab, alog, dtb):
    lane = lax.broadcasted_iota(jnp.int32, hab.shape, 1)
    g = -jnp.exp(alog) * _softplus(hab + dtb)
    beta = jax.nn.sigmoid(hab)
    return jnp.where(lane < N_HEADS, g, jnp.where(lane < 2 * N_HEADS, beta, 0.0))


def _dn_head_slices(q_ref, k_ref, v_ref):
    return [(pl.ds((part * N_HEADS + h) * HEAD_DIM, HEAD_DIM), ref, h, part < 2)
            for part, ref in enumerate((q_ref, k_ref, v_ref)) for h in range(N_HEADS)]


def _dn_prep_fwd(c, hab, alog, dtb):
    t = c.shape[0]
    tb = _rows(t)

    def body(c_ref, hab_ref, alog_ref, dtb_ref, q_ref, k_ref, v_ref, gb_ref):
        for sl, ref, h, normed in _dn_head_slices(q_ref, k_ref, v_ref):
            ref[h] = _dn_head(c_ref[:, sl], normed)
        gb_ref[...] = _dn_gates(hab_ref[...], alog_ref[...], dtb_ref[...])

    hm = pl.BlockSpec((N_HEADS, tb, HEAD_DIM), lambda i: (0, i, 0))
    nar = pl.BlockSpec((tb, LANES), lambda i: (i, 0))
    vec = pl.BlockSpec((1, LANES), lambda i: (0, 0))
    return pl.pallas_call(
        body, name="dn_prep_fwd", grid=(t // tb,),
        in_specs=[pl.BlockSpec((tb, 3 * D_MODEL), lambda i: (i, 0)), nar, vec, vec],
        out_specs=[hm, hm, hm, nar],
        out_shape=[jax.ShapeDtypeStruct((N_HEADS, t, HEAD_DIM), F32)] * 3 + [jax.ShapeDtypeStruct((t, LANES), F32)],
        compiler_params=_params(("parallel",)),
    )(c, hab, alog, dtb)


def _dn_prep_bwd(c, hab, alog, dtb, dq, dk, dv, dgb):
    t = c.shape[0]
    tb = _rows(t)

    def body(c_ref, hab_ref, alog_ref, dtb_ref, dq_ref, dk_ref, dv_ref, dgb_ref,
             dc_ref, dhab_ref, dalog_ref, ddtb_ref):
        i = pl.program_id(0)
        for sl, ref, h, normed in _dn_head_slices(dq_ref, dk_ref, dv_ref):
            _, vjp = jax.vjp(functools.partial(_dn_head, normed=normed), c_ref[:, sl])
            dc_ref[:, sl] = vjp(ref[h])[0]
        _, vjp = jax.vjp(_dn_gates, hab_ref[...], alog_ref[...], dtb_ref[...])
        dhab, dalog, ddtb = vjp(dgb_ref[...])
        dhab_ref[...] = dhab

        @pl.when(i == 0)
        def _():
            dalog_ref[...] = jnp.zeros_like(dalog_ref)
            ddtb_ref[...] = jnp.zeros_like(ddtb_ref)

        dalog_ref[...] += dalog
        ddtb_ref[...] += ddtb

    hm = pl.BlockSpec((N_HEADS, tb, HEAD_DIM), lambda i: (0, i, 0))
    wide = pl.BlockSpec((tb, 3 * D_MODEL), lambda i: (i, 0))
    nar = pl.BlockSpec((tb, LANES), lambda i: (i, 0))
    vec = pl.BlockSpec((1, LANES), lambda i: (0, 0))
    return pl.pallas_call(
        body, name="dn_prep_bwd", grid=(t // tb,),
        in_specs=[wide, nar, vec, vec, hm, hm, hm, nar],
        out_specs=[wide, nar, vec, vec],
        out_shape=[jax.ShapeDtypeStruct((t, 3 * D_MODEL), F32), jax.ShapeDtypeStruct((t, LANES), F32),
                   jax.ShapeDtypeStruct((1, LANES), F32), jax.ShapeDtypeStruct((1, LANES), F32)],
        compiler_params=_params(("arbitrary",)),
    )(c, hab, alog, dtb, dq, dk, dv, dgb)


DN_PREC = lax.Precision.HIGH
DN_GROUP = 32


def _dn_prec(a):
    return DN_PREC if a.dtype == F32 else None


def _bdot(a, b):
    return lax.dot_general(a, b, (((2,), (1,)), ((0,), (0,))), precision=_dn_prec(a), preferred_element_type=F32)


def _bdot_nt(a, b):
    return lax.dot_general(a, b, (((2,), (2,)), ((0,), (0,))), precision=_dn_prec(a), preferred_element_type=F32)


def _bdot_tn(a, b):
    return lax.dot_general(a, b, (((1,), (1,)), ((0,), (0,))), precision=_dn_prec(a), preferred_element_type=F32)


def _unit_lower_inverse(lmat):
    c = lmat.shape[-1]
    ri = lax.broadcasted_iota(jnp.int32, (c, c), 0)
    ci = lax.broadcasted_iota(jnp.int32, (c, c), 1)
    p = -lmat
    tinv = jnp.where(ri == ci, 1.0, 0.0) + p
    for _ in range(max(c.bit_length() - 2, 0)):
        p = _bdot(p, p)
        tinv = tinv + _bdot(tinv, p)
    return tinv


@jax.custom_vjp
def _solve_with(lmat, rhs, tinv):
    return _bdot(tinv, rhs)


def _solve_with_fwd(lmat, rhs, tinv):
    sol = _bdot(tinv, rhs)
    return sol, (sol, tinv)


def _solve_with_bwd(res, dsol):
    sol, tinv = res
    drhs = _bdot_tn(tinv, dsol)
    return -_bdot_nt(drhs, sol), drhs, jnp.zeros_like(tinv)


_solve_with.defvjp(_solve_with_fwd, _solve_with_bwd)


def _dn_local(q, k, v, grow, brow, tinv):
    g, c, _ = q.shape
    ri = lax.broadcasted_iota(jnp.int32, (c, c), 0)
    ci = lax.broadcasted_iota(jnp.int32, (c, c), 1)
    lower = ri >= ci
    as_col = lambda r: jnp.sum(jnp.where(ri == ci, jnp.broadcast_to(r, (g, c, c)), 0.0), axis=2, keepdims=True)
    gcol, bcol = as_col(grow), as_col(brow)
    gc_col = jnp.sum(jnp.where(lower, jnp.broadcast_to(grow, (g, c, c)), 0.0), axis=2, keepdims=True)
    gc_row = jnp.sum(jnp.where(ri <= ci, jnp.broadcast_to(gcol, (g, c, c)), 0.0), axis=1, keepdims=True)
    qs = q * (HEAD_DIM ** -0.5)
    kb = k * bcol
    vb = v * bcol
    decay = jnp.where(lower, jnp.exp(jnp.where(lower, gc_col - gc_row, 0.0)), 0.0)
    lmat = jnp.where(ri > ci, _bdot_nt(kb.astype(BF16), k.astype(BF16)) * decay, 0.0)
    eg = jnp.exp(gc_col)
    rhs = jnp.concatenate([vb, kb * eg], axis=2)
    if tinv is None:
        tinv = _unit_lower_inverse(lmat)
    sol = _solve_with(lmat, rhs, tinv)
    a_qk = jnp.where(lower, _bdot_nt(qs.astype(BF16), k.astype(BF16)) * decay, 0.0)
    g_last = jnp.sum(grow, axis=2, keepdims=True)
    kdec = k * jnp.exp(g_last - gc_col)
    egl = jnp.broadcast_to(jnp.exp(g_last), (g, 1, HEAD_DIM))
    b16 = lambda x: x.astype(BF16)
    return sol[:, :, :HEAD_DIM], b16(sol[:, :, HEAD_DIM:]), b16(a_qk), b16(qs * eg), b16(kdec), egl, tinv


def _dn_seq(u, w, a_qk, qe, kdec, egl, s_in):
    b16 = lambda x: x.astype(BF16)
    v_new = u - _bdot(b16(w), b16(s_in))
    o = _bdot(b16(qe), b16(s_in)) + _bdot(b16(a_qk), b16(v_new))
    return o, s_in * egl + _bdot_tn(b16(kdec), b16(v_new))


def _dn_local_specs(t):
    grp = min(DN_GROUP, t // DN_CHUNK)
    rows = grp * DN_CHUNK
    blk = pl.BlockSpec((1, rows, HEAD_DIM), lambda h, i: (h, i, 0))
    row = pl.BlockSpec((1, grp, 1, DN_CHUNK), lambda h, i: (h, i, 0, 0))
    sq = pl.BlockSpec((1, grp, DN_CHUNK, DN_CHUNK), lambda h, i: (h, i, 0, 0))
    lane = pl.BlockSpec((1, grp, 1, HEAD_DIM), lambda h, i: (h, i, 0, 0))
    return grp, blk, row, sq, lane


def half(shape):
    return jax.ShapeDtypeStruct(shape.shape, BF16)


def _dn_shapes(t):
    nchunk = t // DN_CHUNK
    big = jax.ShapeDtypeStruct((N_HEADS, t, HEAD_DIM), F32)
    row = jax.ShapeDtypeStruct((N_HEADS, nchunk, 1, DN_CHUNK), F32)
    sq = jax.ShapeDtypeStruct((N_HEADS, nchunk, DN_CHUNK, DN_CHUNK), F32)
    lane = jax.ShapeDtypeStruct((N_HEADS, nchunk, 1, HEAD_DIM), F32)
    return big, row, sq, lane


def _dn_local_fwd(q, k, v, grow, brow, wire=()):
    t = q.shape[1]
    grp, blk, row, sq, lane = _dn_local_specs(t)
    big, _, sqs, lanes = _dn_shapes(t)
    n = len(wire)
    groups = t // (grp * DN_CHUNK)
    steps = N_HEADS * groups

    def body(q_ref, k_ref, v_ref, gr_ref, br_ref, *rest):
        u_ref, w_ref, a_ref, qe_ref, kd_ref, egl_ref, t_ref = rest[n:n + 7]
        if n:
            begin, middle, end = _gather_protocol(rest[:n], rest[n + 7:2 * n + 7], *rest[2 * n + 7:])
            step = pl.program_id(0) * groups + pl.program_id(1)
            pl.when(step == 0)(begin)
            pl.when(step == (GATHER_FORWARD_AT * steps) // 8)(middle)
        split = lambda r: r[0].reshape(grp, DN_CHUNK, HEAD_DIM)
        u, w, a_qk, qe, kdec, egl, tinv = _dn_local(split(q_ref), split(k_ref), split(v_ref), gr_ref[0],
                                                     br_ref[0], None)
        for ref, val in ((u_ref, u), (w_ref, w), (qe_ref, qe), (kd_ref, kdec)):
            ref[0] = val.reshape(grp * DN_CHUNK, HEAD_DIM)
        a_ref[0] = a_qk
        egl_ref[0] = egl
        t_ref[0] = tinv
        if n:
            pl.when(step == steps - 1)(end)

    assert n == 0 or steps >= 3
    return pl.pallas_call(
        body, name="dn_local_fwd", grid=(N_HEADS, groups),
        in_specs=[blk, blk, blk, row, row] + _hbm_specs(n),
        out_specs=[blk, blk, sq, blk, blk, lane, sq] + _hbm_specs(n),
        out_shape=[big, half(big), half(sqs), half(big), half(big), lanes, sqs] + _gather_out_shapes(wire),
        scratch_shapes=_gather_sems(n) if n else [],
        compiler_params=_params(("arbitrary", "arbitrary")),
    )(q, k, v, grow, brow, *wire)


def _dn_local_bwd(q, k, v, grow, brow, tinv, du, dw, da, dqe, dkd, degl):
    t = q.shape[1]
    grp, blk, row, sq, lane = _dn_local_specs(t)
    big, rows_, _, _ = _dn_shapes(t)

    def body(q_ref, k_ref, v_ref, gr_ref, br_ref, t_ref, du_ref, dw_ref, da_ref, dqe_ref, dkd_ref,
             degl_ref, dq_ref, dk_ref, dv_ref, dgr_ref, dbr_ref):
        split = lambda r: r[0].reshape(grp, DN_CHUNK, HEAD_DIM)
        tinv_v = t_ref[0]
        fn = lambda q_, k_, v_, gr_, br_: _dn_local(q_, k_, v_, gr_, br_, tinv_v)[:6]
        _, vjp = jax.vjp(fn, split(q_ref), split(k_ref), split(v_ref), gr_ref[0], br_ref[0])
        dq, dk, dv, dgr, dbr = vjp((split(du_ref), split(dw_ref), da_ref[0], split(dqe_ref), split(dkd_ref),
                                    degl_ref[0]))
        for ref, val in ((dq_ref, dq), (dk_ref, dk), (dv_ref, dv)):
            ref[0] = val.reshape(grp * DN_CHUNK, HEAD_DIM)
        dgr_ref[0] = dgr
        dbr_ref[0] = dbr

    return pl.pallas_call(
        body, name="dn_local_bwd", grid=(N_HEADS, t // (grp * DN_CHUNK)),
        in_specs=[blk, blk, blk, row, row, sq, blk, blk, sq, blk, blk, lane],
        out_specs=[blk, blk, blk, row, row],
        out_shape=[big, big, big, rows_, rows_],
        compiler_params=_params(("parallel", "parallel")),
    )(q, k, v, grow, brow, tinv, du, dw, da, dqe, dkd, degl)


DN_SEQ_CHUNKS = 4


def _dn_seq_specs(nchunk, rev):
    per = min(DN_SEQ_CHUNKS, nchunk)
    nstep = nchunk // per

    def idx(n):
        return nstep - 1 - n if rev else n

    blk = pl.BlockSpec((N_HEADS, per * DN_CHUNK, HEAD_DIM), lambda n: (0, idx(n), 0))
    sq = pl.BlockSpec((N_HEADS, per, DN_CHUNK, DN_CHUNK), lambda n: (0, idx(n), 0, 0))
    lane = pl.BlockSpec((N_HEADS, per, 1, HEAD_DIM), lambda n: (0, idx(n), 0, 0))
    st = pl.BlockSpec((N_HEADS, per, HEAD_DIM, HEAD_DIM), lambda n: (0, idx(n), 0, 0))
    return per, nstep, blk, sq, lane, st


def _dn_seq_fwd(u, w, a_qk, qe, kdec, egl):
    t = u.shape[1]
    nchunk = t // DN_CHUNK
    per, nstep, blk, sq, lane, st = _dn_seq_specs(nchunk, False)

    def body(u_ref, w_ref, a_ref, qe_ref, kd_ref, egl_ref, o_ref, s_ref, state):
        @pl.when(pl.program_id(0) == 0)
        def _():
            state[...] = jnp.zeros_like(state)

        for c in range(per):
            rows = pl.ds(c * DN_CHUNK, DN_CHUNK)
            s_in = state[...]
            s_ref[:, c] = s_in.astype(BF16)
            o_ref[:, rows], state[...] = _dn_seq(u_ref[:, rows], w_ref[:, rows], a_ref[:, c], qe_ref[:, rows],
                                                 kd_ref[:, rows], egl_ref[:, c], s_in)

    return pl.pallas_call(
        body, name="dn_seq_fwd", grid=(nstep,),
        in_specs=[blk, blk, sq, blk, blk, lane],
        out_specs=[blk, st],
        out_shape=[jax.ShapeDtypeStruct((N_HEADS, t, HEAD_DIM), F32),
                   jax.ShapeDtypeStruct((N_HEADS, nchunk, HEAD_DIM, HEAD_DIM), BF16)],
        scratch_shapes=[pltpu.VMEM((N_HEADS, HEAD_DIM, HEAD_DIM), F32)],
        compiler_params=_params(("arbitrary",)),
    )(u, w, a_qk, qe, kdec, egl)


def _dn_seq_bwd(u, w, a_qk, qe, kdec, egl, states, do):
    t = u.shape[1]
    nchunk = t // DN_CHUNK
    per, nstep, blk, sq, lane, st = _dn_seq_specs(nchunk, True)
    big, _, sqs, lanes = _dn_shapes(t)

    def body(u_ref, w_ref, a_ref, qe_ref, kd_ref, egl_ref, s_ref, do_ref,
             du_ref, dw_ref, da_ref, dqe_ref, dkd_ref, degl_ref, dstate):
        @pl.when(pl.program_id(0) == 0)
        def _():
            dstate[...] = jnp.zeros_like(dstate)

        for c in reversed(range(per)):
            rows = pl.ds(c * DN_CHUNK, DN_CHUNK)
            _, vjp = jax.vjp(_dn_seq, u_ref[:, rows], w_ref[:, rows], a_ref[:, c], qe_ref[:, rows], kd_ref[:, rows],
                             egl_ref[:, c], s_ref[:, c].astype(F32))
            (du_ref[:, rows], dw_ref[:, rows], da_ref[:, c], dqe_ref[:, rows], dkd_ref[:, rows], degl_ref[:, c],
             dstate[...]) = vjp((do_ref[:, rows], dstate[...]))

    return pl.pallas_call(
        body, name="dn_seq_bwd", grid=(nstep,),
        in_specs=[blk, blk, sq, blk, blk, lane, st, blk],
        out_specs=[blk, blk, sq, blk, blk, lane],
        out_shape=[big, half(big), half(sqs), half(big), half(big), lanes],
        scratch_shapes=[pltpu.VMEM((N_HEADS, HEAD_DIM, HEAD_DIM), F32)],
        compiler_params=_params(("arbitrary",)),
    )(u, w, a_qk, qe, kdec, egl, states, do)


def _dn_post_head(o, gate, w):
    return _rms(o, w) * _silu(gate)


def _dn_post_fwd(o, gate, w):
    t = gate.shape[0]
    tb = _rows(t)

    def body(o_ref, g_ref, w_ref, y_ref):
        for h in range(N_HEADS):
            sl = pl.ds(h * HEAD_DIM, HEAD_DIM)
            y_ref[:, sl] = _dn_post_head(o_ref[h], g_ref[:, sl], w_ref[...]).astype(BF16)

    row = pl.BlockSpec((tb, D_MODEL), lambda i: (i, 0))
    hm = pl.BlockSpec((N_HEADS, tb, HEAD_DIM), lambda i: (0, i, 0))
    return pl.pallas_call(
        body, name="dn_post_fwd", grid=(t // tb,),
        in_specs=[hm, row, pl.BlockSpec((1, HEAD_DIM), lambda i: (0, 0))],
        out_specs=row, out_shape=jax.ShapeDtypeStruct((t, D_MODEL), BF16),
        compiler_params=_params(("parallel",)),
    )(o, gate, w)


def _dn_post_bwd(o, gate, w, dy):
    t = gate.shape[0]
    tb = _rows(t)

    def body(o_ref, g_ref, w_ref, dy_ref, do_ref, dg_ref, dw_ref):
        i = pl.program_id(0)
        @pl.when(i == 0)
        def _():
            dw_ref[...] = jnp.zeros_like(dw_ref)

        for h in range(N_HEADS):
            sl = pl.ds(h * HEAD_DIM, HEAD_DIM)
            _, vjp = jax.vjp(_dn_post_head, o_ref[h], g_ref[:, sl], w_ref[...])
            do_ref[h], dg, dw = vjp(dy_ref[:, sl])
            dg_ref[:, sl] = dg.astype(BF16)
            dw_ref[...] += dw

    row = pl.BlockSpec((tb, D_MODEL), lambda i: (i, 0))
    hm = pl.BlockSpec((N_HEADS, tb, HEAD_DIM), lambda i: (0, i, 0))
    vec = pl.BlockSpec((1, HEAD_DIM), lambda i: (0, 0))
    return pl.pallas_call(
        body, name="dn_post_bwd", grid=(t // tb,),
        in_specs=[hm, row, vec, row],
        out_specs=[hm, row, vec],
        out_shape=[jax.ShapeDtypeStruct((N_HEADS, t, HEAD_DIM), F32), jax.ShapeDtypeStruct((t, D_MODEL), BF16),
                   jax.ShapeDtypeStruct((1, HEAD_DIM), F32)],
        compiler_params=_params(("arbitrary",)),
    )(o, gate, w, dy)


def _split_bf16(x):
    hi = x.astype(BF16)
    lo = (x - hi.astype(F32)).astype(BF16)
    return hi, lo


SB_Q_BLOCK = 512
SB_K_BLOCK = 256
SB_NEGLIGIBLE = -60.0


def _sb_logits(q, kb, mask, scale):
    z = _dot_nt(q, kb) * scale
    ls = jnp.minimum(z, 0.0) - jnp.log(1.0 + jnp.exp(-jnp.abs(z)))
    lk = ls - z
    if mask is not None:
        lk = jnp.where(mask, lk, 0.0)
    return ls, lk


def _sb_blocks(t):
    bq = min(SB_Q_BLOCK, t)
    bk = min(SB_K_BLOCK, bq)
    return bq, bk, bq // bk


def _sb_fwd(qkv):
    t = qkv.shape[0]
    bq, bk, nd = _sb_blocks(t)
    scale = HEAD_DIM ** -0.5

    def body(q_ref, k_ref, v_ref, o_ref, tot_ref, used_ref):
        i = pl.program_id(1)
        q = q_ref[...]
        rj = lax.broadcasted_iota(jnp.int32, (bk, bk), 0)
        cj = lax.broadcasted_iota(jnp.int32, (bk, bk), 1)
        after = (rj > cj).astype(BF16)
        trow = lax.broadcasted_iota(jnp.int32, (bq, bk), 0)
        scol = lax.broadcasted_iota(jnp.int32, (bq, bk), 1)

        def tile(j, run, acc, mask):
            off = pl.multiple_of(j * bk, bk)
            kb = k_ref[pl.ds(off, bk), :]
            vb = v_ref[pl.ds(off, bk), :]
            ls, lk = _sb_logits(q, kb, mask, scale)
            hi, lo = _split_bf16(lk)
            between = _dot(hi, after) + _dot(lo, after) + run
            a = jnp.exp(ls + between)
            if mask is not None:
                a = jnp.where(mask, a, 0.0)
            acc = acc + _dot(a.astype(BF16), vb)
            return run + jnp.sum(lk, axis=1, keepdims=True), acc

        run, acc = jnp.zeros((bq, 1), F32), jnp.zeros((bq, HEAD_DIM), F32)
        for d in reversed(range(nd)):
            run, acc = tile(i * nd + d, run, acc, scol + d * bk < trow)
        def more(c):
            return jnp.logical_and(c[0] < i * nd, jnp.max(c[1]) > SB_NEGLIGIBLE)

        def far(c):
            run_, acc_ = tile(i * nd - 1 - c[0], c[1], c[2], None)
            return c[0] + 1, run_, acc_

        used, run, acc = lax.while_loop(more, far, (jnp.int32(0), run, acc))
        o_ref[...] = acc.astype(BF16)
        tot_ref[...] = jnp.broadcast_to(run, (bq, HEAD_DIM))
        used_ref[...] = jnp.full(used_ref.shape, used, F32)

    qs = pl.BlockSpec((bq, HEAD_DIM), lambda h, i: (i, h))
    ks = pl.BlockSpec((t, HEAD_DIM), lambda h, i: (0, N_HEADS + h))
    vs = pl.BlockSpec((t, HEAD_DIM), lambda h, i: (0, 2 * N_HEADS + h))
    return pl.pallas_call(
        body, name="sb_fwd", grid=(N_HEADS, t // bq),
        in_specs=[qs, ks, vs], out_specs=[qs, qs, pl.BlockSpec((1, 1, 1, LANES), lambda h, i: (h, i, 0, 0))],
        out_shape=[jax.ShapeDtypeStruct((t, D_MODEL), BF16), jax.ShapeDtypeStruct((t, D_MODEL), F32),
                   jax.ShapeDtypeStruct((N_HEADS, t // bq, 1, LANES), F32)],
        compiler_params=_params(("parallel", "arbitrary")),
    )(qkv, qkv, qkv)


def _sb_bwd(qkv, tot, used, do, partials=()):
    t = qkv.shape[0]
    bq, bk, nd = _sb_blocks(t)
    scale = HEAD_DIM ** -0.5
    n = len(partials)
    nq = t // bq

    def body(q_ref, k_ref, v_ref, tot_ref, used_ref, do_ref, *rest):
        dq_ref, dk_ref, dv_ref = rest[n:n + 3]
        i = pl.program_id(1)
        if n:
            begin, end = _chip_exchange_protocol(rest[:n], rest[n + 3:2 * n + 3], *rest[2 * n + 3:])
            step = pl.program_id(0) * nq + i
            pl.when(step == 0)(begin)

        @pl.when(i == 0)
        def _():
            dk_ref[...] = jnp.zeros_like(dk_ref)
            dv_ref[...] = jnp.zeros_like(dv_ref)

        q = q_ref[...]
        do = do_ref[...]
        total = tot_ref[:, 0:1]
        rj = lax.broadcasted_iota(jnp.int32, (bk, bk), 0)
        cj = lax.broadcasted_iota(jnp.int32, (bk, bk), 1)
        upto = (rj <= cj).astype(BF16)
        before = (rj < cj).astype(BF16)
        trow = lax.broadcasted_iota(jnp.int32, (bq, bk), 0)
        scol = lax.broadcasted_iota(jnp.int32, (bq, bk), 1)

        def tile(j, run_k, run_e, dq, mask):
            off = pl.multiple_of(j * bk, bk)
            kb = k_ref[pl.ds(off, bk), :]
            vb = v_ref[pl.ds(off, bk), :]
            ls, lk = _sb_logits(q, kb, mask, scale)
            hi, lo = _split_bf16(lk)
            between = total - (_dot(hi, upto) + _dot(lo, upto) + run_k)
            a = jnp.exp(ls + between)
            if mask is not None:
                a = jnp.where(mask, a, 0.0)
            e = a * _dot_nt(do, vb)
            ehi, elo = _split_bf16(e)
            pre = _dot(ehi, before) + _dot(elo, before) + run_e
            sig = jnp.exp(ls)
            dz = e * (1.0 - sig) - pre * sig
            if mask is not None:
                dz = jnp.where(mask, dz, 0.0)
            dz = (dz * scale).astype(BF16)
            dq = dq + _dot(dz, kb)
            dk_ref[pl.ds(off, bk), :] += _dot_tn(dz, q)
            dv_ref[pl.ds(off, bk), :] += _dot_tn(a.astype(BF16), do)
            return (run_k + jnp.sum(lk, axis=1, keepdims=True),
                    run_e + jnp.sum(e, axis=1, keepdims=True), dq)

        zero = jnp.zeros((bq, 1), F32)
        visited = jnp.clip(jnp.max(used_ref[...]).astype(jnp.int32), 0, i * nd)
        carry = lax.fori_loop(i * nd - visited, i * nd, lambda j, c: tile(j, c[0], c[1], c[2], None),
                              (zero, zero, jnp.zeros((bq, HEAD_DIM), F32)))
        for d in range(nd):
            carry = tile(i * nd + d, *carry, scol + d * bk < trow)
        dq_ref[...] = carry[2]
        if n:
            pl.when(step == N_HEADS * nq - 1)(end)

    qs = pl.BlockSpec((bq, HEAD_DIM), lambda h, i: (i, h))
    ks = pl.BlockSpec((t, HEAD_DIM), lambda h, i: (0, N_HEADS + h))
    vs = pl.BlockSpec((t, HEAD_DIM), lambda h, i: (0, 2 * N_HEADS + h))
    full = pl.BlockSpec((t, HEAD_DIM), lambda h, i: (0, h))
    big = jax.ShapeDtypeStruct((t, D_MODEL), F32)
    return pl.pallas_call(
        body, name="sb_bwd", grid=(N_HEADS, nq),
        in_specs=[qs, ks, vs, qs, pl.BlockSpec((1, 1, 1, LANES), lambda h, i: (h, i, 0, 0)), qs] + _hbm_specs(n),
        out_specs=[qs, full, full] + _hbm_specs(n),
        out_shape=[big, big, big] + _chip_exchange_shapes(partials),
        scratch_shapes=_chip_exchange_sems(n) if n else [],
        compiler_params=_params(("arbitrary", "arbitrary")),
    )(qkv, qkv, qkv, tot, used, do, *partials)


def _merge_fwd(o_dn, o_sb, gl, x, wp_dn, wp_sb, w_out, w2):
    t = x.shape[0]
    tb = _rows(t)

    def body(odn_ref, osb_ref, gl_ref, x_ref, wpd_ref, wps_ref, wo_ref, w2_ref,
             pdn_ref, psb_ref, mix_ref, x1_ref, n2_ref):
        pdn = _dot(odn_ref[...], wpd_ref[...])
        psb = _dot(osb_ref[...], wps_ref[...])
        gates = jax.nn.sigmoid(gl_ref[...])
        mixed = (gates[:, :D_MODEL] * pdn + gates[:, D_MODEL:] * psb).astype(BF16)
        x1 = x_ref[...] + _dot(mixed, wo_ref[...])
        pdn_ref[...] = pdn.astype(BF16)
        psb_ref[...] = psb.astype(BF16)
        mix_ref[...] = mixed
        x1_ref[...] = x1
        n2_ref[...] = _rms(x1, w2_ref[...]).astype(BF16)

    row = pl.BlockSpec((tb, D_MODEL), lambda i: (i, 0))
    sq = pl.BlockSpec((D_MODEL, D_MODEL), lambda i: (0, 0))
    f = jax.ShapeDtypeStruct((t, D_MODEL), F32)
    b = jax.ShapeDtypeStruct((t, D_MODEL), BF16)
    return pl.pallas_call(
        body, name="merge_fwd", grid=(t // tb,),
        in_specs=[row, row, pl.BlockSpec((tb, 2 * D_MODEL), lambda i: (i, 0)), row, sq, sq, sq,
                  pl.BlockSpec((1, D_MODEL), lambda i: (0, 0))],
        out_specs=[row] * 5, out_shape=[b, b, b, f, b],
        compiler_params=_params(("parallel",)),
    )(o_dn, o_sb, gl, x, wp_dn, wp_sb, w_out, w2)


def _merge_bwd(dx2, dn2, x1, w2, gl, pdn, psb, wp_dn, wp_sb, w_out):
    t = x1.shape[0]
    tb = _rows(t)

    def body(dx2_ref, dn2_ref, x1_ref, w2_ref, gl_ref, pdn_ref, psb_ref, wpd_ref, wps_ref, wo_ref,
             dx1_ref, dw2_ref, dgl_ref, dpdn_ref, dpsb_ref, dodn_ref, dosb_ref):
        i = pl.program_id(0)
        _, vjp = jax.vjp(_rms, x1_ref[...], w2_ref[...])
        dxn, dw2 = vjp(dn2_ref[...])
        dx1 = dx2_ref[...] + dxn
        dx1_ref[...] = dx1

        @pl.when(i == 0)
        def _():
            dw2_ref[...] = jnp.zeros_like(dw2_ref)

        dw2_ref[...] += dw2
        dmix = _dot_nt(dx1.astype(BF16), wo_ref[...])
        gates = jax.nn.sigmoid(gl_ref[...])
        g_dn, g_sb = gates[:, :D_MODEL], gates[:, D_MODEL:]
        dpdn = (dmix * g_dn).astype(BF16)
        dpsb = (dmix * g_sb).astype(BF16)
        dgl_ref[:, :D_MODEL] = (dmix * pdn_ref[...].astype(F32) * g_dn * (1.0 - g_dn)).astype(BF16)
        dgl_ref[:, D_MODEL:] = (dmix * psb_ref[...].astype(F32) * g_sb * (1.0 - g_sb)).astype(BF16)
        dpdn_ref[...] = dpdn
        dpsb_ref[...] = dpsb
        dodn_ref[...] = _dot_nt(dpdn, wpd_ref[...])
        dosb_ref[...] = _dot_nt(dpsb, wps_ref[...]).astype(BF16)

    row = pl.BlockSpec((tb, D_MODEL), lambda i: (i, 0))
    wide = pl.BlockSpec((tb, 2 * D_MODEL), lambda i: (i, 0))
    sq = pl.BlockSpec((D_MODEL, D_MODEL), lambda i: (0, 0))
    vec = pl.BlockSpec((1, D_MODEL), lambda i: (0, 0))
    f = jax.ShapeDtypeStruct((t, D_MODEL), F32)
    b = jax.ShapeDtypeStruct((t, D_MODEL), BF16)
    return pl.pallas_call(
        body, name="merge_bwd", grid=(t // tb,),
        in_specs=[row, row, row, vec, wide, row, row, sq, sq, sq],
        out_specs=[row, vec, wide, row, row, row, row],
        out_shape=[f, jax.ShapeDtypeStruct((1, D_MODEL), F32), jax.ShapeDtypeStruct((t, 2 * D_MODEL), BF16),
                   b, b, f, b],
        compiler_params=_params(("arbitrary",)),
    )(dx2, dn2, x1, w2, gl, pdn, psb, wp_dn, wp_sb, w_out)


def _conv_taps(buf, w_ref, first, rows, cols=slice(None)):
    y = w_ref[0:1, cols] * buf[pl.ds(first, rows), cols]
    for s in range(1, w_ref.shape[0]):
        y = y + w_ref[s:s + 1, cols] * buf[pl.ds(first + s, rows), cols]
    return y


def _ffn_mid_fwd(pre_g, pre_u, wg, wu):
    t, c = pre_g.shape
    kk = wg.shape[0]
    tb, cb = _rows(t), _pick(c, ELEMENTWISE_COLS)
    per = tb // HALO

    def body(g_ref, gh_ref, u_ref, uh_ref, wg_ref, wu_ref, a_ref, gbuf, ubuf):
        i = pl.program_id(0)
        for buf, ref, halo in ((gbuf, g_ref, gh_ref), (ubuf, u_ref, uh_ref)):
            buf[pl.ds(HALO, tb), :] = ref[...]
            buf[pl.ds(0, HALO), :] = jnp.where(i == 0, 0.0, halo[...])
        for j in range(cb // LANES):
            sl = pl.ds(j * LANES, LANES)
            ug = _conv_taps(gbuf, wg_ref, HALO - (kk - 1), tb, sl)
            uu = _conv_taps(ubuf, wu_ref, HALO - (kk - 1), tb, sl)
            a_ref[:, sl] = (_silu(ug) * uu).astype(BF16)

    blk = pl.BlockSpec((tb, cb), lambda i, j: (i, j))
    halo = pl.BlockSpec((HALO, cb), lambda i, j: (jnp.maximum(i * per - 1, 0), j))
    wspec = pl.BlockSpec((kk, cb), lambda i, j: (0, j))
    return pl.pallas_call(
        body, name="ffn_mid_fwd", grid=(t // tb, c // cb),
        in_specs=[blk, halo, blk, halo, wspec, wspec], out_specs=blk,
        out_shape=jax.ShapeDtypeStruct((t, c), BF16),
        scratch_shapes=[pltpu.VMEM((tb + HALO, cb), F32)] * 2,
        compiler_params=_params(("parallel", "parallel")),
    )(pre_g, pre_g, pre_u, pre_u, wg, wu)


def _ffn_mid_bwd(pre_g, pre_u, wg, wu, da):
    t, c = pre_g.shape
    kk = wg.shape[0]
    tb, cb = _rows(t), _pick(c, ELEMENTWISE_COLS)
    per = tb // HALO
    nblk = t // tb
    ext = tb + HALO

    def body(g_ref, gb_ref, ga_ref, u_ref, ub_ref, ua_ref, da_ref, daa_ref, wg_ref, wu_ref,
             dg_ref, du_ref, dwg_ref, dwu_ref, gbuf, ubuf, dabuf, dgbuf, dubuf):
        i = pl.program_id(1)
        last = i == nblk - 1
        for buf, ref, before, after in ((gbuf, g_ref, gb_ref, ga_ref), (ubuf, u_ref, ub_ref, ua_ref)):
            buf[pl.ds(0, HALO), :] = jnp.where(i == 0, 0.0, before[...])
            buf[pl.ds(HALO, tb), :] = ref[...]
            buf[pl.ds(HALO + tb, HALO), :] = jnp.where(last, 0.0, after[...])
        dabuf[pl.ds(0, tb), :] = da_ref[...]
        dabuf[pl.ds(tb, HALO), :] = jnp.where(last, 0.0, daa_ref[...])

        @pl.when(i == 0)
        def _():
            dwg_ref[...] = jnp.zeros_like(dwg_ref)
            dwu_ref[...] = jnp.zeros_like(dwu_ref)

        for j in range(cb // LANES):
            sl = pl.ds(j * LANES, LANES)
            ug = _conv_taps(gbuf, wg_ref, HALO - (kk - 1), ext, sl)
            uu = _conv_taps(ubuf, wu_ref, HALO - (kk - 1), ext, sl)
            _, vjp = jax.vjp(lambda g, u: _silu(g) * u, ug, uu)
            dgbuf[:, sl], dubuf[:, sl] = vjp(dabuf[:, sl])
            for dbuf, xbuf, w_ref, dx_ref, dw_ref in ((dgbuf, gbuf, wg_ref, dg_ref, dwg_ref),
                                                      (dubuf, ubuf, wu_ref, du_ref, dwu_ref)):
                x = xbuf[pl.ds(HALO, tb), sl]
                dx = None
                for s in range(kk):
                    shifted = dbuf[pl.ds(kk - 1 - s, tb), sl]
                    term = w_ref[s:s + 1, sl] * shifted
                    dx = term if dx is None else dx + term
                    dw_ref[s:s + 1, sl] += jnp.sum(shifted * x, axis=0, keepdims=True)
                dx_ref[:, sl] = dx.astype(BF16)

    blk = pl.BlockSpec((tb, cb), lambda j, i: (i, j))
    before = pl.BlockSpec((HALO, cb), lambda j, i: (jnp.maximum(i * per - 1, 0), j))
    after = pl.BlockSpec((HALO, cb), lambda j, i: (jnp.minimum((i + 1) * per, t // HALO - 1), j))
    wspec = pl.BlockSpec((kk, cb), lambda j, i: (0, j))
    dwspec = pl.BlockSpec((HALO, cb), lambda j, i: (0, j))
    half = jax.ShapeDtypeStruct((t, c), BF16)
    dwshape = jax.ShapeDtypeStruct((HALO, c), F32)
    return pl.pallas_call(
        body, name="ffn_mid_bwd", grid=(c // cb, nblk),
        in_specs=[blk, before, after, blk, before, after, blk, after, wspec, wspec],
        out_specs=[blk, blk, dwspec, dwspec],
        out_shape=[half, half, dwshape, dwshape],
        scratch_shapes=[pltpu.VMEM((ext + HALO, cb), F32)] * 2 + [pltpu.VMEM((ext, cb), F32)] * 3,
        compiler_params=_params(("parallel", "arbitrary")),
    )(pre_g, pre_g, pre_g, pre_u, pre_u, pre_u, da, da, wg, wu)


def _down_loss(a, w_down, x1, wf, target):
    t = x1.shape[0]
    tb = _rows(t)

    def body(a_ref, wd_ref, x1_ref, wf_ref, tgt_ref, dx2_ref, dwf_ref, loss_ref):
        i = pl.program_id(0)
        x2 = x1_ref[...] + _dot(a_ref[...], wd_ref[...])
        y, vjp = jax.vjp(_rms, x2, wf_ref[...])
        err = y - tgt_ref[...]
        dx2, dwf = vjp(err * (1.0 / D_MODEL))
        dx2_ref[...] = dx2
        part = jnp.sum(jnp.sum(err * err, axis=1, keepdims=True), axis=0, keepdims=True) * (0.5 / D_MODEL)

        @pl.when(i == 0)
        def _():
            dwf_ref[...] = jnp.zeros_like(dwf_ref)
            loss_ref[...] = jnp.zeros_like(loss_ref)

        dwf_ref[...] += dwf
        loss_ref[...] += jnp.broadcast_to(part, loss_ref.shape)

    row = pl.BlockSpec((tb, D_MODEL), lambda i: (i, 0))
    vec = pl.BlockSpec((1, D_MODEL), lambda i: (0, 0))
    return pl.pallas_call(
        body, name="down_loss", grid=(t // tb,),
        in_specs=[pl.BlockSpec((tb, D_FF), lambda i: (i, 0)), pl.BlockSpec((D_FF, D_MODEL), lambda i: (0, 0)),
                  row, vec, row],
        out_specs=[row, vec, pl.BlockSpec((1, LANES), lambda i: (0, 0))],
        out_shape=[jax.ShapeDtypeStruct((t, D_MODEL), F32), jax.ShapeDtypeStruct((1, D_MODEL), F32),
                   jax.ShapeDtypeStruct((1, LANES), F32)],
        compiler_params=_params(("arbitrary",)),
    )(a, w_down, x1, wf, target)


def _local_step(x, target, wts, late_wire=(), late_weights=None, early_partials=None):
    t = x.shape[0]
    nchunk = t // DN_CHUNK

    n1, hab = _norm1_fwd(x, wts["norm1"], wts["w_ab"])
    dnqkv = _mm(n1, wts["w_dnqkv"], name="h_dnqkv")
    dngate = _mm(n1, wts["w_dngate"], name="h_dngate")
    sbqkv = _mm(n1, wts["w_sbqkv"], out_dtype=BF16, name="h_sbqkv")
    gl = _mm(n1, wts["w_gl"], name="h_gl")

    cdn = _conv_fwd(dnqkv, wts["dn_conv"], "dn_conv_fwd")
    qn, kn, vv, gb = _dn_prep_fwd(cdn, hab, wts["alog"], wts["dtb"])
    per_head = gb[:, :2 * N_HEADS].T.reshape(2 * N_HEADS, nchunk, DN_CHUNK)
    grow, brow = per_head[:N_HEADS, :, None, :], per_head[N_HEADS:, :, None, :]
    u_dn, w_dn, a_qk, qe, kdec, egl, tinv, *late = _dn_local_fwd(qn, kn, vv, grow, brow, late_wire)
    if late_wire:
        wts = {**wts, **late_weights(late)}
    o_raw, states = _dn_seq_fwd(u_dn, w_dn, a_qk, qe, kdec, egl)
    o_dn = _dn_post_fwd(o_raw, dngate, wts["dn_norm"])

    o_sb, tot, sb_used = _sb_fwd(sbqkv)

    pdn, psb, mixed, x1, n2 = _merge_fwd(o_dn, o_sb, gl, x, wts["wp_dn"], wts["wp_sb"], wts["w_out"],
                                         wts["norm2"])
    pre_g = _mm(n2, wts["w_up_g"], name="ffn_up_g")
    pre_u = _mm(n2, wts["w_up_u"], name="ffn_up_u")
    act = _ffn_mid_fwd(pre_g, pre_u, wts["ffn_conv_g"], wts["ffn_conv_u"])
    dx2, d_normf, loss_part = _down_loss(act, wts["w_down"], x1, wts["normf"], target)

    grads = {"normf": d_normf}
    da = _mm(dx2, wts["w_down"], tb=True, name="d_act")
    grads["w_down"] = _mm(act, dx2, ta=True, out_dtype=BF16, name="dw_down")
    dpre_g, dpre_u, dcw_g, dcw_u = _ffn_mid_bwd(pre_g, pre_u, wts["ffn_conv_g"], wts["ffn_conv_u"], da)
    grads["ffn_conv"] = jnp.concatenate([dcw_g[:FFN_CONV], dcw_u[:FFN_CONV]], axis=1)
    dn2 = _mm(dpre_g, wts["w_up_g"], tb=True, name="dn2_g")
    dn2 = _mm(dpre_u, wts["w_up_u"], tb=True, add=dn2, name="dn2_u")
    grads["w_up_g"] = _mm(n2, dpre_g, ta=True, out_dtype=BF16, name="dw_up_g")
    grads["w_up_u"] = _mm(n2, dpre_u, ta=True, out_dtype=BF16, name="dw_up_u")

    dx1, grads["norm2"], dgl, dpdn, dpsb, do_dn, do_sb = _merge_bwd(
        dx2, dn2, x1, wts["norm2"], gl, pdn, psb, wts["wp_dn"], wts["wp_sb"], wts["w_out"])
    grads["w_out"] = _mm(mixed, dx1, ta=True, out_dtype=BF16, name="dw_out")
    grads["wp_dn"] = _mm(o_dn, dpdn, ta=True, out_dtype=BF16, name="dw_proj_dn")
    grads["wp_sb"] = _mm(o_sb, dpsb, ta=True, out_dtype=BF16, name="dw_proj_sb")

    partials = early_partials(grads) if early_partials else ()
    dsq, dsk, dsv, *arrived = _sb_bwd(sbqkv, tot, sb_used, do_sb, partials)
    dsbqkv = jnp.concatenate([dsq, dsk, dsv], axis=1).astype(BF16)

    do_raw, ddngate, grads["dn_norm"] = _dn_post_bwd(o_raw, dngate, wts["dn_norm"], do_dn)
    seq_grads = _dn_seq_bwd(u_dn, w_dn, a_qk, qe, kdec, egl, states, do_raw)
    dqn, dkn, dvv, dgrow, dbrow = _dn_local_bwd(qn, kn, vv, grow, brow, tinv, *seq_grads)
    dgb = jnp.concatenate([dgrow.reshape(N_HEADS, t), dbrow.reshape(N_HEADS, t)], axis=0).T
    dgb = jnp.pad(dgb, ((0, 0), (0, LANES - 2 * N_HEADS)))
    dcdn, dhab, grads["alog"], grads["dtb"] = _dn_prep_bwd(cdn, hab, wts["alog"], wts["dtb"], dqn, dkn, dvv, dgb)
    ddnqkv, dcw_dn = _conv_bwd(dcdn, dnqkv, wts["dn_conv"], "dn_conv_bwd", BF16)
    grads["dn_conv"] = dcw_dn[:DN_CONV]

    dn1 = _mm(ddnqkv, wts["w_dnqkv"], tb=True, name="dn1_dnqkv")
    dn1 = _mm(ddngate, wts["w_dngate"], tb=True, add=dn1, name="dn1_dngate")
    dn1 = _mm(dsbqkv, wts["w_sbqkv"], tb=True, add=dn1, name="dn1_sbqkv")
    dn1 = _mm(dgl, wts["w_gl"], tb=True, add=dn1, name="dn1_gl")
    grads["w_dnqkv"] = _mm(n1, ddnqkv, ta=True, out_dtype=BF16, name="dw_dnqkv")
    grads["w_dngate"] = _mm(n1, ddngate, ta=True, out_dtype=BF16, name="dw_dngate")
    grads["w_sbqkv"] = _mm(n1, dsbqkv, ta=True, out_dtype=BF16, name="dw_sbqkv")
    grads["w_gl"] = _mm(n1, dgl, ta=True, out_dtype=BF16, name="dw_gl")
    grads["w_ab"] = _mm(n1, dhab, ta=True, out_dtype=BF16, name="dw_ab")
    grad_x, grads["norm1"] = _norm1_bwd(x, wts["norm1"], dn1, dx1, dhab, wts["w_ab"])
    return loss_part, grad_x, grads, (list(partials), arrived)


def _place():
    return lax.axis_index("x"), lax.axis_index("y"), lax.axis_index("c")


def _hbm_specs(n):
    return [pl.BlockSpec(memory_space=pltpu.HBM)] * n


GATHER_SEMS = 8
GATHER_FORWARD_AT = 5


def _gather_protocol(ins, outs, send_sems, recv_sems):
    n = len(ins)
    x, y, c = _place()
    me = 2 * x + y
    sibling = (x, y, 1 - c)
    xn, yn, dg = (1 - x, y), (x, 1 - y), (1 - x, 1 - y)
    idx = lambda chip: 2 * chip[0] + chip[1]

    def part(a, chip_index, core, quarter=None):
        half = ins[a].shape[0] // 2
        if quarter is None:
            return outs[a].at[chip_index, pl.ds(core * half, half), :]
        return outs[a].at[chip_index, pl.ds(core * half + quarter * (half // 2), half // 2), :]

    def copy(a, k, src, dst, to):
        return pltpu.make_async_remote_copy(src_ref=src, dst_ref=dst, send_sem=send_sems.at[GATHER_SEMS * a + k],
                                            recv_sem=recv_sems.at[GATHER_SEMS * a + k], device_id=to,
                                            device_id_type=MESH)

    def sent(a, k):
        half = ins[a].shape[0] // 2
        my_half = ins[a].at[pl.ds(c * half, half), :]
        if k < 2:
            return copy(a, k, my_half, part(a, me, c), (*(xn, yn)[k], c))
        if k < 4:
            src = part(a, idx((xn, yn)[k - 2]), c, k - 2)
            return copy(a, k, src, src, (*(yn, xn)[k - 2], c))
        src = (part(a, idx(xn), c), part(a, idx(yn), c), part(a, idx(dg), c, 0), part(a, idx(dg), c, 1))[k - 4]
        return copy(a, k, src, src, sibling)

    def landed(a, k):
        dst = (part(a, idx(xn), c), part(a, idx(yn), c), part(a, idx(dg), c, 0), part(a, idx(dg), c, 1),
               part(a, idx(xn), 1 - c), part(a, idx(yn), 1 - c), part(a, idx(dg), 1 - c, 0),
               part(a, idx(dg), 1 - c, 1))[k]
        return copy(a, k, dst, dst, sibling)

    def begin():
        for a in range(n):
            sent(a, 0).start()
            sent(a, 1).start()

    def middle():
        for a in range(n):
            for k in range(2):
                landed(a, k).wait_recv()
                sent(a, 2 + k).start()
                sent(a, 4 + k).start()

    def end():
        for a in range(n):
            for k in (2, 3):
                landed(a, k).wait_recv()
                sent(a, 4 + k).start()
        for a in range(n):
            for k in range(4, GATHER_SEMS):
                landed(a, k).wait_recv()
        for a in range(n):
            for k in range(GATHER_SEMS):
                sent(a, k).wait_send()

    return begin, middle, end


def _gather_out_shapes(shards):
    return [jax.ShapeDtypeStruct((N_CHIPS,) + s.shape, s.dtype) for s in shards]


def _gather_sems(n):
    return [pltpu.SemaphoreType.DMA((GATHER_SEMS * n,)), pltpu.SemaphoreType.DMA((GATHER_SEMS * n,))]


def _gather_shards(shards):
    n = len(shards)

    def body(*refs):
        begin, middle, end = _gather_protocol(refs[:n], refs[n:2 * n], *refs[2 * n:])
        begin()
        middle()
        end()

    return pl.pallas_call(
        body, name="gather_weights", in_specs=_hbm_specs(n), out_specs=_hbm_specs(n),
        out_shape=_gather_out_shapes(shards), scratch_shapes=_gather_sems(n),
    )(*shards)


def _pair_exchange_halves(gs, tag):
    n = len(gs)

    def body(*refs):
        ins, outs, (send_sems, recv_sems) = refs[:n], refs[n:2 * n], refs[2 * n:]
        x, y, c = _place()
        cps = []
        for a in range(n):
            half = ins[a].shape[1] // 2
            cp = pltpu.make_async_remote_copy(src_ref=ins[a].at[:, pl.ds((1 - c) * half, half), :], dst_ref=outs[a],
                                              send_sem=send_sems.at[a], recv_sem=recv_sems.at[a],
                                              device_id=(x, y, 1 - c), device_id_type=MESH)
            cp.start()
            cps.append(cp)
        for cp in cps:
            cp.wait()

    return pl.pallas_call(
        body, name="grad_pair_exchange_" + tag, in_specs=_hbm_specs(n), out_specs=_hbm_specs(n),
        out_shape=[jax.ShapeDtypeStruct((g.shape[0], g.shape[1] // 2, g.shape[2]), g.dtype) for g in gs],
        scratch_shapes=[pltpu.SemaphoreType.DMA((n,)), pltpu.SemaphoreType.DMA((n,))],
    )(*gs)


def _pick_rows(n, target=1024):
    best = 16
    for b in range(16, min(n, target) + 1, 16):
        if n % b == 0:
            best = b
    return best


def _pair_add(g, got, c_idx, tag):
    nsh, rows, cols = g.shape
    half = rows // 2
    rb = _pick_rows(half)

    def body(c_ref, g_ref, got_ref, o_ref):
        o_ref[...] = (g_ref[...].astype(F32) + got_ref[...].astype(F32)).astype(BF16)

    nb = half // rb
    grid_spec = pltpu.PrefetchScalarGridSpec(
        num_scalar_prefetch=1, grid=(nsh, nb),
        in_specs=[pl.BlockSpec((1, rb, cols), lambda s, i, c_ref: (s, c_ref[0] * nb + i, 0)),
                  pl.BlockSpec((1, rb, cols), lambda s, i, c_ref: (s, i, 0))],
        out_specs=pl.BlockSpec((1, rb, cols), lambda s, i, c_ref: (s, i, 0)))
    return pl.pallas_call(
        body, name="grad_pair_add_" + tag, grid_spec=grid_spec,
        out_shape=jax.ShapeDtypeStruct((nsh, half, cols), BF16),
        compiler_params=_params(("parallel", "parallel")),
    )(c_idx, g, got)


def _chip_exchange_protocol(ins, outs, send_sems, recv_sems):
    x, y, c = _place()
    chips = [(1 - x, y), (x, 1 - y), (1 - x, 1 - y)]

    def copies():
        return [pltpu.make_async_remote_copy(src_ref=ins[a].at[2 * px + py], dst_ref=outs[a].at[j],
                                             send_sem=send_sems.at[3 * a + j], recv_sem=recv_sems.at[3 * a + j],
                                             device_id=(px, py, c), device_id_type=MESH)
                for a in range(len(ins)) for j, (px, py) in enumerate(chips)]

    def begin():
        for cp in copies():
            cp.start()

    def end():
        for cp in copies():
            cp.wait_recv()
        for cp in copies():
            cp.wait_send()

    return begin, end


def _chip_exchange_shapes(ps):
    return [jax.ShapeDtypeStruct((N_CHIPS - 1,) + p.shape[1:], p.dtype) for p in ps]


def _chip_exchange_sems(n):
    return [pltpu.SemaphoreType.DMA((3 * n,)), pltpu.SemaphoreType.DMA((3 * n,))]


def _chip_exchange(ps):
    n = len(ps)

    def body(*refs):
        begin, end = _chip_exchange_protocol(refs[:n], refs[n:2 * n], *refs[2 * n:])
        begin()
        end()

    return pl.pallas_call(
        body, name="grad_chip_exchange", in_specs=_hbm_specs(n), out_specs=_hbm_specs(n),
        out_shape=_chip_exchange_shapes(ps), scratch_shapes=_chip_exchange_sems(n),
    )(*ps)


def _sum_partials(p, got, chip_idx, tag):
    nsh, half, cols = got.shape
    rb = _pick_rows(half)

    def body(me_ref, p_ref, got_ref, o_ref):
        acc = p_ref[0].astype(F32)
        for s in range(nsh):
            acc = acc + got_ref[s].astype(F32)
        o_ref[...] = acc

    grid_spec = pltpu.PrefetchScalarGridSpec(
        num_scalar_prefetch=1, grid=(half // rb,),
        in_specs=[pl.BlockSpec((1, rb, cols), lambda i, me_ref: (me_ref[0], i, 0)),
                  pl.BlockSpec((nsh, rb, cols), lambda i, me_ref: (0, i, 0))],
        out_specs=pl.BlockSpec((rb, cols), lambda i, me_ref: (i, 0)))
    return pl.pallas_call(
        body, name="grad_sum_chips_" + tag, grid_spec=grid_spec,
        out_shape=jax.ShapeDtypeStruct((half, cols), F32),
        compiler_params=_params(("parallel",)),
    )(chip_idx, p, got)


def _pair_share(rs):
    n = len(rs)

    def body(*refs):
        ins, outs, (send_sems, recv_sems) = refs[:n], refs[n:2 * n], refs[2 * n:]
        x, y, c = _place()
        cps = []
        for a in range(n):
            cp = pltpu.make_async_remote_copy(src_ref=ins[a], dst_ref=outs[a], send_sem=send_sems.at[a],
                                              recv_sem=recv_sems.at[a], device_id=(x, y, 1 - c),
                                              device_id_type=MESH)
            cp.start()
            cps.append(cp)
        for cp in cps:
            cp.wait()

    return pl.pallas_call(
        body, name="grad_pair_share", in_specs=_hbm_specs(n), out_specs=_hbm_specs(n),
        out_shape=[jax.ShapeDtypeStruct(r.shape, r.dtype) for r in rs],
        scratch_shapes=[pltpu.SemaphoreType.DMA((n,)), pltpu.SemaphoreType.DMA((n,))],
    )(*rs)


def _small_allreduce(v):
    rows, cols = v.shape
    ndev = 8

    def body(in_ref, out_ref, slots, send_sems, recv_sems):
        x, y, c = _place()
        me = 4 * x + 2 * y + c
        slots[me] = in_ref[...]
        sends = []
        for k in range(1, ndev):
            peer = (x ^ (k >> 2), y ^ ((k >> 1) & 1), c ^ (k & 1))
            cp = pltpu.make_async_remote_copy(src_ref=in_ref, dst_ref=slots.at[me], send_sem=send_sems.at[k - 1],
                                              recv_sem=recv_sems.at[k - 1], device_id=peer, device_id_type=MESH)
            cp.start()
            sends.append(cp)
        for k in range(1, ndev):
            there = slots.at[me ^ k]
            pltpu.make_async_remote_copy(src_ref=there, dst_ref=there, send_sem=send_sems.at[k - 1],
                                         recv_sem=recv_sems.at[k - 1], device_id=(x, y, c),
                                         device_id_type=MESH).wait_recv()
        for cp in sends:
            cp.wait_send()
        acc = slots[0]
        for s in range(1, ndev):
            acc = acc + slots[s]
        out_ref[...] = acc

    return pl.pallas_call(
        body, name="small_allreduce",
        in_specs=[pl.BlockSpec(memory_space=pltpu.VMEM)],
        out_specs=pl.BlockSpec(memory_space=pltpu.VMEM),
        out_shape=jax.ShapeDtypeStruct((rows, cols), F32),
        scratch_shapes=[pltpu.VMEM((ndev, rows, cols), F32), pltpu.SemaphoreType.DMA((ndev - 1,)),
                        pltpu.SemaphoreType.DMA((ndev - 1,))],
    )(v)


def _adamw(w, g, m, v, name):
    r, c = w.shape
    rb = r if r <= 128 else _pick_rows_8(r, 128)
    c1 = 1.0 - ADAM_B1 ** ADAM_STEP
    c2 = 1.0 - ADAM_B2 ** ADAM_STEP

    def body(w_ref, g_ref, m_ref, v_ref, d_ref, nm_ref, nv_ref):
        gg = g_ref[...]
        nm = ADAM_B1 * m_ref[...] + (1.0 - ADAM_B1) * gg
        nv = ADAM_B2 * v_ref[...] + (1.0 - ADAM_B2) * (gg * gg)
        d_ref[...] = -ADAM_LR * ((nm / c1) / (jnp.sqrt(nv / c2) + ADAM_EPS) + ADAM_WD * w_ref[...])
        nm_ref[...] = nm
        nv_ref[...] = nv

    blk = pl.BlockSpec((rb, c), lambda i: (i, 0))
    shp = jax.ShapeDtypeStruct((r, c), F32)
    return pl.pallas_call(
        body, name=name, grid=(r // rb,), in_specs=[blk] * 4, out_specs=[blk] * 3, out_shape=[shp] * 3,
        compiler_params=_params(("parallel",)),
    )(w, g, m, v)


def _pick_rows_8(n, target):
    best = n
    for b in range(8, min(n, target) + 1, 8):
        if n % b == 0:
            best = b
    return best


W_IN_COLS = 2308
W_UP_COLS = 1408
W_DOWN_ROWS = 704
DN_CONV_COLS = 768
FFN_CONV_COLS = 1408
PROJ_ROWS = 256
ROW_TILE = 16
ROW_SEGS = [("wp_dn", PROJ_ROWS), ("wp_sb", PROJ_ROWS), ("w_out", PROJ_ROWS), ("w_down", W_DOWN_ROWS),
            ("dn_conv", ROW_TILE), ("ffn_conv", ROW_TILE), ("spare", 2 * ROW_TILE)]
ROW_OFFS = {nm: (sum(n for _, n in ROW_SEGS[:i]), n) for i, (nm, n) in enumerate(ROW_SEGS)}
STACK_ROWS = sum(n for _, n in ROW_SEGS)
assert all(n % ROW_TILE == 0 for _, n in ROW_SEGS) and STACK_ROWS % (4 * ROW_TILE) == 0
Q_END, A_END, G_END, S_END = 3 * D_MODEL, 3 * D_MODEL + 2 * N_HEADS, 4 * D_MODEL + 2 * N_HEADS, 7 * D_MODEL + 2 * N_HEADS


def _flat_rows(a, nrows):
    flat = a.reshape(-1)
    return jnp.pad(flat, (0, nrows * D_MODEL - flat.shape[0])).reshape(nrows, D_MODEL)


IN_EXTRA_ROWS = 64


def _weight_wire(w_in, wp_dn, wp_sb, w_out, w_up, w_down, dn_conv, ffn_conv):
    bits = lax.bitcast_convert_type(dn_conv, BF16).reshape(-1)
    extra = jnp.pad(bits, (0, IN_EXTRA_ROWS * W_IN_COLS - bits.shape[0])).reshape(IN_EXTRA_ROWS, W_IN_COLS)
    stack = jnp.concatenate([wp_dn.astype(BF16), wp_sb.astype(BF16), w_out.astype(BF16), w_down.astype(BF16),
                             jnp.zeros((ROW_TILE, D_MODEL), BF16),
                             _flat_rows(lax.bitcast_convert_type(ffn_conv, BF16), ROW_TILE),
                             jnp.zeros((ROW_OFFS["spare"][1], D_MODEL), BF16)], axis=0)
    return [jnp.concatenate([w_in.astype(BF16), extra], axis=0)], [w_up.astype(BF16), stack]


def _col_range(g, lo, hi, width):
    parts = []
    for s in range(g.shape[0]):
        a, b = max(lo, s * width), min(hi, (s + 1) * width)
        if a < b:
            parts.append(g[s][:, a - s * width:b - s * width])
    return parts[0] if len(parts) == 1 else jnp.concatenate(parts, axis=1)


def _f32_rows(raw, k, ncols):
    raw = raw.reshape(N_CHIPS, -1)[:, :2 * k * ncols].reshape(N_CHIPS, k * ncols, 2)
    vals = lax.bitcast_convert_type(raw, F32).reshape(N_CHIPS, k, ncols)
    return vals.transpose(1, 0, 2).reshape(k, N_CHIPS * ncols)


def _unpack_early(g_in):
    w = g_in[:, :D_MODEL, :]
    return {
        "w_dnqkv": _col_range(w, 0, Q_END, W_IN_COLS),
        "w_ab": jnp.pad(_col_range(w, Q_END, A_END, W_IN_COLS), ((0, 0), (0, LANES - 2 * N_HEADS))),
        "w_dngate": _col_range(w, A_END, G_END, W_IN_COLS),
        "w_sbqkv": _col_range(w, G_END, S_END, W_IN_COLS),
        "w_gl": _col_range(w, S_END, N_CHIPS * W_IN_COLS, W_IN_COLS),
        "dn_conv": _f32_rows(g_in[:, D_MODEL:, :], DN_CONV, DN_CONV_COLS),
    }


def _unpack_late(g_up, g_stack):
    def seg(nm):
        at, n = ROW_OFFS[nm]
        return g_stack[:, at:at + n, :]

    ffn_conv = _f32_rows(seg("ffn_conv"), FFN_CONV, FFN_CONV_COLS)
    return {
        "wp_dn": seg("wp_dn").reshape(D_MODEL, D_MODEL),
        "wp_sb": seg("wp_sb").reshape(D_MODEL, D_MODEL),
        "w_out": seg("w_out").reshape(D_MODEL, D_MODEL),
        "w_up_g": _col_range(g_up, 0, D_FF, W_UP_COLS), "w_up_u": _col_range(g_up, D_FF, 2 * D_FF, W_UP_COLS),
        "w_down": seg("w_down").reshape(D_FF, D_MODEL),
        "ffn_conv_g": ffn_conv[:, :D_FF], "ffn_conv_u": ffn_conv[:, D_FF:],
    }


def _grad_wire_early(gr):
    def cols(a, ncols):
        return a.reshape(a.shape[0], N_CHIPS, ncols).transpose(1, 0, 2)

    def rows(a, nrows):
        return a.astype(BF16).reshape(N_CHIPS, nrows, a.shape[1])

    def flat(a, nrows):
        a = a.astype(BF16).reshape(N_CHIPS, -1)
        return jnp.pad(a, ((0, 0), (0, nrows * D_MODEL - a.shape[1]))).reshape(N_CHIPS, nrows, D_MODEL)

    up = [gr["w_up_g"], gr["w_up_u"]]
    g_up = jnp.stack([up[s // 2][:, (s % 2) * W_UP_COLS:(s % 2 + 1) * W_UP_COLS].astype(BF16) for s in range(N_CHIPS)])
    g_stack = jnp.concatenate([rows(gr["wp_dn"], PROJ_ROWS), rows(gr["wp_sb"], PROJ_ROWS), rows(gr["w_out"], PROJ_ROWS),
                               rows(gr["w_down"], W_DOWN_ROWS), jnp.zeros((N_CHIPS, ROW_TILE, D_MODEL), BF16),
                               flat(cols(gr["ffn_conv"], FFN_CONV_COLS), ROW_TILE),
                               jnp.zeros((N_CHIPS, ROW_OFFS["spare"][1], D_MODEL), BF16)], axis=1)
    return [g_up, g_stack]


def _grad_wire_late(gr):
    pieces = [(gr["w_dnqkv"], 0), (gr["w_ab"][:, :2 * N_HEADS], Q_END), (gr["w_dngate"], A_END),
              (gr["w_sbqkv"], G_END), (gr["w_gl"], S_END)]
    conv = gr["dn_conv"].reshape(DN_CONV, N_CHIPS, DN_CONV_COLS).transpose(1, 0, 2).reshape(N_CHIPS, -1)

    def block(s):
        lo, hi = s * W_IN_COLS, (s + 1) * W_IN_COLS
        parts = []
        for a, at in pieces:
            b0, b1 = max(lo, at), min(hi, at + a.shape[1])
            if b0 < b1:
                parts.append(a[:, b0 - at:b1 - at].astype(BF16))
        w = parts[0] if len(parts) == 1 else jnp.concatenate(parts, axis=1)
        extra = jnp.pad(conv[s].astype(BF16), (0, IN_EXTRA_ROWS * W_IN_COLS - conv.shape[1]))
        return jnp.concatenate([w, extra.reshape(IN_EXTRA_ROWS, W_IN_COLS)], axis=0)

    return [jnp.stack([block(s) for s in range(N_CHIPS)])]


def _unpack_grad_shard(r_in, r_up, r_stack):
    def seg(nm):
        at, n = ROW_OFFS[nm]
        return r_stack[at:at + n, :]

    return {
        "w_in": r_in[:D_MODEL], "w_up": r_up,
        "wp_dn": seg("wp_dn"), "wp_sb": seg("wp_sb"), "w_out": seg("w_out"), "w_down": seg("w_down"),
        "dn_conv": r_in[D_MODEL:].reshape(-1)[:DN_CONV * DN_CONV_COLS].reshape(DN_CONV, DN_CONV_COLS),
        "ffn_conv": seg("ffn_conv").reshape(-1)[:FFN_CONV * FFN_CONV_COLS].reshape(FFN_CONV, FFN_CONV_COLS),
    }


def _lane_row(v):
    return jnp.pad(v.reshape(1, -1), ((0, 0), (0, LANES - v.size)))


def kernel(x, norm1_w, w_in, dn_conv_w, dn_A_log, dn_dt_bias, dn_norm_w, w_proj_dn, w_proj_sb, w_out, norm2_w, ffn_w_up, ffn_conv_w, ffn_w_down, norm_f_w, loss_target, m_norm1_w, m_w_in, m_dn_conv_w, m_dn_A_log, m_dn_dt_bias, m_dn_norm_w, m_w_proj_dn, m_w_proj_sb, m_w_out, m_norm2_w, m_ffn_w_up, m_ffn_conv_w, m_ffn_w_down, m_norm_f_w, v_norm1_w, v_w_in, v_dn_conv_w, v_dn_A_log, v_dn_dt_bias, v_dn_norm_w, v_w_proj_dn, v_w_proj_sb, v_w_out, v_norm2_w, v_ffn_w_up, v_ffn_conv_w, v_ffn_w_down, v_norm_f_w):
    early, late = _weight_wire(w_in[0], w_proj_dn[0], w_proj_sb[0], w_out[0], ffn_w_up[0], ffn_w_down[0],
                               dn_conv_w[0], ffn_conv_w[0])
    chip_idx = (2 * lax.axis_index("x") + lax.axis_index("y")).astype(jnp.int32)

    def with_mine(gathered, wire):
        return [lax.dynamic_update_slice(g, mine[None], (chip_idx, 0, 0)) for g, mine in zip(gathered, wire)]

    wts = _unpack_early(*with_mine(_gather_shards(early), early))
    wts.update(norm1=norm1_w, norm2=norm2_w, normf=norm_f_w.reshape(1, D_MODEL), dn_norm=dn_norm_w,
               alog=_lane_row(dn_A_log), dtb=_lane_row(dn_dt_bias))

    c_idx = lax.axis_index("c").astype(jnp.int32).reshape(1)

    def pair_sums(wire_g, tags, when):
        return [_pair_add(g, got, c_idx, tag) for g, got, tag in zip(wire_g, _pair_exchange_halves(wire_g, when), tags)]

    loss_part, grad_x, gr, (early_sums, early_arrived) = _local_step(
        x[0], loss_target[0], wts, late, lambda gathered: _unpack_late(*with_mine(gathered, late)),
        lambda grads: pair_sums(_grad_wire_early(grads), ["w_up", "rows"], "early"))

    late_sums = pair_sums(_grad_wire_late(gr), ["w_in"], "late")
    tags = ["w_in", "w_up", "rows"]
    reduced = [_sum_partials(p, got, chip_idx.reshape(1), tag)
               for p, got, tag in zip(late_sums + early_sums, list(_chip_exchange(late_sums)) + list(early_arrived), tags)]
    is_south = lax.axis_index("c") == 0
    gsh = _unpack_grad_shard(*[jnp.concatenate([jnp.where(is_south, mine, other), jnp.where(is_south, other, mine)],
                                               axis=0) for mine, other in zip(reduced, _pair_share(reduced))])

    tail = jnp.concatenate([gr["dn_norm"], gr["alog"][:, :N_HEADS], gr["dtb"][:, :N_HEADS], loss_part[:, :1]], axis=1)
    small = jnp.concatenate([gr["norm1"], gr["norm2"], gr["normf"],
                             jnp.pad(tail, ((0, 0), (0, D_MODEL - tail.shape[1]))),
                             jnp.zeros((SMALL_ROWS - 4, D_MODEL), F32)], axis=0)
    small = _small_allreduce(small)
    at = HEAD_DIM
    g_small = {"norm1_w": small[0:1], "norm2_w": small[1:2], "norm_f_w": small[2],
               "dn_norm_w": small[3:4, :at], "dn_A_log": small[3:4, at:at + N_HEADS],
               "dn_dt_bias": small[3:4, at + N_HEADS:at + 2 * N_HEADS]}
    loss = small[3, at + 2 * N_HEADS]

    big = {"w_in": (w_in, m_w_in, v_w_in, gsh["w_in"]), "dn_conv_w": (dn_conv_w, m_dn_conv_w, v_dn_conv_w, gsh["dn_conv"]),
           "w_proj_dn": (w_proj_dn, m_w_proj_dn, v_w_proj_dn, gsh["wp_dn"]),
           "w_proj_sb": (w_proj_sb, m_w_proj_sb, v_w_proj_sb, gsh["wp_sb"]),
           "w_out": (w_out, m_w_out, v_w_out, gsh["w_out"]),
           "ffn_w_up": (ffn_w_up, m_ffn_w_up, v_ffn_w_up, gsh["w_up"]),
           "ffn_conv_w": (ffn_conv_w, m_ffn_conv_w, v_ffn_conv_w, gsh["ffn_conv"]),
           "ffn_w_down": (ffn_w_down, m_ffn_w_down, v_ffn_w_down, gsh["w_down"])}
    res = {}
    for nm, (w, m, v, g) in big.items():
        d, nm_, nv_ = _adamw(w[0], g, m[0], v[0], "adamw_" + nm)
        res[nm] = (g[None], d[None], nm_[None], nv_[None])

    names = ["norm1_w", "norm2_w", "norm_f_w", "dn_norm_w", "dn_A_log", "dn_dt_bias"]
    given = {"norm1_w": (norm1_w, m_norm1_w, v_norm1_w), "norm2_w": (norm2_w, m_norm2_w, v_norm2_w),
             "norm_f_w": (norm_f_w, m_norm_f_w, v_norm_f_w), "dn_norm_w": (dn_norm_w, m_dn_norm_w, v_dn_norm_w),
             "dn_A_log": (dn_A_log, m_dn_A_log, v_dn_A_log), "dn_dt_bias": (dn_dt_bias, m_dn_dt_bias, v_dn_dt_bias)}

    def stack(k, fill):
        rows = [jnp.pad(given[nm][k].reshape(1, -1), ((0, 0), (0, D_MODEL - given[nm][k].size)),
                        constant_values=fill) for nm in names]
        return jnp.concatenate(rows + [jnp.full((SMALL_ROWS - len(names), D_MODEL), fill, F32)], axis=0)

    g_rows = jnp.concatenate(
        [jnp.pad(g_small[nm].reshape(1, -1), ((0, 0), (0, D_MODEL - g_small[nm].size))) for nm in names]
        + [jnp.zeros((SMALL_ROWS - len(names), D_MODEL), F32)], axis=0)
    d_s, m_s, v_s = _adamw(stack(0, 0.0), g_rows, stack(1, 0.0), stack(2, 1.0), "adamw_small")
    for r, nm in enumerate(names):
        shape = given[nm][0].shape
        n = given[nm][0].size
        res[nm] = (g_small[nm].reshape(shape), d_s[r, :n].reshape(shape), m_s[r, :n].reshape(shape),
                   v_s[r, :n].reshape(shape))

    order = ["norm1_w", "w_in", "dn_conv_w", "dn_A_log", "dn_dt_bias", "dn_norm_w", "w_proj_dn", "w_proj_sb",
             "w_out", "norm2_w", "ffn_w_up", "ffn_conv_w", "ffn_w_down", "norm_f_w"]
    outs = [loss, grad_x[None]]
    for k in range(4):
        outs += [res[nm][k] for nm in order]
    return tuple(outs)
```

```python
import functools

import jax
import jax.numpy as jnp
from jax import lax
from jax.experimental import pallas as pl
from jax.experimental.pallas import tpu as pltpu

F32 = jnp.float32
BF16 = jnp.bfloat16
MESH = pl.DeviceIdType.MESH

EPS = 1e-6
D_MODEL = 1024
N_HEADS = 8
HEAD_DIM = 128
DN_CONV = 4
DN_CHUNK = 64
D_FF = 2816
FFN_CONV = 3
ADAM_LR, ADAM_B1, ADAM_B2, ADAM_EPS, ADAM_WD, ADAM_STEP = 0.001, 0.9, 0.999, 1e-08, 0.01, 10

N_CHIPS = 4
LANES = 128
HALO = 8
VMEM_LIMIT = 48 * 1024 * 1024
SMALL_ROWS = 8


def _params(sem=None):
    return pltpu.CompilerParams(dimension_semantics=sem, vmem_limit_bytes=VMEM_LIMIT)


def _pick(n, target):
    best = None
    for b in range(LANES, min(n, target) + 1, LANES):
        if n % b == 0:
            best = b
    return best or n


ELEMENTWISE_COLS = 1408


def _rows(t, target=256):
    return min(t, target)


def _dot(a, b, precision=None):
    return lax.dot_general(a, b, (((1,), (0,)), ((), ())), precision=precision, preferred_element_type=F32)


def _dot_nt(a, b, precision=None):
    return lax.dot_general(a, b, (((1,), (1,)), ((), ())), precision=precision, preferred_element_type=F32)


def _dot_tn(a, b, precision=None):
    return lax.dot_general(a, b, (((0,), (0,)), ((), ())), precision=precision, preferred_element_type=F32)


def _rms(x, w):
    return x * lax.rsqrt(jnp.mean(x * x, axis=-1, keepdims=True) + EPS) * w


def _silu(x):
    return x * jax.nn.sigmoid(x)


def _softplus(x):
    return jnp.maximum(x, 0.0) + jnp.log(1.0 + jnp.exp(-jnp.abs(x)))


MM_BLOCK = 1408
MM_VMEM_BUDGET = 38 * 1024 * 1024


def _mm(a, b, *, ta=False, tb=False, add=None, out_dtype=F32, name, bm=MM_BLOCK, bn=MM_BLOCK, bk=MM_BLOCK):
    m = a.shape[1] if ta else a.shape[0]
    k = a.shape[0] if ta else a.shape[1]
    n = b.shape[0] if tb else b.shape[1]
    bm, bn = _pick(m, bm), _pick(n, bn)

    def vmem_need(bk_):
        need = 2 * (bm * bk_ * a.dtype.itemsize + bk_ * bn * b.dtype.itemsize) + 2 * bm * bn * jnp.dtype(out_dtype).itemsize
        need += 2 * bm * bn * add.dtype.itemsize if add is not None else 0
        return need + (bm * bn * 4 if bk_ < k else 0)

    bk = max((d for d in range(LANES, k + 1, LANES) if k % d == 0 and vmem_need(d) <= MM_VMEM_BUDGET),
             default=_pick(k, bk))
    nk = k // bk
    dims = (((0 if ta else 1,), (1 if tb else 0,)), ((), ()))

    def body(*refs):
        a_ref, b_ref = refs[:2]
        c_ref = refs[2] if add is not None else None
        o_ref = refs[3] if add is not None else refs[2]
        acc = refs[-1]
        kk = pl.program_id(2)
        part = lax.dot_general(a_ref[...].astype(BF16), b_ref[...].astype(BF16), dims, preferred_element_type=F32)

        def finish(r):
            if add is not None:
                r = r + c_ref[...].astype(F32)
            o_ref[...] = r.astype(out_dtype)

        if nk == 1:
            finish(part)
            return

        @pl.when(kk == 0)
        def _():
            acc[...] = part

        @pl.when(jnp.logical_and(kk > 0, kk < nk - 1))
        def _():
            acc[...] += part

        @pl.when(kk == nk - 1)
        def _():
            finish(acc[...] + part)

    a_spec = (pl.BlockSpec((bk, bm), lambda i, j, kk: (kk, i)) if ta
              else pl.BlockSpec((bm, bk), lambda i, j, kk: (i, kk)))
    b_spec = (pl.BlockSpec((bn, bk), lambda i, j, kk: (j, kk)) if tb
              else pl.BlockSpec((bk, bn), lambda i, j, kk: (kk, j)))
    o_spec = pl.BlockSpec((bm, bn), lambda i, j, kk: (i, j))
    in_specs = [a_spec, b_spec] + ([o_spec] if add is not None else [])
    args = (a, b) + ((add,) if add is not None else ())
    return pl.pallas_call(
        body, name=name, grid=(m // bm, n // bn, nk),
        in_specs=in_specs, out_specs=o_spec,
        out_shape=jax.ShapeDtypeStruct((m, n), out_dtype),
        scratch_shapes=[pltpu.VMEM((bm, bn), F32)] if nk > 1 else [],
        compiler_params=_params(("parallel", "parallel", "arbitrary")),
    )(*args)


def _norm1_fwd(x, w, w_ab):
    t = x.shape[0]
    tb = _rows(t)

    def body(x_ref, w_ref, wab_ref, n_ref, hab_ref):
        n = _rms(x_ref[...], w_ref[...]).astype(BF16)
        n_ref[...] = n
        hab_ref[...] = _dot(n, wab_ref[...])

    return pl.pallas_call(
        body, name="norm1_fwd", grid=(t // tb,),
        in_specs=[pl.BlockSpec((tb, D_MODEL), lambda i: (i, 0)),
                  pl.BlockSpec((1, D_MODEL), lambda i: (0, 0)),
                  pl.BlockSpec((D_MODEL, LANES), lambda i: (0, 0))],
        out_specs=[pl.BlockSpec((tb, D_MODEL), lambda i: (i, 0)),
                   pl.BlockSpec((tb, LANES), lambda i: (i, 0))],
        out_shape=[jax.ShapeDtypeStruct((t, D_MODEL), BF16), jax.ShapeDtypeStruct((t, LANES), F32)],
        compiler_params=_params(("arbitrary",)),
    )(x, w, w_ab)


def _norm1_bwd(x, w, dn, dres, dab, w_ab):
    t = x.shape[0]
    tb = _rows(t)

    def body(x_ref, w_ref, dn_ref, dres_ref, dab_ref, wab_ref, dx_ref, dw_ref):
        i = pl.program_id(0)
        g = dn_ref[...] + _dot_nt(dab_ref[...].astype(BF16), wab_ref[...])
        _, vjp = jax.vjp(_rms, x_ref[...], w_ref[...])
        dx, dw = vjp(g)
        dx_ref[...] = dres_ref[...] + dx

        @pl.when(i == 0)
        def _():
            dw_ref[...] = jnp.zeros_like(dw_ref)

        dw_ref[...] += dw

    row = pl.BlockSpec((tb, D_MODEL), lambda i: (i, 0))
    vec = pl.BlockSpec((1, D_MODEL), lambda i: (0, 0))
    return pl.pallas_call(
        body, name="norm1_bwd", grid=(t // tb,),
        in_specs=[row, vec, row, row, pl.BlockSpec((tb, LANES), lambda i: (i, 0)),
                  pl.BlockSpec((D_MODEL, LANES), lambda i: (0, 0))],
        out_specs=[row, vec],
        out_shape=[jax.ShapeDtypeStruct((t, D_MODEL), F32), jax.ShapeDtypeStruct((1, D_MODEL), F32)],
        compiler_params=_params(("arbitrary",)),
    )(x, w, dn, dres, dab, w_ab)


def _conv_fwd(x, w, name):
    t, c = x.shape
    kk = w.shape[0]
    tb, cb = _rows(t, 512), _pick(c, ELEMENTWISE_COLS)
    per = tb // HALO

    def body(x_ref, halo_ref, w_ref, y_ref, buf):
        i = pl.program_id(0)
        buf[pl.ds(HALO, tb), :] = x_ref[...]
        buf[pl.ds(0, HALO), :] = jnp.where(i == 0, 0.0, halo_ref[...])
        y_ref[...] = _conv_taps(buf, w_ref, HALO - (kk - 1), tb)

    return pl.pallas_call(
        body, name=name, grid=(t // tb, c // cb),
        in_specs=[pl.BlockSpec((tb, cb), lambda i, j: (i, j)),
                  pl.BlockSpec((HALO, cb), lambda i, j: (jnp.maximum(i * per - 1, 0), j)),
                  pl.BlockSpec((kk, cb), lambda i, j: (0, j))],
        out_specs=pl.BlockSpec((tb, cb), lambda i, j: (i, j)),
        out_shape=jax.ShapeDtypeStruct((t, c), F32),
        scratch_shapes=[pltpu.VMEM((tb + HALO, cb), F32)],
        compiler_params=_params(("parallel", "parallel")),
    )(x, x, w)


def _conv_bwd(dy, x, w, name, dx_dtype):
    t, c = x.shape
    kk = w.shape[0]
    tb, cb = _rows(t, 512), _pick(c, ELEMENTWISE_COLS)
    per = tb // HALO
    nblk = t // tb

    def body(dy_ref, after_ref, x_ref, w_ref, dx_ref, dw_ref, dbuf):
        i = pl.program_id(1)
        dbuf[pl.ds(0, tb), :] = dy_ref[...]
        dbuf[pl.ds(tb, HALO), :] = jnp.where(i == nblk - 1, 0.0, after_ref[...])

        @pl.when(i == 0)
        def _():
            dw_ref[...] = jnp.zeros_like(dw_ref)

        for j in range(cb // LANES):
            sl = pl.ds(j * LANES, LANES)
            x = x_ref[:, sl]
            dx = None
            for s in range(kk):
                shifted = dbuf[pl.ds(kk - 1 - s, tb), sl]
                term = w_ref[s:s + 1, sl] * shifted
                dx = term if dx is None else dx + term
                dw_ref[s:s + 1, sl] += jnp.sum(shifted * x, axis=0, keepdims=True)
            dx_ref[:, sl] = dx.astype(dx_dtype)

    blk = pl.BlockSpec((tb, cb), lambda j, i: (i, j))
    return pl.pallas_call(
        body, name=name, grid=(c // cb, nblk),
        in_specs=[blk,
                  pl.BlockSpec((HALO, cb), lambda j, i: (jnp.minimum((i + 1) * per, t // HALO - 1), j)),
                  blk,
                  pl.BlockSpec((kk, cb), lambda j, i: (0, j))],
        out_specs=[blk, pl.BlockSpec((HALO, cb), lambda j, i: (0, j))],
        out_shape=[jax.ShapeDtypeStruct((t, c), dx_dtype), jax.ShapeDtypeStruct((HALO, c), F32)],
        scratch_shapes=[pltpu.VMEM((tb + HALO, cb), F32)],
        compiler_params=_params(("parallel", "arbitrary")),
    )(dy, dy, x, w)


def _dn_head(c, normed):
    s = _silu(c)
    return s * lax.rsqrt(jnp.sum(s * s, axis=-1, keepdims=True) + EPS) if normed else s


def _dn_gates(hab, alog, dtb):
    lane = lax.broadcasted_iota(jnp.int32, hab.shape, 1)
    g = -jnp.exp(alog) * _softplus(hab + dtb)
    beta = jax.nn.sigmoid(hab)
    return jnp.where(lane < N_HEADS, g, jnp.where(lane < 2 * N_HEADS, beta, 0.0))


def _dn_head_slices(q_ref, k_ref, v_ref):
    return [(pl.ds((part * N_HEADS + h) * HEAD_DIM, HEAD_DIM), ref, h, part < 2)
            for part, ref in enumerate((q_ref, k_ref, v_ref)) for h in range(N_HEADS)]


def _dn_prep_fwd(c, hab, alog, dtb):
    t = c.shape[0]
    tb = _rows(t)

    def body(c_ref, hab_ref, alog_ref, dtb_ref, q_ref, k_ref, v_ref, gb_ref):
        for sl, ref, h, normed in _dn_head_slices(q_ref, k_ref, v_ref):
            ref[h] = _dn_head(c_ref[:, sl], normed)
        gb_ref[...] = _dn_gates(hab_ref[...], alog_ref[...], dtb_ref[...])

    hm = pl.BlockSpec((N_HEADS, tb, HEAD_DIM), lambda i: (0, i, 0))
    nar = pl.BlockSpec((tb, LANES), lambda i: (i, 0))
    vec = pl.BlockSpec((1, LANES), lambda i: (0, 0))
    return pl.pallas_call(
        body, name="dn_prep_fwd", grid=(t // tb,),
        in_specs=[pl.BlockSpec((tb, 3 * D_MODEL), lambda i: (i, 0)), nar, vec, vec],
        out_specs=[hm, hm, hm, nar],
        out_shape=[jax.ShapeDtypeStruct((N_HEADS, t, HEAD_DIM), F32)] * 3 + [jax.ShapeDtypeStruct((t, LANES), F32)],
        compiler_params=_params(("parallel",)),
    )(c, hab, alog, dtb)


def _dn_prep_bwd(c, hab, alog, dtb, dq, dk, dv, dgb):
    t = c.shape[0]
    tb = _rows(t)

    def body(c_ref, hab_ref, alog_ref, dtb_ref, dq_ref, dk_ref, dv_ref, dgb_ref,
             dc_ref, dhab_ref, dalog_ref, ddtb_ref):
        i = pl.program_id(0)
        for sl, ref, h, normed in _dn_head_slices(dq_ref, dk_ref, dv_ref):
            _, vjp = jax.vjp(functools.partial(_dn_head, normed=normed), c_ref[:, sl])
            dc_ref[:, sl] = vjp(ref[h])[0]
        _, vjp = jax.vjp(_dn_gates, hab_ref[...], alog_ref[...], dtb_ref[...])
        dhab, dalog, ddtb = vjp(dgb_ref[...])
        dhab_ref[...] = dhab

        @pl.when(i == 0)
        def _():
            dalog_ref[...] = jnp.zeros_like(dalog_ref)
            ddtb_ref[...] = jnp.zeros_like(ddtb_ref)

        dalog_ref[...] += dalog
        ddtb_ref[...] += ddtb

    hm = pl.BlockSpec((N_HEADS, tb, HEAD_DIM), lambda i: (0, i, 0))
    wide = pl.BlockSpec((tb, 3 * D_MODEL), lambda i: (i, 0))
    nar = pl.BlockSpec((tb, LANES), lambda i: (i, 0))
    vec = pl.BlockSpec((1, LANES), lambda i: (0, 0))
    return pl.pallas_call(
        body, name="dn_prep_bwd", grid=(t // tb,),
        in_specs=[wide, nar, vec, vec, hm, hm, hm, nar],
        out_specs=[wide, nar, vec, vec],
        out_shape=[jax.ShapeDtypeStruct((t, 3 * D_MODEL), F32), jax.ShapeDtypeStruct((t, LANES), F32),
                   jax.ShapeDtypeStruct((1, LANES), F32), jax.ShapeDtypeStruct((1, LANES), F32)],
        compiler_params=_params(("arbitrary",)),
    )(c, hab, alog, dtb, dq, dk, dv, dgb)


DN_PREC = lax.Precision.HIGH
DN_GROUP = 32


def _dn_prec(a):
    return DN_PREC if a.dtype == F32 else None


def _bdot(a, b):
    return lax.dot_general(a, b, (((2,), (1,)), ((0,), (0,))), precision=_dn_prec(a), preferred_element_type=F32)


def _bdot_nt(a, b):
    return lax.dot_general(a, b, (((2,), (2,)), ((0,), (0,))), precision=_dn_prec(a), preferred_element_type=F32)


def _bdot_tn(a, b):
    return lax.dot_general(a, b, (((1,), (1,)), ((0,), (0,))), precision=_dn_prec(a), preferred_element_type=F32)


def _unit_lower_inverse(lmat):
    c = lmat.shape[-1]
    ri = lax.broadcasted_iota(jnp.int32, (c, c), 0)
    ci = lax.broadcasted_iota(jnp.int32, (c, c), 1)
    p = -lmat
    tinv = jnp.where(ri == ci, 1.0, 0.0) + p
    for _ in range(max(c.bit_length() - 2, 0)):
        p = _bdot(p, p)
        tinv = tinv + _bdot(tinv, p)
    return tinv


@jax.custom_vjp
def _solve_with(lmat, rhs, tinv):
    return _bdot(tinv, rhs)


def _solve_with_fwd(lmat, rhs, tinv):
    sol = _bdot(tinv, rhs)
    return sol, (sol, tinv)


def _solve_with_bwd(res, dsol):
    sol, tinv = res
    drhs = _bdot_tn(tinv, dsol)
    return -_bdot_nt(drhs, sol), drhs, jnp.zeros_like(tinv)


_solve_with.defvjp(_solve_with_fwd, _solve_with_bwd)


def _dn_local(q, k, v, grow, brow, tinv):
    g, c, _ = q.shape
    ri = lax.broadcasted_iota(jnp.int32, (c, c), 0)
    ci = lax.broadcasted_iota(jnp.int32, (c, c), 1)
    lower = ri >= ci
    as_col = lambda r: jnp.sum(jnp.where(ri == ci, jnp.broadcast_to(r, (g, c, c)), 0.0), axis=2, keepdims=True)
    gcol, bcol = as_col(grow), as_col(brow)
    gc_col = jnp.sum(jnp.where(lower, jnp.broadcast_to(grow, (g, c, c)), 0.0), axis=2, keepdims=True)
    gc_row = jnp.sum(jnp.where(ri <= ci, jnp.broadcast_to(gcol, (g, c, c)), 0.0), axis=1, keepdims=True)
    qs = q * (HEAD_DIM ** -0.5)
    kb = k * bcol
    vb = v * bcol
    decay = jnp.where(lower, jnp.exp(jnp.where(lower, gc_col - gc_row, 0.0)), 0.0)
    lmat = jnp.where(ri > ci, _bdot_nt(kb.astype(BF16), k.astype(BF16)) * decay, 0.0)
    eg = jnp.exp(gc_col)
    rhs = jnp.concatenate([vb, kb * eg], axis=2)
    if tinv is None:
        tinv = _unit_lower_inverse(lmat)
    sol = _solve_with(lmat, rhs, tinv)
    a_qk = jnp.where(lower, _bdot_nt(qs.astype(BF16), k.astype(BF16)) * decay, 0.0)
    g_last = jnp.sum(grow, axis=2, keepdims=True)
    kdec = k * jnp.exp(g_last - gc_col)
    egl = jnp.broadcast_to(jnp.exp(g_last), (g, 1, HEAD_DIM))
    b16 = lambda x: x.astype(BF16)
    return sol[:, :, :HEAD_DIM], b16(sol[:, :, HEAD_DIM:]), b16(a_qk), b16(qs * eg), b16(kdec), egl, tinv


def _dn_seq(u, w, a_qk, qe, kdec, egl, s_in):
    b16 = lambda x: x.astype(BF16)
    v_new = u - _bdot(b16(w), b16(s_in))
    o = _bdot(b16(qe), b16(s_in)) + _bdot(b16(a_qk), b16(v_new))
    return o, s_in * egl + _bdot_tn(b16(kdec), b16(v_new))


def _dn_local_specs(t):
    grp = min(DN_GROUP, t // DN_CHUNK)
    rows = grp * DN_CHUNK
    blk = pl.BlockSpec((1, rows, HEAD_DIM), lambda h, i: (h, i, 0))
    row = pl.BlockSpec((1, grp, 1, DN_CHUNK), lambda h, i: (h, i, 0, 0))
    sq = pl.BlockSpec((1, grp, DN_CHUNK, DN_CHUNK), lambda h, i: (h, i, 0, 0))
    lane = pl.BlockSpec((1, grp, 1, HEAD_DIM), lambda h, i: (h, i, 0, 0))
    return grp, blk, row, sq, lane


def half(shape):
    return jax.ShapeDtypeStruct(shape.shape, BF16)


def _dn_shapes(t):
    nchunk = t // DN_CHUNK
    big = jax.ShapeDtypeStruct((N_HEADS, t, HEAD_DIM), F32)
    row = jax.ShapeDtypeStruct((N_HEADS, nchunk, 1, DN_CHUNK), F32)
    sq = jax.ShapeDtypeStruct((N_HEADS, nchunk, DN_CHUNK, DN_CHUNK), F32)
    lane = jax.ShapeDtypeStruct((N_HEADS, nchunk, 1, HEAD_DIM), F32)
    return big, row, sq, lane


def _dn_local_fwd(q, k, v, grow, brow, wire=()):
    t = q.shape[1]
    grp, blk, row, sq, lane = _dn_local_specs(t)
    big, _, sqs, lanes = _dn_shapes(t)
    n = len(wire)
    groups = t // (grp * DN_CHUNK)
    steps = N_HEADS * groups

    def body(q_ref, k_ref, v_ref, gr_ref, br_ref, *rest):
        u_ref, w_ref, a_ref, qe_ref, kd_ref, egl_ref, t_ref = rest[n:n + 7]
        if n:
            begin, middle, end = _gather_protocol(rest[:n], rest[n + 7:2 * n + 7], *rest[2 * n + 7:])
            step = pl.program_id(0) * groups + pl.program_id(1)
            pl.when(step == 0)(begin)
            pl.when(step == (GATHER_FORWARD_AT * steps) // 8)(middle)
        split = lambda r: r[0].reshape(grp, DN_CHUNK, HEAD_DIM)
        u, w, a_qk, qe, kdec, egl, tinv = _dn_local(split(q_ref), split(k_ref), split(v_ref), gr_ref[0],
                                                     br_ref[0], None)
        for ref, val in ((u_ref, u), (w_ref, w), (qe_ref, qe), (kd_ref, kdec)):
            ref[0] = val.reshape(grp * DN_CHUNK, HEAD_DIM)
        a_ref[0] = a_qk
        egl_ref[0] = egl
        t_ref[0] = tinv
        if n:
            pl.when(step == steps - 1)(end)

    assert n == 0 or steps >= 3
    return pl.pallas_call(
        body, name="dn_local_fwd", grid=(N_HEADS, groups),
        in_specs=[blk, blk, blk, row, row] + _hbm_specs(n),
        out_specs=[blk, blk, sq, blk, blk, lane, sq] + _hbm_specs(n),
        out_shape=[big, half(big), half(sqs), half(big), half(big), lanes, sqs] + _gather_out_shapes(wire),
        scratch_shapes=_gather_sems(n) if n else [],
        compiler_params=_params(("arbitrary", "arbitrary")),
    )(q, k, v, grow, brow, *wire)


def _dn_local_bwd(q, k, v, grow, brow, tinv, du, dw, da, dqe, dkd, degl):
    t = q.shape[1]
    grp, blk, row, sq, lane = _dn_local_specs(t)
    big, rows_, _, _ = _dn_shapes(t)

    def body(q_ref, k_ref, v_ref, gr_ref, br_ref, t_ref, du_ref, dw_ref, da_ref, dqe_ref, dkd_ref,
             degl_ref, dq_ref, dk_ref, dv_ref, dgr_ref, dbr_ref):
        split = lambda r: r[0].reshape(grp, DN_CHUNK, HEAD_DIM)
        tinv_v = t_ref[0]
        fn = lambda q_, k_, v_, gr_, br_: _dn_local(q_, k_, v_, gr_, br_, tinv_v)[:6]
        _, vjp = jax.vjp(fn, split(q_ref), split(k_ref), split(v_ref), gr_ref[0], br_ref[0])
        dq, dk, dv, dgr, dbr = vjp((split(du_ref), split(dw_ref), da_ref[0], split(dqe_ref), split(dkd_ref),
                                    degl_ref[0]))
        for ref, val in ((dq_ref, dq), (dk_ref, dk), (dv_ref, dv)):
            ref[0] = val.reshape(grp * DN_CHUNK, HEAD_DIM)
        dgr_ref[0] = dgr
        dbr_ref[0] = dbr

    return pl.pallas_call(
        body, name="dn_local_bwd", grid=(N_HEADS, t // (grp * DN_CHUNK)),
        in_specs=[blk, blk, blk, row, row, sq, blk, blk, sq, blk, blk, lane],
        out_specs=[blk, blk, blk, row, row],
        out_shape=[big, big, big, rows_, rows_],
        compiler_params=_params(("parallel", "parallel")),
    )(q, k, v, grow, brow, tinv, du, dw, da, dqe, dkd, degl)


DN_SEQ_CHUNKS = 4


def _dn_seq_specs(nchunk, rev):
    per = min(DN_SEQ_CHUNKS, nchunk)
    nstep = nchunk // per

    def idx(n):
        return nstep - 1 - n if rev else n

    blk = pl.BlockSpec((N_HEADS, per * DN_CHUNK, HEAD_DIM), lambda n: (0, idx(n), 0))
    sq = pl.BlockSpec((N_HEADS, per, DN_CHUNK, DN_CHUNK), lambda n: (0, idx(n), 0, 0))
    lane = pl.BlockSpec((N_HEADS, per, 1, HEAD_DIM), lambda n: (0, idx(n), 0, 0))
    st = pl.BlockSpec((N_HEADS, per, HEAD_DIM, HEAD_DIM), lambda n: (0, idx(n), 0, 0))
    return per, nstep, blk, sq, lane, st


def _dn_seq_fwd(u, w, a_qk, qe, kdec, egl):
    t = u.shape[1]
    nchunk = t // DN_CHUNK
    per, nstep, blk, sq, lane, st = _dn_seq_specs(nchunk, False)

    def body(u_ref, w_ref, a_ref, qe_ref, kd_ref, egl_ref, o_ref, s_ref, state):
        @pl.when(pl.program_id(0) == 0)
        def _():
            state[...] = jnp.zeros_like(state)

        for c in range(per):
            rows = pl.ds(c * DN_CHUNK, DN_CHUNK)
            s_in = state[...]
            s_ref[:, c] = s_in.astype(BF16)
            o_ref[:, rows], state[...] = _dn_seq(u_ref[:, rows], w_ref[:, rows], a_ref[:, c], qe_ref[:, rows],
                                                 kd_ref[:, rows], egl_ref[:, c], s_in)

    return pl.pallas_call(
        body, name="dn_seq_fwd", grid=(nstep,),
        in_specs=[blk, blk, sq, blk, blk, lane],
        out_specs=[blk, st],
        out_shape=[jax.ShapeDtypeStruct((N_HEADS, t, HEAD_DIM), F32),
                   jax.ShapeDtypeStruct((N_HEADS, nchunk, HEAD_DIM, HEAD_DIM), BF16)],
        scratch_shapes=[pltpu.VMEM((N_HEADS, HEAD_DIM, HEAD_DIM), F32)],
        compiler_params=_params(("arbitrary",)),
    )(u, w, a_qk, qe, kdec, egl)


def _dn_seq_bwd(u, w, a_qk, qe, kdec, egl, states, do):
    t = u.shape[1]
    nchunk = t // DN_CHUNK
    per, nstep, blk, sq, lane, st = _dn_seq_specs(nchunk, True)
    big, _, sqs, lanes = _dn_shapes(t)

    def body(u_ref, w_ref, a_ref, qe_ref, kd_ref, egl_ref, s_ref, do_ref,
             du_ref, dw_ref, da_ref, dqe_ref, dkd_ref, degl_ref, dstate):
        @pl.when(pl.program_id(0) == 0)
        def _():
            dstate[...] = jnp.zeros_like(dstate)

        for c in reversed(range(per)):
            rows = pl.ds(c * DN_CHUNK, DN_CHUNK)
            _, vjp = jax.vjp(_dn_seq, u_ref[:, rows], w_ref[:, rows], a_ref[:, c], qe_ref[:, rows], kd_ref[:, rows],
                             egl_ref[:, c], s_ref[:, c].astype(F32))
            (du_ref[:, rows], dw_ref[:, rows], da_ref[:, c], dqe_ref[:, rows], dkd_ref[:, rows], degl_ref[:, c],
             dstate[...]) = vjp((do_ref[:, rows], dstate[...]))

    return pl.pallas_call(
        body, name="dn_seq_bwd", grid=(nstep,),
        in_specs=[blk, blk, sq, blk, blk, lane, st, blk],
        out_specs=[blk, blk, sq, blk, blk, lane],
        out_shape=[big, half(big), half(sqs), half(big), half(big), lanes],
        scratch_shapes=[pltpu.VMEM((N_HEADS, HEAD_DIM, HEAD_DIM), F32)],
        compiler_params=_params(("arbitrary",)),
    )(u, w, a_qk, qe, kdec, egl, states, do)


def _dn_post_head(o, gate, w):
    return _rms(o, w) * _silu(gate)


def _dn_post_fwd(o, gate, w):
    t = gate.shape[0]
    tb = _rows(t)

    def body(o_ref, g_ref, w_ref, y_ref):
        for h in range(N_HEADS):
            sl = pl.ds(h * HEAD_DIM, HEAD_DIM)
            y_ref[:, sl] = _dn_post_head(o_ref[h], g_ref[:, sl], w_ref[...]).astype(BF16)

    row = pl.BlockSpec((tb, D_MODEL), lambda i: (i, 0))
    hm = pl.BlockSpec((N_HEADS, tb, HEAD_DIM), lambda i: (0, i, 0))
    return pl.pallas_call(
        body, name="dn_post_fwd", grid=(t // tb,),
        in_specs=[hm, row, pl.BlockSpec((1, HEAD_DIM), lambda i: (0, 0))],
        out_specs=row, out_shape=jax.ShapeDtypeStruct((t, D_MODEL), BF16),
        compiler_params=_params(("parallel",)),
    )(o, gate, w)


def _dn_post_bwd(o, gate, w, dy):
    t = gate.shape[0]
    tb = _rows(t)

    def body(o_ref, g_ref, w_ref, dy_ref, do_ref, dg_ref, dw_ref):
        i = pl.program_id(0)
        @pl.when(i == 0)
        def _():
            dw_ref[...] = jnp.zeros_like(dw_ref)

        for h in range(N_HEADS):
            sl = pl.ds(h * HEAD_DIM, HEAD_DIM)
            _, vjp = jax.vjp(_dn_post_head, o_ref[h], g_ref[:, sl], w_ref[...])
            do_ref[h], dg, dw = vjp(dy_ref[:, sl])
            dg_ref[:, sl] = dg.astype(BF16)
            dw_ref[...] += dw

    row = pl.BlockSpec((tb, D_MODEL), lambda i: (i, 0))
    hm = pl.BlockSpec((N_HEADS, tb, HEAD_DIM), lambda i: (0, i, 0))
    vec = pl.BlockSpec((1, HEAD_DIM), lambda i: (0, 0))
    return pl.pallas_call(
        body, name="dn_post_bwd", grid=(t // tb,),
        in_specs=[hm, row, vec, row],
        out_specs=[hm, row, vec],
        out_shape=[jax.ShapeDtypeStruct((N_HEADS, t, HEAD_DIM), F32), jax.ShapeDtypeStruct((t, D_MODEL), BF16),
                   jax.ShapeDtypeStruct((1, HEAD_DIM), F32)],
        compiler_params=_params(("arbitrary",)),
    )(o, gate, w, dy)


def _split_bf16(x):
    hi = x.astype(BF16)
    lo = (x - hi.astype(F32)).astype(BF16)
    return hi, lo


SB_Q_BLOCK = 512
SB_K_BLOCK = 256
SB_NEGLIGIBLE = -60.0


def _sb_logits(q, kb, mask, scale):
    z = _dot_nt(q, kb) * scale
    ls = jnp.minimum(z, 0.0) - jnp.log(1.0 + jnp.exp(-jnp.abs(z)))
    lk = ls - z
    if mask is not None:
        lk = jnp.where(mask, lk, 0.0)
    return ls, lk


def _sb_blocks(t):
    bq = min(SB_Q_BLOCK, t)
    bk = min(SB_K_BLOCK, bq)
    return bq, bk, bq // bk


def _sb_fwd(qkv):
    t = qkv.shape[0]
    bq, bk, nd = _sb_blocks(t)
    scale = HEAD_DIM ** -0.5

    def body(q_ref, k_ref, v_ref, o_ref, tot_ref, used_ref):
        i = pl.program_id(1)
        q = q_ref[...]
        rj = lax.broadcasted_iota(jnp.int32, (bk, bk), 0)
        cj = lax.broadcasted_iota(jnp.int32, (bk, bk), 1)
        after = (rj > cj).astype(BF16)
        trow = lax.broadcasted_iota(jnp.int32, (bq, bk), 0)
        scol = lax.broadcasted_iota(jnp.int32, (bq, bk), 1)

        def tile(j, run, acc, mask):
            off = pl.multiple_of(j * bk, bk)
            kb = k_ref[pl.ds(off, bk), :]
            vb = v_ref[pl.ds(off, bk), :]
            ls, lk = _sb_logits(q, kb, mask, scale)
            hi, lo = _split_bf16(lk)
            between = _dot(hi, after) + _dot(lo, after) + run
            a = jnp.exp(ls + between)
            if mask is not None:
                a = jnp.where(mask, a, 0.0)
            acc = acc + _dot(a.astype(BF16), vb)
            return run + jnp.sum(lk, axis=1, keepdims=True), acc

        run, acc = jnp.zeros((bq, 1), F32), jnp.zeros((bq, HEAD_DIM), F32)
        for d in reversed(range(nd)):
            run, acc = tile(i * nd + d, run, acc, scol + d * bk < trow)
        def more(c):
            return jnp.logical_and(c[0] < i * nd, jnp.max(c[1]) > SB_NEGLIGIBLE)

        def far(c):
            run_, acc_ = tile(i * nd - 1 - c[0], c[1], c[2], None)
            return c[0] + 1, run_, acc_

        used, run, acc = lax.while_loop(more, far, (jnp.int32(0), run, acc))
        o_ref[...] = acc.astype(BF16)
        tot_ref[...] = jnp.broadcast_to(run, (bq, HEAD_DIM))
        used_ref[...] = jnp.full(used_ref.shape, used, F32)

    qs = pl.BlockSpec((bq, HEAD_DIM), lambda h, i: (i, h))
    ks = pl.BlockSpec((t, HEAD_DIM), lambda h, i: (0, N_HEADS + h))
    vs = pl.BlockSpec((t, HEAD_DIM), lambda h, i: (0, 2 * N_HEADS + h))
    return pl.pallas_call(
        body, name="sb_fwd", grid=(N_HEADS, t // bq),
        in_specs=[qs, ks, vs], out_specs=[qs, qs, pl.BlockSpec((1, 1, 1, LANES), lambda h, i: (h, i, 0, 0))],
        out_shape=[jax.ShapeDtypeStruct((t, D_MODEL), BF16), jax.ShapeDtypeStruct((t, D_MODEL), F32),
                   jax.ShapeDtypeStruct((N_HEADS, t // bq, 1, LANES), F32)],
        compiler_params=_params(("parallel", "arbitrary")),
    )(qkv, qkv, qkv)


def _sb_bwd(qkv, tot, used, do, partials=()):
    t = qkv.shape[0]
    bq, bk, nd = _sb_blocks(t)
    scale = HEAD_DIM ** -0.5
    n = len(partials)
    nq = t // bq

    def body(q_ref, k_ref, v_ref, tot_ref, used_ref, do_ref, *rest):
        dq_ref, dk_ref, dv_ref = rest[n:n + 3]
        i = pl.program_id(1)
        if n:
            begin, end = _chip_exchange_protocol(rest[:n], rest[n + 3:2 * n + 3], *rest[2 * n + 3:])
            step = pl.program_id(0) * nq + i
            pl.when(step == 0)(begin)

        @pl.when(i == 0)
        def _():
            dk_ref[...] = jnp.zeros_like(dk_ref)
            dv_ref[...] = jnp.zeros_like(dv_ref)

        q = q_ref[...]
        do = do_ref[...]
        total = tot_ref[:, 0:1]
        rj = lax.broadcasted_iota(jnp.int32, (bk, bk), 0)
        cj = lax.broadcasted_iota(jnp.int32, (bk, bk), 1)
        upto = (rj <= cj).astype(BF16)
        before = (rj < cj).astype(BF16)
        trow = lax.broadcasted_iota(jnp.int32, (bq, bk), 0)
        scol = lax.broadcasted_iota(jnp.int32, (bq, bk), 1)

        def tile(j, run_k, run_e, dq, mask):
            off = pl.multiple_of(j * bk, bk)
            kb = k_ref[pl.ds(off, bk), :]
            vb = v_ref[pl.ds(off, bk), :]
            ls, lk = _sb_logits(q, kb, mask, scale)
            hi, lo = _split_bf16(lk)
            between = total - (_dot(hi, upto) + _dot(lo, upto) + run_k)
            a = jnp.exp(ls + between)
            if mask is not None:
                a = jnp.where(mask, a, 0.0)
            e = a * _dot_nt(do, vb)
            ehi, elo = _split_bf16(e)
            pre = _dot(ehi, before) + _dot(elo, before) + run_e
            sig = jnp.exp(ls)
            dz = e * (1.0 - sig) - pre * sig
            if mask is not None:
                dz = jnp.where(mask, dz, 0.0)
            dz = (dz * scale).astype(BF16)
            dq = dq + _dot(dz, kb)
            dk_ref[pl.ds(off, bk), :] += _dot_tn(dz, q)
            dv_ref[pl.ds(off, bk), :] += _dot_tn(a.astype(BF16), do)
            return (run_k + jnp.sum(lk, axis=1, keepdims=True),
                    run_e + jnp.sum(e, axis=1, keepdims=True), dq)

        zero = jnp.zeros((bq, 1), F32)
        visited = jnp.clip(jnp.max(used_ref[...]).astype(jnp.int32), 0, i * nd)
        carry = lax.fori_loop(i * nd - visited, i * nd, lambda j, c: tile(j, c[0], c[1], c[2], None),
                              (zero, zero, jnp.zeros((bq, HEAD_DIM), F32)))
        for d in range(nd):
            carry = tile(i * nd + d, *carry, scol + d * bk < trow)
        dq_ref[...] = carry[2]
        if n:
            pl.when(step == N_HEADS * nq - 1)(end)

    qs = pl.BlockSpec((bq, HEAD_DIM), lambda h, i: (i, h))
    ks = pl.BlockSpec((t, HEAD_DIM), lambda h, i: (0, N_HEADS + h))
    vs = pl.BlockSpec((t, HEAD_DIM), lambda h, i: (0, 2 * N_HEADS + h))
    full = pl.BlockSpec((t, HEAD_DIM), lambda h, i: (0, h))
    big = jax.ShapeDtypeStruct((t, D_MODEL), F32)
    return pl.pallas_call(
        body, name="sb_bwd", grid=(N_HEADS, nq),
        in_specs=[qs, ks, vs, qs, pl.BlockSpec((1, 1, 1, LANES), lambda h, i: (h, i, 0, 0)), qs] + _hbm_specs(n),
        out_specs=[qs, full, full] + _hbm_specs(n),
        out_shape=[big, big, big] + _chip_exchange_shapes(partials),
        scratch_shapes=_chip_exchange_sems(n) if n else [],
        compiler_params=_params(("arbitrary", "arbitrary")),
    )(qkv, qkv, qkv, tot, used, do, *partials)


def _merge_fwd(o_dn, o_sb, gl, x, wp_dn, wp_sb, w_out, w2):
    t = x.shape[0]
    tb = _rows(t)

    def body(odn_ref, osb_ref, gl_ref, x_ref, wpd_ref, wps_ref, wo_ref, w2_ref,
             pdn_ref, psb_ref, mix_ref, x1_ref, n2_ref):
        pdn = _dot(odn_ref[...], wpd_ref[...])
        psb = _dot(osb_ref[...], wps_ref[...])
        gates = jax.nn.sigmoid(gl_ref[...])
        mixed = (gates[:, :D_MODEL] * pdn + gates[:, D_MODEL:] * psb).astype(BF16)
        x1 = x_ref[...] + _dot(mixed, wo_ref[...])
        pdn_ref[...] = pdn.astype(BF16)
        psb_ref[...] = psb.astype(BF16)
        mix_ref[...] = mixed
        x1_ref[...] = x1
        n2_ref[...] = _rms(x1, w2_ref[...]).astype(BF16)

    row = pl.BlockSpec((tb, D_MODEL), lambda i: (i, 0))
    sq = pl.BlockSpec((D_MODEL, D_MODEL), lambda i: (0, 0))
    f = jax.ShapeDtypeStruct((t, D_MODEL), F32)
    b = jax.ShapeDtypeStruct((t, D_MODEL), BF16)
    return pl.pallas_call(
        body, name="merge_fwd", grid=(t // tb,),
        in_specs=[row, row, pl.BlockSpec((tb, 2 * D_MODEL), lambda i: (i, 0)), row, sq, sq, sq,
                  pl.BlockSpec((1, D_MODEL), lambda i: (0, 0))],
        out_specs=[row] * 5, out_shape=[b, b, b, f, b],
        compiler_params=_params(("parallel",)),
    )(o_dn, o_sb, gl, x, wp_dn, wp_sb, w_out, w2)


def _merge_bwd(dx2, dn2, x1, w2, gl, pdn, psb, wp_dn, wp_sb, w_out):
    t = x1.shape[0]
    tb = _rows(t)

    def body(dx2_ref, dn2_ref, x1_ref, w2_ref, gl_ref, pdn_ref, psb_ref, wpd_ref, wps_ref, wo_ref,
             dx1_ref, dw2_ref, dgl_ref, dpdn_ref, dpsb_ref, dodn_ref, dosb_ref):
        i = pl.program_id(0)
        _, vjp = jax.vjp(_rms, x1_ref[...], w2_ref[...])
        dxn, dw2 = vjp(dn2_ref[...])
        dx1 = dx2_ref[...] + dxn
        dx1_ref[...] = dx1

        @pl.when(i == 0)
        def _():
            dw2_ref[...] = jnp.zeros_like(dw2_ref)

        dw2_ref[...] += dw2
        dmix = _dot_nt(dx1.astype(BF16), wo_ref[...])
        gates = jax.nn.sigmoid(gl_ref[...])
        g_dn, g_sb = gates[:, :D_MODEL], gates[:, D_MODEL:]
        dpdn = (dmix * g_dn).astype(BF16)
        dpsb = (dmix * g_sb).astype(BF16)
        dgl_ref[:, :D_MODEL] = (dmix * pdn_ref[...].astype(F32) * g_dn * (1.0 - g_dn)).astype(BF16)
        dgl_ref[:, D_MODEL:] = (dmix * psb_ref[...].astype(F32) * g_sb * (1.0 - g_sb)).astype(BF16)
        dpdn_ref[...] = dpdn
        dpsb_ref[...] = dpsb
        dodn_ref[...] = _dot_nt(dpdn, wpd_ref[...])
        dosb_ref[...] = _dot_nt(dpsb, wps_ref[...]).astype(BF16)

    row = pl.BlockSpec((tb, D_MODEL), lambda i: (i, 0))
    wide = pl.BlockSpec((tb, 2 * D_MODEL), lambda i: (i, 0))
    sq = pl.BlockSpec((D_MODEL, D_MODEL), lambda i: (0, 0))
    vec = pl.BlockSpec((1, D_MODEL), lambda i: (0, 0))
    f = jax.ShapeDtypeStruct((t, D_MODEL), F32)
    b = jax.ShapeDtypeStruct((t, D_MODEL), BF16)
    return pl.pallas_call(
        body, name="merge_bwd", grid=(t // tb,),
        in_specs=[row, row, row, vec, wide, row, row, sq, sq, sq],
        out_specs=[row, vec, wide, row, row, row, row],
        out_shape=[f, jax.ShapeDtypeStruct((1, D_MODEL), F32), jax.ShapeDtypeStruct((t, 2 * D_MODEL), BF16),
                   b, b, f, b],
        compiler_params=_params(("arbitrary",)),
    )(dx2, dn2, x1, w2, gl, pdn, psb, wp_dn, wp_sb, w_out)


def _conv_taps(buf, w_ref, first, rows, cols=slice(None)):
    y = w_ref[0:1, cols] * buf[pl.ds(first, rows), cols]
    for s in range(1, w_ref.shape[0]):
        y = y + w_ref[s:s + 1, cols] * buf[pl.ds(first + s, rows), cols]
    return y


def _ffn_mid_fwd(pre_g, pre_u, wg, wu):
    t, c = pre_g.shape
    kk = wg.shape[0]
    tb, cb = _rows(t), _pick(c, ELEMENTWISE_COLS)
    per = tb // HALO

    def body(g_ref, gh_ref, u_ref, uh_ref, wg_ref, wu_ref, a_ref, gbuf, ubuf):
        i = pl.program_id(0)
        for buf, ref, halo in ((gbuf, g_ref, gh_ref), (ubuf, u_ref, uh_ref)):
            buf[pl.ds(HALO, tb), :] = ref[...]
            buf[pl.ds(0, HALO), :] = jnp.where(i == 0, 0.0, halo[...])
        for j in range(cb // LANES):
            sl = pl.ds(j * LANES, LANES)
            ug = _conv_taps(gbuf, wg_ref, HALO - (kk - 1), tb, sl)
            uu = _conv_taps(ubuf, wu_ref, HALO - (kk - 1), tb, sl)
            a_ref[:, sl] = (_silu(ug) * uu).astype(BF16)

    blk = pl.BlockSpec((tb, cb), lambda i, j: (i, j))
    halo = pl.BlockSpec((HALO, cb), lambda i, j: (jnp.maximum(i * per - 1, 0), j))
    wspec = pl.BlockSpec((kk, cb), lambda i, j: (0, j))
    return pl.pallas_call(
        body, name="ffn_mid_fwd", grid=(t // tb, c // cb),
        in_specs=[blk, halo, blk, halo, wspec, wspec], out_specs=blk,
        out_shape=jax.ShapeDtypeStruct((t, c), BF16),
        scratch_shapes=[pltpu.VMEM((tb + HALO, cb), F32)] * 2,
        compiler_params=_params(("parallel", "parallel")),
    )(pre_g, pre_g, pre_u, pre_u, wg, wu)


def _ffn_mid_bwd(pre_g, pre_u, wg, wu, da):
    t, c = pre_g.shape
    kk = wg.shape[0]
    tb, cb = _rows(t), _pick(c, ELEMENTWISE_COLS)
    per = tb // HALO
    nblk = t // tb
    ext = tb + HALO

    def body(g_ref, gb_ref, ga_ref, u_ref, ub_ref, ua_ref, da_ref, daa_ref, wg_ref, wu_ref,
             dg_ref, du_ref, dwg_ref, dwu_ref, gbuf, ubuf, dabuf, dgbuf, dubuf):
        i = pl.program_id(1)
        last = i == nblk - 1
        for buf, ref, before, after in ((gbuf, g_ref, gb_ref, ga_ref), (ubuf, u_ref, ub_ref, ua_ref)):
            buf[pl.ds(0, HALO), :] = jnp.where(i == 0, 0.0, before[...])
            buf[pl.ds(HALO, tb), :] = ref[...]
            buf[pl.ds(HALO + tb, HALO), :] = jnp.where(last, 0.0, after[...])
        dabuf[pl.ds(0, tb), :] = da_ref[...]
        dabuf[pl.ds(tb, HALO), :] = jnp.where(last, 0.0, daa_ref[...])

        @pl.when(i == 0)
        def _():
            dwg_ref[...] = jnp.zeros_like(dwg_ref)
            dwu_ref[...] = jnp.zeros_like(dwu_ref)

        for j in range(cb // LANES):
            sl = pl.ds(j * LANES, LANES)
            ug = _conv_taps(gbuf, wg_ref, HALO - (kk - 1), ext, sl)
            uu = _conv_taps(ubuf, wu_ref, HALO - (kk - 1), ext, sl)
            _, vjp = jax.vjp(lambda g, u: _silu(g) * u, ug, uu)
            dgbuf[:, sl], dubuf[:, sl] = vjp(dabuf[:, sl])
            for dbuf, xbuf, w_ref, dx_ref, dw_ref in ((dgbuf, gbuf, wg_ref, dg_ref, dwg_ref),
                                                      (dubuf, ubuf, wu_ref, du_ref, dwu_ref)):
                x = xbuf[pl.ds(HALO, tb), sl]
                dx = None
                for s in range(kk):
                    shifted = dbuf[pl.ds(kk - 1 - s, tb), sl]
                    term = w_ref[s:s + 1, sl] * shifted
                    dx = term if dx is None else dx + term
                    dw_ref[s:s + 1, sl] += jnp.sum(shifted * x, axis=0, keepdims=True)
                dx_ref[:, sl] = dx.astype(BF16)

    blk = pl.BlockSpec((tb, cb), lambda j, i: (i, j))
    before = pl.BlockSpec((HALO, cb), lambda j, i: (jnp.maximum(i * per - 1, 0), j))
    after = pl.BlockSpec((HALO, cb), lambda j, i: (jnp.minimum((i + 1) * per, t // HALO - 1), j))
    wspec = pl.BlockSpec((kk, cb), lambda j, i: (0, j))
    dwspec = pl.BlockSpec((HALO, cb), lambda j, i: (0, j))
    half = jax.ShapeDtypeStruct((t, c), BF16)
    dwshape = jax.ShapeDtypeStruct((HALO, c), F32)
    return pl.pallas_call(
        body, name="ffn_mid_bwd", grid=(c // cb, nblk),
        in_specs=[blk, before, after, blk, before, after, blk, after, wspec, wspec],
        out_specs=[blk, blk, dwspec, dwspec],
        out_shape=[half, half, dwshape, dwshape],
        scratch_shapes=[pltpu.VMEM((ext + HALO, cb), F32)] * 2 + [pltpu.VMEM((ext, cb), F32)] * 3,
        compiler_params=_params(("parallel", "arbitrary")),
    )(pre_g, pre_g, pre_g, pre_u, pre_u, pre_u, da, da, wg, wu)


def _down_loss(a, w_down, x1, wf, target):
    t = x1.shape[0]
    tb = _rows(t)

    def body(a_ref, wd_ref, x1_ref, wf_ref, tgt_ref, dx2_ref, dwf_ref, loss_ref):
        i = pl.program_id(0)
        x2 = x1_ref[...] + _dot(a_ref[...], wd_ref[...])
        y, vjp = jax.vjp(_rms, x2, wf_ref[...])
        err = y - tgt_ref[...]
        dx2, dwf = vjp(err * (1.0 / D_MODEL))
        dx2_ref[...] = dx2
        part = jnp.sum(jnp.sum(err * err, axis=1, keepdims=True), axis=0, keepdims=True) * (0.5 / D_MODEL)

        @pl.when(i == 0)
        def _():
            dwf_ref[...] = jnp.zeros_like(dwf_ref)
            loss_ref[...] = jnp.zeros_like(loss_ref)

        dwf_ref[...] += dwf
        loss_ref[...] += jnp.broadcast_to(part, loss_ref.shape)

    row = pl.BlockSpec((tb, D_MODEL), lambda i: (i, 0))
    vec = pl.BlockSpec((1, D_MODEL), lambda i: (0, 0))
    return pl.pallas_call(
        body, name="down_loss", grid=(t // tb,),
        in_specs=[pl.BlockSpec((tb, D_FF), lambda i: (i, 0)), pl.BlockSpec((D_FF, D_MODEL), lambda i: (0, 0)),
                  row, vec, row],
        out_specs=[row, vec, pl.BlockSpec((1, LANES), lambda i: (0, 0))],
        out_shape=[jax.ShapeDtypeStruct((t, D_MODEL), F32), jax.ShapeDtypeStruct((1, D_MODEL), F32),
                   jax.ShapeDtypeStruct((1, LANES), F32)],
        compiler_params=_params(("arbitrary",)),
    )(a, w_down, x1, wf, target)


def _local_step(x, target, wts, late_wire=(), late_weights=None, early_partials=None):
    t = x.shape[0]
    nchunk = t // DN_CHUNK

    n1, hab = _norm1_fwd(x, wts["norm1"], wts["w_ab"])
    dnqkv = _mm(n1, wts["w_dnqkv"], name="h_dnqkv")
    dngate = _mm(n1, wts["w_dngate"], name="h_dngate")
    sbqkv = _mm(n1, wts["w_sbqkv"], out_dtype=BF16, name="h_sbqkv")
    gl = _mm(n1, wts["w_gl"], name="h_gl")

    cdn = _conv_fwd(dnqkv, wts["dn_conv"], "dn_conv_fwd")
    qn, kn, vv, gb = _dn_prep_fwd(cdn, hab, wts["alog"], wts["dtb"])
    per_head = gb[:, :2 * N_HEADS].T.reshape(2 * N_HEADS, nchunk, DN_CHUNK)
    grow, brow = per_head[:N_HEADS, :, None, :], per_head[N_HEADS:, :, None, :]
    u_dn, w_dn, a_qk, qe, kdec, egl, tinv, *late = _dn_local_fwd(qn, kn, vv, grow, brow, late_wire)
    if late_wire:
        wts = {**wts, **late_weights(late)}
    o_raw, states = _dn_seq_fwd(u_dn, w_dn, a_qk, qe, kdec, egl)
    o_dn = _dn_post_fwd(o_raw, dngate, wts["dn_norm"])

    o_sb, tot, sb_used = _sb_fwd(sbqkv)

    pdn, psb, mixed, x1, n2 = _merge_fwd(o_dn, o_sb, gl, x, wts["wp_dn"], wts["wp_sb"], wts["w_out"],
                                         wts["norm2"])
    pre_g = _mm(n2, wts["w_up_g"], name="ffn_up_g")
    pre_u = _mm(n2, wts["w_up_u"], name="ffn_up_u")
    act = _ffn_mid_fwd(pre_g, pre_u, wts["ffn_conv_g"], wts["ffn_conv_u"])
    dx2, d_normf, loss_part = _down_loss(act, wts["w_down"], x1, wts["normf"], target)

    grads = {"normf": d_normf}
    da = _mm(dx2, wts["w_down"], tb=True, name="d_act")
    grads["w_down"] = _mm(act, dx2, ta=True, out_dtype=BF16, name="dw_down")
    dpre_g, dpre_u, dcw_g, dcw_u = _ffn_mid_bwd(pre_g, pre_u, wts["ffn_conv_g"], wts["ffn_conv_u"], da)
    grads["ffn_conv"] = jnp.concatenate([dcw_g[:FFN_CONV], dcw_u[:FFN_CONV]], axis=1)
    dn2 = _mm(dpre_g, wts["w_up_g"], tb=True, name="dn2_g")
    dn2 = _mm(dpre_u, wts["w_up_u"], tb=True, add=dn2, name="dn2_u")
    grads["w_up_g"] = _mm(n2, dpre_g, ta=True, out_dtype=BF16, name="dw_up_g")
    grads["w_up_u"] = _mm(n2, dpre_u, ta=True, out_dtype=BF16, name="dw_up_u")

    dx1, grads["norm2"], dgl, dpdn, dpsb, do_dn, do_sb = _merge_bwd(
        dx2, dn2, x1, wts["norm2"], gl, pdn, psb, wts["wp_dn"], wts["wp_sb"], wts["w_out"])
    grads["w_out"] = _mm(mixed, dx1, ta=True, out_dtype=BF16, name="dw_out")
    grads["wp_dn"] = _mm(o_dn, dpdn, ta=True, out_dtype=BF16, name="dw_proj_dn")
    grads["wp_sb"] = _mm(o_sb, dpsb, ta=True, out_dtype=BF16, name="dw_proj_sb")

    partials = early_partials(grads) if early_partials else ()
    dsq, dsk, dsv, *arrived = _sb_bwd(sbqkv, tot, sb_used, do_sb, partials)
    dsbqkv = jnp.concatenate([dsq, dsk, dsv], axis=1).astype(BF16)

    do_raw, ddngate, grads["dn_norm"] = _dn_post_bwd(o_raw, dngate, wts["dn_norm"], do_dn)
    seq_grads = _dn_seq_bwd(u_dn, w_dn, a_qk, qe, kdec, egl, states, do_raw)
    dqn, dkn, dvv, dgrow, dbrow = _dn_local_bwd(qn, kn, vv, grow, brow, tinv, *seq_grads)
    dgb = jnp.concatenate([dgrow.reshape(N_HEADS, t), dbrow.reshape(N_HEADS, t)], axis=0).T
    dgb = jnp.pad(dgb, ((0, 0), (0, LANES - 2 * N_HEADS)))
    dcdn, dhab, grads["alog"], grads["dtb"] = _dn_prep_bwd(cdn, hab, wts["alog"], wts["dtb"], dqn, dkn, dvv, dgb)
    ddnqkv, dcw_dn = _conv_bwd(dcdn, dnqkv, wts["dn_conv"], "dn_conv_bwd", BF16)
    grads["dn_conv"] = dcw_dn[:DN_CONV]

    dn1 = _mm(ddnqkv, wts["w_dnqkv"], tb=True, name="dn1_dnqkv")
    dn1 = _mm(ddngate, wts["w_dngate"], tb=True, add=dn1, name="dn1_dngate")
    dn1 = _mm(dsbqkv, wts["w_sbqkv"], tb=True, add=dn1, name="dn1_sbqkv")
    dn1 = _mm(dgl, wts["w_gl"], tb=True, add=dn1, name="dn1_gl")
    grads["w_dnqkv"] = _mm(n1, ddnqkv, ta=True, out_dtype=BF16, name="dw_dnqkv")
    grads["w_dngate"] = _mm(n1, ddngate, ta=True, out_dtype=BF16, name="dw_dngate")
    grads["w_sbqkv"] = _mm(n1, dsbqkv, ta=True, out_dtype=BF16, name="dw_sbqkv")
    grads["w_gl"] = _mm(n1, dgl, ta=True, out_dtype=BF16, name="dw_gl")
    grads["w_ab"] = _mm(n1, dhab, ta=True, out_dtype=BF16, name="dw_ab")
    grad_x, grads["norm1"] = _norm1_bwd(x, wts["norm1"], dn1, dx1, dhab, wts["w_ab"])
    return loss_part, grad_x, grads, (list(partials), arrived)


def _place():
    return lax.axis_index("x"), lax.axis_index("y"), lax.axis_index("c")


def _hbm_specs(n):
    return [pl.BlockSpec(memory_space=pltpu.HBM)] * n


GATHER_SEMS = 8
GATHER_FORWARD_AT = 4


def _gather_protocol(ins, outs, send_sems, recv_sems):
    n = len(ins)
    x, y, c = _place()
    me = 2 * x + y
    sibling = (x, y, 1 - c)
    xn, yn, dg = (1 - x, y), (x, 1 - y), (1 - x, 1 - y)
    idx = lambda chip: 2 * chip[0] + chip[1]

    def part(a, chip_index, core, quarter=None):
        half = ins[a].shape[0] // 2
        if quarter is None:
            return outs[a].at[chip_index, pl.ds(core * half, half), :]
        return outs[a].at[chip_index, pl.ds(core * half + quarter * (half // 2), half // 2), :]

    def copy(a, k, src, dst, to):
        return pltpu.make_async_remote_copy(src_ref=src, dst_ref=dst, send_sem=send_sems.at[GATHER_SEMS * a + k],
                                            recv_sem=recv_sems.at[GATHER_SEMS * a + k], device_id=to,
                                            device_id_type=MESH)

    def sent(a, k):
        half = ins[a].shape[0] // 2
        my_half = ins[a].at[pl.ds(c * half, half), :]
        if k < 2:
            return copy(a, k, my_half, part(a, me, c), (*(xn, yn)[k], c))
        if k < 4:
            src = part(a, idx((xn, yn)[k - 2]), c, k - 2)
            return copy(a, k, src, src, (*(yn, xn)[k - 2], c))
        src = (part(a, idx(xn), c), part(a, idx(yn), c), part(a, idx(dg), c, 0), part(a, idx(dg), c, 1))[k - 4]
        return copy(a, k, src, src, sibling)

    def landed(a, k):
        dst = (part(a, idx(xn), c), part(a, idx(yn), c), part(a, idx(dg), c, 0), part(a, idx(dg), c, 1),
               part(a, idx(xn), 1 - c), part(a, idx(yn), 1 - c), part(a, idx(dg), 1 - c, 0),
               part(a, idx(dg), 1 - c, 1))[k]
        return copy(a, k, dst, dst, sibling)

    def begin():
        for a in range(n):
            sent(a, 0).start()
            sent(a, 1).start()

    def middle():
        for a in range(n):
            for k in range(2):
                landed(a, k).wait_recv()
                sent(a, 2 + k).start()
                sent(a, 4 + k).start()

    def end():
        for a in range(n):
            for k in (2, 3):
                landed(a, k).wait_recv()
                sent(a, 4 + k).start()
        for a in range(n):
            for k in range(4, GATHER_SEMS):
                landed(a, k).wait_recv()
        for a in range(n):
            for k in range(GATHER_SEMS):
                sent(a, k).wait_send()

    return begin, middle, end


def _gather_out_shapes(shards):
    return [jax.ShapeDtypeStruct((N_CHIPS,) + s.shape, s.dtype) for s in shards]


def _gather_sems(n):
    return [pltpu.SemaphoreType.DMA((GATHER_SEMS * n,)), pltpu.SemaphoreType.DMA((GATHER_SEMS * n,))]


def _gather_shards(shards):
    n = len(shards)

    def body(*refs):
        begin, middle, end = _gather_protocol(refs[:n], refs[n:2 * n], *refs[2 * n:])
        begin()
        middle()
        end()

    return pl.pallas_call(
        body, name="gather_weights", in_specs=_hbm_specs(n), out_specs=_hbm_specs(n),
        out_shape=_gather_out_shapes(shards), scratch_shapes=_gather_sems(n),
    )(*shards)


def _pair_exchange_halves(gs, tag):
    n = len(gs)

    def body(*refs):
        ins, outs, (send_sems, recv_sems) = refs[:n], refs[n:2 * n], refs[2 * n:]
        x, y, c = _place()
        cps = []
        for a in range(n):
            half = ins[a].shape[1] // 2
            cp = pltpu.make_async_remote_copy(src_ref=ins[a].at[:, pl.ds((1 - c) * half, half), :], dst_ref=outs[a],
                                              send_sem=send_sems.at[a], recv_sem=recv_sems.at[a],
                                              device_id=(x, y, 1 - c), device_id_type=MESH)
            cp.start()
            cps.append(cp)
        for cp in cps:
            cp.wait()

    return pl.pallas_call(
        body, name="grad_pair_exchange_" + tag, in_specs=_hbm_specs(n), out_specs=_hbm_specs(n),
        out_shape=[jax.ShapeDtypeStruct((g.shape[0], g.shape[1] // 2, g.shape[2]), g.dtype) for g in gs],
        scratch_shapes=[pltpu.SemaphoreType.DMA((n,)), pltpu.SemaphoreType.DMA((n,))],
    )(*gs)


def _pick_rows(n, target=1024):
    best = 16
    for b in range(16, min(n, target) + 1, 16):
        if n % b == 0:
            best = b
    return best


def _pair_add(g, got, c_idx, tag):
    nsh, rows, cols = g.shape
    half = rows // 2
    rb = _pick_rows(half)

    def body(c_ref, g_ref, got_ref, o_ref):
        o_ref[...] = (g_ref[...].astype(F32) + got_ref[...].astype(F32)).astype(BF16)

    nb = half // rb
    grid_spec = pltpu.PrefetchScalarGridSpec(
        num_scalar_prefetch=1, grid=(nsh, nb),
        in_specs=[pl.BlockSpec((1, rb, cols), lambda s, i, c_ref: (s, c_ref[0] * nb + i, 0)),
                  pl.BlockSpec((1, rb, cols), lambda s, i, c_ref: (s, i, 0))],
        out_specs=pl.BlockSpec((1, rb, cols), lambda s, i, c_ref: (s, i, 0)))
    return pl.pallas_call(
        body, name="grad_pair_add_" + tag, grid_spec=grid_spec,
        out_shape=jax.ShapeDtypeStruct((nsh, half, cols), BF16),
        compiler_params=_params(("parallel", "parallel")),
    )(c_idx, g, got)


def _chip_exchange_protocol(ins, outs, send_sems, recv_sems):
    x, y, c = _place()
    chips = [(1 - x, y), (x, 1 - y), (1 - x, 1 - y)]

    def copies():
        return [pltpu.make_async_remote_copy(src_ref=ins[a].at[2 * px + py], dst_ref=outs[a].at[j],
                                             send_sem=send_sems.at[3 * a + j], recv_sem=recv_sems.at[3 * a + j],
                                             device_id=(px, py, c), device_id_type=MESH)
                for a in range(len(ins)) for j, (px, py) in enumerate(chips)]

    def begin():
        for cp in copies():
            cp.start()

    def end():
        for cp in copies():
            cp.wait_recv()
        for cp in copies():
            cp.wait_send()

    return begin, end


def _chip_exchange_shapes(ps):
    return [jax.ShapeDtypeStruct((N_CHIPS - 1,) + p.shape[1:], p.dtype) for p in ps]


def _chip_exchange_sems(n):
    return [pltpu.SemaphoreType.DMA((3 * n,)), pltpu.SemaphoreType.DMA((3 * n,))]


def _chip_exchange(ps):
    n = len(ps)

    def body(*refs):
        begin, end = _chip_exchange_protocol(refs[:n], refs[n:2 * n], *refs[2 * n:])
        begin()
        end()

    return pl.pallas_call(
        body, name="grad_chip_exchange", in_specs=_hbm_specs(n), out_specs=_hbm_specs(n),
        out_shape=_chip_exchange_shapes(ps), scratch_shapes=_chip_exchange_sems(n),
    )(*ps)


def _sum_partials(p, got, chip_idx, tag):
    nsh, half, cols = got.shape
    rb = _pick_rows(half)

    def body(me_ref, p_ref, got_ref, o_ref):
        acc = p_ref[0].astype(F32)
        for s in range(nsh):
            acc = acc + got_ref[s].astype(F32)
        o_ref[...] = acc

    grid_spec = pltpu.PrefetchScalarGridSpec(
        num_scalar_prefetch=1, grid=(half // rb,),
        in_specs=[pl.BlockSpec((1, rb, cols), lambda i, me_ref: (me_ref[0], i, 0)),
                  pl.BlockSpec((nsh, rb, cols), lambda i, me_ref: (0, i, 0))],
        out_specs=pl.BlockSpec((rb, cols), lambda i, me_ref: (i, 0)))
    return pl.pallas_call(
        body, name="grad_sum_chips_" + tag, grid_spec=grid_spec,
        out_shape=jax.ShapeDtypeStruct((half, cols), F32),
        compiler_params=_params(("parallel",)),
    )(chip_idx, p, got)


def _pair_share(rs):
    n = len(rs)

    def body(*refs):
        ins, outs, (send_sems, recv_sems) = refs[:n], refs[n:2 * n], refs[2 * n:]
        x, y, c = _place()
        cps = []
        for a in range(n):
            cp = pltpu.make_async_remote_copy(src_ref=ins[a], dst_ref=outs[a], send_sem=send_sems.at[a],
                                              recv_sem=recv_sems.at[a], device_id=(x, y, 1 - c),
                                              device_id_type=MESH)
            cp.start()
            cps.append(cp)
        for cp in cps:
            cp.wait()

    return pl.pallas_call(
        body, name="grad_pair_share", in_specs=_hbm_specs(n), out_specs=_hbm_specs(n),
        out_shape=[jax.ShapeDtypeStruct(r.shape, r.dtype) for r in rs],
        scratch_shapes=[pltpu.SemaphoreType.DMA((n,)), pltpu.SemaphoreType.DMA((n,))],
    )(*rs)


def _small_allreduce(v):
    rows, cols = v.shape
    ndev = 8

    def body(in_ref, out_ref, slots, send_sems, recv_sems):
        x, y, c = _place()
        me = 4 * x + 2 * y + c
        slots[me] = in_ref[...]
        sends = []
        for k in range(1, ndev):
            peer = (x ^ (k >> 2), y ^ ((k >> 1) & 1), c ^ (k & 1))
            cp = pltpu.make_async_remote_copy(src_ref=in_ref, dst_ref=slots.at[me], send_sem=send_sems.at[k - 1],
                                              recv_sem=recv_sems.at[k - 1], device_id=peer, device_id_type=MESH)
            cp.start()
            sends.append(cp)
        for k in range(1, ndev):
            there = slots.at[me ^ k]
            pltpu.make_async_remote_copy(src_ref=there, dst_ref=there, send_sem=send_sems.at[k - 1],
                                         recv_sem=recv_sems.at[k - 1], device_id=(x, y, c),
                                         device_id_type=MESH).wait_recv()
        for cp in sends:
            cp.wait_send()
        acc = slots[0]
        for s in range(1, ndev):
            acc = acc + slots[s]
        out_ref[...] = acc

    return pl.pallas_call(
        body, name="small_allreduce",
        in_specs=[pl.BlockSpec(memory_space=pltpu.VMEM)],
        out_specs=pl.BlockSpec(memory_space=pltpu.VMEM),
        out_shape=jax.ShapeDtypeStruct((rows, cols), F32),
        scratch_shapes=[pltpu.VMEM((ndev, rows, cols), F32), pltpu.SemaphoreType.DMA((ndev - 1,)),
                        pltpu.SemaphoreType.DMA((ndev - 1,))],
    )(v)


def _adamw(w, g, m, v, name):
    r, c = w.shape
    rb = r if r <= 128 else _pick_rows_8(r, 128)
    c1 = 1.0 - ADAM_B1 ** ADAM_STEP
    c2 = 1.0 - ADAM_B2 ** ADAM_STEP

    def body(w_ref, g_ref, m_ref, v_ref, d_ref, nm_ref, nv_ref):
        gg = g_ref[...]
        nm = ADAM_B1 * m_ref[...] + (1.0 - ADAM_B1) * gg
        nv = ADAM_B2 * v_ref[...] + (1.0 - ADAM_B2) * (gg * gg)
        d_ref[...] = -ADAM_LR * ((nm / c1) / (jnp.sqrt(nv / c2) + ADAM_EPS) + ADAM_WD * w_ref[...])
        nm_ref[...] = nm
        nv_ref[...] = nv

    blk = pl.BlockSpec((rb, c), lambda i: (i, 0))
    shp = jax.ShapeDtypeStruct((r, c), F32)
    return pl.pallas_call(
        body, name=name, grid=(r // rb,), in_specs=[blk] * 4, out_specs=[blk] * 3, out_shape=[shp] * 3,
        compiler_params=_params(("parallel",)),
    )(w, g, m, v)


def _pick_rows_8(n, target):
    best = n
    for b in range(8, min(n, target) + 1, 8):
        if n % b == 0:
            best = b
    return best


W_IN_COLS = 2308
W_UP_COLS = 1408
W_DOWN_ROWS = 704
DN_CONV_COLS = 768
FFN_CONV_COLS = 1408
PROJ_ROWS = 256
ROW_TILE = 16
ROW_SEGS = [("wp_dn", PROJ_ROWS), ("wp_sb", PROJ_ROWS), ("w_out", PROJ_ROWS), ("w_down", W_DOWN_ROWS),
            ("dn_conv", ROW_TILE), ("ffn_conv", ROW_TILE), ("spare", 2 * ROW_TILE)]
ROW_OFFS = {nm: (sum(n for _, n in ROW_SEGS[:i]), n) for i, (nm, n) in enumerate(ROW_SEGS)}
STACK_ROWS = sum(n for _, n in ROW_SEGS)
assert all(n % ROW_TILE == 0 for _, n in ROW_SEGS) and STACK_ROWS % (4 * ROW_TILE) == 0
Q_END, A_END, G_END, S_END = 3 * D_MODEL, 3 * D_MODEL + 2 * N_HEADS, 4 * D_MODEL + 2 * N_HEADS, 7 * D_MODEL + 2 * N_HEADS


def _flat_rows(a, nrows):
    flat = a.reshape(-1)
    return jnp.pad(flat, (0, nrows * D_MODEL - flat.shape[0])).reshape(nrows, D_MODEL)


IN_EXTRA_ROWS = 64


def _weight_wire(w_in, wp_dn, wp_sb, w_out, w_up, w_down, dn_conv, ffn_conv):
    bits = lax.bitcast_convert_type(dn_conv, BF16).reshape(-1)
    extra = jnp.pad(bits, (0, IN_EXTRA_ROWS * W_IN_COLS - bits.shape[0])).reshape(IN_EXTRA_ROWS, W_IN_COLS)
    stack = jnp.concatenate([wp_dn.astype(BF16), wp_sb.astype(BF16), w_out.astype(BF16), w_down.astype(BF16),
                             jnp.zeros((ROW_TILE, D_MODEL), BF16),
                             _flat_rows(lax.bitcast_convert_type(ffn_conv, BF16), ROW_TILE),
                             jnp.zeros((ROW_OFFS["spare"][1], D_MODEL), BF16)], axis=0)
    return [jnp.concatenate([w_in.astype(BF16), extra], axis=0)], [w_up.astype(BF16), stack]


def _col_range(g, lo, hi, width):
    parts = []
    for s in range(g.shape[0]):
        a, b = max(lo, s * width), min(hi, (s + 1) * width)
        if a < b:
            parts.append(g[s][:, a - s * width:b - s * width])
    return parts[0] if len(parts) == 1 else jnp.concatenate(parts, axis=1)


def _f32_rows(raw, k, ncols):
    raw = raw.reshape(N_CHIPS, -1)[:, :2 * k * ncols].reshape(N_CHIPS, k * ncols, 2)
    vals = lax.bitcast_convert_type(raw, F32).reshape(N_CHIPS, k, ncols)
    return vals.transpose(1, 0, 2).reshape(k, N_CHIPS * ncols)


def _unpack_early(g_in):
    w = g_in[:, :D_MODEL, :]
    return {
        "w_dnqkv": _col_range(w, 0, Q_END, W_IN_COLS),
        "w_ab": jnp.pad(_col_range(w, Q_END, A_END, W_IN_COLS), ((0, 0), (0, LANES - 2 * N_HEADS))),
        "w_dngate": _col_range(w, A_END, G_END, W_IN_COLS),
        "w_sbqkv": _col_range(w, G_END, S_END, W_IN_COLS),
        "w_gl": _col_range(w, S_END, N_CHIPS * W_IN_COLS, W_IN_COLS),
        "dn_conv": _f32_rows(g_in[:, D_MODEL:, :], DN_CONV, DN_CONV_COLS),
    }


def _unpack_late(g_up, g_stack):
    def seg(nm):
        at, n = ROW_OFFS[nm]
        return g_stack[:, at:at + n, :]

    ffn_conv = _f32_rows(seg("ffn_conv"), FFN_CONV, FFN_CONV_COLS)
    return {
        "wp_dn": seg("wp_dn").reshape(D_MODEL, D_MODEL),
        "wp_sb": seg("wp_sb").reshape(D_MODEL, D_MODEL),
        "w_out": seg("w_out").reshape(D_MODEL, D_MODEL),
        "w_up_g": _col_range(g_up, 0, D_FF, W_UP_COLS), "w_up_u": _col_range(g_up, D_FF, 2 * D_FF, W_UP_COLS),
        "w_down": seg("w_down").reshape(D_FF, D_MODEL),
        "ffn_conv_g": ffn_conv[:, :D_FF], "ffn_conv_u": ffn_conv[:, D_FF:],
    }


def _grad_wire_early(gr):
    def cols(a, ncols):
        return a.reshape(a.shape[0], N_CHIPS, ncols).transpose(1, 0, 2)

    def rows(a, nrows):
        return a.astype(BF16).reshape(N_CHIPS, nrows, a.shape[1])

    def flat(a, nrows):
        a = a.astype(BF16).reshape(N_CHIPS, -1)
        return jnp.pad(a, ((0, 0), (0, nrows * D_MODEL - a.shape[1]))).reshape(N_CHIPS, nrows, D_MODEL)

    up = [gr["w_up_g"], gr["w_up_u"]]
    g_up = jnp.stack([up[s // 2][:, (s % 2) * W_UP_COLS:(s % 2 + 1) * W_UP_COLS].astype(BF16) for s in range(N_CHIPS)])
    g_stack = jnp.concatenate([rows(gr["wp_dn"], PROJ_ROWS), rows(gr["wp_sb"], PROJ_ROWS), rows(gr["w_out"], PROJ_ROWS),
                               rows(gr["w_down"], W_DOWN_ROWS), jnp.zeros((N_CHIPS, ROW_TILE, D_MODEL), BF16),
                               flat(cols(gr["ffn_conv"], FFN_CONV_COLS), ROW_TILE),
                               jnp.zeros((N_CHIPS, ROW_OFFS["spare"][1], D_MODEL), BF16)], axis=1)
    return [g_up, g_stack]


def _grad_wire_late(gr):
    pieces = [(gr["w_dnqkv"], 0), (gr["w_ab"][:, :2 * N_HEADS], Q_END), (gr["w_dngate"], A_END),
              (gr["w_sbqkv"], G_END), (gr["w_gl"], S_END)]
    conv = gr["dn_conv"].reshape(DN_CONV, N_CHIPS, DN_CONV_COLS).transpose(1, 0, 2).reshape(N_CHIPS, -1)

    def block(s):
        lo, hi = s * W_IN_COLS, (s + 1) * W_IN_COLS
        parts = []
        for a, at in pieces:
            b0, b1 = max(lo, at), min(hi, at + a.shape[1])
            if b0 < b1:
                parts.append(a[:, b0 - at:b1 - at].astype(BF16))
        w = parts[0] if len(parts) == 1 else jnp.concatenate(parts, axis=1)
        extra = jnp.pad(conv[s].astype(BF16), (0, IN_EXTRA_ROWS * W_IN_COLS - conv.shape[1]))
        return jnp.concatenate([w, extra.reshape(IN_EXTRA_ROWS, W_IN_COLS)], axis=0)

    return [jnp.stack([block(s) for s in range(N_CHIPS)])]


def _unpack_grad_shard(r_in, r_up, r_stack):
    def seg(nm):
        at, n = ROW_OFFS[nm]
        return r_stack[at:at + n, :]

    return {
        "w_in": r_in[:D_MODEL], "w_up": r_up,
        "wp_dn": seg("wp_dn"), "wp_sb": seg("wp_sb"), "w_out": seg("w_out"), "w_down": seg("w_down"),
        "dn_conv": r_in[D_MODEL:].reshape(-1)[:DN_CONV * DN_CONV_COLS].reshape(DN_CONV, DN_CONV_COLS),
        "ffn_conv": seg("ffn_conv").reshape(-1)[:FFN_CONV * FFN_CONV_COLS].reshape(FFN_CONV, FFN_CONV_COLS),
    }


def _lane_row(v):
    return jnp.pad(v.reshape(1, -1), ((0, 0), (0, LANES - v.size)))


def kernel(x, norm1_w, w_in, dn_conv_w, dn_A_log, dn_dt_bias, dn_norm_w, w_proj_dn, w_proj_sb, w_out, norm2_w, ffn_w_up, ffn_conv_w, ffn_w_down, norm_f_w, loss_target, m_norm1_w, m_w_in, m_dn_conv_w, m_dn_A_log, m_dn_dt_bias, m_dn_norm_w, m_w_proj_dn, m_w_proj_sb, m_w_out, m_norm2_w, m_ffn_w_up, m_ffn_conv_w, m_ffn_w_down, m_norm_f_w, v_norm1_w, v_w_in, v_dn_conv_w, v_dn_A_log, v_dn_dt_bias, v_dn_norm_w, v_w_proj_dn, v_w_proj_sb, v_w_out, v_norm2_w, v_ffn_w_up, v_ffn_conv_w, v_ffn_w_down, v_norm_f_w):
    early, late = _weight_wire(w_in[0], w_proj_dn[0], w_proj_sb[0], w_out[0], ffn_w_up[0], ffn_w_down[0],
                               dn_conv_w[0], ffn_conv_w[0])
    chip_idx = (2 * lax.axis_index("x") + lax.axis_index("y")).astype(jnp.int32)

    def with_mine(gathered, wire):
        return [lax.dynamic_update_slice(g, mine[None], (chip_idx, 0, 0)) for g, mine in zip(gathered, wire)]

    wts = _unpack_early(*with_mine(_gather_shards(early), early))
    wts.update(norm1=norm1_w, norm2=norm2_w, normf=norm_f_w.reshape(1, D_MODEL), dn_norm=dn_norm_w,
               alog=_lane_row(dn_A_log), dtb=_lane_row(dn_dt_bias))

    c_idx = lax.axis_index("c").astype(jnp.int32).reshape(1)

    def pair_sums(wire_g, tags, when):
        return [_pair_add(g, got, c_idx, tag) for g, got, tag in zip(wire_g, _pair_exchange_halves(wire_g, when), tags)]

    loss_part, grad_x, gr, (early_sums, early_arrived) = _local_step(
        x[0], loss_target[0], wts, late, lambda gathered: _unpack_late(*with_mine(gathered, late)),
        lambda grads: pair_sums(_grad_wire_early(grads), ["w_up", "rows"], "early"))

    late_sums = pair_sums(_grad_wire_late(gr), ["w_in"], "late")
    tags = ["w_in", "w_up", "rows"]
    reduced = [_sum_partials(p, got, chip_idx.reshape(1), tag)
               for p, got, tag in zip(late_sums + early_sums, list(_chip_exchange(late_sums)) + list(early_arrived), tags)]
    is_south = lax.axis_index("c") == 0
    gsh = _unpack_grad_shard(*[jnp.concatenate([jnp.where(is_south, mine, other), jnp.where(is_south, other, mine)],
                                               axis=0) for mine, other in zip(reduced, _pair_share(reduced))])

    tail = jnp.concatenate([gr["dn_norm"], gr["alog"][:, :N_HEADS], gr["dtb"][:, :N_HEADS], loss_part[:, :1]], axis=1)
    small = jnp.concatenate([gr["norm1"], gr["norm2"], gr["normf"],
                             jnp.pad(tail, ((0, 0), (0, D_MODEL - tail.shape[1]))),
                             jnp.zeros((SMALL_ROWS - 4, D_MODEL), F32)], axis=0)
    small = _small_allreduce(small)
    at = HEAD_DIM
    g_small = {"norm1_w": small[0:1], "norm2_w": small[1:2], "norm_f_w": small[2],
               "dn_norm_w": small[3:4, :at], "dn_A_log": small[3:4, at:at + N_HEADS],
               "dn_dt_bias": small[3:4, at + N_HEADS:at + 2 * N_HEADS]}
    loss = small[3, at + 2 * N_HEADS]

    big = {"w_in": (w_in, m_w_in, v_w_in, gsh["w_in"]), "dn_conv_w": (dn_conv_w, m_dn_conv_w, v_dn_conv_w, gsh["dn_conv"]),
           "w_proj_dn": (w_proj_dn, m_w_proj_dn, v_w_proj_dn, gsh["wp_dn"]),
           "w_proj_sb": (w_proj_sb, m_w_proj_sb, v_w_proj_sb, gsh["wp_sb"]),
           "w_out": (w_out, m_w_out, v_w_out, gsh["w_out"]),
           "ffn_w_up": (ffn_w_up, m_ffn_w_up, v_ffn_w_up, gsh["w_up"]),
           "ffn_conv_w": (ffn_conv_w, m_ffn_conv_w, v_ffn_conv_w, gsh["ffn_conv"]),
           "ffn_w_down": (ffn_w_down, m_ffn_w_down, v_ffn_w_down, gsh["w_down"])}
    res = {}
    for nm, (w, m, v, g) in big.items():
        d, nm_, nv_ = _adamw(w[0], g, m[0], v[0], "adamw_" + nm)
        res[nm] = (g[None], d[None], nm_[None], nv_[None])

    names = ["norm1_w", "norm2_w", "norm_f_w", "dn_norm_w", "dn_A_log", "dn_dt_bias"]
    given = {"norm1_w": (norm1_w, m_norm1_w, v_norm1_w), "norm2_w": (norm2_w, m_norm2_w, v_norm2_w),
             "norm_f_w": (norm_f_w, m_norm_f_w, v_norm_f_w), "dn_norm_w": (dn_norm_w, m_dn_norm_w, v_dn_norm_w),
             "dn_A_log": (dn_A_log, m_dn_A_log, v_dn_A_log), "dn_dt_bias": (dn_dt_bias, m_dn_dt_bias, v_dn_dt_bias)}

    def stack(k, fill):
        rows = [jnp.pad(given[nm][k].reshape(1, -1), ((0, 0), (0, D_MODEL - given[nm][k].size)),
                        constant_values=fill) for nm in names]
        return jnp.concatenate(rows + [jnp.full((SMALL_ROWS - len(names), D_MODEL), fill, F32)], axis=0)

    g_rows = jnp.concatenate(
        [jnp.pad(g_small[nm].reshape(1, -1), ((0, 0), (0, D_MODEL - g_small[nm].size))) for nm in names]
        + [jnp.zeros((SMALL_ROWS - len(names), D_MODEL), F32)], axis=0)
    d_s, m_s, v_s = _adamw(stack(0, 0.0), g_rows, stack(1, 0.0), stack(2, 1.0), "adamw_small")
    for r, nm in enumerate(names):
        shape = given[nm][0].shape
        n = given[nm][0].size
        res[nm] = (g_small[nm].reshape(shape), d_s[r, :n].reshape(shape), m_s[r, :n].reshape(shape),
                   v_s[r, :n].reshape(shape))

    order = ["norm1_w", "w_in", "dn_conv_w", "dn_A_log", "dn_dt_bias", "dn_norm_w", "w_proj_dn", "w_proj_sb",
             "w_out", "norm2_w", "ffn_w_up", "ffn_conv_w", "ffn_w_down", "norm_f_w"]
    outs = [loss, grad_x[None]]
    for k in range(4):
        outs += [res[nm][k] for nm in order]
    return tuple(outs)
```

```python
import functools

import jax
import jax.numpy as jnp
from jax import lax
from jax.experimental import pallas as pl
from jax.experimental.pallas import tpu as pltpu

F32 = jnp.float32
BF16 = jnp.bfloat16
MESH = pl.DeviceIdType.MESH

EPS = 1e-6
D_MODEL = 1024
N_HEADS = 8
HEAD_DIM = 128
DN_CONV = 4
DN_CHUNK = 64
D_FF = 2816
FFN_CONV = 3
ADAM_LR, ADAM_B1, ADAM_B2, ADAM_EPS, ADAM_WD, ADAM_STEP = 0.001, 0.9, 0.999, 1e-08, 0.01, 10

N_CHIPS = 4
LANES = 128
HALO = 8
VMEM_LIMIT = 48 * 1024 * 1024
SMALL_ROWS = 8


def _params(sem=None):
    return pltpu.CompilerParams(dimension_semantics=sem, vmem_limit_bytes=VMEM_LIMIT)


def _pick(n, target):
    best = None
    for b in range(LANES, min(n, target) + 1, LANES):
        if n % b == 0:
            best = b
    return best or n


ELEMENTWISE_COLS = 1408


def _rows(t, target=256):
    return min(t, target)


def _dot(a, b, precision=None):
    return lax.dot_general(a, b, (((1,), (0,)), ((), ())), precision=precision, preferred_element_type=F32)


def _dot_nt(a, b, precision=None):
    return lax.dot_general(a, b, (((1,), (1,)), ((), ())), precision=precision, preferred_element_type=F32)


def _dot_tn(a, b, precision=None):
    return lax.dot_general(a, b, (((0,), (0,)), ((), ())), precision=precision, preferred_element_type=F32)


def _rms(x, w):
    return x * lax.rsqrt(jnp.mean(x * x, axis=-1, keepdims=True) + EPS) * w


def _silu(x):
    return x * jax.nn.sigmoid(x)


def _softplus(x):
    return jnp.maximum(x, 0.0) + jnp.log(1.0 + jnp.exp(-jnp.abs(x)))


MM_BLOCK = 1408
MM_VMEM_BUDGET = 38 * 1024 * 1024


def _mm(a, b, *, ta=False, tb=False, add=None, out_dtype=F32, name, bm=MM_BLOCK, bn=MM_BLOCK, bk=MM_BLOCK):
    m = a.shape[1] if ta else a.shape[0]
    k = a.shape[0] if ta else a.shape[1]
    n = b.shape[0] if tb else b.shape[1]
    bm, bn = _pick(m, bm), _pick(n, bn)

    def vmem_need(bk_):
        need = 2 * (bm * bk_ * a.dtype.itemsize + bk_ * bn * b.dtype.itemsize) + 2 * bm * bn * jnp.dtype(out_dtype).itemsize
        need += 2 * bm * bn * add.dtype.itemsize if add is not None else 0
        return need + (bm * bn * 4 if bk_ < k else 0)

    bk = max((d for d in range(LANES, k + 1, LANES) if k % d == 0 and vmem_need(d) <= MM_VMEM_BUDGET),
             default=_pick(k, bk))
    nk = k // bk
    dims = (((0 if ta else 1,), (1 if tb else 0,)), ((), ()))

    def body(*refs):
        a_ref, b_ref = refs[:2]
        c_ref = refs[2] if add is not None else None
        o_ref = refs[3] if add is not None else refs[2]
        acc = refs[-1]
        kk = pl.program_id(2)
        part = lax.dot_general(a_ref[...].astype(BF16), b_ref[...].astype(BF16), dims, preferred_element_type=F32)

        def finish(r):
            if add is not None:
                r = r + c_ref[...].astype(F32)
            o_ref[...] = r.astype(out_dtype)

        if nk == 1:
            finish(part)
            return

        @pl.when(kk == 0)
        def _():
            acc[...] = part

        @pl.when(jnp.logical_and(kk > 0, kk < nk - 1))
        def _():
            acc[...] += part

        @pl.when(kk == nk - 1)
        def _():
            finish(acc[...] + part)

    a_spec = (pl.BlockSpec((bk, bm), lambda i, j, kk: (kk, i)) if ta
              else pl.BlockSpec((bm, bk), lambda i, j, kk: (i, kk)))
    b_spec = (pl.BlockSpec((bn, bk), lambda i, j, kk: (j, kk)) if tb
              else pl.BlockSpec((bk, bn), lambda i, j, kk: (kk, j)))
    o_spec = pl.BlockSpec((bm, bn), lambda i, j, kk: (i, j))
    in_specs = [a_spec, b_spec] + ([o_spec] if add is not None else [])
    args = (a, b) + ((add,) if add is not None else ())
    return pl.pallas_call(
        body, name=name, grid=(m // bm, n // bn, nk),
        in_specs=in_specs, out_specs=o_spec,
        out_shape=jax.ShapeDtypeStruct((m, n), out_dtype),
        scratch_shapes=[pltpu.VMEM((bm, bn), F32)] if nk > 1 else [],
        compiler_params=_params(("parallel", "parallel", "arbitrary")),
    )(*args)


def _norm1_fwd(x, w, w_ab):
    t = x.shape[0]
    tb = _rows(t)

    def body(x_ref, w_ref, wab_ref, n_ref, hab_ref):
        n = _rms(x_ref[...], w_ref[...]).astype(BF16)
        n_ref[...] = n
        hab_ref[...] = _dot(n, wab_ref[...])

    return pl.pallas_call(
        body, name="norm1_fwd", grid=(t // tb,),
        in_specs=[pl.BlockSpec((tb, D_MODEL), lambda i: (i, 0)),
                  pl.BlockSpec((1, D_MODEL), lambda i: (0, 0)),
                  pl.BlockSpec((D_MODEL, LANES), lambda i: (0, 0))],
        out_specs=[pl.BlockSpec((tb, D_MODEL), lambda i: (i, 0)),
                   pl.BlockSpec((tb, LANES), lambda i: (i, 0))],
        out_shape=[jax.ShapeDtypeStruct((t, D_MODEL), BF16), jax.ShapeDtypeStruct((t, LANES), F32)],
        compiler_params=_params(("arbitrary",)),
    )(x, w, w_ab)


def _norm1_bwd(x, w, dn, dres, dab, w_ab):
    t = x.shape[0]
    tb = _rows(t)

    def body(x_ref, w_ref, dn_ref, dres_ref, dab_ref, wab_ref, dx_ref, dw_ref):
        i = pl.program_id(0)
        g = dn_ref[...] + _dot_nt(dab_ref[...].astype(BF16), wab_ref[...])
        _, vjp = jax.vjp(_rms, x_ref[...], w_ref[...])
        dx, dw = vjp(g)
        dx_ref[...] = dres_ref[...] + dx

        @pl.when(i == 0)
        def _():
            dw_ref[...] = jnp.zeros_like(dw_ref)

        dw_ref[...] += dw

    row = pl.BlockSpec((tb, D_MODEL), lambda i: (i, 0))
    vec = pl.BlockSpec((1, D_MODEL), lambda i: (0, 0))
    return pl.pallas_call(
        body, name="norm1_bwd", grid=(t // tb,),
        in_specs=[row, vec, row, row, pl.BlockSpec((tb, LANES), lambda i: (i, 0)),
                  pl.BlockSpec((D_MODEL, LANES), lambda i: (0, 0))],
        out_specs=[row, vec],
        out_shape=[jax.ShapeDtypeStruct((t, D_MODEL), F32), jax.ShapeDtypeStruct((1, D_MODEL), F32)],
        compiler_params=_params(("arbitrary",)),
    )(x, w, dn, dres, dab, w_ab)


def _conv_fwd(x, w, name):
    t, c = x.shape
    kk = w.shape[0]
    tb, cb = _rows(t, 512), _pick(c, ELEMENTWISE_COLS)
    per = tb // HALO

    def body(x_ref, halo_ref, w_ref, y_ref, buf):
        i = pl.program_id(0)
        buf[pl.ds(HALO, tb), :] = x_ref[...]
        buf[pl.ds(0, HALO), :] = jnp.where(i == 0, 0.0, halo_ref[...])
        y_ref[...] = _conv_taps(buf, w_ref, HALO - (kk - 1), tb)

    return pl.pallas_call(
        body, name=name, grid=(t // tb, c // cb),
        in_specs=[pl.BlockSpec((tb, cb), lambda i, j: (i, j)),
                  pl.BlockSpec((HALO, cb), lambda i, j: (jnp.maximum(i * per - 1, 0), j)),
                  pl.BlockSpec((kk, cb), lambda i, j: (0, j))],
        out_specs=pl.BlockSpec((tb, cb), lambda i, j: (i, j)),
        out_shape=jax.ShapeDtypeStruct((t, c), F32),
        scratch_shapes=[pltpu.VMEM((tb + HALO, cb), F32)],
        compiler_params=_params(("parallel", "parallel")),
    )(x, x, w)


def _conv_bwd(dy, x, w, name, dx_dtype):
    t, c = x.shape
    kk = w.shape[0]
    tb, cb = _rows(t, 512), _pick(c, ELEMENTWISE_COLS)
    per = tb // HALO
    nblk = t // tb

    def body(dy_ref, after_ref, x_ref, w_ref, dx_ref, dw_ref, dbuf):
        i = pl.program_id(1)
        dbuf[pl.ds(0, tb), :] = dy_ref[...]
        dbuf[pl.ds(tb, HALO), :] = jnp.where(i == nblk - 1, 0.0, after_ref[...])

        @pl.when(i == 0)
        def _():
            dw_ref[...] = jnp.zeros_like(dw_ref)

        for j in range(cb // LANES):
            sl = pl.ds(j * LANES, LANES)
            x = x_ref[:, sl]
            dx = None
            for s in range(kk):
                shifted = dbuf[pl.ds(kk - 1 - s, tb), sl]
                term = w_ref[s:s + 1, sl] * shifted
                dx = term if dx is None else dx + term
                dw_ref[s:s + 1, sl] += jnp.sum(shifted * x, axis=0, keepdims=True)
            dx_ref[:, sl] = dx.astype(dx_dtype)

    blk = pl.BlockSpec((tb, cb), lambda j, i: (i, j))
    return pl.pallas_call(
        body, name=name, grid=(c // cb, nblk),
        in_specs=[blk,
                  pl.BlockSpec((HALO, cb), lambda j, i: (jnp.minimum((i + 1) * per, t // HALO - 1), j)),
                  blk,
                  pl.BlockSpec((kk, cb), lambda j, i: (0, j))],
        out_specs=[blk, pl.BlockSpec((HALO, cb), lambda j, i: (0, j))],
        out_shape=[jax.ShapeDtypeStruct((t, c), dx_dtype), jax.ShapeDtypeStruct((HALO, c), F32)],
        scratch_shapes=[pltpu.VMEM((tb + HALO, cb), F32)],
        compiler_params=_params(("parallel", "arbitrary")),
    )(dy, dy, x, w)


def _dn_head(c, normed):
    s = _silu(c)
    return s * lax.rsqrt(jnp.sum(s * s, axis=-1, keepdims=True) + EPS) if normed else s


def _dn_gates(hab, alog, dtb):
    lane = lax.broadcasted_iota(jnp.int32, hab.shape, 1)
    g = -jnp.exp(alog) * _softplus(hab + dtb)
    beta = jax.nn.sigmoid(hab)
    return jnp.where(lane < N_HEADS, g, jnp.where(lane < 2 * N_HEADS, beta, 0.0))


def _dn_head_slices(q_ref, k_ref, v_ref):
    return [(pl.ds((part * N_HEADS + h) * HEAD_DIM, HEAD_DIM), ref, h, part < 2)
            for part, ref in enumerate((q_ref, k_ref, v_ref)) for h in range(N_HEADS)]


def _dn_prep_fwd(c, hab, alog, dtb):
    t = c.shape[0]
    tb = _rows(t)

    def body(c_ref, hab_ref, alog_ref, dtb_ref, q_ref, k_ref, v_ref, gb_ref):
        for sl, ref, h, normed in _dn_head_slices(q_ref, k_ref, v_ref):
            ref[h] = _dn_head(c_ref[:, sl], normed)
        gb_ref[...] = _dn_gates(hab_ref[...], alog_ref[...], dtb_ref[...])

    hm = pl.BlockSpec((N_HEADS, tb, HEAD_DIM), lambda i: (0, i, 0))
    nar = pl.BlockSpec((tb, LANES), lambda i: (i, 0))
    vec = pl.BlockSpec((1, LANES), lambda i: (0, 0))
    return pl.pallas_call(
        body, name="dn_prep_fwd", grid=(t // tb,),
        in_specs=[pl.BlockSpec((tb, 3 * D_MODEL), lambda i: (i, 0)), nar, vec, vec],
        out_specs=[hm, hm, hm, nar],
        out_shape=[jax.ShapeDtypeStruct((N_HEADS, t, HEAD_DIM), F32)] * 3 + [jax.ShapeDtypeStruct((t, LANES), F32)],
        compiler_params=_params(("parallel",)),
    )(c, hab, alog, dtb)


def _dn_prep_bwd(c, hab, alog, dtb, dq, dk, dv, dgb):
    t = c.shape[0]
    tb = _rows(t)

    def body(c_ref, hab_ref, alog_ref, dtb_ref, dq_ref, dk_ref, dv_ref, dgb_ref,
             dc_ref, dhab_ref, dalog_ref, ddtb_ref):
        i = pl.program_id(0)
        for sl, ref, h, normed in _dn_head_slices(dq_ref, dk_ref, dv_ref):
            _, vjp = jax.vjp(functools.partial(_dn_head, normed=normed), c_ref[:, sl])
            dc_ref[:, sl] = vjp(ref[h])[0]
        _, vjp = jax.vjp(_dn_gates, hab_ref[...], alog_ref[...], dtb_ref[...])
        dhab, dalog, ddtb = vjp(dgb_ref[...])
        dhab_ref[...] = dhab

        @pl.when(i == 0)
        def _():
            dalog_ref[...] = jnp.zeros_like(dalog_ref)
            ddtb_ref[...] = jnp.zeros_like(ddtb_ref)

        dalog_ref[...] += dalog
        ddtb_ref[...] += ddtb

    hm = pl.BlockSpec((N_HEADS, tb, HEAD_DIM), lambda i: (0, i, 0))
    wide = pl.BlockSpec((tb, 3 * D_MODEL), lambda i: (i, 0))
    nar = pl.BlockSpec((tb, LANES), lambda i: (i, 0))
    vec = pl.BlockSpec((1, LANES), lambda i: (0, 0))
    return pl.pallas_call(
        body, name="dn_prep_bwd", grid=(t // tb,),
        in_specs=[wide, nar, vec, vec, hm, hm, hm, nar],
        out_specs=[wide, nar, vec, vec],
        out_shape=[jax.ShapeDtypeStruct((t, 3 * D_MODEL), F32), jax.ShapeDtypeStruct((t, LANES), F32),
                   jax.ShapeDtypeStruct((1, LANES), F32), jax.ShapeDtypeStruct((1, LANES), F32)],
        compiler_params=_params(("arbitrary",)),
    )(c, hab, alog, dtb, dq, dk, dv, dgb)


DN_PREC = lax.Precision.HIGH
DN_GROUP = 32


def _dn_prec(a):
    return DN_PREC if a.dtype == F32 else None


def _bdot(a, b):
    return lax.dot_general(a, b, (((2,), (1,)), ((0,), (0,))), precision=_dn_prec(a), preferred_element_type=F32)


def _bdot_nt(a, b):
    return lax.dot_general(a, b, (((2,), (2,)), ((0,), (0,))), precision=_dn_prec(a), preferred_element_type=F32)


def _bdot_tn(a, b):
    return lax.dot_general(a, b, (((1,), (1,)), ((0,), (0,))), precision=_dn_prec(a), preferred_element_type=F32)


def _unit_lower_inverse(lmat):
    c = lmat.shape[-1]
    ri = lax.broadcasted_iota(jnp.int32, (c, c), 0)
    ci = lax.broadcasted_iota(jnp.int32, (c, c), 1)
    p = -lmat
    tinv = jnp.where(ri == ci, 1.0, 0.0) + p
    for _ in range(max(c.bit_length() - 2, 0)):
        p = _bdot(p, p)
        tinv = tinv + _bdot(tinv, p)
    return tinv


@jax.custom_vjp
def _solve_with(lmat, rhs, tinv):
    return _bdot(tinv, rhs)


def _solve_with_fwd(lmat, rhs, tinv):
    sol = _bdot(tinv, rhs)
    return sol, (sol, tinv)


def _solve_with_bwd(res, dsol):
    sol, tinv = res
    drhs = _bdot_tn(tinv, dsol)
    return -_bdot_nt(drhs, sol), drhs, jnp.zeros_like(tinv)


_solve_with.defvjp(_solve_with_fwd, _solve_with_bwd)


def _dn_local(q, k, v, grow, brow, tinv):
    g, c, _ = q.shape
    ri = lax.broadcasted_iota(jnp.int32, (c, c), 0)
    ci = lax.broadcasted_iota(jnp.int32, (c, c), 1)
    lower = ri >= ci
    as_col = lambda r: jnp.sum(jnp.where(ri == ci, jnp.broadcast_to(r, (g, c, c)), 0.0), axis=2, keepdims=True)
    gcol, bcol = as_col(grow), as_col(brow)
    gc_col = jnp.sum(jnp.where(lower, jnp.broadcast_to(grow, (g, c, c)), 0.0), axis=2, keepdims=True)
    gc_row = jnp.sum(jnp.where(ri <= ci, jnp.broadcast_to(gcol, (g, c, c)), 0.0), axis=1, keepdims=True)
    qs = q * (HEAD_DIM ** -0.5)
    kb = k * bcol
    vb = v * bcol
    decay = jnp.where(lower, jnp.exp(jnp.where(lower, gc_col - gc_row, 0.0)), 0.0)
    lmat = jnp.where(ri > ci, _bdot_nt(kb.astype(BF16), k.astype(BF16)) * decay, 0.0)
    eg = jnp.exp(gc_col)
    rhs = jnp.concatenate([vb, kb * eg], axis=2)
    if tinv is None:
        tinv = _unit_lower_inverse(lmat)
    sol = _solve_with(lmat, rhs, tinv)
    a_qk = jnp.where(lower, _bdot_nt(qs.astype(BF16), k.astype(BF16)) * decay, 0.0)
    g_last = jnp.sum(grow, axis=2, keepdims=True)
    kdec = k * jnp.exp(g_last - gc_col)
    egl = jnp.broadcast_to(jnp.exp(g_last), (g, 1, HEAD_DIM))
    b16 = lambda x: x.astype(BF16)
    return sol[:, :, :HEAD_DIM], b16(sol[:, :, HEAD_DIM:]), b16(a_qk), b16(qs * eg), b16(kdec), egl, tinv


def _dn_seq(u, w, a_qk, qe, kdec, egl, s_in):
    b16 = lambda x: x.astype(BF16)
    v_new = u - _bdot(b16(w), b16(s_in))
    o = _bdot(b16(qe), b16(s_in)) + _bdot(b16(a_qk), b16(v_new))
    return o, s_in * egl + _bdot_tn(b16(kdec), b16(v_new))


def _dn_local_specs(t):
    grp = min(DN_GROUP, t // DN_CHUNK)
    rows = grp * DN_CHUNK
    blk = pl.BlockSpec((1, rows, HEAD_DIM), lambda h, i: (h, i, 0))
    row = pl.BlockSpec((1, grp, 1, DN_CHUNK), lambda h, i: (h, i, 0, 0))
    sq = pl.BlockSpec((1, grp, DN_CHUNK, DN_CHUNK), lambda h, i: (h, i, 0, 0))
    lane = pl.BlockSpec((1, grp, 1, HEAD_DIM), lambda h, i: (h, i, 0, 0))
    return grp, blk, row, sq, lane


def half(shape):
    return jax.ShapeDtypeStruct(shape.shape, BF16)


def _dn_shapes(t):
    nchunk = t // DN_CHUNK
    big = jax.ShapeDtypeStruct((N_HEADS, t, HEAD_DIM), F32)
    row = jax.ShapeDtypeStruct((N_HEADS, nchunk, 1, DN_CHUNK), F32)
    sq = jax.ShapeDtypeStruct((N_HEADS, nchunk, DN_CHUNK, DN_CHUNK), F32)
    lane = jax.ShapeDtypeStruct((N_HEADS, nchunk, 1, HEAD_DIM), F32)
    return big, row, sq, lane


def _dn_local_fwd(q, k, v, grow, brow, wire=()):
    t = q.shape[1]
    grp, blk, row, sq, lane = _dn_local_specs(t)
    big, _, sqs, lanes = _dn_shapes(t)
    n = len(wire)
    groups = t // (grp * DN_CHUNK)
    steps = N_HEADS * groups

    def body(q_ref, k_ref, v_ref, gr_ref, br_ref, *rest):
        u_ref, w_ref, a_ref, qe_ref, kd_ref, egl_ref, t_ref = rest[n:n + 7]
        if n:
            begin, middle, end = _gather_protocol(rest[:n], rest[n + 7:2 * n + 7], *rest[2 * n + 7:])
            step = pl.program_id(0) * groups + pl.program_id(1)
            pl.when(step == 0)(begin)
            pl.when(step == (GATHER_FORWARD_AT * steps) // 8)(middle)
        split = lambda r: r[0].reshape(grp, DN_CHUNK, HEAD_DIM)
        u, w, a_qk, qe, kdec, egl, tinv = _dn_local(split(q_ref), split(k_ref), split(v_ref), gr_ref[0],
                                                     br_ref[0], None)
        for ref, val in ((u_ref, u), (w_ref, w), (qe_ref, qe), (kd_ref, kdec)):
            ref[0] = val.reshape(grp * DN_CHUNK, HEAD_DIM)
        a_ref[0] = a_qk
        egl_ref[0] = egl
        t_ref[0] = tinv
        if n:
            pl.when(step == steps - 1)(end)

    assert n == 0 or steps >= 3
    return pl.pallas_call(
        body, name="dn_local_fwd", grid=(N_HEADS, groups),
        in_specs=[blk, blk, blk, row, row] + _hbm_specs(n),
        out_specs=[blk, blk, sq, blk, blk, lane, sq] + _hbm_specs(n),
        out_shape=[big, half(big), half(sqs), half(big), half(big), lanes, sqs] + _gather_out_shapes(wire),
        scratch_shapes=_gather_sems(n) if n else [],
        compiler_params=_params(("arbitrary", "arbitrary")),
    )(q, k, v, grow, brow, *wire)


def _dn_local_bwd(q, k, v, grow, brow, tinv, du, dw, da, dqe, dkd, degl):
    t = q.shape[1]
    grp, blk, row, sq, lane = _dn_local_specs(t)
    big, rows_, _, _ = _dn_shapes(t)

    def body(q_ref, k_ref, v_ref, gr_ref, br_ref, t_ref, du_ref, dw_ref, da_ref, dqe_ref, dkd_ref,
             degl_ref, dq_ref, dk_ref, dv_ref, dgr_ref, dbr_ref):
        split = lambda r: r[0].reshape(grp, DN_CHUNK, HEAD_DIM)
        tinv_v = t_ref[0]
        fn = lambda q_, k_, v_, gr_, br_: _dn_local(q_, k_, v_, gr_, br_, tinv_v)[:6]
        _, vjp = jax.vjp(fn, split(q_ref), split(k_ref), split(v_ref), gr_ref[0], br_ref[0])
        dq, dk, dv, dgr, dbr = vjp((split(du_ref), split(dw_ref), da_ref[0], split(dqe_ref), split(dkd_ref),
                                    degl_ref[0]))
        for ref, val in ((dq_ref, dq), (dk_ref, dk), (dv_ref, dv)):
            ref[0] = val.reshape(grp * DN_CHUNK, HEAD_DIM)
        dgr_ref[0] = dgr
        dbr_ref[0] = dbr

    return pl.pallas_call(
        body, name="dn_local_bwd", grid=(N_HEADS, t // (grp * DN_CHUNK)),
        in_specs=[blk, blk, blk, row, row, sq, blk, blk, sq, blk, blk, lane],
        out_specs=[blk, blk, blk, row, row],
        out_shape=[big, big, big, rows_, rows_],
        compiler_params=_params(("parallel", "parallel")),
    )(q, k, v, grow, brow, tinv, du, dw, da, dqe, dkd, degl)


DN_SEQ_CHUNKS = 4


def _dn_seq_specs(nchunk, rev):
    per = min(DN_SEQ_CHUNKS, nchunk)
    nstep = nchunk // per

    def idx(n):
        return nstep - 1 - n if rev else n

    blk = pl.BlockSpec((N_HEADS, per * DN_CHUNK, HEAD_DIM), lambda n: (0, idx(n), 0))
    sq = pl.BlockSpec((N_HEADS, per, DN_CHUNK, DN_CHUNK), lambda n: (0, idx(n), 0, 0))
    lane = pl.BlockSpec((N_HEADS, per, 1, HEAD_DIM), lambda n: (0, idx(n), 0, 0))
    st = pl.BlockSpec((N_HEADS, per, HEAD_DIM, HEAD_DIM), lambda n: (0, idx(n), 0, 0))
    return per, nstep, blk, sq, lane, st


def _dn_seq_fwd(u, w, a_qk, qe, kdec, egl):
    t = u.shape[1]
    nchunk = t // DN_CHUNK
    per, nstep, blk, sq, lane, st = _dn_seq_specs(nchunk, False)

    def body(u_ref, w_ref, a_ref, qe_ref, kd_ref, egl_ref, o_ref, s_ref, state):
        @pl.when(pl.program_id(0) == 0)
        def _():
            state[...] = jnp.zeros_like(state)

        for c in range(per):
            rows = pl.ds(c * DN_CHUNK, DN_CHUNK)
            s_in = state[...]
            s_ref[:, c] = s_in.astype(BF16)
            o_ref[:, rows], state[...] = _dn_seq(u_ref[:, rows], w_ref[:, rows], a_ref[:, c], qe_ref[:, rows],
                                                 kd_ref[:, rows], egl_ref[:, c], s_in)

    return pl.pallas_call(
        body, name="dn_seq_fwd", grid=(nstep,),
        in_specs=[blk, blk, sq, blk, blk, lane],
        out_specs=[blk, st],
        out_shape=[jax.ShapeDtypeStruct((N_HEADS, t, HEAD_DIM), F32),
                   jax.ShapeDtypeStruct((N_HEADS, nchunk, HEAD_DIM, HEAD_DIM), BF16)],
        scratch_shapes=[pltpu.VMEM((N_HEADS, HEAD_DIM, HEAD_DIM), F32)],
        compiler_params=_params(("arbitrary",)),
    )(u, w, a_qk, qe, kdec, egl)


def _dn_seq_bwd(u, w, a_qk, qe, kdec, egl, states, do):
    t = u.shape[1]
    nchunk = t // DN_CHUNK
    per, nstep, blk, sq, lane, st = _dn_seq_specs(nchunk, True)
    big, _, sqs, lanes = _dn_shapes(t)

    def body(u_ref, w_ref, a_ref, qe_ref, kd_ref, egl_ref, s_ref, do_ref,
             du_ref, dw_ref, da_ref, dqe_ref, dkd_ref, degl_ref, dstate):
        @pl.when(pl.program_id(0) == 0)
        def _():
            dstate[...] = jnp.zeros_like(dstate)

        for c in reversed(range(per)):
            rows = pl.ds(c * DN_CHUNK, DN_CHUNK)
            _, vjp = jax.vjp(_dn_seq, u_ref[:, rows], w_ref[:, rows], a_ref[:, c], qe_ref[:, rows], kd_ref[:, rows],
                             egl_ref[:, c], s_ref[:, c].astype(F32))
            (du_ref[:, rows], dw_ref[:, rows], da_ref[:, c], dqe_ref[:, rows], dkd_ref[:, rows], degl_ref[:, c],
             dstate[...]) = vjp((do_ref[:, rows], dstate[...]))

    return pl.pallas_call(
        body, name="dn_seq_bwd", grid=(nstep,),
        in_specs=[blk, blk, sq, blk, blk, lane, st, blk],
        out_specs=[blk, blk, sq, blk, blk, lane],
        out_shape=[big, half(big), half(sqs), half(big), half(big), lanes],
        scratch_shapes=[pltpu.VMEM((N_HEADS, HEAD_DIM, HEAD_DIM), F32)],
        compiler_params=_params(("arbitrary",)),
    )(u, w, a_qk, qe, kdec, egl, states, do)


def _dn_post_head(o, gate, w):
    return _rms(o, w) * _silu(gate)


def _dn_post_fwd(o, gate, w):
    t = gate.shape[0]
    tb = _rows(t)

    def body(o_ref, g_ref, w_ref, y_ref):
        for h in range(N_HEADS):
            sl = pl.ds(h * HEAD_DIM, HEAD_DIM)
            y_ref[:, sl] = _dn_post_head(o_ref[h], g_ref[:, sl], w_ref[...]).astype(BF16)

    row = pl.BlockSpec((tb, D_MODEL), lambda i: (i, 0))
    hm = pl.BlockSpec((N_HEADS, tb, HEAD_DIM), lambda i: (0, i, 0))
    return pl.pallas_call(
        body, name="dn_post_fwd", grid=(t // tb,),
        in_specs=[hm, row, pl.BlockSpec((1, HEAD_DIM), lambda i: (0, 0))],
        out_specs=row, out_shape=jax.ShapeDtypeStruct((t, D_MODEL), BF16),
        compiler_params=_params(("parallel",)),
    )(o, gate, w)


def _dn_post_bwd(o, gate, w, dy):
    t = gate.shape[0]
    tb = _rows(t)

    def body(o_ref, g_ref, w_ref, dy_ref, do_ref, dg_ref, dw_ref):
        i = pl.program_id(0)
        @pl.when(i == 0)
        def _():
            dw_ref[...] = jnp.zeros_like(dw_ref)

        for h in range(N_HEADS):
            sl = pl.ds(h * HEAD_DIM, HEAD_DIM)
            _, vjp = jax.vjp(_dn_post_head, o_ref[h], g_ref[:, sl], w_ref[...])
            do_ref[h], dg, dw = vjp(dy_ref[:, sl])
            dg_ref[:, sl] = dg.astype(BF16)
            dw_ref[...] += dw

    row = pl.BlockSpec((tb, D_MODEL), lambda i: (i, 0))
    hm = pl.BlockSpec((N_HEADS, tb, HEAD_DIM), lambda i: (0, i, 0))
    vec = pl.BlockSpec((1, HEAD_DIM), lambda i: (0, 0))
    return pl.pallas_call(
        body, name="dn_post_bwd", grid=(t // tb,),
        in_specs=[hm, row, vec, row],
        out_specs=[hm, row, vec],
        out_shape=[jax.ShapeDtypeStruct((N_HEADS, t, HEAD_DIM), F32), jax.ShapeDtypeStruct((t, D_MODEL), BF16),
                   jax.ShapeDtypeStruct((1, HEAD_DIM), F32)],
        compiler_params=_params(("arbitrary",)),
    )(o, gate, w, dy)


def _split_bf16(x):
    hi = x.astype(BF16)
    lo = (x - hi.astype(F32)).astype(BF16)
    return hi, lo


SB_Q_BLOCK = 512
SB_K_BLOCK = 256
SB_NEGLIGIBLE = -60.0


def _sb_logits(q, kb, mask, scale):
    z = _dot_nt(q, kb) * scale
    ls = jnp.minimum(z, 0.0) - jnp.log(1.0 + jnp.exp(-jnp.abs(z)))
    lk = ls - z
    if mask is not None:
        lk = jnp.where(mask, lk, 0.0)
    return ls, lk


def _sb_blocks(t):
    bq = min(SB_Q_BLOCK, t)
    bk = min(SB_K_BLOCK, bq)
    return bq, bk, bq // bk


def _sb_fwd(qkv):
    t = qkv.shape[0]
    bq, bk, nd = _sb_blocks(t)
    scale = HEAD_DIM ** -0.5

    def body(q_ref, k_ref, v_ref, o_ref, tot_ref, used_ref):
        i = pl.program_id(1)
        q = q_ref[...]
        rj = lax.broadcasted_iota(jnp.int32, (bk, bk), 0)
        cj = lax.broadcasted_iota(jnp.int32, (bk, bk), 1)
        after = (rj > cj).astype(BF16)
        trow = lax.broadcasted_iota(jnp.int32, (bq, bk), 0)
        scol = lax.broadcasted_iota(jnp.int32, (bq, bk), 1)

        def tile(j, run, acc, mask):
            off = pl.multiple_of(j * bk, bk)
            kb = k_ref[pl.ds(off, bk), :]
            vb = v_ref[pl.ds(off, bk), :]
            ls, lk = _sb_logits(q, kb, mask, scale)
            hi, lo = _split_bf16(lk)
            between = _dot(hi, after) + _dot(lo, after) + run
            a = jnp.exp(ls + between)
            if mask is not None:
                a = jnp.where(mask, a, 0.0)
            acc = acc + _dot(a.astype(BF16), vb)
            return run + jnp.sum(lk, axis=1, keepdims=True), acc

        run, acc = jnp.zeros((bq, 1), F32), jnp.zeros((bq, HEAD_DIM), F32)
        for d in reversed(range(nd)):
            run, acc = tile(i * nd + d, run, acc, scol + d * bk < trow)
        def more(c):
            return jnp.logical_and(c[0] < i * nd, jnp.max(c[1]) > SB_NEGLIGIBLE)

        def far(c):
            run_, acc_ = tile(i * nd - 1 - c[0], c[1], c[2], None)
            return c[0] + 1, run_, acc_

        used, run, acc = lax.while_loop(more, far, (jnp.int32(0), run, acc))
        o_ref[...] = acc.astype(BF16)
        tot_ref[...] = jnp.broadcast_to(run, (bq, HEAD_DIM))
        used_ref[...] = jnp.full(used_ref.shape, used, F32)

    qs = pl.BlockSpec((bq, HEAD_DIM), lambda h, i: (i, h))
    ks = pl.BlockSpec((t, HEAD_DIM), lambda h, i: (0, N_HEADS + h))
    vs = pl.BlockSpec((t, HEAD_DIM), lambda h, i: (0, 2 * N_HEADS + h))
    return pl.pallas_call(
        body, name="sb_fwd", grid=(N_HEADS, t // bq),
        in_specs=[qs, ks, vs], out_specs=[qs, qs, pl.BlockSpec((1, 1, 1, LANES), lambda h, i: (h, i, 0, 0))],
        out_shape=[jax.ShapeDtypeStruct((t, D_MODEL), BF16), jax.ShapeDtypeStruct((t, D_MODEL), F32),
                   jax.ShapeDtypeStruct((N_HEADS, t // bq, 1, LANES), F32)],
        compiler_params=_params(("parallel", "arbitrary")),
    )(qkv, qkv, qkv)


def _sb_bwd(qkv, tot, used, do, partials=()):
    t = qkv.shape[0]
    bq, bk, nd = _sb_blocks(t)
    scale = HEAD_DIM ** -0.5
    n = len(partials)
    nq = t // bq

    def body(q_ref, k_ref, v_ref, tot_ref, used_ref, do_ref, *rest):
        dq_ref, dk_ref, dv_ref = rest[n:n + 3]
        i = pl.program_id(1)
        if n:
            begin, end = _chip_exchange_protocol(rest[:n], rest[n + 3:2 * n + 3], *rest[2 * n + 3:])
            step = pl.program_id(0) * nq + i
            pl.when(step == 0)(begin)

        @pl.when(i == 0)
        def _():
            dk_ref[...] = jnp.zeros_like(dk_ref)
            dv_ref[...] = jnp.zeros_like(dv_ref)

        q = q_ref[...]
        do = do_ref[...]
        total = tot_ref[:, 0:1]
        rj = lax.broadcasted_iota(jnp.int32, (bk, bk), 0)
        cj = lax.broadcasted_iota(jnp.int32, (bk, bk), 1)
        upto = (rj <= cj).astype(BF16)
        before = (rj < cj).astype(BF16)
        trow = lax.broadcasted_iota(jnp.int32, (bq, bk), 0)
        scol = lax.broadcasted_iota(jnp.int32, (bq, bk), 1)

        def tile(j, run_k, run_e, dq, mask):
            off = pl.multiple_of(j * bk, bk)
            kb = k_ref[pl.ds(off, bk), :]
            vb = v_ref[pl.ds(off, bk), :]
            ls, lk = _sb_logits(q, kb, mask, scale)
            hi, lo = _split_bf16(lk)
            between = total - (_dot(hi, upto) + _dot(lo, upto) + run_k)
            a = jnp.exp(ls + between)
            if mask is not None:
                a = jnp.where(mask, a, 0.0)
            e = a * _dot_nt(do, vb)
            ehi, elo = _split_bf16(e)
            pre = _dot(ehi, before) + _dot(elo, before) + run_e
            sig = jnp.exp(ls)
            dz = e * (1.0 - sig) - pre * sig
            if mask is not None:
                dz = jnp.where(mask, dz, 0.0)
            dz = (dz * scale).astype(BF16)
            dq = dq + _dot(dz, kb)
            dk_ref[pl.ds(off, bk), :] += _dot_tn(dz, q)
            dv_ref[pl.ds(off, bk), :] += _dot_tn(a.astype(BF16), do)
            return (run_k + jnp.sum(lk, axis=1, keepdims=True),
                    run_e + jnp.sum(e, axis=1, keepdims=True), dq)

        zero = jnp.zeros((bq, 1), F32)
        visited = jnp.clip(jnp.max(used_ref[...]).astype(jnp.int32), 0, i * nd)
        carry = lax.fori_loop(i * nd - visited, i * nd, lambda j, c: tile(j, c[0], c[1], c[2], None),
                              (zero, zero, jnp.zeros((bq, HEAD_DIM), F32)))
        for d in range(nd):
            carry = tile(i * nd + d, *carry, scol + d * bk < trow)
        dq_ref[...] = carry[2]
        if n:
            pl.when(step == N_HEADS * nq - 1)(end)

    qs = pl.BlockSpec((bq, HEAD_DIM), lambda h, i: (i, h))
    ks = pl.BlockSpec((t, HEAD_DIM), lambda h, i: (0, N_HEADS + h))
    vs = pl.BlockSpec((t, HEAD_DIM), lambda h, i: (0, 2 * N_HEADS + h))
    full = pl.BlockSpec((t, HEAD_DIM), lambda h, i: (0, h))
    big = jax.ShapeDtypeStruct((t, D_MODEL), F32)
    return pl.pallas_call(
        body, name="sb_bwd", grid=(N_HEADS, nq),
        in_specs=[qs, ks, vs, qs, pl.BlockSpec((1, 1, 1, LANES), lambda h, i: (h, i, 0, 0)), qs] + _hbm_specs(n),
        out_specs=[qs, full, full] + _hbm_specs(n),
        out_shape=[big, big, big] + _chip_exchange_shapes(partials),
        scratch_shapes=_chip_exchange_sems(n) if n else [],
        compiler_params=_params(("arbitrary", "arbitrary")),
    )(qkv, qkv, qkv, tot, used, do, *partials)


def _merge_fwd(o_dn, o_sb, gl, x, wp_dn, wp_sb, w_out, w2):
    t = x.shape[0]
    tb = _rows(t)

    def body(odn_ref, osb_ref, gl_ref, x_ref, wpd_ref, wps_ref, wo_ref, w2_ref,
             pdn_ref, psb_ref, mix_ref, x1_ref, n2_ref):
        pdn = _dot(odn_ref[...], wpd_ref[...])
        psb = _dot(osb_ref[...], wps_ref[...])
        gates = jax.nn.sigmoid(gl_ref[...])
        mixed = (gates[:, :D_MODEL] * pdn + gates[:, D_MODEL:] * psb).astype(BF16)
        x1 = x_ref[...] + _dot(mixed, wo_ref[...])
        pdn_ref[...] = pdn.astype(BF16)
        psb_ref[...] = psb.astype(BF16)
        mix_ref[...] = mixed
        x1_ref[...] = x1
        n2_ref[...] = _rms(x1, w2_ref[...]).astype(BF16)

    row = pl.BlockSpec((tb, D_MODEL), lambda i: (i, 0))
    sq = pl.BlockSpec((D_MODEL, D_MODEL), lambda i: (0, 0))
    f = jax.ShapeDtypeStruct((t, D_MODEL), F32)
    b = jax.ShapeDtypeStruct((t, D_MODEL), BF16)
    return pl.pallas_call(
        body, name="merge_fwd", grid=(t // tb,),
        in_specs=[row, row, pl.BlockSpec((tb, 2 * D_MODEL), lambda i: (i, 0)), row, sq, sq, sq,
                  pl.BlockSpec((1, D_MODEL), lambda i: (0, 0))],
        out_specs=[row] * 5, out_shape=[b, b, b, f, b],
        compiler_params=_params(("parallel",)),
    )(o_dn, o_sb, gl, x, wp_dn, wp_sb, w_out, w2)


def _merge_bwd(dx2, dn2, x1, w2, gl, pdn, psb, wp_dn, wp_sb, w_out):
    t = x1.shape[0]
    tb = _rows(t)

    def body(dx2_ref, dn2_ref, x1_ref, w2_ref, gl_ref, pdn_ref, psb_ref, wpd_ref, wps_ref, wo_ref,
             dx1_ref, dw2_ref, dgl_ref, dpdn_ref, dpsb_ref, dodn_ref, dosb_ref):
        i = pl.program_id(0)
        _, vjp = jax.vjp(_rms, x1_ref[...], w2_ref[...])
        dxn, dw2 = vjp(dn2_ref[...])
        dx1 = dx2_ref[...] + dxn
        dx1_ref[...] = dx1

        @pl.when(i == 0)
        def _():
            dw2_ref[...] = jnp.zeros_like(dw2_ref)

        dw2_ref[...] += dw2
        dmix = _dot_nt(dx1.astype(BF16), wo_ref[...])
        gates = jax.nn.sigmoid(gl_ref[...])
        g_dn, g_sb = gates[:, :D_MODEL], gates[:, D_MODEL:]
        dpdn = (dmix * g_dn).astype(BF16)
        dpsb = (dmix * g_sb).astype(BF16)
        dgl_ref[:, :D_MODEL] = (dmix * pdn_ref[...].astype(F32) * g_dn * (1.0 - g_dn)).astype(BF16)
        dgl_ref[:, D_MODEL:] = (dmix * psb_ref[...].astype(F32) * g_sb * (1.0 - g_sb)).astype(BF16)
        dpdn_ref[...] = dpdn
        dpsb_ref[...] = dpsb
        dodn_ref[...] = _dot_nt(dpdn, wpd_ref[...])
        dosb_ref[...] = _dot_nt(dpsb, wps_ref[...]).astype(BF16)

    row = pl.BlockSpec((tb, D_MODEL), lambda i: (i, 0))
    wide = pl.BlockSpec((tb, 2 * D_MODEL), lambda i: (i, 0))
    sq = pl.BlockSpec((D_MODEL, D_MODEL), lambda i: (0, 0))
    vec = pl.BlockSpec((1, D_MODEL), lambda i: (0, 0))
    f = jax.ShapeDtypeStruct((t, D_MODEL), F32)
    b = jax.ShapeDtypeStruct((t, D_MODEL), BF16)
    return pl.pallas_call(
        body, name="merge_bwd", grid=(t // tb,),
        in_specs=[row, row, row, vec, wide, row, row, sq, sq, sq],
        out_specs=[row, vec, wide, row, row, row, row],
        out_shape=[f, jax.ShapeDtypeStruct((1, D_MODEL), F32), jax.ShapeDtypeStruct((t, 2 * D_MODEL), BF16),
                   b, b, f, b],
        compiler_params=_params(("arbitrary",)),
    )(dx2, dn2, x1, w2, gl, pdn, psb, wp_dn, wp_sb, w_out)


def _conv_taps(buf, w_ref, first, rows, cols=slice(None)):
    y = w_ref[0:1, cols] * buf[pl.ds(first, rows), cols]
    for s in range(1, w_ref.shape[0]):
        y = y + w_ref[s:s + 1, cols] * buf[pl.ds(first + s, rows), cols]
    return y


def _ffn_mid_fwd(pre_g, pre_u, wg, wu):
    t, c = pre_g.shape
    kk = wg.shape[0]
    tb, cb = _rows(t), _pick(c, ELEMENTWISE_COLS)
    per = tb // HALO

    def body(g_ref, gh_ref, u_ref, uh_ref, wg_ref, wu_ref, a_ref, gbuf, ubuf):
        i = pl.program_id(0)
        for buf, ref, halo in ((gbuf, g_ref, gh_ref), (ubuf, u_ref, uh_ref)):
            buf[pl.ds(HALO, tb), :] = ref[...]
            buf[pl.ds(0, HALO), :] = jnp.where(i == 0, 0.0, halo[...])
        for j in range(cb // LANES):
            sl = pl.ds(j * LANES, LANES)
            ug = _conv_taps(gbuf, wg_ref, HALO - (kk - 1), tb, sl)
            uu = _conv_taps(ubuf, wu_ref, HALO - (kk - 1), tb, sl)
            a_ref[:, sl] = (_silu(ug) * uu).astype(BF16)

    blk = pl.BlockSpec((tb, cb), lambda i, j: (i, j))
    halo = pl.BlockSpec((HALO, cb), lambda i, j: (jnp.maximum(i * per - 1, 0), j))
    wspec = pl.BlockSpec((kk, cb), lambda i, j: (0, j))
    return pl.pallas_call(
        body, name="ffn_mid_fwd", grid=(t // tb, c // cb),
        in_specs=[blk, halo, blk, halo, wspec, wspec], out_specs=blk,
        out_shape=jax.ShapeDtypeStruct((t, c), BF16),
        scratch_shapes=[pltpu.VMEM((tb + HALO, cb), F32)] * 2,
        compiler_params=_params(("parallel", "parallel")),
    )(pre_g, pre_g, pre_u, pre_u, wg, wu)


def _ffn_mid_bwd(pre_g, pre_u, wg, wu, da):
    t, c = pre_g.shape
    kk = wg.shape[0]
    tb, cb = _rows(t), _pick(c, ELEMENTWISE_COLS)
    per = tb // HALO
    nblk = t // tb
    ext = tb + HALO

    def body(g_ref, gb_ref, ga_ref, u_ref, ub_ref, ua_ref, da_ref, daa_ref, wg_ref, wu_ref,
             dg_ref, du_ref, dwg_ref, dwu_ref, gbuf, ubuf, dabuf, dgbuf, dubuf):
        i = pl.program_id(1)
        last = i == nblk - 1
        for buf, ref, before, after in ((gbuf, g_ref, gb_ref, ga_ref), (ubuf, u_ref, ub_ref, ua_ref)):
            buf[pl.ds(0, HALO), :] = jnp.where(i == 0, 0.0, before[...])
            buf[pl.ds(HALO, tb), :] = ref[...]
            buf[pl.ds(HALO + tb, HALO), :] = jnp.where(last, 0.0, after[...])
        dabuf[pl.ds(0, tb), :] = da_ref[...]
        dabuf[pl.ds(tb, HALO), :] = jnp.where(last, 0.0, daa_ref[...])

        @pl.when(i == 0)
        def _():
            dwg_ref[...] = jnp.zeros_like(dwg_ref)
            dwu_ref[...] = jnp.zeros_like(dwu_ref)

        for j in range(cb // LANES):
            sl = pl.ds(j * LANES, LANES)
            ug = _conv_taps(gbuf, wg_ref, HALO - (kk - 1), ext, sl)
            uu = _conv_taps(ubuf, wu_ref, HALO - (kk - 1), ext, sl)
            _, vjp = jax.vjp(lambda g, u: _silu(g) * u, ug, uu)
            dgbuf[:, sl], dubuf[:, sl] = vjp(dabuf[:, sl])
            for dbuf, xbuf, w_ref, dx_ref, dw_ref in ((dgbuf, gbuf, wg_ref, dg_ref, dwg_ref),
                                                      (dubuf, ubuf, wu_ref, du_ref, dwu_ref)):
                x = xbuf[pl.ds(HALO, tb), sl]
                dx = None
                for s in range(kk):
                    shifted = dbuf[pl.ds(kk - 1 - s, tb), sl]
                    term = w_ref[s:s + 1, sl] * shifted
                    dx = term if dx is None else dx + term
                    dw_ref[s:s + 1, sl] += jnp.sum(shifted * x, axis=0, keepdims=True)
                dx_ref[:, sl] = dx.astype(BF16)

    blk = pl.BlockSpec((tb, cb), lambda j, i: (i, j))
    before = pl.BlockSpec((HALO, cb), lambda j, i: (jnp.maximum(i * per - 1, 0), j))
    after = pl.BlockSpec((HALO, cb), lambda j, i: (jnp.minimum((i + 1) * per, t // HALO - 1), j))
    wspec = pl.BlockSpec((kk, cb), lambda j, i: (0, j))
    dwspec = pl.BlockSpec((HALO, cb), lambda j, i: (0, j))
    half = jax.ShapeDtypeStruct((t, c), BF16)
    dwshape = jax.ShapeDtypeStruct((HALO, c), F32)
    return pl.pallas_call(
        body, name="ffn_mid_bwd", grid=(c // cb, nblk),
        in_specs=[blk, before, after, blk, before, after, blk, after, wspec, wspec],
        out_specs=[blk, blk, dwspec, dwspec],
        out_shape=[half, half, dwshape, dwshape],
        scratch_shapes=[pltpu.VMEM((ext + HALO, cb), F32)] * 2 + [pltpu.VMEM((ext, cb), F32)] * 3,
        compiler_params=_params(("parallel", "arbitrary")),
    )(pre_g, pre_g, pre_g, pre_u, pre_u, pre_u, da, da, wg, wu)


def _down_loss(a, w_down, x1, wf, target):
    t = x1.shape[0]
    tb = _rows(t)

    def body(a_ref, wd_ref, x1_ref, wf_ref, tgt_ref, dx2_ref, dwf_ref, loss_ref):
        i = pl.program_id(0)
        x2 = x1_ref[...] + _dot(a_ref[...], wd_ref[...])
        y, vjp = jax.vjp(_rms, x2, wf_ref[...])
        err = y - tgt_ref[...]
        dx2, dwf = vjp(err * (1.0 / D_MODEL))
        dx2_ref[...] = dx2
        part = jnp.sum(jnp.sum(err * err, axis=1, keepdims=True), axis=0, keepdims=True) * (0.5 / D_MODEL)

        @pl.when(i == 0)
        def _():
            dwf_ref[...] = jnp.zeros_like(dwf_ref)
            loss_ref[...] = jnp.zeros_like(loss_ref)

        dwf_ref[...] += dwf
        loss_ref[...] += jnp.broadcast_to(part, loss_ref.shape)

    row = pl.BlockSpec((tb, D_MODEL), lambda i: (i, 0))
    vec = pl.BlockSpec((1, D_MODEL), lambda i: (0, 0))
    return pl.pallas_call(
        body, name="down_loss", grid=(t // tb,),
        in_specs=[pl.BlockSpec((tb, D_FF), lambda i: (i, 0)), pl.BlockSpec((D_FF, D_MODEL), lambda i: (0, 0)),
                  row, vec, row],
        out_specs=[row, vec, pl.BlockSpec((1, LANES), lambda i: (0, 0))],
        out_shape=[jax.ShapeDtypeStruct((t, D_MODEL), F32), jax.ShapeDtypeStruct((1, D_MODEL), F32),
                   jax.ShapeDtypeStruct((1, LANES), F32)],
        compiler_params=_params(("arbitrary",)),
    )(a, w_down, x1, wf, target)


def _local_step(x, target, wts, late_wire=(), late_weights=None, early_partials=None):
    t = x.shape[0]
    nchunk = t // DN_CHUNK

    n1, hab = _norm1_fwd(x, wts["norm1"], wts["w_ab"])
    dnqkv = _mm(n1, wts["w_dnqkv"], name="h_dnqkv")
    dngate = _mm(n1, wts["w_dngate"], name="h_dngate")
    sbqkv = _mm(n1, wts["w_sbqkv"], out_dtype=BF16, name="h_sbqkv")
    gl = _mm(n1, wts["w_gl"], name="h_gl")

    cdn = _conv_fwd(dnqkv, wts["dn_conv"], "dn_conv_fwd")
    qn, kn, vv, gb = _dn_prep_fwd(cdn, hab, wts["alog"], wts["dtb"])
    per_head = gb[:, :2 * N_HEADS].T.reshape(2 * N_HEADS, nchunk, DN_CHUNK)
    grow, brow = per_head[:N_HEADS, :, None, :], per_head[N_HEADS:, :, None, :]
    u_dn, w_dn, a_qk, qe, kdec, egl, tinv, *late = _dn_local_fwd(qn, kn, vv, grow, brow, late_wire)
    if late_wire:
        wts = {**wts, **late_weights(late)}
    o_raw, states = _dn_seq_fwd(u_dn, w_dn, a_qk, qe, kdec, egl)
    o_dn = _dn_post_fwd(o_raw, dngate, wts["dn_norm"])

    o_sb, tot, sb_used = _sb_fwd(sbqkv)

    pdn, psb, mixed, x1, n2 = _merge_fwd(o_dn, o_sb, gl, x, wts["wp_dn"], wts["wp_sb"], wts["w_out"],
                                         wts["norm2"])
    pre_g = _mm(n2, wts["w_up_g"], name="ffn_up_g")
    pre_u = _mm(n2, wts["w_up_u"], name="ffn_up_u")
    act = _ffn_mid_fwd(pre_g, pre_u, wts["ffn_conv_g"], wts["ffn_conv_u"])
    dx2, d_normf, loss_part = _down_loss(act, wts["w_down"], x1, wts["normf"], target)

    grads = {"normf": d_normf}
    da = _mm(dx2, wts["w_down"], tb=True, name="d_act")
    grads["w_down"] = _mm(act, dx2, ta=True, out_dtype=BF16, name="dw_down")
    dpre_g, dpre_u, dcw_g, dcw_u = _ffn_mid_bwd(pre_g, pre_u, wts["ffn_conv_g"], wts["ffn_conv_u"], da)
    grads["ffn_conv"] = jnp.concatenate([dcw_g[:FFN_CONV], dcw_u[:FFN_CONV]], axis=1)
    dn2 = _mm(dpre_g, wts["w_up_g"], tb=True, name="dn2_g")
    dn2 = _mm(dpre_u, wts["w_up_u"], tb=True, add=dn2, name="dn2_u")
    grads["w_up_g"] = _mm(n2, dpre_g, ta=True, out_dtype=BF16, name="dw_up_g")
    grads["w_up_u"] = _mm(n2, dpre_u, ta=True, out_dtype=BF16, name="dw_up_u")

    dx1, grads["norm2"], dgl, dpdn, dpsb, do_dn, do_sb = _merge_bwd(
        dx2, dn2, x1, wts["norm2"], gl, pdn, psb, wts["wp_dn"], wts["wp_sb"], wts["w_out"])
    grads["w_out"] = _mm(mixed, dx1, ta=True, out_dtype=BF16, name="dw_out")
    grads["wp_dn"] = _mm(o_dn, dpdn, ta=True, out_dtype=BF16, name="dw_proj_dn")
    grads["wp_sb"] = _mm(o_sb, dpsb, ta=True, out_dtype=BF16, name="dw_proj_sb")

    partials = early_partials(grads) if early_partials else ()
    dsq, dsk, dsv, *arrived = _sb_bwd(sbqkv, tot, sb_used, do_sb, partials)
    dsbqkv = jnp.concatenate([dsq, dsk, dsv], axis=1).astype(BF16)

    do_raw, ddngate, grads["dn_norm"] = _dn_post_bwd(o_raw, dngate, wts["dn_norm"], do_dn)
    seq_grads = _dn_seq_bwd(u_dn, w_dn, a_qk, qe, kdec, egl, states, do_raw)
    dqn, dkn, dvv, dgrow, dbrow = _dn_local_bwd(qn, kn, vv, grow, brow, tinv, *seq_grads)
    dgb = jnp.concatenate([dgrow.reshape(N_HEADS, t), dbrow.reshape(N_HEADS, t)], axis=0).T
    dgb = jnp.pad(dgb, ((0, 0), (0, LANES - 2 * N_HEADS)))
    dcdn, dhab, grads["alog"], grads["dtb"] = _dn_prep_bwd(cdn, hab, wts["alog"], wts["dtb"], dqn, dkn, dvv, dgb)
    ddnqkv, dcw_dn = _conv_bwd(dcdn, dnqkv, wts["dn_conv"], "dn_conv_bwd", BF16)
    grads["dn_conv"] = dcw_dn[:DN_CONV]

    dn1 = _mm(ddnqkv, wts["w_dnqkv"], tb=True, name="dn1_dnqkv")
    dn1 = _mm(ddngate, wts["w_dngate"], tb=True, add=dn1, name="dn1_dngate")
    dn1 = _mm(dsbqkv, wts["w_sbqkv"], tb=True, add=dn1, name="dn1_sbqkv")
    dn1 = _mm(dgl, wts["w_gl"], tb=True, add=dn1, name="dn1_gl")
    grads["w_dnqkv"] = _mm(n1, ddnqkv, ta=True, out_dtype=BF16, name="dw_dnqkv")
    grads["w_dngate"] = _mm(n1, ddngate, ta=True, out_dtype=BF16, name="dw_dngate")
    grads["w_sbqkv"] = _mm(n1, dsbqkv, ta=True, out_dtype=BF16, name="dw_sbqkv")
    grads["w_gl"] = _mm(n1, dgl, ta=True, out_dtype=BF16, name="dw_gl")
    grads["w_ab"] = _mm(n1, dhab, ta=True, out_dtype=BF16, name="dw_ab")
    grad_x, grads["norm1"] = _norm1_bwd(x, wts["norm1"], dn1, dx1, dhab, wts["w_ab"])
    return loss_part, grad_x, grads, (list(partials), arrived)


def _place():
    return lax.axis_index("x"), lax.axis_index("y"), lax.axis_index("c")


def _hbm_specs(n):
    return [pl.BlockSpec(memory_space=pltpu.HBM)] * n


GATHER_SEMS = 8
GATHER_FORWARD_AT = 6


def _gather_protocol(ins, outs, send_sems, recv_sems):
    n = len(ins)
    x, y, c = _place()
    me = 2 * x + y
    sibling = (x, y, 1 - c)
    xn, yn, dg = (1 - x, y), (x, 1 - y), (1 - x, 1 - y)
    idx = lambda chip: 2 * chip[0] + chip[1]

    def part(a, chip_index, core, quarter=None):
        half = ins[a].shape[0] // 2
        if quarter is None:
            return outs[a].at[chip_index, pl.ds(core * half, half), :]
        return outs[a].at[chip_index, pl.ds(core * half + quarter * (half // 2), half // 2), :]

    def copy(a, k, src, dst, to):
        return pltpu.make_async_remote_copy(src_ref=src, dst_ref=dst, send_sem=send_sems.at[GATHER_SEMS * a + k],
                                            recv_sem=recv_sems.at[GATHER_SEMS * a + k], device_id=to,
                                            device_id_type=MESH)

    def sent(a, k):
        half = ins[a].shape[0] // 2
        my_half = ins[a].at[pl.ds(c * half, half), :]
        if k < 2:
            return copy(a, k, my_half, part(a, me, c), (*(xn, yn)[k], c))
        if k < 4:
            src = part(a, idx((xn, yn)[k - 2]), c, k - 2)
            return copy(a, k, src, src, (*(yn, xn)[k - 2], c))
        src = (part(a, idx(xn), c), part(a, idx(yn), c), part(a, idx(dg), c, 0), part(a, idx(dg), c, 1))[k - 4]
        return copy(a, k, src, src, sibling)

    def landed(a, k):
        dst = (part(a, idx(xn), c), part(a, idx(yn), c), part(a, idx(dg), c, 0), part(a, idx(dg), c, 1),
               part(a, idx(xn), 1 - c), part(a, idx(yn), 1 - c), part(a, idx(dg), 1 - c, 0),
               part(a, idx(dg), 1 - c, 1))[k]
        return copy(a, k, dst, dst, sibling)

    def begin():
        for a in range(n):
            sent(a, 0).start()
            sent(a, 1).start()

    def middle():
        for a in range(n):
            for k in range(2):
                landed(a, k).wait_recv()
                sent(a, 2 + k).start()
                sent(a, 4 + k).start()

    def end():
        for a in range(n):
            for k in (2, 3):
                landed(a, k).wait_recv()
                sent(a, 4 + k).start()
        for a in range(n):
            for k in range(4, GATHER_SEMS):
                landed(a, k).wait_recv()
        for a in range(n):
            for k in range(GATHER_SEMS):
                sent(a, k).wait_send()

    return begin, middle, end


def _gather_out_shapes(shards):
    return [jax.ShapeDtypeStruct((N_CHIPS,) + s.shape, s.dtype) for s in shards]


def _gather_sems(n):
    return [pltpu.SemaphoreType.DMA((GATHER_SEMS * n,)), pltpu.SemaphoreType.DMA((GATHER_SEMS * n,))]


def _gather_shards(shards):
    n = len(shards)

    def body(*refs):
        begin, middle, end = _gather_protocol(refs[:n], refs[n:2 * n], *refs[2 * n:])
        begin()
        middle()
        end()

    return pl.pallas_call(
        body, name="gather_weights", in_specs=_hbm_specs(n), out_specs=_hbm_specs(n),
        out_shape=_gather_out_shapes(shards), scratch_shapes=_gather_sems(n),
    )(*shards)


def _pair_exchange_halves(gs, tag):
    n = len(gs)

    def body(*refs):
        ins, outs, (send_sems, recv_sems) = refs[:n], refs[n:2 * n], refs[2 * n:]
        x, y, c = _place()
        cps = []
        for a in range(n):
            half = ins[a].shape[1] // 2
            cp = pltpu.make_async_remote_copy(src_ref=ins[a].at[:, pl.ds((1 - c) * half, half), :], dst_ref=outs[a],
                                              send_sem=send_sems.at[a], recv_sem=recv_sems.at[a],
                                              device_id=(x, y, 1 - c), device_id_type=MESH)
            cp.start()
            cps.append(cp)
        for cp in cps:
            cp.wait()

    return pl.pallas_call(
        body, name="grad_pair_exchange_" + tag, in_specs=_hbm_specs(n), out_specs=_hbm_specs(n),
        out_shape=[jax.ShapeDtypeStruct((g.shape[0], g.shape[1] // 2, g.shape[2]), g.dtype) for g in gs],
        scratch_shapes=[pltpu.SemaphoreType.DMA((n,)), pltpu.SemaphoreType.DMA((n,))],
    )(*gs)


def _pick_rows(n, target=1024):
    best = 16
    for b in range(16, min(n, target) + 1, 16):
        if n % b == 0:
            best = b
    return best


def _pair_add(g, got, c_idx, tag):
    nsh, rows, cols = g.shape
    half = rows // 2
    rb = _pick_rows(half)

    def body(c_ref, g_ref, got_ref, o_ref):
        o_ref[...] = (g_ref[...].astype(F32) + got_ref[...].astype(F32)).astype(BF16)

    nb = half // rb
    grid_spec = pltpu.PrefetchScalarGridSpec(
        num_scalar_prefetch=1, grid=(nsh, nb),
        in_specs=[pl.BlockSpec((1, rb, cols), lambda s, i, c_ref: (s, c_ref[0] * nb + i, 0)),
                  pl.BlockSpec((1, rb, cols), lambda s, i, c_ref: (s, i, 0))],
        out_specs=pl.BlockSpec((1, rb, cols), lambda s, i, c_ref: (s, i, 0)))
    return pl.pallas_call(
        body, name="grad_pair_add_" + tag, grid_spec=grid_spec,
        out_shape=jax.ShapeDtypeStruct((nsh, half, cols), BF16),
        compiler_params=_params(("parallel", "parallel")),
    )(c_idx, g, got)


def _chip_exchange_protocol(ins, outs, send_sems, recv_sems):
    x, y, c = _place()
    chips = [(1 - x, y), (x, 1 - y), (1 - x, 1 - y)]

    def copies():
        return [pltpu.make_async_remote_copy(src_ref=ins[a].at[2 * px + py], dst_ref=outs[a].at[j],
                                             send_sem=send_sems.at[3 * a + j], recv_sem=recv_sems.at[3 * a + j],
                                             device_id=(px, py, c), device_id_type=MESH)
                for a in range(len(ins)) for j, (px, py) in enumerate(chips)]

    def begin():
        for cp in copies():
            cp.start()

    def end():
        for cp in copies():
            cp.wait_recv()
        for cp in copies():
            cp.wait_send()

    return begin, end


def _chip_exchange_shapes(ps):
    return [jax.ShapeDtypeStruct((N_CHIPS - 1,) + p.shape[1:], p.dtype) for p in ps]


def _chip_exchange_sems(n):
    return [pltpu.SemaphoreType.DMA((3 * n,)), pltpu.SemaphoreType.DMA((3 * n,))]


def _chip_exchange(ps):
    n = len(ps)

    def body(*refs):
        begin, end = _chip_exchange_protocol(refs[:n], refs[n:2 * n], *refs[2 * n:])
        begin()
        end()

    return pl.pallas_call(
        body, name="grad_chip_exchange", in_specs=_hbm_specs(n), out_specs=_hbm_specs(n),
        out_shape=_chip_exchange_shapes(ps), scratch_shapes=_chip_exchange_sems(n),
    )(*ps)


def _sum_partials(p, got, chip_idx, tag):
    nsh, half, cols = got.shape
    rb = _pick_rows(half)

    def body(me_ref, p_ref, got_ref, o_ref):
        acc = p_ref[0].astype(F32)
        for s in range(nsh):
            acc = acc + got_ref[s].astype(F32)
        o_ref[...] = acc

    grid_spec = pltpu.PrefetchScalarGridSpec(
        num_scalar_prefetch=1, grid=(half // rb,),
        in_specs=[pl.BlockSpec((1, rb, cols), lambda i, me_ref: (me_ref[0], i, 0)),
                  pl.BlockSpec((nsh, rb, cols), lambda i, me_ref: (0, i, 0))],
        out_specs=pl.BlockSpec((rb, cols), lambda i, me_ref: (i, 0)))
    return pl.pallas_call(
        body, name="grad_sum_chips_" + tag, grid_spec=grid_spec,
        out_shape=jax.ShapeDtypeStruct((half, cols), F32),
        compiler_params=_params(("parallel",)),
    )(chip_idx, p, got)


def _pair_share(rs):
    n = len(rs)

    def body(*refs):
        ins, outs, (send_sems, recv_sems) = refs[:n], refs[n:2 * n], refs[2 * n:]
        x, y, c = _place()
        cps = []
        for a in range(n):
            cp = pltpu.make_async_remote_copy(src_ref=ins[a], dst_ref=outs[a], send_sem=send_sems.at[a],
                                              recv_sem=recv_sems.at[a], device_id=(x, y, 1 - c),
                                              device_id_type=MESH)
            cp.start()
            cps.append(cp)
        for cp in cps:
            cp.wait()

    return pl.pallas_call(
        body, name="grad_pair_share", in_specs=_hbm_specs(n), out_specs=_hbm_specs(n),
        out_shape=[jax.ShapeDtypeStruct(r.shape, r.dtype) for r in rs],
        scratch_shapes=[pltpu.SemaphoreType.DMA((n,)), pltpu.SemaphoreType.DMA((n,))],
    )(*rs)


def _small_allreduce(v):
    rows, cols = v.shape
    ndev = 8

    def body(in_ref, out_ref, slots, send_sems, recv_sems):
        x, y, c = _place()
        me = 4 * x + 2 * y + c
        slots[me] = in_ref[...]
        sends = []
        for k in range(1, ndev):
            peer = (x ^ (k >> 2), y ^ ((k >> 1) & 1), c ^ (k & 1))
            cp = pltpu.make_async_remote_copy(src_ref=in_ref, dst_ref=slots.at[me], send_sem=send_sems.at[k - 1],
                                              recv_sem=recv_sems.at[k - 1], device_id=peer, device_id_type=MESH)
            cp.start()
            sends.append(cp)
        for k in range(1, ndev):
            there = slots.at[me ^ k]
            pltpu.make_async_remote_copy(src_ref=there, dst_ref=there, send_sem=send_sems.at[k - 1],
                                         recv_sem=recv_sems.at[k - 1], device_id=(x, y, c),
                                         device_id_type=MESH).wait_recv()
        for cp in sends:
            cp.wait_send()
        acc = slots[0]
        for s in range(1, ndev):
            acc = acc + slots[s]
        out_ref[...] = acc

    return pl.pallas_call(
        body, name="small_allreduce",
        in_specs=[pl.BlockSpec(memory_space=pltpu.VMEM)],
        out_specs=pl.BlockSpec(memory_space=pltpu.VMEM),
        out_shape=jax.ShapeDtypeStruct((rows, cols), F32),
        scratch_shapes=[pltpu.VMEM((ndev, rows, cols), F32), pltpu.SemaphoreType.DMA((ndev - 1,)),
                        pltpu.SemaphoreType.DMA((ndev - 1,))],
    )(v)


def _adamw(w, g, m, v, name):
    r, c = w.shape
    rb = r if r <= 128 else _pick_rows_8(r, 128)
    c1 = 1.0 - ADAM_B1 ** ADAM_STEP
    c2 = 1.0 - ADAM_B2 ** ADAM_STEP

    def body(w_ref, g_ref, m_ref, v_ref, d_ref, nm_ref, nv_ref):
        gg = g_ref[...]
        nm = ADAM_B1 * m_ref[...] + (1.0 - ADAM_B1) * gg
        nv = ADAM_B2 * v_ref[...] + (1.0 - ADAM_B2) * (gg * gg)
        d_ref[...] = -ADAM_LR * ((nm / c1) / (jnp.sqrt(nv / c2) + ADAM_EPS) + ADAM_WD * w_ref[...])
        nm_ref[...] = nm
        nv_ref[...] = nv

    blk = pl.BlockSpec((rb, c), lambda i: (i, 0))
    shp = jax.ShapeDtypeStruct((r, c), F32)
    return pl.pallas_call(
        body, name=name, grid=(r // rb,), in_specs=[blk] * 4, out_specs=[blk] * 3, out_shape=[shp] * 3,
        compiler_params=_params(("parallel",)),
    )(w, g, m, v)


def _pick_rows_8(n, target):
    best = n
    for b in range(8, min(n, target) + 1, 8):
        if n % b == 0:
            best = b
    return best


W_IN_COLS = 2308
W_UP_COLS = 1408
W_DOWN_ROWS = 704
DN_CONV_COLS = 768
FFN_CONV_COLS = 1408
PROJ_ROWS = 256
ROW_TILE = 16
ROW_SEGS = [("wp_dn", PROJ_ROWS), ("wp_sb", PROJ_ROWS), ("w_out", PROJ_ROWS), ("w_down", W_DOWN_ROWS),
            ("dn_conv", ROW_TILE), ("ffn_conv", ROW_TILE), ("spare", 2 * ROW_TILE)]
ROW_OFFS = {nm: (sum(n for _, n in ROW_SEGS[:i]), n) for i, (nm, n) in enumerate(ROW_SEGS)}
STACK_ROWS = sum(n for _, n in ROW_SEGS)
assert all(n % ROW_TILE == 0 for _, n in ROW_SEGS) and STACK_ROWS % (4 * ROW_TILE) == 0
Q_END, A_END, G_END, S_END = 3 * D_MODEL, 3 * D_MODEL + 2 * N_HEADS, 4 * D_MODEL + 2 * N_HEADS, 7 * D_MODEL + 2 * N_HEADS


def _flat_rows(a, nrows):
    flat = a.reshape(-1)
    return jnp.pad(flat, (0, nrows * D_MODEL - flat.shape[0])).reshape(nrows, D_MODEL)


IN_EXTRA_ROWS = 64


def _weight_wire(w_in, wp_dn, wp_sb, w_out, w_up, w_down, dn_conv, ffn_conv):
    bits = lax.bitcast_convert_type(dn_conv, BF16).reshape(-1)
    extra = jnp.pad(bits, (0, IN_EXTRA_ROWS * W_IN_COLS - bits.shape[0])).reshape(IN_EXTRA_ROWS, W_IN_COLS)
    stack = jnp.concatenate([wp_dn.astype(BF16), wp_sb.astype(BF16), w_out.astype(BF16), w_down.astype(BF16),
                             jnp.zeros((ROW_TILE, D_MODEL), BF16),
                             _flat_rows(lax.bitcast_convert_type(ffn_conv, BF16), ROW_TILE),
                             jnp.zeros((ROW_OFFS["spare"][1], D_MODEL), BF16)], axis=0)
    return [jnp.concatenate([w_in.astype(BF16), extra], axis=0)], [w_up.astype(BF16), stack]


def _col_range(g, lo, hi, width):
    parts = []
    for s in range(g.shape[0]):
        a, b = max(lo, s * width), min(hi, (s + 1) * width)
        if a < b:
            parts.append(g[s][:, a - s * width:b - s * width])
    return parts[0] if len(parts) == 1 else jnp.concatenate(parts, axis=1)


def _f32_rows(raw, k, ncols):
    raw = raw.reshape(N_CHIPS, -1)[:, :2 * k * ncols].reshape(N_CHIPS, k * ncols, 2)
    vals = lax.bitcast_convert_type(raw, F32).reshape(N_CHIPS, k, ncols)
    return vals.transpose(1, 0, 2).reshape(k, N_CHIPS * ncols)


def _unpack_early(g_in):
    w = g_in[:, :D_MODEL, :]
    return {
        "w_dnqkv": _col_range(w, 0, Q_END, W_IN_COLS),
        "w_ab": jnp.pad(_col_range(w, Q_END, A_END, W_IN_COLS), ((0, 0), (0, LANES - 2 * N_HEADS))),
        "w_dngate": _col_range(w, A_END, G_END, W_IN_COLS),
        "w_sbqkv": _col_range(w, G_END, S_END, W_IN_COLS),
        "w_gl": _col_range(w, S_END, N_CHIPS * W_IN_COLS, W_IN_COLS),
        "dn_conv": _f32_rows(g_in[:, D_MODEL:, :], DN_CONV, DN_CONV_COLS),
    }


def _unpack_late(g_up, g_stack):
    def seg(nm):
        at, n = ROW_OFFS[nm]
        return g_stack[:, at:at + n, :]

    ffn_conv = _f32_rows(seg("ffn_conv"), FFN_CONV, FFN_CONV_COLS)
    return {
        "wp_dn": seg("wp_dn").reshape(D_MODEL, D_MODEL),
        "wp_sb": seg("wp_sb").reshape(D_MODEL, D_MODEL),
        "w_out": seg("w_out").reshape(D_MODEL, D_MODEL),
        "w_up_g": _col_range(g_up, 0, D_FF, W_UP_COLS), "w_up_u": _col_range(g_up, D_FF, 2 * D_FF, W_UP_COLS),
        "w_down": seg("w_down").reshape(D_FF, D_MODEL),
        "ffn_conv_g": ffn_conv[:, :D_FF], "ffn_conv_u": ffn_conv[:, D_FF:],
    }


def _grad_wire_early(gr):
    def cols(a, ncols):
        return a.reshape(a.shape[0], N_CHIPS, ncols).transpose(1, 0, 2)

    def rows(a, nrows):
        return a.astype(BF16).reshape(N_CHIPS, nrows, a.shape[1])

    def flat(a, nrows):
        a = a.astype(BF16).reshape(N_CHIPS, -1)
        return jnp.pad(a, ((0, 0), (0, nrows * D_MODEL - a.shape[1]))).reshape(N_CHIPS, nrows, D_MODEL)

    up = [gr["w_up_g"], gr["w_up_u"]]
    g_up = jnp.stack([up[s // 2][:, (s % 2) * W_UP_COLS:(s % 2 + 1) * W_UP_COLS].astype(BF16) for s in range(N_CHIPS)])
    g_stack = jnp.concatenate([rows(gr["wp_dn"], PROJ_ROWS), rows(gr["wp_sb"], PROJ_ROWS), rows(gr["w_out"], PROJ_ROWS),
                               rows(gr["w_down"], W_DOWN_ROWS), jnp.zeros((N_CHIPS, ROW_TILE, D_MODEL), BF16),
                               flat(cols(gr["ffn_conv"], FFN_CONV_COLS), ROW_TILE),
                               jnp.zeros((N_CHIPS, ROW_OFFS["spare"][1], D_MODEL), BF16)], axis=1)
    return [g_up, g_stack]


def _grad_wire_late(gr):
    pieces = [(gr["w_dnqkv"], 0), (gr["w_ab"][:, :2 * N_HEADS], Q_END), (gr["w_dngate"], A_END),
              (gr["w_sbqkv"], G_END), (gr["w_gl"], S_END)]
    conv = gr["dn_conv"].reshape(DN_CONV, N_CHIPS, DN_CONV_COLS).transpose(1, 0, 2).reshape(N_CHIPS, -1)

    def block(s):
        lo, hi = s * W_IN_COLS, (s + 1) * W_IN_COLS
        parts = []
        for a, at in pieces:
            b0, b1 = max(lo, at), min(hi, at + a.shape[1])
            if b0 < b1:
                parts.append(a[:, b0 - at:b1 - at].astype(BF16))
        w = parts[0] if len(parts) == 1 else jnp.concatenate(parts, axis=1)
        extra = jnp.pad(conv[s].astype(BF16), (0, IN_EXTRA_ROWS * W_IN_COLS - conv.shape[1]))
        return jnp.concatenate([w, extra.reshape(IN_EXTRA_ROWS, W_IN_COLS)], axis=0)

    return [jnp.stack([block(s) for s in range(N_CHIPS)])]


def _unpack_grad_shard(r_in, r_up, r_stack):
    def seg(nm):
        at, n = ROW_OFFS[nm]
        return r_stack[at:at + n, :]

    return {
        "w_in": r_in[:D_MODEL], "w_up": r_up,
        "wp_dn": seg("wp_dn"), "wp_sb": seg("wp_sb"), "w_out": seg("w_out"), "w_down": seg("w_down"),
        "dn_conv": r_in[D_MODEL:].reshape(-1)[:DN_CONV * DN_CONV_COLS].reshape(DN_CONV, DN_CONV_COLS),
        "ffn_conv": seg("ffn_conv").reshape(-1)[:FFN_CONV * FFN_CONV_COLS].reshape(FFN_CONV, FFN_CONV_COLS),
    }


def _lane_row(v):
    return jnp.pad(v.reshape(1, -1), ((0, 0), (0, LANES - v.size)))


def kernel(x, norm1_w, w_in, dn_conv_w, dn_A_log, dn_dt_bias, dn_norm_w, w_proj_dn, w_proj_sb, w_out, norm2_w, ffn_w_up, ffn_conv_w, ffn_w_down, norm_f_w, loss_target, m_norm1_w, m_w_in, m_dn_conv_w, m_dn_A_log, m_dn_dt_bias, m_dn_norm_w, m_w_proj_dn, m_w_proj_sb, m_w_out, m_norm2_w, m_ffn_w_up, m_ffn_conv_w, m_ffn_w_down, m_norm_f_w, v_norm1_w, v_w_in, v_dn_conv_w, v_dn_A_log, v_dn_dt_bias, v_dn_norm_w, v_w_proj_dn, v_w_proj_sb, v_w_out, v_norm2_w, v_ffn_w_up, v_ffn_conv_w, v_ffn_w_down, v_norm_f_w):
    early, late = _weight_wire(w_in[0], w_proj_dn[0], w_proj_sb[0], w_out[0], ffn_w_up[0], ffn_w_down[0],
                               dn_conv_w[0], ffn_conv_w[0])
    chip_idx = (2 * lax.axis_index("x") + lax.axis_index("y")).astype(jnp.int32)

    def with_mine(gathered, wire):
        return [lax.dynamic_update_slice(g, mine[None], (chip_idx, 0, 0)) for g, mine in zip(gathered, wire)]

    wts = _unpack_early(*with_mine(_gather_shards(early), early))
    wts.update(norm1=norm1_w, norm2=norm2_w, normf=norm_f_w.reshape(1, D_MODEL), dn_norm=dn_norm_w,
               alog=_lane_row(dn_A_log), dtb=_lane_row(dn_dt_bias))

    c_idx = lax.axis_index("c").astype(jnp.int32).reshape(1)

    def pair_sums(wire_g, tags, when):
        return [_pair_add(g, got, c_idx, tag) for g, got, tag in zip(wire_g, _pair_exchange_halves(wire_g, when), tags)]

    loss_part, grad_x, gr, (early_sums, early_arrived) = _local_step(
        x[0], loss_target[0], wts, late, lambda gathered: _unpack_late(*with_mine(gathered, late)),
        lambda grads: pair_sums(_grad_wire_early(grads), ["w_up", "rows"], "early"))

    late_sums = pair_sums(_grad_wire_late(gr), ["w_in"], "late")
    tags = ["w_in", "w_up", "rows"]
    reduced = [_sum_partials(p, got, chip_idx.reshape(1), tag)
               for p, got, tag in zip(late_sums + early_sums, list(_chip_exchange(late_sums)) + list(early_arrived), tags)]
    is_south = lax.axis_index("c") == 0
    gsh = _unpack_grad_shard(*[jnp.concatenate([jnp.where(is_south, mine, other), jnp.where(is_south, other, mine)],
                                               axis=0) for mine, other in zip(reduced, _pair_share(reduced))])

    tail = jnp.concatenate([gr["dn_norm"], gr["alog"][:, :N_HEADS], gr["dtb"][:, :N_HEADS], loss_part[:, :1]], axis=1)
    small = jnp.concatenate([gr["norm1"], gr["norm2"], gr["normf"],
                             jnp.pad(tail, ((0, 0), (0, D_MODEL - tail.shape[1]))),
                             jnp.zeros((SMALL_ROWS - 4, D_MODEL), F32)], axis=0)
    small = _small_allreduce(small)
    at = HEAD_DIM
    g_small = {"norm1_w": small[0:1], "norm2_w": small[1:2], "norm_f_w": small[2],
               "dn_norm_w": small[3:4, :at], "dn_A_log": small[3:4, at:at + N_HEADS],
               "dn_dt_bias": small[3:4, at + N_HEADS:at + 2 * N_HEADS]}
    loss = small[3, at + 2 * N_HEADS]

    big = {"w_in": (w_in, m_w_in, v_w_in, gsh["w_in"]), "dn_conv_w": (dn_conv_w, m_dn_conv_w, v_dn_conv_w, gsh["dn_conv"]),
           "w_proj_dn": (w_proj_dn, m_w_proj_dn, v_w_proj_dn, gsh["wp_dn"]),
           "w_proj_sb": (w_proj_sb, m_w_proj_sb, v_w_proj_sb, gsh["wp_sb"]),
           "w_out": (w_out, m_w_out, v_w_out, gsh["w_out"]),
           "ffn_w_up": (ffn_w_up, m_ffn_w_up, v_ffn_w_up, gsh["w_up"]),
           "ffn_conv_w": (ffn_conv_w, m_ffn_conv_w, v_ffn_conv_w, gsh["ffn_conv"]),
           "ffn_w_down": (ffn_w_down, m_ffn_w_down, v_ffn_w_down, gsh["w_down"])}
    res = {}
    for nm, (w, m, v, g) in big.items():
        d, nm_, nv_ = _adamw(w[0], g, m[0], v[0], "adamw_" + nm)
        res[nm] = (g[None], d[None], nm_[None], nv_[None])

    names = ["norm1_w", "norm2_w", "norm_f_w", "dn_norm_w", "dn_A_log", "dn_dt_bias"]
    given = {"norm1_w": (norm1_w, m_norm1_w, v_norm1_w), "norm2_w": (norm2_w, m_norm2_w, v_norm2_w),
             "norm_f_w": (norm_f_w, m_norm_f_w, v_norm_f_w), "dn_norm_w": (dn_norm_w, m_dn_norm_w, v_dn_norm_w),
             "dn_A_log": (dn_A_log, m_dn_A_log, v_dn_A_log), "dn_dt_bias": (dn_dt_bias, m_dn_dt_bias, v_dn_dt_bias)}

    def stack(k, fill):
        rows = [jnp.pad(given[nm][k].reshape(1, -1), ((0, 0), (0, D_MODEL - given[nm][k].size)),
                        constant_values=fill) for nm in names]
        return jnp.concatenate(rows + [jnp.full((SMALL_ROWS - len(names), D_MODEL), fill, F32)], axis=0)

    g_rows = jnp.concatenate(
        [jnp.pad(g_small[nm].reshape(1, -1), ((0, 0), (0, D_MODEL - g_small[nm].size))) for nm in names]
        + [jnp.zeros((SMALL_ROWS - len(names), D_MODEL), F32)], axis=0)
    d_s, m_s, v_s = _adamw(stack(0, 0.0), g_rows, stack(1, 0.0), stack(2, 1.0), "adamw_small")
    for r, nm in enumerate(names):
        shape = given[nm][0].shape
        n = given[nm][0].size
        res[nm] = (g_small[nm].reshape(shape), d_s[r, :n].reshape(shape), m_s[r, :n].reshape(shape),
                   v_s[r, :n].reshape(shape))

    order = ["norm1_w", "w_in", "dn_conv_w", "dn_A_log", "dn_dt_bias", "dn_norm_w", "w_proj_dn", "w_proj_sb",
             "w_out", "norm2_w", "ffn_w_up", "ffn_conv_w", "ffn_w_down", "norm_f_w"]
    outs = [loss, grad_x[None]]
    for k in range(4):
        outs += [res[nm][k] for nm in order]
    return tuple(outs)
```

```python
import functools

import jax
import jax.numpy as jnp
from jax import lax
from jax.experimental import pallas as pl
from jax.experimental.pallas import tpu as pltpu

F32 = jnp.float32
BF16 = jnp.bfloat16
MESH = pl.DeviceIdType.MESH

EPS = 1e-6
D_MODEL = 1024
N_HEADS = 8
HEAD_DIM = 128
DN_CONV = 4
DN_CHUNK = 64
D_FF = 2816
FFN_CONV = 3
ADAM_LR, ADAM_B1, ADAM_B2, ADAM_EPS, ADAM_WD, ADAM_STEP = 0.001, 0.9, 0.999, 1e-08, 0.01, 10

N_CHIPS = 4
LANES = 128
HALO = 8
VMEM_LIMIT = 48 * 1024 * 1024
SMALL_ROWS = 8


def _params(sem=None):
    return pltpu.CompilerParams(dimension_semantics=sem, vmem_limit_bytes=VMEM_LIMIT)


def _pick(n, target):
    best = None
    for b in range(LANES, min(n, target) + 1, LANES):
        if n % b == 0:
            best = b
    return best or n


ELEMENTWISE_COLS = 1408


def _rows(t, target=256):
    return min(t, target)


def _dot(a, b, precision=None):
    return lax.dot_general(a, b, (((1,), (0,)), ((), ())), precision=precision, preferred_element_type=F32)


def _dot_nt(a, b, precision=None):
    return lax.dot_general(a, b, (((1,), (1,)), ((), ())), precision=precision, preferred_element_type=F32)


def _dot_tn(a, b, precision=None):
    return lax.dot_general(a, b, (((0,), (0,)), ((), ())), precision=precision, preferred_element_type=F32)


def _rms(x, w):
    return x * lax.rsqrt(jnp.mean(x * x, axis=-1, keepdims=True) + EPS) * w


def _silu(x):
    return x * jax.nn.sigmoid(x)


def _softplus(x):
    return jnp.maximum(x, 0.0) + jnp.log(1.0 + jnp.exp(-jnp.abs(x)))


MM_BLOCK = 1408
MM_VMEM_BUDGET = 38 * 1024 * 1024


def _mm(a, b, *, ta=False, tb=False, add=None, out_dtype=F32, name, bm=MM_BLOCK, bn=MM_BLOCK, bk=MM_BLOCK):
    m = a.shape[1] if ta else a.shape[0]
    k = a.shape[0] if ta else a.shape[1]
    n = b.shape[0] if tb else b.shape[1]
    bm, bn = _pick(m, bm), _pick(n, bn)

    def vmem_need(bk_):
        need = 2 * (bm * bk_ * a.dtype.itemsize + bk_ * bn * b.dtype.itemsize) + 2 * bm * bn * jnp.dtype(out_dtype).itemsize
        need += 2 * bm * bn * add.dtype.itemsize if add is not None else 0
        return need + (bm * bn * 4 if bk_ < k else 0)

    bk = max((d for d in range(LANES, k + 1, LANES) if k % d == 0 and vmem_need(d) <= MM_VMEM_BUDGET),
             default=_pick(k, bk))
    nk = k // bk
    dims = (((0 if ta else 1,), (1 if tb else 0,)), ((), ()))

    def body(*refs):
        a_ref, b_ref = refs[:2]
        c_ref = refs[2] if add is not None else None
        o_ref = refs[3] if add is not None else refs[2]
        acc = refs[-1]
        kk = pl.program_id(2)
        part = lax.dot_general(a_ref[...].astype(BF16), b_ref[...].astype(BF16), dims, preferred_element_type=F32)

        def finish(r):
            if add is not None:
                r = r + c_ref[...].astype(F32)
            o_ref[...] = r.astype(out_dtype)

        if nk == 1:
            finish(part)
            return

        @pl.when(kk == 0)
        def _():
            acc[...] = part

        @pl.when(jnp.logical_and(kk > 0, kk < nk - 1))
        def _():
            acc[...] += part

        @pl.when(kk == nk - 1)
        def _():
            finish(acc[...] + part)

    a_spec = (pl.BlockSpec((bk, bm), lambda i, j, kk: (kk, i)) if ta
              else pl.BlockSpec((bm, bk), lambda i, j, kk: (i, kk)))
    b_spec = (pl.BlockSpec((bn, bk), lambda i, j, kk: (j, kk)) if tb
              else pl.BlockSpec((bk, bn), lambda i, j, kk: (kk, j)))
    o_spec = pl.BlockSpec((bm, bn), lambda i, j, kk: (i, j))
    in_specs = [a_spec, b_spec] + ([o_spec] if add is not None else [])
    args = (a, b) + ((add,) if add is not None else ())
    return pl.pallas_call(
        body, name=name, grid=(m // bm, n // bn, nk),
        in_specs=in_specs, out_specs=o_spec,
        out_shape=jax.ShapeDtypeStruct((m, n), out_dtype),
        scratch_shapes=[pltpu.VMEM((bm, bn), F32)] if nk > 1 else [],
        compiler_params=_params(("parallel", "parallel", "arbitrary")),
    )(*args)


def _norm1_fwd(x, w, w_ab):
    t = x.shape[0]
    tb = _rows(t)

    def body(x_ref, w_ref, wab_ref, n_ref, hab_ref):
        n = _rms(x_ref[...], w_ref[...]).astype(BF16)
        n_ref[...] = n
        hab_ref[...] = _dot(n, wab_ref[...])

    return pl.pallas_call(
        body, name="norm1_fwd", grid=(t // tb,),
        in_specs=[pl.BlockSpec((tb, D_MODEL), lambda i: (i, 0)),
                  pl.BlockSpec((1, D_MODEL), lambda i: (0, 0)),
                  pl.BlockSpec((D_MODEL, LANES), lambda i: (0, 0))],
        out_specs=[pl.BlockSpec((tb, D_MODEL), lambda i: (i, 0)),
                   pl.BlockSpec((tb, LANES), lambda i: (i, 0))],
        out_shape=[jax.ShapeDtypeStruct((t, D_MODEL), BF16), jax.ShapeDtypeStruct((t, LANES), F32)],
        compiler_params=_params(("arbitrary",)),
    )(x, w, w_ab)


def _norm1_bwd(x, w, dn, dres, dab, w_ab):
    t = x.shape[0]
    tb = _rows(t)

    def body(x_ref, w_ref, dn_ref, dres_ref, dab_ref, wab_ref, dx_ref, dw_ref):
        i = pl.program_id(0)
        g = dn_ref[...] + _dot_nt(dab_ref[...].astype(BF16), wab_ref[...])
        _, vjp = jax.vjp(_rms, x_ref[...], w_ref[...])
        dx, dw = vjp(g)
        dx_ref[...] = dres_ref[...] + dx

        @pl.when(i == 0)
        def _():
            dw_ref[...] = jnp.zeros_like(dw_ref)

        dw_ref[...] += dw

    row = pl.BlockSpec((tb, D_MODEL), lambda i: (i, 0))
    vec = pl.BlockSpec((1, D_MODEL), lambda i: (0, 0))
    return pl.pallas_call(
        body, name="norm1_bwd", grid=(t // tb,),
        in_specs=[row, vec, row, row, pl.BlockSpec((tb, LANES), lambda i: (i, 0)),
                  pl.BlockSpec((D_MODEL, LANES), lambda i: (0, 0))],
        out_specs=[row, vec],
        out_shape=[jax.ShapeDtypeStruct((t, D_MODEL), F32), jax.ShapeDtypeStruct((1, D_MODEL), F32)],
        compiler_params=_params(("arbitrary",)),
    )(x, w, dn, dres, dab, w_ab)


def _conv_fwd(x, w, name):
    t, c = x.shape
    kk = w.shape[0]
    tb, cb = _rows(t, 512), _pick(c, ELEMENTWISE_COLS)
    per = tb // HALO

    def body(x_ref, halo_ref, w_ref, y_ref, buf):
        i = pl.program_id(0)
        buf[pl.ds(HALO, tb), :] = x_ref[...]
        buf[pl.ds(0, HALO), :] = jnp.where(i == 0, 0.0, halo_ref[...])
        y_ref[...] = _conv_taps(buf, w_ref, HALO - (kk - 1), tb)

    return pl.pallas_call(
        body, name=name, grid=(t // tb, c // cb),
        in_specs=[pl.BlockSpec((tb, cb), lambda i, j: (i, j)),
                  pl.BlockSpec((HALO, cb), lambda i, j: (jnp.maximum(i * per - 1, 0), j)),
                  pl.BlockSpec((kk, cb), lambda i, j: (0, j))],
        out_specs=pl.BlockSpec((tb, cb), lambda i, j: (i, j)),
        out_shape=jax.ShapeDtypeStruct((t, c), F32),
        scratch_shapes=[pltpu.VMEM((tb + HALO, cb), F32)],
        compiler_params=_params(("parallel", "parallel")),
    )(x, x, w)


def _conv_bwd(dy, x, w, name, dx_dtype):
    t, c = x.shape
    kk = w.shape[0]
    tb, cb = _rows(t, 512), _pick(c, ELEMENTWISE_COLS)
    per = tb // HALO
    nblk = t // tb

    def body(dy_ref, after_ref, x_ref, w_ref, dx_ref, dw_ref, dbuf):
        i = pl.program_id(1)
        dbuf[pl.ds(0, tb), :] = dy_ref[...]
        dbuf[pl.ds(tb, HALO), :] = jnp.where(i == nblk - 1, 0.0, after_ref[...])

        @pl.when(i == 0)
        def _():
            dw_ref[...] = jnp.zeros_like(dw_ref)

        for j in range(cb // LANES):
            sl = pl.ds(j * LANES, LANES)
            x = x_ref[:, sl]
            dx = None
            for s in range(kk):
                shifted = dbuf[pl.ds(kk - 1 - s, tb), sl]
                term = w_ref[s:s + 1, sl] * shifted
                dx = term if dx is None else dx + term
                dw_ref[s:s + 1, sl] += jnp.sum(shifted * x, axis=0, keepdims=True)
            dx_ref[:, sl] = dx.astype(dx_dtype)

    blk = pl.BlockSpec((tb, cb), lambda j, i: (i, j))
    return pl.pallas_call(
        body, name=name, grid=(c // cb, nblk),
        in_specs=[blk,
                  pl.BlockSpec((HALO, cb), lambda j, i: (jnp.minimum((i + 1) * per, t // HALO - 1), j)),
                  blk,
                  pl.BlockSpec((kk, cb), lambda j, i: (0, j))],
        out_specs=[blk, pl.BlockSpec((HALO, cb), lambda j, i: (0, j))],
        out_shape=[jax.ShapeDtypeStruct((t, c), dx_dtype), jax.ShapeDtypeStruct((HALO, c), F32)],
        scratch_shapes=[pltpu.VMEM((tb + HALO, cb), F32)],
        compiler_params=_params(("parallel", "arbitrary")),
    )(dy, dy, x, w)


def _dn_head(c, normed):
    s = _silu(c)
    return s * lax.rsqrt(jnp.sum(s * s, axis=-1, keepdims=True) + EPS) if normed else s


def _dn_gates(hab, alog, dtb):
    lane = lax.broadcasted_iota(jnp.int32, hab.shape, 1)
    g = -jnp.exp(alog) * _softplus(hab + dtb)
    beta = jax.nn.sigmoid(hab)
    return jnp.where(lane < N_HEADS, g, jnp.where(lane < 2 * N_HEADS, beta, 0.0))


def _dn_head_slices(q_ref, k_ref, v_ref):
    return [(pl.ds((part * N_HEADS + h) * HEAD_DIM, HEAD_DIM), ref, h, part < 2)
            for part, ref in enumerate((q_ref, k_ref, v_ref)) for h in range(N_HEADS)]


def _dn_prep_fwd(c, hab, alog, dtb):
    t = c.shape[0]
    tb = _rows(t)

    def body(c_ref, hab_ref, alog_ref, dtb_ref, q_ref, k_ref, v_ref, gb_ref):
        for sl, ref, h, normed in _dn_head_slices(q_ref, k_ref, v_ref):
            ref[h] = _dn_head(c_ref[:, sl], normed)
        gb_ref[...] = _dn_gates(hab_ref[...], alog_ref[...], dtb_ref[...])

    hm = pl.BlockSpec((N_HEADS, tb, HEAD_DIM), lambda i: (0, i, 0))
    nar = pl.BlockSpec((tb, LANES), lambda i: (i, 0))
    vec = pl.BlockSpec((1, LANES), lambda i: (0, 0))
    return pl.pallas_call(
        body, name="dn_prep_fwd", grid=(t // tb,),
        in_specs=[pl.BlockSpec((tb, 3 * D_MODEL), lambda i: (i, 0)), nar, vec, vec],
        out_specs=[hm, hm, hm, nar],
        out_shape=[jax.ShapeDtypeStruct((N_HEADS, t, HEAD_DIM), F32)] * 3 + [jax.ShapeDtypeStruct((t, LANES), F32)],
        compiler_params=_params(("parallel",)),
    )(c, hab, alog, dtb)


def _dn_prep_bwd(c, hab, alog, dtb, dq, dk, dv, dgb):
    t = c.shape[0]
    tb = _rows(t)

    def body(c_ref, hab_ref, alog_ref, dtb_ref, dq_ref, dk_ref, dv_ref, dgb_ref,
             dc_ref, dhab_ref, dalog_ref, ddtb_ref):
        i = pl.program_id(0)
        for sl, ref, h, normed in _dn_head_slices(dq_ref, dk_ref, dv_ref):
            _, vjp = jax.vjp(functools.partial(_dn_head, normed=normed), c_ref[:, sl])
            dc_ref[:, sl] = vjp(ref[h])[0]
        _, vjp = jax.vjp(_dn_gates, hab_ref[...], alog_ref[...], dtb_ref[...])
        dhab, dalog, ddtb = vjp(dgb_ref[...])
        dhab_ref[...] = dhab

        @pl.when(i == 0)
        def _():
            dalog_ref[...] = jnp.zeros_like(dalog_ref)
            ddtb_ref[...] = jnp.zeros_like(ddtb_ref)

        dalog_ref[...] += dalog
        ddtb_ref[...] += ddtb

    hm = pl.BlockSpec((N_HEADS, tb, HEAD_DIM), lambda i: (0, i, 0))
    wide = pl.BlockSpec((tb, 3 * D_MODEL), lambda i: (i, 0))
    nar = pl.BlockSpec((tb, LANES), lambda i: (i, 0))
    vec = pl.BlockSpec((1, LANES), lambda i: (0, 0))
    return pl.pallas_call(
        body, name="dn_prep_bwd", grid=(t // tb,),
        in_specs=[wide, nar, vec, vec, hm, hm, hm, nar],
        out_specs=[wide, nar, vec, vec],
        out_shape=[jax.ShapeDtypeStruct((t, 3 * D_MODEL), F32), jax.ShapeDtypeStruct((t, LANES), F32),
                   jax.ShapeDtypeStruct((1, LANES), F32), jax.ShapeDtypeStruct((1, LANES), F32)],
        compiler_params=_params(("arbitrary",)),
    )(c, hab, alog, dtb, dq, dk, dv, dgb)


DN_PREC = lax.Precision.HIGH
DN_GROUP = 32


def _dn_prec(a):
    return DN_PREC if a.dtype == F32 else None


def _bdot(a, b):
    return lax.dot_general(a, b, (((2,), (1,)), ((0,), (0,))), precision=_dn_prec(a), preferred_element_type=F32)


def _bdot_nt(a, b):
    return lax.dot_general(a, b, (((2,), (2,)), ((0,), (0,))), precision=_dn_prec(a), preferred_element_type=F32)


def _bdot_tn(a, b):
    return lax.dot_general(a, b, (((1,), (1,)), ((0,), (0,))), precision=_dn_prec(a), preferred_element_type=F32)


def _unit_lower_inverse(lmat):
    c = lmat.shape[-1]
    ri = lax.broadcasted_iota(jnp.int32, (c, c), 0)
    ci = lax.broadcasted_iota(jnp.int32, (c, c), 1)
    p = -lmat
    tinv = jnp.where(ri == ci, 1.0, 0.0) + p
    for _ in range(max(c.bit_length() - 2, 0)):
        p = _bdot(p, p)
        tinv = tinv + _bdot(tinv, p)
    return tinv


@jax.custom_vjp
def _solve_with(lmat, rhs, tinv):
    return _bdot(tinv, rhs)


def _solve_with_fwd(lmat, rhs, tinv):
    sol = _bdot(tinv, rhs)
    return sol, (sol, tinv)


def _solve_with_bwd(res, dsol):
    sol, tinv = res
    drhs = _bdot_tn(tinv, dsol)
    return -_bdot_nt(drhs, sol), drhs, jnp.zeros_like(tinv)


_solve_with.defvjp(_solve_with_fwd, _solve_with_bwd)


def _dn_local(q, k, v, grow, brow, tinv):
    g, c, _ = q.shape
    ri = lax.broadcasted_iota(jnp.int32, (c, c), 0)
    ci = lax.broadcasted_iota(jnp.int32, (c, c), 1)
    lower = ri >= ci
    as_col = lambda r: jnp.sum(jnp.where(ri == ci, jnp.broadcast_to(r, (g, c, c)), 0.0), axis=2, keepdims=True)
    gcol, bcol = as_col(grow), as_col(brow)
    gc_col = jnp.sum(jnp.where(lower, jnp.broadcast_to(grow, (g, c, c)), 0.0), axis=2, keepdims=True)
    gc_row = jnp.sum(jnp.where(ri <= ci, jnp.broadcast_to(gcol, (g, c, c)), 0.0), axis=1, keepdims=True)
    qs = q * (HEAD_DIM ** -0.5)
    kb = k * bcol
    vb = v * bcol
    decay = jnp.where(lower, jnp.exp(jnp.where(lower, gc_col - gc_row, 0.0)), 0.0)
    lmat = jnp.where(ri > ci, _bdot_nt(kb.astype(BF16), k.astype(BF16)) * decay, 0.0)
    eg = jnp.exp(gc_col)
    rhs = jnp.concatenate([vb, kb * eg], axis=2)
    if tinv is None:
        tinv = _unit_lower_inverse(lmat)
    sol = _solve_with(lmat, rhs, tinv)
    a_qk = jnp.where(lower, _bdot_nt(qs.astype(BF16), k.astype(BF16)) * decay, 0.0)
    g_last = jnp.sum(grow, axis=2, keepdims=True)
    kdec = k * jnp.exp(g_last - gc_col)
    egl = jnp.broadcast_to(jnp.exp(g_last), (g, 1, HEAD_DIM))
    b16 = lambda x: x.astype(BF16)
    return sol[:, :, :HEAD_DIM], b16(sol[:, :, HEAD_DIM:]), b16(a_qk), b16(qs * eg), b16(kdec), egl, tinv


def _dn_seq(u, w, a_qk, qe, kdec, egl, s_in):
    b16 = lambda x: x.astype(BF16)
    v_new = u - _bdot(b16(w), b16(s_in))
    o = _bdot(b16(qe), b16(s_in)) + _bdot(b16(a_qk), b16(v_new))
    return o, s_in * egl + _bdot_tn(b16(kdec), b16(v_new))


def _dn_local_specs(t):
    grp = min(DN_GROUP, t // DN_CHUNK)
    rows = grp * DN_CHUNK
    blk = pl.BlockSpec((1, rows, HEAD_DIM), lambda h, i: (h, i, 0))
    row = pl.BlockSpec((1, grp, 1, DN_CHUNK), lambda h, i: (h, i, 0, 0))
    sq = pl.BlockSpec((1, grp, DN_CHUNK, DN_CHUNK), lambda h, i: (h, i, 0, 0))
    lane = pl.BlockSpec((1, grp, 1, HEAD_DIM), lambda h, i: (h, i, 0, 0))
    return grp, blk, row, sq, lane


def half(shape):
    return jax.ShapeDtypeStruct(shape.shape, BF16)


def _dn_shapes(t):
    nchunk = t // DN_CHUNK
    big = jax.ShapeDtypeStruct((N_HEADS, t, HEAD_DIM), F32)
    row = jax.ShapeDtypeStruct((N_HEADS, nchunk, 1, DN_CHUNK), F32)
    sq = jax.ShapeDtypeStruct((N_HEADS, nchunk, DN_CHUNK, DN_CHUNK), F32)
    lane = jax.ShapeDtypeStruct((N_HEADS, nchunk, 1, HEAD_DIM), F32)
    return big, row, sq, lane


def _dn_local_fwd(q, k, v, grow, brow, wire=()):
    t = q.shape[1]
    grp, blk, row, sq, lane = _dn_local_specs(t)
    big, _, sqs, lanes = _dn_shapes(t)
    n = len(wire)
    groups = t // (grp * DN_CHUNK)
    steps = N_HEADS * groups

    def body(q_ref, k_ref, v_ref, gr_ref, br_ref, *rest):
        u_ref, w_ref, a_ref, qe_ref, kd_ref, egl_ref, t_ref = rest[n:n + 7]
        if n:
            begin, middle, end = _gather_protocol(rest[:n], rest[n + 7:2 * n + 7], *rest[2 * n + 7:])
            step = pl.program_id(0) * groups + pl.program_id(1)
            pl.when(step == 0)(begin)
            pl.when(step == (GATHER_FORWARD_AT * steps) // 8)(middle)
        split = lambda r: r[0].reshape(grp, DN_CHUNK, HEAD_DIM)
        u, w, a_qk, qe, kdec, egl, tinv = _dn_local(split(q_ref), split(k_ref), split(v_ref), gr_ref[0],
                                                     br_ref[0], None)
        for ref, val in ((u_ref, u), (w_ref, w), (qe_ref, qe), (kd_ref, kdec)):
            ref[0] = val.reshape(grp * DN_CHUNK, HEAD_DIM)
        a_ref[0] = a_qk
        egl_ref[0] = egl
        t_ref[0] = tinv
        if n:
            pl.when(step == steps - 1)(end)

    assert n == 0 or steps >= 3
    return pl.pallas_call(
        body, name="dn_local_fwd", grid=(N_HEADS, groups),
        in_specs=[blk, blk, blk, row, row] + _hbm_specs(n),
        out_specs=[blk, blk, sq, blk, blk, lane, sq] + _hbm_specs(n),
        out_shape=[big, half(big), half(sqs), half(big), half(big), lanes, sqs] + _gather_out_shapes(wire),
        scratch_shapes=_gather_sems(n) if n else [],
        compiler_params=_params(("arbitrary", "arbitrary")),
    )(q, k, v, grow, brow, *wire)


def _dn_local_bwd(q, k, v, grow, brow, tinv, du, dw, da, dqe, dkd, degl):
    t = q.shape[1]
    grp, blk, row, sq, lane = _dn_local_specs(t)
    big, rows_, _, _ = _dn_shapes(t)

    def body(q_ref, k_ref, v_ref, gr_ref, br_ref, t_ref, du_ref, dw_ref, da_ref, dqe_ref, dkd_ref,
             degl_ref, dq_ref, dk_ref, dv_ref, dgr_ref, dbr_ref):
        split = lambda r: r[0].reshape(grp, DN_CHUNK, HEAD_DIM)
        tinv_v = t_ref[0]
        fn = lambda q_, k_, v_, gr_, br_: _dn_local(q_, k_, v_, gr_, br_, tinv_v)[:6]
        _, vjp = jax.vjp(fn, split(q_ref), split(k_ref), split(v_ref), gr_ref[0], br_ref[0])
        dq, dk, dv, dgr, dbr = vjp((split(du_ref), split(dw_ref), da_ref[0], split(dqe_ref), split(dkd_ref),
                                    degl_ref[0]))
        for ref, val in ((dq_ref, dq), (dk_ref, dk), (dv_ref, dv)):
            ref[0] = val.reshape(grp * DN_CHUNK, HEAD_DIM)
        dgr_ref[0] = dgr
        dbr_ref[0] = dbr

    return pl.pallas_call(
        body, name="dn_local_bwd", grid=(N_HEADS, t // (grp * DN_CHUNK)),
        in_specs=[blk, blk, blk, row, row, sq, blk, blk, sq, blk, blk, lane],
        out_specs=[blk, blk, blk, row, row],
        out_shape=[big, big, big, rows_, rows_],
        compiler_params=_params(("parallel", "parallel")),
    )(q, k, v, grow, brow, tinv, du, dw, da, dqe, dkd, degl)


DN_SEQ_CHUNKS = 8


def _dn_seq_specs(nchunk, rev):
    per = min(DN_SEQ_CHUNKS, nchunk)
    nstep = nchunk // per

    def idx(n):
        return nstep - 1 - n if rev else n

    blk = pl.BlockSpec((N_HEADS, per * DN_CHUNK, HEAD_DIM), lambda n: (0, idx(n), 0))
    sq = pl.BlockSpec((N_HEADS, per, DN_CHUNK, DN_CHUNK), lambda n: (0, idx(n), 0, 0))
    lane = pl.BlockSpec((N_HEADS, per, 1, HEAD_DIM), lambda n: (0, idx(n), 0, 0))
    st = pl.BlockSpec((N_HEADS, per, HEAD_DIM, HEAD_DIM), lambda n: (0, idx(n), 0, 0))
    return per, nstep, blk, sq, lane, st


def _dn_seq_fwd(u, w, a_qk, qe, kdec, egl):
    t = u.shape[1]
    nchunk = t // DN_CHUNK
    per, nstep, blk, sq, lane, st = _dn_seq_specs(nchunk, False)

    def body(u_ref, w_ref, a_ref, qe_ref, kd_ref, egl_ref, o_ref, s_ref, state):
        @pl.when(pl.program_id(0) == 0)
        def _():
            state[...] = jnp.zeros_like(state)

        for c in range(per):
            rows = pl.ds(c * DN_CHUNK, DN_CHUNK)
            s_in = state[...]
            s_ref[:, c] = s_in.astype(BF16)
            o_ref[:, rows], state[...] = _dn_seq(u_ref[:, rows], w_ref[:, rows], a_ref[:, c], qe_ref[:, rows],
                                                 kd_ref[:, rows], egl_ref[:, c], s_in)

    return pl.pallas_call(
        body, name="dn_seq_fwd", grid=(nstep,),
        in_specs=[blk, blk, sq, blk, blk, lane],
        out_specs=[blk, st],
        out_shape=[jax.ShapeDtypeStruct((N_HEADS, t, HEAD_DIM), F32),
                   jax.ShapeDtypeStruct((N_HEADS, nchunk, HEAD_DIM, HEAD_DIM), BF16)],
        scratch_shapes=[pltpu.VMEM((N_HEADS, HEAD_DIM, HEAD_DIM), F32)],
        compiler_params=_params(("arbitrary",)),
    )(u, w, a_qk, qe, kdec, egl)


def _dn_seq_bwd(u, w, a_qk, qe, kdec, egl, states, do):
    t = u.shape[1]
    nchunk = t // DN_CHUNK
    per, nstep, blk, sq, lane, st = _dn_seq_specs(nchunk, True)
    big, _, sqs, lanes = _dn_shapes(t)

    def body(u_ref, w_ref, a_ref, qe_ref, kd_ref, egl_ref, s_ref, do_ref,
             du_ref, dw_ref, da_ref, dqe_ref, dkd_ref, degl_ref, dstate):
        @pl.when(pl.program_id(0) == 0)
        def _():
            dstate[...] = jnp.zeros_like(dstate)

        for c in reversed(range(per)):
            rows = pl.ds(c * DN_CHUNK, DN_CHUNK)
            _, vjp = jax.vjp(_dn_seq, u_ref[:, rows], w_ref[:, rows], a_ref[:, c], qe_ref[:, rows], kd_ref[:, rows],
                             egl_ref[:, c], s_ref[:, c].astype(F32))
            (du_ref[:, rows], dw_ref[:, rows], da_ref[:, c], dqe_ref[:, rows], dkd_ref[:, rows], degl_ref[:, c],
             dstate[...]) = vjp((do_ref[:, rows], dstate[...]))

    return pl.pallas_call(
        body, name="dn_seq_bwd", grid=(nstep,),
        in_specs=[blk, blk, sq, blk, blk, lane, st, blk],
        out_specs=[blk, blk, sq, blk, blk, lane],
        out_shape=[big, half(big), half(sqs), half(big), half(big), lanes],
        scratch_shapes=[pltpu.VMEM((N_HEADS, HEAD_DIM, HEAD_DIM), F32)],
        compiler_params=_params(("arbitrary",)),
    )(u, w, a_qk, qe, kdec, egl, states, do)


def _dn_post_head(o, gate, w):
    return _rms(o, w) * _silu(gate)


def _dn_post_fwd(o, gate, w):
    t = gate.shape[0]
    tb = _rows(t)

    def body(o_ref, g_ref, w_ref, y_ref):
        for h in range(N_HEADS):
            sl = pl.ds(h * HEAD_DIM, HEAD_DIM)
            y_ref[:, sl] = _dn_post_head(o_ref[h], g_ref[:, sl], w_ref[...]).astype(BF16)

    row = pl.BlockSpec((tb, D_MODEL), lambda i: (i, 0))
    hm = pl.BlockSpec((N_HEADS, tb, HEAD_DIM), lambda i: (0, i, 0))
    return pl.pallas_call(
        body, name="dn_post_fwd", grid=(t // tb,),
        in_specs=[hm, row, pl.BlockSpec((1, HEAD_DIM), lambda i: (0, 0))],
        out_specs=row, out_shape=jax.ShapeDtypeStruct((t, D_MODEL), BF16),
        compiler_params=_params(("parallel",)),
    )(o, gate, w)


def _dn_post_bwd(o, gate, w, dy):
    t = gate.shape[0]
    tb = _rows(t)

    def body(o_ref, g_ref, w_ref, dy_ref, do_ref, dg_ref, dw_ref):
        i = pl.program_id(0)
        @pl.when(i == 0)
        def _():
            dw_ref[...] = jnp.zeros_like(dw_ref)

        for h in range(N_HEADS):
            sl = pl.ds(h * HEAD_DIM, HEAD_DIM)
            _, vjp = jax.vjp(_dn_post_head, o_ref[h], g_ref[:, sl], w_ref[...])
            do_ref[h], dg, dw = vjp(dy_ref[:, sl])
            dg_ref[:, sl] = dg.astype(BF16)
            dw_ref[...] += dw

    row = pl.BlockSpec((tb, D_MODEL), lambda i: (i, 0))
    hm = pl.BlockSpec((N_HEADS, tb, HEAD_DIM), lambda i: (0, i, 0))
    vec = pl.BlockSpec((1, HEAD_DIM), lambda i: (0, 0))
    return pl.pallas_call(
        body, name="dn_post_bwd", grid=(t // tb,),
        in_specs=[hm, row, vec, row],
        out_specs=[hm, row, vec],
        out_shape=[jax.ShapeDtypeStruct((N_HEADS, t, HEAD_DIM), F32), jax.ShapeDtypeStruct((t, D_MODEL), BF16),
                   jax.ShapeDtypeStruct((1, HEAD_DIM), F32)],
        compiler_params=_params(("arbitrary",)),
    )(o, gate, w, dy)


def _split_bf16(x):
    hi = x.astype(BF16)
    lo = (x - hi.astype(F32)).astype(BF16)
    return hi, lo


SB_Q_BLOCK = 512
SB_K_BLOCK = 256
SB_NEGLIGIBLE = -60.0


def _sb_logits(q, kb, mask, scale):
    z = _dot_nt(q, kb) * scale
    ls = jnp.minimum(z, 0.0) - jnp.log(1.0 + jnp.exp(-jnp.abs(z)))
    lk = ls - z
    if mask is not None:
        lk = jnp.where(mask, lk, 0.0)
    return ls, lk


def _sb_blocks(t):
    bq = min(SB_Q_BLOCK, t)
    bk = min(SB_K_BLOCK, bq)
    return bq, bk, bq // bk


def _sb_fwd(qkv):
    t = qkv.shape[0]
    bq, bk, nd = _sb_blocks(t)
    scale = HEAD_DIM ** -0.5

    def body(q_ref, k_ref, v_ref, o_ref, tot_ref, used_ref):
        i = pl.program_id(1)
        q = q_ref[...]
        rj = lax.broadcasted_iota(jnp.int32, (bk, bk), 0)
        cj = lax.broadcasted_iota(jnp.int32, (bk, bk), 1)
        after = (rj > cj).astype(BF16)
        trow = lax.broadcasted_iota(jnp.int32, (bq, bk), 0)
        scol = lax.broadcasted_iota(jnp.int32, (bq, bk), 1)

        def tile(j, run, acc, mask):
            off = pl.multiple_of(j * bk, bk)
            kb = k_ref[pl.ds(off, bk), :]
            vb = v_ref[pl.ds(off, bk), :]
            ls, lk = _sb_logits(q, kb, mask, scale)
            hi, lo = _split_bf16(lk)
            between = _dot(hi, after) + _dot(lo, after) + run
            a = jnp.exp(ls + between)
            if mask is not None:
                a = jnp.where(mask, a, 0.0)
            acc = acc + _dot(a.astype(BF16), vb)
            return run + jnp.sum(lk, axis=1, keepdims=True), acc

        run, acc = jnp.zeros((bq, 1), F32), jnp.zeros((bq, HEAD_DIM), F32)
        for d in reversed(range(nd)):
            run, acc = tile(i * nd + d, run, acc, scol + d * bk < trow)
        def more(c):
            return jnp.logical_and(c[0] < i * nd, jnp.max(c[1]) > SB_NEGLIGIBLE)

        def far(c):
            run_, acc_ = tile(i * nd - 1 - c[0], c[1], c[2], None)
            return c[0] + 1, run_, acc_

        used, run, acc = lax.while_loop(more, far, (jnp.int32(0), run, acc))
        o_ref[...] = acc.astype(BF16)
        tot_ref[...] = jnp.broadcast_to(run, (bq, HEAD_DIM))
        used_ref[...] = jnp.full(used_ref.shape, used, F32)

    qs = pl.BlockSpec((bq, HEAD_DIM), lambda h, i: (i, h))
    ks = pl.BlockSpec((t, HEAD_DIM), lambda h, i: (0, N_HEADS + h))
    vs = pl.BlockSpec((t, HEAD_DIM), lambda h, i: (0, 2 * N_HEADS + h))
    return pl.pallas_call(
        body, name="sb_fwd", grid=(N_HEADS, t // bq),
        in_specs=[qs, ks, vs], out_specs=[qs, qs, pl.BlockSpec((1, 1, 1, LANES), lambda h, i: (h, i, 0, 0))],
        out_shape=[jax.ShapeDtypeStruct((t, D_MODEL), BF16), jax.ShapeDtypeStruct((t, D_MODEL), F32),
                   jax.ShapeDtypeStruct((N_HEADS, t // bq, 1, LANES), F32)],
        compiler_params=_params(("parallel", "arbitrary")),
    )(qkv, qkv, qkv)


def _sb_bwd(qkv, tot, used, do, partials=()):
    t = qkv.shape[0]
    bq, bk, nd = _sb_blocks(t)
    scale = HEAD_DIM ** -0.5
    n = len(partials)
    nq = t // bq

    def body(q_ref, k_ref, v_ref, tot_ref, used_ref, do_ref, *rest):
        dq_ref, dk_ref, dv_ref = rest[n:n + 3]
        i = pl.program_id(1)
        if n:
            begin, end = _chip_exchange_protocol(rest[:n], rest[n + 3:2 * n + 3], *rest[2 * n + 3:])
            step = pl.program_id(0) * nq + i
            pl.when(step == 0)(begin)

        @pl.when(i == 0)
        def _():
            dk_ref[...] = jnp.zeros_like(dk_ref)
            dv_ref[...] = jnp.zeros_like(dv_ref)

        q = q_ref[...]
        do = do_ref[...]
        total = tot_ref[:, 0:1]
        rj = lax.broadcasted_iota(jnp.int32, (bk, bk), 0)
        cj = lax.broadcasted_iota(jnp.int32, (bk, bk), 1)
        upto = (rj <= cj).astype(BF16)
        before = (rj < cj).astype(BF16)
        trow = lax.broadcasted_iota(jnp.int32, (bq, bk), 0)
        scol = lax.broadcasted_iota(jnp.int32, (bq, bk), 1)

        def tile(j, run_k, run_e, dq, mask):
            off = pl.multiple_of(j * bk, bk)
            kb = k_ref[pl.ds(off, bk), :]
            vb = v_ref[pl.ds(off, bk), :]
            ls, lk = _sb_logits(q, kb, mask, scale)
            hi, lo = _split_bf16(lk)
            between = total - (_dot(hi, upto) + _dot(lo, upto) + run_k)
            a = jnp.exp(ls + between)
            if mask is not None:
                a = jnp.where(mask, a, 0.0)
            e = a * _dot_nt(do, vb)
            ehi, elo = _split_bf16(e)
            pre = _dot(ehi, before) + _dot(elo, before) + run_e
            sig = jnp.exp(ls)
            dz = e * (1.0 - sig) - pre * sig
            if mask is not None:
                dz = jnp.where(mask, dz, 0.0)
            dz = (dz * scale).astype(BF16)
            dq = dq + _dot(dz, kb)
            dk_ref[pl.ds(off, bk), :] += _dot_tn(dz, q)
            dv_ref[pl.ds(off, bk), :] += _dot_tn(a.astype(BF16), do)
            return (run_k + jnp.sum(lk, axis=1, keepdims=True),
                    run_e + jnp.sum(e, axis=1, keepdims=True), dq)

        zero = jnp.zeros((bq, 1), F32)
        visited = jnp.clip(jnp.max(used_ref[...]).astype(jnp.int32), 0, i * nd)
        carry = lax.fori_loop(i * nd - visited, i * nd, lambda j, c: tile(j, c[0], c[1], c[2], None),
                              (zero, zero, jnp.zeros((bq, HEAD_DIM), F32)))
        for d in range(nd):
            carry = tile(i * nd + d, *carry, scol + d * bk < trow)
        dq_ref[...] = carry[2]
        if n:
            pl.when(step == N_HEADS * nq - 1)(end)

    qs = pl.BlockSpec((bq, HEAD_DIM), lambda h, i: (i, h))
    ks = pl.BlockSpec((t, HEAD_DIM), lambda h, i: (0, N_HEADS + h))
    vs = pl.BlockSpec((t, HEAD_DIM), lambda h, i: (0, 2 * N_HEADS + h))
    full = pl.BlockSpec((t, HEAD_DIM), lambda h, i: (0, h))
    big = jax.ShapeDtypeStruct((t, D_MODEL), F32)
    return pl.pallas_call(
        body, name="sb_bwd", grid=(N_HEADS, nq),
        in_specs=[qs, ks, vs, qs, pl.BlockSpec((1, 1, 1, LANES), lambda h, i: (h, i, 0, 0)), qs] + _hbm_specs(n),
        out_specs=[qs, full, full] + _hbm_specs(n),
        out_shape=[big, big, big] + _chip_exchange_shapes(partials),
        scratch_shapes=_chip_exchange_sems(n) if n else [],
        compiler_params=_params(("arbitrary", "arbitrary")),
    )(qkv, qkv, qkv, tot, used, do, *partials)


def _merge_fwd(o_dn, o_sb, gl, x, wp_dn, wp_sb, w_out, w2):
    t = x.shape[0]
    tb = _rows(t)

    def body(odn_ref, osb_ref, gl_ref, x_ref, wpd_ref, wps_ref, wo_ref, w2_ref,
             pdn_ref, psb_ref, mix_ref, x1_ref, n2_ref):
        pdn = _dot(odn_ref[...], wpd_ref[...])
        psb = _dot(osb_ref[...], wps_ref[...])
        gates = jax.nn.sigmoid(gl_ref[...])
        mixed = (gates[:, :D_MODEL] * pdn + gates[:, D_MODEL:] * psb).astype(BF16)
        x1 = x_ref[...] + _dot(mixed, wo_ref[...])
        pdn_ref[...] = pdn.astype(BF16)
        psb_ref[...] = psb.astype(BF16)
        mix_ref[...] = mixed
        x1_ref[...] = x1
        n2_ref[...] = _rms(x1, w2_ref[...]).astype(BF16)

    row = pl.BlockSpec((tb, D_MODEL), lambda i: (i, 0))
    sq = pl.BlockSpec((D_MODEL, D_MODEL), lambda i: (0, 0))
    f = jax.ShapeDtypeStruct((t, D_MODEL), F32)
    b = jax.ShapeDtypeStruct((t, D_MODEL), BF16)
    return pl.pallas_call(
        body, name="merge_fwd", grid=(t // tb,),
        in_specs=[row, row, pl.BlockSpec((tb, 2 * D_MODEL), lambda i: (i, 0)), row, sq, sq, sq,
                  pl.BlockSpec((1, D_MODEL), lambda i: (0, 0))],
        out_specs=[row] * 5, out_shape=[b, b, b, f, b],
        compiler_params=_params(("parallel",)),
    )(o_dn, o_sb, gl, x, wp_dn, wp_sb, w_out, w2)


def _merge_bwd(dx2, dn2, x1, w2, gl, pdn, psb, wp_dn, wp_sb, w_out):
    t = x1.shape[0]
    tb = _rows(t)

    def body(dx2_ref, dn2_ref, x1_ref, w2_ref, gl_ref, pdn_ref, psb_ref, wpd_ref, wps_ref, wo_ref,
             dx1_ref, dw2_ref, dgl_ref, dpdn_ref, dpsb_ref, dodn_ref, dosb_ref):
        i = pl.program_id(0)
        _, vjp = jax.vjp(_rms, x1_ref[...], w2_ref[...])
        dxn, dw2 = vjp(dn2_ref[...])
        dx1 = dx2_ref[...] + dxn
        dx1_ref[...] = dx1

        @pl.when(i == 0)
        def _():
            dw2_ref[...] = jnp.zeros_like(dw2_ref)

        dw2_ref[...] += dw2
        dmix = _dot_nt(dx1.astype(BF16), wo_ref[...])
        gates = jax.nn.sigmoid(gl_ref[...])
        g_dn, g_sb = gates[:, :D_MODEL], gates[:, D_MODEL:]
        dpdn = (dmix * g_dn).astype(BF16)
        dpsb = (dmix * g_sb).astype(BF16)
        dgl_ref[:, :D_MODEL] = (dmix * pdn_ref[...].astype(F32) * g_dn * (1.0 - g_dn)).astype(BF16)
        dgl_ref[:, D_MODEL:] = (dmix * psb_ref[...].astype(F32) * g_sb * (1.0 - g_sb)).astype(BF16)
        dpdn_ref[...] = dpdn
        dpsb_ref[...] = dpsb
        dodn_ref[...] = _dot_nt(dpdn, wpd_ref[...])
        dosb_ref[...] = _dot_nt(dpsb, wps_ref[...]).astype(BF16)

    row = pl.BlockSpec((tb, D_MODEL), lambda i: (i, 0))
    wide = pl.BlockSpec((tb, 2 * D_MODEL), lambda i: (i, 0))
    sq = pl.BlockSpec((D_MODEL, D_MODEL), lambda i: (0, 0))
    vec = pl.BlockSpec((1, D_MODEL), lambda i: (0, 0))
    f = jax.ShapeDtypeStruct((t, D_MODEL), F32)
    b = jax.ShapeDtypeStruct((t, D_MODEL), BF16)
    return pl.pallas_call(
        body, name="merge_bwd", grid=(t // tb,),
        in_specs=[row, row, row, vec, wide, row, row, sq, sq, sq],
        out_specs=[row, vec, wide, row, row, row, row],
        out_shape=[f, jax.ShapeDtypeStruct((1, D_MODEL), F32), jax.ShapeDtypeStruct((t, 2 * D_MODEL), BF16),
                   b, b, f, b],
        compiler_params=_params(("arbitrary",)),
    )(dx2, dn2, x1, w2, gl, pdn, psb, wp_dn, wp_sb, w_out)


def _conv_taps(buf, w_ref, first, rows, cols=slice(None)):
    y = w_ref[0:1, cols] * buf[pl.ds(first, rows), cols]
    for s in range(1, w_ref.shape[0]):
        y = y + w_ref[s:s + 1, cols] * buf[pl.ds(first + s, rows), cols]
    return y


def _ffn_mid_fwd(pre_g, pre_u, wg, wu):
    t, c = pre_g.shape
    kk = wg.shape[0]
    tb, cb = _rows(t), _pick(c, ELEMENTWISE_COLS)
    per = tb // HALO

    def body(g_ref, gh_ref, u_ref, uh_ref, wg_ref, wu_ref, a_ref, gbuf, ubuf):
        i = pl.program_id(0)
        for buf, ref, halo in ((gbuf, g_ref, gh_ref), (ubuf, u_ref, uh_ref)):
            buf[pl.ds(HALO, tb), :] = ref[...]
            buf[pl.ds(0, HALO), :] = jnp.where(i == 0, 0.0, halo[...])
        for j in range(cb // LANES):
            sl = pl.ds(j * LANES, LANES)
            ug = _conv_taps(gbuf, wg_ref, HALO - (kk - 1), tb, sl)
            uu = _conv_taps(ubuf, wu_ref, HALO - (kk - 1), tb, sl)
            a_ref[:, sl] = (_silu(ug) * uu).astype(BF16)

    blk = pl.BlockSpec((tb, cb), lambda i, j: (i, j))
    halo = pl.BlockSpec((HALO, cb), lambda i, j: (jnp.maximum(i * per - 1, 0), j))
    wspec = pl.BlockSpec((kk, cb), lambda i, j: (0, j))
    return pl.pallas_call(
        body, name="ffn_mid_fwd", grid=(t // tb, c // cb),
        in_specs=[blk, halo, blk, halo, wspec, wspec], out_specs=blk,
        out_shape=jax.ShapeDtypeStruct((t, c), BF16),
        scratch_shapes=[pltpu.VMEM((tb + HALO, cb), F32)] * 2,
        compiler_params=_params(("parallel", "parallel")),
    )(pre_g, pre_g, pre_u, pre_u, wg, wu)


def _ffn_mid_bwd(pre_g, pre_u, wg, wu, da):
    t, c = pre_g.shape
    kk = wg.shape[0]
    tb, cb = _rows(t), _pick(c, ELEMENTWISE_COLS)
    per = tb // HALO
    nblk = t // tb
    ext = tb + HALO

    def body(g_ref, gb_ref, ga_ref, u_ref, ub_ref, ua_ref, da_ref, daa_ref, wg_ref, wu_ref,
             dg_ref, du_ref, dwg_ref, dwu_ref, gbuf, ubuf, dabuf, dgbuf, dubuf):
        i = pl.program_id(1)
        last = i == nblk - 1
        for buf, ref, before, after in ((gbuf, g_ref, gb_ref, ga_ref), (ubuf, u_ref, ub_ref, ua_ref)):
            buf[pl.ds(0, HALO), :] = jnp.where(i == 0, 0.0, before[...])
            buf[pl.ds(HALO, tb), :] = ref[...]
            buf[pl.ds(HALO + tb, HALO), :] = jnp.where(last, 0.0, after[...])
        dabuf[pl.ds(0, tb), :] = da_ref[...]
        dabuf[pl.ds(tb, HALO), :] = jnp.where(last, 0.0, daa_ref[...])

        @pl.when(i == 0)
        def _():
            dwg_ref[...] = jnp.zeros_like(dwg_ref)
            dwu_ref[...] = jnp.zeros_like(dwu_ref)

        for j in range(cb // LANES):
            sl = pl.ds(j * LANES, LANES)
            ug = _conv_taps(gbuf, wg_ref, HALO - (kk - 1), ext, sl)
            uu = _conv_taps(ubuf, wu_ref, HALO - (kk - 1), ext, sl)
            _, vjp = jax.vjp(lambda g, u: _silu(g) * u, ug, uu)
            dgbuf[:, sl], dubuf[:, sl] = vjp(dabuf[:, sl])
            for dbuf, xbuf, w_ref, dx_ref, dw_ref in ((dgbuf, gbuf, wg_ref, dg_ref, dwg_ref),
                                                      (dubuf, ubuf, wu_ref, du_ref, dwu_ref)):
                x = xbuf[pl.ds(HALO, tb), sl]
                dx = None
                for s in range(kk):
                    shifted = dbuf[pl.ds(kk - 1 - s, tb), sl]
                    term = w_ref[s:s + 1, sl] * shifted
                    dx = term if dx is None else dx + term
                    dw_ref[s:s + 1, sl] += jnp.sum(shifted * x, axis=0, keepdims=True)
                dx_ref[:, sl] = dx.astype(BF16)

    blk = pl.BlockSpec((tb, cb), lambda j, i: (i, j))
    before = pl.BlockSpec((HALO, cb), lambda j, i: (jnp.maximum(i * per - 1, 0), j))
    after = pl.BlockSpec((HALO, cb), lambda j, i: (jnp.minimum((i + 1) * per, t // HALO - 1), j))
    wspec = pl.BlockSpec((kk, cb), lambda j, i: (0, j))
    dwspec = pl.BlockSpec((HALO, cb), lambda j, i: (0, j))
    half = jax.ShapeDtypeStruct((t, c), BF16)
    dwshape = jax.ShapeDtypeStruct((HALO, c), F32)
    return pl.pallas_call(
        body, name="ffn_mid_bwd", grid=(c // cb, nblk),
        in_specs=[blk, before, after, blk, before, after, blk, after, wspec, wspec],
        out_specs=[blk, blk, dwspec, dwspec],
        out_shape=[half, half, dwshape, dwshape],
        scratch_shapes=[pltpu.VMEM((ext + HALO, cb), F32)] * 2 + [pltpu.VMEM((ext, cb), F32)] * 3,
        compiler_params=_params(("parallel", "arbitrary")),
    )(pre_g, pre_g, pre_g, pre_u, pre_u, pre_u, da, da, wg, wu)


def _down_loss(a, w_down, x1, wf, target):
    t = x1.shape[0]
    tb = _rows(t)

    def body(a_ref, wd_ref, x1_ref, wf_ref, tgt_ref, dx2_ref, dwf_ref, loss_ref):
        i = pl.program_id(0)
        x2 = x1_ref[...] + _dot(a_ref[...], wd_ref[...])
        y, vjp = jax.vjp(_rms, x2, wf_ref[...])
        err = y - tgt_ref[...]
        dx2, dwf = vjp(err * (1.0 / D_MODEL))
        dx2_ref[...] = dx2
        part = jnp.sum(jnp.sum(err * err, axis=1, keepdims=True), axis=0, keepdims=True) * (0.5 / D_MODEL)

        @pl.when(i == 0)
        def _():
            dwf_ref[...] = jnp.zeros_like(dwf_ref)
            loss_ref[...] = jnp.zeros_like(loss_ref)

        dwf_ref[...] += dwf
        loss_ref[...] += jnp.broadcast_to(part, loss_ref.shape)

    row = pl.BlockSpec((tb, D_MODEL), lambda i: (i, 0))
    vec = pl.BlockSpec((1, D_MODEL), lambda i: (0, 0))
    return pl.pallas_call(
        body, name="down_loss", grid=(t // tb,),
        in_specs=[pl.BlockSpec((tb, D_FF), lambda i: (i, 0)), pl.BlockSpec((D_FF, D_MODEL), lambda i: (0, 0)),
                  row, vec, row],
        out_specs=[row, vec, pl.BlockSpec((1, LANES), lambda i: (0, 0))],
        out_shape=[jax.ShapeDtypeStruct((t, D_MODEL), F32), jax.ShapeDtypeStruct((1, D_MODEL), F32),
                   jax.ShapeDtypeStruct((1, LANES), F32)],
        compiler_params=_params(("arbitrary",)),
    )(a, w_down, x1, wf, target)


def _local_step(x, target, wts, late_wire=(), late_weights=None, early_partials=None):
    t = x.shape[0]
    nchunk = t // DN_CHUNK

    n1, hab = _norm1_fwd(x, wts["norm1"], wts["w_ab"])
    dnqkv = _mm(n1, wts["w_dnqkv"], name="h_dnqkv")
    dngate = _mm(n1, wts["w_dngate"], name="h_dngate")
    sbqkv = _mm(n1, wts["w_sbqkv"], out_dtype=BF16, name="h_sbqkv")
    gl = _mm(n1, wts["w_gl"], name="h_gl")

    cdn = _conv_fwd(dnqkv, wts["dn_conv"], "dn_conv_fwd")
    qn, kn, vv, gb = _dn_prep_fwd(cdn, hab, wts["alog"], wts["dtb"])
    per_head = gb[:, :2 * N_HEADS].T.reshape(2 * N_HEADS, nchunk, DN_CHUNK)
    grow, brow = per_head[:N_HEADS, :, None, :], per_head[N_HEADS:, :, None, :]
    u_dn, w_dn, a_qk, qe, kdec, egl, tinv, *late = _dn_local_fwd(qn, kn, vv, grow, brow, late_wire)
    if late_wire:
        wts = {**wts, **late_weights(late)}
    o_raw, states = _dn_seq_fwd(u_dn, w_dn, a_qk, qe, kdec, egl)
    o_dn = _dn_post_fwd(o_raw, dngate, wts["dn_norm"])

    o_sb, tot, sb_used = _sb_fwd(sbqkv)

    pdn, psb, mixed, x1, n2 = _merge_fwd(o_dn, o_sb, gl, x, wts["wp_dn"], wts["wp_sb"], wts["w_out"],
                                         wts["norm2"])
    pre_g = _mm(n2, wts["w_up_g"], name="ffn_up_g")
    pre_u = _mm(n2, wts["w_up_u"], name="ffn_up_u")
    act = _ffn_mid_fwd(pre_g, pre_u, wts["ffn_conv_g"], wts["ffn_conv_u"])
    dx2, d_normf, loss_part = _down_loss(act, wts["w_down"], x1, wts["normf"], target)

    grads = {"normf": d_normf}
    da = _mm(dx2, wts["w_down"], tb=True, name="d_act")
    grads["w_down"] = _mm(act, dx2, ta=True, out_dtype=BF16, name="dw_down")
    dpre_g, dpre_u, dcw_g, dcw_u = _ffn_mid_bwd(pre_g, pre_u, wts["ffn_conv_g"], wts["ffn_conv_u"], da)
    grads["ffn_conv"] = jnp.concatenate([dcw_g[:FFN_CONV], dcw_u[:FFN_CONV]], axis=1)
    dn2 = _mm(dpre_g, wts["w_up_g"], tb=True, name="dn2_g")
    dn2 = _mm(dpre_u, wts["w_up_u"], tb=True, add=dn2, name="dn2_u")
    grads["w_up_g"] = _mm(n2, dpre_g, ta=True, out_dtype=BF16, name="dw_up_g")
    grads["w_up_u"] = _mm(n2, dpre_u, ta=True, out_dtype=BF16, name="dw_up_u")

    dx1, grads["norm2"], dgl, dpdn, dpsb, do_dn, do_sb = _merge_bwd(
        dx2, dn2, x1, wts["norm2"], gl, pdn, psb, wts["wp_dn"], wts["wp_sb"], wts["w_out"])
    grads["w_out"] = _mm(mixed, dx1, ta=True, out_dtype=BF16, name="dw_out")
    grads["wp_dn"] = _mm(o_dn, dpdn, ta=True, out_dtype=BF16, name="dw_proj_dn")
    grads["wp_sb"] = _mm(o_sb, dpsb, ta=True, out_dtype=BF16, name="dw_proj_sb")

    partials = early_partials(grads) if early_partials else ()
    dsq, dsk, dsv, *arrived = _sb_bwd(sbqkv, tot, sb_used, do_sb, partials)
    dsbqkv = jnp.concatenate([dsq, dsk, dsv], axis=1).astype(BF16)

    do_raw, ddngate, grads["dn_norm"] = _dn_post_bwd(o_raw, dngate, wts["dn_norm"], do_dn)
    seq_grads = _dn_seq_bwd(u_dn, w_dn, a_qk, qe, kdec, egl, states, do_raw)
    dqn, dkn, dvv, dgrow, dbrow = _dn_local_bwd(qn, kn, vv, grow, brow, tinv, *seq_grads)
    dgb = jnp.concatenate([dgrow.reshape(N_HEADS, t), dbrow.reshape(N_HEADS, t)], axis=0).T
    dgb = jnp.pad(dgb, ((0, 0), (0, LANES - 2 * N_HEADS)))
    dcdn, dhab, grads["alog"], grads["dtb"] = _dn_prep_bwd(cdn, hab, wts["alog"], wts["dtb"], dqn, dkn, dvv, dgb)
    ddnqkv, dcw_dn = _conv_bwd(dcdn, dnqkv, wts["dn_conv"], "dn_conv_bwd", BF16)
    grads["dn_conv"] = dcw_dn[:DN_CONV]

    dn1 = _mm(ddnqkv, wts["w_dnqkv"], tb=True, name="dn1_dnqkv")
    dn1 = _mm(ddngate, wts["w_dngate"], tb=True, add=dn1, name="dn1_dngate")
    dn1 = _mm(dsbqkv, wts["w_sbqkv"], tb=True, add=dn1, name="dn1_sbqkv")
    dn1 = _mm(dgl, wts["w_gl"], tb=True, add=dn1, name="dn1_gl")
    grads["w_dnqkv"] = _mm(n1, ddnqkv, ta=True, out_dtype=BF16, name="dw_dnqkv")
    grads["w_dngate"] = _mm(n1, ddngate, ta=True, out_dtype=BF16, name="dw_dngate")
    grads["w_sbqkv"] = _mm(n1, dsbqkv, ta=True, out_dtype=BF16, name="dw_sbqkv")
    grads["w_gl"] = _mm(n1, dgl, ta=True, out_dtype=BF16, name="dw_gl")
    grads["w_ab"] = _mm(n1, dhab, ta=True, out_dtype=BF16, name="dw_ab")
    grad_x, grads["norm1"] = _norm1_bwd(x, wts["norm1"], dn1, dx1, dhab, wts["w_ab"])
    return loss_part, grad_x, grads, (list(partials), arrived)


def _place():
    return lax.axis_index("x"), lax.axis_index("y"), lax.axis_index("c")


def _hbm_specs(n):
    return [pl.BlockSpec(memory_space=pltpu.HBM)] * n


GATHER_SEMS = 8
GATHER_FORWARD_AT = 5


def _gather_protocol(ins, outs, send_sems, recv_sems):
    n = len(ins)
    x, y, c = _place()
    me = 2 * x + y
    sibling = (x, y, 1 - c)
    xn, yn, dg = (1 - x, y), (x, 1 - y), (1 - x, 1 - y)
    idx = lambda chip: 2 * chip[0] + chip[1]

    def part(a, chip_index, core, quarter=None):
        half = ins[a].shape[0] // 2
        if quarter is None:
            return outs[a].at[chip_index, pl.ds(core * half, half), :]
        return outs[a].at[chip_index, pl.ds(core * half + quarter * (half // 2), half // 2), :]

    def copy(a, k, src, dst, to):
        return pltpu.make_async_remote_copy(src_ref=src, dst_ref=dst, send_sem=send_sems.at[GATHER_SEMS * a + k],
                                            recv_sem=recv_sems.at[GATHER_SEMS * a + k], device_id=to,
                                            device_id_type=MESH)

    def sent(a, k):
        half = ins[a].shape[0] // 2
        my_half = ins[a].at[pl.ds(c * half, half), :]
        if k < 2:
            return copy(a, k, my_half, part(a, me, c), (*(xn, yn)[k], c))
        if k < 4:
            src = part(a, idx((xn, yn)[k - 2]), c, k - 2)
            return copy(a, k, src, src, (*(yn, xn)[k - 2], c))
        src = (part(a, idx(xn), c), part(a, idx(yn), c), part(a, idx(dg), c, 0), part(a, idx(dg), c, 1))[k - 4]
        return copy(a, k, src, src, sibling)

    def landed(a, k):
        dst = (part(a, idx(xn), c), part(a, idx(yn), c), part(a, idx(dg), c, 0), part(a, idx(dg), c, 1),
               part(a, idx(xn), 1 - c), part(a, idx(yn), 1 - c), part(a, idx(dg), 1 - c, 0),
               part(a, idx(dg), 1 - c, 1))[k]
        return copy(a, k, dst, dst, sibling)

    def begin():
        for a in range(n):
            sent(a, 0).start()
            sent(a, 1).start()

    def middle():
        for a in range(n):
            for k in range(2):
                landed(a, k).wait_recv()
                sent(a, 2 + k).start()
                sent(a, 4 + k).start()

    def end():
        for a in range(n):
            for k in (2, 3):
                landed(a, k).wait_recv()
                sent(a, 4 + k).start()
        for a in range(n):
            for k in range(4, GATHER_SEMS):
                landed(a, k).wait_recv()
        for a in range(n):
            for k in range(GATHER_SEMS):
                sent(a, k).wait_send()

    return begin, middle, end


def _gather_out_shapes(shards):
    return [jax.ShapeDtypeStruct((N_CHIPS,) + s.shape, s.dtype) for s in shards]


def _gather_sems(n):
    return [pltpu.SemaphoreType.DMA((GATHER_SEMS * n,)), pltpu.SemaphoreType.DMA((GATHER_SEMS * n,))]


def _gather_shards(shards):
    n = len(shards)

    def body(*refs):
        begin, middle, end = _gather_protocol(refs[:n], refs[n:2 * n], *refs[2 * n:])
        begin()
        middle()
        end()

    return pl.pallas_call(
        body, name="gather_weights", in_specs=_hbm_specs(n), out_specs=_hbm_specs(n),
        out_shape=_gather_out_shapes(shards), scratch_shapes=_gather_sems(n),
    )(*shards)


def _pair_exchange_halves(gs, tag):
    n = len(gs)

    def body(*refs):
        ins, outs, (send_sems, recv_sems) = refs[:n], refs[n:2 * n], refs[2 * n:]
        x, y, c = _place()
        cps = []
        for a in range(n):
            half = ins[a].shape[1] // 2
            cp = pltpu.make_async_remote_copy(src_ref=ins[a].at[:, pl.ds((1 - c) * half, half), :], dst_ref=outs[a],
                                              send_sem=send_sems.at[a], recv_sem=recv_sems.at[a],
                                              device_id=(x, y, 1 - c), device_id_type=MESH)
            cp.start()
            cps.append(cp)
        for cp in cps:
            cp.wait()

    return pl.pallas_call(
        body, name="grad_pair_exchange_" + tag, in_specs=_hbm_specs(n), out_specs=_hbm_specs(n),
        out_shape=[jax.ShapeDtypeStruct((g.shape[0], g.shape[1] // 2, g.shape[2]), g.dtype) for g in gs],
        scratch_shapes=[pltpu.SemaphoreType.DMA((n,)), pltpu.SemaphoreType.DMA((n,))],
    )(*gs)


def _pick_rows(n, target=1024):
    best = 16
    for b in range(16, min(n, target) + 1, 16):
        if n % b == 0:
            best = b
    return best


def _pair_add(g, got, c_idx, tag):
    nsh, rows, cols = g.shape
    half = rows // 2
    rb = _pick_rows(half)

    def body(c_ref, g_ref, got_ref, o_ref):
        o_ref[...] = (g_ref[...].astype(F32) + got_ref[...].astype(F32)).astype(BF16)

    nb = half // rb
    grid_spec = pltpu.PrefetchScalarGridSpec(
        num_scalar_prefetch=1, grid=(nsh, nb),
        in_specs=[pl.BlockSpec((1, rb, cols), lambda s, i, c_ref: (s, c_ref[0] * nb + i, 0)),
                  pl.BlockSpec((1, rb, cols), lambda s, i, c_ref: (s, i, 0))],
        out_specs=pl.BlockSpec((1, rb, cols), lambda s, i, c_ref: (s, i, 0)))
    return pl.pallas_call(
        body, name="grad_pair_add_" + tag, grid_spec=grid_spec,
        out_shape=jax.ShapeDtypeStruct((nsh, half, cols), BF16),
        compiler_params=_params(("parallel", "parallel")),
    )(c_idx, g, got)


def _chip_exchange_protocol(ins, outs, send_sems, recv_sems):
    x, y, c = _place()
    chips = [(1 - x, y), (x, 1 - y), (1 - x, 1 - y)]

    def copies():
        return [pltpu.make_async_remote_copy(src_ref=ins[a].at[2 * px + py], dst_ref=outs[a].at[j],
                                             send_sem=send_sems.at[3 * a + j], recv_sem=recv_sems.at[3 * a + j],
                                             device_id=(px, py, c), device_id_type=MESH)
                for a in range(len(ins)) for j, (px, py) in enumerate(chips)]

    def begin():
        for cp in copies():
            cp.start()

    def end():
        for cp in copies():
            cp.wait_recv()
        for cp in copies():
            cp.wait_send()

    return begin, end


def _chip_exchange_shapes(ps):
    return [jax.ShapeDtypeStruct((N_CHIPS - 1,) + p.shape[1:], p.dtype) for p in ps]


def _chip_exchange_sems(n):
    return [pltpu.SemaphoreType.DMA((3 * n,)), pltpu.SemaphoreType.DMA((3 * n,))]


def _chip_exchange(ps):
    n = len(ps)

    def body(*refs):
        begin, end = _chip_exchange_protocol(refs[:n], refs[n:2 * n], *refs[2 * n:])
        begin()
        end()

    return pl.pallas_call(
        body, name="grad_chip_exchange", in_specs=_hbm_specs(n), out_specs=_hbm_specs(n),
        out_shape=_chip_exchange_shapes(ps), scratch_shapes=_chip_exchange_sems(n),
    )(*ps)


def _sum_partials(p, got, chip_idx, tag):
    nsh, half, cols = got.shape
    rb = _pick_rows(half)

    def body(me_ref, p_ref, got_ref, o_ref):
        acc = p_ref[0].astype(F32)
        for s in range(nsh):
            acc = acc + got_ref[s].astype(F32)
        o_ref[...] = acc

    grid_spec = pltpu.PrefetchScalarGridSpec(
        num_scalar_prefetch=1, grid=(half // rb,),
        in_specs=[pl.BlockSpec((1, rb, cols), lambda i, me_ref: (me_ref[0], i, 0)),
                  pl.BlockSpec((nsh, rb, cols), lambda i, me_ref: (0, i, 0))],
        out_specs=pl.BlockSpec((rb, cols), lambda i, me_ref: (i, 0)))
    return pl.pallas_call(
        body, name="grad_sum_chips_" + tag, grid_spec=grid_spec,
        out_shape=jax.ShapeDtypeStruct((half, cols), F32),
        compiler_params=_params(("parallel",)),
    )(chip_idx, p, got)


def _pair_share(rs):
    n = len(rs)

    def body(*refs):
        ins, outs, (send_sems, recv_sems) = refs[:n], refs[n:2 * n], refs[2 * n:]
        x, y, c = _place()
        cps = []
        for a in range(n):
            cp = pltpu.make_async_remote_copy(src_ref=ins[a], dst_ref=outs[a], send_sem=send_sems.at[a],
                                              recv_sem=recv_sems.at[a], device_id=(x, y, 1 - c),
                                              device_id_type=MESH)
            cp.start()
            cps.append(cp)
        for cp in cps:
            cp.wait()

    return pl.pallas_call(
        body, name="grad_pair_share", in_specs=_hbm_specs(n), out_specs=_hbm_specs(n),
        out_shape=[jax.ShapeDtypeStruct(r.shape, r.dtype) for r in rs],
        scratch_shapes=[pltpu.SemaphoreType.DMA((n,)), pltpu.SemaphoreType.DMA((n,))],
    )(*rs)


def _small_allreduce(v):
    rows, cols = v.shape
    ndev = 8

    def body(in_ref, out_ref, slots, send_sems, recv_sems):
        x, y, c = _place()
        me = 4 * x + 2 * y + c
        slots[me] = in_ref[...]
        sends = []
        for k in range(1, ndev):
            peer = (x ^ (k >> 2), y ^ ((k >> 1) & 1), c ^ (k & 1))
            cp = pltpu.make_async_remote_copy(src_ref=in_ref, dst_ref=slots.at[me], send_sem=send_sems.at[k - 1],
                                              recv_sem=recv_sems.at[k - 1], device_id=peer, device_id_type=MESH)
            cp.start()
            sends.append(cp)
        for k in range(1, ndev):
            there = slots.at[me ^ k]
            pltpu.make_async_remote_copy(src_ref=there, dst_ref=there, send_sem=send_sems.at[k - 1],
                                         recv_sem=recv_sems.at[k - 1], device_id=(x, y, c),
                                         device_id_type=MESH).wait_recv()
        for cp in sends:
            cp.wait_send()
        acc = slots[0]
        for s in range(1, ndev):
            acc = acc + slots[s]
        out_ref[...] = acc

    return pl.pallas_call(
        body, name="small_allreduce",
        in_specs=[pl.BlockSpec(memory_space=pltpu.VMEM)],
        out_specs=pl.BlockSpec(memory_space=pltpu.VMEM),
        out_shape=jax.ShapeDtypeStruct((rows, cols), F32),
        scratch_shapes=[pltpu.VMEM((ndev, rows, cols), F32), pltpu.SemaphoreType.DMA((ndev - 1,)),
                        pltpu.SemaphoreType.DMA((ndev - 1,))],
    )(v)


def _adamw(w, g, m, v, name):
    r, c = w.shape
    rb = r if r <= 128 else _pick_rows_8(r, 128)
    c1 = 1.0 - ADAM_B1 ** ADAM_STEP
    c2 = 1.0 - ADAM_B2 ** ADAM_STEP

    def body(w_ref, g_ref, m_ref, v_ref, d_ref, nm_ref, nv_ref):
        gg = g_ref[...]
        nm = ADAM_B1 * m_ref[...] + (1.0 - ADAM_B1) * gg
        nv = ADAM_B2 * v_ref[...] + (1.0 - ADAM_B2) * (gg * gg)
        d_ref[...] = -ADAM_LR * ((nm / c1) / (jnp.sqrt(nv / c2) + ADAM_EPS) + ADAM_WD * w_ref[...])
        nm_ref[...] = nm
        nv_ref[...] = nv

    blk = pl.BlockSpec((rb, c), lambda i: (i, 0))
    shp = jax.ShapeDtypeStruct((r, c), F32)
    return pl.pallas_call(
        body, name=name, grid=(r // rb,), in_specs=[blk] * 4, out_specs=[blk] * 3, out_shape=[shp] * 3,
        compiler_params=_params(("parallel",)),
    )(w, g, m, v)


def _pick_rows_8(n, target):
    best = n
    for b in range(8, min(n, target) + 1, 8):
        if n % b == 0:
            best = b
    return best


W_IN_COLS = 2308
W_UP_COLS = 1408
W_DOWN_ROWS = 704
DN_CONV_COLS = 768
FFN_CONV_COLS = 1408
PROJ_ROWS = 256
ROW_TILE = 16
ROW_SEGS = [("wp_dn", PROJ_ROWS), ("wp_sb", PROJ_ROWS), ("w_out", PROJ_ROWS), ("w_down", W_DOWN_ROWS),
            ("dn_conv", ROW_TILE), ("ffn_conv", ROW_TILE), ("spare", 2 * ROW_TILE)]
ROW_OFFS = {nm: (sum(n for _, n in ROW_SEGS[:i]), n) for i, (nm, n) in enumerate(ROW_SEGS)}
STACK_ROWS = sum(n for _, n in ROW_SEGS)
assert all(n % ROW_TILE == 0 for _, n in ROW_SEGS) and STACK_ROWS % (4 * ROW_TILE) == 0
Q_END, A_END, G_END, S_END = 3 * D_MODEL, 3 * D_MODEL + 2 * N_HEADS, 4 * D_MODEL + 2 * N_HEADS, 7 * D_MODEL + 2 * N_HEADS


def _flat_rows(a, nrows):
    flat = a.reshape(-1)
    return jnp.pad(flat, (0, nrows * D_MODEL - flat.shape[0])).reshape(nrows, D_MODEL)


IN_EXTRA_ROWS = 64


def _weight_wire(w_in, wp_dn, wp_sb, w_out, w_up, w_down, dn_conv, ffn_conv):
    bits = lax.bitcast_convert_type(dn_conv, BF16).reshape(-1)
    extra = jnp.pad(bits, (0, IN_EXTRA_ROWS * W_IN_COLS - bits.shape[0])).reshape(IN_EXTRA_ROWS, W_IN_COLS)
    stack = jnp.concatenate([wp_dn.astype(BF16), wp_sb.astype(BF16), w_out.astype(BF16), w_down.astype(BF16),
                             jnp.zeros((ROW_TILE, D_MODEL), BF16),
                             _flat_rows(lax.bitcast_convert_type(ffn_conv, BF16), ROW_TILE),
                             jnp.zeros((ROW_OFFS["spare"][1], D_MODEL), BF16)], axis=0)
    return [jnp.concatenate([w_in.astype(BF16), extra], axis=0)], [w_up.astype(BF16), stack]


def _col_range(g, lo, hi, width):
    parts = []
    for s in range(g.shape[0]):
        a, b = max(lo, s * width), min(hi, (s + 1) * width)
        if a < b:
            parts.append(g[s][:, a - s * width:b - s * width])
    return parts[0] if len(parts) == 1 else jnp.concatenate(parts, axis=1)


def _f32_rows(raw, k, ncols):
    raw = raw.reshape(N_CHIPS, -1)[:, :2 * k * ncols].reshape(N_CHIPS, k * ncols, 2)
    vals = lax.bitcast_convert_type(raw, F32).reshape(N_CHIPS, k, ncols)
    return vals.transpose(1, 0, 2).reshape(k, N_CHIPS * ncols)


def _unpack_early(g_in):
    w = g_in[:, :D_MODEL, :]
    return {
        "w_dnqkv": _col_range(w, 0, Q_END, W_IN_COLS),
        "w_ab": jnp.pad(_col_range(w, Q_END, A_END, W_IN_COLS), ((0, 0), (0, LANES - 2 * N_HEADS))),
        "w_dngate": _col_range(w, A_END, G_END, W_IN_COLS),
        "w_sbqkv": _col_range(w, G_END, S_END, W_IN_COLS),
        "w_gl": _col_range(w, S_END, N_CHIPS * W_IN_COLS, W_IN_COLS),
        "dn_conv": _f32_rows(g_in[:, D_MODEL:, :], DN_CONV, DN_CONV_COLS),
    }


def _unpack_late(g_up, g_stack):
    def seg(nm):
        at, n = ROW_OFFS[nm]
        return g_stack[:, at:at + n, :]

    ffn_conv = _f32_rows(seg("ffn_conv"), FFN_CONV, FFN_CONV_COLS)
    return {
        "wp_dn": seg("wp_dn").reshape(D_MODEL, D_MODEL),
        "wp_sb": seg("wp_sb").reshape(D_MODEL, D_MODEL),
        "w_out": seg("w_out").reshape(D_MODEL, D_MODEL),
        "w_up_g": _col_range(g_up, 0, D_FF, W_UP_COLS), "w_up_u": _col_range(g_up, D_FF, 2 * D_FF, W_UP_COLS),
        "w_down": seg("w_down").reshape(D_FF, D_MODEL),
        "ffn_conv_g": ffn_conv[:, :D_FF], "ffn_conv_u": ffn_conv[:, D_FF:],
    }


def _grad_wire_early(gr):
    def cols(a, ncols):
        return a.reshape(a.shape[0], N_CHIPS, ncols).transpose(1, 0, 2)

    def rows(a, nrows):
        return a.astype(BF16).reshape(N_CHIPS, nrows, a.shape[1])

    def flat(a, nrows):
        a = a.astype(BF16).reshape(N_CHIPS, -1)
        return jnp.pad(a, ((0, 0), (0, nrows * D_MODEL - a.shape[1]))).reshape(N_CHIPS, nrows, D_MODEL)

    up = [gr["w_up_g"], gr["w_up_u"]]
    g_up = jnp.stack([up[s // 2][:, (s % 2) * W_UP_COLS:(s % 2 + 1) * W_UP_COLS].astype(BF16) for s in range(N_CHIPS)])
    g_stack = jnp.concatenate([rows(gr["wp_dn"], PROJ_ROWS), rows(gr["wp_sb"], PROJ_ROWS), rows(gr["w_out"], PROJ_ROWS),
                               rows(gr["w_down"], W_DOWN_ROWS), jnp.zeros((N_CHIPS, ROW_TILE, D_MODEL), BF16),
                               flat(cols(gr["ffn_conv"], FFN_CONV_COLS), ROW_TILE),
                               jnp.zeros((N_CHIPS, ROW_OFFS["spare"][1], D_MODEL), BF16)], axis=1)
    return [g_up, g_stack]


def _grad_wire_late(gr):
    pieces = [(gr["w_dnqkv"], 0), (gr["w_ab"][:, :2 * N_HEADS], Q_END), (gr["w_dngate"], A_END),
              (gr["w_sbqkv"], G_END), (gr["w_gl"], S_END)]
    conv = gr["dn_conv"].reshape(DN_CONV, N_CHIPS, DN_CONV_COLS).transpose(1, 0, 2).reshape(N_CHIPS, -1)

    def block(s):
        lo, hi = s * W_IN_COLS, (s + 1) * W_IN_COLS
        parts = []
        for a, at in pieces:
            b0, b1 = max(lo, at), min(hi, at + a.shape[1])
            if b0 < b1:
                parts.append(a[:, b0 - at:b1 - at].astype(BF16))
        w = parts[0] if len(parts) == 1 else jnp.concatenate(parts, axis=1)
        extra = jnp.pad(conv[s].astype(BF16), (0, IN_EXTRA_ROWS * W_IN_COLS - conv.shape[1]))
        return jnp.concatenate([w, extra.reshape(IN_EXTRA_ROWS, W_IN_COLS)], axis=0)

    return [jnp.stack([block(s) for s in range(N_CHIPS)])]


def _unpack_grad_shard(r_in, r_up, r_stack):
    def seg(nm):
        at, n = ROW_OFFS[nm]
        return r_stack[at:at + n, :]

    return {
        "w_in": r_in[:D_MODEL], "w_up": r_up,
        "wp_dn": seg("wp_dn"), "wp_sb": seg("wp_sb"), "w_out": seg("w_out"), "w_down": seg("w_down"),
        "dn_conv": r_in[D_MODEL:].reshape(-1)[:DN_CONV * DN_CONV_COLS].reshape(DN_CONV, DN_CONV_COLS),
        "ffn_conv": seg("ffn_conv").reshape(-1)[:FFN_CONV * FFN_CONV_COLS].reshape(FFN_CONV, FFN_CONV_COLS),
    }


def _lane_row(v):
    return jnp.pad(v.reshape(1, -1), ((0, 0), (0, LANES - v.size)))


def kernel(x, norm1_w, w_in, dn_conv_w, dn_A_log, dn_dt_bias, dn_norm_w, w_proj_dn, w_proj_sb, w_out, norm2_w, ffn_w_up, ffn_conv_w, ffn_w_down, norm_f_w, loss_target, m_norm1_w, m_w_in, m_dn_conv_w, m_dn_A_log, m_dn_dt_bias, m_dn_norm_w, m_w_proj_dn, m_w_proj_sb, m_w_out, m_norm2_w, m_ffn_w_up, m_ffn_conv_w, m_ffn_w_down, m_norm_f_w, v_norm1_w, v_w_in, v_dn_conv_w, v_dn_A_log, v_dn_dt_bias, v_dn_norm_w, v_w_proj_dn, v_w_proj_sb, v_w_out, v_norm2_w, v_ffn_w_up, v_ffn_conv_w, v_ffn_w_down, v_norm_f_w):
    early, late = _weight_wire(w_in[0], w_proj_dn[0], w_proj_sb[0], w_out[0], ffn_w_up[0], ffn_w_down[0],
                               dn_conv_w[0], ffn_conv_w[0])
    chip_idx = (2 * lax.axis_index("x") + lax.axis_index("y")).astype(jnp.int32)

    def with_mine(gathered, wire):
        return [lax.dynamic_update_slice(g, mine[None], (chip_idx, 0, 0)) for g, mine in zip(gathered, wire)]

    wts = _unpack_early(*with_mine(_gather_shards(early), early))
    wts.update(norm1=norm1_w, norm2=norm2_w, normf=norm_f_w.reshape(1, D_MODEL), dn_norm=dn_norm_w,
               alog=_lane_row(dn_A_log), dtb=_lane_row(dn_dt_bias))

    c_idx = lax.axis_index("c").astype(jnp.int32).reshape(1)

    def pair_sums(wire_g, tags, when):
        return [_pair_add(g, got, c_idx, tag) for g, got, tag in zip(wire_g, _pair_exchange_halves(wire_g, when), tags)]

    loss_part, grad_x, gr, (early_sums, early_arrived) = _local_step(
        x[0], loss_target[0], wts, late, lambda gathered: _unpack_late(*with_mine(gathered, late)),
        lambda grads: pair_sums(_grad_wire_early(grads), ["w_up", "rows"], "early"))

    late_sums = pair_sums(_grad_wire_late(gr), ["w_in"], "late")
    tags = ["w_in", "w_up", "rows"]
    reduced = [_sum_partials(p, got, chip_idx.reshape(1), tag)
               for p, got, tag in zip(late_sums + early_sums, list(_chip_exchange(late_sums)) + list(early_arrived), tags)]
    is_south = lax.axis_index("c") == 0
    gsh = _unpack_grad_shard(*[jnp.concatenate([jnp.where(is_south, mine, other), jnp.where(is_south, other, mine)],
                                               axis=0) for mine, other in zip(reduced, _pair_share(reduced))])

    tail = jnp.concatenate([gr["dn_norm"], gr["alog"][:, :N_HEADS], gr["dtb"][:, :N_HEADS], loss_part[:, :1]], axis=1)
    small = jnp.concatenate([gr["norm1"], gr["norm2"], gr["normf"],
                             jnp.pad(tail, ((0, 0), (0, D_MODEL - tail.shape[1]))),
                             jnp.zeros((SMALL_ROWS - 4, D_MODEL), F32)], axis=0)
    small = _small_allreduce(small)
    at = HEAD_DIM
    g_small = {"norm1_w": small[0:1], "norm2_w": small[1:2], "norm_f_w": small[2],
               "dn_norm_w": small[3:4, :at], "dn_A_log": small[3:4, at:at + N_HEADS],
               "dn_dt_bias": small[3:4, at + N_HEADS:at + 2 * N_HEADS]}
    loss = small[3, at + 2 * N_HEADS]

    big = {"w_in": (w_in, m_w_in, v_w_in, gsh["w_in"]), "dn_conv_w": (dn_conv_w, m_dn_conv_w, v_dn_conv_w, gsh["dn_conv"]),
           "w_proj_dn": (w_proj_dn, m_w_proj_dn, v_w_proj_dn, gsh["wp_dn"]),
           "w_proj_sb": (w_proj_sb, m_w_proj_sb, v_w_proj_sb, gsh["wp_sb"]),
           "w_out": (w_out, m_w_out, v_w_out, gsh["w_out"]),
           "ffn_w_up": (ffn_w_up, m_ffn_w_up, v_ffn_w_up, gsh["w_up"]),
           "ffn_conv_w": (ffn_conv_w, m_ffn_conv_w, v_ffn_conv_w, gsh["ffn_conv"]),
           "ffn_w_down": (ffn_w_down, m_ffn_w_down, v_ffn_w_down, gsh["w_down"])}
    res = {}
    for nm, (w, m, v, g) in big.items():
        d, nm_, nv_ = _adamw(w[0], g, m[0], v[0], "adamw_" + nm)
        res[nm] = (g[None], d[None], nm_[None], nv_[None])

    names = ["norm1_w", "norm2_w", "norm_f_w", "dn_norm_w", "dn_A_log", "dn_dt_bias"]
    given = {"norm1_w": (norm1_w, m_norm1_w, v_norm1_w), "norm2_w": (norm2_w, m_norm2_w, v_norm2_w),
             "norm_f_w": (norm_f_w, m_norm_f_w, v_norm_f_w), "dn_norm_w": (dn_norm_w, m_dn_norm_w, v_dn_norm_w),
             "dn_A_log": (dn_A_log, m_dn_A_log, v_dn_A_log), "dn_dt_bias": (dn_dt_bias, m_dn_dt_bias, v_dn_dt_bias)}

    def stack(k, fill):
        rows = [jnp.pad(given[nm][k].reshape(1, -1), ((0, 0), (0, D_MODEL - given[nm][k].size)),
                        constant_values=fill) for nm in names]
        return jnp.concatenate(rows + [jnp.full((SMALL_ROWS - len(names), D_MODEL), fill, F32)], axis=0)

    g_rows = jnp.concatenate(
        [jnp.pad(g_small[nm].reshape(1, -1), ((0, 0), (0, D_MODEL - g_small[nm].size))) for nm in names]
        + [jnp.zeros((SMALL_ROWS - len(names), D_MODEL), F32)], axis=0)
    d_s, m_s, v_s = _adamw(stack(0, 0.0), g_rows, stack(1, 0.0), stack(2, 1.0), "adamw_small")
    for r, nm in enumerate(names):
        shape = given[nm][0].shape
        n = given[nm][0].size
        res[nm] = (g_small[nm].reshape(shape), d_s[r, :n].reshape(shape), m_s[r, :n].reshape(shape),
                   v_s[r, :n].reshape(shape))

    order = ["norm1_w", "w_in", "dn_conv_w", "dn_A_log", "dn_dt_bias", "dn_norm_w", "w_proj_dn", "w_proj_sb",
             "w_out", "norm2_w", "ffn_w_up", "ffn_conv_w", "ffn_w_down", "norm_f_w"]
    outs = [loss, grad_x[None]]
    for k in range(4):
        outs += [res[nm][k] for nm in order]
    return tuple(outs)
```

```python
import functools

import jax
import jax.numpy as jnp
from jax import lax
from jax.experimental import pallas as pl
from jax.experimental.pallas import tpu as pltpu

F32 = jnp.float32
BF16 = jnp.bfloat16
MESH = pl.DeviceIdType.MESH

EPS = 1e-6
D_MODEL = 1024
N_HEADS = 8
HEAD_DIM = 128
DN_CONV = 4
DN_CHUNK = 64
D_FF = 2816
FFN_CONV = 3
ADAM_LR, ADAM_B1, ADAM_B2, ADAM_EPS, ADAM_WD, ADAM_STEP = 0.001, 0.9, 0.999, 1e-08, 0.01, 10

N_CHIPS = 4
LANES = 128
HALO = 8
VMEM_LIMIT = 48 * 1024 * 1024
SMALL_ROWS = 8


def _params(sem=None):
    return pltpu.CompilerParams(dimension_semantics=sem, vmem_limit_bytes=VMEM_LIMIT)


def _pick(n, target):
    best = None
    for b in range(LANES, min(n, target) + 1, LANES):
        if n % b == 0:
            best = b
    return best or n


ELEMENTWISE_COLS = 1408


def _rows(t, target=256):
    return min(t, target)


def _dot(a, b, precision=None):
    return lax.dot_general(a, b, (((1,), (0,)), ((), ())), precision=precision, preferred_element_type=F32)


def _dot_nt(a, b, precision=None):
    return lax.dot_general(a, b, (((1,), (1,)), ((), ())), precision=precision, preferred_element_type=F32)


def _dot_tn(a, b, precision=None):
    return lax.dot_general(a, b, (((0,), (0,)), ((), ())), precision=precision, preferred_element_type=F32)


def _rms(x, w):
    return x * lax.rsqrt(jnp.mean(x * x, axis=-1, keepdims=True) + EPS) * w


def _silu(x):
    return x * jax.nn.sigmoid(x)


def _softplus(x):
    return jnp.maximum(x, 0.0) + jnp.log(1.0 + jnp.exp(-jnp.abs(x)))


MM_BLOCK = 1408
MM_VMEM_BUDGET = 38 * 1024 * 1024


def _mm(a, b, *, ta=False, tb=False, add=None, out_dtype=F32, name, bm=MM_BLOCK, bn=MM_BLOCK, bk=MM_BLOCK):
    m = a.shape[1] if ta else a.shape[0]
    k = a.shape[0] if ta else a.shape[1]
    n = b.shape[0] if tb else b.shape[1]
    bm, bn = _pick(m, bm), _pick(n, bn)

    def vmem_need(bk_):
        need = 2 * (bm * bk_ * a.dtype.itemsize + bk_ * bn * b.dtype.itemsize) + 2 * bm * bn * jnp.dtype(out_dtype).itemsize
        need += 2 * bm * bn * add.dtype.itemsize if add is not None else 0
        return need + (bm * bn * 4 if bk_ < k else 0)

    bk = max((d for d in range(LANES, k + 1, LANES) if k % d == 0 and vmem_need(d) <= MM_VMEM_BUDGET),
             default=_pick(k, bk))
    nk = k // bk
    dims = (((0 if ta else 1,), (1 if tb else 0,)), ((), ()))

    def body(*refs):
        a_ref, b_ref = refs[:2]
        c_ref = refs[2] if add is not None else None
        o_ref = refs[3] if add is not None else refs[2]
        acc = refs[-1]
        kk = pl.program_id(2)
        part = lax.dot_general(a_ref[...].astype(BF16), b_ref[...].astype(BF16), dims, preferred_element_type=F32)

        def finish(r):
            if add is not None:
                r = r + c_ref[...].astype(F32)
            o_ref[...] = r.astype(out_dtype)

        if nk == 1:
            finish(part)
            return

        @pl.when(kk == 0)
        def _():
            acc[...] = part

        @pl.when(jnp.logical_and(kk > 0, kk < nk - 1))
        def _():
            acc[...] += part

        @pl.when(kk == nk - 1)
        def _():
            finish(acc[...] + part)

    a_spec = (pl.BlockSpec((bk, bm), lambda i, j, kk: (kk, i)) if ta
              else pl.BlockSpec((bm, bk), lambda i, j, kk: (i, kk)))
    b_spec = (pl.BlockSpec((bn, bk), lambda i, j, kk: (j, kk)) if tb
              else pl.BlockSpec((bk, bn), lambda i, j, kk: (kk, j)))
    o_spec = pl.BlockSpec((bm, bn), lambda i, j, kk: (i, j))
    in_specs = [a_spec, b_spec] + ([o_spec] if add is not None else [])
    args = (a, b) + ((add,) if add is not None else ())
    return pl.pallas_call(
        body, name=name, grid=(m // bm, n // bn, nk),
        in_specs=in_specs, out_specs=o_spec,
        out_shape=jax.ShapeDtypeStruct((m, n), out_dtype),
        scratch_shapes=[pltpu.VMEM((bm, bn), F32)] if nk > 1 else [],
        compiler_params=_params(("parallel", "parallel", "arbitrary")),
    )(*args)


def _norm1_fwd(x, w, w_ab):
    t = x.shape[0]
    tb = _rows(t)

    def body(x_ref, w_ref, wab_ref, n_ref, hab_ref):
        n = _rms(x_ref[...], w_ref[...]).astype(BF16)
        n_ref[...] = n
        hab_ref[...] = _dot(n, wab_ref[...])

    return pl.pallas_call(
        body, name="norm1_fwd", grid=(t // tb,),
        in_specs=[pl.BlockSpec((tb, D_MODEL), lambda i: (i, 0)),
                  pl.BlockSpec((1, D_MODEL), lambda i: (0, 0)),
                  pl.BlockSpec((D_MODEL, LANES), lambda i: (0, 0))],
        out_specs=[pl.BlockSpec((tb, D_MODEL), lambda i: (i, 0)),
                   pl.BlockSpec((tb, LANES), lambda i: (i, 0))],
        out_shape=[jax.ShapeDtypeStruct((t, D_MODEL), BF16), jax.ShapeDtypeStruct((t, LANES), F32)],
        compiler_params=_params(("arbitrary",)),
    )(x, w, w_ab)


def _norm1_bwd(x, w, dn, dres, dab, w_ab):
    t = x.shape[0]
    tb = _rows(t)

    def body(x_ref, w_ref, dn_ref, dres_ref, dab_ref, wab_ref, dx_ref, dw_ref):
        i = pl.program_id(0)
        g = dn_ref[...] + _dot_nt(dab_ref[...].astype(BF16), wab_ref[...])
        _, vjp = jax.vjp(_rms, x_ref[...], w_ref[...])
        dx, dw = vjp(g)
        dx_ref[...] = dres_ref[...] + dx

        @pl.when(i == 0)
        def _():
            dw_ref[...] = jnp.zeros_like(dw_ref)

        dw_ref[...] += dw

    row = pl.BlockSpec((tb, D_MODEL), lambda i: (i, 0))
    vec = pl.BlockSpec((1, D_MODEL), lambda i: (0, 0))
    return pl.pallas_call(
        body, name="norm1_bwd", grid=(t // tb,),
        in_specs=[row, vec, row, row, pl.BlockSpec((tb, LANES), lambda i: (i, 0)),
                  pl.BlockSpec((D_MODEL, LANES), lambda i: (0, 0))],
        out_specs=[row, vec],
        out_shape=[jax.ShapeDtypeStruct((t, D_MODEL), F32), jax.ShapeDtypeStruct((1, D_MODEL), F32)],
        compiler_params=_params(("arbitrary",)),
    )(x, w, dn, dres, dab, w_ab)


def _conv_fwd(x, w, name):
    t, c = x.shape
    kk = w.shape[0]
    tb, cb = _rows(t, 512), _pick(c, ELEMENTWISE_COLS)
    per = tb // HALO

    def body(x_ref, halo_ref, w_ref, y_ref, buf):
        i = pl.program_id(0)
        buf[pl.ds(HALO, tb), :] = x_ref[...]
        buf[pl.ds(0, HALO), :] = jnp.where(i == 0, 0.0, halo_ref[...])
        y_ref[...] = _conv_taps(buf, w_ref, HALO - (kk - 1), tb)

    return pl.pallas_call(
        body, name=name, grid=(t // tb, c // cb),
        in_specs=[pl.BlockSpec((tb, cb), lambda i, j: (i, j)),
                  pl.BlockSpec((HALO, cb), lambda i, j: (jnp.maximum(i * per - 1, 0), j)),
                  pl.BlockSpec((kk, cb), lambda i, j: (0, j))],
        out_specs=pl.BlockSpec((tb, cb), lambda i, j: (i, j)),
        out_shape=jax.ShapeDtypeStruct((t, c), F32),
        scratch_shapes=[pltpu.VMEM((tb + HALO, cb), F32)],
        compiler_params=_params(("parallel", "parallel")),
    )(x, x, w)


def _conv_bwd(dy, x, w, name, dx_dtype):
    t, c = x.shape
    kk = w.shape[0]
    tb, cb = _rows(t, 512), _pick(c, ELEMENTWISE_COLS)
    per = tb // HALO
    nblk = t // tb

    def body(dy_ref, after_ref, x_ref, w_ref, dx_ref, dw_ref, dbuf):
        i = pl.program_id(1)
        dbuf[pl.ds(0, tb), :] = dy_ref[...]
        dbuf[pl.ds(tb, HALO), :] = jnp.where(i == nblk - 1, 0.0, after_ref[...])

        @pl.when(i == 0)
        def _():
            dw_ref[...] = jnp.zeros_like(dw_ref)

        for j in range(cb // LANES):
            sl = pl.ds(j * LANES, LANES)
            x = x_ref[:, sl]
            dx = None
            for s in range(kk):
                shifted = dbuf[pl.ds(kk - 1 - s, tb), sl]
                term = w_ref[s:s + 1, sl] * shifted
                dx = term if dx is None else dx + term
                dw_ref[s:s + 1, sl] += jnp.sum(shifted * x, axis=0, keepdims=True)
            dx_ref[:, sl] = dx.astype(dx_dtype)

    blk = pl.BlockSpec((tb, cb), lambda j, i: (i, j))
    return pl.pallas_call(
        body, name=name, grid=(c // cb, nblk),
        in_specs=[blk,
                  pl.BlockSpec((HALO, cb), lambda j, i: (jnp.minimum((i + 1) * per, t // HALO - 1), j)),
                  blk,
                  pl.BlockSpec((kk, cb), lambda j, i: (0, j))],
        out_specs=[blk, pl.BlockSpec((HALO, cb), lambda j, i: (0, j))],
        out_shape=[jax.ShapeDtypeStruct((t, c), dx_dtype), jax.ShapeDtypeStruct((HALO, c), F32)],
        scratch_shapes=[pltpu.VMEM((tb + HALO, cb), F32)],
        compiler_params=_params(("parallel", "arbitrary")),
    )(dy, dy, x, w)


def _dn_head(c, normed):
    s = _silu(c)
    return s * lax.rsqrt(jnp.sum(s * s, axis=-1, keepdims=True) + EPS) if normed else s


def _dn_gates(hab, alog, dtb):
    lane = lax.broadcasted_iota(jnp.int32, hab.shape, 1)
    g = -jnp.exp(alog) * _softplus(hab + dtb)
    beta = jax.nn.sigmoid(hab)
    return jnp.where(lane < N_HEADS, g, jnp.where(lane < 2 * N_HEADS, beta, 0.0))


def _dn_head_slices(q_ref, k_ref, v_ref):
    return [(pl.ds((part * N_HEADS + h) * HEAD_DIM, HEAD_DIM), ref, h, part < 2)
            for part, ref in enumerate((q_ref, k_ref, v_ref)) for h in range(N_HEADS)]


def _dn_prep_fwd(c, hab, alog, dtb):
    t = c.shape[0]
    tb = _rows(t)

    def body(c_ref, hab_ref, alog_ref, dtb_ref, q_ref, k_ref, v_ref, gb_ref):
        for sl, ref, h, normed in _dn_head_slices(q_ref, k_ref, v_ref):
            ref[h] = _dn_head(c_ref[:, sl], normed)
        gb_ref[...] = _dn_gates(hab_ref[...], alog_ref[...], dtb_ref[...])

    hm = pl.BlockSpec((N_HEADS, tb, HEAD_DIM), lambda i: (0, i, 0))
    nar = pl.BlockSpec((tb, LANES), lambda i: (i, 0))
    vec = pl.BlockSpec((1, LANES), lambda i: (0, 0))
    return pl.pallas_call(
        body, name="dn_prep_fwd", grid=(t // tb,),
        in_specs=[pl.BlockSpec((tb, 3 * D_MODEL), lambda i: (i, 0)), nar, vec, vec],
        out_specs=[hm, hm, hm, nar],
        out_shape=[jax.ShapeDtypeStruct((N_HEADS, t, HEAD_DIM), F32)] * 3 + [jax.ShapeDtypeStruct((t, LANES), F32)],
        compiler_params=_params(("parallel",)),
    )(c, hab, alog, dtb)


def _dn_prep_bwd(c, hab, alog, dtb, dq, dk, dv, dgb):
    t = c.shape[0]
    tb = _rows(t)

    def body(c_ref, hab_ref, alog_ref, dtb_ref, dq_ref, dk_ref, dv_ref, dgb_ref,
             dc_ref, dhab_ref, dalog_ref, ddtb_ref):
        i = pl.program_id(0)
        for sl, ref, h, normed in _dn_head_slices(dq_ref, dk_ref, dv_ref):
            _, vjp = jax.vjp(functools.partial(_dn_head, normed=normed), c_ref[:, sl])
            dc_ref[:, sl] = vjp(ref[h])[0]
        _, vjp = jax.vjp(_dn_gates, hab_ref[...], alog_ref[...], dtb_ref[...])
        dhab, dalog, ddtb = vjp(dgb_ref[...])
        dhab_ref[...] = dhab

        @pl.when(i == 0)
        def _():
            dalog_ref[...] = jnp.zeros_like(dalog_ref)
            ddtb_ref[...] = jnp.zeros_like(ddtb_ref)

        dalog_ref[...] += dalog
        ddtb_ref[...] += ddtb

    hm = pl.BlockSpec((N_HEADS, tb, HEAD_DIM), lambda i: (0, i, 0))
    wide = pl.BlockSpec((tb, 3 * D_MODEL), lambda i: (i, 0))
    nar = pl.BlockSpec((tb, LANES), lambda i: (i, 0))
    vec = pl.BlockSpec((1, LANES), lambda i: (0, 0))
    return pl.pallas_call(
        body, name="dn_prep_bwd", grid=(t // tb,),
        in_specs=[wide, nar, vec, vec, hm, hm, hm, nar],
        out_specs=[wide, nar, vec, vec],
        out_shape=[jax.ShapeDtypeStruct((t, 3 * D_MODEL), F32), jax.ShapeDtypeStruct((t, LANES), F32),
                   jax.ShapeDtypeStruct((1, LANES), F32), jax.ShapeDtypeStruct((1, LANES), F32)],
        compiler_params=_params(("arbitrary",)),
    )(c, hab, alog, dtb, dq, dk, dv, dgb)


DN_PREC = lax.Precision.HIGH
DN_GROUP = 32


def _dn_prec(a):
    return DN_PREC if a.dtype == F32 else None


def _bdot(a, b):
    return lax.dot_general(a, b, (((2,), (1,)), ((0,), (0,))), precision=_dn_prec(a), preferred_element_type=F32)


def _bdot_nt(a, b):
    return lax.dot_general(a, b, (((2,), (2,)), ((0,), (0,))), precision=_dn_prec(a), preferred_element_type=F32)


def _bdot_tn(a, b):
    return lax.dot_general(a, b, (((1,), (1,)), ((0,), (0,))), precision=_dn_prec(a), preferred_element_type=F32)


def _unit_lower_inverse(lmat):
    c = lmat.shape[-1]
    ri = lax.broadcasted_iota(jnp.int32, (c, c), 0)
    ci = lax.broadcasted_iota(jnp.int32, (c, c), 1)
    p = -lmat
    tinv = jnp.where(ri == ci, 1.0, 0.0) + p
    for _ in range(max(c.bit_length() - 2, 0)):
        p = _bdot(p, p)
        tinv = tinv + _bdot(tinv, p)
    return tinv


@jax.custom_vjp
def _solve_with(lmat, rhs, tinv):
    return _bdot(tinv, rhs)


def _solve_with_fwd(lmat, rhs, tinv):
    sol = _bdot(tinv, rhs)
    return sol, (sol, tinv)


def _solve_with_bwd(res, dsol):
    sol, tinv = res
    drhs = _bdot_tn(tinv, dsol)
    return -_bdot_nt(drhs, sol), drhs, jnp.zeros_like(tinv)


_solve_with.defvjp(_solve_with_fwd, _solve_with_bwd)


def _dn_local(q, k, v, grow, brow, tinv):
    g, c, _ = q.shape
    ri = lax.broadcasted_iota(jnp.int32, (c, c), 0)
    ci = lax.broadcasted_iota(jnp.int32, (c, c), 1)
    lower = ri >= ci
    as_col = lambda r: jnp.sum(jnp.where(ri == ci, jnp.broadcast_to(r, (g, c, c)), 0.0), axis=2, keepdims=True)
    gcol, bcol = as_col(grow), as_col(brow)
    gc_col = jnp.sum(jnp.where(lower, jnp.broadcast_to(grow, (g, c, c)), 0.0), axis=2, keepdims=True)
    gc_row = jnp.sum(jnp.where(ri <= ci, jnp.broadcast_to(gcol, (g, c, c)), 0.0), axis=1, keepdims=True)
    qs = q * (HEAD_DIM ** -0.5)
    kb = k * bcol
    vb = v * bcol
    decay = jnp.where(lower, jnp.exp(jnp.where(lower, gc_col - gc_row, 0.0)), 0.0)
    lmat = jnp.where(ri > ci, _bdot_nt(kb.astype(BF16), k.astype(BF16)) * decay, 0.0)
    eg = jnp.exp(gc_col)
    rhs = jnp.concatenate([vb, kb * eg], axis=2)
    if tinv is None:
        tinv = _unit_lower_inverse(lmat)
    sol = _solve_with(lmat, rhs, tinv)
    a_qk = jnp.where(lower, _bdot_nt(qs.astype(BF16), k.astype(BF16)) * decay, 0.0)
    g_last = jnp.sum(grow, axis=2, keepdims=True)
    kdec = k * jnp.exp(g_last - gc_col)
    egl = jnp.broadcast_to(jnp.exp(g_last), (g, 1, HEAD_DIM))
    b16 = lambda x: x.astype(BF16)
    return sol[:, :, :HEAD_DIM], b16(sol[:, :, HEAD_DIM:]), b16(a_qk), b16(qs * eg), b16(kdec), egl, tinv


def _dn_seq(u, w, a_qk, qe, kdec, egl, s_in):
    b16 = lambda x: x.astype(BF16)
    v_new = u - _bdot(b16(w), b16(s_in))
    o = _bdot(b16(qe), b16(s_in)) + _bdot(b16(a_qk), b16(v_new))
    return o, s_in * egl + _bdot_tn(b16(kdec), b16(v_new))


def _dn_local_specs(t):
    grp = min(DN_GROUP, t // DN_CHUNK)
    rows = grp * DN_CHUNK
    blk = pl.BlockSpec((1, rows, HEAD_DIM), lambda h, i: (h, i, 0))
    row = pl.BlockSpec((1, grp, 1, DN_CHUNK), lambda h, i: (h, i, 0, 0))
    sq = pl.BlockSpec((1, grp, DN_CHUNK, DN_CHUNK), lambda h, i: (h, i, 0, 0))
    lane = pl.BlockSpec((1, grp, 1, HEAD_DIM), lambda h, i: (h, i, 0, 0))
    return grp, blk, row, sq, lane


def half(shape):
    return jax.ShapeDtypeStruct(shape.shape, BF16)


def _dn_shapes(t):
    nchunk = t // DN_CHUNK
    big = jax.ShapeDtypeStruct((N_HEADS, t, HEAD_DIM), F32)
    row = jax.ShapeDtypeStruct((N_HEADS, nchunk, 1, DN_CHUNK), F32)
    sq = jax.ShapeDtypeStruct((N_HEADS, nchunk, DN_CHUNK, DN_CHUNK), F32)
    lane = jax.ShapeDtypeStruct((N_HEADS, nchunk, 1, HEAD_DIM), F32)
    return big, row, sq, lane


def _dn_local_fwd(q, k, v, grow, brow, wire=()):
    t = q.shape[1]
    grp, blk, row, sq, lane = _dn_local_specs(t)
    big, _, sqs, lanes = _dn_shapes(t)
    n = len(wire)
    groups = t // (grp * DN_CHUNK)
    steps = N_HEADS * groups

    def body(q_ref, k_ref, v_ref, gr_ref, br_ref, *rest):
        u_ref, w_ref, a_ref, qe_ref, kd_ref, egl_ref, t_ref = rest[n:n + 7]
        if n:
            begin, middle, end = _gather_protocol(rest[:n], rest[n + 7:2 * n + 7], *rest[2 * n + 7:])
            step = pl.program_id(0) * groups + pl.program_id(1)
            pl.when(step == 0)(begin)
            pl.when(step == (GATHER_FORWARD_AT * steps) // 8)(middle)
        split = lambda r: r[0].reshape(grp, DN_CHUNK, HEAD_DIM)
        u, w, a_qk, qe, kdec, egl, tinv = _dn_local(split(q_ref), split(k_ref), split(v_ref), gr_ref[0],
                                                     br_ref[0], None)
        for ref, val in ((u_ref, u), (w_ref, w), (qe_ref, qe), (kd_ref, kdec)):
            ref[0] = val.reshape(grp * DN_CHUNK, HEAD_DIM)
        a_ref[0] = a_qk
        egl_ref[0] = egl
        t_ref[0] = tinv
        if n:
            pl.when(step == steps - 1)(end)

    assert n == 0 or steps >= 3
    return pl.pallas_call(
        body, name="dn_local_fwd", grid=(N_HEADS, groups),
        in_specs=[blk, blk, blk, row, row] + _hbm_specs(n),
        out_specs=[blk, blk, sq, blk, blk, lane, sq] + _hbm_specs(n),
        out_shape=[big, half(big), half(sqs), half(big), half(big), lanes, sqs] + _gather_out_shapes(wire),
        scratch_shapes=_gather_sems(n) if n else [],
        compiler_params=_params(("arbitrary", "arbitrary")),
    )(q, k, v, grow, brow, *wire)


def _dn_local_bwd(q, k, v, grow, brow, tinv, du, dw, da, dqe, dkd, degl):
    t = q.shape[1]
    grp, blk, row, sq, lane = _dn_local_specs(t)
    big, rows_, _, _ = _dn_shapes(t)

    def body(q_ref, k_ref, v_ref, gr_ref, br_ref, t_ref, du_ref, dw_ref, da_ref, dqe_ref, dkd_ref,
             degl_ref, dq_ref, dk_ref, dv_ref, dgr_ref, dbr_ref):
        split = lambda r: r[0].reshape(grp, DN_CHUNK, HEAD_DIM)
        tinv_v = t_ref[0]
        fn = lambda q_, k_, v_, gr_, br_: _dn_local(q_, k_, v_, gr_, br_, tinv_v)[:6]
        _, vjp = jax.vjp(fn, split(q_ref), split(k_ref), split(v_ref), gr_ref[0], br_ref[0])
        dq, dk, dv, dgr, dbr = vjp((split(du_ref), split(dw_ref), da_ref[0], split(dqe_ref), split(dkd_ref),
                                    degl_ref[0]))
        for ref, val in ((dq_ref, dq), (dk_ref, dk), (dv_ref, dv)):
            ref[0] = val.reshape(grp * DN_CHUNK, HEAD_DIM)
        dgr_ref[0] = dgr
        dbr_ref[0] = dbr

    return pl.pallas_call(
        body, name="dn_local_bwd", grid=(N_HEADS, t // (grp * DN_CHUNK)),
        in_specs=[blk, blk, blk, row, row, sq, blk, blk, sq, blk, blk, lane],
        out_specs=[blk, blk, blk, row, row],
        out_shape=[big, big, big, rows_, rows_],
        compiler_params=_params(("parallel", "parallel")),
    )(q, k, v, grow, brow, tinv, du, dw, da, dqe, dkd, degl)


DN_SEQ_CHUNKS = 8


def _dn_seq_specs(nchunk, rev):
    per = min(DN_SEQ_CHUNKS, nchunk)
    nstep = nchunk // per

    def idx(n):
        return nstep - 1 - n if rev else n

    blk = pl.BlockSpec((N_HEADS, per * DN_CHUNK, HEAD_DIM), lambda n: (0, idx(n), 0))
    sq = pl.BlockSpec((N_HEADS, per, DN_CHUNK, DN_CHUNK), lambda n: (0, idx(n), 0, 0))
    lane = pl.BlockSpec((N_HEADS, per, 1, HEAD_DIM), lambda n: (0, idx(n), 0, 0))
    st = pl.BlockSpec((N_HEADS, per, HEAD_DIM, HEAD_DIM), lambda n: (0, idx(n), 0, 0))
    return per, nstep, blk, sq, lane, st


def _dn_seq_fwd(u, w, a_qk, qe, kdec, egl):
    t = u.shape[1]
    nchunk = t // DN_CHUNK
    per, nstep, blk, sq, lane, st = _dn_seq_specs(nchunk, False)

    def body(u_ref, w_ref, a_ref, qe_ref, kd_ref, egl_ref, o_ref, s_ref, state):
        @pl.when(pl.program_id(0) == 0)
        def _():
            state[...] = jnp.zeros_like(state)

        for c in range(per):
            rows = pl.ds(c * DN_CHUNK, DN_CHUNK)
            s_in = state[...]
            s_ref[:, c] = s_in.astype(BF16)
            o_ref[:, rows], state[...] = _dn_seq(u_ref[:, rows], w_ref[:, rows], a_ref[:, c], qe_ref[:, rows],
                                                 kd_ref[:, rows], egl_ref[:, c], s_in)

    return pl.pallas_call(
        body, name="dn_seq_fwd", grid=(nstep,),
        in_specs=[blk, blk, sq, blk, blk, lane],
        out_specs=[blk, st],
        out_shape=[jax.ShapeDtypeStruct((N_HEADS, t, HEAD_DIM), F32),
                   jax.ShapeDtypeStruct((N_HEADS, nchunk, HEAD_DIM, HEAD_DIM), BF16)],
        scratch_shapes=[pltpu.VMEM((N_HEADS, HEAD_DIM, HEAD_DIM), F32)],
        compiler_params=_params(("arbitrary",)),
    )(u, w, a_qk, qe, kdec, egl)


def _dn_seq_bwd(u, w, a_qk, qe, kdec, egl, states, do):
    t = u.shape[1]
    nchunk = t // DN_CHUNK
    per, nstep, blk, sq, lane, st = _dn_seq_specs(nchunk, True)
    big, _, sqs, lanes = _dn_shapes(t)

    def body(u_ref, w_ref, a_ref, qe_ref, kd_ref, egl_ref, s_ref, do_ref,
             du_ref, dw_ref, da_ref, dqe_ref, dkd_ref, degl_ref, dstate):
        @pl.when(pl.program_id(0) == 0)
        def _():
            dstate[...] = jnp.zeros_like(dstate)

        for c in reversed(range(per)):
            rows = pl.ds(c * DN_CHUNK, DN_CHUNK)
            _, vjp = jax.vjp(_dn_seq, u_ref[:, rows], w_ref[:, rows], a_ref[:, c], qe_ref[:, rows], kd_ref[:, rows],
                             egl_ref[:, c], s_ref[:, c].astype(F32))
            (du_ref[:, rows], dw_ref[:, rows], da_ref[:, c], dqe_ref[:, rows], dkd_ref[:, rows], degl_ref[:, c],
             dstate[...]) = vjp((do_ref[:, rows], dstate[...]))

    return pl.pallas_call(
        body, name="dn_seq_bwd", grid=(nstep,),
        in_specs=[blk, blk, sq, blk, blk, lane, st, blk],
        out_specs=[blk, blk, sq, blk, blk, lane],
        out_shape=[big, half(big), half(sqs), half(big), half(big), lanes],
        scratch_shapes=[pltpu.VMEM((N_HEADS, HEAD_DIM, HEAD_DIM), F32)],
        compiler_params=_params(("arbitrary",)),
    )(u, w, a_qk, qe, kdec, egl, states, do)


def _dn_post_head(o, gate, w):
    return _rms(o, w) * _silu(gate)


def _dn_post_fwd(o, gate, w):
    t = gate.shape[0]
    tb = _rows(t)

    def body(o_ref, g_ref, w_ref, y_ref):
        for h in range(N_HEADS):
            sl = pl.ds(h * HEAD_DIM, HEAD_DIM)
            y_ref[:, sl] = _dn_post_head(o_ref[h], g_ref[:, sl], w_ref[...]).astype(BF16)

    row = pl.BlockSpec((tb, D_MODEL), lambda i: (i, 0))
    hm = pl.BlockSpec((N_HEADS, tb, HEAD_DIM), lambda i: (0, i, 0))
    return pl.pallas_call(
        body, name="dn_post_fwd", grid=(t // tb,),
        in_specs=[hm, row, pl.BlockSpec((1, HEAD_DIM), lambda i: (0, 0))],
        out_specs=row, out_shape=jax.ShapeDtypeStruct((t, D_MODEL), BF16),
        compiler_params=_params(("parallel",)),
    )(o, gate, w)


def _dn_post_bwd(o, gate, w, dy):
    t = gate.shape[0]
    tb = _rows(t)

    def body(o_ref, g_ref, w_ref, dy_ref, do_ref, dg_ref, dw_ref):
        i = pl.program_id(0)
        @pl.when(i == 0)
        def _():
            dw_ref[...] = jnp.zeros_like(dw_ref)

        for h in range(N_HEADS):
            sl = pl.ds(h * HEAD_DIM, HEAD_DIM)
            _, vjp = jax.vjp(_dn_post_head, o_ref[h], g_ref[:, sl], w_ref[...])
            do_ref[h], dg, dw = vjp(dy_ref[:, sl])
            dg_ref[:, sl] = dg.astype(BF16)
            dw_ref[...] += dw

    row = pl.BlockSpec((tb, D_MODEL), lambda i: (i, 0))
    hm = pl.BlockSpec((N_HEADS, tb, HEAD_DIM), lambda i: (0, i, 0))
    vec = pl.BlockSpec((1, HEAD_DIM), lambda i: (0, 0))
    return pl.pallas_call(
        body, name="dn_post_bwd", grid=(t // tb,),
        in_specs=[hm, row, vec, row],
        out_specs=[hm, row, vec],
        out_shape=[jax.ShapeDtypeStruct((N_HEADS, t, HEAD_DIM), F32), jax.ShapeDtypeStruct((t, D_MODEL), BF16),
                   jax.ShapeDtypeStruct((1, HEAD_DIM), F32)],
        compiler_params=_params(("arbitrary",)),
    )(o, gate, w, dy)


def _split_bf16(x):
    hi = x.astype(BF16)
    lo = (x - hi.astype(F32)).astype(BF16)
    return hi, lo


SB_Q_BLOCK = 512
SB_K_BLOCK = 256
SB_NEGLIGIBLE = -60.0


def _sb_logits(q, kb, mask, scale):
    z = _dot_nt(q, kb) * scale
    ls = jnp.minimum(z, 0.0) - jnp.log(1.0 + jnp.exp(-jnp.abs(z)))
    lk = ls - z
    if mask is not None:
        lk = jnp.where(mask, lk, 0.0)
    return ls, lk


def _sb_blocks(t):
    bq = min(SB_Q_BLOCK, t)
    bk = min(SB_K_BLOCK, bq)
    return bq, bk, bq // bk


def _sb_fwd(qkv):
    t = qkv.shape[0]
    bq, bk, nd = _sb_blocks(t)
    scale = HEAD_DIM ** -0.5

    def body(q_ref, k_ref, v_ref, o_ref, tot_ref, used_ref):
        i = pl.program_id(1)
        q = q_ref[...]
        rj = lax.broadcasted_iota(jnp.int32, (bk, bk), 0)
        cj = lax.broadcasted_iota(jnp.int32, (bk, bk), 1)
        after = (rj > cj).astype(BF16)
        trow = lax.broadcasted_iota(jnp.int32, (bq, bk), 0)
        scol = lax.broadcasted_iota(jnp.int32, (bq, bk), 1)

        def tile(j, run, acc, mask):
            off = pl.multiple_of(j * bk, bk)
            kb = k_ref[pl.ds(off, bk), :]
            vb = v_ref[pl.ds(off, bk), :]
            ls, lk = _sb_logits(q, kb, mask, scale)
            hi, lo = _split_bf16(lk)
            between = _dot(hi, after) + _dot(lo, after) + run
            a = jnp.exp(ls + between)
            if mask is not None:
                a = jnp.where(mask, a, 0.0)
            acc = acc + _dot(a.astype(BF16), vb)
            return run + jnp.sum(lk, axis=1, keepdims=True), acc

        run, acc = jnp.zeros((bq, 1), F32), jnp.zeros((bq, HEAD_DIM), F32)
        for d in reversed(range(nd)):
            run, acc = tile(i * nd + d, run, acc, scol + d * bk < trow)
        def more(c):
            return jnp.logical_and(c[0] < i * nd, jnp.max(c[1]) > SB_NEGLIGIBLE)

        def far(c):
            run_, acc_ = tile(i * nd - 1 - c[0], c[1], c[2], None)
            return c[0] + 1, run_, acc_

        used, run, acc = lax.while_loop(more, far, (jnp.int32(0), run, acc))
        o_ref[...] = acc.astype(BF16)
        tot_ref[...] = jnp.broadcast_to(run, (bq, HEAD_DIM))
        used_ref[...] = jnp.full(used_ref.shape, used, F32)

    qs = pl.BlockSpec((bq, HEAD_DIM), lambda h, i: (i, h))
    ks = pl.BlockSpec((t, HEAD_DIM), lambda h, i: (0, N_HEADS + h))
    vs = pl.BlockSpec((t, HEAD_DIM), lambda h, i: (0, 2 * N_HEADS + h))
    return pl.pallas_call(
        body, name="sb_fwd", grid=(N_HEADS, t // bq),
        in_specs=[qs, ks, vs], out_specs=[qs, qs, pl.BlockSpec((1, 1, 1, LANES), lambda h, i: (h, i, 0, 0))],
        out_shape=[jax.ShapeDtypeStruct((t, D_MODEL), BF16), jax.ShapeDtypeStruct((t, D_MODEL), F32),
                   jax.ShapeDtypeStruct((N_HEADS, t // bq, 1, LANES), F32)],
        compiler_params=_params(("parallel", "arbitrary")),
    )(qkv, qkv, qkv)


def _sb_bwd(qkv, tot, used, do, partials=()):
    t = qkv.shape[0]
    bq, bk, nd = _sb_blocks(t)
    scale = HEAD_DIM ** -0.5
    n = len(partials)
    nq = t // bq

    def body(q_ref, k_ref, v_ref, tot_ref, used_ref, do_ref, *rest):
        dq_ref, dk_ref, dv_ref = rest[n:n + 3]
        i = pl.program_id(1)
        if n:
            begin, end = _chip_exchange_protocol(rest[:n], rest[n + 3:2 * n + 3], *rest[2 * n + 3:])
            step = pl.program_id(0) * nq + i
            pl.when(step == 0)(begin)

        @pl.when(i == 0)
        def _():
            dk_ref[...] = jnp.zeros_like(dk_ref)
            dv_ref[...] = jnp.zeros_like(dv_ref)

        q = q_ref[...]
        do = do_ref[...]
        total = tot_ref[:, 0:1]
        rj = lax.broadcasted_iota(jnp.int32, (bk, bk), 0)
        cj = lax.broadcasted_iota(jnp.int32, (bk, bk), 1)
        upto = (rj <= cj).astype(BF16)
        before = (rj < cj).astype(BF16)
        trow = lax.broadcasted_iota(jnp.int32, (bq, bk), 0)
        scol = lax.broadcasted_iota(jnp.int32, (bq, bk), 1)

        def tile(j, run_k, run_e, dq, mask):
            off = pl.multiple_of(j * bk, bk)
            kb = k_ref[pl.ds(off, bk), :]
            vb = v_ref[pl.ds(off, bk), :]
            ls, lk = _sb_logits(q, kb, mask, scale)
            hi, lo = _split_bf16(lk)
            between = total - (_dot(hi, upto) + _dot(lo, upto) + run_k)
            a = jnp.exp(ls + between)
            if mask is not None:
                a = jnp.where(mask, a, 0.0)
            e = a * _dot_nt(do, vb)
            ehi, elo = _split_bf16(e)
            pre = _dot(ehi, before) + _dot(elo, before) + run_e
            sig = jnp.exp(ls)
            dz = e * (1.0 - sig) - pre * sig
            if mask is not None:
                dz = jnp.where(mask, dz, 0.0)
            dz = (dz * scale).astype(BF16)
            dq = dq + _dot(dz, kb)
            dk_ref[pl.ds(off, bk), :] += _dot_tn(dz, q)
            dv_ref[pl.ds(off, bk), :] += _dot_tn(a.astype(BF16), do)
            return (run_k + jnp.sum(lk, axis=1, keepdims=True),
                    run_e + jnp.sum(e, axis=1, keepdims=True), dq)

        zero = jnp.zeros((bq, 1), F32)
        visited = jnp.clip(jnp.max(used_ref[...]).astype(jnp.int32), 0, i * nd)
        carry = lax.fori_loop(i * nd - visited, i * nd, lambda j, c: tile(j, c[0], c[1], c[2], None),
                              (zero, zero, jnp.zeros((bq, HEAD_DIM), F32)))
        for d in range(nd):
            carry = tile(i * nd + d, *carry, scol + d * bk < trow)
        dq_ref[...] = carry[2]
        if n:
            pl.when(step == N_HEADS * nq - 1)(end)

    qs = pl.BlockSpec((bq, HEAD_DIM), lambda h, i: (i, h))
    ks = pl.BlockSpec((t, HEAD_DIM), lambda h, i: (0, N_HEADS + h))
    vs = pl.BlockSpec((t, HEAD_DIM), lambda h, i: (0, 2 * N_HEADS + h))
    full = pl.BlockSpec((t, HEAD_DIM), lambda h, i: (0, h))
    big = jax.ShapeDtypeStruct((t, D_MODEL), F32)
    return pl.pallas_call(
        body, name="sb_bwd", grid=(N_HEADS, nq),
        in_specs=[qs, ks, vs, qs, pl.BlockSpec((1, 1, 1, LANES), lambda h, i: (h, i, 0, 0)), qs] + _hbm_specs(n),
        out_specs=[qs, full, full] + _hbm_specs(n),
        out_shape=[big, big, big] + _chip_exchange_shapes(partials),
        scratch_shapes=_chip_exchange_sems(n) if n else [],
        compiler_params=_params(("arbitrary", "arbitrary")),
    )(qkv, qkv, qkv, tot, used, do, *partials)


def _merge_fwd(o_dn, o_sb, gl, x, wp_dn, wp_sb, w_out, w2):
    t = x.shape[0]
    tb = _rows(t)

    def body(odn_ref, osb_ref, gl_ref, x_ref, wpd_ref, wps_ref, wo_ref, w2_ref,
             pdn_ref, psb_ref, mix_ref, x1_ref, n2_ref):
        pdn = _dot(odn_ref[...], wpd_ref[...])
        psb = _dot(osb_ref[...], wps_ref[...])
        gates = jax.nn.sigmoid(gl_ref[...])
        mixed = (gates[:, :D_MODEL] * pdn + gates[:, D_MODEL:] * psb).astype(BF16)
        x1 = x_ref[...] + _dot(mixed, wo_ref[...])
        pdn_ref[...] = pdn.astype(BF16)
        psb_ref[...] = psb.astype(BF16)
        mix_ref[...] = mixed
        x1_ref[...] = x1
        n2_ref[...] = _rms(x1, w2_ref[...]).astype(BF16)

    row = pl.BlockSpec((tb, D_MODEL), lambda i: (i, 0))
    sq = pl.BlockSpec((D_MODEL, D_MODEL), lambda i: (0, 0))
    f = jax.ShapeDtypeStruct((t, D_MODEL), F32)
    b = jax.ShapeDtypeStruct((t, D_MODEL), BF16)
    return pl.pallas_call(
        body, name="merge_fwd", grid=(t // tb,),
        in_specs=[row, row, pl.BlockSpec((tb, 2 * D_MODEL), lambda i: (i, 0)), row, sq, sq, sq,
                  pl.BlockSpec((1, D_MODEL), lambda i: (0, 0))],
        out_specs=[row] * 5, out_shape=[b, b, b, f, b],
        compiler_params=_params(("parallel",)),
    )(o_dn, o_sb, gl, x, wp_dn, wp_sb, w_out, w2)


def _merge_bwd(dx2, dn2, x1, w2, gl, pdn, psb, wp_dn, wp_sb, w_out):
    t = x1.shape[0]
    tb = _rows(t)

    def body(dx2_ref, dn2_ref, x1_ref, w2_ref, gl_ref, pdn_ref, psb_ref, wpd_ref, wps_ref, wo_ref,
             dx1_ref, dw2_ref, dgl_ref, dpdn_ref, dpsb_ref, dodn_ref, dosb_ref):
        i = pl.program_id(0)
        _, vjp = jax.vjp(_rms, x1_ref[...], w2_ref[...])
        dxn, dw2 = vjp(dn2_ref[...])
        dx1 = dx2_ref[...] + dxn
        dx1_ref[...] = dx1

        @pl.when(i == 0)
        def _():
            dw2_ref[...] = jnp.zeros_like(dw2_ref)

        dw2_ref[...] += dw2
        dmix = _dot_nt(dx1.astype(BF16), wo_ref[...])
        gates = jax.nn.sigmoid(gl_ref[...])
        g_dn, g_sb = gates[:, :D_MODEL], gates[:, D_MODEL:]
        dpdn = (dmix * g_dn).astype(BF16)
        dpsb = (dmix * g_sb).astype(BF16)
        dgl_ref[:, :D_MODEL] = (dmix * pdn_ref[...].astype(F32) * g_dn * (1.0 - g_dn)).astype(BF16)
        dgl_ref[:, D_MODEL:] = (dmix * psb_ref[...].astype(F32) * g_sb * (1.0 - g_sb)).astype(BF16)
        dpdn_ref[...] = dpdn
        dpsb_ref[...] = dpsb
        dodn_ref[...] = _dot_nt(dpdn, wpd_ref[...])
        dosb_ref[...] = _dot_nt(dpsb, wps_ref[...]).astype(BF16)

    row = pl.BlockSpec((tb, D_MODEL), lambda i: (i, 0))
    wide = pl.BlockSpec((tb, 2 * D_MODEL), lambda i: (i, 0))
    sq = pl.BlockSpec((D_MODEL, D_MODEL), lambda i: (0, 0))
    vec = pl.BlockSpec((1, D_MODEL), lambda i: (0, 0))
    f = jax.ShapeDtypeStruct((t, D_MODEL), F32)
    b = jax.ShapeDtypeStruct((t, D_MODEL), BF16)
    return pl.pallas_call(
        body, name="merge_bwd", grid=(t // tb,),
        in_specs=[row, row, row, vec, wide, row, row, sq, sq, sq],
        out_specs=[row, vec, wide, row, row, row, row],
        out_shape=[f, jax.ShapeDtypeStruct((1, D_MODEL), F32), jax.ShapeDtypeStruct((t, 2 * D_MODEL), BF16),
                   b, b, f, b],
        compiler_params=_params(("arbitrary",)),
    )(dx2, dn2, x1, w2, gl, pdn, psb, wp_dn, wp_sb, w_out)


def _conv_taps(buf, w_ref, first, rows, cols=slice(None)):
    y = w_ref[0:1, cols] * buf[pl.ds(first, rows), cols]
    for s in range(1, w_ref.shape[0]):
        y = y + w_ref[s:s + 1, cols] * buf[pl.ds(first + s, rows), cols]
    return y


def _ffn_mid_fwd(pre_g, pre_u, wg, wu):
    t, c = pre_g.shape
    kk = wg.shape[0]
    tb, cb = _rows(t), _pick(c, ELEMENTWISE_COLS)
    per = tb // HALO

    def body(g_ref, gh_ref, u_ref, uh_ref, wg_ref, wu_ref, a_ref, gbuf, ubuf):
        i = pl.program_id(0)
        for buf, ref, halo in ((gbuf, g_ref, gh_ref), (ubuf, u_ref, uh_ref)):
            buf[pl.ds(HALO, tb), :] = ref[...]
            buf[pl.ds(0, HALO), :] = jnp.where(i == 0, 0.0, halo[...])
        for j in range(cb // LANES):
            sl = pl.ds(j * LANES, LANES)
            ug = _conv_taps(gbuf, wg_ref, HALO - (kk - 1), tb, sl)
            uu = _conv_taps(ubuf, wu_ref, HALO - (kk - 1), tb, sl)
            a_ref[:, sl] = (_silu(ug) * uu).astype(BF16)

    blk = pl.BlockSpec((tb, cb), lambda i, j: (i, j))
    halo = pl.BlockSpec((HALO, cb), lambda i, j: (jnp.maximum(i * per - 1, 0), j))
    wspec = pl.BlockSpec((kk, cb), lambda i, j: (0, j))
    return pl.pallas_call(
        body, name="ffn_mid_fwd", grid=(t // tb, c // cb),
        in_specs=[blk, halo, blk, halo, wspec, wspec], out_specs=blk,
        out_shape=jax.ShapeDtypeStruct((t, c), BF16),
        scratch_shapes=[pltpu.VMEM((tb + HALO, cb), F32)] * 2,
        compiler_params=_params(("parallel", "parallel")),
    )(pre_g, pre_g, pre_u, pre_u, wg, wu)


def _ffn_mid_bwd(pre_g, pre_u, wg, wu, da):
    t, c = pre_g.shape
    kk = wg.shape[0]
    tb, cb = _rows(t), _pick(c, ELEMENTWISE_COLS)
    per = tb // HALO
    nblk = t // tb
    ext = tb + HALO

    def body(g_ref, gb_ref, ga_ref, u_ref, ub_ref, ua_ref, da_ref, daa_ref, wg_ref, wu_ref,
             dg_ref, du_ref, dwg_ref, dwu_ref, gbuf, ubuf, dabuf, dgbuf, dubuf):
        i = pl.program_id(1)
        last = i == nblk - 1
        for buf, ref, before, after in ((gbuf, g_ref, gb_ref, ga_ref), (ubuf, u_ref, ub_ref, ua_ref)):
            buf[pl.ds(0, HALO), :] = jnp.where(i == 0, 0.0, before[...])
            buf[pl.ds(HALO, tb), :] = ref[...]
            buf[pl.ds(HALO + tb, HALO), :] = jnp.where(last, 0.0, after[...])
        dabuf[pl.ds(0, tb), :] = da_ref[...]
        dabuf[pl.ds(tb, HALO), :] = jnp.where(last, 0.0, daa_ref[...])

        @pl.when(i == 0)
        def _():
            dwg_ref[...] = jnp.zeros_like(dwg_ref)
            dwu_ref[...] = jnp.zeros_like(dwu_ref)

        for j in range(cb // LANES):
            sl = pl.ds(j * LANES, LANES)
            ug = _conv_taps(gbuf, wg_ref, HALO - (kk - 1), ext, sl)
            uu = _conv_taps(ubuf, wu_ref, HALO - (kk - 1), ext, sl)
            _, vjp = jax.vjp(lambda g, u: _silu(g) * u, ug, uu)
            dgbuf[:, sl], dubuf[:, sl] = vjp(dabuf[:, sl])
            for dbuf, xbuf, w_ref, dx_ref, dw_ref in ((dgbuf, gbuf, wg_ref, dg_ref, dwg_ref),
                                                      (dubuf, ubuf, wu_ref, du_ref, dwu_ref)):
                x = xbuf[pl.ds(HALO, tb), sl]
                dx = None
                for s in range(kk):
                    shifted = dbuf[pl.ds(kk - 1 - s, tb), sl]
                    term = w_ref[s:s + 1, sl] * shifted
                    dx = term if dx is None else dx + term
                    dw_ref[s:s + 1, sl] += jnp.sum(shifted * x, axis=0, keepdims=True)
                dx_ref[:, sl] = dx.astype(BF16)

    blk = pl.BlockSpec((tb, cb), lambda j, i: (i, j))
    before = pl.BlockSpec((HALO, cb), lambda j, i: (jnp.maximum(i * per - 1, 0), j))
    after = pl.BlockSpec((HALO, cb), lambda j, i: (jnp.minimum((i + 1) * per, t // HALO - 1), j))
    wspec = pl.BlockSpec((kk, cb), lambda j, i: (0, j))
    dwspec = pl.BlockSpec((HALO, cb), lambda j, i: (0, j))
    half = jax.ShapeDtypeStruct((t, c), BF16)
    dwshape = jax.ShapeDtypeStruct((HALO, c), F32)
    return pl.pallas_call(
        body, name="ffn_mid_bwd", grid=(c // cb, nblk),
        in_specs=[blk, before, after, blk, before, after, blk, after, wspec, wspec],
        out_specs=[blk, blk, dwspec, dwspec],
        out_shape=[half, half, dwshape, dwshape],
        scratch_shapes=[pltpu.VMEM((ext + HALO, cb), F32)] * 2 + [pltpu.VMEM((ext, cb), F32)] * 3,
        compiler_params=_params(("parallel", "arbitrary")),
    )(pre_g, pre_g, pre_g, pre_u, pre_u, pre_u, da, da, wg, wu)


def _down_loss(a, w_down, x1, wf, target):
    t = x1.shape[0]
    tb = _rows(t)

    def body(a_ref, wd_ref, x1_ref, wf_ref, tgt_ref, dx2_ref, dwf_ref, loss_ref):
        i = pl.program_id(0)
        x2 = x1_ref[...] + _dot(a_ref[...], wd_ref[...])
        y, vjp = jax.vjp(_rms, x2, wf_ref[...])
        err = y - tgt_ref[...]
        dx2, dwf = vjp(err * (1.0 / D_MODEL))
        dx2_ref[...] = dx2
        part = jnp.sum(jnp.sum(err * err, axis=1, keepdims=True), axis=0, keepdims=True) * (0.5 / D_MODEL)

        @pl.when(i == 0)
        def _():
            dwf_ref[...] = jnp.zeros_like(dwf_ref)
            loss_ref[...] = jnp.zeros_like(loss_ref)

        dwf_ref[...] += dwf
        loss_ref[...] += jnp.broadcast_to(part, loss_ref.shape)

    row = pl.BlockSpec((tb, D_MODEL), lambda i: (i, 0))
    vec = pl.BlockSpec((1, D_MODEL), lambda i: (0, 0))
    return pl.pallas_call(
        body, name="down_loss", grid=(t // tb,),
        in_specs=[pl.BlockSpec((tb, D_FF), lambda i: (i, 0)), pl.BlockSpec((D_FF, D_MODEL), lambda i: (0, 0)),
                  row, vec, row],
        out_specs=[row, vec, pl.BlockSpec((1, LANES), lambda i: (0, 0))],
        out_shape=[jax.ShapeDtypeStruct((t, D_MODEL), F32), jax.ShapeDtypeStruct((1, D_MODEL), F32),
                   jax.ShapeDtypeStruct((1, LANES), F32)],
        compiler_params=_params(("arbitrary",)),
    )(a, w_down, x1, wf, target)


def _local_step(x, target, wts, late_wire=(), late_weights=None, early_partials=None):
    t = x.shape[0]
    nchunk = t // DN_CHUNK

    n1, hab = _norm1_fwd(x, wts["norm1"], wts["w_ab"])
    dnqkv = _mm(n1, wts["w_dnqkv"], name="h_dnqkv")
    dngate = _mm(n1, wts["w_dngate"], name="h_dngate")
    sbqkv = _mm(n1, wts["w_sbqkv"], out_dtype=BF16, name="h_sbqkv")
    gl = _mm(n1, wts["w_gl"], name="h_gl")

    cdn = _conv_fwd(dnqkv, wts["dn_conv"], "dn_conv_fwd")
    qn, kn, vv, gb = _dn_prep_fwd(cdn, hab, wts["alog"], wts["dtb"])
    per_head = gb[:, :2 * N_HEADS].T.reshape(2 * N_HEADS, nchunk, DN_CHUNK)
    grow, brow = per_head[:N_HEADS, :, None, :], per_head[N_HEADS:, :, None, :]
    u_dn, w_dn, a_qk, qe, kdec, egl, tinv, *late = _dn_local_fwd(qn, kn, vv, grow, brow, late_wire)
    if late_wire:
        wts = {**wts, **late_weights(late)}
    o_raw, states = _dn_seq_fwd(u_dn, w_dn, a_qk, qe, kdec, egl)
    o_dn = _dn_post_fwd(o_raw, dngate, wts["dn_norm"])

    o_sb, tot, sb_used = _sb_fwd(sbqkv)

    pdn, psb, mixed, x1, n2 = _merge_fwd(o_dn, o_sb, gl, x, wts["wp_dn"], wts["wp_sb"], wts["w_out"],
                                         wts["norm2"])
    pre_g = _mm(n2, wts["w_up_g"], name="ffn_up_g")
    pre_u = _mm(n2, wts["w_up_u"], name="ffn_up_u")
    act = _ffn_mid_fwd(pre_g, pre_u, wts["ffn_conv_g"], wts["ffn_conv_u"])
    dx2, d_normf, loss_part = _down_loss(act, wts["w_down"], x1, wts["normf"], target)

    grads = {"normf": d_normf}
    da = _mm(dx2, wts["w_down"], tb=True, name="d_act")
    grads["w_down"] = _mm(act, dx2, ta=True, out_dtype=BF16, name="dw_down")
    dpre_g, dpre_u, dcw_g, dcw_u = _ffn_mid_bwd(pre_g, pre_u, wts["ffn_conv_g"], wts["ffn_conv_u"], da)
    grads["ffn_conv"] = jnp.concatenate([dcw_g[:FFN_CONV], dcw_u[:FFN_CONV]], axis=1)
    dn2 = _mm(dpre_g, wts["w_up_g"], tb=True, name="dn2_g")
    dn2 = _mm(dpre_u, wts["w_up_u"], tb=True, add=dn2, name="dn2_u")
    grads["w_up_g"] = _mm(n2, dpre_g, ta=True, out_dtype=BF16, name="dw_up_g")
    grads["w_up_u"] = _mm(n2, dpre_u, ta=True, out_dtype=BF16, name="dw_up_u")

    dx1, grads["norm2"], dgl, dpdn, dpsb, do_dn, do_sb = _merge_bwd(
        dx2, dn2, x1, wts["norm2"], gl, pdn, psb, wts["wp_dn"], wts["wp_sb"], wts["w_out"])
    grads["w_out"] = _mm(mixed, dx1, ta=True, out_dtype=BF16, name="dw_out")
    grads["wp_dn"] = _mm(o_dn, dpdn, ta=True, out_dtype=BF16, name="dw_proj_dn")
    grads["wp_sb"] = _mm(o_sb, dpsb, ta=True, out_dtype=BF16, name="dw_proj_sb")

    partials = early_partials(grads) if early_partials else ()
    dsq, dsk, dsv, *arrived = _sb_bwd(sbqkv, tot, sb_used, do_sb, partials)
    dsbqkv = jnp.concatenate([dsq, dsk, dsv], axis=1).astype(BF16)

    do_raw, ddngate, grads["dn_norm"] = _dn_post_bwd(o_raw, dngate, wts["dn_norm"], do_dn)
    seq_grads = _dn_seq_bwd(u_dn, w_dn, a_qk, qe, kdec, egl, states, do_raw)
    dqn, dkn, dvv, dgrow, dbrow = _dn_local_bwd(qn, kn, vv, grow, brow, tinv, *seq_grads)
    dgb = jnp.concatenate([dgrow.reshape(N_HEADS, t), dbrow.reshape(N_HEADS, t)], axis=0).T
    dgb = jnp.pad(dgb, ((0, 0), (0, LANES - 2 * N_HEADS)))
    dcdn, dhab, grads["alog"], grads["dtb"] = _dn_prep_bwd(cdn, hab, wts["alog"], wts["dtb"], dqn, dkn, dvv, dgb)
    ddnqkv, dcw_dn = _conv_bwd(dcdn, dnqkv, wts["dn_conv"], "dn_conv_bwd", BF16)
    grads["dn_conv"] = dcw_dn[:DN_CONV]

    dn1 = _mm(ddnqkv, wts["w_dnqkv"], tb=True, name="dn1_dnqkv")
    dn1 = _mm(ddngate, wts["w_dngate"], tb=True, add=dn1, name="dn1_dngate")
    dn1 = _mm(dsbqkv, wts["w_sbqkv"], tb=True, add=dn1, name="dn1_sbqkv")
    dn1 = _mm(dgl, wts["w_gl"], tb=True, add=dn1, name="dn1_gl")
    grads["w_dnqkv"] = _mm(n1, ddnqkv, ta=True, out_dtype=BF16, name="dw_dnqkv")
    grads["w_dngate"] = _mm(n1, ddngate, ta=True, out_dtype=BF16, name="dw_dngate")
    grads["w_sbqkv"] = _mm(n1, dsbqkv, ta=True, out_dtype=BF16, name="dw_sbqkv")
    grads["w_gl"] = _mm(n1, dgl, ta=True, out_dtype=BF16, name="dw_gl")
    grads["w_ab"] = _mm(n1, dhab, ta=True, out_dtype=BF16, name="dw_ab")
    grad_x, grads["norm1"] = _norm1_bwd(x, wts["norm1"], dn1, dx1, dhab, wts["w_ab"])
    return loss_part, grad_x, grads, (list(partials), arrived)


def _place():
    return lax.axis_index("x"), lax.axis_index("y"), lax.axis_index("c")


def _hbm_specs(n):
    return [pl.BlockSpec(memory_space=pltpu.HBM)] * n


GATHER_SEMS = 8
GATHER_FORWARD_AT = 5


def _gather_protocol(ins, outs, send_sems, recv_sems):
    n = len(ins)
    x, y, c = _place()
    me = 2 * x + y
    sibling = (x, y, 1 - c)
    xn, yn, dg = (1 - x, y), (x, 1 - y), (1 - x, 1 - y)
    idx = lambda chip: 2 * chip[0] + chip[1]

    def part(a, chip_index, core, quarter=None):
        half = ins[a].shape[0] // 2
        if quarter is None:
            return outs[a].at[chip_index, pl.ds(core * half, half), :]
        return outs[a].at[chip_index, pl.ds(core * half + quarter * (half // 2), half // 2), :]

    def copy(a, k, src, dst, to):
        return pltpu.make_async_remote_copy(src_ref=src, dst_ref=dst, send_sem=send_sems.at[GATHER_SEMS * a + k],
                                            recv_sem=recv_sems.at[GATHER_SEMS * a + k], device_id=to,
                                            device_id_type=MESH)

    def sent(a, k):
        half = ins[a].shape[0] // 2
        my_half = ins[a].at[pl.ds(c * half, half), :]
        if k < 2:
            return copy(a, k, my_half, part(a, me, c), (*(xn, yn)[k], c))
        if k < 4:
            src = part(a, idx((xn, yn)[k - 2]), c, k - 2)
            return copy(a, k, src, src, (*(yn, xn)[k - 2], c))
        src = (part(a, idx(xn), c), part(a, idx(yn), c), part(a, idx(dg), c, 0), part(a, idx(dg), c, 1))[k - 4]
        return copy(a, k, src, src, sibling)

    def landed(a, k):
        dst = (part(a, idx(xn), c), part(a, idx(yn), c), part(a, idx(dg), c, 0), part(a, idx(dg), c, 1),
               part(a, idx(xn), 1 - c), part(a, idx(yn), 1 - c), part(a, idx(dg), 1 - c, 0),
               part(a, idx(dg), 1 - c, 1))[k]
        return copy(a, k, dst, dst, sibling)

    def begin():
        for a in range(n):
            sent(a, 0).start()
            sent(a, 1).start()

    def middle():
        for a in range(n):
            for k in range(2):
                landed(a, k).wait_recv()
                sent(a, 2 + k).start()
                sent(a, 4 + k).start()

    def end():
        for a in range(n):
            for k in (2, 3):
                landed(a, k).wait_recv()
                sent(a, 4 + k).start()
        for a in range(n):
            for k in range(4, GATHER_SEMS):
                landed(a, k).wait_recv()
        for a in range(n):
            for k in range(GATHER_SEMS):
                sent(a, k).wait_send()

    return begin, middle, end


def _gather_out_shapes(shards):
    return [jax.ShapeDtypeStruct((N_CHIPS,) + s.shape, s.dtype) for s in shards]


def _gather_sems(n):
    return [pltpu.SemaphoreType.DMA((GATHER_SEMS * n,)), pltpu.SemaphoreType.DMA((GATHER_SEMS * n,))]


def _gather_shards(shards):
    n = len(shards)

    def body(*refs):
        begin, middle, end = _gather_protocol(refs[:n], refs[n:2 * n], *refs[2 * n:])
        begin()
        middle()
        end()

    return pl.pallas_call(
        body, name="gather_weights", in_specs=_hbm_specs(n), out_specs=_hbm_specs(n),
        out_shape=_gather_out_shapes(shards), scratch_shapes=_gather_sems(n),
    )(*shards)


def _pair_exchange_halves(gs, tag):
    n = len(gs)

    def body(*refs):
        ins, outs, (send_sems, recv_sems) = refs[:n], refs[n:2 * n], refs[2 * n:]
        x, y, c = _place()
        cps = []
        for a in range(n):
            half = ins[a].shape[1] // 2
            cp = pltpu.make_async_remote_copy(src_ref=ins[a].at[:, pl.ds((1 - c) * half, half), :], dst_ref=outs[a],
                                              send_sem=send_sems.at[a], recv_sem=recv_sems.at[a],
                                              device_id=(x, y, 1 - c), device_id_type=MESH)
            cp.start()
            cps.append(cp)
        for cp in cps:
            cp.wait()

    return pl.pallas_call(
        body, name="grad_pair_exchange_" + tag, in_specs=_hbm_specs(n), out_specs=_hbm_specs(n),
        out_shape=[jax.ShapeDtypeStruct((g.shape[0], g.shape[1] // 2, g.shape[2]), g.dtype) for g in gs],
        scratch_shapes=[pltpu.SemaphoreType.DMA((n,)), pltpu.SemaphoreType.DMA((n,))],
    )(*gs)


def _pick_rows(n, target=1024):
    best = 16
    for b in range(16, min(n, target) + 1, 16):
        if n % b == 0:
            best = b
    return best


def _pair_add(g, got, c_idx, tag):
    nsh, rows, cols = g.shape
    half = rows // 2
    rb = _pick_rows(half)

    def body(c_ref, g_ref, got_ref, o_ref):
        o_ref[...] = (g_ref[...].astype(F32) + got_ref[...].astype(F32)).astype(BF16)

    nb = half // rb
    grid_spec = pltpu.PrefetchScalarGridSpec(
        num_scalar_prefetch=1, grid=(nsh, nb),
        in_specs=[pl.BlockSpec((1, rb, cols), lambda s, i, c_ref: (s, c_ref[0] * nb + i, 0)),
                  pl.BlockSpec((1, rb, cols), lambda s, i, c_ref: (s, i, 0))],
        out_specs=pl.BlockSpec((1, rb, cols), lambda s, i, c_ref: (s, i, 0)))
    return pl.pallas_call(
        body, name="grad_pair_add_" + tag, grid_spec=grid_spec,
        out_shape=jax.ShapeDtypeStruct((nsh, half, cols), BF16),
        compiler_params=_params(("parallel", "parallel")),
    )(c_idx, g, got)


def _chip_exchange_protocol(ins, outs, send_sems, recv_sems):
    x, y, c = _place()
    chips = [(1 - x, y), (x, 1 - y), (1 - x, 1 - y)]

    def copies():
        return [pltpu.make_async_remote_copy(src_ref=ins[a].at[2 * px + py], dst_ref=outs[a].at[j],
                                             send_sem=send_sems.at[3 * a + j], recv_sem=recv_sems.at[3 * a + j],
                                             device_id=(px, py, c), device_id_type=MESH)
                for a in range(len(ins)) for j, (px, py) in enumerate(chips)]

    def begin():
        for cp in copies():
            cp.start()

    def end():
        for cp in copies():
            cp.wait_recv()
        for cp in copies():
            cp.wait_send()

    return begin, end


def _chip_exchange_shapes(ps):
    return [jax.ShapeDtypeStruct((N_CHIPS - 1,) + p.shape[1:], p.dtype) for p in ps]


def _chip_exchange_sems(n):
    return [pltpu.SemaphoreType.DMA((3 * n,)), pltpu.SemaphoreType.DMA((3 * n,))]


def _chip_exchange(ps):
    n = len(ps)

    def body(*refs):
        begin, end = _chip_exchange_protocol(refs[:n], refs[n:2 * n], *refs[2 * n:])
        begin()
        end()

    return pl.pallas_call(
        body, name="grad_chip_exchange", in_specs=_hbm_specs(n), out_specs=_hbm_specs(n),
        out_shape=_chip_exchange_shapes(ps), scratch_shapes=_chip_exchange_sems(n),
    )(*ps)


def _sum_partials(p, got, chip_idx, tag):
    nsh, half, cols = got.shape
    rb = _pick_rows(half)

    def body(me_ref, p_ref, got_ref, o_ref):
        acc = p_ref[0].astype(F32)
        for s in range(nsh):
            acc = acc + got_ref[s].astype(F32)
        o_ref[...] = acc

    grid_spec = pltpu.PrefetchScalarGridSpec(
        num_scalar_prefetch=1, grid=(half // rb,),
        in_specs=[pl.BlockSpec((1, rb, cols), lambda i, me_ref: (me_ref[0], i, 0)),
                  pl.BlockSpec((nsh, rb, cols), lambda i, me_ref: (0, i, 0))],
        out_specs=pl.BlockSpec((rb, cols), lambda i, me_ref: (i, 0)))
    return pl.pallas_call(
        body, name="grad_sum_chips_" + tag, grid_spec=grid_spec,
        out_shape=jax.ShapeDtypeStruct((half, cols), F32),
        compiler_params=_params(("parallel",)),
    )(chip_idx, p, got)


def _pair_share(rs):
    n = len(rs)

    def body(*refs):
        ins, outs, (send_sems, recv_sems) = refs[:n], refs[n:2 * n], refs[2 * n:]
        x, y, c = _place()
        cps = []
        for a in range(n):
            cp = pltpu.make_async_remote_copy(src_ref=ins[a], dst_ref=outs[a], send_sem=send_sems.at[a],
                                              recv_sem=recv_sems.at[a], device_id=(x, y, 1 - c),
                                              device_id_type=MESH)
            cp.start()
            cps.append(cp)
        for cp in cps:
            cp.wait()

    return pl.pallas_call(
        body, name="grad_pair_share", in_specs=_hbm_specs(n), out_specs=_hbm_specs(n),
        out_shape=[jax.ShapeDtypeStruct(r.shape, r.dtype) for r in rs],
        scratch_shapes=[pltpu.SemaphoreType.DMA((n,)), pltpu.SemaphoreType.DMA((n,))],
    )(*rs)


def _small_allreduce(v):
    rows, cols = v.shape
    ndev = 8

    def body(in_ref, out_ref, slots, send_sems, recv_sems):
        x, y, c = _place()
        me = 4 * x + 2 * y + c
        slots[me] = in_ref[...]
        sends = []
        for k in range(1, ndev):
            peer = (x ^ (k >> 2), y ^ ((k >> 1) & 1), c ^ (k & 1))
            cp = pltpu.make_async_remote_copy(src_ref=in_ref, dst_ref=slots.at[me], send_sem=send_sems.at[k - 1],
                                              recv_sem=recv_sems.at[k - 1], device_id=peer, device_id_type=MESH)
            cp.start()
            sends.append(cp)
        for k in range(1, ndev):
            there = slots.at[me ^ k]
            pltpu.make_async_remote_copy(src_ref=there, dst_ref=there, send_sem=send_sems.at[k - 1],
                                         recv_sem=recv_sems.at[k - 1], device_id=(x, y, c),
                                         device_id_type=MESH).wait_recv()
        for cp in sends:
            cp.wait_send()
        acc = slots[0]
        for s in range(1, ndev):
            acc = acc + slots[s]
        out_ref[...] = acc

    return pl.pallas_call(
        body, name="small_allreduce",
        in_specs=[pl.BlockSpec(memory_space=pltpu.VMEM)],
        out_specs=pl.BlockSpec(memory_space=pltpu.VMEM),
        out_shape=jax.ShapeDtypeStruct((rows, cols), F32),
        scratch_shapes=[pltpu.VMEM((ndev, rows, cols), F32), pltpu.SemaphoreType.DMA((ndev - 1,)),
                        pltpu.SemaphoreType.DMA((ndev - 1,))],
    )(v)


def _adamw(w, g, m, v, name):
    r, c = w.shape
    rb = r if r <= 128 else _pick_rows_8(r, 128)
    c1 = 1.0 - ADAM_B1 ** ADAM_STEP
    c2 = 1.0 - ADAM_B2 ** ADAM_STEP

    def body(w_ref, g_ref, m_ref, v_ref, d_ref, nm_ref, nv_ref):
        gg = g_ref[...]
        nm = ADAM_B1 * m_ref[...] + (1.0 - ADAM_B1) * gg
        nv = ADAM_B2 * v_ref[...] + (1.0 - ADAM_B2) * (gg * gg)
        d_ref[...] = -ADAM_LR * ((nm / c1) / (jnp.sqrt(nv / c2) + ADAM_EPS) + ADAM_WD * w_ref[...])
        nm_ref[...] = nm
        nv_ref[...] = nv

    blk = pl.BlockSpec((rb, c), lambda i: (i, 0))
    shp = jax.ShapeDtypeStruct((r, c), F32)
    return pl.pallas_call(
        body, name=name, grid=(r // rb,), in_specs=[blk] * 4, out_specs=[blk] * 3, out_shape=[shp] * 3,
        compiler_params=_params(("parallel",)),
    )(w, g, m, v)


def _pick_rows_8(n, target):
    best = n
    for b in range(8, min(n, target) + 1, 8):
        if n % b == 0:
            best = b
    return best


W_IN_COLS = 2308
W_UP_COLS = 1408
W_DOWN_ROWS = 704
DN_CONV_COLS = 768
FFN_CONV_COLS = 1408
PROJ_ROWS = 256
ROW_TILE = 16
ROW_SEGS = [("wp_dn", PROJ_ROWS), ("wp_sb", PROJ_ROWS), ("w_out", PROJ_ROWS), ("w_down", W_DOWN_ROWS),
            ("dn_conv", ROW_TILE), ("ffn_conv", ROW_TILE), ("spare", 2 * ROW_TILE)]
ROW_OFFS = {nm: (sum(n for _, n in ROW_SEGS[:i]), n) for i, (nm, n) in enumerate(ROW_SEGS)}
STACK_ROWS = sum(n for _, n in ROW_SEGS)
assert all(n % ROW_TILE == 0 for _, n in ROW_SEGS) and STACK_ROWS % (4 * ROW_TILE) == 0
Q_END, A_END, G_END, S_END = 3 * D_MODEL, 3 * D_MODEL + 2 * N_HEADS, 4 * D_MODEL + 2 * N_HEADS, 7 * D_MODEL + 2 * N_HEADS


def _flat_rows(a, nrows):
    flat = a.reshape(-1)
    return jnp.pad(flat, (0, nrows * D_MODEL - flat.shape[0])).reshape(nrows, D_MODEL)


IN_EXTRA_ROWS = 64
IN_CHUNK_ROWS = 576


def _weight_wire(w_in, wp_dn, wp_sb, w_out, w_up, w_down, dn_conv, ffn_conv):
    bits = lax.bitcast_convert_type(dn_conv, BF16).reshape(-1)
    extra = jnp.pad(bits, (0, IN_EXTRA_ROWS * W_IN_COLS - bits.shape[0])).reshape(IN_EXTRA_ROWS, W_IN_COLS)
    stack = jnp.concatenate([wp_dn.astype(BF16), wp_sb.astype(BF16), w_out.astype(BF16), w_down.astype(BF16),
                             jnp.zeros((ROW_TILE, D_MODEL), BF16),
                             _flat_rows(lax.bitcast_convert_type(ffn_conv, BF16), ROW_TILE),
                             jnp.zeros((ROW_OFFS["spare"][1], D_MODEL), BF16)], axis=0)
    first = jnp.concatenate([w_in.astype(BF16), extra], axis=0)
    return [first[:IN_CHUNK_ROWS], first[IN_CHUNK_ROWS:]], [w_up.astype(BF16), stack]


def _col_range(g, lo, hi, width):
    parts = []
    for s in range(g.shape[0]):
        a, b = max(lo, s * width), min(hi, (s + 1) * width)
        if a < b:
            parts.append(g[s][:, a - s * width:b - s * width])
    return parts[0] if len(parts) == 1 else jnp.concatenate(parts, axis=1)


def _f32_rows(raw, k, ncols):
    raw = raw.reshape(N_CHIPS, -1)[:, :2 * k * ncols].reshape(N_CHIPS, k * ncols, 2)
    vals = lax.bitcast_convert_type(raw, F32).reshape(N_CHIPS, k, ncols)
    return vals.transpose(1, 0, 2).reshape(k, N_CHIPS * ncols)


def _unpack_early(*chunks):
    g_in = jnp.concatenate(chunks, axis=1)
    w = g_in[:, :D_MODEL, :]
    return {
        "w_dnqkv": _col_range(w, 0, Q_END, W_IN_COLS),
        "w_ab": jnp.pad(_col_range(w, Q_END, A_END, W_IN_COLS), ((0, 0), (0, LANES - 2 * N_HEADS))),
        "w_dngate": _col_range(w, A_END, G_END, W_IN_COLS),
        "w_sbqkv": _col_range(w, G_END, S_END, W_IN_COLS),
        "w_gl": _col_range(w, S_END, N_CHIPS * W_IN_COLS, W_IN_COLS),
        "dn_conv": _f32_rows(g_in[:, D_MODEL:, :], DN_CONV, DN_CONV_COLS),
    }


def _unpack_late(g_up, g_stack):
    def seg(nm):
        at, n = ROW_OFFS[nm]
        return g_stack[:, at:at + n, :]

    ffn_conv = _f32_rows(seg("ffn_conv"), FFN_CONV, FFN_CONV_COLS)
    return {
        "wp_dn": seg("wp_dn").reshape(D_MODEL, D_MODEL),
        "wp_sb": seg("wp_sb").reshape(D_MODEL, D_MODEL),
        "w_out": seg("w_out").reshape(D_MODEL, D_MODEL),
        "w_up_g": _col_range(g_up, 0, D_FF, W_UP_COLS), "w_up_u": _col_range(g_up, D_FF, 2 * D_FF, W_UP_COLS),
        "w_down": seg("w_down").reshape(D_FF, D_MODEL),
        "ffn_conv_g": ffn_conv[:, :D_FF], "ffn_conv_u": ffn_conv[:, D_FF:],
    }


def _grad_wire_early(gr):
    def cols(a, ncols):
        return a.reshape(a.shape[0], N_CHIPS, ncols).transpose(1, 0, 2)

    def rows(a, nrows):
        return a.astype(BF16).reshape(N_CHIPS, nrows, a.shape[1])

    def flat(a, nrows):
        a = a.astype(BF16).reshape(N_CHIPS, -1)
        return jnp.pad(a, ((0, 0), (0, nrows * D_MODEL - a.shape[1]))).reshape(N_CHIPS, nrows, D_MODEL)

    up = [gr["w_up_g"], gr["w_up_u"]]
    g_up = jnp.stack([up[s // 2][:, (s % 2) * W_UP_COLS:(s % 2 + 1) * W_UP_COLS].astype(BF16) for s in range(N_CHIPS)])
    g_stack = jnp.concatenate([rows(gr["wp_dn"], PROJ_ROWS), rows(gr["wp_sb"], PROJ_ROWS), rows(gr["w_out"], PROJ_ROWS),
                               rows(gr["w_down"], W_DOWN_ROWS), jnp.zeros((N_CHIPS, ROW_TILE, D_MODEL), BF16),
                               flat(cols(gr["ffn_conv"], FFN_CONV_COLS), ROW_TILE),
                               jnp.zeros((N_CHIPS, ROW_OFFS["spare"][1], D_MODEL), BF16)], axis=1)
    return [g_up, g_stack]


def _grad_wire_late(gr):
    pieces = [(gr["w_dnqkv"], 0), (gr["w_ab"][:, :2 * N_HEADS], Q_END), (gr["w_dngate"], A_END),
              (gr["w_sbqkv"], G_END), (gr["w_gl"], S_END)]
    conv = gr["dn_conv"].reshape(DN_CONV, N_CHIPS, DN_CONV_COLS).transpose(1, 0, 2).reshape(N_CHIPS, -1)

    def block(s):
        lo, hi = s * W_IN_COLS, (s + 1) * W_IN_COLS
        parts = []
        for a, at in pieces:
            b0, b1 = max(lo, at), min(hi, at + a.shape[1])
            if b0 < b1:
                parts.append(a[:, b0 - at:b1 - at].astype(BF16))
        w = parts[0] if len(parts) == 1 else jnp.concatenate(parts, axis=1)
        extra = jnp.pad(conv[s].astype(BF16), (0, IN_EXTRA_ROWS * W_IN_COLS - conv.shape[1]))
        return jnp.concatenate([w, extra.reshape(IN_EXTRA_ROWS, W_IN_COLS)], axis=0)

    return [jnp.stack([block(s) for s in range(N_CHIPS)])]


def _unpack_grad_shard(r_in, r_up, r_stack):
    def seg(nm):
        at, n = ROW_OFFS[nm]
        return r_stack[at:at + n, :]

    return {
        "w_in": r_in[:D_MODEL], "w_up": r_up,
        "wp_dn": seg("wp_dn"), "wp_sb": seg("wp_sb"), "w_out": seg("w_out"), "w_down": seg("w_down"),
        "dn_conv": r_in[D_MODEL:].reshape(-1)[:DN_CONV * DN_CONV_COLS].reshape(DN_CONV, DN_CONV_COLS),
        "ffn_conv": seg("ffn_conv").reshape(-1)[:FFN_CONV * FFN_CONV_COLS].reshape(FFN_CONV, FFN_CONV_COLS),
    }


def _lane_row(v):
    return jnp.pad(v.reshape(1, -1), ((0, 0), (0, LANES - v.size)))


def kernel(x, norm1_w, w_in, dn_conv_w, dn_A_log, dn_dt_bias, dn_norm_w, w_proj_dn, w_proj_sb, w_out, norm2_w, ffn_w_up, ffn_conv_w, ffn_w_down, norm_f_w, loss_target, m_norm1_w, m_w_in, m_dn_conv_w, m_dn_A_log, m_dn_dt_bias, m_dn_norm_w, m_w_proj_dn, m_w_proj_sb, m_w_out, m_norm2_w, m_ffn_w_up, m_ffn_conv_w, m_ffn_w_down, m_norm_f_w, v_norm1_w, v_w_in, v_dn_conv_w, v_dn_A_log, v_dn_dt_bias, v_dn_norm_w, v_w_proj_dn, v_w_proj_sb, v_w_out, v_norm2_w, v_ffn_w_up, v_ffn_conv_w, v_ffn_w_down, v_norm_f_w):
    early, late = _weight_wire(w_in[0], w_proj_dn[0], w_proj_sb[0], w_out[0], ffn_w_up[0], ffn_w_down[0],
                               dn_conv_w[0], ffn_conv_w[0])
    chip_idx = (2 * lax.axis_index("x") + lax.axis_index("y")).astype(jnp.int32)

    def with_mine(gathered, wire):
        return [lax.dynamic_update_slice(g, mine[None], (chip_idx, 0, 0)) for g, mine in zip(gathered, wire)]

    wts = _unpack_early(*with_mine(_gather_shards(early), early))
    wts.update(norm1=norm1_w, norm2=norm2_w, normf=norm_f_w.reshape(1, D_MODEL), dn_norm=dn_norm_w,
               alog=_lane_row(dn_A_log), dtb=_lane_row(dn_dt_bias))

    c_idx = lax.axis_index("c").astype(jnp.int32).reshape(1)

    def pair_sums(wire_g, tags, when):
        return [_pair_add(g, got, c_idx, tag) for g, got, tag in zip(wire_g, _pair_exchange_halves(wire_g, when), tags)]

    loss_part, grad_x, gr, (early_sums, early_arrived) = _local_step(
        x[0], loss_target[0], wts, late, lambda gathered: _unpack_late(*with_mine(gathered, late)),
        lambda grads: pair_sums(_grad_wire_early(grads), ["w_up", "rows"], "early"))

    late_sums = pair_sums(_grad_wire_late(gr), ["w_in"], "late")
    tags = ["w_in", "w_up", "rows"]
    reduced = [_sum_partials(p, got, chip_idx.reshape(1), tag)
               for p, got, tag in zip(late_sums + early_sums, list(_chip_exchange(late_sums)) + list(early_arrived), tags)]
    is_south = lax.axis_index("c") == 0
    gsh = _unpack_grad_shard(*[jnp.concatenate([jnp.where(is_south, mine, other), jnp.where(is_south, other, mine)],
                                               axis=0) for mine, other in zip(reduced, _pair_share(reduced))])

    tail = jnp.concatenate([gr["dn_norm"], gr["alog"][:, :N_HEADS], gr["dtb"][:, :N_HEADS], loss_part[:, :1]], axis=1)
    small = jnp.concatenate([gr["norm1"], gr["norm2"], gr["normf"],
                             jnp.pad(tail, ((0, 0), (0, D_MODEL - tail.shape[1]))),
                             jnp.zeros((SMALL_ROWS - 4, D_MODEL), F32)], axis=0)
    small = _small_allreduce(small)
    at = HEAD_DIM
    g_small = {"norm1_w": small[0:1], "norm2_w": small[1:2], "norm_f_w": small[2],
               "dn_norm_w": small[3:4, :at], "dn_A_log": small[3:4, at:at + N_HEADS],
               "dn_dt_bias": small[3:4, at + N_HEADS:at + 2 * N_HEADS]}
    loss = small[3, at + 2 * N_HEADS]

    big = {"w_in": (w_in, m_w_in, v_w_in, gsh["w_in"]), "dn_conv_w": (dn_conv_w, m_dn_conv_w, v_dn_conv_w, gsh["dn_conv"]),
           "w_proj_dn": (w_proj_dn, m_w_proj_dn, v_w_proj_dn, gsh["wp_dn"]),
           "w_proj_sb": (w_proj_sb, m_w_proj_sb, v_w_proj_sb, gsh["wp_sb"]),
           "w_out": (w_out, m_w_out, v_w_out, gsh["w_out"]),
           "ffn_w_up": (ffn_w_up, m_ffn_w_up, v_ffn_w_up, gsh["w_up"]),
           "ffn_conv_w": (ffn_conv_w, m_ffn_conv_w, v_ffn_conv_w, gsh["ffn_conv"]),
           "ffn_w_down": (ffn_w_down, m_ffn_w_down, v_ffn_w_down, gsh["w_down"])}
    res = {}
    for nm, (w, m, v, g) in big.items():
        d, nm_, nv_ = _adamw(w[0], g, m[0], v[0], "adamw_" + nm)
        res[nm] = (g[None], d[None], nm_[None], nv_[None])

    names = ["norm1_w", "norm2_w", "norm_f_w", "dn_norm_w", "dn_A_log", "dn_dt_bias"]
    given = {"norm1_w": (norm1_w, m_norm1_w, v_norm1_w), "norm2_w": (norm2_w, m_norm2_w, v_norm2_w),
             "norm_f_w": (norm_f_w, m_norm_f_w, v_norm_f_w), "dn_norm_w": (dn_norm_w, m_dn_norm_w, v_dn_norm_w),
             "dn_A_log": (dn_A_log, m_dn_A_log, v_dn_A_log), "dn_dt_bias": (dn_dt_bias, m_dn_dt_bias, v_dn_dt_bias)}

    def stack(k, fill):
        rows = [jnp.pad(given[nm][k].reshape(1, -1), ((0, 0), (0, D_MODEL - given[nm][k].size)),
                        constant_values=fill) for nm in names]
        return jnp.concatenate(rows + [jnp.full((SMALL_ROWS - len(names), D_MODEL), fill, F32)], axis=0)

    g_rows = jnp.concatenate(
        [jnp.pad(g_small[nm].reshape(1, -1), ((0, 0), (0, D_MODEL - g_small[nm].size))) for nm in names]
        + [jnp.zeros((SMALL_ROWS - len(names), D_MODEL), F32)], axis=0)
    d_s, m_s, v_s = _adamw(stack(0, 0.0), g_rows, stack(1, 0.0), stack(2, 1.0), "adamw_small")
    for r, nm in enumerate(names):
        shape = given[nm][0].shape
        n = given[nm][0].size
        res[nm] = (g_small[nm].reshape(shape), d_s[r, :n].reshape(shape), m_s[r, :n].reshape(shape),
                   v_s[r, :n].reshape(shape))

    order = ["norm1_w", "w_in", "dn_conv_w", "dn_A_log", "dn_dt_bias", "dn_norm_w", "w_proj_dn", "w_proj_sb",
             "w_out", "norm2_w", "ffn_w_up", "ffn_conv_w", "ffn_w_down", "norm_f_w"]
    outs = [loss, grad_x[None]]
    for k in range(4):
        outs += [res[nm][k] for nm in order]
    return tuple(outs)
```

```python
import functools

import jax
import jax.numpy as jnp
from jax import lax
from jax.experimental import pallas as pl
from jax.experimental.pallas import tpu as pltpu

F32 = jnp.float32
BF16 = jnp.bfloat16
MESH = pl.DeviceIdType.MESH

EPS = 1e-6
D_MODEL = 1024
N_HEADS = 8
HEAD_DIM = 128
DN_CONV = 4
DN_CHUNK = 64
D_FF = 2816
FFN_CONV = 3
ADAM_LR, ADAM_B1, ADAM_B2, ADAM_EPS, ADAM_WD, ADAM_STEP = 0.001, 0.9, 0.999, 1e-08, 0.01, 10

N_CHIPS = 4
LANES = 128
HALO = 8
VMEM_LIMIT = 48 * 1024 * 1024
SMALL_ROWS = 8


def _params(sem=None):
    return pltpu.CompilerParams(dimension_semantics=sem, vmem_limit_bytes=VMEM_LIMIT)


def _pick(n, target):
    best = None
    for b in range(LANES, min(n, target) + 1, LANES):
        if n % b == 0:
            best = b
    return best or n


ELEMENTWISE_COLS = 1408


def _rows(t, target=256):
    return min(t, target)


def _dot(a, b, precision=None):
    return lax.dot_general(a, b, (((1,), (0,)), ((), ())), precision=precision, preferred_element_type=F32)


def _dot_nt(a, b, precision=None):
    return lax.dot_general(a, b, (((1,), (1,)), ((), ())), precision=precision, preferred_element_type=F32)


def _dot_tn(a, b, precision=None):
    return lax.dot_general(a, b, (((0,), (0,)), ((), ())), precision=precision, preferred_element_type=F32)


def _rms(x, w):
    return x * lax.rsqrt(jnp.mean(x * x, axis=-1, keepdims=True) + EPS) * w


def _silu(x):
    return x * jax.nn.sigmoid(x)


def _softplus(x):
    return jnp.maximum(x, 0.0) + jnp.log(1.0 + jnp.exp(-jnp.abs(x)))


MM_BLOCK = 1408
MM_VMEM_BUDGET = 38 * 1024 * 1024


def _mm(a, b, *, ta=False, tb=False, add=None, out_dtype=F32, name, bm=MM_BLOCK, bn=MM_BLOCK, bk=MM_BLOCK):
    m = a.shape[1] if ta else a.shape[0]
    k = a.shape[0] if ta else a.shape[1]
    n = b.shape[0] if tb else b.shape[1]
    bm, bn = _pick(m, bm), _pick(n, bn)

    def vmem_need(bk_):
        need = 2 * (bm * bk_ * a.dtype.itemsize + bk_ * bn * b.dtype.itemsize) + 2 * bm * bn * jnp.dtype(out_dtype).itemsize
        need += 2 * bm * bn * add.dtype.itemsize if add is not None else 0
        return need + (bm * bn * 4 if bk_ < k else 0)

    bk = max((d for d in range(LANES, k + 1, LANES) if k % d == 0 and vmem_need(d) <= MM_VMEM_BUDGET),
             default=_pick(k, bk))
    nk = k // bk
    dims = (((0 if ta else 1,), (1 if tb else 0,)), ((), ()))

    def body(*refs):
        a_ref, b_ref = refs[:2]
        c_ref = refs[2] if add is not None else None
        o_ref = refs[3] if add is not None else refs[2]
        acc = refs[-1]
        kk = pl.program_id(2)
        part = lax.dot_general(a_ref[...].astype(BF16), b_ref[...].astype(BF16), dims, preferred_element_type=F32)

        def finish(r):
            if add is not None:
                r = r + c_ref[...].astype(F32)
            o_ref[...] = r.astype(out_dtype)

        if nk == 1:
            finish(part)
            return

        @pl.when(kk == 0)
        def _():
            acc[...] = part

        @pl.when(jnp.logical_and(kk > 0, kk < nk - 1))
        def _():
            acc[...] += part

        @pl.when(kk == nk - 1)
        def _():
            finish(acc[...] + part)

    a_spec = (pl.BlockSpec((bk, bm), lambda i, j, kk: (kk, i)) if ta
              else pl.BlockSpec((bm, bk), lambda i, j, kk: (i, kk)))
    b_spec = (pl.BlockSpec((bn, bk), lambda i, j, kk: (j, kk)) if tb
              else pl.BlockSpec((bk, bn), lambda i, j, kk: (kk, j)))
    o_spec = pl.BlockSpec((bm, bn), lambda i, j, kk: (i, j))
    in_specs = [a_spec, b_spec] + ([o_spec] if add is not None else [])
    args = (a, b) + ((add,) if add is not None else ())
    return pl.pallas_call(
        body, name=name, grid=(m // bm, n // bn, nk),
        in_specs=in_specs, out_specs=o_spec,
        out_shape=jax.ShapeDtypeStruct((m, n), out_dtype),
        scratch_shapes=[pltpu.VMEM((bm, bn), F32)] if nk > 1 else [],
        compiler_params=_params(("parallel", "parallel", "arbitrary")),
    )(*args)


def _norm1_fwd(x, w, w_ab):
    t = x.shape[0]
    tb = _rows(t)

    def body(x_ref, w_ref, wab_ref, n_ref, hab_ref):
        n = _rms(x_ref[...], w_ref[...]).astype(BF16)
        n_ref[...] = n
        hab_ref[...] = _dot(n, wab_ref[...])

    return pl.pallas_call(
        body, name="norm1_fwd", grid=(t // tb,),
        in_specs=[pl.BlockSpec((tb, D_MODEL), lambda i: (i, 0)),
                  pl.BlockSpec((1, D_MODEL), lambda i: (0, 0)),
                  pl.BlockSpec((D_MODEL, LANES), lambda i: (0, 0))],
        out_specs=[pl.BlockSpec((tb, D_MODEL), lambda i: (i, 0)),
                   pl.BlockSpec((tb, LANES), lambda i: (i, 0))],
        out_shape=[jax.ShapeDtypeStruct((t, D_MODEL), BF16), jax.ShapeDtypeStruct((t, LANES), F32)],
        compiler_params=_params(("arbitrary",)),
    )(x, w, w_ab)


def _norm1_bwd(x, w, dn, dres, dab, w_ab):
    t = x.shape[0]
    tb = _rows(t)

    def body(x_ref, w_ref, dn_ref, dres_ref, dab_ref, wab_ref, dx_ref, dw_ref):
        i = pl.program_id(0)
        g = dn_ref[...] + _dot_nt(dab_ref[...].astype(BF16), wab_ref[...])
        _, vjp = jax.vjp(_rms, x_ref[...], w_ref[...])
        dx, dw = vjp(g)
        dx_ref[...] = dres_ref[...] + dx

        @pl.when(i == 0)
        def _():
            dw_ref[...] = jnp.zeros_like(dw_ref)

        dw_ref[...] += dw

    row = pl.BlockSpec((tb, D_MODEL), lambda i: (i, 0))
    vec = pl.BlockSpec((1, D_MODEL), lambda i: (0, 0))
    return pl.pallas_call(
        body, name="norm1_bwd", grid=(t // tb,),
        in_specs=[row, vec, row, row, pl.BlockSpec((tb, LANES), lambda i: (i, 0)),
                  pl.BlockSpec((D_MODEL, LANES), lambda i: (0, 0))],
        out_specs=[row, vec],
        out_shape=[jax.ShapeDtypeStruct((t, D_MODEL), F32), jax.ShapeDtypeStruct((1, D_MODEL), F32)],
        compiler_params=_params(("arbitrary",)),
    )(x, w, dn, dres, dab, w_ab)


def _conv_fwd(x, w, name):
    t, c = x.shape
    kk = w.shape[0]
    tb, cb = _rows(t, 512), _pick(c, ELEMENTWISE_COLS)
    per = tb // HALO

    def body(x_ref, halo_ref, w_ref, y_ref, buf):
        i = pl.program_id(0)
        buf[pl.ds(HALO, tb), :] = x_ref[...]
        buf[pl.ds(0, HALO), :] = jnp.where(i == 0, 0.0, halo_ref[...])
        y_ref[...] = _conv_taps(buf, w_ref, HALO - (kk - 1), tb)

    return pl.pallas_call(
        body, name=name, grid=(t // tb, c // cb),
        in_specs=[pl.BlockSpec((tb, cb), lambda i, j: (i, j)),
                  pl.BlockSpec((HALO, cb), lambda i, j: (jnp.maximum(i * per - 1, 0), j)),
                  pl.BlockSpec((kk, cb), lambda i, j: (0, j))],
        out_specs=pl.BlockSpec((tb, cb), lambda i, j: (i, j)),
        out_shape=jax.ShapeDtypeStruct((t, c), F32),
        scratch_shapes=[pltpu.VMEM((tb + HALO, cb), F32)],
        compiler_params=_params(("parallel", "parallel")),
    )(x, x, w)


def _conv_bwd(dy, x, w, name, dx_dtype):
    t, c = x.shape
    kk = w.shape[0]
    tb, cb = _rows(t, 512), _pick(c, ELEMENTWISE_COLS)
    per = tb // HALO
    nblk = t // tb

    def body(dy_ref, after_ref, x_ref, w_ref, dx_ref, dw_ref, dbuf):
        i = pl.program_id(1)
        dbuf[pl.ds(0, tb), :] = dy_ref[...]
        dbuf[pl.ds(tb, HALO), :] = jnp.where(i == nblk - 1, 0.0, after_ref[...])

        @pl.when(i == 0)
        def _():
            dw_ref[...] = jnp.zeros_like(dw_ref)

        for j in range(cb // LANES):
            sl = pl.ds(j * LANES, LANES)
            x = x_ref[:, sl]
            dx = None
            for s in range(kk):
                shifted = dbuf[pl.ds(kk - 1 - s, tb), sl]
                term = w_ref[s:s + 1, sl] * shifted
                dx = term if dx is None else dx + term
                dw_ref[s:s + 1, sl] += jnp.sum(shifted * x, axis=0, keepdims=True)
            dx_ref[:, sl] = dx.astype(dx_dtype)

    blk = pl.BlockSpec((tb, cb), lambda j, i: (i, j))
    return pl.pallas_call(
        body, name=name, grid=(c // cb, nblk),
        in_specs=[blk,
                  pl.BlockSpec((HALO, cb), lambda j, i: (jnp.minimum((i + 1) * per, t // HALO - 1), j)),
                  blk,
                  pl.BlockSpec((kk, cb), lambda j, i: (0, j))],
        out_specs=[blk, pl.BlockSpec((HALO, cb), lambda j, i: (0, j))],
        out_shape=[jax.ShapeDtypeStruct((t, c), dx_dtype), jax.ShapeDtypeStruct((HALO, c), F32)],
        scratch_shapes=[pltpu.VMEM((tb + HALO, cb), F32)],
        compiler_params=_params(("parallel", "arbitrary")),
    )(dy, dy, x, w)


def _dn_head(c, normed):
    s = _silu(c)
    return s * lax.rsqrt(jnp.sum(s * s, axis=-1, keepdims=True) + EPS) if normed else s


def _dn_gates(hab, alog, dtb):
    lane = lax.broadcasted_iota(jnp.int32, hab.shape, 1)
    g = -jnp.exp(alog) * _softplus(hab + dtb)
    beta = jax.nn.sigmoid(hab)
    return jnp.where(lane < N_HEADS, g, jnp.where(lane < 2 * N_HEADS, beta, 0.0))


def _dn_head_slices(q_ref, k_ref, v_ref):
    return [(pl.ds((part * N_HEADS + h) * HEAD_DIM, HEAD_DIM), ref, h, part < 2)
            for part, ref in enumerate((q_ref, k_ref, v_ref)) for h in range(N_HEADS)]


def _dn_prep_fwd(c, hab, alog, dtb):
    t = c.shape[0]
    tb = _rows(t)

    def body(c_ref, hab_ref, alog_ref, dtb_ref, q_ref, k_ref, v_ref, gb_ref):
        for sl, ref, h, normed in _dn_head_slices(q_ref, k_ref, v_ref):
            ref[h] = _dn_head(c_ref[:, sl], normed)
        gb_ref[...] = _dn_gates(hab_ref[...], alog_ref[...], dtb_ref[...])

    hm = pl.BlockSpec((N_HEADS, tb, HEAD_DIM), lambda i: (0, i, 0))
    nar = pl.BlockSpec((tb, LANES), lambda i: (i, 0))
    vec = pl.BlockSpec((1, LANES), lambda i: (0, 0))
    return pl.pallas_call(
        body, name="dn_prep_fwd", grid=(t // tb,),
        in_specs=[pl.BlockSpec((tb, 3 * D_MODEL), lambda i: (i, 0)), nar, vec, vec],
        out_specs=[hm, hm, hm, nar],
        out_shape=[jax.ShapeDtypeStruct((N_HEADS, t, HEAD_DIM), F32)] * 3 + [jax.ShapeDtypeStruct((t, LANES), F32)],
        compiler_params=_params(("parallel",)),
    )(c, hab, alog, dtb)


def _dn_prep_bwd(c, hab, alog, dtb, dq, dk, dv, dgb):
    t = c.shape[0]
    tb = _rows(t)

    def body(c_ref, hab_ref, alog_ref, dtb_ref, dq_ref, dk_ref, dv_ref, dgb_ref,
             dc_ref, dhab_ref, dalog_ref, ddtb_ref):
        i = pl.program_id(0)
        for sl, ref, h, normed in _dn_head_slices(dq_ref, dk_ref, dv_ref):
            _, vjp = jax.vjp(functools.partial(_dn_head, normed=normed), c_ref[:, sl])
            dc_ref[:, sl] = vjp(ref[h])[0]
        _, vjp = jax.vjp(_dn_gates, hab_ref[...], alog_ref[...], dtb_ref[...])
        dhab, dalog, ddtb = vjp(dgb_ref[...])
        dhab_ref[...] = dhab

        @pl.when(i == 0)
        def _():
            dalog_ref[...] = jnp.zeros_like(dalog_ref)
            ddtb_ref[...] = jnp.zeros_like(ddtb_ref)

        dalog_ref[...] += dalog
        ddtb_ref[...] += ddtb

    hm = pl.BlockSpec((N_HEADS, tb, HEAD_DIM), lambda i: (0, i, 0))
    wide = pl.BlockSpec((tb, 3 * D_MODEL), lambda i: (i, 0))
    nar = pl.BlockSpec((tb, LANES), lambda i: (i, 0))
    vec = pl.BlockSpec((1, LANES), lambda i: (0, 0))
    return pl.pallas_call(
        body, name="dn_prep_bwd", grid=(t // tb,),
        in_specs=[wide, nar, vec, vec, hm, hm, hm, nar],
        out_specs=[wide, nar, vec, vec],
        out_shape=[jax.ShapeDtypeStruct((t, 3 * D_MODEL), F32), jax.ShapeDtypeStruct((t, LANES), F32),
                   jax.ShapeDtypeStruct((1, LANES), F32), jax.ShapeDtypeStruct((1, LANES), F32)],
        compiler_params=_params(("arbitrary",)),
    )(c, hab, alog, dtb, dq, dk, dv, dgb)


DN_PREC = lax.Precision.HIGH
DN_GROUP = 32


def _dn_prec(a):
    return DN_PREC if a.dtype == F32 else None


def _bdot(a, b):
    return lax.dot_general(a, b, (((2,), (1,)), ((0,), (0,))), precision=_dn_prec(a), preferred_element_type=F32)


def _bdot_nt(a, b):
    return lax.dot_general(a, b, (((2,), (2,)), ((0,), (0,))), precision=_dn_prec(a), preferred_element_type=F32)


def _bdot_tn(a, b):
    return lax.dot_general(a, b, (((1,), (1,)), ((0,), (0,))), precision=_dn_prec(a), preferred_element_type=F32)


def _unit_lower_inverse(lmat):
    c = lmat.shape[-1]
    ri = lax.broadcasted_iota(jnp.int32, (c, c), 0)
    ci = lax.broadcasted_iota(jnp.int32, (c, c), 1)
    p = -lmat
    tinv = jnp.where(ri == ci, 1.0, 0.0) + p
    for _ in range(max(c.bit_length() - 2, 0)):
        p = _bdot(p, p)
        tinv = tinv + _bdot(tinv, p)
    return tinv


@jax.custom_vjp
def _solve_with(lmat, rhs, tinv):
    return _bdot(tinv, rhs)


def _solve_with_fwd(lmat, rhs, tinv):
    sol = _bdot(tinv, rhs)
    return sol, (sol, tinv)


def _solve_with_bwd(res, dsol):
    sol, tinv = res
    drhs = _bdot_tn(tinv, dsol)
    return -_bdot_nt(drhs, sol), drhs, jnp.zeros_like(tinv)


_solve_with.defvjp(_solve_with_fwd, _solve_with_bwd)


def _dn_local(q, k, v, grow, brow, tinv):
    g, c, _ = q.shape
    ri = lax.broadcasted_iota(jnp.int32, (c, c), 0)
    ci = lax.broadcasted_iota(jnp.int32, (c, c), 1)
    lower = ri >= ci
    as_col = lambda r: jnp.sum(jnp.where(ri == ci, jnp.broadcast_to(r, (g, c, c)), 0.0), axis=2, keepdims=True)
    gcol, bcol = as_col(grow), as_col(brow)
    gc_col = jnp.sum(jnp.where(lower, jnp.broadcast_to(grow, (g, c, c)), 0.0), axis=2, keepdims=True)
    gc_row = jnp.sum(jnp.where(ri <= ci, jnp.broadcast_to(gcol, (g, c, c)), 0.0), axis=1, keepdims=True)
    qs = q * (HEAD_DIM ** -0.5)
    kb = k * bcol
    vb = v * bcol
    decay = jnp.where(lower, jnp.exp(jnp.where(lower, gc_col - gc_row, 0.0)), 0.0)
    lmat = jnp.where(ri > ci, _bdot_nt(kb.astype(BF16), k.astype(BF16)) * decay, 0.0)
    eg = jnp.exp(gc_col)
    rhs = jnp.concatenate([vb, kb * eg], axis=2)
    if tinv is None:
        tinv = _unit_lower_inverse(lmat)
    sol = _solve_with(lmat, rhs, tinv)
    a_qk = jnp.where(lower, _bdot_nt(qs.astype(BF16), k.astype(BF16)) * decay, 0.0)
    g_last = jnp.sum(grow, axis=2, keepdims=True)
    kdec = k * jnp.exp(g_last - gc_col)
    egl = jnp.broadcast_to(jnp.exp(g_last), (g, 1, HEAD_DIM))
    b16 = lambda x: x.astype(BF16)
    return sol[:, :, :HEAD_DIM], b16(sol[:, :, HEAD_DIM:]), b16(a_qk), b16(qs * eg), b16(kdec), egl, tinv


def _dn_seq(u, w, a_qk, qe, kdec, egl, s_in):
    b16 = lambda x: x.astype(BF16)
    v_new = u - _bdot(b16(w), b16(s_in))
    o = _bdot(b16(qe), b16(s_in)) + _bdot(b16(a_qk), b16(v_new))
    return o, s_in * egl + _bdot_tn(b16(kdec), b16(v_new))


def _dn_local_specs(t):
    grp = min(DN_GROUP, t // DN_CHUNK)
    rows = grp * DN_CHUNK
    blk = pl.BlockSpec((1, rows, HEAD_DIM), lambda h, i: (h, i, 0))
    row = pl.BlockSpec((1, grp, 1, DN_CHUNK), lambda h, i: (h, i, 0, 0))
    sq = pl.BlockSpec((1, grp, DN_CHUNK, DN_CHUNK), lambda h, i: (h, i, 0, 0))
    lane = pl.BlockSpec((1, grp, 1, HEAD_DIM), lambda h, i: (h, i, 0, 0))
    return grp, blk, row, sq, lane


def half(shape):
    return jax.ShapeDtypeStruct(shape.shape, BF16)


def _dn_shapes(t):
    nchunk = t // DN_CHUNK
    big = jax.ShapeDtypeStruct((N_HEADS, t, HEAD_DIM), F32)
    row = jax.ShapeDtypeStruct((N_HEADS, nchunk, 1, DN_CHUNK), F32)
    sq = jax.ShapeDtypeStruct((N_HEADS, nchunk, DN_CHUNK, DN_CHUNK), F32)
    lane = jax.ShapeDtypeStruct((N_HEADS, nchunk, 1, HEAD_DIM), F32)
    return big, row, sq, lane


def _dn_local_fwd(q, k, v, grow, brow, wire=()):
    t = q.shape[1]
    grp, blk, row, sq, lane = _dn_local_specs(t)
    big, _, sqs, lanes = _dn_shapes(t)
    n = len(wire)
    groups = t // (grp * DN_CHUNK)
    steps = N_HEADS * groups

    def body(q_ref, k_ref, v_ref, gr_ref, br_ref, *rest):
        u_ref, w_ref, a_ref, qe_ref, kd_ref, egl_ref, t_ref = rest[n:n + 7]
        if n:
            begin, middle, end = _gather_protocol(rest[:n], rest[n + 7:2 * n + 7], *rest[2 * n + 7:])
            step = pl.program_id(0) * groups + pl.program_id(1)
            pl.when(step == 0)(begin)
            pl.when(step == (GATHER_FORWARD_AT * steps) // 8)(middle)
        split = lambda r: r[0].reshape(grp, DN_CHUNK, HEAD_DIM)
        u, w, a_qk, qe, kdec, egl, tinv = _dn_local(split(q_ref), split(k_ref), split(v_ref), gr_ref[0],
                                                     br_ref[0], None)
        for ref, val in ((u_ref, u), (w_ref, w), (qe_ref, qe), (kd_ref, kdec)):
            ref[0] = val.reshape(grp * DN_CHUNK, HEAD_DIM)
        a_ref[0] = a_qk
        egl_ref[0] = egl
        t_ref[0] = tinv
        if n:
            pl.when(step == steps - 1)(end)

    assert n == 0 or steps >= 3
    return pl.pallas_call(
        body, name="dn_local_fwd", grid=(N_HEADS, groups),
        in_specs=[blk, blk, blk, row, row] + _hbm_specs(n),
        out_specs=[blk, blk, sq, blk, blk, lane, sq] + _hbm_specs(n),
        out_shape=[big, half(big), half(sqs), half(big), half(big), lanes, sqs] + _gather_out_shapes(wire),
        scratch_shapes=_gather_sems(n) if n else [],
        compiler_params=_params(("arbitrary", "arbitrary")),
    )(q, k, v, grow, brow, *wire)


def _dn_local_bwd(q, k, v, grow, brow, tinv, du, dw, da, dqe, dkd, degl):
    t = q.shape[1]
    grp, blk, row, sq, lane = _dn_local_specs(t)
    big, rows_, _, _ = _dn_shapes(t)

    def body(q_ref, k_ref, v_ref, gr_ref, br_ref, t_ref, du_ref, dw_ref, da_ref, dqe_ref, dkd_ref,
             degl_ref, dq_ref, dk_ref, dv_ref, dgr_ref, dbr_ref):
        split = lambda r: r[0].reshape(grp, DN_CHUNK, HEAD_DIM)
        tinv_v = t_ref[0]
        fn = lambda q_, k_, v_, gr_, br_: _dn_local(q_, k_, v_, gr_, br_, tinv_v)[:6]
        _, vjp = jax.vjp(fn, split(q_ref), split(k_ref), split(v_ref), gr_ref[0], br_ref[0])
        dq, dk, dv, dgr, dbr = vjp((split(du_ref), split(dw_ref), da_ref[0], split(dqe_ref), split(dkd_ref),
                                    degl_ref[0]))
        for ref, val in ((dq_ref, dq), (dk_ref, dk), (dv_ref, dv)):
            ref[0] = val.reshape(grp * DN_CHUNK, HEAD_DIM)
        dgr_ref[0] = dgr
        dbr_ref[0] = dbr

    return pl.pallas_call(
        body, name="dn_local_bwd", grid=(N_HEADS, t // (grp * DN_CHUNK)),
        in_specs=[blk, blk, blk, row, row, sq, blk, blk, sq, blk, blk, lane],
        out_specs=[blk, blk, blk, row, row],
        out_shape=[big, big, big, rows_, rows_],
        compiler_params=_params(("parallel", "parallel")),
    )(q, k, v, grow, brow, tinv, du, dw, da, dqe, dkd, degl)


DN_SEQ_CHUNKS = 8


def _dn_seq_specs(nchunk, rev):
    per = min(DN_SEQ_CHUNKS, nchunk)
    nstep = nchunk // per

    def idx(n):
        return nstep - 1 - n if rev else n

    blk = pl.BlockSpec((N_HEADS, per * DN_CHUNK, HEAD_DIM), lambda n: (0, idx(n), 0))
    sq = pl.BlockSpec((N_HEADS, per, DN_CHUNK, DN_CHUNK), lambda n: (0, idx(n), 0, 0))
    lane = pl.BlockSpec((N_HEADS, per, 1, HEAD_DIM), lambda n: (0, idx(n), 0, 0))
    st = pl.BlockSpec((N_HEADS, per, HEAD_DIM, HEAD_DIM), lambda n: (0, idx(n), 0, 0))
    return per, nstep, blk, sq, lane, st


def _dn_seq_fwd(u, w, a_qk, qe, kdec, egl):
    t = u.shape[1]
    nchunk = t // DN_CHUNK
    per, nstep, blk, sq, lane, st = _dn_seq_specs(nchunk, False)

    def body(u_ref, w_ref, a_ref, qe_ref, kd_ref, egl_ref, o_ref, s_ref, state):
        @pl.when(pl.program_id(0) == 0)
        def _():
            state[...] = jnp.zeros_like(state)

        for c in range(per):
            rows = pl.ds(c * DN_CHUNK, DN_CHUNK)
            s_in = state[...]
            s_ref[:, c] = s_in.astype(BF16)
            o_ref[:, rows], state[...] = _dn_seq(u_ref[:, rows], w_ref[:, rows], a_ref[:, c], qe_ref[:, rows],
                                                 kd_ref[:, rows], egl_ref[:, c], s_in)

    return pl.pallas_call(
        body, name="dn_seq_fwd", grid=(nstep,),
        in_specs=[blk, blk, sq, blk, blk, lane],
        out_specs=[blk, st],
        out_shape=[jax.ShapeDtypeStruct((N_HEADS, t, HEAD_DIM), F32),
                   jax.ShapeDtypeStruct((N_HEADS, nchunk, HEAD_DIM, HEAD_DIM), BF16)],
        scratch_shapes=[pltpu.VMEM((N_HEADS, HEAD_DIM, HEAD_DIM), F32)],
        compiler_params=_params(("arbitrary",)),
    )(u, w, a_qk, qe, kdec, egl)


def _dn_seq_bwd(u, w, a_qk, qe, kdec, egl, states, do):
    t = u.shape[1]
    nchunk = t // DN_CHUNK
    per, nstep, blk, sq, lane, st = _dn_seq_specs(nchunk, True)
    big, _, sqs, lanes = _dn_shapes(t)

    def body(u_ref, w_ref, a_ref, qe_ref, kd_ref, egl_ref, s_ref, do_ref,
             du_ref, dw_ref, da_ref, dqe_ref, dkd_ref, degl_ref, dstate):
        @pl.when(pl.program_id(0) == 0)
        def _():
            dstate[...] = jnp.zeros_like(dstate)

        for c in reversed(range(per)):
            rows = pl.ds(c * DN_CHUNK, DN_CHUNK)
            _, vjp = jax.vjp(_dn_seq, u_ref[:, rows], w_ref[:, rows], a_ref[:, c], qe_ref[:, rows], kd_ref[:, rows],
                             egl_ref[:, c], s_ref[:, c].astype(F32))
            (du_ref[:, rows], dw_ref[:, rows], da_ref[:, c], dqe_ref[:, rows], dkd_ref[:, rows], degl_ref[:, c],
             dstate[...]) = vjp((do_ref[:, rows], dstate[...]))

    return pl.pallas_call(
        body, name="dn_seq_bwd", grid=(nstep,),
        in_specs=[blk, blk, sq, blk, blk, lane, st, blk],
        out_specs=[blk, blk, sq, blk, blk, lane],
        out_shape=[big, half(big), half(sqs), half(big), half(big), lanes],
        scratch_shapes=[pltpu.VMEM((N_HEADS, HEAD_DIM, HEAD_DIM), F32)],
        compiler_params=_params(("arbitrary",)),
    )(u, w, a_qk, qe, kdec, egl, states, do)


def _dn_post_head(o, gate, w):
    return _rms(o, w) * _silu(gate)


def _dn_post_fwd(o, gate, w):
    t = gate.shape[0]
    tb = _rows(t)

    def body(o_ref, g_ref, w_ref, y_ref):
        for h in range(N_HEADS):
            sl = pl.ds(h * HEAD_DIM, HEAD_DIM)
            y_ref[:, sl] = _dn_post_head(o_ref[h], g_ref[:, sl], w_ref[...]).astype(BF16)

    row = pl.BlockSpec((tb, D_MODEL), lambda i: (i, 0))
    hm = pl.BlockSpec((N_HEADS, tb, HEAD_DIM), lambda i: (0, i, 0))
    return pl.pallas_call(
        body, name="dn_post_fwd", grid=(t // tb,),
        in_specs=[hm, row, pl.BlockSpec((1, HEAD_DIM), lambda i: (0, 0))],
        out_specs=row, out_shape=jax.ShapeDtypeStruct((t, D_MODEL), BF16),
        compiler_params=_params(("parallel",)),
    )(o, gate, w)


def _dn_post_bwd(o, gate, w, dy):
    t = gate.shape[0]
    tb = _rows(t)

    def body(o_ref, g_ref, w_ref, dy_ref, do_ref, dg_ref, dw_ref):
        i = pl.program_id(0)
        @pl.when(i == 0)
        def _():
            dw_ref[...] = jnp.zeros_like(dw_ref)

        for h in range(N_HEADS):
            sl = pl.ds(h * HEAD_DIM, HEAD_DIM)
            _, vjp = jax.vjp(_dn_post_head, o_ref[h], g_ref[:, sl], w_ref[...])
            do_ref[h], dg, dw = vjp(dy_ref[:, sl])
            dg_ref[:, sl] = dg.astype(BF16)
            dw_ref[...] += dw

    row = pl.BlockSpec((tb, D_MODEL), lambda i: (i, 0))
    hm = pl.BlockSpec((N_HEADS, tb, HEAD_DIM), lambda i: (0, i, 0))
    vec = pl.BlockSpec((1, HEAD_DIM), lambda i: (0, 0))
    return pl.pallas_call(
        body, name="dn_post_bwd", grid=(t // tb,),
        in_specs=[hm, row, vec, row],
        out_specs=[hm, row, vec],
        out_shape=[jax.ShapeDtypeStruct((N_HEADS, t, HEAD_DIM), F32), jax.ShapeDtypeStruct((t, D_MODEL), BF16),
                   jax.ShapeDtypeStruct((1, HEAD_DIM), F32)],
        compiler_params=_params(("arbitrary",)),
    )(o, gate, w, dy)


def _split_bf16(x):
    hi = x.astype(BF16)
    lo = (x - hi.astype(F32)).astype(BF16)
    return hi, lo


SB_Q_BLOCK = 512
SB_K_BLOCK = 256
SB_NEGLIGIBLE = -60.0


def _sb_logits(q, kb, mask, scale):
    z = _dot_nt(q, kb) * scale
    ls = jnp.minimum(z, 0.0) - jnp.log(1.0 + jnp.exp(-jnp.abs(z)))
    lk = ls - z
    if mask is not None:
        lk = jnp.where(mask, lk, 0.0)
    return ls, lk


def _sb_blocks(t):
    bq = min(SB_Q_BLOCK, t)
    bk = min(SB_K_BLOCK, bq)
    return bq, bk, bq // bk


def _sb_fwd(qkv):
    t = qkv.shape[0]
    bq, bk, nd = _sb_blocks(t)
    scale = HEAD_DIM ** -0.5

    def body(q_ref, k_ref, v_ref, o_ref, tot_ref, used_ref):
        i = pl.program_id(1)
        q = q_ref[...]
        rj = lax.broadcasted_iota(jnp.int32, (bk, bk), 0)
        cj = lax.broadcasted_iota(jnp.int32, (bk, bk), 1)
        after = (rj > cj).astype(BF16)
        trow = lax.broadcasted_iota(jnp.int32, (bq, bk), 0)
        scol = lax.broadcasted_iota(jnp.int32, (bq, bk), 1)

        def tile(j, run, acc, mask):
            off = pl.multiple_of(j * bk, bk)
            kb = k_ref[pl.ds(off, bk), :]
            vb = v_ref[pl.ds(off, bk), :]
            ls, lk = _sb_logits(q, kb, mask, scale)
            hi, lo = _split_bf16(lk)
            between = _dot(hi, after) + _dot(lo, after) + run
            a = jnp.exp(ls + between)
            if mask is not None:
                a = jnp.where(mask, a, 0.0)
            acc = acc + _dot(a.astype(BF16), vb)
            return run + jnp.sum(lk, axis=1, keepdims=True), acc

        run, acc = jnp.zeros((bq, 1), F32), jnp.zeros((bq, HEAD_DIM), F32)
        for d in reversed(range(nd)):
            run, acc = tile(i * nd + d, run, acc, scol + d * bk < trow)
        def more(c):
            return jnp.logical_and(c[0] < i * nd, jnp.max(c[1]) > SB_NEGLIGIBLE)

        def far(c):
            run_, acc_ = tile(i * nd - 1 - c[0], c[1], c[2], None)
            return c[0] + 1, run_, acc_

        used, run, acc = lax.while_loop(more, far, (jnp.int32(0), run, acc))
        o_ref[...] = acc.astype(BF16)
        tot_ref[...] = jnp.broadcast_to(run, (bq, HEAD_DIM))
        used_ref[...] = jnp.full(used_ref.shape, used, F32)

    qs = pl.BlockSpec((bq, HEAD_DIM), lambda h, i: (i, h))
    ks = pl.BlockSpec((t, HEAD_DIM), lambda h, i: (0, N_HEADS + h))
    vs = pl.BlockSpec((t, HEAD_DIM), lambda h, i: (0, 2 * N_HEADS + h))
    return pl.pallas_call(
        body, name="sb_fwd", grid=(N_HEADS, t // bq),
        in_specs=[qs, ks, vs], out_specs=[qs, qs, pl.BlockSpec((1, 1, 1, LANES), lambda h, i: (h, i, 0, 0))],
        out_shape=[jax.ShapeDtypeStruct((t, D_MODEL), BF16), jax.ShapeDtypeStruct((t, D_MODEL), F32),
                   jax.ShapeDtypeStruct((N_HEADS, t // bq, 1, LANES), F32)],
        compiler_params=_params(("parallel", "arbitrary")),
    )(qkv, qkv, qkv)


def _sb_bwd(qkv, tot, used, do, partials=()):
    t = qkv.shape[0]
    bq, bk, nd = _sb_blocks(t)
    scale = HEAD_DIM ** -0.5
    n = len(partials)
    nq = t // bq

    def body(q_ref, k_ref, v_ref, tot_ref, used_ref, do_ref, *rest):
        dq_ref, dk_ref, dv_ref = rest[n:n + 3]
        i = pl.program_id(1)
        if n:
            begin, end = _chip_exchange_protocol(rest[:n], rest[n + 3:2 * n + 3], *rest[2 * n + 3:])
            step = pl.program_id(0) * nq + i
            pl.when(step == 0)(begin)

        @pl.when(i == 0)
        def _():
            dk_ref[...] = jnp.zeros_like(dk_ref)
            dv_ref[...] = jnp.zeros_like(dv_ref)

        q = q_ref[...]
        do = do_ref[...]
        total = tot_ref[:, 0:1]
        rj = lax.broadcasted_iota(jnp.int32, (bk, bk), 0)
        cj = lax.broadcasted_iota(jnp.int32, (bk, bk), 1)
        upto = (rj <= cj).astype(BF16)
        before = (rj < cj).astype(BF16)
        trow = lax.broadcasted_iota(jnp.int32, (bq, bk), 0)
        scol = lax.broadcasted_iota(jnp.int32, (bq, bk), 1)

        def tile(j, run_k, run_e, dq, mask):
            off = pl.multiple_of(j * bk, bk)
            kb = k_ref[pl.ds(off, bk), :]
            vb = v_ref[pl.ds(off, bk), :]
            ls, lk = _sb_logits(q, kb, mask, scale)
            hi, lo = _split_bf16(lk)
            between = total - (_dot(hi, upto) + _dot(lo, upto) + run_k)
            a = jnp.exp(ls + between)
            if mask is not None:
                a = jnp.where(mask, a, 0.0)
            e = a * _dot_nt(do, vb)
            ehi, elo = _split_bf16(e)
            pre = _dot(ehi, before) + _dot(elo, before) + run_e
            sig = jnp.exp(ls)
            dz = e * (1.0 - sig) - pre * sig
            if mask is not None:
                dz = jnp.where(mask, dz, 0.0)
            dz = (dz * scale).astype(BF16)
            dq = dq + _dot(dz, kb)
            dk_ref[pl.ds(off, bk), :] += _dot_tn(dz, q)
            dv_ref[pl.ds(off, bk), :] += _dot_tn(a.astype(BF16), do)
            return (run_k + jnp.sum(lk, axis=1, keepdims=True),
                    run_e + jnp.sum(e, axis=1, keepdims=True), dq)

        zero = jnp.zeros((bq, 1), F32)
        visited = jnp.clip(jnp.max(used_ref[...]).astype(jnp.int32), 0, i * nd)
        carry = lax.fori_loop(i * nd - visited, i * nd, lambda j, c: tile(j, c[0], c[1], c[2], None),
                              (zero, zero, jnp.zeros((bq, HEAD_DIM), F32)))
        for d in range(nd):
            carry = tile(i * nd + d, *carry, scol + d * bk < trow)
        dq_ref[...] = carry[2]
        if n:
            pl.when(step == N_HEADS * nq - 1)(end)

    qs = pl.BlockSpec((bq, HEAD_DIM), lambda h, i: (i, h))
    ks = pl.BlockSpec((t, HEAD_DIM), lambda h, i: (0, N_HEADS + h))
    vs = pl.BlockSpec((t, HEAD_DIM), lambda h, i: (0, 2 * N_HEADS + h))
    full = pl.BlockSpec((t, HEAD_DIM), lambda h, i: (0, h))
    big = jax.ShapeDtypeStruct((t, D_MODEL), F32)
    return pl.pallas_call(
        body, name="sb_bwd", grid=(N_HEADS, nq),
        in_specs=[qs, ks, vs, qs, pl.BlockSpec((1, 1, 1, LANES), lambda h, i: (h, i, 0, 0)), qs] + _hbm_specs(n),
        out_specs=[qs, full, full] + _hbm_specs(n),
        out_shape=[big, big, big] + _chip_exchange_shapes(partials),
        scratch_shapes=_chip_exchange_sems(n) if n else [],
        compiler_params=_params(("arbitrary", "arbitrary")),
    )(qkv, qkv, qkv, tot, used, do, *partials)


def _merge_fwd(o_dn, o_sb, gl, x, wp_dn, wp_sb, w_out, w2):
    t = x.shape[0]
    tb = _rows(t)

    def body(odn_ref, osb_ref, gl_ref, x_ref, wpd_ref, wps_ref, wo_ref, w2_ref,
             pdn_ref, psb_ref, mix_ref, x1_ref, n2_ref):
        pdn = _dot(odn_ref[...], wpd_ref[...])
        psb = _dot(osb_ref[...], wps_ref[...])
        gates = jax.nn.sigmoid(gl_ref[...])
        mixed = (gates[:, :D_MODEL] * pdn + gates[:, D_MODEL:] * psb).astype(BF16)
        x1 = x_ref[...] + _dot(mixed, wo_ref[...])
        pdn_ref[...] = pdn.astype(BF16)
        psb_ref[...] = psb.astype(BF16)
        mix_ref[...] = mixed
        x1_ref[...] = x1
        n2_ref[...] = _rms(x1, w2_ref[...]).astype(BF16)

    row = pl.BlockSpec((tb, D_MODEL), lambda i: (i, 0))
    sq = pl.BlockSpec((D_MODEL, D_MODEL), lambda i: (0, 0))
    f = jax.ShapeDtypeStruct((t, D_MODEL), F32)
    b = jax.ShapeDtypeStruct((t, D_MODEL), BF16)
    return pl.pallas_call(
        body, name="merge_fwd", grid=(t // tb,),
        in_specs=[row, row, pl.BlockSpec((tb, 2 * D_MODEL), lambda i: (i, 0)), row, sq, sq, sq,
                  pl.BlockSpec((1, D_MODEL), lambda i: (0, 0))],
        out_specs=[row] * 5, out_shape=[b, b, b, f, b],
        compiler_params=_params(("parallel",)),
    )(o_dn, o_sb, gl, x, wp_dn, wp_sb, w_out, w2)


def _merge_bwd(dx2, dn2, x1, w2, gl, pdn, psb, wp_dn, wp_sb, w_out):
    t = x1.shape[0]
    tb = _rows(t)

    def body(dx2_ref, dn2_ref, x1_ref, w2_ref, gl_ref, pdn_ref, psb_ref, wpd_ref, wps_ref, wo_ref,
             dx1_ref, dw2_ref, dgl_ref, dpdn_ref, dpsb_ref, dodn_ref, dosb_ref):
        i = pl.program_id(0)
        _, vjp = jax.vjp(_rms, x1_ref[...], w2_ref[...])
        dxn, dw2 = vjp(dn2_ref[...])
        dx1 = dx2_ref[...] + dxn
        dx1_ref[...] = dx1

        @pl.when(i == 0)
        def _():
            dw2_ref[...] = jnp.zeros_like(dw2_ref)

        dw2_ref[...] += dw2
        dmix = _dot_nt(dx1.astype(BF16), wo_ref[...])
        gates = jax.nn.sigmoid(gl_ref[...])
        g_dn, g_sb = gates[:, :D_MODEL], gates[:, D_MODEL:]
        dpdn = (dmix * g_dn).astype(BF16)
        dpsb = (dmix * g_sb).astype(BF16)
        dgl_ref[:, :D_MODEL] = (dmix * pdn_ref[...].astype(F32) * g_dn * (1.0 - g_dn)).astype(BF16)
        dgl_ref[:, D_MODEL:] = (dmix * psb_ref[...].astype(F32) * g_sb * (1.0 - g_sb)).astype(BF16)
        dpdn_ref[...] = dpdn
        dpsb_ref[...] = dpsb
        dodn_ref[...] = _dot_nt(dpdn, wpd_ref[...])
        dosb_ref[...] = _dot_nt(dpsb, wps_ref[...]).astype(BF16)

    row = pl.BlockSpec((tb, D_MODEL), lambda i: (i, 0))
    wide = pl.BlockSpec((tb, 2 * D_MODEL), lambda i: (i, 0))
    sq = pl.BlockSpec((D_MODEL, D_MODEL), lambda i: (0, 0))
    vec = pl.BlockSpec((1, D_MODEL), lambda i: (0, 0))
    f = jax.ShapeDtypeStruct((t, D_MODEL), F32)
    b = jax.ShapeDtypeStruct((t, D_MODEL), BF16)
    return pl.pallas_call(
        body, name="merge_bwd", grid=(t // tb,),
        in_specs=[row, row, row, vec, wide, row, row, sq, sq, sq],
        out_specs=[row, vec, wide, row, row, row, row],
        out_shape=[f, jax.ShapeDtypeStruct((1, D_MODEL), F32), jax.ShapeDtypeStruct((t, 2 * D_MODEL), BF16),
                   b, b, f, b],
        compiler_params=_params(("arbitrary",)),
    )(dx2, dn2, x1, w2, gl, pdn, psb, wp_dn, wp_sb, w_out)


def _conv_taps(buf, w_ref, first, rows, cols=slice(None)):
    y = w_ref[0:1, cols] * buf[pl.ds(first, rows), cols]
    for s in range(1, w_ref.shape[0]):
        y = y + w_ref[s:s + 1, cols] * buf[pl.ds(first + s, rows), cols]
    return y


def _ffn_mid_fwd(pre_g, pre_u, wg, wu):
    t, c = pre_g.shape
    kk = wg.shape[0]
    tb, cb = _rows(t), _pick(c, ELEMENTWISE_COLS)
    per = tb // HALO

    def body(g_ref, gh_ref, u_ref, uh_ref, wg_ref, wu_ref, a_ref, gbuf, ubuf):
        i = pl.program_id(0)
        for buf, ref, halo in ((gbuf, g_ref, gh_ref), (ubuf, u_ref, uh_ref)):
            buf[pl.ds(HALO, tb), :] = ref[...]
            buf[pl.ds(0, HALO), :] = jnp.where(i == 0, 0.0, halo[...])
        for j in range(cb // LANES):
            sl = pl.ds(j * LANES, LANES)
            ug = _conv_taps(gbuf, wg_ref, HALO - (kk - 1), tb, sl)
            uu = _conv_taps(ubuf, wu_ref, HALO - (kk - 1), tb, sl)
            a_ref[:, sl] = (_silu(ug) * uu).astype(BF16)

    blk = pl.BlockSpec((tb, cb), lambda i, j: (i, j))
    halo = pl.BlockSpec((HALO, cb), lambda i, j: (jnp.maximum(i * per - 1, 0), j))
    wspec = pl.BlockSpec((kk, cb), lambda i, j: (0, j))
    return pl.pallas_call(
        body, name="ffn_mid_fwd", grid=(t // tb, c // cb),
        in_specs=[blk, halo, blk, halo, wspec, wspec], out_specs=blk,
        out_shape=jax.ShapeDtypeStruct((t, c), BF16),
        scratch_shapes=[pltpu.VMEM((tb + HALO, cb), F32)] * 2,
        compiler_params=_params(("parallel", "parallel")),
    )(pre_g, pre_g, pre_u, pre_u, wg, wu)


def _ffn_mid_bwd(pre_g, pre_u, wg, wu, da):
    t, c = pre_g.shape
    kk = wg.shape[0]
    tb, cb = _rows(t), _pick(c, ELEMENTWISE_COLS)
    per = tb // HALO
    nblk = t // tb
    ext = tb + HALO

    def body(g_ref, gb_ref, ga_ref, u_ref, ub_ref, ua_ref, da_ref, daa_ref, wg_ref, wu_ref,
             dg_ref, du_ref, dwg_ref, dwu_ref, gbuf, ubuf, dabuf, dgbuf, dubuf):
        i = pl.program_id(1)
        last = i == nblk - 1
        for buf, ref, before, after in ((gbuf, g_ref, gb_ref, ga_ref), (ubuf, u_ref, ub_ref, ua_ref)):
            buf[pl.ds(0, HALO), :] = jnp.where(i == 0, 0.0, before[...])
            buf[pl.ds(HALO, tb), :] = ref[...]
            buf[pl.ds(HALO + tb, HALO), :] = jnp.where(last, 0.0, after[...])
        dabuf[pl.ds(0, tb), :] = da_ref[...]
        dabuf[pl.ds(tb, HALO), :] = jnp.where(last, 0.0, daa_ref[...])

        @pl.when(i == 0)
        def _():
            dwg_ref[...] = jnp.zeros_like(dwg_ref)
            dwu_ref[...] = jnp.zeros_like(dwu_ref)

        for j in range(cb // LANES):
            sl = pl.ds(j * LANES, LANES)
            ug = _conv_taps(gbuf, wg_ref, HALO - (kk - 1), ext, sl)
            uu = _conv_taps(ubuf, wu_ref, HALO - (kk - 1), ext, sl)
            _, vjp = jax.vjp(lambda g, u: _silu(g) * u, ug, uu)
            dgbuf[:, sl], dubuf[:, sl] = vjp(dabuf[:, sl])
            for dbuf, xbuf, w_ref, dx_ref, dw_ref in ((dgbuf, gbuf, wg_ref, dg_ref, dwg_ref),
                                                      (dubuf, ubuf, wu_ref, du_ref, dwu_ref)):
                x = xbuf[pl.ds(HALO, tb), sl]
                dx = None
                for s in range(kk):
                    shifted = dbuf[pl.ds(kk - 1 - s, tb), sl]
                    term = w_ref[s:s + 1, sl] * shifted
                    dx = term if dx is None else dx + term
                    dw_ref[s:s + 1, sl] += jnp.sum(shifted * x, axis=0, keepdims=True)
                dx_ref[:, sl] = dx.astype(BF16)

    blk = pl.BlockSpec((tb, cb), lambda j, i: (i, j))
    before = pl.BlockSpec((HALO, cb), lambda j, i: (jnp.maximum(i * per - 1, 0), j))
    after = pl.BlockSpec((HALO, cb), lambda j, i: (jnp.minimum((i + 1) * per, t // HALO - 1), j))
    wspec = pl.BlockSpec((kk, cb), lambda j, i: (0, j))
    dwspec = pl.BlockSpec((HALO, cb), lambda j, i: (0, j))
    half = jax.ShapeDtypeStruct((t, c), BF16)
    dwshape = jax.ShapeDtypeStruct((HALO, c), F32)
    return pl.pallas_call(
        body, name="ffn_mid_bwd", grid=(c // cb, nblk),
        in_specs=[blk, before, after, blk, before, after, blk, after, wspec, wspec],
        out_specs=[blk, blk, dwspec, dwspec],
        out_shape=[half, half, dwshape, dwshape],
        scratch_shapes=[pltpu.VMEM((ext + HALO, cb), F32)] * 2 + [pltpu.VMEM((ext, cb), F32)] * 3,
        compiler_params=_params(("parallel", "arbitrary")),
    )(pre_g, pre_g, pre_g, pre_u, pre_u, pre_u, da, da, wg, wu)


def _down_loss(a, w_down, x1, wf, target):
    t = x1.shape[0]
    tb = _rows(t)

    def body(a_ref, wd_ref, x1_ref, wf_ref, tgt_ref, dx2_ref, dwf_ref, loss_ref):
        i = pl.program_id(0)
        x2 = x1_ref[...] + _dot(a_ref[...], wd_ref[...])
        y, vjp = jax.vjp(_rms, x2, wf_ref[...])
        err = y - tgt_ref[...]
        dx2, dwf = vjp(err * (1.0 / D_MODEL))
        dx2_ref[...] = dx2
        part = jnp.sum(jnp.sum(err * err, axis=1, keepdims=True), axis=0, keepdims=True) * (0.5 / D_MODEL)

        @pl.when(i == 0)
        def _():
            dwf_ref[...] = jnp.zeros_like(dwf_ref)
            loss_ref[...] = jnp.zeros_like(loss_ref)

        dwf_ref[...] += dwf
        loss_ref[...] += jnp.broadcast_to(part, loss_ref.shape)

    row = pl.BlockSpec((tb, D_MODEL), lambda i: (i, 0))
    vec = pl.BlockSpec((1, D_MODEL), lambda i: (0, 0))
    return pl.pallas_call(
        body, name="down_loss", grid=(t // tb,),
        in_specs=[pl.BlockSpec((tb, D_FF), lambda i: (i, 0)), pl.BlockSpec((D_FF, D_MODEL), lambda i: (0, 0)),
                  row, vec, row],
        out_specs=[row, vec, pl.BlockSpec((1, LANES), lambda i: (0, 0))],
        out_shape=[jax.ShapeDtypeStruct((t, D_MODEL), F32), jax.ShapeDtypeStruct((1, D_MODEL), F32),
                   jax.ShapeDtypeStruct((1, LANES), F32)],
        compiler_params=_params(("arbitrary",)),
    )(a, w_down, x1, wf, target)


def _local_step(x, target, wts, late_wire=(), late_weights=None, early_partials=None):
    t = x.shape[0]
    nchunk = t // DN_CHUNK

    n1, hab = _norm1_fwd(x, wts["norm1"], wts["w_ab"])
    dnqkv = _mm(n1, wts["w_dnqkv"], name="h_dnqkv")
    dngate = _mm(n1, wts["w_dngate"], name="h_dngate")
    sbqkv = _mm(n1, wts["w_sbqkv"], out_dtype=BF16, name="h_sbqkv")
    gl = _mm(n1, wts["w_gl"], name="h_gl")

    cdn = _conv_fwd(dnqkv, wts["dn_conv"], "dn_conv_fwd")
    qn, kn, vv, gb = _dn_prep_fwd(cdn, hab, wts["alog"], wts["dtb"])
    per_head = gb[:, :2 * N_HEADS].T.reshape(2 * N_HEADS, nchunk, DN_CHUNK)
    grow, brow = per_head[:N_HEADS, :, None, :], per_head[N_HEADS:, :, None, :]
    u_dn, w_dn, a_qk, qe, kdec, egl, tinv, *late = _dn_local_fwd(qn, kn, vv, grow, brow, late_wire)
    if late_wire:
        wts = {**wts, **late_weights(late)}
    o_raw, states = _dn_seq_fwd(u_dn, w_dn, a_qk, qe, kdec, egl)
    o_dn = _dn_post_fwd(o_raw, dngate, wts["dn_norm"])

    o_sb, tot, sb_used = _sb_fwd(sbqkv)

    pdn, psb, mixed, x1, n2 = _merge_fwd(o_dn, o_sb, gl, x, wts["wp_dn"], wts["wp_sb"], wts["w_out"],
                                         wts["norm2"])
    pre_g = _mm(n2, wts["w_up_g"], name="ffn_up_g")
    pre_u = _mm(n2, wts["w_up_u"], name="ffn_up_u")
    act = _ffn_mid_fwd(pre_g, pre_u, wts["ffn_conv_g"], wts["ffn_conv_u"])
    dx2, d_normf, loss_part = _down_loss(act, wts["w_down"], x1, wts["normf"], target)

    grads = {"normf": d_normf}
    da = _mm(dx2, wts["w_down"], tb=True, name="d_act")
    grads["w_down"] = _mm(act, dx2, ta=True, out_dtype=BF16, name="dw_down")
    dpre_g, dpre_u, dcw_g, dcw_u = _ffn_mid_bwd(pre_g, pre_u, wts["ffn_conv_g"], wts["ffn_conv_u"], da)
    grads["ffn_conv"] = jnp.concatenate([dcw_g[:FFN_CONV], dcw_u[:FFN_CONV]], axis=1)
    dn2 = _mm(dpre_g, wts["w_up_g"], tb=True, name="dn2_g")
    dn2 = _mm(dpre_u, wts["w_up_u"], tb=True, add=dn2, name="dn2_u")
    grads["w_up_g"] = _mm(n2, dpre_g, ta=True, out_dtype=BF16, name="dw_up_g")
    grads["w_up_u"] = _mm(n2, dpre_u, ta=True, out_dtype=BF16, name="dw_up_u")

    dx1, grads["norm2"], dgl, dpdn, dpsb, do_dn, do_sb = _merge_bwd(
        dx2, dn2, x1, wts["norm2"], gl, pdn, psb, wts["wp_dn"], wts["wp_sb"], wts["w_out"])
    grads["w_out"] = _mm(mixed, dx1, ta=True, out_dtype=BF16, name="dw_out")
    grads["wp_dn"] = _mm(o_dn, dpdn, ta=True, out_dtype=BF16, name="dw_proj_dn")
    grads["wp_sb"] = _mm(o_sb, dpsb, ta=True, out_dtype=BF16, name="dw_proj_sb")

    partials = early_partials(grads) if early_partials else ()
    dsq, dsk, dsv, *arrived = _sb_bwd(sbqkv, tot, sb_used, do_sb, partials)
    dsbqkv = jnp.concatenate([dsq, dsk, dsv], axis=1).astype(BF16)

    do_raw, ddngate, grads["dn_norm"] = _dn_post_bwd(o_raw, dngate, wts["dn_norm"], do_dn)
    seq_grads = _dn_seq_bwd(u_dn, w_dn, a_qk, qe, kdec, egl, states, do_raw)
    dqn, dkn, dvv, dgrow, dbrow = _dn_local_bwd(qn, kn, vv, grow, brow, tinv, *seq_grads)
    dgb = jnp.concatenate([dgrow.reshape(N_HEADS, t), dbrow.reshape(N_HEADS, t)], axis=0).T
    dgb = jnp.pad(dgb, ((0, 0), (0, LANES - 2 * N_HEADS)))
    dcdn, dhab, grads["alog"], grads["dtb"] = _dn_prep_bwd(cdn, hab, wts["alog"], wts["dtb"], dqn, dkn, dvv, dgb)
    ddnqkv, dcw_dn = _conv_bwd(dcdn, dnqkv, wts["dn_conv"], "dn_conv_bwd", BF16)
    grads["dn_conv"] = dcw_dn[:DN_CONV]

    dn1 = _mm(ddnqkv, wts["w_dnqkv"], tb=True, name="dn1_dnqkv")
    dn1 = _mm(ddngate, wts["w_dngate"], tb=True, add=dn1, name="dn1_dngate")
    dn1 = _mm(dsbqkv, wts["w_sbqkv"], tb=True, add=dn1, name="dn1_sbqkv")
    dn1 = _mm(dgl, wts["w_gl"], tb=True, add=dn1, name="dn1_gl")
    grads["w_dnqkv"] = _mm(n1, ddnqkv, ta=True, out_dtype=BF16, name="dw_dnqkv")
    grads["w_dngate"] = _mm(n1, ddngate, ta=True, out_dtype=BF16, name="dw_dngate")
    grads["w_sbqkv"] = _mm(n1, dsbqkv, ta=True, out_dtype=BF16, name="dw_sbqkv")
    grads["w_gl"] = _mm(n1, dgl, ta=True, out_dtype=BF16, name="dw_gl")
    grads["w_ab"] = _mm(n1, dhab, ta=True, out_dtype=BF16, name="dw_ab")
    grad_x, grads["norm1"] = _norm1_bwd(x, wts["norm1"], dn1, dx1, dhab, wts["w_ab"])
    return loss_part, grad_x, grads, (list(partials), arrived)


def _place():
    return lax.axis_index("x"), lax.axis_index("y"), lax.axis_index("c")


def _hbm_specs(n):
    return [pl.BlockSpec(memory_space=pltpu.HBM)] * n


GATHER_SEMS = 8
GATHER_FORWARD_AT = 5


def _gather_protocol(ins, outs, send_sems, recv_sems):
    n = len(ins)
    x, y, c = _place()
    me = 2 * x + y
    sibling = (x, y, 1 - c)
    xn, yn, dg = (1 - x, y), (x, 1 - y), (1 - x, 1 - y)
    idx = lambda chip: 2 * chip[0] + chip[1]

    def part(a, chip_index, core, quarter=None):
        half = ins[a].shape[0] // 2
        if quarter is None:
            return outs[a].at[chip_index, pl.ds(core * half, half), :]
        return outs[a].at[chip_index, pl.ds(core * half + quarter * (half // 2), half // 2), :]

    def copy(a, k, src, dst, to):
        return pltpu.make_async_remote_copy(src_ref=src, dst_ref=dst, send_sem=send_sems.at[GATHER_SEMS * a + k],
                                            recv_sem=recv_sems.at[GATHER_SEMS * a + k], device_id=to,
                                            device_id_type=MESH)

    def sent(a, k):
        half = ins[a].shape[0] // 2
        my_half = ins[a].at[pl.ds(c * half, half), :]
        if k < 2:
            return copy(a, k, my_half, part(a, me, c), (*(xn, yn)[k], c))
        if k < 4:
            src = part(a, idx((xn, yn)[k - 2]), c, k - 2)
            return copy(a, k, src, src, (*(yn, xn)[k - 2], c))
        src = (part(a, idx(xn), c), part(a, idx(yn), c), part(a, idx(dg), c, 0), part(a, idx(dg), c, 1))[k - 4]
        return copy(a, k, src, src, sibling)

    def landed(a, k):
        dst = (part(a, idx(xn), c), part(a, idx(yn), c), part(a, idx(dg), c, 0), part(a, idx(dg), c, 1),
               part(a, idx(xn), 1 - c), part(a, idx(yn), 1 - c), part(a, idx(dg), 1 - c, 0),
               part(a, idx(dg), 1 - c, 1))[k]
        return copy(a, k, dst, dst, sibling)

    def begin():
        for a in range(n):
            sent(a, 0).start()
            sent(a, 1).start()

    def middle():
        for a in range(n):
            for k in range(2):
                landed(a, k).wait_recv()
                sent(a, 2 + k).start()
                sent(a, 4 + k).start()

    def end():
        for a in range(n):
            for k in (2, 3):
                landed(a, k).wait_recv()
                sent(a, 4 + k).start()
        for a in range(n):
            for k in range(4, GATHER_SEMS):
                landed(a, k).wait_recv()
        for a in range(n):
            for k in range(GATHER_SEMS):
                sent(a, k).wait_send()

    return begin, middle, end


def _gather_out_shapes(shards):
    return [jax.ShapeDtypeStruct((N_CHIPS,) + s.shape, s.dtype) for s in shards]


def _gather_sems(n):
    return [pltpu.SemaphoreType.DMA((GATHER_SEMS * n,)), pltpu.SemaphoreType.DMA((GATHER_SEMS * n,))]


def _gather_shards(shards):
    n = len(shards)

    def body(*refs):
        begin, middle, end = _gather_protocol(refs[:n], refs[n:2 * n], *refs[2 * n:])
        begin()
        middle()
        end()

    return pl.pallas_call(
        body, name="gather_weights", in_specs=_hbm_specs(n), out_specs=_hbm_specs(n),
        out_shape=_gather_out_shapes(shards), scratch_shapes=_gather_sems(n),
    )(*shards)


def _pair_exchange_halves(gs, tag):
    n = len(gs)

    def body(*refs):
        ins, outs, (send_sems, recv_sems) = refs[:n], refs[n:2 * n], refs[2 * n:]
        x, y, c = _place()
        cps = []
        for a in range(n):
            half = ins[a].shape[1] // 2
            cp = pltpu.make_async_remote_copy(src_ref=ins[a].at[:, pl.ds((1 - c) * half, half), :], dst_ref=outs[a],
                                              send_sem=send_sems.at[a], recv_sem=recv_sems.at[a],
                                              device_id=(x, y, 1 - c), device_id_type=MESH)
            cp.start()
            cps.append(cp)
        for cp in cps:
            cp.wait()

    return pl.pallas_call(
        body, name="grad_pair_exchange_" + tag, in_specs=_hbm_specs(n), out_specs=_hbm_specs(n),
        out_shape=[jax.ShapeDtypeStruct((g.shape[0], g.shape[1] // 2, g.shape[2]), g.dtype) for g in gs],
        scratch_shapes=[pltpu.SemaphoreType.DMA((n,)), pltpu.SemaphoreType.DMA((n,))],
    )(*gs)


def _pick_rows(n, target=1024):
    best = 16
    for b in range(16, min(n, target) + 1, 16):
        if n % b == 0:
            best = b
    return best


def _pair_add(g, got, c_idx, tag):
    nsh, rows, cols = g.shape
    half = rows // 2
    rb = _pick_rows(half)

    def body(c_ref, g_ref, got_ref, o_ref):
        o_ref[...] = (g_ref[...].astype(F32) + got_ref[...].astype(F32)).astype(BF16)

    nb = half // rb
    grid_spec = pltpu.PrefetchScalarGridSpec(
        num_scalar_prefetch=1, grid=(nsh, nb),
        in_specs=[pl.BlockSpec((1, rb, cols), lambda s, i, c_ref: (s, c_ref[0] * nb + i, 0)),
                  pl.BlockSpec((1, rb, cols), lambda s, i, c_ref: (s, i, 0))],
        out_specs=pl.BlockSpec((1, rb, cols), lambda s, i, c_ref: (s, i, 0)))
    return pl.pallas_call(
        body, name="grad_pair_add_" + tag, grid_spec=grid_spec,
        out_shape=jax.ShapeDtypeStruct((nsh, half, cols), BF16),
        compiler_params=_params(("parallel", "parallel")),
    )(c_idx, g, got)


def _chip_exchange_protocol(ins, outs, send_sems, recv_sems):
    x, y, c = _place()
    chips = [(1 - x, y), (x, 1 - y), (1 - x, 1 - y)]

    def copies():
        return [pltpu.make_async_remote_copy(src_ref=ins[a].at[2 * px + py], dst_ref=outs[a].at[j],
                                             send_sem=send_sems.at[3 * a + j], recv_sem=recv_sems.at[3 * a + j],
                                             device_id=(px, py, c), device_id_type=MESH)
                for a in range(len(ins)) for j, (px, py) in enumerate(chips)]

    def begin():
        for cp in copies():
            cp.start()

    def end():
        for cp in copies():
            cp.wait_recv()
        for cp in copies():
            cp.wait_send()

    return begin, end


def _chip_exchange_shapes(ps):
    return [jax.ShapeDtypeStruct((N_CHIPS - 1,) + p.shape[1:], p.dtype) for p in ps]


def _chip_exchange_sems(n):
    return [pltpu.SemaphoreType.DMA((3 * n,)), pltpu.SemaphoreType.DMA((3 * n,))]


def _chip_exchange(ps):
    n = len(ps)

    def body(*refs):
        begin, end = _chip_exchange_protocol(refs[:n], refs[n:2 * n], *refs[2 * n:])
        begin()
        end()

    return pl.pallas_call(
        body, name="grad_chip_exchange", in_specs=_hbm_specs(n), out_specs=_hbm_specs(n),
        out_shape=_chip_exchange_shapes(ps), scratch_shapes=_chip_exchange_sems(n),
    )(*ps)


def _sum_partials(p, got, chip_idx, tag):
    nsh, half, cols = got.shape
    rb = _pick_rows(half)

    def body(me_ref, p_ref, got_ref, o_ref):
        acc = p_ref[0].astype(F32)
        for s in range(nsh):
            acc = acc + got_ref[s].astype(F32)
        o_ref[...] = acc

    grid_spec = pltpu.PrefetchScalarGridSpec(
        num_scalar_prefetch=1, grid=(half // rb,),
        in_specs=[pl.BlockSpec((1, rb, cols), lambda i, me_ref: (me_ref[0], i, 0)),
                  pl.BlockSpec((nsh, rb, cols), lambda i, me_ref: (0, i, 0))],
        out_specs=pl.BlockSpec((rb, cols), lambda i, me_ref: (i, 0)))
    return pl.pallas_call(
        body, name="grad_sum_chips_" + tag, grid_spec=grid_spec,
        out_shape=jax.ShapeDtypeStruct((half, cols), F32),
        compiler_params=_params(("parallel",)),
    )(chip_idx, p, got)


def _pair_share(rs):
    n = len(rs)

    def body(*refs):
        ins, outs, (send_sems, recv_sems) = refs[:n], refs[n:2 * n], refs[2 * n:]
        x, y, c = _place()
        cps = []
        for a in range(n):
            cp = pltpu.make_async_remote_copy(src_ref=ins[a], dst_ref=outs[a], send_sem=send_sems.at[a],
                                              recv_sem=recv_sems.at[a], device_id=(x, y, 1 - c),
                                              device_id_type=MESH)
            cp.start()
            cps.append(cp)
        for cp in cps:
            cp.wait()

    return pl.pallas_call(
        body, name="grad_pair_share", in_specs=_hbm_specs(n), out_specs=_hbm_specs(n),
        out_shape=[jax.ShapeDtypeStruct(r.shape, r.dtype) for r in rs],
        scratch_shapes=[pltpu.SemaphoreType.DMA((n,)), pltpu.SemaphoreType.DMA((n,))],
    )(*rs)


def _small_allreduce(v):
    rows, cols = v.shape
    ndev = 8

    def body(in_ref, out_ref, slots, send_sems, recv_sems):
        x, y, c = _place()
        me = 4 * x + 2 * y + c
        slots[me] = in_ref[...]
        sends = []
        for k in range(1, ndev):
            peer = (x ^ (k >> 2), y ^ ((k >> 1) & 1), c ^ (k & 1))
            cp = pltpu.make_async_remote_copy(src_ref=in_ref, dst_ref=slots.at[me], send_sem=send_sems.at[k - 1],
                                              recv_sem=recv_sems.at[k - 1], device_id=peer, device_id_type=MESH)
            cp.start()
            sends.append(cp)
        for k in range(1, ndev):
            there = slots.at[me ^ k]
            pltpu.make_async_remote_copy(src_ref=there, dst_ref=there, send_sem=send_sems.at[k - 1],
                                         recv_sem=recv_sems.at[k - 1], device_id=(x, y, c),
                                         device_id_type=MESH).wait_recv()
        for cp in sends:
            cp.wait_send()
        acc = slots[0]
        for s in range(1, ndev):
            acc = acc + slots[s]
        out_ref[...] = acc

    return pl.pallas_call(
        body, name="small_allreduce",
        in_specs=[pl.BlockSpec(memory_space=pltpu.VMEM)],
        out_specs=pl.BlockSpec(memory_space=pltpu.VMEM),
        out_shape=jax.ShapeDtypeStruct((rows, cols), F32),
        scratch_shapes=[pltpu.VMEM((ndev, rows, cols), F32), pltpu.SemaphoreType.DMA((ndev - 1,)),
                        pltpu.SemaphoreType.DMA((ndev - 1,))],
    )(v)


def _adamw(w, g, m, v, name):
    r, c = w.shape
    rb = r if r <= 128 else _pick_rows_8(r, 128)
    c1 = 1.0 - ADAM_B1 ** ADAM_STEP
    c2 = 1.0 - ADAM_B2 ** ADAM_STEP

    def body(w_ref, g_ref, m_ref, v_ref, d_ref, nm_ref, nv_ref):
        gg = g_ref[...]
        nm = ADAM_B1 * m_ref[...] + (1.0 - ADAM_B1) * gg
        nv = ADAM_B2 * v_ref[...] + (1.0 - ADAM_B2) * (gg * gg)
        d_ref[...] = -ADAM_LR * ((nm / c1) / (jnp.sqrt(nv / c2) + ADAM_EPS) + ADAM_WD * w_ref[...])
        nm_ref[...] = nm
        nv_ref[...] = nv

    blk = pl.BlockSpec((rb, c), lambda i: (i, 0))
    shp = jax.ShapeDtypeStruct((r, c), F32)
    return pl.pallas_call(
        body, name=name, grid=(r // rb,), in_specs=[blk] * 4, out_specs=[blk] * 3, out_shape=[shp] * 3,
        compiler_params=_params(("parallel",)),
    )(w, g, m, v)


def _pick_rows_8(n, target):
    best = n
    for b in range(8, min(n, target) + 1, 8):
        if n % b == 0:
            best = b
    return best


W_IN_COLS = 2308
W_UP_COLS = 1408
W_DOWN_ROWS = 704
DN_CONV_COLS = 768
FFN_CONV_COLS = 1408
PROJ_ROWS = 256
ROW_TILE = 16
ROW_SEGS = [("wp_dn", PROJ_ROWS), ("wp_sb", PROJ_ROWS), ("w_out", PROJ_ROWS), ("w_down", W_DOWN_ROWS),
            ("dn_conv", ROW_TILE), ("ffn_conv", ROW_TILE), ("spare", 2 * ROW_TILE)]
ROW_OFFS = {nm: (sum(n for _, n in ROW_SEGS[:i]), n) for i, (nm, n) in enumerate(ROW_SEGS)}
STACK_ROWS = sum(n for _, n in ROW_SEGS)
assert all(n % ROW_TILE == 0 for _, n in ROW_SEGS) and STACK_ROWS % (4 * ROW_TILE) == 0
Q_END, A_END, G_END, S_END = 3 * D_MODEL, 3 * D_MODEL + 2 * N_HEADS, 4 * D_MODEL + 2 * N_HEADS, 7 * D_MODEL + 2 * N_HEADS


def _flat_rows(a, nrows):
    flat = a.reshape(-1)
    return jnp.pad(flat, (0, nrows * D_MODEL - flat.shape[0])).reshape(nrows, D_MODEL)


IN_EXTRA_ROWS = 64
IN_CHUNK_CUTS = (384, 768)


def _weight_wire(w_in, wp_dn, wp_sb, w_out, w_up, w_down, dn_conv, ffn_conv):
    bits = lax.bitcast_convert_type(dn_conv, BF16).reshape(-1)
    extra = jnp.pad(bits, (0, IN_EXTRA_ROWS * W_IN_COLS - bits.shape[0])).reshape(IN_EXTRA_ROWS, W_IN_COLS)
    stack = jnp.concatenate([wp_dn.astype(BF16), wp_sb.astype(BF16), w_out.astype(BF16), w_down.astype(BF16),
                             jnp.zeros((ROW_TILE, D_MODEL), BF16),
                             _flat_rows(lax.bitcast_convert_type(ffn_conv, BF16), ROW_TILE),
                             jnp.zeros((ROW_OFFS["spare"][1], D_MODEL), BF16)], axis=0)
    first = jnp.concatenate([w_in.astype(BF16), extra], axis=0)
    cuts = (0,) + IN_CHUNK_CUTS + (first.shape[0],)
    return [first[a:b] for a, b in zip(cuts[:-1], cuts[1:])], [w_up.astype(BF16), stack]


def _col_range(g, lo, hi, width):
    parts = []
    for s in range(g.shape[0]):
        a, b = max(lo, s * width), min(hi, (s + 1) * width)
        if a < b:
            parts.append(g[s][:, a - s * width:b - s * width])
    return parts[0] if len(parts) == 1 else jnp.concatenate(parts, axis=1)


def _f32_rows(raw, k, ncols):
    raw = raw.reshape(N_CHIPS, -1)[:, :2 * k * ncols].reshape(N_CHIPS, k * ncols, 2)
    vals = lax.bitcast_convert_type(raw, F32).reshape(N_CHIPS, k, ncols)
    return vals.transpose(1, 0, 2).reshape(k, N_CHIPS * ncols)


def _unpack_early(*chunks):
    g_in = jnp.concatenate(chunks, axis=1)
    w = g_in[:, :D_MODEL, :]
    return {
        "w_dnqkv": _col_range(w, 0, Q_END, W_IN_COLS),
        "w_ab": jnp.pad(_col_range(w, Q_END, A_END, W_IN_COLS), ((0, 0), (0, LANES - 2 * N_HEADS))),
        "w_dngate": _col_range(w, A_END, G_END, W_IN_COLS),
        "w_sbqkv": _col_range(w, G_END, S_END, W_IN_COLS),
        "w_gl": _col_range(w, S_END, N_CHIPS * W_IN_COLS, W_IN_COLS),
        "dn_conv": _f32_rows(g_in[:, D_MODEL:, :], DN_CONV, DN_CONV_COLS),
    }


def _unpack_late(g_up, g_stack):
    def seg(nm):
        at, n = ROW_OFFS[nm]
        return g_stack[:, at:at + n, :]

    ffn_conv = _f32_rows(seg("ffn_conv"), FFN_CONV, FFN_CONV_COLS)
    return {
        "wp_dn": seg("wp_dn").reshape(D_MODEL, D_MODEL),
        "wp_sb": seg("wp_sb").reshape(D_MODEL, D_MODEL),
        "w_out": seg("w_out").reshape(D_MODEL, D_MODEL),
        "w_up_g": _col_range(g_up, 0, D_FF, W_UP_COLS), "w_up_u": _col_range(g_up, D_FF, 2 * D_FF, W_UP_COLS),
        "w_down": seg("w_down").reshape(D_FF, D_MODEL),
        "ffn_conv_g": ffn_conv[:, :D_FF], "ffn_conv_u": ffn_conv[:, D_FF:],
    }


def _grad_wire_early(gr):
    def cols(a, ncols):
        return a.reshape(a.shape[0], N_CHIPS, ncols).transpose(1, 0, 2)

    def rows(a, nrows):
        return a.astype(BF16).reshape(N_CHIPS, nrows, a.shape[1])

    def flat(a, nrows):
        a = a.astype(BF16).reshape(N_CHIPS, -1)
        return jnp.pad(a, ((0, 0), (0, nrows * D_MODEL - a.shape[1]))).reshape(N_CHIPS, nrows, D_MODEL)

    up = [gr["w_up_g"], gr["w_up_u"]]
    g_up = jnp.stack([up[s // 2][:, (s % 2) * W_UP_COLS:(s % 2 + 1) * W_UP_COLS].astype(BF16) for s in range(N_CHIPS)])
    g_stack = jnp.concatenate([rows(gr["wp_dn"], PROJ_ROWS), rows(gr["wp_sb"], PROJ_ROWS), rows(gr["w_out"], PROJ_ROWS),
                               rows(gr["w_down"], W_DOWN_ROWS), jnp.zeros((N_CHIPS, ROW_TILE, D_MODEL), BF16),
                               flat(cols(gr["ffn_conv"], FFN_CONV_COLS), ROW_TILE),
                               jnp.zeros((N_CHIPS, ROW_OFFS["spare"][1], D_MODEL), BF16)], axis=1)
    return [g_up, g_stack]


def _grad_wire_late(gr):
    pieces = [(gr["w_dnqkv"], 0), (gr["w_ab"][:, :2 * N_HEADS], Q_END), (gr["w_dngate"], A_END),
              (gr["w_sbqkv"], G_END), (gr["w_gl"], S_END)]
    conv = gr["dn_conv"].reshape(DN_CONV, N_CHIPS, DN_CONV_COLS).transpose(1, 0, 2).reshape(N_CHIPS, -1)

    def block(s):
        lo, hi = s * W_IN_COLS, (s + 1) * W_IN_COLS
        parts = []
        for a, at in pieces:
            b0, b1 = max(lo, at), min(hi, at + a.shape[1])
            if b0 < b1:
                parts.append(a[:, b0 - at:b1 - at].astype(BF16))
        w = parts[0] if len(parts) == 1 else jnp.concatenate(parts, axis=1)
        extra = jnp.pad(conv[s].astype(BF16), (0, IN_EXTRA_ROWS * W_IN_COLS - conv.shape[1]))
        return jnp.concatenate([w, extra.reshape(IN_EXTRA_ROWS, W_IN_COLS)], axis=0)

    return [jnp.stack([block(s) for s in range(N_CHIPS)])]


def _unpack_grad_shard(r_in, r_up, r_stack):
    def seg(nm):
        at, n = ROW_OFFS[nm]
        return r_stack[at:at + n, :]

    return {
        "w_in": r_in[:D_MODEL], "w_up": r_up,
        "wp_dn": seg("wp_dn"), "wp_sb": seg("wp_sb"), "w_out": seg("w_out"), "w_down": seg("w_down"),
        "dn_conv": r_in[D_MODEL:].reshape(-1)[:DN_CONV * DN_CONV_COLS].reshape(DN_CONV, DN_CONV_COLS),
        "ffn_conv": seg("ffn_conv").reshape(-1)[:FFN_CONV * FFN_CONV_COLS].reshape(FFN_CONV, FFN_CONV_COLS),
    }


def _lane_row(v):
    return jnp.pad(v.reshape(1, -1), ((0, 0), (0, LANES - v.size)))


def kernel(x, norm1_w, w_in, dn_conv_w, dn_A_log, dn_dt_bias, dn_norm_w, w_proj_dn, w_proj_sb, w_out, norm2_w, ffn_w_up, ffn_conv_w, ffn_w_down, norm_f_w, loss_target, m_norm1_w, m_w_in, m_dn_conv_w, m_dn_A_log, m_dn_dt_bias, m_dn_norm_w, m_w_proj_dn, m_w_proj_sb, m_w_out, m_norm2_w, m_ffn_w_up, m_ffn_conv_w, m_ffn_w_down, m_norm_f_w, v_norm1_w, v_w_in, v_dn_conv_w, v_dn_A_log, v_dn_dt_bias, v_dn_norm_w, v_w_proj_dn, v_w_proj_sb, v_w_out, v_norm2_w, v_ffn_w_up, v_ffn_conv_w, v_ffn_w_down, v_norm_f_w):
    early, late = _weight_wire(w_in[0], w_proj_dn[0], w_proj_sb[0], w_out[0], ffn_w_up[0], ffn_w_down[0],
                               dn_conv_w[0], ffn_conv_w[0])
    chip_idx = (2 * lax.axis_index("x") + lax.axis_index("y")).astype(jnp.int32)

    def with_mine(gathered, wire):
        return [lax.dynamic_update_slice(g, mine[None], (chip_idx, 0, 0)) for g, mine in zip(gathered, wire)]

    wts = _unpack_early(*with_mine(_gather_shards(early), early))
    wts.update(norm1=norm1_w, norm2=norm2_w, normf=norm_f_w.reshape(1, D_MODEL), dn_norm=dn_norm_w,
               alog=_lane_row(dn_A_log), dtb=_lane_row(dn_dt_bias))

    c_idx = lax.axis_index("c").astype(jnp.int32).reshape(1)

    def pair_sums(wire_g, tags, when):
        return [_pair_add(g, got, c_idx, tag) for g, got, tag in zip(wire_g, _pair_exchange_halves(wire_g, when), tags)]

    loss_part, grad_x, gr, (early_sums, early_arrived) = _local_step(
        x[0], loss_target[0], wts, late, lambda gathered: _unpack_late(*with_mine(gathered, late)),
        lambda grads: pair_sums(_grad_wire_early(grads), ["w_up", "rows"], "early"))

    late_sums = pair_sums(_grad_wire_late(gr), ["w_in"], "late")
    tags = ["w_in", "w_up", "rows"]
    reduced = [_sum_partials(p, got, chip_idx.reshape(1), tag)
               for p, got, tag in zip(late_sums + early_sums, list(_chip_exchange(late_sums)) + list(early_arrived), tags)]
    is_south = lax.axis_index("c") == 0
    gsh = _unpack_grad_shard(*[jnp.concatenate([jnp.where(is_south, mine, other), jnp.where(is_south, other, mine)],
                                               axis=0) for mine, other in zip(reduced, _pair_share(reduced))])

    tail = jnp.concatenate([gr["dn_norm"], gr["alog"][:, :N_HEADS], gr["dtb"][:, :N_HEADS], loss_part[:, :1]], axis=1)
    small = jnp.concatenate([gr["norm1"], gr["norm2"], gr["normf"],
                             jnp.pad(tail, ((0, 0), (0, D_MODEL - tail.shape[1]))),
                             jnp.zeros((SMALL_ROWS - 4, D_MODEL), F32)], axis=0)
    small = _small_allreduce(small)
    at = HEAD_DIM
    g_small = {"norm1_w": small[0:1], "norm2_w": small[1:2], "norm_f_w": small[2],
               "dn_norm_w": small[3:4, :at], "dn_A_log": small[3:4, at:at + N_HEADS],
               "dn_dt_bias": small[3:4, at + N_HEADS:at + 2 * N_HEADS]}
    loss = small[3, at + 2 * N_HEADS]

    big = {"w_in": (w_in, m_w_in, v_w_in, gsh["w_in"]), "dn_conv_w": (dn_conv_w, m_dn_conv_w, v_dn_conv_w, gsh["dn_conv"]),
           "w_proj_dn": (w_proj_dn, m_w_proj_dn, v_w_proj_dn, gsh["wp_dn"]),
           "w_proj_sb": (w_proj_sb, m_w_proj_sb, v_w_proj_sb, gsh["wp_sb"]),
           "w_out": (w_out, m_w_out, v_w_out, gsh["w_out"]),
           "ffn_w_up": (ffn_w_up, m_ffn_w_up, v_ffn_w_up, gsh["w_up"]),
           "ffn_conv_w": (ffn_conv_w, m_ffn_conv_w, v_ffn_conv_w, gsh["ffn_conv"]),
           "ffn_w_down": (ffn_w_down, m_ffn_w_down, v_ffn_w_down, gsh["w_down"])}
    res = {}
    for nm, (w, m, v, g) in big.items():
        d, nm_, nv_ = _adamw(w[0], g, m[0], v[0], "adamw_" + nm)
        res[nm] = (g[None], d[None], nm_[None], nv_[None])

    names = ["norm1_w", "norm2_w", "norm_f_w", "dn_norm_w", "dn_A_log", "dn_dt_bias"]
    given = {"norm1_w": (norm1_w, m_norm1_w, v_norm1_w), "norm2_w": (norm2_w, m_norm2_w, v_norm2_w),
             "norm_f_w": (norm_f_w, m_norm_f_w, v_norm_f_w), "dn_norm_w": (dn_norm_w, m_dn_norm_w, v_dn_norm_w),
             "dn_A_log": (dn_A_log, m_dn_A_log, v_dn_A_log), "dn_dt_bias": (dn_dt_bias, m_dn_dt_bias, v_dn_dt_bias)}

    def stack(k, fill):
        rows = [jnp.pad(given[nm][k].reshape(1, -1), ((0, 0), (0, D_MODEL - given[nm][k].size)),
                        constant_values=fill) for nm in names]
        return jnp.concatenate(rows + [jnp.full((SMALL_ROWS - len(names), D_MODEL), fill, F32)], axis=0)

    g_rows = jnp.concatenate(
        [jnp.pad(g_small[nm].reshape(1, -1), ((0, 0), (0, D_MODEL - g_small[nm].size))) for nm in names]
        + [jnp.zeros((SMALL_ROWS - len(names), D_MODEL), F32)], axis=0)
    d_s, m_s, v_s = _adamw(stack(0, 0.0), g_rows, stack(1, 0.0), stack(2, 1.0), "adamw_small")
    for r, nm in enumerate(names):
        shape = given[nm][0].shape
        n = given[nm][0].size
        res[nm] = (g_small[nm].reshape(shape), d_s[r, :n].reshape(shape), m_s[r, :n].reshape(shape),
                   v_s[r, :n].reshape(shape))

    order = ["norm1_w", "w_in", "dn_conv_w", "dn_A_log", "dn_dt_bias", "dn_norm_w", "w_proj_dn", "w_proj_sb",
             "w_out", "norm2_w", "ffn_w_up", "ffn_conv_w", "ffn_w_down", "norm_f_w"]
    outs = [loss, grad_x[None]]
    for k in range(4):
        outs += [res[nm][k] for nm in order]
    return tuple(outs)
```
